```python
import math
import jax, jax.numpy as jnp
from jax import lax
import numpy as np

D_MODEL = 1024
BATCH = 8
SEQ = 4096
DEPTH = 1

CHUNK = 128
SGU_WIDTH = D_MODEL
SGU_GROUPS = 4
SGU_GROUP_DIM = SGU_WIDTH // SGU_GROUPS
RET_HEADS = 4
RET_QK_DIM = 256
RET_V_DIM = 256
RET_WIDTH = RET_HEADS * RET_V_DIM
D_FF = 2816
ROPE_BASE = 10000.0
NORM_EPS = 1e-6
IN_WIDTHS = (SGU_WIDTH, SGU_WIDTH, RET_HEADS * RET_QK_DIM, RET_HEADS * RET_QK_DIM,
             RET_WIDTH, RET_WIDTH, D_MODEL, D_MODEL)
IN_WIDTH = sum(IN_WIDTHS)

kernel_name = "hybrid_sgu_retention_macaron_block"


def rms_norm(x, g):
    xf = x.astype(jnp.float32)
    y = xf * lax.rsqrt(jnp.mean(xf * xf, axis=-1, keepdims=True) + NORM_EPS)
    return (y * g.astype(jnp.float32)).astype(x.dtype)


def swiglu_ffn(h, w_gate, w_up, w_down):
    return (jax.nn.silu(h @ w_gate) * (h @ w_up)) @ w_down


def rotary(t):
    S, D = t.shape[1], t.shape[3]
    theta = ROPE_BASE ** (-jnp.arange(0, D, 2, dtype=jnp.float32) / D)
    ang = jnp.arange(S, dtype=jnp.float32)[:, None] * theta[None, :]
    cos = jnp.cos(ang)[None, :, None, :]
    sin = jnp.sin(ang)[None, :, None, :]
    t1, t2 = jnp.split(t.astype(jnp.float32), 2, axis=-1)
    return jnp.concatenate([t1 * cos - t2 * sin, t2 * cos + t1 * sin], axis=-1)


def spatial_gating(u, v, norm_g, norm_b, w_s, b_s):
    B, S, _ = v.shape
    vf = v.astype(jnp.float32)
    mu = jnp.mean(vf, axis=-1, keepdims=True)
    var = jnp.mean(jnp.square(vf - mu), axis=-1, keepdims=True)
    vn = (vf - mu) * lax.rsqrt(var + NORM_EPS) * norm_g + norm_b
    vc = vn.reshape(B, S // CHUNK, CHUNK, SGU_GROUPS, SGU_GROUP_DIM)
    s = jnp.einsum('gcm,bnmgd->bncgd', w_s.astype(jnp.float32), vc)
    s = s + b_s.astype(jnp.float32).T[None, None, :, :, None]
    return u * s.reshape(B, S, SGU_WIDTH).astype(u.dtype)


def retention_direction(q, k, v, log_gamma, include_diag):
    C = q.shape[3]
    idx = jnp.arange(C, dtype=jnp.float32)
    diff = idx[:, None] - idx[None, :]
    keep = (diff >= 0) if include_diag else (diff > 0)
    lg = log_gamma[:, None, None]
    decay = jnp.where(keep[None], jnp.exp(jnp.maximum(diff, 0.0)[None] * lg), 0.0)
    scores = jnp.einsum('bhncd,bhnmd->bhncm', q, k) * decay[None, :, None]
    intra = jnp.einsum('bhncm,bhnme->bhnce', scores, v)
    q_dec = q * jnp.exp((idx + 1.0)[None, :] * log_gamma[:, None])[None, :, None, :, None]
    k_dec = k * jnp.exp((C - 1.0 - idx)[None, :] * log_gamma[:, None])[None, :, None, :, None]
    chunk_decay = jnp.exp(C * log_gamma)[None, :, None, None]

    def step(state, xs):
        qn, kn, vn = xs
        out = jnp.einsum('bhcd,bhde->bhce', qn, state)
        state = state * chunk_decay + jnp.einsum('bhcd,bhce->bhde', kn, vn)
        return state, out

    B, H = q.shape[0], q.shape[1]
    init = jnp.zeros((B, H, q.shape[-1], v.shape[-1]), jnp.float32)
    xs = (jnp.moveaxis(q_dec, 2, 0), jnp.moveaxis(k_dec, 2, 0), jnp.moveaxis(v, 2, 0))
    _, cross = lax.scan(step, init, xs)
    return intra + jnp.moveaxis(cross, 0, 2)


def bidirectional_retention(q, k, v, decay_logit):
    B, S, H, dv = v.shape
    N = S // CHUNK
    log_gamma = jax.nn.log_sigmoid(decay_logit.astype(jnp.float32))

    def chunk(t):
        return jnp.transpose(t.astype(jnp.float32), (0, 2, 1, 3)).reshape(B, H, N, CHUNK, t.shape[-1])

    def rev(t):
        return jnp.flip(t, axis=1)

    fwd = retention_direction(chunk(q), chunk(k), chunk(v), log_gamma[0], True)
    bwd = retention_direction(chunk(rev(q)), chunk(rev(k)), chunk(rev(v)), log_gamma[1], False)
    out = fwd.reshape(B, H, S, dv) + jnp.flip(bwd.reshape(B, H, S, dv), axis=2)
    return jnp.transpose(out, (0, 2, 1, 3))


def _fwd_setup_inputs(seed: int = 0) -> dict:
    key = jax.random.key(seed)
    ks = jax.random.split(key, 24)
    L, D = DEPTH, D_MODEL

    def nrm(k, shape, scale):
        return jax.random.normal(k, shape, jnp.float32) * scale

    base_logit = jnp.log(2.0 ** (5.0 + jnp.arange(RET_HEADS, dtype=jnp.float32)) - 1.0)
    return {
        "x": nrm(ks[0], (BATCH, SEQ, D), 1.0),
        "ffn1_norm": 1.0 + nrm(ks[1], (L, D), 0.02),
        "ffn1_w_gate": nrm(ks[2], (L, D, D_FF), D ** -0.5),
        "ffn1_w_up": nrm(ks[3], (L, D, D_FF), D ** -0.5),
        "ffn1_w_down": nrm(ks[4], (L, D_FF, D), D_FF ** -0.5),
        "mix_norm": 1.0 + nrm(ks[5], (L, D), 0.02),
        "w_in": nrm(ks[6], (L, D, IN_WIDTH), D ** -0.5),
        "b_in": nrm(ks[7], (L, IN_WIDTH), 0.02),
        "sgu_norm_g": 1.0 + nrm(ks[8], (L, SGU_WIDTH), 0.02),
        "sgu_norm_b": nrm(ks[9], (L, SGU_WIDTH), 0.02),
        "sgu_w_s": nrm(ks[10], (L, SGU_GROUPS, CHUNK, CHUNK), CHUNK ** -0.5),
        "sgu_b_s": 1.0 + nrm(ks[11], (L, SGU_GROUPS, CHUNK), 0.1),
        "ret_decay_logit": jnp.broadcast_to(base_logit, (L, 2, RET_HEADS)) + nrm(ks[12], (L, 2, RET_HEADS), 0.05),
        "w_branch_a": nrm(ks[13], (L, SGU_WIDTH, D), SGU_WIDTH ** -0.5),
        "w_branch_b": nrm(ks[14], (L, RET_WIDTH, D), RET_WIDTH ** -0.5),
        "w_out": nrm(ks[15], (L, D, D), D ** -0.5),
        "ffn2_norm": 1.0 + nrm(ks[16], (L, D), 0.02),
        "ffn2_w_gate": nrm(ks[17], (L, D, D_FF), D ** -0.5),
        "ffn2_w_up": nrm(ks[18], (L, D, D_FF), D ** -0.5),
        "ffn2_w_down": nrm(ks[19], (L, D_FF, D), D_FF ** -0.5),
        "final_norm": 1.0 + nrm(ks[20], (D,), 0.02),
    }


def _fwd_reference(x, ffn1_norm, ffn1_w_gate, ffn1_w_up, ffn1_w_down, mix_norm, w_in, b_in,
              sgu_norm_g, sgu_norm_b, sgu_w_s, sgu_b_s, ret_decay_logit,
              w_branch_a, w_branch_b, w_out, ffn2_norm, ffn2_w_gate, ffn2_w_up, ffn2_w_down,
              final_norm):
    B, S, _ = x.shape
    split_at = list(np.cumsum(IN_WIDTHS)[:-1])
    for l in range(DEPTH):
        x = x + 0.5 * swiglu_ffn(rms_norm(x, ffn1_norm[l]), ffn1_w_gate[l], ffn1_w_up[l], ffn1_w_down[l])

        h = rms_norm(x, mix_norm[l])
        proj = h @ w_in[l] + b_in[l]
        u_a, v_a, q_r, k_r, v_r, g_r, gate_a, gate_b = jnp.split(proj, split_at, axis=-1)

        a = spatial_gating(jax.nn.gelu(u_a, approximate=False), jax.nn.gelu(v_a, approximate=False),
                           sgu_norm_g[l], sgu_norm_b[l], sgu_w_s[l], sgu_b_s[l])

        q = rotary(q_r.reshape(B, S, RET_HEADS, RET_QK_DIM))
        k = rotary(k_r.reshape(B, S, RET_HEADS, RET_QK_DIM)) * (RET_QK_DIM ** -0.5)
        v = v_r.reshape(B, S, RET_HEADS, RET_V_DIM)
        r = bidirectional_retention(q, k, v, ret_decay_logit[l])
        r = r * lax.rsqrt(jnp.mean(r * r, axis=-1, keepdims=True) + NORM_EPS)
        r = r.reshape(B, S, RET_WIDTH).astype(x.dtype) * jax.nn.silu(g_r)

        mix = jax.nn.sigmoid(gate_a) * (a @ w_branch_a[l]) + jax.nn.sigmoid(gate_b) * (r @ w_branch_b[l])
        x = x + mix @ w_out[l]

        x = x + 0.5 * swiglu_ffn(rms_norm(x, ffn2_norm[l]), ffn2_w_gate[l], ffn2_w_up[l], ffn2_w_down[l])
    return rms_norm(x, final_norm)


import jax as _jax
import jax.numpy as _jnp

TWIN_FORMAT = 'train_step'
FWD_PARAMS = ['x', 'ffn1_norm', 'ffn1_w_gate', 'ffn1_w_up', 'ffn1_w_down', 'mix_norm', 'w_in', 'b_in', 'sgu_norm_g', 'sgu_norm_b', 'sgu_w_s', 'sgu_b_s', 'ret_decay_logit', 'w_branch_a', 'w_branch_b', 'w_out', 'ffn2_norm', 'ffn2_w_gate', 'ffn2_w_up', 'ffn2_w_down', 'final_norm']
TWIN_WEIGHTS = ['ffn1_norm', 'ffn1_w_gate', 'ffn1_w_up', 'ffn1_w_down', 'mix_norm', 'w_in', 'b_in', 'sgu_norm_g', 'sgu_norm_b', 'sgu_w_s', 'sgu_b_s', 'ret_decay_logit', 'w_branch_a', 'w_branch_b', 'w_out', 'ffn2_norm', 'ffn2_w_gate', 'ffn2_w_up', 'ffn2_w_down', 'final_norm']
TWIN_DIFF_INPUT = 'x'
TWIN_INPUTS = ['x', 'ffn1_norm', 'ffn1_w_gate', 'ffn1_w_up', 'ffn1_w_down', 'mix_norm', 'w_in', 'b_in', 'sgu_norm_g', 'sgu_norm_b', 'sgu_w_s', 'sgu_b_s', 'ret_decay_logit', 'w_branch_a', 'w_branch_b', 'w_out', 'ffn2_norm', 'ffn2_w_gate', 'ffn2_w_up', 'ffn2_w_down', 'final_norm', 'loss_target', 'm_ffn1_norm', 'm_ffn1_w_gate', 'm_ffn1_w_up', 'm_ffn1_w_down', 'm_mix_norm', 'm_w_in', 'm_b_in', 'm_sgu_norm_g', 'm_sgu_norm_b', 'm_sgu_w_s', 'm_sgu_b_s', 'm_ret_decay_logit', 'm_w_branch_a', 'm_w_branch_b', 'm_w_out', 'm_ffn2_norm', 'm_ffn2_w_gate', 'm_ffn2_w_up', 'm_ffn2_w_down', 'm_final_norm', 'v_ffn1_norm', 'v_ffn1_w_gate', 'v_ffn1_w_up', 'v_ffn1_w_down', 'v_mix_norm', 'v_w_in', 'v_b_in', 'v_sgu_norm_g', 'v_sgu_norm_b', 'v_sgu_w_s', 'v_sgu_b_s', 'v_ret_decay_logit', 'v_w_branch_a', 'v_w_branch_b', 'v_w_out', 'v_ffn2_norm', 'v_ffn2_w_gate', 'v_ffn2_w_up', 'v_ffn2_w_down', 'v_final_norm']
TWIN_OUTPUTS = ['loss', 'grad_x', 'grad_ffn1_norm', 'grad_ffn1_w_gate', 'grad_ffn1_w_up', 'grad_ffn1_w_down', 'grad_mix_norm', 'grad_w_in', 'grad_b_in', 'grad_sgu_norm_g', 'grad_sgu_norm_b', 'grad_sgu_w_s', 'grad_sgu_b_s', 'grad_ret_decay_logit', 'grad_w_branch_a', 'grad_w_branch_b', 'grad_w_out', 'grad_ffn2_norm', 'grad_ffn2_w_gate', 'grad_ffn2_w_up', 'grad_ffn2_w_down', 'grad_final_norm', 'delta_ffn1_norm', 'delta_ffn1_w_gate', 'delta_ffn1_w_up', 'delta_ffn1_w_down', 'delta_mix_norm', 'delta_w_in', 'delta_b_in', 'delta_sgu_norm_g', 'delta_sgu_norm_b', 'delta_sgu_w_s', 'delta_sgu_b_s', 'delta_ret_decay_logit', 'delta_w_branch_a', 'delta_w_branch_b', 'delta_w_out', 'delta_ffn2_norm', 'delta_ffn2_w_gate', 'delta_ffn2_w_up', 'delta_ffn2_w_down', 'delta_final_norm', 'new_m_ffn1_norm', 'new_m_ffn1_w_gate', 'new_m_ffn1_w_up', 'new_m_ffn1_w_down', 'new_m_mix_norm', 'new_m_w_in', 'new_m_b_in', 'new_m_sgu_norm_g', 'new_m_sgu_norm_b', 'new_m_sgu_w_s', 'new_m_sgu_b_s', 'new_m_ret_decay_logit', 'new_m_w_branch_a', 'new_m_w_branch_b', 'new_m_w_out', 'new_m_ffn2_norm', 'new_m_ffn2_w_gate', 'new_m_ffn2_w_up', 'new_m_ffn2_w_down', 'new_m_final_norm', 'new_v_ffn1_norm', 'new_v_ffn1_w_gate', 'new_v_ffn1_w_up', 'new_v_ffn1_w_down', 'new_v_mix_norm', 'new_v_w_in', 'new_v_b_in', 'new_v_sgu_norm_g', 'new_v_sgu_norm_b', 'new_v_sgu_w_s', 'new_v_sgu_b_s', 'new_v_ret_decay_logit', 'new_v_w_branch_a', 'new_v_w_branch_b', 'new_v_w_out', 'new_v_ffn2_norm', 'new_v_ffn2_w_gate', 'new_v_ffn2_w_up', 'new_v_ffn2_w_down', 'new_v_final_norm']
TWIN_LEAF_KINDS = {'loss': 'loss', 'grad_x': 'grad_x', 'grad_ffn1_norm': 'grad_w', 'grad_ffn1_w_gate': 'grad_w', 'grad_ffn1_w_up': 'grad_w', 'grad_ffn1_w_down': 'grad_w', 'grad_mix_norm': 'grad_w', 'grad_w_in': 'grad_w', 'grad_b_in': 'grad_w', 'grad_sgu_norm_g': 'grad_w', 'grad_sgu_norm_b': 'grad_w', 'grad_sgu_w_s': 'grad_w', 'grad_sgu_b_s': 'grad_w', 'grad_ret_decay_logit': 'grad_w', 'grad_w_branch_a': 'grad_w', 'grad_w_branch_b': 'grad_w', 'grad_w_out': 'grad_w', 'grad_ffn2_norm': 'grad_w', 'grad_ffn2_w_gate': 'grad_w', 'grad_ffn2_w_up': 'grad_w', 'grad_ffn2_w_down': 'grad_w', 'grad_final_norm': 'grad_w', 'delta_ffn1_norm': 'delta_w', 'delta_ffn1_w_gate': 'delta_w', 'delta_ffn1_w_up': 'delta_w', 'delta_ffn1_w_down': 'delta_w', 'delta_mix_norm': 'delta_w', 'delta_w_in': 'delta_w', 'delta_b_in': 'delta_w', 'delta_sgu_norm_g': 'delta_w', 'delta_sgu_norm_b': 'delta_w', 'delta_sgu_w_s': 'delta_w', 'delta_sgu_b_s': 'delta_w', 'delta_ret_decay_logit': 'delta_w', 'delta_w_branch_a': 'delta_w', 'delta_w_branch_b': 'delta_w', 'delta_w_out': 'delta_w', 'delta_ffn2_norm': 'delta_w', 'delta_ffn2_w_gate': 'delta_w', 'delta_ffn2_w_up': 'delta_w', 'delta_ffn2_w_down': 'delta_w', 'delta_final_norm': 'delta_w', 'new_m_ffn1_norm': 'new_m', 'new_m_ffn1_w_gate': 'new_m', 'new_m_ffn1_w_up': 'new_m', 'new_m_ffn1_w_down': 'new_m', 'new_m_mix_norm': 'new_m', 'new_m_w_in': 'new_m', 'new_m_b_in': 'new_m', 'new_m_sgu_norm_g': 'new_m', 'new_m_sgu_norm_b': 'new_m', 'new_m_sgu_w_s': 'new_m', 'new_m_sgu_b_s': 'new_m', 'new_m_ret_decay_logit': 'new_m', 'new_m_w_branch_a': 'new_m', 'new_m_w_branch_b': 'new_m', 'new_m_w_out': 'new_m', 'new_m_ffn2_norm': 'new_m', 'new_m_ffn2_w_gate': 'new_m', 'new_m_ffn2_w_up': 'new_m', 'new_m_ffn2_w_down': 'new_m', 'new_m_final_norm': 'new_m', 'new_v_ffn1_norm': 'new_v', 'new_v_ffn1_w_gate': 'new_v', 'new_v_ffn1_w_up': 'new_v', 'new_v_ffn1_w_down': 'new_v', 'new_v_mix_norm': 'new_v', 'new_v_w_in': 'new_v', 'new_v_b_in': 'new_v', 'new_v_sgu_norm_g': 'new_v', 'new_v_sgu_norm_b': 'new_v', 'new_v_sgu_w_s': 'new_v', 'new_v_sgu_b_s': 'new_v', 'new_v_ret_decay_logit': 'new_v', 'new_v_w_branch_a': 'new_v', 'new_v_w_branch_b': 'new_v', 'new_v_w_out': 'new_v', 'new_v_ffn2_norm': 'new_v', 'new_v_ffn2_w_gate': 'new_v', 'new_v_ffn2_w_up': 'new_v', 'new_v_ffn2_w_down': 'new_v', 'new_v_final_norm': 'new_v'}


def _forward(args):
    return _fwd_reference(*[args[k] for k in FWD_PARAMS])


def _output_shape():
    def fwd():
        inp = _fwd_setup_inputs(0)
        return _fwd_reference(*[inp[k] for k in FWD_PARAMS])
    out = _jax.eval_shape(fwd)
    return out.shape, out.dtype

N_MICROBATCH = 1
ADAM_LR = 0.001
ADAM_B1 = 0.9
ADAM_B2 = 0.999
ADAM_EPS = 1e-08
ADAM_WD = 0.01
ADAM_STEP = 10
PER_EXAMPLE_BATCH_AXIS = {'x': 0, 'loss_target': 0}
SHARED_INPUTS = []
_WEIGHT_DTYPES = {'ffn1_norm': _jnp.float32, 'ffn1_w_gate': _jnp.float32, 'ffn1_w_up': _jnp.float32, 'ffn1_w_down': _jnp.float32, 'mix_norm': _jnp.float32, 'w_in': _jnp.float32, 'b_in': _jnp.float32, 'sgu_norm_g': _jnp.float32, 'sgu_norm_b': _jnp.float32, 'sgu_w_s': _jnp.float32, 'sgu_b_s': _jnp.float32, 'ret_decay_logit': _jnp.float32, 'w_branch_a': _jnp.float32, 'w_branch_b': _jnp.float32, 'w_out': _jnp.float32, 'ffn2_norm': _jnp.float32, 'ffn2_w_gate': _jnp.float32, 'ffn2_w_up': _jnp.float32, 'ffn2_w_down': _jnp.float32, 'final_norm': _jnp.float32}
MOMENT_SCALE = {'ffn1_norm': 8.996756e-02, 'ffn1_w_gate': 3.814726e-02, 'ffn1_w_up': 3.689190e-02, 'ffn1_w_down': 6.119773e-02, 'mix_norm': 1.559799e-01, 'w_in': 5.157344e-02, 'b_in': 5.168211e-02, 'sgu_norm_g': 5.552659e-02, 'sgu_norm_b': 5.506681e-02, 'sgu_w_s': 7.581522e-02, 'sgu_b_s': 8.047773e-02, 'ret_decay_logit': 3.161622e-01, 'w_branch_a': 7.740241e-02, 'w_branch_b': 4.919625e-02, 'w_out': 9.170522e-02, 'ffn2_norm': 6.573935e-02, 'ffn2_w_gate': 2.661599e-02, 'ffn2_w_up': 2.582453e-02, 'ffn2_w_down': 4.269743e-02, 'final_norm': 3.203787e+01}


def _to_microbatches(a, axis):
    t = _jnp.moveaxis(a, axis, 0)
    t = t.reshape((N_MICROBATCH, t.shape[0] // N_MICROBATCH) + t.shape[1:])
    return _jnp.moveaxis(t, 1, axis + 1)


def setup_inputs(seed: int = 0) -> dict:
    inp = _fwd_setup_inputs(seed)
    key = _jax.random.fold_in(_jax.random.key(seed), 7919)
    shape, _ = _output_shape()
    out = dict(inp)
    out["loss_target"] = _jax.random.normal(_jax.random.fold_in(key, 0), shape, _jnp.float32)
    for i, name in enumerate(TWIN_WEIGHTS):
        w = inp[name].astype(_jnp.float32)
        if MOMENT_SCALE is None:
            s = _jnp.sqrt(_jnp.mean(_jnp.square(w)) + 1e-30)
        else:
            s = MOMENT_SCALE[name]
        km, kv = _jax.random.split(_jax.random.fold_in(key, i + 1))
        out[name] = w
        out["m_" + name] = s * _jax.random.normal(km, w.shape, _jnp.float32)
        out["v_" + name] = (s * s) * _jax.random.uniform(kv, w.shape, _jnp.float32, 0.5, 1.5)
    if N_MICROBATCH > 1:
        for name, axis in PER_EXAMPLE_BATCH_AXIS.items():
            out[name] = _to_microbatches(out[name], axis)
    return {'x': out['x'], 'ffn1_norm': out['ffn1_norm'], 'ffn1_w_gate': out['ffn1_w_gate'], 'ffn1_w_up': out['ffn1_w_up'], 'ffn1_w_down': out['ffn1_w_down'], 'mix_norm': out['mix_norm'], 'w_in': out['w_in'], 'b_in': out['b_in'], 'sgu_norm_g': out['sgu_norm_g'], 'sgu_norm_b': out['sgu_norm_b'], 'sgu_w_s': out['sgu_w_s'], 'sgu_b_s': out['sgu_b_s'], 'ret_decay_logit': out['ret_decay_logit'], 'w_branch_a': out['w_branch_a'], 'w_branch_b': out['w_branch_b'], 'w_out': out['w_out'], 'ffn2_norm': out['ffn2_norm'], 'ffn2_w_gate': out['ffn2_w_gate'], 'ffn2_w_up': out['ffn2_w_up'], 'ffn2_w_down': out['ffn2_w_down'], 'final_norm': out['final_norm'], 'loss_target': out['loss_target'], 'm_ffn1_norm': out['m_ffn1_norm'], 'm_ffn1_w_gate': out['m_ffn1_w_gate'], 'm_ffn1_w_up': out['m_ffn1_w_up'], 'm_ffn1_w_down': out['m_ffn1_w_down'], 'm_mix_norm': out['m_mix_norm'], 'm_w_in': out['m_w_in'], 'm_b_in': out['m_b_in'], 'm_sgu_norm_g': out['m_sgu_norm_g'], 'm_sgu_norm_b': out['m_sgu_norm_b'], 'm_sgu_w_s': out['m_sgu_w_s'], 'm_sgu_b_s': out['m_sgu_b_s'], 'm_ret_decay_logit': out['m_ret_decay_logit'], 'm_w_branch_a': out['m_w_branch_a'], 'm_w_branch_b': out['m_w_branch_b'], 'm_w_out': out['m_w_out'], 'm_ffn2_norm': out['m_ffn2_norm'], 'm_ffn2_w_gate': out['m_ffn2_w_gate'], 'm_ffn2_w_up': out['m_ffn2_w_up'], 'm_ffn2_w_down': out['m_ffn2_w_down'], 'm_final_norm': out['m_final_norm'], 'v_ffn1_norm': out['v_ffn1_norm'], 'v_ffn1_w_gate': out['v_ffn1_w_gate'], 'v_ffn1_w_up': out['v_ffn1_w_up'], 'v_ffn1_w_down': out['v_ffn1_w_down'], 'v_mix_norm': out['v_mix_norm'], 'v_w_in': out['v_w_in'], 'v_b_in': out['v_b_in'], 'v_sgu_norm_g': out['v_sgu_norm_g'], 'v_sgu_norm_b': out['v_sgu_norm_b'], 'v_sgu_w_s': out['v_sgu_w_s'], 'v_sgu_b_s': out['v_sgu_b_s'], 'v_ret_decay_logit': out['v_ret_decay_logit'], 'v_w_branch_a': out['v_w_branch_a'], 'v_w_branch_b': out['v_w_branch_b'], 'v_w_out': out['v_w_out'], 'v_ffn2_norm': out['v_ffn2_norm'], 'v_ffn2_w_gate': out['v_ffn2_w_gate'], 'v_ffn2_w_up': out['v_ffn2_w_up'], 'v_ffn2_w_down': out['v_ffn2_w_down'], 'v_final_norm': out['v_final_norm']}


def _loss(weights, diff, rest, loss_target):
    with _jax.named_scope("forward"):
        args = {**rest, TWIN_DIFF_INPUT: diff, **{k: w.astype(_WEIGHT_DTYPES[k]) for k, w in weights.items()}}
        y = _forward(args)
    with _jax.named_scope("loss_head"):
        err = _jnp.square(y.astype(_jnp.float32) - loss_target)
        return 0.5 * _jnp.sum(_jnp.mean(err, axis=-1)) if err.ndim else 0.5 * err


def _adamw(w, g, m, v):
    m = ADAM_B1 * m + (1.0 - ADAM_B1) * g
    v = ADAM_B2 * v + (1.0 - ADAM_B2) * _jnp.square(g)
    m_hat = m / (1.0 - ADAM_B1 ** ADAM_STEP)
    v_hat = v / (1.0 - ADAM_B2 ** ADAM_STEP)
    delta = -ADAM_LR * (m_hat / (_jnp.sqrt(v_hat) + ADAM_EPS) + ADAM_WD * w)
    return delta, m, v


def reference(x, ffn1_norm, ffn1_w_gate, ffn1_w_up, ffn1_w_down, mix_norm, w_in, b_in, sgu_norm_g, sgu_norm_b, sgu_w_s, sgu_b_s, ret_decay_logit, w_branch_a, w_branch_b, w_out, ffn2_norm, ffn2_w_gate, ffn2_w_up, ffn2_w_down, final_norm, loss_target, m_ffn1_norm, m_ffn1_w_gate, m_ffn1_w_up, m_ffn1_w_down, m_mix_norm, m_w_in, m_b_in, m_sgu_norm_g, m_sgu_norm_b, m_sgu_w_s, m_sgu_b_s, m_ret_decay_logit, m_w_branch_a, m_w_branch_b, m_w_out, m_ffn2_norm, m_ffn2_w_gate, m_ffn2_w_up, m_ffn2_w_down, m_final_norm, v_ffn1_norm, v_ffn1_w_gate, v_ffn1_w_up, v_ffn1_w_down, v_mix_norm, v_w_in, v_b_in, v_sgu_norm_g, v_sgu_norm_b, v_sgu_w_s, v_sgu_b_s, v_ret_decay_logit, v_w_branch_a, v_w_branch_b, v_w_out, v_ffn2_norm, v_ffn2_w_gate, v_ffn2_w_up, v_ffn2_w_down, v_final_norm):
    given = dict(x=x, ffn1_norm=ffn1_norm, ffn1_w_gate=ffn1_w_gate, ffn1_w_up=ffn1_w_up, ffn1_w_down=ffn1_w_down, mix_norm=mix_norm, w_in=w_in, b_in=b_in, sgu_norm_g=sgu_norm_g, sgu_norm_b=sgu_norm_b, sgu_w_s=sgu_w_s, sgu_b_s=sgu_b_s, ret_decay_logit=ret_decay_logit, w_branch_a=w_branch_a, w_branch_b=w_branch_b, w_out=w_out, ffn2_norm=ffn2_norm, ffn2_w_gate=ffn2_w_gate, ffn2_w_up=ffn2_w_up, ffn2_w_down=ffn2_w_down, final_norm=final_norm, loss_target=loss_target, m_ffn1_norm=m_ffn1_norm, m_ffn1_w_gate=m_ffn1_w_gate, m_ffn1_w_up=m_ffn1_w_up, m_ffn1_w_down=m_ffn1_w_down, m_mix_norm=m_mix_norm, m_w_in=m_w_in, m_b_in=m_b_in, m_sgu_norm_g=m_sgu_norm_g, m_sgu_norm_b=m_sgu_norm_b, m_sgu_w_s=m_sgu_w_s, m_sgu_b_s=m_sgu_b_s, m_ret_decay_logit=m_ret_decay_logit, m_w_branch_a=m_w_branch_a, m_w_branch_b=m_w_branch_b, m_w_out=m_w_out, m_ffn2_norm=m_ffn2_norm, m_ffn2_w_gate=m_ffn2_w_gate, m_ffn2_w_up=m_ffn2_w_up, m_ffn2_w_down=m_ffn2_w_down, m_final_norm=m_final_norm, v_ffn1_norm=v_ffn1_norm, v_ffn1_w_gate=v_ffn1_w_gate, v_ffn1_w_up=v_ffn1_w_up, v_ffn1_w_down=v_ffn1_w_down, v_mix_norm=v_mix_norm, v_w_in=v_w_in, v_b_in=v_b_in, v_sgu_norm_g=v_sgu_norm_g, v_sgu_norm_b=v_sgu_norm_b, v_sgu_w_s=v_sgu_w_s, v_sgu_b_s=v_sgu_b_s, v_ret_decay_logit=v_ret_decay_logit, v_w_branch_a=v_w_branch_a, v_w_branch_b=v_w_branch_b, v_w_out=v_w_out, v_ffn2_norm=v_ffn2_norm, v_ffn2_w_gate=v_ffn2_w_gate, v_ffn2_w_up=v_ffn2_w_up, v_ffn2_w_down=v_ffn2_w_down, v_final_norm=v_final_norm)
    weights = {n: given[n] for n in TWIN_WEIGHTS}
    shared = {n: given[n] for n in SHARED_INPUTS}
    per_example = {n: given[n] for n in ['x']}
    grad_fn = _jax.value_and_grad(_loss, argnums=(0, 1))

    def one_microbatch(ex, loss_target):
        ex = dict(ex)
        diff = ex.pop(TWIN_DIFF_INPUT)
        return grad_fn(weights, diff, {**shared, **ex}, loss_target)

    if N_MICROBATCH == 1:
        loss, (grad_w, grad_x) = one_microbatch(per_example, given["loss_target"])
    else:
        def body(carry, xs):
            loss_sum, grad_sum = carry
            l_k, (gw_k, gx_k) = one_microbatch(xs[0], xs[1])
            with _jax.named_scope("update"):
                return (loss_sum + l_k, _jax.tree.map(_jnp.add, grad_sum, gw_k)), gx_k

        init = (_jnp.zeros((), _jnp.float32), _jax.tree.map(_jnp.zeros_like, weights))
        (loss, grad_w), grad_x = _jax.lax.scan(body, init, (per_example, given["loss_target"]))
    with _jax.named_scope("update"):
        delta_w, new_m, new_v = {}, {}, {}
        for n in TWIN_WEIGHTS:
            delta_w[n], new_m[n], new_v[n] = _adamw(weights[n], grad_w[n], given["m_" + n], given["v_" + n])
    return (loss, grad_x, *[grad_w[n] for n in TWIN_WEIGHTS], *[delta_w[n] for n in TWIN_WEIGHTS],
            *[new_m[n] for n in TWIN_WEIGHTS], *[new_v[n] for n in TWIN_WEIGHTS])
```

```python
import functools

import jax
import jax.numpy as jnp
from jax import lax
from jax.experimental import pallas as pl
from jax.experimental.pallas import tpu as pltpu

f32 = jnp.float32
bf16 = jnp.bfloat16

CHUNK = 128
RET_HEADS = 4
SGU_GROUPS = 4
ROPE_BASE = 10000.0
NORM_EPS = 1e-6
ADAM_LR = 0.001
ADAM_B1 = 0.9
ADAM_B2 = 0.999
ADAM_EPS = 1e-08
ADAM_WD = 0.01
ADAM_STEP = 10
N_CHIPS = 4
N_DEV = 8
MESH = pl.DeviceIdType.MESH
VMEM_LIMIT = 52 * 1024 * 1024

_NT = (((1,), (1,)), ((), ()))
_TN = (((0,), (0,)), ((), ()))


def _cparams():
    return pltpu.CompilerParams(vmem_limit_bytes=VMEM_LIMIT)


def _row_tile(t):
    return 512 if t >= 2048 else t // 2


def _dot(a, b):
    return jnp.dot(a, b, preferred_element_type=f32)


def _dot_nt(a, b):
    return lax.dot_general(a, b, _NT, preferred_element_type=f32)


def _dot_tn(a, b):
    return lax.dot_general(a, b, _TN, preferred_element_type=f32)


def _rms(x, g):
    r = lax.rsqrt(jnp.mean(x * x, axis=-1, keepdims=True) + NORM_EPS)
    xh = x * r
    return xh * g, xh, r


def _rms_bwd(dy, xh, r, g):
    dxh = dy * g
    return r * (dxh - xh * jnp.mean(dxh * xh, axis=-1, keepdims=True))


def _sigmoid(x):
    return jax.nn.sigmoid(x)


def _dsilu(g, sg):
    return sg * (1.0 + g * (1.0 - sg))


def _gelu(x):
    return 0.5 * x * (1.0 + lax.erf(x * 0.7071067811865476))


def _dgelu(x):
    return 0.5 * (1.0 + lax.erf(x * 0.7071067811865476)) + x * jnp.exp(-0.5 * x * x) * 0.3989422804014327


def _acc_out(ref, first, val):
    @pl.when(first)
    def _():
        ref[...] = val

    @pl.when(jnp.logical_not(first))
    def _():
        ref[...] += val


def ffn_fwd(x, ng, wg, wu, wd, name):
    t, d = x.shape
    s4, _, fs = wg.shape
    tm = _row_tile(t)

    def body(x_ref, ng_ref, wg_ref, wu_ref, wd_ref, xo_ref, g_ref, u_ref, h_scr, acc_scr):
        s = pl.program_id(1)

        @pl.when(s == 0)
        def _():
            y, _, _ = _rms(x_ref[...], ng_ref[...])
            h_scr[...] = y.astype(bf16)
            acc_scr[...] = jnp.zeros_like(acc_scr)

        h = h_scr[...]
        g = _dot(h, wg_ref[0])
        u = _dot(h, wu_ref[0])
        g_ref[0] = g.astype(bf16)
        u_ref[0] = u.astype(bf16)
        act = (g * _sigmoid(g) * u).astype(bf16)
        acc_scr[...] += _dot(act, wd_ref[0])

        @pl.when(s == s4 - 1)
        def _():
            xo_ref[...] = x_ref[...] + 0.5 * acc_scr[...]

    return pl.pallas_call(
        body, name=name, grid=(t // tm, s4),
        in_specs=[pl.BlockSpec((tm, d), lambda i, s: (i, 0)), pl.BlockSpec((1, d), lambda i, s: (0, 0)),
                  pl.BlockSpec((1, d, fs), lambda i, s: (s, 0, 0)), pl.BlockSpec((1, d, fs), lambda i, s: (s, 0, 0)),
                  pl.BlockSpec((1, fs, d), lambda i, s: (s, 0, 0))],
        out_specs=[pl.BlockSpec((tm, d), lambda i, s: (i, 0)), pl.BlockSpec((1, tm, fs), lambda i, s: (s, i, 0)),
                   pl.BlockSpec((1, tm, fs), lambda i, s: (s, i, 0))],
        out_shape=[jax.ShapeDtypeStruct((t, d), f32), jax.ShapeDtypeStruct((s4, t, fs), bf16),
                   jax.ShapeDtypeStruct((s4, t, fs), bf16)],
        scratch_shapes=[pltpu.VMEM((tm, d), bf16), pltpu.VMEM((tm, d), f32)],
        compiler_params=_cparams(),
    )(x, ng, wg, wu, wd)


def ffn_bwd_act(dxo, x, ng, g, u, wg, wu, wd, name):
    t, d = x.shape
    s4, _, fs = wg.shape
    tm = _row_tile(t)

    def body(dxo_ref, x_ref, ng_ref, g_ref, u_ref, wg_ref, wu_ref, wd_ref,
             dx_ref, dg_ref, du_ref, act_ref, hb_ref, dyb_ref, dng_ref, dy_scr, acc_scr):
        i = pl.program_id(0)
        s = pl.program_id(1)

        @pl.when(s == 0)
        def _():
            dyb = (0.5 * dxo_ref[...]).astype(bf16)
            dy_scr[...] = dyb
            dyb_ref[...] = dyb
            acc_scr[...] = jnp.zeros_like(acc_scr)

        dact = _dot_nt(dy_scr[...], wd_ref[0])
        gg = g_ref[0].astype(f32)
        uu = u_ref[0].astype(f32)
        sg = _sigmoid(gg)
        sil = gg * sg
        dgb = (dact * uu * _dsilu(gg, sg)).astype(bf16)
        dub = (dact * sil).astype(bf16)
        dg_ref[0] = dgb
        du_ref[0] = dub
        act_ref[0] = (sil * uu).astype(bf16)
        acc_scr[...] += _dot_nt(dgb, wg_ref[0]) + _dot_nt(dub, wu_ref[0])

        @pl.when(s == s4 - 1)
        def _():
            y, xh, r = _rms(x_ref[...], ng_ref[...])
            hb_ref[...] = y.astype(bf16)
            dh = acc_scr[...]
            dx_ref[...] = dxo_ref[...] + _rms_bwd(dh, xh, r, ng_ref[...])
            _acc_out(dng_ref, i == 0, jnp.sum(dh * xh, axis=0, keepdims=True))

    row = lambda i, s: (i, 0)
    shard = lambda i, s: (s, i, 0)
    wsp = lambda i, s: (s, 0, 0)
    return pl.pallas_call(
        body, name=name, grid=(t // tm, s4),
        in_specs=[pl.BlockSpec((tm, d), row), pl.BlockSpec((tm, d), row), pl.BlockSpec((1, d), lambda i, s: (0, 0)),
                  pl.BlockSpec((1, tm, fs), shard), pl.BlockSpec((1, tm, fs), shard),
                  pl.BlockSpec((1, d, fs), wsp), pl.BlockSpec((1, d, fs), wsp), pl.BlockSpec((1, fs, d), wsp)],
        out_specs=[pl.BlockSpec((tm, d), row), pl.BlockSpec((1, tm, fs), shard), pl.BlockSpec((1, tm, fs), shard),
                   pl.BlockSpec((1, tm, fs), shard), pl.BlockSpec((tm, d), row), pl.BlockSpec((tm, d), row),
                   pl.BlockSpec((1, d), lambda i, s: (0, 0))],
        out_shape=[jax.ShapeDtypeStruct((t, d), f32), jax.ShapeDtypeStruct((s4, t, fs), bf16),
                   jax.ShapeDtypeStruct((s4, t, fs), bf16), jax.ShapeDtypeStruct((s4, t, fs), bf16),
                   jax.ShapeDtypeStruct((t, d), bf16), jax.ShapeDtypeStruct((t, d), bf16),
                   jax.ShapeDtypeStruct((1, d), f32)],
        scratch_shapes=[pltpu.VMEM((tm, d), bf16), pltpu.VMEM((tm, d), f32)],
        compiler_params=_cparams(),
    )(dxo, x, ng, g, u, wg, wu, wd)


def tn_matmul(xs, ys, x_spec, y_specs, n_shards, k1, k2s, t, tm, name):
    k2 = sum(k2s)
    ny = len(ys)

    def body(*refs):
        x_ref = refs[0]
        y_refs = refs[1:1 + ny]
        o_ref = refs[1 + ny]
        acc = refs[2 + ny]
        i = pl.program_id(1)
        xb = x_ref[0] if len(x_ref.shape) == 3 else x_ref[...]
        off = 0
        for y_ref, w in zip(y_refs, k2s):
            yb = y_ref[0] if len(y_ref.shape) == 3 else y_ref[...]
            part = _dot_tn(xb, yb)
            sl = (slice(None), slice(off, off + w))

            @pl.when(i == 0)
            def _(part=part, sl=sl):
                acc[sl] = part

            @pl.when(i > 0)
            def _(part=part, sl=sl):
                acc[sl] += part

            off += w

        @pl.when(i == t // tm - 1)
        def _():
            o_ref[0] = acc[...].astype(bf16)

    return pl.pallas_call(
        body, name=name, grid=(n_shards, t // tm),
        in_specs=[x_spec] + list(y_specs),
        out_specs=pl.BlockSpec((1, k1, k2), lambda s, i: (s, 0, 0)),
        out_shape=jax.ShapeDtypeStruct((n_shards, k1, k2), bf16),
        scratch_shapes=[pltpu.VMEM((k1, k2), f32)],
        compiler_params=_cparams(),
    )(xs, *ys)


def ffn_weight_grads(hb, dyb, dg, du, act, name):
    t, d = hb.shape
    s4, _, fs = dg.shape
    tm = _row_tile(t)
    row = pl.BlockSpec((tm, d), lambda s, i: (i, 0))
    shard = pl.BlockSpec((1, tm, fs), lambda s, i: (s, i, 0))
    gwg = tn_matmul(hb, [dg], row, [shard], s4, d, [fs], t, tm, name + "_wg")
    gwu = tn_matmul(hb, [du], row, [shard], s4, d, [fs], t, tm, name + "_wu")
    gwd = tn_matmul(act, [dyb], shard, [row], s4, fs, [d], t, tm, name + "_wd")
    return gwg, gwu, gwd


def inproj_fwd(x1, ng, win, bin4):
    t, d = x1.shape
    s4, _, w2 = win.shape
    tm = _row_tile(t)

    def body(x_ref, ng_ref, w_ref, b_ref, p_ref, hb_ref, h_scr):
        s = pl.program_id(1)

        @pl.when(s == 0)
        def _():
            y, _, _ = _rms(x_ref[...], ng_ref[...])
            h_scr[...] = y.astype(bf16)
            hb_ref[...] = y.astype(bf16)

        p_ref[0] = (_dot(h_scr[...], w_ref[0]) + b_ref[0]).astype(bf16)

    return pl.pallas_call(
        body, name="inproj_fwd", grid=(t // tm, s4),
        in_specs=[pl.BlockSpec((tm, d), lambda i, s: (i, 0)), pl.BlockSpec((1, d), lambda i, s: (0, 0)),
                  pl.BlockSpec((1, d, w2), lambda i, s: (s, 0, 0)), pl.BlockSpec((1, 1, w2), lambda i, s: (s, 0, 0))],
        out_specs=[pl.BlockSpec((1, tm, w2), lambda i, s: (s, i, 0)), pl.BlockSpec((tm, d), lambda i, s: (i, 0))],
        out_shape=[jax.ShapeDtypeStruct((s4, t, w2), bf16), jax.ShapeDtypeStruct((t, d), bf16)],
        scratch_shapes=[pltpu.VMEM((tm, d), bf16)],
        compiler_params=_cparams(),
    )(x1, ng, win, bin4)


def _sgu_norm(va, ng, nb):
    gv = _gelu(va)
    mu = jnp.mean(gv, axis=-1, keepdims=True)
    xc = gv - mu
    rstd = lax.rsqrt(jnp.mean(xc * xc, axis=-1, keepdims=True) + NORM_EPS)
    xh = xc * rstd
    return xh, rstd, (xh * ng + nb).astype(bf16)


def sgu_fwd(proj, ng, nb, ws, bs):
    _, t, w2 = proj.shape
    d = w2 // 2
    gd = d // SGU_GROUPS
    tm = _row_tile(t)

    def body(p_ref, ng_ref, nb_ref, ws_ref, bs_ref, a_ref):
        ua = p_ref[0, :, 0:d].astype(f32)
        va = p_ref[0, :, d:w2].astype(f32)
        gu = _gelu(ua)
        _, _, vn = _sgu_norm(va, ng_ref[...], nb_ref[...])
        for c in range(tm // CHUNK):
            rows = slice(c * CHUNK, (c + 1) * CHUNK)
            for g in range(SGU_GROUPS):
                cols = slice(g * gd, (g + 1) * gd)
                sg = _dot(ws_ref[g], vn[rows, cols]) + bs_ref[g]
                a_ref[rows, cols] = (gu[rows, cols] * sg).astype(bf16)

    return pl.pallas_call(
        body, name="sgu_fwd", grid=(t // tm,),
        in_specs=[pl.BlockSpec((1, tm, w2), lambda i: (0, i, 0)), pl.BlockSpec((1, d), lambda i: (0, 0)),
                  pl.BlockSpec((1, d), lambda i: (0, 0)), pl.BlockSpec((SGU_GROUPS, CHUNK, CHUNK), lambda i: (0, 0, 0)),
                  pl.BlockSpec((SGU_GROUPS, CHUNK, 1), lambda i: (0, 0, 0))],
        out_specs=pl.BlockSpec((tm, d), lambda i: (i, 0)),
        out_shape=jax.ShapeDtypeStruct((t, d), bf16),
        compiler_params=_cparams(),
    )(proj, ng, nb, ws, bs)


def sgu_bwd(da, proj, ng, nb, ws, bs):
    _, t, w2 = proj.shape
    d = w2 // 2
    gd = d // SGU_GROUPS
    tm = _row_tile(t)

    def body(da_ref, p_ref, ng_ref, nb_ref, ws_ref, bs_ref,
             dua_ref, dva_ref, dws_ref, dbs_ref, dng_ref, dnb_ref, dvn_scr):
        i = pl.program_id(0)
        ua = p_ref[0, :, 0:d].astype(f32)
        va = p_ref[0, :, d:w2].astype(f32)
        gu = _gelu(ua)
        xh, rstd, vn = _sgu_norm(va, ng_ref[...], nb_ref[...])
        dad = da_ref[...].astype(f32)
        dsb = (dad * gu).astype(bf16)
        for c in range(tm // CHUNK):
            rows = slice(c * CHUNK, (c + 1) * CHUNK)
            for g in range(SGU_GROUPS):
                cols = slice(g * gd, (g + 1) * gd)
                sg = _dot(ws_ref[g], vn[rows, cols]) + bs_ref[g]
                dua_ref[rows, cols] = (dad[rows, cols] * sg * _dgelu(ua[rows, cols])).astype(bf16)
                ds = dsb[rows, cols]
                dvn_scr[rows, cols] = _dot_tn(ws_ref[g], ds)
                dw = _dot_nt(ds, vn[rows, cols])
                db = jnp.sum(ds.astype(f32), axis=1, keepdims=True)
                if c == 0:
                    _acc_out(dws_ref.at[g], i == 0, dw)
                    _acc_out(dbs_ref.at[g], i == 0, db)
                else:
                    dws_ref[g] += dw
                    dbs_ref[g] += db
        dvn = dvn_scr[...]
        _acc_out(dng_ref, i == 0, jnp.sum(dvn * xh, axis=0, keepdims=True))
        _acc_out(dnb_ref, i == 0, jnp.sum(dvn, axis=0, keepdims=True))
        dxh = dvn * ng_ref[...]
        dgv = rstd * (dxh - jnp.mean(dxh, axis=-1, keepdims=True) - xh * jnp.mean(dxh * xh, axis=-1, keepdims=True))
        dva_ref[...] = (dgv * _dgelu(va)).astype(bf16)

    row = pl.BlockSpec((tm, d), lambda i: (i, 0))
    vec = pl.BlockSpec((1, d), lambda i: (0, 0))
    wsp = pl.BlockSpec((SGU_GROUPS, CHUNK, CHUNK), lambda i: (0, 0, 0))
    bsp = pl.BlockSpec((SGU_GROUPS, CHUNK, 1), lambda i: (0, 0, 0))
    return pl.pallas_call(
        body, name="sgu_bwd", grid=(t // tm,),
        in_specs=[row, pl.BlockSpec((1, tm, w2), lambda i: (0, i, 0)), vec, vec, wsp, bsp],
        out_specs=[row, row, wsp, bsp, vec, vec],
        out_shape=[jax.ShapeDtypeStruct((t, d), bf16), jax.ShapeDtypeStruct((t, d), bf16),
                   jax.ShapeDtypeStruct((SGU_GROUPS, CHUNK, CHUNK), f32), jax.ShapeDtypeStruct((SGU_GROUPS, CHUNK, 1), f32),
                   jax.ShapeDtypeStruct((1, d), f32), jax.ShapeDtypeStruct((1, d), f32)],
        scratch_shapes=[pltpu.VMEM((tm, d), f32)],
        compiler_params=_cparams(),
    )(da, proj, ng, nb, ws, bs)


def retention_constants(decay_logit, t, dk):
    lg = jax.nn.log_sigmoid(decay_logit.astype(f32))
    lgf = lg[0][:, None]
    lgb = lg[1][:, None]
    idx = jnp.arange(CHUNK, dtype=f32)[None, :]
    af = jnp.exp((idx + 1.0) * lgf)
    ab = jnp.exp((CHUNK - idx) * lgb)
    kf = jnp.exp((CHUNK - 1.0 - idx) * lgf)
    kb = jnp.exp(idx * lgb)
    cols = jnp.stack([af, ab, kf, kb, af * (idx + 1.0), ab * (CHUNK - idx), kf * (CHUNK - 1.0 - idx), kb * idx], axis=1)
    cols = cols[..., None]
    diff = idx[0][:, None] - idx[0][None, :]
    dfm = jnp.where(diff >= 0, jnp.exp(jnp.maximum(diff, 0.0)[None] * lgf[:, :, None]), 0.0)
    dbm = jnp.where(diff < 0, jnp.exp(jnp.maximum(-diff, 0.0)[None] * lgb[:, :, None]), 0.0)
    mats = jnp.stack([dfm + dbm, dfm * diff[None], dbm * (-diff)[None]], axis=1)
    cdec = jnp.stack([jnp.broadcast_to(jnp.exp(CHUNK * lgf), (RET_HEADS, dk)),
                      jnp.broadcast_to(jnp.exp(CHUNK * lgb), (RET_HEADS, dk))], axis=1)
    theta = ROPE_BASE ** (-jnp.arange(0, dk, 2, dtype=f32) / dk)
    ang = jnp.arange(t, dtype=f32)[:, None] * theta[None, :]
    return cols, mats, cdec, jnp.cos(ang), jnp.sin(ang)


def _rot(tr, cos, sin):
    half = tr.shape[-1] // 2
    t1 = tr[:, :half]
    t2 = tr[:, half:]
    return jnp.concatenate([t1 * cos - t2 * sin, t2 * cos + t1 * sin], axis=-1)


def _rot_inv(dt, cos, sin):
    half = dt.shape[-1] // 2
    d1 = dt[:, :half]
    d2 = dt[:, half:]
    return jnp.concatenate([d1 * cos + d2 * sin, d2 * cos - d1 * sin], axis=-1)


def _ret_specs(t, d, dk, rt):
    nr = t // rt
    hq = d // dk

    def blk(p, n):
        return (1 - p) * (nr - 1 - n) + p * n

    q_spec = pl.BlockSpec((1, rt, dk), lambda h, p, n: (1, blk(p, n), h))
    k_spec = pl.BlockSpec((1, rt, dk), lambda h, p, n: (1, blk(p, n), hq + h))
    v_spec = pl.BlockSpec((1, rt, dk), lambda h, p, n: (2, blk(p, n), h))
    g_spec = pl.BlockSpec((1, rt, dk), lambda h, p, n: (2, blk(p, n), hq + h))
    tab_spec = pl.BlockSpec((rt, dk // 2), lambda h, p, n: (blk(p, n), 0))
    cols_spec = pl.BlockSpec((1, 8, CHUNK, 1), lambda h, p, n: (h, 0, 0, 0))
    mats_spec = pl.BlockSpec((1, 3, CHUNK, CHUNK), lambda h, p, n: (h, 0, 0, 0))
    cdec_spec = pl.BlockSpec((1, 2, dk), lambda h, p, n: (h, 0, 0))
    in_row = pl.BlockSpec((rt, dk), lambda h, p, n: (blk(p, n), h))
    out_row = pl.BlockSpec((rt, dk), lambda h, p, n: (p * n, h))
    return nr, blk, q_spec, k_spec, v_spec, g_spec, tab_spec, cols_spec, mats_spec, cdec_spec, in_row, out_row


def ret_fwd(proj, cols, mats, cdec, cos, sin):
    _, t, w2 = proj.shape
    d = w2 // 2
    dk = d // RET_HEADS
    rt = _row_tile(t)
    cpt = rt // CHUNK
    nr, blk, q_spec, k_spec, v_spec, g_spec, tab_spec, cols_spec, mats_spec, cdec_spec, _, out_row = _ret_specs(t, d, dk, rt)
    scale = dk ** -0.5

    def body(q_ref, k_ref, v_ref, g_ref, cos_ref, sin_ref, cols_ref, mats_ref, cdec_ref, r_ref, rn_ref, sb_scr, st):
        p = pl.program_id(1)
        n = pl.program_id(2)
        af, ab, kf, kb = cols_ref[0, 0], cols_ref[0, 1], cols_ref[0, 2], cols_ref[0, 3]
        cf = cdec_ref[0, 0:1, :]
        cb = cdec_ref[0, 1:2, :]

        @pl.when(n == 0)
        def _():
            st[...] = jnp.zeros_like(st)

        @pl.when(p == 0)
        def _():
            for j in reversed(range(cpt)):
                rows = slice(j * CHUNK, (j + 1) * CHUNK)
                ch = blk(p, n) * cpt + j
                kk = _rot(k_ref[0, rows, :].astype(f32), cos_ref[rows, :], sin_ref[rows, :]) * scale
                sb_scr[ch] = st[...].astype(bf16)
                st[...] = st[...] * cb + _dot_tn((kk * kb).astype(bf16), v_ref[0, rows, :])

        @pl.when(p == 1)
        def _():
            for j in range(cpt):
                rows = slice(j * CHUNK, (j + 1) * CHUNK)
                ch = blk(p, n) * cpt + j
                cs, sn = cos_ref[rows, :], sin_ref[rows, :]
                q = _rot(q_ref[0, rows, :].astype(f32), cs, sn)
                kk = _rot(k_ref[0, rows, :].astype(f32), cs, sn) * scale
                v = v_ref[0, rows, :]
                pm = (_dot_nt(q.astype(bf16), kk.astype(bf16)) * mats_ref[0, 0]).astype(bf16)
                out = (_dot(pm, v) + _dot((q * af).astype(bf16), st[...].astype(bf16))
                       + _dot((q * ab).astype(bf16), sb_scr[ch]))
                st[...] = st[...] * cf + _dot_tn((kk * kf).astype(bf16), v)
                rhat = out * lax.rsqrt(jnp.mean(out * out, axis=-1, keepdims=True) + NORM_EPS)
                gg = g_ref[0, rows, :].astype(f32)
                r_ref[rows, :] = out.astype(bf16)
                rn_ref[rows, :] = (rhat * gg * _sigmoid(gg)).astype(bf16)

    return pl.pallas_call(
        body, name="ret_fwd", grid=(RET_HEADS, 2, nr),
        in_specs=[q_spec, k_spec, v_spec, g_spec, tab_spec, tab_spec, cols_spec, mats_spec, cdec_spec],
        out_specs=[out_row, out_row],
        out_shape=[jax.ShapeDtypeStruct((t, d), bf16), jax.ShapeDtypeStruct((t, d), bf16)],
        scratch_shapes=[pltpu.VMEM((t // CHUNK, dk, dk), bf16), pltpu.VMEM((dk, dk), f32)],
        compiler_params=_cparams(),
    )(proj, proj, proj, proj, cos, sin, cols, mats, cdec)


def ret_bwd(drn, r, proj, cols, mats, cdec, cos, sin):
    _, t, w2 = proj.shape
    d = w2 // 2
    dk = d // RET_HEADS
    rt = _row_tile(t)
    cpt = rt // CHUNK
    nr, blk, q_spec, k_spec, v_spec, g_spec, tab_spec, cols_spec, mats_spec, cdec_spec, in_row, out_row = _ret_specs(t, d, dk, rt)
    scale = dk ** -0.5

    def body(drn_ref, r_ref, q_ref, k_ref, v_ref, g_ref, cos_ref, sin_ref, cols_ref, mats_ref, cdec_ref,
             dq_ref, dk_ref, dv_ref, dg_ref, dlg_ref,
             sb_scr, gf_scr, st_s, st_g, acc_af, acc_ab, acc_vf, acc_vb, acc_sf, acc_sb):
        p = pl.program_id(1)
        n = pl.program_id(2)
        af, ab, kf, kb = cols_ref[0, 0], cols_ref[0, 1], cols_ref[0, 2], cols_ref[0, 3]
        af1, ab1, kf1, kb1 = cols_ref[0, 4], cols_ref[0, 5], cols_ref[0, 6], cols_ref[0, 7]
        cf = cdec_ref[0, 0:1, :]
        cb = cdec_ref[0, 1:2, :]

        @pl.when(n == 0)
        def _():
            st_s[...] = jnp.zeros_like(st_s)
            st_g[...] = jnp.zeros_like(st_g)

        @pl.when(jnp.logical_and(n == 0, p == 1))
        def _():
            for a in (acc_af, acc_ab, acc_vf, acc_vb, acc_sf, acc_sb):
                a[...] = jnp.zeros_like(a)

        def load(rows):
            cs, sn = cos_ref[rows, :], sin_ref[rows, :]
            q = _rot(q_ref[0, rows, :].astype(f32), cs, sn)
            kk = _rot(k_ref[0, rows, :].astype(f32), cs, sn) * scale
            rr = r_ref[rows, :].astype(f32)
            rstd = lax.rsqrt(jnp.mean(rr * rr, axis=-1, keepdims=True) + NORM_EPS)
            rhat = rr * rstd
            gg = g_ref[0, rows, :].astype(f32)
            sg = _sigmoid(gg)
            dd = drn_ref[rows, :].astype(f32)
            drhat = dd * gg * sg
            dout = rstd * (drhat - rhat * jnp.mean(drhat * rhat, axis=-1, keepdims=True))
            dgr = dd * rhat * _dsilu(gg, sg)
            return q, kk, dout.astype(bf16), dgr, cs, sn

        @pl.when(p == 0)
        def _():
            for j in reversed(range(cpt)):
                rows = slice(j * CHUNK, (j + 1) * CHUNK)
                ch = blk(p, n) * cpt + j
                q, kk, doutb, _, _, _ = load(rows)
                sb_scr[ch] = st_s[...].astype(bf16)
                gf_scr[ch] = st_g[...].astype(bf16)
                st_s[...] = st_s[...] * cb + _dot_tn((kk * kb).astype(bf16), v_ref[0, rows, :])
                st_g[...] = st_g[...] * cf + _dot_tn((q * af).astype(bf16), doutb)

        @pl.when(p == 1)
        def _():
            for j in range(cpt):
                rows = slice(j * CHUNK, (j + 1) * CHUNK)
                ch = blk(p, n) * cpt + j
                q, kk, doutb, dgr, cs, sn = load(rows)
                v = v_ref[0, rows, :]
                qb = q.astype(bf16)
                kkb = kk.astype(bf16)
                sf = st_s[...]
                gb = st_g[...]
                sfb = sf.astype(bf16)
                gbb = gb.astype(bf16)
                sbb = sb_scr[ch]
                gfb = gf_scr[ch]
                dmat = mats_ref[0, 0]
                scores = _dot_nt(qb, kkb)
                dpraw = _dot_nt(doutb, v)
                dpb = (dpraw * dmat).astype(bf16)
                pmb = (scores * dmat).astype(bf16)
                x1 = _dot_nt(doutb, sfb)
                x2 = _dot_nt(doutb, sbb)
                y1 = _dot_nt(v, gfb)
                y2 = _dot_nt(v, gbb)
                kdf = (kk * kf).astype(bf16)
                kdb = (kk * kb).astype(bf16)
                dq = _dot(dpb, kkb) + x1 * af + x2 * ab
                dkk = _dot_tn(dpb, qb) + y1 * kf + y2 * kb
                dv = _dot_tn(pmb, doutb) + _dot(kdf, gfb) + _dot(kdb, gbb)
                ps = dpraw * scores
                acc_af[...] += ps * mats_ref[0, 1]
                acc_ab[...] += ps * mats_ref[0, 2]
                acc_vf[...] += x1 * q * af1 + y1 * kk * kf1
                acc_vb[...] += x2 * q * ab1 + y2 * kk * kb1
                acc_sf[...] += gfb.astype(f32) * sf
                acc_sb[...] += gb * sbb.astype(f32)
                st_s[...] = sf * cf + _dot_tn(kdf, v)
                st_g[...] = gb * cb + _dot_tn((q * ab).astype(bf16), doutb)
                dq_ref[rows, :] = _rot_inv(dq, cs, sn).astype(bf16)
                dk_ref[rows, :] = (_rot_inv(dkk, cs, sn) * scale).astype(bf16)
                dv_ref[rows, :] = dv.astype(bf16)
                dg_ref[rows, :] = dgr.astype(bf16)

        @pl.when(jnp.logical_and(p == 1, n == nr - 1))
        def _():
            tf = jnp.sum(acc_af[...]) + jnp.sum(acc_vf[...]) + CHUNK * jnp.sum(acc_sf[...] * cf)
            tb = jnp.sum(acc_ab[...]) + jnp.sum(acc_vb[...]) + CHUNK * jnp.sum(acc_sb[...] * cb)
            rid = lax.broadcasted_iota(jnp.int32, (8, 128), 0)
            dlg_ref[0] = jnp.where(rid == 0, tf, jnp.where(rid == 1, tb, 0.0))

    nch = t // CHUNK
    return pl.pallas_call(
        body, name="ret_bwd", grid=(RET_HEADS, 2, nr),
        in_specs=[in_row, in_row, q_spec, k_spec, v_spec, g_spec, tab_spec, tab_spec, cols_spec, mats_spec, cdec_spec],
        out_specs=[out_row, out_row, out_row, out_row, pl.BlockSpec((1, 8, 128), lambda h, p, n: (h, 0, 0))],
        out_shape=[jax.ShapeDtypeStruct((t, d), bf16)] * 4 + [jax.ShapeDtypeStruct((RET_HEADS, 8, 128), f32)],
        scratch_shapes=[pltpu.VMEM((nch, dk, dk), bf16), pltpu.VMEM((nch, dk, dk), bf16),
                        pltpu.VMEM((dk, dk), f32), pltpu.VMEM((dk, dk), f32),
                        pltpu.VMEM((CHUNK, CHUNK), f32), pltpu.VMEM((CHUNK, CHUNK), f32),
                        pltpu.VMEM((CHUNK, dk), f32), pltpu.VMEM((CHUNK, dk), f32),
                        pltpu.VMEM((dk, dk), f32), pltpu.VMEM((dk, dk), f32)],
        compiler_params=_cparams(),
    )(drn, r, proj, proj, proj, proj, cos, sin, cols, mats, cdec)


def mix_fwd(a, rn, proj, wa, wb, wo, x1):
    t, d = x1.shape
    tm = _row_tile(t)

    def body(a_ref, rn_ref, p_ref, wa_ref, wb_ref, wo_ref, x_ref, xo_ref, ba_ref, br_ref):
        ba = _dot(a_ref[...], wa_ref[...])
        br = _dot(rn_ref[...], wb_ref[...])
        sa = _sigmoid(p_ref[0, :, 0:d].astype(f32))
        sb = _sigmoid(p_ref[0, :, d:2 * d].astype(f32))
        mix = (sa * ba + sb * br).astype(bf16)
        xo_ref[...] = x_ref[...] + _dot(mix, wo_ref[...])
        ba_ref[...] = ba.astype(bf16)
        br_ref[...] = br.astype(bf16)

    row = pl.BlockSpec((tm, d), lambda i: (i, 0))
    wsp = pl.BlockSpec((d, d), lambda i: (0, 0))
    return pl.pallas_call(
        body, name="mix_fwd", grid=(t // tm,),
        in_specs=[row, row, pl.BlockSpec((1, tm, 2 * d), lambda i: (3, i, 0)), wsp, wsp, wsp, row],
        out_specs=[row, row, row],
        out_shape=[jax.ShapeDtypeStruct((t, d), f32), jax.ShapeDtypeStruct((t, d), bf16), jax.ShapeDtypeStruct((t, d), bf16)],
        compiler_params=_cparams(),
    )(a, rn, proj, wa, wb, wo, x1)


def mix_bwd_act(dx2, ba, br, proj, wa, wb, wo):
    t, d = dx2.shape
    tm = _row_tile(t)

    def body(dx_ref, ba_ref, br_ref, p_ref, wa_ref, wb_ref, wo_ref,
             da_ref, drn_ref, dga_ref, dgb_ref, mix_ref, dba_ref, dbr_ref, dxb_ref):
        dxb = dx_ref[...].astype(bf16)
        dxb_ref[...] = dxb
        dmix = _dot_nt(dxb, wo_ref[...])
        ba = ba_ref[...].astype(f32)
        br = br_ref[...].astype(f32)
        sa = _sigmoid(p_ref[0, :, 0:d].astype(f32))
        sb = _sigmoid(p_ref[0, :, d:2 * d].astype(f32))
        mix_ref[...] = (sa * ba + sb * br).astype(bf16)
        dba = (dmix * sa).astype(bf16)
        dbr = (dmix * sb).astype(bf16)
        dba_ref[...] = dba
        dbr_ref[...] = dbr
        dga_ref[...] = (dmix * ba * sa * (1.0 - sa)).astype(bf16)
        dgb_ref[...] = (dmix * br * sb * (1.0 - sb)).astype(bf16)
        da_ref[...] = _dot_nt(dba, wa_ref[...]).astype(bf16)
        drn_ref[...] = _dot_nt(dbr, wb_ref[...]).astype(bf16)

    row = pl.BlockSpec((tm, d), lambda i: (i, 0))
    wsp = pl.BlockSpec((d, d), lambda i: (0, 0))
    return pl.pallas_call(
        body, name="mix_bwd_act", grid=(t // tm,),
        in_specs=[row, row, row, pl.BlockSpec((1, tm, 2 * d), lambda i: (3, i, 0)), wsp, wsp, wsp],
        out_specs=[row] * 8,
        out_shape=[jax.ShapeDtypeStruct((t, d), bf16)] * 8,
        compiler_params=_cparams(),
    )(dx2, ba, br, proj, wa, wb, wo)


def inproj_bwd_act(segs, win, x1, ng, dx2):
    t, d = x1.shape
    s4 = win.shape[0]
    tm = _row_tile(t) // 2
    nseg = len(segs)

    def body(*refs):
        seg_refs = refs[:nseg]
        w_ref, x_ref, ng_ref, dx2_ref, dx1_ref, db_ref, dng_ref = refs[nseg:]
        i = pl.program_id(0)
        dh = None
        for e, sr in enumerate(seg_refs):
            sb = sr[...]
            part = _dot_nt(sb, w_ref[e // 2, :, (e % 2) * d:(e % 2 + 1) * d])
            dh = part if dh is None else dh + part
            _acc_out(db_ref.at[e], i == 0, jnp.sum(sb.astype(f32), axis=0, keepdims=True))
        _, xh, r = _rms(x_ref[...], ng_ref[...])
        dx1_ref[...] = dx2_ref[...] + _rms_bwd(dh, xh, r, ng_ref[...])
        _acc_out(dng_ref, i == 0, jnp.sum(dh * xh, axis=0, keepdims=True))

    row = pl.BlockSpec((tm, d), lambda i: (i, 0))
    vec = pl.BlockSpec((1, d), lambda i: (0, 0))
    return pl.pallas_call(
        body, name="inproj_bwd_act", grid=(t // tm,),
        in_specs=[row] * nseg + [pl.BlockSpec((s4, d, 2 * d), lambda i: (0, 0, 0), pipeline_mode=pl.Buffered(1)),
                                 row, vec, row],
        out_specs=[row, pl.BlockSpec((nseg, 1, d), lambda i: (0, 0, 0)), vec],
        out_shape=[jax.ShapeDtypeStruct((t, d), f32), jax.ShapeDtypeStruct((nseg, 1, d), f32),
                   jax.ShapeDtypeStruct((1, d), f32)],
        compiler_params=_cparams(),
    )(*segs, win, x1, ng, dx2)


def loss_head(x3, fng, tgt):
    t, d = x3.shape
    tm = _row_tile(t)

    def body(x_ref, g_ref, t_ref, loss_ref, dx_ref, dg_ref):
        i = pl.program_id(0)
        y, xh, r = _rms(x_ref[...], g_ref[...])
        diff = y - t_ref[...]
        part = 0.5 * jnp.sum(jnp.sum(diff * diff, axis=0, keepdims=True), axis=1, keepdims=True) / d
        _acc_out(loss_ref, i == 0, jnp.broadcast_to(part, (1, 128)))
        dy = diff * (1.0 / d)
        dx_ref[...] = _rms_bwd(dy, xh, r, g_ref[...])
        _acc_out(dg_ref, i == 0, jnp.sum(dy * xh, axis=0, keepdims=True))

    row = pl.BlockSpec((tm, d), lambda i: (i, 0))
    vec = pl.BlockSpec((1, d), lambda i: (0, 0))
    return pl.pallas_call(
        body, name="loss_head", grid=(t // tm,),
        in_specs=[row, vec, row],
        out_specs=[pl.BlockSpec((1, 128), lambda i: (0, 0)), row, vec],
        out_shape=[jax.ShapeDtypeStruct((1, 128), f32), jax.ShapeDtypeStruct((t, d), f32), jax.ShapeDtypeStruct((1, d), f32)],
        compiler_params=_cparams(),
    )(x3, fng, tgt)


def _place():
    return lax.axis_index("x"), lax.axis_index("y"), lax.axis_index("c")


def _other_chips(x, y):
    return [(1 - x, y), (x, 1 - y), (1 - x, 1 - y)]


_ANY = pl.BlockSpec(memory_space=pl.ANY)


def gather_weights(shards):
    nw = len(shards)

    def body(*refs):
        ins = refs[:nw]
        outs = refs[nw:2 * nw]
        send_sems, recv_sems, loc_sems = refs[2 * nw:]
        x, y, c = _place()
        k = 2 * x + y
        chips = _other_chips(x, y)
        local = []
        for w in range(nw):
            lc = pltpu.make_async_copy(ins[w], outs[w].at[k], loc_sems.at[w])
            lc.start()
            local.append(lc)
            for j, (px, py) in enumerate(chips):
                pltpu.make_async_remote_copy(
                    src_ref=ins[w], dst_ref=outs[w].at[k], send_sem=send_sems.at[3 * w + j],
                    recv_sem=recv_sems.at[3 * w + j], device_id=(px, py, c), device_id_type=MESH).start()
        for w in range(nw):
            for j, (px, py) in enumerate(chips):
                pltpu.make_async_remote_copy(
                    src_ref=ins[w], dst_ref=outs[w].at[2 * px + py], send_sem=send_sems.at[3 * w + j],
                    recv_sem=recv_sems.at[3 * w + j], device_id=(px, py, c), device_id_type=MESH).wait()
        for lc in local:
            lc.wait()

    return pl.pallas_call(
        body, name="gather_weights",
        in_specs=[_ANY] * nw, out_specs=[_ANY] * nw,
        out_shape=[jax.ShapeDtypeStruct((N_CHIPS,) + s.shape, s.dtype) for s in shards],
        scratch_shapes=[pltpu.SemaphoreType.DMA((3 * nw,)), pltpu.SemaphoreType.DMA((3 * nw,)),
                        pltpu.SemaphoreType.DMA((nw,))],
    )(*shards)


def exchange_grads(grads):
    nw = len(grads)

    def body(*refs):
        ins = refs[:nw]
        outs = refs[nw:2 * nw]
        send_sems, recv_sems, loc_sems = refs[2 * nw:]
        x, y, c = _place()
        k = 2 * x + y
        chips = _other_chips(x, y)
        local = []
        for w in range(nw):
            lc = pltpu.make_async_copy(ins[w].at[k], outs[w].at[3], loc_sems.at[w])
            lc.start()
            local.append(lc)
            for j, (px, py) in enumerate(chips):
                pltpu.make_async_remote_copy(
                    src_ref=ins[w].at[2 * px + py], dst_ref=outs[w].at[j], send_sem=send_sems.at[3 * w + j],
                    recv_sem=recv_sems.at[3 * w + j], device_id=(px, py, c), device_id_type=MESH).start()
        for w in range(nw):
            for j, (px, py) in enumerate(chips):
                pltpu.make_async_remote_copy(
                    src_ref=ins[w].at[2 * px + py], dst_ref=outs[w].at[j], send_sem=send_sems.at[3 * w + j],
                    recv_sem=recv_sems.at[3 * w + j], device_id=(px, py, c), device_id_type=MESH).wait()
        for lc in local:
            lc.wait()

    return pl.pallas_call(
        body, name="exchange_grads",
        in_specs=[_ANY] * nw, out_specs=[_ANY] * nw,
        out_shape=[jax.ShapeDtypeStruct(g.shape, g.dtype) for g in grads],
        scratch_shapes=[pltpu.SemaphoreType.DMA((3 * nw,)), pltpu.SemaphoreType.DMA((3 * nw,)),
                        pltpu.SemaphoreType.DMA((nw,))],
    )(*grads)


def swap_with_sibling(parts):
    nw = len(parts)

    def body(*refs):
        ins = refs[:nw]
        outs = refs[nw:2 * nw]
        send_sems, recv_sems = refs[2 * nw:]
        x, y, c = _place()
        copies = [pltpu.make_async_remote_copy(
            src_ref=ins[w], dst_ref=outs[w], send_sem=send_sems.at[w], recv_sem=recv_sems.at[w],
            device_id=(x, y, 1 - c), device_id_type=MESH) for w in range(nw)]
        for cp in copies:
            cp.start()
        for cp in copies:
            cp.wait()

    return pl.pallas_call(
        body, name="swap_with_sibling",
        in_specs=[_ANY] * nw, out_specs=[_ANY] * nw,
        out_shape=[jax.ShapeDtypeStruct(p.shape, p.dtype) for p in parts],
        scratch_shapes=[pltpu.SemaphoreType.DMA((nw,)), pltpu.SemaphoreType.DMA((nw,))],
    )(*parts)


def gather_small(block):
    r, lanes = block.shape

    def body(b_ref, o_ref, send_sems, recv_sems):
        x, y, c = _place()
        me = 4 * x + 2 * y + c
        o_ref[me] = b_ref[...]
        peers = []
        for m in range(1, N_DEV):
            px = 1 - x if m & 4 else x
            py = 1 - y if m & 2 else y
            pc = 1 - c if m & 1 else c
            peers.append((px, py, pc))
        for m, peer in enumerate(peers):
            pltpu.make_async_remote_copy(
                src_ref=b_ref, dst_ref=o_ref.at[me], send_sem=send_sems.at[m], recv_sem=recv_sems.at[m],
                device_id=peer, device_id_type=MESH).start()
        for m, (px, py, pc) in enumerate(peers):
            pltpu.make_async_remote_copy(
                src_ref=b_ref, dst_ref=o_ref.at[4 * px + 2 * py + pc], send_sem=send_sems.at[m],
                recv_sem=recv_sems.at[m], device_id=(px, py, pc), device_id_type=MESH).wait()

    return pl.pallas_call(
        body, name="gather_small",
        in_specs=[pl.BlockSpec(memory_space=pltpu.VMEM)], out_specs=pl.BlockSpec(memory_space=pltpu.VMEM),
        out_shape=jax.ShapeDtypeStruct((N_DEV, r, lanes), block.dtype),
        scratch_shapes=[pltpu.SemaphoreType.DMA((N_DEV - 1,)), pltpu.SemaphoreType.DMA((N_DEV - 1,))],
    )(block)


def _adamw(w, g, m, v):
    m = ADAM_B1 * m + (1.0 - ADAM_B1) * g
    v = ADAM_B2 * v + (1.0 - ADAM_B2) * (g * g)
    m_hat = m / (1.0 - ADAM_B1 ** ADAM_STEP)
    v_hat = v / (1.0 - ADAM_B2 ** ADAM_STEP)
    delta = -ADAM_LR * (m_hat / (jnp.sqrt(v_hat) + ADAM_EPS) + ADAM_WD * w)
    return delta, m, v


def _ew_tile(rows):
    for cand in (256, 176, 128, 64, 32, 16, 8):
        if rows % cand == 0:
            return cand
    return rows


def sum_partials(parts, name):
    _, r, c = parts.shape
    tr = _ew_tile(r)

    def body(p_ref, o_ref):
        o_ref[...] = ((p_ref[0].astype(f32) + p_ref[1].astype(f32)) + p_ref[2].astype(f32)) + p_ref[3].astype(f32)

    return pl.pallas_call(
        body, name=name, grid=(r // tr,),
        in_specs=[pl.BlockSpec((4, tr, c), lambda i: (0, i, 0))],
        out_specs=pl.BlockSpec((tr, c), lambda i: (i, 0)),
        out_shape=jax.ShapeDtypeStruct((r, c), f32),
        compiler_params=_cparams(),
    )(parts)


def adamw_shard(p_mine, p_sibling, w, m, v, name):
    r, c = w.shape
    tr = _ew_tile(r)

    def body(a_ref, b_ref, w_ref, m_ref, v_ref, g_ref, d_ref, mo_ref, vo_ref):
        g = a_ref[...] + b_ref[...]
        delta, mn, vn = _adamw(w_ref[...], g, m_ref[...], v_ref[...])
        g_ref[...] = g
        d_ref[...] = delta
        mo_ref[...] = mn
        vo_ref[...] = vn

    blk = pl.BlockSpec((tr, c), lambda i: (i, 0))
    return pl.pallas_call(
        body, name=name, grid=(r // tr,),
        in_specs=[blk] * 5, out_specs=[blk] * 4,
        out_shape=[jax.ShapeDtypeStruct((r, c), f32)] * 4,
        compiler_params=_cparams(),
    )(p_mine, p_sibling, w, m, v)


def adamw_small(g8, w, m, v):
    _, r, lanes = g8.shape

    def body(g_ref, w_ref, m_ref, v_ref, go_ref, d_ref, mo_ref, vo_ref):
        g = g_ref[0]
        for i in range(1, N_DEV):
            g = g + g_ref[i]
        delta, mn, vn = _adamw(w_ref[...], g, m_ref[...], v_ref[...])
        go_ref[...] = g
        d_ref[...] = delta
        mo_ref[...] = mn
        vo_ref[...] = vn

    return pl.pallas_call(
        body, name="adamw_small",
        out_shape=[jax.ShapeDtypeStruct((r, lanes), f32)] * 4,
        compiler_params=_cparams(),
    )(g8, w, m, v)


def _pack(arrs, rows):
    flat = jnp.concatenate([a.reshape(-1).astype(f32) for a in arrs])
    return jnp.pad(flat, (0, rows * 128 - flat.shape[0])).reshape(rows, 128)


def _unpack(block, shapes):
    flat = block.reshape(-1)
    out, off = [], 0
    for s in shapes:
        n = 1
        for e in s:
            n *= e
        out.append(flat[off:off + n].reshape(s))
        off += n
    return out


BIG = ("ffn1_w_gate", "ffn1_w_up", "ffn1_w_down", "w_in", "w_branch_a", "w_branch_b", "w_out",
       "ffn2_w_gate", "ffn2_w_up", "ffn2_w_down")
SMALL = ("ffn1_norm", "mix_norm", "b_in", "sgu_norm_g", "sgu_norm_b", "sgu_w_s", "sgu_b_s", "ret_decay_logit",
         "ffn2_norm", "final_norm")
WEIGHTS = ("ffn1_norm", "ffn1_w_gate", "ffn1_w_up", "ffn1_w_down", "mix_norm", "w_in", "b_in", "sgu_norm_g",
           "sgu_norm_b", "sgu_w_s", "sgu_b_s", "ret_decay_logit", "w_branch_a", "w_branch_b", "w_out", "ffn2_norm",
           "ffn2_w_gate", "ffn2_w_up", "ffn2_w_down", "final_norm")


def kernel(x, ffn1_norm, ffn1_w_gate, ffn1_w_up, ffn1_w_down, mix_norm, w_in, b_in, sgu_norm_g, sgu_norm_b, sgu_w_s, sgu_b_s, ret_decay_logit, w_branch_a, w_branch_b, w_out, ffn2_norm, ffn2_w_gate, ffn2_w_up, ffn2_w_down, final_norm, loss_target, m_ffn1_norm, m_ffn1_w_gate, m_ffn1_w_up, m_ffn1_w_down, m_mix_norm, m_w_in, m_b_in, m_sgu_norm_g, m_sgu_norm_b, m_sgu_w_s, m_sgu_b_s, m_ret_decay_logit, m_w_branch_a, m_w_branch_b, m_w_out, m_ffn2_norm, m_ffn2_w_gate, m_ffn2_w_up, m_ffn2_w_down, m_final_norm, v_ffn1_norm, v_ffn1_w_gate, v_ffn1_w_up, v_ffn1_w_down, v_mix_norm, v_w_in, v_b_in, v_sgu_norm_g, v_sgu_norm_b, v_sgu_w_s, v_sgu_b_s, v_ret_decay_logit, v_w_branch_a, v_w_branch_b, v_w_out, v_ffn2_norm, v_ffn2_w_gate, v_ffn2_w_up, v_ffn2_w_down, v_final_norm):
    p = dict(ffn1_norm=ffn1_norm, ffn1_w_gate=ffn1_w_gate, ffn1_w_up=ffn1_w_up, ffn1_w_down=ffn1_w_down,
             mix_norm=mix_norm, w_in=w_in, b_in=b_in, sgu_norm_g=sgu_norm_g, sgu_norm_b=sgu_norm_b, sgu_w_s=sgu_w_s,
             sgu_b_s=sgu_b_s, ret_decay_logit=ret_decay_logit, w_branch_a=w_branch_a, w_branch_b=w_branch_b,
             w_out=w_out, ffn2_norm=ffn2_norm, ffn2_w_gate=ffn2_w_gate, ffn2_w_up=ffn2_w_up, ffn2_w_down=ffn2_w_down,
             final_norm=final_norm)
    mom = dict(ffn1_norm=m_ffn1_norm, ffn1_w_gate=m_ffn1_w_gate, ffn1_w_up=m_ffn1_w_up, ffn1_w_down=m_ffn1_w_down,
               mix_norm=m_mix_norm, w_in=m_w_in, b_in=m_b_in, sgu_norm_g=m_sgu_norm_g, sgu_norm_b=m_sgu_norm_b,
               sgu_w_s=m_sgu_w_s, sgu_b_s=m_sgu_b_s, ret_decay_logit=m_ret_decay_logit, w_branch_a=m_w_branch_a,
               w_branch_b=m_w_branch_b, w_out=m_w_out, ffn2_norm=m_ffn2_norm, ffn2_w_gate=m_ffn2_w_gate,
               ffn2_w_up=m_ffn2_w_up, ffn2_w_down=m_ffn2_w_down, final_norm=m_final_norm)
    var = dict(ffn1_norm=v_ffn1_norm, ffn1_w_gate=v_ffn1_w_gate, ffn1_w_up=v_ffn1_w_up, ffn1_w_down=v_ffn1_w_down,
               mix_norm=v_mix_norm, w_in=v_w_in, b_in=v_b_in, sgu_norm_g=v_sgu_norm_g, sgu_norm_b=v_sgu_norm_b,
               sgu_w_s=v_sgu_w_s, sgu_b_s=v_sgu_b_s, ret_decay_logit=v_ret_decay_logit, w_branch_a=v_w_branch_a,
               w_branch_b=v_w_branch_b, w_out=v_w_out, ffn2_norm=v_ffn2_norm, ffn2_w_gate=v_ffn2_w_gate,
               ffn2_w_up=v_ffn2_w_up, ffn2_w_down=v_ffn2_w_down, final_norm=v_final_norm)

    xs = x[0]
    tgt = loss_target[0]
    t, d = xs.shape
    dk = d // RET_HEADS
    tm = _row_tile(t)

    shards2d = {n: p[n][0] for n in BIG}
    full = dict(zip(BIG, gather_weights([shards2d[n].astype(bf16) for n in BIG])))
    wa = full["w_branch_a"].reshape(d, d)
    wb = full["w_branch_b"].reshape(d, d)
    wo = full["w_out"].reshape(d, d)
    win = full["w_in"]
    bin4 = b_in.reshape(N_CHIPS, 1, 2 * d)
    ws_b = sgu_w_s[0].astype(bf16)
    bs_c = sgu_b_s[0][:, :, None]
    cols, mats, cdec, cos, sin = retention_constants(ret_decay_logit[0], t, dk)

    x1, g1, u1 = ffn_fwd(xs, ffn1_norm, full["ffn1_w_gate"], full["ffn1_w_up"], full["ffn1_w_down"], "ffn1_fwd")
    proj, hb2 = inproj_fwd(x1, mix_norm, win, bin4)
    a = sgu_fwd(proj, sgu_norm_g, sgu_norm_b, ws_b, bs_c)
    r, rn = ret_fwd(proj, cols, mats, cdec, cos, sin)
    x2, ba, br = mix_fwd(a, rn, proj, wa, wb, wo, x1)
    x3, g2, u2 = ffn_fwd(x2, ffn2_norm, full["ffn2_w_gate"], full["ffn2_w_up"], full["ffn2_w_down"], "ffn2_fwd")
    loss_blk, dx3, d_final = loss_head(x3, final_norm.reshape(1, d), tgt)

    grads = {}
    dx2, dg2, du2, act2, hb3, dyb2, d_ffn2n = ffn_bwd_act(
        dx3, x2, ffn2_norm, g2, u2, full["ffn2_w_gate"], full["ffn2_w_up"], full["ffn2_w_down"], "ffn2_bwd_act")
    grads["ffn2_w_gate"], grads["ffn2_w_up"], grads["ffn2_w_down"] = ffn_weight_grads(hb3, dyb2, dg2, du2, act2, "ffn2_grad")
    da, drn, dga, dgb, mixb, dba, dbr, dx2b = mix_bwd_act(dx2, ba, br, proj, wa, wb, wo)
    row = pl.BlockSpec((tm, d), lambda s, i: (i, 0))
    colblk = pl.BlockSpec((tm, d // N_CHIPS), lambda s, i: (i, s))
    grads["w_branch_a"] = tn_matmul(a, [dba], colblk, [row], N_CHIPS, d // N_CHIPS, [d], t, tm, "grad_w_branch_a")
    grads["w_branch_b"] = tn_matmul(rn, [dbr], colblk, [row], N_CHIPS, d // N_CHIPS, [d], t, tm, "grad_w_branch_b")
    grads["w_out"] = tn_matmul(mixb, [dx2b], colblk, [row], N_CHIPS, d // N_CHIPS, [d], t, tm, "grad_w_out")
    dua, dva, d_ws, d_bs, d_sng, d_snb = sgu_bwd(da, proj, sgu_norm_g, sgu_norm_b, ws_b, bs_c)
    dq, dkr, dv, dgr, dlg = ret_bwd(drn, r, proj, cols, mats, cdec, cos, sin)
    segs = [dua, dva, dq, dkr, dv, dgr, dga, dgb]
    dx1, d_bin, d_mixn = inproj_bwd_act(segs, win, x1, mix_norm, dx2)
    row1 = pl.BlockSpec((tm, d), lambda s, i: (i, 0))
    grads["w_in"] = jnp.concatenate(
        [tn_matmul(hb2, [segs[2 * s], segs[2 * s + 1]], row1, [row1, row1], 1, d, [d, d], t, tm, "grad_w_in_%d" % s)
         for s in range(N_CHIPS)], axis=0)
    grad_x, dg1, du1, act1, hb1, dyb1, d_ffn1n = ffn_bwd_act(
        dx1, xs, ffn1_norm, g1, u1, full["ffn1_w_gate"], full["ffn1_w_up"], full["ffn1_w_down"], "ffn1_bwd_act")
    grads["ffn1_w_gate"], grads["ffn1_w_up"], grads["ffn1_w_down"] = ffn_weight_grads(hb1, dyb1, dg1, du1, act1, "ffn1_grad")

    received = exchange_grads([grads[n] for n in BIG])
    plane = [sum_partials(rcv, "sum_" + n) for n, rcv in zip(BIG, received)]
    other = swap_with_sibling(plane)
    out_g, out_d, out_m, out_v = {}, {}, {}, {}
    for n, mine, sib in zip(BIG, plane, other):
        g, dl, mn, vn = adamw_shard(mine, sib, shards2d[n], mom[n][0], var[n][0], "adamw_" + n)
        out_g[n], out_d[n], out_m[n], out_v[n] = g[None], dl[None], mn[None], vn[None]

    dlogit = dlg[:, 0:2, 0].T * jax.nn.sigmoid(-ret_decay_logit[0].astype(f32))
    small_g = dict(ffn1_norm=d_ffn1n, mix_norm=d_mixn, b_in=d_bin, sgu_norm_g=d_sng, sgu_norm_b=d_snb, sgu_w_s=d_ws,
                   sgu_b_s=d_bs, ret_decay_logit=dlogit, ffn2_norm=d_ffn2n, final_norm=d_final)
    shapes = [p[n].shape for n in SMALL]
    total = sum(int(jnp.size(p[n])) for n in SMALL)
    rows = -(-total // (8 * 128)) * 8
    g8 = gather_small(_pack([small_g[n] for n in SMALL], rows))
    sg, sd, sm, sv = adamw_small(g8, _pack([p[n] for n in SMALL], rows), _pack([mom[n] for n in SMALL], rows),
                                 _pack([var[n] for n in SMALL], rows))
    for res, blockv in ((out_g, sg), (out_d, sd), (out_m, sm), (out_v, sv)):
        for n, val in zip(SMALL, _unpack(blockv, shapes)):
            res[n] = val

    loss = lax.psum(loss_blk[0, 0], ("x", "y", "c"))
    return (loss, grad_x[None], *[out_g[n] for n in WEIGHTS], *[out_d[n] for n in WEIGHTS],
            *[out_m[n] for n in WEIGHTS], *[out_v[n] for n in WEIGHTS])
```

```python
import functools

import jax
import jax.numpy as jnp
from jax import lax
from jax.experimental import pallas as pl
from jax.experimental.pallas import tpu as pltpu

f32 = jnp.float32
bf16 = jnp.bfloat16

CHUNK = 128
RET_HEADS = 4
SGU_GROUPS = 4
ROPE_BASE = 10000.0
NORM_EPS = 1e-6
ADAM_LR = 0.001
ADAM_B1 = 0.9
ADAM_B2 = 0.999
ADAM_EPS = 1e-08
ADAM_WD = 0.01
ADAM_STEP = 10
N_CHIPS = 4
N_DEV = 8
MESH = pl.DeviceIdType.MESH
VMEM_LIMIT = 52 * 1024 * 1024

_NT = (((1,), (1,)), ((), ()))
_TN = (((0,), (0,)), ((), ()))


def _cparams():
    return pltpu.CompilerParams(vmem_limit_bytes=VMEM_LIMIT)


def _row_tile(t):
    return 512 if t >= 2048 else t // 2


def _dot(a, b):
    return jnp.dot(a, b, preferred_element_type=f32)


def _dot_nt(a, b):
    return lax.dot_general(a, b, _NT, preferred_element_type=f32)


def _dot_tn(a, b):
    return lax.dot_general(a, b, _TN, preferred_element_type=f32)


def _rms(x, g):
    r = lax.rsqrt(jnp.mean(x * x, axis=-1, keepdims=True) + NORM_EPS)
    xh = x * r
    return xh * g, xh, r


def _rms_bwd(dy, xh, r, g):
    dxh = dy * g
    return r * (dxh - xh * jnp.mean(dxh * xh, axis=-1, keepdims=True))


def _sigmoid(x):
    return jax.nn.sigmoid(x)


def _dsilu(g, sg):
    return sg * (1.0 + g * (1.0 - sg))


def _gelu(x):
    return 0.5 * x * (1.0 + lax.erf(x * 0.7071067811865476))


def _dgelu(x):
    return 0.5 * (1.0 + lax.erf(x * 0.7071067811865476)) + x * jnp.exp(-0.5 * x * x) * 0.3989422804014327


def _acc_out(ref, first, val):
    @pl.when(first)
    def _():
        ref[...] = val

    @pl.when(jnp.logical_not(first))
    def _():
        ref[...] += val


def ffn_fwd(x, ng, wg, wu, wd, name):
    t, d = x.shape
    s4, _, fs = wg.shape
    tm = _row_tile(t)

    def body(x_ref, ng_ref, wg_ref, wu_ref, wd_ref, xo_ref, g_ref, u_ref, h_scr, acc_scr):
        s = pl.program_id(1)

        @pl.when(s == 0)
        def _():
            y, _, _ = _rms(x_ref[...], ng_ref[...])
            h_scr[...] = y.astype(bf16)
            acc_scr[...] = jnp.zeros_like(acc_scr)

        h = h_scr[...]
        g = _dot(h, wg_ref[0])
        u = _dot(h, wu_ref[0])
        g_ref[0] = g.astype(bf16)
        u_ref[0] = u.astype(bf16)
        act = (g * _sigmoid(g) * u).astype(bf16)
        acc_scr[...] += _dot(act, wd_ref[0])

        @pl.when(s == s4 - 1)
        def _():
            xo_ref[...] = x_ref[...] + 0.5 * acc_scr[...]

    return pl.pallas_call(
        body, name=name, grid=(t // tm, s4),
        in_specs=[pl.BlockSpec((tm, d), lambda i, s: (i, 0)), pl.BlockSpec((1, d), lambda i, s: (0, 0)),
                  pl.BlockSpec((1, d, fs), lambda i, s: (s, 0, 0)), pl.BlockSpec((1, d, fs), lambda i, s: (s, 0, 0)),
                  pl.BlockSpec((1, fs, d), lambda i, s: (s, 0, 0))],
        out_specs=[pl.BlockSpec((tm, d), lambda i, s: (i, 0)), pl.BlockSpec((1, tm, fs), lambda i, s: (s, i, 0)),
                   pl.BlockSpec((1, tm, fs), lambda i, s: (s, i, 0))],
        out_shape=[jax.ShapeDtypeStruct((t, d), f32), jax.ShapeDtypeStruct((s4, t, fs), bf16),
                   jax.ShapeDtypeStruct((s4, t, fs), bf16)],
        scratch_shapes=[pltpu.VMEM((tm, d), bf16), pltpu.VMEM((tm, d), f32)],
        compiler_params=_cparams(),
    )(x, ng, wg, wu, wd)


def ffn_bwd_act(dxo, x, ng, g, u, wg, wu, wd, name, dep):
    t, d = x.shape
    s4, _, fs = wg.shape
    tm = _row_tile(t)

    def body(dxo_ref, x_ref, ng_ref, g_ref, u_ref, wg_ref, wu_ref, wd_ref, dep_ref,
             dx_ref, dg_ref, du_ref, act_ref, hb_ref, dyb_ref, dng_ref, dy_scr, acc_scr):
        i = pl.program_id(0)
        s = pl.program_id(1)

        @pl.when(s == 0)
        def _():
            dyb = (0.5 * dxo_ref[...]).astype(bf16)
            dy_scr[...] = dyb
            dyb_ref[...] = dyb
            acc_scr[...] = jnp.zeros_like(acc_scr)

        dact = _dot_nt(dy_scr[...], wd_ref[0])
        gg = g_ref[0].astype(f32)
        uu = u_ref[0].astype(f32)
        sg = _sigmoid(gg)
        sil = gg * sg
        dgb = (dact * uu * _dsilu(gg, sg)).astype(bf16)
        dub = (dact * sil).astype(bf16)
        dg_ref[0] = dgb
        du_ref[0] = dub
        act_ref[0] = (sil * uu).astype(bf16)
        acc_scr[...] += _dot_nt(dgb, wg_ref[0]) + _dot_nt(dub, wu_ref[0])

        @pl.when(s == s4 - 1)
        def _():
            y, xh, r = _rms(x_ref[...], ng_ref[...])
            hb_ref[...] = y.astype(bf16)
            dh = acc_scr[...]
            dx_ref[...] = dxo_ref[...] + _rms_bwd(dh, xh, r, ng_ref[...])
            _acc_out(dng_ref, i == 0, jnp.sum(dh * xh, axis=0, keepdims=True))

    row = lambda i, s: (i, 0)
    shard = lambda i, s: (s, i, 0)
    wsp = lambda i, s: (s, 0, 0)
    return pl.pallas_call(
        body, name=name, grid=(t // tm, s4),
        in_specs=[pl.BlockSpec((tm, d), row), pl.BlockSpec((tm, d), row), pl.BlockSpec((1, d), lambda i, s: (0, 0)),
                  pl.BlockSpec((1, tm, fs), shard), pl.BlockSpec((1, tm, fs), shard),
                  pl.BlockSpec((1, d, fs), wsp), pl.BlockSpec((1, d, fs), wsp), pl.BlockSpec((1, fs, d), wsp), _ANY],
        out_specs=[pl.BlockSpec((tm, d), row), pl.BlockSpec((1, tm, fs), shard), pl.BlockSpec((1, tm, fs), shard),
                   pl.BlockSpec((1, tm, fs), shard), pl.BlockSpec((tm, d), row), pl.BlockSpec((tm, d), row),
                   pl.BlockSpec((1, d), lambda i, s: (0, 0))],
        out_shape=[jax.ShapeDtypeStruct((t, d), f32), jax.ShapeDtypeStruct((s4, t, fs), bf16),
                   jax.ShapeDtypeStruct((s4, t, fs), bf16), jax.ShapeDtypeStruct((s4, t, fs), bf16),
                   jax.ShapeDtypeStruct((t, d), bf16), jax.ShapeDtypeStruct((t, d), bf16),
                   jax.ShapeDtypeStruct((1, d), f32)],
        scratch_shapes=[pltpu.VMEM((tm, d), bf16), pltpu.VMEM((tm, d), f32)],
        compiler_params=_cparams(),
    )(dxo, x, ng, g, u, wg, wu, wd, dep)


def tn_matmul(xs, ys, x_spec, y_specs, n_shards, k1, k2s, t, tm, name):
    k2 = sum(k2s)
    ny = len(ys)

    def body(*refs):
        x_ref = refs[0]
        y_refs = refs[1:1 + ny]
        o_ref = refs[1 + ny]
        acc = refs[2 + ny]
        i = pl.program_id(1)
        xb = x_ref[0] if len(x_ref.shape) == 3 else x_ref[...]
        off = 0
        for y_ref, w in zip(y_refs, k2s):
            yb = y_ref[0] if len(y_ref.shape) == 3 else y_ref[...]
            part = _dot_tn(xb, yb)
            sl = (slice(None), slice(off, off + w))

            @pl.when(i == 0)
            def _(part=part, sl=sl):
                acc[sl] = part

            @pl.when(i > 0)
            def _(part=part, sl=sl):
                acc[sl] += part

            off += w

        @pl.when(i == t // tm - 1)
        def _():
            o_ref[0] = acc[...].astype(bf16)

    return pl.pallas_call(
        body, name=name, grid=(n_shards, t // tm),
        in_specs=[x_spec] + list(y_specs),
        out_specs=pl.BlockSpec((1, k1, k2), lambda s, i: (s, 0, 0)),
        out_shape=jax.ShapeDtypeStruct((n_shards, k1, k2), bf16),
        scratch_shapes=[pltpu.VMEM((k1, k2), f32)],
        compiler_params=_cparams(),
    )(xs, *ys)


def ffn_weight_grads(hb, dyb, dg, du, act, name):
    t, d = hb.shape
    s4, _, fs = dg.shape
    tm = _row_tile(t)
    row = pl.BlockSpec((tm, d), lambda s, i: (i, 0))
    shard = pl.BlockSpec((1, tm, fs), lambda s, i: (s, i, 0))
    gwg = tn_matmul(hb, [dg], row, [shard], s4, d, [fs], t, tm, name + "_wg")
    gwu = tn_matmul(hb, [du], row, [shard], s4, d, [fs], t, tm, name + "_wu")
    gwd = tn_matmul(act, [dyb], shard, [row], s4, fs, [d], t, tm, name + "_wd")
    return gwg, gwu, gwd


def inproj_fwd(x1, ng, win, bin4):
    t, d = x1.shape
    s4, _, w2 = win.shape
    tm = _row_tile(t)

    def body(x_ref, ng_ref, w_ref, b_ref, p_ref, hb_ref, h_scr):
        s = pl.program_id(1)

        @pl.when(s == 0)
        def _():
            y, _, _ = _rms(x_ref[...], ng_ref[...])
            h_scr[...] = y.astype(bf16)
            hb_ref[...] = y.astype(bf16)

        p_ref[0] = (_dot(h_scr[...], w_ref[0]) + b_ref[0]).astype(bf16)

    return pl.pallas_call(
        body, name="inproj_fwd", grid=(t // tm, s4),
        in_specs=[pl.BlockSpec((tm, d), lambda i, s: (i, 0)), pl.BlockSpec((1, d), lambda i, s: (0, 0)),
                  pl.BlockSpec((1, d, w2), lambda i, s: (s, 0, 0)), pl.BlockSpec((1, 1, w2), lambda i, s: (s, 0, 0))],
        out_specs=[pl.BlockSpec((1, tm, w2), lambda i, s: (s, i, 0)), pl.BlockSpec((tm, d), lambda i, s: (i, 0))],
        out_shape=[jax.ShapeDtypeStruct((s4, t, w2), bf16), jax.ShapeDtypeStruct((t, d), bf16)],
        scratch_shapes=[pltpu.VMEM((tm, d), bf16)],
        compiler_params=_cparams(),
    )(x1, ng, win, bin4)


def _sgu_norm(va, ng, nb):
    gv = _gelu(va)
    mu = jnp.mean(gv, axis=-1, keepdims=True)
    xc = gv - mu
    rstd = lax.rsqrt(jnp.mean(xc * xc, axis=-1, keepdims=True) + NORM_EPS)
    xh = xc * rstd
    return xh, rstd, (xh * ng + nb).astype(bf16)


def sgu_fwd(proj, ng, nb, ws, bs):
    _, t, w2 = proj.shape
    d = w2 // 2
    gd = d // SGU_GROUPS
    tm = _row_tile(t)

    def body(p_ref, ng_ref, nb_ref, ws_ref, bs_ref, a_ref):
        ua = p_ref[0, :, 0:d].astype(f32)
        va = p_ref[0, :, d:w2].astype(f32)
        gu = _gelu(ua)
        _, _, vn = _sgu_norm(va, ng_ref[...], nb_ref[...])
        for c in range(tm // CHUNK):
            rows = slice(c * CHUNK, (c + 1) * CHUNK)
            for g in range(SGU_GROUPS):
                cols = slice(g * gd, (g + 1) * gd)
                sg = _dot(ws_ref[g], vn[rows, cols]) + bs_ref[g]
                a_ref[rows, cols] = (gu[rows, cols] * sg).astype(bf16)

    return pl.pallas_call(
        body, name="sgu_fwd", grid=(t // tm,),
        in_specs=[pl.BlockSpec((1, tm, w2), lambda i: (0, i, 0)), pl.BlockSpec((1, d), lambda i: (0, 0)),
                  pl.BlockSpec((1, d), lambda i: (0, 0)), pl.BlockSpec((SGU_GROUPS, CHUNK, CHUNK), lambda i: (0, 0, 0)),
                  pl.BlockSpec((SGU_GROUPS, CHUNK, 1), lambda i: (0, 0, 0))],
        out_specs=pl.BlockSpec((tm, d), lambda i: (i, 0)),
        out_shape=jax.ShapeDtypeStruct((t, d), bf16),
        compiler_params=_cparams(),
    )(proj, ng, nb, ws, bs)


def sgu_bwd(da, proj, ng, nb, ws, bs, dep):
    _, t, w2 = proj.shape
    d = w2 // 2
    gd = d // SGU_GROUPS
    tm = _row_tile(t)

    def body(da_ref, p_ref, ng_ref, nb_ref, ws_ref, bs_ref, dep_ref,
             dua_ref, dva_ref, dws_ref, dbs_ref, dng_ref, dnb_ref, dvn_scr):
        i = pl.program_id(0)
        ua = p_ref[0, :, 0:d].astype(f32)
        va = p_ref[0, :, d:w2].astype(f32)
        gu = _gelu(ua)
        xh, rstd, vn = _sgu_norm(va, ng_ref[...], nb_ref[...])
        dad = da_ref[...].astype(f32)
        dsb = (dad * gu).astype(bf16)
        for c in range(tm // CHUNK):
            rows = slice(c * CHUNK, (c + 1) * CHUNK)
            for g in range(SGU_GROUPS):
                cols = slice(g * gd, (g + 1) * gd)
                sg = _dot(ws_ref[g], vn[rows, cols]) + bs_ref[g]
                dua_ref[rows, cols] = (dad[rows, cols] * sg * _dgelu(ua[rows, cols])).astype(bf16)
                ds = dsb[rows, cols]
                dvn_scr[rows, cols] = _dot_tn(ws_ref[g], ds)
                dw = _dot_nt(ds, vn[rows, cols])
                db = jnp.sum(ds.astype(f32), axis=1, keepdims=True)
                if c == 0:
                    _acc_out(dws_ref.at[g], i == 0, dw)
                    _acc_out(dbs_ref.at[g], i == 0, db)
                else:
                    dws_ref[g] += dw
                    dbs_ref[g] += db
        dvn = dvn_scr[...]
        _acc_out(dng_ref, i == 0, jnp.sum(dvn * xh, axis=0, keepdims=True))
        _acc_out(dnb_ref, i == 0, jnp.sum(dvn, axis=0, keepdims=True))
        dxh = dvn * ng_ref[...]
        dgv = rstd * (dxh - jnp.mean(dxh, axis=-1, keepdims=True) - xh * jnp.mean(dxh * xh, axis=-1, keepdims=True))
        dva_ref[...] = (dgv * _dgelu(va)).astype(bf16)

    row = pl.BlockSpec((tm, d), lambda i: (i, 0))
    vec = pl.BlockSpec((1, d), lambda i: (0, 0))
    wsp = pl.BlockSpec((SGU_GROUPS, CHUNK, CHUNK), lambda i: (0, 0, 0))
    bsp = pl.BlockSpec((SGU_GROUPS, CHUNK, 1), lambda i: (0, 0, 0))
    return pl.pallas_call(
        body, name="sgu_bwd", grid=(t // tm,),
        in_specs=[row, pl.BlockSpec((1, tm, w2), lambda i: (0, i, 0)), vec, vec, wsp, bsp, _ANY],
        out_specs=[row, row, wsp, bsp, vec, vec],
        out_shape=[jax.ShapeDtypeStruct((t, d), bf16), jax.ShapeDtypeStruct((t, d), bf16),
                   jax.ShapeDtypeStruct((SGU_GROUPS, CHUNK, CHUNK), f32), jax.ShapeDtypeStruct((SGU_GROUPS, CHUNK, 1), f32),
                   jax.ShapeDtypeStruct((1, d), f32), jax.ShapeDtypeStruct((1, d), f32)],
        scratch_shapes=[pltpu.VMEM((tm, d), f32)],
        compiler_params=_cparams(),
    )(da, proj, ng, nb, ws, bs, dep)


def retention_constants(decay_logit, t, dk):
    lg = jax.nn.log_sigmoid(decay_logit.astype(f32))
    lgf = lg[0][:, None]
    lgb = lg[1][:, None]
    idx = jnp.arange(CHUNK, dtype=f32)[None, :]
    af = jnp.exp((idx + 1.0) * lgf)
    ab = jnp.exp((CHUNK - idx) * lgb)
    kf = jnp.exp((CHUNK - 1.0 - idx) * lgf)
    kb = jnp.exp(idx * lgb)
    cols = jnp.stack([af, ab, kf, kb, af * (idx + 1.0), ab * (CHUNK - idx), kf * (CHUNK - 1.0 - idx), kb * idx], axis=1)
    cols = cols[..., None]
    diff = idx[0][:, None] - idx[0][None, :]
    dfm = jnp.where(diff >= 0, jnp.exp(jnp.maximum(diff, 0.0)[None] * lgf[:, :, None]), 0.0)
    dbm = jnp.where(diff < 0, jnp.exp(jnp.maximum(-diff, 0.0)[None] * lgb[:, :, None]), 0.0)
    mats = jnp.stack([dfm + dbm, dfm * diff[None], dbm * (-diff)[None]], axis=1)
    cdec = jnp.stack([jnp.broadcast_to(jnp.exp(CHUNK * lgf), (RET_HEADS, dk)),
                      jnp.broadcast_to(jnp.exp(CHUNK * lgb), (RET_HEADS, dk))], axis=1)
    theta = ROPE_BASE ** (-jnp.arange(0, dk, 2, dtype=f32) / dk)
    ang = jnp.arange(t, dtype=f32)[:, None] * theta[None, :]
    return cols, mats, cdec, jnp.cos(ang), jnp.sin(ang)


def _rot(tr, cos, sin):
    half = tr.shape[-1] // 2
    t1 = tr[:, :half]
    t2 = tr[:, half:]
    return jnp.concatenate([t1 * cos - t2 * sin, t2 * cos + t1 * sin], axis=-1)


def _rot_inv(dt, cos, sin):
    half = dt.shape[-1] // 2
    d1 = dt[:, :half]
    d2 = dt[:, half:]
    return jnp.concatenate([d1 * cos + d2 * sin, d2 * cos - d1 * sin], axis=-1)


def _ret_specs(t, d, dk, rt):
    nr = t // rt
    hq = d // dk

    def blk(p, n):
        return (1 - p) * (nr - 1 - n) + p * n

    q_spec = pl.BlockSpec((1, rt, dk), lambda h, p, n: (1, blk(p, n), h))
    k_spec = pl.BlockSpec((1, rt, dk), lambda h, p, n: (1, blk(p, n), hq + h))
    v_spec = pl.BlockSpec((1, rt, dk), lambda h, p, n: (2, blk(p, n), h))
    g_spec = pl.BlockSpec((1, rt, dk), lambda h, p, n: (2, blk(p, n), hq + h))
    tab_spec = pl.BlockSpec((rt, dk // 2), lambda h, p, n: (blk(p, n), 0))
    cols_spec = pl.BlockSpec((1, 8, CHUNK, 1), lambda h, p, n: (h, 0, 0, 0))
    mats_spec = pl.BlockSpec((1, 3, CHUNK, CHUNK), lambda h, p, n: (h, 0, 0, 0))
    cdec_spec = pl.BlockSpec((1, 2, dk), lambda h, p, n: (h, 0, 0))
    in_row = pl.BlockSpec((rt, dk), lambda h, p, n: (blk(p, n), h))
    out_row = pl.BlockSpec((rt, dk), lambda h, p, n: (p * n, h))
    return nr, blk, q_spec, k_spec, v_spec, g_spec, tab_spec, cols_spec, mats_spec, cdec_spec, in_row, out_row


def ret_fwd(proj, cols, mats, cdec, cos, sin):
    _, t, w2 = proj.shape
    d = w2 // 2
    dk = d // RET_HEADS
    rt = _row_tile(t)
    cpt = rt // CHUNK
    nr, blk, q_spec, k_spec, v_spec, g_spec, tab_spec, cols_spec, mats_spec, cdec_spec, _, out_row = _ret_specs(t, d, dk, rt)
    scale = dk ** -0.5

    def body(q_ref, k_ref, v_ref, g_ref, cos_ref, sin_ref, cols_ref, mats_ref, cdec_ref, r_ref, rn_ref, sb_scr, st):
        p = pl.program_id(1)
        n = pl.program_id(2)
        af, ab, kf, kb = cols_ref[0, 0], cols_ref[0, 1], cols_ref[0, 2], cols_ref[0, 3]
        cf = cdec_ref[0, 0:1, :]
        cb = cdec_ref[0, 1:2, :]

        @pl.when(n == 0)
        def _():
            st[...] = jnp.zeros_like(st)

        @pl.when(p == 0)
        def _():
            for j in reversed(range(cpt)):
                rows = slice(j * CHUNK, (j + 1) * CHUNK)
                ch = blk(p, n) * cpt + j
                kk = _rot(k_ref[0, rows, :].astype(f32), cos_ref[rows, :], sin_ref[rows, :]) * scale
                sb_scr[ch] = st[...].astype(bf16)
                st[...] = st[...] * cb + _dot_tn((kk * kb).astype(bf16), v_ref[0, rows, :])

        @pl.when(p == 1)
        def _():
            for j in range(cpt):
                rows = slice(j * CHUNK, (j + 1) * CHUNK)
                ch = blk(p, n) * cpt + j
                cs, sn = cos_ref[rows, :], sin_ref[rows, :]
                q = _rot(q_ref[0, rows, :].astype(f32), cs, sn)
                kk = _rot(k_ref[0, rows, :].astype(f32), cs, sn) * scale
                v = v_ref[0, rows, :]
                pm = (_dot_nt(q.astype(bf16), kk.astype(bf16)) * mats_ref[0, 0]).astype(bf16)
                out = (_dot(pm, v) + _dot((q * af).astype(bf16), st[...].astype(bf16))
                       + _dot((q * ab).astype(bf16), sb_scr[ch]))
                st[...] = st[...] * cf + _dot_tn((kk * kf).astype(bf16), v)
                rhat = out * lax.rsqrt(jnp.mean(out * out, axis=-1, keepdims=True) + NORM_EPS)
                gg = g_ref[0, rows, :].astype(f32)
                r_ref[rows, :] = out.astype(bf16)
                rn_ref[rows, :] = (rhat * gg * _sigmoid(gg)).astype(bf16)

    return pl.pallas_call(
        body, name="ret_fwd", grid=(RET_HEADS, 2, nr),
        in_specs=[q_spec, k_spec, v_spec, g_spec, tab_spec, tab_spec, cols_spec, mats_spec, cdec_spec],
        out_specs=[out_row, out_row],
        out_shape=[jax.ShapeDtypeStruct((t, d), bf16), jax.ShapeDtypeStruct((t, d), bf16)],
        scratch_shapes=[pltpu.VMEM((t // CHUNK, dk, dk), bf16), pltpu.VMEM((dk, dk), f32)],
        compiler_params=_cparams(),
    )(proj, proj, proj, proj, cos, sin, cols, mats, cdec)


def ret_bwd(drn, r, proj, cols, mats, cdec, cos, sin):
    _, t, w2 = proj.shape
    d = w2 // 2
    dk = d // RET_HEADS
    rt = _row_tile(t)
    cpt = rt // CHUNK
    nr, blk, q_spec, k_spec, v_spec, g_spec, tab_spec, cols_spec, mats_spec, cdec_spec, in_row, out_row = _ret_specs(t, d, dk, rt)
    scale = dk ** -0.5

    def body(drn_ref, r_ref, q_ref, k_ref, v_ref, g_ref, cos_ref, sin_ref, cols_ref, mats_ref, cdec_ref,
             dq_ref, dk_ref, dv_ref, dg_ref, dlg_ref,
             sb_scr, gf_scr, st_s, st_g, acc_af, acc_ab, acc_vf, acc_vb, acc_sf, acc_sb):
        p = pl.program_id(1)
        n = pl.program_id(2)
        af, ab, kf, kb = cols_ref[0, 0], cols_ref[0, 1], cols_ref[0, 2], cols_ref[0, 3]
        af1, ab1, kf1, kb1 = cols_ref[0, 4], cols_ref[0, 5], cols_ref[0, 6], cols_ref[0, 7]
        cf = cdec_ref[0, 0:1, :]
        cb = cdec_ref[0, 1:2, :]

        @pl.when(n == 0)
        def _():
            st_s[...] = jnp.zeros_like(st_s)
            st_g[...] = jnp.zeros_like(st_g)

        @pl.when(jnp.logical_and(n == 0, p == 1))
        def _():
            for a in (acc_af, acc_ab, acc_vf, acc_vb, acc_sf, acc_sb):
                a[...] = jnp.zeros_like(a)

        def load(rows):
            cs, sn = cos_ref[rows, :], sin_ref[rows, :]
            q = _rot(q_ref[0, rows, :].astype(f32), cs, sn)
            kk = _rot(k_ref[0, rows, :].astype(f32), cs, sn) * scale
            rr = r_ref[rows, :].astype(f32)
            rstd = lax.rsqrt(jnp.mean(rr * rr, axis=-1, keepdims=True) + NORM_EPS)
            rhat = rr * rstd
            gg = g_ref[0, rows, :].astype(f32)
            sg = _sigmoid(gg)
            dd = drn_ref[rows, :].astype(f32)
            drhat = dd * gg * sg
            dout = rstd * (drhat - rhat * jnp.mean(drhat * rhat, axis=-1, keepdims=True))
            dgr = dd * rhat * _dsilu(gg, sg)
            return q, kk, dout.astype(bf16), dgr, cs, sn

        @pl.when(p == 0)
        def _():
            for j in reversed(range(cpt)):
                rows = slice(j * CHUNK, (j + 1) * CHUNK)
                ch = blk(p, n) * cpt + j
                q, kk, doutb, _, _, _ = load(rows)
                sb_scr[ch] = st_s[...].astype(bf16)
                gf_scr[ch] = st_g[...].astype(bf16)
                st_s[...] = st_s[...] * cb + _dot_tn((kk * kb).astype(bf16), v_ref[0, rows, :])
                st_g[...] = st_g[...] * cf + _dot_tn((q * af).astype(bf16), doutb)

        @pl.when(p == 1)
        def _():
            for j in range(cpt):
                rows = slice(j * CHUNK, (j + 1) * CHUNK)
                ch = blk(p, n) * cpt + j
                q, kk, doutb, dgr, cs, sn = load(rows)
                v = v_ref[0, rows, :]
                qb = q.astype(bf16)
                kkb = kk.astype(bf16)
                sf = st_s[...]
                gb = st_g[...]
                sfb = sf.astype(bf16)
                gbb = gb.astype(bf16)
                sbb = sb_scr[ch]
                gfb = gf_scr[ch]
                dmat = mats_ref[0, 0]
                scores = _dot_nt(qb, kkb)
                dpraw = _dot_nt(doutb, v)
                dpb = (dpraw * dmat).astype(bf16)
                pmb = (scores * dmat).astype(bf16)
                x1 = _dot_nt(doutb, sfb)
                x2 = _dot_nt(doutb, sbb)
                y1 = _dot_nt(v, gfb)
                y2 = _dot_nt(v, gbb)
                kdf = (kk * kf).astype(bf16)
                kdb = (kk * kb).astype(bf16)
                dq = _dot(dpb, kkb) + x1 * af + x2 * ab
                dkk = _dot_tn(dpb, qb) + y1 * kf + y2 * kb
                dv = _dot_tn(pmb, doutb) + _dot(kdf, gfb) + _dot(kdb, gbb)
                ps = dpraw * scores
                acc_af[...] += ps * mats_ref[0, 1]
                acc_ab[...] += ps * mats_ref[0, 2]
                acc_vf[...] += x1 * q * af1 + y1 * kk * kf1
                acc_vb[...] += x2 * q * ab1 + y2 * kk * kb1
                acc_sf[...] += gfb.astype(f32) * sf
                acc_sb[...] += gb * sbb.astype(f32)
                st_s[...] = sf * cf + _dot_tn(kdf, v)
                st_g[...] = gb * cb + _dot_tn((q * ab).astype(bf16), doutb)
                dq_ref[rows, :] = _rot_inv(dq, cs, sn).astype(bf16)
                dk_ref[rows, :] = (_rot_inv(dkk, cs, sn) * scale).astype(bf16)
                dv_ref[rows, :] = dv.astype(bf16)
                dg_ref[rows, :] = dgr.astype(bf16)

        @pl.when(jnp.logical_and(p == 1, n == nr - 1))
        def _():
            tf = jnp.sum(acc_af[...]) + jnp.sum(acc_vf[...]) + CHUNK * jnp.sum(acc_sf[...] * cf)
            tb = jnp.sum(acc_ab[...]) + jnp.sum(acc_vb[...]) + CHUNK * jnp.sum(acc_sb[...] * cb)
            rid = lax.broadcasted_iota(jnp.int32, (8, 128), 0)
            dlg_ref[0] = jnp.where(rid == 0, tf, jnp.where(rid == 1, tb, 0.0))

    nch = t // CHUNK
    return pl.pallas_call(
        body, name="ret_bwd", grid=(RET_HEADS, 2, nr),
        in_specs=[in_row, in_row, q_spec, k_spec, v_spec, g_spec, tab_spec, tab_spec, cols_spec, mats_spec, cdec_spec],
        out_specs=[out_row, out_row, out_row, out_row, pl.BlockSpec((1, 8, 128), lambda h, p, n: (h, 0, 0))],
        out_shape=[jax.ShapeDtypeStruct((t, d), bf16)] * 4 + [jax.ShapeDtypeStruct((RET_HEADS, 8, 128), f32)],
        scratch_shapes=[pltpu.VMEM((nch, dk, dk), bf16), pltpu.VMEM((nch, dk, dk), bf16),
                        pltpu.VMEM((dk, dk), f32), pltpu.VMEM((dk, dk), f32),
                        pltpu.VMEM((CHUNK, CHUNK), f32), pltpu.VMEM((CHUNK, CHUNK), f32),
                        pltpu.VMEM((CHUNK, dk), f32), pltpu.VMEM((CHUNK, dk), f32),
                        pltpu.VMEM((dk, dk), f32), pltpu.VMEM((dk, dk), f32)],
        compiler_params=_cparams(),
    )(drn, r, proj, proj, proj, proj, cos, sin, cols, mats, cdec)


def mix_fwd(a, rn, proj, wa, wb, wo, x1):
    t, d = x1.shape
    tm = _row_tile(t)

    def body(a_ref, rn_ref, p_ref, wa_ref, wb_ref, wo_ref, x_ref, xo_ref, ba_ref, br_ref):
        ba = _dot(a_ref[...], wa_ref[...])
        br = _dot(rn_ref[...], wb_ref[...])
        sa = _sigmoid(p_ref[0, :, 0:d].astype(f32))
        sb = _sigmoid(p_ref[0, :, d:2 * d].astype(f32))
        mix = (sa * ba + sb * br).astype(bf16)
        xo_ref[...] = x_ref[...] + _dot(mix, wo_ref[...])
        ba_ref[...] = ba.astype(bf16)
        br_ref[...] = br.astype(bf16)

    row = pl.BlockSpec((tm, d), lambda i: (i, 0))
    wsp = pl.BlockSpec((d, d), lambda i: (0, 0))
    return pl.pallas_call(
        body, name="mix_fwd", grid=(t // tm,),
        in_specs=[row, row, pl.BlockSpec((1, tm, 2 * d), lambda i: (3, i, 0)), wsp, wsp, wsp, row],
        out_specs=[row, row, row],
        out_shape=[jax.ShapeDtypeStruct((t, d), f32), jax.ShapeDtypeStruct((t, d), bf16), jax.ShapeDtypeStruct((t, d), bf16)],
        compiler_params=_cparams(),
    )(a, rn, proj, wa, wb, wo, x1)


def mix_bwd_act(dx2, ba, br, proj, wa, wb, wo, dep):
    t, d = dx2.shape
    tm = _row_tile(t)

    def body(dx_ref, ba_ref, br_ref, p_ref, wa_ref, wb_ref, wo_ref, dep_ref,
             da_ref, drn_ref, dga_ref, dgb_ref, mix_ref, dba_ref, dbr_ref, dxb_ref):
        dxb = dx_ref[...].astype(bf16)
        dxb_ref[...] = dxb
        dmix = _dot_nt(dxb, wo_ref[...])
        ba = ba_ref[...].astype(f32)
        br = br_ref[...].astype(f32)
        sa = _sigmoid(p_ref[0, :, 0:d].astype(f32))
        sb = _sigmoid(p_ref[0, :, d:2 * d].astype(f32))
        mix_ref[...] = (sa * ba + sb * br).astype(bf16)
        dba = (dmix * sa).astype(bf16)
        dbr = (dmix * sb).astype(bf16)
        dba_ref[...] = dba
        dbr_ref[...] = dbr
        dga_ref[...] = (dmix * ba * sa * (1.0 - sa)).astype(bf16)
        dgb_ref[...] = (dmix * br * sb * (1.0 - sb)).astype(bf16)
        da_ref[...] = _dot_nt(dba, wa_ref[...]).astype(bf16)
        drn_ref[...] = _dot_nt(dbr, wb_ref[...]).astype(bf16)

    row = pl.BlockSpec((tm, d), lambda i: (i, 0))
    wsp = pl.BlockSpec((d, d), lambda i: (0, 0))
    return pl.pallas_call(
        body, name="mix_bwd_act", grid=(t // tm,),
        in_specs=[row, row, row, pl.BlockSpec((1, tm, 2 * d), lambda i: (3, i, 0)), wsp, wsp, wsp, _ANY],
        out_specs=[row] * 8,
        out_shape=[jax.ShapeDtypeStruct((t, d), bf16)] * 8,
        compiler_params=_cparams(),
    )(dx2, ba, br, proj, wa, wb, wo, dep)


def inproj_bwd_act(segs, win, x1, ng, dx2):
    t, d = x1.shape
    s4 = win.shape[0]
    tm = _row_tile(t) // 2
    nseg = len(segs)

    def body(*refs):
        seg_refs = refs[:nseg]
        w_ref, x_ref, ng_ref, dx2_ref, dx1_ref, db_ref, dng_ref = refs[nseg:]
        i = pl.program_id(0)
        dh = None
        for e, sr in enumerate(seg_refs):
            sb = sr[...]
            part = _dot_nt(sb, w_ref[e // 2, :, (e % 2) * d:(e % 2 + 1) * d])
            dh = part if dh is None else dh + part
            _acc_out(db_ref.at[e], i == 0, jnp.sum(sb.astype(f32), axis=0, keepdims=True))
        _, xh, r = _rms(x_ref[...], ng_ref[...])
        dx1_ref[...] = dx2_ref[...] + _rms_bwd(dh, xh, r, ng_ref[...])
        _acc_out(dng_ref, i == 0, jnp.sum(dh * xh, axis=0, keepdims=True))

    row = pl.BlockSpec((tm, d), lambda i: (i, 0))
    vec = pl.BlockSpec((1, d), lambda i: (0, 0))
    return pl.pallas_call(
        body, name="inproj_bwd_act", grid=(t // tm,),
        in_specs=[row] * nseg + [pl.BlockSpec((s4, d, 2 * d), lambda i: (0, 0, 0), pipeline_mode=pl.Buffered(1)),
                                 row, vec, row],
        out_specs=[row, pl.BlockSpec((nseg, 1, d), lambda i: (0, 0, 0)), vec],
        out_shape=[jax.ShapeDtypeStruct((t, d), f32), jax.ShapeDtypeStruct((nseg, 1, d), f32),
                   jax.ShapeDtypeStruct((1, d), f32)],
        compiler_params=_cparams(),
    )(*segs, win, x1, ng, dx2)


def loss_head(x3, fng, tgt):
    t, d = x3.shape
    tm = _row_tile(t)

    def body(x_ref, g_ref, t_ref, loss_ref, dx_ref, dg_ref):
        i = pl.program_id(0)
        y, xh, r = _rms(x_ref[...], g_ref[...])
        diff = y - t_ref[...]
        part = 0.5 * jnp.sum(jnp.sum(diff * diff, axis=0, keepdims=True), axis=1, keepdims=True) / d
        _acc_out(loss_ref, i == 0, jnp.broadcast_to(part, (1, 128)))
        dy = diff * (1.0 / d)
        dx_ref[...] = _rms_bwd(dy, xh, r, g_ref[...])
        _acc_out(dg_ref, i == 0, jnp.sum(dy * xh, axis=0, keepdims=True))

    row = pl.BlockSpec((tm, d), lambda i: (i, 0))
    vec = pl.BlockSpec((1, d), lambda i: (0, 0))
    return pl.pallas_call(
        body, name="loss_head", grid=(t // tm,),
        in_specs=[row, vec, row],
        out_specs=[pl.BlockSpec((1, 128), lambda i: (0, 0)), row, vec],
        out_shape=[jax.ShapeDtypeStruct((1, 128), f32), jax.ShapeDtypeStruct((t, d), f32), jax.ShapeDtypeStruct((1, d), f32)],
        compiler_params=_cparams(),
    )(x3, fng, tgt)


def _place():
    return lax.axis_index("x"), lax.axis_index("y"), lax.axis_index("c")


def _other_chips(x, y):
    return [(1 - x, y), (x, 1 - y), (1 - x, 1 - y)]


_ANY = pl.BlockSpec(memory_space=pl.ANY)


_HBM = pl.BlockSpec(memory_space=pltpu.HBM)
_SEM = pl.BlockSpec(memory_space=pltpu.SEMAPHORE)
_EFFECT = pltpu.SideEffectType.DATAFLOW_SIDE_EFFECTING


def _hbm(a):
    return pltpu.with_memory_space_constraint(a, pltpu.HBM)


def _chip_copy(src, dst, send_sem, recv_sem, chip, c):
    return pltpu.make_async_remote_copy(src_ref=src, dst_ref=dst, send_sem=send_sem, recv_sem=recv_sem,
                                        device_id=(chip[0], chip[1], c), device_id_type=MESH)


def gather_start(bufs, groups):
    nb, ng = len(bufs), len(groups)

    def body(*refs):
        ins = refs[:nb]
        sems = refs[nb:nb + 2 * ng]
        token = refs[-1]
        x, y, c = _place()
        k = 2 * x + y
        for gi, grp in enumerate(groups):
            for wi, w in enumerate(grp):
                for j, chip in enumerate(_other_chips(x, y)):
                    _chip_copy(ins[w].at[k], ins[w].at[k], sems[2 * gi].at[3 * wi + j], sems[2 * gi + 1].at[3 * wi + j],
                               chip, c).start()
        token[...] = jnp.zeros_like(token)

    sem_shapes = []
    for grp in groups:
        sem_shapes += [pltpu.SemaphoreType.DMA((3 * len(grp),)), pltpu.SemaphoreType.DMA((3 * len(grp),))]
    outs = pl.pallas_call(
        body, name="gather_start",
        out_shape=sem_shapes + [pltpu.HBM(b.shape, b.dtype) for b in bufs] + [jax.ShapeDtypeStruct((8, 128), f32)],
        in_specs=[_HBM] * nb,
        out_specs=[_SEM] * (2 * ng) + [_HBM] * nb + [pl.BlockSpec(memory_space=pltpu.VMEM)],
        input_output_aliases={w: 2 * ng + w for w in range(nb)},
        compiler_params=pltpu.CompilerParams(has_side_effects=_EFFECT),
    )(*[_hbm(b) for b in bufs])
    sems = [(outs[2 * gi], outs[2 * gi + 1]) for gi in range(ng)]
    return sems, list(outs[2 * ng:2 * ng + nb]), outs[-1]


def gather_wait(bufs, sems, after, name):
    n = len(bufs)

    def body(*refs):
        ins = refs[:n]
        send_sems, recv_sems = refs[n], refs[n + 1]
        x, y, c = _place()
        k = 2 * x + y
        for wi in range(n):
            for j, chip in enumerate(_other_chips(x, y)):
                cp = _chip_copy(ins[wi].at[k], ins[wi].at[2 * chip[0] + chip[1]], send_sems.at[3 * wi + j],
                                recv_sems.at[3 * wi + j], chip, c)
                cp.wait_send()
                cp.wait_recv()

    outs = pl.pallas_call(
        body, name=name,
        out_shape=[pltpu.HBM(b.shape, b.dtype) for b in bufs],
        in_specs=[_HBM] * n + [_SEM, _SEM, _ANY],
        out_specs=[_HBM] * n,
        input_output_aliases={i: i for i in range(n)},
        compiler_params=pltpu.CompilerParams(has_side_effects=_EFFECT),
    )(*bufs, sems[0], sems[1], after)
    return list(outs)


def exchange_start(grads, name):
    n = len(grads)
    lands = [lax.empty((3,) + g.shape[1:], g.dtype) for g in grads]

    def body(*refs):
        ins = refs[:n]
        land = refs[n:2 * n]
        send_sems, recv_sems = refs[2 * n], refs[2 * n + 1]
        token = refs[-1]
        x, y, c = _place()
        for wi in range(n):
            for j, chip in enumerate(_other_chips(x, y)):
                _chip_copy(ins[wi].at[2 * chip[0] + chip[1]], land[wi].at[j], send_sems.at[3 * wi + j],
                           recv_sems.at[3 * wi + j], chip, c).start()
        token[...] = jnp.zeros_like(token)

    outs = pl.pallas_call(
        body, name=name,
        out_shape=[pltpu.SemaphoreType.DMA((3 * n,)), pltpu.SemaphoreType.DMA((3 * n,))]
        + [pltpu.HBM(g.shape, g.dtype) for g in grads] + [pltpu.HBM(l.shape, l.dtype) for l in lands]
        + [jax.ShapeDtypeStruct((8, 128), f32)],
        in_specs=[_HBM] * (2 * n),
        out_specs=[_SEM, _SEM] + [_HBM] * (2 * n) + [pl.BlockSpec(memory_space=pltpu.VMEM)],
        input_output_aliases={i: 2 + i for i in range(2 * n)},
        compiler_params=pltpu.CompilerParams(has_side_effects=_EFFECT),
    )(*[_hbm(g) for g in grads], *[_hbm(l) for l in lands])
    return (outs[0], outs[1]), list(outs[2:2 + n]), list(outs[2 + n:2 + 2 * n]), outs[-1]


def exchange_wait(grads, lands, sems, after, name):
    n = len(grads)

    def body(*refs):
        ins = refs[:n]
        land = refs[n:2 * n]
        send_sems, recv_sems = refs[2 * n], refs[2 * n + 1]
        x, y, c = _place()
        for wi in range(n):
            for j, chip in enumerate(_other_chips(x, y)):
                cp = _chip_copy(ins[wi].at[2 * chip[0] + chip[1]], land[wi].at[j], send_sems.at[3 * wi + j],
                                recv_sems.at[3 * wi + j], chip, c)
                cp.wait_send()
                cp.wait_recv()

    outs = pl.pallas_call(
        body, name=name,
        out_shape=[pltpu.HBM(g.shape, g.dtype) for g in grads] + [pltpu.HBM(l.shape, l.dtype) for l in lands],
        in_specs=[_HBM] * (2 * n) + [_SEM, _SEM, _ANY],
        out_specs=[_HBM] * (2 * n),
        input_output_aliases={i: i for i in range(2 * n)},
        compiler_params=pltpu.CompilerParams(has_side_effects=_EFFECT),
    )(*grads, *lands, sems[0], sems[1], after)
    return list(outs[:n]), list(outs[n:])


def swap_with_sibling(parts, name):
    nw = len(parts)

    def body(*refs):
        ins = refs[:nw]
        outs = refs[nw:2 * nw]
        send_sems, recv_sems = refs[2 * nw:]
        x, y, c = _place()
        copies = [pltpu.make_async_remote_copy(
            src_ref=ins[w], dst_ref=outs[w], send_sem=send_sems.at[w], recv_sem=recv_sems.at[w],
            device_id=(x, y, 1 - c), device_id_type=MESH) for w in range(nw)]
        for cp in copies:
            cp.start()
        for cp in copies:
            cp.wait()

    return pl.pallas_call(
        body, name=name,
        in_specs=[_ANY] * nw, out_specs=[_ANY] * nw,
        out_shape=[jax.ShapeDtypeStruct(p.shape, p.dtype) for p in parts],
        scratch_shapes=[pltpu.SemaphoreType.DMA((nw,)), pltpu.SemaphoreType.DMA((nw,))],
    )(*parts)


def gather_small(block):
    r, lanes = block.shape

    def body(b_ref, o_ref, send_sems, recv_sems):
        x, y, c = _place()
        me = 4 * x + 2 * y + c
        o_ref[me] = b_ref[...]
        peers = []
        for m in range(1, N_DEV):
            px = 1 - x if m & 4 else x
            py = 1 - y if m & 2 else y
            pc = 1 - c if m & 1 else c
            peers.append((px, py, pc))
        for m, peer in enumerate(peers):
            pltpu.make_async_remote_copy(
                src_ref=b_ref, dst_ref=o_ref.at[me], send_sem=send_sems.at[m], recv_sem=recv_sems.at[m],
                device_id=peer, device_id_type=MESH).start()
        for m, (px, py, pc) in enumerate(peers):
            pltpu.make_async_remote_copy(
                src_ref=b_ref, dst_ref=o_ref.at[4 * px + 2 * py + pc], send_sem=send_sems.at[m],
                recv_sem=recv_sems.at[m], device_id=(px, py, pc), device_id_type=MESH).wait()

    return pl.pallas_call(
        body, name="gather_small",
        in_specs=[pl.BlockSpec(memory_space=pltpu.VMEM)], out_specs=pl.BlockSpec(memory_space=pltpu.VMEM),
        out_shape=jax.ShapeDtypeStruct((N_DEV, r, lanes), block.dtype),
        scratch_shapes=[pltpu.SemaphoreType.DMA((N_DEV - 1,)), pltpu.SemaphoreType.DMA((N_DEV - 1,))],
    )(block)


def _adamw(w, g, m, v):
    m = ADAM_B1 * m + (1.0 - ADAM_B1) * g
    v = ADAM_B2 * v + (1.0 - ADAM_B2) * (g * g)
    m_hat = m / (1.0 - ADAM_B1 ** ADAM_STEP)
    v_hat = v / (1.0 - ADAM_B2 ** ADAM_STEP)
    delta = -ADAM_LR * (m_hat / (jnp.sqrt(v_hat) + ADAM_EPS) + ADAM_WD * w)
    return delta, m, v


def _ew_tile(rows):
    for cand in (256, 176, 128, 64, 32, 16, 8):
        if rows % cand == 0:
            return cand
    return rows


def sum_partials(chip, own, land, name):
    _, r, c = own.shape
    tr = _ew_tile(r)

    def body(k_ref, own_ref, p_ref, o_ref):
        o_ref[...] = ((own_ref[0].astype(f32) + p_ref[0].astype(f32)) + p_ref[1].astype(f32)) + p_ref[2].astype(f32)

    return pl.pallas_call(
        body, name=name,
        grid_spec=pltpu.PrefetchScalarGridSpec(
            num_scalar_prefetch=1, grid=(r // tr,),
            in_specs=[pl.BlockSpec((1, tr, c), lambda i, k: (k[0], i, 0)), pl.BlockSpec((3, tr, c), lambda i, k: (0, i, 0))],
            out_specs=pl.BlockSpec((tr, c), lambda i, k: (i, 0))),
        out_shape=jax.ShapeDtypeStruct((r, c), f32),
        compiler_params=_cparams(),
    )(chip, own, land)


def adamw_shard(p_mine, p_sibling, w, m, v, name):
    r, c = w.shape
    tr = _ew_tile(r)

    def body(a_ref, b_ref, w_ref, m_ref, v_ref, g_ref, d_ref, mo_ref, vo_ref):
        g = a_ref[...] + b_ref[...]
        delta, mn, vn = _adamw(w_ref[...], g, m_ref[...], v_ref[...])
        g_ref[...] = g
        d_ref[...] = delta
        mo_ref[...] = mn
        vo_ref[...] = vn

    blk = pl.BlockSpec((tr, c), lambda i: (i, 0))
    return pl.pallas_call(
        body, name=name, grid=(r // tr,),
        in_specs=[blk] * 5, out_specs=[blk] * 4,
        out_shape=[jax.ShapeDtypeStruct((r, c), f32)] * 4,
        compiler_params=_cparams(),
    )(p_mine, p_sibling, w, m, v)


def adamw_small(g8, w, m, v):
    _, r, lanes = g8.shape

    def body(g_ref, w_ref, m_ref, v_ref, go_ref, d_ref, mo_ref, vo_ref):
        g = g_ref[0]
        for i in range(1, N_DEV):
            g = g + g_ref[i]
        delta, mn, vn = _adamw(w_ref[...], g, m_ref[...], v_ref[...])
        go_ref[...] = g
        d_ref[...] = delta
        mo_ref[...] = mn
        vo_ref[...] = vn

    return pl.pallas_call(
        body, name="adamw_small",
        out_shape=[jax.ShapeDtypeStruct((r, lanes), f32)] * 4,
        compiler_params=_cparams(),
    )(g8, w, m, v)


def _pack(arrs, rows):
    flat = jnp.concatenate([a.reshape(-1).astype(f32) for a in arrs])
    return jnp.pad(flat, (0, rows * 128 - flat.shape[0])).reshape(rows, 128)


def _unpack(block, shapes):
    flat = block.reshape(-1)
    out, off = [], 0
    for s in shapes:
        n = 1
        for e in s:
            n *= e
        out.append(flat[off:off + n].reshape(s))
        off += n
    return out


BIG = ("ffn1_w_gate", "ffn1_w_up", "ffn1_w_down", "w_in", "w_branch_a", "w_branch_b", "w_out",
       "ffn2_w_gate", "ffn2_w_up", "ffn2_w_down")
SMALL = ("ffn1_norm", "mix_norm", "b_in", "sgu_norm_g", "sgu_norm_b", "sgu_w_s", "sgu_b_s", "ret_decay_logit",
         "ffn2_norm", "final_norm")
WEIGHTS = ("ffn1_norm", "ffn1_w_gate", "ffn1_w_up", "ffn1_w_down", "mix_norm", "w_in", "b_in", "sgu_norm_g",
           "sgu_norm_b", "sgu_w_s", "sgu_b_s", "ret_decay_logit", "w_branch_a", "w_branch_b", "w_out", "ffn2_norm",
           "ffn2_w_gate", "ffn2_w_up", "ffn2_w_down", "final_norm")


def kernel(x, ffn1_norm, ffn1_w_gate, ffn1_w_up, ffn1_w_down, mix_norm, w_in, b_in, sgu_norm_g, sgu_norm_b, sgu_w_s, sgu_b_s, ret_decay_logit, w_branch_a, w_branch_b, w_out, ffn2_norm, ffn2_w_gate, ffn2_w_up, ffn2_w_down, final_norm, loss_target, m_ffn1_norm, m_ffn1_w_gate, m_ffn1_w_up, m_ffn1_w_down, m_mix_norm, m_w_in, m_b_in, m_sgu_norm_g, m_sgu_norm_b, m_sgu_w_s, m_sgu_b_s, m_ret_decay_logit, m_w_branch_a, m_w_branch_b, m_w_out, m_ffn2_norm, m_ffn2_w_gate, m_ffn2_w_up, m_ffn2_w_down, m_final_norm, v_ffn1_norm, v_ffn1_w_gate, v_ffn1_w_up, v_ffn1_w_down, v_mix_norm, v_w_in, v_b_in, v_sgu_norm_g, v_sgu_norm_b, v_sgu_w_s, v_sgu_b_s, v_ret_decay_logit, v_w_branch_a, v_w_branch_b, v_w_out, v_ffn2_norm, v_ffn2_w_gate, v_ffn2_w_up, v_ffn2_w_down, v_final_norm):
    p = dict(ffn1_norm=ffn1_norm, ffn1_w_gate=ffn1_w_gate, ffn1_w_up=ffn1_w_up, ffn1_w_down=ffn1_w_down,
             mix_norm=mix_norm, w_in=w_in, b_in=b_in, sgu_norm_g=sgu_norm_g, sgu_norm_b=sgu_norm_b, sgu_w_s=sgu_w_s,
             sgu_b_s=sgu_b_s, ret_decay_logit=ret_decay_logit, w_branch_a=w_branch_a, w_branch_b=w_branch_b,
             w_out=w_out, ffn2_norm=ffn2_norm, ffn2_w_gate=ffn2_w_gate, ffn2_w_up=ffn2_w_up, ffn2_w_down=ffn2_w_down,
             final_norm=final_norm)
    mom = dict(ffn1_norm=m_ffn1_norm, ffn1_w_gate=m_ffn1_w_gate, ffn1_w_up=m_ffn1_w_up, ffn1_w_down=m_ffn1_w_down,
               mix_norm=m_mix_norm, w_in=m_w_in, b_in=m_b_in, sgu_norm_g=m_sgu_norm_g, sgu_norm_b=m_sgu_norm_b,
               sgu_w_s=m_sgu_w_s, sgu_b_s=m_sgu_b_s, ret_decay_logit=m_ret_decay_logit, w_branch_a=m_w_branch_a,
               w_branch_b=m_w_branch_b, w_out=m_w_out, ffn2_norm=m_ffn2_norm, ffn2_w_gate=m_ffn2_w_gate,
               ffn2_w_up=m_ffn2_w_up, ffn2_w_down=m_ffn2_w_down, final_norm=m_final_norm)
    var = dict(ffn1_norm=v_ffn1_norm, ffn1_w_gate=v_ffn1_w_gate, ffn1_w_up=v_ffn1_w_up, ffn1_w_down=v_ffn1_w_down,
               mix_norm=v_mix_norm, w_in=v_w_in, b_in=v_b_in, sgu_norm_g=v_sgu_norm_g, sgu_norm_b=v_sgu_norm_b,
               sgu_w_s=v_sgu_w_s, sgu_b_s=v_sgu_b_s, ret_decay_logit=v_ret_decay_logit, w_branch_a=v_w_branch_a,
               w_branch_b=v_w_branch_b, w_out=v_w_out, ffn2_norm=v_ffn2_norm, ffn2_w_gate=v_ffn2_w_gate,
               ffn2_w_up=v_ffn2_w_up, ffn2_w_down=v_ffn2_w_down, final_norm=v_final_norm)

    xs = x[0]
    tgt = loss_target[0]
    t, d = xs.shape
    dk = d // RET_HEADS
    tm = _row_tile(t)

    shards2d = {n: p[n][0] for n in BIG}
    chip = (2 * lax.axis_index("x") + lax.axis_index("y")).astype(jnp.int32).reshape(1)
    groups = {"ffn1": ("ffn1_w_gate", "ffn1_w_up", "ffn1_w_down"), "in": ("w_in",),
              "mix": ("w_branch_a", "w_branch_b", "w_out"), "ffn2": ("ffn2_w_gate", "ffn2_w_up", "ffn2_w_down")}
    bufs = [jnp.broadcast_to(shards2d[n].astype(bf16)[None], (N_CHIPS,) + shards2d[n].shape) for n in BIG]
    sems, bufs, tok = gather_start(bufs, [[BIG.index(n) for n in groups[g]] for g in ("ffn1", "in", "mix", "ffn2")])
    gsem = dict(zip(("ffn1", "in", "mix", "ffn2"), sems))
    pending = dict(zip(BIG, bufs))

    def arrive(g, after):
        return gather_wait([pending[n] for n in groups[g]], gsem[g], after, "gather_wait_" + g)

    bin4 = b_in.reshape(N_CHIPS, 1, 2 * d)
    ws_b = sgu_w_s[0].astype(bf16)
    bs_c = sgu_b_s[0][:, :, None]
    cols, mats, cdec, cos, sin = retention_constants(ret_decay_logit[0], t, dk)

    wg1, wu1, wd1 = arrive("ffn1", tok)
    x1, g1, u1 = ffn_fwd(xs, ffn1_norm, wg1, wu1, wd1, "ffn1_fwd")
    win, = arrive("in", x1)
    proj, hb2 = inproj_fwd(x1, mix_norm, win, bin4)
    a = sgu_fwd(proj, sgu_norm_g, sgu_norm_b, ws_b, bs_c)
    r, rn = ret_fwd(proj, cols, mats, cdec, cos, sin)
    wa, wb, wo = [w.reshape(d, d) for w in arrive("mix", rn)]
    x2, ba, br = mix_fwd(a, rn, proj, wa, wb, wo, x1)
    wg2, wu2, wd2 = arrive("ffn2", x2)
    x3, g2, u2 = ffn_fwd(x2, ffn2_norm, wg2, wu2, wd2, "ffn2_fwd")
    loss_blk, dx3, d_final = loss_head(x3, final_norm.reshape(1, d), tgt)

    sent = {}
    dx2, dg2, du2, act2, hb3, dyb2, d_ffn2n = ffn_bwd_act(dx3, x2, ffn2_norm, g2, u2, wg2, wu2, wd2, "ffn2_bwd_act", tok)
    sent["ffn2"] = exchange_start(list(ffn_weight_grads(hb3, dyb2, dg2, du2, act2, "ffn2_grad")), "exchange_start_ffn2")
    da, drn, dga, dgb, mixb, dba, dbr, dx2b = mix_bwd_act(dx2, ba, br, proj, wa, wb, wo, sent["ffn2"][3])
    row = pl.BlockSpec((tm, d), lambda s, i: (i, 0))
    colblk = pl.BlockSpec((tm, d // N_CHIPS), lambda s, i: (i, s))
    sent["mix"] = exchange_start(
        [tn_matmul(a, [dba], colblk, [row], N_CHIPS, d // N_CHIPS, [d], t, tm, "grad_w_branch_a"),
         tn_matmul(rn, [dbr], colblk, [row], N_CHIPS, d // N_CHIPS, [d], t, tm, "grad_w_branch_b"),
         tn_matmul(mixb, [dx2b], colblk, [row], N_CHIPS, d // N_CHIPS, [d], t, tm, "grad_w_out")], "exchange_start_mix")
    dua, dva, d_ws, d_bs, d_sng, d_snb = sgu_bwd(da, proj, sgu_norm_g, sgu_norm_b, ws_b, bs_c, sent["mix"][3])
    dq, dkr, dv, dgr, dlg = ret_bwd(drn, r, proj, cols, mats, cdec, cos, sin)
    segs = [dua, dva, dq, dkr, dv, dgr, dga, dgb]
    dx1, d_bin, d_mixn = inproj_bwd_act(segs, win, x1, mix_norm, dx2)
    row1 = pl.BlockSpec((tm, d), lambda s, i: (i, 0))
    sent["in"] = exchange_start([jnp.concatenate(
        [tn_matmul(hb2, [segs[2 * s], segs[2 * s + 1]], row1, [row1, row1], 1, d, [d, d], t, tm, "grad_w_in_%d" % s)
         for s in range(N_CHIPS)], axis=0)], "exchange_start_in")
    grad_x, dg1, du1, act1, hb1, dyb1, d_ffn1n = ffn_bwd_act(dx1, xs, ffn1_norm, g1, u1, wg1, wu1, wd1, "ffn1_bwd_act",
                                                              sent["in"][3])
    sent["ffn1"] = exchange_start(list(ffn_weight_grads(hb1, dyb1, dg1, du1, act1, "ffn1_grad")), "exchange_start_ffn1")

    out_g, out_d, out_m, out_v = {}, {}, {}, {}
    after = sent["ffn1"][3]
    for g in ("ffn2", "mix", "in", "ffn1"):
        gsems, own, lands, _ = sent[g]
        own, lands = exchange_wait(own, lands, gsems, after, "exchange_wait_" + g)
        plane = [sum_partials(chip, o, l, "sum_" + n) for n, o, l in zip(groups[g], own, lands)]
        other = swap_with_sibling(plane, "swap_" + g)
        for n, mine, sib in zip(groups[g], plane, other):
            gr, dl, mn, vn = adamw_shard(mine, sib, shards2d[n], mom[n][0], var[n][0], "adamw_" + n)
            out_g[n], out_d[n], out_m[n], out_v[n] = gr[None], dl[None], mn[None], vn[None]
        after = out_g[groups[g][-1]]

    dlogit = dlg[:, 0:2, 0].T * jax.nn.sigmoid(-ret_decay_logit[0].astype(f32))
    small_g = dict(ffn1_norm=d_ffn1n, mix_norm=d_mixn, b_in=d_bin, sgu_norm_g=d_sng, sgu_norm_b=d_snb, sgu_w_s=d_ws,
                   sgu_b_s=d_bs, ret_decay_logit=dlogit, ffn2_norm=d_ffn2n, final_norm=d_final)
    shapes = [p[n].shape for n in SMALL]
    total = sum(int(jnp.size(p[n])) for n in SMALL)
    rows = -(-total // (8 * 128)) * 8
    g8 = gather_small(_pack([small_g[n] for n in SMALL], rows))
    sg, sd, sm, sv = adamw_small(g8, _pack([p[n] for n in SMALL], rows), _pack([mom[n] for n in SMALL], rows),
                                 _pack([var[n] for n in SMALL], rows))
    for res, blockv in ((out_g, sg), (out_d, sd), (out_m, sm), (out_v, sv)):
        for n, val in zip(SMALL, _unpack(blockv, shapes)):
            res[n] = val

    loss = lax.psum(loss_blk[0, 0], ("x", "y", "c"))
    return (loss, grad_x[None], *[out_g[n] for n in WEIGHTS], *[out_d[n] for n in WEIGHTS],
            *[out_m[n] for n in WEIGHTS], *[out_v[n] for n in WEIGHTS])
```

```python
import functools

import jax
import jax.numpy as jnp
from jax import lax
from jax.experimental import pallas as pl
from jax.experimental.pallas import tpu as pltpu

f32 = jnp.float32
bf16 = jnp.bfloat16

CHUNK = 128
RET_HEADS = 4
SGU_GROUPS = 4
ROPE_BASE = 10000.0
NORM_EPS = 1e-6
ADAM_LR = 0.001
ADAM_B1 = 0.9
ADAM_B2 = 0.999
ADAM_EPS = 1e-08
ADAM_WD = 0.01
ADAM_STEP = 10
N_CHIPS = 4
N_DEV = 8
MESH = pl.DeviceIdType.MESH
VMEM_LIMIT = 52 * 1024 * 1024

_NT = (((1,), (1,)), ((), ()))
_TN = (((0,), (0,)), ((), ()))


def _cparams():
    return pltpu.CompilerParams(vmem_limit_bytes=VMEM_LIMIT)


def _row_tile(t):
    return 512 if t >= 2048 else t // 2


def _dot(a, b):
    return jnp.dot(a, b, preferred_element_type=f32)


def _dot_nt(a, b):
    return lax.dot_general(a, b, _NT, preferred_element_type=f32)


def _dot_tn(a, b):
    return lax.dot_general(a, b, _TN, preferred_element_type=f32)


def _rms(x, g):
    r = lax.rsqrt(jnp.mean(x * x, axis=-1, keepdims=True) + NORM_EPS)
    xh = x * r
    return xh * g, xh, r


def _rms_bwd(dy, xh, r, g):
    dxh = dy * g
    return r * (dxh - xh * jnp.mean(dxh * xh, axis=-1, keepdims=True))


def _sigmoid(x):
    return jax.nn.sigmoid(x)


def _dsilu(g, sg):
    return sg * (1.0 + g * (1.0 - sg))


def _gelu(x):
    return 0.5 * x * (1.0 + lax.erf(x * 0.7071067811865476))


def _dgelu(x):
    return 0.5 * (1.0 + lax.erf(x * 0.7071067811865476)) + x * jnp.exp(-0.5 * x * x) * 0.3989422804014327


def _acc_out(ref, first, val):
    @pl.when(first)
    def _():
        ref[...] = val

    @pl.when(jnp.logical_not(first))
    def _():
        ref[...] += val


def ffn_fwd(x, ng, wg, wu, wd, name):
    t, d = x.shape
    s4, fs, _ = wg.shape
    tm = _row_tile(t)

    def body(x_ref, ng_ref, wg_ref, wu_ref, wd_ref, xo_ref, g_ref, u_ref, h_scr, acc_scr):
        s = pl.program_id(1)

        @pl.when(s == 0)
        def _():
            y, _, _ = _rms(x_ref[...], ng_ref[...])
            h_scr[...] = y.astype(bf16)
            acc_scr[...] = jnp.zeros_like(acc_scr)

        h = h_scr[...]
        g = _dot_nt(h, wg_ref[0])
        u = _dot_nt(h, wu_ref[0])
        g_ref[0] = g.astype(bf16)
        u_ref[0] = u.astype(bf16)
        act = (g * _sigmoid(g) * u).astype(bf16)
        acc_scr[...] += _dot(act, wd_ref[0])

        @pl.when(s == s4 - 1)
        def _():
            xo_ref[...] = x_ref[...] + 0.5 * acc_scr[...]

    return pl.pallas_call(
        body, name=name, grid=(t // tm, s4),
        in_specs=[pl.BlockSpec((tm, d), lambda i, s: (i, 0)), pl.BlockSpec((1, d), lambda i, s: (0, 0)),
                  pl.BlockSpec((1, fs, d), lambda i, s: (s, 0, 0)), pl.BlockSpec((1, fs, d), lambda i, s: (s, 0, 0)),
                  pl.BlockSpec((1, fs, d), lambda i, s: (s, 0, 0))],
        out_specs=[pl.BlockSpec((tm, d), lambda i, s: (i, 0)), pl.BlockSpec((1, tm, fs), lambda i, s: (s, i, 0)),
                   pl.BlockSpec((1, tm, fs), lambda i, s: (s, i, 0))],
        out_shape=[jax.ShapeDtypeStruct((t, d), f32), jax.ShapeDtypeStruct((s4, t, fs), bf16),
                   jax.ShapeDtypeStruct((s4, t, fs), bf16)],
        scratch_shapes=[pltpu.VMEM((tm, d), bf16), pltpu.VMEM((tm, d), f32)],
        compiler_params=_cparams(),
    )(x, ng, wg, wu, wd)


def ffn_bwd_act(dxo, x, ng, g, u, wg, wu, wd, name, dep):
    t, d = x.shape
    s4, fs, _ = wg.shape
    tm = _row_tile(t)

    def body(dxo_ref, x_ref, ng_ref, g_ref, u_ref, wg_ref, wu_ref, wd_ref, dep_ref,
             dx_ref, dg_ref, du_ref, act_ref, hb_ref, dyb_ref, dng_ref, dy_scr, acc_scr):
        i = pl.program_id(0)
        s = pl.program_id(1)

        @pl.when(s == 0)
        def _():
            dyb = (0.5 * dxo_ref[...]).astype(bf16)
            dy_scr[...] = dyb
            dyb_ref[...] = dyb
            acc_scr[...] = jnp.zeros_like(acc_scr)

        dact = _dot_nt(dy_scr[...], wd_ref[0])
        gg = g_ref[0].astype(f32)
        uu = u_ref[0].astype(f32)
        sg = _sigmoid(gg)
        sil = gg * sg
        dgb = (dact * uu * _dsilu(gg, sg)).astype(bf16)
        dub = (dact * sil).astype(bf16)
        dg_ref[0] = dgb
        du_ref[0] = dub
        act_ref[0] = (sil * uu).astype(bf16)
        acc_scr[...] += _dot(dgb, wg_ref[0]) + _dot(dub, wu_ref[0])

        @pl.when(s == s4 - 1)
        def _():
            y, xh, r = _rms(x_ref[...], ng_ref[...])
            hb_ref[...] = y.astype(bf16)
            dh = acc_scr[...]
            dx_ref[...] = dxo_ref[...] + _rms_bwd(dh, xh, r, ng_ref[...])
            _acc_out(dng_ref, i == 0, jnp.sum(dh * xh, axis=0, keepdims=True))

    row = lambda i, s: (i, 0)
    shard = lambda i, s: (s, i, 0)
    wsp = lambda i, s: (s, 0, 0)
    return pl.pallas_call(
        body, name=name, grid=(t // tm, s4),
        in_specs=[pl.BlockSpec((tm, d), row), pl.BlockSpec((tm, d), row), pl.BlockSpec((1, d), lambda i, s: (0, 0)),
                  pl.BlockSpec((1, tm, fs), shard), pl.BlockSpec((1, tm, fs), shard),
                  pl.BlockSpec((1, fs, d), wsp), pl.BlockSpec((1, fs, d), wsp), pl.BlockSpec((1, fs, d), wsp), _ANY],
        out_specs=[pl.BlockSpec((tm, d), row), pl.BlockSpec((1, tm, fs), shard), pl.BlockSpec((1, tm, fs), shard),
                   pl.BlockSpec((1, tm, fs), shard), pl.BlockSpec((tm, d), row), pl.BlockSpec((tm, d), row),
                   pl.BlockSpec((1, d), lambda i, s: (0, 0))],
        out_shape=[jax.ShapeDtypeStruct((t, d), f32), jax.ShapeDtypeStruct((s4, t, fs), bf16),
                   jax.ShapeDtypeStruct((s4, t, fs), bf16), jax.ShapeDtypeStruct((s4, t, fs), bf16),
                   jax.ShapeDtypeStruct((t, d), bf16), jax.ShapeDtypeStruct((t, d), bf16),
                   jax.ShapeDtypeStruct((1, d), f32)],
        scratch_shapes=[pltpu.VMEM((tm, d), bf16), pltpu.VMEM((tm, d), f32)],
        compiler_params=_cparams(),
    )(dxo, x, ng, g, u, wg, wu, wd, dep)


def tn_matmul(xs, ys, x_spec, y_specs, n_shards, k1, k2s, t, tm, name):
    k2 = sum(k2s)
    ny = len(ys)

    def body(*refs):
        x_ref = refs[0]
        y_refs = refs[1:1 + ny]
        o_ref = refs[1 + ny]
        acc = refs[2 + ny]
        i = pl.program_id(1)
        xb = x_ref[0] if len(x_ref.shape) == 3 else x_ref[...]
        off = 0
        for y_ref, w in zip(y_refs, k2s):
            yb = y_ref[0] if len(y_ref.shape) == 3 else y_ref[...]
            part = _dot_tn(xb, yb)
            sl = (slice(None), slice(off, off + w))

            @pl.when(i == 0)
            def _(part=part, sl=sl):
                acc[sl] = part

            @pl.when(i > 0)
            def _(part=part, sl=sl):
                acc[sl] += part

            off += w

        @pl.when(i == t // tm - 1)
        def _():
            o_ref[0] = acc[...].astype(bf16)

    return pl.pallas_call(
        body, name=name, grid=(n_shards, t // tm),
        in_specs=[x_spec] + list(y_specs),
        out_specs=pl.BlockSpec((1, k1, k2), lambda s, i: (s, 0, 0)),
        out_shape=jax.ShapeDtypeStruct((n_shards, k1, k2), bf16),
        scratch_shapes=[pltpu.VMEM((k1, k2), f32)],
        compiler_params=_cparams(),
    )(xs, *ys)


def ffn_weight_grads(hb, dyb, dg, du, act, name):
    t, d = hb.shape
    s4, _, fs = dg.shape
    tm = _row_tile(t)
    row = pl.BlockSpec((tm, d), lambda s, i: (i, 0))
    shard = pl.BlockSpec((1, tm, fs), lambda s, i: (s, i, 0))
    gwg = tn_matmul(dg, [hb], shard, [row], s4, fs, [d], t, tm, name + "_wg")
    gwu = tn_matmul(du, [hb], shard, [row], s4, fs, [d], t, tm, name + "_wu")
    gwd = tn_matmul(act, [dyb], shard, [row], s4, fs, [d], t, tm, name + "_wd")
    return gwg, gwu, gwd


def inproj_fwd(x1, ng, win, bin4):
    t, d = x1.shape
    s4, _, w2 = win.shape
    tm = _row_tile(t)

    def body(x_ref, ng_ref, w_ref, b_ref, p_ref, hb_ref, h_scr):
        s = pl.program_id(1)

        @pl.when(s == 0)
        def _():
            y, _, _ = _rms(x_ref[...], ng_ref[...])
            h_scr[...] = y.astype(bf16)
            hb_ref[...] = y.astype(bf16)

        p_ref[0] = (_dot(h_scr[...], w_ref[0]) + b_ref[0]).astype(bf16)

    return pl.pallas_call(
        body, name="inproj_fwd", grid=(t // tm, s4),
        in_specs=[pl.BlockSpec((tm, d), lambda i, s: (i, 0)), pl.BlockSpec((1, d), lambda i, s: (0, 0)),
                  pl.BlockSpec((1, d, w2), lambda i, s: (s, 0, 0)), pl.BlockSpec((1, 1, w2), lambda i, s: (s, 0, 0))],
        out_specs=[pl.BlockSpec((1, tm, w2), lambda i, s: (s, i, 0)), pl.BlockSpec((tm, d), lambda i, s: (i, 0))],
        out_shape=[jax.ShapeDtypeStruct((s4, t, w2), bf16), jax.ShapeDtypeStruct((t, d), bf16)],
        scratch_shapes=[pltpu.VMEM((tm, d), bf16)],
        compiler_params=_cparams(),
    )(x1, ng, win, bin4)


def _sgu_norm(va, ng, nb):
    gv = _gelu(va)
    mu = jnp.mean(gv, axis=-1, keepdims=True)
    xc = gv - mu
    rstd = lax.rsqrt(jnp.mean(xc * xc, axis=-1, keepdims=True) + NORM_EPS)
    xh = xc * rstd
    return xh, rstd, (xh * ng + nb).astype(bf16)


def sgu_fwd(proj, ng, nb, ws, bs):
    _, t, w2 = proj.shape
    d = w2 // 2
    gd = d // SGU_GROUPS
    tm = _row_tile(t)

    def body(p_ref, ng_ref, nb_ref, ws_ref, bs_ref, a_ref):
        ua = p_ref[0, :, 0:d].astype(f32)
        va = p_ref[0, :, d:w2].astype(f32)
        gu = _gelu(ua)
        _, _, vn = _sgu_norm(va, ng_ref[...], nb_ref[...])
        for c in range(tm // CHUNK):
            rows = slice(c * CHUNK, (c + 1) * CHUNK)
            for g in range(SGU_GROUPS):
                cols = slice(g * gd, (g + 1) * gd)
                sg = _dot(ws_ref[g], vn[rows, cols]) + bs_ref[g]
                a_ref[rows, cols] = (gu[rows, cols] * sg).astype(bf16)

    return pl.pallas_call(
        body, name="sgu_fwd", grid=(t // tm,),
        in_specs=[pl.BlockSpec((1, tm, w2), lambda i: (0, i, 0)), pl.BlockSpec((1, d), lambda i: (0, 0)),
                  pl.BlockSpec((1, d), lambda i: (0, 0)), pl.BlockSpec((SGU_GROUPS, CHUNK, CHUNK), lambda i: (0, 0, 0)),
                  pl.BlockSpec((SGU_GROUPS, CHUNK, 1), lambda i: (0, 0, 0))],
        out_specs=pl.BlockSpec((tm, d), lambda i: (i, 0)),
        out_shape=jax.ShapeDtypeStruct((t, d), bf16),
        compiler_params=_cparams(),
    )(proj, ng, nb, ws, bs)


def sgu_bwd(da, proj, ng, nb, ws, bs, dep):
    _, t, w2 = proj.shape
    d = w2 // 2
    gd = d // SGU_GROUPS
    tm = _row_tile(t)

    def body(da_ref, p_ref, ng_ref, nb_ref, ws_ref, bs_ref, dep_ref,
             dua_ref, dva_ref, dws_ref, dbs_ref, dng_ref, dnb_ref, dvn_scr):
        i = pl.program_id(0)
        ua = p_ref[0, :, 0:d].astype(f32)
        va = p_ref[0, :, d:w2].astype(f32)
        gu = _gelu(ua)
        xh, rstd, vn = _sgu_norm(va, ng_ref[...], nb_ref[...])
        dad = da_ref[...].astype(f32)
        dsb = (dad * gu).astype(bf16)
        for c in range(tm // CHUNK):
            rows = slice(c * CHUNK, (c + 1) * CHUNK)
            for g in range(SGU_GROUPS):
                cols = slice(g * gd, (g + 1) * gd)
                sg = _dot(ws_ref[g], vn[rows, cols]) + bs_ref[g]
                dua_ref[rows, cols] = (dad[rows, cols] * sg * _dgelu(ua[rows, cols])).astype(bf16)
                ds = dsb[rows, cols]
                dvn_scr[rows, cols] = _dot_tn(ws_ref[g], ds)
                dw = _dot_nt(ds, vn[rows, cols])
                db = jnp.sum(ds.astype(f32), axis=1, keepdims=True)
                if c == 0:
                    _acc_out(dws_ref.at[g], i == 0, dw)
                    _acc_out(dbs_ref.at[g], i == 0, db)
                else:
                    dws_ref[g] += dw
                    dbs_ref[g] += db
        dvn = dvn_scr[...]
        _acc_out(dng_ref, i == 0, jnp.sum(dvn * xh, axis=0, keepdims=True))
        _acc_out(dnb_ref, i == 0, jnp.sum(dvn, axis=0, keepdims=True))
        dxh = dvn * ng_ref[...]
        dgv = rstd * (dxh - jnp.mean(dxh, axis=-1, keepdims=True) - xh * jnp.mean(dxh * xh, axis=-1, keepdims=True))
        dva_ref[...] = (dgv * _dgelu(va)).astype(bf16)

    row = pl.BlockSpec((tm, d), lambda i: (i, 0))
    vec = pl.BlockSpec((1, d), lambda i: (0, 0))
    wsp = pl.BlockSpec((SGU_GROUPS, CHUNK, CHUNK), lambda i: (0, 0, 0))
    bsp = pl.BlockSpec((SGU_GROUPS, CHUNK, 1), lambda i: (0, 0, 0))
    return pl.pallas_call(
        body, name="sgu_bwd", grid=(t // tm,),
        in_specs=[row, pl.BlockSpec((1, tm, w2), lambda i: (0, i, 0)), vec, vec, wsp, bsp, _ANY],
        out_specs=[row, row, wsp, bsp, vec, vec],
        out_shape=[jax.ShapeDtypeStruct((t, d), bf16), jax.ShapeDtypeStruct((t, d), bf16),
                   jax.ShapeDtypeStruct((SGU_GROUPS, CHUNK, CHUNK), f32), jax.ShapeDtypeStruct((SGU_GROUPS, CHUNK, 1), f32),
                   jax.ShapeDtypeStruct((1, d), f32), jax.ShapeDtypeStruct((1, d), f32)],
        scratch_shapes=[pltpu.VMEM((tm, d), f32)],
        compiler_params=_cparams(),
    )(da, proj, ng, nb, ws, bs, dep)


def retention_constants(decay_logit, t, dk):
    lg = jax.nn.log_sigmoid(decay_logit.astype(f32))
    lgf = lg[0][:, None]
    lgb = lg[1][:, None]
    idx = jnp.arange(CHUNK, dtype=f32)[None, :]
    af = jnp.exp((idx + 1.0) * lgf)
    ab = jnp.exp((CHUNK - idx) * lgb)
    kf = jnp.exp((CHUNK - 1.0 - idx) * lgf)
    kb = jnp.exp(idx * lgb)
    cols = jnp.stack([af, ab, kf, kb, af * (idx + 1.0), ab * (CHUNK - idx), kf * (CHUNK - 1.0 - idx), kb * idx], axis=1)
    cols = cols[..., None]
    diff = idx[0][:, None] - idx[0][None, :]
    dfm = jnp.where(diff >= 0, jnp.exp(jnp.maximum(diff, 0.0)[None] * lgf[:, :, None]), 0.0)
    dbm = jnp.where(diff < 0, jnp.exp(jnp.maximum(-diff, 0.0)[None] * lgb[:, :, None]), 0.0)
    mats = jnp.stack([dfm + dbm, dfm * diff[None], dbm * (-diff)[None]], axis=1)
    cdec = jnp.stack([jnp.broadcast_to(jnp.exp(CHUNK * lgf), (RET_HEADS, dk)),
                      jnp.broadcast_to(jnp.exp(CHUNK * lgb), (RET_HEADS, dk))], axis=1)
    theta = ROPE_BASE ** (-jnp.arange(0, dk, 2, dtype=f32) / dk)
    ang = jnp.arange(t, dtype=f32)[:, None] * theta[None, :]
    return cols, mats, cdec, jnp.cos(ang), jnp.sin(ang)


def _rot(tr, cos, sin):
    half = tr.shape[-1] // 2
    t1 = tr[:, :half]
    t2 = tr[:, half:]
    return jnp.concatenate([t1 * cos - t2 * sin, t2 * cos + t1 * sin], axis=-1)


def _rot_inv(dt, cos, sin):
    half = dt.shape[-1] // 2
    d1 = dt[:, :half]
    d2 = dt[:, half:]
    return jnp.concatenate([d1 * cos + d2 * sin, d2 * cos - d1 * sin], axis=-1)


def _ret_specs(t, d, dk, rt):
    nr = t // rt
    hq = d // dk

    def blk(p, n):
        return (1 - p) * (nr - 1 - n) + p * n

    q_spec = pl.BlockSpec((1, rt, dk), lambda h, p, n: (1, blk(p, n), h))
    k_spec = pl.BlockSpec((1, rt, dk), lambda h, p, n: (1, blk(p, n), hq + h))
    v_spec = pl.BlockSpec((1, rt, dk), lambda h, p, n: (2, blk(p, n), h))
    g_spec = pl.BlockSpec((1, rt, dk), lambda h, p, n: (2, blk(p, n), hq + h))
    tab_spec = pl.BlockSpec((rt, dk // 2), lambda h, p, n: (blk(p, n), 0))
    cols_spec = pl.BlockSpec((1, 8, CHUNK, 1), lambda h, p, n: (h, 0, 0, 0))
    mats_spec = pl.BlockSpec((1, 3, CHUNK, CHUNK), lambda h, p, n: (h, 0, 0, 0))
    cdec_spec = pl.BlockSpec((1, 2, dk), lambda h, p, n: (h, 0, 0))
    in_row = pl.BlockSpec((rt, dk), lambda h, p, n: (blk(p, n), h))
    out_row = pl.BlockSpec((rt, dk), lambda h, p, n: (p * n, h))
    return nr, blk, q_spec, k_spec, v_spec, g_spec, tab_spec, cols_spec, mats_spec, cdec_spec, in_row, out_row


def ret_fwd(proj, cols, mats, cdec, cos, sin):
    _, t, w2 = proj.shape
    d = w2 // 2
    dk = d // RET_HEADS
    rt = _row_tile(t)
    cpt = rt // CHUNK
    nr, blk, q_spec, k_spec, v_spec, g_spec, tab_spec, cols_spec, mats_spec, cdec_spec, _, out_row = _ret_specs(t, d, dk, rt)
    scale = dk ** -0.5

    def body(q_ref, k_ref, v_ref, g_ref, cos_ref, sin_ref, cols_ref, mats_ref, cdec_ref, r_ref, rn_ref, sb_scr, st):
        p = pl.program_id(1)
        n = pl.program_id(2)
        af, ab, kf, kb = cols_ref[0, 0], cols_ref[0, 1], cols_ref[0, 2], cols_ref[0, 3]
        cf = cdec_ref[0, 0:1, :]
        cb = cdec_ref[0, 1:2, :]

        @pl.when(n == 0)
        def _():
            st[...] = jnp.zeros_like(st)

        @pl.when(p == 0)
        def _():
            for j in reversed(range(cpt)):
                rows = slice(j * CHUNK, (j + 1) * CHUNK)
                ch = blk(p, n) * cpt + j
                kk = _rot(k_ref[0, rows, :].astype(f32), cos_ref[rows, :], sin_ref[rows, :]) * scale
                sb_scr[ch] = st[...].astype(bf16)
                st[...] = st[...] * cb + _dot_tn((kk * kb).astype(bf16), v_ref[0, rows, :])

        @pl.when(p == 1)
        def _():
            for j in range(cpt):
                rows = slice(j * CHUNK, (j + 1) * CHUNK)
                ch = blk(p, n) * cpt + j
                cs, sn = cos_ref[rows, :], sin_ref[rows, :]
                q = _rot(q_ref[0, rows, :].astype(f32), cs, sn)
                kk = _rot(k_ref[0, rows, :].astype(f32), cs, sn) * scale
                v = v_ref[0, rows, :]
                pm = (_dot_nt(q.astype(bf16), kk.astype(bf16)) * mats_ref[0, 0]).astype(bf16)
                out = (_dot(pm, v) + _dot((q * af).astype(bf16), st[...].astype(bf16))
                       + _dot((q * ab).astype(bf16), sb_scr[ch]))
                st[...] = st[...] * cf + _dot_tn((kk * kf).astype(bf16), v)
                rhat = out * lax.rsqrt(jnp.mean(out * out, axis=-1, keepdims=True) + NORM_EPS)
                gg = g_ref[0, rows, :].astype(f32)
                r_ref[rows, :] = out.astype(bf16)
                rn_ref[rows, :] = (rhat * gg * _sigmoid(gg)).astype(bf16)

    return pl.pallas_call(
        body, name="ret_fwd", grid=(RET_HEADS, 2, nr),
        in_specs=[q_spec, k_spec, v_spec, g_spec, tab_spec, tab_spec, cols_spec, mats_spec, cdec_spec],
        out_specs=[out_row, out_row],
        out_shape=[jax.ShapeDtypeStruct((t, d), bf16), jax.ShapeDtypeStruct((t, d), bf16)],
        scratch_shapes=[pltpu.VMEM((t // CHUNK, dk, dk), bf16), pltpu.VMEM((dk, dk), f32)],
        compiler_params=_cparams(),
    )(proj, proj, proj, proj, cos, sin, cols, mats, cdec)


def ret_bwd(drn, r, proj, cols, mats, cdec, cos, sin):
    _, t, w2 = proj.shape
    d = w2 // 2
    dk = d // RET_HEADS
    rt = _row_tile(t)
    cpt = rt // CHUNK
    nr, blk, q_spec, k_spec, v_spec, g_spec, tab_spec, cols_spec, mats_spec, cdec_spec, in_row, out_row = _ret_specs(t, d, dk, rt)
    scale = dk ** -0.5

    def body(drn_ref, r_ref, q_ref, k_ref, v_ref, g_ref, cos_ref, sin_ref, cols_ref, mats_ref, cdec_ref,
             dq_ref, dk_ref, dv_ref, dg_ref, dlg_ref,
             sb_scr, gf_scr, st_s, st_g, acc_af, acc_ab, acc_vf, acc_vb, acc_sf, acc_sb):
        p = pl.program_id(1)
        n = pl.program_id(2)
        af, ab, kf, kb = cols_ref[0, 0], cols_ref[0, 1], cols_ref[0, 2], cols_ref[0, 3]
        af1, ab1, kf1, kb1 = cols_ref[0, 4], cols_ref[0, 5], cols_ref[0, 6], cols_ref[0, 7]
        cf = cdec_ref[0, 0:1, :]
        cb = cdec_ref[0, 1:2, :]

        @pl.when(n == 0)
        def _():
            st_s[...] = jnp.zeros_like(st_s)
            st_g[...] = jnp.zeros_like(st_g)

        @pl.when(jnp.logical_and(n == 0, p == 1))
        def _():
            for a in (acc_af, acc_ab, acc_vf, acc_vb, acc_sf, acc_sb):
                a[...] = jnp.zeros_like(a)

        def load(rows):
            cs, sn = cos_ref[rows, :], sin_ref[rows, :]
            q = _rot(q_ref[0, rows, :].astype(f32), cs, sn)
            kk = _rot(k_ref[0, rows, :].astype(f32), cs, sn) * scale
            rr = r_ref[rows, :].astype(f32)
            rstd = lax.rsqrt(jnp.mean(rr * rr, axis=-1, keepdims=True) + NORM_EPS)
            rhat = rr * rstd
            gg = g_ref[0, rows, :].astype(f32)
            sg = _sigmoid(gg)
            dd = drn_ref[rows, :].astype(f32)
            drhat = dd * gg * sg
            dout = rstd * (drhat - rhat * jnp.mean(drhat * rhat, axis=-1, keepdims=True))
            dgr = dd * rhat * _dsilu(gg, sg)
            return q, kk, dout.astype(bf16), dgr, cs, sn

        @pl.when(p == 0)
        def _():
            for j in reversed(range(cpt)):
                rows = slice(j * CHUNK, (j + 1) * CHUNK)
                ch = blk(p, n) * cpt + j
                q, kk, doutb, _, _, _ = load(rows)
                sb_scr[ch] = st_s[...].astype(bf16)
                gf_scr[ch] = st_g[...].astype(bf16)
                st_s[...] = st_s[...] * cb + _dot_tn((kk * kb).astype(bf16), v_ref[0, rows, :])
                st_g[...] = st_g[...] * cf + _dot_tn((q * af).astype(bf16), doutb)

        @pl.when(p == 1)
        def _():
            for j in range(cpt):
                rows = slice(j * CHUNK, (j + 1) * CHUNK)
                ch = blk(p, n) * cpt + j
                q, kk, doutb, dgr, cs, sn = load(rows)
                v = v_ref[0, rows, :]
                qb = q.astype(bf16)
                kkb = kk.astype(bf16)
                sf = st_s[...]
                gb = st_g[...]
                sfb = sf.astype(bf16)
                gbb = gb.astype(bf16)
                sbb = sb_scr[ch]
                gfb = gf_scr[ch]
                dmat = mats_ref[0, 0]
                scores = _dot_nt(qb, kkb)
                dpraw = _dot_nt(doutb, v)
                dpb = (dpraw * dmat).astype(bf16)
                pmb = (scores * dmat).astype(bf16)
                x1 = _dot_nt(doutb, sfb)
                x2 = _dot_nt(doutb, sbb)
                y1 = _dot_nt(v, gfb)
                y2 = _dot_nt(v, gbb)
                kdf = (kk * kf).astype(bf16)
                kdb = (kk * kb).astype(bf16)
                dq = _dot(dpb, kkb) + x1 * af + x2 * ab
                dkk = _dot_tn(dpb, qb) + y1 * kf + y2 * kb
                dv = _dot_tn(pmb, doutb) + _dot(kdf, gfb) + _dot(kdb, gbb)
                ps = dpraw * scores
                acc_af[...] += ps * mats_ref[0, 1]
                acc_ab[...] += ps * mats_ref[0, 2]
                acc_vf[...] += x1 * q * af1 + y1 * kk * kf1
                acc_vb[...] += x2 * q * ab1 + y2 * kk * kb1
                acc_sf[...] += gfb.astype(f32) * sf
                acc_sb[...] += gb * sbb.astype(f32)
                st_s[...] = sf * cf + _dot_tn(kdf, v)
                st_g[...] = gb * cb + _dot_tn((q * ab).astype(bf16), doutb)
                dq_ref[rows, :] = _rot_inv(dq, cs, sn).astype(bf16)
                dk_ref[rows, :] = (_rot_inv(dkk, cs, sn) * scale).astype(bf16)
                dv_ref[rows, :] = dv.astype(bf16)
                dg_ref[rows, :] = dgr.astype(bf16)

        @pl.when(jnp.logical_and(p == 1, n == nr - 1))
        def _():
            tf = jnp.sum(acc_af[...]) + jnp.sum(acc_vf[...]) + CHUNK * jnp.sum(acc_sf[...] * cf)
            tb = jnp.sum(acc_ab[...]) + jnp.sum(acc_vb[...]) + CHUNK * jnp.sum(acc_sb[...] * cb)
            rid = lax.broadcasted_iota(jnp.int32, (8, 128), 0)
            dlg_ref[0] = jnp.where(rid == 0, tf, jnp.where(rid == 1, tb, 0.0))

    nch = t // CHUNK
    return pl.pallas_call(
        body, name="ret_bwd", grid=(RET_HEADS, 2, nr),
        in_specs=[in_row, in_row, q_spec, k_spec, v_spec, g_spec, tab_spec, tab_spec, cols_spec, mats_spec, cdec_spec],
        out_specs=[out_row, out_row, out_row, out_row, pl.BlockSpec((1, 8, 128), lambda h, p, n: (h, 0, 0))],
        out_shape=[jax.ShapeDtypeStruct((t, d), bf16)] * 4 + [jax.ShapeDtypeStruct((RET_HEADS, 8, 128), f32)],
        scratch_shapes=[pltpu.VMEM((nch, dk, dk), bf16), pltpu.VMEM((nch, dk, dk), bf16),
                        pltpu.VMEM((dk, dk), f32), pltpu.VMEM((dk, dk), f32),
                        pltpu.VMEM((CHUNK, CHUNK), f32), pltpu.VMEM((CHUNK, CHUNK), f32),
                        pltpu.VMEM((CHUNK, dk), f32), pltpu.VMEM((CHUNK, dk), f32),
                        pltpu.VMEM((dk, dk), f32), pltpu.VMEM((dk, dk), f32)],
        compiler_params=_cparams(),
    )(drn, r, proj, proj, proj, proj, cos, sin, cols, mats, cdec)


def mix_fwd(a, rn, proj, wa, wb, wo, x1):
    t, d = x1.shape
    tm = _row_tile(t)

    def body(a_ref, rn_ref, p_ref, wa_ref, wb_ref, wo_ref, x_ref, xo_ref, ba_ref, br_ref):
        ba = _dot(a_ref[...], wa_ref[...])
        br = _dot(rn_ref[...], wb_ref[...])
        sa = _sigmoid(p_ref[0, :, 0:d].astype(f32))
        sb = _sigmoid(p_ref[0, :, d:2 * d].astype(f32))
        mix = (sa * ba + sb * br).astype(bf16)
        xo_ref[...] = x_ref[...] + _dot(mix, wo_ref[...])
        ba_ref[...] = ba.astype(bf16)
        br_ref[...] = br.astype(bf16)

    row = pl.BlockSpec((tm, d), lambda i: (i, 0))
    wsp = pl.BlockSpec((d, d), lambda i: (0, 0))
    return pl.pallas_call(
        body, name="mix_fwd", grid=(t // tm,),
        in_specs=[row, row, pl.BlockSpec((1, tm, 2 * d), lambda i: (3, i, 0)), wsp, wsp, wsp, row],
        out_specs=[row, row, row],
        out_shape=[jax.ShapeDtypeStruct((t, d), f32), jax.ShapeDtypeStruct((t, d), bf16), jax.ShapeDtypeStruct((t, d), bf16)],
        compiler_params=_cparams(),
    )(a, rn, proj, wa, wb, wo, x1)


def mix_bwd_act(dx2, ba, br, proj, wa, wb, wo, dep):
    t, d = dx2.shape
    tm = _row_tile(t)

    def body(dx_ref, ba_ref, br_ref, p_ref, wa_ref, wb_ref, wo_ref, dep_ref,
             da_ref, drn_ref, dga_ref, dgb_ref, mix_ref, dba_ref, dbr_ref, dxb_ref):
        dxb = dx_ref[...].astype(bf16)
        dxb_ref[...] = dxb
        dmix = _dot_nt(dxb, wo_ref[...])
        ba = ba_ref[...].astype(f32)
        br = br_ref[...].astype(f32)
        sa = _sigmoid(p_ref[0, :, 0:d].astype(f32))
        sb = _sigmoid(p_ref[0, :, d:2 * d].astype(f32))
        mix_ref[...] = (sa * ba + sb * br).astype(bf16)
        dba = (dmix * sa).astype(bf16)
        dbr = (dmix * sb).astype(bf16)
        dba_ref[...] = dba
        dbr_ref[...] = dbr
        dga_ref[...] = (dmix * ba * sa * (1.0 - sa)).astype(bf16)
        dgb_ref[...] = (dmix * br * sb * (1.0 - sb)).astype(bf16)
        da_ref[...] = _dot_nt(dba, wa_ref[...]).astype(bf16)
        drn_ref[...] = _dot_nt(dbr, wb_ref[...]).astype(bf16)

    row = pl.BlockSpec((tm, d), lambda i: (i, 0))
    wsp = pl.BlockSpec((d, d), lambda i: (0, 0))
    return pl.pallas_call(
        body, name="mix_bwd_act", grid=(t // tm,),
        in_specs=[row, row, row, pl.BlockSpec((1, tm, 2 * d), lambda i: (3, i, 0)), wsp, wsp, wsp, _ANY],
        out_specs=[row] * 8,
        out_shape=[jax.ShapeDtypeStruct((t, d), bf16)] * 8,
        compiler_params=_cparams(),
    )(dx2, ba, br, proj, wa, wb, wo, dep)


def inproj_bwd_act(segs, win, x1, ng, dx2):
    t, d = x1.shape
    s4 = win.shape[0]
    tm = _row_tile(t) // 2
    nseg = len(segs)

    def body(*refs):
        seg_refs = refs[:nseg]
        w_ref, x_ref, ng_ref, dx2_ref, dx1_ref, db_ref, dng_ref = refs[nseg:]
        i = pl.program_id(0)
        dh = None
        for e, sr in enumerate(seg_refs):
            sb = sr[...]
            part = _dot_nt(sb, w_ref[e // 2, :, (e % 2) * d:(e % 2 + 1) * d])
            dh = part if dh is None else dh + part
            _acc_out(db_ref.at[e], i == 0, jnp.sum(sb.astype(f32), axis=0, keepdims=True))
        _, xh, r = _rms(x_ref[...], ng_ref[...])
        dx1_ref[...] = dx2_ref[...] + _rms_bwd(dh, xh, r, ng_ref[...])
        _acc_out(dng_ref, i == 0, jnp.sum(dh * xh, axis=0, keepdims=True))

    row = pl.BlockSpec((tm, d), lambda i: (i, 0))
    vec = pl.BlockSpec((1, d), lambda i: (0, 0))
    return pl.pallas_call(
        body, name="inproj_bwd_act", grid=(t // tm,),
        in_specs=[row] * nseg + [pl.BlockSpec((s4, d, 2 * d), lambda i: (0, 0, 0), pipeline_mode=pl.Buffered(1)),
                                 row, vec, row],
        out_specs=[row, pl.BlockSpec((nseg, 1, d), lambda i: (0, 0, 0)), vec],
        out_shape=[jax.ShapeDtypeStruct((t, d), f32), jax.ShapeDtypeStruct((nseg, 1, d), f32),
                   jax.ShapeDtypeStruct((1, d), f32)],
        compiler_params=_cparams(),
    )(*segs, win, x1, ng, dx2)


def loss_head(x3, fng, tgt):
    t, d = x3.shape
    tm = _row_tile(t)

    def body(x_ref, g_ref, t_ref, loss_ref, dx_ref, dg_ref):
        i = pl.program_id(0)
        y, xh, r = _rms(x_ref[...], g_ref[...])
        diff = y - t_ref[...]
        part = 0.5 * jnp.sum(jnp.sum(diff * diff, axis=0, keepdims=True), axis=1, keepdims=True) / d
        _acc_out(loss_ref, i == 0, jnp.broadcast_to(part, (1, 128)))
        dy = diff * (1.0 / d)
        dx_ref[...] = _rms_bwd(dy, xh, r, g_ref[...])
        _acc_out(dg_ref, i == 0, jnp.sum(dy * xh, axis=0, keepdims=True))

    row = pl.BlockSpec((tm, d), lambda i: (i, 0))
    vec = pl.BlockSpec((1, d), lambda i: (0, 0))
    return pl.pallas_call(
        body, name="loss_head", grid=(t // tm,),
        in_specs=[row, vec, row],
        out_specs=[pl.BlockSpec((1, 128), lambda i: (0, 0)), row, vec],
        out_shape=[jax.ShapeDtypeStruct((1, 128), f32), jax.ShapeDtypeStruct((t, d), f32), jax.ShapeDtypeStruct((1, d), f32)],
        compiler_params=_cparams(),
    )(x3, fng, tgt)


def _place():
    return lax.axis_index("x"), lax.axis_index("y"), lax.axis_index("c")


def _other_chips(x, y):
    return [(1 - x, y), (x, 1 - y), (1 - x, 1 - y)]


_ANY = pl.BlockSpec(memory_space=pl.ANY)


_HBM = pl.BlockSpec(memory_space=pltpu.HBM)
_SEM = pl.BlockSpec(memory_space=pltpu.SEMAPHORE)
_EFFECT = pltpu.SideEffectType.DATAFLOW_SIDE_EFFECTING


def _hbm(a):
    return pltpu.with_memory_space_constraint(a, pltpu.HBM)


def _chip_copy(src, dst, send_sem, recv_sem, chip, c):
    return pltpu.make_async_remote_copy(src_ref=src, dst_ref=dst, send_sem=send_sem, recv_sem=recv_sem,
                                        device_id=(chip[0], chip[1], c), device_id_type=MESH)


def gather_start(bufs, groups):
    nb, ng = len(bufs), len(groups)

    def body(*refs):
        ins = refs[:nb]
        sems = refs[nb:nb + 2 * ng]
        token = refs[-1]
        x, y, c = _place()
        k = 2 * x + y
        for gi, grp in enumerate(groups):
            for wi, w in enumerate(grp):
                for j, chip in enumerate(_other_chips(x, y)):
                    _chip_copy(ins[w].at[k], ins[w].at[k], sems[2 * gi].at[3 * wi + j], sems[2 * gi + 1].at[3 * wi + j],
                               chip, c).start()
        token[...] = jnp.zeros_like(token)

    sem_shapes = []
    for grp in groups:
        sem_shapes += [pltpu.SemaphoreType.DMA((3 * len(grp),)), pltpu.SemaphoreType.DMA((3 * len(grp),))]
    outs = pl.pallas_call(
        body, name="gather_start",
        out_shape=sem_shapes + [pltpu.HBM(b.shape, b.dtype) for b in bufs] + [jax.ShapeDtypeStruct((8, 128), f32)],
        in_specs=[_HBM] * nb,
        out_specs=[_SEM] * (2 * ng) + [_HBM] * nb + [pl.BlockSpec(memory_space=pltpu.VMEM)],
        input_output_aliases={w: 2 * ng + w for w in range(nb)},
        compiler_params=pltpu.CompilerParams(has_side_effects=_EFFECT),
    )(*[_hbm(b) for b in bufs])
    sems = [(outs[2 * gi], outs[2 * gi + 1]) for gi in range(ng)]
    return sems, list(outs[2 * ng:2 * ng + nb]), outs[-1]


def gather_wait(bufs, sems, after, name):
    n = len(bufs)

    def body(*refs):
        ins = refs[:n]
        send_sems, recv_sems = refs[n], refs[n + 1]
        x, y, c = _place()
        k = 2 * x + y
        for wi in range(n):
            for j, chip in enumerate(_other_chips(x, y)):
                cp = _chip_copy(ins[wi].at[k], ins[wi].at[2 * chip[0] + chip[1]], send_sems.at[3 * wi + j],
                                recv_sems.at[3 * wi + j], chip, c)
                cp.wait_send()
                cp.wait_recv()

    outs = pl.pallas_call(
        body, name=name,
        out_shape=[pltpu.HBM(b.shape, b.dtype) for b in bufs],
        in_specs=[_HBM] * n + [_SEM, _SEM, _ANY],
        out_specs=[_HBM] * n,
        input_output_aliases={i: i for i in range(n)},
        compiler_params=pltpu.CompilerParams(has_side_effects=_EFFECT),
    )(*bufs, sems[0], sems[1], after)
    return list(outs)


def exchange_start(grads, name):
    n = len(grads)
    lands = [lax.empty((3,) + g.shape[1:], g.dtype) for g in grads]

    def body(*refs):
        ins = refs[:n]
        land = refs[n:2 * n]
        send_sems, recv_sems = refs[2 * n], refs[2 * n + 1]
        token = refs[-1]
        x, y, c = _place()
        for wi in range(n):
            for j, chip in enumerate(_other_chips(x, y)):
                _chip_copy(ins[wi].at[2 * chip[0] + chip[1]], land[wi].at[j], send_sems.at[3 * wi + j],
                           recv_sems.at[3 * wi + j], chip, c).start()
        token[...] = jnp.zeros_like(token)

    outs = pl.pallas_call(
        body, name=name,
        out_shape=[pltpu.SemaphoreType.DMA((3 * n,)), pltpu.SemaphoreType.DMA((3 * n,))]
        + [pltpu.HBM(g.shape, g.dtype) for g in grads] + [pltpu.HBM(l.shape, l.dtype) for l in lands]
        + [jax.ShapeDtypeStruct((8, 128), f32)],
        in_specs=[_HBM] * (2 * n),
        out_specs=[_SEM, _SEM] + [_HBM] * (2 * n) + [pl.BlockSpec(memory_space=pltpu.VMEM)],
        input_output_aliases={i: 2 + i for i in range(2 * n)},
        compiler_params=pltpu.CompilerParams(has_side_effects=_EFFECT),
    )(*[_hbm(g) for g in grads], *[_hbm(l) for l in lands])
    return (outs[0], outs[1]), list(outs[2:2 + n]), list(outs[2 + n:2 + 2 * n]), outs[-1]


def exchange_wait(grads, lands, sems, after, name):
    n = len(grads)

    def body(*refs):
        ins = refs[:n]
        land = refs[n:2 * n]
        send_sems, recv_sems = refs[2 * n], refs[2 * n + 1]
        x, y, c = _place()
        for wi in range(n):
            for j, chip in enumerate(_other_chips(x, y)):
                cp = _chip_copy(ins[wi].at[2 * chip[0] + chip[1]], land[wi].at[j], send_sems.at[3 * wi + j],
                                recv_sems.at[3 * wi + j], chip, c)
                cp.wait_send()
                cp.wait_recv()

    outs = pl.pallas_call(
        body, name=name,
        out_shape=[pltpu.HBM(g.shape, g.dtype) for g in grads] + [pltpu.HBM(l.shape, l.dtype) for l in lands],
        in_specs=[_HBM] * (2 * n) + [_SEM, _SEM, _ANY],
        out_specs=[_HBM] * (2 * n),
        input_output_aliases={i: i for i in range(2 * n)},
        compiler_params=pltpu.CompilerParams(has_side_effects=_EFFECT),
    )(*grads, *lands, sems[0], sems[1], after)
    return list(outs[:n]), list(outs[n:])


def swap_with_sibling(parts, name):
    nw = len(parts)

    def body(*refs):
        ins = refs[:nw]
        outs = refs[nw:2 * nw]
        send_sems, recv_sems = refs[2 * nw:]
        x, y, c = _place()
        copies = [pltpu.make_async_remote_copy(
            src_ref=ins[w], dst_ref=outs[w], send_sem=send_sems.at[w], recv_sem=recv_sems.at[w],
            device_id=(x, y, 1 - c), device_id_type=MESH) for w in range(nw)]
        for cp in copies:
            cp.start()
        for cp in copies:
            cp.wait()

    return pl.pallas_call(
        body, name=name,
        in_specs=[_ANY] * nw, out_specs=[_ANY] * nw,
        out_shape=[jax.ShapeDtypeStruct(p.shape, p.dtype) for p in parts],
        scratch_shapes=[pltpu.SemaphoreType.DMA((nw,)), pltpu.SemaphoreType.DMA((nw,))],
    )(*parts)


def gather_small(block):
    r, lanes = block.shape

    def body(b_ref, o_ref, send_sems, recv_sems):
        x, y, c = _place()
        me = 4 * x + 2 * y + c
        o_ref[me] = b_ref[...]
        peers = []
        for m in range(1, N_DEV):
            px = 1 - x if m & 4 else x
            py = 1 - y if m & 2 else y
            pc = 1 - c if m & 1 else c
            peers.append((px, py, pc))
        for m, peer in enumerate(peers):
            pltpu.make_async_remote_copy(
                src_ref=b_ref, dst_ref=o_ref.at[me], send_sem=send_sems.at[m], recv_sem=recv_sems.at[m],
                device_id=peer, device_id_type=MESH).start()
        for m, (px, py, pc) in enumerate(peers):
            pltpu.make_async_remote_copy(
                src_ref=b_ref, dst_ref=o_ref.at[4 * px + 2 * py + pc], send_sem=send_sems.at[m],
                recv_sem=recv_sems.at[m], device_id=(px, py, pc), device_id_type=MESH).wait()

    return pl.pallas_call(
        body, name="gather_small",
        in_specs=[pl.BlockSpec(memory_space=pltpu.VMEM)], out_specs=pl.BlockSpec(memory_space=pltpu.VMEM),
        out_shape=jax.ShapeDtypeStruct((N_DEV, r, lanes), block.dtype),
        scratch_shapes=[pltpu.SemaphoreType.DMA((N_DEV - 1,)), pltpu.SemaphoreType.DMA((N_DEV - 1,))],
    )(block)


def _adamw(w, g, m, v):
    m = ADAM_B1 * m + (1.0 - ADAM_B1) * g
    v = ADAM_B2 * v + (1.0 - ADAM_B2) * (g * g)
    m_hat = m / (1.0 - ADAM_B1 ** ADAM_STEP)
    v_hat = v / (1.0 - ADAM_B2 ** ADAM_STEP)
    delta = -ADAM_LR * (m_hat / (jnp.sqrt(v_hat) + ADAM_EPS) + ADAM_WD * w)
    return delta, m, v


def _ew_tile(rows):
    for cand in (256, 176, 128, 64, 32, 16, 8):
        if rows % cand == 0:
            return cand
    return rows


def sum_partials(chip, own, land, name):
    _, r, c = own.shape
    tr = _ew_tile(r)

    def body(k_ref, own_ref, p_ref, o_ref):
        o_ref[...] = ((own_ref[0].astype(f32) + p_ref[0].astype(f32)) + p_ref[1].astype(f32)) + p_ref[2].astype(f32)

    return pl.pallas_call(
        body, name=name,
        grid_spec=pltpu.PrefetchScalarGridSpec(
            num_scalar_prefetch=1, grid=(r // tr,),
            in_specs=[pl.BlockSpec((1, tr, c), lambda i, k: (k[0], i, 0)), pl.BlockSpec((3, tr, c), lambda i, k: (0, i, 0))],
            out_specs=pl.BlockSpec((tr, c), lambda i, k: (i, 0))),
        out_shape=jax.ShapeDtypeStruct((r, c), f32),
        compiler_params=_cparams(),
    )(chip, own, land)


def adamw_shard(p_mine, p_sibling, w, m, v, name):
    r, c = w.shape
    tr = _ew_tile(r)

    def body(a_ref, b_ref, w_ref, m_ref, v_ref, g_ref, d_ref, mo_ref, vo_ref):
        g = a_ref[...] + b_ref[...]
        delta, mn, vn = _adamw(w_ref[...], g, m_ref[...], v_ref[...])
        g_ref[...] = g
        d_ref[...] = delta
        mo_ref[...] = mn
        vo_ref[...] = vn

    blk = pl.BlockSpec((tr, c), lambda i: (i, 0))
    return pl.pallas_call(
        body, name=name, grid=(r // tr,),
        in_specs=[blk] * 5, out_specs=[blk] * 4,
        out_shape=[jax.ShapeDtypeStruct((r, c), f32)] * 4,
        compiler_params=_cparams(),
    )(p_mine, p_sibling, w, m, v)


def adamw_small(g8, w, m, v):
    _, r, lanes = g8.shape

    def body(g_ref, w_ref, m_ref, v_ref, go_ref, d_ref, mo_ref, vo_ref):
        g = g_ref[0]
        for i in range(1, N_DEV):
            g = g + g_ref[i]
        delta, mn, vn = _adamw(w_ref[...], g, m_ref[...], v_ref[...])
        go_ref[...] = g
        d_ref[...] = delta
        mo_ref[...] = mn
        vo_ref[...] = vn

    return pl.pallas_call(
        body, name="adamw_small",
        out_shape=[jax.ShapeDtypeStruct((r, lanes), f32)] * 4,
        compiler_params=_cparams(),
    )(g8, w, m, v)


def _size(shape):
    n = 1
    for e in shape:
        n *= e
    return n


def _pack_rows(shapes):
    rows = [-(-_size(s) // 1024) * 8 for s in shapes]
    return rows, sum(rows)


def _pack(arrs, shapes):
    rows, _ = _pack_rows(shapes)
    parts = [jnp.pad(a.reshape(-1).astype(f32), (0, r * 128 - _size(s))).reshape(r, 128)
             for a, s, r in zip(arrs, shapes, rows)]
    return jnp.concatenate(parts, axis=0)


def _unpack(block, shapes):
    rows, _ = _pack_rows(shapes)
    out, off = [], 0
    for s, r in zip(shapes, rows):
        out.append(block[off:off + r].reshape(-1)[:_size(s)].reshape(s))
        off += r
    return out


TRANSPOSED = ("ffn1_w_gate", "ffn1_w_up", "ffn2_w_gate", "ffn2_w_up")


def _shard2d(a, n):
    return a[0].T if n in TRANSPOSED else a[0]


def _unshard(a, n):
    return (a.T if n in TRANSPOSED else a)[None]


BIG = ("ffn1_w_gate", "ffn1_w_up", "ffn1_w_down", "w_in", "w_branch_a", "w_branch_b", "w_out",
       "ffn2_w_gate", "ffn2_w_up", "ffn2_w_down")
SMALL = ("ffn1_norm", "mix_norm", "b_in", "sgu_norm_g", "sgu_norm_b", "sgu_w_s", "sgu_b_s", "ret_decay_logit",
         "ffn2_norm", "final_norm")
WEIGHTS = ("ffn1_norm", "ffn1_w_gate", "ffn1_w_up", "ffn1_w_down", "mix_norm", "w_in", "b_in", "sgu_norm_g",
           "sgu_norm_b", "sgu_w_s", "sgu_b_s", "ret_decay_logit", "w_branch_a", "w_branch_b", "w_out", "ffn2_norm",
           "ffn2_w_gate", "ffn2_w_up", "ffn2_w_down", "final_norm")


def kernel(x, ffn1_norm, ffn1_w_gate, ffn1_w_up, ffn1_w_down, mix_norm, w_in, b_in, sgu_norm_g, sgu_norm_b, sgu_w_s, sgu_b_s, ret_decay_logit, w_branch_a, w_branch_b, w_out, ffn2_norm, ffn2_w_gate, ffn2_w_up, ffn2_w_down, final_norm, loss_target, m_ffn1_norm, m_ffn1_w_gate, m_ffn1_w_up, m_ffn1_w_down, m_mix_norm, m_w_in, m_b_in, m_sgu_norm_g, m_sgu_norm_b, m_sgu_w_s, m_sgu_b_s, m_ret_decay_logit, m_w_branch_a, m_w_branch_b, m_w_out, m_ffn2_norm, m_ffn2_w_gate, m_ffn2_w_up, m_ffn2_w_down, m_final_norm, v_ffn1_norm, v_ffn1_w_gate, v_ffn1_w_up, v_ffn1_w_down, v_mix_norm, v_w_in, v_b_in, v_sgu_norm_g, v_sgu_norm_b, v_sgu_w_s, v_sgu_b_s, v_ret_decay_logit, v_w_branch_a, v_w_branch_b, v_w_out, v_ffn2_norm, v_ffn2_w_gate, v_ffn2_w_up, v_ffn2_w_down, v_final_norm):
    p = dict(ffn1_norm=ffn1_norm, ffn1_w_gate=ffn1_w_gate, ffn1_w_up=ffn1_w_up, ffn1_w_down=ffn1_w_down,
             mix_norm=mix_norm, w_in=w_in, b_in=b_in, sgu_norm_g=sgu_norm_g, sgu_norm_b=sgu_norm_b, sgu_w_s=sgu_w_s,
             sgu_b_s=sgu_b_s, ret_decay_logit=ret_decay_logit, w_branch_a=w_branch_a, w_branch_b=w_branch_b,
             w_out=w_out, ffn2_norm=ffn2_norm, ffn2_w_gate=ffn2_w_gate, ffn2_w_up=ffn2_w_up, ffn2_w_down=ffn2_w_down,
             final_norm=final_norm)
    mom = dict(ffn1_norm=m_ffn1_norm, ffn1_w_gate=m_ffn1_w_gate, ffn1_w_up=m_ffn1_w_up, ffn1_w_down=m_ffn1_w_down,
               mix_norm=m_mix_norm, w_in=m_w_in, b_in=m_b_in, sgu_norm_g=m_sgu_norm_g, sgu_norm_b=m_sgu_norm_b,
               sgu_w_s=m_sgu_w_s, sgu_b_s=m_sgu_b_s, ret_decay_logit=m_ret_decay_logit, w_branch_a=m_w_branch_a,
               w_branch_b=m_w_branch_b, w_out=m_w_out, ffn2_norm=m_ffn2_norm, ffn2_w_gate=m_ffn2_w_gate,
               ffn2_w_up=m_ffn2_w_up, ffn2_w_down=m_ffn2_w_down, final_norm=m_final_norm)
    var = dict(ffn1_norm=v_ffn1_norm, ffn1_w_gate=v_ffn1_w_gate, ffn1_w_up=v_ffn1_w_up, ffn1_w_down=v_ffn1_w_down,
               mix_norm=v_mix_norm, w_in=v_w_in, b_in=v_b_in, sgu_norm_g=v_sgu_norm_g, sgu_norm_b=v_sgu_norm_b,
               sgu_w_s=v_sgu_w_s, sgu_b_s=v_sgu_b_s, ret_decay_logit=v_ret_decay_logit, w_branch_a=v_w_branch_a,
               w_branch_b=v_w_branch_b, w_out=v_w_out, ffn2_norm=v_ffn2_norm, ffn2_w_gate=v_ffn2_w_gate,
               ffn2_w_up=v_ffn2_w_up, ffn2_w_down=v_ffn2_w_down, final_norm=v_final_norm)

    xs = x[0]
    tgt = loss_target[0]
    t, d = xs.shape
    dk = d // RET_HEADS
    tm = _row_tile(t)

    shards2d = {n: _shard2d(p[n], n) for n in BIG}
    chip = (2 * lax.axis_index("x") + lax.axis_index("y")).astype(jnp.int32).reshape(1)
    groups = {"ffn1": ("ffn1_w_gate", "ffn1_w_up", "ffn1_w_down"), "in": ("w_in",),
              "mix": ("w_branch_a", "w_branch_b", "w_out"), "ffn2": ("ffn2_w_gate", "ffn2_w_up", "ffn2_w_down")}
    bufs = [jnp.broadcast_to(shards2d[n].astype(bf16)[None], (N_CHIPS,) + shards2d[n].shape) for n in BIG]
    sems, bufs, tok = gather_start(bufs, [[BIG.index(n) for n in groups[g]] for g in ("ffn1", "in", "mix", "ffn2")])
    gsem = dict(zip(("ffn1", "in", "mix", "ffn2"), sems))
    pending = dict(zip(BIG, bufs))

    def arrive(g, after):
        return gather_wait([pending[n] for n in groups[g]], gsem[g], after, "gather_wait_" + g)

    bin4 = b_in.reshape(N_CHIPS, 1, 2 * d)
    ws_b = sgu_w_s[0].astype(bf16)
    bs_c = sgu_b_s[0][:, :, None]
    cols, mats, cdec, cos, sin = retention_constants(ret_decay_logit[0], t, dk)

    wg1, wu1, wd1 = arrive("ffn1", tok)
    x1, g1, u1 = ffn_fwd(xs, ffn1_norm, wg1, wu1, wd1, "ffn1_fwd")
    win, = arrive("in", x1)
    proj, hb2 = inproj_fwd(x1, mix_norm, win, bin4)
    a = sgu_fwd(proj, sgu_norm_g, sgu_norm_b, ws_b, bs_c)
    r, rn = ret_fwd(proj, cols, mats, cdec, cos, sin)
    wa, wb, wo = [w.reshape(d, d) for w in arrive("mix", rn)]
    x2, ba, br = mix_fwd(a, rn, proj, wa, wb, wo, x1)
    wg2, wu2, wd2 = arrive("ffn2", x2)
    x3, g2, u2 = ffn_fwd(x2, ffn2_norm, wg2, wu2, wd2, "ffn2_fwd")
    loss_blk, dx3, d_final = loss_head(x3, final_norm.reshape(1, d), tgt)

    sent = {}
    dx2, dg2, du2, act2, hb3, dyb2, d_ffn2n = ffn_bwd_act(dx3, x2, ffn2_norm, g2, u2, wg2, wu2, wd2, "ffn2_bwd_act", tok)
    sent["ffn2"] = exchange_start(list(ffn_weight_grads(hb3, dyb2, dg2, du2, act2, "ffn2_grad")), "exchange_start_ffn2")
    da, drn, dga, dgb, mixb, dba, dbr, dx2b = mix_bwd_act(dx2, ba, br, proj, wa, wb, wo, sent["ffn2"][3])
    row = pl.BlockSpec((tm, d), lambda s, i: (i, 0))
    colblk = pl.BlockSpec((tm, d // N_CHIPS), lambda s, i: (i, s))
    sent["mix"] = exchange_start(
        [tn_matmul(a, [dba], colblk, [row], N_CHIPS, d // N_CHIPS, [d], t, tm, "grad_w_branch_a"),
         tn_matmul(rn, [dbr], colblk, [row], N_CHIPS, d // N_CHIPS, [d], t, tm, "grad_w_branch_b"),
         tn_matmul(mixb, [dx2b], colblk, [row], N_CHIPS, d // N_CHIPS, [d], t, tm, "grad_w_out")], "exchange_start_mix")
    dua, dva, d_ws, d_bs, d_sng, d_snb = sgu_bwd(da, proj, sgu_norm_g, sgu_norm_b, ws_b, bs_c, sent["mix"][3])
    dq, dkr, dv, dgr, dlg = ret_bwd(drn, r, proj, cols, mats, cdec, cos, sin)
    segs = [dua, dva, dq, dkr, dv, dgr, dga, dgb]
    dx1, d_bin, d_mixn = inproj_bwd_act(segs, win, x1, mix_norm, dx2)
    row1 = pl.BlockSpec((tm, d), lambda s, i: (i, 0))
    sent["in"] = exchange_start([jnp.concatenate(
        [tn_matmul(hb2, [segs[2 * s], segs[2 * s + 1]], row1, [row1, row1], 1, d, [d, d], t, tm, "grad_w_in_%d" % s)
         for s in range(N_CHIPS)], axis=0)], "exchange_start_in")
    grad_x, dg1, du1, act1, hb1, dyb1, d_ffn1n = ffn_bwd_act(dx1, xs, ffn1_norm, g1, u1, wg1, wu1, wd1, "ffn1_bwd_act",
                                                              sent["in"][3])
    sent["ffn1"] = exchange_start(list(ffn_weight_grads(hb1, dyb1, dg1, du1, act1, "ffn1_grad")), "exchange_start_ffn1")

    out_g, out_d, out_m, out_v = {}, {}, {}, {}
    after = sent["ffn1"][3]
    for g in ("ffn2", "mix", "in", "ffn1"):
        gsems, own, lands, _ = sent[g]
        own, lands = exchange_wait(own, lands, gsems, after, "exchange_wait_" + g)
        plane = [sum_partials(chip, o, l, "sum_" + n) for n, o, l in zip(groups[g], own, lands)]
        other = swap_with_sibling(plane, "swap_" + g)
        for n, mine, sib in zip(groups[g], plane, other):
            res = adamw_shard(mine, sib, shards2d[n], _shard2d(mom[n], n), _shard2d(var[n], n), "adamw_" + n)
            out_g[n], out_d[n], out_m[n], out_v[n] = [_unshard(o, n) for o in res]
        after = out_g[groups[g][-1]]

    dlogit = dlg[:, 0:2, 0].T * jax.nn.sigmoid(-ret_decay_logit[0].astype(f32))
    small_g = dict(ffn1_norm=d_ffn1n, mix_norm=d_mixn, b_in=d_bin, sgu_norm_g=d_sng, sgu_norm_b=d_snb, sgu_w_s=d_ws,
                   sgu_b_s=d_bs, ret_decay_logit=dlogit, ffn2_norm=d_ffn2n, final_norm=d_final)
    shapes = [p[n].shape for n in SMALL]
    g8 = gather_small(_pack([small_g[n] for n in SMALL], shapes))
    sg, sd, sm, sv = adamw_small(g8, _pack([p[n] for n in SMALL], shapes), _pack([mom[n] for n in SMALL], shapes),
                                 _pack([var[n] for n in SMALL], shapes))
    for res, blockv in ((out_g, sg), (out_d, sd), (out_m, sm), (out_v, sv)):
        for n, val in zip(SMALL, _unpack(blockv, shapes)):
            res[n] = val

    loss = lax.psum(loss_blk[0, 0], ("x", "y", "c"))
    return (loss, grad_x[None], *[out_g[n] for n in WEIGHTS], *[out_d[n] for n in WEIGHTS],
            *[out_m[n] for n in WEIGHTS], *[out_v[n] for n in WEIGHTS])
```

```python
import functools

import jax
import jax.numpy as jnp
from jax import lax
from jax.experimental import pallas as pl
from jax.experimental.pallas import tpu as pltpu

f32 = jnp.float32
bf16 = jnp.bfloat16

CHUNK = 128
RET_HEADS = 4
SGU_GROUPS = 4
ROPE_BASE = 10000.0
NORM_EPS = 1e-6
ADAM_LR = 0.001
ADAM_B1 = 0.9
ADAM_B2 = 0.999
ADAM_EPS = 1e-08
ADAM_WD = 0.01
ADAM_STEP = 10
N_CHIPS = 4
N_DEV = 8
MESH = pl.DeviceIdType.MESH
VMEM_LIMIT = 52 * 1024 * 1024

_NT = (((1,), (1,)), ((), ()))
_TN = (((0,), (0,)), ((), ()))


def _cparams():
    return pltpu.CompilerParams(vmem_limit_bytes=VMEM_LIMIT)


def _row_tile(t):
    return 512 if t >= 2048 else t // 2


def _dot(a, b):
    return jnp.dot(a, b, preferred_element_type=f32)


def _dot_nt(a, b):
    return lax.dot_general(a, b, _NT, preferred_element_type=f32)


def _dot_tn(a, b):
    return lax.dot_general(a, b, _TN, preferred_element_type=f32)


def _rms(x, g):
    r = lax.rsqrt(jnp.mean(x * x, axis=-1, keepdims=True) + NORM_EPS)
    xh = x * r
    return xh * g, xh, r


def _rms_bwd(dy, xh, r, g):
    dxh = dy * g
    return r * (dxh - xh * jnp.mean(dxh * xh, axis=-1, keepdims=True))


def _sigmoid(x):
    return jax.nn.sigmoid(x)


def _dsilu(g, sg):
    return sg * (1.0 + g * (1.0 - sg))


def _gelu(x):
    return 0.5 * x * (1.0 + lax.erf(x * 0.7071067811865476))


def _dgelu(x):
    return 0.5 * (1.0 + lax.erf(x * 0.7071067811865476)) + x * jnp.exp(-0.5 * x * x) * 0.3989422804014327


def _acc_out(ref, first, val):
    @pl.when(first)
    def _():
        ref[...] = val

    @pl.when(jnp.logical_not(first))
    def _():
        ref[...] += val


def ffn_fwd(x, ng, wg, wu, wd, name):
    t, d = x.shape
    s4, fs, _ = wg.shape
    tm = _row_tile(t)

    def body(x_ref, ng_ref, wg_ref, wu_ref, wd_ref, xo_ref, g_ref, u_ref, h_scr, acc_scr):
        s = pl.program_id(1)

        @pl.when(s == 0)
        def _():
            y, _, _ = _rms(x_ref[...], ng_ref[...])
            h_scr[...] = y.astype(bf16)
            acc_scr[...] = jnp.zeros_like(acc_scr)

        h = h_scr[...]
        g = _dot_nt(h, wg_ref[0])
        u = _dot_nt(h, wu_ref[0])
        g_ref[0] = g.astype(bf16)
        u_ref[0] = u.astype(bf16)
        act = (g * _sigmoid(g) * u).astype(bf16)
        acc_scr[...] += _dot(act, wd_ref[0])

        @pl.when(s == s4 - 1)
        def _():
            xo_ref[...] = x_ref[...] + 0.5 * acc_scr[...]

    return pl.pallas_call(
        body, name=name, grid=(t // tm, s4),
        in_specs=[pl.BlockSpec((tm, d), lambda i, s: (i, 0)), pl.BlockSpec((1, d), lambda i, s: (0, 0)),
                  pl.BlockSpec((1, fs, d), lambda i, s: (s, 0, 0)), pl.BlockSpec((1, fs, d), lambda i, s: (s, 0, 0)),
                  pl.BlockSpec((1, fs, d), lambda i, s: (s, 0, 0))],
        out_specs=[pl.BlockSpec((tm, d), lambda i, s: (i, 0)), pl.BlockSpec((1, tm, fs), lambda i, s: (s, i, 0)),
                   pl.BlockSpec((1, tm, fs), lambda i, s: (s, i, 0))],
        out_shape=[jax.ShapeDtypeStruct((t, d), f32), jax.ShapeDtypeStruct((s4, t, fs), bf16),
                   jax.ShapeDtypeStruct((s4, t, fs), bf16)],
        scratch_shapes=[pltpu.VMEM((tm, d), bf16), pltpu.VMEM((tm, d), f32)],
        compiler_params=_cparams(),
    )(x, ng, wg, wu, wd)


def ffn_bwd_act(dxo, x, ng, g, u, wg, wu, wd, name, dep):
    t, d = x.shape
    s4, fs, _ = wg.shape
    tm = _row_tile(t)

    def body(dxo_ref, x_ref, ng_ref, g_ref, u_ref, wg_ref, wu_ref, wd_ref, dep_ref,
             dx_ref, dg_ref, du_ref, act_ref, hb_ref, dyb_ref, dng_ref, dy_scr, acc_scr):
        i = pl.program_id(0)
        s = pl.program_id(1)

        @pl.when(s == 0)
        def _():
            dyb = (0.5 * dxo_ref[...]).astype(bf16)
            dy_scr[...] = dyb
            dyb_ref[...] = dyb
            acc_scr[...] = jnp.zeros_like(acc_scr)

        dact = _dot_nt(dy_scr[...], wd_ref[0])
        gg = g_ref[0].astype(f32)
        uu = u_ref[0].astype(f32)
        sg = _sigmoid(gg)
        sil = gg * sg
        dgb = (dact * uu * _dsilu(gg, sg)).astype(bf16)
        dub = (dact * sil).astype(bf16)
        dg_ref[0] = dgb
        du_ref[0] = dub
        act_ref[0] = (sil * uu).astype(bf16)
        acc_scr[...] += _dot(dgb, wg_ref[0]) + _dot(dub, wu_ref[0])

        @pl.when(s == s4 - 1)
        def _():
            y, xh, r = _rms(x_ref[...], ng_ref[...])
            hb_ref[...] = y.astype(bf16)
            dh = acc_scr[...]
            dx_ref[...] = dxo_ref[...] + _rms_bwd(dh, xh, r, ng_ref[...])
            _acc_out(dng_ref, i == 0, jnp.sum(dh * xh, axis=0, keepdims=True))

    row = lambda i, s: (i, 0)
    shard = lambda i, s: (s, i, 0)
    wsp = lambda i, s: (s, 0, 0)
    return pl.pallas_call(
        body, name=name, grid=(t // tm, s4),
        in_specs=[pl.BlockSpec((tm, d), row), pl.BlockSpec((tm, d), row), pl.BlockSpec((1, d), lambda i, s: (0, 0)),
                  pl.BlockSpec((1, tm, fs), shard), pl.BlockSpec((1, tm, fs), shard),
                  pl.BlockSpec((1, fs, d), wsp), pl.BlockSpec((1, fs, d), wsp), pl.BlockSpec((1, fs, d), wsp), _ANY],
        out_specs=[pl.BlockSpec((tm, d), row), pl.BlockSpec((1, tm, fs), shard), pl.BlockSpec((1, tm, fs), shard),
                   pl.BlockSpec((1, tm, fs), shard), pl.BlockSpec((tm, d), row), pl.BlockSpec((tm, d), row),
                   pl.BlockSpec((1, d), lambda i, s: (0, 0))],
        out_shape=[jax.ShapeDtypeStruct((t, d), f32), jax.ShapeDtypeStruct((s4, t, fs), bf16),
                   jax.ShapeDtypeStruct((s4, t, fs), bf16), jax.ShapeDtypeStruct((s4, t, fs), bf16),
                   jax.ShapeDtypeStruct((t, d), bf16), jax.ShapeDtypeStruct((t, d), bf16),
                   jax.ShapeDtypeStruct((1, d), f32)],
        scratch_shapes=[pltpu.VMEM((tm, d), bf16), pltpu.VMEM((tm, d), f32)],
        compiler_params=_cparams(),
    )(dxo, x, ng, g, u, wg, wu, wd, dep)


def tn_matmul(xs, ys, x_spec, y_specs, n_shards, k1, k2s, t, tm, name):
    k2 = sum(k2s)
    ny = len(ys)

    def body(*refs):
        x_ref = refs[0]
        y_refs = refs[1:1 + ny]
        o_ref = refs[1 + ny]
        acc = refs[2 + ny]
        i = pl.program_id(1)
        xb = x_ref[0] if len(x_ref.shape) == 3 else x_ref[...]
        off = 0
        for y_ref, w in zip(y_refs, k2s):
            yb = y_ref[0] if len(y_ref.shape) == 3 else y_ref[...]
            part = _dot_tn(xb, yb)
            sl = (slice(None), slice(off, off + w))

            @pl.when(i == 0)
            def _(part=part, sl=sl):
                acc[sl] = part

            @pl.when(i > 0)
            def _(part=part, sl=sl):
                acc[sl] += part

            off += w

        @pl.when(i == t // tm - 1)
        def _():
            o_ref[0] = acc[...].astype(bf16)

    return pl.pallas_call(
        body, name=name, grid=(n_shards, t // tm),
        in_specs=[x_spec] + list(y_specs),
        out_specs=pl.BlockSpec((1, k1, k2), lambda s, i: (s, 0, 0)),
        out_shape=jax.ShapeDtypeStruct((n_shards, k1, k2), bf16),
        scratch_shapes=[pltpu.VMEM((k1, k2), f32)],
        compiler_params=_cparams(),
    )(xs, *ys)


def ffn_weight_grads(hb, dyb, dg, du, act, name):
    t, d = hb.shape
    s4, _, fs = dg.shape
    tm = t
    row = pl.BlockSpec((tm, d), lambda s, i: (i, 0))
    shard = pl.BlockSpec((1, tm, fs), lambda s, i: (s, i, 0))
    gwg = tn_matmul(dg, [hb], shard, [row], s4, fs, [d], t, tm, name + "_wg")
    gwu = tn_matmul(du, [hb], shard, [row], s4, fs, [d], t, tm, name + "_wu")
    gwd = tn_matmul(act, [dyb], shard, [row], s4, fs, [d], t, tm, name + "_wd")
    return gwg, gwu, gwd


def inproj_fwd(x1, ng, win, bin4):
    t, d = x1.shape
    s4, _, w2 = win.shape
    tm = _row_tile(t)

    def body(x_ref, ng_ref, w_ref, b_ref, p_ref, hb_ref, h_scr):
        s = pl.program_id(1)

        @pl.when(s == 0)
        def _():
            y, _, _ = _rms(x_ref[...], ng_ref[...])
            h_scr[...] = y.astype(bf16)
            hb_ref[...] = y.astype(bf16)

        p_ref[0] = (_dot(h_scr[...], w_ref[0]) + b_ref[0]).astype(bf16)

    return pl.pallas_call(
        body, name="inproj_fwd", grid=(t // tm, s4),
        in_specs=[pl.BlockSpec((tm, d), lambda i, s: (i, 0)), pl.BlockSpec((1, d), lambda i, s: (0, 0)),
                  pl.BlockSpec((1, d, w2), lambda i, s: (s, 0, 0)), pl.BlockSpec((1, 1, w2), lambda i, s: (s, 0, 0))],
        out_specs=[pl.BlockSpec((1, tm, w2), lambda i, s: (s, i, 0)), pl.BlockSpec((tm, d), lambda i, s: (i, 0))],
        out_shape=[jax.ShapeDtypeStruct((s4, t, w2), bf16), jax.ShapeDtypeStruct((t, d), bf16)],
        scratch_shapes=[pltpu.VMEM((tm, d), bf16)],
        compiler_params=_cparams(),
    )(x1, ng, win, bin4)


def _sgu_norm(va, ng, nb):
    gv = _gelu(va)
    mu = jnp.mean(gv, axis=-1, keepdims=True)
    xc = gv - mu
    rstd = lax.rsqrt(jnp.mean(xc * xc, axis=-1, keepdims=True) + NORM_EPS)
    xh = xc * rstd
    return xh, rstd, (xh * ng + nb).astype(bf16)


def sgu_fwd(proj, ng, nb, ws, bs):
    _, t, w2 = proj.shape
    d = w2 // 2
    gd = d // SGU_GROUPS
    tm = _row_tile(t)

    def body(p_ref, ng_ref, nb_ref, ws_ref, bs_ref, a_ref):
        ua = p_ref[0, :, 0:d].astype(f32)
        va = p_ref[0, :, d:w2].astype(f32)
        gu = _gelu(ua)
        _, _, vn = _sgu_norm(va, ng_ref[...], nb_ref[...])
        for c in range(tm // CHUNK):
            rows = slice(c * CHUNK, (c + 1) * CHUNK)
            for g in range(SGU_GROUPS):
                cols = slice(g * gd, (g + 1) * gd)
                sg = _dot(ws_ref[g], vn[rows, cols]) + bs_ref[g]
                a_ref[rows, cols] = (gu[rows, cols] * sg).astype(bf16)

    return pl.pallas_call(
        body, name="sgu_fwd", grid=(t // tm,),
        in_specs=[pl.BlockSpec((1, tm, w2), lambda i: (0, i, 0)), pl.BlockSpec((1, d), lambda i: (0, 0)),
                  pl.BlockSpec((1, d), lambda i: (0, 0)), pl.BlockSpec((SGU_GROUPS, CHUNK, CHUNK), lambda i: (0, 0, 0)),
                  pl.BlockSpec((SGU_GROUPS, CHUNK, 1), lambda i: (0, 0, 0))],
        out_specs=pl.BlockSpec((tm, d), lambda i: (i, 0)),
        out_shape=jax.ShapeDtypeStruct((t, d), bf16),
        compiler_params=_cparams(),
    )(proj, ng, nb, ws, bs)


def sgu_bwd(da, proj, ng, nb, ws, bs, dep):
    _, t, w2 = proj.shape
    d = w2 // 2
    gd = d // SGU_GROUPS
    tm = _row_tile(t)

    def body(da_ref, p_ref, ng_ref, nb_ref, ws_ref, bs_ref, dep_ref,
             dua_ref, dva_ref, dws_ref, dbs_ref, dng_ref, dnb_ref, dvn_scr):
        i = pl.program_id(0)
        ua = p_ref[0, :, 0:d].astype(f32)
        va = p_ref[0, :, d:w2].astype(f32)
        gu = _gelu(ua)
        xh, rstd, vn = _sgu_norm(va, ng_ref[...], nb_ref[...])
        dad = da_ref[...].astype(f32)
        dsb = (dad * gu).astype(bf16)
        for c in range(tm // CHUNK):
            rows = slice(c * CHUNK, (c + 1) * CHUNK)
            for g in range(SGU_GROUPS):
                cols = slice(g * gd, (g + 1) * gd)
                sg = _dot(ws_ref[g], vn[rows, cols]) + bs_ref[g]
                dua_ref[rows, cols] = (dad[rows, cols] * sg * _dgelu(ua[rows, cols])).astype(bf16)
                ds = dsb[rows, cols]
                dvn_scr[rows, cols] = _dot_tn(ws_ref[g], ds)
                dw = _dot_nt(ds, vn[rows, cols])
                db = jnp.sum(ds.astype(f32), axis=1, keepdims=True)
                if c == 0:
                    _acc_out(dws_ref.at[g], i == 0, dw)
                    _acc_out(dbs_ref.at[g], i == 0, db)
                else:
                    dws_ref[g] += dw
                    dbs_ref[g] += db
        dvn = dvn_scr[...]
        _acc_out(dng_ref, i == 0, jnp.sum(dvn * xh, axis=0, keepdims=True))
        _acc_out(dnb_ref, i == 0, jnp.sum(dvn, axis=0, keepdims=True))
        dxh = dvn * ng_ref[...]
        dgv = rstd * (dxh - jnp.mean(dxh, axis=-1, keepdims=True) - xh * jnp.mean(dxh * xh, axis=-1, keepdims=True))
        dva_ref[...] = (dgv * _dgelu(va)).astype(bf16)

    row = pl.BlockSpec((tm, d), lambda i: (i, 0))
    vec = pl.BlockSpec((1, d), lambda i: (0, 0))
    wsp = pl.BlockSpec((SGU_GROUPS, CHUNK, CHUNK), lambda i: (0, 0, 0))
    bsp = pl.BlockSpec((SGU_GROUPS, CHUNK, 1), lambda i: (0, 0, 0))
    return pl.pallas_call(
        body, name="sgu_bwd", grid=(t // tm,),
        in_specs=[row, pl.BlockSpec((1, tm, w2), lambda i: (0, i, 0)), vec, vec, wsp, bsp, _ANY],
        out_specs=[row, row, wsp, bsp, vec, vec],
        out_shape=[jax.ShapeDtypeStruct((t, d), bf16), jax.ShapeDtypeStruct((t, d), bf16),
                   jax.ShapeDtypeStruct((SGU_GROUPS, CHUNK, CHUNK), f32), jax.ShapeDtypeStruct((SGU_GROUPS, CHUNK, 1), f32),
                   jax.ShapeDtypeStruct((1, d), f32), jax.ShapeDtypeStruct((1, d), f32)],
        scratch_shapes=[pltpu.VMEM((tm, d), f32)],
        compiler_params=_cparams(),
    )(da, proj, ng, nb, ws, bs, dep)


def retention_constants(decay_logit, t, dk):
    lg = jax.nn.log_sigmoid(decay_logit.astype(f32))
    lgf = lg[0][:, None]
    lgb = lg[1][:, None]
    idx = jnp.arange(CHUNK, dtype=f32)[None, :]
    af = jnp.exp((idx + 1.0) * lgf)
    ab = jnp.exp((CHUNK - idx) * lgb)
    kf = jnp.exp((CHUNK - 1.0 - idx) * lgf)
    kb = jnp.exp(idx * lgb)
    cols = jnp.stack([af, ab, kf, kb, af * (idx + 1.0), ab * (CHUNK - idx), kf * (CHUNK - 1.0 - idx), kb * idx], axis=1)
    cols = cols[..., None]
    diff = idx[0][:, None] - idx[0][None, :]
    dfm = jnp.where(diff >= 0, jnp.exp(jnp.maximum(diff, 0.0)[None] * lgf[:, :, None]), 0.0)
    dbm = jnp.where(diff < 0, jnp.exp(jnp.maximum(-diff, 0.0)[None] * lgb[:, :, None]), 0.0)
    mats = jnp.stack([dfm + dbm, dfm * diff[None], dbm * (-diff)[None]], axis=1)
    cdec = jnp.stack([jnp.broadcast_to(jnp.exp(CHUNK * lgf), (RET_HEADS, dk)),
                      jnp.broadcast_to(jnp.exp(CHUNK * lgb), (RET_HEADS, dk))], axis=1)
    theta = ROPE_BASE ** (-jnp.arange(0, dk, 2, dtype=f32) / dk)
    ang = jnp.arange(t, dtype=f32)[:, None] * theta[None, :]
    return cols, mats, cdec, jnp.cos(ang), jnp.sin(ang)


def _rot(tr, cos, sin):
    half = tr.shape[-1] // 2
    t1 = tr[:, :half]
    t2 = tr[:, half:]
    return jnp.concatenate([t1 * cos - t2 * sin, t2 * cos + t1 * sin], axis=-1)


def _rot_inv(dt, cos, sin):
    half = dt.shape[-1] // 2
    d1 = dt[:, :half]
    d2 = dt[:, half:]
    return jnp.concatenate([d1 * cos + d2 * sin, d2 * cos - d1 * sin], axis=-1)


def _ret_specs(t, d, dk, rt):
    nr = t // rt
    hq = d // dk

    def blk(p, n):
        return (1 - p) * (nr - 1 - n) + p * n

    q_spec = pl.BlockSpec((1, rt, dk), lambda h, p, n: (1, blk(p, n), h))
    k_spec = pl.BlockSpec((1, rt, dk), lambda h, p, n: (1, blk(p, n), hq + h))
    v_spec = pl.BlockSpec((1, rt, dk), lambda h, p, n: (2, blk(p, n), h))
    g_spec = pl.BlockSpec((1, rt, dk), lambda h, p, n: (2, blk(p, n), hq + h))
    tab_spec = pl.BlockSpec((rt, dk // 2), lambda h, p, n: (blk(p, n), 0))
    cols_spec = pl.BlockSpec((1, 8, CHUNK, 1), lambda h, p, n: (h, 0, 0, 0))
    mats_spec = pl.BlockSpec((1, 3, CHUNK, CHUNK), lambda h, p, n: (h, 0, 0, 0))
    cdec_spec = pl.BlockSpec((1, 2, dk), lambda h, p, n: (h, 0, 0))
    in_row = pl.BlockSpec((rt, dk), lambda h, p, n: (blk(p, n), h))
    out_row = pl.BlockSpec((rt, dk), lambda h, p, n: (p * n, h))
    return nr, blk, q_spec, k_spec, v_spec, g_spec, tab_spec, cols_spec, mats_spec, cdec_spec, in_row, out_row


def ret_fwd(proj, cols, mats, cdec, cos, sin):
    _, t, w2 = proj.shape
    d = w2 // 2
    dk = d // RET_HEADS
    rt = _row_tile(t)
    cpt = rt // CHUNK
    nr, blk, q_spec, k_spec, v_spec, g_spec, tab_spec, cols_spec, mats_spec, cdec_spec, _, out_row = _ret_specs(t, d, dk, rt)
    scale = dk ** -0.5

    def body(q_ref, k_ref, v_ref, g_ref, cos_ref, sin_ref, cols_ref, mats_ref, cdec_ref, r_ref, rn_ref, sb_scr, st):
        p = pl.program_id(1)
        n = pl.program_id(2)
        af, ab, kf, kb = cols_ref[0, 0], cols_ref[0, 1], cols_ref[0, 2], cols_ref[0, 3]
        cf = cdec_ref[0, 0:1, :]
        cb = cdec_ref[0, 1:2, :]

        @pl.when(n == 0)
        def _():
            st[...] = jnp.zeros_like(st)

        @pl.when(p == 0)
        def _():
            for j in reversed(range(cpt)):
                rows = slice(j * CHUNK, (j + 1) * CHUNK)
                ch = blk(p, n) * cpt + j
                kk = _rot(k_ref[0, rows, :].astype(f32), cos_ref[rows, :], sin_ref[rows, :]) * scale
                sb_scr[ch] = st[...].astype(bf16)
                st[...] = st[...] * cb + _dot_tn((kk * kb).astype(bf16), v_ref[0, rows, :])

        @pl.when(p == 1)
        def _():
            for j in range(cpt):
                rows = slice(j * CHUNK, (j + 1) * CHUNK)
                ch = blk(p, n) * cpt + j
                cs, sn = cos_ref[rows, :], sin_ref[rows, :]
                q = _rot(q_ref[0, rows, :].astype(f32), cs, sn)
                kk = _rot(k_ref[0, rows, :].astype(f32), cs, sn) * scale
                v = v_ref[0, rows, :]
                pm = (_dot_nt(q.astype(bf16), kk.astype(bf16)) * mats_ref[0, 0]).astype(bf16)
                out = (_dot(pm, v) + _dot((q * af).astype(bf16), st[...].astype(bf16))
                       + _dot((q * ab).astype(bf16), sb_scr[ch]))
                st[...] = st[...] * cf + _dot_tn((kk * kf).astype(bf16), v)
                rhat = out * lax.rsqrt(jnp.mean(out * out, axis=-1, keepdims=True) + NORM_EPS)
                gg = g_ref[0, rows, :].astype(f32)
                r_ref[rows, :] = out.astype(bf16)
                rn_ref[rows, :] = (rhat * gg * _sigmoid(gg)).astype(bf16)

    return pl.pallas_call(
        body, name="ret_fwd", grid=(RET_HEADS, 2, nr),
        in_specs=[q_spec, k_spec, v_spec, g_spec, tab_spec, tab_spec, cols_spec, mats_spec, cdec_spec],
        out_specs=[out_row, out_row],
        out_shape=[jax.ShapeDtypeStruct((t, d), bf16), jax.ShapeDtypeStruct((t, d), bf16)],
        scratch_shapes=[pltpu.VMEM((t // CHUNK, dk, dk), bf16), pltpu.VMEM((dk, dk), f32)],
        compiler_params=_cparams(),
    )(proj, proj, proj, proj, cos, sin, cols, mats, cdec)


def ret_bwd(drn, r, proj, cols, mats, cdec, cos, sin):
    _, t, w2 = proj.shape
    d = w2 // 2
    dk = d // RET_HEADS
    rt = _row_tile(t)
    cpt = rt // CHUNK
    nr, blk, q_spec, k_spec, v_spec, g_spec, tab_spec, cols_spec, mats_spec, cdec_spec, in_row, out_row = _ret_specs(t, d, dk, rt)
    scale = dk ** -0.5

    def body(drn_ref, r_ref, q_ref, k_ref, v_ref, g_ref, cos_ref, sin_ref, cols_ref, mats_ref, cdec_ref,
             dq_ref, dk_ref, dv_ref, dg_ref, dlg_ref,
             sb_scr, gf_scr, st_s, st_g, acc_af, acc_ab, acc_vf, acc_vb, acc_sf, acc_sb):
        p = pl.program_id(1)
        n = pl.program_id(2)
        af, ab, kf, kb = cols_ref[0, 0], cols_ref[0, 1], cols_ref[0, 2], cols_ref[0, 3]
        af1, ab1, kf1, kb1 = cols_ref[0, 4], cols_ref[0, 5], cols_ref[0, 6], cols_ref[0, 7]
        cf = cdec_ref[0, 0:1, :]
        cb = cdec_ref[0, 1:2, :]

        @pl.when(n == 0)
        def _():
            st_s[...] = jnp.zeros_like(st_s)
            st_g[...] = jnp.zeros_like(st_g)

        @pl.when(jnp.logical_and(n == 0, p == 1))
        def _():
            for a in (acc_af, acc_ab, acc_vf, acc_vb, acc_sf, acc_sb):
                a[...] = jnp.zeros_like(a)

        def load(rows):
            cs, sn = cos_ref[rows, :], sin_ref[rows, :]
            q = _rot(q_ref[0, rows, :].astype(f32), cs, sn)
            kk = _rot(k_ref[0, rows, :].astype(f32), cs, sn) * scale
            rr = r_ref[rows, :].astype(f32)
            rstd = lax.rsqrt(jnp.mean(rr * rr, axis=-1, keepdims=True) + NORM_EPS)
            rhat = rr * rstd
            gg = g_ref[0, rows, :].astype(f32)
            sg = _sigmoid(gg)
            dd = drn_ref[rows, :].astype(f32)
            drhat = dd * gg * sg
            dout = rstd * (drhat - rhat * jnp.mean(drhat * rhat, axis=-1, keepdims=True))
            dgr = dd * rhat * _dsilu(gg, sg)
            return q, kk, dout.astype(bf16), dgr, cs, sn

        @pl.when(p == 0)
        def _():
            for j in reversed(range(cpt)):
                rows = slice(j * CHUNK, (j + 1) * CHUNK)
                ch = blk(p, n) * cpt + j
                q, kk, doutb, _, _, _ = load(rows)
                sb_scr[ch] = st_s[...].astype(bf16)
                gf_scr[ch] = st_g[...].astype(bf16)
                st_s[...] = st_s[...] * cb + _dot_tn((kk * kb).astype(bf16), v_ref[0, rows, :])
                st_g[...] = st_g[...] * cf + _dot_tn((q * af).astype(bf16), doutb)

        @pl.when(p == 1)
        def _():
            for j in range(cpt):
                rows = slice(j * CHUNK, (j + 1) * CHUNK)
                ch = blk(p, n) * cpt + j
                q, kk, doutb, dgr, cs, sn = load(rows)
                v = v_ref[0, rows, :]
                qb = q.astype(bf16)
                kkb = kk.astype(bf16)
                sf = st_s[...]
                gb = st_g[...]
                sfb = sf.astype(bf16)
                gbb = gb.astype(bf16)
                sbb = sb_scr[ch]
                gfb = gf_scr[ch]
                dmat = mats_ref[0, 0]
                scores = _dot_nt(qb, kkb)
                dpraw = _dot_nt(doutb, v)
                dpb = (dpraw * dmat).astype(bf16)
                pmb = (scores * dmat).astype(bf16)
                x1 = _dot_nt(doutb, sfb)
                x2 = _dot_nt(doutb, sbb)
                y1 = _dot_nt(v, gfb)
                y2 = _dot_nt(v, gbb)
                kdf = (kk * kf).astype(bf16)
                kdb = (kk * kb).astype(bf16)
                dq = _dot(dpb, kkb) + x1 * af + x2 * ab
                dkk = _dot_tn(dpb, qb) + y1 * kf + y2 * kb
                dv = _dot_tn(pmb, doutb) + _dot(kdf, gfb) + _dot(kdb, gbb)
                ps = dpraw * scores
                acc_af[...] += ps * mats_ref[0, 1]
                acc_ab[...] += ps * mats_ref[0, 2]
                acc_vf[...] += x1 * q * af1 + y1 * kk * kf1
                acc_vb[...] += x2 * q * ab1 + y2 * kk * kb1
                acc_sf[...] += gfb.astype(f32) * sf
                acc_sb[...] += gb * sbb.astype(f32)
                st_s[...] = sf * cf + _dot_tn(kdf, v)
                st_g[...] = gb * cb + _dot_tn((q * ab).astype(bf16), doutb)
                dq_ref[rows, :] = _rot_inv(dq, cs, sn).astype(bf16)
                dk_ref[rows, :] = (_rot_inv(dkk, cs, sn) * scale).astype(bf16)
                dv_ref[rows, :] = dv.astype(bf16)
                dg_ref[rows, :] = dgr.astype(bf16)

        @pl.when(jnp.logical_and(p == 1, n == nr - 1))
        def _():
            tf = jnp.sum(acc_af[...]) + jnp.sum(acc_vf[...]) + CHUNK * jnp.sum(acc_sf[...] * cf)
            tb = jnp.sum(acc_ab[...]) + jnp.sum(acc_vb[...]) + CHUNK * jnp.sum(acc_sb[...] * cb)
            rid = lax.broadcasted_iota(jnp.int32, (8, 128), 0)
            dlg_ref[0] = jnp.where(rid == 0, tf, jnp.where(rid == 1, tb, 0.0))

    nch = t // CHUNK
    return pl.pallas_call(
        body, name="ret_bwd", grid=(RET_HEADS, 2, nr),
        in_specs=[in_row, in_row, q_spec, k_spec, v_spec, g_spec, tab_spec, tab_spec, cols_spec, mats_spec, cdec_spec],
        out_specs=[out_row, out_row, out_row, out_row, pl.BlockSpec((1, 8, 128), lambda h, p, n: (h, 0, 0))],
        out_shape=[jax.ShapeDtypeStruct((t, d), bf16)] * 4 + [jax.ShapeDtypeStruct((RET_HEADS, 8, 128), f32)],
        scratch_shapes=[pltpu.VMEM((nch, dk, dk), bf16), pltpu.VMEM((nch, dk, dk), bf16),
                        pltpu.VMEM((dk, dk), f32), pltpu.VMEM((dk, dk), f32),
                        pltpu.VMEM((CHUNK, CHUNK), f32), pltpu.VMEM((CHUNK, CHUNK), f32),
                        pltpu.VMEM((CHUNK, dk), f32), pltpu.VMEM((CHUNK, dk), f32),
                        pltpu.VMEM((dk, dk), f32), pltpu.VMEM((dk, dk), f32)],
        compiler_params=_cparams(),
    )(drn, r, proj, proj, proj, proj, cos, sin, cols, mats, cdec)


def mix_fwd(a, rn, proj, wa, wb, wo, x1):
    t, d = x1.shape
    tm = _row_tile(t)

    def body(a_ref, rn_ref, p_ref, wa_ref, wb_ref, wo_ref, x_ref, xo_ref, ba_ref, br_ref):
        ba = _dot(a_ref[...], wa_ref[...])
        br = _dot(rn_ref[...], wb_ref[...])
        sa = _sigmoid(p_ref[0, :, 0:d].astype(f32))
        sb = _sigmoid(p_ref[0, :, d:2 * d].astype(f32))
        mix = (sa * ba + sb * br).astype(bf16)
        xo_ref[...] = x_ref[...] + _dot(mix, wo_ref[...])
        ba_ref[...] = ba.astype(bf16)
        br_ref[...] = br.astype(bf16)

    row = pl.BlockSpec((tm, d), lambda i: (i, 0))
    wsp = pl.BlockSpec((d, d), lambda i: (0, 0))
    return pl.pallas_call(
        body, name="mix_fwd", grid=(t // tm,),
        in_specs=[row, row, pl.BlockSpec((1, tm, 2 * d), lambda i: (3, i, 0)), wsp, wsp, wsp, row],
        out_specs=[row, row, row],
        out_shape=[jax.ShapeDtypeStruct((t, d), f32), jax.ShapeDtypeStruct((t, d), bf16), jax.ShapeDtypeStruct((t, d), bf16)],
        compiler_params=_cparams(),
    )(a, rn, proj, wa, wb, wo, x1)


def mix_bwd_act(dx2, ba, br, proj, wa, wb, wo, dep):
    t, d = dx2.shape
    tm = _row_tile(t)

    def body(dx_ref, ba_ref, br_ref, p_ref, wa_ref, wb_ref, wo_ref, dep_ref,
             da_ref, drn_ref, dga_ref, dgb_ref, mix_ref, dba_ref, dbr_ref, dxb_ref):
        dxb = dx_ref[...].astype(bf16)
        dxb_ref[...] = dxb
        dmix = _dot_nt(dxb, wo_ref[...])
        ba = ba_ref[...].astype(f32)
        br = br_ref[...].astype(f32)
        sa = _sigmoid(p_ref[0, :, 0:d].astype(f32))
        sb = _sigmoid(p_ref[0, :, d:2 * d].astype(f32))
        mix_ref[...] = (sa * ba + sb * br).astype(bf16)
        dba = (dmix * sa).astype(bf16)
        dbr = (dmix * sb).astype(bf16)
        dba_ref[...] = dba
        dbr_ref[...] = dbr
        dga_ref[...] = (dmix * ba * sa * (1.0 - sa)).astype(bf16)
        dgb_ref[...] = (dmix * br * sb * (1.0 - sb)).astype(bf16)
        da_ref[...] = _dot_nt(dba, wa_ref[...]).astype(bf16)
        drn_ref[...] = _dot_nt(dbr, wb_ref[...]).astype(bf16)

    row = pl.BlockSpec((tm, d), lambda i: (i, 0))
    wsp = pl.BlockSpec((d, d), lambda i: (0, 0))
    return pl.pallas_call(
        body, name="mix_bwd_act", grid=(t // tm,),
        in_specs=[row, row, row, pl.BlockSpec((1, tm, 2 * d), lambda i: (3, i, 0)), wsp, wsp, wsp, _ANY],
        out_specs=[row] * 8,
        out_shape=[jax.ShapeDtypeStruct((t, d), bf16)] * 8,
        compiler_params=_cparams(),
    )(dx2, ba, br, proj, wa, wb, wo, dep)


def inproj_bwd_act(segs, win, x1, ng, dx2):
    t, d = x1.shape
    s4 = win.shape[0]
    tm = _row_tile(t) // 2
    nseg = len(segs)

    def body(*refs):
        seg_refs = refs[:nseg]
        w_ref, x_ref, ng_ref, dx2_ref, dx1_ref, db_ref, dng_ref = refs[nseg:]
        i = pl.program_id(0)
        dh = None
        for e, sr in enumerate(seg_refs):
            sb = sr[...]
            part = _dot_nt(sb, w_ref[e // 2, :, (e % 2) * d:(e % 2 + 1) * d])
            dh = part if dh is None else dh + part
            _acc_out(db_ref.at[e], i == 0, jnp.sum(sb.astype(f32), axis=0, keepdims=True))
        _, xh, r = _rms(x_ref[...], ng_ref[...])
        dx1_ref[...] = dx2_ref[...] + _rms_bwd(dh, xh, r, ng_ref[...])
        _acc_out(dng_ref, i == 0, jnp.sum(dh * xh, axis=0, keepdims=True))

    row = pl.BlockSpec((tm, d), lambda i: (i, 0))
    vec = pl.BlockSpec((1, d), lambda i: (0, 0))
    return pl.pallas_call(
        body, name="inproj_bwd_act", grid=(t // tm,),
        in_specs=[row] * nseg + [pl.BlockSpec((s4, d, 2 * d), lambda i: (0, 0, 0), pipeline_mode=pl.Buffered(1)),
                                 row, vec, row],
        out_specs=[row, pl.BlockSpec((nseg, 1, d), lambda i: (0, 0, 0)), vec],
        out_shape=[jax.ShapeDtypeStruct((t, d), f32), jax.ShapeDtypeStruct((nseg, 1, d), f32),
                   jax.ShapeDtypeStruct((1, d), f32)],
        compiler_params=_cparams(),
    )(*segs, win, x1, ng, dx2)


def loss_head(x3, fng, tgt):
    t, d = x3.shape
    tm = _row_tile(t)

    def body(x_ref, g_ref, t_ref, loss_ref, dx_ref, dg_ref):
        i = pl.program_id(0)
        y, xh, r = _rms(x_ref[...], g_ref[...])
        diff = y - t_ref[...]
        part = 0.5 * jnp.sum(jnp.sum(diff * diff, axis=0, keepdims=True), axis=1, keepdims=True) / d
        _acc_out(loss_ref, i == 0, jnp.broadcast_to(part, (1, 128)))
        dy = diff * (1.0 / d)
        dx_ref[...] = _rms_bwd(dy, xh, r, g_ref[...])
        _acc_out(dg_ref, i == 0, jnp.sum(dy * xh, axis=0, keepdims=True))

    row = pl.BlockSpec((tm, d), lambda i: (i, 0))
    vec = pl.BlockSpec((1, d), lambda i: (0, 0))
    return pl.pallas_call(
        body, name="loss_head", grid=(t // tm,),
        in_specs=[row, vec, row],
        out_specs=[pl.BlockSpec((1, 128), lambda i: (0, 0)), row, vec],
        out_shape=[jax.ShapeDtypeStruct((1, 128), f32), jax.ShapeDtypeStruct((t, d), f32), jax.ShapeDtypeStruct((1, d), f32)],
        compiler_params=_cparams(),
    )(x3, fng, tgt)


def _place():
    return lax.axis_index("x"), lax.axis_index("y"), lax.axis_index("c")


def _other_chips(x, y):
    return [(1 - x, y), (x, 1 - y), (1 - x, 1 - y)]


_ANY = pl.BlockSpec(memory_space=pl.ANY)


_HBM = pl.BlockSpec(memory_space=pltpu.HBM)
_SEM = pl.BlockSpec(memory_space=pltpu.SEMAPHORE)
_EFFECT = pltpu.SideEffectType.DATAFLOW_SIDE_EFFECTING


def _hbm(a):
    return pltpu.with_memory_space_constraint(a, pltpu.HBM)


def _half_rows(ref, c):
    half = ref.shape[1] // 2
    return pl.ds(pl.multiple_of(c * half, 16), half)


def _chip_copy(src, dst, send_sem, recv_sem, chip, c):
    return pltpu.make_async_remote_copy(src_ref=src, dst_ref=dst, send_sem=send_sem, recv_sem=recv_sem,
                                        device_id=(chip[0], chip[1], c), device_id_type=MESH)


def gather_start(bufs, groups):
    nb, ng = len(bufs), len(groups)

    def body(*refs):
        ins = refs[:nb]
        sems = refs[nb:nb + 2 * ng]
        token = refs[-1]
        x, y, c = _place()
        k = 2 * x + y
        for gi, grp in enumerate(groups):
            for wi, w in enumerate(grp):
                mine = ins[w].at[k, _half_rows(ins[w], c)]
                for j, chip in enumerate(_other_chips(x, y)):
                    _chip_copy(mine, mine, sems[2 * gi].at[3 * wi + j], sems[2 * gi + 1].at[3 * wi + j], chip, c).start()
        token[...] = jnp.zeros_like(token)

    sem_shapes = []
    for grp in groups:
        sem_shapes += [pltpu.SemaphoreType.DMA((3 * len(grp),)), pltpu.SemaphoreType.DMA((3 * len(grp),))]
    outs = pl.pallas_call(
        body, name="gather_start",
        out_shape=sem_shapes + [pltpu.HBM(b.shape, b.dtype) for b in bufs] + [jax.ShapeDtypeStruct((8, 128), f32)],
        in_specs=[_HBM] * nb,
        out_specs=[_SEM] * (2 * ng) + [_HBM] * nb + [pl.BlockSpec(memory_space=pltpu.VMEM)],
        input_output_aliases={w: 2 * ng + w for w in range(nb)},
        compiler_params=pltpu.CompilerParams(has_side_effects=_EFFECT),
    )(*[_hbm(b) for b in bufs])
    sems = [(outs[2 * gi], outs[2 * gi + 1]) for gi in range(ng)]
    return sems, list(outs[2 * ng:2 * ng + nb]), outs[-1]


def gather_wait(bufs, sems, after, name):
    n = len(bufs)

    def body(*refs):
        ins = refs[:n]
        send_sems, recv_sems = refs[n], refs[n + 1]
        x, y, c = _place()
        k = 2 * x + y
        for wi in range(n):
            half = _half_rows(ins[wi], c)
            for j, chip in enumerate(_other_chips(x, y)):
                cp = _chip_copy(ins[wi].at[k, half], ins[wi].at[2 * chip[0] + chip[1], half], send_sems.at[3 * wi + j],
                                recv_sems.at[3 * wi + j], chip, c)
                cp.wait_send()
                cp.wait_recv()

    outs = pl.pallas_call(
        body, name=name,
        out_shape=[pltpu.HBM(b.shape, b.dtype) for b in bufs],
        in_specs=[_HBM] * n + [_SEM, _SEM, _ANY],
        out_specs=[_HBM] * n,
        input_output_aliases={i: i for i in range(n)},
        compiler_params=pltpu.CompilerParams(has_side_effects=_EFFECT),
    )(*bufs, sems[0], sems[1], after)
    return list(outs)


def gather_forward(bufs, name):
    n = len(bufs)

    def body(*refs):
        ins = refs[n:2 * n]
        send_sems, recv_sems = refs[2 * n], refs[2 * n + 1]
        x, y, c = _place()
        copies = []
        for wi in range(n):
            for j, chip in enumerate(_other_chips(x, y)):
                kp = 2 * chip[0] + chip[1]
                got = ins[wi].at[kp, _half_rows(ins[wi], c)]
                cp = pltpu.make_async_remote_copy(
                    src_ref=got, dst_ref=got, send_sem=send_sems.at[3 * wi + j], recv_sem=recv_sems.at[3 * wi + j],
                    device_id=(x, y, 1 - c), device_id_type=MESH)
                cp.start()
                copies.append((cp, wi, kp, j))
        for cp, wi, kp, j in copies:
            cp.wait_send()
            theirs = ins[wi].at[kp, _half_rows(ins[wi], 1 - c)]
            pltpu.make_async_remote_copy(
                src_ref=theirs, dst_ref=theirs, send_sem=send_sems.at[3 * wi + j], recv_sem=recv_sems.at[3 * wi + j],
                device_id=(x, y, 1 - c), device_id_type=MESH).wait_recv()

    outs = pl.pallas_call(
        body, name=name,
        out_shape=[jax.ShapeDtypeStruct(b.shape, b.dtype) for b in bufs],
        in_specs=[_ANY] * n, out_specs=[_ANY] * n,
        input_output_aliases={i: i for i in range(n)},
        scratch_shapes=[pltpu.SemaphoreType.DMA((3 * n,)), pltpu.SemaphoreType.DMA((3 * n,))],
    )(*bufs)
    return list(outs)


def exchange_start(grads, name):
    n = len(grads)
    lands = [lax.empty((3,) + g.shape[1:], g.dtype) for g in grads]

    def body(*refs):
        ins = refs[:n]
        land = refs[n:2 * n]
        send_sems, recv_sems = refs[2 * n], refs[2 * n + 1]
        token = refs[-1]
        x, y, c = _place()
        for wi in range(n):
            for j, chip in enumerate(_other_chips(x, y)):
                _chip_copy(ins[wi].at[2 * chip[0] + chip[1]], land[wi].at[j], send_sems.at[3 * wi + j],
                           recv_sems.at[3 * wi + j], chip, c).start()
        token[...] = jnp.zeros_like(token)

    outs = pl.pallas_call(
        body, name=name,
        out_shape=[pltpu.SemaphoreType.DMA((3 * n,)), pltpu.SemaphoreType.DMA((3 * n,))]
        + [pltpu.HBM(g.shape, g.dtype) for g in grads] + [pltpu.HBM(l.shape, l.dtype) for l in lands]
        + [jax.ShapeDtypeStruct((8, 128), f32)],
        in_specs=[_HBM] * (2 * n),
        out_specs=[_SEM, _SEM] + [_HBM] * (2 * n) + [pl.BlockSpec(memory_space=pltpu.VMEM)],
        input_output_aliases={i: 2 + i for i in range(2 * n)},
        compiler_params=pltpu.CompilerParams(has_side_effects=_EFFECT),
    )(*[_hbm(g) for g in grads], *[_hbm(l) for l in lands])
    return (outs[0], outs[1]), list(outs[2:2 + n]), list(outs[2 + n:2 + 2 * n]), outs[-1]


def exchange_wait(grads, lands, sems, after, name):
    n = len(grads)

    def body(*refs):
        ins = refs[:n]
        land = refs[n:2 * n]
        send_sems, recv_sems = refs[2 * n], refs[2 * n + 1]
        x, y, c = _place()
        for wi in range(n):
            for j, chip in enumerate(_other_chips(x, y)):
                cp = _chip_copy(ins[wi].at[2 * chip[0] + chip[1]], land[wi].at[j], send_sems.at[3 * wi + j],
                                recv_sems.at[3 * wi + j], chip, c)
                cp.wait_send()
                cp.wait_recv()

    outs = pl.pallas_call(
        body, name=name,
        out_shape=[pltpu.HBM(g.shape, g.dtype) for g in grads] + [pltpu.HBM(l.shape, l.dtype) for l in lands],
        in_specs=[_HBM] * (2 * n) + [_SEM, _SEM, _ANY],
        out_specs=[_HBM] * (2 * n),
        input_output_aliases={i: i for i in range(2 * n)},
        compiler_params=pltpu.CompilerParams(has_side_effects=_EFFECT),
    )(*grads, *lands, sems[0], sems[1], after)
    return list(outs[:n]), list(outs[n:])


def swap_with_sibling(parts, name):
    nw = len(parts)

    def body(*refs):
        ins = refs[:nw]
        outs = refs[nw:2 * nw]
        send_sems, recv_sems = refs[2 * nw:]
        x, y, c = _place()
        copies = [pltpu.make_async_remote_copy(
            src_ref=ins[w], dst_ref=outs[w], send_sem=send_sems.at[w], recv_sem=recv_sems.at[w],
            device_id=(x, y, 1 - c), device_id_type=MESH) for w in range(nw)]
        for cp in copies:
            cp.start()
        for cp in copies:
            cp.wait()

    return pl.pallas_call(
        body, name=name,
        in_specs=[_ANY] * nw, out_specs=[_ANY] * nw,
        out_shape=[jax.ShapeDtypeStruct(p.shape, p.dtype) for p in parts],
        scratch_shapes=[pltpu.SemaphoreType.DMA((nw,)), pltpu.SemaphoreType.DMA((nw,))],
    )(*parts)


def gather_small(block):
    r, lanes = block.shape

    def body(b_ref, o_ref, send_sems, recv_sems):
        x, y, c = _place()
        me = 4 * x + 2 * y + c
        o_ref[me] = b_ref[...]
        peers = []
        for m in range(1, N_DEV):
            px = 1 - x if m & 4 else x
            py = 1 - y if m & 2 else y
            pc = 1 - c if m & 1 else c
            peers.append((px, py, pc))
        for m, peer in enumerate(peers):
            pltpu.make_async_remote_copy(
                src_ref=b_ref, dst_ref=o_ref.at[me], send_sem=send_sems.at[m], recv_sem=recv_sems.at[m],
                device_id=peer, device_id_type=MESH).start()
        for m, (px, py, pc) in enumerate(peers):
            pltpu.make_async_remote_copy(
                src_ref=b_ref, dst_ref=o_ref.at[4 * px + 2 * py + pc], send_sem=send_sems.at[m],
                recv_sem=recv_sems.at[m], device_id=(px, py, pc), device_id_type=MESH).wait()

    return pl.pallas_call(
        body, name="gather_small",
        in_specs=[pl.BlockSpec(memory_space=pltpu.VMEM)], out_specs=pl.BlockSpec(memory_space=pltpu.VMEM),
        out_shape=jax.ShapeDtypeStruct((N_DEV, r, lanes), block.dtype),
        scratch_shapes=[pltpu.SemaphoreType.DMA((N_DEV - 1,)), pltpu.SemaphoreType.DMA((N_DEV - 1,))],
    )(block)


def _adamw(w, g, m, v):
    m = ADAM_B1 * m + (1.0 - ADAM_B1) * g
    v = ADAM_B2 * v + (1.0 - ADAM_B2) * (g * g)
    m_hat = m / (1.0 - ADAM_B1 ** ADAM_STEP)
    v_hat = v / (1.0 - ADAM_B2 ** ADAM_STEP)
    delta = -ADAM_LR * (m_hat / (jnp.sqrt(v_hat) + ADAM_EPS) + ADAM_WD * w)
    return delta, m, v


def _ew_tile(rows):
    for cand in (256, 176, 128, 64, 32, 16, 8):
        if rows % cand == 0:
            return cand
    return rows


def sum_partials(chip, own, land, name):
    _, r, c = own.shape
    tr = _ew_tile(r)

    def body(k_ref, own_ref, p_ref, o_ref):
        o_ref[...] = ((own_ref[0].astype(f32) + p_ref[0].astype(f32)) + p_ref[1].astype(f32)) + p_ref[2].astype(f32)

    return pl.pallas_call(
        body, name=name,
        grid_spec=pltpu.PrefetchScalarGridSpec(
            num_scalar_prefetch=1, grid=(r // tr,),
            in_specs=[pl.BlockSpec((1, tr, c), lambda i, k: (k[0], i, 0)), pl.BlockSpec((3, tr, c), lambda i, k: (0, i, 0))],
            out_specs=pl.BlockSpec((tr, c), lambda i, k: (i, 0))),
        out_shape=jax.ShapeDtypeStruct((r, c), f32),
        compiler_params=_cparams(),
    )(chip, own, land)


def adamw_shard(p_mine, p_sibling, w, m, v, name):
    r, c = w.shape
    tr = _ew_tile(r)

    def body(a_ref, b_ref, w_ref, m_ref, v_ref, g_ref, d_ref, mo_ref, vo_ref):
        g = a_ref[...] + b_ref[...]
        delta, mn, vn = _adamw(w_ref[...], g, m_ref[...], v_ref[...])
        g_ref[...] = g
        d_ref[...] = delta
        mo_ref[...] = mn
        vo_ref[...] = vn

    blk = pl.BlockSpec((tr, c), lambda i: (i, 0))
    return pl.pallas_call(
        body, name=name, grid=(r // tr,),
        in_specs=[blk] * 5, out_specs=[blk] * 4,
        out_shape=[jax.ShapeDtypeStruct((r, c), f32)] * 4,
        compiler_params=_cparams(),
    )(p_mine, p_sibling, w, m, v)


def adamw_small(g8, w, m, v):
    _, r, lanes = g8.shape

    def body(g_ref, w_ref, m_ref, v_ref, go_ref, d_ref, mo_ref, vo_ref):
        g = g_ref[0]
        for i in range(1, N_DEV):
            g = g + g_ref[i]
        delta, mn, vn = _adamw(w_ref[...], g, m_ref[...], v_ref[...])
        go_ref[...] = g
        d_ref[...] = delta
        mo_ref[...] = mn
        vo_ref[...] = vn

    return pl.pallas_call(
        body, name="adamw_small",
        out_shape=[jax.ShapeDtypeStruct((r, lanes), f32)] * 4,
        compiler_params=_cparams(),
    )(g8, w, m, v)


def _size(shape):
    n = 1
    for e in shape:
        n *= e
    return n


def _pack_rows(shapes):
    rows = [-(-_size(s) // 1024) * 8 for s in shapes]
    return rows, sum(rows)


def _pack(arrs, shapes):
    rows, _ = _pack_rows(shapes)
    parts = [jnp.pad(a.reshape(-1).astype(f32), (0, r * 128 - _size(s))).reshape(r, 128)
             for a, s, r in zip(arrs, shapes, rows)]
    return jnp.concatenate(parts, axis=0)


def _unpack(block, shapes):
    rows, _ = _pack_rows(shapes)
    out, off = [], 0
    for s, r in zip(shapes, rows):
        out.append(block[off:off + r].reshape(-1)[:_size(s)].reshape(s))
        off += r
    return out


TRANSPOSED = ("ffn1_w_gate", "ffn1_w_up", "ffn2_w_gate", "ffn2_w_up")


def _shard2d(a, n):
    return a[0].T if n in TRANSPOSED else a[0]


def _unshard(a, n):
    return (a.T if n in TRANSPOSED else a)[None]


BIG = ("ffn1_w_gate", "ffn1_w_up", "ffn1_w_down", "w_in", "w_branch_a", "w_branch_b", "w_out",
       "ffn2_w_gate", "ffn2_w_up", "ffn2_w_down")
SMALL = ("ffn1_norm", "mix_norm", "b_in", "sgu_norm_g", "sgu_norm_b", "sgu_w_s", "sgu_b_s", "ret_decay_logit",
         "ffn2_norm", "final_norm")
WEIGHTS = ("ffn1_norm", "ffn1_w_gate", "ffn1_w_up", "ffn1_w_down", "mix_norm", "w_in", "b_in", "sgu_norm_g",
           "sgu_norm_b", "sgu_w_s", "sgu_b_s", "ret_decay_logit", "w_branch_a", "w_branch_b", "w_out", "ffn2_norm",
           "ffn2_w_gate", "ffn2_w_up", "ffn2_w_down", "final_norm")


def kernel(x, ffn1_norm, ffn1_w_gate, ffn1_w_up, ffn1_w_down, mix_norm, w_in, b_in, sgu_norm_g, sgu_norm_b, sgu_w_s, sgu_b_s, ret_decay_logit, w_branch_a, w_branch_b, w_out, ffn2_norm, ffn2_w_gate, ffn2_w_up, ffn2_w_down, final_norm, loss_target, m_ffn1_norm, m_ffn1_w_gate, m_ffn1_w_up, m_ffn1_w_down, m_mix_norm, m_w_in, m_b_in, m_sgu_norm_g, m_sgu_norm_b, m_sgu_w_s, m_sgu_b_s, m_ret_decay_logit, m_w_branch_a, m_w_branch_b, m_w_out, m_ffn2_norm, m_ffn2_w_gate, m_ffn2_w_up, m_ffn2_w_down, m_final_norm, v_ffn1_norm, v_ffn1_w_gate, v_ffn1_w_up, v_ffn1_w_down, v_mix_norm, v_w_in, v_b_in, v_sgu_norm_g, v_sgu_norm_b, v_sgu_w_s, v_sgu_b_s, v_ret_decay_logit, v_w_branch_a, v_w_branch_b, v_w_out, v_ffn2_norm, v_ffn2_w_gate, v_ffn2_w_up, v_ffn2_w_down, v_final_norm):
    p = dict(ffn1_norm=ffn1_norm, ffn1_w_gate=ffn1_w_gate, ffn1_w_up=ffn1_w_up, ffn1_w_down=ffn1_w_down,
             mix_norm=mix_norm, w_in=w_in, b_in=b_in, sgu_norm_g=sgu_norm_g, sgu_norm_b=sgu_norm_b, sgu_w_s=sgu_w_s,
             sgu_b_s=sgu_b_s, ret_decay_logit=ret_decay_logit, w_branch_a=w_branch_a, w_branch_b=w_branch_b,
             w_out=w_out, ffn2_norm=ffn2_norm, ffn2_w_gate=ffn2_w_gate, ffn2_w_up=ffn2_w_up, ffn2_w_down=ffn2_w_down,
             final_norm=final_norm)
    mom = dict(ffn1_norm=m_ffn1_norm, ffn1_w_gate=m_ffn1_w_gate, ffn1_w_up=m_ffn1_w_up, ffn1_w_down=m_ffn1_w_down,
               mix_norm=m_mix_norm, w_in=m_w_in, b_in=m_b_in, sgu_norm_g=m_sgu_norm_g, sgu_norm_b=m_sgu_norm_b,
               sgu_w_s=m_sgu_w_s, sgu_b_s=m_sgu_b_s, ret_decay_logit=m_ret_decay_logit, w_branch_a=m_w_branch_a,
               w_branch_b=m_w_branch_b, w_out=m_w_out, ffn2_norm=m_ffn2_norm, ffn2_w_gate=m_ffn2_w_gate,
               ffn2_w_up=m_ffn2_w_up, ffn2_w_down=m_ffn2_w_down, final_norm=m_final_norm)
    var = dict(ffn1_norm=v_ffn1_norm, ffn1_w_gate=v_ffn1_w_gate, ffn1_w_up=v_ffn1_w_up, ffn1_w_down=v_ffn1_w_down,
               mix_norm=v_mix_norm, w_in=v_w_in, b_in=v_b_in, sgu_norm_g=v_sgu_norm_g, sgu_norm_b=v_sgu_norm_b,
               sgu_w_s=v_sgu_w_s, sgu_b_s=v_sgu_b_s, ret_decay_logit=v_ret_decay_logit, w_branch_a=v_w_branch_a,
               w_branch_b=v_w_branch_b, w_out=v_w_out, ffn2_norm=v_ffn2_norm, ffn2_w_gate=v_ffn2_w_gate,
               ffn2_w_up=v_ffn2_w_up, ffn2_w_down=v_ffn2_w_down, final_norm=v_final_norm)

    xs = x[0]
    tgt = loss_target[0]
    t, d = xs.shape
    dk = d // RET_HEADS
    tm = _row_tile(t)

    shards2d = {n: _shard2d(p[n], n) for n in BIG}
    chip = (2 * lax.axis_index("x") + lax.axis_index("y")).astype(jnp.int32).reshape(1)
    groups = {"ffn1": ("ffn1_w_gate", "ffn1_w_up", "ffn1_w_down"), "in": ("w_in",),
              "mix": ("w_branch_a", "w_branch_b", "w_out"), "ffn2": ("ffn2_w_gate", "ffn2_w_up", "ffn2_w_down")}
    bufs = [jnp.broadcast_to(shards2d[n].astype(bf16)[None], (N_CHIPS,) + shards2d[n].shape) for n in BIG]
    sems, bufs, tok = gather_start(bufs, [[BIG.index(n) for n in groups[g]] for g in ("ffn1", "in", "mix", "ffn2")])
    gsem = dict(zip(("ffn1", "in", "mix", "ffn2"), sems))
    pending = dict(zip(BIG, bufs))

    def arrive(gs, after):
        got = []
        for g in gs:
            got += gather_wait([pending[n] for n in groups[g]], gsem[g], after, "gather_wait_" + g)
        return gather_forward(got, "gather_forward_" + gs[0])

    bin4 = b_in.reshape(N_CHIPS, 1, 2 * d)
    ws_b = sgu_w_s[0].astype(bf16)
    bs_c = sgu_b_s[0][:, :, None]
    cols, mats, cdec, cos, sin = retention_constants(ret_decay_logit[0], t, dk)

    wg1, wu1, wd1 = arrive(["ffn1"], tok)
    x1, g1, u1 = ffn_fwd(xs, ffn1_norm, wg1, wu1, wd1, "ffn1_fwd")
    win, = arrive(["in"], x1)
    proj, hb2 = inproj_fwd(x1, mix_norm, win, bin4)
    a = sgu_fwd(proj, sgu_norm_g, sgu_norm_b, ws_b, bs_c)
    r, rn = ret_fwd(proj, cols, mats, cdec, cos, sin)
    wa, wb, wo, wg2, wu2, wd2 = arrive(["mix", "ffn2"], rn)
    wa, wb, wo = [w.reshape(d, d) for w in (wa, wb, wo)]
    x2, ba, br = mix_fwd(a, rn, proj, wa, wb, wo, x1)
    x3, g2, u2 = ffn_fwd(x2, ffn2_norm, wg2, wu2, wd2, "ffn2_fwd")
    loss_blk, dx3, d_final = loss_head(x3, final_norm.reshape(1, d), tgt)

    sent = {}
    dx2, dg2, du2, act2, hb3, dyb2, d_ffn2n = ffn_bwd_act(dx3, x2, ffn2_norm, g2, u2, wg2, wu2, wd2, "ffn2_bwd_act", tok)
    sent["ffn2"] = exchange_start(list(ffn_weight_grads(hb3, dyb2, dg2, du2, act2, "ffn2_grad")), "exchange_start_ffn2")
    da, drn, dga, dgb, mixb, dba, dbr, dx2b = mix_bwd_act(dx2, ba, br, proj, wa, wb, wo, sent["ffn2"][3])
    tg = min(t, 2048)
    row = pl.BlockSpec((tg, d), lambda s, i: (i, 0))

    def square_grad(xa, ya, name):
        return tn_matmul(xa, [ya], row, [row], 1, d, [d], t, tg, name).reshape(N_CHIPS, d // N_CHIPS, d)

    sent["mix"] = exchange_start([square_grad(a, dba, "grad_w_branch_a"), square_grad(rn, dbr, "grad_w_branch_b"),
                                  square_grad(mixb, dx2b, "grad_w_out")], "exchange_start_mix")
    dua, dva, d_ws, d_bs, d_sng, d_snb = sgu_bwd(da, proj, sgu_norm_g, sgu_norm_b, ws_b, bs_c, sent["mix"][3])
    dq, dkr, dv, dgr, dlg = ret_bwd(drn, r, proj, cols, mats, cdec, cos, sin)
    segs = [dua, dva, dq, dkr, dv, dgr, dga, dgb]
    dx1, d_bin, d_mixn = inproj_bwd_act(segs, win, x1, mix_norm, dx2)
    sent["in"] = exchange_start([jnp.concatenate(
        [tn_matmul(hb2, [segs[2 * s], segs[2 * s + 1]], row, [row, row], 1, d, [d, d], t, tg, "grad_w_in_%d" % s)
         for s in range(N_CHIPS)], axis=0)], "exchange_start_in")
    grad_x, dg1, du1, act1, hb1, dyb1, d_ffn1n = ffn_bwd_act(dx1, xs, ffn1_norm, g1, u1, wg1, wu1, wd1, "ffn1_bwd_act",
                                                              sent["in"][3])
    sent["ffn1"] = exchange_start(list(ffn_weight_grads(hb1, dyb1, dg1, du1, act1, "ffn1_grad")), "exchange_start_ffn1")

    out_g, out_d, out_m, out_v = {}, {}, {}, {}
    after = sent["ffn1"][3]
    for g in ("ffn2", "mix", "in", "ffn1"):
        gsems, own, lands, _ = sent[g]
        own, lands = exchange_wait(own, lands, gsems, after, "exchange_wait_" + g)
        plane = [sum_partials(chip, o, l, "sum_" + n) for n, o, l in zip(groups[g], own, lands)]
        other = swap_with_sibling(plane, "swap_" + g)
        for n, mine, sib in zip(groups[g], plane, other):
            res = adamw_shard(mine, sib, shards2d[n], _shard2d(mom[n], n), _shard2d(var[n], n), "adamw_" + n)
            out_g[n], out_d[n], out_m[n], out_v[n] = [_unshard(o, n) for o in res]
        after = out_g[groups[g][-1]]

    dlogit = dlg[:, 0:2, 0].T * jax.nn.sigmoid(-ret_decay_logit[0].astype(f32))
    small_g = dict(ffn1_norm=d_ffn1n, mix_norm=d_mixn, b_in=d_bin, sgu_norm_g=d_sng, sgu_norm_b=d_snb, sgu_w_s=d_ws,
                   sgu_b_s=d_bs, ret_decay_logit=dlogit, ffn2_norm=d_ffn2n, final_norm=d_final)
    shapes = [p[n].shape for n in SMALL]
    g8 = gather_small(_pack([small_g[n] for n in SMALL], shapes))
    sg, sd, sm, sv = adamw_small(g8, _pack([p[n] for n in SMALL], shapes), _pack([mom[n] for n in SMALL], shapes),
                                 _pack([var[n] for n in SMALL], shapes))
    for res, blockv in ((out_g, sg), (out_d, sd), (out_m, sm), (out_v, sv)):
        for n, val in zip(SMALL, _unpack(blockv, shapes)):
            res[n] = val

    loss = lax.psum(loss_blk[0, 0], ("x", "y", "c"))
    return (loss, grad_x[None], *[out_g[n] for n in WEIGHTS], *[out_d[n] for n in WEIGHTS],
            *[out_m[n] for n in WEIGHTS], *[out_v[n] for n in WEIGHTS])
```

```python
import functools

import jax
import jax.numpy as jnp
from jax import lax
from jax.experimental import pallas as pl
from jax.experimental.pallas import tpu as pltpu

f32 = jnp.float32
bf16 = jnp.bfloat16

SGU_CHUNK = 128
CHUNK = 128
RET_HEADS = 4
SGU_GROUPS = 4
ROPE_BASE = 10000.0
NORM_EPS = 1e-6
ADAM_LR = 0.001
ADAM_B1 = 0.9
ADAM_B2 = 0.999
ADAM_EPS = 1e-08
ADAM_WD = 0.01
ADAM_STEP = 10
N_CHIPS = 4
N_DEV = 8
MESH = pl.DeviceIdType.MESH
VMEM_LIMIT = 52 * 1024 * 1024

_NT = (((1,), (1,)), ((), ()))
_TN = (((0,), (0,)), ((), ()))


def _cparams():
    return pltpu.CompilerParams(vmem_limit_bytes=VMEM_LIMIT)


def _row_tile(t):
    return 512 if t >= 2048 else t // 2


def _dot(a, b):
    return jnp.dot(a, b, preferred_element_type=f32)


def _dot_nt(a, b):
    return lax.dot_general(a, b, _NT, preferred_element_type=f32)


def _dot_tn(a, b):
    return lax.dot_general(a, b, _TN, preferred_element_type=f32)


def _rms(x, g):
    r = lax.rsqrt(jnp.mean(x * x, axis=-1, keepdims=True) + NORM_EPS)
    xh = x * r
    return xh * g, xh, r


def _rms_bwd(dy, xh, r, g):
    dxh = dy * g
    return r * (dxh - xh * jnp.mean(dxh * xh, axis=-1, keepdims=True))


def _sigmoid(x):
    return jax.nn.sigmoid(x)


def _dsilu(g, sg):
    return sg * (1.0 + g * (1.0 - sg))


def _gelu(x):
    return 0.5 * x * (1.0 + lax.erf(x * 0.7071067811865476))


def _dgelu(x):
    return 0.5 * (1.0 + lax.erf(x * 0.7071067811865476)) + x * jnp.exp(-0.5 * x * x) * 0.3989422804014327


def _acc_out(ref, first, val):
    @pl.when(first)
    def _():
        ref[...] = val

    @pl.when(jnp.logical_not(first))
    def _():
        ref[...] += val


def ffn_fwd(x, ng, wg, wu, wd, name):
    t, d = x.shape
    s4, fs, _ = wg.shape
    tm = _row_tile(t)

    def body(x_ref, ng_ref, wg_ref, wu_ref, wd_ref, xo_ref, g_ref, u_ref, h_scr, acc_scr):
        s = pl.program_id(1)

        @pl.when(s == 0)
        def _():
            y, _, _ = _rms(x_ref[...], ng_ref[...])
            h_scr[...] = y.astype(bf16)
            acc_scr[...] = jnp.zeros_like(acc_scr)

        h = h_scr[...]
        g = _dot_nt(h, wg_ref[0])
        u = _dot_nt(h, wu_ref[0])
        g_ref[0] = g.astype(bf16)
        u_ref[0] = u.astype(bf16)
        act = (g * _sigmoid(g) * u).astype(bf16)
        acc_scr[...] += _dot(act, wd_ref[0])

        @pl.when(s == s4 - 1)
        def _():
            xo_ref[...] = x_ref[...] + 0.5 * acc_scr[...]

    return pl.pallas_call(
        body, name=name, grid=(t // tm, s4),
        in_specs=[pl.BlockSpec((tm, d), lambda i, s: (i, 0)), pl.BlockSpec((1, d), lambda i, s: (0, 0)),
                  pl.BlockSpec((1, fs, d), lambda i, s: (s, 0, 0)), pl.BlockSpec((1, fs, d), lambda i, s: (s, 0, 0)),
                  pl.BlockSpec((1, fs, d), lambda i, s: (s, 0, 0))],
        out_specs=[pl.BlockSpec((tm, d), lambda i, s: (i, 0)), pl.BlockSpec((1, tm, fs), lambda i, s: (s, i, 0)),
                   pl.BlockSpec((1, tm, fs), lambda i, s: (s, i, 0))],
        out_shape=[jax.ShapeDtypeStruct((t, d), f32), jax.ShapeDtypeStruct((s4, t, fs), bf16),
                   jax.ShapeDtypeStruct((s4, t, fs), bf16)],
        scratch_shapes=[pltpu.VMEM((tm, d), bf16), pltpu.VMEM((tm, d), f32)],
        compiler_params=_cparams(),
    )(x, ng, wg, wu, wd)


def ffn_bwd_act(dxo, x, ng, g, u, wg, wu, wd, name, dep):
    t, d = x.shape
    s4, fs, _ = wg.shape
    tm = _row_tile(t)

    def body(dxo_ref, x_ref, ng_ref, g_ref, u_ref, wg_ref, wu_ref, wd_ref, dep_ref,
             dx_ref, dg_ref, du_ref, act_ref, hb_ref, dyb_ref, dng_ref, dy_scr, acc_scr):
        i = pl.program_id(0)
        s = pl.program_id(1)

        @pl.when(s == 0)
        def _():
            dyb = (0.5 * dxo_ref[...]).astype(bf16)
            dy_scr[...] = dyb
            dyb_ref[...] = dyb
            acc_scr[...] = jnp.zeros_like(acc_scr)

        dact = _dot_nt(dy_scr[...], wd_ref[0])
        gg = g_ref[0].astype(f32)
        uu = u_ref[0].astype(f32)
        sg = _sigmoid(gg)
        sil = gg * sg
        dgb = (dact * uu * _dsilu(gg, sg)).astype(bf16)
        dub = (dact * sil).astype(bf16)
        dg_ref[0] = dgb
        du_ref[0] = dub
        act_ref[0] = (sil * uu).astype(bf16)
        acc_scr[...] += _dot(dgb, wg_ref[0]) + _dot(dub, wu_ref[0])

        @pl.when(s == s4 - 1)
        def _():
            y, xh, r = _rms(x_ref[...], ng_ref[...])
            hb_ref[...] = y.astype(bf16)
            dh = acc_scr[...]
            dx_ref[...] = dxo_ref[...] + _rms_bwd(dh, xh, r, ng_ref[...])
            _acc_out(dng_ref, i == 0, jnp.sum(dh * xh, axis=0, keepdims=True))

    row = lambda i, s: (i, 0)
    shard = lambda i, s: (s, i, 0)
    wsp = lambda i, s: (s, 0, 0)
    return pl.pallas_call(
        body, name=name, grid=(t // tm, s4),
        in_specs=[pl.BlockSpec((tm, d), row), pl.BlockSpec((tm, d), row), pl.BlockSpec((1, d), lambda i, s: (0, 0)),
                  pl.BlockSpec((1, tm, fs), shard), pl.BlockSpec((1, tm, fs), shard),
                  pl.BlockSpec((1, fs, d), wsp), pl.BlockSpec((1, fs, d), wsp), pl.BlockSpec((1, fs, d), wsp), _ANY],
        out_specs=[pl.BlockSpec((tm, d), row), pl.BlockSpec((1, tm, fs), shard), pl.BlockSpec((1, tm, fs), shard),
                   pl.BlockSpec((1, tm, fs), shard), pl.BlockSpec((tm, d), row), pl.BlockSpec((tm, d), row),
                   pl.BlockSpec((1, d), lambda i, s: (0, 0))],
        out_shape=[jax.ShapeDtypeStruct((t, d), f32), jax.ShapeDtypeStruct((s4, t, fs), bf16),
                   jax.ShapeDtypeStruct((s4, t, fs), bf16), jax.ShapeDtypeStruct((s4, t, fs), bf16),
                   jax.ShapeDtypeStruct((t, d), bf16), jax.ShapeDtypeStruct((t, d), bf16),
                   jax.ShapeDtypeStruct((1, d), f32)],
        scratch_shapes=[pltpu.VMEM((tm, d), bf16), pltpu.VMEM((tm, d), f32)],
        compiler_params=_cparams(),
    )(dxo, x, ng, g, u, wg, wu, wd, dep)


def tn_matmul(xs, ys, x_spec, y_specs, n_shards, k1, k2s, t, tm, name):
    k2 = sum(k2s)
    ny = len(ys)

    def body(*refs):
        x_ref = refs[0]
        y_refs = refs[1:1 + ny]
        o_ref = refs[1 + ny]
        acc = refs[2 + ny]
        i = pl.program_id(1)
        xb = x_ref[0] if len(x_ref.shape) == 3 else x_ref[...]
        off = 0
        for y_ref, w in zip(y_refs, k2s):
            yb = y_ref[0] if len(y_ref.shape) == 3 else y_ref[...]
            part = _dot_tn(xb, yb)
            sl = (slice(None), slice(off, off + w))

            @pl.when(i == 0)
            def _(part=part, sl=sl):
                acc[sl] = part

            @pl.when(i > 0)
            def _(part=part, sl=sl):
                acc[sl] += part

            off += w

        @pl.when(i == t // tm - 1)
        def _():
            o_ref[0] = acc[...].astype(bf16)

    return pl.pallas_call(
        body, name=name, grid=(n_shards, t // tm),
        in_specs=[x_spec] + list(y_specs),
        out_specs=pl.BlockSpec((1, k1, k2), lambda s, i: (s, 0, 0)),
        out_shape=jax.ShapeDtypeStruct((n_shards, k1, k2), bf16),
        scratch_shapes=[pltpu.VMEM((k1, k2), f32)],
        compiler_params=_cparams(),
    )(xs, *ys)


def ffn_weight_grads(hb, dyb, dg, du, act, name):
    t, d = hb.shape
    s4, _, fs = dg.shape
    tm = t
    row = pl.BlockSpec((tm, d), lambda s, i: (i, 0))
    shard = pl.BlockSpec((1, tm, fs), lambda s, i: (s, i, 0))
    gwg = tn_matmul(dg, [hb], shard, [row], s4, fs, [d], t, tm, name + "_wg")
    gwu = tn_matmul(du, [hb], shard, [row], s4, fs, [d], t, tm, name + "_wu")
    gwd = tn_matmul(act, [dyb], shard, [row], s4, fs, [d], t, tm, name + "_wd")
    return gwg, gwu, gwd


def inproj_fwd(x1, ng, win, bin4, cos, sin):
    t, d = x1.shape
    s4, _, w2 = win.shape
    tm = _row_tile(t)
    dk = d // RET_HEADS
    scale = dk ** -0.5

    def body(x_ref, ng_ref, w_ref, b_ref, cos_ref, sin_ref, p_ref, hb_ref, h_scr):
        s = pl.program_id(1)

        @pl.when(s == 0)
        def _():
            y, _, _ = _rms(x_ref[...], ng_ref[...])
            h_scr[...] = y.astype(bf16)
            hb_ref[...] = y.astype(bf16)

        p = _dot(h_scr[...], w_ref[0]) + b_ref[0]

        @pl.when(s != 1)
        def _():
            p_ref[0] = p.astype(bf16)

        @pl.when(s == 1)
        def _():
            cs, sn = cos_ref[...], sin_ref[...]
            for e in range(2 * RET_HEADS):
                cols = slice(e * dk, (e + 1) * dk)
                rot = _rot(p[:, cols], cs, sn)
                p_ref[0, :, cols] = (rot if e < RET_HEADS else rot * scale).astype(bf16)

    tab = pl.BlockSpec((tm, dk // 2), lambda i, s: (i, 0))
    return pl.pallas_call(
        body, name="inproj_fwd", grid=(t // tm, s4),
        in_specs=[pl.BlockSpec((tm, d), lambda i, s: (i, 0)), pl.BlockSpec((1, d), lambda i, s: (0, 0)),
                  pl.BlockSpec((1, d, w2), lambda i, s: (s, 0, 0)), pl.BlockSpec((1, 1, w2), lambda i, s: (s, 0, 0)),
                  tab, tab],
        out_specs=[pl.BlockSpec((1, tm, w2), lambda i, s: (s, i, 0)), pl.BlockSpec((tm, d), lambda i, s: (i, 0))],
        out_shape=[jax.ShapeDtypeStruct((s4, t, w2), bf16), jax.ShapeDtypeStruct((t, d), bf16)],
        scratch_shapes=[pltpu.VMEM((tm, d), bf16)],
        compiler_params=_cparams(),
    )(x1, ng, win, bin4, cos, sin)


def _sgu_norm(va, ng, nb):
    gv = _gelu(va)
    mu = jnp.mean(gv, axis=-1, keepdims=True)
    xc = gv - mu
    rstd = lax.rsqrt(jnp.mean(xc * xc, axis=-1, keepdims=True) + NORM_EPS)
    xh = xc * rstd
    return xh, rstd, (xh * ng + nb).astype(bf16)


def sgu_fwd(proj, ng, nb, ws, bs):
    _, t, w2 = proj.shape
    d = w2 // 2
    gd = d // SGU_GROUPS
    tm = _row_tile(t)

    def body(p_ref, ng_ref, nb_ref, ws_ref, bs_ref, a_ref):
        ua = p_ref[0, :, 0:d].astype(f32)
        va = p_ref[0, :, d:w2].astype(f32)
        gu = _gelu(ua)
        _, _, vn = _sgu_norm(va, ng_ref[...], nb_ref[...])
        for c in range(tm // SGU_CHUNK):
            rows = slice(c * SGU_CHUNK, (c + 1) * SGU_CHUNK)
            for g in range(SGU_GROUPS):
                cols = slice(g * gd, (g + 1) * gd)
                sg = _dot(ws_ref[g], vn[rows, cols]) + bs_ref[g]
                a_ref[rows, cols] = (gu[rows, cols] * sg).astype(bf16)

    return pl.pallas_call(
        body, name="sgu_fwd", grid=(t // tm,),
        in_specs=[pl.BlockSpec((1, tm, w2), lambda i: (0, i, 0)), pl.BlockSpec((1, d), lambda i: (0, 0)),
                  pl.BlockSpec((1, d), lambda i: (0, 0)), pl.BlockSpec((SGU_GROUPS, SGU_CHUNK, SGU_CHUNK), lambda i: (0, 0, 0)),
                  pl.BlockSpec((SGU_GROUPS, SGU_CHUNK, 1), lambda i: (0, 0, 0))],
        out_specs=pl.BlockSpec((tm, d), lambda i: (i, 0)),
        out_shape=jax.ShapeDtypeStruct((t, d), bf16),
        compiler_params=_cparams(),
    )(proj, ng, nb, ws, bs)


def sgu_bwd(da, proj, ng, nb, ws, bs, dep):
    _, t, w2 = proj.shape
    d = w2 // 2
    gd = d // SGU_GROUPS
    tm = _row_tile(t)

    def body(da_ref, p_ref, ng_ref, nb_ref, ws_ref, bs_ref, dep_ref,
             dua_ref, dva_ref, dws_ref, dbs_ref, dng_ref, dnb_ref, dvn_scr):
        i = pl.program_id(0)
        ua = p_ref[0, :, 0:d].astype(f32)
        va = p_ref[0, :, d:w2].astype(f32)
        gu = _gelu(ua)
        xh, rstd, vn = _sgu_norm(va, ng_ref[...], nb_ref[...])
        dad = da_ref[...].astype(f32)
        dsb = (dad * gu).astype(bf16)
        for c in range(tm // SGU_CHUNK):
            rows = slice(c * SGU_CHUNK, (c + 1) * SGU_CHUNK)
            for g in range(SGU_GROUPS):
                cols = slice(g * gd, (g + 1) * gd)
                sg = _dot(ws_ref[g], vn[rows, cols]) + bs_ref[g]
                dua_ref[rows, cols] = (dad[rows, cols] * sg * _dgelu(ua[rows, cols])).astype(bf16)
                ds = dsb[rows, cols]
                dvn_scr[rows, cols] = _dot_tn(ws_ref[g], ds)
                dw = _dot_nt(ds, vn[rows, cols])
                db = jnp.sum(ds.astype(f32), axis=1, keepdims=True)
                if c == 0:
                    _acc_out(dws_ref.at[g], i == 0, dw)
                    _acc_out(dbs_ref.at[g], i == 0, db)
                else:
                    dws_ref[g] += dw
                    dbs_ref[g] += db
        dvn = dvn_scr[...]
        _acc_out(dng_ref, i == 0, jnp.sum(dvn * xh, axis=0, keepdims=True))
        _acc_out(dnb_ref, i == 0, jnp.sum(dvn, axis=0, keepdims=True))
        dxh = dvn * ng_ref[...]
        dgv = rstd * (dxh - jnp.mean(dxh, axis=-1, keepdims=True) - xh * jnp.mean(dxh * xh, axis=-1, keepdims=True))
        dva_ref[...] = (dgv * _dgelu(va)).astype(bf16)

    row = pl.BlockSpec((tm, d), lambda i: (i, 0))
    vec = pl.BlockSpec((1, d), lambda i: (0, 0))
    wsp = pl.BlockSpec((SGU_GROUPS, SGU_CHUNK, SGU_CHUNK), lambda i: (0, 0, 0))
    bsp = pl.BlockSpec((SGU_GROUPS, SGU_CHUNK, 1), lambda i: (0, 0, 0))
    return pl.pallas_call(
        body, name="sgu_bwd", grid=(t // tm,),
        in_specs=[row, pl.BlockSpec((1, tm, w2), lambda i: (0, i, 0)), vec, vec, wsp, bsp, _ANY],
        out_specs=[row, row, wsp, bsp, vec, vec],
        out_shape=[jax.ShapeDtypeStruct((t, d), bf16), jax.ShapeDtypeStruct((t, d), bf16),
                   jax.ShapeDtypeStruct((SGU_GROUPS, SGU_CHUNK, SGU_CHUNK), f32), jax.ShapeDtypeStruct((SGU_GROUPS, SGU_CHUNK, 1), f32),
                   jax.ShapeDtypeStruct((1, d), f32), jax.ShapeDtypeStruct((1, d), f32)],
        scratch_shapes=[pltpu.VMEM((tm, d), f32)],
        compiler_params=_cparams(),
    )(da, proj, ng, nb, ws, bs, dep)


def retention_constants(decay_logit, t, dk):
    lg = jax.nn.log_sigmoid(decay_logit.astype(f32))
    lgf = lg[0][:, None]
    lgb = lg[1][:, None]
    idx = jnp.arange(CHUNK, dtype=f32)[None, :]
    af = jnp.exp((idx + 1.0) * lgf)
    ab = jnp.exp((CHUNK - idx) * lgb)
    kf = jnp.exp((CHUNK - 1.0 - idx) * lgf)
    kb = jnp.exp(idx * lgb)
    cols = jnp.stack([af, ab, kf, kb, af * (idx + 1.0), ab * (CHUNK - idx), kf * (CHUNK - 1.0 - idx), kb * idx], axis=1)
    cols = cols[..., None]
    diff = idx[0][:, None] - idx[0][None, :]
    dfm = jnp.where(diff >= 0, jnp.exp(jnp.maximum(diff, 0.0)[None] * lgf[:, :, None]), 0.0)
    dbm = jnp.where(diff < 0, jnp.exp(jnp.maximum(-diff, 0.0)[None] * lgb[:, :, None]), 0.0)
    mats = jnp.stack([dfm + dbm, dfm * diff[None], dbm * (-diff)[None]], axis=1)
    cdec = jnp.stack([jnp.broadcast_to(jnp.exp(CHUNK * lgf), (RET_HEADS, dk)),
                      jnp.broadcast_to(jnp.exp(CHUNK * lgb), (RET_HEADS, dk))], axis=1)
    theta = ROPE_BASE ** (-jnp.arange(0, dk, 2, dtype=f32) / dk)
    ang = jnp.arange(t, dtype=f32)[:, None] * theta[None, :]
    return cols, mats, cdec, jnp.cos(ang), jnp.sin(ang)


def _rot(tr, cos, sin):
    half = tr.shape[-1] // 2
    t1 = tr[:, :half]
    t2 = tr[:, half:]
    return jnp.concatenate([t1 * cos - t2 * sin, t2 * cos + t1 * sin], axis=-1)


def _rot_inv(dt, cos, sin):
    half = dt.shape[-1] // 2
    d1 = dt[:, :half]
    d2 = dt[:, half:]
    return jnp.concatenate([d1 * cos + d2 * sin, d2 * cos - d1 * sin], axis=-1)


def _ret_specs(t, d, dk, rt):
    nr = t // rt
    hq = d // dk

    def blk(p, n):
        return (1 - p) * (nr - 1 - n) + p * n

    q_spec = pl.BlockSpec((1, rt, dk), lambda h, p, n: (1, blk(p, n), h))
    k_spec = pl.BlockSpec((1, rt, dk), lambda h, p, n: (1, blk(p, n), hq + h))
    v_spec = pl.BlockSpec((1, rt, dk), lambda h, p, n: (2, blk(p, n), h))
    g_spec = pl.BlockSpec((1, rt, dk), lambda h, p, n: (2, blk(p, n), hq + h))
    tab_spec = pl.BlockSpec((rt, dk // 2), lambda h, p, n: (blk(p, n), 0))
    cols_spec = pl.BlockSpec((1, 8, CHUNK, 1), lambda h, p, n: (h, 0, 0, 0))
    mats_spec = pl.BlockSpec((1, 3, CHUNK, CHUNK), lambda h, p, n: (h, 0, 0, 0))
    cdec_spec = pl.BlockSpec((1, 2, dk), lambda h, p, n: (h, 0, 0))
    in_row = pl.BlockSpec((rt, dk), lambda h, p, n: (blk(p, n), h))
    out_row = pl.BlockSpec((rt, dk), lambda h, p, n: (p * n, h))
    return nr, blk, q_spec, k_spec, v_spec, g_spec, tab_spec, cols_spec, mats_spec, cdec_spec, in_row, out_row


def ret_fwd(proj, cols, mats, cdec):
    _, t, w2 = proj.shape
    d = w2 // 2
    dk = d // RET_HEADS
    rt = _row_tile(t)
    cpt = rt // CHUNK
    nr, blk, q_spec, k_spec, v_spec, g_spec, _, cols_spec, mats_spec, cdec_spec, _, out_row = _ret_specs(t, d, dk, rt)

    def body(q_ref, k_ref, v_ref, g_ref, cols_ref, mats_ref, cdec_ref, r_ref, rn_ref, sb_scr, st):
        p = pl.program_id(1)
        n = pl.program_id(2)
        af, ab, kf, kb = cols_ref[0, 0], cols_ref[0, 1], cols_ref[0, 2], cols_ref[0, 3]
        cf = cdec_ref[0, 0:1, :]
        cb = cdec_ref[0, 1:2, :]

        @pl.when(n == 0)
        def _():
            st[...] = jnp.zeros_like(st)

        @pl.when(p == 0)
        def _():
            for j in reversed(range(cpt)):
                rows = slice(j * CHUNK, (j + 1) * CHUNK)
                ch = blk(p, n) * cpt + j
                kk = k_ref[0, rows, :].astype(f32)
                sb_scr[ch] = st[...].astype(bf16)
                st[...] = st[...] * cb + _dot_tn((kk * kb).astype(bf16), v_ref[0, rows, :])

        @pl.when(p == 1)
        def _():
            for j in range(cpt):
                rows = slice(j * CHUNK, (j + 1) * CHUNK)
                ch = blk(p, n) * cpt + j
                qb = q_ref[0, rows, :]
                kkb = k_ref[0, rows, :]
                q = qb.astype(f32)
                kk = kkb.astype(f32)
                v = v_ref[0, rows, :]
                pm = (_dot_nt(qb, kkb) * mats_ref[0, 0]).astype(bf16)
                out = (_dot(pm, v) + _dot((q * af).astype(bf16), st[...].astype(bf16))
                       + _dot((q * ab).astype(bf16), sb_scr[ch]))
                st[...] = st[...] * cf + _dot_tn((kk * kf).astype(bf16), v)
                rhat = out * lax.rsqrt(jnp.mean(out * out, axis=-1, keepdims=True) + NORM_EPS)
                gg = g_ref[0, rows, :].astype(f32)
                r_ref[rows, :] = out.astype(bf16)
                rn_ref[rows, :] = (rhat * gg * _sigmoid(gg)).astype(bf16)

    return pl.pallas_call(
        body, name="ret_fwd", grid=(RET_HEADS, 2, nr),
        in_specs=[q_spec, k_spec, v_spec, g_spec, cols_spec, mats_spec, cdec_spec],
        out_specs=[out_row, out_row],
        out_shape=[jax.ShapeDtypeStruct((t, d), bf16), jax.ShapeDtypeStruct((t, d), bf16)],
        scratch_shapes=[pltpu.VMEM((t // CHUNK, dk, dk), bf16), pltpu.VMEM((dk, dk), f32)],
        compiler_params=_cparams(),
    )(proj, proj, proj, proj, cols, mats, cdec)


def ret_bwd(drn, r, proj, cols, mats, cdec, cos, sin):
    _, t, w2 = proj.shape
    d = w2 // 2
    dk = d // RET_HEADS
    rt = _row_tile(t)
    cpt = rt // CHUNK
    nr, blk, q_spec, k_spec, v_spec, g_spec, tab_spec, cols_spec, mats_spec, cdec_spec, in_row, out_row = _ret_specs(t, d, dk, rt)
    scale = dk ** -0.5

    def body(drn_ref, r_ref, q_ref, k_ref, v_ref, g_ref, cos_ref, sin_ref, cols_ref, mats_ref, cdec_ref,
             dq_ref, dk_ref, dv_ref, dg_ref, dlg_ref,
             sb_scr, gf_scr, st_s, st_g, acc_af, acc_ab, acc_vf, acc_vb, acc_sf, acc_sb):
        p = pl.program_id(1)
        n = pl.program_id(2)
        af, ab, kf, kb = cols_ref[0, 0], cols_ref[0, 1], cols_ref[0, 2], cols_ref[0, 3]
        af1, ab1, kf1, kb1 = cols_ref[0, 4], cols_ref[0, 5], cols_ref[0, 6], cols_ref[0, 7]
        cf = cdec_ref[0, 0:1, :]
        cb = cdec_ref[0, 1:2, :]

        @pl.when(n == 0)
        def _():
            st_s[...] = jnp.zeros_like(st_s)
            st_g[...] = jnp.zeros_like(st_g)

        @pl.when(jnp.logical_and(n == 0, p == 1))
        def _():
            for a in (acc_af, acc_ab, acc_vf, acc_vb, acc_sf, acc_sb):
                a[...] = jnp.zeros_like(a)

        def load(rows):
            cs, sn = cos_ref[rows, :], sin_ref[rows, :]
            q = q_ref[0, rows, :].astype(f32)
            kk = k_ref[0, rows, :].astype(f32)
            rr = r_ref[rows, :].astype(f32)
            rstd = lax.rsqrt(jnp.mean(rr * rr, axis=-1, keepdims=True) + NORM_EPS)
            rhat = rr * rstd
            gg = g_ref[0, rows, :].astype(f32)
            sg = _sigmoid(gg)
            dd = drn_ref[rows, :].astype(f32)
            drhat = dd * gg * sg
            dout = rstd * (drhat - rhat * jnp.mean(drhat * rhat, axis=-1, keepdims=True))
            dgr = dd * rhat * _dsilu(gg, sg)
            return q, kk, dout.astype(bf16), dgr, cs, sn

        @pl.when(p == 0)
        def _():
            for j in reversed(range(cpt)):
                rows = slice(j * CHUNK, (j + 1) * CHUNK)
                ch = blk(p, n) * cpt + j
                q, kk, doutb, _, _, _ = load(rows)
                sb_scr[ch] = st_s[...].astype(bf16)
                gf_scr[ch] = st_g[...].astype(bf16)
                st_s[...] = st_s[...] * cb + _dot_tn((kk * kb).astype(bf16), v_ref[0, rows, :])
                st_g[...] = st_g[...] * cf + _dot_tn((q * af).astype(bf16), doutb)

        @pl.when(p == 1)
        def _():
            for j in range(cpt):
                rows = slice(j * CHUNK, (j + 1) * CHUNK)
                ch = blk(p, n) * cpt + j
                q, kk, doutb, dgr, cs, sn = load(rows)
                v = v_ref[0, rows, :]
                qb = q_ref[0, rows, :]
                kkb = k_ref[0, rows, :]
                sf = st_s[...]
                gb = st_g[...]
                sfb = sf.astype(bf16)
                gbb = gb.astype(bf16)
                sbb = sb_scr[ch]
                gfb = gf_scr[ch]
                dmat = mats_ref[0, 0]
                scores = _dot_nt(qb, kkb)
                dpraw = _dot_nt(doutb, v)
                dpb = (dpraw * dmat).astype(bf16)
                pmb = (scores * dmat).astype(bf16)
                x1 = _dot_nt(doutb, sfb)
                x2 = _dot_nt(doutb, sbb)
                y1 = _dot_nt(v, gfb)
                y2 = _dot_nt(v, gbb)
                kdf = (kk * kf).astype(bf16)
                kdb = (kk * kb).astype(bf16)
                dq = _dot(dpb, kkb) + x1 * af + x2 * ab
                dkk = _dot_tn(dpb, qb) + y1 * kf + y2 * kb
                dv = _dot_tn(pmb, doutb) + _dot(kdf, gfb) + _dot(kdb, gbb)
                ps = dpraw * scores
                acc_af[...] += ps * mats_ref[0, 1]
                acc_ab[...] += ps * mats_ref[0, 2]
                acc_vf[...] += x1 * q * af1 + y1 * kk * kf1
                acc_vb[...] += x2 * q * ab1 + y2 * kk * kb1
                acc_sf[...] += gfb.astype(f32) * sf
                acc_sb[...] += gb * sbb.astype(f32)
                st_s[...] = sf * cf + _dot_tn(kdf, v)
                st_g[...] = gb * cb + _dot_tn((q * ab).astype(bf16), doutb)
                dq_ref[rows, :] = _rot_inv(dq, cs, sn).astype(bf16)
                dk_ref[rows, :] = (_rot_inv(dkk, cs, sn) * scale).astype(bf16)
                dv_ref[rows, :] = dv.astype(bf16)
                dg_ref[rows, :] = dgr.astype(bf16)

        @pl.when(jnp.logical_and(p == 1, n == nr - 1))
        def _():
            tf = jnp.sum(acc_af[...]) + jnp.sum(acc_vf[...]) + CHUNK * jnp.sum(acc_sf[...] * cf)
            tb = jnp.sum(acc_ab[...]) + jnp.sum(acc_vb[...]) + CHUNK * jnp.sum(acc_sb[...] * cb)
            rid = lax.broadcasted_iota(jnp.int32, (8, 128), 0)
            dlg_ref[0] = jnp.where(rid == 0, tf, jnp.where(rid == 1, tb, 0.0))

    nch = t // CHUNK
    return pl.pallas_call(
        body, name="ret_bwd", grid=(RET_HEADS, 2, nr),
        in_specs=[in_row, in_row, q_spec, k_spec, v_spec, g_spec, tab_spec, tab_spec, cols_spec, mats_spec, cdec_spec],
        out_specs=[out_row, out_row, out_row, out_row, pl.BlockSpec((1, 8, 128), lambda h, p, n: (h, 0, 0))],
        out_shape=[jax.ShapeDtypeStruct((t, d), bf16)] * 4 + [jax.ShapeDtypeStruct((RET_HEADS, 8, 128), f32)],
        scratch_shapes=[pltpu.VMEM((nch, dk, dk), bf16), pltpu.VMEM((nch, dk, dk), bf16),
                        pltpu.VMEM((dk, dk), f32), pltpu.VMEM((dk, dk), f32),
                        pltpu.VMEM((CHUNK, CHUNK), f32), pltpu.VMEM((CHUNK, CHUNK), f32),
                        pltpu.VMEM((CHUNK, dk), f32), pltpu.VMEM((CHUNK, dk), f32),
                        pltpu.VMEM((dk, dk), f32), pltpu.VMEM((dk, dk), f32)],
        compiler_params=_cparams(),
    )(drn, r, proj, proj, proj, proj, cos, sin, cols, mats, cdec)


def mix_fwd(a, rn, proj, wa, wb, wo, x1):
    t, d = x1.shape
    tm = _row_tile(t)

    def body(a_ref, rn_ref, p_ref, wa_ref, wb_ref, wo_ref, x_ref, xo_ref, ba_ref, br_ref):
        ba = _dot(a_ref[...], wa_ref[...])
        br = _dot(rn_ref[...], wb_ref[...])
        sa = _sigmoid(p_ref[0, :, 0:d].astype(f32))
        sb = _sigmoid(p_ref[0, :, d:2 * d].astype(f32))
        mix = (sa * ba + sb * br).astype(bf16)
        xo_ref[...] = x_ref[...] + _dot(mix, wo_ref[...])
        ba_ref[...] = ba.astype(bf16)
        br_ref[...] = br.astype(bf16)

    row = pl.BlockSpec((tm, d), lambda i: (i, 0))
    wsp = pl.BlockSpec((d, d), lambda i: (0, 0))
    return pl.pallas_call(
        body, name="mix_fwd", grid=(t // tm,),
        in_specs=[row, row, pl.BlockSpec((1, tm, 2 * d), lambda i: (3, i, 0)), wsp, wsp, wsp, row],
        out_specs=[row, row, row],
        out_shape=[jax.ShapeDtypeStruct((t, d), f32), jax.ShapeDtypeStruct((t, d), bf16), jax.ShapeDtypeStruct((t, d), bf16)],
        compiler_params=_cparams(),
    )(a, rn, proj, wa, wb, wo, x1)


def mix_bwd_act(dx2, ba, br, proj, wa, wb, wo, dep):
    t, d = dx2.shape
    tm = _row_tile(t)

    def body(dx_ref, ba_ref, br_ref, p_ref, wa_ref, wb_ref, wo_ref, dep_ref,
             da_ref, drn_ref, dga_ref, dgb_ref, mix_ref, dba_ref, dbr_ref, dxb_ref):
        dxb = dx_ref[...].astype(bf16)
        dxb_ref[...] = dxb
        dmix = _dot_nt(dxb, wo_ref[...])
        ba = ba_ref[...].astype(f32)
        br = br_ref[...].astype(f32)
        sa = _sigmoid(p_ref[0, :, 0:d].astype(f32))
        sb = _sigmoid(p_ref[0, :, d:2 * d].astype(f32))
        mix_ref[...] = (sa * ba + sb * br).astype(bf16)
        dba = (dmix * sa).astype(bf16)
        dbr = (dmix * sb).astype(bf16)
        dba_ref[...] = dba
        dbr_ref[...] = dbr
        dga_ref[...] = (dmix * ba * sa * (1.0 - sa)).astype(bf16)
        dgb_ref[...] = (dmix * br * sb * (1.0 - sb)).astype(bf16)
        da_ref[...] = _dot_nt(dba, wa_ref[...]).astype(bf16)
        drn_ref[...] = _dot_nt(dbr, wb_ref[...]).astype(bf16)

    row = pl.BlockSpec((tm, d), lambda i: (i, 0))
    wsp = pl.BlockSpec((d, d), lambda i: (0, 0))
    return pl.pallas_call(
        body, name="mix_bwd_act", grid=(t // tm,),
        in_specs=[row, row, row, pl.BlockSpec((1, tm, 2 * d), lambda i: (3, i, 0)), wsp, wsp, wsp, _ANY],
        out_specs=[row] * 8,
        out_shape=[jax.ShapeDtypeStruct((t, d), bf16)] * 8,
        compiler_params=_cparams(),
    )(dx2, ba, br, proj, wa, wb, wo, dep)


def inproj_bwd_act(segs, win, x1, ng, dx2):
    t, d = x1.shape
    s4 = win.shape[0]
    tm = _row_tile(t) // 2
    nseg = len(segs)

    def body(*refs):
        seg_refs = refs[:nseg]
        w_ref, x_ref, ng_ref, dx2_ref, dx1_ref, db_ref, dng_ref = refs[nseg:]
        i = pl.program_id(0)
        dh = None
        for e, sr in enumerate(seg_refs):
            sb = sr[...]
            part = _dot_nt(sb, w_ref[e // 2, :, (e % 2) * d:(e % 2 + 1) * d])
            dh = part if dh is None else dh + part
            _acc_out(db_ref.at[e], i == 0, jnp.sum(sb.astype(f32), axis=0, keepdims=True))
        _, xh, r = _rms(x_ref[...], ng_ref[...])
        dx1_ref[...] = dx2_ref[...] + _rms_bwd(dh, xh, r, ng_ref[...])
        _acc_out(dng_ref, i == 0, jnp.sum(dh * xh, axis=0, keepdims=True))

    row = pl.BlockSpec((tm, d), lambda i: (i, 0))
    vec = pl.BlockSpec((1, d), lambda i: (0, 0))
    return pl.pallas_call(
        body, name="inproj_bwd_act", grid=(t // tm,),
        in_specs=[row] * nseg + [pl.BlockSpec((s4, d, 2 * d), lambda i: (0, 0, 0), pipeline_mode=pl.Buffered(1)),
                                 row, vec, row],
        out_specs=[row, pl.BlockSpec((nseg, 1, d), lambda i: (0, 0, 0)), vec],
        out_shape=[jax.ShapeDtypeStruct((t, d), f32), jax.ShapeDtypeStruct((nseg, 1, d), f32),
                   jax.ShapeDtypeStruct((1, d), f32)],
        compiler_params=_cparams(),
    )(*segs, win, x1, ng, dx2)


def loss_head(x3, fng, tgt):
    t, d = x3.shape
    tm = _row_tile(t)

    def body(x_ref, g_ref, t_ref, loss_ref, dx_ref, dg_ref):
        i = pl.program_id(0)
        y, xh, r = _rms(x_ref[...], g_ref[...])
        diff = y - t_ref[...]
        part = 0.5 * jnp.sum(jnp.sum(diff * diff, axis=0, keepdims=True), axis=1, keepdims=True) / d
        _acc_out(loss_ref, i == 0, jnp.broadcast_to(part, (1, 128)))
        dy = diff * (1.0 / d)
        dx_ref[...] = _rms_bwd(dy, xh, r, g_ref[...])
        _acc_out(dg_ref, i == 0, jnp.sum(dy * xh, axis=0, keepdims=True))

    row = pl.BlockSpec((tm, d), lambda i: (i, 0))
    vec = pl.BlockSpec((1, d), lambda i: (0, 0))
    return pl.pallas_call(
        body, name="loss_head", grid=(t // tm,),
        in_specs=[row, vec, row],
        out_specs=[pl.BlockSpec((1, 128), lambda i: (0, 0)), row, vec],
        out_shape=[jax.ShapeDtypeStruct((1, 128), f32), jax.ShapeDtypeStruct((t, d), f32), jax.ShapeDtypeStruct((1, d), f32)],
        compiler_params=_cparams(),
    )(x3, fng, tgt)


def _place():
    return lax.axis_index("x"), lax.axis_index("y"), lax.axis_index("c")


def _other_chips(x, y):
    return [(1 - x, y), (x, 1 - y), (1 - x, 1 - y)]


_ANY = pl.BlockSpec(memory_space=pl.ANY)


_HBM = pl.BlockSpec(memory_space=pltpu.HBM)
_SEM = pl.BlockSpec(memory_space=pltpu.SEMAPHORE)
_EFFECT = pltpu.SideEffectType.DATAFLOW_SIDE_EFFECTING


def _hbm(a):
    return pltpu.with_memory_space_constraint(a, pltpu.HBM)


def _half_rows(ref, c):
    half = ref.shape[1] // 2
    return pl.ds(pl.multiple_of(c * half, 16), half)


def _chip_copy(src, dst, send_sem, recv_sem, chip, c):
    return pltpu.make_async_remote_copy(src_ref=src, dst_ref=dst, send_sem=send_sem, recv_sem=recv_sem,
                                        device_id=(chip[0], chip[1], c), device_id_type=MESH)


def gather_start(bufs, groups):
    nb, ng = len(bufs), len(groups)

    def body(*refs):
        ins = refs[:nb]
        sems = refs[nb:nb + 2 * ng]
        token = refs[-1]
        x, y, c = _place()
        k = 2 * x + y
        for gi, grp in enumerate(groups):
            for wi, w in enumerate(grp):
                mine = ins[w].at[k, _half_rows(ins[w], c)]
                for j, chip in enumerate(_other_chips(x, y)):
                    _chip_copy(mine, mine, sems[2 * gi].at[3 * wi + j], sems[2 * gi + 1].at[3 * wi + j], chip, c).start()
        token[...] = jnp.zeros_like(token)

    sem_shapes = []
    for grp in groups:
        sem_shapes += [pltpu.SemaphoreType.DMA((3 * len(grp),)), pltpu.SemaphoreType.DMA((3 * len(grp),))]
    outs = pl.pallas_call(
        body, name="gather_start",
        out_shape=sem_shapes + [pltpu.HBM(b.shape, b.dtype) for b in bufs] + [jax.ShapeDtypeStruct((8, 128), f32)],
        in_specs=[_HBM] * nb,
        out_specs=[_SEM] * (2 * ng) + [_HBM] * nb + [pl.BlockSpec(memory_space=pltpu.VMEM)],
        input_output_aliases={w: 2 * ng + w for w in range(nb)},
        compiler_params=pltpu.CompilerParams(has_side_effects=_EFFECT),
    )(*[_hbm(b) for b in bufs])
    sems = [(outs[2 * gi], outs[2 * gi + 1]) for gi in range(ng)]
    return sems, list(outs[2 * ng:2 * ng + nb]), outs[-1]


def gather_wait(bufs, sems, after, name):
    n = len(bufs)

    def body(*refs):
        ins = refs[:n]
        send_sems, recv_sems = refs[n], refs[n + 1]
        x, y, c = _place()
        k = 2 * x + y
        for wi in range(n):
            half = _half_rows(ins[wi], c)
            for j, chip in enumerate(_other_chips(x, y)):
                cp = _chip_copy(ins[wi].at[k, half], ins[wi].at[2 * chip[0] + chip[1], half], send_sems.at[3 * wi + j],
                                recv_sems.at[3 * wi + j], chip, c)
                cp.wait_send()
                cp.wait_recv()

    outs = pl.pallas_call(
        body, name=name,
        out_shape=[pltpu.HBM(b.shape, b.dtype) for b in bufs],
        in_specs=[_HBM] * n + [_SEM, _SEM, _ANY],
        out_specs=[_HBM] * n,
        input_output_aliases={i: i for i in range(n)},
        compiler_params=pltpu.CompilerParams(has_side_effects=_EFFECT),
    )(*bufs, sems[0], sems[1], after)
    return list(outs)


def gather_forward(bufs, name):
    n = len(bufs)

    def body(*refs):
        ins = refs[n:2 * n]
        send_sems, recv_sems = refs[2 * n], refs[2 * n + 1]
        x, y, c = _place()
        copies = []
        for wi in range(n):
            for j, chip in enumerate(_other_chips(x, y)):
                kp = 2 * chip[0] + chip[1]
                got = ins[wi].at[kp, _half_rows(ins[wi], c)]
                cp = pltpu.make_async_remote_copy(
                    src_ref=got, dst_ref=got, send_sem=send_sems.at[3 * wi + j], recv_sem=recv_sems.at[3 * wi + j],
                    device_id=(x, y, 1 - c), device_id_type=MESH)
                cp.start()
                copies.append((cp, wi, kp, j))
        for cp, wi, kp, j in copies:
            cp.wait_send()
            theirs = ins[wi].at[kp, _half_rows(ins[wi], 1 - c)]
            pltpu.make_async_remote_copy(
                src_ref=theirs, dst_ref=theirs, send_sem=send_sems.at[3 * wi + j], recv_sem=recv_sems.at[3 * wi + j],
                device_id=(x, y, 1 - c), device_id_type=MESH).wait_recv()

    outs = pl.pallas_call(
        body, name=name,
        out_shape=[jax.ShapeDtypeStruct(b.shape, b.dtype) for b in bufs],
        in_specs=[_ANY] * n, out_specs=[_ANY] * n,
        input_output_aliases={i: i for i in range(n)},
        scratch_shapes=[pltpu.SemaphoreType.DMA((3 * n,)), pltpu.SemaphoreType.DMA((3 * n,))],
    )(*bufs)
    return list(outs)


def exchange_start(grads, name):
    n = len(grads)
    lands = [lax.empty((3,) + g.shape[1:], g.dtype) for g in grads]

    def body(*refs):
        ins = refs[:n]
        land = refs[n:2 * n]
        send_sems, recv_sems = refs[2 * n], refs[2 * n + 1]
        token = refs[-1]
        x, y, c = _place()
        for wi in range(n):
            for j, chip in enumerate(_other_chips(x, y)):
                _chip_copy(ins[wi].at[2 * chip[0] + chip[1]], land[wi].at[j], send_sems.at[3 * wi + j],
                           recv_sems.at[3 * wi + j], chip, c).start()
        token[...] = jnp.zeros_like(token)

    outs = pl.pallas_call(
        body, name=name,
        out_shape=[pltpu.SemaphoreType.DMA((3 * n,)), pltpu.SemaphoreType.DMA((3 * n,))]
        + [pltpu.HBM(g.shape, g.dtype) for g in grads] + [pltpu.HBM(l.shape, l.dtype) for l in lands]
        + [jax.ShapeDtypeStruct((8, 128), f32)],
        in_specs=[_HBM] * (2 * n),
        out_specs=[_SEM, _SEM] + [_HBM] * (2 * n) + [pl.BlockSpec(memory_space=pltpu.VMEM)],
        input_output_aliases={i: 2 + i for i in range(2 * n)},
        compiler_params=pltpu.CompilerParams(has_side_effects=_EFFECT),
    )(*[_hbm(g) for g in grads], *[_hbm(l) for l in lands])
    return (outs[0], outs[1]), list(outs[2:2 + n]), list(outs[2 + n:2 + 2 * n]), outs[-1]


def exchange_wait(grads, lands, sems, after, name):
    n = len(grads)

    def body(*refs):
        ins = refs[:n]
        land = refs[n:2 * n]
        send_sems, recv_sems = refs[2 * n], refs[2 * n + 1]
        x, y, c = _place()
        for wi in range(n):
            for j, chip in enumerate(_other_chips(x, y)):
                cp = _chip_copy(ins[wi].at[2 * chip[0] + chip[1]], land[wi].at[j], send_sems.at[3 * wi + j],
                                recv_sems.at[3 * wi + j], chip, c)
                cp.wait_send()
                cp.wait_recv()

    outs = pl.pallas_call(
        body, name=name,
        out_shape=[pltpu.HBM(g.shape, g.dtype) for g in grads] + [pltpu.HBM(l.shape, l.dtype) for l in lands],
        in_specs=[_HBM] * (2 * n) + [_SEM, _SEM, _ANY],
        out_specs=[_HBM] * (2 * n),
        input_output_aliases={i: i for i in range(2 * n)},
        compiler_params=pltpu.CompilerParams(has_side_effects=_EFFECT),
    )(*grads, *lands, sems[0], sems[1], after)
    return list(outs[:n]), list(outs[n:])


def swap_with_sibling(parts, name):
    nw = len(parts)

    def body(*refs):
        ins = refs[:nw]
        outs = refs[nw:2 * nw]
        send_sems, recv_sems = refs[2 * nw:]
        x, y, c = _place()
        copies = [pltpu.make_async_remote_copy(
            src_ref=ins[w], dst_ref=outs[w], send_sem=send_sems.at[w], recv_sem=recv_sems.at[w],
            device_id=(x, y, 1 - c), device_id_type=MESH) for w in range(nw)]
        for cp in copies:
            cp.start()
        for cp in copies:
            cp.wait()

    return pl.pallas_call(
        body, name=name,
        in_specs=[_ANY] * nw, out_specs=[_ANY] * nw,
        out_shape=[jax.ShapeDtypeStruct(p.shape, p.dtype) for p in parts],
        scratch_shapes=[pltpu.SemaphoreType.DMA((nw,)), pltpu.SemaphoreType.DMA((nw,))],
    )(*parts)


def gather_small(block, dep):
    r, lanes = block.shape

    def body(b_ref, dep_ref, o_ref, send_sems, recv_sems):
        x, y, c = _place()
        me = 4 * x + 2 * y + c
        o_ref[me] = b_ref[...]
        peers = []
        for m in range(1, N_DEV):
            px = 1 - x if m & 4 else x
            py = 1 - y if m & 2 else y
            pc = 1 - c if m & 1 else c
            peers.append((px, py, pc))
        for m, peer in enumerate(peers):
            pltpu.make_async_remote_copy(
                src_ref=b_ref, dst_ref=o_ref.at[me], send_sem=send_sems.at[m], recv_sem=recv_sems.at[m],
                device_id=peer, device_id_type=MESH).start()
        for m, (px, py, pc) in enumerate(peers):
            pltpu.make_async_remote_copy(
                src_ref=b_ref, dst_ref=o_ref.at[4 * px + 2 * py + pc], send_sem=send_sems.at[m],
                recv_sem=recv_sems.at[m], device_id=(px, py, pc), device_id_type=MESH).wait()

    return pl.pallas_call(
        body, name="gather_small",
        in_specs=[pl.BlockSpec(memory_space=pltpu.VMEM), _ANY], out_specs=pl.BlockSpec(memory_space=pltpu.VMEM),
        out_shape=jax.ShapeDtypeStruct((N_DEV, r, lanes), block.dtype),
        scratch_shapes=[pltpu.SemaphoreType.DMA((N_DEV - 1,)), pltpu.SemaphoreType.DMA((N_DEV - 1,))],
    )(block, dep)


def _adamw(w, g, m, v):
    m = ADAM_B1 * m + (1.0 - ADAM_B1) * g
    v = ADAM_B2 * v + (1.0 - ADAM_B2) * (g * g)
    m_hat = m / (1.0 - ADAM_B1 ** ADAM_STEP)
    v_hat = v / (1.0 - ADAM_B2 ** ADAM_STEP)
    delta = -ADAM_LR * (m_hat / (jnp.sqrt(v_hat) + ADAM_EPS) + ADAM_WD * w)
    return delta, m, v


def _ew_tile(rows):
    for cand in (256, 176, 128, 64, 32, 16, 8):
        if rows % cand == 0:
            return cand
    return rows


def sum_partials(chip, own, land, name):
    _, r, c = own.shape
    tr = _ew_tile(r)

    def body(k_ref, own_ref, p_ref, o_ref):
        o_ref[...] = ((own_ref[0].astype(f32) + p_ref[0].astype(f32)) + p_ref[1].astype(f32)) + p_ref[2].astype(f32)

    return pl.pallas_call(
        body, name=name,
        grid_spec=pltpu.PrefetchScalarGridSpec(
            num_scalar_prefetch=1, grid=(r // tr,),
            in_specs=[pl.BlockSpec((1, tr, c), lambda i, k: (k[0], i, 0)), pl.BlockSpec((3, tr, c), lambda i, k: (0, i, 0))],
            out_specs=pl.BlockSpec((tr, c), lambda i, k: (i, 0))),
        out_shape=jax.ShapeDtypeStruct((r, c), f32),
        compiler_params=_cparams(),
    )(chip, own, land)


def adamw_shard(p_mine, p_sibling, w, m, v, name):
    r, c = w.shape
    tr = _ew_tile(r)

    def body(a_ref, b_ref, w_ref, m_ref, v_ref, g_ref, d_ref, mo_ref, vo_ref):
        g = a_ref[...] + b_ref[...]
        delta, mn, vn = _adamw(w_ref[...], g, m_ref[...], v_ref[...])
        g_ref[...] = g
        d_ref[...] = delta
        mo_ref[...] = mn
        vo_ref[...] = vn

    blk = pl.BlockSpec((tr, c), lambda i: (i, 0))
    return pl.pallas_call(
        body, name=name, grid=(r // tr,),
        in_specs=[blk] * 5, out_specs=[blk] * 4,
        out_shape=[jax.ShapeDtypeStruct((r, c), f32)] * 4,
        compiler_params=_cparams(),
    )(p_mine, p_sibling, w, m, v)


def adamw_small(g8, w, m, v):
    _, r, lanes = g8.shape

    def body(g_ref, w_ref, m_ref, v_ref, go_ref, d_ref, mo_ref, vo_ref):
        g = g_ref[0]
        for i in range(1, N_DEV):
            g = g + g_ref[i]
        delta, mn, vn = _adamw(w_ref[...], g, m_ref[...], v_ref[...])
        go_ref[...] = g
        d_ref[...] = delta
        mo_ref[...] = mn
        vo_ref[...] = vn

    return pl.pallas_call(
        body, name="adamw_small",
        out_shape=[jax.ShapeDtypeStruct((r, lanes), f32)] * 4,
        compiler_params=_cparams(),
    )(g8, w, m, v)


def _size(shape):
    n = 1
    for e in shape:
        n *= e
    return n


def _pack_rows(shapes):
    rows = [-(-_size(s) // 1024) * 8 for s in shapes]
    return rows, sum(rows)


def _pack(arrs, shapes):
    rows, _ = _pack_rows(shapes)
    parts = [jnp.pad(a.reshape(-1).astype(f32), (0, r * 128 - _size(s))).reshape(r, 128)
             for a, s, r in zip(arrs, shapes, rows)]
    return jnp.concatenate(parts, axis=0)


def _unpack(block, shapes):
    rows, _ = _pack_rows(shapes)
    out, off = [], 0
    for s, r in zip(shapes, rows):
        out.append(block[off:off + r].reshape(-1)[:_size(s)].reshape(s))
        off += r
    return out


TRANSPOSED = ("ffn1_w_gate", "ffn1_w_up", "ffn2_w_gate", "ffn2_w_up")


def _shard2d(a, n):
    return a[0].T if n in TRANSPOSED else a[0]


def _unshard(a, n):
    return (a.T if n in TRANSPOSED else a)[None]


BIG = ("ffn1_w_gate", "ffn1_w_up", "ffn1_w_down", "w_in", "w_branch_a", "w_branch_b", "w_out",
       "ffn2_w_gate", "ffn2_w_up", "ffn2_w_down")
SMALL = ("ffn1_norm", "mix_norm", "b_in", "sgu_norm_g", "sgu_norm_b", "sgu_w_s", "sgu_b_s", "ret_decay_logit",
         "ffn2_norm", "final_norm")
WEIGHTS = ("ffn1_norm", "ffn1_w_gate", "ffn1_w_up", "ffn1_w_down", "mix_norm", "w_in", "b_in", "sgu_norm_g",
           "sgu_norm_b", "sgu_w_s", "sgu_b_s", "ret_decay_logit", "w_branch_a", "w_branch_b", "w_out", "ffn2_norm",
           "ffn2_w_gate", "ffn2_w_up", "ffn2_w_down", "final_norm")


def kernel(x, ffn1_norm, ffn1_w_gate, ffn1_w_up, ffn1_w_down, mix_norm, w_in, b_in, sgu_norm_g, sgu_norm_b, sgu_w_s, sgu_b_s, ret_decay_logit, w_branch_a, w_branch_b, w_out, ffn2_norm, ffn2_w_gate, ffn2_w_up, ffn2_w_down, final_norm, loss_target, m_ffn1_norm, m_ffn1_w_gate, m_ffn1_w_up, m_ffn1_w_down, m_mix_norm, m_w_in, m_b_in, m_sgu_norm_g, m_sgu_norm_b, m_sgu_w_s, m_sgu_b_s, m_ret_decay_logit, m_w_branch_a, m_w_branch_b, m_w_out, m_ffn2_norm, m_ffn2_w_gate, m_ffn2_w_up, m_ffn2_w_down, m_final_norm, v_ffn1_norm, v_ffn1_w_gate, v_ffn1_w_up, v_ffn1_w_down, v_mix_norm, v_w_in, v_b_in, v_sgu_norm_g, v_sgu_norm_b, v_sgu_w_s, v_sgu_b_s, v_ret_decay_logit, v_w_branch_a, v_w_branch_b, v_w_out, v_ffn2_norm, v_ffn2_w_gate, v_ffn2_w_up, v_ffn2_w_down, v_final_norm):
    p = dict(ffn1_norm=ffn1_norm, ffn1_w_gate=ffn1_w_gate, ffn1_w_up=ffn1_w_up, ffn1_w_down=ffn1_w_down,
             mix_norm=mix_norm, w_in=w_in, b_in=b_in, sgu_norm_g=sgu_norm_g, sgu_norm_b=sgu_norm_b, sgu_w_s=sgu_w_s,
             sgu_b_s=sgu_b_s, ret_decay_logit=ret_decay_logit, w_branch_a=w_branch_a, w_branch_b=w_branch_b,
             w_out=w_out, ffn2_norm=ffn2_norm, ffn2_w_gate=ffn2_w_gate, ffn2_w_up=ffn2_w_up, ffn2_w_down=ffn2_w_down,
             final_norm=final_norm)
    mom = dict(ffn1_norm=m_ffn1_norm, ffn1_w_gate=m_ffn1_w_gate, ffn1_w_up=m_ffn1_w_up, ffn1_w_down=m_ffn1_w_down,
               mix_norm=m_mix_norm, w_in=m_w_in, b_in=m_b_in, sgu_norm_g=m_sgu_norm_g, sgu_norm_b=m_sgu_norm_b,
               sgu_w_s=m_sgu_w_s, sgu_b_s=m_sgu_b_s, ret_decay_logit=m_ret_decay_logit, w_branch_a=m_w_branch_a,
               w_branch_b=m_w_branch_b, w_out=m_w_out, ffn2_norm=m_ffn2_norm, ffn2_w_gate=m_ffn2_w_gate,
               ffn2_w_up=m_ffn2_w_up, ffn2_w_down=m_ffn2_w_down, final_norm=m_final_norm)
    var = dict(ffn1_norm=v_ffn1_norm, ffn1_w_gate=v_ffn1_w_gate, ffn1_w_up=v_ffn1_w_up, ffn1_w_down=v_ffn1_w_down,
               mix_norm=v_mix_norm, w_in=v_w_in, b_in=v_b_in, sgu_norm_g=v_sgu_norm_g, sgu_norm_b=v_sgu_norm_b,
               sgu_w_s=v_sgu_w_s, sgu_b_s=v_sgu_b_s, ret_decay_logit=v_ret_decay_logit, w_branch_a=v_w_branch_a,
               w_branch_b=v_w_branch_b, w_out=v_w_out, ffn2_norm=v_ffn2_norm, ffn2_w_gate=v_ffn2_w_gate,
               ffn2_w_up=v_ffn2_w_up, ffn2_w_down=v_ffn2_w_down, final_norm=v_final_norm)

    xs = x[0]
    tgt = loss_target[0]
    t, d = xs.shape
    dk = d // RET_HEADS
    tm = _row_tile(t)

    shards2d = {n: _shard2d(p[n], n) for n in BIG}
    chip = (2 * lax.axis_index("x") + lax.axis_index("y")).astype(jnp.int32).reshape(1)
    groups = {"ffn1": ("ffn1_w_gate", "ffn1_w_up", "ffn1_w_down"), "in": ("w_in",),
              "mix": ("w_branch_a", "w_branch_b", "w_out"), "ffn2": ("ffn2_w_gate", "ffn2_w_up", "ffn2_w_down")}
    bufs = [jnp.broadcast_to(shards2d[n].astype(bf16)[None], (N_CHIPS,) + shards2d[n].shape) for n in BIG]
    sems, bufs, tok = gather_start(bufs, [[BIG.index(n) for n in groups[g]] for g in ("ffn1", "in", "mix", "ffn2")])
    gsem = dict(zip(("ffn1", "in", "mix", "ffn2"), sems))
    pending = dict(zip(BIG, bufs))

    def arrive(gs, after):
        got = []
        for g in gs:
            got += gather_wait([pending[n] for n in groups[g]], gsem[g], after, "gather_wait_" + g)
        return gather_forward(got, "gather_forward_" + gs[0])

    bin4 = b_in.reshape(N_CHIPS, 1, 2 * d)
    ws_b = sgu_w_s[0].astype(bf16)
    bs_c = sgu_b_s[0][:, :, None]
    cols, mats, cdec, cos, sin = retention_constants(ret_decay_logit[0], t, dk)

    wg1, wu1, wd1 = arrive(["ffn1"], tok)
    x1, g1, u1 = ffn_fwd(xs, ffn1_norm, wg1, wu1, wd1, "ffn1_fwd")
    win, = arrive(["in"], x1)
    proj, hb2 = inproj_fwd(x1, mix_norm, win, bin4, cos, sin)
    a = sgu_fwd(proj, sgu_norm_g, sgu_norm_b, ws_b, bs_c)
    r, rn = ret_fwd(proj, cols, mats, cdec)
    wa, wb, wo, wg2, wu2, wd2 = arrive(["mix", "ffn2"], rn)
    wa, wb, wo = [w.reshape(d, d) for w in (wa, wb, wo)]
    x2, ba, br = mix_fwd(a, rn, proj, wa, wb, wo, x1)
    x3, g2, u2 = ffn_fwd(x2, ffn2_norm, wg2, wu2, wd2, "ffn2_fwd")
    loss_blk, dx3, d_final = loss_head(x3, final_norm.reshape(1, d), tgt)

    sent = {}
    dx2, dg2, du2, act2, hb3, dyb2, d_ffn2n = ffn_bwd_act(dx3, x2, ffn2_norm, g2, u2, wg2, wu2, wd2, "ffn2_bwd_act", tok)
    sent["ffn2"] = exchange_start(list(ffn_weight_grads(hb3, dyb2, dg2, du2, act2, "ffn2_grad")), "exchange_start_ffn2")
    da, drn, dga, dgb, mixb, dba, dbr, dx2b = mix_bwd_act(dx2, ba, br, proj, wa, wb, wo, sent["ffn2"][3])
    tg = min(t, 2048)
    row = pl.BlockSpec((tg, d), lambda s, i: (i, 0))

    def square_grad(xa, ya, name):
        return tn_matmul(xa, [ya], row, [row], 1, d, [d], t, tg, name).reshape(N_CHIPS, d // N_CHIPS, d)

    sent["mix"] = exchange_start([square_grad(a, dba, "grad_w_branch_a"), square_grad(rn, dbr, "grad_w_branch_b"),
                                  square_grad(mixb, dx2b, "grad_w_out")], "exchange_start_mix")
    dua, dva, d_ws, d_bs, d_sng, d_snb = sgu_bwd(da, proj, sgu_norm_g, sgu_norm_b, ws_b, bs_c, sent["mix"][3])
    dq, dkr, dv, dgr, dlg = ret_bwd(drn, r, proj, cols, mats, cdec, cos, sin)
    segs = [dua, dva, dq, dkr, dv, dgr, dga, dgb]
    dx1, d_bin, d_mixn = inproj_bwd_act(segs, win, x1, mix_norm, dx2)
    sent["in"] = exchange_start([jnp.concatenate(
        [tn_matmul(hb2, [segs[2 * s], segs[2 * s + 1]], row, [row, row], 1, d, [d, d], t, tg, "grad_w_in_%d" % s)
         for s in range(N_CHIPS)], axis=0)], "exchange_start_in")
    grad_x, dg1, du1, act1, hb1, dyb1, d_ffn1n = ffn_bwd_act(dx1, xs, ffn1_norm, g1, u1, wg1, wu1, wd1, "ffn1_bwd_act",
                                                              sent["in"][3])
    sent["ffn1"] = exchange_start(list(ffn_weight_grads(hb1, dyb1, dg1, du1, act1, "ffn1_grad")), "exchange_start_ffn1")

    out_g, out_d, out_m, out_v = {}, {}, {}, {}

    def finish(g, after):
        gsems, own, lands, _ = sent[g]
        own, lands = exchange_wait(own, lands, gsems, after, "exchange_wait_" + g)
        plane = [sum_partials(chip, o, l, "sum_" + n) for n, o, l in zip(groups[g], own, lands)]
        other = swap_with_sibling(plane, "swap_" + g)
        for n, mine, sib in zip(groups[g], plane, other):
            res = adamw_shard(mine, sib, shards2d[n], _shard2d(mom[n], n), _shard2d(var[n], n), "adamw_" + n)
            out_g[n], out_d[n], out_m[n], out_v[n] = [_unshard(o, n) for o in res]
        return out_g[groups[g][-1]]

    after = sent["ffn1"][3]
    for g in ("ffn2", "mix", "in"):
        after = finish(g, after)

    dlogit = dlg[:, 0:2, 0].T * jax.nn.sigmoid(-ret_decay_logit[0].astype(f32))
    small_g = dict(ffn1_norm=d_ffn1n, mix_norm=d_mixn, b_in=d_bin, sgu_norm_g=d_sng, sgu_norm_b=d_snb, sgu_w_s=d_ws,
                   sgu_b_s=d_bs, ret_decay_logit=dlogit, ffn2_norm=d_ffn2n, final_norm=d_final)
    shapes = [p[n].shape for n in SMALL]
    g8 = gather_small(_pack([small_g[n] for n in SMALL], shapes), after)
    sg, sd, sm, sv = adamw_small(g8, _pack([p[n] for n in SMALL], shapes), _pack([mom[n] for n in SMALL], shapes),
                                 _pack([var[n] for n in SMALL], shapes))
    for res, blockv in ((out_g, sg), (out_d, sd), (out_m, sm), (out_v, sv)):
        for n, val in zip(SMALL, _unpack(blockv, shapes)):
            res[n] = val
    finish("ffn1", sg)

    loss = lax.psum(loss_blk[0, 0], ("x", "y", "c"))
    return (loss, grad_x[None], *[out_g[n] for n in WEIGHTS], *[out_d[n] for n in WEIGHTS],
            *[out_m[n] for n in WEIGHTS], *[out_v[n] for n in WEIGHTS])
```

```python
import functools

import jax
import jax.numpy as jnp
from jax import lax
from jax.experimental import pallas as pl
from jax.experimental.pallas import tpu as pltpu

f32 = jnp.float32
bf16 = jnp.bfloat16

SGU_CHUNK = 128
CHUNK = 128
RET_HEADS = 4
SGU_GROUPS = 4
ROPE_BASE = 10000.0
NORM_EPS = 1e-6
ADAM_LR = 0.001
ADAM_B1 = 0.9
ADAM_B2 = 0.999
ADAM_EPS = 1e-08
ADAM_WD = 0.01
ADAM_STEP = 10
N_CHIPS = 4
N_DEV = 8
MESH = pl.DeviceIdType.MESH
VMEM_LIMIT = 52 * 1024 * 1024
VMEM_LIMIT_WIDE = 62 * 1024 * 1024

_NT = (((1,), (1,)), ((), ()))
_TN = (((0,), (0,)), ((), ()))


def _cparams(limit=None):
    return pltpu.CompilerParams(vmem_limit_bytes=VMEM_LIMIT if limit is None else limit)


def _row_tile(t):
    return 512 if t >= 2048 else t // 2


def _dot(a, b):
    return jnp.dot(a, b, preferred_element_type=f32)


def _dot_nt(a, b):
    return lax.dot_general(a, b, _NT, preferred_element_type=f32)


def _dot_tn(a, b):
    return lax.dot_general(a, b, _TN, preferred_element_type=f32)


def _rms(x, g):
    r = lax.rsqrt(jnp.mean(x * x, axis=-1, keepdims=True) + NORM_EPS)
    xh = x * r
    return xh * g, xh, r


def _rms_bwd(dy, xh, r, g):
    dxh = dy * g
    return r * (dxh - xh * jnp.mean(dxh * xh, axis=-1, keepdims=True))


def _sigmoid(x):
    return jax.nn.sigmoid(x)


def _dsilu(g, sg):
    return sg * (1.0 + g * (1.0 - sg))


def _gelu(x):
    return 0.5 * x * (1.0 + lax.erf(x * 0.7071067811865476))


def _dgelu(x):
    return 0.5 * (1.0 + lax.erf(x * 0.7071067811865476)) + x * jnp.exp(-0.5 * x * x) * 0.3989422804014327


def _acc_out(ref, first, val):
    @pl.when(first)
    def _():
        ref[...] = val

    @pl.when(jnp.logical_not(first))
    def _():
        ref[...] += val


def ffn_fwd(x, ng, wg, wu, wd, name):
    t, d = x.shape
    s4, fs, _ = wg.shape
    tm = _row_tile(t)

    def body(x_ref, ng_ref, wg_ref, wu_ref, wd_ref, xo_ref, g_ref, u_ref, h_scr, acc_scr):
        s = pl.program_id(1)

        @pl.when(s == 0)
        def _():
            y, _, _ = _rms(x_ref[...], ng_ref[...])
            h_scr[...] = y.astype(bf16)
            acc_scr[...] = jnp.zeros_like(acc_scr)

        h = h_scr[...]
        g = _dot_nt(h, wg_ref[0])
        u = _dot_nt(h, wu_ref[0])
        g_ref[0] = g.astype(bf16)
        u_ref[0] = u.astype(bf16)
        act = (g * _sigmoid(g) * u).astype(bf16)
        acc_scr[...] += _dot(act, wd_ref[0])

        @pl.when(s == s4 - 1)
        def _():
            xo_ref[...] = x_ref[...] + 0.5 * acc_scr[...]

    return pl.pallas_call(
        body, name=name, grid=(t // tm, s4),
        in_specs=[pl.BlockSpec((tm, d), lambda i, s: (i, 0)), pl.BlockSpec((1, d), lambda i, s: (0, 0)),
                  pl.BlockSpec((1, fs, d), lambda i, s: (s, 0, 0)), pl.BlockSpec((1, fs, d), lambda i, s: (s, 0, 0)),
                  pl.BlockSpec((1, fs, d), lambda i, s: (s, 0, 0))],
        out_specs=[pl.BlockSpec((tm, d), lambda i, s: (i, 0)), pl.BlockSpec((1, tm, fs), lambda i, s: (s, i, 0)),
                   pl.BlockSpec((1, tm, fs), lambda i, s: (s, i, 0))],
        out_shape=[jax.ShapeDtypeStruct((t, d), f32), jax.ShapeDtypeStruct((s4, t, fs), bf16),
                   jax.ShapeDtypeStruct((s4, t, fs), bf16)],
        scratch_shapes=[pltpu.VMEM((tm, d), bf16), pltpu.VMEM((tm, d), f32)],
        compiler_params=_cparams(),
    )(x, ng, wg, wu, wd)


def ffn_bwd_act(dxo, x, ng, g, u, wg, wu, wd, name, dep):
    t, d = x.shape
    s4, fs, _ = wg.shape
    tm = _row_tile(t)

    def body(dxo_ref, x_ref, ng_ref, g_ref, u_ref, wg_ref, wu_ref, wd_ref, dep_ref,
             dx_ref, dg_ref, du_ref, act_ref, hb_ref, dyb_ref, dng_ref, dy_scr, acc_scr):
        i = pl.program_id(0)
        s = pl.program_id(1)

        @pl.when(s == 0)
        def _():
            dyb = (0.5 * dxo_ref[...]).astype(bf16)
            dy_scr[...] = dyb
            dyb_ref[...] = dyb
            acc_scr[...] = jnp.zeros_like(acc_scr)

        dact = _dot_nt(dy_scr[...], wd_ref[0])
        gg = g_ref[0].astype(f32)
        uu = u_ref[0].astype(f32)
        sg = _sigmoid(gg)
        sil = gg * sg
        dgb = (dact * uu * _dsilu(gg, sg)).astype(bf16)
        dub = (dact * sil).astype(bf16)
        dg_ref[0] = dgb
        du_ref[0] = dub
        act_ref[0] = (sil * uu).astype(bf16)
        acc_scr[...] += _dot(dgb, wg_ref[0]) + _dot(dub, wu_ref[0])

        @pl.when(s == s4 - 1)
        def _():
            y, xh, r = _rms(x_ref[...], ng_ref[...])
            hb_ref[...] = y.astype(bf16)
            dh = acc_scr[...]
            dx_ref[...] = dxo_ref[...] + _rms_bwd(dh, xh, r, ng_ref[...])
            _acc_out(dng_ref, i == 0, jnp.sum(dh * xh, axis=0, keepdims=True))

    row = lambda i, s: (i, 0)
    shard = lambda i, s: (s, i, 0)
    wsp = lambda i, s: (s, 0, 0)
    return pl.pallas_call(
        body, name=name, grid=(t // tm, s4),
        in_specs=[pl.BlockSpec((tm, d), row), pl.BlockSpec((tm, d), row), pl.BlockSpec((1, d), lambda i, s: (0, 0)),
                  pl.BlockSpec((1, tm, fs), shard), pl.BlockSpec((1, tm, fs), shard),
                  pl.BlockSpec((1, fs, d), wsp), pl.BlockSpec((1, fs, d), wsp), pl.BlockSpec((1, fs, d), wsp), _ANY],
        out_specs=[pl.BlockSpec((tm, d), row), pl.BlockSpec((1, tm, fs), shard), pl.BlockSpec((1, tm, fs), shard),
                   pl.BlockSpec((1, tm, fs), shard), pl.BlockSpec((tm, d), row), pl.BlockSpec((tm, d), row),
                   pl.BlockSpec((1, d), lambda i, s: (0, 0))],
        out_shape=[jax.ShapeDtypeStruct((t, d), f32), jax.ShapeDtypeStruct((s4, t, fs), bf16),
                   jax.ShapeDtypeStruct((s4, t, fs), bf16), jax.ShapeDtypeStruct((s4, t, fs), bf16),
                   jax.ShapeDtypeStruct((t, d), bf16), jax.ShapeDtypeStruct((t, d), bf16),
                   jax.ShapeDtypeStruct((1, d), f32)],
        scratch_shapes=[pltpu.VMEM((tm, d), bf16), pltpu.VMEM((tm, d), f32)],
        compiler_params=_cparams(VMEM_LIMIT_WIDE),
    )(dxo, x, ng, g, u, wg, wu, wd, dep)


def tn_matmul(xs, ys, x_spec, y_specs, n_shards, k1, k2s, t, tm, name):
    k2 = sum(k2s)
    ny = len(ys)

    def body(*refs):
        x_ref = refs[0]
        y_refs = refs[1:1 + ny]
        o_ref = refs[1 + ny]
        acc = refs[2 + ny]
        i = pl.program_id(1)
        xb = x_ref[0] if len(x_ref.shape) == 3 else x_ref[...]
        off = 0
        for y_ref, w in zip(y_refs, k2s):
            yb = y_ref[0] if len(y_ref.shape) == 3 else y_ref[...]
            part = _dot_tn(xb, yb)
            sl = (slice(None), slice(off, off + w))

            @pl.when(i == 0)
            def _(part=part, sl=sl):
                acc[sl] = part

            @pl.when(i > 0)
            def _(part=part, sl=sl):
                acc[sl] += part

            off += w

        @pl.when(i == t // tm - 1)
        def _():
            o_ref[0] = acc[...].astype(bf16)

    return pl.pallas_call(
        body, name=name, grid=(n_shards, t // tm),
        in_specs=[x_spec] + list(y_specs),
        out_specs=pl.BlockSpec((1, k1, k2), lambda s, i: (s, 0, 0)),
        out_shape=jax.ShapeDtypeStruct((n_shards, k1, k2), bf16),
        scratch_shapes=[pltpu.VMEM((k1, k2), f32)],
        compiler_params=_cparams(),
    )(xs, *ys)


def _pair_shards(w):
    s4, fs, d = w.shape
    return w.reshape(s4 // 2, 2 * fs, d)


def ffn_weight_grads(hb, dyb, dg, du, act, name):
    t, d = hb.shape
    s2, _, fs2 = dg.shape
    tm = t
    row = pl.BlockSpec((tm, d), lambda s, i: (i, 0))
    shard = pl.BlockSpec((1, tm, fs2), lambda s, i: (s, i, 0))
    gwg = tn_matmul(dg, [hb], shard, [row], s2, fs2, [d], t, tm, name + "_wg")
    gwu = tn_matmul(du, [hb], shard, [row], s2, fs2, [d], t, tm, name + "_wu")
    gwd = tn_matmul(act, [dyb], shard, [row], s2, fs2, [d], t, tm, name + "_wd")
    return [g.reshape(2 * s2, fs2 // 2, d) for g in (gwg, gwu, gwd)]


def inproj_fwd(x1, ng, win, bin4, cos, sin):
    t, d = x1.shape
    s4, _, w2 = win.shape
    tm = _row_tile(t)
    dk = d // RET_HEADS
    scale = dk ** -0.5

    def body(x_ref, ng_ref, w_ref, b_ref, cos_ref, sin_ref, p_ref, hb_ref, h_scr):
        s = pl.program_id(1)

        @pl.when(s == 0)
        def _():
            y, _, _ = _rms(x_ref[...], ng_ref[...])
            h_scr[...] = y.astype(bf16)
            hb_ref[...] = y.astype(bf16)

        p = _dot(h_scr[...], w_ref[0]) + b_ref[0]

        @pl.when(s != 1)
        def _():
            p_ref[0] = p.astype(bf16)

        @pl.when(s == 1)
        def _():
            cs, sn = cos_ref[...], sin_ref[...]
            for e in range(2 * RET_HEADS):
                cols = slice(e * dk, (e + 1) * dk)
                rot = _rot(p[:, cols], cs, sn)
                p_ref[0, :, cols] = (rot if e < RET_HEADS else rot * scale).astype(bf16)

    tab = pl.BlockSpec((tm, dk // 2), lambda i, s: (i, 0))
    return pl.pallas_call(
        body, name="inproj_fwd", grid=(t // tm, s4),
        in_specs=[pl.BlockSpec((tm, d), lambda i, s: (i, 0)), pl.BlockSpec((1, d), lambda i, s: (0, 0)),
                  pl.BlockSpec((1, d, w2), lambda i, s: (s, 0, 0)), pl.BlockSpec((1, 1, w2), lambda i, s: (s, 0, 0)),
                  tab, tab],
        out_specs=[pl.BlockSpec((1, tm, w2), lambda i, s: (s, i, 0)), pl.BlockSpec((tm, d), lambda i, s: (i, 0))],
        out_shape=[jax.ShapeDtypeStruct((s4, t, w2), bf16), jax.ShapeDtypeStruct((t, d), bf16)],
        scratch_shapes=[pltpu.VMEM((tm, d), bf16)],
        compiler_params=_cparams(),
    )(x1, ng, win, bin4, cos, sin)


def _sgu_norm(va, ng, nb):
    gv = _gelu(va)
    mu = jnp.mean(gv, axis=-1, keepdims=True)
    xc = gv - mu
    rstd = lax.rsqrt(jnp.mean(xc * xc, axis=-1, keepdims=True) + NORM_EPS)
    xh = xc * rstd
    return xh, rstd, (xh * ng + nb).astype(bf16)


def sgu_fwd(proj, ng, nb, ws, bs):
    _, t, w2 = proj.shape
    d = w2 // 2
    gd = d // SGU_GROUPS
    tm = _row_tile(t)

    def body(p_ref, ng_ref, nb_ref, ws_ref, bs_ref, a_ref):
        ua = p_ref[0, :, 0:d].astype(f32)
        va = p_ref[0, :, d:w2].astype(f32)
        gu = _gelu(ua)
        _, _, vn = _sgu_norm(va, ng_ref[...], nb_ref[...])
        for c in range(tm // SGU_CHUNK):
            rows = slice(c * SGU_CHUNK, (c + 1) * SGU_CHUNK)
            for g in range(SGU_GROUPS):
                cols = slice(g * gd, (g + 1) * gd)
                sg = _dot(ws_ref[g], vn[rows, cols]) + bs_ref[g]
                a_ref[rows, cols] = (gu[rows, cols] * sg).astype(bf16)

    return pl.pallas_call(
        body, name="sgu_fwd", grid=(t // tm,),
        in_specs=[pl.BlockSpec((1, tm, w2), lambda i: (0, i, 0)), pl.BlockSpec((1, d), lambda i: (0, 0)),
                  pl.BlockSpec((1, d), lambda i: (0, 0)), pl.BlockSpec((SGU_GROUPS, SGU_CHUNK, SGU_CHUNK), lambda i: (0, 0, 0)),
                  pl.BlockSpec((SGU_GROUPS, SGU_CHUNK, 1), lambda i: (0, 0, 0))],
        out_specs=pl.BlockSpec((tm, d), lambda i: (i, 0)),
        out_shape=jax.ShapeDtypeStruct((t, d), bf16),
        compiler_params=_cparams(),
    )(proj, ng, nb, ws, bs)


def sgu_bwd(da, proj, ng, nb, ws, bs, dep):
    _, t, w2 = proj.shape
    d = w2 // 2
    gd = d // SGU_GROUPS
    tm = _row_tile(t)

    def body(da_ref, p_ref, ng_ref, nb_ref, ws_ref, bs_ref, dep_ref,
             dua_ref, dva_ref, dws_ref, dbs_ref, dng_ref, dnb_ref, dvn_scr):
        i = pl.program_id(0)
        ua = p_ref[0, :, 0:d].astype(f32)
        va = p_ref[0, :, d:w2].astype(f32)
        gu = _gelu(ua)
        xh, rstd, vn = _sgu_norm(va, ng_ref[...], nb_ref[...])
        dad = da_ref[...].astype(f32)
        dsb = (dad * gu).astype(bf16)
        for c in range(tm // SGU_CHUNK):
            rows = slice(c * SGU_CHUNK, (c + 1) * SGU_CHUNK)
            for g in range(SGU_GROUPS):
                cols = slice(g * gd, (g + 1) * gd)
                sg = _dot(ws_ref[g], vn[rows, cols]) + bs_ref[g]
                dua_ref[rows, cols] = (dad[rows, cols] * sg * _dgelu(ua[rows, cols])).astype(bf16)
                ds = dsb[rows, cols]
                dvn_scr[rows, cols] = _dot_tn(ws_ref[g], ds)
                dw = _dot_nt(ds, vn[rows, cols])
                db = jnp.sum(ds.astype(f32), axis=1, keepdims=True)
                if c == 0:
                    _acc_out(dws_ref.at[g], i == 0, dw)
                    _acc_out(dbs_ref.at[g], i == 0, db)
                else:
                    dws_ref[g] += dw
                    dbs_ref[g] += db
        dvn = dvn_scr[...]
        _acc_out(dng_ref, i == 0, jnp.sum(dvn * xh, axis=0, keepdims=True))
        _acc_out(dnb_ref, i == 0, jnp.sum(dvn, axis=0, keepdims=True))
        dxh = dvn * ng_ref[...]
        dgv = rstd * (dxh - jnp.mean(dxh, axis=-1, keepdims=True) - xh * jnp.mean(dxh * xh, axis=-1, keepdims=True))
        dva_ref[...] = (dgv * _dgelu(va)).astype(bf16)

    row = pl.BlockSpec((tm, d), lambda i: (i, 0))
    vec = pl.BlockSpec((1, d), lambda i: (0, 0))
    wsp = pl.BlockSpec((SGU_GROUPS, SGU_CHUNK, SGU_CHUNK), lambda i: (0, 0, 0))
    bsp = pl.BlockSpec((SGU_GROUPS, SGU_CHUNK, 1), lambda i: (0, 0, 0))
    return pl.pallas_call(
        body, name="sgu_bwd", grid=(t // tm,),
        in_specs=[row, pl.BlockSpec((1, tm, w2), lambda i: (0, i, 0)), vec, vec, wsp, bsp, _ANY],
        out_specs=[row, row, wsp, bsp, vec, vec],
        out_shape=[jax.ShapeDtypeStruct((t, d), bf16), jax.ShapeDtypeStruct((t, d), bf16),
                   jax.ShapeDtypeStruct((SGU_GROUPS, SGU_CHUNK, SGU_CHUNK), f32), jax.ShapeDtypeStruct((SGU_GROUPS, SGU_CHUNK, 1), f32),
                   jax.ShapeDtypeStruct((1, d), f32), jax.ShapeDtypeStruct((1, d), f32)],
        scratch_shapes=[pltpu.VMEM((tm, d), f32)],
        compiler_params=_cparams(),
    )(da, proj, ng, nb, ws, bs, dep)


def retention_constants(decay_logit, t, dk):
    lg = jax.nn.log_sigmoid(decay_logit.astype(f32))
    lgf = lg[0][:, None]
    lgb = lg[1][:, None]
    idx = jnp.arange(CHUNK, dtype=f32)[None, :]
    af = jnp.exp((idx + 1.0) * lgf)
    ab = jnp.exp((CHUNK - idx) * lgb)
    kf = jnp.exp((CHUNK - 1.0 - idx) * lgf)
    kb = jnp.exp(idx * lgb)
    cols = jnp.stack([af, ab, kf, kb, af * (idx + 1.0), ab * (CHUNK - idx), kf * (CHUNK - 1.0 - idx), kb * idx], axis=1)
    cols = cols[..., None]
    diff = idx[0][:, None] - idx[0][None, :]
    dfm = jnp.where(diff >= 0, jnp.exp(jnp.maximum(diff, 0.0)[None] * lgf[:, :, None]), 0.0)
    dbm = jnp.where(diff < 0, jnp.exp(jnp.maximum(-diff, 0.0)[None] * lgb[:, :, None]), 0.0)
    mats = jnp.stack([dfm + dbm, dfm * diff[None], dbm * (-diff)[None]], axis=1)
    cdec = jnp.stack([jnp.broadcast_to(jnp.exp(CHUNK * lgf), (RET_HEADS, dk)),
                      jnp.broadcast_to(jnp.exp(CHUNK * lgb), (RET_HEADS, dk))], axis=1)
    theta = ROPE_BASE ** (-jnp.arange(0, dk, 2, dtype=f32) / dk)
    ang = jnp.arange(t, dtype=f32)[:, None] * theta[None, :]
    return cols, mats, cdec, jnp.cos(ang), jnp.sin(ang)


def _rot(tr, cos, sin):
    half = tr.shape[-1] // 2
    t1 = tr[:, :half]
    t2 = tr[:, half:]
    return jnp.concatenate([t1 * cos - t2 * sin, t2 * cos + t1 * sin], axis=-1)


def _rot_inv(dt, cos, sin):
    half = dt.shape[-1] // 2
    d1 = dt[:, :half]
    d2 = dt[:, half:]
    return jnp.concatenate([d1 * cos + d2 * sin, d2 * cos - d1 * sin], axis=-1)


def _ret_specs(t, d, dk, rt):
    nr = t // rt
    hq = d // dk

    def blk(p, n):
        return (1 - p) * (nr - 1 - n) + p * n

    q_spec = pl.BlockSpec((1, rt, dk), lambda h, p, n: (1, blk(p, n), h))
    k_spec = pl.BlockSpec((1, rt, dk), lambda h, p, n: (1, blk(p, n), hq + h))
    v_spec = pl.BlockSpec((1, rt, dk), lambda h, p, n: (2, blk(p, n), h))
    g_spec = pl.BlockSpec((1, rt, dk), lambda h, p, n: (2, blk(p, n), hq + h))
    tab_spec = pl.BlockSpec((rt, dk // 2), lambda h, p, n: (blk(p, n), 0))
    cols_spec = pl.BlockSpec((1, 8, CHUNK, 1), lambda h, p, n: (h, 0, 0, 0))
    mats_spec = pl.BlockSpec((1, 3, CHUNK, CHUNK), lambda h, p, n: (h, 0, 0, 0))
    cdec_spec = pl.BlockSpec((1, 2, dk), lambda h, p, n: (h, 0, 0))
    in_row = pl.BlockSpec((rt, dk), lambda h, p, n: (blk(p, n), h))
    out_row = pl.BlockSpec((rt, dk), lambda h, p, n: (p * n, h))
    return nr, blk, q_spec, k_spec, v_spec, g_spec, tab_spec, cols_spec, mats_spec, cdec_spec, in_row, out_row


def ret_fwd(proj, cols, mats, cdec):
    _, t, w2 = proj.shape
    d = w2 // 2
    dk = d // RET_HEADS
    rt = _row_tile(t)
    cpt = rt // CHUNK
    nr, blk, q_spec, k_spec, v_spec, g_spec, _, cols_spec, mats_spec, cdec_spec, _, out_row = _ret_specs(t, d, dk, rt)

    def body(q_ref, k_ref, v_ref, g_ref, cols_ref, mats_ref, cdec_ref, r_ref, rn_ref, sb_scr, st):
        p = pl.program_id(1)
        n = pl.program_id(2)
        af, ab, kf, kb = cols_ref[0, 0], cols_ref[0, 1], cols_ref[0, 2], cols_ref[0, 3]
        cf = cdec_ref[0, 0:1, :]
        cb = cdec_ref[0, 1:2, :]

        @pl.when(n == 0)
        def _():
            st[...] = jnp.zeros_like(st)

        @pl.when(p == 0)
        def _():
            for j in reversed(range(cpt)):
                rows = slice(j * CHUNK, (j + 1) * CHUNK)
                ch = blk(p, n) * cpt + j
                kk = k_ref[0, rows, :].astype(f32)
                sb_scr[ch] = st[...].astype(bf16)
                st[...] = st[...] * cb + _dot_tn((kk * kb).astype(bf16), v_ref[0, rows, :])

        @pl.when(p == 1)
        def _():
            for j in range(cpt):
                rows = slice(j * CHUNK, (j + 1) * CHUNK)
                ch = blk(p, n) * cpt + j
                qb = q_ref[0, rows, :]
                kkb = k_ref[0, rows, :]
                q = qb.astype(f32)
                kk = kkb.astype(f32)
                v = v_ref[0, rows, :]
                pm = (_dot_nt(qb, kkb) * mats_ref[0, 0]).astype(bf16)
                out = (_dot(pm, v) + _dot((q * af).astype(bf16), st[...].astype(bf16))
                       + _dot((q * ab).astype(bf16), sb_scr[ch]))
                st[...] = st[...] * cf + _dot_tn((kk * kf).astype(bf16), v)
                rhat = out * lax.rsqrt(jnp.mean(out * out, axis=-1, keepdims=True) + NORM_EPS)
                gg = g_ref[0, rows, :].astype(f32)
                r_ref[rows, :] = out.astype(bf16)
                rn_ref[rows, :] = (rhat * gg * _sigmoid(gg)).astype(bf16)

    return pl.pallas_call(
        body, name="ret_fwd", grid=(RET_HEADS, 2, nr),
        in_specs=[q_spec, k_spec, v_spec, g_spec, cols_spec, mats_spec, cdec_spec],
        out_specs=[out_row, out_row],
        out_shape=[jax.ShapeDtypeStruct((t, d), bf16), jax.ShapeDtypeStruct((t, d), bf16)],
        scratch_shapes=[pltpu.VMEM((t // CHUNK, dk, dk), bf16), pltpu.VMEM((dk, dk), f32)],
        compiler_params=_cparams(),
    )(proj, proj, proj, proj, cols, mats, cdec)


def ret_bwd(drn, r, proj, cols, mats, cdec, cos, sin):
    _, t, w2 = proj.shape
    d = w2 // 2
    dk = d // RET_HEADS
    rt = _row_tile(t)
    cpt = rt // CHUNK
    nr, blk, q_spec, k_spec, v_spec, g_spec, tab_spec, cols_spec, mats_spec, cdec_spec, in_row, out_row = _ret_specs(t, d, dk, rt)
    scale = dk ** -0.5

    def body(drn_ref, r_ref, q_ref, k_ref, v_ref, g_ref, cos_ref, sin_ref, cols_ref, mats_ref, cdec_ref,
             dq_ref, dk_ref, dv_ref, dg_ref, dlg_ref,
             sb_scr, gf_scr, st_s, st_g, acc_af, acc_ab, acc_vf, acc_vb, acc_sf, acc_sb):
        p = pl.program_id(1)
        n = pl.program_id(2)
        af, ab, kf, kb = cols_ref[0, 0], cols_ref[0, 1], cols_ref[0, 2], cols_ref[0, 3]
        af1, ab1, kf1, kb1 = cols_ref[0, 4], cols_ref[0, 5], cols_ref[0, 6], cols_ref[0, 7]
        cf = cdec_ref[0, 0:1, :]
        cb = cdec_ref[0, 1:2, :]

        @pl.when(n == 0)
        def _():
            st_s[...] = jnp.zeros_like(st_s)
            st_g[...] = jnp.zeros_like(st_g)

        @pl.when(jnp.logical_and(n == 0, p == 1))
        def _():
            for a in (acc_af, acc_ab, acc_vf, acc_vb, acc_sf, acc_sb):
                a[...] = jnp.zeros_like(a)

        def load(rows):
            cs, sn = cos_ref[rows, :], sin_ref[rows, :]
            q = q_ref[0, rows, :].astype(f32)
            kk = k_ref[0, rows, :].astype(f32)
            rr = r_ref[rows, :].astype(f32)
            rstd = lax.rsqrt(jnp.mean(rr * rr, axis=-1, keepdims=True) + NORM_EPS)
            rhat = rr * rstd
            gg = g_ref[0, rows, :].astype(f32)
            sg = _sigmoid(gg)
            dd = drn_ref[rows, :].astype(f32)
            drhat = dd * gg * sg
            dout = rstd * (drhat - rhat * jnp.mean(drhat * rhat, axis=-1, keepdims=True))
            dgr = dd * rhat * _dsilu(gg, sg)
            return q, kk, dout.astype(bf16), dgr, cs, sn

        @pl.when(p == 0)
        def _():
            for j in reversed(range(cpt)):
                rows = slice(j * CHUNK, (j + 1) * CHUNK)
                ch = blk(p, n) * cpt + j
                q, kk, doutb, _, _, _ = load(rows)
                sb_scr[ch] = st_s[...].astype(bf16)
                gf_scr[ch] = st_g[...].astype(bf16)
                st_s[...] = st_s[...] * cb + _dot_tn((kk * kb).astype(bf16), v_ref[0, rows, :])
                st_g[...] = st_g[...] * cf + _dot_tn((q * af).astype(bf16), doutb)

        @pl.when(p == 1)
        def _():
            for j in range(cpt):
                rows = slice(j * CHUNK, (j + 1) * CHUNK)
                ch = blk(p, n) * cpt + j
                q, kk, doutb, dgr, cs, sn = load(rows)
                v = v_ref[0, rows, :]
                qb = q_ref[0, rows, :]
                kkb = k_ref[0, rows, :]
                sf = st_s[...]
                gb = st_g[...]
                sfb = sf.astype(bf16)
                gbb = gb.astype(bf16)
                sbb = sb_scr[ch]
                gfb = gf_scr[ch]
                dmat = mats_ref[0, 0]
                scores = _dot_nt(qb, kkb)
                dpraw = _dot_nt(doutb, v)
                dpb = (dpraw * dmat).astype(bf16)
                pmb = (scores * dmat).astype(bf16)
                x1 = _dot_nt(doutb, sfb)
                x2 = _dot_nt(doutb, sbb)
                y1 = _dot_nt(v, gfb)
                y2 = _dot_nt(v, gbb)
                kdf = (kk * kf).astype(bf16)
                kdb = (kk * kb).astype(bf16)
                dq = _dot(dpb, kkb) + x1 * af + x2 * ab
                dkk = _dot_tn(dpb, qb) + y1 * kf + y2 * kb
                dv = _dot_tn(pmb, doutb) + _dot(kdf, gfb) + _dot(kdb, gbb)
                ps = dpraw * scores
                acc_af[...] += ps * mats_ref[0, 1]
                acc_ab[...] += ps * mats_ref[0, 2]
                acc_vf[...] += x1 * q * af1 + y1 * kk * kf1
                acc_vb[...] += x2 * q * ab1 + y2 * kk * kb1
                acc_sf[...] += gfb.astype(f32) * sf
                acc_sb[...] += gb * sbb.astype(f32)
                st_s[...] = sf * cf + _dot_tn(kdf, v)
                st_g[...] = gb * cb + _dot_tn((q * ab).astype(bf16), doutb)
                dq_ref[rows, :] = _rot_inv(dq, cs, sn).astype(bf16)
                dk_ref[rows, :] = (_rot_inv(dkk, cs, sn) * scale).astype(bf16)
                dv_ref[rows, :] = dv.astype(bf16)
                dg_ref[rows, :] = dgr.astype(bf16)

        @pl.when(jnp.logical_and(p == 1, n == nr - 1))
        def _():
            tf = jnp.sum(acc_af[...]) + jnp.sum(acc_vf[...]) + CHUNK * jnp.sum(acc_sf[...] * cf)
            tb = jnp.sum(acc_ab[...]) + jnp.sum(acc_vb[...]) + CHUNK * jnp.sum(acc_sb[...] * cb)
            rid = lax.broadcasted_iota(jnp.int32, (8, 128), 0)
            dlg_ref[0] = jnp.where(rid == 0, tf, jnp.where(rid == 1, tb, 0.0))

    nch = t // CHUNK
    return pl.pallas_call(
        body, name="ret_bwd", grid=(RET_HEADS, 2, nr),
        in_specs=[in_row, in_row, q_spec, k_spec, v_spec, g_spec, tab_spec, tab_spec, cols_spec, mats_spec, cdec_spec],
        out_specs=[out_row, out_row, out_row, out_row, pl.BlockSpec((1, 8, 128), lambda h, p, n: (h, 0, 0))],
        out_shape=[jax.ShapeDtypeStruct((t, d), bf16)] * 4 + [jax.ShapeDtypeStruct((RET_HEADS, 8, 128), f32)],
        scratch_shapes=[pltpu.VMEM((nch, dk, dk), bf16), pltpu.VMEM((nch, dk, dk), bf16),
                        pltpu.VMEM((dk, dk), f32), pltpu.VMEM((dk, dk), f32),
                        pltpu.VMEM((CHUNK, CHUNK), f32), pltpu.VMEM((CHUNK, CHUNK), f32),
                        pltpu.VMEM((CHUNK, dk), f32), pltpu.VMEM((CHUNK, dk), f32),
                        pltpu.VMEM((dk, dk), f32), pltpu.VMEM((dk, dk), f32)],
        compiler_params=_cparams(),
    )(drn, r, proj, proj, proj, proj, cos, sin, cols, mats, cdec)


def mix_fwd(a, rn, proj, wa, wb, wo, x1):
    t, d = x1.shape
    tm = _row_tile(t)

    def body(a_ref, rn_ref, p_ref, wa_ref, wb_ref, wo_ref, x_ref, xo_ref, ba_ref, br_ref):
        ba = _dot(a_ref[...], wa_ref[...])
        br = _dot(rn_ref[...], wb_ref[...])
        sa = _sigmoid(p_ref[0, :, 0:d].astype(f32))
        sb = _sigmoid(p_ref[0, :, d:2 * d].astype(f32))
        mix = (sa * ba + sb * br).astype(bf16)
        xo_ref[...] = x_ref[...] + _dot(mix, wo_ref[...])
        ba_ref[...] = ba.astype(bf16)
        br_ref[...] = br.astype(bf16)

    row = pl.BlockSpec((tm, d), lambda i: (i, 0))
    wsp = pl.BlockSpec((d, d), lambda i: (0, 0))
    return pl.pallas_call(
        body, name="mix_fwd", grid=(t // tm,),
        in_specs=[row, row, pl.BlockSpec((1, tm, 2 * d), lambda i: (3, i, 0)), wsp, wsp, wsp, row],
        out_specs=[row, row, row],
        out_shape=[jax.ShapeDtypeStruct((t, d), f32), jax.ShapeDtypeStruct((t, d), bf16), jax.ShapeDtypeStruct((t, d), bf16)],
        compiler_params=_cparams(),
    )(a, rn, proj, wa, wb, wo, x1)


def mix_bwd_act(dx2, ba, br, proj, wa, wb, wo, dep):
    t, d = dx2.shape
    tm = _row_tile(t)

    def body(dx_ref, ba_ref, br_ref, p_ref, wa_ref, wb_ref, wo_ref, dep_ref,
             da_ref, drn_ref, dga_ref, dgb_ref, mix_ref, dba_ref, dbr_ref, dxb_ref):
        dxb = dx_ref[...].astype(bf16)
        dxb_ref[...] = dxb
        dmix = _dot_nt(dxb, wo_ref[...])
        ba = ba_ref[...].astype(f32)
        br = br_ref[...].astype(f32)
        sa = _sigmoid(p_ref[0, :, 0:d].astype(f32))
        sb = _sigmoid(p_ref[0, :, d:2 * d].astype(f32))
        mix_ref[...] = (sa * ba + sb * br).astype(bf16)
        dba = (dmix * sa).astype(bf16)
        dbr = (dmix * sb).astype(bf16)
        dba_ref[...] = dba
        dbr_ref[...] = dbr
        dga_ref[...] = (dmix * ba * sa * (1.0 - sa)).astype(bf16)
        dgb_ref[...] = (dmix * br * sb * (1.0 - sb)).astype(bf16)
        da_ref[...] = _dot_nt(dba, wa_ref[...]).astype(bf16)
        drn_ref[...] = _dot_nt(dbr, wb_ref[...]).astype(bf16)

    row = pl.BlockSpec((tm, d), lambda i: (i, 0))
    wsp = pl.BlockSpec((d, d), lambda i: (0, 0))
    return pl.pallas_call(
        body, name="mix_bwd_act", grid=(t // tm,),
        in_specs=[row, row, row, pl.BlockSpec((1, tm, 2 * d), lambda i: (3, i, 0)), wsp, wsp, wsp, _ANY],
        out_specs=[row] * 8,
        out_shape=[jax.ShapeDtypeStruct((t, d), bf16)] * 8,
        compiler_params=_cparams(),
    )(dx2, ba, br, proj, wa, wb, wo, dep)


def inproj_bwd_act(segs, win, x1, ng, dx2):
    t, d = x1.shape
    s4 = win.shape[0]
    tm = _row_tile(t) // 2
    nseg = len(segs)

    def body(*refs):
        seg_refs = refs[:nseg]
        w_ref, x_ref, ng_ref, dx2_ref, dx1_ref, db_ref, dng_ref = refs[nseg:]
        i = pl.program_id(0)
        dh = None
        for e, sr in enumerate(seg_refs):
            sb = sr[...]
            part = _dot_nt(sb, w_ref[e // 2, :, (e % 2) * d:(e % 2 + 1) * d])
            dh = part if dh is None else dh + part
            _acc_out(db_ref.at[e], i == 0, jnp.sum(sb.astype(f32), axis=0, keepdims=True))
        _, xh, r = _rms(x_ref[...], ng_ref[...])
        dx1_ref[...] = dx2_ref[...] + _rms_bwd(dh, xh, r, ng_ref[...])
        _acc_out(dng_ref, i == 0, jnp.sum(dh * xh, axis=0, keepdims=True))

    row = pl.BlockSpec((tm, d), lambda i: (i, 0))
    vec = pl.BlockSpec((1, d), lambda i: (0, 0))
    return pl.pallas_call(
        body, name="inproj_bwd_act", grid=(t // tm,),
        in_specs=[row] * nseg + [pl.BlockSpec((s4, d, 2 * d), lambda i: (0, 0, 0), pipeline_mode=pl.Buffered(1)),
                                 row, vec, row],
        out_specs=[row, pl.BlockSpec((nseg, 1, d), lambda i: (0, 0, 0)), vec],
        out_shape=[jax.ShapeDtypeStruct((t, d), f32), jax.ShapeDtypeStruct((nseg, 1, d), f32),
                   jax.ShapeDtypeStruct((1, d), f32)],
        compiler_params=_cparams(),
    )(*segs, win, x1, ng, dx2)


def loss_head(x3, fng, tgt):
    t, d = x3.shape
    tm = _row_tile(t)

    def body(x_ref, g_ref, t_ref, loss_ref, dx_ref, dg_ref):
        i = pl.program_id(0)
        y, xh, r = _rms(x_ref[...], g_ref[...])
        diff = y - t_ref[...]
        part = 0.5 * jnp.sum(jnp.sum(diff * diff, axis=0, keepdims=True), axis=1, keepdims=True) / d
        _acc_out(loss_ref, i == 0, jnp.broadcast_to(part, (1, 128)))
        dy = diff * (1.0 / d)
        dx_ref[...] = _rms_bwd(dy, xh, r, g_ref[...])
        _acc_out(dg_ref, i == 0, jnp.sum(dy * xh, axis=0, keepdims=True))

    row = pl.BlockSpec((tm, d), lambda i: (i, 0))
    vec = pl.BlockSpec((1, d), lambda i: (0, 0))
    return pl.pallas_call(
        body, name="loss_head", grid=(t // tm,),
        in_specs=[row, vec, row],
        out_specs=[pl.BlockSpec((1, 128), lambda i: (0, 0)), row, vec],
        out_shape=[jax.ShapeDtypeStruct((1, 128), f32), jax.ShapeDtypeStruct((t, d), f32), jax.ShapeDtypeStruct((1, d), f32)],
        compiler_params=_cparams(),
    )(x3, fng, tgt)


def _place():
    return lax.axis_index("x"), lax.axis_index("y"), lax.axis_index("c")


def _other_chips(x, y):
    return [(1 - x, y), (x, 1 - y), (1 - x, 1 - y)]


_ANY = pl.BlockSpec(memory_space=pl.ANY)


_HBM = pl.BlockSpec(memory_space=pltpu.HBM)
_SEM = pl.BlockSpec(memory_space=pltpu.SEMAPHORE)
_EFFECT = pltpu.SideEffectType.DATAFLOW_SIDE_EFFECTING


def _hbm(a):
    return pltpu.with_memory_space_constraint(a, pltpu.HBM)


def _half_rows(ref, c):
    half = ref.shape[1] // 2
    return pl.ds(pl.multiple_of(c * half, 16), half)


def _chip_copy(src, dst, send_sem, recv_sem, chip, c):
    return pltpu.make_async_remote_copy(src_ref=src, dst_ref=dst, send_sem=send_sem, recv_sem=recv_sem,
                                        device_id=(chip[0], chip[1], c), device_id_type=MESH)


def gather_start(bufs, groups):
    nb, ng = len(bufs), len(groups)

    def body(*refs):
        ins = refs[:nb]
        sems = refs[nb:nb + 2 * ng]
        token = refs[-1]
        x, y, c = _place()
        k = 2 * x + y
        for gi, grp in enumerate(groups):
            for wi, w in enumerate(grp):
                mine = ins[w].at[k, _half_rows(ins[w], c)]
                for j, chip in enumerate(_other_chips(x, y)):
                    _chip_copy(mine, mine, sems[2 * gi].at[3 * wi + j], sems[2 * gi + 1].at[3 * wi + j], chip, c).start()
        token[...] = jnp.zeros_like(token)

    sem_shapes = []
    for grp in groups:
        sem_shapes += [pltpu.SemaphoreType.DMA((3 * len(grp),)), pltpu.SemaphoreType.DMA((3 * len(grp),))]
    outs = pl.pallas_call(
        body, name="gather_start",
        out_shape=sem_shapes + [pltpu.HBM(b.shape, b.dtype) for b in bufs] + [jax.ShapeDtypeStruct((8, 128), f32)],
        in_specs=[_HBM] * nb,
        out_specs=[_SEM] * (2 * ng) + [_HBM] * nb + [pl.BlockSpec(memory_space=pltpu.VMEM)],
        input_output_aliases={w: 2 * ng + w for w in range(nb)},
        compiler_params=pltpu.CompilerParams(has_side_effects=_EFFECT),
    )(*[_hbm(b) for b in bufs])
    sems = [(outs[2 * gi], outs[2 * gi + 1]) for gi in range(ng)]
    return sems, list(outs[2 * ng:2 * ng + nb]), outs[-1]


def gather_wait(bufs, sems, after, name):
    n = len(bufs)

    def body(*refs):
        ins = refs[:n]
        send_sems, recv_sems = refs[n], refs[n + 1]
        x, y, c = _place()
        k = 2 * x + y
        for wi in range(n):
            half = _half_rows(ins[wi], c)
            for j, chip in enumerate(_other_chips(x, y)):
                cp = _chip_copy(ins[wi].at[k, half], ins[wi].at[2 * chip[0] + chip[1], half], send_sems.at[3 * wi + j],
                                recv_sems.at[3 * wi + j], chip, c)
                cp.wait_send()
                cp.wait_recv()

    outs = pl.pallas_call(
        body, name=name,
        out_shape=[pltpu.HBM(b.shape, b.dtype) for b in bufs],
        in_specs=[_HBM] * n + [_SEM, _SEM, _ANY],
        out_specs=[_HBM] * n,
        input_output_aliases={i: i for i in range(n)},
        compiler_params=pltpu.CompilerParams(has_side_effects=_EFFECT),
    )(*bufs, sems[0], sems[1], after)
    return list(outs)


def gather_forward(bufs, name):
    n = len(bufs)

    def body(*refs):
        ins = refs[n:2 * n]
        send_sems, recv_sems = refs[2 * n], refs[2 * n + 1]
        x, y, c = _place()
        copies = []
        for wi in range(n):
            for j, chip in enumerate(_other_chips(x, y)):
                kp = 2 * chip[0] + chip[1]
                got = ins[wi].at[kp, _half_rows(ins[wi], c)]
                cp = pltpu.make_async_remote_copy(
                    src_ref=got, dst_ref=got, send_sem=send_sems.at[3 * wi + j], recv_sem=recv_sems.at[3 * wi + j],
                    device_id=(x, y, 1 - c), device_id_type=MESH)
                cp.start()
                copies.append((cp, wi, kp, j))
        for cp, wi, kp, j in copies:
            cp.wait_send()
            theirs = ins[wi].at[kp, _half_rows(ins[wi], 1 - c)]
            pltpu.make_async_remote_copy(
                src_ref=theirs, dst_ref=theirs, send_sem=send_sems.at[3 * wi + j], recv_sem=recv_sems.at[3 * wi + j],
                device_id=(x, y, 1 - c), device_id_type=MESH).wait_recv()

    outs = pl.pallas_call(
        body, name=name,
        out_shape=[jax.ShapeDtypeStruct(b.shape, b.dtype) for b in bufs],
        in_specs=[_ANY] * n, out_specs=[_ANY] * n,
        input_output_aliases={i: i for i in range(n)},
        scratch_shapes=[pltpu.SemaphoreType.DMA((3 * n,)), pltpu.SemaphoreType.DMA((3 * n,))],
    )(*bufs)
    return list(outs)


def exchange_start(grads, name):
    n = len(grads)
    lands = [lax.empty((3,) + g.shape[1:], g.dtype) for g in grads]

    def body(*refs):
        ins = refs[:n]
        land = refs[n:2 * n]
        send_sems, recv_sems = refs[2 * n], refs[2 * n + 1]
        token = refs[-1]
        x, y, c = _place()
        for wi in range(n):
            for j, chip in enumerate(_other_chips(x, y)):
                _chip_copy(ins[wi].at[2 * chip[0] + chip[1]], land[wi].at[j], send_sems.at[3 * wi + j],
                           recv_sems.at[3 * wi + j], chip, c).start()
        token[...] = jnp.zeros_like(token)

    outs = pl.pallas_call(
        body, name=name,
        out_shape=[pltpu.SemaphoreType.DMA((3 * n,)), pltpu.SemaphoreType.DMA((3 * n,))]
        + [pltpu.HBM(g.shape, g.dtype) for g in grads] + [pltpu.HBM(l.shape, l.dtype) for l in lands]
        + [jax.ShapeDtypeStruct((8, 128), f32)],
        in_specs=[_HBM] * (2 * n),
        out_specs=[_SEM, _SEM] + [_HBM] * (2 * n) + [pl.BlockSpec(memory_space=pltpu.VMEM)],
        input_output_aliases={i: 2 + i for i in range(2 * n)},
        compiler_params=pltpu.CompilerParams(has_side_effects=_EFFECT),
    )(*[_hbm(g) for g in grads], *[_hbm(l) for l in lands])
    return (outs[0], outs[1]), list(outs[2:2 + n]), list(outs[2 + n:2 + 2 * n]), outs[-1]


def exchange_wait(grads, lands, sems, after, name):
    n = len(grads)

    def body(*refs):
        ins = refs[:n]
        land = refs[n:2 * n]
        send_sems, recv_sems = refs[2 * n], refs[2 * n + 1]
        x, y, c = _place()
        for wi in range(n):
            for j, chip in enumerate(_other_chips(x, y)):
                cp = _chip_copy(ins[wi].at[2 * chip[0] + chip[1]], land[wi].at[j], send_sems.at[3 * wi + j],
                                recv_sems.at[3 * wi + j], chip, c)
                cp.wait_send()
                cp.wait_recv()

    outs = pl.pallas_call(
        body, name=name,
        out_shape=[pltpu.HBM(g.shape, g.dtype) for g in grads] + [pltpu.HBM(l.shape, l.dtype) for l in lands],
        in_specs=[_HBM] * (2 * n) + [_SEM, _SEM, _ANY],
        out_specs=[_HBM] * (2 * n),
        input_output_aliases={i: i for i in range(2 * n)},
        compiler_params=pltpu.CompilerParams(has_side_effects=_EFFECT),
    )(*grads, *lands, sems[0], sems[1], after)
    return list(outs[:n]), list(outs[n:])


def swap_with_sibling(parts, name):
    nw = len(parts)

    def body(*refs):
        ins = refs[:nw]
        outs = refs[nw:2 * nw]
        send_sems, recv_sems = refs[2 * nw:]
        x, y, c = _place()
        copies = [pltpu.make_async_remote_copy(
            src_ref=ins[w], dst_ref=outs[w], send_sem=send_sems.at[w], recv_sem=recv_sems.at[w],
            device_id=(x, y, 1 - c), device_id_type=MESH) for w in range(nw)]
        for cp in copies:
            cp.start()
        for cp in copies:
            cp.wait()

    return pl.pallas_call(
        body, name=name,
        in_specs=[_ANY] * nw, out_specs=[_ANY] * nw,
        out_shape=[jax.ShapeDtypeStruct(p.shape, p.dtype) for p in parts],
        scratch_shapes=[pltpu.SemaphoreType.DMA((nw,)), pltpu.SemaphoreType.DMA((nw,))],
    )(*parts)


def gather_small(block, dep):
    r, lanes = block.shape

    def body(b_ref, dep_ref, o_ref, send_sems, recv_sems):
        x, y, c = _place()
        me = 4 * x + 2 * y + c
        o_ref[me] = b_ref[...]
        peers = []
        for m in range(1, N_DEV):
            px = 1 - x if m & 4 else x
            py = 1 - y if m & 2 else y
            pc = 1 - c if m & 1 else c
            peers.append((px, py, pc))
        for m, peer in enumerate(peers):
            pltpu.make_async_remote_copy(
                src_ref=b_ref, dst_ref=o_ref.at[me], send_sem=send_sems.at[m], recv_sem=recv_sems.at[m],
                device_id=peer, device_id_type=MESH).start()
        for m, (px, py, pc) in enumerate(peers):
            pltpu.make_async_remote_copy(
                src_ref=b_ref, dst_ref=o_ref.at[4 * px + 2 * py + pc], send_sem=send_sems.at[m],
                recv_sem=recv_sems.at[m], device_id=(px, py, pc), device_id_type=MESH).wait()

    return pl.pallas_call(
        body, name="gather_small",
        in_specs=[pl.BlockSpec(memory_space=pltpu.VMEM), _ANY], out_specs=pl.BlockSpec(memory_space=pltpu.VMEM),
        out_shape=jax.ShapeDtypeStruct((N_DEV, r, lanes), block.dtype),
        scratch_shapes=[pltpu.SemaphoreType.DMA((N_DEV - 1,)), pltpu.SemaphoreType.DMA((N_DEV - 1,))],
    )(block, dep)


def _adamw(w, g, m, v):
    m = ADAM_B1 * m + (1.0 - ADAM_B1) * g
    v = ADAM_B2 * v + (1.0 - ADAM_B2) * (g * g)
    m_hat = m / (1.0 - ADAM_B1 ** ADAM_STEP)
    v_hat = v / (1.0 - ADAM_B2 ** ADAM_STEP)
    delta = -ADAM_LR * (m_hat / (jnp.sqrt(v_hat) + ADAM_EPS) + ADAM_WD * w)
    return delta, m, v


def _ew_tile(rows):
    for cand in (256, 176, 128, 64, 32, 16, 8):
        if rows % cand == 0:
            return cand
    return rows


def sum_partials(chip, own, land, name):
    _, r, c = own.shape
    tr = _ew_tile(r)

    def body(k_ref, own_ref, p_ref, o_ref):
        o_ref[...] = ((own_ref[0].astype(f32) + p_ref[0].astype(f32)) + p_ref[1].astype(f32)) + p_ref[2].astype(f32)

    return pl.pallas_call(
        body, name=name,
        grid_spec=pltpu.PrefetchScalarGridSpec(
            num_scalar_prefetch=1, grid=(r // tr,),
            in_specs=[pl.BlockSpec((1, tr, c), lambda i, k: (k[0], i, 0)), pl.BlockSpec((3, tr, c), lambda i, k: (0, i, 0))],
            out_specs=pl.BlockSpec((tr, c), lambda i, k: (i, 0))),
        out_shape=jax.ShapeDtypeStruct((r, c), f32),
        compiler_params=_cparams(),
    )(chip, own, land)


def adamw_shard(p_mine, p_sibling, w, m, v, name):
    r, c = w.shape
    tr = _ew_tile(r)

    def body(a_ref, b_ref, w_ref, m_ref, v_ref, g_ref, d_ref, mo_ref, vo_ref):
        g = a_ref[...] + b_ref[...]
        delta, mn, vn = _adamw(w_ref[...], g, m_ref[...], v_ref[...])
        g_ref[...] = g
        d_ref[...] = delta
        mo_ref[...] = mn
        vo_ref[...] = vn

    blk = pl.BlockSpec((tr, c), lambda i: (i, 0))
    return pl.pallas_call(
        body, name=name, grid=(r // tr,),
        in_specs=[blk] * 5, out_specs=[blk] * 4,
        out_shape=[jax.ShapeDtypeStruct((r, c), f32)] * 4,
        compiler_params=_cparams(),
    )(p_mine, p_sibling, w, m, v)


def adamw_small(g8, w, m, v):
    _, r, lanes = g8.shape

    def body(g_ref, w_ref, m_ref, v_ref, go_ref, d_ref, mo_ref, vo_ref):
        g = g_ref[0]
        for i in range(1, N_DEV):
            g = g + g_ref[i]
        delta, mn, vn = _adamw(w_ref[...], g, m_ref[...], v_ref[...])
        go_ref[...] = g
        d_ref[...] = delta
        mo_ref[...] = mn
        vo_ref[...] = vn

    return pl.pallas_call(
        body, name="adamw_small",
        out_shape=[jax.ShapeDtypeStruct((r, lanes), f32)] * 4,
        compiler_params=_cparams(),
    )(g8, w, m, v)


def _size(shape):
    n = 1
    for e in shape:
        n *= e
    return n


def _pack_rows(shapes):
    rows = [-(-_size(s) // 1024) * 8 for s in shapes]
    return rows, sum(rows)


def _pack(arrs, shapes):
    rows, _ = _pack_rows(shapes)
    parts = [jnp.pad(a.reshape(-1).astype(f32), (0, r * 128 - _size(s))).reshape(r, 128)
             for a, s, r in zip(arrs, shapes, rows)]
    return jnp.concatenate(parts, axis=0)


def _unpack(block, shapes):
    rows, _ = _pack_rows(shapes)
    out, off = [], 0
    for s, r in zip(shapes, rows):
        out.append(block[off:off + r].reshape(-1)[:_size(s)].reshape(s))
        off += r
    return out


TRANSPOSED = ("ffn1_w_gate", "ffn1_w_up", "ffn2_w_gate", "ffn2_w_up")


def _shard2d(a, n):
    return a[0].T if n in TRANSPOSED else a[0]


def _unshard(a, n):
    return (a.T if n in TRANSPOSED else a)[None]


BIG = ("ffn1_w_gate", "ffn1_w_up", "ffn1_w_down", "w_in", "w_branch_a", "w_branch_b", "w_out",
       "ffn2_w_gate", "ffn2_w_up", "ffn2_w_down")
SMALL = ("ffn1_norm", "mix_norm", "b_in", "sgu_norm_g", "sgu_norm_b", "sgu_w_s", "sgu_b_s", "ret_decay_logit",
         "ffn2_norm", "final_norm")
WEIGHTS = ("ffn1_norm", "ffn1_w_gate", "ffn1_w_up", "ffn1_w_down", "mix_norm", "w_in", "b_in", "sgu_norm_g",
           "sgu_norm_b", "sgu_w_s", "sgu_b_s", "ret_decay_logit", "w_branch_a", "w_branch_b", "w_out", "ffn2_norm",
           "ffn2_w_gate", "ffn2_w_up", "ffn2_w_down", "final_norm")


def kernel(x, ffn1_norm, ffn1_w_gate, ffn1_w_up, ffn1_w_down, mix_norm, w_in, b_in, sgu_norm_g, sgu_norm_b, sgu_w_s, sgu_b_s, ret_decay_logit, w_branch_a, w_branch_b, w_out, ffn2_norm, ffn2_w_gate, ffn2_w_up, ffn2_w_down, final_norm, loss_target, m_ffn1_norm, m_ffn1_w_gate, m_ffn1_w_up, m_ffn1_w_down, m_mix_norm, m_w_in, m_b_in, m_sgu_norm_g, m_sgu_norm_b, m_sgu_w_s, m_sgu_b_s, m_ret_decay_logit, m_w_branch_a, m_w_branch_b, m_w_out, m_ffn2_norm, m_ffn2_w_gate, m_ffn2_w_up, m_ffn2_w_down, m_final_norm, v_ffn1_norm, v_ffn1_w_gate, v_ffn1_w_up, v_ffn1_w_down, v_mix_norm, v_w_in, v_b_in, v_sgu_norm_g, v_sgu_norm_b, v_sgu_w_s, v_sgu_b_s, v_ret_decay_logit, v_w_branch_a, v_w_branch_b, v_w_out, v_ffn2_norm, v_ffn2_w_gate, v_ffn2_w_up, v_ffn2_w_down, v_final_norm):
    p = dict(ffn1_norm=ffn1_norm, ffn1_w_gate=ffn1_w_gate, ffn1_w_up=ffn1_w_up, ffn1_w_down=ffn1_w_down,
             mix_norm=mix_norm, w_in=w_in, b_in=b_in, sgu_norm_g=sgu_norm_g, sgu_norm_b=sgu_norm_b, sgu_w_s=sgu_w_s,
             sgu_b_s=sgu_b_s, ret_decay_logit=ret_decay_logit, w_branch_a=w_branch_a, w_branch_b=w_branch_b,
             w_out=w_out, ffn2_norm=ffn2_norm, ffn2_w_gate=ffn2_w_gate, ffn2_w_up=ffn2_w_up, ffn2_w_down=ffn2_w_down,
             final_norm=final_norm)
    mom = dict(ffn1_norm=m_ffn1_norm, ffn1_w_gate=m_ffn1_w_gate, ffn1_w_up=m_ffn1_w_up, ffn1_w_down=m_ffn1_w_down,
               mix_norm=m_mix_norm, w_in=m_w_in, b_in=m_b_in, sgu_norm_g=m_sgu_norm_g, sgu_norm_b=m_sgu_norm_b,
               sgu_w_s=m_sgu_w_s, sgu_b_s=m_sgu_b_s, ret_decay_logit=m_ret_decay_logit, w_branch_a=m_w_branch_a,
               w_branch_b=m_w_branch_b, w_out=m_w_out, ffn2_norm=m_ffn2_norm, ffn2_w_gate=m_ffn2_w_gate,
               ffn2_w_up=m_ffn2_w_up, ffn2_w_down=m_ffn2_w_down, final_norm=m_final_norm)
    var = dict(ffn1_norm=v_ffn1_norm, ffn1_w_gate=v_ffn1_w_gate, ffn1_w_up=v_ffn1_w_up, ffn1_w_down=v_ffn1_w_down,
               mix_norm=v_mix_norm, w_in=v_w_in, b_in=v_b_in, sgu_norm_g=v_sgu_norm_g, sgu_norm_b=v_sgu_norm_b,
               sgu_w_s=v_sgu_w_s, sgu_b_s=v_sgu_b_s, ret_decay_logit=v_ret_decay_logit, w_branch_a=v_w_branch_a,
               w_branch_b=v_w_branch_b, w_out=v_w_out, ffn2_norm=v_ffn2_norm, ffn2_w_gate=v_ffn2_w_gate,
               ffn2_w_up=v_ffn2_w_up, ffn2_w_down=v_ffn2_w_down, final_norm=v_final_norm)

    xs = x[0]
    tgt = loss_target[0]
    t, d = xs.shape
    dk = d // RET_HEADS
    tm = _row_tile(t)

    shards2d = {n: _shard2d(p[n], n) for n in BIG}
    chip = (2 * lax.axis_index("x") + lax.axis_index("y")).astype(jnp.int32).reshape(1)
    groups = {"ffn1": ("ffn1_w_gate", "ffn1_w_up", "ffn1_w_down"), "in": ("w_in",),
              "mix": ("w_branch_a", "w_branch_b", "w_out"), "ffn2": ("ffn2_w_gate", "ffn2_w_up", "ffn2_w_down")}
    bufs = [jnp.broadcast_to(shards2d[n].astype(bf16)[None], (N_CHIPS,) + shards2d[n].shape) for n in BIG]
    sems, bufs, tok = gather_start(bufs, [[BIG.index(n) for n in groups[g]] for g in ("ffn1", "in", "mix", "ffn2")])
    gsem = dict(zip(("ffn1", "in", "mix", "ffn2"), sems))
    pending = dict(zip(BIG, bufs))

    def arrive(gs, after):
        got = []
        for g in gs:
            got += gather_wait([pending[n] for n in groups[g]], gsem[g], after, "gather_wait_" + g)
        return gather_forward(got, "gather_forward_" + gs[0])

    bin4 = b_in.reshape(N_CHIPS, 1, 2 * d)
    ws_b = sgu_w_s[0].astype(bf16)
    bs_c = sgu_b_s[0][:, :, None]
    cols, mats, cdec, cos, sin = retention_constants(ret_decay_logit[0], t, dk)

    wg1, wu1, wd1 = [_pair_shards(w) for w in arrive(["ffn1"], tok)]
    x1, g1, u1 = ffn_fwd(xs, ffn1_norm, wg1, wu1, wd1, "ffn1_fwd")
    win, = arrive(["in"], x1)
    proj, hb2 = inproj_fwd(x1, mix_norm, win, bin4, cos, sin)
    a = sgu_fwd(proj, sgu_norm_g, sgu_norm_b, ws_b, bs_c)
    r, rn = ret_fwd(proj, cols, mats, cdec)
    wa, wb, wo, wg2, wu2, wd2 = arrive(["mix", "ffn2"], rn)
    wa, wb, wo = [w.reshape(d, d) for w in (wa, wb, wo)]
    wg2, wu2, wd2 = [_pair_shards(w) for w in (wg2, wu2, wd2)]
    x2, ba, br = mix_fwd(a, rn, proj, wa, wb, wo, x1)
    x3, g2, u2 = ffn_fwd(x2, ffn2_norm, wg2, wu2, wd2, "ffn2_fwd")
    loss_blk, dx3, d_final = loss_head(x3, final_norm.reshape(1, d), tgt)

    sent = {}
    dx2, dg2, du2, act2, hb3, dyb2, d_ffn2n = ffn_bwd_act(dx3, x2, ffn2_norm, g2, u2, wg2, wu2, wd2, "ffn2_bwd_act", tok)
    sent["ffn2"] = exchange_start(list(ffn_weight_grads(hb3, dyb2, dg2, du2, act2, "ffn2_grad")), "exchange_start_ffn2")
    da, drn, dga, dgb, mixb, dba, dbr, dx2b = mix_bwd_act(dx2, ba, br, proj, wa, wb, wo, sent["ffn2"][3])
    tg = min(t, 2048)
    row = pl.BlockSpec((tg, d), lambda s, i: (i, 0))

    def square_grad(xa, ya, name):
        return tn_matmul(xa, [ya], row, [row], 1, d, [d], t, tg, name).reshape(N_CHIPS, d // N_CHIPS, d)

    sent["mix"] = exchange_start([square_grad(a, dba, "grad_w_branch_a"), square_grad(rn, dbr, "grad_w_branch_b"),
                                  square_grad(mixb, dx2b, "grad_w_out")], "exchange_start_mix")
    dua, dva, d_ws, d_bs, d_sng, d_snb = sgu_bwd(da, proj, sgu_norm_g, sgu_norm_b, ws_b, bs_c, sent["mix"][3])
    dq, dkr, dv, dgr, dlg = ret_bwd(drn, r, proj, cols, mats, cdec, cos, sin)
    segs = [dua, dva, dq, dkr, dv, dgr, dga, dgb]
    dx1, d_bin, d_mixn = inproj_bwd_act(segs, win, x1, mix_norm, dx2)
    sent["in"] = exchange_start([jnp.concatenate(
        [tn_matmul(hb2, [segs[2 * s], segs[2 * s + 1]], row, [row, row], 1, d, [d, d], t, tg, "grad_w_in_%d" % s)
         for s in range(N_CHIPS)], axis=0)], "exchange_start_in")
    grad_x, dg1, du1, act1, hb1, dyb1, d_ffn1n = ffn_bwd_act(dx1, xs, ffn1_norm, g1, u1, wg1, wu1, wd1, "ffn1_bwd_act",
                                                              sent["in"][3])
    sent["ffn1"] = exchange_start(list(ffn_weight_grads(hb1, dyb1, dg1, du1, act1, "ffn1_grad")), "exchange_start_ffn1")

    out_g, out_d, out_m, out_v = {}, {}, {}, {}

    def finish(g, after):
        gsems, own, lands, _ = sent[g]
        own, lands = exchange_wait(own, lands, gsems, after, "exchange_wait_" + g)
        plane = [sum_partials(chip, o, l, "sum_" + n) for n, o, l in zip(groups[g], own, lands)]
        other = swap_with_sibling(plane, "swap_" + g)
        for n, mine, sib in zip(groups[g], plane, other):
            res = adamw_shard(mine, sib, shards2d[n], _shard2d(mom[n], n), _shard2d(var[n], n), "adamw_" + n)
            out_g[n], out_d[n], out_m[n], out_v[n] = [_unshard(o, n) for o in res]
        return out_g[groups[g][-1]]

    after = sent["ffn1"][3]
    for g in ("ffn2", "mix", "in"):
        after = finish(g, after)

    dlogit = dlg[:, 0:2, 0].T * jax.nn.sigmoid(-ret_decay_logit[0].astype(f32))
    small_g = dict(ffn1_norm=d_ffn1n, mix_norm=d_mixn, b_in=d_bin, sgu_norm_g=d_sng, sgu_norm_b=d_snb, sgu_w_s=d_ws,
                   sgu_b_s=d_bs, ret_decay_logit=dlogit, ffn2_norm=d_ffn2n, final_norm=d_final)
    shapes = [p[n].shape for n in SMALL]
    g8 = gather_small(_pack([small_g[n] for n in SMALL], shapes), after)
    sg, sd, sm, sv = adamw_small(g8, _pack([p[n] for n in SMALL], shapes), _pack([mom[n] for n in SMALL], shapes),
                                 _pack([var[n] for n in SMALL], shapes))
    for res, blockv in ((out_g, sg), (out_d, sd), (out_m, sm), (out_v, sv)):
        for n, val in zip(SMALL, _unpack(blockv, shapes)):
            res[n] = val
    finish("ffn1", sg)

    loss = lax.psum(loss_blk[0, 0], ("x", "y", "c"))
    return (loss, grad_x[None], *[out_g[n] for n in WEIGHTS], *[out_d[n] for n in WEIGHTS],
            *[out_m[n] for n in WEIGHTS], *[out_v[n] for n in WEIGHTS])
```

```python
import functools

import jax
import jax.numpy as jnp
from jax import lax
from jax.experimental import pallas as pl
from jax.experimental.pallas import tpu as pltpu

f32 = jnp.float32
bf16 = jnp.bfloat16

SGU_CHUNK = 128
CHUNK = 128
RET_HEADS = 4
SGU_GROUPS = 4
ROPE_BASE = 10000.0
NORM_EPS = 1e-6
ADAM_LR = 0.001
ADAM_B1 = 0.9
ADAM_B2 = 0.999
ADAM_EPS = 1e-08
ADAM_WD = 0.01
ADAM_STEP = 10
N_CHIPS = 4
N_DEV = 8
MESH = pl.DeviceIdType.MESH
VMEM_LIMIT = 52 * 1024 * 1024
VMEM_LIMIT_WIDE = 62 * 1024 * 1024

_NT = (((1,), (1,)), ((), ()))
_TN = (((0,), (0,)), ((), ()))


def _cparams(limit=None):
    return pltpu.CompilerParams(vmem_limit_bytes=VMEM_LIMIT if limit is None else limit)


def _row_tile(t):
    return 512 if t >= 2048 else t // 2


def _dot(a, b):
    return jnp.dot(a, b, preferred_element_type=f32)


def _dot_nt(a, b):
    return lax.dot_general(a, b, _NT, preferred_element_type=f32)


def _dot_tn(a, b):
    return lax.dot_general(a, b, _TN, preferred_element_type=f32)


def _rms(x, g):
    r = lax.rsqrt(jnp.mean(x * x, axis=-1, keepdims=True) + NORM_EPS)
    xh = x * r
    return xh * g, xh, r


def _rms_bwd(dy, xh, r, g):
    dxh = dy * g
    return r * (dxh - xh * jnp.mean(dxh * xh, axis=-1, keepdims=True))


def _sigmoid(x):
    return jax.nn.sigmoid(x)


def _dsilu(g, sg):
    return sg * (1.0 + g * (1.0 - sg))


def _gelu(x):
    return 0.5 * x * (1.0 + lax.erf(x * 0.7071067811865476))


def _dgelu(x):
    return 0.5 * (1.0 + lax.erf(x * 0.7071067811865476)) + x * jnp.exp(-0.5 * x * x) * 0.3989422804014327


def _acc_out(ref, first, val):
    @pl.when(first)
    def _():
        ref[...] = val

    @pl.when(jnp.logical_not(first))
    def _():
        ref[...] += val


def ffn_fwd(x, ng, wg, wu, wd, name):
    t, d = x.shape
    s4, fs, _ = wg.shape
    tm = _row_tile(t)

    def body(x_ref, ng_ref, wg_ref, wu_ref, wd_ref, xo_ref, g_ref, u_ref, h_scr, acc_scr):
        s = pl.program_id(1)

        @pl.when(s == 0)
        def _():
            y, _, _ = _rms(x_ref[...], ng_ref[...])
            h_scr[...] = y.astype(bf16)
            acc_scr[...] = jnp.zeros_like(acc_scr)

        h = h_scr[...]
        g = _dot_nt(h, wg_ref[0])
        u = _dot_nt(h, wu_ref[0])
        g_ref[0] = g.astype(bf16)
        u_ref[0] = u.astype(bf16)
        act = (g * _sigmoid(g) * u).astype(bf16)
        acc_scr[...] += _dot(act, wd_ref[0])

        @pl.when(s == s4 - 1)
        def _():
            xo_ref[...] = x_ref[...] + 0.5 * acc_scr[...]

    return pl.pallas_call(
        body, name=name, grid=(t // tm, s4),
        in_specs=[pl.BlockSpec((tm, d), lambda i, s: (i, 0)), pl.BlockSpec((1, d), lambda i, s: (0, 0)),
                  pl.BlockSpec((1, fs, d), lambda i, s: (s, 0, 0)), pl.BlockSpec((1, fs, d), lambda i, s: (s, 0, 0)),
                  pl.BlockSpec((1, fs, d), lambda i, s: (s, 0, 0))],
        out_specs=[pl.BlockSpec((tm, d), lambda i, s: (i, 0)), pl.BlockSpec((1, tm, fs), lambda i, s: (s, i, 0)),
                   pl.BlockSpec((1, tm, fs), lambda i, s: (s, i, 0))],
        out_shape=[jax.ShapeDtypeStruct((t, d), f32), jax.ShapeDtypeStruct((s4, t, fs), bf16),
                   jax.ShapeDtypeStruct((s4, t, fs), bf16)],
        scratch_shapes=[pltpu.VMEM((tm, d), bf16), pltpu.VMEM((tm, d), f32)],
        compiler_params=_cparams(),
    )(x, ng, wg, wu, wd)


def ffn_bwd_act(dxo, x, ng, g, u, wg, wu, wd, name, dep):
    t, d = x.shape
    s4, fs, _ = wg.shape
    tm = _row_tile(t)

    def body(dxo_ref, x_ref, ng_ref, g_ref, u_ref, wg_ref, wu_ref, wd_ref, dep_ref,
             dx_ref, dg_ref, du_ref, act_ref, hb_ref, dyb_ref, dng_ref, dy_scr, acc_scr):
        i = pl.program_id(0)
        s = pl.program_id(1)

        @pl.when(s == 0)
        def _():
            dyb = (0.5 * dxo_ref[...]).astype(bf16)
            dy_scr[...] = dyb
            dyb_ref[...] = dyb
            acc_scr[...] = jnp.zeros_like(acc_scr)

        dact = _dot_nt(dy_scr[...], wd_ref[0])
        gg = g_ref[0].astype(f32)
        uu = u_ref[0].astype(f32)
        sg = _sigmoid(gg)
        sil = gg * sg
        dgb = (dact * uu * _dsilu(gg, sg)).astype(bf16)
        dub = (dact * sil).astype(bf16)
        dg_ref[0] = dgb
        du_ref[0] = dub
        act_ref[0] = (sil * uu).astype(bf16)
        acc_scr[...] += _dot(dgb, wg_ref[0]) + _dot(dub, wu_ref[0])

        @pl.when(s == s4 - 1)
        def _():
            y, xh, r = _rms(x_ref[...], ng_ref[...])
            hb_ref[...] = y.astype(bf16)
            dh = acc_scr[...]
            dx_ref[...] = dxo_ref[...] + _rms_bwd(dh, xh, r, ng_ref[...])
            _acc_out(dng_ref, i == 0, jnp.sum(dh * xh, axis=0, keepdims=True))

    row = lambda i, s: (i, 0)
    shard = lambda i, s: (s, i, 0)
    wsp = lambda i, s: (s, 0, 0)
    return pl.pallas_call(
        body, name=name, grid=(t // tm, s4),
        in_specs=[pl.BlockSpec((tm, d), row), pl.BlockSpec((tm, d), row), pl.BlockSpec((1, d), lambda i, s: (0, 0)),
                  pl.BlockSpec((1, tm, fs), shard), pl.BlockSpec((1, tm, fs), shard),
                  pl.BlockSpec((1, fs, d), wsp), pl.BlockSpec((1, fs, d), wsp), pl.BlockSpec((1, fs, d), wsp), _ANY],
        out_specs=[pl.BlockSpec((tm, d), row), pl.BlockSpec((1, tm, fs), shard), pl.BlockSpec((1, tm, fs), shard),
                   pl.BlockSpec((1, tm, fs), shard), pl.BlockSpec((tm, d), row), pl.BlockSpec((tm, d), row),
                   pl.BlockSpec((1, d), lambda i, s: (0, 0))],
        out_shape=[jax.ShapeDtypeStruct((t, d), f32), jax.ShapeDtypeStruct((s4, t, fs), bf16),
                   jax.ShapeDtypeStruct((s4, t, fs), bf16), jax.ShapeDtypeStruct((s4, t, fs), bf16),
                   jax.ShapeDtypeStruct((t, d), bf16), jax.ShapeDtypeStruct((t, d), bf16),
                   jax.ShapeDtypeStruct((1, d), f32)],
        scratch_shapes=[pltpu.VMEM((tm, d), bf16), pltpu.VMEM((tm, d), f32)],
        compiler_params=_cparams(VMEM_LIMIT_WIDE),
    )(dxo, x, ng, g, u, wg, wu, wd, dep)


def tn_matmul(xs, ys, x_spec, y_specs, n_shards, k1, k2s, t, tm, name):
    k2 = sum(k2s)
    ny = len(ys)

    def body(*refs):
        x_ref = refs[0]
        y_refs = refs[1:1 + ny]
        o_ref = refs[1 + ny]
        acc = refs[2 + ny]
        i = pl.program_id(1)
        xb = x_ref[0] if len(x_ref.shape) == 3 else x_ref[...]
        off = 0
        for y_ref, w in zip(y_refs, k2s):
            yb = y_ref[0] if len(y_ref.shape) == 3 else y_ref[...]
            part = _dot_tn(xb, yb)
            sl = (slice(None), slice(off, off + w))

            @pl.when(i == 0)
            def _(part=part, sl=sl):
                acc[sl] = part

            @pl.when(i > 0)
            def _(part=part, sl=sl):
                acc[sl] += part

            off += w

        @pl.when(i == t // tm - 1)
        def _():
            o_ref[0] = acc[...].astype(bf16)

    return pl.pallas_call(
        body, name=name, grid=(n_shards, t // tm),
        in_specs=[x_spec] + list(y_specs),
        out_specs=pl.BlockSpec((1, k1, k2), lambda s, i: (s, 0, 0)),
        out_shape=jax.ShapeDtypeStruct((n_shards, k1, k2), bf16),
        scratch_shapes=[pltpu.VMEM((k1, k2), f32)],
        compiler_params=_cparams(),
    )(xs, *ys)


def _pair_shards(w):
    s4, fs, d = w.shape
    return w.reshape(s4 // 2, 2 * fs, d)


def ffn_weight_grads(hb, dyb, dg, du, act, name):
    t, d = hb.shape
    s2, _, fs2 = dg.shape
    tm = t
    row = pl.BlockSpec((tm, d), lambda s, i: (i, 0))
    shard = pl.BlockSpec((1, tm, fs2), lambda s, i: (s, i, 0))
    gwg = tn_matmul(dg, [hb], shard, [row], s2, fs2, [d], t, tm, name + "_wg")
    gwu = tn_matmul(du, [hb], shard, [row], s2, fs2, [d], t, tm, name + "_wu")
    gwd = tn_matmul(act, [dyb], shard, [row], s2, fs2, [d], t, tm, name + "_wd")
    return [g.reshape(2 * s2, fs2 // 2, d) for g in (gwg, gwu, gwd)]


def inproj_fwd(x1, ng, win, bin4, cos, sin):
    t, d = x1.shape
    s4, _, w2 = win.shape
    tm = _row_tile(t)
    dk = d // RET_HEADS
    scale = dk ** -0.5

    def body(x_ref, ng_ref, w_ref, b_ref, cos_ref, sin_ref, p_ref, hb_ref, h_scr):
        s = pl.program_id(1)

        @pl.when(s == 0)
        def _():
            y, _, _ = _rms(x_ref[...], ng_ref[...])
            h_scr[...] = y.astype(bf16)
            hb_ref[...] = y.astype(bf16)

        p = _dot(h_scr[...], w_ref[0]) + b_ref[0]

        @pl.when(s != 1)
        def _():
            p_ref[0] = p.astype(bf16)

        @pl.when(s == 1)
        def _():
            cs, sn = cos_ref[...], sin_ref[...]
            for e in range(2 * RET_HEADS):
                cols = slice(e * dk, (e + 1) * dk)
                rot = _rot(p[:, cols], cs, sn)
                p_ref[0, :, cols] = (rot if e < RET_HEADS else rot * scale).astype(bf16)

    tab = pl.BlockSpec((tm, dk // 2), lambda i, s: (i, 0))
    return pl.pallas_call(
        body, name="inproj_fwd", grid=(t // tm, s4),
        in_specs=[pl.BlockSpec((tm, d), lambda i, s: (i, 0)), pl.BlockSpec((1, d), lambda i, s: (0, 0)),
                  pl.BlockSpec((1, d, w2), lambda i, s: (s, 0, 0)), pl.BlockSpec((1, 1, w2), lambda i, s: (s, 0, 0)),
                  tab, tab],
        out_specs=[pl.BlockSpec((1, tm, w2), lambda i, s: (s, i, 0)), pl.BlockSpec((tm, d), lambda i, s: (i, 0))],
        out_shape=[jax.ShapeDtypeStruct((s4, t, w2), bf16), jax.ShapeDtypeStruct((t, d), bf16)],
        scratch_shapes=[pltpu.VMEM((tm, d), bf16)],
        compiler_params=_cparams(),
    )(x1, ng, win, bin4, cos, sin)


def _sgu_norm(va, ng, nb):
    gv = _gelu(va)
    mu = jnp.mean(gv, axis=-1, keepdims=True)
    xc = gv - mu
    rstd = lax.rsqrt(jnp.mean(xc * xc, axis=-1, keepdims=True) + NORM_EPS)
    xh = xc * rstd
    return xh, rstd, (xh * ng + nb).astype(bf16)


def sgu_fwd(proj, ng, nb, ws, bs):
    _, t, w2 = proj.shape
    d = w2 // 2
    gd = d // SGU_GROUPS
    tm = _row_tile(t)

    def body(p_ref, ng_ref, nb_ref, ws_ref, bs_ref, a_ref):
        ua = p_ref[0, :, 0:d].astype(f32)
        va = p_ref[0, :, d:w2].astype(f32)
        gu = _gelu(ua)
        _, _, vn = _sgu_norm(va, ng_ref[...], nb_ref[...])
        for c in range(tm // SGU_CHUNK):
            rows = slice(c * SGU_CHUNK, (c + 1) * SGU_CHUNK)
            for g in range(SGU_GROUPS):
                cols = slice(g * gd, (g + 1) * gd)
                sg = _dot(ws_ref[g], vn[rows, cols]) + bs_ref[g]
                a_ref[rows, cols] = (gu[rows, cols] * sg).astype(bf16)

    return pl.pallas_call(
        body, name="sgu_fwd", grid=(t // tm,),
        in_specs=[pl.BlockSpec((1, tm, w2), lambda i: (0, i, 0)), pl.BlockSpec((1, d), lambda i: (0, 0)),
                  pl.BlockSpec((1, d), lambda i: (0, 0)), pl.BlockSpec((SGU_GROUPS, SGU_CHUNK, SGU_CHUNK), lambda i: (0, 0, 0)),
                  pl.BlockSpec((SGU_GROUPS, SGU_CHUNK, 1), lambda i: (0, 0, 0))],
        out_specs=pl.BlockSpec((tm, d), lambda i: (i, 0)),
        out_shape=jax.ShapeDtypeStruct((t, d), bf16),
        compiler_params=_cparams(),
    )(proj, ng, nb, ws, bs)


def sgu_bwd(da, proj, ng, nb, ws, bs, dep):
    _, t, w2 = proj.shape
    d = w2 // 2
    gd = d // SGU_GROUPS
    tm = _row_tile(t)

    def body(da_ref, p_ref, ng_ref, nb_ref, ws_ref, bs_ref, dep_ref,
             dua_ref, dva_ref, dws_ref, dbs_ref, dng_ref, dnb_ref, dvn_scr):
        i = pl.program_id(0)
        ua = p_ref[0, :, 0:d].astype(f32)
        va = p_ref[0, :, d:w2].astype(f32)
        gu = _gelu(ua)
        xh, rstd, vn = _sgu_norm(va, ng_ref[...], nb_ref[...])
        dad = da_ref[...].astype(f32)
        dsb = (dad * gu).astype(bf16)
        for c in range(tm // SGU_CHUNK):
            rows = slice(c * SGU_CHUNK, (c + 1) * SGU_CHUNK)
            for g in range(SGU_GROUPS):
                cols = slice(g * gd, (g + 1) * gd)
                sg = _dot(ws_ref[g], vn[rows, cols]) + bs_ref[g]
                dua_ref[rows, cols] = (dad[rows, cols] * sg * _dgelu(ua[rows, cols])).astype(bf16)
                ds = dsb[rows, cols]
                dvn_scr[rows, cols] = _dot_tn(ws_ref[g], ds)
                dw = _dot_nt(ds, vn[rows, cols])
                db = jnp.sum(ds.astype(f32), axis=1, keepdims=True)
                if c == 0:
                    _acc_out(dws_ref.at[g], i == 0, dw)
                    _acc_out(dbs_ref.at[g], i == 0, db)
                else:
                    dws_ref[g] += dw
                    dbs_ref[g] += db
        dvn = dvn_scr[...]
        _acc_out(dng_ref, i == 0, jnp.sum(dvn * xh, axis=0, keepdims=True))
        _acc_out(dnb_ref, i == 0, jnp.sum(dvn, axis=0, keepdims=True))
        dxh = dvn * ng_ref[...]
        dgv = rstd * (dxh - jnp.mean(dxh, axis=-1, keepdims=True) - xh * jnp.mean(dxh * xh, axis=-1, keepdims=True))
        dva_ref[...] = (dgv * _dgelu(va)).astype(bf16)

    row = pl.BlockSpec((tm, d), lambda i: (i, 0))
    vec = pl.BlockSpec((1, d), lambda i: (0, 0))
    wsp = pl.BlockSpec((SGU_GROUPS, SGU_CHUNK, SGU_CHUNK), lambda i: (0, 0, 0))
    bsp = pl.BlockSpec((SGU_GROUPS, SGU_CHUNK, 1), lambda i: (0, 0, 0))
    return pl.pallas_call(
        body, name="sgu_bwd", grid=(t // tm,),
        in_specs=[row, pl.BlockSpec((1, tm, w2), lambda i: (0, i, 0)), vec, vec, wsp, bsp, _ANY],
        out_specs=[row, row, wsp, bsp, vec, vec],
        out_shape=[jax.ShapeDtypeStruct((t, d), bf16), jax.ShapeDtypeStruct((t, d), bf16),
                   jax.ShapeDtypeStruct((SGU_GROUPS, SGU_CHUNK, SGU_CHUNK), f32), jax.ShapeDtypeStruct((SGU_GROUPS, SGU_CHUNK, 1), f32),
                   jax.ShapeDtypeStruct((1, d), f32), jax.ShapeDtypeStruct((1, d), f32)],
        scratch_shapes=[pltpu.VMEM((tm, d), f32)],
        compiler_params=_cparams(),
    )(da, proj, ng, nb, ws, bs, dep)


def retention_constants(decay_logit, t, dk):
    lg = jax.nn.log_sigmoid(decay_logit.astype(f32))
    lgf = lg[0][:, None]
    lgb = lg[1][:, None]
    idx = jnp.arange(CHUNK, dtype=f32)[None, :]
    af = jnp.exp((idx + 1.0) * lgf)
    ab = jnp.exp((CHUNK - idx) * lgb)
    kf = jnp.exp((CHUNK - 1.0 - idx) * lgf)
    kb = jnp.exp(idx * lgb)
    cols = jnp.stack([af, ab, kf, kb, af * (idx + 1.0), ab * (CHUNK - idx), kf * (CHUNK - 1.0 - idx), kb * idx], axis=1)
    cols = cols[..., None]
    diff = idx[0][:, None] - idx[0][None, :]
    dfm = jnp.where(diff >= 0, jnp.exp(jnp.maximum(diff, 0.0)[None] * lgf[:, :, None]), 0.0)
    dbm = jnp.where(diff < 0, jnp.exp(jnp.maximum(-diff, 0.0)[None] * lgb[:, :, None]), 0.0)
    mats = jnp.stack([dfm + dbm, dfm * diff[None], dbm * (-diff)[None]], axis=1)
    cdec = jnp.stack([jnp.broadcast_to(jnp.exp(CHUNK * lgf), (RET_HEADS, dk)),
                      jnp.broadcast_to(jnp.exp(CHUNK * lgb), (RET_HEADS, dk))], axis=1)
    theta = ROPE_BASE ** (-jnp.arange(0, dk, 2, dtype=f32) / dk)
    ang = jnp.arange(t, dtype=f32)[:, None] * theta[None, :]
    return cols, mats, cdec, jnp.cos(ang), jnp.sin(ang)


def _rot(tr, cos, sin):
    half = tr.shape[-1] // 2
    t1 = tr[:, :half]
    t2 = tr[:, half:]
    return jnp.concatenate([t1 * cos - t2 * sin, t2 * cos + t1 * sin], axis=-1)


def _rot_inv(dt, cos, sin):
    half = dt.shape[-1] // 2
    d1 = dt[:, :half]
    d2 = dt[:, half:]
    return jnp.concatenate([d1 * cos + d2 * sin, d2 * cos - d1 * sin], axis=-1)


def _ret_specs(t, d, dk, rt):
    nr = t // rt
    hq = d // dk

    def blk(p, n):
        return (1 - p) * (nr - 1 - n) + p * n

    q_spec = pl.BlockSpec((1, rt, dk), lambda h, p, n: (1, blk(p, n), h))
    k_spec = pl.BlockSpec((1, rt, dk), lambda h, p, n: (1, blk(p, n), hq + h))
    v_spec = pl.BlockSpec((1, rt, dk), lambda h, p, n: (2, blk(p, n), h))
    g_spec = pl.BlockSpec((1, rt, dk), lambda h, p, n: (2, blk(p, n), hq + h))
    tab_spec = pl.BlockSpec((rt, dk // 2), lambda h, p, n: (blk(p, n), 0))
    cols_spec = pl.BlockSpec((1, 8, CHUNK, 1), lambda h, p, n: (h, 0, 0, 0))
    mats_spec = pl.BlockSpec((1, 3, CHUNK, CHUNK), lambda h, p, n: (h, 0, 0, 0))
    cdec_spec = pl.BlockSpec((1, 2, dk), lambda h, p, n: (h, 0, 0))
    in_row = pl.BlockSpec((rt, dk), lambda h, p, n: (blk(p, n), h))
    out_row = pl.BlockSpec((rt, dk), lambda h, p, n: (p * n, h))
    return nr, blk, q_spec, k_spec, v_spec, g_spec, tab_spec, cols_spec, mats_spec, cdec_spec, in_row, out_row


def ret_fwd(proj, cols, mats, cdec):
    _, t, w2 = proj.shape
    d = w2 // 2
    dk = d // RET_HEADS
    rt = _row_tile(t)
    cpt = rt // CHUNK
    nr, blk, q_spec, k_spec, v_spec, g_spec, _, cols_spec, mats_spec, cdec_spec, _, out_row = _ret_specs(t, d, dk, rt)

    def body(q_ref, k_ref, v_ref, g_ref, cols_ref, mats_ref, cdec_ref, r_ref, rn_ref, sb_scr, st):
        p = pl.program_id(1)
        n = pl.program_id(2)
        af, ab, kf, kb = cols_ref[0, 0], cols_ref[0, 1], cols_ref[0, 2], cols_ref[0, 3]
        cf = cdec_ref[0, 0:1, :]
        cb = cdec_ref[0, 1:2, :]

        @pl.when(n == 0)
        def _():
            st[...] = jnp.zeros_like(st)

        @pl.when(p == 0)
        def _():
            for j in reversed(range(cpt)):
                rows = slice(j * CHUNK, (j + 1) * CHUNK)
                ch = blk(p, n) * cpt + j
                kk = k_ref[0, rows, :].astype(f32)
                sb_scr[ch] = st[...].astype(bf16)
                st[...] = st[...] * cb + _dot_tn((kk * kb).astype(bf16), v_ref[0, rows, :])

        @pl.when(p == 1)
        def _():
            for j in range(cpt):
                rows = slice(j * CHUNK, (j + 1) * CHUNK)
                ch = blk(p, n) * cpt + j
                qb = q_ref[0, rows, :]
                kkb = k_ref[0, rows, :]
                q = qb.astype(f32)
                kk = kkb.astype(f32)
                v = v_ref[0, rows, :]
                pm = (_dot_nt(qb, kkb) * mats_ref[0, 0]).astype(bf16)
                out = (_dot(pm, v) + _dot((q * af).astype(bf16), st[...].astype(bf16))
                       + _dot((q * ab).astype(bf16), sb_scr[ch]))
                st[...] = st[...] * cf + _dot_tn((kk * kf).astype(bf16), v)
                rhat = out * lax.rsqrt(jnp.mean(out * out, axis=-1, keepdims=True) + NORM_EPS)
                gg = g_ref[0, rows, :].astype(f32)
                r_ref[rows, :] = out.astype(bf16)
                rn_ref[rows, :] = (rhat * gg * _sigmoid(gg)).astype(bf16)

    return pl.pallas_call(
        body, name="ret_fwd", grid=(RET_HEADS, 2, nr),
        in_specs=[q_spec, k_spec, v_spec, g_spec, cols_spec, mats_spec, cdec_spec],
        out_specs=[out_row, out_row],
        out_shape=[jax.ShapeDtypeStruct((t, d), bf16), jax.ShapeDtypeStruct((t, d), bf16)],
        scratch_shapes=[pltpu.VMEM((t // CHUNK, dk, dk), bf16), pltpu.VMEM((dk, dk), f32)],
        compiler_params=_cparams(),
    )(proj, proj, proj, proj, cols, mats, cdec)


def ret_bwd(drn, r, proj, cols, mats, cdec, cos, sin):
    _, t, w2 = proj.shape
    d = w2 // 2
    dk = d // RET_HEADS
    rt = _row_tile(t)
    cpt = rt // CHUNK
    nr, blk, q_spec, k_spec, v_spec, g_spec, tab_spec, cols_spec, mats_spec, cdec_spec, in_row, out_row = _ret_specs(t, d, dk, rt)
    scale = dk ** -0.5

    def body(drn_ref, r_ref, q_ref, k_ref, v_ref, g_ref, cos_ref, sin_ref, cols_ref, mats_ref, cdec_ref,
             dq_ref, dk_ref, dv_ref, dg_ref, dlg_ref,
             sb_scr, gf_scr, st_s, st_g, acc_af, acc_ab, acc_vf, acc_vb, acc_sf, acc_sb):
        p = pl.program_id(1)
        n = pl.program_id(2)
        af, ab, kf, kb = cols_ref[0, 0], cols_ref[0, 1], cols_ref[0, 2], cols_ref[0, 3]
        af1, ab1, kf1, kb1 = cols_ref[0, 4], cols_ref[0, 5], cols_ref[0, 6], cols_ref[0, 7]
        cf = cdec_ref[0, 0:1, :]
        cb = cdec_ref[0, 1:2, :]

        @pl.when(n == 0)
        def _():
            st_s[...] = jnp.zeros_like(st_s)
            st_g[...] = jnp.zeros_like(st_g)

        @pl.when(jnp.logical_and(n == 0, p == 1))
        def _():
            for a in (acc_af, acc_ab, acc_vf, acc_vb, acc_sf, acc_sb):
                a[...] = jnp.zeros_like(a)

        def load(rows):
            cs, sn = cos_ref[rows, :], sin_ref[rows, :]
            q = q_ref[0, rows, :].astype(f32)
            kk = k_ref[0, rows, :].astype(f32)
            rr = r_ref[rows, :].astype(f32)
            rstd = lax.rsqrt(jnp.mean(rr * rr, axis=-1, keepdims=True) + NORM_EPS)
            rhat = rr * rstd
            gg = g_ref[0, rows, :].astype(f32)
            sg = _sigmoid(gg)
            dd = drn_ref[rows, :].astype(f32)
            drhat = dd * gg * sg
            dout = rstd * (drhat - rhat * jnp.mean(drhat * rhat, axis=-1, keepdims=True))
            dgr = dd * rhat * _dsilu(gg, sg)
            return q, kk, dout.astype(bf16), dgr, cs, sn

        @pl.when(p == 0)
        def _():
            for j in reversed(range(cpt)):
                rows = slice(j * CHUNK, (j + 1) * CHUNK)
                ch = blk(p, n) * cpt + j
                q, kk, doutb, _, _, _ = load(rows)
                sb_scr[ch] = st_s[...].astype(bf16)
                gf_scr[ch] = st_g[...].astype(bf16)
                st_s[...] = st_s[...] * cb + _dot_tn((kk * kb).astype(bf16), v_ref[0, rows, :])
                st_g[...] = st_g[...] * cf + _dot_tn((q * af).astype(bf16), doutb)

        @pl.when(p == 1)
        def _():
            for j in range(cpt):
                rows = slice(j * CHUNK, (j + 1) * CHUNK)
                ch = blk(p, n) * cpt + j
                q, kk, doutb, dgr, cs, sn = load(rows)
                v = v_ref[0, rows, :]
                qb = q_ref[0, rows, :]
                kkb = k_ref[0, rows, :]
                sf = st_s[...]
                gb = st_g[...]
                sfb = sf.astype(bf16)
                gbb = gb.astype(bf16)
                sbb = sb_scr[ch]
                gfb = gf_scr[ch]
                dmat = mats_ref[0, 0]
                scores = _dot_nt(qb, kkb)
                dpraw = _dot_nt(doutb, v)
                dpb = (dpraw * dmat).astype(bf16)
                pmb = (scores * dmat).astype(bf16)
                x1 = _dot_nt(doutb, sfb)
                x2 = _dot_nt(doutb, sbb)
                y1 = _dot_nt(v, gfb)
                y2 = _dot_nt(v, gbb)
                kdf = (kk * kf).astype(bf16)
                kdb = (kk * kb).astype(bf16)
                dq = _dot(dpb, kkb) + x1 * af + x2 * ab
                dkk = _dot_tn(dpb, qb) + y1 * kf + y2 * kb
                dv = _dot_tn(pmb, doutb) + _dot(kdf, gfb) + _dot(kdb, gbb)
                ps = dpraw * scores
                acc_af[...] += ps * mats_ref[0, 1]
                acc_ab[...] += ps * mats_ref[0, 2]
                acc_vf[...] += x1 * q * af1 + y1 * kk * kf1
                acc_vb[...] += x2 * q * ab1 + y2 * kk * kb1
                acc_sf[...] += gfb.astype(f32) * sf
                acc_sb[...] += gb * sbb.astype(f32)
                st_s[...] = sf * cf + _dot_tn(kdf, v)
                st_g[...] = gb * cb + _dot_tn((q * ab).astype(bf16), doutb)
                dq_ref[rows, :] = _rot_inv(dq, cs, sn).astype(bf16)
                dk_ref[rows, :] = (_rot_inv(dkk, cs, sn) * scale).astype(bf16)
                dv_ref[rows, :] = dv.astype(bf16)
                dg_ref[rows, :] = dgr.astype(bf16)

        @pl.when(jnp.logical_and(p == 1, n == nr - 1))
        def _():
            tf = jnp.sum(acc_af[...]) + jnp.sum(acc_vf[...]) + CHUNK * jnp.sum(acc_sf[...] * cf)
            tb = jnp.sum(acc_ab[...]) + jnp.sum(acc_vb[...]) + CHUNK * jnp.sum(acc_sb[...] * cb)
            rid = lax.broadcasted_iota(jnp.int32, (8, 128), 0)
            dlg_ref[0] = jnp.where(rid == 0, tf, jnp.where(rid == 1, tb, 0.0))

    nch = t // CHUNK
    return pl.pallas_call(
        body, name="ret_bwd", grid=(RET_HEADS, 2, nr),
        in_specs=[in_row, in_row, q_spec, k_spec, v_spec, g_spec, tab_spec, tab_spec, cols_spec, mats_spec, cdec_spec],
        out_specs=[out_row, out_row, out_row, out_row, pl.BlockSpec((1, 8, 128), lambda h, p, n: (h, 0, 0))],
        out_shape=[jax.ShapeDtypeStruct((t, d), bf16)] * 4 + [jax.ShapeDtypeStruct((RET_HEADS, 8, 128), f32)],
        scratch_shapes=[pltpu.VMEM((nch, dk, dk), bf16), pltpu.VMEM((nch, dk, dk), bf16),
                        pltpu.VMEM((dk, dk), f32), pltpu.VMEM((dk, dk), f32),
                        pltpu.VMEM((CHUNK, CHUNK), f32), pltpu.VMEM((CHUNK, CHUNK), f32),
                        pltpu.VMEM((CHUNK, dk), f32), pltpu.VMEM((CHUNK, dk), f32),
                        pltpu.VMEM((dk, dk), f32), pltpu.VMEM((dk, dk), f32)],
        compiler_params=_cparams(),
    )(drn, r, proj, proj, proj, proj, cos, sin, cols, mats, cdec)


def mix_fwd(a, rn, proj, wa, wb, wo, x1):
    t, d = x1.shape
    tm = _row_tile(t)

    def body(a_ref, rn_ref, p_ref, wa_ref, wb_ref, wo_ref, x_ref, xo_ref, ba_ref, br_ref):
        ba = _dot(a_ref[...], wa_ref[...])
        br = _dot(rn_ref[...], wb_ref[...])
        sa = _sigmoid(p_ref[0, :, 0:d].astype(f32))
        sb = _sigmoid(p_ref[0, :, d:2 * d].astype(f32))
        mix = (sa * ba + sb * br).astype(bf16)
        xo_ref[...] = x_ref[...] + _dot(mix, wo_ref[...])
        ba_ref[...] = ba.astype(bf16)
        br_ref[...] = br.astype(bf16)

    row = pl.BlockSpec((tm, d), lambda i: (i, 0))
    wsp = pl.BlockSpec((d, d), lambda i: (0, 0))
    return pl.pallas_call(
        body, name="mix_fwd", grid=(t // tm,),
        in_specs=[row, row, pl.BlockSpec((1, tm, 2 * d), lambda i: (3, i, 0)), wsp, wsp, wsp, row],
        out_specs=[row, row, row],
        out_shape=[jax.ShapeDtypeStruct((t, d), f32), jax.ShapeDtypeStruct((t, d), bf16), jax.ShapeDtypeStruct((t, d), bf16)],
        compiler_params=_cparams(),
    )(a, rn, proj, wa, wb, wo, x1)


def mix_bwd_act(dx2, ba, br, proj, wa, wb, wo, dep):
    t, d = dx2.shape
    tm = _row_tile(t)

    def body(dx_ref, ba_ref, br_ref, p_ref, wa_ref, wb_ref, wo_ref, dep_ref,
             da_ref, drn_ref, dga_ref, dgb_ref, mix_ref, dba_ref, dbr_ref, dxb_ref):
        dxb = dx_ref[...].astype(bf16)
        dxb_ref[...] = dxb
        dmix = _dot_nt(dxb, wo_ref[...])
        ba = ba_ref[...].astype(f32)
        br = br_ref[...].astype(f32)
        sa = _sigmoid(p_ref[0, :, 0:d].astype(f32))
        sb = _sigmoid(p_ref[0, :, d:2 * d].astype(f32))
        mix_ref[...] = (sa * ba + sb * br).astype(bf16)
        dba = (dmix * sa).astype(bf16)
        dbr = (dmix * sb).astype(bf16)
        dba_ref[...] = dba
        dbr_ref[...] = dbr
        dga_ref[...] = (dmix * ba * sa * (1.0 - sa)).astype(bf16)
        dgb_ref[...] = (dmix * br * sb * (1.0 - sb)).astype(bf16)
        da_ref[...] = _dot_nt(dba, wa_ref[...]).astype(bf16)
        drn_ref[...] = _dot_nt(dbr, wb_ref[...]).astype(bf16)

    row = pl.BlockSpec((tm, d), lambda i: (i, 0))
    wsp = pl.BlockSpec((d, d), lambda i: (0, 0))
    return pl.pallas_call(
        body, name="mix_bwd_act", grid=(t // tm,),
        in_specs=[row, row, row, pl.BlockSpec((1, tm, 2 * d), lambda i: (3, i, 0)), wsp, wsp, wsp, _ANY],
        out_specs=[row] * 8,
        out_shape=[jax.ShapeDtypeStruct((t, d), bf16)] * 8,
        compiler_params=_cparams(),
    )(dx2, ba, br, proj, wa, wb, wo, dep)


def inproj_bwd_act(segs, win, x1, ng, dx2):
    t, d = x1.shape
    s4 = win.shape[0]
    tm = _row_tile(t) // 2
    nseg = len(segs)

    def body(*refs):
        seg_refs = refs[:nseg]
        w_ref, x_ref, ng_ref, dx2_ref, dx1_ref, db_ref, dng_ref = refs[nseg:]
        i = pl.program_id(0)
        dh = None
        for e, sr in enumerate(seg_refs):
            sb = sr[...]
            part = _dot_nt(sb, w_ref[e // 2, :, (e % 2) * d:(e % 2 + 1) * d])
            dh = part if dh is None else dh + part
            _acc_out(db_ref.at[e], i == 0, jnp.sum(sb.astype(f32), axis=0, keepdims=True))
        _, xh, r = _rms(x_ref[...], ng_ref[...])
        dx1_ref[...] = dx2_ref[...] + _rms_bwd(dh, xh, r, ng_ref[...])
        _acc_out(dng_ref, i == 0, jnp.sum(dh * xh, axis=0, keepdims=True))

    row = pl.BlockSpec((tm, d), lambda i: (i, 0))
    vec = pl.BlockSpec((1, d), lambda i: (0, 0))
    return pl.pallas_call(
        body, name="inproj_bwd_act", grid=(t // tm,),
        in_specs=[row] * nseg + [pl.BlockSpec((s4, d, 2 * d), lambda i: (0, 0, 0), pipeline_mode=pl.Buffered(1)),
                                 row, vec, row],
        out_specs=[row, pl.BlockSpec((nseg, 1, d), lambda i: (0, 0, 0)), vec],
        out_shape=[jax.ShapeDtypeStruct((t, d), f32), jax.ShapeDtypeStruct((nseg, 1, d), f32),
                   jax.ShapeDtypeStruct((1, d), f32)],
        compiler_params=_cparams(),
    )(*segs, win, x1, ng, dx2)


def loss_head(x3, fng, tgt):
    t, d = x3.shape
    tm = _row_tile(t)

    def body(x_ref, g_ref, t_ref, loss_ref, dx_ref, dg_ref):
        i = pl.program_id(0)
        y, xh, r = _rms(x_ref[...], g_ref[...])
        diff = y - t_ref[...]
        part = 0.5 * jnp.sum(jnp.sum(diff * diff, axis=0, keepdims=True), axis=1, keepdims=True) / d
        _acc_out(loss_ref, i == 0, jnp.broadcast_to(part, (1, 128)))
        dy = diff * (1.0 / d)
        dx_ref[...] = _rms_bwd(dy, xh, r, g_ref[...])
        _acc_out(dg_ref, i == 0, jnp.sum(dy * xh, axis=0, keepdims=True))

    row = pl.BlockSpec((tm, d), lambda i: (i, 0))
    vec = pl.BlockSpec((1, d), lambda i: (0, 0))
    return pl.pallas_call(
        body, name="loss_head", grid=(t // tm,),
        in_specs=[row, vec, row],
        out_specs=[pl.BlockSpec((1, 128), lambda i: (0, 0)), row, vec],
        out_shape=[jax.ShapeDtypeStruct((1, 128), f32), jax.ShapeDtypeStruct((t, d), f32), jax.ShapeDtypeStruct((1, d), f32)],
        compiler_params=_cparams(),
    )(x3, fng, tgt)


def _place():
    return lax.axis_index("x"), lax.axis_index("y"), lax.axis_index("c")


def _other_chips(x, y):
    return [(1 - x, y), (x, 1 - y), (1 - x, 1 - y)]


_ANY = pl.BlockSpec(memory_space=pl.ANY)


_HBM = pl.BlockSpec(memory_space=pltpu.HBM)
_SEM = pl.BlockSpec(memory_space=pltpu.SEMAPHORE)
_EFFECT = pltpu.SideEffectType.DATAFLOW_SIDE_EFFECTING


def _hbm(a):
    return pltpu.with_memory_space_constraint(a, pltpu.HBM)


def _half_rows(ref, c):
    half = ref.shape[1] // 2
    return pl.ds(pl.multiple_of(c * half, 16), half)


def _chip_copy(src, dst, send_sem, recv_sem, chip, c):
    return pltpu.make_async_remote_copy(src_ref=src, dst_ref=dst, send_sem=send_sem, recv_sem=recv_sem,
                                        device_id=(chip[0], chip[1], c), device_id_type=MESH)


def gather_start(bufs, groups, name):
    nb, ng = len(bufs), len(groups)

    def body(*refs):
        ins = refs[:nb]
        sems = refs[nb:nb + 2 * ng]
        token = refs[-1]
        x, y, c = _place()
        k = 2 * x + y
        for gi, grp in enumerate(groups):
            for wi, w in enumerate(grp):
                mine = ins[w].at[k, _half_rows(ins[w], c)]
                for j, chip in enumerate(_other_chips(x, y)):
                    _chip_copy(mine, mine, sems[2 * gi].at[3 * wi + j], sems[2 * gi + 1].at[3 * wi + j], chip, c).start()
        token[...] = jnp.zeros_like(token)

    sem_shapes = []
    for grp in groups:
        sem_shapes += [pltpu.SemaphoreType.DMA((3 * len(grp),)), pltpu.SemaphoreType.DMA((3 * len(grp),))]
    outs = pl.pallas_call(
        body, name=name,
        out_shape=sem_shapes + [pltpu.HBM(b.shape, b.dtype) for b in bufs] + [jax.ShapeDtypeStruct((8, 128), f32)],
        in_specs=[_HBM] * nb,
        out_specs=[_SEM] * (2 * ng) + [_HBM] * nb + [pl.BlockSpec(memory_space=pltpu.VMEM)],
        input_output_aliases={w: 2 * ng + w for w in range(nb)},
        compiler_params=pltpu.CompilerParams(has_side_effects=_EFFECT),
    )(*[_hbm(b) for b in bufs])
    sems = [(outs[2 * gi], outs[2 * gi + 1]) for gi in range(ng)]
    return sems, list(outs[2 * ng:2 * ng + nb]), outs[-1]


def gather_wait(bufs, sems, after, name):
    n = len(bufs)

    def body(*refs):
        ins = refs[:n]
        send_sems, recv_sems = refs[n], refs[n + 1]
        x, y, c = _place()
        k = 2 * x + y
        for wi in range(n):
            half = _half_rows(ins[wi], c)
            for j, chip in enumerate(_other_chips(x, y)):
                cp = _chip_copy(ins[wi].at[k, half], ins[wi].at[2 * chip[0] + chip[1], half], send_sems.at[3 * wi + j],
                                recv_sems.at[3 * wi + j], chip, c)
                cp.wait_send()
                cp.wait_recv()

    outs = pl.pallas_call(
        body, name=name,
        out_shape=[pltpu.HBM(b.shape, b.dtype) for b in bufs],
        in_specs=[_HBM] * n + [_SEM, _SEM, _ANY],
        out_specs=[_HBM] * n,
        input_output_aliases={i: i for i in range(n)},
        compiler_params=pltpu.CompilerParams(has_side_effects=_EFFECT),
    )(*bufs, sems[0], sems[1], after)
    return list(outs)


def gather_forward(bufs, name):
    n = len(bufs)

    def body(*refs):
        ins = refs[n:2 * n]
        send_sems, recv_sems = refs[2 * n], refs[2 * n + 1]
        x, y, c = _place()
        copies = []
        for wi in range(n):
            for j, chip in enumerate(_other_chips(x, y)):
                kp = 2 * chip[0] + chip[1]
                got = ins[wi].at[kp, _half_rows(ins[wi], c)]
                cp = pltpu.make_async_remote_copy(
                    src_ref=got, dst_ref=got, send_sem=send_sems.at[3 * wi + j], recv_sem=recv_sems.at[3 * wi + j],
                    device_id=(x, y, 1 - c), device_id_type=MESH)
                cp.start()
                copies.append((cp, wi, kp, j))
        for cp, wi, kp, j in copies:
            cp.wait_send()
            theirs = ins[wi].at[kp, _half_rows(ins[wi], 1 - c)]
            pltpu.make_async_remote_copy(
                src_ref=theirs, dst_ref=theirs, send_sem=send_sems.at[3 * wi + j], recv_sem=recv_sems.at[3 * wi + j],
                device_id=(x, y, 1 - c), device_id_type=MESH).wait_recv()

    outs = pl.pallas_call(
        body, name=name,
        out_shape=[jax.ShapeDtypeStruct(b.shape, b.dtype) for b in bufs],
        in_specs=[_ANY] * n, out_specs=[_ANY] * n,
        input_output_aliases={i: i for i in range(n)},
        scratch_shapes=[pltpu.SemaphoreType.DMA((3 * n,)), pltpu.SemaphoreType.DMA((3 * n,))],
    )(*bufs)
    return list(outs)


def exchange_start(grads, name):
    n = len(grads)
    lands = [lax.empty((3,) + g.shape[1:], g.dtype) for g in grads]

    def body(*refs):
        ins = refs[:n]
        land = refs[n:2 * n]
        send_sems, recv_sems = refs[2 * n], refs[2 * n + 1]
        token = refs[-1]
        x, y, c = _place()
        for wi in range(n):
            for j, chip in enumerate(_other_chips(x, y)):
                _chip_copy(ins[wi].at[2 * chip[0] + chip[1]], land[wi].at[j], send_sems.at[3 * wi + j],
                           recv_sems.at[3 * wi + j], chip, c).start()
        token[...] = jnp.zeros_like(token)

    outs = pl.pallas_call(
        body, name=name,
        out_shape=[pltpu.SemaphoreType.DMA((3 * n,)), pltpu.SemaphoreType.DMA((3 * n,))]
        + [pltpu.HBM(g.shape, g.dtype) for g in grads] + [pltpu.HBM(l.shape, l.dtype) for l in lands]
        + [jax.ShapeDtypeStruct((8, 128), f32)],
        in_specs=[_HBM] * (2 * n),
        out_specs=[_SEM, _SEM] + [_HBM] * (2 * n) + [pl.BlockSpec(memory_space=pltpu.VMEM)],
        input_output_aliases={i: 2 + i for i in range(2 * n)},
        compiler_params=pltpu.CompilerParams(has_side_effects=_EFFECT),
    )(*[_hbm(g) for g in grads], *[_hbm(l) for l in lands])
    return (outs[0], outs[1]), list(outs[2:2 + n]), list(outs[2 + n:2 + 2 * n]), outs[-1]


def exchange_wait(grads, lands, sems, after, name):
    n = len(grads)

    def body(*refs):
        ins = refs[:n]
        land = refs[n:2 * n]
        send_sems, recv_sems = refs[2 * n], refs[2 * n + 1]
        x, y, c = _place()
        for wi in range(n):
            for j, chip in enumerate(_other_chips(x, y)):
                cp = _chip_copy(ins[wi].at[2 * chip[0] + chip[1]], land[wi].at[j], send_sems.at[3 * wi + j],
                                recv_sems.at[3 * wi + j], chip, c)
                cp.wait_send()
                cp.wait_recv()

    outs = pl.pallas_call(
        body, name=name,
        out_shape=[pltpu.HBM(g.shape, g.dtype) for g in grads] + [pltpu.HBM(l.shape, l.dtype) for l in lands],
        in_specs=[_HBM] * (2 * n) + [_SEM, _SEM, _ANY],
        out_specs=[_HBM] * (2 * n),
        input_output_aliases={i: i for i in range(2 * n)},
        compiler_params=pltpu.CompilerParams(has_side_effects=_EFFECT),
    )(*grads, *lands, sems[0], sems[1], after)
    return list(outs[:n]), list(outs[n:])


def _split_start(body, name, n_sems, operands):
    n = len(operands)
    outs = pl.pallas_call(
        body, name=name,
        out_shape=[pltpu.SemaphoreType.DMA((n_sems,)), pltpu.SemaphoreType.DMA((n_sems,))]
        + [pltpu.HBM(o.shape, o.dtype) for o in operands] + [jax.ShapeDtypeStruct((8, 128), f32)],
        in_specs=[_HBM] * n,
        out_specs=[_SEM, _SEM] + [_HBM] * n + [pl.BlockSpec(memory_space=pltpu.VMEM)],
        input_output_aliases={i: 2 + i for i in range(n)},
        compiler_params=pltpu.CompilerParams(has_side_effects=_EFFECT),
    )(*[_hbm(o) for o in operands])
    return (outs[0], outs[1]), list(outs[2:2 + n]), outs[-1]


def _split_wait(body, name, operands, sems, after):
    n = len(operands)
    outs = pl.pallas_call(
        body, name=name,
        out_shape=[pltpu.HBM(o.shape, o.dtype) for o in operands],
        in_specs=[_HBM] * n + [_SEM, _SEM, _ANY],
        out_specs=[_HBM] * n,
        input_output_aliases={i: i for i in range(n)},
        compiler_params=pltpu.CompilerParams(has_side_effects=_EFFECT),
    )(*operands, sems[0], sems[1], after)
    return list(outs)


def _sibling_copy(src, dst, send_sem, recv_sem):
    x, y, c = _place()
    return pltpu.make_async_remote_copy(src_ref=src, dst_ref=dst, send_sem=send_sem, recv_sem=recv_sem,
                                        device_id=(x, y, 1 - c), device_id_type=MESH)


def swap_start(parts, name):
    n = len(parts)

    def body(*refs):
        for w in range(n):
            _sibling_copy(refs[w], refs[n + w], refs[2 * n].at[w], refs[2 * n + 1].at[w]).start()
        refs[-1][...] = jnp.zeros_like(refs[-1])

    sems, ops, token = _split_start(body, name, n, list(parts) + [lax.empty(p.shape, p.dtype) for p in parts])
    return sems, ops[:n], ops[n:], token


def swap_wait(parts, lands, sems, after, name):
    n = len(parts)

    def body(*refs):
        for w in range(n):
            cp = _sibling_copy(refs[w], refs[n + w], refs[2 * n].at[w], refs[2 * n + 1].at[w])
            cp.wait_send()
            cp.wait_recv()

    return _split_wait(body, name, list(parts) + list(lands), sems, after)[n:]


def _all_peers(x, y, c):
    return [(1 - x if m & 4 else x, 1 - y if m & 2 else y, 1 - c if m & 1 else c) for m in range(1, N_DEV)]


def small_start(block):
    land = jnp.broadcast_to(block[None], (N_DEV,) + block.shape)

    def body(b_ref, land_ref, send_sems, recv_sems, b_thru, land_thru, token):
        x, y, c = _place()
        me = 4 * x + 2 * y + c
        for m, peer in enumerate(_all_peers(x, y, c)):
            pltpu.make_async_remote_copy(src_ref=b_ref, dst_ref=land_ref.at[me], send_sem=send_sems.at[m],
                                         recv_sem=recv_sems.at[m], device_id=peer, device_id_type=MESH).start()
        token[...] = jnp.zeros_like(token)

    sems, ops, token = _split_start(body, "small_start", N_DEV - 1, [block, land])
    return sems, ops[0], ops[1], token


def small_wait(block, land, sems, after):
    def body(b_ref, land_ref, send_sems, recv_sems, after_ref, b_thru, land_thru):
        x, y, c = _place()
        for m, (px, py, pc) in enumerate(_all_peers(x, y, c)):
            cp = pltpu.make_async_remote_copy(src_ref=b_ref, dst_ref=land_ref.at[4 * px + 2 * py + pc],
                                              send_sem=send_sems.at[m], recv_sem=recv_sems.at[m],
                                              device_id=(px, py, pc), device_id_type=MESH)
            cp.wait_send()
            cp.wait_recv()

    return _split_wait(body, "small_wait", [block, land], sems, after)[1]


def _adamw(w, g, m, v):
    m = ADAM_B1 * m + (1.0 - ADAM_B1) * g
    v = ADAM_B2 * v + (1.0 - ADAM_B2) * (g * g)
    m_hat = m / (1.0 - ADAM_B1 ** ADAM_STEP)
    v_hat = v / (1.0 - ADAM_B2 ** ADAM_STEP)
    delta = -ADAM_LR * (m_hat / (jnp.sqrt(v_hat) + ADAM_EPS) + ADAM_WD * w)
    return delta, m, v


def _ew_tile(rows):
    for cand in (256, 176, 128, 64, 32, 16, 8):
        if rows % cand == 0:
            return cand
    return rows


def sum_partials(chip, own, land, name):
    _, r, c = own.shape
    tr = _ew_tile(r)

    def body(k_ref, own_ref, p_ref, o_ref):
        o_ref[...] = ((own_ref[0].astype(f32) + p_ref[0].astype(f32)) + p_ref[1].astype(f32)) + p_ref[2].astype(f32)

    return pl.pallas_call(
        body, name=name,
        grid_spec=pltpu.PrefetchScalarGridSpec(
            num_scalar_prefetch=1, grid=(r // tr,),
            in_specs=[pl.BlockSpec((1, tr, c), lambda i, k: (k[0], i, 0)), pl.BlockSpec((3, tr, c), lambda i, k: (0, i, 0))],
            out_specs=pl.BlockSpec((tr, c), lambda i, k: (i, 0))),
        out_shape=jax.ShapeDtypeStruct((r, c), f32),
        compiler_params=_cparams(),
    )(chip, own, land)


def adamw_shard(p_mine, p_sibling, w, m, v, name):
    r, c = w.shape
    tr = _ew_tile(r)

    def body(a_ref, b_ref, w_ref, m_ref, v_ref, g_ref, d_ref, mo_ref, vo_ref):
        g = a_ref[...] + b_ref[...]
        delta, mn, vn = _adamw(w_ref[...], g, m_ref[...], v_ref[...])
        g_ref[...] = g
        d_ref[...] = delta
        mo_ref[...] = mn
        vo_ref[...] = vn

    blk = pl.BlockSpec((tr, c), lambda i: (i, 0))
    return pl.pallas_call(
        body, name=name, grid=(r // tr,),
        in_specs=[blk] * 5, out_specs=[blk] * 4,
        out_shape=[jax.ShapeDtypeStruct((r, c), f32)] * 4,
        compiler_params=_cparams(),
    )(p_mine, p_sibling, w, m, v)


def adamw_small(g8, w, m, v):
    _, r, lanes = g8.shape

    def body(g_ref, w_ref, m_ref, v_ref, go_ref, d_ref, mo_ref, vo_ref):
        g = g_ref[0]
        for i in range(1, N_DEV):
            g = g + g_ref[i]
        delta, mn, vn = _adamw(w_ref[...], g, m_ref[...], v_ref[...])
        go_ref[...] = g
        d_ref[...] = delta
        mo_ref[...] = mn
        vo_ref[...] = vn

    return pl.pallas_call(
        body, name="adamw_small",
        out_shape=[jax.ShapeDtypeStruct((r, lanes), f32)] * 4,
        compiler_params=_cparams(),
    )(g8, w, m, v)


def _size(shape):
    n = 1
    for e in shape:
        n *= e
    return n


def _pack_rows(shapes):
    rows = [-(-_size(s) // 1024) * 8 for s in shapes]
    return rows, sum(rows)


def _pack(arrs, shapes):
    rows, _ = _pack_rows(shapes)
    parts = [jnp.pad(a.reshape(-1).astype(f32), (0, r * 128 - _size(s))).reshape(r, 128)
             for a, s, r in zip(arrs, shapes, rows)]
    return jnp.concatenate(parts, axis=0)


def _unpack(block, shapes):
    rows, _ = _pack_rows(shapes)
    out, off = [], 0
    for s, r in zip(shapes, rows):
        out.append(block[off:off + r].reshape(-1)[:_size(s)].reshape(s))
        off += r
    return out


TRANSPOSED = ("ffn1_w_gate", "ffn1_w_up", "ffn2_w_gate", "ffn2_w_up")


def _shard2d(a, n):
    return a[0].T if n in TRANSPOSED else a[0]


def _unshard(a, n):
    return (a.T if n in TRANSPOSED else a)[None]


BIG = ("ffn1_w_gate", "ffn1_w_up", "ffn1_w_down", "w_in", "w_branch_a", "w_branch_b", "w_out",
       "ffn2_w_gate", "ffn2_w_up", "ffn2_w_down")
SMALL = ("ffn1_norm", "mix_norm", "b_in", "sgu_norm_g", "sgu_norm_b", "sgu_w_s", "sgu_b_s", "ret_decay_logit",
         "ffn2_norm", "final_norm")
WEIGHTS = ("ffn1_norm", "ffn1_w_gate", "ffn1_w_up", "ffn1_w_down", "mix_norm", "w_in", "b_in", "sgu_norm_g",
           "sgu_norm_b", "sgu_w_s", "sgu_b_s", "ret_decay_logit", "w_branch_a", "w_branch_b", "w_out", "ffn2_norm",
           "ffn2_w_gate", "ffn2_w_up", "ffn2_w_down", "final_norm")


def kernel(x, ffn1_norm, ffn1_w_gate, ffn1_w_up, ffn1_w_down, mix_norm, w_in, b_in, sgu_norm_g, sgu_norm_b, sgu_w_s, sgu_b_s, ret_decay_logit, w_branch_a, w_branch_b, w_out, ffn2_norm, ffn2_w_gate, ffn2_w_up, ffn2_w_down, final_norm, loss_target, m_ffn1_norm, m_ffn1_w_gate, m_ffn1_w_up, m_ffn1_w_down, m_mix_norm, m_w_in, m_b_in, m_sgu_norm_g, m_sgu_norm_b, m_sgu_w_s, m_sgu_b_s, m_ret_decay_logit, m_w_branch_a, m_w_branch_b, m_w_out, m_ffn2_norm, m_ffn2_w_gate, m_ffn2_w_up, m_ffn2_w_down, m_final_norm, v_ffn1_norm, v_ffn1_w_gate, v_ffn1_w_up, v_ffn1_w_down, v_mix_norm, v_w_in, v_b_in, v_sgu_norm_g, v_sgu_norm_b, v_sgu_w_s, v_sgu_b_s, v_ret_decay_logit, v_w_branch_a, v_w_branch_b, v_w_out, v_ffn2_norm, v_ffn2_w_gate, v_ffn2_w_up, v_ffn2_w_down, v_final_norm):
    p = dict(ffn1_norm=ffn1_norm, ffn1_w_gate=ffn1_w_gate, ffn1_w_up=ffn1_w_up, ffn1_w_down=ffn1_w_down,
             mix_norm=mix_norm, w_in=w_in, b_in=b_in, sgu_norm_g=sgu_norm_g, sgu_norm_b=sgu_norm_b, sgu_w_s=sgu_w_s,
             sgu_b_s=sgu_b_s, ret_decay_logit=ret_decay_logit, w_branch_a=w_branch_a, w_branch_b=w_branch_b,
             w_out=w_out, ffn2_norm=ffn2_norm, ffn2_w_gate=ffn2_w_gate, ffn2_w_up=ffn2_w_up, ffn2_w_down=ffn2_w_down,
             final_norm=final_norm)
    mom = dict(ffn1_norm=m_ffn1_norm, ffn1_w_gate=m_ffn1_w_gate, ffn1_w_up=m_ffn1_w_up, ffn1_w_down=m_ffn1_w_down,
               mix_norm=m_mix_norm, w_in=m_w_in, b_in=m_b_in, sgu_norm_g=m_sgu_norm_g, sgu_norm_b=m_sgu_norm_b,
               sgu_w_s=m_sgu_w_s, sgu_b_s=m_sgu_b_s, ret_decay_logit=m_ret_decay_logit, w_branch_a=m_w_branch_a,
               w_branch_b=m_w_branch_b, w_out=m_w_out, ffn2_norm=m_ffn2_norm, ffn2_w_gate=m_ffn2_w_gate,
               ffn2_w_up=m_ffn2_w_up, ffn2_w_down=m_ffn2_w_down, final_norm=m_final_norm)
    var = dict(ffn1_norm=v_ffn1_norm, ffn1_w_gate=v_ffn1_w_gate, ffn1_w_up=v_ffn1_w_up, ffn1_w_down=v_ffn1_w_down,
               mix_norm=v_mix_norm, w_in=v_w_in, b_in=v_b_in, sgu_norm_g=v_sgu_norm_g, sgu_norm_b=v_sgu_norm_b,
               sgu_w_s=v_sgu_w_s, sgu_b_s=v_sgu_b_s, ret_decay_logit=v_ret_decay_logit, w_branch_a=v_w_branch_a,
               w_branch_b=v_w_branch_b, w_out=v_w_out, ffn2_norm=v_ffn2_norm, ffn2_w_gate=v_ffn2_w_gate,
               ffn2_w_up=v_ffn2_w_up, ffn2_w_down=v_ffn2_w_down, final_norm=v_final_norm)

    xs = x[0]
    tgt = loss_target[0]
    t, d = xs.shape
    dk = d // RET_HEADS
    tm = _row_tile(t)

    shards2d = {n: _shard2d(p[n], n) for n in BIG}
    chip = (2 * lax.axis_index("x") + lax.axis_index("y")).astype(jnp.int32).reshape(1)
    groups = {"ffn1": ("ffn1_w_gate", "ffn1_w_up", "ffn1_w_down"), "in": ("w_in",),
              "mix": ("w_branch_a", "w_branch_b", "w_out"), "ffn2": ("ffn2_w_gate", "ffn2_w_up", "ffn2_w_down")}
    def own_slot(n):
        sh = shards2d[n].astype(bf16)
        return lax.dynamic_update_index_in_dim(lax.empty((N_CHIPS,) + sh.shape, bf16), sh, chip[0], 0)

    sems, bufs, tok = gather_start([own_slot(n) for n in groups["ffn1"]], [[0, 1, 2]], "gather_start_ffn1")
    gsem = {"ffn1": sems[0]}
    pending = dict(zip(groups["ffn1"], bufs))
    rest = [n for g in ("in", "mix", "ffn2") for n in groups[g]]
    sems, bufs, tok_rest = gather_start([own_slot(n) for n in rest],
                                 [[rest.index(n) for n in groups[g]] for g in ("in", "mix", "ffn2")], "gather_start_rest")
    gsem.update(zip(("in", "mix", "ffn2"), sems))
    pending.update(zip(rest, bufs))

    def arrive(gs, after):
        got = []
        for g in gs:
            got += gather_wait([pending[n] for n in groups[g]], gsem[g], after, "gather_wait_" + g)
        return gather_forward(got, "gather_forward_" + gs[0])

    bin4 = b_in.reshape(N_CHIPS, 1, 2 * d)
    ws_b = sgu_w_s[0].astype(bf16)
    bs_c = sgu_b_s[0][:, :, None]
    cols, mats, cdec, cos, sin = retention_constants(ret_decay_logit[0], t, dk)

    wg1, wu1, wd1 = [_pair_shards(w) for w in arrive(["ffn1"], tok_rest)]
    x1, g1, u1 = ffn_fwd(xs, ffn1_norm, wg1, wu1, wd1, "ffn1_fwd")
    win, = arrive(["in"], x1)
    proj, hb2 = inproj_fwd(x1, mix_norm, win, bin4, cos, sin)
    a = sgu_fwd(proj, sgu_norm_g, sgu_norm_b, ws_b, bs_c)
    r, rn = ret_fwd(proj, cols, mats, cdec)
    wa, wb, wo, wg2, wu2, wd2 = arrive(["mix", "ffn2"], rn)
    wa, wb, wo = [w.reshape(d, d) for w in (wa, wb, wo)]
    wg2, wu2, wd2 = [_pair_shards(w) for w in (wg2, wu2, wd2)]
    x2, ba, br = mix_fwd(a, rn, proj, wa, wb, wo, x1)
    x3, g2, u2 = ffn_fwd(x2, ffn2_norm, wg2, wu2, wd2, "ffn2_fwd")
    loss_blk, dx3, d_final = loss_head(x3, final_norm.reshape(1, d), tgt)

    sent = {}
    dx2, dg2, du2, act2, hb3, dyb2, d_ffn2n = ffn_bwd_act(dx3, x2, ffn2_norm, g2, u2, wg2, wu2, wd2, "ffn2_bwd_act", tok)
    sent["ffn2"] = exchange_start(list(ffn_weight_grads(hb3, dyb2, dg2, du2, act2, "ffn2_grad")), "exchange_start_ffn2")
    da, drn, dga, dgb, mixb, dba, dbr, dx2b = mix_bwd_act(dx2, ba, br, proj, wa, wb, wo, sent["ffn2"][3])
    tg = min(t, 2048)
    row = pl.BlockSpec((tg, d), lambda s, i: (i, 0))

    def square_grad(xa, ya, name):
        return tn_matmul(xa, [ya], row, [row], 1, d, [d], t, tg, name).reshape(N_CHIPS, d // N_CHIPS, d)

    sent["mix"] = exchange_start([square_grad(a, dba, "grad_w_branch_a"), square_grad(rn, dbr, "grad_w_branch_b"),
                                  square_grad(mixb, dx2b, "grad_w_out")], "exchange_start_mix")
    dua, dva, d_ws, d_bs, d_sng, d_snb = sgu_bwd(da, proj, sgu_norm_g, sgu_norm_b, ws_b, bs_c, sent["mix"][3])
    dq, dkr, dv, dgr, dlg = ret_bwd(drn, r, proj, cols, mats, cdec, cos, sin)
    segs = [dua, dva, dq, dkr, dv, dgr, dga, dgb]
    dx1, d_bin, d_mixn = inproj_bwd_act(segs, win, x1, mix_norm, dx2)
    sent["in"] = exchange_start([jnp.concatenate(
        [tn_matmul(hb2, [segs[2 * s], segs[2 * s + 1]], row, [row, row], 1, d, [d, d], t, tg, "grad_w_in_%d" % s)
         for s in range(N_CHIPS)], axis=0)], "exchange_start_in")
    grad_x, dg1, du1, act1, hb1, dyb1, d_ffn1n = ffn_bwd_act(dx1, xs, ffn1_norm, g1, u1, wg1, wu1, wd1, "ffn1_bwd_act",
                                                              sent["in"][3])
    dlogit = dlg[:, 0:2, 0].T * jax.nn.sigmoid(-ret_decay_logit[0].astype(f32))
    small_g = dict(ffn1_norm=d_ffn1n, mix_norm=d_mixn, b_in=d_bin, sgu_norm_g=d_sng, sgu_norm_b=d_snb, sgu_w_s=d_ws,
                   sgu_b_s=d_bs, ret_decay_logit=dlogit, ffn2_norm=d_ffn2n, final_norm=d_final)
    shapes = [p[n].shape for n in SMALL]
    small_sems, small_blk, small_land, _ = small_start(_pack([small_g[n] for n in SMALL], shapes))
    sent["ffn1"] = exchange_start(list(ffn_weight_grads(hb1, dyb1, dg1, du1, act1, "ffn1_grad")), "exchange_start_ffn1")

    out_g, out_d, out_m, out_v = {}, {}, {}, {}
    swaps = {}

    def reduce_plane(g, after):
        gsems, own, lands, _ = sent[g]
        own, lands = exchange_wait(own, lands, gsems, after, "exchange_wait_" + g)
        plane = [sum_partials(chip, o, l, "sum_" + n) for n, o, l in zip(groups[g], own, lands)]
        swaps[g] = swap_start(plane, "swap_start_" + g)
        return swaps[g][3]

    def update(g, after):
        ssems, plane, lands, _ = swaps[g]
        other = swap_wait(plane, lands, ssems, after, "swap_wait_" + g)
        for n, mine, sib in zip(groups[g], plane, other):
            res = adamw_shard(mine, sib, shards2d[n], _shard2d(mom[n], n), _shard2d(var[n], n), "adamw_" + n)
            out_g[n], out_d[n], out_m[n], out_v[n] = [_unshard(o, n) for o in res]
        return out_g[groups[g][-1]]

    after = reduce_plane("ffn2", sent["ffn1"][3])
    after = reduce_plane("mix", after)
    after = update("ffn2", after)
    after = reduce_plane("in", after)
    after = update("mix", after)
    g8 = small_wait(small_blk, small_land, small_sems, after)
    sg, sd, sm, sv = adamw_small(g8, _pack([p[n] for n in SMALL], shapes), _pack([mom[n] for n in SMALL], shapes),
                                 _pack([var[n] for n in SMALL], shapes))
    for res, blockv in ((out_g, sg), (out_d, sd), (out_m, sm), (out_v, sv)):
        for n, val in zip(SMALL, _unpack(blockv, shapes)):
            res[n] = val
    after = update("in", sg)
    after = reduce_plane("ffn1", after)
    update("ffn1", after)

    loss = lax.psum(loss_blk[0, 0], ("x", "y", "c"))
    return (loss, grad_x[None], *[out_g[n] for n in WEIGHTS], *[out_d[n] for n in WEIGHTS],
            *[out_m[n] for n in WEIGHTS], *[out_v[n] for n in WEIGHTS])
```

```python
import functools

import jax
import jax.numpy as jnp
from jax import lax
from jax.experimental import pallas as pl
from jax.experimental.pallas import tpu as pltpu

f32 = jnp.float32
bf16 = jnp.bfloat16

SGU_CHUNK = 128
CHUNK = 128
RET_HEADS = 4
SGU_GROUPS = 4
ROPE_BASE = 10000.0
NORM_EPS = 1e-6
ADAM_LR = 0.001
ADAM_B1 = 0.9
ADAM_B2 = 0.999
ADAM_EPS = 1e-08
ADAM_WD = 0.01
ADAM_STEP = 10
N_CHIPS = 4
N_DEV = 8
MESH = pl.DeviceIdType.MESH
VMEM_LIMIT = 52 * 1024 * 1024
VMEM_LIMIT_WIDE = 62 * 1024 * 1024

_NT = (((1,), (1,)), ((), ()))
_TN = (((0,), (0,)), ((), ()))


def _cparams(limit=None):
    return pltpu.CompilerParams(vmem_limit_bytes=VMEM_LIMIT if limit is None else limit)


def _row_tile(t):
    return 512 if t >= 2048 else t // 2


def _dot(a, b):
    return jnp.dot(a, b, preferred_element_type=f32)


def _dot_nt(a, b):
    return lax.dot_general(a, b, _NT, preferred_element_type=f32)


def _dot_tn(a, b):
    return lax.dot_general(a, b, _TN, preferred_element_type=f32)


def _rms(x, g):
    r = lax.rsqrt(jnp.mean(x * x, axis=-1, keepdims=True) + NORM_EPS)
    xh = x * r
    return xh * g, xh, r


def _rms_bwd(dy, xh, r, g):
    dxh = dy * g
    return r * (dxh - xh * jnp.mean(dxh * xh, axis=-1, keepdims=True))


def _sigmoid(x):
    return jax.nn.sigmoid(x)


def _dsilu(g, sg):
    return sg * (1.0 + g * (1.0 - sg))


def _gelu(x):
    return 0.5 * x * (1.0 + lax.erf(x * 0.7071067811865476))


def _dgelu(x):
    return 0.5 * (1.0 + lax.erf(x * 0.7071067811865476)) + x * jnp.exp(-0.5 * x * x) * 0.3989422804014327


def _acc_out(ref, first, val):
    @pl.when(first)
    def _():
        ref[...] = val

    @pl.when(jnp.logical_not(first))
    def _():
        ref[...] += val


def ffn_fwd(x, ng, wg, wu, wd, name):
    t, d = x.shape
    s4, fs, _ = wg.shape
    tm = _row_tile(t)

    def body(x_ref, ng_ref, wg_ref, wu_ref, wd_ref, xo_ref, g_ref, u_ref, h_scr, acc_scr):
        s = pl.program_id(1)

        @pl.when(s == 0)
        def _():
            y, _, _ = _rms(x_ref[...], ng_ref[...])
            h_scr[...] = y.astype(bf16)
            acc_scr[...] = jnp.zeros_like(acc_scr)

        h = h_scr[...]
        g = _dot_nt(h, wg_ref[0])
        u = _dot_nt(h, wu_ref[0])
        g_ref[0] = g.astype(bf16)
        u_ref[0] = u.astype(bf16)
        act = (g * _sigmoid(g) * u).astype(bf16)
        acc_scr[...] += _dot(act, wd_ref[0])

        @pl.when(s == s4 - 1)
        def _():
            xo_ref[...] = x_ref[...] + 0.5 * acc_scr[...]

    return pl.pallas_call(
        body, name=name, grid=(t // tm, s4),
        in_specs=[pl.BlockSpec((tm, d), lambda i, s: (i, 0)), pl.BlockSpec((1, d), lambda i, s: (0, 0)),
                  pl.BlockSpec((1, fs, d), lambda i, s: (s, 0, 0)), pl.BlockSpec((1, fs, d), lambda i, s: (s, 0, 0)),
                  pl.BlockSpec((1, fs, d), lambda i, s: (s, 0, 0))],
        out_specs=[pl.BlockSpec((tm, d), lambda i, s: (i, 0)), pl.BlockSpec((1, tm, fs), lambda i, s: (s, i, 0)),
                   pl.BlockSpec((1, tm, fs), lambda i, s: (s, i, 0))],
        out_shape=[jax.ShapeDtypeStruct((t, d), f32), jax.ShapeDtypeStruct((s4, t, fs), bf16),
                   jax.ShapeDtypeStruct((s4, t, fs), bf16)],
        scratch_shapes=[pltpu.VMEM((tm, d), bf16), pltpu.VMEM((tm, d), f32)],
        compiler_params=_cparams(),
    )(x, ng, wg, wu, wd)


def ffn_bwd_act(dxo, x, ng, g, u, wg, wu, wd, name, dep):
    t, d = x.shape
    s4, fs, _ = wg.shape
    tm = _row_tile(t)

    def body(dxo_ref, x_ref, ng_ref, g_ref, u_ref, wg_ref, wu_ref, wd_ref, dep_ref,
             dx_ref, dg_ref, du_ref, act_ref, hb_ref, dyb_ref, dng_ref, dy_scr, acc_scr):
        i = pl.program_id(0)
        s = pl.program_id(1)

        @pl.when(s == 0)
        def _():
            dyb = (0.5 * dxo_ref[...]).astype(bf16)
            dy_scr[...] = dyb
            dyb_ref[...] = dyb
            acc_scr[...] = jnp.zeros_like(acc_scr)

        dact = _dot_nt(dy_scr[...], wd_ref[0])
        gg = g_ref[0].astype(f32)
        uu = u_ref[0].astype(f32)
        sg = _sigmoid(gg)
        sil = gg * sg
        dgb = (dact * uu * _dsilu(gg, sg)).astype(bf16)
        dub = (dact * sil).astype(bf16)
        dg_ref[0] = dgb
        du_ref[0] = dub
        act_ref[0] = (sil * uu).astype(bf16)
        acc_scr[...] += _dot(dgb, wg_ref[0]) + _dot(dub, wu_ref[0])

        @pl.when(s == s4 - 1)
        def _():
            y, xh, r = _rms(x_ref[...], ng_ref[...])
            hb_ref[...] = y.astype(bf16)
            dh = acc_scr[...]
            dx_ref[...] = dxo_ref[...] + _rms_bwd(dh, xh, r, ng_ref[...])
            _acc_out(dng_ref, i == 0, jnp.sum(dh * xh, axis=0, keepdims=True))

    row = lambda i, s: (i, 0)
    shard = lambda i, s: (s, i, 0)
    wsp = lambda i, s: (s, 0, 0)
    return pl.pallas_call(
        body, name=name, grid=(t // tm, s4),
        in_specs=[pl.BlockSpec((tm, d), row), pl.BlockSpec((tm, d), row), pl.BlockSpec((1, d), lambda i, s: (0, 0)),
                  pl.BlockSpec((1, tm, fs), shard), pl.BlockSpec((1, tm, fs), shard),
                  pl.BlockSpec((1, fs, d), wsp), pl.BlockSpec((1, fs, d), wsp), pl.BlockSpec((1, fs, d), wsp), _ANY],
        out_specs=[pl.BlockSpec((tm, d), row), pl.BlockSpec((1, tm, fs), shard), pl.BlockSpec((1, tm, fs), shard),
                   pl.BlockSpec((1, tm, fs), shard), pl.BlockSpec((tm, d), row), pl.BlockSpec((tm, d), row),
                   pl.BlockSpec((1, d), lambda i, s: (0, 0))],
        out_shape=[jax.ShapeDtypeStruct((t, d), f32), jax.ShapeDtypeStruct((s4, t, fs), bf16),
                   jax.ShapeDtypeStruct((s4, t, fs), bf16), jax.ShapeDtypeStruct((s4, t, fs), bf16),
                   jax.ShapeDtypeStruct((t, d), bf16), jax.ShapeDtypeStruct((t, d), bf16),
                   jax.ShapeDtypeStruct((1, d), f32)],
        scratch_shapes=[pltpu.VMEM((tm, d), bf16), pltpu.VMEM((tm, d), f32)],
        compiler_params=_cparams(VMEM_LIMIT_WIDE),
    )(dxo, x, ng, g, u, wg, wu, wd, dep)


def tn_matmul(xs, ys, x_spec, y_specs, n_shards, k1, k2s, t, tm, name, dep):
    k2 = sum(k2s)
    ny = len(ys)

    def body(*refs):
        x_ref = refs[0]
        y_refs = refs[1:1 + ny]
        o_ref = refs[2 + ny]
        acc = refs[3 + ny]
        i = pl.program_id(1)
        xb = x_ref[0] if len(x_ref.shape) == 3 else x_ref[...]
        off = 0
        for y_ref, w in zip(y_refs, k2s):
            yb = y_ref[0] if len(y_ref.shape) == 3 else y_ref[...]
            part = _dot_tn(xb, yb)
            sl = (slice(None), slice(off, off + w))

            @pl.when(i == 0)
            def _(part=part, sl=sl):
                acc[sl] = part

            @pl.when(i > 0)
            def _(part=part, sl=sl):
                acc[sl] += part

            off += w

        @pl.when(i == t // tm - 1)
        def _():
            o_ref[0] = acc[...].astype(bf16)

    return pl.pallas_call(
        body, name=name, grid=(n_shards, t // tm),
        in_specs=[x_spec] + list(y_specs) + [_ANY],
        out_specs=pl.BlockSpec((1, k1, k2), lambda s, i: (s, 0, 0)),
        out_shape=jax.ShapeDtypeStruct((n_shards, k1, k2), bf16),
        scratch_shapes=[pltpu.VMEM((k1, k2), f32)],
        compiler_params=_cparams(),
    )(xs, *ys, dep)


def _pair_shards(w):
    s4, fs, d = w.shape
    return w.reshape(s4 // 2, 2 * fs, d)


def ffn_weight_grads(hb, dyb, dg, du, act, name, dep):
    t, d = hb.shape
    s2, _, fs2 = dg.shape
    tm = t
    row = pl.BlockSpec((tm, d), lambda s, i: (i, 0))
    shard = pl.BlockSpec((1, tm, fs2), lambda s, i: (s, i, 0))
    gwg = tn_matmul(dg, [hb], shard, [row], s2, fs2, [d], t, tm, name + "_wg", dep)
    gwu = tn_matmul(du, [hb], shard, [row], s2, fs2, [d], t, tm, name + "_wu", dep)
    gwd = tn_matmul(act, [dyb], shard, [row], s2, fs2, [d], t, tm, name + "_wd", dep)
    return [g.reshape(2 * s2, fs2 // 2, d) for g in (gwg, gwu, gwd)]


def inproj_fwd(x1, ng, win, bin4, cos, sin):
    t, d = x1.shape
    s4, _, w2 = win.shape
    tm = _row_tile(t)
    dk = d // RET_HEADS
    scale = dk ** -0.5

    def body(x_ref, ng_ref, w_ref, b_ref, cos_ref, sin_ref, p_ref, hb_ref, h_scr):
        s = pl.program_id(1)

        @pl.when(s == 0)
        def _():
            y, _, _ = _rms(x_ref[...], ng_ref[...])
            h_scr[...] = y.astype(bf16)
            hb_ref[...] = y.astype(bf16)

        p = _dot(h_scr[...], w_ref[0]) + b_ref[0]

        @pl.when(s != 1)
        def _():
            p_ref[0] = p.astype(bf16)

        @pl.when(s == 1)
        def _():
            cs, sn = cos_ref[...], sin_ref[...]
            for e in range(2 * RET_HEADS):
                cols = slice(e * dk, (e + 1) * dk)
                rot = _rot(p[:, cols], cs, sn)
                p_ref[0, :, cols] = (rot if e < RET_HEADS else rot * scale).astype(bf16)

    tab = pl.BlockSpec((tm, dk // 2), lambda i, s: (i, 0))
    return pl.pallas_call(
        body, name="inproj_fwd", grid=(t // tm, s4),
        in_specs=[pl.BlockSpec((tm, d), lambda i, s: (i, 0)), pl.BlockSpec((1, d), lambda i, s: (0, 0)),
                  pl.BlockSpec((1, d, w2), lambda i, s: (s, 0, 0)), pl.BlockSpec((1, 1, w2), lambda i, s: (s, 0, 0)),
                  tab, tab],
        out_specs=[pl.BlockSpec((1, tm, w2), lambda i, s: (s, i, 0)), pl.BlockSpec((tm, d), lambda i, s: (i, 0))],
        out_shape=[jax.ShapeDtypeStruct((s4, t, w2), bf16), jax.ShapeDtypeStruct((t, d), bf16)],
        scratch_shapes=[pltpu.VMEM((tm, d), bf16)],
        compiler_params=_cparams(),
    )(x1, ng, win, bin4, cos, sin)


def _sgu_norm(va, ng, nb):
    gv = _gelu(va)
    mu = jnp.mean(gv, axis=-1, keepdims=True)
    xc = gv - mu
    rstd = lax.rsqrt(jnp.mean(xc * xc, axis=-1, keepdims=True) + NORM_EPS)
    xh = xc * rstd
    return xh, rstd, (xh * ng + nb).astype(bf16)


def sgu_fwd(proj, ng, nb, ws, bs):
    _, t, w2 = proj.shape
    d = w2 // 2
    gd = d // SGU_GROUPS
    tm = _row_tile(t)

    def body(p_ref, ng_ref, nb_ref, ws_ref, bs_ref, a_ref):
        ua = p_ref[0, :, 0:d].astype(f32)
        va = p_ref[0, :, d:w2].astype(f32)
        gu = _gelu(ua)
        _, _, vn = _sgu_norm(va, ng_ref[...], nb_ref[...])
        for c in range(tm // SGU_CHUNK):
            rows = slice(c * SGU_CHUNK, (c + 1) * SGU_CHUNK)
            for g in range(SGU_GROUPS):
                cols = slice(g * gd, (g + 1) * gd)
                sg = _dot(ws_ref[g], vn[rows, cols]) + bs_ref[g]
                a_ref[rows, cols] = (gu[rows, cols] * sg).astype(bf16)

    return pl.pallas_call(
        body, name="sgu_fwd", grid=(t // tm,),
        in_specs=[pl.BlockSpec((1, tm, w2), lambda i: (0, i, 0)), pl.BlockSpec((1, d), lambda i: (0, 0)),
                  pl.BlockSpec((1, d), lambda i: (0, 0)), pl.BlockSpec((SGU_GROUPS, SGU_CHUNK, SGU_CHUNK), lambda i: (0, 0, 0)),
                  pl.BlockSpec((SGU_GROUPS, SGU_CHUNK, 1), lambda i: (0, 0, 0))],
        out_specs=pl.BlockSpec((tm, d), lambda i: (i, 0)),
        out_shape=jax.ShapeDtypeStruct((t, d), bf16),
        compiler_params=_cparams(),
    )(proj, ng, nb, ws, bs)


def sgu_bwd(da, proj, ng, nb, ws, bs, dep):
    _, t, w2 = proj.shape
    d = w2 // 2
    gd = d // SGU_GROUPS
    tm = _row_tile(t)

    def body(da_ref, p_ref, ng_ref, nb_ref, ws_ref, bs_ref, dep_ref,
             dua_ref, dva_ref, dws_ref, dbs_ref, dng_ref, dnb_ref, dvn_scr):
        i = pl.program_id(0)
        ua = p_ref[0, :, 0:d].astype(f32)
        va = p_ref[0, :, d:w2].astype(f32)
        gu = _gelu(ua)
        xh, rstd, vn = _sgu_norm(va, ng_ref[...], nb_ref[...])
        dad = da_ref[...].astype(f32)
        dsb = (dad * gu).astype(bf16)
        for c in range(tm // SGU_CHUNK):
            rows = slice(c * SGU_CHUNK, (c + 1) * SGU_CHUNK)
            for g in range(SGU_GROUPS):
                cols = slice(g * gd, (g + 1) * gd)
                sg = _dot(ws_ref[g], vn[rows, cols]) + bs_ref[g]
                dua_ref[rows, cols] = (dad[rows, cols] * sg * _dgelu(ua[rows, cols])).astype(bf16)
                ds = dsb[rows, cols]
                dvn_scr[rows, cols] = _dot_tn(ws_ref[g], ds)
                dw = _dot_nt(ds, vn[rows, cols])
                db = jnp.sum(ds.astype(f32), axis=1, keepdims=True)
                if c == 0:
                    _acc_out(dws_ref.at[g], i == 0, dw)
                    _acc_out(dbs_ref.at[g], i == 0, db)
                else:
                    dws_ref[g] += dw
                    dbs_ref[g] += db
        dvn = dvn_scr[...]
        _acc_out(dng_ref, i == 0, jnp.sum(dvn * xh, axis=0, keepdims=True))
        _acc_out(dnb_ref, i == 0, jnp.sum(dvn, axis=0, keepdims=True))
        dxh = dvn * ng_ref[...]
        dgv = rstd * (dxh - jnp.mean(dxh, axis=-1, keepdims=True) - xh * jnp.mean(dxh * xh, axis=-1, keepdims=True))
        dva_ref[...] = (dgv * _dgelu(va)).astype(bf16)

    row = pl.BlockSpec((tm, d), lambda i: (i, 0))
    vec = pl.BlockSpec((1, d), lambda i: (0, 0))
    wsp = pl.BlockSpec((SGU_GROUPS, SGU_CHUNK, SGU_CHUNK), lambda i: (0, 0, 0))
    bsp = pl.BlockSpec((SGU_GROUPS, SGU_CHUNK, 1), lambda i: (0, 0, 0))
    return pl.pallas_call(
        body, name="sgu_bwd", grid=(t // tm,),
        in_specs=[row, pl.BlockSpec((1, tm, w2), lambda i: (0, i, 0)), vec, vec, wsp, bsp, _ANY],
        out_specs=[row, row, wsp, bsp, vec, vec],
        out_shape=[jax.ShapeDtypeStruct((t, d), bf16), jax.ShapeDtypeStruct((t, d), bf16),
                   jax.ShapeDtypeStruct((SGU_GROUPS, SGU_CHUNK, SGU_CHUNK), f32), jax.ShapeDtypeStruct((SGU_GROUPS, SGU_CHUNK, 1), f32),
                   jax.ShapeDtypeStruct((1, d), f32), jax.ShapeDtypeStruct((1, d), f32)],
        scratch_shapes=[pltpu.VMEM((tm, d), f32)],
        compiler_params=_cparams(),
    )(da, proj, ng, nb, ws, bs, dep)


def retention_constants(decay_logit, t, dk):
    lg = jax.nn.log_sigmoid(decay_logit.astype(f32))
    lgf = lg[0][:, None]
    lgb = lg[1][:, None]
    idx = jnp.arange(CHUNK, dtype=f32)[None, :]
    af = jnp.exp((idx + 1.0) * lgf)
    ab = jnp.exp((CHUNK - idx) * lgb)
    kf = jnp.exp((CHUNK - 1.0 - idx) * lgf)
    kb = jnp.exp(idx * lgb)
    cols = jnp.stack([af, ab, kf, kb, af * (idx + 1.0), ab * (CHUNK - idx), kf * (CHUNK - 1.0 - idx), kb * idx], axis=1)
    cols = cols[..., None]
    diff = idx[0][:, None] - idx[0][None, :]
    dfm = jnp.where(diff >= 0, jnp.exp(jnp.maximum(diff, 0.0)[None] * lgf[:, :, None]), 0.0)
    dbm = jnp.where(diff < 0, jnp.exp(jnp.maximum(-diff, 0.0)[None] * lgb[:, :, None]), 0.0)
    mats = jnp.stack([dfm + dbm, dfm * diff[None], dbm * (-diff)[None]], axis=1)
    cdec = jnp.stack([jnp.broadcast_to(jnp.exp(CHUNK * lgf), (RET_HEADS, dk)),
                      jnp.broadcast_to(jnp.exp(CHUNK * lgb), (RET_HEADS, dk))], axis=1)
    theta = ROPE_BASE ** (-jnp.arange(0, dk, 2, dtype=f32) / dk)
    ang = jnp.arange(t, dtype=f32)[:, None] * theta[None, :]
    return cols, mats, cdec, jnp.cos(ang), jnp.sin(ang)


def _rot(tr, cos, sin):
    half = tr.shape[-1] // 2
    t1 = tr[:, :half]
    t2 = tr[:, half:]
    return jnp.concatenate([t1 * cos - t2 * sin, t2 * cos + t1 * sin], axis=-1)


def _rot_inv(dt, cos, sin):
    half = dt.shape[-1] // 2
    d1 = dt[:, :half]
    d2 = dt[:, half:]
    return jnp.concatenate([d1 * cos + d2 * sin, d2 * cos - d1 * sin], axis=-1)


def _ret_specs(t, d, dk, rt):
    nr = t // rt
    hq = d // dk

    def blk(p, n):
        return (1 - p) * (nr - 1 - n) + p * n

    q_spec = pl.BlockSpec((1, rt, dk), lambda h, p, n: (1, blk(p, n), h))
    k_spec = pl.BlockSpec((1, rt, dk), lambda h, p, n: (1, blk(p, n), hq + h))
    v_spec = pl.BlockSpec((1, rt, dk), lambda h, p, n: (2, blk(p, n), h))
    g_spec = pl.BlockSpec((1, rt, dk), lambda h, p, n: (2, blk(p, n), hq + h))
    tab_spec = pl.BlockSpec((rt, dk // 2), lambda h, p, n: (blk(p, n), 0))
    cols_spec = pl.BlockSpec((1, 8, CHUNK, 1), lambda h, p, n: (h, 0, 0, 0))
    mats_spec = pl.BlockSpec((1, 3, CHUNK, CHUNK), lambda h, p, n: (h, 0, 0, 0))
    cdec_spec = pl.BlockSpec((1, 2, dk), lambda h, p, n: (h, 0, 0))
    in_row = pl.BlockSpec((rt, dk), lambda h, p, n: (blk(p, n), h))
    out_row = pl.BlockSpec((rt, dk), lambda h, p, n: (p * n, h))
    return nr, blk, q_spec, k_spec, v_spec, g_spec, tab_spec, cols_spec, mats_spec, cdec_spec, in_row, out_row


def ret_fwd(proj, cols, mats, cdec):
    _, t, w2 = proj.shape
    d = w2 // 2
    dk = d // RET_HEADS
    rt = _row_tile(t)
    cpt = rt // CHUNK
    nr, blk, q_spec, k_spec, v_spec, g_spec, _, cols_spec, mats_spec, cdec_spec, _, out_row = _ret_specs(t, d, dk, rt)

    def body(q_ref, k_ref, v_ref, g_ref, cols_ref, mats_ref, cdec_ref, r_ref, rn_ref, sb_scr, st):
        p = pl.program_id(1)
        n = pl.program_id(2)
        af, ab, kf, kb = cols_ref[0, 0], cols_ref[0, 1], cols_ref[0, 2], cols_ref[0, 3]
        cf = cdec_ref[0, 0:1, :]
        cb = cdec_ref[0, 1:2, :]

        @pl.when(n == 0)
        def _():
            st[...] = jnp.zeros_like(st)

        @pl.when(p == 0)
        def _():
            for j in reversed(range(cpt)):
                rows = slice(j * CHUNK, (j + 1) * CHUNK)
                ch = blk(p, n) * cpt + j
                kk = k_ref[0, rows, :].astype(f32)
                sb_scr[ch] = st[...].astype(bf16)
                st[...] = st[...] * cb + _dot_tn((kk * kb).astype(bf16), v_ref[0, rows, :])

        @pl.when(p == 1)
        def _():
            for j in range(cpt):
                rows = slice(j * CHUNK, (j + 1) * CHUNK)
                ch = blk(p, n) * cpt + j
                qb = q_ref[0, rows, :]
                kkb = k_ref[0, rows, :]
                q = qb.astype(f32)
                kk = kkb.astype(f32)
                v = v_ref[0, rows, :]
                pm = (_dot_nt(qb, kkb) * mats_ref[0, 0]).astype(bf16)
                out = (_dot(pm, v) + _dot((q * af).astype(bf16), st[...].astype(bf16))
                       + _dot((q * ab).astype(bf16), sb_scr[ch]))
                st[...] = st[...] * cf + _dot_tn((kk * kf).astype(bf16), v)
                rhat = out * lax.rsqrt(jnp.mean(out * out, axis=-1, keepdims=True) + NORM_EPS)
                gg = g_ref[0, rows, :].astype(f32)
                r_ref[rows, :] = out.astype(bf16)
                rn_ref[rows, :] = (rhat * gg * _sigmoid(gg)).astype(bf16)

    return pl.pallas_call(
        body, name="ret_fwd", grid=(RET_HEADS, 2, nr),
        in_specs=[q_spec, k_spec, v_spec, g_spec, cols_spec, mats_spec, cdec_spec],
        out_specs=[out_row, out_row],
        out_shape=[jax.ShapeDtypeStruct((t, d), bf16), jax.ShapeDtypeStruct((t, d), bf16)],
        scratch_shapes=[pltpu.VMEM((t // CHUNK, dk, dk), bf16), pltpu.VMEM((dk, dk), f32)],
        compiler_params=_cparams(),
    )(proj, proj, proj, proj, cols, mats, cdec)


def ret_bwd(drn, r, proj, cols, mats, cdec, cos, sin):
    _, t, w2 = proj.shape
    d = w2 // 2
    dk = d // RET_HEADS
    rt = _row_tile(t)
    cpt = rt // CHUNK
    nr, blk, q_spec, k_spec, v_spec, g_spec, tab_spec, cols_spec, mats_spec, cdec_spec, in_row, out_row = _ret_specs(t, d, dk, rt)
    scale = dk ** -0.5

    def body(drn_ref, r_ref, q_ref, k_ref, v_ref, g_ref, cos_ref, sin_ref, cols_ref, mats_ref, cdec_ref,
             dq_ref, dk_ref, dv_ref, dg_ref, dlg_ref,
             sb_scr, gf_scr, st_s, st_g, acc_af, acc_ab, acc_vf, acc_vb, acc_sf, acc_sb):
        p = pl.program_id(1)
        n = pl.program_id(2)
        af, ab, kf, kb = cols_ref[0, 0], cols_ref[0, 1], cols_ref[0, 2], cols_ref[0, 3]
        af1, ab1, kf1, kb1 = cols_ref[0, 4], cols_ref[0, 5], cols_ref[0, 6], cols_ref[0, 7]
        cf = cdec_ref[0, 0:1, :]
        cb = cdec_ref[0, 1:2, :]

        @pl.when(n == 0)
        def _():
            st_s[...] = jnp.zeros_like(st_s)
            st_g[...] = jnp.zeros_like(st_g)

        @pl.when(jnp.logical_and(n == 0, p == 1))
        def _():
            for a in (acc_af, acc_ab, acc_vf, acc_vb, acc_sf, acc_sb):
                a[...] = jnp.zeros_like(a)

        def load(rows):
            cs, sn = cos_ref[rows, :], sin_ref[rows, :]
            q = q_ref[0, rows, :].astype(f32)
            kk = k_ref[0, rows, :].astype(f32)
            rr = r_ref[rows, :].astype(f32)
            rstd = lax.rsqrt(jnp.mean(rr * rr, axis=-1, keepdims=True) + NORM_EPS)
            rhat = rr * rstd
            gg = g_ref[0, rows, :].astype(f32)
            sg = _sigmoid(gg)
            dd = drn_ref[rows, :].astype(f32)
            drhat = dd * gg * sg
            dout = rstd * (drhat - rhat * jnp.mean(drhat * rhat, axis=-1, keepdims=True))
            dgr = dd * rhat * _dsilu(gg, sg)
            return q, kk, dout.astype(bf16), dgr, cs, sn

        @pl.when(p == 0)
        def _():
            for j in reversed(range(cpt)):
                rows = slice(j * CHUNK, (j + 1) * CHUNK)
                ch = blk(p, n) * cpt + j
                q, kk, doutb, _, _, _ = load(rows)
                sb_scr[ch] = st_s[...].astype(bf16)
                gf_scr[ch] = st_g[...].astype(bf16)
                st_s[...] = st_s[...] * cb + _dot_tn((kk * kb).astype(bf16), v_ref[0, rows, :])
                st_g[...] = st_g[...] * cf + _dot_tn((q * af).astype(bf16), doutb)

        @pl.when(p == 1)
        def _():
            for j in range(cpt):
                rows = slice(j * CHUNK, (j + 1) * CHUNK)
                ch = blk(p, n) * cpt + j
                q, kk, doutb, dgr, cs, sn = load(rows)
                v = v_ref[0, rows, :]
                qb = q_ref[0, rows, :]
                kkb = k_ref[0, rows, :]
                sf = st_s[...]
                gb = st_g[...]
                sfb = sf.astype(bf16)
                gbb = gb.astype(bf16)
                sbb = sb_scr[ch]
                gfb = gf_scr[ch]
                dmat = mats_ref[0, 0]
                scores = _dot_nt(qb, kkb)
                dpraw = _dot_nt(doutb, v)
                dpb = (dpraw * dmat).astype(bf16)
                pmb = (scores * dmat).astype(bf16)
                x1 = _dot_nt(doutb, sfb)
                x2 = _dot_nt(doutb, sbb)
                y1 = _dot_nt(v, gfb)
                y2 = _dot_nt(v, gbb)
                kdf = (kk * kf).astype(bf16)
                kdb = (kk * kb).astype(bf16)
                dq = _dot(dpb, kkb) + x1 * af + x2 * ab
                dkk = _dot_tn(dpb, qb) + y1 * kf + y2 * kb
                dv = _dot_tn(pmb, doutb) + _dot(kdf, gfb) + _dot(kdb, gbb)
                ps = dpraw * scores
                acc_af[...] += ps * mats_ref[0, 1]
                acc_ab[...] += ps * mats_ref[0, 2]
                acc_vf[...] += x1 * q * af1 + y1 * kk * kf1
                acc_vb[...] += x2 * q * ab1 + y2 * kk * kb1
                acc_sf[...] += gfb.astype(f32) * sf
                acc_sb[...] += gb * sbb.astype(f32)
                st_s[...] = sf * cf + _dot_tn(kdf, v)
                st_g[...] = gb * cb + _dot_tn((q * ab).astype(bf16), doutb)
                dq_ref[rows, :] = _rot_inv(dq, cs, sn).astype(bf16)
                dk_ref[rows, :] = (_rot_inv(dkk, cs, sn) * scale).astype(bf16)
                dv_ref[rows, :] = dv.astype(bf16)
                dg_ref[rows, :] = dgr.astype(bf16)

        @pl.when(jnp.logical_and(p == 1, n == nr - 1))
        def _():
            tf = jnp.sum(acc_af[...]) + jnp.sum(acc_vf[...]) + CHUNK * jnp.sum(acc_sf[...] * cf)
            tb = jnp.sum(acc_ab[...]) + jnp.sum(acc_vb[...]) + CHUNK * jnp.sum(acc_sb[...] * cb)
            rid = lax.broadcasted_iota(jnp.int32, (8, 128), 0)
            dlg_ref[0] = jnp.where(rid == 0, tf, jnp.where(rid == 1, tb, 0.0))

    nch = t // CHUNK
    return pl.pallas_call(
        body, name="ret_bwd", grid=(RET_HEADS, 2, nr),
        in_specs=[in_row, in_row, q_spec, k_spec, v_spec, g_spec, tab_spec, tab_spec, cols_spec, mats_spec, cdec_spec],
        out_specs=[out_row, out_row, out_row, out_row, pl.BlockSpec((1, 8, 128), lambda h, p, n: (h, 0, 0))],
        out_shape=[jax.ShapeDtypeStruct((t, d), bf16)] * 4 + [jax.ShapeDtypeStruct((RET_HEADS, 8, 128), f32)],
        scratch_shapes=[pltpu.VMEM((nch, dk, dk), bf16), pltpu.VMEM((nch, dk, dk), bf16),
                        pltpu.VMEM((dk, dk), f32), pltpu.VMEM((dk, dk), f32),
                        pltpu.VMEM((CHUNK, CHUNK), f32), pltpu.VMEM((CHUNK, CHUNK), f32),
                        pltpu.VMEM((CHUNK, dk), f32), pltpu.VMEM((CHUNK, dk), f32),
                        pltpu.VMEM((dk, dk), f32), pltpu.VMEM((dk, dk), f32)],
        compiler_params=_cparams(),
    )(drn, r, proj, proj, proj, proj, cos, sin, cols, mats, cdec)


def mix_fwd(a, rn, proj, wa, wb, wo, x1):
    t, d = x1.shape
    tm = _row_tile(t)

    def body(a_ref, rn_ref, p_ref, wa_ref, wb_ref, wo_ref, x_ref, xo_ref, ba_ref, br_ref):
        ba = _dot(a_ref[...], wa_ref[...])
        br = _dot(rn_ref[...], wb_ref[...])
        sa = _sigmoid(p_ref[0, :, 0:d].astype(f32))
        sb = _sigmoid(p_ref[0, :, d:2 * d].astype(f32))
        mix = (sa * ba + sb * br).astype(bf16)
        xo_ref[...] = x_ref[...] + _dot(mix, wo_ref[...])
        ba_ref[...] = ba.astype(bf16)
        br_ref[...] = br.astype(bf16)

    row = pl.BlockSpec((tm, d), lambda i: (i, 0))
    wsp = pl.BlockSpec((d, d), lambda i: (0, 0))
    return pl.pallas_call(
        body, name="mix_fwd", grid=(t // tm,),
        in_specs=[row, row, pl.BlockSpec((1, tm, 2 * d), lambda i: (3, i, 0)), wsp, wsp, wsp, row],
        out_specs=[row, row, row],
        out_shape=[jax.ShapeDtypeStruct((t, d), f32), jax.ShapeDtypeStruct((t, d), bf16), jax.ShapeDtypeStruct((t, d), bf16)],
        compiler_params=_cparams(),
    )(a, rn, proj, wa, wb, wo, x1)


def mix_bwd_act(dx2, ba, br, proj, wa, wb, wo, dep):
    t, d = dx2.shape
    tm = _row_tile(t)

    def body(dx_ref, ba_ref, br_ref, p_ref, wa_ref, wb_ref, wo_ref, dep_ref,
             da_ref, drn_ref, dga_ref, dgb_ref, mix_ref, dba_ref, dbr_ref, dxb_ref):
        dxb = dx_ref[...].astype(bf16)
        dxb_ref[...] = dxb
        dmix = _dot_nt(dxb, wo_ref[...])
        ba = ba_ref[...].astype(f32)
        br = br_ref[...].astype(f32)
        sa = _sigmoid(p_ref[0, :, 0:d].astype(f32))
        sb = _sigmoid(p_ref[0, :, d:2 * d].astype(f32))
        mix_ref[...] = (sa * ba + sb * br).astype(bf16)
        dba = (dmix * sa).astype(bf16)
        dbr = (dmix * sb).astype(bf16)
        dba_ref[...] = dba
        dbr_ref[...] = dbr
        dga_ref[...] = (dmix * ba * sa * (1.0 - sa)).astype(bf16)
        dgb_ref[...] = (dmix * br * sb * (1.0 - sb)).astype(bf16)
        da_ref[...] = _dot_nt(dba, wa_ref[...]).astype(bf16)
        drn_ref[...] = _dot_nt(dbr, wb_ref[...]).astype(bf16)

    row = pl.BlockSpec((tm, d), lambda i: (i, 0))
    wsp = pl.BlockSpec((d, d), lambda i: (0, 0))
    return pl.pallas_call(
        body, name="mix_bwd_act", grid=(t // tm,),
        in_specs=[row, row, row, pl.BlockSpec((1, tm, 2 * d), lambda i: (3, i, 0)), wsp, wsp, wsp, _ANY],
        out_specs=[row] * 8,
        out_shape=[jax.ShapeDtypeStruct((t, d), bf16)] * 8,
        compiler_params=_cparams(),
    )(dx2, ba, br, proj, wa, wb, wo, dep)


def inproj_bwd_act(segs, win, x1, ng, dx2):
    t, d = x1.shape
    s4 = win.shape[0]
    tm = _row_tile(t) // 2
    nseg = len(segs)

    def body(*refs):
        seg_refs = refs[:nseg]
        w_ref, x_ref, ng_ref, dx2_ref, dx1_ref, db_ref, dng_ref = refs[nseg:]
        i = pl.program_id(0)
        dh = None
        for e, sr in enumerate(seg_refs):
            sb = sr[...]
            part = _dot_nt(sb, w_ref[e // 2, :, (e % 2) * d:(e % 2 + 1) * d])
            dh = part if dh is None else dh + part
            _acc_out(db_ref.at[e], i == 0, jnp.sum(sb.astype(f32), axis=0, keepdims=True))
        _, xh, r = _rms(x_ref[...], ng_ref[...])
        dx1_ref[...] = dx2_ref[...] + _rms_bwd(dh, xh, r, ng_ref[...])
        _acc_out(dng_ref, i == 0, jnp.sum(dh * xh, axis=0, keepdims=True))

    row = pl.BlockSpec((tm, d), lambda i: (i, 0))
    vec = pl.BlockSpec((1, d), lambda i: (0, 0))
    return pl.pallas_call(
        body, name="inproj_bwd_act", grid=(t // tm,),
        in_specs=[row] * nseg + [pl.BlockSpec((s4, d, 2 * d), lambda i: (0, 0, 0), pipeline_mode=pl.Buffered(1)),
                                 row, vec, row],
        out_specs=[row, pl.BlockSpec((nseg, 1, d), lambda i: (0, 0, 0)), vec],
        out_shape=[jax.ShapeDtypeStruct((t, d), f32), jax.ShapeDtypeStruct((nseg, 1, d), f32),
                   jax.ShapeDtypeStruct((1, d), f32)],
        compiler_params=_cparams(),
    )(*segs, win, x1, ng, dx2)


def loss_head(x3, fng, tgt):
    t, d = x3.shape
    tm = _row_tile(t)

    def body(x_ref, g_ref, t_ref, loss_ref, dx_ref, dg_ref):
        i = pl.program_id(0)
        y, xh, r = _rms(x_ref[...], g_ref[...])
        diff = y - t_ref[...]
        part = 0.5 * jnp.sum(jnp.sum(diff * diff, axis=0, keepdims=True), axis=1, keepdims=True) / d
        _acc_out(loss_ref, i == 0, jnp.broadcast_to(part, (1, 128)))
        dy = diff * (1.0 / d)
        dx_ref[...] = _rms_bwd(dy, xh, r, g_ref[...])
        _acc_out(dg_ref, i == 0, jnp.sum(dy * xh, axis=0, keepdims=True))

    row = pl.BlockSpec((tm, d), lambda i: (i, 0))
    vec = pl.BlockSpec((1, d), lambda i: (0, 0))
    return pl.pallas_call(
        body, name="loss_head", grid=(t // tm,),
        in_specs=[row, vec, row],
        out_specs=[pl.BlockSpec((1, 128), lambda i: (0, 0)), row, vec],
        out_shape=[jax.ShapeDtypeStruct((1, 128), f32), jax.ShapeDtypeStruct((t, d), f32), jax.ShapeDtypeStruct((1, d), f32)],
        compiler_params=_cparams(),
    )(x3, fng, tgt)


def _place():
    return lax.axis_index("x"), lax.axis_index("y"), lax.axis_index("c")


def _other_chips(x, y):
    return [(1 - x, y), (x, 1 - y), (1 - x, 1 - y)]


_ANY = pl.BlockSpec(memory_space=pl.ANY)


_HBM = pl.BlockSpec(memory_space=pltpu.HBM)
_SEM = pl.BlockSpec(memory_space=pltpu.SEMAPHORE)
_EFFECT = pltpu.SideEffectType.DATAFLOW_SIDE_EFFECTING


def _hbm(a):
    return pltpu.with_memory_space_constraint(a, pltpu.HBM)


def _half_rows(ref, c):
    half = ref.shape[1] // 2
    return pl.ds(pl.multiple_of(c * half, 16), half)


def _chip_copy(src, dst, send_sem, recv_sem, chip, c):
    return pltpu.make_async_remote_copy(src_ref=src, dst_ref=dst, send_sem=send_sem, recv_sem=recv_sem,
                                        device_id=(chip[0], chip[1], c), device_id_type=MESH)


def gather_start(bufs, groups, name):
    nb, ng = len(bufs), len(groups)

    def body(*refs):
        ins = refs[:nb]
        sems = refs[nb:nb + 2 * ng]
        token = refs[-1]
        x, y, c = _place()
        k = 2 * x + y
        for gi, grp in enumerate(groups):
            for wi, w in enumerate(grp):
                mine = ins[w].at[k, _half_rows(ins[w], c)]
                for j, chip in enumerate(_other_chips(x, y)):
                    _chip_copy(mine, mine, sems[2 * gi].at[3 * wi + j], sems[2 * gi + 1].at[3 * wi + j], chip, c).start()
        token[...] = jnp.zeros_like(token)

    sem_shapes = []
    for grp in groups:
        sem_shapes += [pltpu.SemaphoreType.DMA((3 * len(grp),)), pltpu.SemaphoreType.DMA((3 * len(grp),))]
    outs = pl.pallas_call(
        body, name=name,
        out_shape=sem_shapes + [pltpu.HBM(b.shape, b.dtype) for b in bufs] + [jax.ShapeDtypeStruct((8, 128), f32)],
        in_specs=[_HBM] * nb,
        out_specs=[_SEM] * (2 * ng) + [_HBM] * nb + [pl.BlockSpec(memory_space=pltpu.VMEM)],
        input_output_aliases={w: 2 * ng + w for w in range(nb)},
        compiler_params=pltpu.CompilerParams(has_side_effects=_EFFECT),
    )(*[_hbm(b) for b in bufs])
    sems = [(outs[2 * gi], outs[2 * gi + 1]) for gi in range(ng)]
    return sems, list(outs[2 * ng:2 * ng + nb]), outs[-1]


def gather_wait(bufs, sems, after, name):
    n = len(bufs)

    def body(*refs):
        ins = refs[:n]
        send_sems, recv_sems = refs[n], refs[n + 1]
        x, y, c = _place()
        k = 2 * x + y
        for wi in range(n):
            half = _half_rows(ins[wi], c)
            for j, chip in enumerate(_other_chips(x, y)):
                cp = _chip_copy(ins[wi].at[k, half], ins[wi].at[2 * chip[0] + chip[1], half], send_sems.at[3 * wi + j],
                                recv_sems.at[3 * wi + j], chip, c)
                cp.wait_send()
                cp.wait_recv()

    outs = pl.pallas_call(
        body, name=name,
        out_shape=[pltpu.HBM(b.shape, b.dtype) for b in bufs],
        in_specs=[_HBM] * n + [_SEM, _SEM, _ANY],
        out_specs=[_HBM] * n,
        input_output_aliases={i: i for i in range(n)},
        compiler_params=pltpu.CompilerParams(has_side_effects=_EFFECT),
    )(*bufs, sems[0], sems[1], after)
    return list(outs)


def gather_forward(bufs, name):
    n = len(bufs)

    def body(*refs):
        ins = refs[n:2 * n]
        send_sems, recv_sems = refs[2 * n], refs[2 * n + 1]
        x, y, c = _place()
        copies = []
        for wi in range(n):
            for j, chip in enumerate(_other_chips(x, y)):
                kp = 2 * chip[0] + chip[1]
                got = ins[wi].at[kp, _half_rows(ins[wi], c)]
                cp = pltpu.make_async_remote_copy(
                    src_ref=got, dst_ref=got, send_sem=send_sems.at[3 * wi + j], recv_sem=recv_sems.at[3 * wi + j],
                    device_id=(x, y, 1 - c), device_id_type=MESH)
                cp.start()
                copies.append((cp, wi, kp, j))
        for cp, wi, kp, j in copies:
            cp.wait_send()
            theirs = ins[wi].at[kp, _half_rows(ins[wi], 1 - c)]
            pltpu.make_async_remote_copy(
                src_ref=theirs, dst_ref=theirs, send_sem=send_sems.at[3 * wi + j], recv_sem=recv_sems.at[3 * wi + j],
                device_id=(x, y, 1 - c), device_id_type=MESH).wait_recv()

    outs = pl.pallas_call(
        body, name=name,
        out_shape=[jax.ShapeDtypeStruct(b.shape, b.dtype) for b in bufs],
        in_specs=[_ANY] * n, out_specs=[_ANY] * n,
        input_output_aliases={i: i for i in range(n)},
        scratch_shapes=[pltpu.SemaphoreType.DMA((3 * n,)), pltpu.SemaphoreType.DMA((3 * n,))],
    )(*bufs)
    return list(outs)


def exchange_start(grads, name):
    n = len(grads)
    lands = [lax.empty((3,) + g.shape[1:], g.dtype) for g in grads]

    def body(*refs):
        ins = refs[:n]
        land = refs[n:2 * n]
        send_sems, recv_sems = refs[2 * n], refs[2 * n + 1]
        token = refs[-1]
        x, y, c = _place()
        for wi in range(n):
            for j, chip in enumerate(_other_chips(x, y)):
                _chip_copy(ins[wi].at[2 * chip[0] + chip[1]], land[wi].at[j], send_sems.at[3 * wi + j],
                           recv_sems.at[3 * wi + j], chip, c).start()
        token[...] = jnp.zeros_like(token)

    outs = pl.pallas_call(
        body, name=name,
        out_shape=[pltpu.SemaphoreType.DMA((3 * n,)), pltpu.SemaphoreType.DMA((3 * n,))]
        + [pltpu.HBM(g.shape, g.dtype) for g in grads] + [pltpu.HBM(l.shape, l.dtype) for l in lands]
        + [jax.ShapeDtypeStruct((8, 128), f32)],
        in_specs=[_HBM] * (2 * n),
        out_specs=[_SEM, _SEM] + [_HBM] * (2 * n) + [pl.BlockSpec(memory_space=pltpu.VMEM)],
        input_output_aliases={i: 2 + i for i in range(2 * n)},
        compiler_params=pltpu.CompilerParams(has_side_effects=_EFFECT),
    )(*[_hbm(g) for g in grads], *[_hbm(l) for l in lands])
    return (outs[0], outs[1]), list(outs[2:2 + n]), list(outs[2 + n:2 + 2 * n]), outs[-1]


def exchange_wait(grads, lands, sems, after, name):
    n = len(grads)

    def body(*refs):
        ins = refs[:n]
        land = refs[n:2 * n]
        send_sems, recv_sems = refs[2 * n], refs[2 * n + 1]
        x, y, c = _place()
        for wi in range(n):
            for j, chip in enumerate(_other_chips(x, y)):
                cp = _chip_copy(ins[wi].at[2 * chip[0] + chip[1]], land[wi].at[j], send_sems.at[3 * wi + j],
                                recv_sems.at[3 * wi + j], chip, c)
                cp.wait_send()
                cp.wait_recv()

    outs = pl.pallas_call(
        body, name=name,
        out_shape=[pltpu.HBM(g.shape, g.dtype) for g in grads] + [pltpu.HBM(l.shape, l.dtype) for l in lands],
        in_specs=[_HBM] * (2 * n) + [_SEM, _SEM, _ANY],
        out_specs=[_HBM] * (2 * n),
        input_output_aliases={i: i for i in range(2 * n)},
        compiler_params=pltpu.CompilerParams(has_side_effects=_EFFECT),
    )(*grads, *lands, sems[0], sems[1], after)
    return list(outs[:n]), list(outs[n:])


def _split_start(body, name, n_sems, operands):
    n = len(operands)
    outs = pl.pallas_call(
        body, name=name,
        out_shape=[pltpu.SemaphoreType.DMA((n_sems,)), pltpu.SemaphoreType.DMA((n_sems,))]
        + [pltpu.HBM(o.shape, o.dtype) for o in operands] + [jax.ShapeDtypeStruct((8, 128), f32)],
        in_specs=[_HBM] * n,
        out_specs=[_SEM, _SEM] + [_HBM] * n + [pl.BlockSpec(memory_space=pltpu.VMEM)],
        input_output_aliases={i: 2 + i for i in range(n)},
        compiler_params=pltpu.CompilerParams(has_side_effects=_EFFECT),
    )(*[_hbm(o) for o in operands])
    return (outs[0], outs[1]), list(outs[2:2 + n]), outs[-1]


def _split_wait(body, name, operands, sems, after):
    n = len(operands)
    outs = pl.pallas_call(
        body, name=name,
        out_shape=[pltpu.HBM(o.shape, o.dtype) for o in operands],
        in_specs=[_HBM] * n + [_SEM, _SEM, _ANY],
        out_specs=[_HBM] * n,
        input_output_aliases={i: i for i in range(n)},
        compiler_params=pltpu.CompilerParams(has_side_effects=_EFFECT),
    )(*operands, sems[0], sems[1], after)
    return list(outs)


def _sibling_copy(src, dst, send_sem, recv_sem):
    x, y, c = _place()
    return pltpu.make_async_remote_copy(src_ref=src, dst_ref=dst, send_sem=send_sem, recv_sem=recv_sem,
                                        device_id=(x, y, 1 - c), device_id_type=MESH)


def swap_start(parts, name):
    n = len(parts)

    def body(*refs):
        for w in range(n):
            _sibling_copy(refs[w], refs[n + w], refs[2 * n].at[w], refs[2 * n + 1].at[w]).start()
        refs[-1][...] = jnp.zeros_like(refs[-1])

    sems, ops, token = _split_start(body, name, n, list(parts) + [lax.empty(p.shape, p.dtype) for p in parts])
    return sems, ops[:n], ops[n:], token


def swap_wait(parts, lands, sems, after, name):
    n = len(parts)

    def body(*refs):
        for w in range(n):
            cp = _sibling_copy(refs[w], refs[n + w], refs[2 * n].at[w], refs[2 * n + 1].at[w])
            cp.wait_send()
            cp.wait_recv()

    return _split_wait(body, name, list(parts) + list(lands), sems, after)[n:]


def _all_peers(x, y, c):
    return [(1 - x if m & 4 else x, 1 - y if m & 2 else y, 1 - c if m & 1 else c) for m in range(1, N_DEV)]


def small_start(block):
    land = jnp.broadcast_to(block[None], (N_DEV,) + block.shape)

    def body(b_ref, land_ref, send_sems, recv_sems, b_thru, land_thru, token):
        x, y, c = _place()
        me = 4 * x + 2 * y + c
        for m, peer in enumerate(_all_peers(x, y, c)):
            pltpu.make_async_remote_copy(src_ref=b_ref, dst_ref=land_ref.at[me], send_sem=send_sems.at[m],
                                         recv_sem=recv_sems.at[m], device_id=peer, device_id_type=MESH).start()
        token[...] = jnp.zeros_like(token)

    sems, ops, token = _split_start(body, "small_start", N_DEV - 1, [block, land])
    return sems, ops[0], ops[1], token


def small_wait(block, land, sems, after):
    def body(b_ref, land_ref, send_sems, recv_sems, after_ref, b_thru, land_thru):
        x, y, c = _place()
        for m, (px, py, pc) in enumerate(_all_peers(x, y, c)):
            cp = pltpu.make_async_remote_copy(src_ref=b_ref, dst_ref=land_ref.at[4 * px + 2 * py + pc],
                                              send_sem=send_sems.at[m], recv_sem=recv_sems.at[m],
                                              device_id=(px, py, pc), device_id_type=MESH)
            cp.wait_send()
            cp.wait_recv()

    return _split_wait(body, "small_wait", [block, land], sems, after)[1]


def _adamw(w, g, m, v):
    m = ADAM_B1 * m + (1.0 - ADAM_B1) * g
    v = ADAM_B2 * v + (1.0 - ADAM_B2) * (g * g)
    m_hat = m / (1.0 - ADAM_B1 ** ADAM_STEP)
    v_hat = v / (1.0 - ADAM_B2 ** ADAM_STEP)
    delta = -ADAM_LR * (m_hat / (jnp.sqrt(v_hat) + ADAM_EPS) + ADAM_WD * w)
    return delta, m, v


def _ew_tile(rows):
    for cand in (256, 176, 128, 64, 32, 16, 8):
        if rows % cand == 0:
            return cand
    return rows


def sum_partials(chip, own, land, name):
    _, r, c = own.shape
    tr = _ew_tile(r)

    def body(k_ref, own_ref, p_ref, o_ref):
        o_ref[...] = ((own_ref[0].astype(f32) + p_ref[0].astype(f32)) + p_ref[1].astype(f32)) + p_ref[2].astype(f32)

    return pl.pallas_call(
        body, name=name,
        grid_spec=pltpu.PrefetchScalarGridSpec(
            num_scalar_prefetch=1, grid=(r // tr,),
            in_specs=[pl.BlockSpec((1, tr, c), lambda i, k: (k[0], i, 0)), pl.BlockSpec((3, tr, c), lambda i, k: (0, i, 0))],
            out_specs=pl.BlockSpec((tr, c), lambda i, k: (i, 0))),
        out_shape=jax.ShapeDtypeStruct((r, c), f32),
        compiler_params=_cparams(),
    )(chip, own, land)


def adamw_shard(p_mine, p_sibling, w, m, v, name):
    r, c = w.shape
    tr = _ew_tile(r)

    def body(a_ref, b_ref, w_ref, m_ref, v_ref, g_ref, d_ref, mo_ref, vo_ref):
        g = a_ref[...] + b_ref[...]
        delta, mn, vn = _adamw(w_ref[...], g, m_ref[...], v_ref[...])
        g_ref[...] = g
        d_ref[...] = delta
        mo_ref[...] = mn
        vo_ref[...] = vn

    blk = pl.BlockSpec((tr, c), lambda i: (i, 0))
    return pl.pallas_call(
        body, name=name, grid=(r // tr,),
        in_specs=[blk] * 5, out_specs=[blk] * 4,
        out_shape=[jax.ShapeDtypeStruct((r, c), f32)] * 4,
        compiler_params=_cparams(),
    )(p_mine, p_sibling, w, m, v)


def adamw_small(g8, w, m, v):
    _, r, lanes = g8.shape

    def body(g_ref, w_ref, m_ref, v_ref, go_ref, d_ref, mo_ref, vo_ref):
        g = g_ref[0]
        for i in range(1, N_DEV):
            g = g + g_ref[i]
        delta, mn, vn = _adamw(w_ref[...], g, m_ref[...], v_ref[...])
        go_ref[...] = g
        d_ref[...] = delta
        mo_ref[...] = mn
        vo_ref[...] = vn

    return pl.pallas_call(
        body, name="adamw_small",
        out_shape=[jax.ShapeDtypeStruct((r, lanes), f32)] * 4,
        compiler_params=_cparams(),
    )(g8, w, m, v)


def _size(shape):
    n = 1
    for e in shape:
        n *= e
    return n


def _pack_rows(shapes):
    rows = [-(-_size(s) // 1024) * 8 for s in shapes]
    return rows, sum(rows)


def _pack(arrs, shapes):
    rows, _ = _pack_rows(shapes)
    parts = [jnp.pad(a.reshape(-1).astype(f32), (0, r * 128 - _size(s))).reshape(r, 128)
             for a, s, r in zip(arrs, shapes, rows)]
    return jnp.concatenate(parts, axis=0)


def _unpack(block, shapes):
    rows, _ = _pack_rows(shapes)
    out, off = [], 0
    for s, r in zip(shapes, rows):
        out.append(block[off:off + r].reshape(-1)[:_size(s)].reshape(s))
        off += r
    return out


TRANSPOSED = ("ffn1_w_gate", "ffn1_w_up", "ffn2_w_gate", "ffn2_w_up")


def _shard2d(a, n):
    return a[0].T if n in TRANSPOSED else a[0]


def _unshard(a, n):
    return (a.T if n in TRANSPOSED else a)[None]


BIG = ("ffn1_w_gate", "ffn1_w_up", "ffn1_w_down", "w_in", "w_branch_a", "w_branch_b", "w_out",
       "ffn2_w_gate", "ffn2_w_up", "ffn2_w_down")
SMALL = ("ffn1_norm", "mix_norm", "b_in", "sgu_norm_g", "sgu_norm_b", "sgu_w_s", "sgu_b_s", "ret_decay_logit",
         "ffn2_norm", "final_norm")
WEIGHTS = ("ffn1_norm", "ffn1_w_gate", "ffn1_w_up", "ffn1_w_down", "mix_norm", "w_in", "b_in", "sgu_norm_g",
           "sgu_norm_b", "sgu_w_s", "sgu_b_s", "ret_decay_logit", "w_branch_a", "w_branch_b", "w_out", "ffn2_norm",
           "ffn2_w_gate", "ffn2_w_up", "ffn2_w_down", "final_norm")


def kernel(x, ffn1_norm, ffn1_w_gate, ffn1_w_up, ffn1_w_down, mix_norm, w_in, b_in, sgu_norm_g, sgu_norm_b, sgu_w_s, sgu_b_s, ret_decay_logit, w_branch_a, w_branch_b, w_out, ffn2_norm, ffn2_w_gate, ffn2_w_up, ffn2_w_down, final_norm, loss_target, m_ffn1_norm, m_ffn1_w_gate, m_ffn1_w_up, m_ffn1_w_down, m_mix_norm, m_w_in, m_b_in, m_sgu_norm_g, m_sgu_norm_b, m_sgu_w_s, m_sgu_b_s, m_ret_decay_logit, m_w_branch_a, m_w_branch_b, m_w_out, m_ffn2_norm, m_ffn2_w_gate, m_ffn2_w_up, m_ffn2_w_down, m_final_norm, v_ffn1_norm, v_ffn1_w_gate, v_ffn1_w_up, v_ffn1_w_down, v_mix_norm, v_w_in, v_b_in, v_sgu_norm_g, v_sgu_norm_b, v_sgu_w_s, v_sgu_b_s, v_ret_decay_logit, v_w_branch_a, v_w_branch_b, v_w_out, v_ffn2_norm, v_ffn2_w_gate, v_ffn2_w_up, v_ffn2_w_down, v_final_norm):
    p = dict(ffn1_norm=ffn1_norm, ffn1_w_gate=ffn1_w_gate, ffn1_w_up=ffn1_w_up, ffn1_w_down=ffn1_w_down,
             mix_norm=mix_norm, w_in=w_in, b_in=b_in, sgu_norm_g=sgu_norm_g, sgu_norm_b=sgu_norm_b, sgu_w_s=sgu_w_s,
             sgu_b_s=sgu_b_s, ret_decay_logit=ret_decay_logit, w_branch_a=w_branch_a, w_branch_b=w_branch_b,
             w_out=w_out, ffn2_norm=ffn2_norm, ffn2_w_gate=ffn2_w_gate, ffn2_w_up=ffn2_w_up, ffn2_w_down=ffn2_w_down,
             final_norm=final_norm)
    mom = dict(ffn1_norm=m_ffn1_norm, ffn1_w_gate=m_ffn1_w_gate, ffn1_w_up=m_ffn1_w_up, ffn1_w_down=m_ffn1_w_down,
               mix_norm=m_mix_norm, w_in=m_w_in, b_in=m_b_in, sgu_norm_g=m_sgu_norm_g, sgu_norm_b=m_sgu_norm_b,
               sgu_w_s=m_sgu_w_s, sgu_b_s=m_sgu_b_s, ret_decay_logit=m_ret_decay_logit, w_branch_a=m_w_branch_a,
               w_branch_b=m_w_branch_b, w_out=m_w_out, ffn2_norm=m_ffn2_norm, ffn2_w_gate=m_ffn2_w_gate,
               ffn2_w_up=m_ffn2_w_up, ffn2_w_down=m_ffn2_w_down, final_norm=m_final_norm)
    var = dict(ffn1_norm=v_ffn1_norm, ffn1_w_gate=v_ffn1_w_gate, ffn1_w_up=v_ffn1_w_up, ffn1_w_down=v_ffn1_w_down,
               mix_norm=v_mix_norm, w_in=v_w_in, b_in=v_b_in, sgu_norm_g=v_sgu_norm_g, sgu_norm_b=v_sgu_norm_b,
               sgu_w_s=v_sgu_w_s, sgu_b_s=v_sgu_b_s, ret_decay_logit=v_ret_decay_logit, w_branch_a=v_w_branch_a,
               w_branch_b=v_w_branch_b, w_out=v_w_out, ffn2_norm=v_ffn2_norm, ffn2_w_gate=v_ffn2_w_gate,
               ffn2_w_up=v_ffn2_w_up, ffn2_w_down=v_ffn2_w_down, final_norm=v_final_norm)

    xs = x[0]
    tgt = loss_target[0]
    t, d = xs.shape
    dk = d // RET_HEADS
    tm = _row_tile(t)

    shards2d = {n: _shard2d(p[n], n) for n in BIG}
    chip = (2 * lax.axis_index("x") + lax.axis_index("y")).astype(jnp.int32).reshape(1)
    groups = {"ffn1": ("ffn1_w_gate", "ffn1_w_up", "ffn1_w_down"), "in": ("w_in",),
              "mix": ("w_branch_a", "w_branch_b", "w_out"), "ffn2": ("ffn2_w_gate", "ffn2_w_up", "ffn2_w_down")}
    def own_slot(n, zero):
        sh = shards2d[n].astype(bf16) + zero
        return lax.dynamic_update_index_in_dim(lax.empty((N_CHIPS,) + sh.shape, bf16), sh, chip[0], 0)

    sems, bufs, tok = gather_start([own_slot(n, jnp.zeros((), bf16)) for n in groups["ffn1"]], [[0, 1, 2]],
                                   "gather_start_ffn1")
    gsem = {"ffn1": sems[0]}
    pending = dict(zip(groups["ffn1"], bufs))
    rest = [n for g in ("in", "mix", "ffn2") for n in groups[g]]
    sems, bufs, tok_rest = gather_start([own_slot(n, tok[0, 0].astype(bf16)) for n in rest],
                                 [[rest.index(n) for n in groups[g]] for g in ("in", "mix", "ffn2")], "gather_start_rest")
    gsem.update(zip(("in", "mix", "ffn2"), sems))
    pending.update(zip(rest, bufs))

    def arrive(gs, after):
        got = []
        for g in gs:
            got += gather_wait([pending[n] for n in groups[g]], gsem[g], after, "gather_wait_" + g)
        return gather_forward(got, "gather_forward_" + gs[0])

    bin4 = b_in.reshape(N_CHIPS, 1, 2 * d)
    ws_b = sgu_w_s[0].astype(bf16)
    bs_c = sgu_b_s[0][:, :, None]
    cols, mats, cdec, cos, sin = retention_constants(ret_decay_logit[0], t, dk)

    wg1, wu1, wd1 = [_pair_shards(w) for w in arrive(["ffn1"], tok_rest)]
    x1, g1, u1 = ffn_fwd(xs, ffn1_norm, wg1, wu1, wd1, "ffn1_fwd")
    win, = arrive(["in"], x1)
    proj, hb2 = inproj_fwd(x1, mix_norm, win, bin4, cos, sin)
    a = sgu_fwd(proj, sgu_norm_g, sgu_norm_b, ws_b, bs_c)
    r, rn = ret_fwd(proj, cols, mats, cdec)
    wa, wb, wo, wg2, wu2, wd2 = arrive(["mix", "ffn2"], rn)
    wa, wb, wo = [w.reshape(d, d) for w in (wa, wb, wo)]
    wg2, wu2, wd2 = [_pair_shards(w) for w in (wg2, wu2, wd2)]
    x2, ba, br = mix_fwd(a, rn, proj, wa, wb, wo, x1)
    x3, g2, u2 = ffn_fwd(x2, ffn2_norm, wg2, wu2, wd2, "ffn2_fwd")
    loss_blk, dx3, d_final = loss_head(x3, final_norm.reshape(1, d), tgt)

    sent = {}
    dx2, dg2, du2, act2, hb3, dyb2, d_ffn2n = ffn_bwd_act(dx3, x2, ffn2_norm, g2, u2, wg2, wu2, wd2, "ffn2_bwd_act", tok)
    sent["ffn2"] = exchange_start(ffn_weight_grads(hb3, dyb2, dg2, du2, act2, "ffn2_grad", tok), "exchange_start_ffn2")
    da, drn, dga, dgb, mixb, dba, dbr, dx2b = mix_bwd_act(dx2, ba, br, proj, wa, wb, wo, sent["ffn2"][3])
    tg = min(t, 2048)
    row = pl.BlockSpec((tg, d), lambda s, i: (i, 0))

    def square_grad(xa, ya, name):
        return tn_matmul(xa, [ya], row, [row], 1, d, [d], t, tg, name, tok).reshape(N_CHIPS, d // N_CHIPS, d)

    sent["mix"] = exchange_start([square_grad(a, dba, "grad_w_branch_a"), square_grad(rn, dbr, "grad_w_branch_b"),
                                  square_grad(mixb, dx2b, "grad_w_out")], "exchange_start_mix")
    dua, dva, d_ws, d_bs, d_sng, d_snb = sgu_bwd(da, proj, sgu_norm_g, sgu_norm_b, ws_b, bs_c, sent["mix"][3])
    dq, dkr, dv, dgr, dlg = ret_bwd(drn, r, proj, cols, mats, cdec, cos, sin)
    segs = [dua, dva, dq, dkr, dv, dgr, dga, dgb]
    dx1, d_bin, d_mixn = inproj_bwd_act(segs, win, x1, mix_norm, dx2)
    sent["in"] = exchange_start([jnp.concatenate(
        [tn_matmul(hb2, [segs[2 * s], segs[2 * s + 1]], row, [row, row], 1, d, [d, d], t, tg, "grad_w_in_%d" % s, tok)
         for s in range(N_CHIPS)], axis=0)], "exchange_start_in")
    grad_x, dg1, du1, act1, hb1, dyb1, d_ffn1n = ffn_bwd_act(dx1, xs, ffn1_norm, g1, u1, wg1, wu1, wd1, "ffn1_bwd_act",
                                                              sent["in"][3])
    dlogit = dlg[:, 0:2, 0].T * jax.nn.sigmoid(-ret_decay_logit[0].astype(f32))
    small_g = dict(ffn1_norm=d_ffn1n, mix_norm=d_mixn, b_in=d_bin, sgu_norm_g=d_sng, sgu_norm_b=d_snb, sgu_w_s=d_ws,
                   sgu_b_s=d_bs, ret_decay_logit=dlogit, ffn2_norm=d_ffn2n, final_norm=d_final)
    shapes = [p[n].shape for n in SMALL]
    small_sems, small_blk, small_land, small_tok = small_start(_pack([small_g[n] for n in SMALL], shapes))
    sent["ffn1"] = exchange_start(ffn_weight_grads(hb1, dyb1, dg1, du1, act1, "ffn1_grad", small_tok),
                                  "exchange_start_ffn1")

    out_g, out_d, out_m, out_v = {}, {}, {}, {}
    swaps = {}

    def reduce_plane(g, after):
        gsems, own, lands, _ = sent[g]
        own, lands = exchange_wait(own, lands, gsems, after, "exchange_wait_" + g)
        plane = [sum_partials(chip, o, l, "sum_" + n) for n, o, l in zip(groups[g], own, lands)]
        swaps[g] = swap_start(plane, "swap_start_" + g)
        return swaps[g][3]

    def update(g, after):
        ssems, plane, lands, _ = swaps[g]
        other = swap_wait(plane, lands, ssems, after, "swap_wait_" + g)
        for n, mine, sib in zip(groups[g], plane, other):
            res = adamw_shard(mine, sib, shards2d[n], _shard2d(mom[n], n), _shard2d(var[n], n), "adamw_" + n)
            out_g[n], out_d[n], out_m[n], out_v[n] = [_unshard(o, n) for o in res]
        return out_g[groups[g][-1]]

    after = reduce_plane("ffn2", sent["ffn1"][3])
    after = reduce_plane("mix", after)
    after = update("ffn2", after)
    after = reduce_plane("in", after)
    after = update("mix", after)
    g8 = small_wait(small_blk, small_land, small_sems, after)
    sg, sd, sm, sv = adamw_small(g8, _pack([p[n] for n in SMALL], shapes), _pack([mom[n] for n in SMALL], shapes),
                                 _pack([var[n] for n in SMALL], shapes))
    for res, blockv in ((out_g, sg), (out_d, sd), (out_m, sm), (out_v, sv)):
        for n, val in zip(SMALL, _unpack(blockv, shapes)):
            res[n] = val
    after = update("in", sg)
    after = reduce_plane("ffn1", after)
    update("ffn1", after)

    loss = lax.psum(loss_blk[0, 0], ("x", "y", "c"))
    return (loss, grad_x[None], *[out_g[n] for n in WEIGHTS], *[out_d[n] for n in WEIGHTS],
            *[out_m[n] for n in WEIGHTS], *[out_v[n] for n in WEIGHTS])
```

```python
import functools

import jax
import jax.numpy as jnp
from jax import lax
from jax.experimental import pallas as pl
from jax.experimental.pallas import tpu as pltpu

f32 = jnp.float32
bf16 = jnp.bfloat16

SGU_CHUNK = 128
CHUNK = 128
RET_HEADS = 4
SGU_GROUPS = 4
ROPE_BASE = 10000.0
NORM_EPS = 1e-6
ADAM_LR = 0.001
ADAM_B1 = 0.9
ADAM_B2 = 0.999
ADAM_EPS = 1e-08
ADAM_WD = 0.01
ADAM_STEP = 10
N_CHIPS = 4
N_DEV = 8
MESH = pl.DeviceIdType.MESH
VMEM_LIMIT = 52 * 1024 * 1024
VMEM_LIMIT_WIDE = 62 * 1024 * 1024

_NT = (((1,), (1,)), ((), ()))
_TN = (((0,), (0,)), ((), ()))


def _cparams(limit=None):
    return pltpu.CompilerParams(vmem_limit_bytes=VMEM_LIMIT if limit is None else limit)


def _row_tile(t):
    return 512 if t >= 2048 else t // 2


def _dot(a, b):
    return jnp.dot(a, b, preferred_element_type=f32)


def _dot_nt(a, b):
    return lax.dot_general(a, b, _NT, preferred_element_type=f32)


def _dot_tn(a, b):
    return lax.dot_general(a, b, _TN, preferred_element_type=f32)


def _rms(x, g):
    r = lax.rsqrt(jnp.mean(x * x, axis=-1, keepdims=True) + NORM_EPS)
    xh = x * r
    return xh * g, xh, r


def _rms_bwd(dy, xh, r, g):
    dxh = dy * g
    return r * (dxh - xh * jnp.mean(dxh * xh, axis=-1, keepdims=True))


def _sigmoid(x):
    return jax.nn.sigmoid(x)


def _dsilu(g, sg):
    return sg * (1.0 + g * (1.0 - sg))


def _gelu(x):
    return 0.5 * x * (1.0 + lax.erf(x * 0.7071067811865476))


def _dgelu(x):
    return 0.5 * (1.0 + lax.erf(x * 0.7071067811865476)) + x * jnp.exp(-0.5 * x * x) * 0.3989422804014327


def _acc_out(ref, first, val):
    @pl.when(first)
    def _():
        ref[...] = val

    @pl.when(jnp.logical_not(first))
    def _():
        ref[...] += val


def _ffn_tile(t):
    return 256 if t >= 2048 else t // 2


def ffn_fwd(x, ng, wg, wu, wd, name):
    t, d = x.shape
    ns, fs, _ = wg.shape
    tm = _ffn_tile(t)

    def body(x_ref, ng_ref, wg_ref, wu_ref, wd_ref, xo_ref, g_ref, u_ref):
        xx = x_ref[...]
        y, _, _ = _rms(xx, ng_ref[...])
        h = y.astype(bf16)
        acc = None
        for s in range(ns):
            g = _dot_nt(h, wg_ref[s])
            u = _dot_nt(h, wu_ref[s])
            g_ref[s] = g.astype(bf16)
            u_ref[s] = u.astype(bf16)
            part = _dot((g * _sigmoid(g) * u).astype(bf16), wd_ref[s])
            acc = part if acc is None else acc + part
        xo_ref[...] = xx + 0.5 * acc

    row = pl.BlockSpec((tm, d), lambda i: (i, 0))
    shard = pl.BlockSpec((ns, tm, fs), lambda i: (0, i, 0))
    wspec = pl.BlockSpec((ns, fs, d), lambda i: (0, 0, 0), pipeline_mode=pl.Buffered(1))
    return pl.pallas_call(
        body, name=name, grid=(t // tm,),
        in_specs=[row, pl.BlockSpec((1, d), lambda i: (0, 0)), wspec, wspec, wspec],
        out_specs=[row, shard, shard],
        out_shape=[jax.ShapeDtypeStruct((t, d), f32), jax.ShapeDtypeStruct((ns, t, fs), bf16),
                   jax.ShapeDtypeStruct((ns, t, fs), bf16)],
        compiler_params=_cparams(),
    )(x, ng, wg, wu, wd)


def ffn_bwd_act(dxo, x, ng, g, u, wg, wu, wd, name, dep):
    t, d = x.shape
    ns, fs, _ = wg.shape
    tm = _ffn_tile(t)

    def body(dxo_ref, x_ref, ng_ref, g_ref, u_ref, wg_ref, wu_ref, wd_ref, dep_ref,
             dx_ref, dg_ref, du_ref, act_ref, hb_ref, dyb_ref, dng_ref):
        i = pl.program_id(0)
        dxo = dxo_ref[...]
        dyb = (0.5 * dxo).astype(bf16)
        dyb_ref[...] = dyb
        dh = None
        for s in range(ns):
            dact = _dot_nt(dyb, wd_ref[s])
            gg = g_ref[s].astype(f32)
            uu = u_ref[s].astype(f32)
            sg = _sigmoid(gg)
            sil = gg * sg
            dgb = (dact * uu * _dsilu(gg, sg)).astype(bf16)
            dub = (dact * sil).astype(bf16)
            dg_ref[s] = dgb
            du_ref[s] = dub
            act_ref[s] = (sil * uu).astype(bf16)
            part = _dot(dgb, wg_ref[s]) + _dot(dub, wu_ref[s])
            dh = part if dh is None else dh + part
        y, xh, r = _rms(x_ref[...], ng_ref[...])
        hb_ref[...] = y.astype(bf16)
        dx_ref[...] = dxo + _rms_bwd(dh, xh, r, ng_ref[...])
        _acc_out(dng_ref, i == 0, jnp.sum(dh * xh, axis=0, keepdims=True))

    row = pl.BlockSpec((tm, d), lambda i: (i, 0))
    shard = pl.BlockSpec((ns, tm, fs), lambda i: (0, i, 0))
    wspec = pl.BlockSpec((ns, fs, d), lambda i: (0, 0, 0), pipeline_mode=pl.Buffered(1))
    vec = pl.BlockSpec((1, d), lambda i: (0, 0))
    return pl.pallas_call(
        body, name=name, grid=(t // tm,),
        in_specs=[row, row, vec, shard, shard, wspec, wspec, wspec, _ANY],
        out_specs=[row, shard, shard, shard, row, row, vec],
        out_shape=[jax.ShapeDtypeStruct((t, d), f32)] + [jax.ShapeDtypeStruct((ns, t, fs), bf16)] * 3
        + [jax.ShapeDtypeStruct((t, d), bf16)] * 2 + [jax.ShapeDtypeStruct((1, d), f32)],
        compiler_params=_cparams(VMEM_LIMIT_WIDE),
    )(dxo, x, ng, g, u, wg, wu, wd, dep)


def tn_matmul(xs, ys, x_spec, y_specs, n_shards, k1, k2s, t, tm, name, dep):
    k2 = sum(k2s)
    ny = len(ys)

    def body(*refs):
        x_ref = refs[0]
        y_refs = refs[1:1 + ny]
        o_ref = refs[2 + ny]
        acc = refs[3 + ny]
        i = pl.program_id(1)
        xb = x_ref[0] if len(x_ref.shape) == 3 else x_ref[...]
        off = 0
        for y_ref, w in zip(y_refs, k2s):
            yb = y_ref[0] if len(y_ref.shape) == 3 else y_ref[...]
            part = _dot_tn(xb, yb)
            sl = (slice(None), slice(off, off + w))

            @pl.when(i == 0)
            def _(part=part, sl=sl):
                acc[sl] = part

            @pl.when(i > 0)
            def _(part=part, sl=sl):
                acc[sl] += part

            off += w

        @pl.when(i == t // tm - 1)
        def _():
            o_ref[0] = acc[...].astype(bf16)

    return pl.pallas_call(
        body, name=name, grid=(n_shards, t // tm),
        in_specs=[x_spec] + list(y_specs) + [_ANY],
        out_specs=pl.BlockSpec((1, k1, k2), lambda s, i: (s, 0, 0)),
        out_shape=jax.ShapeDtypeStruct((n_shards, k1, k2), bf16),
        scratch_shapes=[pltpu.VMEM((k1, k2), f32)],
        compiler_params=_cparams(),
    )(xs, *ys, dep)


def _pair_shards(w):
    s4, fs, d = w.shape
    return w.reshape(s4 // 2, 2 * fs, d)


def ffn_weight_grads(hb, dyb, dg, du, act, name, dep):
    t, d = hb.shape
    s2, _, fs2 = dg.shape
    tm = t
    row = pl.BlockSpec((tm, d), lambda s, i: (i, 0))
    shard = pl.BlockSpec((1, tm, fs2), lambda s, i: (s, i, 0))
    gwg = tn_matmul(dg, [hb], shard, [row], s2, fs2, [d], t, tm, name + "_wg", dep)
    gwu = tn_matmul(du, [hb], shard, [row], s2, fs2, [d], t, tm, name + "_wu", dep)
    gwd = tn_matmul(act, [dyb], shard, [row], s2, fs2, [d], t, tm, name + "_wd", dep)
    return [g.reshape(2 * s2, fs2 // 2, d) for g in (gwg, gwu, gwd)]


def inproj_fwd(x1, ng, win, bin4, cos, sin):
    t, d = x1.shape
    s4, _, w2 = win.shape
    tm = _row_tile(t)
    dk = d // RET_HEADS
    scale = dk ** -0.5

    def body(x_ref, ng_ref, w_ref, b_ref, cos_ref, sin_ref, p_ref, hb_ref):
        y, _, _ = _rms(x_ref[...], ng_ref[...])
        h = y.astype(bf16)
        hb_ref[...] = h
        for s in range(s4):
            p = _dot(h, w_ref[s]) + b_ref[s]
            if s != 1:
                p_ref[s] = p.astype(bf16)
            else:
                cs, sn = cos_ref[...], sin_ref[...]
                for e in range(2 * RET_HEADS):
                    cols = slice(e * dk, (e + 1) * dk)
                    rot = _rot(p[:, cols], cs, sn)
                    p_ref[s, :, cols] = (rot if e < RET_HEADS else rot * scale).astype(bf16)

    tab = pl.BlockSpec((tm, dk // 2), lambda i: (i, 0))
    return pl.pallas_call(
        body, name="inproj_fwd", grid=(t // tm,),
        in_specs=[pl.BlockSpec((tm, d), lambda i: (i, 0)), pl.BlockSpec((1, d), lambda i: (0, 0)),
                  pl.BlockSpec((s4, d, w2), lambda i: (0, 0, 0), pipeline_mode=pl.Buffered(1)),
                  pl.BlockSpec((s4, 1, w2), lambda i: (0, 0, 0)), tab, tab],
        out_specs=[pl.BlockSpec((s4, tm, w2), lambda i: (0, i, 0)), pl.BlockSpec((tm, d), lambda i: (i, 0))],
        out_shape=[jax.ShapeDtypeStruct((s4, t, w2), bf16), jax.ShapeDtypeStruct((t, d), bf16)],
        compiler_params=_cparams(),
    )(x1, ng, win, bin4, cos, sin)


def _sgu_norm(va, ng, nb):
    gv = _gelu(va)
    mu = jnp.mean(gv, axis=-1, keepdims=True)
    xc = gv - mu
    rstd = lax.rsqrt(jnp.mean(xc * xc, axis=-1, keepdims=True) + NORM_EPS)
    xh = xc * rstd
    return xh, rstd, (xh * ng + nb).astype(bf16)


def sgu_fwd(proj, ng, nb, ws, bs):
    _, t, w2 = proj.shape
    d = w2 // 2
    gd = d // SGU_GROUPS
    tm = _row_tile(t)

    def body(p_ref, ng_ref, nb_ref, ws_ref, bs_ref, a_ref):
        ua = p_ref[0, :, 0:d].astype(f32)
        va = p_ref[0, :, d:w2].astype(f32)
        gu = _gelu(ua)
        _, _, vn = _sgu_norm(va, ng_ref[...], nb_ref[...])
        for c in range(tm // SGU_CHUNK):
            rows = slice(c * SGU_CHUNK, (c + 1) * SGU_CHUNK)
            for g in range(SGU_GROUPS):
                cols = slice(g * gd, (g + 1) * gd)
                sg = _dot(ws_ref[g], vn[rows, cols]) + bs_ref[g]
                a_ref[rows, cols] = (gu[rows, cols] * sg).astype(bf16)

    return pl.pallas_call(
        body, name="sgu_fwd", grid=(t // tm,),
        in_specs=[pl.BlockSpec((1, tm, w2), lambda i: (0, i, 0)), pl.BlockSpec((1, d), lambda i: (0, 0)),
                  pl.BlockSpec((1, d), lambda i: (0, 0)), pl.BlockSpec((SGU_GROUPS, SGU_CHUNK, SGU_CHUNK), lambda i: (0, 0, 0)),
                  pl.BlockSpec((SGU_GROUPS, SGU_CHUNK, 1), lambda i: (0, 0, 0))],
        out_specs=pl.BlockSpec((tm, d), lambda i: (i, 0)),
        out_shape=jax.ShapeDtypeStruct((t, d), bf16),
        compiler_params=_cparams(),
    )(proj, ng, nb, ws, bs)


def sgu_bwd(da, proj, ng, nb, ws, bs, dep):
    _, t, w2 = proj.shape
    d = w2 // 2
    gd = d // SGU_GROUPS
    tm = _row_tile(t)

    def body(da_ref, p_ref, ng_ref, nb_ref, ws_ref, bs_ref, dep_ref,
             dua_ref, dva_ref, dws_ref, dbs_ref, dng_ref, dnb_ref, dvn_scr):
        i = pl.program_id(0)
        ua = p_ref[0, :, 0:d].astype(f32)
        va = p_ref[0, :, d:w2].astype(f32)
        gu = _gelu(ua)
        xh, rstd, vn = _sgu_norm(va, ng_ref[...], nb_ref[...])
        dad = da_ref[...].astype(f32)
        dsb = (dad * gu).astype(bf16)
        for c in range(tm // SGU_CHUNK):
            rows = slice(c * SGU_CHUNK, (c + 1) * SGU_CHUNK)
            for g in range(SGU_GROUPS):
                cols = slice(g * gd, (g + 1) * gd)
                sg = _dot(ws_ref[g], vn[rows, cols]) + bs_ref[g]
                dua_ref[rows, cols] = (dad[rows, cols] * sg * _dgelu(ua[rows, cols])).astype(bf16)
                ds = dsb[rows, cols]
                dvn_scr[rows, cols] = _dot_tn(ws_ref[g], ds)
                dw = _dot_nt(ds, vn[rows, cols])
                db = jnp.sum(ds.astype(f32), axis=1, keepdims=True)
                if c == 0:
                    _acc_out(dws_ref.at[g], i == 0, dw)
                    _acc_out(dbs_ref.at[g], i == 0, db)
                else:
                    dws_ref[g] += dw
                    dbs_ref[g] += db
        dvn = dvn_scr[...]
        _acc_out(dng_ref, i == 0, jnp.sum(dvn * xh, axis=0, keepdims=True))
        _acc_out(dnb_ref, i == 0, jnp.sum(dvn, axis=0, keepdims=True))
        dxh = dvn * ng_ref[...]
        dgv = rstd * (dxh - jnp.mean(dxh, axis=-1, keepdims=True) - xh * jnp.mean(dxh * xh, axis=-1, keepdims=True))
        dva_ref[...] = (dgv * _dgelu(va)).astype(bf16)

    row = pl.BlockSpec((tm, d), lambda i: (i, 0))
    vec = pl.BlockSpec((1, d), lambda i: (0, 0))
    wsp = pl.BlockSpec((SGU_GROUPS, SGU_CHUNK, SGU_CHUNK), lambda i: (0, 0, 0))
    bsp = pl.BlockSpec((SGU_GROUPS, SGU_CHUNK, 1), lambda i: (0, 0, 0))
    return pl.pallas_call(
        body, name="sgu_bwd", grid=(t // tm,),
        in_specs=[row, pl.BlockSpec((1, tm, w2), lambda i: (0, i, 0)), vec, vec, wsp, bsp, _ANY],
        out_specs=[row, row, wsp, bsp, vec, vec],
        out_shape=[jax.ShapeDtypeStruct((t, d), bf16), jax.ShapeDtypeStruct((t, d), bf16),
                   jax.ShapeDtypeStruct((SGU_GROUPS, SGU_CHUNK, SGU_CHUNK), f32), jax.ShapeDtypeStruct((SGU_GROUPS, SGU_CHUNK, 1), f32),
                   jax.ShapeDtypeStruct((1, d), f32), jax.ShapeDtypeStruct((1, d), f32)],
        scratch_shapes=[pltpu.VMEM((tm, d), f32)],
        compiler_params=_cparams(),
    )(da, proj, ng, nb, ws, bs, dep)


def retention_constants(decay_logit, t, dk):
    lg = jax.nn.log_sigmoid(decay_logit.astype(f32))
    lgf = lg[0][:, None]
    lgb = lg[1][:, None]
    idx = jnp.arange(CHUNK, dtype=f32)[None, :]
    af = jnp.exp((idx + 1.0) * lgf)
    ab = jnp.exp((CHUNK - idx) * lgb)
    kf = jnp.exp((CHUNK - 1.0 - idx) * lgf)
    kb = jnp.exp(idx * lgb)
    cols = jnp.stack([af, ab, kf, kb, af * (idx + 1.0), ab * (CHUNK - idx), kf * (CHUNK - 1.0 - idx), kb * idx], axis=1)
    cols = cols[..., None]
    diff = idx[0][:, None] - idx[0][None, :]
    dfm = jnp.where(diff >= 0, jnp.exp(jnp.maximum(diff, 0.0)[None] * lgf[:, :, None]), 0.0)
    dbm = jnp.where(diff < 0, jnp.exp(jnp.maximum(-diff, 0.0)[None] * lgb[:, :, None]), 0.0)
    mats = jnp.stack([dfm + dbm, dfm * diff[None], dbm * (-diff)[None]], axis=1)
    cdec = jnp.stack([jnp.broadcast_to(jnp.exp(CHUNK * lgf), (RET_HEADS, dk)),
                      jnp.broadcast_to(jnp.exp(CHUNK * lgb), (RET_HEADS, dk))], axis=1)
    theta = ROPE_BASE ** (-jnp.arange(0, dk, 2, dtype=f32) / dk)
    ang = jnp.arange(t, dtype=f32)[:, None] * theta[None, :]
    return cols, mats, cdec, jnp.cos(ang), jnp.sin(ang)


def _rot(tr, cos, sin):
    half = tr.shape[-1] // 2
    t1 = tr[:, :half]
    t2 = tr[:, half:]
    return jnp.concatenate([t1 * cos - t2 * sin, t2 * cos + t1 * sin], axis=-1)


def _rot_inv(dt, cos, sin):
    half = dt.shape[-1] // 2
    d1 = dt[:, :half]
    d2 = dt[:, half:]
    return jnp.concatenate([d1 * cos + d2 * sin, d2 * cos - d1 * sin], axis=-1)


def _ret_specs(t, d, dk, rt):
    nr = t // rt
    hq = d // dk

    def blk(p, n):
        return (1 - p) * (nr - 1 - n) + p * n

    q_spec = pl.BlockSpec((1, rt, dk), lambda h, p, n: (1, blk(p, n), h))
    k_spec = pl.BlockSpec((1, rt, dk), lambda h, p, n: (1, blk(p, n), hq + h))
    v_spec = pl.BlockSpec((1, rt, dk), lambda h, p, n: (2, blk(p, n), h))
    g_spec = pl.BlockSpec((1, rt, dk), lambda h, p, n: (2, blk(p, n), hq + h))
    tab_spec = pl.BlockSpec((rt, dk // 2), lambda h, p, n: (blk(p, n), 0))
    cols_spec = pl.BlockSpec((1, 8, CHUNK, 1), lambda h, p, n: (h, 0, 0, 0))
    mats_spec = pl.BlockSpec((1, 3, CHUNK, CHUNK), lambda h, p, n: (h, 0, 0, 0))
    cdec_spec = pl.BlockSpec((1, 2, dk), lambda h, p, n: (h, 0, 0))
    in_row = pl.BlockSpec((rt, dk), lambda h, p, n: (blk(p, n), h))
    out_row = pl.BlockSpec((rt, dk), lambda h, p, n: (p * n, h))
    return nr, blk, q_spec, k_spec, v_spec, g_spec, tab_spec, cols_spec, mats_spec, cdec_spec, in_row, out_row


def ret_fwd(proj, cols, mats, cdec):
    _, t, w2 = proj.shape
    d = w2 // 2
    dk = d // RET_HEADS
    rt = _row_tile(t)
    cpt = rt // CHUNK
    nr, blk, q_spec, k_spec, v_spec, g_spec, _, cols_spec, mats_spec, cdec_spec, _, out_row = _ret_specs(t, d, dk, rt)

    def body(q_ref, k_ref, v_ref, g_ref, cols_ref, mats_ref, cdec_ref, r_ref, rn_ref, sb_scr, st):
        p = pl.program_id(1)
        n = pl.program_id(2)
        af, ab, kf, kb = cols_ref[0, 0], cols_ref[0, 1], cols_ref[0, 2], cols_ref[0, 3]
        cf = cdec_ref[0, 0:1, :]
        cb = cdec_ref[0, 1:2, :]

        @pl.when(n == 0)
        def _():
            st[...] = jnp.zeros_like(st)

        @pl.when(p == 0)
        def _():
            for j in reversed(range(cpt)):
                rows = slice(j * CHUNK, (j + 1) * CHUNK)
                ch = blk(p, n) * cpt + j
                kk = k_ref[0, rows, :].astype(f32)
                sb_scr[ch] = st[...].astype(bf16)
                st[...] = st[...] * cb + _dot_tn((kk * kb).astype(bf16), v_ref[0, rows, :])

        @pl.when(p == 1)
        def _():
            for j in range(cpt):
                rows = slice(j * CHUNK, (j + 1) * CHUNK)
                ch = blk(p, n) * cpt + j
                qb = q_ref[0, rows, :]
                kkb = k_ref[0, rows, :]
                q = qb.astype(f32)
                kk = kkb.astype(f32)
                v = v_ref[0, rows, :]
                pm = (_dot_nt(qb, kkb) * mats_ref[0, 0]).astype(bf16)
                out = (_dot(pm, v) + _dot((q * af).astype(bf16), st[...].astype(bf16))
                       + _dot((q * ab).astype(bf16), sb_scr[ch]))
                st[...] = st[...] * cf + _dot_tn((kk * kf).astype(bf16), v)
                rhat = out * lax.rsqrt(jnp.mean(out * out, axis=-1, keepdims=True) + NORM_EPS)
                gg = g_ref[0, rows, :].astype(f32)
                r_ref[rows, :] = out.astype(bf16)
                rn_ref[rows, :] = (rhat * gg * _sigmoid(gg)).astype(bf16)

    return pl.pallas_call(
        body, name="ret_fwd", grid=(RET_HEADS, 2, nr),
        in_specs=[q_spec, k_spec, v_spec, g_spec, cols_spec, mats_spec, cdec_spec],
        out_specs=[out_row, out_row],
        out_shape=[jax.ShapeDtypeStruct((t, d), bf16), jax.ShapeDtypeStruct((t, d), bf16)],
        scratch_shapes=[pltpu.VMEM((t // CHUNK, dk, dk), bf16), pltpu.VMEM((dk, dk), f32)],
        compiler_params=_cparams(),
    )(proj, proj, proj, proj, cols, mats, cdec)


def ret_bwd(drn, r, proj, cols, mats, cdec, cos, sin):
    _, t, w2 = proj.shape
    d = w2 // 2
    dk = d // RET_HEADS
    rt = _row_tile(t)
    cpt = rt // CHUNK
    nr, blk, q_spec, k_spec, v_spec, g_spec, tab_spec, cols_spec, mats_spec, cdec_spec, in_row, out_row = _ret_specs(t, d, dk, rt)
    scale = dk ** -0.5

    def body(drn_ref, r_ref, q_ref, k_ref, v_ref, g_ref, cos_ref, sin_ref, cols_ref, mats_ref, cdec_ref,
             dq_ref, dk_ref, dv_ref, dg_ref, dlg_ref,
             sb_scr, gf_scr, st_s, st_g, acc_af, acc_ab, acc_vf, acc_vb, acc_sf, acc_sb):
        p = pl.program_id(1)
        n = pl.program_id(2)
        af, ab, kf, kb = cols_ref[0, 0], cols_ref[0, 1], cols_ref[0, 2], cols_ref[0, 3]
        af1, ab1, kf1, kb1 = cols_ref[0, 4], cols_ref[0, 5], cols_ref[0, 6], cols_ref[0, 7]
        cf = cdec_ref[0, 0:1, :]
        cb = cdec_ref[0, 1:2, :]

        @pl.when(n == 0)
        def _():
            st_s[...] = jnp.zeros_like(st_s)
            st_g[...] = jnp.zeros_like(st_g)

        @pl.when(jnp.logical_and(n == 0, p == 1))
        def _():
            for a in (acc_af, acc_ab, acc_vf, acc_vb, acc_sf, acc_sb):
                a[...] = jnp.zeros_like(a)

        def load(rows):
            cs, sn = cos_ref[rows, :], sin_ref[rows, :]
            q = q_ref[0, rows, :].astype(f32)
            kk = k_ref[0, rows, :].astype(f32)
            rr = r_ref[rows, :].astype(f32)
            rstd = lax.rsqrt(jnp.mean(rr * rr, axis=-1, keepdims=True) + NORM_EPS)
            rhat = rr * rstd
            gg = g_ref[0, rows, :].astype(f32)
            sg = _sigmoid(gg)
            dd = drn_ref[rows, :].astype(f32)
            drhat = dd * gg * sg
            dout = rstd * (drhat - rhat * jnp.mean(drhat * rhat, axis=-1, keepdims=True))
            dgr = dd * rhat * _dsilu(gg, sg)
            return q, kk, dout.astype(bf16), dgr, cs, sn

        @pl.when(p == 0)
        def _():
            for j in reversed(range(cpt)):
                rows = slice(j * CHUNK, (j + 1) * CHUNK)
                ch = blk(p, n) * cpt + j
                q, kk, doutb, _, _, _ = load(rows)
                sb_scr[ch] = st_s[...].astype(bf16)
                gf_scr[ch] = st_g[...].astype(bf16)
                st_s[...] = st_s[...] * cb + _dot_tn((kk * kb).astype(bf16), v_ref[0, rows, :])
                st_g[...] = st_g[...] * cf + _dot_tn((q * af).astype(bf16), doutb)

        @pl.when(p == 1)
        def _():
            for j in range(cpt):
                rows = slice(j * CHUNK, (j + 1) * CHUNK)
                ch = blk(p, n) * cpt + j
                q, kk, doutb, dgr, cs, sn = load(rows)
                v = v_ref[0, rows, :]
                qb = q_ref[0, rows, :]
                kkb = k_ref[0, rows, :]
                sf = st_s[...]
                gb = st_g[...]
                sfb = sf.astype(bf16)
                gbb = gb.astype(bf16)
                sbb = sb_scr[ch]
                gfb = gf_scr[ch]
                dmat = mats_ref[0, 0]
                scores = _dot_nt(qb, kkb)
                dpraw = _dot_nt(doutb, v)
                dpb = (dpraw * dmat).astype(bf16)
                pmb = (scores * dmat).astype(bf16)
                x1 = _dot_nt(doutb, sfb)
                x2 = _dot_nt(doutb, sbb)
                y1 = _dot_nt(v, gfb)
                y2 = _dot_nt(v, gbb)
                kdf = (kk * kf).astype(bf16)
                kdb = (kk * kb).astype(bf16)
                dq = _dot(dpb, kkb) + x1 * af + x2 * ab
                dkk = _dot_tn(dpb, qb) + y1 * kf + y2 * kb
                dv = _dot_tn(pmb, doutb) + _dot(kdf, gfb) + _dot(kdb, gbb)
                ps = dpraw * scores
                acc_af[...] += ps * mats_ref[0, 1]
                acc_ab[...] += ps * mats_ref[0, 2]
                acc_vf[...] += x1 * q * af1 + y1 * kk * kf1
                acc_vb[...] += x2 * q * ab1 + y2 * kk * kb1
                acc_sf[...] += gfb.astype(f32) * sf
                acc_sb[...] += gb * sbb.astype(f32)
                st_s[...] = sf * cf + _dot_tn(kdf, v)
                st_g[...] = gb * cb + _dot_tn((q * ab).astype(bf16), doutb)
                dq_ref[rows, :] = _rot_inv(dq, cs, sn).astype(bf16)
                dk_ref[rows, :] = (_rot_inv(dkk, cs, sn) * scale).astype(bf16)
                dv_ref[rows, :] = dv.astype(bf16)
                dg_ref[rows, :] = dgr.astype(bf16)

        @pl.when(jnp.logical_and(p == 1, n == nr - 1))
        def _():
            tf = jnp.sum(acc_af[...]) + jnp.sum(acc_vf[...]) + CHUNK * jnp.sum(acc_sf[...] * cf)
            tb = jnp.sum(acc_ab[...]) + jnp.sum(acc_vb[...]) + CHUNK * jnp.sum(acc_sb[...] * cb)
            rid = lax.broadcasted_iota(jnp.int32, (8, 128), 0)
            dlg_ref[0] = jnp.where(rid == 0, tf, jnp.where(rid == 1, tb, 0.0))

    nch = t // CHUNK
    return pl.pallas_call(
        body, name="ret_bwd", grid=(RET_HEADS, 2, nr),
        in_specs=[in_row, in_row, q_spec, k_spec, v_spec, g_spec, tab_spec, tab_spec, cols_spec, mats_spec, cdec_spec],
        out_specs=[out_row, out_row, out_row, out_row, pl.BlockSpec((1, 8, 128), lambda h, p, n: (h, 0, 0))],
        out_shape=[jax.ShapeDtypeStruct((t, d), bf16)] * 4 + [jax.ShapeDtypeStruct((RET_HEADS, 8, 128), f32)],
        scratch_shapes=[pltpu.VMEM((nch, dk, dk), bf16), pltpu.VMEM((nch, dk, dk), bf16),
                        pltpu.VMEM((dk, dk), f32), pltpu.VMEM((dk, dk), f32),
                        pltpu.VMEM((CHUNK, CHUNK), f32), pltpu.VMEM((CHUNK, CHUNK), f32),
                        pltpu.VMEM((CHUNK, dk), f32), pltpu.VMEM((CHUNK, dk), f32),
                        pltpu.VMEM((dk, dk), f32), pltpu.VMEM((dk, dk), f32)],
        compiler_params=_cparams(),
    )(drn, r, proj, proj, proj, proj, cos, sin, cols, mats, cdec)


def mix_fwd(a, rn, proj, wa, wb, wo, x1):
    t, d = x1.shape
    tm = _row_tile(t)

    def body(a_ref, rn_ref, p_ref, wa_ref, wb_ref, wo_ref, x_ref, xo_ref, ba_ref, br_ref):
        ba = _dot(a_ref[...], wa_ref[...])
        br = _dot(rn_ref[...], wb_ref[...])
        sa = _sigmoid(p_ref[0, :, 0:d].astype(f32))
        sb = _sigmoid(p_ref[0, :, d:2 * d].astype(f32))
        mix = (sa * ba + sb * br).astype(bf16)
        xo_ref[...] = x_ref[...] + _dot(mix, wo_ref[...])
        ba_ref[...] = ba.astype(bf16)
        br_ref[...] = br.astype(bf16)

    row = pl.BlockSpec((tm, d), lambda i: (i, 0))
    wsp = pl.BlockSpec((d, d), lambda i: (0, 0))
    return pl.pallas_call(
        body, name="mix_fwd", grid=(t // tm,),
        in_specs=[row, row, pl.BlockSpec((1, tm, 2 * d), lambda i: (3, i, 0)), wsp, wsp, wsp, row],
        out_specs=[row, row, row],
        out_shape=[jax.ShapeDtypeStruct((t, d), f32), jax.ShapeDtypeStruct((t, d), bf16), jax.ShapeDtypeStruct((t, d), bf16)],
        compiler_params=_cparams(),
    )(a, rn, proj, wa, wb, wo, x1)


def mix_bwd_act(dx2, ba, br, proj, wa, wb, wo, dep):
    t, d = dx2.shape
    tm = _row_tile(t)

    def body(dx_ref, ba_ref, br_ref, p_ref, wa_ref, wb_ref, wo_ref, dep_ref,
             da_ref, drn_ref, dga_ref, dgb_ref, mix_ref, dba_ref, dbr_ref, dxb_ref):
        dxb = dx_ref[...].astype(bf16)
        dxb_ref[...] = dxb
        dmix = _dot_nt(dxb, wo_ref[...])
        ba = ba_ref[...].astype(f32)
        br = br_ref[...].astype(f32)
        sa = _sigmoid(p_ref[0, :, 0:d].astype(f32))
        sb = _sigmoid(p_ref[0, :, d:2 * d].astype(f32))
        mix_ref[...] = (sa * ba + sb * br).astype(bf16)
        dba = (dmix * sa).astype(bf16)
        dbr = (dmix * sb).astype(bf16)
        dba_ref[...] = dba
        dbr_ref[...] = dbr
        dga_ref[...] = (dmix * ba * sa * (1.0 - sa)).astype(bf16)
        dgb_ref[...] = (dmix * br * sb * (1.0 - sb)).astype(bf16)
        da_ref[...] = _dot_nt(dba, wa_ref[...]).astype(bf16)
        drn_ref[...] = _dot_nt(dbr, wb_ref[...]).astype(bf16)

    row = pl.BlockSpec((tm, d), lambda i: (i, 0))
    wsp = pl.BlockSpec((d, d), lambda i: (0, 0))
    return pl.pallas_call(
        body, name="mix_bwd_act", grid=(t // tm,),
        in_specs=[row, row, row, pl.BlockSpec((1, tm, 2 * d), lambda i: (3, i, 0)), wsp, wsp, wsp, _ANY],
        out_specs=[row] * 8,
        out_shape=[jax.ShapeDtypeStruct((t, d), bf16)] * 8,
        compiler_params=_cparams(),
    )(dx2, ba, br, proj, wa, wb, wo, dep)


def inproj_bwd_act(segs, win, x1, ng, dx2):
    t, d = x1.shape
    s4 = win.shape[0]
    tm = _row_tile(t) // 2
    nseg = len(segs)

    def body(*refs):
        seg_refs = refs[:nseg]
        w_ref, x_ref, ng_ref, dx2_ref, dx1_ref, db_ref, dng_ref = refs[nseg:]
        i = pl.program_id(0)
        dh = None
        for e, sr in enumerate(seg_refs):
            sb = sr[...]
            part = _dot_nt(sb, w_ref[e // 2, :, (e % 2) * d:(e % 2 + 1) * d])
            dh = part if dh is None else dh + part
            _acc_out(db_ref.at[e], i == 0, jnp.sum(sb.astype(f32), axis=0, keepdims=True))
        _, xh, r = _rms(x_ref[...], ng_ref[...])
        dx1_ref[...] = dx2_ref[...] + _rms_bwd(dh, xh, r, ng_ref[...])
        _acc_out(dng_ref, i == 0, jnp.sum(dh * xh, axis=0, keepdims=True))

    row = pl.BlockSpec((tm, d), lambda i: (i, 0))
    vec = pl.BlockSpec((1, d), lambda i: (0, 0))
    return pl.pallas_call(
        body, name="inproj_bwd_act", grid=(t // tm,),
        in_specs=[row] * nseg + [pl.BlockSpec((s4, d, 2 * d), lambda i: (0, 0, 0), pipeline_mode=pl.Buffered(1)),
                                 row, vec, row],
        out_specs=[row, pl.BlockSpec((nseg, 1, d), lambda i: (0, 0, 0)), vec],
        out_shape=[jax.ShapeDtypeStruct((t, d), f32), jax.ShapeDtypeStruct((nseg, 1, d), f32),
                   jax.ShapeDtypeStruct((1, d), f32)],
        compiler_params=_cparams(),
    )(*segs, win, x1, ng, dx2)


def loss_head(x3, fng, tgt):
    t, d = x3.shape
    tm = _row_tile(t)

    def body(x_ref, g_ref, t_ref, loss_ref, dx_ref, dg_ref):
        i = pl.program_id(0)
        y, xh, r = _rms(x_ref[...], g_ref[...])
        diff = y - t_ref[...]
        part = 0.5 * jnp.sum(jnp.sum(diff * diff, axis=0, keepdims=True), axis=1, keepdims=True) / d
        _acc_out(loss_ref, i == 0, jnp.broadcast_to(part, (1, 128)))
        dy = diff * (1.0 / d)
        dx_ref[...] = _rms_bwd(dy, xh, r, g_ref[...])
        _acc_out(dg_ref, i == 0, jnp.sum(dy * xh, axis=0, keepdims=True))

    row = pl.BlockSpec((tm, d), lambda i: (i, 0))
    vec = pl.BlockSpec((1, d), lambda i: (0, 0))
    return pl.pallas_call(
        body, name="loss_head", grid=(t // tm,),
        in_specs=[row, vec, row],
        out_specs=[pl.BlockSpec((1, 128), lambda i: (0, 0)), row, vec],
        out_shape=[jax.ShapeDtypeStruct((1, 128), f32), jax.ShapeDtypeStruct((t, d), f32), jax.ShapeDtypeStruct((1, d), f32)],
        compiler_params=_cparams(),
    )(x3, fng, tgt)


def _place():
    return lax.axis_index("x"), lax.axis_index("y"), lax.axis_index("c")


def _other_chips(x, y):
    return [(1 - x, y), (x, 1 - y), (1 - x, 1 - y)]


_ANY = pl.BlockSpec(memory_space=pl.ANY)


_HBM = pl.BlockSpec(memory_space=pltpu.HBM)
_SEM = pl.BlockSpec(memory_space=pltpu.SEMAPHORE)
_EFFECT = pltpu.SideEffectType.DATAFLOW_SIDE_EFFECTING


def _hbm(a):
    return pltpu.with_memory_space_constraint(a, pltpu.HBM)


def _half_rows(ref, c):
    half = ref.shape[1] // 2
    return pl.ds(pl.multiple_of(c * half, 16), half)


def _chip_copy(src, dst, send_sem, recv_sem, chip, c):
    return pltpu.make_async_remote_copy(src_ref=src, dst_ref=dst, send_sem=send_sem, recv_sem=recv_sem,
                                        device_id=(chip[0], chip[1], c), device_id_type=MESH)


def gather_start(bufs, groups, name):
    nb, ng = len(bufs), len(groups)

    def body(*refs):
        ins = refs[:nb]
        sems = refs[nb:nb + 2 * ng]
        token = refs[-1]
        x, y, c = _place()
        k = 2 * x + y
        for gi, grp in enumerate(groups):
            for wi, w in enumerate(grp):
                mine = ins[w].at[k, _half_rows(ins[w], c)]
                for j, chip in enumerate(_other_chips(x, y)):
                    _chip_copy(mine, mine, sems[2 * gi].at[3 * wi + j], sems[2 * gi + 1].at[3 * wi + j], chip, c).start()
        token[...] = jnp.zeros_like(token)

    sem_shapes = []
    for grp in groups:
        sem_shapes += [pltpu.SemaphoreType.DMA((3 * len(grp),)), pltpu.SemaphoreType.DMA((3 * len(grp),))]
    outs = pl.pallas_call(
        body, name=name,
        out_shape=sem_shapes + [pltpu.HBM(b.shape, b.dtype) for b in bufs] + [jax.ShapeDtypeStruct((8, 128), f32)],
        in_specs=[_HBM] * nb,
        out_specs=[_SEM] * (2 * ng) + [_HBM] * nb + [pl.BlockSpec(memory_space=pltpu.VMEM)],
        input_output_aliases={w: 2 * ng + w for w in range(nb)},
        compiler_params=pltpu.CompilerParams(has_side_effects=_EFFECT),
    )(*[_hbm(b) for b in bufs])
    sems = [(outs[2 * gi], outs[2 * gi + 1]) for gi in range(ng)]
    return sems, list(outs[2 * ng:2 * ng + nb]), outs[-1]


def gather_wait(bufs, sems, after, name):
    n = len(bufs)

    def body(*refs):
        ins = refs[:n]
        send_sems, recv_sems = refs[n], refs[n + 1]
        x, y, c = _place()
        k = 2 * x + y
        for wi in range(n):
            half = _half_rows(ins[wi], c)
            for j, chip in enumerate(_other_chips(x, y)):
                cp = _chip_copy(ins[wi].at[k, half], ins[wi].at[2 * chip[0] + chip[1], half], send_sems.at[3 * wi + j],
                                recv_sems.at[3 * wi + j], chip, c)
                cp.wait_send()
                cp.wait_recv()

    outs = pl.pallas_call(
        body, name=name,
        out_shape=[pltpu.HBM(b.shape, b.dtype) for b in bufs],
        in_specs=[_HBM] * n + [_SEM, _SEM, _ANY],
        out_specs=[_HBM] * n,
        input_output_aliases={i: i for i in range(n)},
        compiler_params=pltpu.CompilerParams(has_side_effects=_EFFECT),
    )(*bufs, sems[0], sems[1], after)
    return list(outs)


def gather_forward(bufs, name):
    n = len(bufs)

    def body(*refs):
        ins = refs[n:2 * n]
        send_sems, recv_sems = refs[2 * n], refs[2 * n + 1]
        x, y, c = _place()
        copies = []
        for wi in range(n):
            for j, chip in enumerate(_other_chips(x, y)):
                kp = 2 * chip[0] + chip[1]
                got = ins[wi].at[kp, _half_rows(ins[wi], c)]
                cp = pltpu.make_async_remote_copy(
                    src_ref=got, dst_ref=got, send_sem=send_sems.at[3 * wi + j], recv_sem=recv_sems.at[3 * wi + j],
                    device_id=(x, y, 1 - c), device_id_type=MESH)
                cp.start()
                copies.append((cp, wi, kp, j))
        for cp, wi, kp, j in copies:
            cp.wait_send()
            theirs = ins[wi].at[kp, _half_rows(ins[wi], 1 - c)]
            pltpu.make_async_remote_copy(
                src_ref=theirs, dst_ref=theirs, send_sem=send_sems.at[3 * wi + j], recv_sem=recv_sems.at[3 * wi + j],
                device_id=(x, y, 1 - c), device_id_type=MESH).wait_recv()

    outs = pl.pallas_call(
        body, name=name,
        out_shape=[jax.ShapeDtypeStruct(b.shape, b.dtype) for b in bufs],
        in_specs=[_ANY] * n, out_specs=[_ANY] * n,
        input_output_aliases={i: i for i in range(n)},
        scratch_shapes=[pltpu.SemaphoreType.DMA((3 * n,)), pltpu.SemaphoreType.DMA((3 * n,))],
    )(*bufs)
    return list(outs)


def exchange_start(grads, name):
    n = len(grads)
    lands = [lax.empty((3,) + g.shape[1:], g.dtype) for g in grads]

    def body(*refs):
        ins = refs[:n]
        land = refs[n:2 * n]
        send_sems, recv_sems = refs[2 * n], refs[2 * n + 1]
        token = refs[-1]
        x, y, c = _place()
        for wi in range(n):
            for j, chip in enumerate(_other_chips(x, y)):
                _chip_copy(ins[wi].at[2 * chip[0] + chip[1]], land[wi].at[j], send_sems.at[3 * wi + j],
                           recv_sems.at[3 * wi + j], chip, c).start()
        token[...] = jnp.zeros_like(token)

    outs = pl.pallas_call(
        body, name=name,
        out_shape=[pltpu.SemaphoreType.DMA((3 * n,)), pltpu.SemaphoreType.DMA((3 * n,))]
        + [pltpu.HBM(g.shape, g.dtype) for g in grads] + [pltpu.HBM(l.shape, l.dtype) for l in lands]
        + [jax.ShapeDtypeStruct((8, 128), f32)],
        in_specs=[_HBM] * (2 * n),
        out_specs=[_SEM, _SEM] + [_HBM] * (2 * n) + [pl.BlockSpec(memory_space=pltpu.VMEM)],
        input_output_aliases={i: 2 + i for i in range(2 * n)},
        compiler_params=pltpu.CompilerParams(has_side_effects=_EFFECT),
    )(*[_hbm(g) for g in grads], *[_hbm(l) for l in lands])
    return (outs[0], outs[1]), list(outs[2:2 + n]), list(outs[2 + n:2 + 2 * n]), outs[-1]


def exchange_wait(grads, lands, sems, after, name):
    n = len(grads)

    def body(*refs):
        ins = refs[:n]
        land = refs[n:2 * n]
        send_sems, recv_sems = refs[2 * n], refs[2 * n + 1]
        x, y, c = _place()
        for wi in range(n):
            for j, chip in enumerate(_other_chips(x, y)):
                cp = _chip_copy(ins[wi].at[2 * chip[0] + chip[1]], land[wi].at[j], send_sems.at[3 * wi + j],
                                recv_sems.at[3 * wi + j], chip, c)
                cp.wait_send()
                cp.wait_recv()

    outs = pl.pallas_call(
        body, name=name,
        out_shape=[pltpu.HBM(g.shape, g.dtype) for g in grads] + [pltpu.HBM(l.shape, l.dtype) for l in lands],
        in_specs=[_HBM] * (2 * n) + [_SEM, _SEM, _ANY],
        out_specs=[_HBM] * (2 * n),
        input_output_aliases={i: i for i in range(2 * n)},
        compiler_params=pltpu.CompilerParams(has_side_effects=_EFFECT),
    )(*grads, *lands, sems[0], sems[1], after)
    return list(outs[:n]), list(outs[n:])


def _split_start(body, name, n_sems, operands):
    n = len(operands)
    outs = pl.pallas_call(
        body, name=name,
        out_shape=[pltpu.SemaphoreType.DMA((n_sems,)), pltpu.SemaphoreType.DMA((n_sems,))]
        + [pltpu.HBM(o.shape, o.dtype) for o in operands] + [jax.ShapeDtypeStruct((8, 128), f32)],
        in_specs=[_HBM] * n,
        out_specs=[_SEM, _SEM] + [_HBM] * n + [pl.BlockSpec(memory_space=pltpu.VMEM)],
        input_output_aliases={i: 2 + i for i in range(n)},
        compiler_params=pltpu.CompilerParams(has_side_effects=_EFFECT),
    )(*[_hbm(o) for o in operands])
    return (outs[0], outs[1]), list(outs[2:2 + n]), outs[-1]


def _split_wait(body, name, operands, sems, after):
    n = len(operands)
    outs = pl.pallas_call(
        body, name=name,
        out_shape=[pltpu.HBM(o.shape, o.dtype) for o in operands],
        in_specs=[_HBM] * n + [_SEM, _SEM, _ANY],
        out_specs=[_HBM] * n,
        input_output_aliases={i: i for i in range(n)},
        compiler_params=pltpu.CompilerParams(has_side_effects=_EFFECT),
    )(*operands, sems[0], sems[1], after)
    return list(outs)


def _sibling_copy(src, dst, send_sem, recv_sem):
    x, y, c = _place()
    return pltpu.make_async_remote_copy(src_ref=src, dst_ref=dst, send_sem=send_sem, recv_sem=recv_sem,
                                        device_id=(x, y, 1 - c), device_id_type=MESH)


def swap_start(parts, name):
    n = len(parts)

    def body(*refs):
        for w in range(n):
            _sibling_copy(refs[w], refs[n + w], refs[2 * n].at[w], refs[2 * n + 1].at[w]).start()
        refs[-1][...] = jnp.zeros_like(refs[-1])

    sems, ops, token = _split_start(body, name, n, list(parts) + [lax.empty(p.shape, p.dtype) for p in parts])
    return sems, ops[:n], ops[n:], token


def swap_wait(parts, lands, sems, after, name):
    n = len(parts)

    def body(*refs):
        for w in range(n):
            cp = _sibling_copy(refs[w], refs[n + w], refs[2 * n].at[w], refs[2 * n + 1].at[w])
            cp.wait_send()
            cp.wait_recv()

    outs = _split_wait(body, name, list(parts) + list(lands), sems, after)
    return outs[:n], outs[n:]


def _all_peers(x, y, c):
    return [(1 - x if m & 4 else x, 1 - y if m & 2 else y, 1 - c if m & 1 else c) for m in range(1, N_DEV)]


def small_start(block):
    land = jnp.broadcast_to(block[None], (N_DEV,) + block.shape)

    def body(b_ref, land_ref, send_sems, recv_sems, b_thru, land_thru, token):
        x, y, c = _place()
        me = 4 * x + 2 * y + c
        for m, peer in enumerate(_all_peers(x, y, c)):
            pltpu.make_async_remote_copy(src_ref=b_ref, dst_ref=land_ref.at[me], send_sem=send_sems.at[m],
                                         recv_sem=recv_sems.at[m], device_id=peer, device_id_type=MESH).start()
        token[...] = jnp.zeros_like(token)

    sems, ops, token = _split_start(body, "small_start", N_DEV - 1, [block, land])
    return sems, ops[0], ops[1], token


def small_wait(block, land, sems, after):
    def body(b_ref, land_ref, send_sems, recv_sems, after_ref, b_thru, land_thru):
        x, y, c = _place()
        for m, (px, py, pc) in enumerate(_all_peers(x, y, c)):
            cp = pltpu.make_async_remote_copy(src_ref=b_ref, dst_ref=land_ref.at[4 * px + 2 * py + pc],
                                              send_sem=send_sems.at[m], recv_sem=recv_sems.at[m],
                                              device_id=(px, py, pc), device_id_type=MESH)
            cp.wait_send()
            cp.wait_recv()

    return _split_wait(body, "small_wait", [block, land], sems, after)[1]


def _adamw(w, g, m, v):
    m = ADAM_B1 * m + (1.0 - ADAM_B1) * g
    v = ADAM_B2 * v + (1.0 - ADAM_B2) * (g * g)
    m_hat = m / (1.0 - ADAM_B1 ** ADAM_STEP)
    v_hat = v / (1.0 - ADAM_B2 ** ADAM_STEP)
    delta = -ADAM_LR * (m_hat / (jnp.sqrt(v_hat) + ADAM_EPS) + ADAM_WD * w)
    return delta, m, v


def _ew_tile(rows):
    for cand in (256, 176, 128, 64, 32, 16, 8):
        if rows % cand == 0:
            return cand
    return rows


def sum_partials(chip, own, land, name):
    _, r, c = own.shape
    tr = _ew_tile(r)

    def body(k_ref, own_ref, p_ref, o_ref):
        o_ref[...] = ((own_ref[0].astype(f32) + p_ref[0].astype(f32)) + p_ref[1].astype(f32)) + p_ref[2].astype(f32)

    return pl.pallas_call(
        body, name=name,
        grid_spec=pltpu.PrefetchScalarGridSpec(
            num_scalar_prefetch=1, grid=(r // tr,),
            in_specs=[pl.BlockSpec((1, tr, c), lambda i, k: (k[0], i, 0)), pl.BlockSpec((3, tr, c), lambda i, k: (0, i, 0))],
            out_specs=pl.BlockSpec((tr, c), lambda i, k: (i, 0))),
        out_shape=jax.ShapeDtypeStruct((r, c), f32),
        compiler_params=_cparams(),
    )(chip, own, land)


def adamw_shard(p_mine, p_sibling, w, m, v, name):
    r, c = w.shape
    tr = _ew_tile(r)

    def body(a_ref, b_ref, w_ref, m_ref, v_ref, g_ref, d_ref, mo_ref, vo_ref):
        g = a_ref[...] + b_ref[...]
        delta, mn, vn = _adamw(w_ref[...], g, m_ref[...], v_ref[...])
        g_ref[...] = g
        d_ref[...] = delta
        mo_ref[...] = mn
        vo_ref[...] = vn

    blk = pl.BlockSpec((tr, c), lambda i: (i, 0))
    return pl.pallas_call(
        body, name=name, grid=(r // tr,),
        in_specs=[blk] * 5, out_specs=[blk] * 4,
        out_shape=[jax.ShapeDtypeStruct((r, c), f32)] * 4,
        compiler_params=_cparams(),
    )(p_mine, p_sibling, w, m, v)


def adamw_small(g8, w, m, v):
    _, r, lanes = g8.shape

    def body(g_ref, w_ref, m_ref, v_ref, go_ref, d_ref, mo_ref, vo_ref):
        g = g_ref[0]
        for i in range(1, N_DEV):
            g = g + g_ref[i]
        delta, mn, vn = _adamw(w_ref[...], g, m_ref[...], v_ref[...])
        go_ref[...] = g
        d_ref[...] = delta
        mo_ref[...] = mn
        vo_ref[...] = vn

    return pl.pallas_call(
        body, name="adamw_small",
        out_shape=[jax.ShapeDtypeStruct((r, lanes), f32)] * 4,
        compiler_params=_cparams(),
    )(g8, w, m, v)


def _size(shape):
    n = 1
    for e in shape:
        n *= e
    return n


def _pack_rows(shapes):
    rows = [-(-_size(s) // 1024) * 8 for s in shapes]
    return rows, sum(rows)


def _pack(arrs, shapes):
    rows, _ = _pack_rows(shapes)
    parts = [jnp.pad(a.reshape(-1).astype(f32), (0, r * 128 - _size(s))).reshape(r, 128)
             for a, s, r in zip(arrs, shapes, rows)]
    return jnp.concatenate(parts, axis=0)


def _unpack(block, shapes):
    rows, _ = _pack_rows(shapes)
    out, off = [], 0
    for s, r in zip(shapes, rows):
        out.append(block[off:off + r].reshape(-1)[:_size(s)].reshape(s))
        off += r
    return out


TRANSPOSED = ("ffn1_w_gate", "ffn1_w_up", "ffn2_w_gate", "ffn2_w_up")


def _shard2d(a, n):
    return a[0].T if n in TRANSPOSED else a[0]


def _unshard(a, n):
    return (a.T if n in TRANSPOSED else a)[None]


BIG = ("ffn1_w_gate", "ffn1_w_up", "ffn1_w_down", "w_in", "w_branch_a", "w_branch_b", "w_out",
       "ffn2_w_gate", "ffn2_w_up", "ffn2_w_down")
SMALL = ("ffn1_norm", "mix_norm", "b_in", "sgu_norm_g", "sgu_norm_b", "sgu_w_s", "sgu_b_s", "ret_decay_logit",
         "ffn2_norm", "final_norm")
WEIGHTS = ("ffn1_norm", "ffn1_w_gate", "ffn1_w_up", "ffn1_w_down", "mix_norm", "w_in", "b_in", "sgu_norm_g",
           "sgu_norm_b", "sgu_w_s", "sgu_b_s", "ret_decay_logit", "w_branch_a", "w_branch_b", "w_out", "ffn2_norm",
           "ffn2_w_gate", "ffn2_w_up", "ffn2_w_down", "final_norm")


def kernel(x, ffn1_norm, ffn1_w_gate, ffn1_w_up, ffn1_w_down, mix_norm, w_in, b_in, sgu_norm_g, sgu_norm_b, sgu_w_s, sgu_b_s, ret_decay_logit, w_branch_a, w_branch_b, w_out, ffn2_norm, ffn2_w_gate, ffn2_w_up, ffn2_w_down, final_norm, loss_target, m_ffn1_norm, m_ffn1_w_gate, m_ffn1_w_up, m_ffn1_w_down, m_mix_norm, m_w_in, m_b_in, m_sgu_norm_g, m_sgu_norm_b, m_sgu_w_s, m_sgu_b_s, m_ret_decay_logit, m_w_branch_a, m_w_branch_b, m_w_out, m_ffn2_norm, m_ffn2_w_gate, m_ffn2_w_up, m_ffn2_w_down, m_final_norm, v_ffn1_norm, v_ffn1_w_gate, v_ffn1_w_up, v_ffn1_w_down, v_mix_norm, v_w_in, v_b_in, v_sgu_norm_g, v_sgu_norm_b, v_sgu_w_s, v_sgu_b_s, v_ret_decay_logit, v_w_branch_a, v_w_branch_b, v_w_out, v_ffn2_norm, v_ffn2_w_gate, v_ffn2_w_up, v_ffn2_w_down, v_final_norm):
    p = dict(ffn1_norm=ffn1_norm, ffn1_w_gate=ffn1_w_gate, ffn1_w_up=ffn1_w_up, ffn1_w_down=ffn1_w_down,
             mix_norm=mix_norm, w_in=w_in, b_in=b_in, sgu_norm_g=sgu_norm_g, sgu_norm_b=sgu_norm_b, sgu_w_s=sgu_w_s,
             sgu_b_s=sgu_b_s, ret_decay_logit=ret_decay_logit, w_branch_a=w_branch_a, w_branch_b=w_branch_b,
             w_out=w_out, ffn2_norm=ffn2_norm, ffn2_w_gate=ffn2_w_gate, ffn2_w_up=ffn2_w_up, ffn2_w_down=ffn2_w_down,
             final_norm=final_norm)
    mom = dict(ffn1_norm=m_ffn1_norm, ffn1_w_gate=m_ffn1_w_gate, ffn1_w_up=m_ffn1_w_up, ffn1_w_down=m_ffn1_w_down,
               mix_norm=m_mix_norm, w_in=m_w_in, b_in=m_b_in, sgu_norm_g=m_sgu_norm_g, sgu_norm_b=m_sgu_norm_b,
               sgu_w_s=m_sgu_w_s, sgu_b_s=m_sgu_b_s, ret_decay_logit=m_ret_decay_logit, w_branch_a=m_w_branch_a,
               w_branch_b=m_w_branch_b, w_out=m_w_out, ffn2_norm=m_ffn2_norm, ffn2_w_gate=m_ffn2_w_gate,
               ffn2_w_up=m_ffn2_w_up, ffn2_w_down=m_ffn2_w_down, final_norm=m_final_norm)
    var = dict(ffn1_norm=v_ffn1_norm, ffn1_w_gate=v_ffn1_w_gate, ffn1_w_up=v_ffn1_w_up, ffn1_w_down=v_ffn1_w_down,
               mix_norm=v_mix_norm, w_in=v_w_in, b_in=v_b_in, sgu_norm_g=v_sgu_norm_g, sgu_norm_b=v_sgu_norm_b,
               sgu_w_s=v_sgu_w_s, sgu_b_s=v_sgu_b_s, ret_decay_logit=v_ret_decay_logit, w_branch_a=v_w_branch_a,
               w_branch_b=v_w_branch_b, w_out=v_w_out, ffn2_norm=v_ffn2_norm, ffn2_w_gate=v_ffn2_w_gate,
               ffn2_w_up=v_ffn2_w_up, ffn2_w_down=v_ffn2_w_down, final_norm=v_final_norm)

    xs = x[0]
    tgt = loss_target[0]
    t, d = xs.shape
    dk = d // RET_HEADS
    tm = _row_tile(t)

    shards2d = {n: _shard2d(p[n], n) for n in BIG}
    chip = (2 * lax.axis_index("x") + lax.axis_index("y")).astype(jnp.int32).reshape(1)
    groups = {"ffn1": ("ffn1_w_gate", "ffn1_w_up", "ffn1_w_down"), "in": ("w_in",),
              "mix": ("w_branch_a", "w_branch_b", "w_out"), "ffn2": ("ffn2_w_gate", "ffn2_w_up", "ffn2_w_down")}
    def own_slot(n, zero):
        sh = shards2d[n].astype(bf16) + zero
        return lax.dynamic_update_index_in_dim(lax.empty((N_CHIPS,) + sh.shape, bf16), sh, chip[0], 0)

    sems, bufs, tok = gather_start([own_slot(n, jnp.zeros((), bf16)) for n in groups["ffn1"]], [[0, 1, 2]],
                                   "gather_start_ffn1")
    gsem = {"ffn1": sems[0]}
    pending = dict(zip(groups["ffn1"], bufs))
    rest = [n for g in ("in", "mix", "ffn2") for n in groups[g]]
    sems, bufs, tok_rest = gather_start([own_slot(n, tok[0, 0].astype(bf16)) for n in rest],
                                 [[rest.index(n) for n in groups[g]] for g in ("in", "mix", "ffn2")], "gather_start_rest")
    gsem.update(zip(("in", "mix", "ffn2"), sems))
    pending.update(zip(rest, bufs))

    def arrive(gs, after):
        got = []
        for g in gs:
            got += gather_wait([pending[n] for n in groups[g]], gsem[g], after, "gather_wait_" + g)
        return gather_forward(got, "gather_forward_" + gs[0])

    bin4 = b_in.reshape(N_CHIPS, 1, 2 * d)
    ws_b = sgu_w_s[0].astype(bf16)
    bs_c = sgu_b_s[0][:, :, None]
    cols, mats, cdec, cos, sin = retention_constants(ret_decay_logit[0], t, dk)

    wg1, wu1, wd1 = [_pair_shards(w) for w in arrive(["ffn1"], tok_rest)]
    x1, g1, u1 = ffn_fwd(xs, ffn1_norm, wg1, wu1, wd1, "ffn1_fwd")
    win, = arrive(["in"], x1)
    proj, hb2 = inproj_fwd(x1, mix_norm, win, bin4, cos, sin)
    a = sgu_fwd(proj, sgu_norm_g, sgu_norm_b, ws_b, bs_c)
    r, rn = ret_fwd(proj, cols, mats, cdec)
    wa, wb, wo, wg2, wu2, wd2 = arrive(["mix", "ffn2"], rn)
    wa, wb, wo = [w.reshape(d, d) for w in (wa, wb, wo)]
    wg2, wu2, wd2 = [_pair_shards(w) for w in (wg2, wu2, wd2)]
    x2, ba, br = mix_fwd(a, rn, proj, wa, wb, wo, x1)
    x3, g2, u2 = ffn_fwd(x2, ffn2_norm, wg2, wu2, wd2, "ffn2_fwd")
    loss_blk, dx3, d_final = loss_head(x3, final_norm.reshape(1, d), tgt)

    sent = {}
    dx2, dg2, du2, act2, hb3, dyb2, d_ffn2n = ffn_bwd_act(dx3, x2, ffn2_norm, g2, u2, wg2, wu2, wd2, "ffn2_bwd_act", tok)
    sent["ffn2"] = exchange_start(ffn_weight_grads(hb3, dyb2, dg2, du2, act2, "ffn2_grad", tok), "exchange_start_ffn2")
    da, drn, dga, dgb, mixb, dba, dbr, dx2b = mix_bwd_act(dx2, ba, br, proj, wa, wb, wo, sent["ffn2"][3])
    tg = min(t, 2048)
    row = pl.BlockSpec((tg, d), lambda s, i: (i, 0))

    def square_grad(xa, ya, name):
        return tn_matmul(xa, [ya], row, [row], 1, d, [d], t, tg, name, tok).reshape(N_CHIPS, d // N_CHIPS, d)

    sent["mix"] = exchange_start([square_grad(a, dba, "grad_w_branch_a"), square_grad(rn, dbr, "grad_w_branch_b"),
                                  square_grad(mixb, dx2b, "grad_w_out")], "exchange_start_mix")
    dua, dva, d_ws, d_bs, d_sng, d_snb = sgu_bwd(da, proj, sgu_norm_g, sgu_norm_b, ws_b, bs_c, sent["mix"][3])
    dq, dkr, dv, dgr, dlg = ret_bwd(drn, r, proj, cols, mats, cdec, cos, sin)
    segs = [dua, dva, dq, dkr, dv, dgr, dga, dgb]
    dx1, d_bin, d_mixn = inproj_bwd_act(segs, win, x1, mix_norm, dx2)
    sent["in"] = exchange_start([jnp.concatenate(
        [tn_matmul(hb2, [segs[2 * s], segs[2 * s + 1]], row, [row, row], 1, d, [d, d], t, tg, "grad_w_in_%d" % s, tok)
         for s in range(N_CHIPS)], axis=0)], "exchange_start_in")
    grad_x, dg1, du1, act1, hb1, dyb1, d_ffn1n = ffn_bwd_act(dx1, xs, ffn1_norm, g1, u1, wg1, wu1, wd1, "ffn1_bwd_act",
                                                              sent["in"][3])
    dlogit = dlg[:, 0:2, 0].T * jax.nn.sigmoid(-ret_decay_logit[0].astype(f32))
    small_g = dict(ffn1_norm=d_ffn1n, mix_norm=d_mixn, b_in=d_bin, sgu_norm_g=d_sng, sgu_norm_b=d_snb, sgu_w_s=d_ws,
                   sgu_b_s=d_bs, ret_decay_logit=dlogit, ffn2_norm=d_ffn2n, final_norm=d_final)
    shapes = [p[n].shape for n in SMALL]
    small_sems, small_blk, small_land, small_tok = small_start(_pack([small_g[n] for n in SMALL], shapes))
    sent["ffn1"] = exchange_start(ffn_weight_grads(hb1, dyb1, dg1, du1, act1, "ffn1_grad", small_tok),
                                  "exchange_start_ffn1")

    out_g, out_d, out_m, out_v = {}, {}, {}, {}
    swaps = {}

    def reduce_plane(g, after):
        gsems, own, lands, _ = sent[g]
        own, lands = exchange_wait(own, lands, gsems, after, "exchange_wait_" + g)
        plane = [sum_partials(chip, o, l, "sum_" + n) for n, o, l in zip(groups[g], own, lands)]
        swaps[g] = swap_start(plane, "swap_start_" + g)
        return swaps[g][3]

    def update(g, after):
        ssems, plane, lands, _ = swaps[g]
        plane, other = swap_wait(plane, lands, ssems, after, "swap_wait_" + g)
        for n, mine, sib in zip(groups[g], plane, other):
            res = adamw_shard(mine, sib, shards2d[n], _shard2d(mom[n], n), _shard2d(var[n], n), "adamw_" + n)
            out_g[n], out_d[n], out_m[n], out_v[n] = [_unshard(o, n) for o in res]
        return out_g[groups[g][-1]]

    after = reduce_plane("ffn2", sent["ffn1"][3])
    after = reduce_plane("mix", after)
    after = update("ffn2", after)
    after = reduce_plane("in", after)
    after = update("mix", after)
    g8 = small_wait(small_blk, small_land, small_sems, after)
    sg, sd, sm, sv = adamw_small(g8, _pack([p[n] for n in SMALL], shapes), _pack([mom[n] for n in SMALL], shapes),
                                 _pack([var[n] for n in SMALL], shapes))
    for res, blockv in ((out_g, sg), (out_d, sd), (out_m, sm), (out_v, sv)):
        for n, val in zip(SMALL, _unpack(blockv, shapes)):
            res[n] = val
    after = update("in", sg)
    after = reduce_plane("ffn1", after)
    update("ffn1", after)

    loss = lax.psum(loss_blk[0, 0], ("x", "y", "c"))
    return (loss, grad_x[None], *[out_g[n] for n in WEIGHTS], *[out_d[n] for n in WEIGHTS],
            *[out_m[n] for n in WEIGHTS], *[out_v[n] for n in WEIGHTS])
```

```python
import functools

import jax
import jax.numpy as jnp
from jax import lax
from jax.experimental import pallas as pl
from jax.experimental.pallas import tpu as pltpu

f32 = jnp.float32
bf16 = jnp.bfloat16

SGU_CHUNK = 128
CHUNK = 128
RET_HEADS = 4
SGU_GROUPS = 4
ROPE_BASE = 10000.0
NORM_EPS = 1e-6
ADAM_LR = 0.001
ADAM_B1 = 0.9
ADAM_B2 = 0.999
ADAM_EPS = 1e-08
ADAM_WD = 0.01
ADAM_STEP = 10
N_CHIPS = 4
N_DEV = 8
MESH = pl.DeviceIdType.MESH
VMEM_LIMIT = 52 * 1024 * 1024
VMEM_LIMIT_WIDE = 62 * 1024 * 1024

_NT = (((1,), (1,)), ((), ()))
_TN = (((0,), (0,)), ((), ()))


def _cparams(limit=None):
    return pltpu.CompilerParams(vmem_limit_bytes=VMEM_LIMIT if limit is None else limit)


def _row_tile(t):
    return 512 if t >= 2048 else t // 2


def _dot(a, b):
    return jnp.dot(a, b, preferred_element_type=f32)


def _dot_nt(a, b):
    return lax.dot_general(a, b, _NT, preferred_element_type=f32)


def _dot_tn(a, b):
    return lax.dot_general(a, b, _TN, preferred_element_type=f32)


def _rms(x, g):
    r = lax.rsqrt(jnp.mean(x * x, axis=-1, keepdims=True) + NORM_EPS)
    xh = x * r
    return xh * g, xh, r


def _rms_bwd(dy, xh, r, g):
    dxh = dy * g
    return r * (dxh - xh * jnp.mean(dxh * xh, axis=-1, keepdims=True))


def _sigmoid(x):
    return jax.nn.sigmoid(x)


def _dsilu(g, sg):
    return sg * (1.0 + g * (1.0 - sg))


def _gelu(x):
    return 0.5 * x * (1.0 + lax.erf(x * 0.7071067811865476))


def _dgelu(x):
    return 0.5 * (1.0 + lax.erf(x * 0.7071067811865476)) + x * jnp.exp(-0.5 * x * x) * 0.3989422804014327


def _acc_out(ref, first, val):
    @pl.when(first)
    def _():
        ref[...] = val

    @pl.when(jnp.logical_not(first))
    def _():
        ref[...] += val


def _ffn_tile(t):
    return 256 if t >= 2048 else t // 2


def ffn_fwd(x, ng, wg, wu, wd, name):
    t, d = x.shape
    ns, fs, _ = wg.shape
    tm = _ffn_tile(t)

    def body(x_ref, ng_ref, wg_ref, wu_ref, wd_ref, xo_ref, g_ref, u_ref):
        xx = x_ref[...]
        y, _, _ = _rms(xx, ng_ref[...])
        h = y.astype(bf16)
        acc = None
        for s in range(ns):
            g = _dot_nt(h, wg_ref[s])
            u = _dot_nt(h, wu_ref[s])
            g_ref[s] = g.astype(bf16)
            u_ref[s] = u.astype(bf16)
            part = _dot((g * _sigmoid(g) * u).astype(bf16), wd_ref[s])
            acc = part if acc is None else acc + part
        xo_ref[...] = xx + 0.5 * acc

    row = pl.BlockSpec((tm, d), lambda i: (i, 0))
    shard = pl.BlockSpec((ns, tm, fs), lambda i: (0, i, 0))
    wspec = pl.BlockSpec((ns, fs, d), lambda i: (0, 0, 0), pipeline_mode=pl.Buffered(1))
    return pl.pallas_call(
        body, name=name, grid=(t // tm,),
        in_specs=[row, pl.BlockSpec((1, d), lambda i: (0, 0)), wspec, wspec, wspec],
        out_specs=[row, shard, shard],
        out_shape=[jax.ShapeDtypeStruct((t, d), f32), jax.ShapeDtypeStruct((ns, t, fs), bf16),
                   jax.ShapeDtypeStruct((ns, t, fs), bf16)],
        compiler_params=_cparams(),
    )(x, ng, wg, wu, wd)


def ffn_bwd_act(dxo, x, ng, g, u, wg, wu, wd, name, dep):
    t, d = x.shape
    ns, fs, _ = wg.shape
    tm = _ffn_tile(t)

    def body(dxo_ref, x_ref, ng_ref, g_ref, u_ref, wg_ref, wu_ref, wd_ref, dep_ref,
             dx_ref, dg_ref, du_ref, act_ref, hb_ref, dyb_ref, dng_ref):
        i = pl.program_id(0)
        dxo = dxo_ref[...]
        dyb = (0.5 * dxo).astype(bf16)
        dyb_ref[...] = dyb
        dh = None
        for s in range(ns):
            dact = _dot_nt(dyb, wd_ref[s])
            gg = g_ref[s].astype(f32)
            uu = u_ref[s].astype(f32)
            sg = _sigmoid(gg)
            sil = gg * sg
            dgb = (dact * uu * _dsilu(gg, sg)).astype(bf16)
            dub = (dact * sil).astype(bf16)
            dg_ref[s] = dgb
            du_ref[s] = dub
            act_ref[s] = (sil * uu).astype(bf16)
            part = _dot(dgb, wg_ref[s]) + _dot(dub, wu_ref[s])
            dh = part if dh is None else dh + part
        y, xh, r = _rms(x_ref[...], ng_ref[...])
        hb_ref[...] = y.astype(bf16)
        dx_ref[...] = dxo + _rms_bwd(dh, xh, r, ng_ref[...])
        _acc_out(dng_ref, i == 0, jnp.sum(dh * xh, axis=0, keepdims=True))

    row = pl.BlockSpec((tm, d), lambda i: (i, 0))
    shard = pl.BlockSpec((ns, tm, fs), lambda i: (0, i, 0))
    wspec = pl.BlockSpec((ns, fs, d), lambda i: (0, 0, 0), pipeline_mode=pl.Buffered(1))
    vec = pl.BlockSpec((1, d), lambda i: (0, 0))
    return pl.pallas_call(
        body, name=name, grid=(t // tm,),
        in_specs=[row, row, vec, shard, shard, wspec, wspec, wspec, _ANY],
        out_specs=[row, shard, shard, shard, row, row, vec],
        out_shape=[jax.ShapeDtypeStruct((t, d), f32)] + [jax.ShapeDtypeStruct((ns, t, fs), bf16)] * 3
        + [jax.ShapeDtypeStruct((t, d), bf16)] * 2 + [jax.ShapeDtypeStruct((1, d), f32)],
        compiler_params=_cparams(VMEM_LIMIT_WIDE),
    )(dxo, x, ng, g, u, wg, wu, wd, dep)


def tn_matmul(xs, ys, x_spec, y_specs, n_shards, k1, k2s, t, tm, name, dep):
    k2 = sum(k2s)
    ny = len(ys)

    def body(*refs):
        x_ref = refs[0]
        y_refs = refs[1:1 + ny]
        o_ref = refs[2 + ny]
        acc = refs[3 + ny]
        i = pl.program_id(1)
        xb = x_ref[0] if len(x_ref.shape) == 3 else x_ref[...]
        off = 0
        for y_ref, w in zip(y_refs, k2s):
            yb = y_ref[0] if len(y_ref.shape) == 3 else y_ref[...]
            part = _dot_tn(xb, yb)
            sl = (slice(None), slice(off, off + w))

            @pl.when(i == 0)
            def _(part=part, sl=sl):
                acc[sl] = part

            @pl.when(i > 0)
            def _(part=part, sl=sl):
                acc[sl] += part

            off += w

        @pl.when(i == t // tm - 1)
        def _():
            o_ref[0] = acc[...].astype(bf16)

    return pl.pallas_call(
        body, name=name, grid=(n_shards, t // tm),
        in_specs=[x_spec] + list(y_specs) + [_ANY],
        out_specs=pl.BlockSpec((1, k1, k2), lambda s, i: (s, 0, 0)),
        out_shape=jax.ShapeDtypeStruct((n_shards, k1, k2), bf16),
        scratch_shapes=[pltpu.VMEM((k1, k2), f32)],
        compiler_params=_cparams(),
    )(xs, *ys, dep)


def _pair_shards(w):
    s4, fs, d = w.shape
    return w.reshape(s4 // 2, 2 * fs, d)


def ffn_weight_grads(hb, dyb, dg, du, act, name, dep, each=None):
    t, d = hb.shape
    s2, _, fs2 = dg.shape
    tm = t
    row = pl.BlockSpec((tm, d), lambda s, i: (i, 0))
    shard = pl.BlockSpec((1, tm, fs2), lambda s, i: (s, i, 0))
    grads = []
    for xa, ya, which in ((dg, hb, "w_gate"), (du, hb, "w_up"), (act, dyb, "w_down")):
        g = tn_matmul(xa, [ya], shard, [row], s2, fs2, [d], t, tm, name + "_" + which, dep)
        g = g.reshape(2 * s2, fs2 // 2, d)
        if each is not None:
            dep = each(which, g)
        grads.append(g)
    return grads


def inproj_fwd(x1, ng, win, bin4, cos, sin):
    t, d = x1.shape
    s4, _, w2 = win.shape
    tm = _row_tile(t)
    dk = d // RET_HEADS
    scale = dk ** -0.5

    def body(x_ref, ng_ref, w_ref, b_ref, cos_ref, sin_ref, p_ref, hb_ref):
        y, _, _ = _rms(x_ref[...], ng_ref[...])
        h = y.astype(bf16)
        hb_ref[...] = h
        for s in range(s4):
            p = _dot(h, w_ref[s]) + b_ref[s]
            if s != 1:
                p_ref[s] = p.astype(bf16)
            else:
                cs, sn = cos_ref[...], sin_ref[...]
                for e in range(2 * RET_HEADS):
                    cols = slice(e * dk, (e + 1) * dk)
                    rot = _rot(p[:, cols], cs, sn)
                    p_ref[s, :, cols] = (rot if e < RET_HEADS else rot * scale).astype(bf16)

    tab = pl.BlockSpec((tm, dk // 2), lambda i: (i, 0))
    return pl.pallas_call(
        body, name="inproj_fwd", grid=(t // tm,),
        in_specs=[pl.BlockSpec((tm, d), lambda i: (i, 0)), pl.BlockSpec((1, d), lambda i: (0, 0)),
                  pl.BlockSpec((s4, d, w2), lambda i: (0, 0, 0), pipeline_mode=pl.Buffered(1)),
                  pl.BlockSpec((s4, 1, w2), lambda i: (0, 0, 0)), tab, tab],
        out_specs=[pl.BlockSpec((s4, tm, w2), lambda i: (0, i, 0)), pl.BlockSpec((tm, d), lambda i: (i, 0))],
        out_shape=[jax.ShapeDtypeStruct((s4, t, w2), bf16), jax.ShapeDtypeStruct((t, d), bf16)],
        compiler_params=_cparams(),
    )(x1, ng, win, bin4, cos, sin)


def _sgu_norm(va, ng, nb):
    gv = _gelu(va)
    mu = jnp.mean(gv, axis=-1, keepdims=True)
    xc = gv - mu
    rstd = lax.rsqrt(jnp.mean(xc * xc, axis=-1, keepdims=True) + NORM_EPS)
    xh = xc * rstd
    return xh, rstd, (xh * ng + nb).astype(bf16)


def sgu_fwd(proj, ng, nb, ws, bs):
    _, t, w2 = proj.shape
    d = w2 // 2
    gd = d // SGU_GROUPS
    tm = _row_tile(t)

    def body(p_ref, ng_ref, nb_ref, ws_ref, bs_ref, a_ref):
        ua = p_ref[0, :, 0:d].astype(f32)
        va = p_ref[0, :, d:w2].astype(f32)
        gu = _gelu(ua)
        _, _, vn = _sgu_norm(va, ng_ref[...], nb_ref[...])
        for c in range(tm // SGU_CHUNK):
            rows = slice(c * SGU_CHUNK, (c + 1) * SGU_CHUNK)
            for g in range(SGU_GROUPS):
                cols = slice(g * gd, (g + 1) * gd)
                sg = _dot(ws_ref[g], vn[rows, cols]) + bs_ref[g]
                a_ref[rows, cols] = (gu[rows, cols] * sg).astype(bf16)

    return pl.pallas_call(
        body, name="sgu_fwd", grid=(t // tm,),
        in_specs=[pl.BlockSpec((1, tm, w2), lambda i: (0, i, 0)), pl.BlockSpec((1, d), lambda i: (0, 0)),
                  pl.BlockSpec((1, d), lambda i: (0, 0)), pl.BlockSpec((SGU_GROUPS, SGU_CHUNK, SGU_CHUNK), lambda i: (0, 0, 0)),
                  pl.BlockSpec((SGU_GROUPS, SGU_CHUNK, 1), lambda i: (0, 0, 0))],
        out_specs=pl.BlockSpec((tm, d), lambda i: (i, 0)),
        out_shape=jax.ShapeDtypeStruct((t, d), bf16),
        compiler_params=_cparams(),
    )(proj, ng, nb, ws, bs)


def sgu_bwd(da, proj, ng, nb, ws, bs, dep):
    _, t, w2 = proj.shape
    d = w2 // 2
    gd = d // SGU_GROUPS
    tm = _row_tile(t)

    def body(da_ref, p_ref, ng_ref, nb_ref, ws_ref, bs_ref, dep_ref,
             dua_ref, dva_ref, dws_ref, dbs_ref, dng_ref, dnb_ref, dvn_scr):
        i = pl.program_id(0)
        ua = p_ref[0, :, 0:d].astype(f32)
        va = p_ref[0, :, d:w2].astype(f32)
        gu = _gelu(ua)
        xh, rstd, vn = _sgu_norm(va, ng_ref[...], nb_ref[...])
        dad = da_ref[...].astype(f32)
        dsb = (dad * gu).astype(bf16)
        for c in range(tm // SGU_CHUNK):
            rows = slice(c * SGU_CHUNK, (c + 1) * SGU_CHUNK)
            for g in range(SGU_GROUPS):
                cols = slice(g * gd, (g + 1) * gd)
                sg = _dot(ws_ref[g], vn[rows, cols]) + bs_ref[g]
                dua_ref[rows, cols] = (dad[rows, cols] * sg * _dgelu(ua[rows, cols])).astype(bf16)
                ds = dsb[rows, cols]
                dvn_scr[rows, cols] = _dot_tn(ws_ref[g], ds)
                dw = _dot_nt(ds, vn[rows, cols])
                db = jnp.sum(ds.astype(f32), axis=1, keepdims=True)
                if c == 0:
                    _acc_out(dws_ref.at[g], i == 0, dw)
                    _acc_out(dbs_ref.at[g], i == 0, db)
                else:
                    dws_ref[g] += dw
                    dbs_ref[g] += db
        dvn = dvn_scr[...]
        _acc_out(dng_ref, i == 0, jnp.sum(dvn * xh, axis=0, keepdims=True))
        _acc_out(dnb_ref, i == 0, jnp.sum(dvn, axis=0, keepdims=True))
        dxh = dvn * ng_ref[...]
        dgv = rstd * (dxh - jnp.mean(dxh, axis=-1, keepdims=True) - xh * jnp.mean(dxh * xh, axis=-1, keepdims=True))
        dva_ref[...] = (dgv * _dgelu(va)).astype(bf16)

    row = pl.BlockSpec((tm, d), lambda i: (i, 0))
    vec = pl.BlockSpec((1, d), lambda i: (0, 0))
    wsp = pl.BlockSpec((SGU_GROUPS, SGU_CHUNK, SGU_CHUNK), lambda i: (0, 0, 0))
    bsp = pl.BlockSpec((SGU_GROUPS, SGU_CHUNK, 1), lambda i: (0, 0, 0))
    return pl.pallas_call(
        body, name="sgu_bwd", grid=(t // tm,),
        in_specs=[row, pl.BlockSpec((1, tm, w2), lambda i: (0, i, 0)), vec, vec, wsp, bsp, _ANY],
        out_specs=[row, row, wsp, bsp, vec, vec],
        out_shape=[jax.ShapeDtypeStruct((t, d), bf16), jax.ShapeDtypeStruct((t, d), bf16),
                   jax.ShapeDtypeStruct((SGU_GROUPS, SGU_CHUNK, SGU_CHUNK), f32), jax.ShapeDtypeStruct((SGU_GROUPS, SGU_CHUNK, 1), f32),
                   jax.ShapeDtypeStruct((1, d), f32), jax.ShapeDtypeStruct((1, d), f32)],
        scratch_shapes=[pltpu.VMEM((tm, d), f32)],
        compiler_params=_cparams(),
    )(da, proj, ng, nb, ws, bs, dep)


def retention_constants(decay_logit, t, dk):
    lg = jax.nn.log_sigmoid(decay_logit.astype(f32))
    lgf = lg[0][:, None]
    lgb = lg[1][:, None]
    idx = jnp.arange(CHUNK, dtype=f32)[None, :]
    af = jnp.exp((idx + 1.0) * lgf)
    ab = jnp.exp((CHUNK - idx) * lgb)
    kf = jnp.exp((CHUNK - 1.0 - idx) * lgf)
    kb = jnp.exp(idx * lgb)
    cols = jnp.stack([af, ab, kf, kb, af * (idx + 1.0), ab * (CHUNK - idx), kf * (CHUNK - 1.0 - idx), kb * idx], axis=1)
    cols = cols[..., None]
    diff = idx[0][:, None] - idx[0][None, :]
    dfm = jnp.where(diff >= 0, jnp.exp(jnp.maximum(diff, 0.0)[None] * lgf[:, :, None]), 0.0)
    dbm = jnp.where(diff < 0, jnp.exp(jnp.maximum(-diff, 0.0)[None] * lgb[:, :, None]), 0.0)
    mats = jnp.stack([dfm + dbm, dfm * diff[None], dbm * (-diff)[None]], axis=1)
    cdec = jnp.stack([jnp.broadcast_to(jnp.exp(CHUNK * lgf), (RET_HEADS, dk)),
                      jnp.broadcast_to(jnp.exp(CHUNK * lgb), (RET_HEADS, dk))], axis=1)
    theta = ROPE_BASE ** (-jnp.arange(0, dk, 2, dtype=f32) / dk)
    ang = jnp.arange(t, dtype=f32)[:, None] * theta[None, :]
    return cols, mats, cdec, jnp.cos(ang), jnp.sin(ang)


def _rot(tr, cos, sin):
    half = tr.shape[-1] // 2
    t1 = tr[:, :half]
    t2 = tr[:, half:]
    return jnp.concatenate([t1 * cos - t2 * sin, t2 * cos + t1 * sin], axis=-1)


def _rot_inv(dt, cos, sin):
    half = dt.shape[-1] // 2
    d1 = dt[:, :half]
    d2 = dt[:, half:]
    return jnp.concatenate([d1 * cos + d2 * sin, d2 * cos - d1 * sin], axis=-1)


def _ret_specs(t, d, dk, rt):
    nr = t // rt
    hq = d // dk

    def blk(p, n):
        return (1 - p) * (nr - 1 - n) + p * n

    q_spec = pl.BlockSpec((1, rt, dk), lambda h, p, n: (1, blk(p, n), h))
    k_spec = pl.BlockSpec((1, rt, dk), lambda h, p, n: (1, blk(p, n), hq + h))
    v_spec = pl.BlockSpec((1, rt, dk), lambda h, p, n: (2, blk(p, n), h))
    g_spec = pl.BlockSpec((1, rt, dk), lambda h, p, n: (2, blk(p, n), hq + h))
    tab_spec = pl.BlockSpec((rt, dk // 2), lambda h, p, n: (blk(p, n), 0))
    cols_spec = pl.BlockSpec((1, 8, CHUNK, 1), lambda h, p, n: (h, 0, 0, 0))
    mats_spec = pl.BlockSpec((1, 3, CHUNK, CHUNK), lambda h, p, n: (h, 0, 0, 0))
    cdec_spec = pl.BlockSpec((1, 2, dk), lambda h, p, n: (h, 0, 0))
    in_row = pl.BlockSpec((rt, dk), lambda h, p, n: (blk(p, n), h))
    out_row = pl.BlockSpec((rt, dk), lambda h, p, n: (p * n, h))
    return nr, blk, q_spec, k_spec, v_spec, g_spec, tab_spec, cols_spec, mats_spec, cdec_spec, in_row, out_row


def ret_fwd(proj, cols, mats, cdec):
    _, t, w2 = proj.shape
    d = w2 // 2
    dk = d // RET_HEADS
    rt = _row_tile(t)
    cpt = rt // CHUNK
    nr, blk, q_spec, k_spec, v_spec, g_spec, _, cols_spec, mats_spec, cdec_spec, _, out_row = _ret_specs(t, d, dk, rt)

    def body(q_ref, k_ref, v_ref, g_ref, cols_ref, mats_ref, cdec_ref, r_ref, rn_ref, sb_scr, st):
        p = pl.program_id(1)
        n = pl.program_id(2)
        af, ab, kf, kb = cols_ref[0, 0], cols_ref[0, 1], cols_ref[0, 2], cols_ref[0, 3]
        cf = cdec_ref[0, 0:1, :]
        cb = cdec_ref[0, 1:2, :]

        @pl.when(n == 0)
        def _():
            st[...] = jnp.zeros_like(st)

        @pl.when(p == 0)
        def _():
            for j in reversed(range(cpt)):
                rows = slice(j * CHUNK, (j + 1) * CHUNK)
                ch = blk(p, n) * cpt + j
                kk = k_ref[0, rows, :].astype(f32)
                sb_scr[ch] = st[...].astype(bf16)
                st[...] = st[...] * cb + _dot_tn((kk * kb).astype(bf16), v_ref[0, rows, :])

        @pl.when(p == 1)
        def _():
            for j in range(cpt):
                rows = slice(j * CHUNK, (j + 1) * CHUNK)
                ch = blk(p, n) * cpt + j
                qb = q_ref[0, rows, :]
                kkb = k_ref[0, rows, :]
                q = qb.astype(f32)
                kk = kkb.astype(f32)
                v = v_ref[0, rows, :]
                pm = (_dot_nt(qb, kkb) * mats_ref[0, 0]).astype(bf16)
                out = (_dot(pm, v) + _dot((q * af).astype(bf16), st[...].astype(bf16))
                       + _dot((q * ab).astype(bf16), sb_scr[ch]))
                st[...] = st[...] * cf + _dot_tn((kk * kf).astype(bf16), v)
                rhat = out * lax.rsqrt(jnp.mean(out * out, axis=-1, keepdims=True) + NORM_EPS)
                gg = g_ref[0, rows, :].astype(f32)
                r_ref[rows, :] = out.astype(bf16)
                rn_ref[rows, :] = (rhat * gg * _sigmoid(gg)).astype(bf16)

    return pl.pallas_call(
        body, name="ret_fwd", grid=(RET_HEADS, 2, nr),
        in_specs=[q_spec, k_spec, v_spec, g_spec, cols_spec, mats_spec, cdec_spec],
        out_specs=[out_row, out_row],
        out_shape=[jax.ShapeDtypeStruct((t, d), bf16), jax.ShapeDtypeStruct((t, d), bf16)],
        scratch_shapes=[pltpu.VMEM((t // CHUNK, dk, dk), bf16), pltpu.VMEM((dk, dk), f32)],
        compiler_params=_cparams(),
    )(proj, proj, proj, proj, cols, mats, cdec)


def ret_bwd(drn, r, proj, cols, mats, cdec, cos, sin):
    _, t, w2 = proj.shape
    d = w2 // 2
    dk = d // RET_HEADS
    rt = _row_tile(t)
    cpt = rt // CHUNK
    nr, blk, q_spec, k_spec, v_spec, g_spec, tab_spec, cols_spec, mats_spec, cdec_spec, in_row, out_row = _ret_specs(t, d, dk, rt)
    scale = dk ** -0.5

    def body(drn_ref, r_ref, q_ref, k_ref, v_ref, g_ref, cos_ref, sin_ref, cols_ref, mats_ref, cdec_ref,
             dq_ref, dk_ref, dv_ref, dg_ref, dlg_ref,
             sb_scr, gf_scr, st_s, st_g, acc_af, acc_ab, acc_vf, acc_vb, acc_sf, acc_sb, dout_scr, dgr_scr):
        p = pl.program_id(1)
        n = pl.program_id(2)
        af, ab, kf, kb = cols_ref[0, 0], cols_ref[0, 1], cols_ref[0, 2], cols_ref[0, 3]
        af1, ab1, kf1, kb1 = cols_ref[0, 4], cols_ref[0, 5], cols_ref[0, 6], cols_ref[0, 7]
        cf = cdec_ref[0, 0:1, :]
        cb = cdec_ref[0, 1:2, :]

        @pl.when(n == 0)
        def _():
            st_s[...] = jnp.zeros_like(st_s)
            st_g[...] = jnp.zeros_like(st_g)

        @pl.when(jnp.logical_and(n == 0, p == 1))
        def _():
            for a in (acc_af, acc_ab, acc_vf, acc_vb, acc_sf, acc_sb):
                a[...] = jnp.zeros_like(a)

        def load(rows):
            cs, sn = cos_ref[rows, :], sin_ref[rows, :]
            q = q_ref[0, rows, :].astype(f32)
            kk = k_ref[0, rows, :].astype(f32)
            rr = r_ref[rows, :].astype(f32)
            rstd = lax.rsqrt(jnp.mean(rr * rr, axis=-1, keepdims=True) + NORM_EPS)
            rhat = rr * rstd
            gg = g_ref[0, rows, :].astype(f32)
            sg = _sigmoid(gg)
            dd = drn_ref[rows, :].astype(f32)
            drhat = dd * gg * sg
            dout = rstd * (drhat - rhat * jnp.mean(drhat * rhat, axis=-1, keepdims=True))
            dgr = dd * rhat * _dsilu(gg, sg)
            return q, kk, dout.astype(bf16), dgr, cs, sn

        @pl.when(p == 0)
        def _():
            for j in reversed(range(cpt)):
                rows = slice(j * CHUNK, (j + 1) * CHUNK)
                ch = blk(p, n) * cpt + j
                q, kk, doutb, dgr, _, _ = load(rows)
                kept = pl.ds(pl.multiple_of(ch * CHUNK, CHUNK), CHUNK)
                dout_scr[kept, :] = doutb
                dgr_scr[kept, :] = dgr.astype(bf16)
                sb_scr[ch] = st_s[...].astype(bf16)
                gf_scr[ch] = st_g[...].astype(bf16)
                st_s[...] = st_s[...] * cb + _dot_tn((kk * kb).astype(bf16), v_ref[0, rows, :])
                st_g[...] = st_g[...] * cf + _dot_tn((q * af).astype(bf16), doutb)

        @pl.when(p == 1)
        def _():
            for j in range(cpt):
                rows = slice(j * CHUNK, (j + 1) * CHUNK)
                ch = blk(p, n) * cpt + j
                kept = pl.ds(pl.multiple_of(ch * CHUNK, CHUNK), CHUNK)
                doutb = dout_scr[kept, :]
                cs, sn = cos_ref[rows, :], sin_ref[rows, :]
                v = v_ref[0, rows, :]
                qb = q_ref[0, rows, :]
                kkb = k_ref[0, rows, :]
                q = qb.astype(f32)
                kk = kkb.astype(f32)
                sf = st_s[...]
                gb = st_g[...]
                sfb = sf.astype(bf16)
                gbb = gb.astype(bf16)
                sbb = sb_scr[ch]
                gfb = gf_scr[ch]
                dmat = mats_ref[0, 0]
                scores = _dot_nt(qb, kkb)
                dpraw = _dot_nt(doutb, v)
                dpb = (dpraw * dmat).astype(bf16)
                pmb = (scores * dmat).astype(bf16)
                x1 = _dot_nt(doutb, sfb)
                x2 = _dot_nt(doutb, sbb)
                y1 = _dot_nt(v, gfb)
                y2 = _dot_nt(v, gbb)
                kdf = (kk * kf).astype(bf16)
                kdb = (kk * kb).astype(bf16)
                dq = _dot(dpb, kkb) + x1 * af + x2 * ab
                dkk = _dot_tn(dpb, qb) + y1 * kf + y2 * kb
                dv = _dot_tn(pmb, doutb) + _dot(kdf, gfb) + _dot(kdb, gbb)
                ps = dpraw * scores
                acc_af[...] += ps * mats_ref[0, 1]
                acc_ab[...] += ps * mats_ref[0, 2]
                acc_vf[...] += x1 * q * af1 + y1 * kk * kf1
                acc_vb[...] += x2 * q * ab1 + y2 * kk * kb1
                acc_sf[...] += gfb.astype(f32) * sf
                acc_sb[...] += gb * sbb.astype(f32)
                st_s[...] = sf * cf + _dot_tn(kdf, v)
                st_g[...] = gb * cb + _dot_tn((q * ab).astype(bf16), doutb)
                dq_ref[rows, :] = _rot_inv(dq, cs, sn).astype(bf16)
                dk_ref[rows, :] = (_rot_inv(dkk, cs, sn) * scale).astype(bf16)
                dv_ref[rows, :] = dv.astype(bf16)
                dg_ref[rows, :] = dgr_scr[kept, :]

        @pl.when(jnp.logical_and(p == 1, n == nr - 1))
        def _():
            tf = jnp.sum(acc_af[...]) + jnp.sum(acc_vf[...]) + CHUNK * jnp.sum(acc_sf[...] * cf)
            tb = jnp.sum(acc_ab[...]) + jnp.sum(acc_vb[...]) + CHUNK * jnp.sum(acc_sb[...] * cb)
            rid = lax.broadcasted_iota(jnp.int32, (8, 128), 0)
            dlg_ref[0] = jnp.where(rid == 0, tf, jnp.where(rid == 1, tb, 0.0))

    nch = t // CHUNK
    return pl.pallas_call(
        body, name="ret_bwd", grid=(RET_HEADS, 2, nr),
        in_specs=[in_row, in_row, q_spec, k_spec, v_spec, g_spec, tab_spec, tab_spec, cols_spec, mats_spec, cdec_spec],
        out_specs=[out_row, out_row, out_row, out_row, pl.BlockSpec((1, 8, 128), lambda h, p, n: (h, 0, 0))],
        out_shape=[jax.ShapeDtypeStruct((t, d), bf16)] * 4 + [jax.ShapeDtypeStruct((RET_HEADS, 8, 128), f32)],
        scratch_shapes=[pltpu.VMEM((nch, dk, dk), bf16), pltpu.VMEM((nch, dk, dk), bf16),
                        pltpu.VMEM((dk, dk), f32), pltpu.VMEM((dk, dk), f32),
                        pltpu.VMEM((CHUNK, CHUNK), f32), pltpu.VMEM((CHUNK, CHUNK), f32),
                        pltpu.VMEM((CHUNK, dk), f32), pltpu.VMEM((CHUNK, dk), f32),
                        pltpu.VMEM((dk, dk), f32), pltpu.VMEM((dk, dk), f32),
                        pltpu.VMEM((t, dk), bf16), pltpu.VMEM((t, dk), bf16)],
        compiler_params=_cparams(),
    )(drn, r, proj, proj, proj, proj, cos, sin, cols, mats, cdec)


def mix_fwd(a, rn, proj, wa, wb, wo, x1):
    t, d = x1.shape
    tm = _row_tile(t)

    def body(a_ref, rn_ref, p_ref, wa_ref, wb_ref, wo_ref, x_ref, xo_ref, ba_ref, br_ref):
        ba = _dot(a_ref[...], wa_ref[...])
        br = _dot(rn_ref[...], wb_ref[...])
        sa = _sigmoid(p_ref[0, :, 0:d].astype(f32))
        sb = _sigmoid(p_ref[0, :, d:2 * d].astype(f32))
        mix = (sa * ba + sb * br).astype(bf16)
        xo_ref[...] = x_ref[...] + _dot(mix, wo_ref[...])
        ba_ref[...] = ba.astype(bf16)
        br_ref[...] = br.astype(bf16)

    row = pl.BlockSpec((tm, d), lambda i: (i, 0))
    wsp = pl.BlockSpec((d, d), lambda i: (0, 0))
    return pl.pallas_call(
        body, name="mix_fwd", grid=(t // tm,),
        in_specs=[row, row, pl.BlockSpec((1, tm, 2 * d), lambda i: (3, i, 0)), wsp, wsp, wsp, row],
        out_specs=[row, row, row],
        out_shape=[jax.ShapeDtypeStruct((t, d), f32), jax.ShapeDtypeStruct((t, d), bf16), jax.ShapeDtypeStruct((t, d), bf16)],
        compiler_params=_cparams(),
    )(a, rn, proj, wa, wb, wo, x1)


def mix_bwd_act(dx2, ba, br, proj, wa, wb, wo, dep):
    t, d = dx2.shape
    tm = _row_tile(t)

    def body(dx_ref, ba_ref, br_ref, p_ref, wa_ref, wb_ref, wo_ref, dep_ref,
             da_ref, drn_ref, dga_ref, dgb_ref, mix_ref, dba_ref, dbr_ref, dxb_ref):
        dxb = dx_ref[...].astype(bf16)
        dxb_ref[...] = dxb
        dmix = _dot_nt(dxb, wo_ref[...])
        ba = ba_ref[...].astype(f32)
        br = br_ref[...].astype(f32)
        sa = _sigmoid(p_ref[0, :, 0:d].astype(f32))
        sb = _sigmoid(p_ref[0, :, d:2 * d].astype(f32))
        mix_ref[...] = (sa * ba + sb * br).astype(bf16)
        dba = (dmix * sa).astype(bf16)
        dbr = (dmix * sb).astype(bf16)
        dba_ref[...] = dba
        dbr_ref[...] = dbr
        dga_ref[...] = (dmix * ba * sa * (1.0 - sa)).astype(bf16)
        dgb_ref[...] = (dmix * br * sb * (1.0 - sb)).astype(bf16)
        da_ref[...] = _dot_nt(dba, wa_ref[...]).astype(bf16)
        drn_ref[...] = _dot_nt(dbr, wb_ref[...]).astype(bf16)

    row = pl.BlockSpec((tm, d), lambda i: (i, 0))
    wsp = pl.BlockSpec((d, d), lambda i: (0, 0))
    return pl.pallas_call(
        body, name="mix_bwd_act", grid=(t // tm,),
        in_specs=[row, row, row, pl.BlockSpec((1, tm, 2 * d), lambda i: (3, i, 0)), wsp, wsp, wsp, _ANY],
        out_specs=[row] * 8,
        out_shape=[jax.ShapeDtypeStruct((t, d), bf16)] * 8,
        compiler_params=_cparams(),
    )(dx2, ba, br, proj, wa, wb, wo, dep)


def inproj_bwd_act(segs, win, x1, ng, dx2):
    t, d = x1.shape
    s4 = win.shape[0]
    tm = _row_tile(t)
    nseg = len(segs)

    def body(*refs):
        seg_refs = refs[:nseg]
        w_ref, x_ref, ng_ref, dx2_ref, dx1_ref, db_ref, dng_ref = refs[nseg:]
        i = pl.program_id(0)
        dh = None
        for e, sr in enumerate(seg_refs):
            sb = sr[...]
            part = _dot_nt(sb, w_ref[e // 2, :, (e % 2) * d:(e % 2 + 1) * d])
            dh = part if dh is None else dh + part
            _acc_out(db_ref.at[e], i == 0, jnp.sum(sb.astype(f32), axis=0, keepdims=True))
        _, xh, r = _rms(x_ref[...], ng_ref[...])
        dx1_ref[...] = dx2_ref[...] + _rms_bwd(dh, xh, r, ng_ref[...])
        _acc_out(dng_ref, i == 0, jnp.sum(dh * xh, axis=0, keepdims=True))

    row = pl.BlockSpec((tm, d), lambda i: (i, 0))
    vec = pl.BlockSpec((1, d), lambda i: (0, 0))
    return pl.pallas_call(
        body, name="inproj_bwd_act", grid=(t // tm,),
        in_specs=[row] * nseg + [pl.BlockSpec((s4, d, 2 * d), lambda i: (0, 0, 0), pipeline_mode=pl.Buffered(1)),
                                 row, vec, row],
        out_specs=[row, pl.BlockSpec((nseg, 1, d), lambda i: (0, 0, 0)), vec],
        out_shape=[jax.ShapeDtypeStruct((t, d), f32), jax.ShapeDtypeStruct((nseg, 1, d), f32),
                   jax.ShapeDtypeStruct((1, d), f32)],
        compiler_params=_cparams(VMEM_LIMIT_WIDE),
    )(*segs, win, x1, ng, dx2)


def loss_head(x3, fng, tgt):
    t, d = x3.shape
    tm = _row_tile(t)

    def body(x_ref, g_ref, t_ref, loss_ref, dx_ref, dg_ref):
        i = pl.program_id(0)
        y, xh, r = _rms(x_ref[...], g_ref[...])
        diff = y - t_ref[...]
        part = 0.5 * jnp.sum(jnp.sum(diff * diff, axis=0, keepdims=True), axis=1, keepdims=True) / d
        _acc_out(loss_ref, i == 0, jnp.broadcast_to(part, (1, 128)))
        dy = diff * (1.0 / d)
        dx_ref[...] = _rms_bwd(dy, xh, r, g_ref[...])
        _acc_out(dg_ref, i == 0, jnp.sum(dy * xh, axis=0, keepdims=True))

    row = pl.BlockSpec((tm, d), lambda i: (i, 0))
    vec = pl.BlockSpec((1, d), lambda i: (0, 0))
    return pl.pallas_call(
        body, name="loss_head", grid=(t // tm,),
        in_specs=[row, vec, row],
        out_specs=[pl.BlockSpec((1, 128), lambda i: (0, 0)), row, vec],
        out_shape=[jax.ShapeDtypeStruct((1, 128), f32), jax.ShapeDtypeStruct((t, d), f32), jax.ShapeDtypeStruct((1, d), f32)],
        compiler_params=_cparams(),
    )(x3, fng, tgt)


def _place():
    return lax.axis_index("x"), lax.axis_index("y"), lax.axis_index("c")


def _other_chips(x, y):
    return [(1 - x, y), (x, 1 - y), (1 - x, 1 - y)]


_ANY = pl.BlockSpec(memory_space=pl.ANY)


_HBM = pl.BlockSpec(memory_space=pltpu.HBM)
_SEM = pl.BlockSpec(memory_space=pltpu.SEMAPHORE)
_EFFECT = pltpu.SideEffectType.DATAFLOW_SIDE_EFFECTING


def _hbm(a):
    return pltpu.with_memory_space_constraint(a, pltpu.HBM)


def _half_rows(ref, c):
    half = ref.shape[1] // 2
    return pl.ds(pl.multiple_of(c * half, 16), half)


def _chip_copy(src, dst, send_sem, recv_sem, chip, c):
    return pltpu.make_async_remote_copy(src_ref=src, dst_ref=dst, send_sem=send_sem, recv_sem=recv_sem,
                                        device_id=(chip[0], chip[1], c), device_id_type=MESH)


def gather_start(bufs, groups, name):
    nb, ng = len(bufs), len(groups)

    def body(*refs):
        ins = refs[:nb]
        sems = refs[nb:nb + 2 * ng]
        token = refs[-1]
        x, y, c = _place()
        k = 2 * x + y
        for gi, grp in enumerate(groups):
            for wi, w in enumerate(grp):
                mine = ins[w].at[k, _half_rows(ins[w], c)]
                for j, chip in enumerate(_other_chips(x, y)):
                    _chip_copy(mine, mine, sems[2 * gi].at[3 * wi + j], sems[2 * gi + 1].at[3 * wi + j], chip, c).start()
        token[...] = jnp.zeros_like(token)

    sem_shapes = []
    for grp in groups:
        sem_shapes += [pltpu.SemaphoreType.DMA((3 * len(grp),)), pltpu.SemaphoreType.DMA((3 * len(grp),))]
    outs = pl.pallas_call(
        body, name=name,
        out_shape=sem_shapes + [pltpu.HBM(b.shape, b.dtype) for b in bufs] + [jax.ShapeDtypeStruct((8, 128), f32)],
        in_specs=[_HBM] * nb,
        out_specs=[_SEM] * (2 * ng) + [_HBM] * nb + [pl.BlockSpec(memory_space=pltpu.VMEM)],
        input_output_aliases={w: 2 * ng + w for w in range(nb)},
        compiler_params=pltpu.CompilerParams(has_side_effects=_EFFECT),
    )(*[_hbm(b) for b in bufs])
    sems = [(outs[2 * gi], outs[2 * gi + 1]) for gi in range(ng)]
    return sems, list(outs[2 * ng:2 * ng + nb]), outs[-1]


def gather_wait(bufs, sems, after, name):
    n = len(bufs)

    def body(*refs):
        ins = refs[:n]
        send_sems, recv_sems = refs[n], refs[n + 1]
        x, y, c = _place()
        k = 2 * x + y
        for wi in range(n):
            half = _half_rows(ins[wi], c)
            for j, chip in enumerate(_other_chips(x, y)):
                cp = _chip_copy(ins[wi].at[k, half], ins[wi].at[2 * chip[0] + chip[1], half], send_sems.at[3 * wi + j],
                                recv_sems.at[3 * wi + j], chip, c)
                cp.wait_send()
                cp.wait_recv()

    outs = pl.pallas_call(
        body, name=name,
        out_shape=[pltpu.HBM(b.shape, b.dtype) for b in bufs],
        in_specs=[_HBM] * n + [_SEM, _SEM, _ANY],
        out_specs=[_HBM] * n,
        input_output_aliases={i: i for i in range(n)},
        compiler_params=pltpu.CompilerParams(has_side_effects=_EFFECT),
    )(*bufs, sems[0], sems[1], after)
    return list(outs)


def gather_forward(bufs, name):
    n = len(bufs)

    def body(*refs):
        ins = refs[n:2 * n]
        send_sems, recv_sems = refs[2 * n], refs[2 * n + 1]
        x, y, c = _place()
        copies = []
        for wi in range(n):
            for j, chip in enumerate(_other_chips(x, y)):
                kp = 2 * chip[0] + chip[1]
                got = ins[wi].at[kp, _half_rows(ins[wi], c)]
                cp = pltpu.make_async_remote_copy(
                    src_ref=got, dst_ref=got, send_sem=send_sems.at[3 * wi + j], recv_sem=recv_sems.at[3 * wi + j],
                    device_id=(x, y, 1 - c), device_id_type=MESH)
                cp.start()
                copies.append((cp, wi, kp, j))
        for cp, wi, kp, j in copies:
            cp.wait_send()
            theirs = ins[wi].at[kp, _half_rows(ins[wi], 1 - c)]
            pltpu.make_async_remote_copy(
                src_ref=theirs, dst_ref=theirs, send_sem=send_sems.at[3 * wi + j], recv_sem=recv_sems.at[3 * wi + j],
                device_id=(x, y, 1 - c), device_id_type=MESH).wait_recv()

    outs = pl.pallas_call(
        body, name=name,
        out_shape=[jax.ShapeDtypeStruct(b.shape, b.dtype) for b in bufs],
        in_specs=[_ANY] * n, out_specs=[_ANY] * n,
        input_output_aliases={i: i for i in range(n)},
        scratch_shapes=[pltpu.SemaphoreType.DMA((3 * n,)), pltpu.SemaphoreType.DMA((3 * n,))],
    )(*bufs)
    return list(outs)


def exchange_start(grads, name):
    n = len(grads)
    lands = [lax.empty((3,) + g.shape[1:], g.dtype) for g in grads]

    def body(*refs):
        ins = refs[:n]
        land = refs[n:2 * n]
        send_sems, recv_sems = refs[2 * n], refs[2 * n + 1]
        token = refs[-1]
        x, y, c = _place()
        for wi in range(n):
            for j, chip in enumerate(_other_chips(x, y)):
                _chip_copy(ins[wi].at[2 * chip[0] + chip[1]], land[wi].at[j], send_sems.at[3 * wi + j],
                           recv_sems.at[3 * wi + j], chip, c).start()
        token[...] = jnp.zeros_like(token)

    outs = pl.pallas_call(
        body, name=name,
        out_shape=[pltpu.SemaphoreType.DMA((3 * n,)), pltpu.SemaphoreType.DMA((3 * n,))]
        + [pltpu.HBM(g.shape, g.dtype) for g in grads] + [pltpu.HBM(l.shape, l.dtype) for l in lands]
        + [jax.ShapeDtypeStruct((8, 128), f32)],
        in_specs=[_HBM] * (2 * n),
        out_specs=[_SEM, _SEM] + [_HBM] * (2 * n) + [pl.BlockSpec(memory_space=pltpu.VMEM)],
        input_output_aliases={i: 2 + i for i in range(2 * n)},
        compiler_params=pltpu.CompilerParams(has_side_effects=_EFFECT),
    )(*[_hbm(g) for g in grads], *[_hbm(l) for l in lands])
    return (outs[0], outs[1]), list(outs[2:2 + n]), list(outs[2 + n:2 + 2 * n]), outs[-1]


def exchange_wait(grads, lands, sems, after, name):
    n = len(grads)

    def body(*refs):
        ins = refs[:n]
        land = refs[n:2 * n]
        send_sems, recv_sems = refs[2 * n], refs[2 * n + 1]
        x, y, c = _place()
        for wi in range(n):
            for j, chip in enumerate(_other_chips(x, y)):
                cp = _chip_copy(ins[wi].at[2 * chip[0] + chip[1]], land[wi].at[j], send_sems.at[3 * wi + j],
                                recv_sems.at[3 * wi + j], chip, c)
                cp.wait_send()
                cp.wait_recv()

    outs = pl.pallas_call(
        body, name=name,
        out_shape=[pltpu.HBM(g.shape, g.dtype) for g in grads] + [pltpu.HBM(l.shape, l.dtype) for l in lands],
        in_specs=[_HBM] * (2 * n) + [_SEM, _SEM, _ANY],
        out_specs=[_HBM] * (2 * n),
        input_output_aliases={i: i for i in range(2 * n)},
        compiler_params=pltpu.CompilerParams(has_side_effects=_EFFECT),
    )(*grads, *lands, sems[0], sems[1], after)
    return list(outs[:n]), list(outs[n:])


def _split_start(body, name, n_sems, operands):
    n = len(operands)
    outs = pl.pallas_call(
        body, name=name,
        out_shape=[pltpu.SemaphoreType.DMA((n_sems,)), pltpu.SemaphoreType.DMA((n_sems,))]
        + [pltpu.HBM(o.shape, o.dtype) for o in operands] + [jax.ShapeDtypeStruct((8, 128), f32)],
        in_specs=[_HBM] * n,
        out_specs=[_SEM, _SEM] + [_HBM] * n + [pl.BlockSpec(memory_space=pltpu.VMEM)],
        input_output_aliases={i: 2 + i for i in range(n)},
        compiler_params=pltpu.CompilerParams(has_side_effects=_EFFECT),
    )(*[_hbm(o) for o in operands])
    return (outs[0], outs[1]), list(outs[2:2 + n]), outs[-1]


def _split_wait(body, name, operands, sems, after):
    n = len(operands)
    outs = pl.pallas_call(
        body, name=name,
        out_shape=[pltpu.HBM(o.shape, o.dtype) for o in operands],
        in_specs=[_HBM] * n + [_SEM, _SEM, _ANY],
        out_specs=[_HBM] * n,
        input_output_aliases={i: i for i in range(n)},
        compiler_params=pltpu.CompilerParams(has_side_effects=_EFFECT),
    )(*operands, sems[0], sems[1], after)
    return list(outs)


def _sibling_copy(src, dst, send_sem, recv_sem):
    x, y, c = _place()
    return pltpu.make_async_remote_copy(src_ref=src, dst_ref=dst, send_sem=send_sem, recv_sem=recv_sem,
                                        device_id=(x, y, 1 - c), device_id_type=MESH)


def swap_start(parts, name):
    n = len(parts)

    def body(*refs):
        for w in range(n):
            _sibling_copy(refs[w], refs[n + w], refs[2 * n].at[w], refs[2 * n + 1].at[w]).start()
        refs[-1][...] = jnp.zeros_like(refs[-1])

    sems, ops, token = _split_start(body, name, n, list(parts) + [lax.empty(p.shape, p.dtype) for p in parts])
    return sems, ops[:n], ops[n:], token


def swap_wait(parts, lands, sems, after, name):
    n = len(parts)

    def body(*refs):
        for w in range(n):
            cp = _sibling_copy(refs[w], refs[n + w], refs[2 * n].at[w], refs[2 * n + 1].at[w])
            cp.wait_send()
            cp.wait_recv()

    outs = _split_wait(body, name, list(parts) + list(lands), sems, after)
    return outs[:n], outs[n:]


def _all_peers(x, y, c):
    return [(1 - x if m & 4 else x, 1 - y if m & 2 else y, 1 - c if m & 1 else c) for m in range(1, N_DEV)]


def small_start(block):
    land = jnp.broadcast_to(block[None], (N_DEV,) + block.shape)

    def body(b_ref, land_ref, send_sems, recv_sems, b_thru, land_thru, token):
        x, y, c = _place()
        me = 4 * x + 2 * y + c
        for m, peer in enumerate(_all_peers(x, y, c)):
            pltpu.make_async_remote_copy(src_ref=b_ref, dst_ref=land_ref.at[me], send_sem=send_sems.at[m],
                                         recv_sem=recv_sems.at[m], device_id=peer, device_id_type=MESH).start()
        token[...] = jnp.zeros_like(token)

    sems, ops, token = _split_start(body, "small_start", N_DEV - 1, [block, land])
    return sems, ops[0], ops[1], token


def small_wait(block, land, sems, after):
    def body(b_ref, land_ref, send_sems, recv_sems, after_ref, b_thru, land_thru):
        x, y, c = _place()
        for m, (px, py, pc) in enumerate(_all_peers(x, y, c)):
            cp = pltpu.make_async_remote_copy(src_ref=b_ref, dst_ref=land_ref.at[4 * px + 2 * py + pc],
                                              send_sem=send_sems.at[m], recv_sem=recv_sems.at[m],
                                              device_id=(px, py, pc), device_id_type=MESH)
            cp.wait_send()
            cp.wait_recv()

    return _split_wait(body, "small_wait", [block, land], sems, after)[1]


def _adamw(w, g, m, v):
    m = ADAM_B1 * m + (1.0 - ADAM_B1) * g
    v = ADAM_B2 * v + (1.0 - ADAM_B2) * (g * g)
    m_hat = m / (1.0 - ADAM_B1 ** ADAM_STEP)
    v_hat = v / (1.0 - ADAM_B2 ** ADAM_STEP)
    delta = -ADAM_LR * (m_hat / (jnp.sqrt(v_hat) + ADAM_EPS) + ADAM_WD * w)
    return delta, m, v


EW_BLOCK_BYTES = 2 * 1024 * 1024


def _ew_tile(rows, cols):
    for cand in (512, 352, 256, 176, 128, 64, 32, 16, 8):
        if rows % cand == 0 and cand * cols * 4 <= EW_BLOCK_BYTES:
            return cand
    return rows


def sum_partials(chip, own, land, name):
    _, r, c = own.shape
    tr = _ew_tile(r, c)

    def body(k_ref, own_ref, p_ref, o_ref):
        o_ref[...] = ((own_ref[0].astype(f32) + p_ref[0].astype(f32)) + p_ref[1].astype(f32)) + p_ref[2].astype(f32)

    return pl.pallas_call(
        body, name=name,
        grid_spec=pltpu.PrefetchScalarGridSpec(
            num_scalar_prefetch=1, grid=(r // tr,),
            in_specs=[pl.BlockSpec((1, tr, c), lambda i, k: (k[0], i, 0)), pl.BlockSpec((3, tr, c), lambda i, k: (0, i, 0))],
            out_specs=pl.BlockSpec((tr, c), lambda i, k: (i, 0))),
        out_shape=jax.ShapeDtypeStruct((r, c), f32),
        compiler_params=_cparams(),
    )(chip, own, land)


def adamw_shard(p_mine, p_sibling, w, m, v, name):
    r, c = w.shape
    tr = _ew_tile(r, c)

    def body(a_ref, b_ref, w_ref, m_ref, v_ref, g_ref, d_ref, mo_ref, vo_ref):
        g = a_ref[...] + b_ref[...]
        delta, mn, vn = _adamw(w_ref[...], g, m_ref[...], v_ref[...])
        g_ref[...] = g
        d_ref[...] = delta
        mo_ref[...] = mn
        vo_ref[...] = vn

    blk = pl.BlockSpec((tr, c), lambda i: (i, 0))
    return pl.pallas_call(
        body, name=name, grid=(r // tr,),
        in_specs=[blk] * 5, out_specs=[blk] * 4,
        out_shape=[jax.ShapeDtypeStruct((r, c), f32)] * 4,
        compiler_params=_cparams(),
    )(p_mine, p_sibling, w, m, v)


def adamw_small(g8, w, m, v):
    _, r, lanes = g8.shape

    def body(g_ref, w_ref, m_ref, v_ref, go_ref, d_ref, mo_ref, vo_ref):
        g = g_ref[0]
        for i in range(1, N_DEV):
            g = g + g_ref[i]
        delta, mn, vn = _adamw(w_ref[...], g, m_ref[...], v_ref[...])
        go_ref[...] = g
        d_ref[...] = delta
        mo_ref[...] = mn
        vo_ref[...] = vn

    return pl.pallas_call(
        body, name="adamw_small",
        out_shape=[jax.ShapeDtypeStruct((r, lanes), f32)] * 4,
        compiler_params=_cparams(),
    )(g8, w, m, v)


def _size(shape):
    n = 1
    for e in shape:
        n *= e
    return n


def _pack_rows(shapes):
    rows = [-(-_size(s) // 1024) * 8 for s in shapes]
    return rows, sum(rows)


def _pack(arrs, shapes):
    rows, _ = _pack_rows(shapes)
    parts = [jnp.pad(a.reshape(-1).astype(f32), (0, r * 128 - _size(s))).reshape(r, 128)
             for a, s, r in zip(arrs, shapes, rows)]
    return jnp.concatenate(parts, axis=0)


def _unpack(block, shapes):
    rows, _ = _pack_rows(shapes)
    out, off = [], 0
    for s, r in zip(shapes, rows):
        out.append(block[off:off + r].reshape(-1)[:_size(s)].reshape(s))
        off += r
    return out


TRANSPOSED = ("ffn1_w_gate", "ffn1_w_up", "ffn2_w_gate", "ffn2_w_up")


def _shard2d(a, n):
    return a[0].T if n in TRANSPOSED else a[0]


def _unshard(a, n):
    return (a.T if n in TRANSPOSED else a)[None]


BIG = ("ffn1_w_gate", "ffn1_w_up", "ffn1_w_down", "w_in", "w_branch_a", "w_branch_b", "w_out",
       "ffn2_w_gate", "ffn2_w_up", "ffn2_w_down")
SMALL = ("ffn1_norm", "mix_norm", "b_in", "sgu_norm_g", "sgu_norm_b", "sgu_w_s", "sgu_b_s", "ret_decay_logit",
         "ffn2_norm", "final_norm")
WEIGHTS = ("ffn1_norm", "ffn1_w_gate", "ffn1_w_up", "ffn1_w_down", "mix_norm", "w_in", "b_in", "sgu_norm_g",
           "sgu_norm_b", "sgu_w_s", "sgu_b_s", "ret_decay_logit", "w_branch_a", "w_branch_b", "w_out", "ffn2_norm",
           "ffn2_w_gate", "ffn2_w_up", "ffn2_w_down", "final_norm")


def kernel(x, ffn1_norm, ffn1_w_gate, ffn1_w_up, ffn1_w_down, mix_norm, w_in, b_in, sgu_norm_g, sgu_norm_b, sgu_w_s, sgu_b_s, ret_decay_logit, w_branch_a, w_branch_b, w_out, ffn2_norm, ffn2_w_gate, ffn2_w_up, ffn2_w_down, final_norm, loss_target, m_ffn1_norm, m_ffn1_w_gate, m_ffn1_w_up, m_ffn1_w_down, m_mix_norm, m_w_in, m_b_in, m_sgu_norm_g, m_sgu_norm_b, m_sgu_w_s, m_sgu_b_s, m_ret_decay_logit, m_w_branch_a, m_w_branch_b, m_w_out, m_ffn2_norm, m_ffn2_w_gate, m_ffn2_w_up, m_ffn2_w_down, m_final_norm, v_ffn1_norm, v_ffn1_w_gate, v_ffn1_w_up, v_ffn1_w_down, v_mix_norm, v_w_in, v_b_in, v_sgu_norm_g, v_sgu_norm_b, v_sgu_w_s, v_sgu_b_s, v_ret_decay_logit, v_w_branch_a, v_w_branch_b, v_w_out, v_ffn2_norm, v_ffn2_w_gate, v_ffn2_w_up, v_ffn2_w_down, v_final_norm):
    p = dict(ffn1_norm=ffn1_norm, ffn1_w_gate=ffn1_w_gate, ffn1_w_up=ffn1_w_up, ffn1_w_down=ffn1_w_down,
             mix_norm=mix_norm, w_in=w_in, b_in=b_in, sgu_norm_g=sgu_norm_g, sgu_norm_b=sgu_norm_b, sgu_w_s=sgu_w_s,
             sgu_b_s=sgu_b_s, ret_decay_logit=ret_decay_logit, w_branch_a=w_branch_a, w_branch_b=w_branch_b,
             w_out=w_out, ffn2_norm=ffn2_norm, ffn2_w_gate=ffn2_w_gate, ffn2_w_up=ffn2_w_up, ffn2_w_down=ffn2_w_down,
             final_norm=final_norm)
    mom = dict(ffn1_norm=m_ffn1_norm, ffn1_w_gate=m_ffn1_w_gate, ffn1_w_up=m_ffn1_w_up, ffn1_w_down=m_ffn1_w_down,
               mix_norm=m_mix_norm, w_in=m_w_in, b_in=m_b_in, sgu_norm_g=m_sgu_norm_g, sgu_norm_b=m_sgu_norm_b,
               sgu_w_s=m_sgu_w_s, sgu_b_s=m_sgu_b_s, ret_decay_logit=m_ret_decay_logit, w_branch_a=m_w_branch_a,
               w_branch_b=m_w_branch_b, w_out=m_w_out, ffn2_norm=m_ffn2_norm, ffn2_w_gate=m_ffn2_w_gate,
               ffn2_w_up=m_ffn2_w_up, ffn2_w_down=m_ffn2_w_down, final_norm=m_final_norm)
    var = dict(ffn1_norm=v_ffn1_norm, ffn1_w_gate=v_ffn1_w_gate, ffn1_w_up=v_ffn1_w_up, ffn1_w_down=v_ffn1_w_down,
               mix_norm=v_mix_norm, w_in=v_w_in, b_in=v_b_in, sgu_norm_g=v_sgu_norm_g, sgu_norm_b=v_sgu_norm_b,
               sgu_w_s=v_sgu_w_s, sgu_b_s=v_sgu_b_s, ret_decay_logit=v_ret_decay_logit, w_branch_a=v_w_branch_a,
               w_branch_b=v_w_branch_b, w_out=v_w_out, ffn2_norm=v_ffn2_norm, ffn2_w_gate=v_ffn2_w_gate,
               ffn2_w_up=v_ffn2_w_up, ffn2_w_down=v_ffn2_w_down, final_norm=v_final_norm)

    xs = x[0]
    tgt = loss_target[0]
    t, d = xs.shape
    dk = d // RET_HEADS
    tm = _row_tile(t)

    shards2d = {n: _shard2d(p[n], n) for n in BIG}
    chip = (2 * lax.axis_index("x") + lax.axis_index("y")).astype(jnp.int32).reshape(1)
    groups = {"ffn1": ("ffn1_w_gate", "ffn1_w_up", "ffn1_w_down"), "in": ("w_in",),
              "mix": ("w_branch_a", "w_branch_b", "w_out"), "ffn2": ("ffn2_w_gate", "ffn2_w_up", "ffn2_w_down")}
    def own_slot(n, zero):
        sh = shards2d[n].astype(bf16) + zero
        return lax.dynamic_update_index_in_dim(lax.empty((N_CHIPS,) + sh.shape, bf16), sh, chip[0], 0)

    sems, bufs, tok = gather_start([own_slot(n, jnp.zeros((), bf16)) for n in groups["ffn1"]], [[0, 1, 2]],
                                   "gather_start_ffn1")
    gsem = {"ffn1": sems[0]}
    pending = dict(zip(groups["ffn1"], bufs))
    rest = [n for g in ("in", "mix", "ffn2") for n in groups[g]]
    sems, bufs, tok_rest = gather_start([own_slot(n, tok[0, 0].astype(bf16)) for n in rest],
                                 [[rest.index(n) for n in groups[g]] for g in ("in", "mix", "ffn2")], "gather_start_rest")
    gsem.update(zip(("in", "mix", "ffn2"), sems))
    pending.update(zip(rest, bufs))

    def arrive(gs, after):
        got = []
        for g in gs:
            got += gather_wait([pending[n] for n in groups[g]], gsem[g], after, "gather_wait_" + g)
        return gather_forward(got, "gather_forward_" + gs[0])

    bin4 = b_in.reshape(N_CHIPS, 1, 2 * d)
    ws_b = sgu_w_s[0].astype(bf16)
    bs_c = sgu_b_s[0][:, :, None]
    cols, mats, cdec, cos, sin = retention_constants(ret_decay_logit[0], t, dk)

    wg1, wu1, wd1 = [_pair_shards(w) for w in arrive(["ffn1"], tok_rest)]
    x1, g1, u1 = ffn_fwd(xs, ffn1_norm, wg1, wu1, wd1, "ffn1_fwd")
    win, = arrive(["in"], x1)
    proj, hb2 = inproj_fwd(x1, mix_norm, win, bin4, cos, sin)
    a = sgu_fwd(proj, sgu_norm_g, sgu_norm_b, ws_b, bs_c)
    r, rn = ret_fwd(proj, cols, mats, cdec)
    wa, wb, wo, wg2, wu2, wd2 = arrive(["mix", "ffn2"], rn)
    wa, wb, wo = [w.reshape(d, d) for w in (wa, wb, wo)]
    wg2, wu2, wd2 = [_pair_shards(w) for w in (wg2, wu2, wd2)]
    x2, ba, br = mix_fwd(a, rn, proj, wa, wb, wo, x1)
    x3, g2, u2 = ffn_fwd(x2, ffn2_norm, wg2, wu2, wd2, "ffn2_fwd")
    loss_blk, dx3, d_final = loss_head(x3, final_norm.reshape(1, d), tgt)

    sent = {}
    dx2, dg2, du2, act2, hb3, dyb2, d_ffn2n = ffn_bwd_act(dx3, x2, ffn2_norm, g2, u2, wg2, wu2, wd2, "ffn2_bwd_act", tok)
    sent["ffn2"] = exchange_start(ffn_weight_grads(hb3, dyb2, dg2, du2, act2, "ffn2_grad", tok), "exchange_start_ffn2")
    da, drn, dga, dgb, mixb, dba, dbr, dx2b = mix_bwd_act(dx2, ba, br, proj, wa, wb, wo, sent["ffn2"][3])
    tg = min(t, 2048)
    row = pl.BlockSpec((tg, d), lambda s, i: (i, 0))

    def square_grad(xa, ya, name):
        return tn_matmul(xa, [ya], row, [row], 1, d, [d], t, tg, name, tok).reshape(N_CHIPS, d // N_CHIPS, d)

    sent["mix"] = exchange_start([square_grad(a, dba, "grad_w_branch_a"), square_grad(rn, dbr, "grad_w_branch_b"),
                                  square_grad(mixb, dx2b, "grad_w_out")], "exchange_start_mix")
    dua, dva, d_ws, d_bs, d_sng, d_snb = sgu_bwd(da, proj, sgu_norm_g, sgu_norm_b, ws_b, bs_c, sent["mix"][3])
    dq, dkr, dv, dgr, dlg = ret_bwd(drn, r, proj, cols, mats, cdec, cos, sin)
    segs = [dua, dva, dq, dkr, dv, dgr, dga, dgb]
    dx1, d_bin, d_mixn = inproj_bwd_act(segs, win, x1, mix_norm, dx2)
    sent["in"] = exchange_start([jnp.concatenate(
        [tn_matmul(hb2, [segs[2 * s], segs[2 * s + 1]], row, [row, row], 1, d, [d, d], t, tg, "grad_w_in_%d" % s, tok)
         for s in range(N_CHIPS)], axis=0)], "exchange_start_in")
    grad_x, dg1, du1, act1, hb1, dyb1, d_ffn1n = ffn_bwd_act(dx1, xs, ffn1_norm, g1, u1, wg1, wu1, wd1, "ffn1_bwd_act",
                                                              sent["in"][3])
    dlogit = dlg[:, 0:2, 0].T * jax.nn.sigmoid(-ret_decay_logit[0].astype(f32))
    small_g = dict(ffn1_norm=d_ffn1n, mix_norm=d_mixn, b_in=d_bin, sgu_norm_g=d_sng, sgu_norm_b=d_snb, sgu_w_s=d_ws,
                   sgu_b_s=d_bs, ret_decay_logit=dlogit, ffn2_norm=d_ffn2n, final_norm=d_final)
    shapes = [p[n].shape for n in SMALL]
    small_sems, small_blk, small_land, small_tok = small_start(_pack([small_g[n] for n in SMALL], shapes))

    def send_one(which, grad):
        n = "ffn1_" + which
        groups[n] = (n,)
        sent[n] = exchange_start([grad], "exchange_start_" + n)
        return sent[n][3]

    ffn_weight_grads(hb1, dyb1, dg1, du1, act1, "ffn1_grad", small_tok, send_one)

    out_g, out_d, out_m, out_v = {}, {}, {}, {}
    swaps = {}

    def reduce_plane(g, after):
        gsems, own, lands, _ = sent[g]
        own, lands = exchange_wait(own, lands, gsems, after, "exchange_wait_" + g)
        plane = [sum_partials(chip, o, l, "sum_" + n) for n, o, l in zip(groups[g], own, lands)]
        swaps[g] = swap_start(plane, "swap_start_" + g)
        return swaps[g][3]

    def update(g, after):
        ssems, plane, lands, _ = swaps[g]
        plane, other = swap_wait(plane, lands, ssems, after, "swap_wait_" + g)
        for n, mine, sib in zip(groups[g], plane, other):
            res = adamw_shard(mine, sib, shards2d[n], _shard2d(mom[n], n), _shard2d(var[n], n), "adamw_" + n)
            out_g[n], out_d[n], out_m[n], out_v[n] = [_unshard(o, n) for o in res]
        return out_g[groups[g][-1]]

    after = reduce_plane("ffn2", sent["ffn1_w_down"][3])
    after = reduce_plane("mix", after)
    after = update("ffn2", after)
    after = reduce_plane("in", after)
    after = update("mix", after)
    g8 = small_wait(small_blk, small_land, small_sems, after)
    sg, sd, sm, sv = adamw_small(g8, _pack([p[n] for n in SMALL], shapes), _pack([mom[n] for n in SMALL], shapes),
                                 _pack([var[n] for n in SMALL], shapes))
    for res, blockv in ((out_g, sg), (out_d, sd), (out_m, sm), (out_v, sv)):
        for n, val in zip(SMALL, _unpack(blockv, shapes)):
            res[n] = val
    after = update("in", sg)
    after = reduce_plane("ffn1_w_gate", after)
    after = reduce_plane("ffn1_w_up", after)
    after = update("ffn1_w_gate", after)
    after = reduce_plane("ffn1_w_down", after)
    after = update("ffn1_w_up", after)
    update("ffn1_w_down", after)

    loss = lax.psum(loss_blk[0, 0], ("x", "y", "c"))
    return (loss, grad_x[None], *[out_g[n] for n in WEIGHTS], *[out_d[n] for n in WEIGHTS],
            *[out_m[n] for n in WEIGHTS], *[out_v[n] for n in WEIGHTS])
```

```python
import functools

import jax
import jax.numpy as jnp
from jax import lax
from jax.experimental import pallas as pl
from jax.experimental.pallas import tpu as pltpu

f32 = jnp.float32
bf16 = jnp.bfloat16

SGU_CHUNK = 128
CHUNK = 128
RET_HEADS = 4
SGU_GROUPS = 4
ROPE_BASE = 10000.0
NORM_EPS = 1e-6
ADAM_LR = 0.001
ADAM_B1 = 0.9
ADAM_B2 = 0.999
ADAM_EPS = 1e-08
ADAM_WD = 0.01
ADAM_STEP = 10
N_CHIPS = 4
N_DEV = 8
MESH = pl.DeviceIdType.MESH
VMEM_LIMIT = 52 * 1024 * 1024
VMEM_LIMIT_WIDE = 62 * 1024 * 1024

_NT = (((1,), (1,)), ((), ()))
_TN = (((0,), (0,)), ((), ()))


def _cparams(limit=None):
    return pltpu.CompilerParams(vmem_limit_bytes=VMEM_LIMIT if limit is None else limit)


def _row_tile(t):
    return 512 if t >= 2048 else t // 2


def _dot(a, b):
    return jnp.dot(a, b, preferred_element_type=f32)


def _dot_nt(a, b):
    return lax.dot_general(a, b, _NT, preferred_element_type=f32)


def _dot_tn(a, b):
    return lax.dot_general(a, b, _TN, preferred_element_type=f32)


def _rms(x, g):
    r = lax.rsqrt(jnp.mean(x * x, axis=-1, keepdims=True) + NORM_EPS)
    xh = x * r
    return xh * g, xh, r


def _rms_bwd(dy, xh, r, g):
    dxh = dy * g
    return r * (dxh - xh * jnp.mean(dxh * xh, axis=-1, keepdims=True))


def _sigmoid(x):
    return jax.nn.sigmoid(x)


def _dsilu(g, sg):
    return sg * (1.0 + g * (1.0 - sg))


def _gelu(x):
    return 0.5 * x * (1.0 + lax.erf(x * 0.7071067811865476))


def _dgelu(x):
    return 0.5 * (1.0 + lax.erf(x * 0.7071067811865476)) + x * jnp.exp(-0.5 * x * x) * 0.3989422804014327


def _acc_out(ref, first, val):
    @pl.when(first)
    def _():
        ref[...] = val

    @pl.when(jnp.logical_not(first))
    def _():
        ref[...] += val


def _ffn_tile(t):
    return 256 if t >= 2048 else t // 2


def _ffn_fwd_rows(xx, ng_ref, wg_ref, wu_ref, wd_ref, g_ref, u_ref):
    y, _, _ = _rms(xx, ng_ref[...])
    h = y.astype(bf16)
    acc = None
    for s in range(wg_ref.shape[0]):
        g = _dot_nt(h, wg_ref[s])
        u = _dot_nt(h, wu_ref[s])
        g_ref[s] = g.astype(bf16)
        u_ref[s] = u.astype(bf16)
        part = _dot((g * _sigmoid(g) * u).astype(bf16), wd_ref[s])
        acc = part if acc is None else acc + part
    return xx + 0.5 * acc


def ffn_fwd(x, ng, wg, wu, wd, name):
    t, d = x.shape
    ns, fs, _ = wg.shape
    tm = _ffn_tile(t)

    def body(x_ref, ng_ref, wg_ref, wu_ref, wd_ref, xo_ref, g_ref, u_ref):
        xo_ref[...] = _ffn_fwd_rows(x_ref[...], ng_ref, wg_ref, wu_ref, wd_ref, g_ref, u_ref)

    row = pl.BlockSpec((tm, d), lambda i: (i, 0))
    shard = pl.BlockSpec((ns, tm, fs), lambda i: (0, i, 0))
    wspec = pl.BlockSpec((ns, fs, d), lambda i: (0, 0, 0), pipeline_mode=pl.Buffered(1))
    return pl.pallas_call(
        body, name=name, grid=(t // tm,),
        in_specs=[row, pl.BlockSpec((1, d), lambda i: (0, 0)), wspec, wspec, wspec],
        out_specs=[row, shard, shard],
        out_shape=[jax.ShapeDtypeStruct((t, d), f32), jax.ShapeDtypeStruct((ns, t, fs), bf16),
                   jax.ShapeDtypeStruct((ns, t, fs), bf16)],
        compiler_params=_cparams(),
    )(x, ng, wg, wu, wd)


def ffn_fwd_loss(x, ng, wg, wu, wd, fng, tgt, name):
    t, d = x.shape
    ns, fs, _ = wg.shape
    tm = _ffn_tile(t)

    def body(x_ref, ng_ref, wg_ref, wu_ref, wd_ref, fng_ref, t_ref, loss_ref, dx_ref, dfn_ref, g_ref, u_ref):
        i = pl.program_id(0)
        x3 = _ffn_fwd_rows(x_ref[...], ng_ref, wg_ref, wu_ref, wd_ref, g_ref, u_ref)
        y, xh, r = _rms(x3, fng_ref[...])
        diff = y - t_ref[...]
        part = 0.5 * jnp.sum(jnp.sum(diff * diff, axis=0, keepdims=True), axis=1, keepdims=True) / d
        _acc_out(loss_ref, i == 0, jnp.broadcast_to(part, (1, 128)))
        dy = diff * (1.0 / d)
        dx_ref[...] = _rms_bwd(dy, xh, r, fng_ref[...])
        _acc_out(dfn_ref, i == 0, jnp.sum(dy * xh, axis=0, keepdims=True))

    row = pl.BlockSpec((tm, d), lambda i: (i, 0))
    vec = pl.BlockSpec((1, d), lambda i: (0, 0))
    shard = pl.BlockSpec((ns, tm, fs), lambda i: (0, i, 0))
    wspec = pl.BlockSpec((ns, fs, d), lambda i: (0, 0, 0), pipeline_mode=pl.Buffered(1))
    return pl.pallas_call(
        body, name=name, grid=(t // tm,),
        in_specs=[row, vec, wspec, wspec, wspec, vec, row],
        out_specs=[pl.BlockSpec((1, 128), lambda i: (0, 0)), row, vec, shard, shard],
        out_shape=[jax.ShapeDtypeStruct((1, 128), f32), jax.ShapeDtypeStruct((t, d), f32), jax.ShapeDtypeStruct((1, d), f32),
                   jax.ShapeDtypeStruct((ns, t, fs), bf16), jax.ShapeDtypeStruct((ns, t, fs), bf16)],
        compiler_params=_cparams(),
    )(x, ng, wg, wu, wd, fng, tgt)


def ffn_bwd_act(dxo, x, ng, g, u, wg, wu, wd, name, dep):
    t, d = x.shape
    ns, fs, _ = wg.shape
    tm = _ffn_tile(t)

    def body(dxo_ref, x_ref, ng_ref, g_ref, u_ref, wg_ref, wu_ref, wd_ref, dep_ref,
             dx_ref, dg_ref, du_ref, act_ref, hb_ref, dyb_ref, dng_ref):
        i = pl.program_id(0)
        dxo = dxo_ref[...]
        dyb = (0.5 * dxo).astype(bf16)
        dyb_ref[...] = dyb
        dh = None
        for s in range(ns):
            dact = _dot_nt(dyb, wd_ref[s])
            gg = g_ref[s].astype(f32)
            uu = u_ref[s].astype(f32)
            sg = _sigmoid(gg)
            sil = gg * sg
            dgb = (dact * uu * _dsilu(gg, sg)).astype(bf16)
            dub = (dact * sil).astype(bf16)
            dg_ref[s] = dgb
            du_ref[s] = dub
            act_ref[s] = (sil * uu).astype(bf16)
            part = _dot(dgb, wg_ref[s]) + _dot(dub, wu_ref[s])
            dh = part if dh is None else dh + part
        y, xh, r = _rms(x_ref[...], ng_ref[...])
        hb_ref[...] = y.astype(bf16)
        dx_ref[...] = dxo + _rms_bwd(dh, xh, r, ng_ref[...])
        _acc_out(dng_ref, i == 0, jnp.sum(dh * xh, axis=0, keepdims=True))

    row = pl.BlockSpec((tm, d), lambda i: (i, 0))
    shard = pl.BlockSpec((ns, tm, fs), lambda i: (0, i, 0))
    wspec = pl.BlockSpec((ns, fs, d), lambda i: (0, 0, 0), pipeline_mode=pl.Buffered(1))
    vec = pl.BlockSpec((1, d), lambda i: (0, 0))
    return pl.pallas_call(
        body, name=name, grid=(t // tm,),
        in_specs=[row, row, vec, shard, shard, wspec, wspec, wspec, _ANY],
        out_specs=[row, shard, shard, shard, row, row, vec],
        out_shape=[jax.ShapeDtypeStruct((t, d), f32)] + [jax.ShapeDtypeStruct((ns, t, fs), bf16)] * 3
        + [jax.ShapeDtypeStruct((t, d), bf16)] * 2 + [jax.ShapeDtypeStruct((1, d), f32)],
        compiler_params=_cparams(VMEM_LIMIT_WIDE),
    )(dxo, x, ng, g, u, wg, wu, wd, dep)


def tn_matmul(xs, ys, x_spec, y_specs, n_shards, k1, k2s, t, tm, name, dep):
    k2 = sum(k2s)
    ny = len(ys)

    def body(*refs):
        x_ref = refs[0]
        y_refs = refs[1:1 + ny]
        o_ref = refs[2 + ny]
        acc = refs[3 + ny]
        i = pl.program_id(1)
        xb = x_ref[0] if len(x_ref.shape) == 3 else x_ref[...]
        off = 0
        for y_ref, w in zip(y_refs, k2s):
            yb = y_ref[0] if len(y_ref.shape) == 3 else y_ref[...]
            part = _dot_tn(xb, yb)
            sl = (slice(None), slice(off, off + w))

            @pl.when(i == 0)
            def _(part=part, sl=sl):
                acc[sl] = part

            @pl.when(i > 0)
            def _(part=part, sl=sl):
                acc[sl] += part

            off += w

        @pl.when(i == t // tm - 1)
        def _():
            o_ref[0] = acc[...].astype(bf16)

    return pl.pallas_call(
        body, name=name, grid=(n_shards, t // tm),
        in_specs=[x_spec] + list(y_specs) + [_ANY],
        out_specs=pl.BlockSpec((1, k1, k2), lambda s, i: (s, 0, 0)),
        out_shape=jax.ShapeDtypeStruct((n_shards, k1, k2), bf16),
        scratch_shapes=[pltpu.VMEM((k1, k2), f32)],
        compiler_params=_cparams(),
    )(xs, *ys, dep)


def _pair_shards(w):
    s4, fs, d = w.shape
    return w.reshape(s4 // 2, 2 * fs, d)


def ffn_weight_grads(hb, dyb, dg, du, act, name, dep, each=None):
    t, d = hb.shape
    s2, _, fs2 = dg.shape
    tm = t
    row = pl.BlockSpec((tm, d), lambda s, i: (i, 0))
    shard = pl.BlockSpec((1, tm, fs2), lambda s, i: (s, i, 0))
    grads = []
    for xa, ya, which in ((dg, hb, "w_gate"), (du, hb, "w_up"), (act, dyb, "w_down")):
        g = tn_matmul(xa, [ya], shard, [row], s2, fs2, [d], t, tm, name + "_" + which, dep)
        g = g.reshape(2 * s2, fs2 // 2, d)
        if each is not None:
            dep = each(which, g)
        grads.append(g)
    return grads


def inproj_fwd(x1, ng, win, bin4, cos, sin):
    t, d = x1.shape
    s4, _, w2 = win.shape
    tm = _row_tile(t)
    dk = d // RET_HEADS
    scale = dk ** -0.5

    def body(x_ref, ng_ref, w_ref, b_ref, cos_ref, sin_ref, p_ref, hb_ref):
        y, _, _ = _rms(x_ref[...], ng_ref[...])
        h = y.astype(bf16)
        hb_ref[...] = h
        for s in range(s4):
            p = _dot(h, w_ref[s]) + b_ref[s]
            if s != 1:
                p_ref[s] = p.astype(bf16)
            else:
                cs, sn = cos_ref[...], sin_ref[...]
                for e in range(2 * RET_HEADS):
                    cols = slice(e * dk, (e + 1) * dk)
                    rot = _rot(p[:, cols], cs, sn)
                    p_ref[s, :, cols] = (rot if e < RET_HEADS else rot * scale).astype(bf16)

    tab = pl.BlockSpec((tm, dk // 2), lambda i: (i, 0))
    return pl.pallas_call(
        body, name="inproj_fwd", grid=(t // tm,),
        in_specs=[pl.BlockSpec((tm, d), lambda i: (i, 0)), pl.BlockSpec((1, d), lambda i: (0, 0)),
                  pl.BlockSpec((s4, d, w2), lambda i: (0, 0, 0), pipeline_mode=pl.Buffered(1)),
                  pl.BlockSpec((s4, 1, w2), lambda i: (0, 0, 0)), tab, tab],
        out_specs=[pl.BlockSpec((s4, tm, w2), lambda i: (0, i, 0)), pl.BlockSpec((tm, d), lambda i: (i, 0))],
        out_shape=[jax.ShapeDtypeStruct((s4, t, w2), bf16), jax.ShapeDtypeStruct((t, d), bf16)],
        compiler_params=_cparams(),
    )(x1, ng, win, bin4, cos, sin)


def _sgu_norm(va, ng, nb):
    gv = _gelu(va)
    mu = jnp.mean(gv, axis=-1, keepdims=True)
    xc = gv - mu
    rstd = lax.rsqrt(jnp.mean(xc * xc, axis=-1, keepdims=True) + NORM_EPS)
    xh = xc * rstd
    return xh, rstd, (xh * ng + nb).astype(bf16)


def sgu_fwd(proj, ng, nb, ws, bs, dep):
    _, t, w2 = proj.shape
    d = w2 // 2
    gd = d // SGU_GROUPS
    tm = _row_tile(t)

    def body(p_ref, ng_ref, nb_ref, ws_ref, bs_ref, dep_ref, a_ref):
        ua = p_ref[0, :, 0:d].astype(f32)
        va = p_ref[0, :, d:w2].astype(f32)
        gu = _gelu(ua)
        _, _, vn = _sgu_norm(va, ng_ref[...], nb_ref[...])
        for c in range(tm // SGU_CHUNK):
            rows = slice(c * SGU_CHUNK, (c + 1) * SGU_CHUNK)
            for g in range(SGU_GROUPS):
                cols = slice(g * gd, (g + 1) * gd)
                sg = _dot(ws_ref[g], vn[rows, cols]) + bs_ref[g]
                a_ref[rows, cols] = (gu[rows, cols] * sg).astype(bf16)

    return pl.pallas_call(
        body, name="sgu_fwd", grid=(t // tm,),
        in_specs=[pl.BlockSpec((1, tm, w2), lambda i: (0, i, 0)), pl.BlockSpec((1, d), lambda i: (0, 0)),
                  pl.BlockSpec((1, d), lambda i: (0, 0)), pl.BlockSpec((SGU_GROUPS, SGU_CHUNK, SGU_CHUNK), lambda i: (0, 0, 0)),
                  pl.BlockSpec((SGU_GROUPS, SGU_CHUNK, 1), lambda i: (0, 0, 0)), _ANY],
        out_specs=pl.BlockSpec((tm, d), lambda i: (i, 0)),
        out_shape=jax.ShapeDtypeStruct((t, d), bf16),
        compiler_params=_cparams(),
    )(proj, ng, nb, ws, bs, dep)


def sgu_bwd(da, proj, ng, nb, ws, bs, dep):
    _, t, w2 = proj.shape
    d = w2 // 2
    gd = d // SGU_GROUPS
    tm = _row_tile(t)

    def body(da_ref, p_ref, ng_ref, nb_ref, ws_ref, bs_ref, dep_ref,
             dua_ref, dva_ref, dws_ref, dbs_ref, dng_ref, dnb_ref, dvn_scr):
        i = pl.program_id(0)
        ua = p_ref[0, :, 0:d].astype(f32)
        va = p_ref[0, :, d:w2].astype(f32)
        gu = _gelu(ua)
        xh, rstd, vn = _sgu_norm(va, ng_ref[...], nb_ref[...])
        dad = da_ref[...].astype(f32)
        dsb = (dad * gu).astype(bf16)
        for c in range(tm // SGU_CHUNK):
            rows = slice(c * SGU_CHUNK, (c + 1) * SGU_CHUNK)
            for g in range(SGU_GROUPS):
                cols = slice(g * gd, (g + 1) * gd)
                sg = _dot(ws_ref[g], vn[rows, cols]) + bs_ref[g]
                dua_ref[rows, cols] = (dad[rows, cols] * sg * _dgelu(ua[rows, cols])).astype(bf16)
                ds = dsb[rows, cols]
                dvn_scr[rows, cols] = _dot_tn(ws_ref[g], ds)
                dw = _dot_nt(ds, vn[rows, cols])
                db = jnp.sum(ds.astype(f32), axis=1, keepdims=True)
                if c == 0:
                    _acc_out(dws_ref.at[g], i == 0, dw)
                    _acc_out(dbs_ref.at[g], i == 0, db)
                else:
                    dws_ref[g] += dw
                    dbs_ref[g] += db
        dvn = dvn_scr[...]
        _acc_out(dng_ref, i == 0, jnp.sum(dvn * xh, axis=0, keepdims=True))
        _acc_out(dnb_ref, i == 0, jnp.sum(dvn, axis=0, keepdims=True))
        dxh = dvn * ng_ref[...]
        dgv = rstd * (dxh - jnp.mean(dxh, axis=-1, keepdims=True) - xh * jnp.mean(dxh * xh, axis=-1, keepdims=True))
        dva_ref[...] = (dgv * _dgelu(va)).astype(bf16)

    row = pl.BlockSpec((tm, d), lambda i: (i, 0))
    vec = pl.BlockSpec((1, d), lambda i: (0, 0))
    wsp = pl.BlockSpec((SGU_GROUPS, SGU_CHUNK, SGU_CHUNK), lambda i: (0, 0, 0))
    bsp = pl.BlockSpec((SGU_GROUPS, SGU_CHUNK, 1), lambda i: (0, 0, 0))
    return pl.pallas_call(
        body, name="sgu_bwd", grid=(t // tm,),
        in_specs=[row, pl.BlockSpec((1, tm, w2), lambda i: (0, i, 0)), vec, vec, wsp, bsp, _ANY],
        out_specs=[row, row, wsp, bsp, vec, vec],
        out_shape=[jax.ShapeDtypeStruct((t, d), bf16), jax.ShapeDtypeStruct((t, d), bf16),
                   jax.ShapeDtypeStruct((SGU_GROUPS, SGU_CHUNK, SGU_CHUNK), f32), jax.ShapeDtypeStruct((SGU_GROUPS, SGU_CHUNK, 1), f32),
                   jax.ShapeDtypeStruct((1, d), f32), jax.ShapeDtypeStruct((1, d), f32)],
        scratch_shapes=[pltpu.VMEM((tm, d), f32)],
        compiler_params=_cparams(),
    )(da, proj, ng, nb, ws, bs, dep)


def retention_constants(decay_logit, t, dk, zero):
    lg = jax.nn.log_sigmoid(decay_logit.astype(f32) + zero)
    lgf = lg[0][:, None]
    lgb = lg[1][:, None]
    idx = jnp.arange(CHUNK, dtype=f32)[None, :]
    af = jnp.exp((idx + 1.0) * lgf)
    ab = jnp.exp((CHUNK - idx) * lgb)
    kf = jnp.exp((CHUNK - 1.0 - idx) * lgf)
    kb = jnp.exp(idx * lgb)
    cols = jnp.stack([af, ab, kf, kb, af * (idx + 1.0), ab * (CHUNK - idx), kf * (CHUNK - 1.0 - idx), kb * idx], axis=1)
    cols = cols[..., None]
    diff = idx[0][:, None] - idx[0][None, :]
    dfm = jnp.where(diff >= 0, jnp.exp(jnp.maximum(diff, 0.0)[None] * lgf[:, :, None]), 0.0)
    dbm = jnp.where(diff < 0, jnp.exp(jnp.maximum(-diff, 0.0)[None] * lgb[:, :, None]), 0.0)
    mats = jnp.stack([dfm + dbm, dfm * diff[None], dbm * (-diff)[None]], axis=1)
    cdec = jnp.stack([jnp.broadcast_to(jnp.exp(CHUNK * lgf), (RET_HEADS, dk)),
                      jnp.broadcast_to(jnp.exp(CHUNK * lgb), (RET_HEADS, dk))], axis=1)
    theta = ROPE_BASE ** (-jnp.arange(0, dk, 2, dtype=f32) / dk)
    ang = (jnp.arange(t, dtype=f32) + zero)[:, None] * theta[None, :]
    return cols, mats, cdec, jnp.cos(ang), jnp.sin(ang)


def _rot(tr, cos, sin):
    half = tr.shape[-1] // 2
    t1 = tr[:, :half]
    t2 = tr[:, half:]
    return jnp.concatenate([t1 * cos - t2 * sin, t2 * cos + t1 * sin], axis=-1)


def _rot_inv(dt, cos, sin):
    half = dt.shape[-1] // 2
    d1 = dt[:, :half]
    d2 = dt[:, half:]
    return jnp.concatenate([d1 * cos + d2 * sin, d2 * cos - d1 * sin], axis=-1)


def _ret_specs(t, d, dk, rt):
    nr = t // rt
    hq = d // dk

    def blk(p, n):
        return (1 - p) * (nr - 1 - n) + p * n

    q_spec = pl.BlockSpec((1, rt, dk), lambda h, p, n: (1, blk(p, n), h))
    k_spec = pl.BlockSpec((1, rt, dk), lambda h, p, n: (1, blk(p, n), hq + h))
    v_spec = pl.BlockSpec((1, rt, dk), lambda h, p, n: (2, blk(p, n), h))
    g_spec = pl.BlockSpec((1, rt, dk), lambda h, p, n: (2, blk(p, n), hq + h))
    tab_spec = pl.BlockSpec((rt, dk // 2), lambda h, p, n: (blk(p, n), 0))
    cols_spec = pl.BlockSpec((1, 8, CHUNK, 1), lambda h, p, n: (h, 0, 0, 0))
    mats_spec = pl.BlockSpec((1, 3, CHUNK, CHUNK), lambda h, p, n: (h, 0, 0, 0))
    cdec_spec = pl.BlockSpec((1, 2, dk), lambda h, p, n: (h, 0, 0))
    in_row = pl.BlockSpec((rt, dk), lambda h, p, n: (blk(p, n), h))
    out_row = pl.BlockSpec((rt, dk), lambda h, p, n: (p * n, h))
    return nr, blk, q_spec, k_spec, v_spec, g_spec, tab_spec, cols_spec, mats_spec, cdec_spec, in_row, out_row


def ret_fwd(proj, cols, mats, cdec):
    _, t, w2 = proj.shape
    d = w2 // 2
    dk = d // RET_HEADS
    rt = _row_tile(t)
    cpt = rt // CHUNK
    nr, blk, q_spec, k_spec, v_spec, g_spec, _, cols_spec, mats_spec, cdec_spec, _, out_row = _ret_specs(t, d, dk, rt)

    def body(q_ref, k_ref, v_ref, g_ref, cols_ref, mats_ref, cdec_ref, r_ref, rn_ref, sb_scr, st):
        p = pl.program_id(1)
        n = pl.program_id(2)
        af, ab, kf, kb = cols_ref[0, 0], cols_ref[0, 1], cols_ref[0, 2], cols_ref[0, 3]
        cf = cdec_ref[0, 0:1, :]
        cb = cdec_ref[0, 1:2, :]

        @pl.when(n == 0)
        def _():
            st[...] = jnp.zeros_like(st)

        @pl.when(p == 0)
        def _():
            for j in reversed(range(cpt)):
                rows = slice(j * CHUNK, (j + 1) * CHUNK)
                ch = blk(p, n) * cpt + j
                kk = k_ref[0, rows, :].astype(f32)
                sb_scr[ch] = st[...].astype(bf16)
                st[...] = st[...] * cb + _dot_tn((kk * kb).astype(bf16), v_ref[0, rows, :])

        @pl.when(p == 1)
        def _():
            for j in range(cpt):
                rows = slice(j * CHUNK, (j + 1) * CHUNK)
                ch = blk(p, n) * cpt + j
                qb = q_ref[0, rows, :]
                kkb = k_ref[0, rows, :]
                q = qb.astype(f32)
                kk = kkb.astype(f32)
                v = v_ref[0, rows, :]
                pm = (_dot_nt(qb, kkb) * mats_ref[0, 0]).astype(bf16)
                out = (_dot(pm, v) + _dot((q * af).astype(bf16), st[...].astype(bf16))
                       + _dot((q * ab).astype(bf16), sb_scr[ch]))
                st[...] = st[...] * cf + _dot_tn((kk * kf).astype(bf16), v)
                rhat = out * lax.rsqrt(jnp.mean(out * out, axis=-1, keepdims=True) + NORM_EPS)
                gg = g_ref[0, rows, :].astype(f32)
                r_ref[rows, :] = out.astype(bf16)
                rn_ref[rows, :] = (rhat * gg * _sigmoid(gg)).astype(bf16)

    return pl.pallas_call(
        body, name="ret_fwd", grid=(RET_HEADS, 2, nr),
        in_specs=[q_spec, k_spec, v_spec, g_spec, cols_spec, mats_spec, cdec_spec],
        out_specs=[out_row, out_row],
        out_shape=[jax.ShapeDtypeStruct((t, d), bf16), jax.ShapeDtypeStruct((t, d), bf16)],
        scratch_shapes=[pltpu.VMEM((t // CHUNK, dk, dk), bf16), pltpu.VMEM((dk, dk), f32)],
        compiler_params=_cparams(),
    )(proj, proj, proj, proj, cols, mats, cdec)


def ret_bwd(drn, r, proj, cols, mats, cdec, cos, sin):
    _, t, w2 = proj.shape
    d = w2 // 2
    dk = d // RET_HEADS
    rt = _row_tile(t)
    cpt = rt // CHUNK
    nr, blk, q_spec, k_spec, v_spec, g_spec, tab_spec, cols_spec, mats_spec, cdec_spec, in_row, out_row = _ret_specs(t, d, dk, rt)
    scale = dk ** -0.5

    def body(drn_ref, r_ref, q_ref, k_ref, v_ref, g_ref, cos_ref, sin_ref, cols_ref, mats_ref, cdec_ref,
             dq_ref, dk_ref, dv_ref, dg_ref, dlg_ref,
             sb_scr, gf_scr, st_s, st_g, acc_af, acc_ab, acc_vf, acc_vb, acc_sf, acc_sb, dout_scr, dgr_scr):
        p = pl.program_id(1)
        n = pl.program_id(2)
        af, ab, kf, kb = cols_ref[0, 0], cols_ref[0, 1], cols_ref[0, 2], cols_ref[0, 3]
        af1, ab1, kf1, kb1 = cols_ref[0, 4], cols_ref[0, 5], cols_ref[0, 6], cols_ref[0, 7]
        cf = cdec_ref[0, 0:1, :]
        cb = cdec_ref[0, 1:2, :]

        @pl.when(n == 0)
        def _():
            st_s[...] = jnp.zeros_like(st_s)
            st_g[...] = jnp.zeros_like(st_g)

        @pl.when(jnp.logical_and(n == 0, p == 1))
        def _():
            for a in (acc_af, acc_ab, acc_vf, acc_vb, acc_sf, acc_sb):
                a[...] = jnp.zeros_like(a)

        def load(rows):
            cs, sn = cos_ref[rows, :], sin_ref[rows, :]
            q = q_ref[0, rows, :].astype(f32)
            kk = k_ref[0, rows, :].astype(f32)
            rr = r_ref[rows, :].astype(f32)
            rstd = lax.rsqrt(jnp.mean(rr * rr, axis=-1, keepdims=True) + NORM_EPS)
            rhat = rr * rstd
            gg = g_ref[0, rows, :].astype(f32)
            sg = _sigmoid(gg)
            dd = drn_ref[rows, :].astype(f32)
            drhat = dd * gg * sg
            dout = rstd * (drhat - rhat * jnp.mean(drhat * rhat, axis=-1, keepdims=True))
            dgr = dd * rhat * _dsilu(gg, sg)
            return q, kk, dout.astype(bf16), dgr, cs, sn

        @pl.when(p == 0)
        def _():
            for j in reversed(range(cpt)):
                rows = slice(j * CHUNK, (j + 1) * CHUNK)
                ch = blk(p, n) * cpt + j
                q, kk, doutb, dgr, _, _ = load(rows)
                kept = pl.ds(pl.multiple_of(ch * CHUNK, CHUNK), CHUNK)
                dout_scr[kept, :] = doutb
                dgr_scr[kept, :] = dgr.astype(bf16)
                sb_scr[ch] = st_s[...].astype(bf16)
                gf_scr[ch] = st_g[...].astype(bf16)
                st_s[...] = st_s[...] * cb + _dot_tn((kk * kb).astype(bf16), v_ref[0, rows, :])
                st_g[...] = st_g[...] * cf + _dot_tn((q * af).astype(bf16), doutb)

        @pl.when(p == 1)
        def _():
            for j in range(cpt):
                rows = slice(j * CHUNK, (j + 1) * CHUNK)
                ch = blk(p, n) * cpt + j
                kept = pl.ds(pl.multiple_of(ch * CHUNK, CHUNK), CHUNK)
                doutb = dout_scr[kept, :]
                cs, sn = cos_ref[rows, :], sin_ref[rows, :]
                v = v_ref[0, rows, :]
                qb = q_ref[0, rows, :]
                kkb = k_ref[0, rows, :]
                q = qb.astype(f32)
                kk = kkb.astype(f32)
                sf = st_s[...]
                gb = st_g[...]
                sfb = sf.astype(bf16)
                gbb = gb.astype(bf16)
                sbb = sb_scr[ch]
                gfb = gf_scr[ch]
                dmat = mats_ref[0, 0]
                scores = _dot_nt(qb, kkb)
                dpraw = _dot_nt(doutb, v)
                dpb = (dpraw * dmat).astype(bf16)
                pmb = (scores * dmat).astype(bf16)
                x1 = _dot_nt(doutb, sfb)
                x2 = _dot_nt(doutb, sbb)
                y1 = _dot_nt(v, gfb)
                y2 = _dot_nt(v, gbb)
                kdf = (kk * kf).astype(bf16)
                kdb = (kk * kb).astype(bf16)
                dq = _dot(dpb, kkb) + x1 * af + x2 * ab
                dkk = _dot_tn(dpb, qb) + y1 * kf + y2 * kb
                dv = _dot_tn(pmb, doutb) + _dot(kdf, gfb) + _dot(kdb, gbb)
                ps = dpraw * scores
                acc_af[...] += ps * mats_ref[0, 1]
                acc_ab[...] += ps * mats_ref[0, 2]
                acc_vf[...] += x1 * q * af1 + y1 * kk * kf1
                acc_vb[...] += x2 * q * ab1 + y2 * kk * kb1
                acc_sf[...] += gfb.astype(f32) * sf
                acc_sb[...] += gb * sbb.astype(f32)
                st_s[...] = sf * cf + _dot_tn(kdf, v)
                st_g[...] = gb * cb + _dot_tn((q * ab).astype(bf16), doutb)
                dq_ref[rows, :] = _rot_inv(dq, cs, sn).astype(bf16)
                dk_ref[rows, :] = (_rot_inv(dkk, cs, sn) * scale).astype(bf16)
                dv_ref[rows, :] = dv.astype(bf16)
                dg_ref[rows, :] = dgr_scr[kept, :]

        @pl.when(jnp.logical_and(p == 1, n == nr - 1))
        def _():
            tf = jnp.sum(acc_af[...]) + jnp.sum(acc_vf[...]) + CHUNK * jnp.sum(acc_sf[...] * cf)
            tb = jnp.sum(acc_ab[...]) + jnp.sum(acc_vb[...]) + CHUNK * jnp.sum(acc_sb[...] * cb)
            rid = lax.broadcasted_iota(jnp.int32, (8, 128), 0)
            dlg_ref[0] = jnp.where(rid == 0, tf, jnp.where(rid == 1, tb, 0.0))

    nch = t // CHUNK
    return pl.pallas_call(
        body, name="ret_bwd", grid=(RET_HEADS, 2, nr),
        in_specs=[in_row, in_row, q_spec, k_spec, v_spec, g_spec, tab_spec, tab_spec, cols_spec, mats_spec, cdec_spec],
        out_specs=[out_row, out_row, out_row, out_row, pl.BlockSpec((1, 8, 128), lambda h, p, n: (h, 0, 0))],
        out_shape=[jax.ShapeDtypeStruct((t, d), bf16)] * 4 + [jax.ShapeDtypeStruct((RET_HEADS, 8, 128), f32)],
        scratch_shapes=[pltpu.VMEM((nch, dk, dk), bf16), pltpu.VMEM((nch, dk, dk), bf16),
                        pltpu.VMEM((dk, dk), f32), pltpu.VMEM((dk, dk), f32),
                        pltpu.VMEM((CHUNK, CHUNK), f32), pltpu.VMEM((CHUNK, CHUNK), f32),
                        pltpu.VMEM((CHUNK, dk), f32), pltpu.VMEM((CHUNK, dk), f32),
                        pltpu.VMEM((dk, dk), f32), pltpu.VMEM((dk, dk), f32),
                        pltpu.VMEM((t, dk), bf16), pltpu.VMEM((t, dk), bf16)],
        compiler_params=_cparams(),
    )(drn, r, proj, proj, proj, proj, cos, sin, cols, mats, cdec)


def mix_fwd(a, rn, proj, wa, wb, wo, x1):
    t, d = x1.shape
    tm = _row_tile(t)

    def body(a_ref, rn_ref, p_ref, wa_ref, wb_ref, wo_ref, x_ref, xo_ref, ba_ref, br_ref):
        ba = _dot(a_ref[...], wa_ref[...])
        br = _dot(rn_ref[...], wb_ref[...])
        sa = _sigmoid(p_ref[0, :, 0:d].astype(f32))
        sb = _sigmoid(p_ref[0, :, d:2 * d].astype(f32))
        mix = (sa * ba + sb * br).astype(bf16)
        xo_ref[...] = x_ref[...] + _dot(mix, wo_ref[...])
        ba_ref[...] = ba.astype(bf16)
        br_ref[...] = br.astype(bf16)

    row = pl.BlockSpec((tm, d), lambda i: (i, 0))
    wsp = pl.BlockSpec((d, d), lambda i: (0, 0))
    return pl.pallas_call(
        body, name="mix_fwd", grid=(t // tm,),
        in_specs=[row, row, pl.BlockSpec((1, tm, 2 * d), lambda i: (3, i, 0)), wsp, wsp, wsp, row],
        out_specs=[row, row, row],
        out_shape=[jax.ShapeDtypeStruct((t, d), f32), jax.ShapeDtypeStruct((t, d), bf16), jax.ShapeDtypeStruct((t, d), bf16)],
        compiler_params=_cparams(),
    )(a, rn, proj, wa, wb, wo, x1)


def mix_bwd_act(dx2, ba, br, proj, wa, wb, wo, dep):
    t, d = dx2.shape
    tm = _row_tile(t)

    def body(dx_ref, ba_ref, br_ref, p_ref, wa_ref, wb_ref, wo_ref, dep_ref,
             da_ref, drn_ref, dga_ref, dgb_ref, mix_ref, dba_ref, dbr_ref, dxb_ref):
        dxb = dx_ref[...].astype(bf16)
        dxb_ref[...] = dxb
        dmix = _dot_nt(dxb, wo_ref[...])
        ba = ba_ref[...].astype(f32)
        br = br_ref[...].astype(f32)
        sa = _sigmoid(p_ref[0, :, 0:d].astype(f32))
        sb = _sigmoid(p_ref[0, :, d:2 * d].astype(f32))
        mix_ref[...] = (sa * ba + sb * br).astype(bf16)
        dba = (dmix * sa).astype(bf16)
        dbr = (dmix * sb).astype(bf16)
        dba_ref[...] = dba
        dbr_ref[...] = dbr
        dga_ref[...] = (dmix * ba * sa * (1.0 - sa)).astype(bf16)
        dgb_ref[...] = (dmix * br * sb * (1.0 - sb)).astype(bf16)
        da_ref[...] = _dot_nt(dba, wa_ref[...]).astype(bf16)
        drn_ref[...] = _dot_nt(dbr, wb_ref[...]).astype(bf16)

    row = pl.BlockSpec((tm, d), lambda i: (i, 0))
    wsp = pl.BlockSpec((d, d), lambda i: (0, 0))
    return pl.pallas_call(
        body, name="mix_bwd_act", grid=(t // tm,),
        in_specs=[row, row, row, pl.BlockSpec((1, tm, 2 * d), lambda i: (3, i, 0)), wsp, wsp, wsp, _ANY],
        out_specs=[row] * 8,
        out_shape=[jax.ShapeDtypeStruct((t, d), bf16)] * 8,
        compiler_params=_cparams(),
    )(dx2, ba, br, proj, wa, wb, wo, dep)


def inproj_bwd_act(segs, win, x1, ng, dx2):
    t, d = x1.shape
    s4 = win.shape[0]
    tm = _row_tile(t)
    nseg = len(segs)

    def body(*refs):
        seg_refs = refs[:nseg]
        w_ref, x_ref, ng_ref, dx2_ref, dx1_ref, db_ref, dng_ref = refs[nseg:]
        i = pl.program_id(0)
        dh = None
        for e, sr in enumerate(seg_refs):
            sb = sr[...]
            part = _dot_nt(sb, w_ref[e // 2, :, (e % 2) * d:(e % 2 + 1) * d])
            dh = part if dh is None else dh + part
            _acc_out(db_ref.at[e], i == 0, jnp.sum(sb.astype(f32), axis=0, keepdims=True))
        _, xh, r = _rms(x_ref[...], ng_ref[...])
        dx1_ref[...] = dx2_ref[...] + _rms_bwd(dh, xh, r, ng_ref[...])
        _acc_out(dng_ref, i == 0, jnp.sum(dh * xh, axis=0, keepdims=True))

    row = pl.BlockSpec((tm, d), lambda i: (i, 0))
    vec = pl.BlockSpec((1, d), lambda i: (0, 0))
    return pl.pallas_call(
        body, name="inproj_bwd_act", grid=(t // tm,),
        in_specs=[row] * nseg + [pl.BlockSpec((s4, d, 2 * d), lambda i: (0, 0, 0), pipeline_mode=pl.Buffered(1)),
                                 row, vec, row],
        out_specs=[row, pl.BlockSpec((nseg, 1, d), lambda i: (0, 0, 0)), vec],
        out_shape=[jax.ShapeDtypeStruct((t, d), f32), jax.ShapeDtypeStruct((nseg, 1, d), f32),
                   jax.ShapeDtypeStruct((1, d), f32)],
        compiler_params=_cparams(VMEM_LIMIT_WIDE),
    )(*segs, win, x1, ng, dx2)


def _place():
    return lax.axis_index("x"), lax.axis_index("y"), lax.axis_index("c")


def _other_chips(x, y):
    return [(1 - x, y), (x, 1 - y), (1 - x, 1 - y)]


_ANY = pl.BlockSpec(memory_space=pl.ANY)


_HBM = pl.BlockSpec(memory_space=pltpu.HBM)
_SEM = pl.BlockSpec(memory_space=pltpu.SEMAPHORE)
_EFFECT = pltpu.SideEffectType.DATAFLOW_SIDE_EFFECTING


def _hbm(a):
    return pltpu.with_memory_space_constraint(a, pltpu.HBM)


def _half_rows(ref, c):
    half = ref.shape[1] // 2
    return pl.ds(pl.multiple_of(c * half, 16), half)


def _chip_copy(src, dst, send_sem, recv_sem, chip, c):
    return pltpu.make_async_remote_copy(src_ref=src, dst_ref=dst, send_sem=send_sem, recv_sem=recv_sem,
                                        device_id=(chip[0], chip[1], c), device_id_type=MESH)


def gather_start(bufs, groups, name):
    nb, ng = len(bufs), len(groups)

    def body(*refs):
        ins = refs[:nb]
        sems = refs[nb:nb + 2 * ng]
        token = refs[-1]
        x, y, c = _place()
        k = 2 * x + y
        for gi, grp in enumerate(groups):
            for wi, w in enumerate(grp):
                mine = ins[w].at[k, _half_rows(ins[w], c)]
                for j, chip in enumerate(_other_chips(x, y)):
                    _chip_copy(mine, mine, sems[2 * gi].at[3 * wi + j], sems[2 * gi + 1].at[3 * wi + j], chip, c).start()
        token[...] = jnp.zeros_like(token)

    sem_shapes = []
    for grp in groups:
        sem_shapes += [pltpu.SemaphoreType.DMA((3 * len(grp),)), pltpu.SemaphoreType.DMA((3 * len(grp),))]
    outs = pl.pallas_call(
        body, name=name,
        out_shape=sem_shapes + [pltpu.HBM(b.shape, b.dtype) for b in bufs] + [jax.ShapeDtypeStruct((8, 128), f32)],
        in_specs=[_HBM] * nb,
        out_specs=[_SEM] * (2 * ng) + [_HBM] * nb + [pl.BlockSpec(memory_space=pltpu.VMEM)],
        input_output_aliases={w: 2 * ng + w for w in range(nb)},
        compiler_params=pltpu.CompilerParams(has_side_effects=_EFFECT),
    )(*[_hbm(b) for b in bufs])
    sems = [(outs[2 * gi], outs[2 * gi + 1]) for gi in range(ng)]
    return sems, list(outs[2 * ng:2 * ng + nb]), outs[-1]


def gather_wait(bufs, sems, after, name):
    n = len(bufs)

    def body(*refs):
        ins = refs[:n]
        send_sems, recv_sems = refs[n], refs[n + 1]
        x, y, c = _place()
        k = 2 * x + y
        for wi in range(n):
            half = _half_rows(ins[wi], c)
            for j, chip in enumerate(_other_chips(x, y)):
                cp = _chip_copy(ins[wi].at[k, half], ins[wi].at[2 * chip[0] + chip[1], half], send_sems.at[3 * wi + j],
                                recv_sems.at[3 * wi + j], chip, c)
                cp.wait_send()
                cp.wait_recv()

    outs = pl.pallas_call(
        body, name=name,
        out_shape=[pltpu.HBM(b.shape, b.dtype) for b in bufs],
        in_specs=[_HBM] * n + [_SEM, _SEM, _ANY],
        out_specs=[_HBM] * n,
        input_output_aliases={i: i for i in range(n)},
        compiler_params=pltpu.CompilerParams(has_side_effects=_EFFECT),
    )(*bufs, sems[0], sems[1], after)
    return list(outs)


def gather_forward(bufs, name):
    n = len(bufs)

    def body(*refs):
        ins = refs[n:2 * n]
        send_sems, recv_sems = refs[2 * n], refs[2 * n + 1]
        x, y, c = _place()
        copies = []
        for wi in range(n):
            for j, chip in enumerate(_other_chips(x, y)):
                kp = 2 * chip[0] + chip[1]
                got = ins[wi].at[kp, _half_rows(ins[wi], c)]
                cp = pltpu.make_async_remote_copy(
                    src_ref=got, dst_ref=got, send_sem=send_sems.at[3 * wi + j], recv_sem=recv_sems.at[3 * wi + j],
                    device_id=(x, y, 1 - c), device_id_type=MESH)
                cp.start()
                copies.append((cp, wi, kp, j))
        for cp, wi, kp, j in copies:
            cp.wait_send()
            theirs = ins[wi].at[kp, _half_rows(ins[wi], 1 - c)]
            pltpu.make_async_remote_copy(
                src_ref=theirs, dst_ref=theirs, send_sem=send_sems.at[3 * wi + j], recv_sem=recv_sems.at[3 * wi + j],
                device_id=(x, y, 1 - c), device_id_type=MESH).wait_recv()

    outs = pl.pallas_call(
        body, name=name,
        out_shape=[jax.ShapeDtypeStruct(b.shape, b.dtype) for b in bufs],
        in_specs=[_ANY] * n, out_specs=[_ANY] * n,
        input_output_aliases={i: i for i in range(n)},
        scratch_shapes=[pltpu.SemaphoreType.DMA((3 * n,)), pltpu.SemaphoreType.DMA((3 * n,))],
    )(*bufs)
    return list(outs)


def forward_start(bufs, name):
    n = len(bufs)

    def body(*refs):
        x, y, c = _place()
        for wi in range(n):
            for j, chip in enumerate(_other_chips(x, y)):
                got = refs[wi].at[2 * chip[0] + chip[1], _half_rows(refs[wi], c)]
                _sibling_copy(got, got, refs[n].at[3 * wi + j], refs[n + 1].at[3 * wi + j]).start()
        refs[-1][...] = jnp.zeros_like(refs[-1])

    return _split_start(body, name, 3 * n, list(bufs))


def forward_wait(bufs, sems, after, name):
    n = len(bufs)

    def body(*refs):
        x, y, c = _place()
        for wi in range(n):
            for j, chip in enumerate(_other_chips(x, y)):
                kp = 2 * chip[0] + chip[1]
                got = refs[wi].at[kp, _half_rows(refs[wi], c)]
                theirs = refs[wi].at[kp, _half_rows(refs[wi], 1 - c)]
                _sibling_copy(got, got, refs[n].at[3 * wi + j], refs[n + 1].at[3 * wi + j]).wait_send()
                _sibling_copy(theirs, theirs, refs[n].at[3 * wi + j], refs[n + 1].at[3 * wi + j]).wait_recv()

    return _split_wait(body, name, list(bufs), sems, after)


def exchange_start(grads, name):
    n = len(grads)
    lands = [lax.empty((3,) + g.shape[1:], g.dtype) for g in grads]

    def body(*refs):
        ins = refs[:n]
        land = refs[n:2 * n]
        send_sems, recv_sems = refs[2 * n], refs[2 * n + 1]
        token = refs[-1]
        x, y, c = _place()
        for wi in range(n):
            for j, chip in enumerate(_other_chips(x, y)):
                _chip_copy(ins[wi].at[2 * chip[0] + chip[1]], land[wi].at[j], send_sems.at[3 * wi + j],
                           recv_sems.at[3 * wi + j], chip, c).start()
        token[...] = jnp.zeros_like(token)

    outs = pl.pallas_call(
        body, name=name,
        out_shape=[pltpu.SemaphoreType.DMA((3 * n,)), pltpu.SemaphoreType.DMA((3 * n,))]
        + [pltpu.HBM(g.shape, g.dtype) for g in grads] + [pltpu.HBM(l.shape, l.dtype) for l in lands]
        + [jax.ShapeDtypeStruct((8, 128), f32)],
        in_specs=[_HBM] * (2 * n),
        out_specs=[_SEM, _SEM] + [_HBM] * (2 * n) + [pl.BlockSpec(memory_space=pltpu.VMEM)],
        input_output_aliases={i: 2 + i for i in range(2 * n)},
        compiler_params=pltpu.CompilerParams(has_side_effects=_EFFECT),
    )(*[_hbm(g) for g in grads], *[_hbm(l) for l in lands])
    return (outs[0], outs[1]), list(outs[2:2 + n]), list(outs[2 + n:2 + 2 * n]), outs[-1]


def exchange_wait(grads, lands, sems, after, name):
    n = len(grads)

    def body(*refs):
        ins = refs[:n]
        land = refs[n:2 * n]
        send_sems, recv_sems = refs[2 * n], refs[2 * n + 1]
        x, y, c = _place()
        for wi in range(n):
            for j, chip in enumerate(_other_chips(x, y)):
                cp = _chip_copy(ins[wi].at[2 * chip[0] + chip[1]], land[wi].at[j], send_sems.at[3 * wi + j],
                                recv_sems.at[3 * wi + j], chip, c)
                cp.wait_send()
                cp.wait_recv()

    outs = pl.pallas_call(
        body, name=name,
        out_shape=[pltpu.HBM(g.shape, g.dtype) for g in grads] + [pltpu.HBM(l.shape, l.dtype) for l in lands],
        in_specs=[_HBM] * (2 * n) + [_SEM, _SEM, _ANY],
        out_specs=[_HBM] * (2 * n),
        input_output_aliases={i: i for i in range(2 * n)},
        compiler_params=pltpu.CompilerParams(has_side_effects=_EFFECT),
    )(*grads, *lands, sems[0], sems[1], after)
    return list(outs[:n]), list(outs[n:])


def _split_start(body, name, n_sems, operands):
    n = len(operands)
    outs = pl.pallas_call(
        body, name=name,
        out_shape=[pltpu.SemaphoreType.DMA((n_sems,)), pltpu.SemaphoreType.DMA((n_sems,))]
        + [pltpu.HBM(o.shape, o.dtype) for o in operands] + [jax.ShapeDtypeStruct((8, 128), f32)],
        in_specs=[_HBM] * n,
        out_specs=[_SEM, _SEM] + [_HBM] * n + [pl.BlockSpec(memory_space=pltpu.VMEM)],
        input_output_aliases={i: 2 + i for i in range(n)},
        compiler_params=pltpu.CompilerParams(has_side_effects=_EFFECT),
    )(*[_hbm(o) for o in operands])
    return (outs[0], outs[1]), list(outs[2:2 + n]), outs[-1]


def _split_wait(body, name, operands, sems, after):
    n = len(operands)
    outs = pl.pallas_call(
        body, name=name,
        out_shape=[pltpu.HBM(o.shape, o.dtype) for o in operands],
        in_specs=[_HBM] * n + [_SEM, _SEM, _ANY],
        out_specs=[_HBM] * n,
        input_output_aliases={i: i for i in range(n)},
        compiler_params=pltpu.CompilerParams(has_side_effects=_EFFECT),
    )(*operands, sems[0], sems[1], after)
    return list(outs)


def _sibling_copy(src, dst, send_sem, recv_sem):
    x, y, c = _place()
    return pltpu.make_async_remote_copy(src_ref=src, dst_ref=dst, send_sem=send_sem, recv_sem=recv_sem,
                                        device_id=(x, y, 1 - c), device_id_type=MESH)


def swap_start(parts, name):
    n = len(parts)

    def body(*refs):
        for w in range(n):
            _sibling_copy(refs[w], refs[n + w], refs[2 * n].at[w], refs[2 * n + 1].at[w]).start()
        refs[-1][...] = jnp.zeros_like(refs[-1])

    sems, ops, token = _split_start(body, name, n, list(parts) + [lax.empty(p.shape, p.dtype) for p in parts])
    return sems, ops[:n], ops[n:], token


def swap_wait(parts, lands, sems, after, name):
    n = len(parts)

    def body(*refs):
        for w in range(n):
            cp = _sibling_copy(refs[w], refs[n + w], refs[2 * n].at[w], refs[2 * n + 1].at[w])
            cp.wait_send()
            cp.wait_recv()

    outs = _split_wait(body, name, list(parts) + list(lands), sems, after)
    return outs[:n], outs[n:]


def _all_peers(x, y, c):
    return [(1 - x if m & 4 else x, 1 - y if m & 2 else y, 1 - c if m & 1 else c) for m in range(1, N_DEV)]


def small_start(block):
    land = jnp.broadcast_to(block[None], (N_DEV,) + block.shape)

    def body(b_ref, land_ref, send_sems, recv_sems, b_thru, land_thru, token):
        x, y, c = _place()
        me = 4 * x + 2 * y + c
        for m, peer in enumerate(_all_peers(x, y, c)):
            pltpu.make_async_remote_copy(src_ref=b_ref, dst_ref=land_ref.at[me], send_sem=send_sems.at[m],
                                         recv_sem=recv_sems.at[m], device_id=peer, device_id_type=MESH).start()
        token[...] = jnp.zeros_like(token)

    sems, ops, token = _split_start(body, "small_start", N_DEV - 1, [block, land])
    return sems, ops[0], ops[1], token


def small_wait(block, land, sems, after):
    def body(b_ref, land_ref, send_sems, recv_sems, after_ref, b_thru, land_thru):
        x, y, c = _place()
        for m, (px, py, pc) in enumerate(_all_peers(x, y, c)):
            cp = pltpu.make_async_remote_copy(src_ref=b_ref, dst_ref=land_ref.at[4 * px + 2 * py + pc],
                                              send_sem=send_sems.at[m], recv_sem=recv_sems.at[m],
                                              device_id=(px, py, pc), device_id_type=MESH)
            cp.wait_send()
            cp.wait_recv()

    return _split_wait(body, "small_wait", [block, land], sems, after)[1]


def _adamw(w, g, m, v):
    m = ADAM_B1 * m + (1.0 - ADAM_B1) * g
    v = ADAM_B2 * v + (1.0 - ADAM_B2) * (g * g)
    m_hat = m / (1.0 - ADAM_B1 ** ADAM_STEP)
    v_hat = v / (1.0 - ADAM_B2 ** ADAM_STEP)
    delta = -ADAM_LR * (m_hat / (jnp.sqrt(v_hat) + ADAM_EPS) + ADAM_WD * w)
    return delta, m, v


EW_BLOCK_BYTES = 2 * 1024 * 1024


def _ew_tile(rows, cols):
    for cand in (512, 352, 256, 176, 128, 64, 32, 16, 8):
        if rows % cand == 0 and cand * cols * 4 <= EW_BLOCK_BYTES:
            return cand
    return rows


def sum_partials(chip, own, land, name):
    _, r, c = own.shape
    tr = _ew_tile(r, c)

    def body(k_ref, own_ref, p_ref, o_ref):
        o_ref[...] = ((own_ref[0].astype(f32) + p_ref[0].astype(f32)) + p_ref[1].astype(f32)) + p_ref[2].astype(f32)

    return pl.pallas_call(
        body, name=name,
        grid_spec=pltpu.PrefetchScalarGridSpec(
            num_scalar_prefetch=1, grid=(r // tr,),
            in_specs=[pl.BlockSpec((1, tr, c), lambda i, k: (k[0], i, 0)), pl.BlockSpec((3, tr, c), lambda i, k: (0, i, 0))],
            out_specs=pl.BlockSpec((tr, c), lambda i, k: (i, 0))),
        out_shape=jax.ShapeDtypeStruct((r, c), f32),
        compiler_params=_cparams(),
    )(chip, own, land)


def adamw_shard(p_mine, p_sibling, w, m, v, name):
    r, c = w.shape
    tr = _ew_tile(r, c)

    def body(a_ref, b_ref, w_ref, m_ref, v_ref, g_ref, d_ref, mo_ref, vo_ref):
        g = a_ref[...] + b_ref[...]
        delta, mn, vn = _adamw(w_ref[...], g, m_ref[...], v_ref[...])
        g_ref[...] = g
        d_ref[...] = delta
        mo_ref[...] = mn
        vo_ref[...] = vn

    blk = pl.BlockSpec((tr, c), lambda i: (i, 0))
    return pl.pallas_call(
        body, name=name, grid=(r // tr,),
        in_specs=[blk] * 5, out_specs=[blk] * 4,
        out_shape=[jax.ShapeDtypeStruct((r, c), f32)] * 4,
        compiler_params=_cparams(),
    )(p_mine, p_sibling, w, m, v)


def adamw_small(g8, w, m, v):
    _, r, lanes = g8.shape

    def body(g_ref, w_ref, m_ref, v_ref, go_ref, d_ref, mo_ref, vo_ref):
        g = g_ref[0]
        for i in range(1, N_DEV):
            g = g + g_ref[i]
        delta, mn, vn = _adamw(w_ref[...], g, m_ref[...], v_ref[...])
        go_ref[...] = g
        d_ref[...] = delta
        mo_ref[...] = mn
        vo_ref[...] = vn

    return pl.pallas_call(
        body, name="adamw_small",
        out_shape=[jax.ShapeDtypeStruct((r, lanes), f32)] * 4,
        compiler_params=_cparams(),
    )(g8, w, m, v)


def _size(shape):
    n = 1
    for e in shape:
        n *= e
    return n


def _pack_rows(shapes):
    rows = [-(-_size(s) // 1024) * 8 for s in shapes]
    return rows, sum(rows)


def _pack(arrs, shapes):
    rows, _ = _pack_rows(shapes)
    parts = [jnp.pad(a.reshape(-1).astype(f32), (0, r * 128 - _size(s))).reshape(r, 128)
             for a, s, r in zip(arrs, shapes, rows)]
    return jnp.concatenate(parts, axis=0)


def _unpack(block, shapes):
    rows, _ = _pack_rows(shapes)
    out, off = [], 0
    for s, r in zip(shapes, rows):
        out.append(block[off:off + r].reshape(-1)[:_size(s)].reshape(s))
        off += r
    return out


TRANSPOSED = ("ffn1_w_gate", "ffn1_w_up", "ffn2_w_gate", "ffn2_w_up")


def _shard2d(a, n):
    return a[0].T if n in TRANSPOSED else a[0]


def _unshard(a, n):
    return (a.T if n in TRANSPOSED else a)[None]


BIG = ("ffn1_w_gate", "ffn1_w_up", "ffn1_w_down", "w_in", "w_branch_a", "w_branch_b", "w_out",
       "ffn2_w_gate", "ffn2_w_up", "ffn2_w_down")
SMALL = ("ffn1_norm", "mix_norm", "b_in", "sgu_norm_g", "sgu_norm_b", "sgu_w_s", "sgu_b_s", "ret_decay_logit",
         "ffn2_norm", "final_norm")
WEIGHTS = ("ffn1_norm", "ffn1_w_gate", "ffn1_w_up", "ffn1_w_down", "mix_norm", "w_in", "b_in", "sgu_norm_g",
           "sgu_norm_b", "sgu_w_s", "sgu_b_s", "ret_decay_logit", "w_branch_a", "w_branch_b", "w_out", "ffn2_norm",
           "ffn2_w_gate", "ffn2_w_up", "ffn2_w_down", "final_norm")


def kernel(x, ffn1_norm, ffn1_w_gate, ffn1_w_up, ffn1_w_down, mix_norm, w_in, b_in, sgu_norm_g, sgu_norm_b, sgu_w_s, sgu_b_s, ret_decay_logit, w_branch_a, w_branch_b, w_out, ffn2_norm, ffn2_w_gate, ffn2_w_up, ffn2_w_down, final_norm, loss_target, m_ffn1_norm, m_ffn1_w_gate, m_ffn1_w_up, m_ffn1_w_down, m_mix_norm, m_w_in, m_b_in, m_sgu_norm_g, m_sgu_norm_b, m_sgu_w_s, m_sgu_b_s, m_ret_decay_logit, m_w_branch_a, m_w_branch_b, m_w_out, m_ffn2_norm, m_ffn2_w_gate, m_ffn2_w_up, m_ffn2_w_down, m_final_norm, v_ffn1_norm, v_ffn1_w_gate, v_ffn1_w_up, v_ffn1_w_down, v_mix_norm, v_w_in, v_b_in, v_sgu_norm_g, v_sgu_norm_b, v_sgu_w_s, v_sgu_b_s, v_ret_decay_logit, v_w_branch_a, v_w_branch_b, v_w_out, v_ffn2_norm, v_ffn2_w_gate, v_ffn2_w_up, v_ffn2_w_down, v_final_norm):
    p = dict(ffn1_norm=ffn1_norm, ffn1_w_gate=ffn1_w_gate, ffn1_w_up=ffn1_w_up, ffn1_w_down=ffn1_w_down,
             mix_norm=mix_norm, w_in=w_in, b_in=b_in, sgu_norm_g=sgu_norm_g, sgu_norm_b=sgu_norm_b, sgu_w_s=sgu_w_s,
             sgu_b_s=sgu_b_s, ret_decay_logit=ret_decay_logit, w_branch_a=w_branch_a, w_branch_b=w_branch_b,
             w_out=w_out, ffn2_norm=ffn2_norm, ffn2_w_gate=ffn2_w_gate, ffn2_w_up=ffn2_w_up, ffn2_w_down=ffn2_w_down,
             final_norm=final_norm)
    mom = dict(ffn1_norm=m_ffn1_norm, ffn1_w_gate=m_ffn1_w_gate, ffn1_w_up=m_ffn1_w_up, ffn1_w_down=m_ffn1_w_down,
               mix_norm=m_mix_norm, w_in=m_w_in, b_in=m_b_in, sgu_norm_g=m_sgu_norm_g, sgu_norm_b=m_sgu_norm_b,
               sgu_w_s=m_sgu_w_s, sgu_b_s=m_sgu_b_s, ret_decay_logit=m_ret_decay_logit, w_branch_a=m_w_branch_a,
               w_branch_b=m_w_branch_b, w_out=m_w_out, ffn2_norm=m_ffn2_norm, ffn2_w_gate=m_ffn2_w_gate,
               ffn2_w_up=m_ffn2_w_up, ffn2_w_down=m_ffn2_w_down, final_norm=m_final_norm)
    var = dict(ffn1_norm=v_ffn1_norm, ffn1_w_gate=v_ffn1_w_gate, ffn1_w_up=v_ffn1_w_up, ffn1_w_down=v_ffn1_w_down,
               mix_norm=v_mix_norm, w_in=v_w_in, b_in=v_b_in, sgu_norm_g=v_sgu_norm_g, sgu_norm_b=v_sgu_norm_b,
               sgu_w_s=v_sgu_w_s, sgu_b_s=v_sgu_b_s, ret_decay_logit=v_ret_decay_logit, w_branch_a=v_w_branch_a,
               w_branch_b=v_w_branch_b, w_out=v_w_out, ffn2_norm=v_ffn2_norm, ffn2_w_gate=v_ffn2_w_gate,
               ffn2_w_up=v_ffn2_w_up, ffn2_w_down=v_ffn2_w_down, final_norm=v_final_norm)

    xs = x[0]
    tgt = loss_target[0]
    t, d = xs.shape
    dk = d // RET_HEADS
    tm = _row_tile(t)

    shards2d = {n: _shard2d(p[n], n) for n in BIG}
    chip = (2 * lax.axis_index("x") + lax.axis_index("y")).astype(jnp.int32).reshape(1)
    groups = {"ffn1": ("ffn1_w_gate", "ffn1_w_up", "ffn1_w_down"), "in": ("w_in",),
              "mix": ("w_branch_a", "w_branch_b", "w_out"), "ffn2": ("ffn2_w_gate", "ffn2_w_up", "ffn2_w_down")}
    def own_slot(n, zero):
        sh = shards2d[n].astype(bf16) + zero
        return lax.dynamic_update_index_in_dim(lax.empty((N_CHIPS,) + sh.shape, bf16), sh, chip[0], 0)

    sems, bufs, tok = gather_start([own_slot(n, jnp.zeros((), bf16)) for n in groups["ffn1"]], [[0, 1, 2]],
                                   "gather_start_ffn1")
    gsem = {"ffn1": sems[0]}
    pending = dict(zip(groups["ffn1"], bufs))
    rest = [n for g in ("in", "mix", "ffn2") for n in groups[g]]
    sems, bufs, tok_rest = gather_start([own_slot(n, tok[0, 0].astype(bf16)) for n in rest],
                                 [[rest.index(n) for n in groups[g]] for g in ("in", "mix", "ffn2")], "gather_start_rest")
    gsem.update(zip(("in", "mix", "ffn2"), sems))
    pending.update(zip(rest, bufs))

    def arrive(gs, after):
        got = []
        for g in gs:
            got += gather_wait([pending[n] for n in groups[g]], gsem[g], after, "gather_wait_" + g)
        return gather_forward(got, "gather_forward_" + gs[0])

    bin4 = b_in.reshape(N_CHIPS, 1, 2 * d)
    ws_b = sgu_w_s[0].astype(bf16)
    bs_c = sgu_b_s[0][:, :, None]
    cols, mats, cdec, cos, sin = retention_constants(ret_decay_logit[0], t, dk, tok_rest[0, 0])

    wg1, wu1, wd1 = [_pair_shards(w) for w in arrive(["ffn1"], cos)]
    x1, g1, u1 = ffn_fwd(xs, ffn1_norm, wg1, wu1, wd1, "ffn1_fwd")
    win, = arrive(["in"], x1)
    proj, hb2 = inproj_fwd(x1, mix_norm, win, bin4, cos, sin)
    late = []
    for g in ("mix", "ffn2"):
        late += gather_wait([pending[n] for n in groups[g]], gsem[g], proj, "gather_wait_" + g)
    fsems, late, ftok = forward_start(late, "forward_start_mix")
    a = sgu_fwd(proj, sgu_norm_g, sgu_norm_b, ws_b, bs_c, ftok)
    r, rn = ret_fwd(proj, cols, mats, cdec)
    wa, wb, wo, wg2, wu2, wd2 = forward_wait(late, fsems, rn, "forward_wait_mix")
    wa, wb, wo = [w.reshape(d, d) for w in (wa, wb, wo)]
    wg2, wu2, wd2 = [_pair_shards(w) for w in (wg2, wu2, wd2)]
    x2, ba, br = mix_fwd(a, rn, proj, wa, wb, wo, x1)
    loss_blk, dx3, d_final, g2, u2 = ffn_fwd_loss(x2, ffn2_norm, wg2, wu2, wd2, final_norm.reshape(1, d), tgt, "ffn2_fwd")

    sent = {}
    dx2, dg2, du2, act2, hb3, dyb2, d_ffn2n = ffn_bwd_act(dx3, x2, ffn2_norm, g2, u2, wg2, wu2, wd2, "ffn2_bwd_act", tok)
    sent["ffn2"] = exchange_start(ffn_weight_grads(hb3, dyb2, dg2, du2, act2, "ffn2_grad", tok), "exchange_start_ffn2")
    da, drn, dga, dgb, mixb, dba, dbr, dx2b = mix_bwd_act(dx2, ba, br, proj, wa, wb, wo, sent["ffn2"][3])
    tg = min(t, 2048)
    row = pl.BlockSpec((tg, d), lambda s, i: (i, 0))

    def square_grad(xa, ya, name):
        return tn_matmul(xa, [ya], row, [row], 1, d, [d], t, tg, name, tok).reshape(N_CHIPS, d // N_CHIPS, d)

    sent["mix"] = exchange_start([square_grad(a, dba, "grad_w_branch_a"), square_grad(rn, dbr, "grad_w_branch_b"),
                                  square_grad(mixb, dx2b, "grad_w_out")], "exchange_start_mix")
    dua, dva, d_ws, d_bs, d_sng, d_snb = sgu_bwd(da, proj, sgu_norm_g, sgu_norm_b, ws_b, bs_c, sent["mix"][3])
    dq, dkr, dv, dgr, dlg = ret_bwd(drn, r, proj, cols, mats, cdec, cos, sin)
    segs = [dua, dva, dq, dkr, dv, dgr, dga, dgb]
    dx1, d_bin, d_mixn = inproj_bwd_act(segs, win, x1, mix_norm, dx2)
    sent["in"] = exchange_start([jnp.concatenate(
        [tn_matmul(hb2, [segs[2 * s], segs[2 * s + 1]], row, [row, row], 1, d, [d, d], t, tg, "grad_w_in_%d" % s, tok)
         for s in range(N_CHIPS)], axis=0)], "exchange_start_in")
    grad_x, dg1, du1, act1, hb1, dyb1, d_ffn1n = ffn_bwd_act(dx1, xs, ffn1_norm, g1, u1, wg1, wu1, wd1, "ffn1_bwd_act",
                                                              sent["in"][3])
    dlogit = dlg[:, 0:2, 0].T * jax.nn.sigmoid(-ret_decay_logit[0].astype(f32))
    small_g = dict(ffn1_norm=d_ffn1n, mix_norm=d_mixn, b_in=d_bin, sgu_norm_g=d_sng, sgu_norm_b=d_snb, sgu_w_s=d_ws,
                   sgu_b_s=d_bs, ret_decay_logit=dlogit, ffn2_norm=d_ffn2n, final_norm=d_final)
    shapes = [p[n].shape for n in SMALL]
    small_sems, small_blk, small_land, small_tok = small_start(_pack([small_g[n] for n in SMALL], shapes))

    def send_one(which, grad):
        n = "ffn1_" + which
        groups[n] = (n,)
        sent[n] = exchange_start([grad], "exchange_start_" + n)
        return sent[n][3]

    ffn_weight_grads(hb1, dyb1, dg1, du1, act1, "ffn1_grad", small_tok, send_one)

    out_g, out_d, out_m, out_v = {}, {}, {}, {}
    swaps = {}

    def reduce_plane(g, after):
        gsems, own, lands, _ = sent[g]
        own, lands = exchange_wait(own, lands, gsems, after, "exchange_wait_" + g)
        plane = [sum_partials(chip, o, l, "sum_" + n) for n, o, l in zip(groups[g], own, lands)]
        swaps[g] = swap_start(plane, "swap_start_" + g)
        return swaps[g][3]

    def update(g, after):
        ssems, plane, lands, _ = swaps[g]
        plane, other = swap_wait(plane, lands, ssems, after, "swap_wait_" + g)
        for n, mine, sib in zip(groups[g], plane, other):
            res = adamw_shard(mine, sib, shards2d[n], _shard2d(mom[n], n), _shard2d(var[n], n), "adamw_" + n)
            out_g[n], out_d[n], out_m[n], out_v[n] = [_unshard(o, n) for o in res]
        return out_g[groups[g][-1]]

    after = reduce_plane("ffn2", sent["ffn1_w_down"][3])
    after = reduce_plane("mix", after)
    after = update("ffn2", after)
    after = reduce_plane("in", after)
    after = update("mix", after)
    g8 = small_wait(small_blk, small_land, small_sems, after)
    sg, sd, sm, sv = adamw_small(g8, _pack([p[n] for n in SMALL], shapes), _pack([mom[n] for n in SMALL], shapes),
                                 _pack([var[n] for n in SMALL], shapes))
    for res, blockv in ((out_g, sg), (out_d, sd), (out_m, sm), (out_v, sv)):
        for n, val in zip(SMALL, _unpack(blockv, shapes)):
            res[n] = val
    after = update("in", sg)
    after = reduce_plane("ffn1_w_gate", after)
    after = reduce_plane("ffn1_w_up", after)
    after = update("ffn1_w_gate", after)
    after = reduce_plane("ffn1_w_down", after)
    after = update("ffn1_w_up", after)
    update("ffn1_w_down", after)

    loss = lax.psum(loss_blk[0, 0], ("x", "y", "c"))
    return (loss, grad_x[None], *[out_g[n] for n in WEIGHTS], *[out_d[n] for n in WEIGHTS],
            *[out_m[n] for n in WEIGHTS], *[out_v[n] for n in WEIGHTS])
```

```python
import functools

import jax
import jax.numpy as jnp
from jax import lax
from jax.experimental import pallas as pl
from jax.experimental.pallas import tpu as pltpu

f32 = jnp.float32
bf16 = jnp.bfloat16

SGU_CHUNK = 128
CHUNK = 128
RET_HEADS = 4
SGU_GROUPS = 4
ROPE_BASE = 10000.0
NORM_EPS = 1e-6
ADAM_LR = 0.001
ADAM_B1 = 0.9
ADAM_B2 = 0.999
ADAM_EPS = 1e-08
ADAM_WD = 0.01
ADAM_STEP = 10
N_CHIPS = 4
N_DEV = 8
MESH = pl.DeviceIdType.MESH
VMEM_LIMIT = 52 * 1024 * 1024
VMEM_LIMIT_WIDE = 62 * 1024 * 1024

_NT = (((1,), (1,)), ((), ()))
_TN = (((0,), (0,)), ((), ()))


def _cparams(limit=None):
    return pltpu.CompilerParams(vmem_limit_bytes=VMEM_LIMIT if limit is None else limit)


def _row_tile(t):
    return 512 if t >= 2048 else t // 2


def _dot(a, b):
    return jnp.dot(a, b, preferred_element_type=f32)


def _dot_nt(a, b):
    return lax.dot_general(a, b, _NT, preferred_element_type=f32)


def _dot_tn(a, b):
    return lax.dot_general(a, b, _TN, preferred_element_type=f32)


def _rms(x, g):
    r = lax.rsqrt(jnp.mean(x * x, axis=-1, keepdims=True) + NORM_EPS)
    xh = x * r
    return xh * g, xh, r


def _rms_bwd(dy, xh, r, g):
    dxh = dy * g
    return r * (dxh - xh * jnp.mean(dxh * xh, axis=-1, keepdims=True))


def _sigmoid(x):
    return jax.nn.sigmoid(x)


def _dsilu(g, sg):
    return sg * (1.0 + g * (1.0 - sg))


def _gelu(x):
    return 0.5 * x * (1.0 + lax.erf(x * 0.7071067811865476))


def _dgelu(x):
    return 0.5 * (1.0 + lax.erf(x * 0.7071067811865476)) + x * jnp.exp(-0.5 * x * x) * 0.3989422804014327


def _acc_out(ref, first, val):
    @pl.when(first)
    def _():
        ref[...] = val

    @pl.when(jnp.logical_not(first))
    def _():
        ref[...] += val


def _ffn_tile(t):
    return 256 if t >= 2048 else t // 2


def _ffn_fwd_rows(xx, ng_ref, wg_ref, wu_ref, wd_ref, g_ref, u_ref):
    y, _, _ = _rms(xx, ng_ref[...])
    h = y.astype(bf16)
    acc = None
    for s in range(wg_ref.shape[0]):
        g = _dot_nt(h, wg_ref[s])
        u = _dot_nt(h, wu_ref[s])
        g_ref[s] = g.astype(bf16)
        u_ref[s] = u.astype(bf16)
        part = _dot((g * _sigmoid(g) * u).astype(bf16), wd_ref[s])
        acc = part if acc is None else acc + part
    return xx + 0.5 * acc


def ffn_fwd(x, ng, wg, wu, wd, name):
    t, d = x.shape
    ns, fs, _ = wg.shape
    tm = _ffn_tile(t)

    def body(x_ref, ng_ref, wg_ref, wu_ref, wd_ref, xo_ref, g_ref, u_ref):
        xo_ref[...] = _ffn_fwd_rows(x_ref[...], ng_ref, wg_ref, wu_ref, wd_ref, g_ref, u_ref)

    row = pl.BlockSpec((tm, d), lambda i: (i, 0))
    shard = pl.BlockSpec((ns, tm, fs), lambda i: (0, i, 0))
    wspec = pl.BlockSpec((ns, fs, d), lambda i: (0, 0, 0), pipeline_mode=pl.Buffered(1))
    return pl.pallas_call(
        body, name=name, grid=(t // tm,),
        in_specs=[row, pl.BlockSpec((1, d), lambda i: (0, 0)), wspec, wspec, wspec],
        out_specs=[row, shard, shard],
        out_shape=[jax.ShapeDtypeStruct((t, d), f32), jax.ShapeDtypeStruct((ns, t, fs), bf16),
                   jax.ShapeDtypeStruct((ns, t, fs), bf16)],
        compiler_params=_cparams(),
    )(x, ng, wg, wu, wd)


def ffn_fwd_loss(x, ng, wg, wu, wd, fng, tgt, name):
    t, d = x.shape
    ns, fs, _ = wg.shape
    tm = _ffn_tile(t)

    def body(x_ref, ng_ref, wg_ref, wu_ref, wd_ref, fng_ref, t_ref, loss_ref, dx_ref, dfn_ref, g_ref, u_ref):
        i = pl.program_id(0)
        x3 = _ffn_fwd_rows(x_ref[...], ng_ref, wg_ref, wu_ref, wd_ref, g_ref, u_ref)
        y, xh, r = _rms(x3, fng_ref[...])
        diff = y - t_ref[...]
        part = 0.5 * jnp.sum(jnp.sum(diff * diff, axis=0, keepdims=True), axis=1, keepdims=True) / d
        _acc_out(loss_ref, i == 0, jnp.broadcast_to(part, (1, 128)))
        dy = diff * (1.0 / d)
        dx_ref[...] = _rms_bwd(dy, xh, r, fng_ref[...])
        _acc_out(dfn_ref, i == 0, jnp.sum(dy * xh, axis=0, keepdims=True))

    row = pl.BlockSpec((tm, d), lambda i: (i, 0))
    vec = pl.BlockSpec((1, d), lambda i: (0, 0))
    shard = pl.BlockSpec((ns, tm, fs), lambda i: (0, i, 0))
    wspec = pl.BlockSpec((ns, fs, d), lambda i: (0, 0, 0), pipeline_mode=pl.Buffered(1))
    return pl.pallas_call(
        body, name=name, grid=(t // tm,),
        in_specs=[row, vec, wspec, wspec, wspec, vec, row],
        out_specs=[pl.BlockSpec((1, 128), lambda i: (0, 0)), row, vec, shard, shard],
        out_shape=[jax.ShapeDtypeStruct((1, 128), f32), jax.ShapeDtypeStruct((t, d), f32), jax.ShapeDtypeStruct((1, d), f32),
                   jax.ShapeDtypeStruct((ns, t, fs), bf16), jax.ShapeDtypeStruct((ns, t, fs), bf16)],
        compiler_params=_cparams(),
    )(x, ng, wg, wu, wd, fng, tgt)


def ffn_bwd_act(dxo, x, ng, g, u, wg, wu, wd, name, dep):
    t, d = x.shape
    ns, fs, _ = wg.shape
    tm = _ffn_tile(t)

    def body(dxo_ref, x_ref, ng_ref, g_ref, u_ref, wg_ref, wu_ref, wd_ref, dep_ref,
             dx_ref, dg_ref, du_ref, act_ref, hb_ref, dyb_ref, dng_ref):
        i = pl.program_id(0)
        dxo = dxo_ref[...]
        dyb = (0.5 * dxo).astype(bf16)
        dyb_ref[...] = dyb
        dh = None
        for s in range(ns):
            dact = _dot_nt(dyb, wd_ref[s])
            gg = g_ref[s].astype(f32)
            uu = u_ref[s].astype(f32)
            sg = _sigmoid(gg)
            sil = gg * sg
            dgb = (dact * uu * _dsilu(gg, sg)).astype(bf16)
            dub = (dact * sil).astype(bf16)
            dg_ref[s] = dgb
            du_ref[s] = dub
            act_ref[s] = (sil * uu).astype(bf16)
            part = _dot(dgb, wg_ref[s]) + _dot(dub, wu_ref[s])
            dh = part if dh is None else dh + part
        y, xh, r = _rms(x_ref[...], ng_ref[...])
        hb_ref[...] = y.astype(bf16)
        dx_ref[...] = dxo + _rms_bwd(dh, xh, r, ng_ref[...])
        _acc_out(dng_ref, i == 0, jnp.sum(dh * xh, axis=0, keepdims=True))

    row = pl.BlockSpec((tm, d), lambda i: (i, 0))
    shard = pl.BlockSpec((ns, tm, fs), lambda i: (0, i, 0))
    wspec = pl.BlockSpec((ns, fs, d), lambda i: (0, 0, 0), pipeline_mode=pl.Buffered(1))
    vec = pl.BlockSpec((1, d), lambda i: (0, 0))
    return pl.pallas_call(
        body, name=name, grid=(t // tm,),
        in_specs=[row, row, vec, shard, shard, wspec, wspec, wspec, _ANY],
        out_specs=[row, shard, shard, shard, row, row, vec],
        out_shape=[jax.ShapeDtypeStruct((t, d), f32)] + [jax.ShapeDtypeStruct((ns, t, fs), bf16)] * 3
        + [jax.ShapeDtypeStruct((t, d), bf16)] * 2 + [jax.ShapeDtypeStruct((1, d), f32)],
        compiler_params=_cparams(VMEM_LIMIT_WIDE),
    )(dxo, x, ng, g, u, wg, wu, wd, dep)


def tn_matmul(xs, ys, x_spec, y_specs, n_shards, k1, k2s, t, tm, name, dep, into=None):
    k2 = sum(k2s)
    ny = len(ys)

    def body(*refs):
        x_ref = refs[0]
        y_refs = refs[1:1 + ny]
        o_ref, acc = refs[-2], refs[-1]
        i = pl.program_id(1)
        xb = x_ref[0] if len(x_ref.shape) == 3 else x_ref[...]
        off = 0
        for y_ref, w in zip(y_refs, k2s):
            yb = y_ref[0] if len(y_ref.shape) == 3 else y_ref[...]
            part = _dot_tn(xb, yb)
            sl = (slice(None), slice(off, off + w))

            @pl.when(i == 0)
            def _(part=part, sl=sl):
                acc[sl] = part

            @pl.when(i > 0)
            def _(part=part, sl=sl):
                acc[sl] += part

            off += w

        @pl.when(i == t // tm - 1)
        def _():
            o_ref[0] = acc[...].astype(bf16)

    if into is None:
        slot0, total, extra, aliases = 0, n_shards, [], {}
    else:
        buf, slot0, total = into
        extra = [] if buf is None else [buf]
        aliases = {} if buf is None else {2 + ny: 0}
    return pl.pallas_call(
        body, name=name, grid=(n_shards, t // tm),
        in_specs=[x_spec] + list(y_specs) + [_ANY] * (1 + len(extra)),
        out_specs=pl.BlockSpec((1, k1, k2), lambda s, i: (slot0 + s, 0, 0)),
        out_shape=jax.ShapeDtypeStruct((total, k1, k2), bf16),
        scratch_shapes=[pltpu.VMEM((k1, k2), f32)],
        input_output_aliases=aliases,
        compiler_params=_cparams(),
    )(xs, *ys, dep, *extra)


def _pair_shards(w):
    s4, fs, d = w.shape
    return w.reshape(s4 // 2, 2 * fs, d)


def ffn_weight_grads(hb, dyb, dg, du, act, name, dep, each=None):
    t, d = hb.shape
    s2, _, fs2 = dg.shape
    tm = t
    row = pl.BlockSpec((tm, d), lambda s, i: (i, 0))
    shard = pl.BlockSpec((1, tm, fs2), lambda s, i: (s, i, 0))
    grads = []
    for xa, ya, which in ((dg, hb, "w_gate"), (du, hb, "w_up"), (act, dyb, "w_down")):
        g = tn_matmul(xa, [ya], shard, [row], s2, fs2, [d], t, tm, name + "_" + which, dep)
        g = g.reshape(2 * s2, fs2 // 2, d)
        if each is not None:
            dep = each(which, g)
        grads.append(g)
    return grads


def inproj_fwd(x1, ng, win, bin4, cos, sin):
    t, d = x1.shape
    s4, _, w2 = win.shape
    tm = _row_tile(t)
    dk = d // RET_HEADS
    scale = dk ** -0.5

    def body(x_ref, ng_ref, w_ref, b_ref, cos_ref, sin_ref, p_ref, hb_ref):
        y, _, _ = _rms(x_ref[...], ng_ref[...])
        h = y.astype(bf16)
        hb_ref[...] = h
        for s in range(s4):
            p = _dot(h, w_ref[s]) + b_ref[s]
            if s != 1:
                p_ref[s] = p.astype(bf16)
            else:
                cs, sn = cos_ref[...], sin_ref[...]
                for e in range(2 * RET_HEADS):
                    cols = slice(e * dk, (e + 1) * dk)
                    rot = _rot(p[:, cols], cs, sn)
                    p_ref[s, :, cols] = (rot if e < RET_HEADS else rot * scale).astype(bf16)

    tab = pl.BlockSpec((tm, dk // 2), lambda i: (i, 0))
    return pl.pallas_call(
        body, name="inproj_fwd", grid=(t // tm,),
        in_specs=[pl.BlockSpec((tm, d), lambda i: (i, 0)), pl.BlockSpec((1, d), lambda i: (0, 0)),
                  pl.BlockSpec((s4, d, w2), lambda i: (0, 0, 0), pipeline_mode=pl.Buffered(1)),
                  pl.BlockSpec((s4, 1, w2), lambda i: (0, 0, 0)), tab, tab],
        out_specs=[pl.BlockSpec((s4, tm, w2), lambda i: (0, i, 0)), pl.BlockSpec((tm, d), lambda i: (i, 0))],
        out_shape=[jax.ShapeDtypeStruct((s4, t, w2), bf16), jax.ShapeDtypeStruct((t, d), bf16)],
        compiler_params=_cparams(),
    )(x1, ng, win, bin4, cos, sin)


def _sgu_norm(va, ng, nb):
    gv = _gelu(va)
    mu = jnp.mean(gv, axis=-1, keepdims=True)
    xc = gv - mu
    rstd = lax.rsqrt(jnp.mean(xc * xc, axis=-1, keepdims=True) + NORM_EPS)
    xh = xc * rstd
    return xh, rstd, (xh * ng + nb).astype(bf16)


def sgu_fwd(proj, ng, nb, ws, bs, dep):
    _, t, w2 = proj.shape
    d = w2 // 2
    gd = d // SGU_GROUPS
    tm = _row_tile(t)

    def body(p_ref, ng_ref, nb_ref, ws_ref, bs_ref, dep_ref, a_ref):
        ua = p_ref[0, :, 0:d].astype(f32)
        va = p_ref[0, :, d:w2].astype(f32)
        gu = _gelu(ua)
        _, _, vn = _sgu_norm(va, ng_ref[...], nb_ref[...])
        for c in range(tm // SGU_CHUNK):
            rows = slice(c * SGU_CHUNK, (c + 1) * SGU_CHUNK)
            for g in range(SGU_GROUPS):
                cols = slice(g * gd, (g + 1) * gd)
                sg = _dot(ws_ref[g], vn[rows, cols]) + bs_ref[g]
                a_ref[rows, cols] = (gu[rows, cols] * sg).astype(bf16)

    return pl.pallas_call(
        body, name="sgu_fwd", grid=(t // tm,),
        in_specs=[pl.BlockSpec((1, tm, w2), lambda i: (0, i, 0)), pl.BlockSpec((1, d), lambda i: (0, 0)),
                  pl.BlockSpec((1, d), lambda i: (0, 0)), pl.BlockSpec((SGU_GROUPS, SGU_CHUNK, SGU_CHUNK), lambda i: (0, 0, 0)),
                  pl.BlockSpec((SGU_GROUPS, SGU_CHUNK, 1), lambda i: (0, 0, 0)), _ANY],
        out_specs=pl.BlockSpec((tm, d), lambda i: (i, 0)),
        out_shape=jax.ShapeDtypeStruct((t, d), bf16),
        compiler_params=_cparams(),
    )(proj, ng, nb, ws, bs, dep)


def sgu_bwd(da, proj, ng, nb, ws, bs, dep):
    _, t, w2 = proj.shape
    d = w2 // 2
    gd = d // SGU_GROUPS
    tm = _row_tile(t)

    def body(da_ref, p_ref, ng_ref, nb_ref, ws_ref, bs_ref, dep_ref,
             dua_ref, dva_ref, dws_ref, dbs_ref, dng_ref, dnb_ref, dvn_scr):
        i = pl.program_id(0)
        ua = p_ref[0, :, 0:d].astype(f32)
        va = p_ref[0, :, d:w2].astype(f32)
        gu = _gelu(ua)
        xh, rstd, vn = _sgu_norm(va, ng_ref[...], nb_ref[...])
        dad = da_ref[...].astype(f32)
        dsb = (dad * gu).astype(bf16)
        for c in range(tm // SGU_CHUNK):
            rows = slice(c * SGU_CHUNK, (c + 1) * SGU_CHUNK)
            for g in range(SGU_GROUPS):
                cols = slice(g * gd, (g + 1) * gd)
                sg = _dot(ws_ref[g], vn[rows, cols]) + bs_ref[g]
                dua_ref[rows, cols] = (dad[rows, cols] * sg * _dgelu(ua[rows, cols])).astype(bf16)
                ds = dsb[rows, cols]
                dvn_scr[rows, cols] = _dot_tn(ws_ref[g], ds)
                dw = _dot_nt(ds, vn[rows, cols])
                db = jnp.sum(ds.astype(f32), axis=1, keepdims=True)
                if c == 0:
                    _acc_out(dws_ref.at[g], i == 0, dw)
                    _acc_out(dbs_ref.at[g], i == 0, db)
                else:
                    dws_ref[g] += dw
                    dbs_ref[g] += db
        dvn = dvn_scr[...]
        _acc_out(dng_ref, i == 0, jnp.sum(dvn * xh, axis=0, keepdims=True))
        _acc_out(dnb_ref, i == 0, jnp.sum(dvn, axis=0, keepdims=True))
        dxh = dvn * ng_ref[...]
        dgv = rstd * (dxh - jnp.mean(dxh, axis=-1, keepdims=True) - xh * jnp.mean(dxh * xh, axis=-1, keepdims=True))
        dva_ref[...] = (dgv * _dgelu(va)).astype(bf16)

    row = pl.BlockSpec((tm, d), lambda i: (i, 0))
    vec = pl.BlockSpec((1, d), lambda i: (0, 0))
    wsp = pl.BlockSpec((SGU_GROUPS, SGU_CHUNK, SGU_CHUNK), lambda i: (0, 0, 0))
    bsp = pl.BlockSpec((SGU_GROUPS, SGU_CHUNK, 1), lambda i: (0, 0, 0))
    return pl.pallas_call(
        body, name="sgu_bwd", grid=(t // tm,),
        in_specs=[row, pl.BlockSpec((1, tm, w2), lambda i: (0, i, 0)), vec, vec, wsp, bsp, _ANY],
        out_specs=[row, row, wsp, bsp, vec, vec],
        out_shape=[jax.ShapeDtypeStruct((t, d), bf16), jax.ShapeDtypeStruct((t, d), bf16),
                   jax.ShapeDtypeStruct((SGU_GROUPS, SGU_CHUNK, SGU_CHUNK), f32), jax.ShapeDtypeStruct((SGU_GROUPS, SGU_CHUNK, 1), f32),
                   jax.ShapeDtypeStruct((1, d), f32), jax.ShapeDtypeStruct((1, d), f32)],
        scratch_shapes=[pltpu.VMEM((tm, d), f32)],
        compiler_params=_cparams(),
    )(da, proj, ng, nb, ws, bs, dep)


def retention_constants(decay_logit, t, dk, zero):
    lg = jax.nn.log_sigmoid(decay_logit.astype(f32) + zero)
    lgf = lg[0][:, None]
    lgb = lg[1][:, None]
    idx = jnp.arange(CHUNK, dtype=f32)[None, :]
    af = jnp.exp((idx + 1.0) * lgf)
    ab = jnp.exp((CHUNK - idx) * lgb)
    kf = jnp.exp((CHUNK - 1.0 - idx) * lgf)
    kb = jnp.exp(idx * lgb)
    cols = jnp.stack([af, ab, kf, kb, af * (idx + 1.0), ab * (CHUNK - idx), kf * (CHUNK - 1.0 - idx), kb * idx], axis=1)
    cols = cols[..., None]
    diff = idx[0][:, None] - idx[0][None, :]
    dfm = jnp.where(diff >= 0, jnp.exp(jnp.maximum(diff, 0.0)[None] * lgf[:, :, None]), 0.0)
    dbm = jnp.where(diff < 0, jnp.exp(jnp.maximum(-diff, 0.0)[None] * lgb[:, :, None]), 0.0)
    mats = jnp.stack([dfm + dbm, dfm * diff[None], dbm * (-diff)[None]], axis=1)
    cdec = jnp.stack([jnp.broadcast_to(jnp.exp(CHUNK * lgf), (RET_HEADS, dk)),
                      jnp.broadcast_to(jnp.exp(CHUNK * lgb), (RET_HEADS, dk))], axis=1)
    theta = ROPE_BASE ** (-jnp.arange(0, dk, 2, dtype=f32) / dk)
    ang = (jnp.arange(t, dtype=f32) + zero)[:, None] * theta[None, :]
    return cols, mats, cdec, jnp.cos(ang), jnp.sin(ang)


def _rot(tr, cos, sin):
    half = tr.shape[-1] // 2
    t1 = tr[:, :half]
    t2 = tr[:, half:]
    return jnp.concatenate([t1 * cos - t2 * sin, t2 * cos + t1 * sin], axis=-1)


def _rot_inv(dt, cos, sin):
    half = dt.shape[-1] // 2
    d1 = dt[:, :half]
    d2 = dt[:, half:]
    return jnp.concatenate([d1 * cos + d2 * sin, d2 * cos - d1 * sin], axis=-1)


def _ret_specs(t, d, dk, rt):
    nr = t // rt
    hq = d // dk

    def blk(p, n):
        return (1 - p) * (nr - 1 - n) + p * n

    q_spec = pl.BlockSpec((1, rt, dk), lambda h, p, n: (1, blk(p, n), h))
    k_spec = pl.BlockSpec((1, rt, dk), lambda h, p, n: (1, blk(p, n), hq + h))
    v_spec = pl.BlockSpec((1, rt, dk), lambda h, p, n: (2, blk(p, n), h))
    g_spec = pl.BlockSpec((1, rt, dk), lambda h, p, n: (2, blk(p, n), hq + h))
    tab_spec = pl.BlockSpec((rt, dk // 2), lambda h, p, n: (blk(p, n), 0))
    cols_spec = pl.BlockSpec((1, 8, CHUNK, 1), lambda h, p, n: (h, 0, 0, 0))
    mats_spec = pl.BlockSpec((1, 3, CHUNK, CHUNK), lambda h, p, n: (h, 0, 0, 0))
    cdec_spec = pl.BlockSpec((1, 2, dk), lambda h, p, n: (h, 0, 0))
    in_row = pl.BlockSpec((rt, dk), lambda h, p, n: (blk(p, n), h))
    out_row = pl.BlockSpec((rt, dk), lambda h, p, n: (p * n, h))
    return nr, blk, q_spec, k_spec, v_spec, g_spec, tab_spec, cols_spec, mats_spec, cdec_spec, in_row, out_row


def ret_fwd(proj, cols, mats, cdec):
    _, t, w2 = proj.shape
    d = w2 // 2
    dk = d // RET_HEADS
    rt = _row_tile(t)
    cpt = rt // CHUNK
    nr, blk, q_spec, k_spec, v_spec, g_spec, _, cols_spec, mats_spec, cdec_spec, _, out_row = _ret_specs(t, d, dk, rt)

    def body(q_ref, k_ref, v_ref, g_ref, cols_ref, mats_ref, cdec_ref, r_ref, rn_ref, sb_scr, st):
        p = pl.program_id(1)
        n = pl.program_id(2)
        af, ab, kf, kb = cols_ref[0, 0], cols_ref[0, 1], cols_ref[0, 2], cols_ref[0, 3]
        cf = cdec_ref[0, 0:1, :]
        cb = cdec_ref[0, 1:2, :]

        @pl.when(n == 0)
        def _():
            st[...] = jnp.zeros_like(st)

        @pl.when(p == 0)
        def _():
            for j in reversed(range(cpt)):
                rows = slice(j * CHUNK, (j + 1) * CHUNK)
                ch = blk(p, n) * cpt + j
                kk = k_ref[0, rows, :].astype(f32)
                sb_scr[ch] = st[...].astype(bf16)
                st[...] = st[...] * cb + _dot_tn((kk * kb).astype(bf16), v_ref[0, rows, :])

        @pl.when(p == 1)
        def _():
            for j in range(cpt):
                rows = slice(j * CHUNK, (j + 1) * CHUNK)
                ch = blk(p, n) * cpt + j
                qb = q_ref[0, rows, :]
                kkb = k_ref[0, rows, :]
                q = qb.astype(f32)
                kk = kkb.astype(f32)
                v = v_ref[0, rows, :]
                pm = (_dot_nt(qb, kkb) * mats_ref[0, 0]).astype(bf16)
                out = (_dot(pm, v) + _dot((q * af).astype(bf16), st[...].astype(bf16))
                       + _dot((q * ab).astype(bf16), sb_scr[ch]))
                st[...] = st[...] * cf + _dot_tn((kk * kf).astype(bf16), v)
                rhat = out * lax.rsqrt(jnp.mean(out * out, axis=-1, keepdims=True) + NORM_EPS)
                gg = g_ref[0, rows, :].astype(f32)
                r_ref[rows, :] = out.astype(bf16)
                rn_ref[rows, :] = (rhat * gg * _sigmoid(gg)).astype(bf16)

    return pl.pallas_call(
        body, name="ret_fwd", grid=(RET_HEADS, 2, nr),
        in_specs=[q_spec, k_spec, v_spec, g_spec, cols_spec, mats_spec, cdec_spec],
        out_specs=[out_row, out_row],
        out_shape=[jax.ShapeDtypeStruct((t, d), bf16), jax.ShapeDtypeStruct((t, d), bf16)],
        scratch_shapes=[pltpu.VMEM((t // CHUNK, dk, dk), bf16), pltpu.VMEM((dk, dk), f32)],
        compiler_params=_cparams(),
    )(proj, proj, proj, proj, cols, mats, cdec)


def ret_bwd(drn, r, proj, cols, mats, cdec, cos, sin):
    _, t, w2 = proj.shape
    d = w2 // 2
    dk = d // RET_HEADS
    rt = _row_tile(t)
    cpt = rt // CHUNK
    nr, blk, q_spec, k_spec, v_spec, g_spec, tab_spec, cols_spec, mats_spec, cdec_spec, in_row, out_row = _ret_specs(t, d, dk, rt)
    scale = dk ** -0.5

    def body(drn_ref, r_ref, q_ref, k_ref, v_ref, g_ref, cos_ref, sin_ref, cols_ref, mats_ref, cdec_ref,
             dq_ref, dk_ref, dv_ref, dg_ref, dlg_ref,
             sb_scr, gf_scr, st_s, st_g, acc_af, acc_ab, acc_vf, acc_vb, acc_sf, acc_sb, dout_scr, dgr_scr):
        p = pl.program_id(1)
        n = pl.program_id(2)
        af, ab, kf, kb = cols_ref[0, 0], cols_ref[0, 1], cols_ref[0, 2], cols_ref[0, 3]
        af1, ab1, kf1, kb1 = cols_ref[0, 4], cols_ref[0, 5], cols_ref[0, 6], cols_ref[0, 7]
        cf = cdec_ref[0, 0:1, :]
        cb = cdec_ref[0, 1:2, :]

        @pl.when(n == 0)
        def _():
            st_s[...] = jnp.zeros_like(st_s)
            st_g[...] = jnp.zeros_like(st_g)

        @pl.when(jnp.logical_and(n == 0, p == 1))
        def _():
            for a in (acc_af, acc_ab, acc_vf, acc_vb, acc_sf, acc_sb):
                a[...] = jnp.zeros_like(a)

        def load(rows):
            cs, sn = cos_ref[rows, :], sin_ref[rows, :]
            q = q_ref[0, rows, :].astype(f32)
            kk = k_ref[0, rows, :].astype(f32)
            rr = r_ref[rows, :].astype(f32)
            rstd = lax.rsqrt(jnp.mean(rr * rr, axis=-1, keepdims=True) + NORM_EPS)
            rhat = rr * rstd
            gg = g_ref[0, rows, :].astype(f32)
            sg = _sigmoid(gg)
            dd = drn_ref[rows, :].astype(f32)
            drhat = dd * gg * sg
            dout = rstd * (drhat - rhat * jnp.mean(drhat * rhat, axis=-1, keepdims=True))
            dgr = dd * rhat * _dsilu(gg, sg)
            return q, kk, dout.astype(bf16), dgr, cs, sn

        @pl.when(p == 0)
        def _():
            for j in reversed(range(cpt)):
                rows = slice(j * CHUNK, (j + 1) * CHUNK)
                ch = blk(p, n) * cpt + j
                q, kk, doutb, dgr, _, _ = load(rows)
                kept = pl.ds(pl.multiple_of(ch * CHUNK, CHUNK), CHUNK)
                dout_scr[kept, :] = doutb
                dgr_scr[kept, :] = dgr.astype(bf16)
                sb_scr[ch] = st_s[...].astype(bf16)
                gf_scr[ch] = st_g[...].astype(bf16)
                st_s[...] = st_s[...] * cb + _dot_tn((kk * kb).astype(bf16), v_ref[0, rows, :])
                st_g[...] = st_g[...] * cf + _dot_tn((q * af).astype(bf16), doutb)

        @pl.when(p == 1)
        def _():
            for j in range(cpt):
                rows = slice(j * CHUNK, (j + 1) * CHUNK)
                ch = blk(p, n) * cpt + j
                kept = pl.ds(pl.multiple_of(ch * CHUNK, CHUNK), CHUNK)
                doutb = dout_scr[kept, :]
                cs, sn = cos_ref[rows, :], sin_ref[rows, :]
                v = v_ref[0, rows, :]
                qb = q_ref[0, rows, :]
                kkb = k_ref[0, rows, :]
                q = qb.astype(f32)
                kk = kkb.astype(f32)
                sf = st_s[...]
                gb = st_g[...]
                sfb = sf.astype(bf16)
                gbb = gb.astype(bf16)
                sbb = sb_scr[ch]
                gfb = gf_scr[ch]
                dmat = mats_ref[0, 0]
                scores = _dot_nt(qb, kkb)
                dpraw = _dot_nt(doutb, v)
                dpb = (dpraw * dmat).astype(bf16)
                pmb = (scores * dmat).astype(bf16)
                x1 = _dot_nt(doutb, sfb)
                x2 = _dot_nt(doutb, sbb)
                y1 = _dot_nt(v, gfb)
                y2 = _dot_nt(v, gbb)
                kdf = (kk * kf).astype(bf16)
                kdb = (kk * kb).astype(bf16)
                dq = _dot(dpb, kkb) + x1 * af + x2 * ab
                dkk = _dot_tn(dpb, qb) + y1 * kf + y2 * kb
                dv = _dot_tn(pmb, doutb) + _dot(kdf, gfb) + _dot(kdb, gbb)
                ps = dpraw * scores
                acc_af[...] += ps * mats_ref[0, 1]
                acc_ab[...] += ps * mats_ref[0, 2]
                acc_vf[...] += x1 * q * af1 + y1 * kk * kf1
                acc_vb[...] += x2 * q * ab1 + y2 * kk * kb1
                acc_sf[...] += gfb.astype(f32) * sf
                acc_sb[...] += gb * sbb.astype(f32)
                st_s[...] = sf * cf + _dot_tn(kdf, v)
                st_g[...] = gb * cb + _dot_tn((q * ab).astype(bf16), doutb)
                dq_ref[rows, :] = _rot_inv(dq, cs, sn).astype(bf16)
                dk_ref[rows, :] = (_rot_inv(dkk, cs, sn) * scale).astype(bf16)
                dv_ref[rows, :] = dv.astype(bf16)
                dg_ref[rows, :] = dgr_scr[kept, :]

        @pl.when(jnp.logical_and(p == 1, n == nr - 1))
        def _():
            tf = jnp.sum(acc_af[...]) + jnp.sum(acc_vf[...]) + CHUNK * jnp.sum(acc_sf[...] * cf)
            tb = jnp.sum(acc_ab[...]) + jnp.sum(acc_vb[...]) + CHUNK * jnp.sum(acc_sb[...] * cb)
            rid = lax.broadcasted_iota(jnp.int32, (8, 128), 0)
            dlg_ref[0] = jnp.where(rid == 0, tf, jnp.where(rid == 1, tb, 0.0))

    nch = t // CHUNK
    return pl.pallas_call(
        body, name="ret_bwd", grid=(RET_HEADS, 2, nr),
        in_specs=[in_row, in_row, q_spec, k_spec, v_spec, g_spec, tab_spec, tab_spec, cols_spec, mats_spec, cdec_spec],
        out_specs=[out_row, out_row, out_row, out_row, pl.BlockSpec((1, 8, 128), lambda h, p, n: (h, 0, 0))],
        out_shape=[jax.ShapeDtypeStruct((t, d), bf16)] * 4 + [jax.ShapeDtypeStruct((RET_HEADS, 8, 128), f32)],
        scratch_shapes=[pltpu.VMEM((nch, dk, dk), bf16), pltpu.VMEM((nch, dk, dk), bf16),
                        pltpu.VMEM((dk, dk), f32), pltpu.VMEM((dk, dk), f32),
                        pltpu.VMEM((CHUNK, CHUNK), f32), pltpu.VMEM((CHUNK, CHUNK), f32),
                        pltpu.VMEM((CHUNK, dk), f32), pltpu.VMEM((CHUNK, dk), f32),
                        pltpu.VMEM((dk, dk), f32), pltpu.VMEM((dk, dk), f32),
                        pltpu.VMEM((t, dk), bf16), pltpu.VMEM((t, dk), bf16)],
        compiler_params=_cparams(),
    )(drn, r, proj, proj, proj, proj, cos, sin, cols, mats, cdec)


def mix_fwd(a, rn, proj, wa, wb, wo, x1):
    t, d = x1.shape
    tm = _row_tile(t)

    def body(a_ref, rn_ref, p_ref, wa_ref, wb_ref, wo_ref, x_ref, xo_ref, ba_ref, br_ref):
        ba = _dot(a_ref[...], wa_ref[...])
        br = _dot(rn_ref[...], wb_ref[...])
        sa = _sigmoid(p_ref[0, :, 0:d].astype(f32))
        sb = _sigmoid(p_ref[0, :, d:2 * d].astype(f32))
        mix = (sa * ba + sb * br).astype(bf16)
        xo_ref[...] = x_ref[...] + _dot(mix, wo_ref[...])
        ba_ref[...] = ba.astype(bf16)
        br_ref[...] = br.astype(bf16)

    row = pl.BlockSpec((tm, d), lambda i: (i, 0))
    wsp = pl.BlockSpec((d, d), lambda i: (0, 0))
    return pl.pallas_call(
        body, name="mix_fwd", grid=(t // tm,),
        in_specs=[row, row, pl.BlockSpec((1, tm, 2 * d), lambda i: (3, i, 0)), wsp, wsp, wsp, row],
        out_specs=[row, row, row],
        out_shape=[jax.ShapeDtypeStruct((t, d), f32), jax.ShapeDtypeStruct((t, d), bf16), jax.ShapeDtypeStruct((t, d), bf16)],
        compiler_params=_cparams(),
    )(a, rn, proj, wa, wb, wo, x1)


def mix_bwd_act(dx2, ba, br, proj, wa, wb, wo, dep):
    t, d = dx2.shape
    tm = _row_tile(t)

    def body(dx_ref, ba_ref, br_ref, p_ref, wa_ref, wb_ref, wo_ref, dep_ref,
             da_ref, drn_ref, dga_ref, dgb_ref, mix_ref, dba_ref, dbr_ref, dxb_ref):
        dxb = dx_ref[...].astype(bf16)
        dxb_ref[...] = dxb
        dmix = _dot_nt(dxb, wo_ref[...])
        ba = ba_ref[...].astype(f32)
        br = br_ref[...].astype(f32)
        sa = _sigmoid(p_ref[0, :, 0:d].astype(f32))
        sb = _sigmoid(p_ref[0, :, d:2 * d].astype(f32))
        mix_ref[...] = (sa * ba + sb * br).astype(bf16)
        dba = (dmix * sa).astype(bf16)
        dbr = (dmix * sb).astype(bf16)
        dba_ref[...] = dba
        dbr_ref[...] = dbr
        dga_ref[...] = (dmix * ba * sa * (1.0 - sa)).astype(bf16)
        dgb_ref[...] = (dmix * br * sb * (1.0 - sb)).astype(bf16)
        da_ref[...] = _dot_nt(dba, wa_ref[...]).astype(bf16)
        drn_ref[...] = _dot_nt(dbr, wb_ref[...]).astype(bf16)

    row = pl.BlockSpec((tm, d), lambda i: (i, 0))
    wsp = pl.BlockSpec((d, d), lambda i: (0, 0))
    return pl.pallas_call(
        body, name="mix_bwd_act", grid=(t // tm,),
        in_specs=[row, row, row, pl.BlockSpec((1, tm, 2 * d), lambda i: (3, i, 0)), wsp, wsp, wsp, _ANY],
        out_specs=[row] * 8,
        out_shape=[jax.ShapeDtypeStruct((t, d), bf16)] * 8,
        compiler_params=_cparams(),
    )(dx2, ba, br, proj, wa, wb, wo, dep)


def inproj_bwd_act(segs, win, x1, ng, dx2):
    t, d = x1.shape
    s4 = win.shape[0]
    tm = _row_tile(t)
    nseg = len(segs)

    def body(*refs):
        seg_refs = refs[:nseg]
        w_ref, x_ref, ng_ref, dx2_ref, dx1_ref, db_ref, dng_ref = refs[nseg:]
        i = pl.program_id(0)
        dh = None
        for e, sr in enumerate(seg_refs):
            sb = sr[...]
            part = _dot_nt(sb, w_ref[e // 2, :, (e % 2) * d:(e % 2 + 1) * d])
            dh = part if dh is None else dh + part
            _acc_out(db_ref.at[e], i == 0, jnp.sum(sb.astype(f32), axis=0, keepdims=True))
        _, xh, r = _rms(x_ref[...], ng_ref[...])
        dx1_ref[...] = dx2_ref[...] + _rms_bwd(dh, xh, r, ng_ref[...])
        _acc_out(dng_ref, i == 0, jnp.sum(dh * xh, axis=0, keepdims=True))

    row = pl.BlockSpec((tm, d), lambda i: (i, 0))
    vec = pl.BlockSpec((1, d), lambda i: (0, 0))
    return pl.pallas_call(
        body, name="inproj_bwd_act", grid=(t // tm,),
        in_specs=[row] * nseg + [pl.BlockSpec((s4, d, 2 * d), lambda i: (0, 0, 0), pipeline_mode=pl.Buffered(1)),
                                 row, vec, row],
        out_specs=[row, pl.BlockSpec((nseg, 1, d), lambda i: (0, 0, 0)), vec],
        out_shape=[jax.ShapeDtypeStruct((t, d), f32), jax.ShapeDtypeStruct((nseg, 1, d), f32),
                   jax.ShapeDtypeStruct((1, d), f32)],
        compiler_params=_cparams(VMEM_LIMIT_WIDE),
    )(*segs, win, x1, ng, dx2)


def _place():
    return lax.axis_index("x"), lax.axis_index("y"), lax.axis_index("c")


def _other_chips(x, y):
    return [(1 - x, y), (x, 1 - y), (1 - x, 1 - y)]


_ANY = pl.BlockSpec(memory_space=pl.ANY)


_HBM = pl.BlockSpec(memory_space=pltpu.HBM)
_SEM = pl.BlockSpec(memory_space=pltpu.SEMAPHORE)
_EFFECT = pltpu.SideEffectType.DATAFLOW_SIDE_EFFECTING


def _hbm(a):
    return pltpu.with_memory_space_constraint(a, pltpu.HBM)


def _half_rows(ref, c):
    half = ref.shape[1] // 2
    return pl.ds(pl.multiple_of(c * half, 16), half)


def _chip_copy(src, dst, send_sem, recv_sem, chip, c):
    return pltpu.make_async_remote_copy(src_ref=src, dst_ref=dst, send_sem=send_sem, recv_sem=recv_sem,
                                        device_id=(chip[0], chip[1], c), device_id_type=MESH)


def gather_start(bufs, groups, name):
    nb, ng = len(bufs), len(groups)

    def body(*refs):
        ins = refs[:nb]
        sems = refs[nb:nb + 2 * ng]
        token = refs[-1]
        x, y, c = _place()
        k = 2 * x + y
        for gi, grp in enumerate(groups):
            for wi, w in enumerate(grp):
                mine = ins[w].at[k, _half_rows(ins[w], c)]
                for j, chip in enumerate(_other_chips(x, y)):
                    _chip_copy(mine, mine, sems[2 * gi].at[3 * wi + j], sems[2 * gi + 1].at[3 * wi + j], chip, c).start()
        token[...] = jnp.zeros_like(token)

    sem_shapes = []
    for grp in groups:
        sem_shapes += [pltpu.SemaphoreType.DMA((3 * len(grp),)), pltpu.SemaphoreType.DMA((3 * len(grp),))]
    outs = pl.pallas_call(
        body, name=name,
        out_shape=sem_shapes + [pltpu.HBM(b.shape, b.dtype) for b in bufs] + [jax.ShapeDtypeStruct((8, 128), f32)],
        in_specs=[_HBM] * nb,
        out_specs=[_SEM] * (2 * ng) + [_HBM] * nb + [pl.BlockSpec(memory_space=pltpu.VMEM)],
        input_output_aliases={w: 2 * ng + w for w in range(nb)},
        compiler_params=pltpu.CompilerParams(has_side_effects=_EFFECT),
    )(*[_hbm(b) for b in bufs])
    sems = [(outs[2 * gi], outs[2 * gi + 1]) for gi in range(ng)]
    return sems, list(outs[2 * ng:2 * ng + nb]), outs[-1]


def gather_wait(bufs, sems, after, name):
    n = len(bufs)

    def body(*refs):
        ins = refs[:n]
        send_sems, recv_sems = refs[n], refs[n + 1]
        x, y, c = _place()
        k = 2 * x + y
        for wi in range(n):
            half = _half_rows(ins[wi], c)
            for j, chip in enumerate(_other_chips(x, y)):
                cp = _chip_copy(ins[wi].at[k, half], ins[wi].at[2 * chip[0] + chip[1], half], send_sems.at[3 * wi + j],
                                recv_sems.at[3 * wi + j], chip, c)
                cp.wait_send()
                cp.wait_recv()

    outs = pl.pallas_call(
        body, name=name,
        out_shape=[pltpu.HBM(b.shape, b.dtype) for b in bufs],
        in_specs=[_HBM] * n + [_SEM, _SEM, _ANY],
        out_specs=[_HBM] * n,
        input_output_aliases={i: i for i in range(n)},
        compiler_params=pltpu.CompilerParams(has_side_effects=_EFFECT),
    )(*bufs, sems[0], sems[1], after)
    return list(outs)


def gather_forward(bufs, name):
    n = len(bufs)

    def body(*refs):
        ins = refs[n:2 * n]
        send_sems, recv_sems = refs[2 * n], refs[2 * n + 1]
        x, y, c = _place()
        copies = []
        for wi in range(n):
            for j, chip in enumerate(_other_chips(x, y)):
                kp = 2 * chip[0] + chip[1]
                got = ins[wi].at[kp, _half_rows(ins[wi], c)]
                cp = pltpu.make_async_remote_copy(
                    src_ref=got, dst_ref=got, send_sem=send_sems.at[3 * wi + j], recv_sem=recv_sems.at[3 * wi + j],
                    device_id=(x, y, 1 - c), device_id_type=MESH)
                cp.start()
                copies.append((cp, wi, kp, j))
        for cp, wi, kp, j in copies:
            cp.wait_send()
            theirs = ins[wi].at[kp, _half_rows(ins[wi], 1 - c)]
            pltpu.make_async_remote_copy(
                src_ref=theirs, dst_ref=theirs, send_sem=send_sems.at[3 * wi + j], recv_sem=recv_sems.at[3 * wi + j],
                device_id=(x, y, 1 - c), device_id_type=MESH).wait_recv()

    outs = pl.pallas_call(
        body, name=name,
        out_shape=[jax.ShapeDtypeStruct(b.shape, b.dtype) for b in bufs],
        in_specs=[_ANY] * n, out_specs=[_ANY] * n,
        input_output_aliases={i: i for i in range(n)},
        scratch_shapes=[pltpu.SemaphoreType.DMA((3 * n,)), pltpu.SemaphoreType.DMA((3 * n,))],
    )(*bufs)
    return list(outs)


def forward_start(bufs, name):
    n = len(bufs)

    def body(*refs):
        x, y, c = _place()
        for wi in range(n):
            for j, chip in enumerate(_other_chips(x, y)):
                got = refs[wi].at[2 * chip[0] + chip[1], _half_rows(refs[wi], c)]
                _sibling_copy(got, got, refs[n].at[3 * wi + j], refs[n + 1].at[3 * wi + j]).start()
        refs[-1][...] = jnp.zeros_like(refs[-1])

    return _split_start(body, name, 3 * n, list(bufs))


def forward_wait(bufs, sems, after, name):
    n = len(bufs)

    def body(*refs):
        x, y, c = _place()
        for wi in range(n):
            for j, chip in enumerate(_other_chips(x, y)):
                kp = 2 * chip[0] + chip[1]
                got = refs[wi].at[kp, _half_rows(refs[wi], c)]
                theirs = refs[wi].at[kp, _half_rows(refs[wi], 1 - c)]
                _sibling_copy(got, got, refs[n].at[3 * wi + j], refs[n + 1].at[3 * wi + j]).wait_send()
                _sibling_copy(theirs, theirs, refs[n].at[3 * wi + j], refs[n + 1].at[3 * wi + j]).wait_recv()

    return _split_wait(body, name, list(bufs), sems, after)


def exchange_start(grads, name):
    n = len(grads)
    lands = [lax.empty((3,) + g.shape[1:], g.dtype) for g in grads]

    def body(*refs):
        ins = refs[:n]
        land = refs[n:2 * n]
        send_sems, recv_sems = refs[2 * n], refs[2 * n + 1]
        token = refs[-1]
        x, y, c = _place()
        for wi in range(n):
            for j, chip in enumerate(_other_chips(x, y)):
                _chip_copy(ins[wi].at[2 * chip[0] + chip[1]], land[wi].at[j], send_sems.at[3 * wi + j],
                           recv_sems.at[3 * wi + j], chip, c).start()
        token[...] = jnp.zeros_like(token)

    outs = pl.pallas_call(
        body, name=name,
        out_shape=[pltpu.SemaphoreType.DMA((3 * n,)), pltpu.SemaphoreType.DMA((3 * n,))]
        + [pltpu.HBM(g.shape, g.dtype) for g in grads] + [pltpu.HBM(l.shape, l.dtype) for l in lands]
        + [jax.ShapeDtypeStruct((8, 128), f32)],
        in_specs=[_HBM] * (2 * n),
        out_specs=[_SEM, _SEM] + [_HBM] * (2 * n) + [pl.BlockSpec(memory_space=pltpu.VMEM)],
        input_output_aliases={i: 2 + i for i in range(2 * n)},
        compiler_params=pltpu.CompilerParams(has_side_effects=_EFFECT),
    )(*[_hbm(g) for g in grads], *[_hbm(l) for l in lands])
    return (outs[0], outs[1]), list(outs[2:2 + n]), list(outs[2 + n:2 + 2 * n]), outs[-1]


def exchange_wait(grads, lands, sems, after, name):
    n = len(grads)

    def body(*refs):
        ins = refs[:n]
        land = refs[n:2 * n]
        send_sems, recv_sems = refs[2 * n], refs[2 * n + 1]
        x, y, c = _place()
        for wi in range(n):
            for j, chip in enumerate(_other_chips(x, y)):
                cp = _chip_copy(ins[wi].at[2 * chip[0] + chip[1]], land[wi].at[j], send_sems.at[3 * wi + j],
                                recv_sems.at[3 * wi + j], chip, c)
                cp.wait_send()
                cp.wait_recv()

    outs = pl.pallas_call(
        body, name=name,
        out_shape=[pltpu.HBM(g.shape, g.dtype) for g in grads] + [pltpu.HBM(l.shape, l.dtype) for l in lands],
        in_specs=[_HBM] * (2 * n) + [_SEM, _SEM, _ANY],
        out_specs=[_HBM] * (2 * n),
        input_output_aliases={i: i for i in range(2 * n)},
        compiler_params=pltpu.CompilerParams(has_side_effects=_EFFECT),
    )(*grads, *lands, sems[0], sems[1], after)
    return list(outs[:n]), list(outs[n:])


def _split_start(body, name, n_sems, operands):
    n = len(operands)
    outs = pl.pallas_call(
        body, name=name,
        out_shape=[pltpu.SemaphoreType.DMA((n_sems,)), pltpu.SemaphoreType.DMA((n_sems,))]
        + [pltpu.HBM(o.shape, o.dtype) for o in operands] + [jax.ShapeDtypeStruct((8, 128), f32)],
        in_specs=[_HBM] * n,
        out_specs=[_SEM, _SEM] + [_HBM] * n + [pl.BlockSpec(memory_space=pltpu.VMEM)],
        input_output_aliases={i: 2 + i for i in range(n)},
        compiler_params=pltpu.CompilerParams(has_side_effects=_EFFECT),
    )(*[_hbm(o) for o in operands])
    return (outs[0], outs[1]), list(outs[2:2 + n]), outs[-1]


def _split_wait(body, name, operands, sems, after):
    n = len(operands)
    outs = pl.pallas_call(
        body, name=name,
        out_shape=[pltpu.HBM(o.shape, o.dtype) for o in operands],
        in_specs=[_HBM] * n + [_SEM, _SEM, _ANY],
        out_specs=[_HBM] * n,
        input_output_aliases={i: i for i in range(n)},
        compiler_params=pltpu.CompilerParams(has_side_effects=_EFFECT),
    )(*operands, sems[0], sems[1], after)
    return list(outs)


def _sibling_copy(src, dst, send_sem, recv_sem):
    x, y, c = _place()
    return pltpu.make_async_remote_copy(src_ref=src, dst_ref=dst, send_sem=send_sem, recv_sem=recv_sem,
                                        device_id=(x, y, 1 - c), device_id_type=MESH)


def swap_start(parts, name):
    n = len(parts)

    def body(*refs):
        for w in range(n):
            _sibling_copy(refs[w], refs[n + w], refs[2 * n].at[w], refs[2 * n + 1].at[w]).start()
        refs[-1][...] = jnp.zeros_like(refs[-1])

    sems, ops, token = _split_start(body, name, n, list(parts) + [lax.empty(p.shape, p.dtype) for p in parts])
    return sems, ops[:n], ops[n:], token


def swap_wait(parts, lands, sems, after, name):
    n = len(parts)

    def body(*refs):
        for w in range(n):
            cp = _sibling_copy(refs[w], refs[n + w], refs[2 * n].at[w], refs[2 * n + 1].at[w])
            cp.wait_send()
            cp.wait_recv()

    outs = _split_wait(body, name, list(parts) + list(lands), sems, after)
    return outs[:n], outs[n:]


def _all_peers(x, y, c):
    return [(1 - x if m & 4 else x, 1 - y if m & 2 else y, 1 - c if m & 1 else c) for m in range(1, N_DEV)]


def small_start(block):
    land = jnp.broadcast_to(block[None], (N_DEV,) + block.shape)

    def body(b_ref, land_ref, send_sems, recv_sems, b_thru, land_thru, token):
        x, y, c = _place()
        me = 4 * x + 2 * y + c
        for m, peer in enumerate(_all_peers(x, y, c)):
            pltpu.make_async_remote_copy(src_ref=b_ref, dst_ref=land_ref.at[me], send_sem=send_sems.at[m],
                                         recv_sem=recv_sems.at[m], device_id=peer, device_id_type=MESH).start()
        token[...] = jnp.zeros_like(token)

    sems, ops, token = _split_start(body, "small_start", N_DEV - 1, [block, land])
    return sems, ops[0], ops[1], token


def small_wait(block, land, sems, after):
    def body(b_ref, land_ref, send_sems, recv_sems, after_ref, b_thru, land_thru):
        x, y, c = _place()
        for m, (px, py, pc) in enumerate(_all_peers(x, y, c)):
            cp = pltpu.make_async_remote_copy(src_ref=b_ref, dst_ref=land_ref.at[4 * px + 2 * py + pc],
                                              send_sem=send_sems.at[m], recv_sem=recv_sems.at[m],
                                              device_id=(px, py, pc), device_id_type=MESH)
            cp.wait_send()
            cp.wait_recv()

    return _split_wait(body, "small_wait", [block, land], sems, after)[1]


def _adamw(w, g, m, v):
    m = ADAM_B1 * m + (1.0 - ADAM_B1) * g
    v = ADAM_B2 * v + (1.0 - ADAM_B2) * (g * g)
    m_hat = m / (1.0 - ADAM_B1 ** ADAM_STEP)
    v_hat = v / (1.0 - ADAM_B2 ** ADAM_STEP)
    delta = -ADAM_LR * (m_hat / (jnp.sqrt(v_hat) + ADAM_EPS) + ADAM_WD * w)
    return delta, m, v


EW_BLOCK_BYTES = 2 * 1024 * 1024


def _ew_tile(rows, cols):
    for cand in (512, 352, 256, 176, 128, 64, 32, 16, 8):
        if rows % cand == 0 and cand * cols * 4 <= EW_BLOCK_BYTES:
            return cand
    return rows


def sum_partials(chip, own, land, name):
    _, r, c = own.shape
    tr = _ew_tile(r, c)

    def body(k_ref, own_ref, p_ref, o_ref):
        o_ref[...] = ((own_ref[0].astype(f32) + p_ref[0].astype(f32)) + p_ref[1].astype(f32)) + p_ref[2].astype(f32)

    return pl.pallas_call(
        body, name=name,
        grid_spec=pltpu.PrefetchScalarGridSpec(
            num_scalar_prefetch=1, grid=(r // tr,),
            in_specs=[pl.BlockSpec((1, tr, c), lambda i, k: (k[0], i, 0)), pl.BlockSpec((3, tr, c), lambda i, k: (0, i, 0))],
            out_specs=pl.BlockSpec((tr, c), lambda i, k: (i, 0))),
        out_shape=jax.ShapeDtypeStruct((r, c), f32),
        compiler_params=_cparams(),
    )(chip, own, land)


def adamw_shard(p_mine, p_sibling, w, m, v, name):
    r, c = w.shape
    tr = _ew_tile(r, c)

    def body(a_ref, b_ref, w_ref, m_ref, v_ref, g_ref, d_ref, mo_ref, vo_ref):
        g = a_ref[...] + b_ref[...]
        delta, mn, vn = _adamw(w_ref[...], g, m_ref[...], v_ref[...])
        g_ref[...] = g
        d_ref[...] = delta
        mo_ref[...] = mn
        vo_ref[...] = vn

    blk = pl.BlockSpec((tr, c), lambda i: (i, 0))
    return pl.pallas_call(
        body, name=name, grid=(r // tr,),
        in_specs=[blk] * 5, out_specs=[blk] * 4,
        out_shape=[jax.ShapeDtypeStruct((r, c), f32)] * 4,
        compiler_params=_cparams(),
    )(p_mine, p_sibling, w, m, v)


def adamw_small(g8, w, m, v):
    _, r, lanes = g8.shape

    def body(g_ref, w_ref, m_ref, v_ref, go_ref, d_ref, mo_ref, vo_ref):
        g = g_ref[0]
        for i in range(1, N_DEV):
            g = g + g_ref[i]
        delta, mn, vn = _adamw(w_ref[...], g, m_ref[...], v_ref[...])
        go_ref[...] = g
        d_ref[...] = delta
        mo_ref[...] = mn
        vo_ref[...] = vn

    return pl.pallas_call(
        body, name="adamw_small",
        out_shape=[jax.ShapeDtypeStruct((r, lanes), f32)] * 4,
        compiler_params=_cparams(),
    )(g8, w, m, v)


def _size(shape):
    n = 1
    for e in shape:
        n *= e
    return n


def _pack_rows(shapes):
    rows = [-(-_size(s) // 1024) * 8 for s in shapes]
    return rows, sum(rows)


def _pack(arrs, shapes):
    rows, _ = _pack_rows(shapes)
    parts = [jnp.pad(a.reshape(-1).astype(f32), (0, r * 128 - _size(s))).reshape(r, 128)
             for a, s, r in zip(arrs, shapes, rows)]
    return jnp.concatenate(parts, axis=0)


def _unpack(block, shapes):
    rows, _ = _pack_rows(shapes)
    out, off = [], 0
    for s, r in zip(shapes, rows):
        out.append(block[off:off + r].reshape(-1)[:_size(s)].reshape(s))
        off += r
    return out


TRANSPOSED = ("ffn1_w_gate", "ffn1_w_up", "ffn2_w_gate", "ffn2_w_up")


def _shard2d(a, n):
    return a[0].T if n in TRANSPOSED else a[0]


def _unshard(a, n):
    return (a.T if n in TRANSPOSED else a)[None]


BIG = ("ffn1_w_gate", "ffn1_w_up", "ffn1_w_down", "w_in", "w_branch_a", "w_branch_b", "w_out",
       "ffn2_w_gate", "ffn2_w_up", "ffn2_w_down")
SMALL = ("ffn1_norm", "mix_norm", "b_in", "sgu_norm_g", "sgu_norm_b", "sgu_w_s", "sgu_b_s", "ret_decay_logit",
         "ffn2_norm", "final_norm")
WEIGHTS = ("ffn1_norm", "ffn1_w_gate", "ffn1_w_up", "ffn1_w_down", "mix_norm", "w_in", "b_in", "sgu_norm_g",
           "sgu_norm_b", "sgu_w_s", "sgu_b_s", "ret_decay_logit", "w_branch_a", "w_branch_b", "w_out", "ffn2_norm",
           "ffn2_w_gate", "ffn2_w_up", "ffn2_w_down", "final_norm")


def kernel(x, ffn1_norm, ffn1_w_gate, ffn1_w_up, ffn1_w_down, mix_norm, w_in, b_in, sgu_norm_g, sgu_norm_b, sgu_w_s, sgu_b_s, ret_decay_logit, w_branch_a, w_branch_b, w_out, ffn2_norm, ffn2_w_gate, ffn2_w_up, ffn2_w_down, final_norm, loss_target, m_ffn1_norm, m_ffn1_w_gate, m_ffn1_w_up, m_ffn1_w_down, m_mix_norm, m_w_in, m_b_in, m_sgu_norm_g, m_sgu_norm_b, m_sgu_w_s, m_sgu_b_s, m_ret_decay_logit, m_w_branch_a, m_w_branch_b, m_w_out, m_ffn2_norm, m_ffn2_w_gate, m_ffn2_w_up, m_ffn2_w_down, m_final_norm, v_ffn1_norm, v_ffn1_w_gate, v_ffn1_w_up, v_ffn1_w_down, v_mix_norm, v_w_in, v_b_in, v_sgu_norm_g, v_sgu_norm_b, v_sgu_w_s, v_sgu_b_s, v_ret_decay_logit, v_w_branch_a, v_w_branch_b, v_w_out, v_ffn2_norm, v_ffn2_w_gate, v_ffn2_w_up, v_ffn2_w_down, v_final_norm):
    p = dict(ffn1_norm=ffn1_norm, ffn1_w_gate=ffn1_w_gate, ffn1_w_up=ffn1_w_up, ffn1_w_down=ffn1_w_down,
             mix_norm=mix_norm, w_in=w_in, b_in=b_in, sgu_norm_g=sgu_norm_g, sgu_norm_b=sgu_norm_b, sgu_w_s=sgu_w_s,
             sgu_b_s=sgu_b_s, ret_decay_logit=ret_decay_logit, w_branch_a=w_branch_a, w_branch_b=w_branch_b,
             w_out=w_out, ffn2_norm=ffn2_norm, ffn2_w_gate=ffn2_w_gate, ffn2_w_up=ffn2_w_up, ffn2_w_down=ffn2_w_down,
             final_norm=final_norm)
    mom = dict(ffn1_norm=m_ffn1_norm, ffn1_w_gate=m_ffn1_w_gate, ffn1_w_up=m_ffn1_w_up, ffn1_w_down=m_ffn1_w_down,
               mix_norm=m_mix_norm, w_in=m_w_in, b_in=m_b_in, sgu_norm_g=m_sgu_norm_g, sgu_norm_b=m_sgu_norm_b,
               sgu_w_s=m_sgu_w_s, sgu_b_s=m_sgu_b_s, ret_decay_logit=m_ret_decay_logit, w_branch_a=m_w_branch_a,
               w_branch_b=m_w_branch_b, w_out=m_w_out, ffn2_norm=m_ffn2_norm, ffn2_w_gate=m_ffn2_w_gate,
               ffn2_w_up=m_ffn2_w_up, ffn2_w_down=m_ffn2_w_down, final_norm=m_final_norm)
    var = dict(ffn1_norm=v_ffn1_norm, ffn1_w_gate=v_ffn1_w_gate, ffn1_w_up=v_ffn1_w_up, ffn1_w_down=v_ffn1_w_down,
               mix_norm=v_mix_norm, w_in=v_w_in, b_in=v_b_in, sgu_norm_g=v_sgu_norm_g, sgu_norm_b=v_sgu_norm_b,
               sgu_w_s=v_sgu_w_s, sgu_b_s=v_sgu_b_s, ret_decay_logit=v_ret_decay_logit, w_branch_a=v_w_branch_a,
               w_branch_b=v_w_branch_b, w_out=v_w_out, ffn2_norm=v_ffn2_norm, ffn2_w_gate=v_ffn2_w_gate,
               ffn2_w_up=v_ffn2_w_up, ffn2_w_down=v_ffn2_w_down, final_norm=v_final_norm)

    xs = x[0]
    tgt = loss_target[0]
    t, d = xs.shape
    dk = d // RET_HEADS
    tm = _row_tile(t)

    shards2d = {n: _shard2d(p[n], n) for n in BIG}
    chip = (2 * lax.axis_index("x") + lax.axis_index("y")).astype(jnp.int32).reshape(1)
    groups = {"ffn1": ("ffn1_w_gate", "ffn1_w_up", "ffn1_w_down"), "in": ("w_in",),
              "mix": ("w_branch_a", "w_branch_b", "w_out"), "ffn2": ("ffn2_w_gate", "ffn2_w_up", "ffn2_w_down")}
    def own_slot(n, zero):
        sh = shards2d[n].astype(bf16) + zero
        return lax.dynamic_update_index_in_dim(lax.empty((N_CHIPS,) + sh.shape, bf16), sh, chip[0], 0)

    sems, bufs, tok = gather_start([own_slot(n, jnp.zeros((), bf16)) for n in groups["ffn1"]], [[0, 1, 2]],
                                   "gather_start_ffn1")
    gsem = {"ffn1": sems[0]}
    pending = dict(zip(groups["ffn1"], bufs))
    rest = [n for g in ("in", "mix", "ffn2") for n in groups[g]]
    sems, bufs, tok_rest = gather_start([own_slot(n, tok[0, 0].astype(bf16)) for n in rest],
                                 [[rest.index(n) for n in groups[g]] for g in ("in", "mix", "ffn2")], "gather_start_rest")
    gsem.update(zip(("in", "mix", "ffn2"), sems))
    pending.update(zip(rest, bufs))

    def arrive(gs, after):
        got = []
        for g in gs:
            got += gather_wait([pending[n] for n in groups[g]], gsem[g], after, "gather_wait_" + g)
        return gather_forward(got, "gather_forward_" + gs[0])

    bin4 = b_in.reshape(N_CHIPS, 1, 2 * d)
    ws_b = sgu_w_s[0].astype(bf16)
    bs_c = sgu_b_s[0][:, :, None]
    cols, mats, cdec, cos, sin = retention_constants(ret_decay_logit[0], t, dk, tok_rest[0, 0])

    wg1, wu1, wd1 = [_pair_shards(w) for w in arrive(["ffn1"], cos)]
    x1, g1, u1 = ffn_fwd(xs, ffn1_norm, wg1, wu1, wd1, "ffn1_fwd")
    win, = arrive(["in"], x1)
    proj, hb2 = inproj_fwd(x1, mix_norm, win, bin4, cos, sin)
    late = []
    for g in ("mix", "ffn2"):
        late += gather_wait([pending[n] for n in groups[g]], gsem[g], proj, "gather_wait_" + g)
    fsems, late, ftok = forward_start(late, "forward_start_mix")
    a = sgu_fwd(proj, sgu_norm_g, sgu_norm_b, ws_b, bs_c, ftok)
    r, rn = ret_fwd(proj, cols, mats, cdec)
    wa, wb, wo, wg2, wu2, wd2 = forward_wait(late, fsems, rn, "forward_wait_mix")
    wa, wb, wo = [w.reshape(d, d) for w in (wa, wb, wo)]
    wg2, wu2, wd2 = [_pair_shards(w) for w in (wg2, wu2, wd2)]
    x2, ba, br = mix_fwd(a, rn, proj, wa, wb, wo, x1)
    loss_blk, dx3, d_final, g2, u2 = ffn_fwd_loss(x2, ffn2_norm, wg2, wu2, wd2, final_norm.reshape(1, d), tgt, "ffn2_fwd")

    sent = {}
    dx2, dg2, du2, act2, hb3, dyb2, d_ffn2n = ffn_bwd_act(dx3, x2, ffn2_norm, g2, u2, wg2, wu2, wd2, "ffn2_bwd_act", tok)
    sent["ffn2"] = exchange_start(ffn_weight_grads(hb3, dyb2, dg2, du2, act2, "ffn2_grad", tok), "exchange_start_ffn2")
    da, drn, dga, dgb, mixb, dba, dbr, dx2b = mix_bwd_act(dx2, ba, br, proj, wa, wb, wo, sent["ffn2"][3])
    tg = min(t, 2048)
    row = pl.BlockSpec((tg, d), lambda s, i: (i, 0))

    def square_grad(xa, ya, name):
        return tn_matmul(xa, [ya], row, [row], 1, d, [d], t, tg, name, tok).reshape(N_CHIPS, d // N_CHIPS, d)

    sent["mix"] = exchange_start([square_grad(a, dba, "grad_w_branch_a"), square_grad(rn, dbr, "grad_w_branch_b"),
                                  square_grad(mixb, dx2b, "grad_w_out")], "exchange_start_mix")
    dua, dva, d_ws, d_bs, d_sng, d_snb = sgu_bwd(da, proj, sgu_norm_g, sgu_norm_b, ws_b, bs_c, sent["mix"][3])
    dq, dkr, dv, dgr, dlg = ret_bwd(drn, r, proj, cols, mats, cdec, cos, sin)
    segs = [dua, dva, dq, dkr, dv, dgr, dga, dgb]
    dx1, d_bin, d_mixn = inproj_bwd_act(segs, win, x1, mix_norm, dx2)
    g_in = None
    for s in range(N_CHIPS):
        g_in = tn_matmul(hb2, [segs[2 * s], segs[2 * s + 1]], row, [row, row], 1, d, [d, d], t, tg, "grad_w_in_%d" % s,
                         tok, (g_in, s, N_CHIPS))
    sent["in"] = exchange_start([g_in], "exchange_start_in")
    grad_x, dg1, du1, act1, hb1, dyb1, d_ffn1n = ffn_bwd_act(dx1, xs, ffn1_norm, g1, u1, wg1, wu1, wd1, "ffn1_bwd_act",
                                                              sent["in"][3])
    dlogit = dlg[:, 0:2, 0].T * jax.nn.sigmoid(-ret_decay_logit[0].astype(f32))
    small_g = dict(ffn1_norm=d_ffn1n, mix_norm=d_mixn, b_in=d_bin, sgu_norm_g=d_sng, sgu_norm_b=d_snb, sgu_w_s=d_ws,
                   sgu_b_s=d_bs, ret_decay_logit=dlogit, ffn2_norm=d_ffn2n, final_norm=d_final)
    shapes = [p[n].shape for n in SMALL]
    small_sems, small_blk, small_land, small_tok = small_start(_pack([small_g[n] for n in SMALL], shapes))

    def send_one(which, grad):
        n = "ffn1_" + which
        groups[n] = (n,)
        sent[n] = exchange_start([grad], "exchange_start_" + n)
        return sent[n][3]

    ffn_weight_grads(hb1, dyb1, dg1, du1, act1, "ffn1_grad", small_tok, send_one)

    out_g, out_d, out_m, out_v = {}, {}, {}, {}
    swaps = {}

    def reduce_plane(g, after):
        gsems, own, lands, _ = sent[g]
        own, lands = exchange_wait(own, lands, gsems, after, "exchange_wait_" + g)
        plane = [sum_partials(chip, o, l, "sum_" + n) for n, o, l in zip(groups[g], own, lands)]
        swaps[g] = swap_start(plane, "swap_start_" + g)
        return swaps[g][3]

    def update(g, after):
        ssems, plane, lands, _ = swaps[g]
        plane, other = swap_wait(plane, lands, ssems, after, "swap_wait_" + g)
        for n, mine, sib in zip(groups[g], plane, other):
            res = adamw_shard(mine, sib, shards2d[n], _shard2d(mom[n], n), _shard2d(var[n], n), "adamw_" + n)
            out_g[n], out_d[n], out_m[n], out_v[n] = [_unshard(o, n) for o in res]
        return res[0]

    after = reduce_plane("ffn2", sent["ffn1_w_down"][3])
    after = reduce_plane("mix", after)
    after = update("ffn2", after)
    after = reduce_plane("in", after)
    after = update("mix", after)
    g8 = small_wait(small_blk, small_land, small_sems, after)
    sg, sd, sm, sv = adamw_small(g8, _pack([p[n] for n in SMALL], shapes), _pack([mom[n] for n in SMALL], shapes),
                                 _pack([var[n] for n in SMALL], shapes))
    for res, blockv in ((out_g, sg), (out_d, sd), (out_m, sm), (out_v, sv)):
        for n, val in zip(SMALL, _unpack(blockv, shapes)):
            res[n] = val
    after = update("in", sg)
    after = reduce_plane("ffn1_w_gate", after)
    after = reduce_plane("ffn1_w_up", after)
    after = update("ffn1_w_gate", after)
    after = reduce_plane("ffn1_w_down", after)
    after = update("ffn1_w_up", after)
    update("ffn1_w_down", after)

    loss = lax.psum(loss_blk[0, 0], ("x", "y", "c"))
    return (loss, grad_x[None], *[out_g[n] for n in WEIGHTS], *[out_d[n] for n in WEIGHTS],
            *[out_m[n] for n in WEIGHTS], *[out_v[n] for n in WEIGHTS])
```

```python
import functools

import jax
import jax.numpy as jnp
from jax import lax
from jax.experimental import pallas as pl
from jax.experimental.pallas import tpu as pltpu
from jax.experimental.pallas import tpu_sc as plsc

f32 = jnp.float32
bf16 = jnp.bfloat16

SGU_CHUNK = 128
CHUNK = 128
RET_HEADS = 4
SGU_GROUPS = 4
ROPE_BASE = 10000.0
NORM_EPS = 1e-6
ADAM_LR = 0.001
ADAM_B1 = 0.9
ADAM_B2 = 0.999
ADAM_EPS = 1e-08
ADAM_WD = 0.01
ADAM_STEP = 10
N_CHIPS = 4
N_DEV = 8
MESH = pl.DeviceIdType.MESH
VMEM_LIMIT = 52 * 1024 * 1024
VMEM_LIMIT_WIDE = 62 * 1024 * 1024

_NT = (((1,), (1,)), ((), ()))
_TN = (((0,), (0,)), ((), ()))


def _cparams(limit=None):
    return pltpu.CompilerParams(vmem_limit_bytes=VMEM_LIMIT if limit is None else limit)


def _row_tile(t):
    return 512 if t >= 2048 else t // 2


def _dot(a, b):
    return jnp.dot(a, b, preferred_element_type=f32)


def _dot_nt(a, b):
    return lax.dot_general(a, b, _NT, preferred_element_type=f32)


def _dot_tn(a, b):
    return lax.dot_general(a, b, _TN, preferred_element_type=f32)


def _rms(x, g):
    r = lax.rsqrt(jnp.mean(x * x, axis=-1, keepdims=True) + NORM_EPS)
    xh = x * r
    return xh * g, xh, r


def _rms_bwd(dy, xh, r, g):
    dxh = dy * g
    return r * (dxh - xh * jnp.mean(dxh * xh, axis=-1, keepdims=True))


def _sigmoid(x):
    return jax.nn.sigmoid(x)


def _dsilu(g, sg):
    return sg * (1.0 + g * (1.0 - sg))


def _gelu(x):
    return 0.5 * x * (1.0 + lax.erf(x * 0.7071067811865476))


def _dgelu(x):
    return 0.5 * (1.0 + lax.erf(x * 0.7071067811865476)) + x * jnp.exp(-0.5 * x * x) * 0.3989422804014327


def _acc_out(ref, first, val):
    @pl.when(first)
    def _():
        ref[...] = val

    @pl.when(jnp.logical_not(first))
    def _():
        ref[...] += val


def _ffn_tile(t):
    return 256 if t >= 2048 else t // 2


def _ffn_fwd_rows(xx, ng_ref, wg_ref, wu_ref, wd_ref, g_ref, u_ref):
    y, _, _ = _rms(xx, ng_ref[...])
    h = y.astype(bf16)
    acc = None
    for s in range(wg_ref.shape[0]):
        g = _dot_nt(h, wg_ref[s])
        u = _dot_nt(h, wu_ref[s])
        g_ref[s] = g.astype(bf16)
        u_ref[s] = u.astype(bf16)
        part = _dot((g * _sigmoid(g) * u).astype(bf16), wd_ref[s])
        acc = part if acc is None else acc + part
    return xx + 0.5 * acc


def ffn_fwd(x, ng, wg, wu, wd, name):
    t, d = x.shape
    ns, fs, _ = wg.shape
    tm = _ffn_tile(t)

    def body(x_ref, ng_ref, wg_ref, wu_ref, wd_ref, xo_ref, g_ref, u_ref):
        xo_ref[...] = _ffn_fwd_rows(x_ref[...], ng_ref, wg_ref, wu_ref, wd_ref, g_ref, u_ref)

    row = pl.BlockSpec((tm, d), lambda i: (i, 0))
    shard = pl.BlockSpec((ns, tm, fs), lambda i: (0, i, 0))
    wspec = pl.BlockSpec((ns, fs, d), lambda i: (0, 0, 0), pipeline_mode=pl.Buffered(1))
    return pl.pallas_call(
        body, name=name, grid=(t // tm,),
        in_specs=[row, pl.BlockSpec((1, d), lambda i: (0, 0)), wspec, wspec, wspec],
        out_specs=[row, shard, shard],
        out_shape=[jax.ShapeDtypeStruct((t, d), f32), jax.ShapeDtypeStruct((ns, t, fs), bf16),
                   jax.ShapeDtypeStruct((ns, t, fs), bf16)],
        compiler_params=_cparams(),
    )(x, ng, wg, wu, wd)


def ffn_fwd_loss(x, ng, wg, wu, wd, fng, tgt, name):
    t, d = x.shape
    ns, fs, _ = wg.shape
    tm = _ffn_tile(t)

    def body(x_ref, ng_ref, wg_ref, wu_ref, wd_ref, fng_ref, t_ref, loss_ref, dx_ref, dfn_ref, g_ref, u_ref):
        i = pl.program_id(0)
        x3 = _ffn_fwd_rows(x_ref[...], ng_ref, wg_ref, wu_ref, wd_ref, g_ref, u_ref)
        y, xh, r = _rms(x3, fng_ref[...])
        diff = y - t_ref[...]
        part = 0.5 * jnp.sum(jnp.sum(diff * diff, axis=0, keepdims=True), axis=1, keepdims=True) / d
        _acc_out(loss_ref, i == 0, jnp.broadcast_to(part, (1, 128)))
        dy = diff * (1.0 / d)
        dx_ref[...] = _rms_bwd(dy, xh, r, fng_ref[...])
        _acc_out(dfn_ref, i == 0, jnp.sum(dy * xh, axis=0, keepdims=True))

    row = pl.BlockSpec((tm, d), lambda i: (i, 0))
    vec = pl.BlockSpec((1, d), lambda i: (0, 0))
    shard = pl.BlockSpec((ns, tm, fs), lambda i: (0, i, 0))
    wspec = pl.BlockSpec((ns, fs, d), lambda i: (0, 0, 0), pipeline_mode=pl.Buffered(1))
    return pl.pallas_call(
        body, name=name, grid=(t // tm,),
        in_specs=[row, vec, wspec, wspec, wspec, vec, row],
        out_specs=[pl.BlockSpec((1, 128), lambda i: (0, 0)), row, vec, shard, shard],
        out_shape=[jax.ShapeDtypeStruct((1, 128), f32), jax.ShapeDtypeStruct((t, d), f32), jax.ShapeDtypeStruct((1, d), f32),
                   jax.ShapeDtypeStruct((ns, t, fs), bf16), jax.ShapeDtypeStruct((ns, t, fs), bf16)],
        compiler_params=_cparams(),
    )(x, ng, wg, wu, wd, fng, tgt)


def ffn_bwd_act(dxo, x, ng, g, u, wg, wu, wd, name, dep):
    t, d = x.shape
    ns, fs, _ = wg.shape
    tm = _ffn_tile(t)

    def body(dxo_ref, x_ref, ng_ref, g_ref, u_ref, wg_ref, wu_ref, wd_ref, dep_ref,
             dx_ref, dg_ref, du_ref, act_ref, hb_ref, dyb_ref, dng_ref):
        i = pl.program_id(0)
        dxo = dxo_ref[...]
        dyb = (0.5 * dxo).astype(bf16)
        dyb_ref[...] = dyb
        dh = None
        for s in range(ns):
            dact = _dot_nt(dyb, wd_ref[s])
            gg = g_ref[s].astype(f32)
            uu = u_ref[s].astype(f32)
            sg = _sigmoid(gg)
            sil = gg * sg
            dgb = (dact * uu * _dsilu(gg, sg)).astype(bf16)
            dub = (dact * sil).astype(bf16)
            dg_ref[s] = dgb
            du_ref[s] = dub
            act_ref[s] = (sil * uu).astype(bf16)
            part = _dot(dgb, wg_ref[s]) + _dot(dub, wu_ref[s])
            dh = part if dh is None else dh + part
        y, xh, r = _rms(x_ref[...], ng_ref[...])
        hb_ref[...] = y.astype(bf16)
        dx_ref[...] = dxo + _rms_bwd(dh, xh, r, ng_ref[...])
        _acc_out(dng_ref, i == 0, jnp.sum(dh * xh, axis=0, keepdims=True))

    row = pl.BlockSpec((tm, d), lambda i: (i, 0))
    shard = pl.BlockSpec((ns, tm, fs), lambda i: (0, i, 0))
    wspec = pl.BlockSpec((ns, fs, d), lambda i: (0, 0, 0), pipeline_mode=pl.Buffered(1))
    vec = pl.BlockSpec((1, d), lambda i: (0, 0))
    return pl.pallas_call(
        body, name=name, grid=(t // tm,),
        in_specs=[row, row, vec, shard, shard, wspec, wspec, wspec, _ANY],
        out_specs=[row, shard, shard, shard, row, row, vec],
        out_shape=[jax.ShapeDtypeStruct((t, d), f32)] + [jax.ShapeDtypeStruct((ns, t, fs), bf16)] * 3
        + [jax.ShapeDtypeStruct((t, d), bf16)] * 2 + [jax.ShapeDtypeStruct((1, d), f32)],
        compiler_params=_cparams(VMEM_LIMIT_WIDE),
    )(dxo, x, ng, g, u, wg, wu, wd, dep)


def tn_matmul(xs, ys, x_spec, y_specs, n_shards, k1, k2s, t, tm, name, dep, into=None):
    k2 = sum(k2s)
    ny = len(ys)

    def body(*refs):
        x_ref = refs[0]
        y_refs = refs[1:1 + ny]
        o_ref, acc = refs[-2], refs[-1]
        i = pl.program_id(1)
        xb = x_ref[0] if len(x_ref.shape) == 3 else x_ref[...]
        off = 0
        for y_ref, w in zip(y_refs, k2s):
            yb = y_ref[0] if len(y_ref.shape) == 3 else y_ref[...]
            part = _dot_tn(xb, yb)
            sl = (slice(None), slice(off, off + w))

            @pl.when(i == 0)
            def _(part=part, sl=sl):
                acc[sl] = part

            @pl.when(i > 0)
            def _(part=part, sl=sl):
                acc[sl] += part

            off += w

        @pl.when(i == t // tm - 1)
        def _():
            o_ref[0] = acc[...].astype(bf16)

    if into is None:
        slot0, total, extra, aliases = 0, n_shards, [], {}
    else:
        buf, slot0, total = into
        extra = [] if buf is None else [buf]
        aliases = {} if buf is None else {2 + ny: 0}
    return pl.pallas_call(
        body, name=name, grid=(n_shards, t // tm),
        in_specs=[x_spec] + list(y_specs) + [_ANY] * (1 + len(extra)),
        out_specs=pl.BlockSpec((1, k1, k2), lambda s, i: (slot0 + s, 0, 0)),
        out_shape=jax.ShapeDtypeStruct((total, k1, k2), bf16),
        scratch_shapes=[pltpu.VMEM((k1, k2), f32)],
        input_output_aliases=aliases,
        compiler_params=_cparams(),
    )(xs, *ys, dep, *extra)


def _pair_shards(w):
    s4, fs, d = w.shape
    return w.reshape(s4 // 2, 2 * fs, d)


def ffn_weight_grads(hb, dyb, dg, du, act, name, dep, each=None):
    t, d = hb.shape
    s2, _, fs2 = dg.shape
    tm = t
    row = pl.BlockSpec((tm, d), lambda s, i: (i, 0))
    shard = pl.BlockSpec((1, tm, fs2), lambda s, i: (s, i, 0))
    grads = []
    for xa, ya, which in ((dg, hb, "w_gate"), (du, hb, "w_up"), (act, dyb, "w_down")):
        g = tn_matmul(xa, [ya], shard, [row], s2, fs2, [d], t, tm, name + "_" + which, dep)
        g = g.reshape(2 * s2, fs2 // 2, d)
        if each is not None:
            dep = each(which, g)
        grads.append(g)
    return grads


def inproj_fwd(x1, ng, win, bin4, cos, sin):
    t, d = x1.shape
    s4, _, w2 = win.shape
    tm = _row_tile(t)
    dk = d // RET_HEADS
    scale = dk ** -0.5

    def body(x_ref, ng_ref, w_ref, b_ref, cos_ref, sin_ref, p_ref, hb_ref):
        y, _, _ = _rms(x_ref[...], ng_ref[...])
        h = y.astype(bf16)
        hb_ref[...] = h
        for s in range(s4):
            p = _dot(h, w_ref[s]) + b_ref[s]
            if s != 1:
                p_ref[s] = p.astype(bf16)
            else:
                cs, sn = cos_ref[...], sin_ref[...]
                for e in range(2 * RET_HEADS):
                    cols = slice(e * dk, (e + 1) * dk)
                    rot = _rot(p[:, cols], cs, sn)
                    p_ref[s, :, cols] = (rot if e < RET_HEADS else rot * scale).astype(bf16)

    tab = pl.BlockSpec((tm, dk // 2), lambda i: (i, 0))
    return pl.pallas_call(
        body, name="inproj_fwd", grid=(t // tm,),
        in_specs=[pl.BlockSpec((tm, d), lambda i: (i, 0)), pl.BlockSpec((1, d), lambda i: (0, 0)),
                  pl.BlockSpec((s4, d, w2), lambda i: (0, 0, 0), pipeline_mode=pl.Buffered(1)),
                  pl.BlockSpec((s4, 1, w2), lambda i: (0, 0, 0)), tab, tab],
        out_specs=[pl.BlockSpec((s4, tm, w2), lambda i: (0, i, 0)), pl.BlockSpec((tm, d), lambda i: (i, 0))],
        out_shape=[jax.ShapeDtypeStruct((s4, t, w2), bf16), jax.ShapeDtypeStruct((t, d), bf16)],
        compiler_params=_cparams(),
    )(x1, ng, win, bin4, cos, sin)


def _sgu_norm(va, ng, nb):
    gv = _gelu(va)
    mu = jnp.mean(gv, axis=-1, keepdims=True)
    xc = gv - mu
    rstd = lax.rsqrt(jnp.mean(xc * xc, axis=-1, keepdims=True) + NORM_EPS)
    xh = xc * rstd
    return xh, rstd, (xh * ng + nb).astype(bf16)


def sgu_fwd(proj, ng, nb, ws, bs, dep):
    _, t, w2 = proj.shape
    d = w2 // 2
    gd = d // SGU_GROUPS
    tm = _row_tile(t)

    def body(p_ref, ng_ref, nb_ref, ws_ref, bs_ref, dep_ref, a_ref):
        ua = p_ref[0, :, 0:d].astype(f32)
        va = p_ref[0, :, d:w2].astype(f32)
        gu = _gelu(ua)
        _, _, vn = _sgu_norm(va, ng_ref[...], nb_ref[...])
        for c in range(tm // SGU_CHUNK):
            rows = slice(c * SGU_CHUNK, (c + 1) * SGU_CHUNK)
            for g in range(SGU_GROUPS):
                cols = slice(g * gd, (g + 1) * gd)
                sg = _dot(ws_ref[g], vn[rows, cols]) + bs_ref[g]
                a_ref[rows, cols] = (gu[rows, cols] * sg).astype(bf16)

    return pl.pallas_call(
        body, name="sgu_fwd", grid=(t // tm,),
        in_specs=[pl.BlockSpec((1, tm, w2), lambda i: (0, i, 0)), pl.BlockSpec((1, d), lambda i: (0, 0)),
                  pl.BlockSpec((1, d), lambda i: (0, 0)), pl.BlockSpec((SGU_GROUPS, SGU_CHUNK, SGU_CHUNK), lambda i: (0, 0, 0)),
                  pl.BlockSpec((SGU_GROUPS, SGU_CHUNK, 1), lambda i: (0, 0, 0)), _ANY],
        out_specs=pl.BlockSpec((tm, d), lambda i: (i, 0)),
        out_shape=jax.ShapeDtypeStruct((t, d), bf16),
        compiler_params=_cparams(),
    )(proj, ng, nb, ws, bs, dep)


def sgu_bwd(da, proj, ng, nb, ws, bs, dep):
    _, t, w2 = proj.shape
    d = w2 // 2
    gd = d // SGU_GROUPS
    tm = _row_tile(t)

    def body(da_ref, p_ref, ng_ref, nb_ref, ws_ref, bs_ref, dep_ref,
             dua_ref, dva_ref, dws_ref, dbs_ref, dng_ref, dnb_ref, dvn_scr):
        i = pl.program_id(0)
        ua = p_ref[0, :, 0:d].astype(f32)
        va = p_ref[0, :, d:w2].astype(f32)
        gu = _gelu(ua)
        xh, rstd, vn = _sgu_norm(va, ng_ref[...], nb_ref[...])
        dad = da_ref[...].astype(f32)
        dsb = (dad * gu).astype(bf16)
        for c in range(tm // SGU_CHUNK):
            rows = slice(c * SGU_CHUNK, (c + 1) * SGU_CHUNK)
            for g in range(SGU_GROUPS):
                cols = slice(g * gd, (g + 1) * gd)
                sg = _dot(ws_ref[g], vn[rows, cols]) + bs_ref[g]
                dua_ref[rows, cols] = (dad[rows, cols] * sg * _dgelu(ua[rows, cols])).astype(bf16)
                ds = dsb[rows, cols]
                dvn_scr[rows, cols] = _dot_tn(ws_ref[g], ds)
                dw = _dot_nt(ds, vn[rows, cols])
                db = jnp.sum(ds.astype(f32), axis=1, keepdims=True)
                if c == 0:
                    _acc_out(dws_ref.at[g], i == 0, dw)
                    _acc_out(dbs_ref.at[g], i == 0, db)
                else:
                    dws_ref[g] += dw
                    dbs_ref[g] += db
        dvn = dvn_scr[...]
        _acc_out(dng_ref, i == 0, jnp.sum(dvn * xh, axis=0, keepdims=True))
        _acc_out(dnb_ref, i == 0, jnp.sum(dvn, axis=0, keepdims=True))
        dxh = dvn * ng_ref[...]
        dgv = rstd * (dxh - jnp.mean(dxh, axis=-1, keepdims=True) - xh * jnp.mean(dxh * xh, axis=-1, keepdims=True))
        dva_ref[...] = (dgv * _dgelu(va)).astype(bf16)

    row = pl.BlockSpec((tm, d), lambda i: (i, 0))
    vec = pl.BlockSpec((1, d), lambda i: (0, 0))
    wsp = pl.BlockSpec((SGU_GROUPS, SGU_CHUNK, SGU_CHUNK), lambda i: (0, 0, 0))
    bsp = pl.BlockSpec((SGU_GROUPS, SGU_CHUNK, 1), lambda i: (0, 0, 0))
    return pl.pallas_call(
        body, name="sgu_bwd", grid=(t // tm,),
        in_specs=[row, pl.BlockSpec((1, tm, w2), lambda i: (0, i, 0)), vec, vec, wsp, bsp, _ANY],
        out_specs=[row, row, wsp, bsp, vec, vec],
        out_shape=[jax.ShapeDtypeStruct((t, d), bf16), jax.ShapeDtypeStruct((t, d), bf16),
                   jax.ShapeDtypeStruct((SGU_GROUPS, SGU_CHUNK, SGU_CHUNK), f32), jax.ShapeDtypeStruct((SGU_GROUPS, SGU_CHUNK, 1), f32),
                   jax.ShapeDtypeStruct((1, d), f32), jax.ShapeDtypeStruct((1, d), f32)],
        scratch_shapes=[pltpu.VMEM((tm, d), f32)],
        compiler_params=_cparams(),
    )(da, proj, ng, nb, ws, bs, dep)


def retention_constants(decay_logit, t, dk, zero):
    lg = jax.nn.log_sigmoid(decay_logit.astype(f32) + zero)
    lgf = lg[0][:, None]
    lgb = lg[1][:, None]
    idx = jnp.arange(CHUNK, dtype=f32)[None, :]
    af = jnp.exp((idx + 1.0) * lgf)
    ab = jnp.exp((CHUNK - idx) * lgb)
    kf = jnp.exp((CHUNK - 1.0 - idx) * lgf)
    kb = jnp.exp(idx * lgb)
    cols = jnp.stack([af, ab, kf, kb, af * (idx + 1.0), ab * (CHUNK - idx), kf * (CHUNK - 1.0 - idx), kb * idx], axis=1)
    cols = cols[..., None]
    diff = idx[0][:, None] - idx[0][None, :]
    dfm = jnp.where(diff >= 0, jnp.exp(jnp.maximum(diff, 0.0)[None] * lgf[:, :, None]), 0.0)
    dbm = jnp.where(diff < 0, jnp.exp(jnp.maximum(-diff, 0.0)[None] * lgb[:, :, None]), 0.0)
    mats = jnp.stack([dfm + dbm, dfm * diff[None], dbm * (-diff)[None]], axis=1)
    cdec = jnp.stack([jnp.broadcast_to(jnp.exp(CHUNK * lgf), (RET_HEADS, dk)),
                      jnp.broadcast_to(jnp.exp(CHUNK * lgb), (RET_HEADS, dk))], axis=1)
    theta = ROPE_BASE ** (-jnp.arange(0, dk, 2, dtype=f32) / dk)
    ang = (jnp.arange(t, dtype=f32) + zero)[:, None] * theta[None, :]
    return cols, mats, cdec, jnp.cos(ang), jnp.sin(ang)


def _rot(tr, cos, sin):
    half = tr.shape[-1] // 2
    t1 = tr[:, :half]
    t2 = tr[:, half:]
    return jnp.concatenate([t1 * cos - t2 * sin, t2 * cos + t1 * sin], axis=-1)


def _rot_inv(dt, cos, sin):
    half = dt.shape[-1] // 2
    d1 = dt[:, :half]
    d2 = dt[:, half:]
    return jnp.concatenate([d1 * cos + d2 * sin, d2 * cos - d1 * sin], axis=-1)


def _ret_specs(t, d, dk, rt):
    nr = t // rt
    hq = d // dk

    def blk(p, n):
        return (1 - p) * (nr - 1 - n) + p * n

    q_spec = pl.BlockSpec((1, rt, dk), lambda h, p, n: (1, blk(p, n), h))
    k_spec = pl.BlockSpec((1, rt, dk), lambda h, p, n: (1, blk(p, n), hq + h))
    v_spec = pl.BlockSpec((1, rt, dk), lambda h, p, n: (2, blk(p, n), h))
    g_spec = pl.BlockSpec((1, rt, dk), lambda h, p, n: (2, blk(p, n), hq + h))
    tab_spec = pl.BlockSpec((rt, dk // 2), lambda h, p, n: (blk(p, n), 0))
    cols_spec = pl.BlockSpec((1, 8, CHUNK, 1), lambda h, p, n: (h, 0, 0, 0))
    mats_spec = pl.BlockSpec((1, 3, CHUNK, CHUNK), lambda h, p, n: (h, 0, 0, 0))
    cdec_spec = pl.BlockSpec((1, 2, dk), lambda h, p, n: (h, 0, 0))
    in_row = pl.BlockSpec((rt, dk), lambda h, p, n: (blk(p, n), h))
    out_row = pl.BlockSpec((rt, dk), lambda h, p, n: (p * n, h))
    return nr, blk, q_spec, k_spec, v_spec, g_spec, tab_spec, cols_spec, mats_spec, cdec_spec, in_row, out_row


def ret_fwd(proj, cols, mats, cdec):
    _, t, w2 = proj.shape
    d = w2 // 2
    dk = d // RET_HEADS
    rt = _row_tile(t)
    cpt = rt // CHUNK
    nr, blk, q_spec, k_spec, v_spec, g_spec, _, cols_spec, mats_spec, cdec_spec, _, out_row = _ret_specs(t, d, dk, rt)

    def body(q_ref, k_ref, v_ref, g_ref, cols_ref, mats_ref, cdec_ref, r_ref, rn_ref, sb_scr, st):
        p = pl.program_id(1)
        n = pl.program_id(2)
        af, ab, kf, kb = cols_ref[0, 0], cols_ref[0, 1], cols_ref[0, 2], cols_ref[0, 3]
        cf = cdec_ref[0, 0:1, :]
        cb = cdec_ref[0, 1:2, :]

        @pl.when(n == 0)
        def _():
            st[...] = jnp.zeros_like(st)

        @pl.when(p == 0)
        def _():
            for j in reversed(range(cpt)):
                rows = slice(j * CHUNK, (j + 1) * CHUNK)
                ch = blk(p, n) * cpt + j
                kk = k_ref[0, rows, :].astype(f32)
                sb_scr[ch] = st[...].astype(bf16)
                st[...] = st[...] * cb + _dot_tn((kk * kb).astype(bf16), v_ref[0, rows, :])

        @pl.when(p == 1)
        def _():
            for j in range(cpt):
                rows = slice(j * CHUNK, (j + 1) * CHUNK)
                ch = blk(p, n) * cpt + j
                qb = q_ref[0, rows, :]
                kkb = k_ref[0, rows, :]
                q = qb.astype(f32)
                kk = kkb.astype(f32)
                v = v_ref[0, rows, :]
                pm = (_dot_nt(qb, kkb) * mats_ref[0, 0]).astype(bf16)
                out = (_dot(pm, v) + _dot((q * af).astype(bf16), st[...].astype(bf16))
                       + _dot((q * ab).astype(bf16), sb_scr[ch]))
                st[...] = st[...] * cf + _dot_tn((kk * kf).astype(bf16), v)
                rhat = out * lax.rsqrt(jnp.mean(out * out, axis=-1, keepdims=True) + NORM_EPS)
                gg = g_ref[0, rows, :].astype(f32)
                r_ref[rows, :] = out.astype(bf16)
                rn_ref[rows, :] = (rhat * gg * _sigmoid(gg)).astype(bf16)

    return pl.pallas_call(
        body, name="ret_fwd", grid=(RET_HEADS, 2, nr),
        in_specs=[q_spec, k_spec, v_spec, g_spec, cols_spec, mats_spec, cdec_spec],
        out_specs=[out_row, out_row],
        out_shape=[jax.ShapeDtypeStruct((t, d), bf16), jax.ShapeDtypeStruct((t, d), bf16)],
        scratch_shapes=[pltpu.VMEM((t // CHUNK, dk, dk), bf16), pltpu.VMEM((dk, dk), f32)],
        compiler_params=_cparams(),
    )(proj, proj, proj, proj, cols, mats, cdec)


def ret_bwd(drn, r, proj, cols, mats, cdec, cos, sin):
    _, t, w2 = proj.shape
    d = w2 // 2
    dk = d // RET_HEADS
    rt = _row_tile(t)
    cpt = rt // CHUNK
    nr, blk, q_spec, k_spec, v_spec, g_spec, tab_spec, cols_spec, mats_spec, cdec_spec, in_row, out_row = _ret_specs(t, d, dk, rt)
    scale = dk ** -0.5

    def body(drn_ref, r_ref, q_ref, k_ref, v_ref, g_ref, cos_ref, sin_ref, cols_ref, mats_ref, cdec_ref,
             dq_ref, dk_ref, dv_ref, dg_ref, dlg_ref,
             sb_scr, gf_scr, st_s, st_g, acc_af, acc_ab, acc_vf, acc_vb, acc_sf, acc_sb, dout_scr, dgr_scr):
        p = pl.program_id(1)
        n = pl.program_id(2)
        af, ab, kf, kb = cols_ref[0, 0], cols_ref[0, 1], cols_ref[0, 2], cols_ref[0, 3]
        af1, ab1, kf1, kb1 = cols_ref[0, 4], cols_ref[0, 5], cols_ref[0, 6], cols_ref[0, 7]
        cf = cdec_ref[0, 0:1, :]
        cb = cdec_ref[0, 1:2, :]

        @pl.when(n == 0)
        def _():
            st_s[...] = jnp.zeros_like(st_s)
            st_g[...] = jnp.zeros_like(st_g)

        @pl.when(jnp.logical_and(n == 0, p == 1))
        def _():
            for a in (acc_af, acc_ab, acc_vf, acc_vb, acc_sf, acc_sb):
                a[...] = jnp.zeros_like(a)

        def load(rows):
            cs, sn = cos_ref[rows, :], sin_ref[rows, :]
            q = q_ref[0, rows, :].astype(f32)
            kk = k_ref[0, rows, :].astype(f32)
            rr = r_ref[rows, :].astype(f32)
            rstd = lax.rsqrt(jnp.mean(rr * rr, axis=-1, keepdims=True) + NORM_EPS)
            rhat = rr * rstd
            gg = g_ref[0, rows, :].astype(f32)
            sg = _sigmoid(gg)
            dd = drn_ref[rows, :].astype(f32)
            drhat = dd * gg * sg
            dout = rstd * (drhat - rhat * jnp.mean(drhat * rhat, axis=-1, keepdims=True))
            dgr = dd * rhat * _dsilu(gg, sg)
            return q, kk, dout.astype(bf16), dgr, cs, sn

        @pl.when(p == 0)
        def _():
            for j in reversed(range(cpt)):
                rows = slice(j * CHUNK, (j + 1) * CHUNK)
                ch = blk(p, n) * cpt + j
                q, kk, doutb, dgr, _, _ = load(rows)
                kept = pl.ds(pl.multiple_of(ch * CHUNK, CHUNK), CHUNK)
                dout_scr[kept, :] = doutb
                dgr_scr[kept, :] = dgr.astype(bf16)
                sb_scr[ch] = st_s[...].astype(bf16)
                gf_scr[ch] = st_g[...].astype(bf16)
                st_s[...] = st_s[...] * cb + _dot_tn((kk * kb).astype(bf16), v_ref[0, rows, :])
                st_g[...] = st_g[...] * cf + _dot_tn((q * af).astype(bf16), doutb)

        @pl.when(p == 1)
        def _():
            for j in range(cpt):
                rows = slice(j * CHUNK, (j + 1) * CHUNK)
                ch = blk(p, n) * cpt + j
                kept = pl.ds(pl.multiple_of(ch * CHUNK, CHUNK), CHUNK)
                doutb = dout_scr[kept, :]
                cs, sn = cos_ref[rows, :], sin_ref[rows, :]
                v = v_ref[0, rows, :]
                qb = q_ref[0, rows, :]
                kkb = k_ref[0, rows, :]
                q = qb.astype(f32)
                kk = kkb.astype(f32)
                sf = st_s[...]
                gb = st_g[...]
                sfb = sf.astype(bf16)
                gbb = gb.astype(bf16)
                sbb = sb_scr[ch]
                gfb = gf_scr[ch]
                dmat = mats_ref[0, 0]
                scores = _dot_nt(qb, kkb)
                dpraw = _dot_nt(doutb, v)
                dpb = (dpraw * dmat).astype(bf16)
                pmb = (scores * dmat).astype(bf16)
                x1 = _dot_nt(doutb, sfb)
                x2 = _dot_nt(doutb, sbb)
                y1 = _dot_nt(v, gfb)
                y2 = _dot_nt(v, gbb)
                kdf = (kk * kf).astype(bf16)
                kdb = (kk * kb).astype(bf16)
                dq = _dot(dpb, kkb) + x1 * af + x2 * ab
                dkk = _dot_tn(dpb, qb) + y1 * kf + y2 * kb
                dv = _dot_tn(pmb, doutb) + _dot(kdf, gfb) + _dot(kdb, gbb)
                ps = dpraw * scores
                acc_af[...] += ps * mats_ref[0, 1]
                acc_ab[...] += ps * mats_ref[0, 2]
                acc_vf[...] += x1 * q * af1 + y1 * kk * kf1
                acc_vb[...] += x2 * q * ab1 + y2 * kk * kb1
                acc_sf[...] += gfb.astype(f32) * sf
                acc_sb[...] += gb * sbb.astype(f32)
                st_s[...] = sf * cf + _dot_tn(kdf, v)
                st_g[...] = gb * cb + _dot_tn((q * ab).astype(bf16), doutb)
                dq_ref[rows, :] = _rot_inv(dq, cs, sn).astype(bf16)
                dk_ref[rows, :] = (_rot_inv(dkk, cs, sn) * scale).astype(bf16)
                dv_ref[rows, :] = dv.astype(bf16)
                dg_ref[rows, :] = dgr_scr[kept, :]

        @pl.when(jnp.logical_and(p == 1, n == nr - 1))
        def _():
            tf = jnp.sum(acc_af[...]) + jnp.sum(acc_vf[...]) + CHUNK * jnp.sum(acc_sf[...] * cf)
            tb = jnp.sum(acc_ab[...]) + jnp.sum(acc_vb[...]) + CHUNK * jnp.sum(acc_sb[...] * cb)
            rid = lax.broadcasted_iota(jnp.int32, (8, 128), 0)
            dlg_ref[0] = jnp.where(rid == 0, tf, jnp.where(rid == 1, tb, 0.0))

    nch = t // CHUNK
    return pl.pallas_call(
        body, name="ret_bwd", grid=(RET_HEADS, 2, nr),
        in_specs=[in_row, in_row, q_spec, k_spec, v_spec, g_spec, tab_spec, tab_spec, cols_spec, mats_spec, cdec_spec],
        out_specs=[out_row, out_row, out_row, out_row, pl.BlockSpec((1, 8, 128), lambda h, p, n: (h, 0, 0))],
        out_shape=[jax.ShapeDtypeStruct((t, d), bf16)] * 4 + [jax.ShapeDtypeStruct((RET_HEADS, 8, 128), f32)],
        scratch_shapes=[pltpu.VMEM((nch, dk, dk), bf16), pltpu.VMEM((nch, dk, dk), bf16),
                        pltpu.VMEM((dk, dk), f32), pltpu.VMEM((dk, dk), f32),
                        pltpu.VMEM((CHUNK, CHUNK), f32), pltpu.VMEM((CHUNK, CHUNK), f32),
                        pltpu.VMEM((CHUNK, dk), f32), pltpu.VMEM((CHUNK, dk), f32),
                        pltpu.VMEM((dk, dk), f32), pltpu.VMEM((dk, dk), f32),
                        pltpu.VMEM((t, dk), bf16), pltpu.VMEM((t, dk), bf16)],
        compiler_params=_cparams(),
    )(drn, r, proj, proj, proj, proj, cos, sin, cols, mats, cdec)


def mix_fwd(a, rn, proj, wa, wb, wo, x1):
    t, d = x1.shape
    tm = _row_tile(t)

    def body(a_ref, rn_ref, p_ref, wa_ref, wb_ref, wo_ref, x_ref, xo_ref, ba_ref, br_ref):
        ba = _dot(a_ref[...], wa_ref[...])
        br = _dot(rn_ref[...], wb_ref[...])
        sa = _sigmoid(p_ref[0, :, 0:d].astype(f32))
        sb = _sigmoid(p_ref[0, :, d:2 * d].astype(f32))
        mix = (sa * ba + sb * br).astype(bf16)
        xo_ref[...] = x_ref[...] + _dot(mix, wo_ref[...])
        ba_ref[...] = ba.astype(bf16)
        br_ref[...] = br.astype(bf16)

    row = pl.BlockSpec((tm, d), lambda i: (i, 0))
    wsp = pl.BlockSpec((d, d), lambda i: (0, 0))
    return pl.pallas_call(
        body, name="mix_fwd", grid=(t // tm,),
        in_specs=[row, row, pl.BlockSpec((1, tm, 2 * d), lambda i: (3, i, 0)), wsp, wsp, wsp, row],
        out_specs=[row, row, row],
        out_shape=[jax.ShapeDtypeStruct((t, d), f32), jax.ShapeDtypeStruct((t, d), bf16), jax.ShapeDtypeStruct((t, d), bf16)],
        compiler_params=_cparams(),
    )(a, rn, proj, wa, wb, wo, x1)


def mix_bwd_act(dx2, ba, br, proj, wa, wb, wo, dep):
    t, d = dx2.shape
    tm = _row_tile(t)

    def body(dx_ref, ba_ref, br_ref, p_ref, wa_ref, wb_ref, wo_ref, dep_ref,
             da_ref, drn_ref, dga_ref, dgb_ref, mix_ref, dba_ref, dbr_ref, dxb_ref):
        dxb = dx_ref[...].astype(bf16)
        dxb_ref[...] = dxb
        dmix = _dot_nt(dxb, wo_ref[...])
        ba = ba_ref[...].astype(f32)
        br = br_ref[...].astype(f32)
        sa = _sigmoid(p_ref[0, :, 0:d].astype(f32))
        sb = _sigmoid(p_ref[0, :, d:2 * d].astype(f32))
        mix_ref[...] = (sa * ba + sb * br).astype(bf16)
        dba = (dmix * sa).astype(bf16)
        dbr = (dmix * sb).astype(bf16)
        dba_ref[...] = dba
        dbr_ref[...] = dbr
        dga_ref[...] = (dmix * ba * sa * (1.0 - sa)).astype(bf16)
        dgb_ref[...] = (dmix * br * sb * (1.0 - sb)).astype(bf16)
        da_ref[...] = _dot_nt(dba, wa_ref[...]).astype(bf16)
        drn_ref[...] = _dot_nt(dbr, wb_ref[...]).astype(bf16)

    row = pl.BlockSpec((tm, d), lambda i: (i, 0))
    wsp = pl.BlockSpec((d, d), lambda i: (0, 0))
    return pl.pallas_call(
        body, name="mix_bwd_act", grid=(t // tm,),
        in_specs=[row, row, row, pl.BlockSpec((1, tm, 2 * d), lambda i: (3, i, 0)), wsp, wsp, wsp, _ANY],
        out_specs=[row] * 8,
        out_shape=[jax.ShapeDtypeStruct((t, d), bf16)] * 8,
        compiler_params=_cparams(),
    )(dx2, ba, br, proj, wa, wb, wo, dep)


def inproj_bwd_act(segs, win, x1, ng, dx2):
    t, d = x1.shape
    s4 = win.shape[0]
    tm = _row_tile(t)
    nseg = len(segs)

    def body(*refs):
        seg_refs = refs[:nseg]
        w_ref, x_ref, ng_ref, dx2_ref, dx1_ref, db_ref, dng_ref = refs[nseg:]
        i = pl.program_id(0)
        dh = None
        for e, sr in enumerate(seg_refs):
            sb = sr[...]
            part = _dot_nt(sb, w_ref[e // 2, :, (e % 2) * d:(e % 2 + 1) * d])
            dh = part if dh is None else dh + part
            _acc_out(db_ref.at[e], i == 0, jnp.sum(sb.astype(f32), axis=0, keepdims=True))
        _, xh, r = _rms(x_ref[...], ng_ref[...])
        dx1_ref[...] = dx2_ref[...] + _rms_bwd(dh, xh, r, ng_ref[...])
        _acc_out(dng_ref, i == 0, jnp.sum(dh * xh, axis=0, keepdims=True))

    row = pl.BlockSpec((tm, d), lambda i: (i, 0))
    vec = pl.BlockSpec((1, d), lambda i: (0, 0))
    return pl.pallas_call(
        body, name="inproj_bwd_act", grid=(t // tm,),
        in_specs=[row] * nseg + [pl.BlockSpec((s4, d, 2 * d), lambda i: (0, 0, 0), pipeline_mode=pl.Buffered(1)),
                                 row, vec, row],
        out_specs=[row, pl.BlockSpec((nseg, 1, d), lambda i: (0, 0, 0)), vec],
        out_shape=[jax.ShapeDtypeStruct((t, d), f32), jax.ShapeDtypeStruct((nseg, 1, d), f32),
                   jax.ShapeDtypeStruct((1, d), f32)],
        compiler_params=_cparams(VMEM_LIMIT_WIDE),
    )(*segs, win, x1, ng, dx2)


def _place():
    return lax.axis_index("x"), lax.axis_index("y"), lax.axis_index("c")


def _other_chips(x, y):
    return [(1 - x, y), (x, 1 - y), (1 - x, 1 - y)]


_ANY = pl.BlockSpec(memory_space=pl.ANY)


_HBM = pl.BlockSpec(memory_space=pltpu.HBM)
_SEM = pl.BlockSpec(memory_space=pltpu.SEMAPHORE)
_EFFECT = pltpu.SideEffectType.DATAFLOW_SIDE_EFFECTING


def _hbm(a):
    return pltpu.with_memory_space_constraint(a, pltpu.HBM)


def _half_rows(ref, c):
    half = ref.shape[1] // 2
    return pl.ds(pl.multiple_of(c * half, 16), half)


def _chip_copy(src, dst, send_sem, recv_sem, chip, c):
    return pltpu.make_async_remote_copy(src_ref=src, dst_ref=dst, send_sem=send_sem, recv_sem=recv_sem,
                                        device_id=(chip[0], chip[1], c), device_id_type=MESH)


def gather_start(bufs, groups, name):
    nb, ng = len(bufs), len(groups)

    def body(*refs):
        ins = refs[:nb]
        sems = refs[nb:nb + 2 * ng]
        token = refs[-1]
        x, y, c = _place()
        k = 2 * x + y
        for gi, grp in enumerate(groups):
            for wi, w in enumerate(grp):
                mine = ins[w].at[k, _half_rows(ins[w], c)]
                for j, chip in enumerate(_other_chips(x, y)):
                    _chip_copy(mine, mine, sems[2 * gi].at[3 * wi + j], sems[2 * gi + 1].at[3 * wi + j], chip, c).start()
        token[...] = jnp.zeros_like(token)

    sem_shapes = []
    for grp in groups:
        sem_shapes += [pltpu.SemaphoreType.DMA((3 * len(grp),)), pltpu.SemaphoreType.DMA((3 * len(grp),))]
    outs = pl.pallas_call(
        body, name=name,
        out_shape=sem_shapes + [pltpu.HBM(b.shape, b.dtype) for b in bufs] + [jax.ShapeDtypeStruct((8, 128), f32)],
        in_specs=[_HBM] * nb,
        out_specs=[_SEM] * (2 * ng) + [_HBM] * nb + [pl.BlockSpec(memory_space=pltpu.VMEM)],
        input_output_aliases={w: 2 * ng + w for w in range(nb)},
        compiler_params=pltpu.CompilerParams(has_side_effects=_EFFECT),
    )(*[_hbm(b) for b in bufs])
    sems = [(outs[2 * gi], outs[2 * gi + 1]) for gi in range(ng)]
    return sems, list(outs[2 * ng:2 * ng + nb]), outs[-1]


def gather_wait(bufs, sems, after, name):
    n = len(bufs)

    def body(*refs):
        ins = refs[:n]
        send_sems, recv_sems = refs[n], refs[n + 1]
        x, y, c = _place()
        k = 2 * x + y
        for wi in range(n):
            half = _half_rows(ins[wi], c)
            for j, chip in enumerate(_other_chips(x, y)):
                cp = _chip_copy(ins[wi].at[k, half], ins[wi].at[2 * chip[0] + chip[1], half], send_sems.at[3 * wi + j],
                                recv_sems.at[3 * wi + j], chip, c)
                cp.wait_send()
                cp.wait_recv()

    outs = pl.pallas_call(
        body, name=name,
        out_shape=[pltpu.HBM(b.shape, b.dtype) for b in bufs],
        in_specs=[_HBM] * n + [_SEM, _SEM, _ANY],
        out_specs=[_HBM] * n,
        input_output_aliases={i: i for i in range(n)},
        compiler_params=pltpu.CompilerParams(has_side_effects=_EFFECT),
    )(*bufs, sems[0], sems[1], after)
    return list(outs)


def gather_forward(bufs, name):
    n = len(bufs)

    def body(*refs):
        ins = refs[n:2 * n]
        send_sems, recv_sems = refs[2 * n], refs[2 * n + 1]
        x, y, c = _place()
        copies = []
        for wi in range(n):
            for j, chip in enumerate(_other_chips(x, y)):
                kp = 2 * chip[0] + chip[1]
                got = ins[wi].at[kp, _half_rows(ins[wi], c)]
                cp = pltpu.make_async_remote_copy(
                    src_ref=got, dst_ref=got, send_sem=send_sems.at[3 * wi + j], recv_sem=recv_sems.at[3 * wi + j],
                    device_id=(x, y, 1 - c), device_id_type=MESH)
                cp.start()
                copies.append((cp, wi, kp, j))
        for cp, wi, kp, j in copies:
            cp.wait_send()
            theirs = ins[wi].at[kp, _half_rows(ins[wi], 1 - c)]
            pltpu.make_async_remote_copy(
                src_ref=theirs, dst_ref=theirs, send_sem=send_sems.at[3 * wi + j], recv_sem=recv_sems.at[3 * wi + j],
                device_id=(x, y, 1 - c), device_id_type=MESH).wait_recv()

    outs = pl.pallas_call(
        body, name=name,
        out_shape=[jax.ShapeDtypeStruct(b.shape, b.dtype) for b in bufs],
        in_specs=[_ANY] * n, out_specs=[_ANY] * n,
        input_output_aliases={i: i for i in range(n)},
        scratch_shapes=[pltpu.SemaphoreType.DMA((3 * n,)), pltpu.SemaphoreType.DMA((3 * n,))],
    )(*bufs)
    return list(outs)


def forward_start(bufs, name):
    n = len(bufs)

    def body(*refs):
        x, y, c = _place()
        for wi in range(n):
            for j, chip in enumerate(_other_chips(x, y)):
                got = refs[wi].at[2 * chip[0] + chip[1], _half_rows(refs[wi], c)]
                _sibling_copy(got, got, refs[n].at[3 * wi + j], refs[n + 1].at[3 * wi + j]).start()
        refs[-1][...] = jnp.zeros_like(refs[-1])

    return _split_start(body, name, 3 * n, list(bufs))


def forward_wait(bufs, sems, after, name):
    n = len(bufs)

    def body(*refs):
        x, y, c = _place()
        for wi in range(n):
            for j, chip in enumerate(_other_chips(x, y)):
                kp = 2 * chip[0] + chip[1]
                got = refs[wi].at[kp, _half_rows(refs[wi], c)]
                theirs = refs[wi].at[kp, _half_rows(refs[wi], 1 - c)]
                _sibling_copy(got, got, refs[n].at[3 * wi + j], refs[n + 1].at[3 * wi + j]).wait_send()
                _sibling_copy(theirs, theirs, refs[n].at[3 * wi + j], refs[n + 1].at[3 * wi + j]).wait_recv()

    return _split_wait(body, name, list(bufs), sems, after)


def exchange_start(grads, name):
    n = len(grads)
    lands = [lax.empty((3,) + g.shape[1:], g.dtype) for g in grads]

    def body(*refs):
        ins = refs[:n]
        land = refs[n:2 * n]
        send_sems, recv_sems = refs[2 * n], refs[2 * n + 1]
        token = refs[-1]
        x, y, c = _place()
        for wi in range(n):
            for j, chip in enumerate(_other_chips(x, y)):
                _chip_copy(ins[wi].at[2 * chip[0] + chip[1]], land[wi].at[j], send_sems.at[3 * wi + j],
                           recv_sems.at[3 * wi + j], chip, c).start()
        token[...] = jnp.zeros_like(token)

    outs = pl.pallas_call(
        body, name=name,
        out_shape=[pltpu.SemaphoreType.DMA((3 * n,)), pltpu.SemaphoreType.DMA((3 * n,))]
        + [pltpu.HBM(g.shape, g.dtype) for g in grads] + [pltpu.HBM(l.shape, l.dtype) for l in lands]
        + [jax.ShapeDtypeStruct((8, 128), f32)],
        in_specs=[_HBM] * (2 * n),
        out_specs=[_SEM, _SEM] + [_HBM] * (2 * n) + [pl.BlockSpec(memory_space=pltpu.VMEM)],
        input_output_aliases={i: 2 + i for i in range(2 * n)},
        compiler_params=pltpu.CompilerParams(has_side_effects=_EFFECT),
    )(*[_hbm(g) for g in grads], *[_hbm(l) for l in lands])
    return (outs[0], outs[1]), list(outs[2:2 + n]), list(outs[2 + n:2 + 2 * n]), outs[-1]


def exchange_wait(grads, lands, sems, after, name):
    n = len(grads)

    def body(*refs):
        ins = refs[:n]
        land = refs[n:2 * n]
        send_sems, recv_sems = refs[2 * n], refs[2 * n + 1]
        x, y, c = _place()
        for wi in range(n):
            for j, chip in enumerate(_other_chips(x, y)):
                cp = _chip_copy(ins[wi].at[2 * chip[0] + chip[1]], land[wi].at[j], send_sems.at[3 * wi + j],
                                recv_sems.at[3 * wi + j], chip, c)
                cp.wait_send()
                cp.wait_recv()

    outs = pl.pallas_call(
        body, name=name,
        out_shape=[pltpu.HBM(g.shape, g.dtype) for g in grads] + [pltpu.HBM(l.shape, l.dtype) for l in lands],
        in_specs=[_HBM] * (2 * n) + [_SEM, _SEM, _ANY],
        out_specs=[_HBM] * (2 * n),
        input_output_aliases={i: i for i in range(2 * n)},
        compiler_params=pltpu.CompilerParams(has_side_effects=_EFFECT),
    )(*grads, *lands, sems[0], sems[1], after)
    return list(outs[:n]), list(outs[n:])


def _split_start(body, name, n_sems, operands):
    n = len(operands)
    outs = pl.pallas_call(
        body, name=name,
        out_shape=[pltpu.SemaphoreType.DMA((n_sems,)), pltpu.SemaphoreType.DMA((n_sems,))]
        + [pltpu.HBM(o.shape, o.dtype) for o in operands] + [jax.ShapeDtypeStruct((8, 128), f32)],
        in_specs=[_HBM] * n,
        out_specs=[_SEM, _SEM] + [_HBM] * n + [pl.BlockSpec(memory_space=pltpu.VMEM)],
        input_output_aliases={i: 2 + i for i in range(n)},
        compiler_params=pltpu.CompilerParams(has_side_effects=_EFFECT),
    )(*[_hbm(o) for o in operands])
    return (outs[0], outs[1]), list(outs[2:2 + n]), outs[-1]


def _split_wait(body, name, operands, sems, after):
    n = len(operands)
    outs = pl.pallas_call(
        body, name=name,
        out_shape=[pltpu.HBM(o.shape, o.dtype) for o in operands],
        in_specs=[_HBM] * n + [_SEM, _SEM, _ANY],
        out_specs=[_HBM] * n,
        input_output_aliases={i: i for i in range(n)},
        compiler_params=pltpu.CompilerParams(has_side_effects=_EFFECT),
    )(*operands, sems[0], sems[1], after)
    return list(outs)


def _sibling_copy(src, dst, send_sem, recv_sem):
    x, y, c = _place()
    return pltpu.make_async_remote_copy(src_ref=src, dst_ref=dst, send_sem=send_sem, recv_sem=recv_sem,
                                        device_id=(x, y, 1 - c), device_id_type=MESH)


def swap_start(parts, name):
    n = len(parts)

    def body(*refs):
        for w in range(n):
            _sibling_copy(refs[w], refs[n + w], refs[2 * n].at[w], refs[2 * n + 1].at[w]).start()
        refs[-1][...] = jnp.zeros_like(refs[-1])

    sems, ops, token = _split_start(body, name, n, list(parts) + [lax.empty(p.shape, p.dtype) for p in parts])
    return sems, ops[:n], ops[n:], token


def swap_wait(parts, lands, sems, after, name):
    n = len(parts)

    def body(*refs):
        for w in range(n):
            cp = _sibling_copy(refs[w], refs[n + w], refs[2 * n].at[w], refs[2 * n + 1].at[w])
            cp.wait_send()
            cp.wait_recv()

    outs = _split_wait(body, name, list(parts) + list(lands), sems, after)
    return outs[:n], outs[n:]


def _all_peers(x, y, c):
    return [(1 - x if m & 4 else x, 1 - y if m & 2 else y, 1 - c if m & 1 else c) for m in range(1, N_DEV)]


def small_start(block):
    land = jnp.broadcast_to(block[None], (N_DEV,) + block.shape)

    def body(b_ref, land_ref, send_sems, recv_sems, b_thru, land_thru, token):
        x, y, c = _place()
        me = 4 * x + 2 * y + c
        for m, peer in enumerate(_all_peers(x, y, c)):
            pltpu.make_async_remote_copy(src_ref=b_ref, dst_ref=land_ref.at[me], send_sem=send_sems.at[m],
                                         recv_sem=recv_sems.at[m], device_id=peer, device_id_type=MESH).start()
        token[...] = jnp.zeros_like(token)

    sems, ops, token = _split_start(body, "small_start", N_DEV - 1, [block, land])
    return sems, ops[0], ops[1], token


def small_wait(block, land, sems, after):
    def body(b_ref, land_ref, send_sems, recv_sems, after_ref, b_thru, land_thru):
        x, y, c = _place()
        for m, (px, py, pc) in enumerate(_all_peers(x, y, c)):
            cp = pltpu.make_async_remote_copy(src_ref=b_ref, dst_ref=land_ref.at[4 * px + 2 * py + pc],
                                              send_sem=send_sems.at[m], recv_sem=recv_sems.at[m],
                                              device_id=(px, py, pc), device_id_type=MESH)
            cp.wait_send()
            cp.wait_recv()

    return _split_wait(body, "small_wait", [block, land], sems, after)[1]


def _adamw(w, g, m, v):
    m = ADAM_B1 * m + (1.0 - ADAM_B1) * g
    v = ADAM_B2 * v + (1.0 - ADAM_B2) * (g * g)
    m_hat = m / (1.0 - ADAM_B1 ** ADAM_STEP)
    v_hat = v / (1.0 - ADAM_B2 ** ADAM_STEP)
    delta = -ADAM_LR * (m_hat / (jnp.sqrt(v_hat) + ADAM_EPS) + ADAM_WD * w)
    return delta, m, v


EW_BLOCK_BYTES = 2 * 1024 * 1024


def _ew_tile(rows, cols):
    for cand in (512, 352, 256, 176, 128, 64, 32, 16, 8):
        if rows % cand == 0 and cand * cols * 4 <= EW_BLOCK_BYTES:
            return cand
    return rows


def sum_partials(chip, own, land, name):
    _, r, c = own.shape
    tr = _ew_tile(r, c)

    def body(k_ref, own_ref, p_ref, o_ref):
        o_ref[...] = ((own_ref[0].astype(f32) + p_ref[0].astype(f32)) + p_ref[1].astype(f32)) + p_ref[2].astype(f32)

    return pl.pallas_call(
        body, name=name,
        grid_spec=pltpu.PrefetchScalarGridSpec(
            num_scalar_prefetch=1, grid=(r // tr,),
            in_specs=[pl.BlockSpec((1, tr, c), lambda i, k: (k[0], i, 0)), pl.BlockSpec((3, tr, c), lambda i, k: (0, i, 0))],
            out_specs=pl.BlockSpec((tr, c), lambda i, k: (i, 0))),
        out_shape=jax.ShapeDtypeStruct((r, c), f32),
        compiler_params=_cparams(),
    )(chip, own, land)


def adamw_shard(p_mine, p_sibling, w, m, v, name):
    r, c = w.shape
    tr = _ew_tile(r, c)

    def body(a_ref, b_ref, w_ref, m_ref, v_ref, g_ref, d_ref, mo_ref, vo_ref):
        g = a_ref[...] + b_ref[...]
        delta, mn, vn = _adamw(w_ref[...], g, m_ref[...], v_ref[...])
        g_ref[...] = g
        d_ref[...] = delta
        mo_ref[...] = mn
        vo_ref[...] = vn

    blk = pl.BlockSpec((tr, c), lambda i: (i, 0))
    return pl.pallas_call(
        body, name=name, grid=(r // tr,),
        in_specs=[blk] * 5, out_specs=[blk] * 4,
        out_shape=[jax.ShapeDtypeStruct((r, c), f32)] * 4,
        compiler_params=_cparams(),
    )(p_mine, p_sibling, w, m, v)


SC_TILES = 32
SC_CHUNK_MAX = 16384


def _sc_chunk(per_tile):
    for cand in range(min(per_tile, SC_CHUNK_MAX) // 16 * 16, 15, -16):
        if per_tile % cand == 0:
            return cand
    raise ValueError("a SparseCore tile's share must be a multiple of 16 elements")


def adamw_group_sc(planes, others, ws, ms, vs, name):
    k = len(ws)
    shape = ws[0].shape
    n = _size(shape)
    per_tile = n // SC_TILES
    chunk = _sc_chunk(per_tile)

    def body(*refs):
        ins, outs, bufs = refs[:5 * k], refs[5 * k:9 * k], refs[9 * k:]
        tile = lax.axis_index("sc_tile") * 2 + lax.axis_index("sc_core")
        for j in range(k):
            srcs = ins[5 * j:5 * j + 5]
            dsts = outs[4 * j:4 * j + 4]

            @pl.loop(0, per_tile // chunk)
            def _(ci, srcs=srcs, dsts=dsts):
                part = pl.ds(tile * per_tile + ci * chunk, chunk)
                for src, buf in zip(srcs, bufs):
                    pltpu.sync_copy(src.at[part], buf)
                a_buf, b_buf, w_buf, m_buf, v_buf = bufs

                @pl.loop(0, chunk, step=16)
                def _(i):
                    s = pl.ds(i, 16)
                    g = a_buf[s] + b_buf[s]
                    delta, mn, vn = _adamw(w_buf[s], g, m_buf[s], v_buf[s])
                    a_buf[s] = g
                    b_buf[s] = delta
                    m_buf[s] = mn
                    v_buf[s] = vn

                for buf, dst in zip((a_buf, b_buf, m_buf, v_buf), dsts):
                    pltpu.sync_copy(buf, dst.at[part])

    flat = []
    for j in range(k):
        flat += [a.reshape(n) for a in (planes[j], others[j], ws[j], ms[j], vs[j])]
    outs = pl.kernel(
        body, name=name,
        out_type=[jax.ShapeDtypeStruct((n,), f32)] * (4 * k),
        mesh=plsc.VectorSubcoreMesh(core_axis_name="sc_core", subcore_axis_name="sc_tile"),
        scratch_types=[pltpu.VMEM((chunk,), f32)] * 5,
    )(*flat)
    return [[o.reshape(shape) for o in outs[4 * j:4 * j + 4]] for j in range(k)]


def adamw_small(g8, w, m, v):
    _, r, lanes = g8.shape

    def body(g_ref, w_ref, m_ref, v_ref, go_ref, d_ref, mo_ref, vo_ref):
        g = g_ref[0]
        for i in range(1, N_DEV):
            g = g + g_ref[i]
        delta, mn, vn = _adamw(w_ref[...], g, m_ref[...], v_ref[...])
        go_ref[...] = g
        d_ref[...] = delta
        mo_ref[...] = mn
        vo_ref[...] = vn

    return pl.pallas_call(
        body, name="adamw_small",
        out_shape=[jax.ShapeDtypeStruct((r, lanes), f32)] * 4,
        compiler_params=_cparams(),
    )(g8, w, m, v)


def _size(shape):
    n = 1
    for e in shape:
        n *= e
    return n


def _pack_rows(shapes):
    rows = [-(-_size(s) // 1024) * 8 for s in shapes]
    return rows, sum(rows)


def _pack(arrs, shapes):
    rows, _ = _pack_rows(shapes)
    parts = [jnp.pad(a.reshape(-1).astype(f32), (0, r * 128 - _size(s))).reshape(r, 128)
             for a, s, r in zip(arrs, shapes, rows)]
    return jnp.concatenate(parts, axis=0)


def _unpack(block, shapes):
    rows, _ = _pack_rows(shapes)
    out, off = [], 0
    for s, r in zip(shapes, rows):
        out.append(block[off:off + r].reshape(-1)[:_size(s)].reshape(s))
        off += r
    return out


TRANSPOSED = ("ffn1_w_gate", "ffn1_w_up", "ffn2_w_gate", "ffn2_w_up")


def _shard2d(a, n):
    return a[0].T if n in TRANSPOSED else a[0]


def _unshard(a, n):
    return (a.T if n in TRANSPOSED else a)[None]


BIG = ("ffn1_w_gate", "ffn1_w_up", "ffn1_w_down", "w_in", "w_branch_a", "w_branch_b", "w_out",
       "ffn2_w_gate", "ffn2_w_up", "ffn2_w_down")
SMALL = ("ffn1_norm", "mix_norm", "b_in", "sgu_norm_g", "sgu_norm_b", "sgu_w_s", "sgu_b_s", "ret_decay_logit",
         "ffn2_norm", "final_norm")
WEIGHTS = ("ffn1_norm", "ffn1_w_gate", "ffn1_w_up", "ffn1_w_down", "mix_norm", "w_in", "b_in", "sgu_norm_g",
           "sgu_norm_b", "sgu_w_s", "sgu_b_s", "ret_decay_logit", "w_branch_a", "w_branch_b", "w_out", "ffn2_norm",
           "ffn2_w_gate", "ffn2_w_up", "ffn2_w_down", "final_norm")


def kernel(x, ffn1_norm, ffn1_w_gate, ffn1_w_up, ffn1_w_down, mix_norm, w_in, b_in, sgu_norm_g, sgu_norm_b, sgu_w_s, sgu_b_s, ret_decay_logit, w_branch_a, w_branch_b, w_out, ffn2_norm, ffn2_w_gate, ffn2_w_up, ffn2_w_down, final_norm, loss_target, m_ffn1_norm, m_ffn1_w_gate, m_ffn1_w_up, m_ffn1_w_down, m_mix_norm, m_w_in, m_b_in, m_sgu_norm_g, m_sgu_norm_b, m_sgu_w_s, m_sgu_b_s, m_ret_decay_logit, m_w_branch_a, m_w_branch_b, m_w_out, m_ffn2_norm, m_ffn2_w_gate, m_ffn2_w_up, m_ffn2_w_down, m_final_norm, v_ffn1_norm, v_ffn1_w_gate, v_ffn1_w_up, v_ffn1_w_down, v_mix_norm, v_w_in, v_b_in, v_sgu_norm_g, v_sgu_norm_b, v_sgu_w_s, v_sgu_b_s, v_ret_decay_logit, v_w_branch_a, v_w_branch_b, v_w_out, v_ffn2_norm, v_ffn2_w_gate, v_ffn2_w_up, v_ffn2_w_down, v_final_norm):
    p = dict(ffn1_norm=ffn1_norm, ffn1_w_gate=ffn1_w_gate, ffn1_w_up=ffn1_w_up, ffn1_w_down=ffn1_w_down,
             mix_norm=mix_norm, w_in=w_in, b_in=b_in, sgu_norm_g=sgu_norm_g, sgu_norm_b=sgu_norm_b, sgu_w_s=sgu_w_s,
             sgu_b_s=sgu_b_s, ret_decay_logit=ret_decay_logit, w_branch_a=w_branch_a, w_branch_b=w_branch_b,
             w_out=w_out, ffn2_norm=ffn2_norm, ffn2_w_gate=ffn2_w_gate, ffn2_w_up=ffn2_w_up, ffn2_w_down=ffn2_w_down,
             final_norm=final_norm)
    mom = dict(ffn1_norm=m_ffn1_norm, ffn1_w_gate=m_ffn1_w_gate, ffn1_w_up=m_ffn1_w_up, ffn1_w_down=m_ffn1_w_down,
               mix_norm=m_mix_norm, w_in=m_w_in, b_in=m_b_in, sgu_norm_g=m_sgu_norm_g, sgu_norm_b=m_sgu_norm_b,
               sgu_w_s=m_sgu_w_s, sgu_b_s=m_sgu_b_s, ret_decay_logit=m_ret_decay_logit, w_branch_a=m_w_branch_a,
               w_branch_b=m_w_branch_b, w_out=m_w_out, ffn2_norm=m_ffn2_norm, ffn2_w_gate=m_ffn2_w_gate,
               ffn2_w_up=m_ffn2_w_up, ffn2_w_down=m_ffn2_w_down, final_norm=m_final_norm)
    var = dict(ffn1_norm=v_ffn1_norm, ffn1_w_gate=v_ffn1_w_gate, ffn1_w_up=v_ffn1_w_up, ffn1_w_down=v_ffn1_w_down,
               mix_norm=v_mix_norm, w_in=v_w_in, b_in=v_b_in, sgu_norm_g=v_sgu_norm_g, sgu_norm_b=v_sgu_norm_b,
               sgu_w_s=v_sgu_w_s, sgu_b_s=v_sgu_b_s, ret_decay_logit=v_ret_decay_logit, w_branch_a=v_w_branch_a,
               w_branch_b=v_w_branch_b, w_out=v_w_out, ffn2_norm=v_ffn2_norm, ffn2_w_gate=v_ffn2_w_gate,
               ffn2_w_up=v_ffn2_w_up, ffn2_w_down=v_ffn2_w_down, final_norm=v_final_norm)

    xs = x[0]
    tgt = loss_target[0]
    t, d = xs.shape
    dk = d // RET_HEADS
    tm = _row_tile(t)

    shards2d = {n: _shard2d(p[n], n) for n in BIG}
    chip = (2 * lax.axis_index("x") + lax.axis_index("y")).astype(jnp.int32).reshape(1)
    groups = {"ffn1": ("ffn1_w_gate", "ffn1_w_up", "ffn1_w_down"), "in": ("w_in",),
              "mix": ("w_branch_a", "w_branch_b", "w_out"), "ffn2": ("ffn2_w_gate", "ffn2_w_up", "ffn2_w_down")}
    def own_slot(n, zero):
        sh = shards2d[n].astype(bf16) + zero
        return lax.dynamic_update_index_in_dim(lax.empty((N_CHIPS,) + sh.shape, bf16), sh, chip[0], 0)

    sems, bufs, tok = gather_start([own_slot(n, jnp.zeros((), bf16)) for n in groups["ffn1"]], [[0, 1, 2]],
                                   "gather_start_ffn1")
    gsem = {"ffn1": sems[0]}
    pending = dict(zip(groups["ffn1"], bufs))
    rest = [n for g in ("in", "mix", "ffn2") for n in groups[g]]
    sems, bufs, tok_rest = gather_start([own_slot(n, tok[0, 0].astype(bf16)) for n in rest],
                                 [[rest.index(n) for n in groups[g]] for g in ("in", "mix", "ffn2")], "gather_start_rest")
    gsem.update(zip(("in", "mix", "ffn2"), sems))
    pending.update(zip(rest, bufs))

    def arrive(gs, after):
        got = []
        for g in gs:
            got += gather_wait([pending[n] for n in groups[g]], gsem[g], after, "gather_wait_" + g)
        return gather_forward(got, "gather_forward_" + gs[0])

    bin4 = b_in.reshape(N_CHIPS, 1, 2 * d)
    ws_b = sgu_w_s[0].astype(bf16)
    bs_c = sgu_b_s[0][:, :, None]
    cols, mats, cdec, cos, sin = retention_constants(ret_decay_logit[0], t, dk, tok_rest[0, 0])

    wg1, wu1, wd1 = [_pair_shards(w) for w in arrive(["ffn1"], cos)]
    x1, g1, u1 = ffn_fwd(xs, ffn1_norm, wg1, wu1, wd1, "ffn1_fwd")
    win, = arrive(["in"], x1)
    proj, hb2 = inproj_fwd(x1, mix_norm, win, bin4, cos, sin)
    late = []
    for g in ("mix", "ffn2"):
        late += gather_wait([pending[n] for n in groups[g]], gsem[g], proj, "gather_wait_" + g)
    fsems, late, ftok = forward_start(late, "forward_start_mix")
    a = sgu_fwd(proj, sgu_norm_g, sgu_norm_b, ws_b, bs_c, ftok)
    r, rn = ret_fwd(proj, cols, mats, cdec)
    wa, wb, wo, wg2, wu2, wd2 = forward_wait(late, fsems, rn, "forward_wait_mix")
    wa, wb, wo = [w.reshape(d, d) for w in (wa, wb, wo)]
    wg2, wu2, wd2 = [_pair_shards(w) for w in (wg2, wu2, wd2)]
    x2, ba, br = mix_fwd(a, rn, proj, wa, wb, wo, x1)
    loss_blk, dx3, d_final, g2, u2 = ffn_fwd_loss(x2, ffn2_norm, wg2, wu2, wd2, final_norm.reshape(1, d), tgt, "ffn2_fwd")

    sent, swaps = {}, {}
    out_g, out_d, out_m, out_v = {}, {}, {}, {}

    def reduce_plane(g, after):
        gsems, own, lands, _ = sent[g]
        own, lands = exchange_wait(own, lands, gsems, after, "exchange_wait_" + g)
        plane = [sum_partials(chip, o, l, "sum_" + n) for n, o, l in zip(groups[g], own, lands)]
        swaps[g] = swap_start(plane, "swap_start_" + g)
        return swaps[g][3]

    def update(g, after, on_sparsecore=False):
        ssems, plane, lands, _ = swaps[g]
        plane, other = swap_wait(plane, lands, ssems, after, "swap_wait_" + g)
        names = groups[g]
        state = [[shards2d[n] for n in names], [_shard2d(mom[n], n) for n in names], [_shard2d(var[n], n) for n in names]]
        if on_sparsecore:
            results = adamw_group_sc(plane, other, *state, "adamw_sc_" + g)
        else:
            results = [adamw_shard(mine, sib, w, m, v, "adamw_" + n)
                       for n, mine, sib, w, m, v in zip(names, plane, other, *state)]
        for n, res in zip(names, results):
            out_g[n], out_d[n], out_m[n], out_v[n] = [_unshard(o, n) for o in res]
        return plane[-1]

    dx2, dg2, du2, act2, hb3, dyb2, d_ffn2n = ffn_bwd_act(dx3, x2, ffn2_norm, g2, u2, wg2, wu2, wd2, "ffn2_bwd_act", tok)
    sent["ffn2"] = exchange_start(ffn_weight_grads(hb3, dyb2, dg2, du2, act2, "ffn2_grad", tok), "exchange_start_ffn2")
    da, drn, dga, dgb, mixb, dba, dbr, dx2b = mix_bwd_act(dx2, ba, br, proj, wa, wb, wo, sent["ffn2"][3])
    tg = min(t, 2048)
    row = pl.BlockSpec((tg, d), lambda s, i: (i, 0))

    def square_grad(xa, ya, name):
        return tn_matmul(xa, [ya], row, [row], 1, d, [d], t, tg, name, tok).reshape(N_CHIPS, d // N_CHIPS, d)

    sent["mix"] = exchange_start([square_grad(a, dba, "grad_w_branch_a"), square_grad(rn, dbr, "grad_w_branch_b"),
                                  square_grad(mixb, dx2b, "grad_w_out")], "exchange_start_mix")
    dua, dva, d_ws, d_bs, d_sng, d_snb = sgu_bwd(da, proj, sgu_norm_g, sgu_norm_b, ws_b, bs_c, sent["mix"][3])
    dq, dkr, dv, dgr, dlg = ret_bwd(drn, r, proj, cols, mats, cdec, cos, sin)
    segs = [dua, dva, dq, dkr, dv, dgr, dga, dgb]
    after = reduce_plane("ffn2", dq)
    after = reduce_plane("mix", after)
    dx1, d_bin, d_mixn = inproj_bwd_act(segs, win, x1, mix_norm, dx2)
    after = update("ffn2", dx1, on_sparsecore=True)
    update("mix", after, on_sparsecore=True)
    g_in = None
    for s in range(N_CHIPS):
        g_in = tn_matmul(hb2, [segs[2 * s], segs[2 * s + 1]], row, [row, row], 1, d, [d, d], t, tg, "grad_w_in_%d" % s,
                         tok, (g_in, s, N_CHIPS))
    sent["in"] = exchange_start([g_in], "exchange_start_in")
    grad_x, dg1, du1, act1, hb1, dyb1, d_ffn1n = ffn_bwd_act(dx1, xs, ffn1_norm, g1, u1, wg1, wu1, wd1, "ffn1_bwd_act",
                                                              sent["in"][3])
    dlogit = dlg[:, 0:2, 0].T * jax.nn.sigmoid(-ret_decay_logit[0].astype(f32))
    small_g = dict(ffn1_norm=d_ffn1n, mix_norm=d_mixn, b_in=d_bin, sgu_norm_g=d_sng, sgu_norm_b=d_snb, sgu_w_s=d_ws,
                   sgu_b_s=d_bs, ret_decay_logit=dlogit, ffn2_norm=d_ffn2n, final_norm=d_final)
    shapes = [p[n].shape for n in SMALL]
    small_sems, small_blk, small_land, small_tok = small_start(_pack([small_g[n] for n in SMALL], shapes))

    def send_one(which, grad):
        n = "ffn1_" + which
        groups[n] = (n,)
        sent[n] = exchange_start([grad], "exchange_start_" + n)
        return sent[n][3]

    ffn_weight_grads(hb1, dyb1, dg1, du1, act1, "ffn1_grad", small_tok, send_one)

    after = reduce_plane("in", sent["ffn1_w_down"][3])
    g8 = small_wait(small_blk, small_land, small_sems, after)
    sg, sd, sm, sv = adamw_small(g8, _pack([p[n] for n in SMALL], shapes), _pack([mom[n] for n in SMALL], shapes),
                                 _pack([var[n] for n in SMALL], shapes))
    for res, blockv in ((out_g, sg), (out_d, sd), (out_m, sm), (out_v, sv)):
        for n, val in zip(SMALL, _unpack(blockv, shapes)):
            res[n] = val
    after = update("in", sg)
    after = reduce_plane("ffn1_w_gate", after)
    after = reduce_plane("ffn1_w_up", after)
    after = update("ffn1_w_gate", after)
    after = reduce_plane("ffn1_w_down", after)
    after = update("ffn1_w_up", after)
    update("ffn1_w_down", after)

    loss = lax.psum(loss_blk[0, 0], ("x", "y", "c"))
    return (loss, grad_x[None], *[out_g[n] for n in WEIGHTS], *[out_d[n] for n in WEIGHTS],
            *[out_m[n] for n in WEIGHTS], *[out_v[n] for n in WEIGHTS])
```

```python
import functools

import jax
import jax.numpy as jnp
from jax import lax
from jax.experimental import pallas as pl
from jax.experimental.pallas import tpu as pltpu

f32 = jnp.float32
bf16 = jnp.bfloat16

SGU_CHUNK = 128
CHUNK = 256
RET_HEADS = 4
SGU_GROUPS = 4
ROPE_BASE = 10000.0
NORM_EPS = 1e-6
ADAM_LR = 0.001
ADAM_B1 = 0.9
ADAM_B2 = 0.999
ADAM_EPS = 1e-08
ADAM_WD = 0.01
ADAM_STEP = 10
N_CHIPS = 4
N_DEV = 8
MESH = pl.DeviceIdType.MESH
VMEM_LIMIT = 52 * 1024 * 1024
VMEM_LIMIT_WIDE = 62 * 1024 * 1024

_NT = (((1,), (1,)), ((), ()))
_TN = (((0,), (0,)), ((), ()))


def _cparams(limit=None):
    return pltpu.CompilerParams(vmem_limit_bytes=VMEM_LIMIT if limit is None else limit)


def _row_tile(t):
    return 512 if t >= 2048 else t // 2


def _dot(a, b):
    return jnp.dot(a, b, preferred_element_type=f32)


def _dot_nt(a, b):
    return lax.dot_general(a, b, _NT, preferred_element_type=f32)


def _dot_tn(a, b):
    return lax.dot_general(a, b, _TN, preferred_element_type=f32)


def _rms(x, g):
    r = lax.rsqrt(jnp.mean(x * x, axis=-1, keepdims=True) + NORM_EPS)
    xh = x * r
    return xh * g, xh, r


def _rms_bwd(dy, xh, r, g):
    dxh = dy * g
    return r * (dxh - xh * jnp.mean(dxh * xh, axis=-1, keepdims=True))


def _sigmoid(x):
    return jax.nn.sigmoid(x)


def _dsilu(g, sg):
    return sg * (1.0 + g * (1.0 - sg))


def _gelu(x):
    return 0.5 * x * (1.0 + lax.erf(x * 0.7071067811865476))


def _dgelu(x):
    return 0.5 * (1.0 + lax.erf(x * 0.7071067811865476)) + x * jnp.exp(-0.5 * x * x) * 0.3989422804014327


def _acc_out(ref, first, val):
    @pl.when(first)
    def _():
        ref[...] = val

    @pl.when(jnp.logical_not(first))
    def _():
        ref[...] += val


def _ffn_tile(t):
    return 256 if t >= 2048 else t // 2


def _ffn_fwd_rows(xx, ng_ref, wg_ref, wu_ref, wd_ref, g_ref, u_ref):
    y, _, _ = _rms(xx, ng_ref[...])
    h = y.astype(bf16)
    acc = None
    for s in range(wg_ref.shape[0]):
        g = _dot_nt(h, wg_ref[s])
        u = _dot_nt(h, wu_ref[s])
        g_ref[s] = g.astype(bf16)
        u_ref[s] = u.astype(bf16)
        part = _dot((g * _sigmoid(g) * u).astype(bf16), wd_ref[s])
        acc = part if acc is None else acc + part
    return xx + 0.5 * acc


def ffn_fwd(x, ng, wg, wu, wd, name):
    t, d = x.shape
    ns, fs, _ = wg.shape
    tm = _ffn_tile(t)

    def body(x_ref, ng_ref, wg_ref, wu_ref, wd_ref, xo_ref, g_ref, u_ref):
        xo_ref[...] = _ffn_fwd_rows(x_ref[...], ng_ref, wg_ref, wu_ref, wd_ref, g_ref, u_ref)

    row = pl.BlockSpec((tm, d), lambda i: (i, 0))
    shard = pl.BlockSpec((ns, tm, fs), lambda i: (0, i, 0))
    wspec = pl.BlockSpec((ns, fs, d), lambda i: (0, 0, 0), pipeline_mode=pl.Buffered(1))
    return pl.pallas_call(
        body, name=name, grid=(t // tm,),
        in_specs=[row, pl.BlockSpec((1, d), lambda i: (0, 0)), wspec, wspec, wspec],
        out_specs=[row, shard, shard],
        out_shape=[jax.ShapeDtypeStruct((t, d), f32), jax.ShapeDtypeStruct((ns, t, fs), bf16),
                   jax.ShapeDtypeStruct((ns, t, fs), bf16)],
        compiler_params=_cparams(),
    )(x, ng, wg, wu, wd)


def ffn_fwd_loss(x, ng, wg, wu, wd, fng, tgt, name):
    t, d = x.shape
    ns, fs, _ = wg.shape
    tm = _ffn_tile(t)

    def body(x_ref, ng_ref, wg_ref, wu_ref, wd_ref, fng_ref, t_ref, loss_ref, dx_ref, dfn_ref, g_ref, u_ref):
        i = pl.program_id(0)
        x3 = _ffn_fwd_rows(x_ref[...], ng_ref, wg_ref, wu_ref, wd_ref, g_ref, u_ref)
        y, xh, r = _rms(x3, fng_ref[...])
        diff = y - t_ref[...]
        part = 0.5 * jnp.sum(jnp.sum(diff * diff, axis=0, keepdims=True), axis=1, keepdims=True) / d
        _acc_out(loss_ref, i == 0, jnp.broadcast_to(part, (1, 128)))
        dy = diff * (1.0 / d)
        dx_ref[...] = _rms_bwd(dy, xh, r, fng_ref[...])
        _acc_out(dfn_ref, i == 0, jnp.sum(dy * xh, axis=0, keepdims=True))

    row = pl.BlockSpec((tm, d), lambda i: (i, 0))
    vec = pl.BlockSpec((1, d), lambda i: (0, 0))
    shard = pl.BlockSpec((ns, tm, fs), lambda i: (0, i, 0))
    wspec = pl.BlockSpec((ns, fs, d), lambda i: (0, 0, 0), pipeline_mode=pl.Buffered(1))
    return pl.pallas_call(
        body, name=name, grid=(t // tm,),
        in_specs=[row, vec, wspec, wspec, wspec, vec, row],
        out_specs=[pl.BlockSpec((1, 128), lambda i: (0, 0)), row, vec, shard, shard],
        out_shape=[jax.ShapeDtypeStruct((1, 128), f32), jax.ShapeDtypeStruct((t, d), f32), jax.ShapeDtypeStruct((1, d), f32),
                   jax.ShapeDtypeStruct((ns, t, fs), bf16), jax.ShapeDtypeStruct((ns, t, fs), bf16)],
        compiler_params=_cparams(),
    )(x, ng, wg, wu, wd, fng, tgt)


def ffn_bwd_act(dxo, x, ng, g, u, wg, wu, wd, name, dep):
    t, d = x.shape
    ns, fs, _ = wg.shape
    tm = _ffn_tile(t)

    def body(dxo_ref, x_ref, ng_ref, g_ref, u_ref, wg_ref, wu_ref, wd_ref, dep_ref,
             dx_ref, dg_ref, du_ref, act_ref, hb_ref, dyb_ref, dng_ref):
        i = pl.program_id(0)
        dxo = dxo_ref[...]
        dyb = (0.5 * dxo).astype(bf16)
        dyb_ref[...] = dyb
        dh = None
        for s in range(ns):
            dact = _dot_nt(dyb, wd_ref[s])
            gg = g_ref[s].astype(f32)
            uu = u_ref[s].astype(f32)
            sg = _sigmoid(gg)
            sil = gg * sg
            dgb = (dact * uu * _dsilu(gg, sg)).astype(bf16)
            dub = (dact * sil).astype(bf16)
            dg_ref[s] = dgb
            du_ref[s] = dub
            act_ref[s] = (sil * uu).astype(bf16)
            part = _dot(dgb, wg_ref[s]) + _dot(dub, wu_ref[s])
            dh = part if dh is None else dh + part
        y, xh, r = _rms(x_ref[...], ng_ref[...])
        hb_ref[...] = y.astype(bf16)
        dx_ref[...] = dxo + _rms_bwd(dh, xh, r, ng_ref[...])
        _acc_out(dng_ref, i == 0, jnp.sum(dh * xh, axis=0, keepdims=True))

    row = pl.BlockSpec((tm, d), lambda i: (i, 0))
    shard = pl.BlockSpec((ns, tm, fs), lambda i: (0, i, 0))
    wspec = pl.BlockSpec((ns, fs, d), lambda i: (0, 0, 0), pipeline_mode=pl.Buffered(1))
    vec = pl.BlockSpec((1, d), lambda i: (0, 0))
    return pl.pallas_call(
        body, name=name, grid=(t // tm,),
        in_specs=[row, row, vec, shard, shard, wspec, wspec, wspec, _ANY],
        out_specs=[row, shard, shard, shard, row, row, vec],
        out_shape=[jax.ShapeDtypeStruct((t, d), f32)] + [jax.ShapeDtypeStruct((ns, t, fs), bf16)] * 3
        + [jax.ShapeDtypeStruct((t, d), bf16)] * 2 + [jax.ShapeDtypeStruct((1, d), f32)],
        compiler_params=_cparams(VMEM_LIMIT_WIDE),
    )(dxo, x, ng, g, u, wg, wu, wd, dep)


def tn_matmul(xs, ys, x_spec, y_specs, n_shards, k1, k2s, t, tm, name, dep, into=None):
    k2 = sum(k2s)
    ny = len(ys)

    def body(*refs):
        x_ref = refs[0]
        y_refs = refs[1:1 + ny]
        o_ref, acc = refs[-2], refs[-1]
        i = pl.program_id(1)
        xb = x_ref[0] if len(x_ref.shape) == 3 else x_ref[...]
        off = 0
        for y_ref, w in zip(y_refs, k2s):
            yb = y_ref[0] if len(y_ref.shape) == 3 else y_ref[...]
            part = _dot_tn(xb, yb)
            sl = (slice(None), slice(off, off + w))

            @pl.when(i == 0)
            def _(part=part, sl=sl):
                acc[sl] = part

            @pl.when(i > 0)
            def _(part=part, sl=sl):
                acc[sl] += part

            off += w

        @pl.when(i == t // tm - 1)
        def _():
            o_ref[0] = acc[...].astype(bf16)

    if into is None:
        slot0, total, extra, aliases = 0, n_shards, [], {}
    else:
        buf, slot0, total = into
        extra = [] if buf is None else [buf]
        aliases = {} if buf is None else {2 + ny: 0}
    return pl.pallas_call(
        body, name=name, grid=(n_shards, t // tm),
        in_specs=[x_spec] + list(y_specs) + [_ANY] * (1 + len(extra)),
        out_specs=pl.BlockSpec((1, k1, k2), lambda s, i: (slot0 + s, 0, 0)),
        out_shape=jax.ShapeDtypeStruct((total, k1, k2), bf16),
        scratch_shapes=[pltpu.VMEM((k1, k2), f32)],
        input_output_aliases=aliases,
        compiler_params=_cparams(),
    )(xs, *ys, dep, *extra)


def _pair_shards(w):
    s4, fs, d = w.shape
    return w.reshape(s4 // 2, 2 * fs, d)


def ffn_weight_grads(hb, dyb, dg, du, act, name, dep, each=None):
    t, d = hb.shape
    s2, _, fs2 = dg.shape
    tm = t
    row = pl.BlockSpec((tm, d), lambda s, i: (i, 0))
    shard = pl.BlockSpec((1, tm, fs2), lambda s, i: (s, i, 0))
    grads = []
    for xa, ya, which in ((dg, hb, "w_gate"), (du, hb, "w_up"), (act, dyb, "w_down")):
        g = tn_matmul(xa, [ya], shard, [row], s2, fs2, [d], t, tm, name + "_" + which, dep)
        g = g.reshape(2 * s2, fs2 // 2, d)
        if each is not None:
            dep = each(which, g)
        grads.append(g)
    return grads


def inproj_fwd(x1, ng, win, bin4, cos, sin):
    t, d = x1.shape
    s4, _, w2 = win.shape
    tm = _row_tile(t)
    dk = d // RET_HEADS
    scale = dk ** -0.5

    def body(x_ref, ng_ref, w_ref, b_ref, cos_ref, sin_ref, p_ref, hb_ref):
        y, _, _ = _rms(x_ref[...], ng_ref[...])
        h = y.astype(bf16)
        hb_ref[...] = h
        for s in range(s4):
            p = _dot(h, w_ref[s]) + b_ref[s]
            if s != 1:
                p_ref[s] = p.astype(bf16)
            else:
                cs, sn = cos_ref[...], sin_ref[...]
                for e in range(2 * RET_HEADS):
                    cols = slice(e * dk, (e + 1) * dk)
                    rot = _rot(p[:, cols], cs, sn)
                    p_ref[s, :, cols] = (rot if e < RET_HEADS else rot * scale).astype(bf16)

    tab = pl.BlockSpec((tm, dk // 2), lambda i: (i, 0))
    return pl.pallas_call(
        body, name="inproj_fwd", grid=(t // tm,),
        in_specs=[pl.BlockSpec((tm, d), lambda i: (i, 0)), pl.BlockSpec((1, d), lambda i: (0, 0)),
                  pl.BlockSpec((s4, d, w2), lambda i: (0, 0, 0), pipeline_mode=pl.Buffered(1)),
                  pl.BlockSpec((s4, 1, w2), lambda i: (0, 0, 0)), tab, tab],
        out_specs=[pl.BlockSpec((s4, tm, w2), lambda i: (0, i, 0)), pl.BlockSpec((tm, d), lambda i: (i, 0))],
        out_shape=[jax.ShapeDtypeStruct((s4, t, w2), bf16), jax.ShapeDtypeStruct((t, d), bf16)],
        compiler_params=_cparams(),
    )(x1, ng, win, bin4, cos, sin)


def _sgu_norm(va, ng, nb):
    gv = _gelu(va)
    mu = jnp.mean(gv, axis=-1, keepdims=True)
    xc = gv - mu
    rstd = lax.rsqrt(jnp.mean(xc * xc, axis=-1, keepdims=True) + NORM_EPS)
    xh = xc * rstd
    return xh, rstd, (xh * ng + nb).astype(bf16)


def sgu_fwd(proj, ng, nb, ws, bs, dep):
    _, t, w2 = proj.shape
    d = w2 // 2
    gd = d // SGU_GROUPS
    tm = _row_tile(t)

    def body(p_ref, ng_ref, nb_ref, ws_ref, bs_ref, dep_ref, a_ref):
        ua = p_ref[0, :, 0:d].astype(f32)
        va = p_ref[0, :, d:w2].astype(f32)
        gu = _gelu(ua)
        _, _, vn = _sgu_norm(va, ng_ref[...], nb_ref[...])
        for c in range(tm // SGU_CHUNK):
            rows = slice(c * SGU_CHUNK, (c + 1) * SGU_CHUNK)
            for g in range(SGU_GROUPS):
                cols = slice(g * gd, (g + 1) * gd)
                sg = _dot(ws_ref[g], vn[rows, cols]) + bs_ref[g]
                a_ref[rows, cols] = (gu[rows, cols] * sg).astype(bf16)

    return pl.pallas_call(
        body, name="sgu_fwd", grid=(t // tm,),
        in_specs=[pl.BlockSpec((1, tm, w2), lambda i: (0, i, 0)), pl.BlockSpec((1, d), lambda i: (0, 0)),
                  pl.BlockSpec((1, d), lambda i: (0, 0)), pl.BlockSpec((SGU_GROUPS, SGU_CHUNK, SGU_CHUNK), lambda i: (0, 0, 0)),
                  pl.BlockSpec((SGU_GROUPS, SGU_CHUNK, 1), lambda i: (0, 0, 0)), _ANY],
        out_specs=pl.BlockSpec((tm, d), lambda i: (i, 0)),
        out_shape=jax.ShapeDtypeStruct((t, d), bf16),
        compiler_params=_cparams(),
    )(proj, ng, nb, ws, bs, dep)


def sgu_bwd(da, proj, ng, nb, ws, bs, dep):
    _, t, w2 = proj.shape
    d = w2 // 2
    gd = d // SGU_GROUPS
    tm = _row_tile(t)

    def body(da_ref, p_ref, ng_ref, nb_ref, ws_ref, bs_ref, dep_ref,
             dua_ref, dva_ref, dws_ref, dbs_ref, dng_ref, dnb_ref, dvn_scr):
        i = pl.program_id(0)
        ua = p_ref[0, :, 0:d].astype(f32)
        va = p_ref[0, :, d:w2].astype(f32)
        gu = _gelu(ua)
        xh, rstd, vn = _sgu_norm(va, ng_ref[...], nb_ref[...])
        dad = da_ref[...].astype(f32)
        dsb = (dad * gu).astype(bf16)
        for c in range(tm // SGU_CHUNK):
            rows = slice(c * SGU_CHUNK, (c + 1) * SGU_CHUNK)
            for g in range(SGU_GROUPS):
                cols = slice(g * gd, (g + 1) * gd)
                sg = _dot(ws_ref[g], vn[rows, cols]) + bs_ref[g]
                dua_ref[rows, cols] = (dad[rows, cols] * sg * _dgelu(ua[rows, cols])).astype(bf16)
                ds = dsb[rows, cols]
                dvn_scr[rows, cols] = _dot_tn(ws_ref[g], ds)
                dw = _dot_nt(ds, vn[rows, cols])
                db = jnp.sum(ds.astype(f32), axis=1, keepdims=True)
                if c == 0:
                    _acc_out(dws_ref.at[g], i == 0, dw)
                    _acc_out(dbs_ref.at[g], i == 0, db)
                else:
                    dws_ref[g] += dw
                    dbs_ref[g] += db
        dvn = dvn_scr[...]
        _acc_out(dng_ref, i == 0, jnp.sum(dvn * xh, axis=0, keepdims=True))
        _acc_out(dnb_ref, i == 0, jnp.sum(dvn, axis=0, keepdims=True))
        dxh = dvn * ng_ref[...]
        dgv = rstd * (dxh - jnp.mean(dxh, axis=-1, keepdims=True) - xh * jnp.mean(dxh * xh, axis=-1, keepdims=True))
        dva_ref[...] = (dgv * _dgelu(va)).astype(bf16)

    row = pl.BlockSpec((tm, d), lambda i: (i, 0))
    vec = pl.BlockSpec((1, d), lambda i: (0, 0))
    wsp = pl.BlockSpec((SGU_GROUPS, SGU_CHUNK, SGU_CHUNK), lambda i: (0, 0, 0))
    bsp = pl.BlockSpec((SGU_GROUPS, SGU_CHUNK, 1), lambda i: (0, 0, 0))
    return pl.pallas_call(
        body, name="sgu_bwd", grid=(t // tm,),
        in_specs=[row, pl.BlockSpec((1, tm, w2), lambda i: (0, i, 0)), vec, vec, wsp, bsp, _ANY],
        out_specs=[row, row, wsp, bsp, vec, vec],
        out_shape=[jax.ShapeDtypeStruct((t, d), bf16), jax.ShapeDtypeStruct((t, d), bf16),
                   jax.ShapeDtypeStruct((SGU_GROUPS, SGU_CHUNK, SGU_CHUNK), f32), jax.ShapeDtypeStruct((SGU_GROUPS, SGU_CHUNK, 1), f32),
                   jax.ShapeDtypeStruct((1, d), f32), jax.ShapeDtypeStruct((1, d), f32)],
        scratch_shapes=[pltpu.VMEM((tm, d), f32)],
        compiler_params=_cparams(),
    )(da, proj, ng, nb, ws, bs, dep)


def retention_constants(decay_logit, t, dk, zero):
    lg = jax.nn.log_sigmoid(decay_logit.astype(f32) + zero)
    lgf = lg[0][:, None]
    lgb = lg[1][:, None]
    idx = jnp.arange(CHUNK, dtype=f32)[None, :]
    af = jnp.exp((idx + 1.0) * lgf)
    ab = jnp.exp((CHUNK - idx) * lgb)
    kf = jnp.exp((CHUNK - 1.0 - idx) * lgf)
    kb = jnp.exp(idx * lgb)
    cols = jnp.stack([af, ab, kf, kb, af * (idx + 1.0), ab * (CHUNK - idx), kf * (CHUNK - 1.0 - idx), kb * idx], axis=1)
    cols = cols[..., None]
    diff = idx[0][:, None] - idx[0][None, :]
    dfm = jnp.where(diff >= 0, jnp.exp(jnp.maximum(diff, 0.0)[None] * lgf[:, :, None]), 0.0)
    dbm = jnp.where(diff < 0, jnp.exp(jnp.maximum(-diff, 0.0)[None] * lgb[:, :, None]), 0.0)
    mats = jnp.stack([dfm + dbm, dfm * diff[None], dbm * (-diff)[None]], axis=1)
    cdec = jnp.stack([jnp.broadcast_to(jnp.exp(CHUNK * lgf), (RET_HEADS, dk)),
                      jnp.broadcast_to(jnp.exp(CHUNK * lgb), (RET_HEADS, dk))], axis=1)
    theta = ROPE_BASE ** (-jnp.arange(0, dk, 2, dtype=f32) / dk)
    ang = (jnp.arange(t, dtype=f32) + zero)[:, None] * theta[None, :]
    return cols, mats, cdec, jnp.cos(ang), jnp.sin(ang)


def _rot(tr, cos, sin):
    half = tr.shape[-1] // 2
    t1 = tr[:, :half]
    t2 = tr[:, half:]
    return jnp.concatenate([t1 * cos - t2 * sin, t2 * cos + t1 * sin], axis=-1)


def _rot_inv(dt, cos, sin):
    half = dt.shape[-1] // 2
    d1 = dt[:, :half]
    d2 = dt[:, half:]
    return jnp.concatenate([d1 * cos + d2 * sin, d2 * cos - d1 * sin], axis=-1)


def _ret_specs(t, d, dk, rt):
    nr = t // rt
    hq = d // dk

    def blk(p, n):
        return (1 - p) * (nr - 1 - n) + p * n

    q_spec = pl.BlockSpec((1, rt, dk), lambda h, p, n: (1, blk(p, n), h))
    k_spec = pl.BlockSpec((1, rt, dk), lambda h, p, n: (1, blk(p, n), hq + h))
    v_spec = pl.BlockSpec((1, rt, dk), lambda h, p, n: (2, blk(p, n), h))
    g_spec = pl.BlockSpec((1, rt, dk), lambda h, p, n: (2, blk(p, n), hq + h))
    tab_spec = pl.BlockSpec((rt, dk // 2), lambda h, p, n: (blk(p, n), 0))
    cols_spec = pl.BlockSpec((1, 8, CHUNK, 1), lambda h, p, n: (h, 0, 0, 0))
    mats_spec = pl.BlockSpec((1, 3, CHUNK, CHUNK), lambda h, p, n: (h, 0, 0, 0))
    cdec_spec = pl.BlockSpec((1, 2, dk), lambda h, p, n: (h, 0, 0))
    in_row = pl.BlockSpec((rt, dk), lambda h, p, n: (blk(p, n), h))
    out_row = pl.BlockSpec((rt, dk), lambda h, p, n: (p * n, h))
    return nr, blk, q_spec, k_spec, v_spec, g_spec, tab_spec, cols_spec, mats_spec, cdec_spec, in_row, out_row


def ret_fwd(proj, cols, mats, cdec, dep):
    _, t, w2 = proj.shape
    d = w2 // 2
    dk = d // RET_HEADS
    rt = _row_tile(t)
    cpt = rt // CHUNK
    nr, blk, q_spec, k_spec, v_spec, g_spec, _, cols_spec, mats_spec, cdec_spec, _, out_row = _ret_specs(t, d, dk, rt)

    def body(q_ref, k_ref, v_ref, g_ref, cols_ref, mats_ref, cdec_ref, dep_ref, r_ref, rn_ref, sb_scr, st):
        p = pl.program_id(1)
        n = pl.program_id(2)
        af, ab, kf, kb = cols_ref[0, 0], cols_ref[0, 1], cols_ref[0, 2], cols_ref[0, 3]
        cf = cdec_ref[0, 0:1, :]
        cb = cdec_ref[0, 1:2, :]

        @pl.when(n == 0)
        def _():
            st[...] = jnp.zeros_like(st)

        @pl.when(p == 0)
        def _():
            for j in reversed(range(cpt)):
                rows = slice(j * CHUNK, (j + 1) * CHUNK)
                ch = blk(p, n) * cpt + j
                kk = k_ref[0, rows, :].astype(f32)
                sb_scr[ch] = st[...].astype(bf16)
                st[...] = st[...] * cb + _dot_tn((kk * kb).astype(bf16), v_ref[0, rows, :])

        @pl.when(p == 1)
        def _():
            for j in range(cpt):
                rows = slice(j * CHUNK, (j + 1) * CHUNK)
                ch = blk(p, n) * cpt + j
                qb = q_ref[0, rows, :]
                kkb = k_ref[0, rows, :]
                q = qb.astype(f32)
                kk = kkb.astype(f32)
                v = v_ref[0, rows, :]
                pm = (_dot_nt(qb, kkb) * mats_ref[0, 0]).astype(bf16)
                out = (_dot(pm, v) + _dot((q * af).astype(bf16), st[...].astype(bf16))
                       + _dot((q * ab).astype(bf16), sb_scr[ch]))
                st[...] = st[...] * cf + _dot_tn((kk * kf).astype(bf16), v)
                rhat = out * lax.rsqrt(jnp.mean(out * out, axis=-1, keepdims=True) + NORM_EPS)
                gg = g_ref[0, rows, :].astype(f32)
                r_ref[rows, :] = out.astype(bf16)
                rn_ref[rows, :] = (rhat * gg * _sigmoid(gg)).astype(bf16)

    return pl.pallas_call(
        body, name="ret_fwd", grid=(RET_HEADS, 2, nr),
        in_specs=[q_spec, k_spec, v_spec, g_spec, cols_spec, mats_spec, cdec_spec, _ANY],
        out_specs=[out_row, out_row],
        out_shape=[jax.ShapeDtypeStruct((t, d), bf16), jax.ShapeDtypeStruct((t, d), bf16)],
        scratch_shapes=[pltpu.VMEM((t // CHUNK, dk, dk), bf16), pltpu.VMEM((dk, dk), f32)],
        compiler_params=_cparams(),
    )(proj, proj, proj, proj, cols, mats, cdec, dep)


def ret_bwd(drn, r, proj, cols, mats, cdec, cos, sin):
    _, t, w2 = proj.shape
    d = w2 // 2
    dk = d // RET_HEADS
    rt = _row_tile(t)
    cpt = rt // CHUNK
    nr, blk, q_spec, k_spec, v_spec, g_spec, tab_spec, cols_spec, mats_spec, cdec_spec, in_row, out_row = _ret_specs(t, d, dk, rt)
    scale = dk ** -0.5

    def body(drn_ref, r_ref, q_ref, k_ref, v_ref, g_ref, cos_ref, sin_ref, cols_ref, mats_ref, cdec_ref,
             dq_ref, dk_ref, dv_ref, dg_ref, dlg_ref,
             sb_scr, gf_scr, st_s, st_g, acc_af, acc_ab, acc_vf, acc_vb, acc_sf, acc_sb, dout_scr, dgr_scr):
        p = pl.program_id(1)
        n = pl.program_id(2)
        af, ab, kf, kb = cols_ref[0, 0], cols_ref[0, 1], cols_ref[0, 2], cols_ref[0, 3]
        af1, ab1, kf1, kb1 = cols_ref[0, 4], cols_ref[0, 5], cols_ref[0, 6], cols_ref[0, 7]
        cf = cdec_ref[0, 0:1, :]
        cb = cdec_ref[0, 1:2, :]

        @pl.when(n == 0)
        def _():
            st_s[...] = jnp.zeros_like(st_s)
            st_g[...] = jnp.zeros_like(st_g)

        @pl.when(jnp.logical_and(n == 0, p == 1))
        def _():
            for a in (acc_af, acc_ab, acc_vf, acc_vb, acc_sf, acc_sb):
                a[...] = jnp.zeros_like(a)

        def load(rows):
            cs, sn = cos_ref[rows, :], sin_ref[rows, :]
            q = q_ref[0, rows, :].astype(f32)
            kk = k_ref[0, rows, :].astype(f32)
            rr = r_ref[rows, :].astype(f32)
            rstd = lax.rsqrt(jnp.mean(rr * rr, axis=-1, keepdims=True) + NORM_EPS)
            rhat = rr * rstd
            gg = g_ref[0, rows, :].astype(f32)
            sg = _sigmoid(gg)
            dd = drn_ref[rows, :].astype(f32)
            drhat = dd * gg * sg
            dout = rstd * (drhat - rhat * jnp.mean(drhat * rhat, axis=-1, keepdims=True))
            dgr = dd * rhat * _dsilu(gg, sg)
            return q, kk, dout.astype(bf16), dgr, cs, sn

        @pl.when(p == 0)
        def _():
            for j in reversed(range(cpt)):
                rows = slice(j * CHUNK, (j + 1) * CHUNK)
                ch = blk(p, n) * cpt + j
                q, kk, doutb, dgr, _, _ = load(rows)
                kept = pl.ds(pl.multiple_of(ch * CHUNK, CHUNK), CHUNK)
                dout_scr[kept, :] = doutb
                dgr_scr[kept, :] = dgr.astype(bf16)
                sb_scr[ch] = st_s[...].astype(bf16)
                gf_scr[ch] = st_g[...].astype(bf16)
                st_s[...] = st_s[...] * cb + _dot_tn((kk * kb).astype(bf16), v_ref[0, rows, :])
                st_g[...] = st_g[...] * cf + _dot_tn((q * af).astype(bf16), doutb)

        @pl.when(p == 1)
        def _():
            for j in range(cpt):
                rows = slice(j * CHUNK, (j + 1) * CHUNK)
                ch = blk(p, n) * cpt + j
                kept = pl.ds(pl.multiple_of(ch * CHUNK, CHUNK), CHUNK)
                doutb = dout_scr[kept, :]
                cs, sn = cos_ref[rows, :], sin_ref[rows, :]
                v = v_ref[0, rows, :]
                qb = q_ref[0, rows, :]
                kkb = k_ref[0, rows, :]
                q = qb.astype(f32)
                kk = kkb.astype(f32)
                sf = st_s[...]
                gb = st_g[...]
                sfb = sf.astype(bf16)
                gbb = gb.astype(bf16)
                sbb = sb_scr[ch]
                gfb = gf_scr[ch]
                dmat = mats_ref[0, 0]
                scores = _dot_nt(qb, kkb)
                dpraw = _dot_nt(doutb, v)
                dpb = (dpraw * dmat).astype(bf16)
                pmb = (scores * dmat).astype(bf16)
                x1 = _dot_nt(doutb, sfb)
                x2 = _dot_nt(doutb, sbb)
                y1 = _dot_nt(v, gfb)
                y2 = _dot_nt(v, gbb)
                kdf = (kk * kf).astype(bf16)
                kdb = (kk * kb).astype(bf16)
                dq = _dot(dpb, kkb) + x1 * af + x2 * ab
                dkk = _dot_tn(dpb, qb) + y1 * kf + y2 * kb
                dv = _dot_tn(pmb, doutb) + _dot(kdf, gfb) + _dot(kdb, gbb)
                ps = dpraw * scores
                acc_af[...] += ps * mats_ref[0, 1]
                acc_ab[...] += ps * mats_ref[0, 2]
                acc_vf[...] += x1 * q * af1 + y1 * kk * kf1
                acc_vb[...] += x2 * q * ab1 + y2 * kk * kb1
                acc_sf[...] += gfb.astype(f32) * sf
                acc_sb[...] += gb * sbb.astype(f32)
                st_s[...] = sf * cf + _dot_tn(kdf, v)
                st_g[...] = gb * cb + _dot_tn((q * ab).astype(bf16), doutb)
                dq_ref[rows, :] = _rot_inv(dq, cs, sn).astype(bf16)
                dk_ref[rows, :] = (_rot_inv(dkk, cs, sn) * scale).astype(bf16)
                dv_ref[rows, :] = dv.astype(bf16)
                dg_ref[rows, :] = dgr_scr[kept, :]

        @pl.when(jnp.logical_and(p == 1, n == nr - 1))
        def _():
            tf = jnp.sum(acc_af[...]) + jnp.sum(acc_vf[...]) + CHUNK * jnp.sum(acc_sf[...] * cf)
            tb = jnp.sum(acc_ab[...]) + jnp.sum(acc_vb[...]) + CHUNK * jnp.sum(acc_sb[...] * cb)
            rid = lax.broadcasted_iota(jnp.int32, (8, 128), 0)
            dlg_ref[0] = jnp.where(rid == 0, tf, jnp.where(rid == 1, tb, 0.0))

    nch = t // CHUNK
    return pl.pallas_call(
        body, name="ret_bwd", grid=(RET_HEADS, 2, nr),
        in_specs=[in_row, in_row, q_spec, k_spec, v_spec, g_spec, tab_spec, tab_spec, cols_spec, mats_spec, cdec_spec],
        out_specs=[out_row, out_row, out_row, out_row, pl.BlockSpec((1, 8, 128), lambda h, p, n: (h, 0, 0))],
        out_shape=[jax.ShapeDtypeStruct((t, d), bf16)] * 4 + [jax.ShapeDtypeStruct((RET_HEADS, 8, 128), f32)],
        scratch_shapes=[pltpu.VMEM((nch, dk, dk), bf16), pltpu.VMEM((nch, dk, dk), bf16),
                        pltpu.VMEM((dk, dk), f32), pltpu.VMEM((dk, dk), f32),
                        pltpu.VMEM((CHUNK, CHUNK), f32), pltpu.VMEM((CHUNK, CHUNK), f32),
                        pltpu.VMEM((CHUNK, dk), f32), pltpu.VMEM((CHUNK, dk), f32),
                        pltpu.VMEM((dk, dk), f32), pltpu.VMEM((dk, dk), f32),
                        pltpu.VMEM((t, dk), bf16), pltpu.VMEM((t, dk), bf16)],
        compiler_params=_cparams(),
    )(drn, r, proj, proj, proj, proj, cos, sin, cols, mats, cdec)


def mix_fwd(a, rn, proj, wa, wb, wo, x1):
    t, d = x1.shape
    tm = _row_tile(t)

    def body(a_ref, rn_ref, p_ref, wa_ref, wb_ref, wo_ref, x_ref, xo_ref, ba_ref, br_ref):
        ba = _dot(a_ref[...], wa_ref[...])
        br = _dot(rn_ref[...], wb_ref[...])
        sa = _sigmoid(p_ref[0, :, 0:d].astype(f32))
        sb = _sigmoid(p_ref[0, :, d:2 * d].astype(f32))
        mix = (sa * ba + sb * br).astype(bf16)
        xo_ref[...] = x_ref[...] + _dot(mix, wo_ref[...])
        ba_ref[...] = ba.astype(bf16)
        br_ref[...] = br.astype(bf16)

    row = pl.BlockSpec((tm, d), lambda i: (i, 0))
    wsp = pl.BlockSpec((d, d), lambda i: (0, 0))
    return pl.pallas_call(
        body, name="mix_fwd", grid=(t // tm,),
        in_specs=[row, row, pl.BlockSpec((1, tm, 2 * d), lambda i: (3, i, 0)), wsp, wsp, wsp, row],
        out_specs=[row, row, row],
        out_shape=[jax.ShapeDtypeStruct((t, d), f32), jax.ShapeDtypeStruct((t, d), bf16), jax.ShapeDtypeStruct((t, d), bf16)],
        compiler_params=_cparams(),
    )(a, rn, proj, wa, wb, wo, x1)


def mix_bwd_act(dx2, ba, br, proj, wa, wb, wo, dep):
    t, d = dx2.shape
    tm = _row_tile(t)

    def body(dx_ref, ba_ref, br_ref, p_ref, wa_ref, wb_ref, wo_ref, dep_ref,
             da_ref, drn_ref, dga_ref, dgb_ref, mix_ref, dba_ref, dbr_ref, dxb_ref):
        dxb = dx_ref[...].astype(bf16)
        dxb_ref[...] = dxb
        dmix = _dot_nt(dxb, wo_ref[...])
        ba = ba_ref[...].astype(f32)
        br = br_ref[...].astype(f32)
        sa = _sigmoid(p_ref[0, :, 0:d].astype(f32))
        sb = _sigmoid(p_ref[0, :, d:2 * d].astype(f32))
        mix_ref[...] = (sa * ba + sb * br).astype(bf16)
        dba = (dmix * sa).astype(bf16)
        dbr = (dmix * sb).astype(bf16)
        dba_ref[...] = dba
        dbr_ref[...] = dbr
        dga_ref[...] = (dmix * ba * sa * (1.0 - sa)).astype(bf16)
        dgb_ref[...] = (dmix * br * sb * (1.0 - sb)).astype(bf16)
        da_ref[...] = _dot_nt(dba, wa_ref[...]).astype(bf16)
        drn_ref[...] = _dot_nt(dbr, wb_ref[...]).astype(bf16)

    row = pl.BlockSpec((tm, d), lambda i: (i, 0))
    wsp = pl.BlockSpec((d, d), lambda i: (0, 0))
    return pl.pallas_call(
        body, name="mix_bwd_act", grid=(t // tm,),
        in_specs=[row, row, row, pl.BlockSpec((1, tm, 2 * d), lambda i: (3, i, 0)), wsp, wsp, wsp, _ANY],
        out_specs=[row] * 8,
        out_shape=[jax.ShapeDtypeStruct((t, d), bf16)] * 8,
        compiler_params=_cparams(),
    )(dx2, ba, br, proj, wa, wb, wo, dep)


def inproj_bwd_act(segs, win, x1, ng, dx2):
    t, d = x1.shape
    s4 = win.shape[0]
    tm = _row_tile(t)
    nseg = len(segs)

    def body(*refs):
        seg_refs = refs[:nseg]
        w_ref, x_ref, ng_ref, dx2_ref, dx1_ref, db_ref, dng_ref = refs[nseg:]
        i = pl.program_id(0)
        dh = None
        for e, sr in enumerate(seg_refs):
            sb = sr[...]
            part = _dot_nt(sb, w_ref[e // 2, :, (e % 2) * d:(e % 2 + 1) * d])
            dh = part if dh is None else dh + part
            _acc_out(db_ref.at[e], i == 0, jnp.sum(sb.astype(f32), axis=0, keepdims=True))
        _, xh, r = _rms(x_ref[...], ng_ref[...])
        dx1_ref[...] = dx2_ref[...] + _rms_bwd(dh, xh, r, ng_ref[...])
        _acc_out(dng_ref, i == 0, jnp.sum(dh * xh, axis=0, keepdims=True))

    row = pl.BlockSpec((tm, d), lambda i: (i, 0))
    vec = pl.BlockSpec((1, d), lambda i: (0, 0))
    return pl.pallas_call(
        body, name="inproj_bwd_act", grid=(t // tm,),
        in_specs=[row] * nseg + [pl.BlockSpec((s4, d, 2 * d), lambda i: (0, 0, 0), pipeline_mode=pl.Buffered(1)),
                                 row, vec, row],
        out_specs=[row, pl.BlockSpec((nseg, 1, d), lambda i: (0, 0, 0)), vec],
        out_shape=[jax.ShapeDtypeStruct((t, d), f32), jax.ShapeDtypeStruct((nseg, 1, d), f32),
                   jax.ShapeDtypeStruct((1, d), f32)],
        compiler_params=_cparams(VMEM_LIMIT_WIDE),
    )(*segs, win, x1, ng, dx2)


def _place():
    return lax.axis_index("x"), lax.axis_index("y"), lax.axis_index("c")


def _other_chips(x, y):
    return [(1 - x, y), (x, 1 - y), (1 - x, 1 - y)]


_ANY = pl.BlockSpec(memory_space=pl.ANY)


_HBM = pl.BlockSpec(memory_space=pltpu.HBM)
_SEM = pl.BlockSpec(memory_space=pltpu.SEMAPHORE)
_EFFECT = pltpu.SideEffectType.DATAFLOW_SIDE_EFFECTING


def _hbm(a):
    return pltpu.with_memory_space_constraint(a, pltpu.HBM)


def _half_rows(ref, c):
    half = ref.shape[1] // 2
    return pl.ds(pl.multiple_of(c * half, 16), half)


def _chip_copy(src, dst, send_sem, recv_sem, chip, c):
    return pltpu.make_async_remote_copy(src_ref=src, dst_ref=dst, send_sem=send_sem, recv_sem=recv_sem,
                                        device_id=(chip[0], chip[1], c), device_id_type=MESH)


def gather_start(bufs, groups, name):
    nb, ng = len(bufs), len(groups)

    def body(*refs):
        ins = refs[:nb]
        sems = refs[nb:nb + 2 * ng]
        token = refs[-1]
        x, y, c = _place()
        k = 2 * x + y
        for gi, grp in enumerate(groups):
            for wi, w in enumerate(grp):
                mine = ins[w].at[k, _half_rows(ins[w], c)]
                for j, chip in enumerate(_other_chips(x, y)):
                    _chip_copy(mine, mine, sems[2 * gi].at[3 * wi + j], sems[2 * gi + 1].at[3 * wi + j], chip, c).start()
        token[...] = jnp.zeros_like(token)

    sem_shapes = []
    for grp in groups:
        sem_shapes += [pltpu.SemaphoreType.DMA((3 * len(grp),)), pltpu.SemaphoreType.DMA((3 * len(grp),))]
    outs = pl.pallas_call(
        body, name=name,
        out_shape=sem_shapes + [pltpu.HBM(b.shape, b.dtype) for b in bufs] + [jax.ShapeDtypeStruct((8, 128), f32)],
        in_specs=[_HBM] * nb,
        out_specs=[_SEM] * (2 * ng) + [_HBM] * nb + [pl.BlockSpec(memory_space=pltpu.VMEM)],
        input_output_aliases={w: 2 * ng + w for w in range(nb)},
        compiler_params=pltpu.CompilerParams(has_side_effects=_EFFECT),
    )(*[_hbm(b) for b in bufs])
    sems = [(outs[2 * gi], outs[2 * gi + 1]) for gi in range(ng)]
    return sems, list(outs[2 * ng:2 * ng + nb]), outs[-1]


def gather_wait(bufs, sems, after, name):
    n = len(bufs)

    def body(*refs):
        ins = refs[:n]
        send_sems, recv_sems = refs[n], refs[n + 1]
        x, y, c = _place()
        k = 2 * x + y
        for wi in range(n):
            half = _half_rows(ins[wi], c)
            for j, chip in enumerate(_other_chips(x, y)):
                cp = _chip_copy(ins[wi].at[k, half], ins[wi].at[2 * chip[0] + chip[1], half], send_sems.at[3 * wi + j],
                                recv_sems.at[3 * wi + j], chip, c)
                cp.wait_send()
                cp.wait_recv()

    outs = pl.pallas_call(
        body, name=name,
        out_shape=[pltpu.HBM(b.shape, b.dtype) for b in bufs],
        in_specs=[_HBM] * n + [_SEM, _SEM, _ANY],
        out_specs=[_HBM] * n,
        input_output_aliases={i: i for i in range(n)},
        compiler_params=pltpu.CompilerParams(has_side_effects=_EFFECT),
    )(*bufs, sems[0], sems[1], after)
    return list(outs)


def gather_forward(bufs, name):
    n = len(bufs)

    def body(*refs):
        ins = refs[n:2 * n]
        send_sems, recv_sems = refs[2 * n], refs[2 * n + 1]
        x, y, c = _place()
        copies = []
        for wi in range(n):
            for j, chip in enumerate(_other_chips(x, y)):
                kp = 2 * chip[0] + chip[1]
                got = ins[wi].at[kp, _half_rows(ins[wi], c)]
                cp = pltpu.make_async_remote_copy(
                    src_ref=got, dst_ref=got, send_sem=send_sems.at[3 * wi + j], recv_sem=recv_sems.at[3 * wi + j],
                    device_id=(x, y, 1 - c), device_id_type=MESH)
                cp.start()
                copies.append((cp, wi, kp, j))
        for cp, wi, kp, j in copies:
            cp.wait_send()
            theirs = ins[wi].at[kp, _half_rows(ins[wi], 1 - c)]
            pltpu.make_async_remote_copy(
                src_ref=theirs, dst_ref=theirs, send_sem=send_sems.at[3 * wi + j], recv_sem=recv_sems.at[3 * wi + j],
                device_id=(x, y, 1 - c), device_id_type=MESH).wait_recv()

    outs = pl.pallas_call(
        body, name=name,
        out_shape=[jax.ShapeDtypeStruct(b.shape, b.dtype) for b in bufs],
        in_specs=[_ANY] * n, out_specs=[_ANY] * n,
        input_output_aliases={i: i for i in range(n)},
        scratch_shapes=[pltpu.SemaphoreType.DMA((3 * n,)), pltpu.SemaphoreType.DMA((3 * n,))],
    )(*bufs)
    return list(outs)


def forward_start(bufs, name):
    n = len(bufs)

    def body(*refs):
        x, y, c = _place()
        for wi in range(n):
            for j, chip in enumerate(_other_chips(x, y)):
                got = refs[wi].at[2 * chip[0] + chip[1], _half_rows(refs[wi], c)]
                _sibling_copy(got, got, refs[n].at[3 * wi + j], refs[n + 1].at[3 * wi + j]).start()
        refs[-1][...] = jnp.zeros_like(refs[-1])

    return _split_start(body, name, 3 * n, list(bufs))


def forward_wait(bufs, sems, after, name):
    n = len(bufs)

    def body(*refs):
        x, y, c = _place()
        for wi in range(n):
            for j, chip in enumerate(_other_chips(x, y)):
                kp = 2 * chip[0] + chip[1]
                got = refs[wi].at[kp, _half_rows(refs[wi], c)]
                theirs = refs[wi].at[kp, _half_rows(refs[wi], 1 - c)]
                _sibling_copy(got, got, refs[n].at[3 * wi + j], refs[n + 1].at[3 * wi + j]).wait_send()
                _sibling_copy(theirs, theirs, refs[n].at[3 * wi + j], refs[n + 1].at[3 * wi + j]).wait_recv()

    return _split_wait(body, name, list(bufs), sems, after)


def exchange_start(grads, name):
    n = len(grads)
    lands = [lax.empty((3,) + g.shape[1:], g.dtype) for g in grads]

    def body(*refs):
        ins = refs[:n]
        land = refs[n:2 * n]
        send_sems, recv_sems = refs[2 * n], refs[2 * n + 1]
        token = refs[-1]
        x, y, c = _place()
        for wi in range(n):
            for j, chip in enumerate(_other_chips(x, y)):
                _chip_copy(ins[wi].at[2 * chip[0] + chip[1]], land[wi].at[j], send_sems.at[3 * wi + j],
                           recv_sems.at[3 * wi + j], chip, c).start()
        token[...] = jnp.zeros_like(token)

    outs = pl.pallas_call(
        body, name=name,
        out_shape=[pltpu.SemaphoreType.DMA((3 * n,)), pltpu.SemaphoreType.DMA((3 * n,))]
        + [pltpu.HBM(g.shape, g.dtype) for g in grads] + [pltpu.HBM(l.shape, l.dtype) for l in lands]
        + [jax.ShapeDtypeStruct((8, 128), f32)],
        in_specs=[_HBM] * (2 * n),
        out_specs=[_SEM, _SEM] + [_HBM] * (2 * n) + [pl.BlockSpec(memory_space=pltpu.VMEM)],
        input_output_aliases={i: 2 + i for i in range(2 * n)},
        compiler_params=pltpu.CompilerParams(has_side_effects=_EFFECT),
    )(*[_hbm(g) for g in grads], *[_hbm(l) for l in lands])
    return (outs[0], outs[1]), list(outs[2:2 + n]), list(outs[2 + n:2 + 2 * n]), outs[-1]


def exchange_wait(grads, lands, sems, after, name):
    n = len(grads)

    def body(*refs):
        ins = refs[:n]
        land = refs[n:2 * n]
        send_sems, recv_sems = refs[2 * n], refs[2 * n + 1]
        x, y, c = _place()
        for wi in range(n):
            for j, chip in enumerate(_other_chips(x, y)):
                cp = _chip_copy(ins[wi].at[2 * chip[0] + chip[1]], land[wi].at[j], send_sems.at[3 * wi + j],
                                recv_sems.at[3 * wi + j], chip, c)
                cp.wait_send()
                cp.wait_recv()

    outs = pl.pallas_call(
        body, name=name,
        out_shape=[pltpu.HBM(g.shape, g.dtype) for g in grads] + [pltpu.HBM(l.shape, l.dtype) for l in lands],
        in_specs=[_HBM] * (2 * n) + [_SEM, _SEM, _ANY],
        out_specs=[_HBM] * (2 * n),
        input_output_aliases={i: i for i in range(2 * n)},
        compiler_params=pltpu.CompilerParams(has_side_effects=_EFFECT),
    )(*grads, *lands, sems[0], sems[1], after)
    return list(outs[:n]), list(outs[n:])


def _split_start(body, name, n_sems, operands):
    n = len(operands)
    outs = pl.pallas_call(
        body, name=name,
        out_shape=[pltpu.SemaphoreType.DMA((n_sems,)), pltpu.SemaphoreType.DMA((n_sems,))]
        + [pltpu.HBM(o.shape, o.dtype) for o in operands] + [jax.ShapeDtypeStruct((8, 128), f32)],
        in_specs=[_HBM] * n,
        out_specs=[_SEM, _SEM] + [_HBM] * n + [pl.BlockSpec(memory_space=pltpu.VMEM)],
        input_output_aliases={i: 2 + i for i in range(n)},
        compiler_params=pltpu.CompilerParams(has_side_effects=_EFFECT),
    )(*[_hbm(o) for o in operands])
    return (outs[0], outs[1]), list(outs[2:2 + n]), outs[-1]


def _split_wait(body, name, operands, sems, after):
    n = len(operands)
    outs = pl.pallas_call(
        body, name=name,
        out_shape=[pltpu.HBM(o.shape, o.dtype) for o in operands],
        in_specs=[_HBM] * n + [_SEM, _SEM, _ANY],
        out_specs=[_HBM] * n,
        input_output_aliases={i: i for i in range(n)},
        compiler_params=pltpu.CompilerParams(has_side_effects=_EFFECT),
    )(*operands, sems[0], sems[1], after)
    return list(outs)


def _sibling_copy(src, dst, send_sem, recv_sem):
    x, y, c = _place()
    return pltpu.make_async_remote_copy(src_ref=src, dst_ref=dst, send_sem=send_sem, recv_sem=recv_sem,
                                        device_id=(x, y, 1 - c), device_id_type=MESH)


def swap_start(parts, name):
    n = len(parts)

    def body(*refs):
        for w in range(n):
            _sibling_copy(refs[w], refs[n + w], refs[2 * n].at[w], refs[2 * n + 1].at[w]).start()
        refs[-1][...] = jnp.zeros_like(refs[-1])

    sems, ops, token = _split_start(body, name, n, list(parts) + [lax.empty(p.shape, p.dtype) for p in parts])
    return sems, ops[:n], ops[n:], token


def swap_wait(parts, lands, sems, after, name):
    n = len(parts)

    def body(*refs):
        for w in range(n):
            cp = _sibling_copy(refs[w], refs[n + w], refs[2 * n].at[w], refs[2 * n + 1].at[w])
            cp.wait_send()
            cp.wait_recv()

    outs = _split_wait(body, name, list(parts) + list(lands), sems, after)
    return outs[:n], outs[n:]


def _all_peers(x, y, c):
    return [(1 - x if m & 4 else x, 1 - y if m & 2 else y, 1 - c if m & 1 else c) for m in range(1, N_DEV)]


def small_start(block):
    land = jnp.broadcast_to(block[None], (N_DEV,) + block.shape)

    def body(b_ref, land_ref, send_sems, recv_sems, b_thru, land_thru, token):
        x, y, c = _place()
        me = 4 * x + 2 * y + c
        for m, peer in enumerate(_all_peers(x, y, c)):
            pltpu.make_async_remote_copy(src_ref=b_ref, dst_ref=land_ref.at[me], send_sem=send_sems.at[m],
                                         recv_sem=recv_sems.at[m], device_id=peer, device_id_type=MESH).start()
        token[...] = jnp.zeros_like(token)

    sems, ops, token = _split_start(body, "small_start", N_DEV - 1, [block, land])
    return sems, ops[0], ops[1], token


def small_wait(block, land, sems, after):
    def body(b_ref, land_ref, send_sems, recv_sems, after_ref, b_thru, land_thru):
        x, y, c = _place()
        for m, (px, py, pc) in enumerate(_all_peers(x, y, c)):
            cp = pltpu.make_async_remote_copy(src_ref=b_ref, dst_ref=land_ref.at[4 * px + 2 * py + pc],
                                              send_sem=send_sems.at[m], recv_sem=recv_sems.at[m],
                                              device_id=(px, py, pc), device_id_type=MESH)
            cp.wait_send()
            cp.wait_recv()

    return _split_wait(body, "small_wait", [block, land], sems, after)[1]


def _adamw(w, g, m, v):
    m = ADAM_B1 * m + (1.0 - ADAM_B1) * g
    v = ADAM_B2 * v + (1.0 - ADAM_B2) * (g * g)
    m_hat = m / (1.0 - ADAM_B1 ** ADAM_STEP)
    v_hat = v / (1.0 - ADAM_B2 ** ADAM_STEP)
    delta = -ADAM_LR * (m_hat / (jnp.sqrt(v_hat) + ADAM_EPS) + ADAM_WD * w)
    return delta, m, v


EW_BLOCK_BYTES = 2 * 1024 * 1024


def _ew_tile(rows, cols):
    for cand in (512, 352, 256, 176, 128, 64, 32, 16, 8):
        if rows % cand == 0 and cand * cols * 4 <= EW_BLOCK_BYTES:
            return cand
    return rows


def sum_partials(chip, own, land, name):
    _, r, c = own.shape
    tr = _ew_tile(r, c)

    def body(k_ref, own_ref, p_ref, o_ref):
        o_ref[...] = ((own_ref[0].astype(f32) + p_ref[0].astype(f32)) + p_ref[1].astype(f32)) + p_ref[2].astype(f32)

    return pl.pallas_call(
        body, name=name,
        grid_spec=pltpu.PrefetchScalarGridSpec(
            num_scalar_prefetch=1, grid=(r // tr,),
            in_specs=[pl.BlockSpec((1, tr, c), lambda i, k: (k[0], i, 0)), pl.BlockSpec((3, tr, c), lambda i, k: (0, i, 0))],
            out_specs=pl.BlockSpec((tr, c), lambda i, k: (i, 0))),
        out_shape=jax.ShapeDtypeStruct((r, c), f32),
        compiler_params=_cparams(),
    )(chip, own, land)


def adamw_shard(p_mine, p_sibling, w, m, v, name):
    r, c = w.shape
    tr = _ew_tile(r, c)

    def body(a_ref, b_ref, w_ref, m_ref, v_ref, g_ref, d_ref, mo_ref, vo_ref):
        g = a_ref[...] + b_ref[...]
        delta, mn, vn = _adamw(w_ref[...], g, m_ref[...], v_ref[...])
        g_ref[...] = g
        d_ref[...] = delta
        mo_ref[...] = mn
        vo_ref[...] = vn

    blk = pl.BlockSpec((tr, c), lambda i: (i, 0))
    return pl.pallas_call(
        body, name=name, grid=(r // tr,),
        in_specs=[blk] * 5, out_specs=[blk] * 4,
        out_shape=[jax.ShapeDtypeStruct((r, c), f32)] * 4,
        compiler_params=_cparams(),
    )(p_mine, p_sibling, w, m, v)


def adamw_small(g8, w, m, v):
    _, r, lanes = g8.shape

    def body(g_ref, w_ref, m_ref, v_ref, go_ref, d_ref, mo_ref, vo_ref):
        g = g_ref[0]
        for i in range(1, N_DEV):
            g = g + g_ref[i]
        delta, mn, vn = _adamw(w_ref[...], g, m_ref[...], v_ref[...])
        go_ref[...] = g
        d_ref[...] = delta
        mo_ref[...] = mn
        vo_ref[...] = vn

    return pl.pallas_call(
        body, name="adamw_small",
        out_shape=[jax.ShapeDtypeStruct((r, lanes), f32)] * 4,
        compiler_params=_cparams(),
    )(g8, w, m, v)


def _size(shape):
    n = 1
    for e in shape:
        n *= e
    return n


def _pack_rows(shapes):
    rows = [-(-_size(s) // 1024) * 8 for s in shapes]
    return rows, sum(rows)


def _pack(arrs, shapes):
    rows, _ = _pack_rows(shapes)
    parts = [jnp.pad(a.reshape(-1).astype(f32), (0, r * 128 - _size(s))).reshape(r, 128)
             for a, s, r in zip(arrs, shapes, rows)]
    return jnp.concatenate(parts, axis=0)


def _unpack(block, shapes):
    rows, _ = _pack_rows(shapes)
    out, off = [], 0
    for s, r in zip(shapes, rows):
        out.append(block[off:off + r].reshape(-1)[:_size(s)].reshape(s))
        off += r
    return out


TRANSPOSED = ("ffn1_w_gate", "ffn1_w_up", "ffn2_w_gate", "ffn2_w_up")


def _shard2d(a, n):
    return a[0].T if n in TRANSPOSED else a[0]


def _unshard(a, n):
    return (a.T if n in TRANSPOSED else a)[None]


BIG = ("ffn1_w_gate", "ffn1_w_up", "ffn1_w_down", "w_in", "w_branch_a", "w_branch_b", "w_out",
       "ffn2_w_gate", "ffn2_w_up", "ffn2_w_down")
SMALL = ("ffn1_norm", "mix_norm", "b_in", "sgu_norm_g", "sgu_norm_b", "sgu_w_s", "sgu_b_s", "ret_decay_logit",
         "ffn2_norm", "final_norm")
WEIGHTS = ("ffn1_norm", "ffn1_w_gate", "ffn1_w_up", "ffn1_w_down", "mix_norm", "w_in", "b_in", "sgu_norm_g",
           "sgu_norm_b", "sgu_w_s", "sgu_b_s", "ret_decay_logit", "w_branch_a", "w_branch_b", "w_out", "ffn2_norm",
           "ffn2_w_gate", "ffn2_w_up", "ffn2_w_down", "final_norm")


def kernel(x, ffn1_norm, ffn1_w_gate, ffn1_w_up, ffn1_w_down, mix_norm, w_in, b_in, sgu_norm_g, sgu_norm_b, sgu_w_s, sgu_b_s, ret_decay_logit, w_branch_a, w_branch_b, w_out, ffn2_norm, ffn2_w_gate, ffn2_w_up, ffn2_w_down, final_norm, loss_target, m_ffn1_norm, m_ffn1_w_gate, m_ffn1_w_up, m_ffn1_w_down, m_mix_norm, m_w_in, m_b_in, m_sgu_norm_g, m_sgu_norm_b, m_sgu_w_s, m_sgu_b_s, m_ret_decay_logit, m_w_branch_a, m_w_branch_b, m_w_out, m_ffn2_norm, m_ffn2_w_gate, m_ffn2_w_up, m_ffn2_w_down, m_final_norm, v_ffn1_norm, v_ffn1_w_gate, v_ffn1_w_up, v_ffn1_w_down, v_mix_norm, v_w_in, v_b_in, v_sgu_norm_g, v_sgu_norm_b, v_sgu_w_s, v_sgu_b_s, v_ret_decay_logit, v_w_branch_a, v_w_branch_b, v_w_out, v_ffn2_norm, v_ffn2_w_gate, v_ffn2_w_up, v_ffn2_w_down, v_final_norm):
    p = dict(ffn1_norm=ffn1_norm, ffn1_w_gate=ffn1_w_gate, ffn1_w_up=ffn1_w_up, ffn1_w_down=ffn1_w_down,
             mix_norm=mix_norm, w_in=w_in, b_in=b_in, sgu_norm_g=sgu_norm_g, sgu_norm_b=sgu_norm_b, sgu_w_s=sgu_w_s,
             sgu_b_s=sgu_b_s, ret_decay_logit=ret_decay_logit, w_branch_a=w_branch_a, w_branch_b=w_branch_b,
             w_out=w_out, ffn2_norm=ffn2_norm, ffn2_w_gate=ffn2_w_gate, ffn2_w_up=ffn2_w_up, ffn2_w_down=ffn2_w_down,
             final_norm=final_norm)
    mom = dict(ffn1_norm=m_ffn1_norm, ffn1_w_gate=m_ffn1_w_gate, ffn1_w_up=m_ffn1_w_up, ffn1_w_down=m_ffn1_w_down,
               mix_norm=m_mix_norm, w_in=m_w_in, b_in=m_b_in, sgu_norm_g=m_sgu_norm_g, sgu_norm_b=m_sgu_norm_b,
               sgu_w_s=m_sgu_w_s, sgu_b_s=m_sgu_b_s, ret_decay_logit=m_ret_decay_logit, w_branch_a=m_w_branch_a,
               w_branch_b=m_w_branch_b, w_out=m_w_out, ffn2_norm=m_ffn2_norm, ffn2_w_gate=m_ffn2_w_gate,
               ffn2_w_up=m_ffn2_w_up, ffn2_w_down=m_ffn2_w_down, final_norm=m_final_norm)
    var = dict(ffn1_norm=v_ffn1_norm, ffn1_w_gate=v_ffn1_w_gate, ffn1_w_up=v_ffn1_w_up, ffn1_w_down=v_ffn1_w_down,
               mix_norm=v_mix_norm, w_in=v_w_in, b_in=v_b_in, sgu_norm_g=v_sgu_norm_g, sgu_norm_b=v_sgu_norm_b,
               sgu_w_s=v_sgu_w_s, sgu_b_s=v_sgu_b_s, ret_decay_logit=v_ret_decay_logit, w_branch_a=v_w_branch_a,
               w_branch_b=v_w_branch_b, w_out=v_w_out, ffn2_norm=v_ffn2_norm, ffn2_w_gate=v_ffn2_w_gate,
               ffn2_w_up=v_ffn2_w_up, ffn2_w_down=v_ffn2_w_down, final_norm=v_final_norm)

    xs = x[0]
    tgt = loss_target[0]
    t, d = xs.shape
    dk = d // RET_HEADS
    tm = _row_tile(t)

    shards2d = {n: _shard2d(p[n], n) for n in BIG}
    chip = (2 * lax.axis_index("x") + lax.axis_index("y")).astype(jnp.int32).reshape(1)
    groups = {"ffn1": ("ffn1_w_gate", "ffn1_w_up", "ffn1_w_down"), "in": ("w_in",),
              "mix": ("w_branch_a", "w_branch_b", "w_out"), "ffn2": ("ffn2_w_gate", "ffn2_w_up", "ffn2_w_down")}
    def own_slot(n, zero):
        sh = shards2d[n].astype(bf16) + zero
        return lax.dynamic_update_index_in_dim(lax.empty((N_CHIPS,) + sh.shape, bf16), sh, chip[0], 0)

    sems, bufs, tok = gather_start([own_slot(n, jnp.zeros((), bf16)) for n in groups["ffn1"]], [[0, 1, 2]],
                                   "gather_start_ffn1")
    gsem = {"ffn1": sems[0]}
    pending = dict(zip(groups["ffn1"], bufs))
    rest = [n for g in ("in", "mix", "ffn2") for n in groups[g]]
    sems, bufs, tok_rest = gather_start([own_slot(n, tok[0, 0].astype(bf16)) for n in rest],
                                 [[rest.index(n) for n in groups[g]] for g in ("in", "mix", "ffn2")], "gather_start_rest")
    gsem.update(zip(("in", "mix", "ffn2"), sems))
    pending.update(zip(rest, bufs))

    def arrive(gs, after):
        got = []
        for g in gs:
            got += gather_wait([pending[n] for n in groups[g]], gsem[g], after, "gather_wait_" + g)
        return gather_forward(got, "gather_forward_" + gs[0])

    bin4 = b_in.reshape(N_CHIPS, 1, 2 * d)
    ws_b = sgu_w_s[0].astype(bf16)
    bs_c = sgu_b_s[0][:, :, None]
    cols, mats, cdec, cos, sin = retention_constants(ret_decay_logit[0], t, dk, tok_rest[0, 0])

    wg1, wu1, wd1 = [_pair_shards(w) for w in arrive(["ffn1"], cos)]
    x1, g1, u1 = ffn_fwd(xs, ffn1_norm, wg1, wu1, wd1, "ffn1_fwd")
    win, = arrive(["in"], x1)
    proj, hb2 = inproj_fwd(x1, mix_norm, win, bin4, cos, sin)
    late = []
    for g in ("mix", "ffn2"):
        late += gather_wait([pending[n] for n in groups[g]], gsem[g], proj, "gather_wait_" + g)
    fsems, late, ftok = forward_start(late, "forward_start_mix")
    a = sgu_fwd(proj, sgu_norm_g, sgu_norm_b, ws_b, bs_c, ftok)
    r, rn = ret_fwd(proj, cols, mats, cdec, a)
    wa, wb, wo, wg2, wu2, wd2 = forward_wait(late, fsems, rn, "forward_wait_mix")
    wa, wb, wo = [w.reshape(d, d) for w in (wa, wb, wo)]
    wg2, wu2, wd2 = [_pair_shards(w) for w in (wg2, wu2, wd2)]
    x2, ba, br = mix_fwd(a, rn, proj, wa, wb, wo, x1)
    loss_blk, dx3, d_final, g2, u2 = ffn_fwd_loss(x2, ffn2_norm, wg2, wu2, wd2, final_norm.reshape(1, d), tgt, "ffn2_fwd")

    sent, swaps = {}, {}
    out_g, out_d, out_m, out_v = {}, {}, {}, {}

    def reduce_plane(g, after):
        gsems, own, lands, _ = sent[g]
        own, lands = exchange_wait(own, lands, gsems, after, "exchange_wait_" + g)
        plane = [sum_partials(chip, o, l, "sum_" + n) for n, o, l in zip(groups[g], own, lands)]
        swaps[g] = swap_start(plane, "swap_start_" + g)
        return swaps[g][3]

    def update(g, after):
        ssems, plane, lands, _ = swaps[g]
        plane, other = swap_wait(plane, lands, ssems, after, "swap_wait_" + g)
        for n, mine, sib in zip(groups[g], plane, other):
            res = adamw_shard(mine, sib, shards2d[n], _shard2d(mom[n], n), _shard2d(var[n], n), "adamw_" + n)
            out_g[n], out_d[n], out_m[n], out_v[n] = [_unshard(o, n) for o in res]
        return res[0]

    dx2, dg2, du2, act2, hb3, dyb2, d_ffn2n = ffn_bwd_act(dx3, x2, ffn2_norm, g2, u2, wg2, wu2, wd2, "ffn2_bwd_act", tok)
    sent["ffn2"] = exchange_start(ffn_weight_grads(hb3, dyb2, dg2, du2, act2, "ffn2_grad", tok), "exchange_start_ffn2")
    da, drn, dga, dgb, mixb, dba, dbr, dx2b = mix_bwd_act(dx2, ba, br, proj, wa, wb, wo, sent["ffn2"][3])
    tg = min(t, 2048)
    row = pl.BlockSpec((tg, d), lambda s, i: (i, 0))

    def square_grad(xa, ya, name):
        return tn_matmul(xa, [ya], row, [row], 1, d, [d], t, tg, name, tok).reshape(N_CHIPS, d // N_CHIPS, d)

    g_mix = [square_grad(a, dba, "grad_w_branch_a"), square_grad(rn, dbr, "grad_w_branch_b"),
             square_grad(mixb, dx2b, "grad_w_out")]
    dua, dva, d_ws, d_bs, d_sng, d_snb = sgu_bwd(da, proj, sgu_norm_g, sgu_norm_b, ws_b, bs_c, sent["ffn2"][3])
    dq, dkr, dv, dgr, dlg = ret_bwd(drn, r, proj, cols, mats, cdec, cos, sin)
    segs = [dua, dva, dq, dkr, dv, dgr, dga, dgb]
    dx1, d_bin, d_mixn = inproj_bwd_act(segs, win, x1, mix_norm, dx2)
    g_in = None
    for s in range(N_CHIPS):
        g_in = tn_matmul(hb2, [segs[2 * s], segs[2 * s + 1]], row, [row, row], 1, d, [d, d], t, tg, "grad_w_in_%d" % s,
                         tok, (g_in, s, N_CHIPS))
    groups["mix_in"] = groups["mix"] + groups["in"]
    sent["mix_in"] = exchange_start(g_mix + [g_in], "exchange_start_mix_in")
    grad_x, dg1, du1, act1, hb1, dyb1, d_ffn1n = ffn_bwd_act(dx1, xs, ffn1_norm, g1, u1, wg1, wu1, wd1, "ffn1_bwd_act",
                                                              sent["mix_in"][3])
    dlogit = dlg[:, 0:2, 0].T * jax.nn.sigmoid(-ret_decay_logit[0].astype(f32))
    small_g = dict(ffn1_norm=d_ffn1n, mix_norm=d_mixn, b_in=d_bin, sgu_norm_g=d_sng, sgu_norm_b=d_snb, sgu_w_s=d_ws,
                   sgu_b_s=d_bs, ret_decay_logit=dlogit, ffn2_norm=d_ffn2n, final_norm=d_final)
    shapes = [p[n].shape for n in SMALL]
    small_sems, small_blk, small_land, small_tok = small_start(_pack([small_g[n] for n in SMALL], shapes))

    def send_one(which, grad):
        n = "ffn1_" + which
        groups[n] = (n,)
        sent[n] = exchange_start([grad], "exchange_start_" + n)
        return sent[n][3]

    ffn_weight_grads(hb1, dyb1, dg1, du1, act1, "ffn1_grad", small_tok, send_one)

    after = reduce_plane("ffn2", sent["ffn1_w_down"][3])
    after = reduce_plane("mix_in", after)
    after = update("ffn2", after)
    g8 = small_wait(small_blk, small_land, small_sems, after)
    sg, sd, sm, sv = adamw_small(g8, _pack([p[n] for n in SMALL], shapes), _pack([mom[n] for n in SMALL], shapes),
                                 _pack([var[n] for n in SMALL], shapes))
    for res, blockv in ((out_g, sg), (out_d, sd), (out_m, sm), (out_v, sv)):
        for n, val in zip(SMALL, _unpack(blockv, shapes)):
            res[n] = val
    after = update("mix_in", sg)
    after = reduce_plane("ffn1_w_gate", after)
    after = reduce_plane("ffn1_w_up", after)
    after = update("ffn1_w_gate", after)
    after = reduce_plane("ffn1_w_down", after)
    after = update("ffn1_w_up", after)
    update("ffn1_w_down", after)

    loss = lax.psum(loss_blk[0, 0], ("x", "y", "c"))
    return (loss, grad_x[None], *[out_g[n] for n in WEIGHTS], *[out_d[n] for n in WEIGHTS],
            *[out_m[n] for n in WEIGHTS], *[out_v[n] for n in WEIGHTS])
```

```python
import jax
import jax.numpy as jnp
from jax import lax
from jax.experimental import pallas as pl
from jax.experimental.pallas import tpu as pltpu

f32 = jnp.float32
bf16 = jnp.bfloat16

SGU_CHUNK = 128
CHUNK = 256
RET_HEADS = 4
SGU_GROUPS = 4
ROPE_BASE = 10000.0
NORM_EPS = 1e-6
ADAM_LR = 0.001
ADAM_B1 = 0.9
ADAM_B2 = 0.999
ADAM_EPS = 1e-08
ADAM_WD = 0.01
ADAM_STEP = 10
N_CHIPS = 4
N_DEV = 8
MESH = pl.DeviceIdType.MESH
VMEM_LIMIT = 52 * 1024 * 1024
VMEM_LIMIT_WIDE = 62 * 1024 * 1024

_NT = (((1,), (1,)), ((), ()))
_TN = (((0,), (0,)), ((), ()))


def _cparams(limit=None):
    return pltpu.CompilerParams(vmem_limit_bytes=VMEM_LIMIT if limit is None else limit)


def _row_tile(t):
    return 512 if t >= 2048 else t // 2


def _dot(a, b):
    return jnp.dot(a, b, preferred_element_type=f32)


def _dot_nt(a, b):
    return lax.dot_general(a, b, _NT, preferred_element_type=f32)


def _dot_tn(a, b):
    return lax.dot_general(a, b, _TN, preferred_element_type=f32)


def _rms(x, g):
    r = lax.rsqrt(jnp.mean(x * x, axis=-1, keepdims=True) + NORM_EPS)
    xh = x * r
    return xh * g, xh, r


def _rms_bwd(dy, xh, r, g):
    dxh = dy * g
    return r * (dxh - xh * jnp.mean(dxh * xh, axis=-1, keepdims=True))


def _sigmoid(x):
    return jax.nn.sigmoid(x)


def _dsilu(g, sg):
    return sg * (1.0 + g * (1.0 - sg))


def _gelu(x):
    return 0.5 * x * (1.0 + lax.erf(x * 0.7071067811865476))


def _dgelu(x):
    return 0.5 * (1.0 + lax.erf(x * 0.7071067811865476)) + x * jnp.exp(-0.5 * x * x) * 0.3989422804014327


def _acc_out(ref, first, val):
    @pl.when(first)
    def _():
        ref[...] = val

    @pl.when(jnp.logical_not(first))
    def _():
        ref[...] += val


def _ffn_tile(t):
    return 256 if t >= 2048 else t // 2


def _ffn_fwd_rows(xx, ng_ref, wg_ref, wu_ref, wd_ref, g_ref, u_ref):
    y, _, _ = _rms(xx, ng_ref[...])
    h = y.astype(bf16)
    acc = None
    for s in range(wg_ref.shape[0]):
        g = _dot_nt(h, wg_ref[s])
        u = _dot_nt(h, wu_ref[s])
        g_ref[s] = g.astype(bf16)
        u_ref[s] = u.astype(bf16)
        part = _dot((g * _sigmoid(g) * u).astype(bf16), wd_ref[s])
        acc = part if acc is None else acc + part
    return xx + 0.5 * acc


def ffn_fwd(x, ng, wg, wu, wd, name):
    t, d = x.shape
    ns, fs, _ = wg.shape
    tm = _ffn_tile(t)

    def body(x_ref, ng_ref, wg_ref, wu_ref, wd_ref, xo_ref, g_ref, u_ref):
        xo_ref[...] = _ffn_fwd_rows(x_ref[...], ng_ref, wg_ref, wu_ref, wd_ref, g_ref, u_ref)

    row = pl.BlockSpec((tm, d), lambda i: (i, 0))
    shard = pl.BlockSpec((ns, tm, fs), lambda i: (0, i, 0))
    wspec = pl.BlockSpec((ns, fs, d), lambda i: (0, 0, 0), pipeline_mode=pl.Buffered(1))
    return pl.pallas_call(
        body, name=name, grid=(t // tm,),
        in_specs=[row, pl.BlockSpec((1, d), lambda i: (0, 0)), wspec, wspec, wspec],
        out_specs=[row, shard, shard],
        out_shape=[jax.ShapeDtypeStruct((t, d), f32), jax.ShapeDtypeStruct((ns, t, fs), bf16),
                   jax.ShapeDtypeStruct((ns, t, fs), bf16)],
        compiler_params=_cparams(),
    )(x, ng, wg, wu, wd)


def ffn_fwd_loss(x, ng, wg, wu, wd, fng, tgt, name):
    t, d = x.shape
    ns, fs, _ = wg.shape
    tm = _ffn_tile(t)

    def body(x_ref, ng_ref, wg_ref, wu_ref, wd_ref, fng_ref, t_ref, loss_ref, dx_ref, dfn_ref, g_ref, u_ref):
        i = pl.program_id(0)
        x3 = _ffn_fwd_rows(x_ref[...], ng_ref, wg_ref, wu_ref, wd_ref, g_ref, u_ref)
        y, xh, r = _rms(x3, fng_ref[...])
        diff = y - t_ref[...]
        part = 0.5 * jnp.sum(jnp.sum(diff * diff, axis=0, keepdims=True), axis=1, keepdims=True) / d
        _acc_out(loss_ref, i == 0, jnp.broadcast_to(part, (1, 128)))
        dy = diff * (1.0 / d)
        dx_ref[...] = _rms_bwd(dy, xh, r, fng_ref[...])
        _acc_out(dfn_ref, i == 0, jnp.sum(dy * xh, axis=0, keepdims=True))

    row = pl.BlockSpec((tm, d), lambda i: (i, 0))
    vec = pl.BlockSpec((1, d), lambda i: (0, 0))
    shard = pl.BlockSpec((ns, tm, fs), lambda i: (0, i, 0))
    wspec = pl.BlockSpec((ns, fs, d), lambda i: (0, 0, 0), pipeline_mode=pl.Buffered(1))
    return pl.pallas_call(
        body, name=name, grid=(t // tm,),
        in_specs=[row, vec, wspec, wspec, wspec, vec, row],
        out_specs=[pl.BlockSpec((1, 128), lambda i: (0, 0)), row, vec, shard, shard],
        out_shape=[jax.ShapeDtypeStruct((1, 128), f32), jax.ShapeDtypeStruct((t, d), f32), jax.ShapeDtypeStruct((1, d), f32),
                   jax.ShapeDtypeStruct((ns, t, fs), bf16), jax.ShapeDtypeStruct((ns, t, fs), bf16)],
        compiler_params=_cparams(),
    )(x, ng, wg, wu, wd, fng, tgt)


def ffn_bwd_act(dxo, x, ng, g, u, wg, wu, wd, name, dep):
    t, d = x.shape
    ns, fs, _ = wg.shape
    tm = _ffn_tile(t)

    def body(dxo_ref, x_ref, ng_ref, g_ref, u_ref, wg_ref, wu_ref, wd_ref, dep_ref,
             dx_ref, dg_ref, du_ref, act_ref, hb_ref, dyb_ref, dng_ref):
        i = pl.program_id(0)
        dxo = dxo_ref[...]
        dyb = (0.5 * dxo).astype(bf16)
        dyb_ref[...] = dyb
        dh = None
        for s in range(ns):
            dact = _dot_nt(dyb, wd_ref[s])
            gg = g_ref[s].astype(f32)
            uu = u_ref[s].astype(f32)
            sg = _sigmoid(gg)
            sil = gg * sg
            dgb = (dact * uu * _dsilu(gg, sg)).astype(bf16)
            dub = (dact * sil).astype(bf16)
            dg_ref[s] = dgb
            du_ref[s] = dub
            act_ref[s] = (sil * uu).astype(bf16)
            part = _dot(dgb, wg_ref[s]) + _dot(dub, wu_ref[s])
            dh = part if dh is None else dh + part
        y, xh, r = _rms(x_ref[...], ng_ref[...])
        hb_ref[...] = y.astype(bf16)
        dx_ref[...] = dxo + _rms_bwd(dh, xh, r, ng_ref[...])
        _acc_out(dng_ref, i == 0, jnp.sum(dh * xh, axis=0, keepdims=True))

    row = pl.BlockSpec((tm, d), lambda i: (i, 0))
    shard = pl.BlockSpec((ns, tm, fs), lambda i: (0, i, 0))
    wspec = pl.BlockSpec((ns, fs, d), lambda i: (0, 0, 0), pipeline_mode=pl.Buffered(1))
    vec = pl.BlockSpec((1, d), lambda i: (0, 0))
    return pl.pallas_call(
        body, name=name, grid=(t // tm,),
        in_specs=[row, row, vec, shard, shard, wspec, wspec, wspec, _ANY],
        out_specs=[row, shard, shard, shard, row, row, vec],
        out_shape=[jax.ShapeDtypeStruct((t, d), f32)] + [jax.ShapeDtypeStruct((ns, t, fs), bf16)] * 3
        + [jax.ShapeDtypeStruct((t, d), bf16)] * 2 + [jax.ShapeDtypeStruct((1, d), f32)],
        compiler_params=_cparams(VMEM_LIMIT_WIDE),
    )(dxo, x, ng, g, u, wg, wu, wd, dep)


def tn_matmul(xs, ys, x_spec, y_specs, n_shards, k1, k2s, t, tm, name, dep, into=None):
    k2 = sum(k2s)
    ny = len(ys)

    def body(*refs):
        x_ref = refs[0]
        y_refs = refs[1:1 + ny]
        o_ref, acc = refs[-2], refs[-1]
        i = pl.program_id(1)
        xb = x_ref[0] if len(x_ref.shape) == 3 else x_ref[...]
        off = 0
        for y_ref, w in zip(y_refs, k2s):
            yb = y_ref[0] if len(y_ref.shape) == 3 else y_ref[...]
            part = _dot_tn(xb, yb)
            sl = (slice(None), slice(off, off + w))

            @pl.when(i == 0)
            def _(part=part, sl=sl):
                acc[sl] = part

            @pl.when(i > 0)
            def _(part=part, sl=sl):
                acc[sl] += part

            off += w

        @pl.when(i == t // tm - 1)
        def _():
            o_ref[0] = acc[...].astype(bf16)

    if into is None:
        slot0, total, extra, aliases = 0, n_shards, [], {}
    else:
        buf, slot0, total = into
        extra = [] if buf is None else [buf]
        aliases = {} if buf is None else {2 + ny: 0}
    return pl.pallas_call(
        body, name=name, grid=(n_shards, t // tm),
        in_specs=[x_spec] + list(y_specs) + [_ANY] * (1 + len(extra)),
        out_specs=pl.BlockSpec((1, k1, k2), lambda s, i: (slot0 + s, 0, 0)),
        out_shape=jax.ShapeDtypeStruct((total, k1, k2), bf16),
        scratch_shapes=[pltpu.VMEM((k1, k2), f32)],
        input_output_aliases=aliases,
        compiler_params=_cparams(),
    )(xs, *ys, dep, *extra)


def _pair_shards(w):
    s4, fs, d = w.shape
    return w.reshape(s4 // 2, 2 * fs, d)


def ffn_weight_grads(hb, dyb, dg, du, act, name, dep, each=None):
    t, d = hb.shape
    s2, _, fs2 = dg.shape
    tm = t
    row = pl.BlockSpec((tm, d), lambda s, i: (i, 0))
    shard = pl.BlockSpec((1, tm, fs2), lambda s, i: (s, i, 0))
    grads = []
    for xa, ya, which in ((dg, hb, "w_gate"), (du, hb, "w_up"), (act, dyb, "w_down")):
        g = tn_matmul(xa, [ya], shard, [row], s2, fs2, [d], t, tm, name + "_" + which, dep)
        g = g.reshape(2 * s2, fs2 // 2, d)
        if each is not None:
            dep = each(which, g)
        grads.append(g)
    return grads


def inproj_fwd(x1, ng, win, bin4, cos, sin):
    t, d = x1.shape
    s4, _, w2 = win.shape
    tm = _row_tile(t)
    dk = d // RET_HEADS
    scale = dk ** -0.5

    def body(x_ref, ng_ref, w_ref, b_ref, cos_ref, sin_ref, p_ref, hb_ref):
        y, _, _ = _rms(x_ref[...], ng_ref[...])
        h = y.astype(bf16)
        hb_ref[...] = h
        for s in range(s4):
            p = _dot(h, w_ref[s]) + b_ref[s]
            if s != 1:
                p_ref[s] = p.astype(bf16)
            else:
                cs, sn = cos_ref[...], sin_ref[...]
                for e in range(2 * RET_HEADS):
                    cols = slice(e * dk, (e + 1) * dk)
                    rot = _rot(p[:, cols], cs, sn)
                    p_ref[s, :, cols] = (rot if e < RET_HEADS else rot * scale).astype(bf16)

    tab = pl.BlockSpec((tm, dk // 2), lambda i: (i, 0))
    return pl.pallas_call(
        body, name="inproj_fwd", grid=(t // tm,),
        in_specs=[pl.BlockSpec((tm, d), lambda i: (i, 0)), pl.BlockSpec((1, d), lambda i: (0, 0)),
                  pl.BlockSpec((s4, d, w2), lambda i: (0, 0, 0), pipeline_mode=pl.Buffered(1)),
                  pl.BlockSpec((s4, 1, w2), lambda i: (0, 0, 0)), tab, tab],
        out_specs=[pl.BlockSpec((s4, tm, w2), lambda i: (0, i, 0)), pl.BlockSpec((tm, d), lambda i: (i, 0))],
        out_shape=[jax.ShapeDtypeStruct((s4, t, w2), bf16), jax.ShapeDtypeStruct((t, d), bf16)],
        compiler_params=_cparams(),
    )(x1, ng, win, bin4, cos, sin)


def _sgu_norm(va, ng, nb):
    gv = _gelu(va)
    mu = jnp.mean(gv, axis=-1, keepdims=True)
    xc = gv - mu
    rstd = lax.rsqrt(jnp.mean(xc * xc, axis=-1, keepdims=True) + NORM_EPS)
    xh = xc * rstd
    return xh, rstd, (xh * ng + nb).astype(bf16)


def sgu_fwd(proj, ng, nb, ws, bs, dep):
    _, t, w2 = proj.shape
    d = w2 // 2
    gd = d // SGU_GROUPS
    tm = _row_tile(t)

    def body(p_ref, ng_ref, nb_ref, ws_ref, bs_ref, dep_ref, a_ref):
        ua = p_ref[0, :, 0:d].astype(f32)
        va = p_ref[0, :, d:w2].astype(f32)
        gu = _gelu(ua)
        _, _, vn = _sgu_norm(va, ng_ref[...], nb_ref[...])
        for c in range(tm // SGU_CHUNK):
            rows = slice(c * SGU_CHUNK, (c + 1) * SGU_CHUNK)
            for g in range(SGU_GROUPS):
                cols = slice(g * gd, (g + 1) * gd)
                sg = _dot(ws_ref[g], vn[rows, cols]) + bs_ref[g]
                a_ref[rows, cols] = (gu[rows, cols] * sg).astype(bf16)

    return pl.pallas_call(
        body, name="sgu_fwd", grid=(t // tm,),
        in_specs=[pl.BlockSpec((1, tm, w2), lambda i: (0, i, 0)), pl.BlockSpec((1, d), lambda i: (0, 0)),
                  pl.BlockSpec((1, d), lambda i: (0, 0)), pl.BlockSpec((SGU_GROUPS, SGU_CHUNK, SGU_CHUNK), lambda i: (0, 0, 0)),
                  pl.BlockSpec((SGU_GROUPS, SGU_CHUNK, 1), lambda i: (0, 0, 0)), _ANY],
        out_specs=pl.BlockSpec((tm, d), lambda i: (i, 0)),
        out_shape=jax.ShapeDtypeStruct((t, d), bf16),
        compiler_params=_cparams(),
    )(proj, ng, nb, ws, bs, dep)


def sgu_bwd(da, proj, ng, nb, ws, bs, dep):
    _, t, w2 = proj.shape
    d = w2 // 2
    gd = d // SGU_GROUPS
    tm = _row_tile(t)

    def body(da_ref, p_ref, ng_ref, nb_ref, ws_ref, bs_ref, dep_ref,
             dua_ref, dva_ref, dws_ref, dbs_ref, dng_ref, dnb_ref, dvn_scr):
        i = pl.program_id(0)
        ua = p_ref[0, :, 0:d].astype(f32)
        va = p_ref[0, :, d:w2].astype(f32)
        gu = _gelu(ua)
        xh, rstd, vn = _sgu_norm(va, ng_ref[...], nb_ref[...])
        dad = da_ref[...].astype(f32)
        dsb = (dad * gu).astype(bf16)
        for c in range(tm // SGU_CHUNK):
            rows = slice(c * SGU_CHUNK, (c + 1) * SGU_CHUNK)
            for g in range(SGU_GROUPS):
                cols = slice(g * gd, (g + 1) * gd)
                sg = _dot(ws_ref[g], vn[rows, cols]) + bs_ref[g]
                dua_ref[rows, cols] = (dad[rows, cols] * sg * _dgelu(ua[rows, cols])).astype(bf16)
                ds = dsb[rows, cols]
                dvn_scr[rows, cols] = _dot_tn(ws_ref[g], ds)
                dw = _dot_nt(ds, vn[rows, cols])
                db = jnp.sum(ds.astype(f32), axis=1, keepdims=True)
                if c == 0:
                    _acc_out(dws_ref.at[g], i == 0, dw)
                    _acc_out(dbs_ref.at[g], i == 0, db)
                else:
                    dws_ref[g] += dw
                    dbs_ref[g] += db
        dvn = dvn_scr[...]
        _acc_out(dng_ref, i == 0, jnp.sum(dvn * xh, axis=0, keepdims=True))
        _acc_out(dnb_ref, i == 0, jnp.sum(dvn, axis=0, keepdims=True))
        dxh = dvn * ng_ref[...]
        dgv = rstd * (dxh - jnp.mean(dxh, axis=-1, keepdims=True) - xh * jnp.mean(dxh * xh, axis=-1, keepdims=True))
        dva_ref[...] = (dgv * _dgelu(va)).astype(bf16)

    row = pl.BlockSpec((tm, d), lambda i: (i, 0))
    vec = pl.BlockSpec((1, d), lambda i: (0, 0))
    wsp = pl.BlockSpec((SGU_GROUPS, SGU_CHUNK, SGU_CHUNK), lambda i: (0, 0, 0))
    bsp = pl.BlockSpec((SGU_GROUPS, SGU_CHUNK, 1), lambda i: (0, 0, 0))
    return pl.pallas_call(
        body, name="sgu_bwd", grid=(t // tm,),
        in_specs=[row, pl.BlockSpec((1, tm, w2), lambda i: (0, i, 0)), vec, vec, wsp, bsp, _ANY],
        out_specs=[row, row, wsp, bsp, vec, vec],
        out_shape=[jax.ShapeDtypeStruct((t, d), bf16), jax.ShapeDtypeStruct((t, d), bf16),
                   jax.ShapeDtypeStruct((SGU_GROUPS, SGU_CHUNK, SGU_CHUNK), f32), jax.ShapeDtypeStruct((SGU_GROUPS, SGU_CHUNK, 1), f32),
                   jax.ShapeDtypeStruct((1, d), f32), jax.ShapeDtypeStruct((1, d), f32)],
        scratch_shapes=[pltpu.VMEM((tm, d), f32)],
        compiler_params=_cparams(),
    )(da, proj, ng, nb, ws, bs, dep)


def retention_constants(decay_logit, t, dk, zero):
    lg = jax.nn.log_sigmoid(decay_logit.astype(f32) + zero)
    lgf = lg[0][:, None]
    lgb = lg[1][:, None]
    idx = jnp.arange(CHUNK, dtype=f32)[None, :]
    af = jnp.exp((idx + 1.0) * lgf)
    ab = jnp.exp((CHUNK - idx) * lgb)
    kf = jnp.exp((CHUNK - 1.0 - idx) * lgf)
    kb = jnp.exp(idx * lgb)
    cols = jnp.stack([af, ab, kf, kb, af * (idx + 1.0), ab * (CHUNK - idx), kf * (CHUNK - 1.0 - idx), kb * idx], axis=1)
    cols = cols[..., None]
    diff = idx[0][:, None] - idx[0][None, :]
    dfm = jnp.where(diff >= 0, jnp.exp(jnp.maximum(diff, 0.0)[None] * lgf[:, :, None]), 0.0)
    dbm = jnp.where(diff < 0, jnp.exp(jnp.maximum(-diff, 0.0)[None] * lgb[:, :, None]), 0.0)
    mats = jnp.stack([dfm + dbm, dfm * diff[None], dbm * (-diff)[None]], axis=1)
    cdec = jnp.stack([jnp.broadcast_to(jnp.exp(CHUNK * lgf), (RET_HEADS, dk)),
                      jnp.broadcast_to(jnp.exp(CHUNK * lgb), (RET_HEADS, dk))], axis=1)
    theta = ROPE_BASE ** (-jnp.arange(0, dk, 2, dtype=f32) / dk)
    ang = (jnp.arange(t, dtype=f32) + zero)[:, None] * theta[None, :]
    return cols, mats, cdec, jnp.cos(ang), jnp.sin(ang)


def _rot(tr, cos, sin):
    half = tr.shape[-1] // 2
    t1 = tr[:, :half]
    t2 = tr[:, half:]
    return jnp.concatenate([t1 * cos - t2 * sin, t2 * cos + t1 * sin], axis=-1)


def _rot_inv(dt, cos, sin):
    half = dt.shape[-1] // 2
    d1 = dt[:, :half]
    d2 = dt[:, half:]
    return jnp.concatenate([d1 * cos + d2 * sin, d2 * cos - d1 * sin], axis=-1)


def _ret_specs(t, d, dk, rt):
    nr = t // rt
    hq = d // dk

    def blk(p, n):
        return (1 - p) * (nr - 1 - n) + p * n

    q_spec = pl.BlockSpec((1, rt, dk), lambda h, p, n: (1, blk(p, n), h))
    k_spec = pl.BlockSpec((1, rt, dk), lambda h, p, n: (1, blk(p, n), hq + h))
    v_spec = pl.BlockSpec((1, rt, dk), lambda h, p, n: (2, blk(p, n), h))
    g_spec = pl.BlockSpec((1, rt, dk), lambda h, p, n: (2, blk(p, n), hq + h))
    tab_spec = pl.BlockSpec((rt, dk // 2), lambda h, p, n: (blk(p, n), 0))
    cols_spec = pl.BlockSpec((1, 8, CHUNK, 1), lambda h, p, n: (h, 0, 0, 0))
    mats_spec = pl.BlockSpec((1, 3, CHUNK, CHUNK), lambda h, p, n: (h, 0, 0, 0))
    cdec_spec = pl.BlockSpec((1, 2, dk), lambda h, p, n: (h, 0, 0))
    in_row = pl.BlockSpec((rt, dk), lambda h, p, n: (blk(p, n), h))
    out_row = pl.BlockSpec((rt, dk), lambda h, p, n: (p * n, h))
    return nr, blk, q_spec, k_spec, v_spec, g_spec, tab_spec, cols_spec, mats_spec, cdec_spec, in_row, out_row


def ret_fwd(proj, cols, mats, cdec, dep):
    _, t, w2 = proj.shape
    d = w2 // 2
    dk = d // RET_HEADS
    rt = _row_tile(t)
    cpt = rt // CHUNK
    nr, blk, q_spec, k_spec, v_spec, g_spec, _, cols_spec, mats_spec, cdec_spec, _, out_row = _ret_specs(t, d, dk, rt)

    def body(q_ref, k_ref, v_ref, g_ref, cols_ref, mats_ref, cdec_ref, dep_ref, r_ref, rn_ref, sb_scr, st):
        p = pl.program_id(1)
        n = pl.program_id(2)
        af, ab, kf, kb = cols_ref[0, 0], cols_ref[0, 1], cols_ref[0, 2], cols_ref[0, 3]
        cf = cdec_ref[0, 0:1, :]
        cb = cdec_ref[0, 1:2, :]

        @pl.when(n == 0)
        def _():
            st[...] = jnp.zeros_like(st)

        @pl.when(p == 0)
        def _():
            for j in reversed(range(cpt)):
                rows = slice(j * CHUNK, (j + 1) * CHUNK)
                ch = blk(p, n) * cpt + j
                kk = k_ref[0, rows, :].astype(f32)
                sb_scr[ch] = st[...].astype(bf16)
                st[...] = st[...] * cb + _dot_tn((kk * kb).astype(bf16), v_ref[0, rows, :])

        @pl.when(p == 1)
        def _():
            for j in range(cpt):
                rows = slice(j * CHUNK, (j + 1) * CHUNK)
                ch = blk(p, n) * cpt + j
                qb = q_ref[0, rows, :]
                kkb = k_ref[0, rows, :]
                q = qb.astype(f32)
                kk = kkb.astype(f32)
                v = v_ref[0, rows, :]
                pm = (_dot_nt(qb, kkb) * mats_ref[0, 0]).astype(bf16)
                out = (_dot(pm, v) + _dot((q * af).astype(bf16), st[...].astype(bf16))
                       + _dot((q * ab).astype(bf16), sb_scr[ch]))
                st[...] = st[...] * cf + _dot_tn((kk * kf).astype(bf16), v)
                rhat = out * lax.rsqrt(jnp.mean(out * out, axis=-1, keepdims=True) + NORM_EPS)
                gg = g_ref[0, rows, :].astype(f32)
                r_ref[rows, :] = out.astype(bf16)
                rn_ref[rows, :] = (rhat * gg * _sigmoid(gg)).astype(bf16)

    return pl.pallas_call(
        body, name="ret_fwd", grid=(RET_HEADS, 2, nr),
        in_specs=[q_spec, k_spec, v_spec, g_spec, cols_spec, mats_spec, cdec_spec, _ANY],
        out_specs=[out_row, out_row],
        out_shape=[jax.ShapeDtypeStruct((t, d), bf16), jax.ShapeDtypeStruct((t, d), bf16)],
        scratch_shapes=[pltpu.VMEM((t // CHUNK, dk, dk), bf16), pltpu.VMEM((dk, dk), f32)],
        compiler_params=_cparams(),
    )(proj, proj, proj, proj, cols, mats, cdec, dep)


def ret_bwd(drn, r, proj, cols, mats, cdec, cos, sin):
    _, t, w2 = proj.shape
    d = w2 // 2
    dk = d // RET_HEADS
    rt = _row_tile(t)
    cpt = rt // CHUNK
    nr, blk, q_spec, k_spec, v_spec, g_spec, tab_spec, cols_spec, mats_spec, cdec_spec, in_row, out_row = _ret_specs(t, d, dk, rt)
    scale = dk ** -0.5

    def body(drn_ref, r_ref, q_ref, k_ref, v_ref, g_ref, cos_ref, sin_ref, cols_ref, mats_ref, cdec_ref,
             dq_ref, dk_ref, dv_ref, dg_ref, dlg_ref,
             sb_scr, gf_scr, st_s, st_g, acc_af, acc_ab, acc_vf, acc_vb, acc_sf, acc_sb, dout_scr, dgr_scr):
        p = pl.program_id(1)
        n = pl.program_id(2)
        af, ab, kf, kb = cols_ref[0, 0], cols_ref[0, 1], cols_ref[0, 2], cols_ref[0, 3]
        af1, ab1, kf1, kb1 = cols_ref[0, 4], cols_ref[0, 5], cols_ref[0, 6], cols_ref[0, 7]
        cf = cdec_ref[0, 0:1, :]
        cb = cdec_ref[0, 1:2, :]

        @pl.when(n == 0)
        def _():
            st_s[...] = jnp.zeros_like(st_s)
            st_g[...] = jnp.zeros_like(st_g)

        @pl.when(jnp.logical_and(n == 0, p == 1))
        def _():
            for a in (acc_af, acc_ab, acc_vf, acc_vb, acc_sf, acc_sb):
                a[...] = jnp.zeros_like(a)

        def load(rows):
            cs, sn = cos_ref[rows, :], sin_ref[rows, :]
            q = q_ref[0, rows, :].astype(f32)
            kk = k_ref[0, rows, :].astype(f32)
            rr = r_ref[rows, :].astype(f32)
            rstd = lax.rsqrt(jnp.mean(rr * rr, axis=-1, keepdims=True) + NORM_EPS)
            rhat = rr * rstd
            gg = g_ref[0, rows, :].astype(f32)
            sg = _sigmoid(gg)
            dd = drn_ref[rows, :].astype(f32)
            drhat = dd * gg * sg
            dout = rstd * (drhat - rhat * jnp.mean(drhat * rhat, axis=-1, keepdims=True))
            dgr = dd * rhat * _dsilu(gg, sg)
            return q, kk, dout.astype(bf16), dgr, cs, sn

        @pl.when(p == 0)
        def _():
            for j in reversed(range(cpt)):
                rows = slice(j * CHUNK, (j + 1) * CHUNK)
                ch = blk(p, n) * cpt + j
                q, kk, doutb, dgr, _, _ = load(rows)
                kept = pl.ds(pl.multiple_of(ch * CHUNK, CHUNK), CHUNK)
                dout_scr[kept, :] = doutb
                dgr_scr[kept, :] = dgr.astype(bf16)
                sb_scr[ch] = st_s[...].astype(bf16)
                gf_scr[ch] = st_g[...].astype(bf16)
                st_s[...] = st_s[...] * cb + _dot_tn((kk * kb).astype(bf16), v_ref[0, rows, :])
                st_g[...] = st_g[...] * cf + _dot_tn((q * af).astype(bf16), doutb)

        @pl.when(p == 1)
        def _():
            for j in range(cpt):
                rows = slice(j * CHUNK, (j + 1) * CHUNK)
                ch = blk(p, n) * cpt + j
                kept = pl.ds(pl.multiple_of(ch * CHUNK, CHUNK), CHUNK)
                doutb = dout_scr[kept, :]
                cs, sn = cos_ref[rows, :], sin_ref[rows, :]
                v = v_ref[0, rows, :]
                qb = q_ref[0, rows, :]
                kkb = k_ref[0, rows, :]
                q = qb.astype(f32)
                kk = kkb.astype(f32)
                sf = st_s[...]
                gb = st_g[...]
                sfb = sf.astype(bf16)
                gbb = gb.astype(bf16)
                sbb = sb_scr[ch]
                gfb = gf_scr[ch]
                dmat = mats_ref[0, 0]
                scores = _dot_nt(qb, kkb)
                dpraw = _dot_nt(doutb, v)
                dpb = (dpraw * dmat).astype(bf16)
                pmb = (scores * dmat).astype(bf16)
                x1 = _dot_nt(doutb, sfb)
                x2 = _dot_nt(doutb, sbb)
                y1 = _dot_nt(v, gfb)
                y2 = _dot_nt(v, gbb)
                kdf = (kk * kf).astype(bf16)
                kdb = (kk * kb).astype(bf16)
                dq = _dot(dpb, kkb) + x1 * af + x2 * ab
                dkk = _dot_tn(dpb, qb) + y1 * kf + y2 * kb
                dv = _dot_tn(pmb, doutb) + _dot(kdf, gfb) + _dot(kdb, gbb)
                ps = dpraw * scores
                acc_af[...] += ps * mats_ref[0, 1]
                acc_ab[...] += ps * mats_ref[0, 2]
                acc_vf[...] += x1 * q * af1 + y1 * kk * kf1
                acc_vb[...] += x2 * q * ab1 + y2 * kk * kb1
                acc_sf[...] += gfb.astype(f32) * sf
                acc_sb[...] += gb * sbb.astype(f32)
                st_s[...] = sf * cf + _dot_tn(kdf, v)
                st_g[...] = gb * cb + _dot_tn((q * ab).astype(bf16), doutb)
                dq_ref[rows, :] = _rot_inv(dq, cs, sn).astype(bf16)
                dk_ref[rows, :] = (_rot_inv(dkk, cs, sn) * scale).astype(bf16)
                dv_ref[rows, :] = dv.astype(bf16)
                dg_ref[rows, :] = dgr_scr[kept, :]

        @pl.when(jnp.logical_and(p == 1, n == nr - 1))
        def _():
            tf = jnp.sum(acc_af[...]) + jnp.sum(acc_vf[...]) + CHUNK * jnp.sum(acc_sf[...] * cf)
            tb = jnp.sum(acc_ab[...]) + jnp.sum(acc_vb[...]) + CHUNK * jnp.sum(acc_sb[...] * cb)
            rid = lax.broadcasted_iota(jnp.int32, (8, 128), 0)
            dlg_ref[0] = jnp.where(rid == 0, tf, jnp.where(rid == 1, tb, 0.0))

    nch = t // CHUNK
    return pl.pallas_call(
        body, name="ret_bwd", grid=(RET_HEADS, 2, nr),
        in_specs=[in_row, in_row, q_spec, k_spec, v_spec, g_spec, tab_spec, tab_spec, cols_spec, mats_spec, cdec_spec],
        out_specs=[out_row, out_row, out_row, out_row, pl.BlockSpec((1, 8, 128), lambda h, p, n: (h, 0, 0))],
        out_shape=[jax.ShapeDtypeStruct((t, d), bf16)] * 4 + [jax.ShapeDtypeStruct((RET_HEADS, 8, 128), f32)],
        scratch_shapes=[pltpu.VMEM((nch, dk, dk), bf16), pltpu.VMEM((nch, dk, dk), bf16),
                        pltpu.VMEM((dk, dk), f32), pltpu.VMEM((dk, dk), f32),
                        pltpu.VMEM((CHUNK, CHUNK), f32), pltpu.VMEM((CHUNK, CHUNK), f32),
                        pltpu.VMEM((CHUNK, dk), f32), pltpu.VMEM((CHUNK, dk), f32),
                        pltpu.VMEM((dk, dk), f32), pltpu.VMEM((dk, dk), f32),
                        pltpu.VMEM((t, dk), bf16), pltpu.VMEM((t, dk), bf16)],
        compiler_params=_cparams(),
    )(drn, r, proj, proj, proj, proj, cos, sin, cols, mats, cdec)


def mix_fwd(a, rn, proj, wa, wb, wo, x1):
    t, d = x1.shape
    tm = _row_tile(t)

    def body(a_ref, rn_ref, p_ref, wa_ref, wb_ref, wo_ref, x_ref, xo_ref, ba_ref, br_ref):
        ba = _dot(a_ref[...], wa_ref[...])
        br = _dot(rn_ref[...], wb_ref[...])
        sa = _sigmoid(p_ref[0, :, 0:d].astype(f32))
        sb = _sigmoid(p_ref[0, :, d:2 * d].astype(f32))
        mix = (sa * ba + sb * br).astype(bf16)
        xo_ref[...] = x_ref[...] + _dot(mix, wo_ref[...])
        ba_ref[...] = ba.astype(bf16)
        br_ref[...] = br.astype(bf16)

    row = pl.BlockSpec((tm, d), lambda i: (i, 0))
    wsp = pl.BlockSpec((d, d), lambda i: (0, 0))
    return pl.pallas_call(
        body, name="mix_fwd", grid=(t // tm,),
        in_specs=[row, row, pl.BlockSpec((1, tm, 2 * d), lambda i: (3, i, 0)), wsp, wsp, wsp, row],
        out_specs=[row, row, row],
        out_shape=[jax.ShapeDtypeStruct((t, d), f32), jax.ShapeDtypeStruct((t, d), bf16), jax.ShapeDtypeStruct((t, d), bf16)],
        compiler_params=_cparams(),
    )(a, rn, proj, wa, wb, wo, x1)


def mix_bwd_act(dx2, ba, br, proj, wa, wb, wo, dep):
    t, d = dx2.shape
    tm = _row_tile(t)

    def body(dx_ref, ba_ref, br_ref, p_ref, wa_ref, wb_ref, wo_ref, dep_ref,
             da_ref, drn_ref, dga_ref, dgb_ref, mix_ref, dba_ref, dbr_ref, dxb_ref):
        dxb = dx_ref[...].astype(bf16)
        dxb_ref[...] = dxb
        dmix = _dot_nt(dxb, wo_ref[...])
        ba = ba_ref[...].astype(f32)
        br = br_ref[...].astype(f32)
        sa = _sigmoid(p_ref[0, :, 0:d].astype(f32))
        sb = _sigmoid(p_ref[0, :, d:2 * d].astype(f32))
        mix_ref[...] = (sa * ba + sb * br).astype(bf16)
        dba = (dmix * sa).astype(bf16)
        dbr = (dmix * sb).astype(bf16)
        dba_ref[...] = dba
        dbr_ref[...] = dbr
        dga_ref[...] = (dmix * ba * sa * (1.0 - sa)).astype(bf16)
        dgb_ref[...] = (dmix * br * sb * (1.0 - sb)).astype(bf16)
        da_ref[...] = _dot_nt(dba, wa_ref[...]).astype(bf16)
        drn_ref[...] = _dot_nt(dbr, wb_ref[...]).astype(bf16)

    row = pl.BlockSpec((tm, d), lambda i: (i, 0))
    wsp = pl.BlockSpec((d, d), lambda i: (0, 0))
    return pl.pallas_call(
        body, name="mix_bwd_act", grid=(t // tm,),
        in_specs=[row, row, row, pl.BlockSpec((1, tm, 2 * d), lambda i: (3, i, 0)), wsp, wsp, wsp, _ANY],
        out_specs=[row] * 8,
        out_shape=[jax.ShapeDtypeStruct((t, d), bf16)] * 8,
        compiler_params=_cparams(),
    )(dx2, ba, br, proj, wa, wb, wo, dep)


def inproj_bwd_act(segs, win, x1, ng, dx2):
    t, d = x1.shape
    s4 = win.shape[0]
    tm = _row_tile(t)
    nseg = len(segs)

    def body(*refs):
        seg_refs = refs[:nseg]
        w_ref, x_ref, ng_ref, dx2_ref, dx1_ref, db_ref, dng_ref = refs[nseg:]
        i = pl.program_id(0)
        dh = None
        for e, sr in enumerate(seg_refs):
            sb = sr[...]
            part = _dot_nt(sb, w_ref[e // 2, :, (e % 2) * d:(e % 2 + 1) * d])
            dh = part if dh is None else dh + part
            _acc_out(db_ref.at[e], i == 0, jnp.sum(sb.astype(f32), axis=0, keepdims=True))
        _, xh, r = _rms(x_ref[...], ng_ref[...])
        dx1_ref[...] = dx2_ref[...] + _rms_bwd(dh, xh, r, ng_ref[...])
        _acc_out(dng_ref, i == 0, jnp.sum(dh * xh, axis=0, keepdims=True))

    row = pl.BlockSpec((tm, d), lambda i: (i, 0))
    vec = pl.BlockSpec((1, d), lambda i: (0, 0))
    return pl.pallas_call(
        body, name="inproj_bwd_act", grid=(t // tm,),
        in_specs=[row] * nseg + [pl.BlockSpec((s4, d, 2 * d), lambda i: (0, 0, 0), pipeline_mode=pl.Buffered(1)),
                                 row, vec, row],
        out_specs=[row, pl.BlockSpec((nseg, 1, d), lambda i: (0, 0, 0)), vec],
        out_shape=[jax.ShapeDtypeStruct((t, d), f32), jax.ShapeDtypeStruct((nseg, 1, d), f32),
                   jax.ShapeDtypeStruct((1, d), f32)],
        compiler_params=_cparams(VMEM_LIMIT_WIDE),
    )(*segs, win, x1, ng, dx2)


def _place():
    return lax.axis_index("x"), lax.axis_index("y"), lax.axis_index("c")


def _other_chips(x, y):
    return [(1 - x, y), (x, 1 - y), (1 - x, 1 - y)]


_ANY = pl.BlockSpec(memory_space=pl.ANY)


_HBM = pl.BlockSpec(memory_space=pltpu.HBM)
_SEM = pl.BlockSpec(memory_space=pltpu.SEMAPHORE)
_EFFECT = pltpu.SideEffectType.DATAFLOW_SIDE_EFFECTING


def _hbm(a):
    return pltpu.with_memory_space_constraint(a, pltpu.HBM)


def _half_rows(ref, c):
    half = ref.shape[1] // 2
    return pl.ds(pl.multiple_of(c * half, 16), half)


def _chip_copy(src, dst, send_sem, recv_sem, chip, c):
    return pltpu.make_async_remote_copy(src_ref=src, dst_ref=dst, send_sem=send_sem, recv_sem=recv_sem,
                                        device_id=(chip[0], chip[1], c), device_id_type=MESH)


def _gather_targets(c, both_cores):
    return [(c, 0), (1 - c, 1)] if both_cores else [(c, 0)]


def gather_start(bufs, groups, name, both_cores=()):
    nb, ng = len(bufs), len(groups)

    def body(*refs):
        ins = refs[:nb]
        sems = refs[nb:nb + 2 * ng]
        token = refs[-1]
        x, y, c = _place()
        k = 2 * x + y
        for gi, grp in enumerate(groups):
            targets = _gather_targets(c, gi in both_cores)
            for wi, w in enumerate(grp):
                mine = ins[w].at[k, _half_rows(ins[w], c)]
                for j, chip in enumerate(_other_chips(x, y)):
                    for core, off in targets:
                        si = len(targets) * (3 * wi + j) + off
                        _chip_copy(mine, mine, sems[2 * gi].at[si], sems[2 * gi + 1].at[si], chip, core).start()
        token[...] = jnp.zeros_like(token)

    sem_shapes = []
    for gi, grp in enumerate(groups):
        n_sems = 3 * len(grp) * (2 if gi in both_cores else 1)
        sem_shapes += [pltpu.SemaphoreType.DMA((n_sems,)), pltpu.SemaphoreType.DMA((n_sems,))]
    outs = pl.pallas_call(
        body, name=name,
        out_shape=sem_shapes + [pltpu.HBM(b.shape, b.dtype) for b in bufs] + [jax.ShapeDtypeStruct((8, 128), f32)],
        in_specs=[_HBM] * nb,
        out_specs=[_SEM] * (2 * ng) + [_HBM] * nb + [pl.BlockSpec(memory_space=pltpu.VMEM)],
        input_output_aliases={w: 2 * ng + w for w in range(nb)},
        compiler_params=pltpu.CompilerParams(has_side_effects=_EFFECT),
    )(*[_hbm(b) for b in bufs])
    sems = [(outs[2 * gi], outs[2 * gi + 1]) for gi in range(ng)]
    return sems, list(outs[2 * ng:2 * ng + nb]), outs[-1]


def gather_wait(bufs, sems, after, name, both_cores=False):
    n = len(bufs)

    def body(*refs):
        ins = refs[:n]
        send_sems, recv_sems = refs[n], refs[n + 1]
        x, y, c = _place()
        k = 2 * x + y
        peers = _gather_targets(c, both_cores)
        for wi in range(n):
            mine = ins[wi].at[k, _half_rows(ins[wi], c)]
            for j, chip in enumerate(_other_chips(x, y)):
                for core, off in peers:
                    si = len(peers) * (3 * wi + j) + off
                    theirs = ins[wi].at[2 * chip[0] + chip[1], _half_rows(ins[wi], core)]
                    cp = _chip_copy(mine, theirs, send_sems.at[si], recv_sems.at[si], chip, core)
                    cp.wait_send()
                    cp.wait_recv()

    outs = pl.pallas_call(
        body, name=name,
        out_shape=[pltpu.HBM(b.shape, b.dtype) for b in bufs],
        in_specs=[_HBM] * n + [_SEM, _SEM, _ANY],
        out_specs=[_HBM] * n,
        input_output_aliases={i: i for i in range(n)},
        compiler_params=pltpu.CompilerParams(has_side_effects=_EFFECT),
    )(*bufs, sems[0], sems[1], after)
    return list(outs)


def gather_forward(bufs, name):
    n = len(bufs)

    def body(*refs):
        ins = refs[n:2 * n]
        send_sems, recv_sems = refs[2 * n], refs[2 * n + 1]
        x, y, c = _place()
        copies = []
        for wi in range(n):
            for j, chip in enumerate(_other_chips(x, y)):
                kp = 2 * chip[0] + chip[1]
                got = ins[wi].at[kp, _half_rows(ins[wi], c)]
                cp = pltpu.make_async_remote_copy(
                    src_ref=got, dst_ref=got, send_sem=send_sems.at[3 * wi + j], recv_sem=recv_sems.at[3 * wi + j],
                    device_id=(x, y, 1 - c), device_id_type=MESH)
                cp.start()
                copies.append((cp, wi, kp, j))
        for cp, wi, kp, j in copies:
            cp.wait_send()
            theirs = ins[wi].at[kp, _half_rows(ins[wi], 1 - c)]
            pltpu.make_async_remote_copy(
                src_ref=theirs, dst_ref=theirs, send_sem=send_sems.at[3 * wi + j], recv_sem=recv_sems.at[3 * wi + j],
                device_id=(x, y, 1 - c), device_id_type=MESH).wait_recv()

    outs = pl.pallas_call(
        body, name=name,
        out_shape=[jax.ShapeDtypeStruct(b.shape, b.dtype) for b in bufs],
        in_specs=[_ANY] * n, out_specs=[_ANY] * n,
        input_output_aliases={i: i for i in range(n)},
        scratch_shapes=[pltpu.SemaphoreType.DMA((3 * n,)), pltpu.SemaphoreType.DMA((3 * n,))],
    )(*bufs)
    return list(outs)


def forward_start(bufs, name):
    n = len(bufs)

    def body(*refs):
        x, y, c = _place()
        for wi in range(n):
            for j, chip in enumerate(_other_chips(x, y)):
                got = refs[wi].at[2 * chip[0] + chip[1], _half_rows(refs[wi], c)]
                _sibling_copy(got, got, refs[n].at[3 * wi + j], refs[n + 1].at[3 * wi + j]).start()
        refs[-1][...] = jnp.zeros_like(refs[-1])

    return _split_start(body, name, 3 * n, list(bufs))


def forward_wait(bufs, sems, after, name):
    n = len(bufs)

    def body(*refs):
        x, y, c = _place()
        for wi in range(n):
            for j, chip in enumerate(_other_chips(x, y)):
                kp = 2 * chip[0] + chip[1]
                got = refs[wi].at[kp, _half_rows(refs[wi], c)]
                theirs = refs[wi].at[kp, _half_rows(refs[wi], 1 - c)]
                _sibling_copy(got, got, refs[n].at[3 * wi + j], refs[n + 1].at[3 * wi + j]).wait_send()
                _sibling_copy(theirs, theirs, refs[n].at[3 * wi + j], refs[n + 1].at[3 * wi + j]).wait_recv()

    return _split_wait(body, name, list(bufs), sems, after)


def exchange_start(grads, name):
    n = len(grads)
    lands = [lax.empty((3,) + g.shape[1:], g.dtype) for g in grads]

    def body(*refs):
        ins = refs[:n]
        land = refs[n:2 * n]
        send_sems, recv_sems = refs[2 * n], refs[2 * n + 1]
        token = refs[-1]
        x, y, c = _place()
        for wi in range(n):
            for j, chip in enumerate(_other_chips(x, y)):
                _chip_copy(ins[wi].at[2 * chip[0] + chip[1]], land[wi].at[j], send_sems.at[3 * wi + j],
                           recv_sems.at[3 * wi + j], chip, c).start()
        token[...] = jnp.zeros_like(token)

    outs = pl.pallas_call(
        body, name=name,
        out_shape=[pltpu.SemaphoreType.DMA((3 * n,)), pltpu.SemaphoreType.DMA((3 * n,))]
        + [pltpu.HBM(g.shape, g.dtype) for g in grads] + [pltpu.HBM(l.shape, l.dtype) for l in lands]
        + [jax.ShapeDtypeStruct((8, 128), f32)],
        in_specs=[_HBM] * (2 * n),
        out_specs=[_SEM, _SEM] + [_HBM] * (2 * n) + [pl.BlockSpec(memory_space=pltpu.VMEM)],
        input_output_aliases={i: 2 + i for i in range(2 * n)},
        compiler_params=pltpu.CompilerParams(has_side_effects=_EFFECT),
    )(*[_hbm(g) for g in grads], *[_hbm(l) for l in lands])
    return (outs[0], outs[1]), list(outs[2:2 + n]), list(outs[2 + n:2 + 2 * n]), outs[-1]


def exchange_wait(grads, lands, sems, after, name):
    n = len(grads)

    def body(*refs):
        ins = refs[:n]
        land = refs[n:2 * n]
        send_sems, recv_sems = refs[2 * n], refs[2 * n + 1]
        x, y, c = _place()
        for wi in range(n):
            for j, chip in enumerate(_other_chips(x, y)):
                cp = _chip_copy(ins[wi].at[2 * chip[0] + chip[1]], land[wi].at[j], send_sems.at[3 * wi + j],
                                recv_sems.at[3 * wi + j], chip, c)
                cp.wait_send()
                cp.wait_recv()

    outs = pl.pallas_call(
        body, name=name,
        out_shape=[pltpu.HBM(g.shape, g.dtype) for g in grads] + [pltpu.HBM(l.shape, l.dtype) for l in lands],
        in_specs=[_HBM] * (2 * n) + [_SEM, _SEM, _ANY],
        out_specs=[_HBM] * (2 * n),
        input_output_aliases={i: i for i in range(2 * n)},
        compiler_params=pltpu.CompilerParams(has_side_effects=_EFFECT),
    )(*grads, *lands, sems[0], sems[1], after)
    return list(outs[:n]), list(outs[n:])


def _split_start(body, name, n_sems, operands):
    n = len(operands)
    outs = pl.pallas_call(
        body, name=name,
        out_shape=[pltpu.SemaphoreType.DMA((n_sems,)), pltpu.SemaphoreType.DMA((n_sems,))]
        + [pltpu.HBM(o.shape, o.dtype) for o in operands] + [jax.ShapeDtypeStruct((8, 128), f32)],
        in_specs=[_HBM] * n,
        out_specs=[_SEM, _SEM] + [_HBM] * n + [pl.BlockSpec(memory_space=pltpu.VMEM)],
        input_output_aliases={i: 2 + i for i in range(n)},
        compiler_params=pltpu.CompilerParams(has_side_effects=_EFFECT),
    )(*[_hbm(o) for o in operands])
    return (outs[0], outs[1]), list(outs[2:2 + n]), outs[-1]


def _split_wait(body, name, operands, sems, after):
    n = len(operands)
    outs = pl.pallas_call(
        body, name=name,
        out_shape=[pltpu.HBM(o.shape, o.dtype) for o in operands],
        in_specs=[_HBM] * n + [_SEM, _SEM, _ANY],
        out_specs=[_HBM] * n,
        input_output_aliases={i: i for i in range(n)},
        compiler_params=pltpu.CompilerParams(has_side_effects=_EFFECT),
    )(*operands, sems[0], sems[1], after)
    return list(outs)


def _sibling_copy(src, dst, send_sem, recv_sem):
    x, y, c = _place()
    return pltpu.make_async_remote_copy(src_ref=src, dst_ref=dst, send_sem=send_sem, recv_sem=recv_sem,
                                        device_id=(x, y, 1 - c), device_id_type=MESH)


def swap_start(parts, name):
    n = len(parts)

    def body(*refs):
        for w in range(n):
            _sibling_copy(refs[w], refs[n + w], refs[2 * n].at[w], refs[2 * n + 1].at[w]).start()
        refs[-1][...] = jnp.zeros_like(refs[-1])

    sems, ops, token = _split_start(body, name, n, list(parts) + [lax.empty(p.shape, p.dtype) for p in parts])
    return sems, ops[:n], ops[n:], token


def swap_wait(parts, lands, sems, after, name):
    n = len(parts)

    def body(*refs):
        for w in range(n):
            cp = _sibling_copy(refs[w], refs[n + w], refs[2 * n].at[w], refs[2 * n + 1].at[w])
            cp.wait_send()
            cp.wait_recv()

    outs = _split_wait(body, name, list(parts) + list(lands), sems, after)
    return outs[:n], outs[n:]


def _all_peers(x, y, c):
    return [(1 - x if m & 4 else x, 1 - y if m & 2 else y, 1 - c if m & 1 else c) for m in range(1, N_DEV)]


def small_start(block):
    land = jnp.broadcast_to(block[None], (N_DEV,) + block.shape)

    def body(b_ref, land_ref, send_sems, recv_sems, b_thru, land_thru, token):
        x, y, c = _place()
        me = 4 * x + 2 * y + c
        for m, peer in enumerate(_all_peers(x, y, c)):
            pltpu.make_async_remote_copy(src_ref=b_ref, dst_ref=land_ref.at[me], send_sem=send_sems.at[m],
                                         recv_sem=recv_sems.at[m], device_id=peer, device_id_type=MESH).start()
        token[...] = jnp.zeros_like(token)

    sems, ops, token = _split_start(body, "small_start", N_DEV - 1, [block, land])
    return sems, ops[0], ops[1], token


def small_wait(block, land, sems, after):
    def body(b_ref, land_ref, send_sems, recv_sems, after_ref, b_thru, land_thru):
        x, y, c = _place()
        for m, (px, py, pc) in enumerate(_all_peers(x, y, c)):
            cp = pltpu.make_async_remote_copy(src_ref=b_ref, dst_ref=land_ref.at[4 * px + 2 * py + pc],
                                              send_sem=send_sems.at[m], recv_sem=recv_sems.at[m],
                                              device_id=(px, py, pc), device_id_type=MESH)
            cp.wait_send()
            cp.wait_recv()

    return _split_wait(body, "small_wait", [block, land], sems, after)[1]


def _adamw(w, g, m, v):
    m = ADAM_B1 * m + (1.0 - ADAM_B1) * g
    v = ADAM_B2 * v + (1.0 - ADAM_B2) * (g * g)
    m_hat = m / (1.0 - ADAM_B1 ** ADAM_STEP)
    v_hat = v / (1.0 - ADAM_B2 ** ADAM_STEP)
    delta = -ADAM_LR * (m_hat / (jnp.sqrt(v_hat) + ADAM_EPS) + ADAM_WD * w)
    return delta, m, v


EW_BLOCK_BYTES = 2 * 1024 * 1024


def _ew_tile(rows, cols):
    for cand in (512, 352, 256, 176, 128, 64, 32, 16, 8):
        if rows % cand == 0 and cand * cols * 4 <= EW_BLOCK_BYTES:
            return cand
    return rows


def sum_partials(chip, own, land, name):
    _, r, c = own.shape
    tr = _ew_tile(r, c)

    def body(k_ref, own_ref, p_ref, o_ref):
        o_ref[...] = ((own_ref[0].astype(f32) + p_ref[0].astype(f32)) + p_ref[1].astype(f32)) + p_ref[2].astype(f32)

    return pl.pallas_call(
        body, name=name,
        grid_spec=pltpu.PrefetchScalarGridSpec(
            num_scalar_prefetch=1, grid=(r // tr,),
            in_specs=[pl.BlockSpec((1, tr, c), lambda i, k: (k[0], i, 0)), pl.BlockSpec((3, tr, c), lambda i, k: (0, i, 0))],
            out_specs=pl.BlockSpec((tr, c), lambda i, k: (i, 0))),
        out_shape=jax.ShapeDtypeStruct((r, c), f32),
        compiler_params=_cparams(),
    )(chip, own, land)


def adamw_shard(p_mine, p_sibling, w, m, v, name):
    r, c = w.shape
    tr = _ew_tile(r, c)

    def body(a_ref, b_ref, w_ref, m_ref, v_ref, g_ref, d_ref, mo_ref, vo_ref):
        g = a_ref[...] + b_ref[...]
        delta, mn, vn = _adamw(w_ref[...], g, m_ref[...], v_ref[...])
        g_ref[...] = g
        d_ref[...] = delta
        mo_ref[...] = mn
        vo_ref[...] = vn

    blk = pl.BlockSpec((tr, c), lambda i: (i, 0))
    return pl.pallas_call(
        body, name=name, grid=(r // tr,),
        in_specs=[blk] * 5, out_specs=[blk] * 4,
        out_shape=[jax.ShapeDtypeStruct((r, c), f32)] * 4,
        compiler_params=_cparams(),
    )(p_mine, p_sibling, w, m, v)


def adamw_small(g8, w, m, v):
    _, r, lanes = g8.shape

    def body(g_ref, w_ref, m_ref, v_ref, go_ref, d_ref, mo_ref, vo_ref):
        g = g_ref[0]
        for i in range(1, N_DEV):
            g = g + g_ref[i]
        delta, mn, vn = _adamw(w_ref[...], g, m_ref[...], v_ref[...])
        go_ref[...] = g
        d_ref[...] = delta
        mo_ref[...] = mn
        vo_ref[...] = vn

    return pl.pallas_call(
        body, name="adamw_small",
        out_shape=[jax.ShapeDtypeStruct((r, lanes), f32)] * 4,
        compiler_params=_cparams(),
    )(g8, w, m, v)


def _size(shape):
    n = 1
    for e in shape:
        n *= e
    return n


def _pack_rows(shapes):
    rows = [-(-_size(s) // 1024) * 8 for s in shapes]
    return rows, sum(rows)


def _pack(arrs, shapes):
    rows, _ = _pack_rows(shapes)
    parts = [jnp.pad(a.reshape(-1).astype(f32), (0, r * 128 - _size(s))).reshape(r, 128)
             for a, s, r in zip(arrs, shapes, rows)]
    return jnp.concatenate(parts, axis=0)


def _unpack(block, shapes):
    rows, _ = _pack_rows(shapes)
    out, off = [], 0
    for s, r in zip(shapes, rows):
        out.append(block[off:off + r].reshape(-1)[:_size(s)].reshape(s))
        off += r
    return out


TRANSPOSED = ("ffn1_w_gate", "ffn1_w_up", "ffn2_w_gate", "ffn2_w_up")


def _shard2d(a, n):
    return a[0].T if n in TRANSPOSED else a[0]


def _unshard(a, n):
    return (a.T if n in TRANSPOSED else a)[None]


BIG = ("ffn1_w_gate", "ffn1_w_up", "ffn1_w_down", "w_in", "w_branch_a", "w_branch_b", "w_out",
       "ffn2_w_gate", "ffn2_w_up", "ffn2_w_down")
SMALL = ("ffn1_norm", "mix_norm", "b_in", "sgu_norm_g", "sgu_norm_b", "sgu_w_s", "sgu_b_s", "ret_decay_logit",
         "ffn2_norm", "final_norm")
WEIGHTS = ("ffn1_norm", "ffn1_w_gate", "ffn1_w_up", "ffn1_w_down", "mix_norm", "w_in", "b_in", "sgu_norm_g",
           "sgu_norm_b", "sgu_w_s", "sgu_b_s", "ret_decay_logit", "w_branch_a", "w_branch_b", "w_out", "ffn2_norm",
           "ffn2_w_gate", "ffn2_w_up", "ffn2_w_down", "final_norm")


def kernel(x, ffn1_norm, ffn1_w_gate, ffn1_w_up, ffn1_w_down, mix_norm, w_in, b_in, sgu_norm_g, sgu_norm_b, sgu_w_s, sgu_b_s, ret_decay_logit, w_branch_a, w_branch_b, w_out, ffn2_norm, ffn2_w_gate, ffn2_w_up, ffn2_w_down, final_norm, loss_target, m_ffn1_norm, m_ffn1_w_gate, m_ffn1_w_up, m_ffn1_w_down, m_mix_norm, m_w_in, m_b_in, m_sgu_norm_g, m_sgu_norm_b, m_sgu_w_s, m_sgu_b_s, m_ret_decay_logit, m_w_branch_a, m_w_branch_b, m_w_out, m_ffn2_norm, m_ffn2_w_gate, m_ffn2_w_up, m_ffn2_w_down, m_final_norm, v_ffn1_norm, v_ffn1_w_gate, v_ffn1_w_up, v_ffn1_w_down, v_mix_norm, v_w_in, v_b_in, v_sgu_norm_g, v_sgu_norm_b, v_sgu_w_s, v_sgu_b_s, v_ret_decay_logit, v_w_branch_a, v_w_branch_b, v_w_out, v_ffn2_norm, v_ffn2_w_gate, v_ffn2_w_up, v_ffn2_w_down, v_final_norm):
    p = dict(ffn1_norm=ffn1_norm, ffn1_w_gate=ffn1_w_gate, ffn1_w_up=ffn1_w_up, ffn1_w_down=ffn1_w_down,
             mix_norm=mix_norm, w_in=w_in, b_in=b_in, sgu_norm_g=sgu_norm_g, sgu_norm_b=sgu_norm_b, sgu_w_s=sgu_w_s,
             sgu_b_s=sgu_b_s, ret_decay_logit=ret_decay_logit, w_branch_a=w_branch_a, w_branch_b=w_branch_b,
             w_out=w_out, ffn2_norm=ffn2_norm, ffn2_w_gate=ffn2_w_gate, ffn2_w_up=ffn2_w_up, ffn2_w_down=ffn2_w_down,
             final_norm=final_norm)
    mom = dict(ffn1_norm=m_ffn1_norm, ffn1_w_gate=m_ffn1_w_gate, ffn1_w_up=m_ffn1_w_up, ffn1_w_down=m_ffn1_w_down,
               mix_norm=m_mix_norm, w_in=m_w_in, b_in=m_b_in, sgu_norm_g=m_sgu_norm_g, sgu_norm_b=m_sgu_norm_b,
               sgu_w_s=m_sgu_w_s, sgu_b_s=m_sgu_b_s, ret_decay_logit=m_ret_decay_logit, w_branch_a=m_w_branch_a,
               w_branch_b=m_w_branch_b, w_out=m_w_out, ffn2_norm=m_ffn2_norm, ffn2_w_gate=m_ffn2_w_gate,
               ffn2_w_up=m_ffn2_w_up, ffn2_w_down=m_ffn2_w_down, final_norm=m_final_norm)
    var = dict(ffn1_norm=v_ffn1_norm, ffn1_w_gate=v_ffn1_w_gate, ffn1_w_up=v_ffn1_w_up, ffn1_w_down=v_ffn1_w_down,
               mix_norm=v_mix_norm, w_in=v_w_in, b_in=v_b_in, sgu_norm_g=v_sgu_norm_g, sgu_norm_b=v_sgu_norm_b,
               sgu_w_s=v_sgu_w_s, sgu_b_s=v_sgu_b_s, ret_decay_logit=v_ret_decay_logit, w_branch_a=v_w_branch_a,
               w_branch_b=v_w_branch_b, w_out=v_w_out, ffn2_norm=v_ffn2_norm, ffn2_w_gate=v_ffn2_w_gate,
               ffn2_w_up=v_ffn2_w_up, ffn2_w_down=v_ffn2_w_down, final_norm=v_final_norm)

    xs = x[0]
    tgt = loss_target[0]
    t, d = xs.shape
    dk = d // RET_HEADS

    shards2d = {n: _shard2d(p[n], n) for n in BIG}
    chip = (2 * lax.axis_index("x") + lax.axis_index("y")).astype(jnp.int32).reshape(1)
    groups = {"ffn1": ("ffn1_w_gate", "ffn1_w_up", "ffn1_w_down"), "in": ("w_in",),
              "mix": ("w_branch_a", "w_branch_b", "w_out"), "ffn2": ("ffn2_w_gate", "ffn2_w_up", "ffn2_w_down")}
    def own_slot(n, zero):
        sh = shards2d[n].astype(bf16) + zero
        return lax.dynamic_update_index_in_dim(lax.empty((N_CHIPS,) + sh.shape, bf16), sh, chip[0], 0)

    sems, bufs, tok = gather_start([own_slot(n, jnp.zeros((), bf16)) for n in groups["ffn1"]], [[0, 1, 2]],
                                   "gather_start_ffn1")
    gsem = {"ffn1": sems[0]}
    pending = dict(zip(groups["ffn1"], bufs))
    rest = [n for g in ("in", "mix", "ffn2") for n in groups[g]]
    sems, bufs, tok_rest = gather_start([own_slot(n, tok[0, 0].astype(bf16)) for n in rest],
                                 [[rest.index(n) for n in groups[g]] for g in ("in", "mix", "ffn2")], "gather_start_rest",
                                 both_cores=(0,))
    gsem.update(zip(("in", "mix", "ffn2"), sems))
    pending.update(zip(rest, bufs))

    def arrive(gs, after):
        got = []
        for g in gs:
            got += gather_wait([pending[n] for n in groups[g]], gsem[g], after, "gather_wait_" + g)
        return gather_forward(got, "gather_forward_" + gs[0])

    bin4 = b_in.reshape(N_CHIPS, 1, 2 * d)
    ws_b = sgu_w_s[0].astype(bf16)
    bs_c = sgu_b_s[0][:, :, None]
    cols, mats, cdec, cos, sin = retention_constants(ret_decay_logit[0], t, dk, tok_rest[0, 0])

    wg1, wu1, wd1 = [_pair_shards(w) for w in arrive(["ffn1"], cos)]
    x1, g1, u1 = ffn_fwd(xs, ffn1_norm, wg1, wu1, wd1, "ffn1_fwd")
    win, = gather_wait([pending["w_in"]], gsem["in"], x1, "gather_wait_in", both_cores=True)
    proj, hb2 = inproj_fwd(x1, mix_norm, win, bin4, cos, sin)
    late = []
    for g in ("mix", "ffn2"):
        late += gather_wait([pending[n] for n in groups[g]], gsem[g], proj, "gather_wait_" + g)
    fsems, late, ftok = forward_start(late, "forward_start_mix")
    a = sgu_fwd(proj, sgu_norm_g, sgu_norm_b, ws_b, bs_c, ftok)
    r, rn = ret_fwd(proj, cols, mats, cdec, a)
    wa, wb, wo, wg2, wu2, wd2 = forward_wait(late, fsems, rn, "forward_wait_mix")
    wa, wb, wo = [w.reshape(d, d) for w in (wa, wb, wo)]
    wg2, wu2, wd2 = [_pair_shards(w) for w in (wg2, wu2, wd2)]
    x2, ba, br = mix_fwd(a, rn, proj, wa, wb, wo, x1)
    loss_blk, dx3, d_final, g2, u2 = ffn_fwd_loss(x2, ffn2_norm, wg2, wu2, wd2, final_norm.reshape(1, d), tgt, "ffn2_fwd")

    sent, swaps = {}, {}
    out_g, out_d, out_m, out_v = {}, {}, {}, {}

    def reduce_plane(g, after):
        gsems, own, lands, _ = sent[g]
        own, lands = exchange_wait(own, lands, gsems, after, "exchange_wait_" + g)
        plane = [sum_partials(chip, o, l, "sum_" + n) for n, o, l in zip(groups[g], own, lands)]
        swaps[g] = swap_start(plane, "swap_start_" + g)
        return swaps[g][3]

    def update(g, after):
        ssems, plane, lands, _ = swaps[g]
        plane, other = swap_wait(plane, lands, ssems, after, "swap_wait_" + g)
        for n, mine, sib in zip(groups[g], plane, other):
            res = adamw_shard(mine, sib, shards2d[n], _shard2d(mom[n], n), _shard2d(var[n], n), "adamw_" + n)
            out_g[n], out_d[n], out_m[n], out_v[n] = [_unshard(o, n) for o in res]
        return res[0]

    dx2, dg2, du2, act2, hb3, dyb2, d_ffn2n = ffn_bwd_act(dx3, x2, ffn2_norm, g2, u2, wg2, wu2, wd2, "ffn2_bwd_act", tok)
    sent["ffn2"] = exchange_start(ffn_weight_grads(hb3, dyb2, dg2, du2, act2, "ffn2_grad", tok), "exchange_start_ffn2")
    da, drn, dga, dgb, mixb, dba, dbr, dx2b = mix_bwd_act(dx2, ba, br, proj, wa, wb, wo, sent["ffn2"][3])
    tg = min(t, 2048)
    row = pl.BlockSpec((tg, d), lambda s, i: (i, 0))

    def square_grad(xa, ya, name):
        return tn_matmul(xa, [ya], row, [row], 1, d, [d], t, tg, name, tok).reshape(N_CHIPS, d // N_CHIPS, d)

    g_mix = [square_grad(a, dba, "grad_w_branch_a"), square_grad(rn, dbr, "grad_w_branch_b"),
             square_grad(mixb, dx2b, "grad_w_out")]
    dua, dva, d_ws, d_bs, d_sng, d_snb = sgu_bwd(da, proj, sgu_norm_g, sgu_norm_b, ws_b, bs_c, sent["ffn2"][3])
    dq, dkr, dv, dgr, dlg = ret_bwd(drn, r, proj, cols, mats, cdec, cos, sin)
    segs = [dua, dva, dq, dkr, dv, dgr, dga, dgb]
    dx1, d_bin, d_mixn = inproj_bwd_act(segs, win, x1, mix_norm, dx2)
    g_in = None
    for s in range(N_CHIPS):
        g_in = tn_matmul(hb2, [segs[2 * s], segs[2 * s + 1]], row, [row, row], 1, d, [d, d], t, tg, "grad_w_in_%d" % s,
                         tok, (g_in, s, N_CHIPS))
    groups["mix_in"] = groups["mix"] + groups["in"]
    sent["mix_in"] = exchange_start(g_mix + [g_in], "exchange_start_mix_in")
    grad_x, dg1, du1, act1, hb1, dyb1, d_ffn1n = ffn_bwd_act(dx1, xs, ffn1_norm, g1, u1, wg1, wu1, wd1, "ffn1_bwd_act",
                                                              sent["mix_in"][3])
    dlogit = dlg[:, 0:2, 0].T * jax.nn.sigmoid(-ret_decay_logit[0].astype(f32))
    small_g = dict(ffn1_norm=d_ffn1n, mix_norm=d_mixn, b_in=d_bin, sgu_norm_g=d_sng, sgu_norm_b=d_snb, sgu_w_s=d_ws,
                   sgu_b_s=d_bs, ret_decay_logit=dlogit, ffn2_norm=d_ffn2n, final_norm=d_final)
    shapes = [p[n].shape for n in SMALL]
    small_sems, small_blk, small_land, small_tok = small_start(_pack([small_g[n] for n in SMALL], shapes))

    def send_one(which, grad):
        n = "ffn1_" + which
        groups[n] = (n,)
        sent[n] = exchange_start([grad], "exchange_start_" + n)
        return sent[n][3]

    ffn_weight_grads(hb1, dyb1, dg1, du1, act1, "ffn1_grad", small_tok, send_one)

    after = reduce_plane("ffn2", sent["ffn1_w_down"][3])
    after = reduce_plane("mix_in", after)
    after = update("ffn2", after)
    g8 = small_wait(small_blk, small_land, small_sems, after)
    sg, sd, sm, sv = adamw_small(g8, _pack([p[n] for n in SMALL], shapes), _pack([mom[n] for n in SMALL], shapes),
                                 _pack([var[n] for n in SMALL], shapes))
    for res, blockv in ((out_g, sg), (out_d, sd), (out_m, sm), (out_v, sv)):
        for n, val in zip(SMALL, _unpack(blockv, shapes)):
            res[n] = val
    after = update("mix_in", sg)
    after = reduce_plane("ffn1_w_gate", after)
    after = reduce_plane("ffn1_w_up", after)
    after = update("ffn1_w_gate", after)
    after = reduce_plane("ffn1_w_down", after)
    after = update("ffn1_w_up", after)
    update("ffn1_w_down", after)

    loss = lax.psum(loss_blk[0, 0], ("x", "y", "c"))
    return (loss, grad_x[None], *[out_g[n] for n in WEIGHTS], *[out_d[n] for n in WEIGHTS],
            *[out_m[n] for n in WEIGHTS], *[out_v[n] for n in WEIGHTS])
```

```python
import jax
import jax.numpy as jnp
from jax import lax
from jax.experimental import pallas as pl
from jax.experimental.pallas import tpu as pltpu

f32 = jnp.float32
bf16 = jnp.bfloat16

SGU_CHUNK = 128
CHUNK = 256
RET_HEADS = 4
SGU_GROUPS = 4
ROPE_BASE = 10000.0
NORM_EPS = 1e-6
ADAM_LR = 0.001
ADAM_B1 = 0.9
ADAM_B2 = 0.999
ADAM_EPS = 1e-08
ADAM_WD = 0.01
ADAM_STEP = 10
N_CHIPS = 4
N_DEV = 8
MESH = pl.DeviceIdType.MESH
VMEM_LIMIT = 52 * 1024 * 1024
VMEM_LIMIT_WIDE = 62 * 1024 * 1024

_NT = (((1,), (1,)), ((), ()))
_TN = (((0,), (0,)), ((), ()))


def _cparams(limit=None):
    return pltpu.CompilerParams(vmem_limit_bytes=VMEM_LIMIT if limit is None else limit)


def _row_tile(t):
    return 512 if t >= 2048 else t // 2


def _dot(a, b):
    return jnp.dot(a, b, preferred_element_type=f32)


def _dot_nt(a, b):
    return lax.dot_general(a, b, _NT, preferred_element_type=f32)


def _dot_tn(a, b):
    return lax.dot_general(a, b, _TN, preferred_element_type=f32)


def _rms(x, g):
    r = lax.rsqrt(jnp.mean(x * x, axis=-1, keepdims=True) + NORM_EPS)
    xh = x * r
    return xh * g, xh, r


def _rms_bwd(dy, xh, r, g):
    dxh = dy * g
    return r * (dxh - xh * jnp.mean(dxh * xh, axis=-1, keepdims=True))


def _sigmoid(x):
    return jax.nn.sigmoid(x)


def _dsilu(g, sg):
    return sg * (1.0 + g * (1.0 - sg))


def _gelu(x):
    return 0.5 * x * (1.0 + lax.erf(x * 0.7071067811865476))


def _dgelu(x):
    return 0.5 * (1.0 + lax.erf(x * 0.7071067811865476)) + x * jnp.exp(-0.5 * x * x) * 0.3989422804014327


def _acc_out(ref, first, val):
    @pl.when(first)
    def _():
        ref[...] = val

    @pl.when(jnp.logical_not(first))
    def _():
        ref[...] += val


def _ffn_tile(t):
    return 256 if t >= 2048 else t // 2


def _ffn_fwd_rows(xx, ng_ref, wg_ref, wu_ref, wd_ref, g_ref, u_ref):
    y, _, _ = _rms(xx, ng_ref[...])
    h = y.astype(bf16)
    acc = None
    for s in range(wg_ref.shape[0]):
        g = _dot_nt(h, wg_ref[s])
        u = _dot_nt(h, wu_ref[s])
        g_ref[s] = g.astype(bf16)
        u_ref[s] = u.astype(bf16)
        part = _dot((g * _sigmoid(g) * u).astype(bf16), wd_ref[s])
        acc = part if acc is None else acc + part
    return xx + 0.5 * acc


def ffn_fwd(x, ng, wg, wu, wd, name):
    t, d = x.shape
    ns, fs, _ = wg.shape
    tm = _ffn_tile(t)

    def body(x_ref, ng_ref, wg_ref, wu_ref, wd_ref, xo_ref, g_ref, u_ref):
        xo_ref[...] = _ffn_fwd_rows(x_ref[...], ng_ref, wg_ref, wu_ref, wd_ref, g_ref, u_ref)

    row = pl.BlockSpec((tm, d), lambda i: (i, 0))
    shard = pl.BlockSpec((ns, tm, fs), lambda i: (0, i, 0))
    wspec = pl.BlockSpec((ns, fs, d), lambda i: (0, 0, 0), pipeline_mode=pl.Buffered(1))
    return pl.pallas_call(
        body, name=name, grid=(t // tm,),
        in_specs=[row, pl.BlockSpec((1, d), lambda i: (0, 0)), wspec, wspec, wspec],
        out_specs=[row, shard, shard],
        out_shape=[jax.ShapeDtypeStruct((t, d), f32), jax.ShapeDtypeStruct((ns, t, fs), bf16),
                   jax.ShapeDtypeStruct((ns, t, fs), bf16)],
        compiler_params=_cparams(),
    )(x, ng, wg, wu, wd)


def ffn_fwd_loss(x, ng, wg, wu, wd, fng, tgt, name):
    t, d = x.shape
    ns, fs, _ = wg.shape
    tm = _ffn_tile(t)

    def body(x_ref, ng_ref, wg_ref, wu_ref, wd_ref, fng_ref, t_ref, loss_ref, dx_ref, dfn_ref, g_ref, u_ref):
        i = pl.program_id(0)
        x3 = _ffn_fwd_rows(x_ref[...], ng_ref, wg_ref, wu_ref, wd_ref, g_ref, u_ref)
        y, xh, r = _rms(x3, fng_ref[...])
        diff = y - t_ref[...]
        part = 0.5 * jnp.sum(jnp.sum(diff * diff, axis=0, keepdims=True), axis=1, keepdims=True) / d
        _acc_out(loss_ref, i == 0, jnp.broadcast_to(part, (1, 128)))
        dy = diff * (1.0 / d)
        dx_ref[...] = _rms_bwd(dy, xh, r, fng_ref[...])
        _acc_out(dfn_ref, i == 0, jnp.sum(dy * xh, axis=0, keepdims=True))

    row = pl.BlockSpec((tm, d), lambda i: (i, 0))
    vec = pl.BlockSpec((1, d), lambda i: (0, 0))
    shard = pl.BlockSpec((ns, tm, fs), lambda i: (0, i, 0))
    wspec = pl.BlockSpec((ns, fs, d), lambda i: (0, 0, 0), pipeline_mode=pl.Buffered(1))
    return pl.pallas_call(
        body, name=name, grid=(t // tm,),
        in_specs=[row, vec, wspec, wspec, wspec, vec, row],
        out_specs=[pl.BlockSpec((1, 128), lambda i: (0, 0)), row, vec, shard, shard],
        out_shape=[jax.ShapeDtypeStruct((1, 128), f32), jax.ShapeDtypeStruct((t, d), f32), jax.ShapeDtypeStruct((1, d), f32),
                   jax.ShapeDtypeStruct((ns, t, fs), bf16), jax.ShapeDtypeStruct((ns, t, fs), bf16)],
        compiler_params=_cparams(),
    )(x, ng, wg, wu, wd, fng, tgt)


def ffn_bwd_act(dxo, x, ng, g, u, wg, wu, wd, name, dep):
    t, d = x.shape
    ns, fs, _ = wg.shape
    tm = _ffn_tile(t)

    def body(dxo_ref, x_ref, ng_ref, g_ref, u_ref, wg_ref, wu_ref, wd_ref, dep_ref,
             dx_ref, dg_ref, du_ref, act_ref, hb_ref, dyb_ref, dng_ref):
        i = pl.program_id(0)
        dxo = dxo_ref[...]
        dyb = (0.5 * dxo).astype(bf16)
        dyb_ref[...] = dyb
        dh = None
        for s in range(ns):
            dact = _dot_nt(dyb, wd_ref[s])
            gg = g_ref[s].astype(f32)
            uu = u_ref[s].astype(f32)
            sg = _sigmoid(gg)
            sil = gg * sg
            dgb = (dact * uu * _dsilu(gg, sg)).astype(bf16)
            dub = (dact * sil).astype(bf16)
            dg_ref[s] = dgb
            du_ref[s] = dub
            act_ref[s] = (sil * uu).astype(bf16)
            part = _dot(dgb, wg_ref[s]) + _dot(dub, wu_ref[s])
            dh = part if dh is None else dh + part
        y, xh, r = _rms(x_ref[...], ng_ref[...])
        hb_ref[...] = y.astype(bf16)
        dx_ref[...] = dxo + _rms_bwd(dh, xh, r, ng_ref[...])
        _acc_out(dng_ref, i == 0, jnp.sum(dh * xh, axis=0, keepdims=True))

    row = pl.BlockSpec((tm, d), lambda i: (i, 0))
    shard = pl.BlockSpec((ns, tm, fs), lambda i: (0, i, 0))
    wspec = pl.BlockSpec((ns, fs, d), lambda i: (0, 0, 0), pipeline_mode=pl.Buffered(1))
    vec = pl.BlockSpec((1, d), lambda i: (0, 0))
    return pl.pallas_call(
        body, name=name, grid=(t // tm,),
        in_specs=[row, row, vec, shard, shard, wspec, wspec, wspec, _ANY],
        out_specs=[row, shard, shard, shard, row, row, vec],
        out_shape=[jax.ShapeDtypeStruct((t, d), f32)] + [jax.ShapeDtypeStruct((ns, t, fs), bf16)] * 3
        + [jax.ShapeDtypeStruct((t, d), bf16)] * 2 + [jax.ShapeDtypeStruct((1, d), f32)],
        compiler_params=_cparams(VMEM_LIMIT_WIDE),
    )(dxo, x, ng, g, u, wg, wu, wd, dep)


def tn_matmul(xs, ys, x_spec, y_specs, n_shards, k1, k2s, t, tm, name, dep, into=None):
    k2 = sum(k2s)
    ny = len(ys)

    def body(*refs):
        x_ref = refs[0]
        y_refs = refs[1:1 + ny]
        o_ref, acc = refs[-2], refs[-1]
        i = pl.program_id(1)
        xb = x_ref[0] if len(x_ref.shape) == 3 else x_ref[...]
        off = 0
        for y_ref, w in zip(y_refs, k2s):
            yb = y_ref[0] if len(y_ref.shape) == 3 else y_ref[...]
            part = _dot_tn(xb, yb)
            sl = (slice(None), slice(off, off + w))

            @pl.when(i == 0)
            def _(part=part, sl=sl):
                acc[sl] = part

            @pl.when(i > 0)
            def _(part=part, sl=sl):
                acc[sl] += part

            off += w

        @pl.when(i == t // tm - 1)
        def _():
            o_ref[0] = acc[...].astype(bf16)

    if into is None:
        slot0, total, extra, aliases = 0, n_shards, [], {}
    else:
        buf, slot0, total = into
        extra = [] if buf is None else [buf]
        aliases = {} if buf is None else {2 + ny: 0}
    return pl.pallas_call(
        body, name=name, grid=(n_shards, t // tm),
        in_specs=[x_spec] + list(y_specs) + [_ANY] * (1 + len(extra)),
        out_specs=pl.BlockSpec((1, k1, k2), lambda s, i: (slot0 + s, 0, 0)),
        out_shape=jax.ShapeDtypeStruct((total, k1, k2), bf16),
        scratch_shapes=[pltpu.VMEM((k1, k2), f32)],
        input_output_aliases=aliases,
        compiler_params=_cparams(),
    )(xs, *ys, dep, *extra)


def _pair_shards(w):
    s4, fs, d = w.shape
    return w.reshape(s4 // 2, 2 * fs, d)


def ffn_weight_grads(hb, dyb, dg, du, act, name, dep, each=None):
    t, d = hb.shape
    s2, _, fs2 = dg.shape
    tm = t
    row = pl.BlockSpec((tm, d), lambda s, i: (i, 0))
    shard = pl.BlockSpec((1, tm, fs2), lambda s, i: (s, i, 0))
    grads = []
    for xa, ya, which in ((dg, hb, "w_gate"), (du, hb, "w_up"), (act, dyb, "w_down")):
        g = tn_matmul(xa, [ya], shard, [row], s2, fs2, [d], t, tm, name + "_" + which, dep)
        g = g.reshape(2 * s2, fs2 // 2, d)
        if each is not None:
            dep = each(which, g)
        grads.append(g)
    return grads


def inproj_fwd(x1, ng, win, bin4, cos, sin):
    t, d = x1.shape
    s4, _, w2 = win.shape
    tm = _row_tile(t)
    dk = d // RET_HEADS
    scale = dk ** -0.5

    def body(x_ref, ng_ref, w_ref, b_ref, cos_ref, sin_ref, p_ref, hb_ref):
        y, _, _ = _rms(x_ref[...], ng_ref[...])
        h = y.astype(bf16)
        hb_ref[...] = h
        for s in range(s4):
            p = _dot(h, w_ref[s]) + b_ref[s]
            if s != 1:
                p_ref[s] = p.astype(bf16)
            else:
                cs, sn = cos_ref[...], sin_ref[...]
                for e in range(2 * RET_HEADS):
                    cols = slice(e * dk, (e + 1) * dk)
                    rot = _rot(p[:, cols], cs, sn)
                    p_ref[s, :, cols] = (rot if e < RET_HEADS else rot * scale).astype(bf16)

    tab = pl.BlockSpec((tm, dk // 2), lambda i: (i, 0))
    return pl.pallas_call(
        body, name="inproj_fwd", grid=(t // tm,),
        in_specs=[pl.BlockSpec((tm, d), lambda i: (i, 0)), pl.BlockSpec((1, d), lambda i: (0, 0)),
                  pl.BlockSpec((s4, d, w2), lambda i: (0, 0, 0), pipeline_mode=pl.Buffered(1)),
                  pl.BlockSpec((s4, 1, w2), lambda i: (0, 0, 0)), tab, tab],
        out_specs=[pl.BlockSpec((s4, tm, w2), lambda i: (0, i, 0)), pl.BlockSpec((tm, d), lambda i: (i, 0))],
        out_shape=[jax.ShapeDtypeStruct((s4, t, w2), bf16), jax.ShapeDtypeStruct((t, d), bf16)],
        compiler_params=_cparams(),
    )(x1, ng, win, bin4, cos, sin)


def _sgu_norm(va, ng, nb):
    gv = _gelu(va)
    mu = jnp.mean(gv, axis=-1, keepdims=True)
    xc = gv - mu
    rstd = lax.rsqrt(jnp.mean(xc * xc, axis=-1, keepdims=True) + NORM_EPS)
    xh = xc * rstd
    return xh, rstd, (xh * ng + nb).astype(bf16)


def sgu_fwd(proj, ng, nb, ws, bs, dep):
    _, t, w2 = proj.shape
    d = w2 // 2
    gd = d // SGU_GROUPS
    tm = _row_tile(t)

    def body(p_ref, ng_ref, nb_ref, ws_ref, bs_ref, dep_ref, a_ref):
        ua = p_ref[0, :, 0:d].astype(f32)
        va = p_ref[0, :, d:w2].astype(f32)
        gu = _gelu(ua)
        _, _, vn = _sgu_norm(va, ng_ref[...], nb_ref[...])
        for c in range(tm // SGU_CHUNK):
            rows = slice(c * SGU_CHUNK, (c + 1) * SGU_CHUNK)
            for g in range(SGU_GROUPS):
                cols = slice(g * gd, (g + 1) * gd)
                sg = _dot(ws_ref[g], vn[rows, cols]) + bs_ref[g]
                a_ref[rows, cols] = (gu[rows, cols] * sg).astype(bf16)

    return pl.pallas_call(
        body, name="sgu_fwd", grid=(t // tm,),
        in_specs=[pl.BlockSpec((1, tm, w2), lambda i: (0, i, 0)), pl.BlockSpec((1, d), lambda i: (0, 0)),
                  pl.BlockSpec((1, d), lambda i: (0, 0)), pl.BlockSpec((SGU_GROUPS, SGU_CHUNK, SGU_CHUNK), lambda i: (0, 0, 0)),
                  pl.BlockSpec((SGU_GROUPS, SGU_CHUNK, 1), lambda i: (0, 0, 0)), _ANY],
        out_specs=pl.BlockSpec((tm, d), lambda i: (i, 0)),
        out_shape=jax.ShapeDtypeStruct((t, d), bf16),
        compiler_params=_cparams(),
    )(proj, ng, nb, ws, bs, dep)


def sgu_bwd(da, proj, ng, nb, ws, bs, dep):
    _, t, w2 = proj.shape
    d = w2 // 2
    gd = d // SGU_GROUPS
    tm = _row_tile(t)

    def body(da_ref, p_ref, ng_ref, nb_ref, ws_ref, bs_ref, dep_ref,
             dua_ref, dva_ref, dws_ref, dbs_ref, dng_ref, dnb_ref, dvn_scr):
        i = pl.program_id(0)
        ua = p_ref[0, :, 0:d].astype(f32)
        va = p_ref[0, :, d:w2].astype(f32)
        gu = _gelu(ua)
        xh, rstd, vn = _sgu_norm(va, ng_ref[...], nb_ref[...])
        dad = da_ref[...].astype(f32)
        dsb = (dad * gu).astype(bf16)
        for c in range(tm // SGU_CHUNK):
            rows = slice(c * SGU_CHUNK, (c + 1) * SGU_CHUNK)
            for g in range(SGU_GROUPS):
                cols = slice(g * gd, (g + 1) * gd)
                sg = _dot(ws_ref[g], vn[rows, cols]) + bs_ref[g]
                dua_ref[rows, cols] = (dad[rows, cols] * sg * _dgelu(ua[rows, cols])).astype(bf16)
                ds = dsb[rows, cols]
                dvn_scr[rows, cols] = _dot_tn(ws_ref[g], ds)
                dw = _dot_nt(ds, vn[rows, cols])
                db = jnp.sum(ds.astype(f32), axis=1, keepdims=True)
                if c == 0:
                    _acc_out(dws_ref.at[g], i == 0, dw)
                    _acc_out(dbs_ref.at[g], i == 0, db)
                else:
                    dws_ref[g] += dw
                    dbs_ref[g] += db
        dvn = dvn_scr[...]
        _acc_out(dng_ref, i == 0, jnp.sum(dvn * xh, axis=0, keepdims=True))
        _acc_out(dnb_ref, i == 0, jnp.sum(dvn, axis=0, keepdims=True))
        dxh = dvn * ng_ref[...]
        dgv = rstd * (dxh - jnp.mean(dxh, axis=-1, keepdims=True) - xh * jnp.mean(dxh * xh, axis=-1, keepdims=True))
        dva_ref[...] = (dgv * _dgelu(va)).astype(bf16)

    row = pl.BlockSpec((tm, d), lambda i: (i, 0))
    vec = pl.BlockSpec((1, d), lambda i: (0, 0))
    wsp = pl.BlockSpec((SGU_GROUPS, SGU_CHUNK, SGU_CHUNK), lambda i: (0, 0, 0))
    bsp = pl.BlockSpec((SGU_GROUPS, SGU_CHUNK, 1), lambda i: (0, 0, 0))
    return pl.pallas_call(
        body, name="sgu_bwd", grid=(t // tm,),
        in_specs=[row, pl.BlockSpec((1, tm, w2), lambda i: (0, i, 0)), vec, vec, wsp, bsp, _ANY],
        out_specs=[row, row, wsp, bsp, vec, vec],
        out_shape=[jax.ShapeDtypeStruct((t, d), bf16), jax.ShapeDtypeStruct((t, d), bf16),
                   jax.ShapeDtypeStruct((SGU_GROUPS, SGU_CHUNK, SGU_CHUNK), f32), jax.ShapeDtypeStruct((SGU_GROUPS, SGU_CHUNK, 1), f32),
                   jax.ShapeDtypeStruct((1, d), f32), jax.ShapeDtypeStruct((1, d), f32)],
        scratch_shapes=[pltpu.VMEM((tm, d), f32)],
        compiler_params=_cparams(),
    )(da, proj, ng, nb, ws, bs, dep)


def retention_constants(decay_logit, t, dk, zero):
    lg = jax.nn.log_sigmoid(decay_logit.astype(f32) + zero)
    lgf = lg[0][:, None]
    lgb = lg[1][:, None]
    idx = jnp.arange(CHUNK, dtype=f32)[None, :]
    af = jnp.exp((idx + 1.0) * lgf)
    ab = jnp.exp((CHUNK - idx) * lgb)
    kf = jnp.exp((CHUNK - 1.0 - idx) * lgf)
    kb = jnp.exp(idx * lgb)
    cols = jnp.stack([af, ab, kf, kb, af * (idx + 1.0), ab * (CHUNK - idx), kf * (CHUNK - 1.0 - idx), kb * idx], axis=1)
    cols = cols[..., None]
    diff = idx[0][:, None] - idx[0][None, :]
    dfm = jnp.where(diff >= 0, jnp.exp(jnp.maximum(diff, 0.0)[None] * lgf[:, :, None]), 0.0)
    dbm = jnp.where(diff < 0, jnp.exp(jnp.maximum(-diff, 0.0)[None] * lgb[:, :, None]), 0.0)
    mats = jnp.stack([dfm + dbm, dfm * diff[None], dbm * (-diff)[None]], axis=1)
    cdec = jnp.stack([jnp.broadcast_to(jnp.exp(CHUNK * lgf), (RET_HEADS, dk)),
                      jnp.broadcast_to(jnp.exp(CHUNK * lgb), (RET_HEADS, dk))], axis=1)
    theta = ROPE_BASE ** (-jnp.arange(0, dk, 2, dtype=f32) / dk)
    ang = (jnp.arange(t, dtype=f32) + zero)[:, None] * theta[None, :]
    return cols, mats, cdec, jnp.cos(ang), jnp.sin(ang)


def _rot(tr, cos, sin):
    half = tr.shape[-1] // 2
    t1 = tr[:, :half]
    t2 = tr[:, half:]
    return jnp.concatenate([t1 * cos - t2 * sin, t2 * cos + t1 * sin], axis=-1)


def _rot_inv(dt, cos, sin):
    half = dt.shape[-1] // 2
    d1 = dt[:, :half]
    d2 = dt[:, half:]
    return jnp.concatenate([d1 * cos + d2 * sin, d2 * cos - d1 * sin], axis=-1)


def _ret_tile(t):
    return 2048 if t >= 4096 else _row_tile(t)


def _ret_specs(t, d, dk, rt):
    nr = t // rt
    hq = d // dk

    def blk(p, n):
        return (1 - p) * (nr - 1 - n) + p * n

    q_spec = pl.BlockSpec((1, rt, dk), lambda h, p, n: (1, blk(p, n), h))
    k_spec = pl.BlockSpec((1, rt, dk), lambda h, p, n: (1, blk(p, n), hq + h))
    v_spec = pl.BlockSpec((1, rt, dk), lambda h, p, n: (2, blk(p, n), h))
    g_spec = pl.BlockSpec((1, rt, dk), lambda h, p, n: (2, blk(p, n), hq + h))
    tab_spec = pl.BlockSpec((rt, dk // 2), lambda h, p, n: (blk(p, n), 0))
    cols_spec = pl.BlockSpec((1, 8, CHUNK, 1), lambda h, p, n: (h, 0, 0, 0))
    mats_spec = pl.BlockSpec((1, 3, CHUNK, CHUNK), lambda h, p, n: (h, 0, 0, 0))
    cdec_spec = pl.BlockSpec((1, 2, dk), lambda h, p, n: (h, 0, 0))
    in_row = pl.BlockSpec((rt, dk), lambda h, p, n: (blk(p, n), h))
    out_row = pl.BlockSpec((rt, dk), lambda h, p, n: (p * n, h))
    return nr, blk, q_spec, k_spec, v_spec, g_spec, tab_spec, cols_spec, mats_spec, cdec_spec, in_row, out_row


def ret_fwd(proj, cols, mats, cdec, dep):
    _, t, w2 = proj.shape
    d = w2 // 2
    dk = d // RET_HEADS
    rt = _ret_tile(t)
    cpt = rt // CHUNK
    nr, blk, q_spec, k_spec, v_spec, g_spec, _, cols_spec, mats_spec, cdec_spec, _, out_row = _ret_specs(t, d, dk, rt)

    def body(q_ref, k_ref, v_ref, g_ref, cols_ref, mats_ref, cdec_ref, dep_ref, r_ref, rn_ref, sb_scr, st):
        p = pl.program_id(1)
        n = pl.program_id(2)
        af, ab, kf, kb = cols_ref[0, 0], cols_ref[0, 1], cols_ref[0, 2], cols_ref[0, 3]
        cf = cdec_ref[0, 0:1, :]
        cb = cdec_ref[0, 1:2, :]

        @pl.when(n == 0)
        def _():
            st[...] = jnp.zeros_like(st)

        @pl.when(p == 0)
        def _():
            for j in reversed(range(cpt)):
                rows = slice(j * CHUNK, (j + 1) * CHUNK)
                ch = blk(p, n) * cpt + j
                kk = k_ref[0, rows, :].astype(f32)
                sb_scr[ch] = st[...].astype(bf16)
                st[...] = st[...] * cb + _dot_tn((kk * kb).astype(bf16), v_ref[0, rows, :])

        @pl.when(p == 1)
        def _():
            for j in range(cpt):
                rows = slice(j * CHUNK, (j + 1) * CHUNK)
                ch = blk(p, n) * cpt + j
                qb = q_ref[0, rows, :]
                kkb = k_ref[0, rows, :]
                q = qb.astype(f32)
                kk = kkb.astype(f32)
                v = v_ref[0, rows, :]
                pm = (_dot_nt(qb, kkb) * mats_ref[0, 0]).astype(bf16)
                out = (_dot(pm, v) + _dot((q * af).astype(bf16), st[...].astype(bf16))
                       + _dot((q * ab).astype(bf16), sb_scr[ch]))
                st[...] = st[...] * cf + _dot_tn((kk * kf).astype(bf16), v)
                rhat = out * lax.rsqrt(jnp.mean(out * out, axis=-1, keepdims=True) + NORM_EPS)
                gg = g_ref[0, rows, :].astype(f32)
                r_ref[rows, :] = out.astype(bf16)
                rn_ref[rows, :] = (rhat * gg * _sigmoid(gg)).astype(bf16)

    return pl.pallas_call(
        body, name="ret_fwd", grid=(RET_HEADS, 2, nr),
        in_specs=[q_spec, k_spec, v_spec, g_spec, cols_spec, mats_spec, cdec_spec, _ANY],
        out_specs=[out_row, out_row],
        out_shape=[jax.ShapeDtypeStruct((t, d), bf16), jax.ShapeDtypeStruct((t, d), bf16)],
        scratch_shapes=[pltpu.VMEM((t // CHUNK, dk, dk), bf16), pltpu.VMEM((dk, dk), f32)],
        compiler_params=_cparams(),
    )(proj, proj, proj, proj, cols, mats, cdec, dep)


def ret_bwd(drn, r, proj, cols, mats, cdec, cos, sin):
    _, t, w2 = proj.shape
    d = w2 // 2
    dk = d // RET_HEADS
    rt = _ret_tile(t)
    cpt = rt // CHUNK
    nr, blk, q_spec, k_spec, v_spec, g_spec, tab_spec, cols_spec, mats_spec, cdec_spec, in_row, out_row = _ret_specs(t, d, dk, rt)
    scale = dk ** -0.5

    def body(drn_ref, r_ref, q_ref, k_ref, v_ref, g_ref, cos_ref, sin_ref, cols_ref, mats_ref, cdec_ref,
             dq_ref, dk_ref, dv_ref, dg_ref, dlg_ref,
             sb_scr, gf_scr, st_s, st_g, acc_af, acc_ab, acc_vf, acc_vb, acc_sf, acc_sb, dout_scr, dgr_scr):
        p = pl.program_id(1)
        n = pl.program_id(2)
        af, ab, kf, kb = cols_ref[0, 0], cols_ref[0, 1], cols_ref[0, 2], cols_ref[0, 3]
        af1, ab1, kf1, kb1 = cols_ref[0, 4], cols_ref[0, 5], cols_ref[0, 6], cols_ref[0, 7]
        cf = cdec_ref[0, 0:1, :]
        cb = cdec_ref[0, 1:2, :]

        @pl.when(n == 0)
        def _():
            st_s[...] = jnp.zeros_like(st_s)
            st_g[...] = jnp.zeros_like(st_g)

        @pl.when(jnp.logical_and(n == 0, p == 1))
        def _():
            for a in (acc_af, acc_ab, acc_vf, acc_vb, acc_sf, acc_sb):
                a[...] = jnp.zeros_like(a)

        def load(rows):
            cs, sn = cos_ref[rows, :], sin_ref[rows, :]
            q = q_ref[0, rows, :].astype(f32)
            kk = k_ref[0, rows, :].astype(f32)
            rr = r_ref[rows, :].astype(f32)
            rstd = lax.rsqrt(jnp.mean(rr * rr, axis=-1, keepdims=True) + NORM_EPS)
            rhat = rr * rstd
            gg = g_ref[0, rows, :].astype(f32)
            sg = _sigmoid(gg)
            dd = drn_ref[rows, :].astype(f32)
            drhat = dd * gg * sg
            dout = rstd * (drhat - rhat * jnp.mean(drhat * rhat, axis=-1, keepdims=True))
            dgr = dd * rhat * _dsilu(gg, sg)
            return q, kk, dout.astype(bf16), dgr, cs, sn

        @pl.when(p == 0)
        def _():
            for j in reversed(range(cpt)):
                rows = slice(j * CHUNK, (j + 1) * CHUNK)
                ch = blk(p, n) * cpt + j
                q, kk, doutb, dgr, _, _ = load(rows)
                kept = pl.ds(pl.multiple_of(ch * CHUNK, CHUNK), CHUNK)
                dout_scr[kept, :] = doutb
                dgr_scr[kept, :] = dgr.astype(bf16)
                sb_scr[ch] = st_s[...].astype(bf16)
                gf_scr[ch] = st_g[...].astype(bf16)
                st_s[...] = st_s[...] * cb + _dot_tn((kk * kb).astype(bf16), v_ref[0, rows, :])
                st_g[...] = st_g[...] * cf + _dot_tn((q * af).astype(bf16), doutb)

        @pl.when(p == 1)
        def _():
            for j in range(cpt):
                rows = slice(j * CHUNK, (j + 1) * CHUNK)
                ch = blk(p, n) * cpt + j
                kept = pl.ds(pl.multiple_of(ch * CHUNK, CHUNK), CHUNK)
                doutb = dout_scr[kept, :]
                cs, sn = cos_ref[rows, :], sin_ref[rows, :]
                v = v_ref[0, rows, :]
                qb = q_ref[0, rows, :]
                kkb = k_ref[0, rows, :]
                q = qb.astype(f32)
                kk = kkb.astype(f32)
                sf = st_s[...]
                gb = st_g[...]
                sfb = sf.astype(bf16)
                gbb = gb.astype(bf16)
                sbb = sb_scr[ch]
                gfb = gf_scr[ch]
                dmat = mats_ref[0, 0]
                scores = _dot_nt(qb, kkb)
                dpraw = _dot_nt(doutb, v)
                dpb = (dpraw * dmat).astype(bf16)
                pmb = (scores * dmat).astype(bf16)
                x1 = _dot_nt(doutb, sfb)
                x2 = _dot_nt(doutb, sbb)
                y1 = _dot_nt(v, gfb)
                y2 = _dot_nt(v, gbb)
                kdf = (kk * kf).astype(bf16)
                kdb = (kk * kb).astype(bf16)
                dq = _dot(dpb, kkb) + x1 * af + x2 * ab
                dkk = _dot_tn(dpb, qb) + y1 * kf + y2 * kb
                dv = _dot_tn(pmb, doutb) + _dot(kdf, gfb) + _dot(kdb, gbb)
                ps = dpraw * scores
                acc_af[...] += ps * mats_ref[0, 1]
                acc_ab[...] += ps * mats_ref[0, 2]
                acc_vf[...] += x1 * q * af1 + y1 * kk * kf1
                acc_vb[...] += x2 * q * ab1 + y2 * kk * kb1
                acc_sf[...] += gfb.astype(f32) * sf
                acc_sb[...] += gb * sbb.astype(f32)
                st_s[...] = sf * cf + _dot_tn(kdf, v)
                st_g[...] = gb * cb + _dot_tn((q * ab).astype(bf16), doutb)
                dq_ref[rows, :] = _rot_inv(dq, cs, sn).astype(bf16)
                dk_ref[rows, :] = (_rot_inv(dkk, cs, sn) * scale).astype(bf16)
                dv_ref[rows, :] = dv.astype(bf16)
                dg_ref[rows, :] = dgr_scr[kept, :]

        @pl.when(jnp.logical_and(p == 1, n == nr - 1))
        def _():
            tf = jnp.sum(acc_af[...]) + jnp.sum(acc_vf[...]) + CHUNK * jnp.sum(acc_sf[...] * cf)
            tb = jnp.sum(acc_ab[...]) + jnp.sum(acc_vb[...]) + CHUNK * jnp.sum(acc_sb[...] * cb)
            rid = lax.broadcasted_iota(jnp.int32, (8, 128), 0)
            dlg_ref[0] = jnp.where(rid == 0, tf, jnp.where(rid == 1, tb, 0.0))

    nch = t // CHUNK
    return pl.pallas_call(
        body, name="ret_bwd", grid=(RET_HEADS, 2, nr),
        in_specs=[in_row, in_row, q_spec, k_spec, v_spec, g_spec, tab_spec, tab_spec, cols_spec, mats_spec, cdec_spec],
        out_specs=[out_row, out_row, out_row, out_row, pl.BlockSpec((1, 8, 128), lambda h, p, n: (h, 0, 0))],
        out_shape=[jax.ShapeDtypeStruct((t, d), bf16)] * 4 + [jax.ShapeDtypeStruct((RET_HEADS, 8, 128), f32)],
        scratch_shapes=[pltpu.VMEM((nch, dk, dk), bf16), pltpu.VMEM((nch, dk, dk), bf16),
                        pltpu.VMEM((dk, dk), f32), pltpu.VMEM((dk, dk), f32),
                        pltpu.VMEM((CHUNK, CHUNK), f32), pltpu.VMEM((CHUNK, CHUNK), f32),
                        pltpu.VMEM((CHUNK, dk), f32), pltpu.VMEM((CHUNK, dk), f32),
                        pltpu.VMEM((dk, dk), f32), pltpu.VMEM((dk, dk), f32),
                        pltpu.VMEM((t, dk), bf16), pltpu.VMEM((t, dk), bf16)],
        compiler_params=_cparams(VMEM_LIMIT_WIDE),
    )(drn, r, proj, proj, proj, proj, cos, sin, cols, mats, cdec)


def mix_fwd(a, rn, proj, wa, wb, wo, x1):
    t, d = x1.shape
    tm = _row_tile(t)

    def body(a_ref, rn_ref, p_ref, wa_ref, wb_ref, wo_ref, x_ref, xo_ref, ba_ref, br_ref):
        ba = _dot(a_ref[...], wa_ref[...])
        br = _dot(rn_ref[...], wb_ref[...])
        sa = _sigmoid(p_ref[0, :, 0:d].astype(f32))
        sb = _sigmoid(p_ref[0, :, d:2 * d].astype(f32))
        mix = (sa * ba + sb * br).astype(bf16)
        xo_ref[...] = x_ref[...] + _dot(mix, wo_ref[...])
        ba_ref[...] = ba.astype(bf16)
        br_ref[...] = br.astype(bf16)

    row = pl.BlockSpec((tm, d), lambda i: (i, 0))
    wsp = pl.BlockSpec((d, d), lambda i: (0, 0))
    return pl.pallas_call(
        body, name="mix_fwd", grid=(t // tm,),
        in_specs=[row, row, pl.BlockSpec((1, tm, 2 * d), lambda i: (3, i, 0)), wsp, wsp, wsp, row],
        out_specs=[row, row, row],
        out_shape=[jax.ShapeDtypeStruct((t, d), f32), jax.ShapeDtypeStruct((t, d), bf16), jax.ShapeDtypeStruct((t, d), bf16)],
        compiler_params=_cparams(),
    )(a, rn, proj, wa, wb, wo, x1)


def mix_bwd_act(dx2, ba, br, proj, wa, wb, wo, dep):
    t, d = dx2.shape
    tm = _row_tile(t)

    def body(dx_ref, ba_ref, br_ref, p_ref, wa_ref, wb_ref, wo_ref, dep_ref,
             da_ref, drn_ref, dga_ref, dgb_ref, mix_ref, dba_ref, dbr_ref, dxb_ref):
        dxb = dx_ref[...].astype(bf16)
        dxb_ref[...] = dxb
        dmix = _dot_nt(dxb, wo_ref[...])
        ba = ba_ref[...].astype(f32)
        br = br_ref[...].astype(f32)
        sa = _sigmoid(p_ref[0, :, 0:d].astype(f32))
        sb = _sigmoid(p_ref[0, :, d:2 * d].astype(f32))
        mix_ref[...] = (sa * ba + sb * br).astype(bf16)
        dba = (dmix * sa).astype(bf16)
        dbr = (dmix * sb).astype(bf16)
        dba_ref[...] = dba
        dbr_ref[...] = dbr
        dga_ref[...] = (dmix * ba * sa * (1.0 - sa)).astype(bf16)
        dgb_ref[...] = (dmix * br * sb * (1.0 - sb)).astype(bf16)
        da_ref[...] = _dot_nt(dba, wa_ref[...]).astype(bf16)
        drn_ref[...] = _dot_nt(dbr, wb_ref[...]).astype(bf16)

    row = pl.BlockSpec((tm, d), lambda i: (i, 0))
    wsp = pl.BlockSpec((d, d), lambda i: (0, 0))
    return pl.pallas_call(
        body, name="mix_bwd_act", grid=(t // tm,),
        in_specs=[row, row, row, pl.BlockSpec((1, tm, 2 * d), lambda i: (3, i, 0)), wsp, wsp, wsp, _ANY],
        out_specs=[row] * 8,
        out_shape=[jax.ShapeDtypeStruct((t, d), bf16)] * 8,
        compiler_params=_cparams(),
    )(dx2, ba, br, proj, wa, wb, wo, dep)


def inproj_bwd_act(segs, win, x1, ng, dx2):
    t, d = x1.shape
    s4 = win.shape[0]
    tm = _row_tile(t)
    nseg = len(segs)

    def body(*refs):
        seg_refs = refs[:nseg]
        w_ref, x_ref, ng_ref, dx2_ref, dx1_ref, db_ref, dng_ref = refs[nseg:]
        i = pl.program_id(0)
        dh = None
        for e, sr in enumerate(seg_refs):
            sb = sr[...]
            part = _dot_nt(sb, w_ref[e // 2, :, (e % 2) * d:(e % 2 + 1) * d])
            dh = part if dh is None else dh + part
            _acc_out(db_ref.at[e], i == 0, jnp.sum(sb.astype(f32), axis=0, keepdims=True))
        _, xh, r = _rms(x_ref[...], ng_ref[...])
        dx1_ref[...] = dx2_ref[...] + _rms_bwd(dh, xh, r, ng_ref[...])
        _acc_out(dng_ref, i == 0, jnp.sum(dh * xh, axis=0, keepdims=True))

    row = pl.BlockSpec((tm, d), lambda i: (i, 0))
    vec = pl.BlockSpec((1, d), lambda i: (0, 0))
    return pl.pallas_call(
        body, name="inproj_bwd_act", grid=(t // tm,),
        in_specs=[row] * nseg + [pl.BlockSpec((s4, d, 2 * d), lambda i: (0, 0, 0), pipeline_mode=pl.Buffered(1)),
                                 row, vec, row],
        out_specs=[row, pl.BlockSpec((nseg, 1, d), lambda i: (0, 0, 0)), vec],
        out_shape=[jax.ShapeDtypeStruct((t, d), f32), jax.ShapeDtypeStruct((nseg, 1, d), f32),
                   jax.ShapeDtypeStruct((1, d), f32)],
        compiler_params=_cparams(VMEM_LIMIT_WIDE),
    )(*segs, win, x1, ng, dx2)


def _place():
    return lax.axis_index("x"), lax.axis_index("y"), lax.axis_index("c")


def _other_chips(x, y):
    return [(1 - x, y), (x, 1 - y), (1 - x, 1 - y)]


_ANY = pl.BlockSpec(memory_space=pl.ANY)


_HBM = pl.BlockSpec(memory_space=pltpu.HBM)
_SEM = pl.BlockSpec(memory_space=pltpu.SEMAPHORE)
_EFFECT = pltpu.SideEffectType.DATAFLOW_SIDE_EFFECTING


def _hbm(a):
    return pltpu.with_memory_space_constraint(a, pltpu.HBM)


def _half_rows(ref, c):
    half = ref.shape[1] // 2
    return pl.ds(pl.multiple_of(c * half, 16), half)


def _chip_copy(src, dst, send_sem, recv_sem, chip, c):
    return pltpu.make_async_remote_copy(src_ref=src, dst_ref=dst, send_sem=send_sem, recv_sem=recv_sem,
                                        device_id=(chip[0], chip[1], c), device_id_type=MESH)


def gather_start(bufs, groups, name):
    nb, ng = len(bufs), len(groups)

    def body(*refs):
        ins = refs[:nb]
        sems = refs[nb:nb + 2 * ng]
        token = refs[-1]
        x, y, c = _place()
        k = 2 * x + y
        for gi, grp in enumerate(groups):
            for wi, w in enumerate(grp):
                mine = ins[w].at[k, _half_rows(ins[w], c)]
                for j, chip in enumerate(_other_chips(x, y)):
                    _chip_copy(mine, mine, sems[2 * gi].at[3 * wi + j], sems[2 * gi + 1].at[3 * wi + j], chip, c).start()
        token[...] = jnp.zeros_like(token)

    sem_shapes = []
    for grp in groups:
        sem_shapes += [pltpu.SemaphoreType.DMA((3 * len(grp),)), pltpu.SemaphoreType.DMA((3 * len(grp),))]
    outs = pl.pallas_call(
        body, name=name,
        out_shape=sem_shapes + [pltpu.HBM(b.shape, b.dtype) for b in bufs] + [jax.ShapeDtypeStruct((8, 128), f32)],
        in_specs=[_HBM] * nb,
        out_specs=[_SEM] * (2 * ng) + [_HBM] * nb + [pl.BlockSpec(memory_space=pltpu.VMEM)],
        input_output_aliases={w: 2 * ng + w for w in range(nb)},
        compiler_params=pltpu.CompilerParams(has_side_effects=_EFFECT),
    )(*[_hbm(b) for b in bufs])
    sems = [(outs[2 * gi], outs[2 * gi + 1]) for gi in range(ng)]
    return sems, list(outs[2 * ng:2 * ng + nb]), outs[-1]


def gather_wait(bufs, sems, after, name):
    n = len(bufs)

    def body(*refs):
        ins = refs[:n]
        send_sems, recv_sems = refs[n], refs[n + 1]
        x, y, c = _place()
        k = 2 * x + y
        for wi in range(n):
            half = _half_rows(ins[wi], c)
            for j, chip in enumerate(_other_chips(x, y)):
                cp = _chip_copy(ins[wi].at[k, half], ins[wi].at[2 * chip[0] + chip[1], half], send_sems.at[3 * wi + j],
                                recv_sems.at[3 * wi + j], chip, c)
                cp.wait_send()
                cp.wait_recv()

    outs = pl.pallas_call(
        body, name=name,
        out_shape=[pltpu.HBM(b.shape, b.dtype) for b in bufs],
        in_specs=[_HBM] * n + [_SEM, _SEM, _ANY],
        out_specs=[_HBM] * n,
        input_output_aliases={i: i for i in range(n)},
        compiler_params=pltpu.CompilerParams(has_side_effects=_EFFECT),
    )(*bufs, sems[0], sems[1], after)
    return list(outs)


def gather_forward(bufs, name):
    n = len(bufs)

    def body(*refs):
        ins = refs[n:2 * n]
        send_sems, recv_sems = refs[2 * n], refs[2 * n + 1]
        x, y, c = _place()
        copies = []
        for wi in range(n):
            for j, chip in enumerate(_other_chips(x, y)):
                kp = 2 * chip[0] + chip[1]
                got = ins[wi].at[kp, _half_rows(ins[wi], c)]
                cp = pltpu.make_async_remote_copy(
                    src_ref=got, dst_ref=got, send_sem=send_sems.at[3 * wi + j], recv_sem=recv_sems.at[3 * wi + j],
                    device_id=(x, y, 1 - c), device_id_type=MESH)
                cp.start()
                copies.append((cp, wi, kp, j))
        for cp, wi, kp, j in copies:
            cp.wait_send()
            theirs = ins[wi].at[kp, _half_rows(ins[wi], 1 - c)]
            pltpu.make_async_remote_copy(
                src_ref=theirs, dst_ref=theirs, send_sem=send_sems.at[3 * wi + j], recv_sem=recv_sems.at[3 * wi + j],
                device_id=(x, y, 1 - c), device_id_type=MESH).wait_recv()

    outs = pl.pallas_call(
        body, name=name,
        out_shape=[jax.ShapeDtypeStruct(b.shape, b.dtype) for b in bufs],
        in_specs=[_ANY] * n, out_specs=[_ANY] * n,
        input_output_aliases={i: i for i in range(n)},
        scratch_shapes=[pltpu.SemaphoreType.DMA((3 * n,)), pltpu.SemaphoreType.DMA((3 * n,))],
    )(*bufs)
    return list(outs)


def forward_start(bufs, name):
    n = len(bufs)

    def body(*refs):
        x, y, c = _place()
        for wi in range(n):
            for j, chip in enumerate(_other_chips(x, y)):
                got = refs[wi].at[2 * chip[0] + chip[1], _half_rows(refs[wi], c)]
                _sibling_copy(got, got, refs[n].at[3 * wi + j], refs[n + 1].at[3 * wi + j]).start()
        refs[-1][...] = jnp.zeros_like(refs[-1])

    return _split_start(body, name, 3 * n, list(bufs))


def forward_wait(bufs, sems, after, name):
    n = len(bufs)

    def body(*refs):
        x, y, c = _place()
        for wi in range(n):
            for j, chip in enumerate(_other_chips(x, y)):
                kp = 2 * chip[0] + chip[1]
                got = refs[wi].at[kp, _half_rows(refs[wi], c)]
                theirs = refs[wi].at[kp, _half_rows(refs[wi], 1 - c)]
                _sibling_copy(got, got, refs[n].at[3 * wi + j], refs[n + 1].at[3 * wi + j]).wait_send()
                _sibling_copy(theirs, theirs, refs[n].at[3 * wi + j], refs[n + 1].at[3 * wi + j]).wait_recv()

    return _split_wait(body, name, list(bufs), sems, after)


def exchange_start(grads, name):
    n = len(grads)
    lands = [lax.empty((3,) + g.shape[1:], g.dtype) for g in grads]

    def body(*refs):
        ins = refs[:n]
        land = refs[n:2 * n]
        send_sems, recv_sems = refs[2 * n], refs[2 * n + 1]
        token = refs[-1]
        x, y, c = _place()
        for wi in range(n):
            for j, chip in enumerate(_other_chips(x, y)):
                _chip_copy(ins[wi].at[2 * chip[0] + chip[1]], land[wi].at[j], send_sems.at[3 * wi + j],
                           recv_sems.at[3 * wi + j], chip, c).start()
        token[...] = jnp.zeros_like(token)

    outs = pl.pallas_call(
        body, name=name,
        out_shape=[pltpu.SemaphoreType.DMA((3 * n,)), pltpu.SemaphoreType.DMA((3 * n,))]
        + [pltpu.HBM(g.shape, g.dtype) for g in grads] + [pltpu.HBM(l.shape, l.dtype) for l in lands]
        + [jax.ShapeDtypeStruct((8, 128), f32)],
        in_specs=[_HBM] * (2 * n),
        out_specs=[_SEM, _SEM] + [_HBM] * (2 * n) + [pl.BlockSpec(memory_space=pltpu.VMEM)],
        input_output_aliases={i: 2 + i for i in range(2 * n)},
        compiler_params=pltpu.CompilerParams(has_side_effects=_EFFECT),
    )(*[_hbm(g) for g in grads], *[_hbm(l) for l in lands])
    return (outs[0], outs[1]), list(outs[2:2 + n]), list(outs[2 + n:2 + 2 * n]), outs[-1]


def exchange_wait(grads, lands, sems, after, name):
    n = len(grads)

    def body(*refs):
        ins = refs[:n]
        land = refs[n:2 * n]
        send_sems, recv_sems = refs[2 * n], refs[2 * n + 1]
        x, y, c = _place()
        for wi in range(n):
            for j, chip in enumerate(_other_chips(x, y)):
                cp = _chip_copy(ins[wi].at[2 * chip[0] + chip[1]], land[wi].at[j], send_sems.at[3 * wi + j],
                                recv_sems.at[3 * wi + j], chip, c)
                cp.wait_send()
                cp.wait_recv()

    outs = pl.pallas_call(
        body, name=name,
        out_shape=[pltpu.HBM(g.shape, g.dtype) for g in grads] + [pltpu.HBM(l.shape, l.dtype) for l in lands],
        in_specs=[_HBM] * (2 * n) + [_SEM, _SEM, _ANY],
        out_specs=[_HBM] * (2 * n),
        input_output_aliases={i: i for i in range(2 * n)},
        compiler_params=pltpu.CompilerParams(has_side_effects=_EFFECT),
    )(*grads, *lands, sems[0], sems[1], after)
    return list(outs[:n]), list(outs[n:])


def _split_start(body, name, n_sems, operands):
    n = len(operands)
    outs = pl.pallas_call(
        body, name=name,
        out_shape=[pltpu.SemaphoreType.DMA((n_sems,)), pltpu.SemaphoreType.DMA((n_sems,))]
        + [pltpu.HBM(o.shape, o.dtype) for o in operands] + [jax.ShapeDtypeStruct((8, 128), f32)],
        in_specs=[_HBM] * n,
        out_specs=[_SEM, _SEM] + [_HBM] * n + [pl.BlockSpec(memory_space=pltpu.VMEM)],
        input_output_aliases={i: 2 + i for i in range(n)},
        compiler_params=pltpu.CompilerParams(has_side_effects=_EFFECT),
    )(*[_hbm(o) for o in operands])
    return (outs[0], outs[1]), list(outs[2:2 + n]), outs[-1]


def _split_wait(body, name, operands, sems, after):
    n = len(operands)
    outs = pl.pallas_call(
        body, name=name,
        out_shape=[pltpu.HBM(o.shape, o.dtype) for o in operands],
        in_specs=[_HBM] * n + [_SEM, _SEM, _ANY],
        out_specs=[_HBM] * n,
        input_output_aliases={i: i for i in range(n)},
        compiler_params=pltpu.CompilerParams(has_side_effects=_EFFECT),
    )(*operands, sems[0], sems[1], after)
    return list(outs)


def _sibling_copy(src, dst, send_sem, recv_sem):
    x, y, c = _place()
    return pltpu.make_async_remote_copy(src_ref=src, dst_ref=dst, send_sem=send_sem, recv_sem=recv_sem,
                                        device_id=(x, y, 1 - c), device_id_type=MESH)


def swap_start(parts, name):
    n = len(parts)

    def body(*refs):
        for w in range(n):
            _sibling_copy(refs[w], refs[n + w], refs[2 * n].at[w], refs[2 * n + 1].at[w]).start()
        refs[-1][...] = jnp.zeros_like(refs[-1])

    sems, ops, token = _split_start(body, name, n, list(parts) + [lax.empty(p.shape, p.dtype) for p in parts])
    return sems, ops[:n], ops[n:], token


def swap_wait(parts, lands, sems, after, name):
    n = len(parts)

    def body(*refs):
        for w in range(n):
            cp = _sibling_copy(refs[w], refs[n + w], refs[2 * n].at[w], refs[2 * n + 1].at[w])
            cp.wait_send()
            cp.wait_recv()

    outs = _split_wait(body, name, list(parts) + list(lands), sems, after)
    return outs[:n], outs[n:]


def _all_peers(x, y, c):
    return [(1 - x if m & 4 else x, 1 - y if m & 2 else y, 1 - c if m & 1 else c) for m in range(1, N_DEV)]


def small_start(block):
    land = jnp.broadcast_to(block[None], (N_DEV,) + block.shape)

    def body(b_ref, land_ref, send_sems, recv_sems, b_thru, land_thru, token):
        x, y, c = _place()
        me = 4 * x + 2 * y + c
        for m, peer in enumerate(_all_peers(x, y, c)):
            pltpu.make_async_remote_copy(src_ref=b_ref, dst_ref=land_ref.at[me], send_sem=send_sems.at[m],
                                         recv_sem=recv_sems.at[m], device_id=peer, device_id_type=MESH).start()
        token[...] = jnp.zeros_like(token)

    sems, ops, token = _split_start(body, "small_start", N_DEV - 1, [block, land])
    return sems, ops[0], ops[1], token


def small_wait(block, land, sems, after):
    def body(b_ref, land_ref, send_sems, recv_sems, after_ref, b_thru, land_thru):
        x, y, c = _place()
        for m, (px, py, pc) in enumerate(_all_peers(x, y, c)):
            cp = pltpu.make_async_remote_copy(src_ref=b_ref, dst_ref=land_ref.at[4 * px + 2 * py + pc],
                                              send_sem=send_sems.at[m], recv_sem=recv_sems.at[m],
                                              device_id=(px, py, pc), device_id_type=MESH)
            cp.wait_send()
            cp.wait_recv()

    return _split_wait(body, "small_wait", [block, land], sems, after)[1]


def _adamw(w, g, m, v):
    m = ADAM_B1 * m + (1.0 - ADAM_B1) * g
    v = ADAM_B2 * v + (1.0 - ADAM_B2) * (g * g)
    m_hat = m / (1.0 - ADAM_B1 ** ADAM_STEP)
    v_hat = v / (1.0 - ADAM_B2 ** ADAM_STEP)
    delta = -ADAM_LR * (m_hat / (jnp.sqrt(v_hat) + ADAM_EPS) + ADAM_WD * w)
    return delta, m, v


EW_BLOCK_BYTES = 2 * 1024 * 1024


def _ew_tile(rows, cols):
    for cand in (512, 352, 256, 176, 128, 64, 32, 16, 8):
        if rows % cand == 0 and cand * cols * 4 <= EW_BLOCK_BYTES:
            return cand
    return rows


def sum_partials(chip, own, land, name):
    _, r, c = own.shape
    tr = _ew_tile(r, c)

    def body(k_ref, own_ref, p_ref, o_ref):
        o_ref[...] = ((own_ref[0].astype(f32) + p_ref[0].astype(f32)) + p_ref[1].astype(f32)) + p_ref[2].astype(f32)

    return pl.pallas_call(
        body, name=name,
        grid_spec=pltpu.PrefetchScalarGridSpec(
            num_scalar_prefetch=1, grid=(r // tr,),
            in_specs=[pl.BlockSpec((1, tr, c), lambda i, k: (k[0], i, 0)), pl.BlockSpec((3, tr, c), lambda i, k: (0, i, 0))],
            out_specs=pl.BlockSpec((tr, c), lambda i, k: (i, 0))),
        out_shape=jax.ShapeDtypeStruct((r, c), f32),
        compiler_params=_cparams(),
    )(chip, own, land)


def adamw_shard(p_mine, p_sibling, w, m, v, name):
    r, c = w.shape
    tr = _ew_tile(r, c)

    def body(a_ref, b_ref, w_ref, m_ref, v_ref, g_ref, d_ref, mo_ref, vo_ref):
        g = a_ref[...] + b_ref[...]
        delta, mn, vn = _adamw(w_ref[...], g, m_ref[...], v_ref[...])
        g_ref[...] = g
        d_ref[...] = delta
        mo_ref[...] = mn
        vo_ref[...] = vn

    blk = pl.BlockSpec((tr, c), lambda i: (i, 0))
    return pl.pallas_call(
        body, name=name, grid=(r // tr,),
        in_specs=[blk] * 5, out_specs=[blk] * 4,
        out_shape=[jax.ShapeDtypeStruct((r, c), f32)] * 4,
        compiler_params=_cparams(),
    )(p_mine, p_sibling, w, m, v)


def adamw_small(g8, w, m, v):
    _, r, lanes = g8.shape

    def body(g_ref, w_ref, m_ref, v_ref, go_ref, d_ref, mo_ref, vo_ref):
        g = g_ref[0]
        for i in range(1, N_DEV):
            g = g + g_ref[i]
        delta, mn, vn = _adamw(w_ref[...], g, m_ref[...], v_ref[...])
        go_ref[...] = g
        d_ref[...] = delta
        mo_ref[...] = mn
        vo_ref[...] = vn

    return pl.pallas_call(
        body, name="adamw_small",
        out_shape=[jax.ShapeDtypeStruct((r, lanes), f32)] * 4,
        compiler_params=_cparams(),
    )(g8, w, m, v)


def _size(shape):
    n = 1
    for e in shape:
        n *= e
    return n


def _pack_rows(shapes):
    rows = [-(-_size(s) // 1024) * 8 for s in shapes]
    return rows, sum(rows)


def _pack(arrs, shapes):
    rows, _ = _pack_rows(shapes)
    parts = [jnp.pad(a.reshape(-1).astype(f32), (0, r * 128 - _size(s))).reshape(r, 128)
             for a, s, r in zip(arrs, shapes, rows)]
    return jnp.concatenate(parts, axis=0)


def _unpack(block, shapes):
    rows, _ = _pack_rows(shapes)
    out, off = [], 0
    for s, r in zip(shapes, rows):
        out.append(block[off:off + r].reshape(-1)[:_size(s)].reshape(s))
        off += r
    return out


TRANSPOSED = ("ffn1_w_gate", "ffn1_w_up", "ffn2_w_gate", "ffn2_w_up")


def _shard2d(a, n):
    return a[0].T if n in TRANSPOSED else a[0]


def _unshard(a, n):
    return (a.T if n in TRANSPOSED else a)[None]


BIG = ("ffn1_w_gate", "ffn1_w_up", "ffn1_w_down", "w_in", "w_branch_a", "w_branch_b", "w_out",
       "ffn2_w_gate", "ffn2_w_up", "ffn2_w_down")
SMALL = ("ffn1_norm", "mix_norm", "b_in", "sgu_norm_g", "sgu_norm_b", "sgu_w_s", "sgu_b_s", "ret_decay_logit",
         "ffn2_norm", "final_norm")
WEIGHTS = ("ffn1_norm", "ffn1_w_gate", "ffn1_w_up", "ffn1_w_down", "mix_norm", "w_in", "b_in", "sgu_norm_g",
           "sgu_norm_b", "sgu_w_s", "sgu_b_s", "ret_decay_logit", "w_branch_a", "w_branch_b", "w_out", "ffn2_norm",
           "ffn2_w_gate", "ffn2_w_up", "ffn2_w_down", "final_norm")


def kernel(x, ffn1_norm, ffn1_w_gate, ffn1_w_up, ffn1_w_down, mix_norm, w_in, b_in, sgu_norm_g, sgu_norm_b, sgu_w_s, sgu_b_s, ret_decay_logit, w_branch_a, w_branch_b, w_out, ffn2_norm, ffn2_w_gate, ffn2_w_up, ffn2_w_down, final_norm, loss_target, m_ffn1_norm, m_ffn1_w_gate, m_ffn1_w_up, m_ffn1_w_down, m_mix_norm, m_w_in, m_b_in, m_sgu_norm_g, m_sgu_norm_b, m_sgu_w_s, m_sgu_b_s, m_ret_decay_logit, m_w_branch_a, m_w_branch_b, m_w_out, m_ffn2_norm, m_ffn2_w_gate, m_ffn2_w_up, m_ffn2_w_down, m_final_norm, v_ffn1_norm, v_ffn1_w_gate, v_ffn1_w_up, v_ffn1_w_down, v_mix_norm, v_w_in, v_b_in, v_sgu_norm_g, v_sgu_norm_b, v_sgu_w_s, v_sgu_b_s, v_ret_decay_logit, v_w_branch_a, v_w_branch_b, v_w_out, v_ffn2_norm, v_ffn2_w_gate, v_ffn2_w_up, v_ffn2_w_down, v_final_norm):
    p = dict(ffn1_norm=ffn1_norm, ffn1_w_gate=ffn1_w_gate, ffn1_w_up=ffn1_w_up, ffn1_w_down=ffn1_w_down,
             mix_norm=mix_norm, w_in=w_in, b_in=b_in, sgu_norm_g=sgu_norm_g, sgu_norm_b=sgu_norm_b, sgu_w_s=sgu_w_s,
             sgu_b_s=sgu_b_s, ret_decay_logit=ret_decay_logit, w_branch_a=w_branch_a, w_branch_b=w_branch_b,
             w_out=w_out, ffn2_norm=ffn2_norm, ffn2_w_gate=ffn2_w_gate, ffn2_w_up=ffn2_w_up, ffn2_w_down=ffn2_w_down,
             final_norm=final_norm)
    mom = dict(ffn1_norm=m_ffn1_norm, ffn1_w_gate=m_ffn1_w_gate, ffn1_w_up=m_ffn1_w_up, ffn1_w_down=m_ffn1_w_down,
               mix_norm=m_mix_norm, w_in=m_w_in, b_in=m_b_in, sgu_norm_g=m_sgu_norm_g, sgu_norm_b=m_sgu_norm_b,
               sgu_w_s=m_sgu_w_s, sgu_b_s=m_sgu_b_s, ret_decay_logit=m_ret_decay_logit, w_branch_a=m_w_branch_a,
               w_branch_b=m_w_branch_b, w_out=m_w_out, ffn2_norm=m_ffn2_norm, ffn2_w_gate=m_ffn2_w_gate,
               ffn2_w_up=m_ffn2_w_up, ffn2_w_down=m_ffn2_w_down, final_norm=m_final_norm)
    var = dict(ffn1_norm=v_ffn1_norm, ffn1_w_gate=v_ffn1_w_gate, ffn1_w_up=v_ffn1_w_up, ffn1_w_down=v_ffn1_w_down,
               mix_norm=v_mix_norm, w_in=v_w_in, b_in=v_b_in, sgu_norm_g=v_sgu_norm_g, sgu_norm_b=v_sgu_norm_b,
               sgu_w_s=v_sgu_w_s, sgu_b_s=v_sgu_b_s, ret_decay_logit=v_ret_decay_logit, w_branch_a=v_w_branch_a,
               w_branch_b=v_w_branch_b, w_out=v_w_out, ffn2_norm=v_ffn2_norm, ffn2_w_gate=v_ffn2_w_gate,
               ffn2_w_up=v_ffn2_w_up, ffn2_w_down=v_ffn2_w_down, final_norm=v_final_norm)

    xs = x[0]
    tgt = loss_target[0]
    t, d = xs.shape
    dk = d // RET_HEADS

    shards2d = {n: _shard2d(p[n], n) for n in BIG}
    chip = (2 * lax.axis_index("x") + lax.axis_index("y")).astype(jnp.int32).reshape(1)
    groups = {"ffn1": ("ffn1_w_gate", "ffn1_w_up", "ffn1_w_down"), "in": ("w_in",),
              "mix": ("w_branch_a", "w_branch_b", "w_out"), "ffn2": ("ffn2_w_gate", "ffn2_w_up", "ffn2_w_down")}
    def own_slot(n, zero):
        sh = shards2d[n].astype(bf16) + zero
        return lax.dynamic_update_index_in_dim(lax.empty((N_CHIPS,) + sh.shape, bf16), sh, chip[0], 0)

    sems, bufs, tok = gather_start([own_slot(n, jnp.zeros((), bf16)) for n in groups["ffn1"]], [[0, 1, 2]],
                                   "gather_start_ffn1")
    gsem = {"ffn1": sems[0]}
    pending = dict(zip(groups["ffn1"], bufs))
    rest = [n for g in ("in", "mix", "ffn2") for n in groups[g]]
    sems, bufs, tok_rest = gather_start([own_slot(n, tok[0, 0].astype(bf16)) for n in rest],
                                 [[rest.index(n) for n in groups[g]] for g in ("in", "mix", "ffn2")], "gather_start_rest")
    gsem.update(zip(("in", "mix", "ffn2"), sems))
    pending.update(zip(rest, bufs))

    def arrive(gs, after):
        got = []
        for g in gs:
            got += gather_wait([pending[n] for n in groups[g]], gsem[g], after, "gather_wait_" + g)
        return gather_forward(got, "gather_forward_" + gs[0])

    bin4 = b_in.reshape(N_CHIPS, 1, 2 * d)
    ws_b = sgu_w_s[0].astype(bf16)
    bs_c = sgu_b_s[0][:, :, None]
    cols, mats, cdec, cos, sin = retention_constants(ret_decay_logit[0], t, dk, tok_rest[0, 0])

    wg1, wu1, wd1 = [_pair_shards(w) for w in arrive(["ffn1"], cos)]
    x1, g1, u1 = ffn_fwd(xs, ffn1_norm, wg1, wu1, wd1, "ffn1_fwd")
    win, = arrive(["in"], x1)
    proj, hb2 = inproj_fwd(x1, mix_norm, win, bin4, cos, sin)
    late = []
    for g in ("mix", "ffn2"):
        late += gather_wait([pending[n] for n in groups[g]], gsem[g], proj, "gather_wait_" + g)
    fsems, late, ftok = forward_start(late, "forward_start_mix")
    a = sgu_fwd(proj, sgu_norm_g, sgu_norm_b, ws_b, bs_c, ftok)
    r, rn = ret_fwd(proj, cols, mats, cdec, a)
    wa, wb, wo, wg2, wu2, wd2 = forward_wait(late, fsems, rn, "forward_wait_mix")
    wa, wb, wo = [w.reshape(d, d) for w in (wa, wb, wo)]
    wg2, wu2, wd2 = [_pair_shards(w) for w in (wg2, wu2, wd2)]
    x2, ba, br = mix_fwd(a, rn, proj, wa, wb, wo, x1)
    loss_blk, dx3, d_final, g2, u2 = ffn_fwd_loss(x2, ffn2_norm, wg2, wu2, wd2, final_norm.reshape(1, d), tgt, "ffn2_fwd")

    sent, swaps = {}, {}
    out_g, out_d, out_m, out_v = {}, {}, {}, {}

    def reduce_plane(g, after):
        gsems, own, lands, _ = sent[g]
        own, lands = exchange_wait(own, lands, gsems, after, "exchange_wait_" + g)
        plane = [sum_partials(chip, o, l, "sum_" + n) for n, o, l in zip(groups[g], own, lands)]
        swaps[g] = swap_start(plane, "swap_start_" + g)
        return swaps[g][3]

    def update(g, after):
        ssems, plane, lands, _ = swaps[g]
        plane, other = swap_wait(plane, lands, ssems, after, "swap_wait_" + g)
        for n, mine, sib in zip(groups[g], plane, other):
            res = adamw_shard(mine, sib, shards2d[n], _shard2d(mom[n], n), _shard2d(var[n], n), "adamw_" + n)
            out_g[n], out_d[n], out_m[n], out_v[n] = [_unshard(o, n) for o in res]
        return res[0]

    dx2, dg2, du2, act2, hb3, dyb2, d_ffn2n = ffn_bwd_act(dx3, x2, ffn2_norm, g2, u2, wg2, wu2, wd2, "ffn2_bwd_act", tok)
    sent["ffn2"] = exchange_start(ffn_weight_grads(hb3, dyb2, dg2, du2, act2, "ffn2_grad", tok), "exchange_start_ffn2")
    da, drn, dga, dgb, mixb, dba, dbr, dx2b = mix_bwd_act(dx2, ba, br, proj, wa, wb, wo, sent["ffn2"][3])
    tg = min(t, 2048)
    row = pl.BlockSpec((tg, d), lambda s, i: (i, 0))

    def square_grad(xa, ya, name):
        return tn_matmul(xa, [ya], row, [row], 1, d, [d], t, tg, name, tok).reshape(N_CHIPS, d // N_CHIPS, d)

    g_mix = [square_grad(a, dba, "grad_w_branch_a"), square_grad(rn, dbr, "grad_w_branch_b"),
             square_grad(mixb, dx2b, "grad_w_out")]
    dua, dva, d_ws, d_bs, d_sng, d_snb = sgu_bwd(da, proj, sgu_norm_g, sgu_norm_b, ws_b, bs_c, sent["ffn2"][3])
    dq, dkr, dv, dgr, dlg = ret_bwd(drn, r, proj, cols, mats, cdec, cos, sin)
    segs = [dua, dva, dq, dkr, dv, dgr, dga, dgb]
    dx1, d_bin, d_mixn = inproj_bwd_act(segs, win, x1, mix_norm, dx2)
    g_in = None
    for s in range(N_CHIPS):
        g_in = tn_matmul(hb2, [segs[2 * s], segs[2 * s + 1]], row, [row, row], 1, d, [d, d], t, tg, "grad_w_in_%d" % s,
                         tok, (g_in, s, N_CHIPS))
    groups["mix_in"] = groups["mix"] + groups["in"]
    sent["mix_in"] = exchange_start(g_mix + [g_in], "exchange_start_mix_in")
    grad_x, dg1, du1, act1, hb1, dyb1, d_ffn1n = ffn_bwd_act(dx1, xs, ffn1_norm, g1, u1, wg1, wu1, wd1, "ffn1_bwd_act",
                                                              sent["mix_in"][3])
    dlogit = dlg[:, 0:2, 0].T * jax.nn.sigmoid(-ret_decay_logit[0].astype(f32))
    small_g = dict(ffn1_norm=d_ffn1n, mix_norm=d_mixn, b_in=d_bin, sgu_norm_g=d_sng, sgu_norm_b=d_snb, sgu_w_s=d_ws,
                   sgu_b_s=d_bs, ret_decay_logit=dlogit, ffn2_norm=d_ffn2n, final_norm=d_final)
    shapes = [p[n].shape for n in SMALL]
    small_sems, small_blk, small_land, small_tok = small_start(_pack([small_g[n] for n in SMALL], shapes))

    def send_one(which, grad):
        n = "ffn1_" + which
        groups[n] = (n,)
        sent[n] = exchange_start([grad], "exchange_start_" + n)
        return sent[n][3]

    ffn_weight_grads(hb1, dyb1, dg1, du1, act1, "ffn1_grad", small_tok, send_one)

    after = reduce_plane("ffn2", sent["ffn1_w_down"][3])
    after = reduce_plane("mix_in", after)
    after = update("ffn2", after)
    g8 = small_wait(small_blk, small_land, small_sems, after)
    sg, sd, sm, sv = adamw_small(g8, _pack([p[n] for n in SMALL], shapes), _pack([mom[n] for n in SMALL], shapes),
                                 _pack([var[n] for n in SMALL], shapes))
    for res, blockv in ((out_g, sg), (out_d, sd), (out_m, sm), (out_v, sv)):
        for n, val in zip(SMALL, _unpack(blockv, shapes)):
            res[n] = val
    after = update("mix_in", sg)
    after = reduce_plane("ffn1_w_gate", after)
    after = reduce_plane("ffn1_w_up", after)
    after = update("ffn1_w_gate", after)
    after = reduce_plane("ffn1_w_down", after)
    after = update("ffn1_w_up", after)
    update("ffn1_w_down", after)

    loss = lax.psum(loss_blk[0, 0], ("x", "y", "c"))
    return (loss, grad_x[None], *[out_g[n] for n in WEIGHTS], *[out_d[n] for n in WEIGHTS],
            *[out_m[n] for n in WEIGHTS], *[out_v[n] for n in WEIGHTS])
```

```python
import jax
import jax.numpy as jnp
from jax import lax
from jax.experimental import pallas as pl
from jax.experimental.pallas import tpu as pltpu

f32 = jnp.float32
bf16 = jnp.bfloat16

SGU_CHUNK = 128
CHUNK = 256
RET_HEADS = 4
SGU_GROUPS = 4
ROPE_BASE = 10000.0
NORM_EPS = 1e-6
ADAM_LR = 0.001
ADAM_B1 = 0.9
ADAM_B2 = 0.999
ADAM_EPS = 1e-08
ADAM_WD = 0.01
ADAM_STEP = 10
N_CHIPS = 4
N_DEV = 8
MESH = pl.DeviceIdType.MESH
VMEM_LIMIT = 52 * 1024 * 1024
VMEM_LIMIT_WIDE = 62 * 1024 * 1024

_NT = (((1,), (1,)), ((), ()))
_TN = (((0,), (0,)), ((), ()))


def _cparams(limit=None):
    return pltpu.CompilerParams(vmem_limit_bytes=VMEM_LIMIT if limit is None else limit)


def _row_tile(t):
    return 512 if t >= 2048 else t // 2


def _dot(a, b):
    return jnp.dot(a, b, preferred_element_type=f32)


def _dot_nt(a, b):
    return lax.dot_general(a, b, _NT, preferred_element_type=f32)


def _dot_tn(a, b):
    return lax.dot_general(a, b, _TN, preferred_element_type=f32)


def _rms(x, g):
    r = lax.rsqrt(jnp.mean(x * x, axis=-1, keepdims=True) + NORM_EPS)
    xh = x * r
    return xh * g, xh, r


def _rms_bwd(dy, xh, r, g):
    dxh = dy * g
    return r * (dxh - xh * jnp.mean(dxh * xh, axis=-1, keepdims=True))


def _sigmoid(x):
    return jax.nn.sigmoid(x)


def _dsilu(g, sg):
    return sg * (1.0 + g * (1.0 - sg))


def _gelu(x):
    return 0.5 * x * (1.0 + lax.erf(x * 0.7071067811865476))


def _dgelu(x):
    return 0.5 * (1.0 + lax.erf(x * 0.7071067811865476)) + x * jnp.exp(-0.5 * x * x) * 0.3989422804014327


def _acc_out(ref, first, val):
    @pl.when(first)
    def _():
        ref[...] = val

    @pl.when(jnp.logical_not(first))
    def _():
        ref[...] += val


def _ffn_tile(t):
    return 256 if t >= 2048 else t // 2


def _ffn_fwd_rows(xx, ng_ref, wg_ref, wu_ref, wd_ref, g_ref, u_ref):
    y, _, _ = _rms(xx, ng_ref[...])
    h = y.astype(bf16)
    acc = None
    for s in range(wg_ref.shape[0]):
        g = _dot_nt(h, wg_ref[s])
        u = _dot_nt(h, wu_ref[s])
        g_ref[s] = g.astype(bf16)
        u_ref[s] = u.astype(bf16)
        part = _dot((g * _sigmoid(g) * u).astype(bf16), wd_ref[s])
        acc = part if acc is None else acc + part
    return xx + 0.5 * acc


def ffn_fwd(x, ng, wg, wu, wd, name):
    t, d = x.shape
    ns, fs, _ = wg.shape
    tm = _ffn_tile(t)

    def body(x_ref, ng_ref, wg_ref, wu_ref, wd_ref, xo_ref, g_ref, u_ref):
        xo_ref[...] = _ffn_fwd_rows(x_ref[...], ng_ref, wg_ref, wu_ref, wd_ref, g_ref, u_ref)

    row = pl.BlockSpec((tm, d), lambda i: (i, 0))
    shard = pl.BlockSpec((ns, tm, fs), lambda i: (0, i, 0))
    wspec = pl.BlockSpec((ns, fs, d), lambda i: (0, 0, 0), pipeline_mode=pl.Buffered(1))
    return pl.pallas_call(
        body, name=name, grid=(t // tm,),
        in_specs=[row, pl.BlockSpec((1, d), lambda i: (0, 0)), wspec, wspec, wspec],
        out_specs=[row, shard, shard],
        out_shape=[jax.ShapeDtypeStruct((t, d), f32), jax.ShapeDtypeStruct((ns, t, fs), bf16),
                   jax.ShapeDtypeStruct((ns, t, fs), bf16)],
        compiler_params=_cparams(),
    )(x, ng, wg, wu, wd)


def ffn_fwd_loss(x, ng, wg, wu, wd, fng, tgt, name):
    t, d = x.shape
    ns, fs, _ = wg.shape
    tm = _ffn_tile(t)

    def body(x_ref, ng_ref, wg_ref, wu_ref, wd_ref, fng_ref, t_ref, loss_ref, dx_ref, dfn_ref, g_ref, u_ref):
        i = pl.program_id(0)
        x3 = _ffn_fwd_rows(x_ref[...], ng_ref, wg_ref, wu_ref, wd_ref, g_ref, u_ref)
        y, xh, r = _rms(x3, fng_ref[...])
        diff = y - t_ref[...]
        part = 0.5 * jnp.sum(jnp.sum(diff * diff, axis=0, keepdims=True), axis=1, keepdims=True) / d
        _acc_out(loss_ref, i == 0, jnp.broadcast_to(part, (1, 128)))
        dy = diff * (1.0 / d)
        dx_ref[...] = _rms_bwd(dy, xh, r, fng_ref[...])
        _acc_out(dfn_ref, i == 0, jnp.sum(dy * xh, axis=0, keepdims=True))

    row = pl.BlockSpec((tm, d), lambda i: (i, 0))
    vec = pl.BlockSpec((1, d), lambda i: (0, 0))
    shard = pl.BlockSpec((ns, tm, fs), lambda i: (0, i, 0))
    wspec = pl.BlockSpec((ns, fs, d), lambda i: (0, 0, 0), pipeline_mode=pl.Buffered(1))
    return pl.pallas_call(
        body, name=name, grid=(t // tm,),
        in_specs=[row, vec, wspec, wspec, wspec, vec, row],
        out_specs=[pl.BlockSpec((1, 128), lambda i: (0, 0)), row, vec, shard, shard],
        out_shape=[jax.ShapeDtypeStruct((1, 128), f32), jax.ShapeDtypeStruct((t, d), f32), jax.ShapeDtypeStruct((1, d), f32),
                   jax.ShapeDtypeStruct((ns, t, fs), bf16), jax.ShapeDtypeStruct((ns, t, fs), bf16)],
        compiler_params=_cparams(),
    )(x, ng, wg, wu, wd, fng, tgt)


def ffn_bwd_act(dxo, x, ng, g, u, wg, wu, wd, name, dep):
    t, d = x.shape
    ns, fs, _ = wg.shape
    tm = _ffn_tile(t)

    def body(dxo_ref, x_ref, ng_ref, g_ref, u_ref, wg_ref, wu_ref, wd_ref, dep_ref,
             dx_ref, dg_ref, du_ref, act_ref, hb_ref, dyb_ref, dng_ref):
        i = pl.program_id(0)
        dxo = dxo_ref[...]
        dyb = (0.5 * dxo).astype(bf16)
        dyb_ref[...] = dyb
        dh = None
        for s in range(ns):
            dact = _dot_nt(dyb, wd_ref[s])
            gg = g_ref[s].astype(f32)
            uu = u_ref[s].astype(f32)
            sg = _sigmoid(gg)
            sil = gg * sg
            dgb = (dact * uu * _dsilu(gg, sg)).astype(bf16)
            dub = (dact * sil).astype(bf16)
            dg_ref[s] = dgb
            du_ref[s] = dub
            act_ref[s] = (sil * uu).astype(bf16)
            part = _dot(dgb, wg_ref[s]) + _dot(dub, wu_ref[s])
            dh = part if dh is None else dh + part
        y, xh, r = _rms(x_ref[...], ng_ref[...])
        hb_ref[...] = y.astype(bf16)
        dx_ref[...] = dxo + _rms_bwd(dh, xh, r, ng_ref[...])
        _acc_out(dng_ref, i == 0, jnp.sum(dh * xh, axis=0, keepdims=True))

    row = pl.BlockSpec((tm, d), lambda i: (i, 0))
    shard = pl.BlockSpec((ns, tm, fs), lambda i: (0, i, 0))
    wspec = pl.BlockSpec((ns, fs, d), lambda i: (0, 0, 0), pipeline_mode=pl.Buffered(1))
    vec = pl.BlockSpec((1, d), lambda i: (0, 0))
    return pl.pallas_call(
        body, name=name, grid=(t // tm,),
        in_specs=[row, row, vec, shard, shard, wspec, wspec, wspec, _ANY],
        out_specs=[row, shard, shard, shard, row, row, vec],
        out_shape=[jax.ShapeDtypeStruct((t, d), f32)] + [jax.ShapeDtypeStruct((ns, t, fs), bf16)] * 3
        + [jax.ShapeDtypeStruct((t, d), bf16)] * 2 + [jax.ShapeDtypeStruct((1, d), f32)],
        compiler_params=_cparams(VMEM_LIMIT_WIDE),
    )(dxo, x, ng, g, u, wg, wu, wd, dep)


def tn_matmul(xs, ys, x_spec, y_specs, n_shards, k1, k2s, t, tm, name, dep, into=None):
    k2 = sum(k2s)
    ny = len(ys)

    def body(*refs):
        x_ref = refs[0]
        y_refs = refs[1:1 + ny]
        o_ref, acc = refs[-2], refs[-1]
        i = pl.program_id(1)
        xb = x_ref[0] if len(x_ref.shape) == 3 else x_ref[...]
        off = 0
        for y_ref, w in zip(y_refs, k2s):
            yb = y_ref[0] if len(y_ref.shape) == 3 else y_ref[...]
            part = _dot_tn(xb, yb)
            sl = (slice(None), slice(off, off + w))

            @pl.when(i == 0)
            def _(part=part, sl=sl):
                acc[sl] = part

            @pl.when(i > 0)
            def _(part=part, sl=sl):
                acc[sl] += part

            off += w

        @pl.when(i == t // tm - 1)
        def _():
            o_ref[0] = acc[...].astype(bf16)

    if into is None:
        slot0, total, extra, aliases = 0, n_shards, [], {}
    else:
        buf, slot0, total = into
        extra = [] if buf is None else [buf]
        aliases = {} if buf is None else {2 + ny: 0}
    return pl.pallas_call(
        body, name=name, grid=(n_shards, t // tm),
        in_specs=[x_spec] + list(y_specs) + [_ANY] * (1 + len(extra)),
        out_specs=pl.BlockSpec((1, k1, k2), lambda s, i: (slot0 + s, 0, 0)),
        out_shape=jax.ShapeDtypeStruct((total, k1, k2), bf16),
        scratch_shapes=[pltpu.VMEM((k1, k2), f32)],
        input_output_aliases=aliases,
        compiler_params=_cparams(),
    )(xs, *ys, dep, *extra)


def _pair_shards(w):
    s4, fs, d = w.shape
    return w.reshape(s4 // 2, 2 * fs, d)


def ffn_weight_grads(hb, dyb, dg, du, act, name, dep, each=None):
    t, d = hb.shape
    s2, _, fs2 = dg.shape
    tm = t
    row = pl.BlockSpec((tm, d), lambda s, i: (i, 0))
    shard = pl.BlockSpec((1, tm, fs2), lambda s, i: (s, i, 0))
    grads = []
    for xa, ya, which in ((dg, hb, "w_gate"), (du, hb, "w_up"), (act, dyb, "w_down")):
        g = tn_matmul(xa, [ya], shard, [row], s2, fs2, [d], t, tm, name + "_" + which, dep)
        g = g.reshape(2 * s2, fs2 // 2, d)
        if each is not None:
            dep = each(which, g)
        grads.append(g)
    return grads


def inproj_fwd(x1, ng, win, bin4, cos, sin):
    t, d = x1.shape
    s4, _, w2 = win.shape
    tm = _row_tile(t)
    dk = d // RET_HEADS
    scale = dk ** -0.5

    def body(x_ref, ng_ref, w_ref, b_ref, cos_ref, sin_ref, p_ref, hb_ref):
        y, _, _ = _rms(x_ref[...], ng_ref[...])
        h = y.astype(bf16)
        hb_ref[...] = h
        for s in range(s4):
            p = _dot(h, w_ref[s]) + b_ref[s]
            if s != 1:
                p_ref[s] = p.astype(bf16)
            else:
                cs, sn = cos_ref[...], sin_ref[...]
                for e in range(2 * RET_HEADS):
                    cols = slice(e * dk, (e + 1) * dk)
                    rot = _rot(p[:, cols], cs, sn)
                    p_ref[s, :, cols] = (rot if e < RET_HEADS else rot * scale).astype(bf16)

    tab = pl.BlockSpec((tm, dk // 2), lambda i: (i, 0))
    return pl.pallas_call(
        body, name="inproj_fwd", grid=(t // tm,),
        in_specs=[pl.BlockSpec((tm, d), lambda i: (i, 0)), pl.BlockSpec((1, d), lambda i: (0, 0)),
                  pl.BlockSpec((s4, d, w2), lambda i: (0, 0, 0), pipeline_mode=pl.Buffered(1)),
                  pl.BlockSpec((s4, 1, w2), lambda i: (0, 0, 0)), tab, tab],
        out_specs=[pl.BlockSpec((s4, tm, w2), lambda i: (0, i, 0)), pl.BlockSpec((tm, d), lambda i: (i, 0))],
        out_shape=[jax.ShapeDtypeStruct((s4, t, w2), bf16), jax.ShapeDtypeStruct((t, d), bf16)],
        compiler_params=_cparams(),
    )(x1, ng, win, bin4, cos, sin)


def _sgu_norm(va, ng, nb):
    gv = _gelu(va)
    mu = jnp.mean(gv, axis=-1, keepdims=True)
    xc = gv - mu
    rstd = lax.rsqrt(jnp.mean(xc * xc, axis=-1, keepdims=True) + NORM_EPS)
    xh = xc * rstd
    return xh, rstd, (xh * ng + nb).astype(bf16)


def sgu_fwd(proj, ng, nb, ws, bs, dep):
    _, t, w2 = proj.shape
    d = w2 // 2
    gd = d // SGU_GROUPS
    tm = _row_tile(t)

    def body(p_ref, ng_ref, nb_ref, ws_ref, bs_ref, dep_ref, a_ref):
        ua = p_ref[0, :, 0:d].astype(f32)
        va = p_ref[0, :, d:w2].astype(f32)
        gu = _gelu(ua)
        _, _, vn = _sgu_norm(va, ng_ref[...], nb_ref[...])
        for c in range(tm // SGU_CHUNK):
            rows = slice(c * SGU_CHUNK, (c + 1) * SGU_CHUNK)
            for g in range(SGU_GROUPS):
                cols = slice(g * gd, (g + 1) * gd)
                sg = _dot(ws_ref[g], vn[rows, cols]) + bs_ref[g]
                a_ref[rows, cols] = (gu[rows, cols] * sg).astype(bf16)

    return pl.pallas_call(
        body, name="sgu_fwd", grid=(t // tm,),
        in_specs=[pl.BlockSpec((1, tm, w2), lambda i: (0, i, 0)), pl.BlockSpec((1, d), lambda i: (0, 0)),
                  pl.BlockSpec((1, d), lambda i: (0, 0)), pl.BlockSpec((SGU_GROUPS, SGU_CHUNK, SGU_CHUNK), lambda i: (0, 0, 0)),
                  pl.BlockSpec((SGU_GROUPS, SGU_CHUNK, 1), lambda i: (0, 0, 0)), _ANY],
        out_specs=pl.BlockSpec((tm, d), lambda i: (i, 0)),
        out_shape=jax.ShapeDtypeStruct((t, d), bf16),
        compiler_params=_cparams(),
    )(proj, ng, nb, ws, bs, dep)


def sgu_bwd(da, proj, ng, nb, ws, bs, dep):
    _, t, w2 = proj.shape
    d = w2 // 2
    gd = d // SGU_GROUPS
    tm = _row_tile(t)

    def body(da_ref, p_ref, ng_ref, nb_ref, ws_ref, bs_ref, dep_ref,
             dua_ref, dva_ref, dws_ref, dbs_ref, dng_ref, dnb_ref, dvn_scr):
        i = pl.program_id(0)
        ua = p_ref[0, :, 0:d].astype(f32)
        va = p_ref[0, :, d:w2].astype(f32)
        gu = _gelu(ua)
        xh, rstd, vn = _sgu_norm(va, ng_ref[...], nb_ref[...])
        dad = da_ref[...].astype(f32)
        dsb = (dad * gu).astype(bf16)
        for c in range(tm // SGU_CHUNK):
            rows = slice(c * SGU_CHUNK, (c + 1) * SGU_CHUNK)
            for g in range(SGU_GROUPS):
                cols = slice(g * gd, (g + 1) * gd)
                sg = _dot(ws_ref[g], vn[rows, cols]) + bs_ref[g]
                dua_ref[rows, cols] = (dad[rows, cols] * sg * _dgelu(ua[rows, cols])).astype(bf16)
                ds = dsb[rows, cols]
                dvn_scr[rows, cols] = _dot_tn(ws_ref[g], ds)
                dw = _dot_nt(ds, vn[rows, cols])
                db = jnp.sum(ds.astype(f32), axis=1, keepdims=True)
                if c == 0:
                    _acc_out(dws_ref.at[g], i == 0, dw)
                    _acc_out(dbs_ref.at[g], i == 0, db)
                else:
                    dws_ref[g] += dw
                    dbs_ref[g] += db
        dvn = dvn_scr[...]
        _acc_out(dng_ref, i == 0, jnp.sum(dvn * xh, axis=0, keepdims=True))
        _acc_out(dnb_ref, i == 0, jnp.sum(dvn, axis=0, keepdims=True))
        dxh = dvn * ng_ref[...]
        dgv = rstd * (dxh - jnp.mean(dxh, axis=-1, keepdims=True) - xh * jnp.mean(dxh * xh, axis=-1, keepdims=True))
        dva_ref[...] = (dgv * _dgelu(va)).astype(bf16)

    row = pl.BlockSpec((tm, d), lambda i: (i, 0))
    vec = pl.BlockSpec((1, d), lambda i: (0, 0))
    wsp = pl.BlockSpec((SGU_GROUPS, SGU_CHUNK, SGU_CHUNK), lambda i: (0, 0, 0))
    bsp = pl.BlockSpec((SGU_GROUPS, SGU_CHUNK, 1), lambda i: (0, 0, 0))
    return pl.pallas_call(
        body, name="sgu_bwd", grid=(t // tm,),
        in_specs=[row, pl.BlockSpec((1, tm, w2), lambda i: (0, i, 0)), vec, vec, wsp, bsp, _ANY],
        out_specs=[row, row, wsp, bsp, vec, vec],
        out_shape=[jax.ShapeDtypeStruct((t, d), bf16), jax.ShapeDtypeStruct((t, d), bf16),
                   jax.ShapeDtypeStruct((SGU_GROUPS, SGU_CHUNK, SGU_CHUNK), f32), jax.ShapeDtypeStruct((SGU_GROUPS, SGU_CHUNK, 1), f32),
                   jax.ShapeDtypeStruct((1, d), f32), jax.ShapeDtypeStruct((1, d), f32)],
        scratch_shapes=[pltpu.VMEM((tm, d), f32)],
        compiler_params=_cparams(),
    )(da, proj, ng, nb, ws, bs, dep)


def retention_constants(decay_logit, t, dk, zero):
    lg = jax.nn.log_sigmoid(decay_logit.astype(f32) + zero)
    lgf = lg[0][:, None]
    lgb = lg[1][:, None]
    idx = jnp.arange(CHUNK, dtype=f32)[None, :]
    af = jnp.exp((idx + 1.0) * lgf)
    ab = jnp.exp((CHUNK - idx) * lgb)
    kf = jnp.exp((CHUNK - 1.0 - idx) * lgf)
    kb = jnp.exp(idx * lgb)
    cols = jnp.stack([af, ab, kf, kb, af * (idx + 1.0), ab * (CHUNK - idx), kf * (CHUNK - 1.0 - idx), kb * idx], axis=1)
    cols = cols[..., None]
    diff = idx[0][:, None] - idx[0][None, :]
    dfm = jnp.where(diff >= 0, jnp.exp(jnp.maximum(diff, 0.0)[None] * lgf[:, :, None]), 0.0)
    dbm = jnp.where(diff < 0, jnp.exp(jnp.maximum(-diff, 0.0)[None] * lgb[:, :, None]), 0.0)
    mats = jnp.stack([dfm + dbm, dfm * diff[None], dbm * (-diff)[None]], axis=1)
    cdec = jnp.stack([jnp.broadcast_to(jnp.exp(CHUNK * lgf), (RET_HEADS, dk)),
                      jnp.broadcast_to(jnp.exp(CHUNK * lgb), (RET_HEADS, dk))], axis=1)
    theta = ROPE_BASE ** (-jnp.arange(0, dk, 2, dtype=f32) / dk)
    ang = (jnp.arange(t, dtype=f32) + zero)[:, None] * theta[None, :]
    return cols, mats, cdec, jnp.cos(ang), jnp.sin(ang)


def _rot(tr, cos, sin):
    half = tr.shape[-1] // 2
    t1 = tr[:, :half]
    t2 = tr[:, half:]
    return jnp.concatenate([t1 * cos - t2 * sin, t2 * cos + t1 * sin], axis=-1)


def _rot_inv(dt, cos, sin):
    half = dt.shape[-1] // 2
    d1 = dt[:, :half]
    d2 = dt[:, half:]
    return jnp.concatenate([d1 * cos + d2 * sin, d2 * cos - d1 * sin], axis=-1)


def _ret_tile(t):
    return 2048 if t >= 4096 else _row_tile(t)


def _ret_specs(t, d, dk, rt):
    nr = t // rt
    hq = d // dk

    def blk(p, n):
        return (1 - p) * (nr - 1 - n) + p * n

    q_spec = pl.BlockSpec((1, rt, dk), lambda h, p, n: (1, blk(p, n), h))
    k_spec = pl.BlockSpec((1, rt, dk), lambda h, p, n: (1, blk(p, n), hq + h))
    v_spec = pl.BlockSpec((1, rt, dk), lambda h, p, n: (2, blk(p, n), h))
    g_spec = pl.BlockSpec((1, rt, dk), lambda h, p, n: (2, blk(p, n), hq + h))
    tab_spec = pl.BlockSpec((rt, dk // 2), lambda h, p, n: (blk(p, n), 0))
    cols_spec = pl.BlockSpec((1, 8, CHUNK, 1), lambda h, p, n: (h, 0, 0, 0))
    mats_spec = pl.BlockSpec((1, 3, CHUNK, CHUNK), lambda h, p, n: (h, 0, 0, 0))
    cdec_spec = pl.BlockSpec((1, 2, dk), lambda h, p, n: (h, 0, 0))
    in_row = pl.BlockSpec((rt, dk), lambda h, p, n: (blk(p, n), h))
    out_row = pl.BlockSpec((rt, dk), lambda h, p, n: (p * n, h))
    return nr, blk, q_spec, k_spec, v_spec, g_spec, tab_spec, cols_spec, mats_spec, cdec_spec, in_row, out_row


def ret_fwd(proj, cols, mats, cdec, dep):
    _, t, w2 = proj.shape
    d = w2 // 2
    dk = d // RET_HEADS
    rt = _ret_tile(t)
    cpt = rt // CHUNK
    nr, blk, q_spec, k_spec, v_spec, g_spec, _, cols_spec, mats_spec, cdec_spec, _, out_row = _ret_specs(t, d, dk, rt)

    def body(q_ref, k_ref, v_ref, g_ref, cols_ref, mats_ref, cdec_ref, dep_ref, r_ref, rn_ref, sb_scr, st):
        p = pl.program_id(1)
        n = pl.program_id(2)
        af, ab, kf, kb = cols_ref[0, 0], cols_ref[0, 1], cols_ref[0, 2], cols_ref[0, 3]
        cf = cdec_ref[0, 0:1, :]
        cb = cdec_ref[0, 1:2, :]

        @pl.when(n == 0)
        def _():
            st[...] = jnp.zeros_like(st)

        @pl.when(p == 0)
        def _():
            for j in reversed(range(cpt)):
                rows = slice(j * CHUNK, (j + 1) * CHUNK)
                ch = blk(p, n) * cpt + j
                kk = k_ref[0, rows, :].astype(f32)
                sb_scr[ch] = st[...].astype(bf16)
                st[...] = st[...] * cb + _dot_tn((kk * kb).astype(bf16), v_ref[0, rows, :])

        @pl.when(p == 1)
        def _():
            for j in range(cpt):
                rows = slice(j * CHUNK, (j + 1) * CHUNK)
                ch = blk(p, n) * cpt + j
                qb = q_ref[0, rows, :]
                kkb = k_ref[0, rows, :]
                q = qb.astype(f32)
                kk = kkb.astype(f32)
                v = v_ref[0, rows, :]
                pm = (_dot_nt(qb, kkb) * mats_ref[0, 0]).astype(bf16)
                out = (_dot(pm, v) + _dot((q * af).astype(bf16), st[...].astype(bf16))
                       + _dot((q * ab).astype(bf16), sb_scr[ch]))
                st[...] = st[...] * cf + _dot_tn((kk * kf).astype(bf16), v)
                rhat = out * lax.rsqrt(jnp.mean(out * out, axis=-1, keepdims=True) + NORM_EPS)
                gg = g_ref[0, rows, :].astype(f32)
                r_ref[rows, :] = out.astype(bf16)
                rn_ref[rows, :] = (rhat * gg * _sigmoid(gg)).astype(bf16)

    return pl.pallas_call(
        body, name="ret_fwd", grid=(RET_HEADS, 2, nr),
        in_specs=[q_spec, k_spec, v_spec, g_spec, cols_spec, mats_spec, cdec_spec, _ANY],
        out_specs=[out_row, out_row],
        out_shape=[jax.ShapeDtypeStruct((t, d), bf16), jax.ShapeDtypeStruct((t, d), bf16)],
        scratch_shapes=[pltpu.VMEM((t // CHUNK, dk, dk), bf16), pltpu.VMEM((dk, dk), f32)],
        compiler_params=_cparams(),
    )(proj, proj, proj, proj, cols, mats, cdec, dep)


def ret_bwd(drn, r, proj, cols, mats, cdec, cos, sin):
    _, t, w2 = proj.shape
    d = w2 // 2
    dk = d // RET_HEADS
    rt = _ret_tile(t)
    cpt = rt // CHUNK
    nr, blk, q_spec, k_spec, v_spec, g_spec, tab_spec, cols_spec, mats_spec, cdec_spec, in_row, out_row = _ret_specs(t, d, dk, rt)
    scale = dk ** -0.5

    def body(drn_ref, r_ref, q_ref, k_ref, v_ref, g_ref, cos_ref, sin_ref, cols_ref, mats_ref, cdec_ref,
             dq_ref, dk_ref, dv_ref, dg_ref, dlg_ref,
             sb_scr, gf_scr, st_s, st_g, acc_af, acc_ab, acc_vf, acc_vb, acc_sf, acc_sb, dout_scr, dgr_scr):
        p = pl.program_id(1)
        n = pl.program_id(2)
        af, ab, kf, kb = cols_ref[0, 0], cols_ref[0, 1], cols_ref[0, 2], cols_ref[0, 3]
        af1, ab1, kf1, kb1 = cols_ref[0, 4], cols_ref[0, 5], cols_ref[0, 6], cols_ref[0, 7]
        cf = cdec_ref[0, 0:1, :]
        cb = cdec_ref[0, 1:2, :]

        @pl.when(n == 0)
        def _():
            st_s[...] = jnp.zeros_like(st_s)
            st_g[...] = jnp.zeros_like(st_g)

        @pl.when(jnp.logical_and(n == 0, p == 1))
        def _():
            for a in (acc_af, acc_ab, acc_vf, acc_vb, acc_sf, acc_sb):
                a[...] = jnp.zeros_like(a)

        def load(rows):
            cs, sn = cos_ref[rows, :], sin_ref[rows, :]
            q = q_ref[0, rows, :].astype(f32)
            kk = k_ref[0, rows, :].astype(f32)
            rr = r_ref[rows, :].astype(f32)
            rstd = lax.rsqrt(jnp.mean(rr * rr, axis=-1, keepdims=True) + NORM_EPS)
            rhat = rr * rstd
            gg = g_ref[0, rows, :].astype(f32)
            sg = _sigmoid(gg)
            dd = drn_ref[rows, :].astype(f32)
            drhat = dd * gg * sg
            dout = rstd * (drhat - rhat * jnp.mean(drhat * rhat, axis=-1, keepdims=True))
            dgr = dd * rhat * _dsilu(gg, sg)
            return q, kk, dout.astype(bf16), dgr, cs, sn

        @pl.when(p == 0)
        def _():
            for j in reversed(range(cpt)):
                rows = slice(j * CHUNK, (j + 1) * CHUNK)
                ch = blk(p, n) * cpt + j
                q, kk, doutb, dgr, _, _ = load(rows)
                kept = pl.ds(pl.multiple_of(ch * CHUNK, CHUNK), CHUNK)
                dout_scr[kept, :] = doutb
                dgr_scr[kept, :] = dgr.astype(bf16)
                sb_scr[ch] = st_s[...].astype(bf16)
                gf_scr[ch] = st_g[...].astype(bf16)
                st_s[...] = st_s[...] * cb + _dot_tn((kk * kb).astype(bf16), v_ref[0, rows, :])
                st_g[...] = st_g[...] * cf + _dot_tn((q * af).astype(bf16), doutb)

        @pl.when(p == 1)
        def _():
            for j in range(cpt):
                rows = slice(j * CHUNK, (j + 1) * CHUNK)
                ch = blk(p, n) * cpt + j
                kept = pl.ds(pl.multiple_of(ch * CHUNK, CHUNK), CHUNK)
                doutb = dout_scr[kept, :]
                cs, sn = cos_ref[rows, :], sin_ref[rows, :]
                v = v_ref[0, rows, :]
                qb = q_ref[0, rows, :]
                kkb = k_ref[0, rows, :]
                q = qb.astype(f32)
                kk = kkb.astype(f32)
                sf = st_s[...]
                gb = st_g[...]
                sfb = sf.astype(bf16)
                gbb = gb.astype(bf16)
                sbb = sb_scr[ch]
                gfb = gf_scr[ch]
                dmat = mats_ref[0, 0]
                scores = _dot_nt(qb, kkb)
                dpraw = _dot_nt(doutb, v)
                dpb = (dpraw * dmat).astype(bf16)
                pmb = (scores * dmat).astype(bf16)
                x1 = _dot_nt(doutb, sfb)
                x2 = _dot_nt(doutb, sbb)
                y1 = _dot_nt(v, gfb)
                y2 = _dot_nt(v, gbb)
                kdf = (kk * kf).astype(bf16)
                kdb = (kk * kb).astype(bf16)
                dq = _dot(dpb, kkb) + x1 * af + x2 * ab
                dkk = _dot_tn(dpb, qb) + y1 * kf + y2 * kb
                dv = _dot_tn(pmb, doutb) + _dot(kdf, gfb) + _dot(kdb, gbb)
                ps = dpraw * scores
                acc_af[...] += ps * mats_ref[0, 1]
                acc_ab[...] += ps * mats_ref[0, 2]
                acc_vf[...] += x1 * q * af1 + y1 * kk * kf1
                acc_vb[...] += x2 * q * ab1 + y2 * kk * kb1
                acc_sf[...] += gfb.astype(f32) * sf
                acc_sb[...] += gb * sbb.astype(f32)
                st_s[...] = sf * cf + _dot_tn(kdf, v)
                st_g[...] = gb * cb + _dot_tn((q * ab).astype(bf16), doutb)
                dq_ref[rows, :] = _rot_inv(dq, cs, sn).astype(bf16)
                dk_ref[rows, :] = (_rot_inv(dkk, cs, sn) * scale).astype(bf16)
                dv_ref[rows, :] = dv.astype(bf16)
                dg_ref[rows, :] = dgr_scr[kept, :]

        @pl.when(jnp.logical_and(p == 1, n == nr - 1))
        def _():
            tf = jnp.sum(acc_af[...]) + jnp.sum(acc_vf[...]) + CHUNK * jnp.sum(acc_sf[...] * cf)
            tb = jnp.sum(acc_ab[...]) + jnp.sum(acc_vb[...]) + CHUNK * jnp.sum(acc_sb[...] * cb)
            rid = lax.broadcasted_iota(jnp.int32, (8, 128), 0)
            dlg_ref[0] = jnp.where(rid == 0, tf, jnp.where(rid == 1, tb, 0.0))

    nch = t // CHUNK
    return pl.pallas_call(
        body, name="ret_bwd", grid=(RET_HEADS, 2, nr),
        in_specs=[in_row, in_row, q_spec, k_spec, v_spec, g_spec, tab_spec, tab_spec, cols_spec, mats_spec, cdec_spec],
        out_specs=[out_row, out_row, out_row, out_row, pl.BlockSpec((1, 8, 128), lambda h, p, n: (h, 0, 0))],
        out_shape=[jax.ShapeDtypeStruct((t, d), bf16)] * 4 + [jax.ShapeDtypeStruct((RET_HEADS, 8, 128), f32)],
        scratch_shapes=[pltpu.VMEM((nch, dk, dk), bf16), pltpu.VMEM((nch, dk, dk), bf16),
                        pltpu.VMEM((dk, dk), f32), pltpu.VMEM((dk, dk), f32),
                        pltpu.VMEM((CHUNK, CHUNK), f32), pltpu.VMEM((CHUNK, CHUNK), f32),
                        pltpu.VMEM((CHUNK, dk), f32), pltpu.VMEM((CHUNK, dk), f32),
                        pltpu.VMEM((dk, dk), f32), pltpu.VMEM((dk, dk), f32),
                        pltpu.VMEM((t, dk), bf16), pltpu.VMEM((t, dk), bf16)],
        compiler_params=_cparams(VMEM_LIMIT_WIDE),
    )(drn, r, proj, proj, proj, proj, cos, sin, cols, mats, cdec)


def mix_fwd(a, rn, proj, wa, wb, wo, x1):
    t, d = x1.shape
    tm = _row_tile(t)

    def body(a_ref, rn_ref, p_ref, wa_ref, wb_ref, wo_ref, x_ref, xo_ref, ba_ref, br_ref):
        ba = _dot(a_ref[...], wa_ref[...])
        br = _dot(rn_ref[...], wb_ref[...])
        sa = _sigmoid(p_ref[0, :, 0:d].astype(f32))
        sb = _sigmoid(p_ref[0, :, d:2 * d].astype(f32))
        mix = (sa * ba + sb * br).astype(bf16)
        xo_ref[...] = x_ref[...] + _dot(mix, wo_ref[...])
        ba_ref[...] = ba.astype(bf16)
        br_ref[...] = br.astype(bf16)

    row = pl.BlockSpec((tm, d), lambda i: (i, 0))
    wsp = pl.BlockSpec((d, d), lambda i: (0, 0))
    return pl.pallas_call(
        body, name="mix_fwd", grid=(t // tm,),
        in_specs=[row, row, pl.BlockSpec((1, tm, 2 * d), lambda i: (3, i, 0)), wsp, wsp, wsp, row],
        out_specs=[row, row, row],
        out_shape=[jax.ShapeDtypeStruct((t, d), f32), jax.ShapeDtypeStruct((t, d), bf16), jax.ShapeDtypeStruct((t, d), bf16)],
        compiler_params=_cparams(),
    )(a, rn, proj, wa, wb, wo, x1)


def mix_bwd_act(dx2, ba, br, proj, wa, wb, wo, dep):
    t, d = dx2.shape
    tm = _row_tile(t)

    def body(dx_ref, ba_ref, br_ref, p_ref, wa_ref, wb_ref, wo_ref, dep_ref,
             da_ref, drn_ref, dga_ref, dgb_ref, mix_ref, dba_ref, dbr_ref, dxb_ref):
        dxb = dx_ref[...].astype(bf16)
        dxb_ref[...] = dxb
        dmix = _dot_nt(dxb, wo_ref[...])
        ba = ba_ref[...].astype(f32)
        br = br_ref[...].astype(f32)
        sa = _sigmoid(p_ref[0, :, 0:d].astype(f32))
        sb = _sigmoid(p_ref[0, :, d:2 * d].astype(f32))
        mix_ref[...] = (sa * ba + sb * br).astype(bf16)
        dba = (dmix * sa).astype(bf16)
        dbr = (dmix * sb).astype(bf16)
        dba_ref[...] = dba
        dbr_ref[...] = dbr
        dga_ref[...] = (dmix * ba * sa * (1.0 - sa)).astype(bf16)
        dgb_ref[...] = (dmix * br * sb * (1.0 - sb)).astype(bf16)
        da_ref[...] = _dot_nt(dba, wa_ref[...]).astype(bf16)
        drn_ref[...] = _dot_nt(dbr, wb_ref[...]).astype(bf16)

    row = pl.BlockSpec((tm, d), lambda i: (i, 0))
    wsp = pl.BlockSpec((d, d), lambda i: (0, 0))
    return pl.pallas_call(
        body, name="mix_bwd_act", grid=(t // tm,),
        in_specs=[row, row, row, pl.BlockSpec((1, tm, 2 * d), lambda i: (3, i, 0)), wsp, wsp, wsp, _ANY],
        out_specs=[row] * 8,
        out_shape=[jax.ShapeDtypeStruct((t, d), bf16)] * 8,
        compiler_params=_cparams(),
    )(dx2, ba, br, proj, wa, wb, wo, dep)


def inproj_bwd_act(segs, win, x1, ng, dx2):
    t, d = x1.shape
    s4 = win.shape[0]
    tm = _row_tile(t)
    nseg = len(segs)

    def body(*refs):
        seg_refs = refs[:nseg]
        w_ref, x_ref, ng_ref, dx2_ref, dx1_ref, db_ref, dng_ref = refs[nseg:]
        i = pl.program_id(0)
        dh = None
        for e, sr in enumerate(seg_refs):
            sb = sr[...]
            part = _dot_nt(sb, w_ref[e // 2, :, (e % 2) * d:(e % 2 + 1) * d])
            dh = part if dh is None else dh + part
            _acc_out(db_ref.at[e], i == 0, jnp.sum(sb.astype(f32), axis=0, keepdims=True))
        _, xh, r = _rms(x_ref[...], ng_ref[...])
        dx1_ref[...] = dx2_ref[...] + _rms_bwd(dh, xh, r, ng_ref[...])
        _acc_out(dng_ref, i == 0, jnp.sum(dh * xh, axis=0, keepdims=True))

    row = pl.BlockSpec((tm, d), lambda i: (i, 0))
    vec = pl.BlockSpec((1, d), lambda i: (0, 0))
    return pl.pallas_call(
        body, name="inproj_bwd_act", grid=(t // tm,),
        in_specs=[row] * nseg + [pl.BlockSpec((s4, d, 2 * d), lambda i: (0, 0, 0), pipeline_mode=pl.Buffered(1)),
                                 row, vec, row],
        out_specs=[row, pl.BlockSpec((nseg, 1, d), lambda i: (0, 0, 0)), vec],
        out_shape=[jax.ShapeDtypeStruct((t, d), f32), jax.ShapeDtypeStruct((nseg, 1, d), f32),
                   jax.ShapeDtypeStruct((1, d), f32)],
        compiler_params=_cparams(VMEM_LIMIT_WIDE),
    )(*segs, win, x1, ng, dx2)


def _place():
    return lax.axis_index("x"), lax.axis_index("y"), lax.axis_index("c")


def _other_chips(x, y):
    return [(1 - x, y), (x, 1 - y), (1 - x, 1 - y)]


_ANY = pl.BlockSpec(memory_space=pl.ANY)


_HBM = pl.BlockSpec(memory_space=pltpu.HBM)
_SEM = pl.BlockSpec(memory_space=pltpu.SEMAPHORE)
_EFFECT = pltpu.SideEffectType.DATAFLOW_SIDE_EFFECTING


def _hbm(a):
    return pltpu.with_memory_space_constraint(a, pltpu.HBM)


def _half_rows(ref, c):
    half = ref.shape[1] // 2
    return pl.ds(pl.multiple_of(c * half, 16), half)


def _chip_copy(src, dst, send_sem, recv_sem, chip, c):
    return pltpu.make_async_remote_copy(src_ref=src, dst_ref=dst, send_sem=send_sem, recv_sem=recv_sem,
                                        device_id=(chip[0], chip[1], c), device_id_type=MESH)


def gather_start(bufs, groups, name):
    nb, ng = len(bufs), len(groups)

    def body(*refs):
        ins = refs[:nb]
        sems = refs[nb:nb + 2 * ng]
        token = refs[-1]
        x, y, c = _place()
        k = 2 * x + y
        for gi, grp in enumerate(groups):
            for wi, w in enumerate(grp):
                mine = ins[w].at[k, _half_rows(ins[w], c)]
                for j, chip in enumerate(_other_chips(x, y)):
                    _chip_copy(mine, mine, sems[2 * gi].at[3 * wi + j], sems[2 * gi + 1].at[3 * wi + j], chip, c).start()
        token[...] = jnp.zeros_like(token)

    sem_shapes = []
    for grp in groups:
        sem_shapes += [pltpu.SemaphoreType.DMA((3 * len(grp),)), pltpu.SemaphoreType.DMA((3 * len(grp),))]
    outs = pl.pallas_call(
        body, name=name,
        out_shape=sem_shapes + [pltpu.HBM(b.shape, b.dtype) for b in bufs] + [jax.ShapeDtypeStruct((8, 128), f32)],
        in_specs=[_HBM] * nb,
        out_specs=[_SEM] * (2 * ng) + [_HBM] * nb + [pl.BlockSpec(memory_space=pltpu.VMEM)],
        input_output_aliases={w: 2 * ng + w for w in range(nb)},
        compiler_params=pltpu.CompilerParams(has_side_effects=_EFFECT),
    )(*[_hbm(b) for b in bufs])
    sems = [(outs[2 * gi], outs[2 * gi + 1]) for gi in range(ng)]
    return sems, list(outs[2 * ng:2 * ng + nb]), outs[-1]


def gather_wait(bufs, sems, after, name):
    n = len(bufs)

    def body(*refs):
        ins = refs[:n]
        send_sems, recv_sems = refs[n], refs[n + 1]
        x, y, c = _place()
        k = 2 * x + y
        for wi in range(n):
            half = _half_rows(ins[wi], c)
            for j, chip in enumerate(_other_chips(x, y)):
                cp = _chip_copy(ins[wi].at[k, half], ins[wi].at[2 * chip[0] + chip[1], half], send_sems.at[3 * wi + j],
                                recv_sems.at[3 * wi + j], chip, c)
                cp.wait_send()
                cp.wait_recv()

    outs = pl.pallas_call(
        body, name=name,
        out_shape=[pltpu.HBM(b.shape, b.dtype) for b in bufs],
        in_specs=[_HBM] * n + [_SEM, _SEM, _ANY],
        out_specs=[_HBM] * n,
        input_output_aliases={i: i for i in range(n)},
        compiler_params=pltpu.CompilerParams(has_side_effects=_EFFECT),
    )(*bufs, sems[0], sems[1], after)
    return list(outs)


def gather_forward(bufs, name):
    n = len(bufs)

    def body(*refs):
        ins = refs[n:2 * n]
        send_sems, recv_sems = refs[2 * n], refs[2 * n + 1]
        x, y, c = _place()
        copies = []
        for wi in range(n):
            for j, chip in enumerate(_other_chips(x, y)):
                kp = 2 * chip[0] + chip[1]
                got = ins[wi].at[kp, _half_rows(ins[wi], c)]
                cp = pltpu.make_async_remote_copy(
                    src_ref=got, dst_ref=got, send_sem=send_sems.at[3 * wi + j], recv_sem=recv_sems.at[3 * wi + j],
                    device_id=(x, y, 1 - c), device_id_type=MESH)
                cp.start()
                copies.append((cp, wi, kp, j))
        for cp, wi, kp, j in copies:
            cp.wait_send()
            theirs = ins[wi].at[kp, _half_rows(ins[wi], 1 - c)]
            pltpu.make_async_remote_copy(
                src_ref=theirs, dst_ref=theirs, send_sem=send_sems.at[3 * wi + j], recv_sem=recv_sems.at[3 * wi + j],
                device_id=(x, y, 1 - c), device_id_type=MESH).wait_recv()

    outs = pl.pallas_call(
        body, name=name,
        out_shape=[jax.ShapeDtypeStruct(b.shape, b.dtype) for b in bufs],
        in_specs=[_ANY] * n, out_specs=[_ANY] * n,
        input_output_aliases={i: i for i in range(n)},
        scratch_shapes=[pltpu.SemaphoreType.DMA((3 * n,)), pltpu.SemaphoreType.DMA((3 * n,))],
    )(*bufs)
    return list(outs)


def forward_start(bufs, name):
    n = len(bufs)

    def body(*refs):
        x, y, c = _place()
        for wi in range(n):
            for j, chip in enumerate(_other_chips(x, y)):
                got = refs[wi].at[2 * chip[0] + chip[1], _half_rows(refs[wi], c)]
                _sibling_copy(got, got, refs[n].at[3 * wi + j], refs[n + 1].at[3 * wi + j]).start()
        refs[-1][...] = jnp.zeros_like(refs[-1])

    return _split_start(body, name, 3 * n, list(bufs))


def forward_wait(bufs, sems, after, name):
    n = len(bufs)

    def body(*refs):
        x, y, c = _place()
        for wi in range(n):
            for j, chip in enumerate(_other_chips(x, y)):
                kp = 2 * chip[0] + chip[1]
                got = refs[wi].at[kp, _half_rows(refs[wi], c)]
                theirs = refs[wi].at[kp, _half_rows(refs[wi], 1 - c)]
                _sibling_copy(got, got, refs[n].at[3 * wi + j], refs[n + 1].at[3 * wi + j]).wait_send()
                _sibling_copy(theirs, theirs, refs[n].at[3 * wi + j], refs[n + 1].at[3 * wi + j]).wait_recv()

    return _split_wait(body, name, list(bufs), sems, after)


def exchange_start(grads, name):
    n = len(grads)
    lands = [lax.empty((3,) + g.shape[1:], g.dtype) for g in grads]

    def body(*refs):
        ins = refs[:n]
        land = refs[n:2 * n]
        send_sems, recv_sems = refs[2 * n], refs[2 * n + 1]
        token = refs[-1]
        x, y, c = _place()
        for wi in range(n):
            for j, chip in enumerate(_other_chips(x, y)):
                _chip_copy(ins[wi].at[2 * chip[0] + chip[1]], land[wi].at[j], send_sems.at[3 * wi + j],
                           recv_sems.at[3 * wi + j], chip, c).start()
        token[...] = jnp.zeros_like(token)

    outs = pl.pallas_call(
        body, name=name,
        out_shape=[pltpu.SemaphoreType.DMA((3 * n,)), pltpu.SemaphoreType.DMA((3 * n,))]
        + [pltpu.HBM(g.shape, g.dtype) for g in grads] + [pltpu.HBM(l.shape, l.dtype) for l in lands]
        + [jax.ShapeDtypeStruct((8, 128), f32)],
        in_specs=[_HBM] * (2 * n),
        out_specs=[_SEM, _SEM] + [_HBM] * (2 * n) + [pl.BlockSpec(memory_space=pltpu.VMEM)],
        input_output_aliases={i: 2 + i for i in range(2 * n)},
        compiler_params=pltpu.CompilerParams(has_side_effects=_EFFECT),
    )(*[_hbm(g) for g in grads], *[_hbm(l) for l in lands])
    return (outs[0], outs[1]), list(outs[2:2 + n]), list(outs[2 + n:2 + 2 * n]), outs[-1]


def exchange_wait(grads, lands, sems, after, name):
    n = len(grads)

    def body(*refs):
        ins = refs[:n]
        land = refs[n:2 * n]
        send_sems, recv_sems = refs[2 * n], refs[2 * n + 1]
        x, y, c = _place()
        for wi in range(n):
            for j, chip in enumerate(_other_chips(x, y)):
                cp = _chip_copy(ins[wi].at[2 * chip[0] + chip[1]], land[wi].at[j], send_sems.at[3 * wi + j],
                                recv_sems.at[3 * wi + j], chip, c)
                cp.wait_send()
                cp.wait_recv()

    outs = pl.pallas_call(
        body, name=name,
        out_shape=[pltpu.HBM(g.shape, g.dtype) for g in grads] + [pltpu.HBM(l.shape, l.dtype) for l in lands],
        in_specs=[_HBM] * (2 * n) + [_SEM, _SEM, _ANY],
        out_specs=[_HBM] * (2 * n),
        input_output_aliases={i: i for i in range(2 * n)},
        compiler_params=pltpu.CompilerParams(has_side_effects=_EFFECT),
    )(*grads, *lands, sems[0], sems[1], after)
    return list(outs[:n]), list(outs[n:])


def _split_start(body, name, n_sems, operands):
    n = len(operands)
    outs = pl.pallas_call(
        body, name=name,
        out_shape=[pltpu.SemaphoreType.DMA((n_sems,)), pltpu.SemaphoreType.DMA((n_sems,))]
        + [pltpu.HBM(o.shape, o.dtype) for o in operands] + [jax.ShapeDtypeStruct((8, 128), f32)],
        in_specs=[_HBM] * n,
        out_specs=[_SEM, _SEM] + [_HBM] * n + [pl.BlockSpec(memory_space=pltpu.VMEM)],
        input_output_aliases={i: 2 + i for i in range(n)},
        compiler_params=pltpu.CompilerParams(has_side_effects=_EFFECT),
    )(*[_hbm(o) for o in operands])
    return (outs[0], outs[1]), list(outs[2:2 + n]), outs[-1]


def _split_wait(body, name, operands, sems, after):
    n = len(operands)
    outs = pl.pallas_call(
        body, name=name,
        out_shape=[pltpu.HBM(o.shape, o.dtype) for o in operands],
        in_specs=[_HBM] * n + [_SEM, _SEM, _ANY],
        out_specs=[_HBM] * n,
        input_output_aliases={i: i for i in range(n)},
        compiler_params=pltpu.CompilerParams(has_side_effects=_EFFECT),
    )(*operands, sems[0], sems[1], after)
    return list(outs)


def _sibling_copy(src, dst, send_sem, recv_sem):
    x, y, c = _place()
    return pltpu.make_async_remote_copy(src_ref=src, dst_ref=dst, send_sem=send_sem, recv_sem=recv_sem,
                                        device_id=(x, y, 1 - c), device_id_type=MESH)


def swap_start(parts, name):
    n = len(parts)

    def body(*refs):
        for w in range(n):
            _sibling_copy(refs[w], refs[n + w], refs[2 * n].at[w], refs[2 * n + 1].at[w]).start()
        refs[-1][...] = jnp.zeros_like(refs[-1])

    sems, ops, token = _split_start(body, name, n, list(parts) + [lax.empty(p.shape, p.dtype) for p in parts])
    return sems, ops[:n], ops[n:], token


def swap_wait(parts, lands, sems, after, name):
    n = len(parts)

    def body(*refs):
        for w in range(n):
            cp = _sibling_copy(refs[w], refs[n + w], refs[2 * n].at[w], refs[2 * n + 1].at[w])
            cp.wait_send()
            cp.wait_recv()

    outs = _split_wait(body, name, list(parts) + list(lands), sems, after)
    return outs[:n], outs[n:]


def _all_peers(x, y, c):
    return [(1 - x if m & 4 else x, 1 - y if m & 2 else y, 1 - c if m & 1 else c) for m in range(1, N_DEV)]


def small_start(block):
    land = jnp.broadcast_to(block[None], (N_DEV,) + block.shape)

    def body(b_ref, land_ref, send_sems, recv_sems, b_thru, land_thru, token):
        x, y, c = _place()
        me = 4 * x + 2 * y + c
        for m, peer in enumerate(_all_peers(x, y, c)):
            pltpu.make_async_remote_copy(src_ref=b_ref, dst_ref=land_ref.at[me], send_sem=send_sems.at[m],
                                         recv_sem=recv_sems.at[m], device_id=peer, device_id_type=MESH).start()
        token[...] = jnp.zeros_like(token)

    sems, ops, token = _split_start(body, "small_start", N_DEV - 1, [block, land])
    return sems, ops[0], ops[1], token


def small_wait(block, land, sems, after):
    def body(b_ref, land_ref, send_sems, recv_sems, after_ref, b_thru, land_thru):
        x, y, c = _place()
        for m, (px, py, pc) in enumerate(_all_peers(x, y, c)):
            cp = pltpu.make_async_remote_copy(src_ref=b_ref, dst_ref=land_ref.at[4 * px + 2 * py + pc],
                                              send_sem=send_sems.at[m], recv_sem=recv_sems.at[m],
                                              device_id=(px, py, pc), device_id_type=MESH)
            cp.wait_send()
            cp.wait_recv()

    return _split_wait(body, "small_wait", [block, land], sems, after)[1]


def _adamw(w, g, m, v):
    m = ADAM_B1 * m + (1.0 - ADAM_B1) * g
    v = ADAM_B2 * v + (1.0 - ADAM_B2) * (g * g)
    m_hat = m / (1.0 - ADAM_B1 ** ADAM_STEP)
    v_hat = v / (1.0 - ADAM_B2 ** ADAM_STEP)
    delta = -ADAM_LR * (m_hat / (jnp.sqrt(v_hat) + ADAM_EPS) + ADAM_WD * w)
    return delta, m, v


EW_BLOCK_BYTES = 2 * 1024 * 1024


def _ew_tile(rows, cols):
    for cand in (512, 352, 256, 176, 128, 64, 32, 16, 8):
        if rows % cand == 0 and cand * cols * 4 <= EW_BLOCK_BYTES:
            return cand
    return rows


def sum_partials(chip, own, land, name):
    _, r, c = own.shape
    tr = _ew_tile(r, c)

    def body(k_ref, own_ref, p_ref, o_ref):
        o_ref[...] = ((own_ref[0].astype(f32) + p_ref[0].astype(f32)) + p_ref[1].astype(f32)) + p_ref[2].astype(f32)

    return pl.pallas_call(
        body, name=name,
        grid_spec=pltpu.PrefetchScalarGridSpec(
            num_scalar_prefetch=1, grid=(r // tr,),
            in_specs=[pl.BlockSpec((1, tr, c), lambda i, k: (k[0], i, 0)), pl.BlockSpec((3, tr, c), lambda i, k: (0, i, 0))],
            out_specs=pl.BlockSpec((tr, c), lambda i, k: (i, 0))),
        out_shape=jax.ShapeDtypeStruct((r, c), f32),
        compiler_params=_cparams(),
    )(chip, own, land)


def adamw_shard(p_mine, p_sibling, w, m, v, name):
    r, c = w.shape
    tr = _ew_tile(r, c)

    def body(a_ref, b_ref, w_ref, m_ref, v_ref, g_ref, d_ref, mo_ref, vo_ref):
        g = a_ref[...] + b_ref[...]
        delta, mn, vn = _adamw(w_ref[...], g, m_ref[...], v_ref[...])
        g_ref[...] = g
        d_ref[...] = delta
        mo_ref[...] = mn
        vo_ref[...] = vn

    blk = pl.BlockSpec((tr, c), lambda i: (i, 0))
    return pl.pallas_call(
        body, name=name, grid=(r // tr,),
        in_specs=[blk] * 5, out_specs=[blk] * 4,
        out_shape=[jax.ShapeDtypeStruct((r, c), f32)] * 4,
        compiler_params=_cparams(),
    )(p_mine, p_sibling, w, m, v)


def adamw_small(g8, w, m, v):
    _, r, lanes = g8.shape

    def body(g_ref, w_ref, m_ref, v_ref, go_ref, d_ref, mo_ref, vo_ref):
        g = g_ref[0]
        for i in range(1, N_DEV):
            g = g + g_ref[i]
        delta, mn, vn = _adamw(w_ref[...], g, m_ref[...], v_ref[...])
        go_ref[...] = g
        d_ref[...] = delta
        mo_ref[...] = mn
        vo_ref[...] = vn

    return pl.pallas_call(
        body, name="adamw_small",
        out_shape=[jax.ShapeDtypeStruct((r, lanes), f32)] * 4,
        compiler_params=_cparams(),
    )(g8, w, m, v)


def _size(shape):
    n = 1
    for e in shape:
        n *= e
    return n


def _pack_rows(shapes):
    rows = [-(-_size(s) // 1024) * 8 for s in shapes]
    return rows, sum(rows)


def _pack(arrs, shapes):
    rows, _ = _pack_rows(shapes)
    parts = [jnp.pad(a.reshape(-1).astype(f32), (0, r * 128 - _size(s))).reshape(r, 128)
             for a, s, r in zip(arrs, shapes, rows)]
    return jnp.concatenate(parts, axis=0)


def _unpack(block, shapes):
    rows, _ = _pack_rows(shapes)
    out, off = [], 0
    for s, r in zip(shapes, rows):
        out.append(block[off:off + r].reshape(-1)[:_size(s)].reshape(s))
        off += r
    return out


TRANSPOSED = ("ffn1_w_gate", "ffn1_w_up", "ffn2_w_gate", "ffn2_w_up")


def _shard2d(a, n):
    return a[0].T if n in TRANSPOSED else a[0]


def _unshard(a, n):
    return (a.T if n in TRANSPOSED else a)[None]


BIG = ("ffn1_w_gate", "ffn1_w_up", "ffn1_w_down", "w_in", "w_branch_a", "w_branch_b", "w_out",
       "ffn2_w_gate", "ffn2_w_up", "ffn2_w_down")
SMALL = ("ffn1_norm", "mix_norm", "b_in", "sgu_norm_g", "sgu_norm_b", "sgu_w_s", "sgu_b_s", "ret_decay_logit",
         "ffn2_norm", "final_norm")
WEIGHTS = ("ffn1_norm", "ffn1_w_gate", "ffn1_w_up", "ffn1_w_down", "mix_norm", "w_in", "b_in", "sgu_norm_g",
           "sgu_norm_b", "sgu_w_s", "sgu_b_s", "ret_decay_logit", "w_branch_a", "w_branch_b", "w_out", "ffn2_norm",
           "ffn2_w_gate", "ffn2_w_up", "ffn2_w_down", "final_norm")


def kernel(x, ffn1_norm, ffn1_w_gate, ffn1_w_up, ffn1_w_down, mix_norm, w_in, b_in, sgu_norm_g, sgu_norm_b, sgu_w_s, sgu_b_s, ret_decay_logit, w_branch_a, w_branch_b, w_out, ffn2_norm, ffn2_w_gate, ffn2_w_up, ffn2_w_down, final_norm, loss_target, m_ffn1_norm, m_ffn1_w_gate, m_ffn1_w_up, m_ffn1_w_down, m_mix_norm, m_w_in, m_b_in, m_sgu_norm_g, m_sgu_norm_b, m_sgu_w_s, m_sgu_b_s, m_ret_decay_logit, m_w_branch_a, m_w_branch_b, m_w_out, m_ffn2_norm, m_ffn2_w_gate, m_ffn2_w_up, m_ffn2_w_down, m_final_norm, v_ffn1_norm, v_ffn1_w_gate, v_ffn1_w_up, v_ffn1_w_down, v_mix_norm, v_w_in, v_b_in, v_sgu_norm_g, v_sgu_norm_b, v_sgu_w_s, v_sgu_b_s, v_ret_decay_logit, v_w_branch_a, v_w_branch_b, v_w_out, v_ffn2_norm, v_ffn2_w_gate, v_ffn2_w_up, v_ffn2_w_down, v_final_norm):
    p = dict(ffn1_norm=ffn1_norm, ffn1_w_gate=ffn1_w_gate, ffn1_w_up=ffn1_w_up, ffn1_w_down=ffn1_w_down,
             mix_norm=mix_norm, w_in=w_in, b_in=b_in, sgu_norm_g=sgu_norm_g, sgu_norm_b=sgu_norm_b, sgu_w_s=sgu_w_s,
             sgu_b_s=sgu_b_s, ret_decay_logit=ret_decay_logit, w_branch_a=w_branch_a, w_branch_b=w_branch_b,
             w_out=w_out, ffn2_norm=ffn2_norm, ffn2_w_gate=ffn2_w_gate, ffn2_w_up=ffn2_w_up, ffn2_w_down=ffn2_w_down,
             final_norm=final_norm)
    mom = dict(ffn1_norm=m_ffn1_norm, ffn1_w_gate=m_ffn1_w_gate, ffn1_w_up=m_ffn1_w_up, ffn1_w_down=m_ffn1_w_down,
               mix_norm=m_mix_norm, w_in=m_w_in, b_in=m_b_in, sgu_norm_g=m_sgu_norm_g, sgu_norm_b=m_sgu_norm_b,
               sgu_w_s=m_sgu_w_s, sgu_b_s=m_sgu_b_s, ret_decay_logit=m_ret_decay_logit, w_branch_a=m_w_branch_a,
               w_branch_b=m_w_branch_b, w_out=m_w_out, ffn2_norm=m_ffn2_norm, ffn2_w_gate=m_ffn2_w_gate,
               ffn2_w_up=m_ffn2_w_up, ffn2_w_down=m_ffn2_w_down, final_norm=m_final_norm)
    var = dict(ffn1_norm=v_ffn1_norm, ffn1_w_gate=v_ffn1_w_gate, ffn1_w_up=v_ffn1_w_up, ffn1_w_down=v_ffn1_w_down,
               mix_norm=v_mix_norm, w_in=v_w_in, b_in=v_b_in, sgu_norm_g=v_sgu_norm_g, sgu_norm_b=v_sgu_norm_b,
               sgu_w_s=v_sgu_w_s, sgu_b_s=v_sgu_b_s, ret_decay_logit=v_ret_decay_logit, w_branch_a=v_w_branch_a,
               w_branch_b=v_w_branch_b, w_out=v_w_out, ffn2_norm=v_ffn2_norm, ffn2_w_gate=v_ffn2_w_gate,
               ffn2_w_up=v_ffn2_w_up, ffn2_w_down=v_ffn2_w_down, final_norm=v_final_norm)

    xs = x[0]
    tgt = loss_target[0]
    t, d = xs.shape
    dk = d // RET_HEADS

    shards2d = {n: _shard2d(p[n], n) for n in BIG}
    chip = (2 * lax.axis_index("x") + lax.axis_index("y")).astype(jnp.int32).reshape(1)
    groups = {"ffn1": ("ffn1_w_gate", "ffn1_w_up", "ffn1_w_down"), "in": ("w_in",),
              "mix": ("w_branch_a", "w_branch_b", "w_out"), "ffn2": ("ffn2_w_gate", "ffn2_w_up", "ffn2_w_down")}
    def own_slot(n, zero):
        sh = shards2d[n].astype(bf16) + zero
        return lax.dynamic_update_index_in_dim(lax.empty((N_CHIPS,) + sh.shape, bf16), sh, chip[0], 0)

    sems, bufs, tok = gather_start([own_slot(n, jnp.zeros((), bf16)) for n in groups["ffn1"]], [[0, 1, 2]],
                                   "gather_start_ffn1")
    gsem = {"ffn1": sems[0]}
    pending = dict(zip(groups["ffn1"], bufs))
    rest = [n for g in ("in", "mix", "ffn2") for n in groups[g]]
    sems, bufs, tok_rest = gather_start([own_slot(n, tok[0, 0].astype(bf16)) for n in rest],
                                 [[rest.index(n) for n in groups[g]] for g in ("in", "mix", "ffn2")], "gather_start_rest")
    gsem.update(zip(("in", "mix", "ffn2"), sems))
    pending.update(zip(rest, bufs))

    def arrive(gs, after):
        got = []
        for g in gs:
            got += gather_wait([pending[n] for n in groups[g]], gsem[g], after, "gather_wait_" + g)
        return gather_forward(got, "gather_forward_" + gs[0])

    bin4 = b_in.reshape(N_CHIPS, 1, 2 * d)
    ws_b = sgu_w_s[0].astype(bf16)
    bs_c = sgu_b_s[0][:, :, None]
    cols, mats, cdec, cos, sin = retention_constants(ret_decay_logit[0], t, dk, tok_rest[0, 0])

    wg1, wu1, wd1 = [_pair_shards(w) for w in arrive(["ffn1"], cos)]
    x1, g1, u1 = ffn_fwd(xs, ffn1_norm, wg1, wu1, wd1, "ffn1_fwd")
    win, = arrive(["in"], x1)
    proj, hb2 = inproj_fwd(x1, mix_norm, win, bin4, cos, sin)
    late = []
    for g in ("mix", "ffn2"):
        late += gather_wait([pending[n] for n in groups[g]], gsem[g], proj, "gather_wait_" + g)
    fsems, late, ftok = forward_start(late, "forward_start_mix")
    a = sgu_fwd(proj, sgu_norm_g, sgu_norm_b, ws_b, bs_c, ftok)
    r, rn = ret_fwd(proj, cols, mats, cdec, a)
    wa, wb, wo, wg2, wu2, wd2 = forward_wait(late, fsems, rn, "forward_wait_mix")
    wa, wb, wo = [w.reshape(d, d) for w in (wa, wb, wo)]
    wg2, wu2, wd2 = [_pair_shards(w) for w in (wg2, wu2, wd2)]
    x2, ba, br = mix_fwd(a, rn, proj, wa, wb, wo, x1)
    loss_blk, dx3, d_final, g2, u2 = ffn_fwd_loss(x2, ffn2_norm, wg2, wu2, wd2, final_norm.reshape(1, d), tgt, "ffn2_fwd")

    sent, swaps = {}, {}
    out_g, out_d, out_m, out_v = {}, {}, {}, {}

    def reduce_plane(g, after):
        gsems, own, lands, _ = sent[g]
        own, lands = exchange_wait(own, lands, gsems, after, "exchange_wait_" + g)
        plane = [sum_partials(chip, o, l, "sum_" + n) for n, o, l in zip(groups[g], own, lands)]
        swaps[g] = swap_start(plane, "swap_start_" + g)
        return swaps[g][3]

    def update(g, after):
        ssems, plane, lands, _ = swaps[g]
        plane, other = swap_wait(plane, lands, ssems, after, "swap_wait_" + g)
        for n, mine, sib in zip(groups[g], plane, other):
            res = adamw_shard(mine, sib, shards2d[n], _shard2d(mom[n], n), _shard2d(var[n], n), "adamw_" + n)
            out_g[n], out_d[n], out_m[n], out_v[n] = [_unshard(o, n) for o in res]
        return res[0]

    dx2, dg2, du2, act2, hb3, dyb2, d_ffn2n = ffn_bwd_act(dx3, x2, ffn2_norm, g2, u2, wg2, wu2, wd2, "ffn2_bwd_act", tok)
    sent["ffn2"] = exchange_start(ffn_weight_grads(hb3, dyb2, dg2, du2, act2, "ffn2_grad", tok), "exchange_start_ffn2")
    da, drn, dga, dgb, mixb, dba, dbr, dx2b = mix_bwd_act(dx2, ba, br, proj, wa, wb, wo, sent["ffn2"][3])
    tg = min(t, 2048)
    row = pl.BlockSpec((tg, d), lambda s, i: (i, 0))

    def square_grad(xa, ya, name):
        return tn_matmul(xa, [ya], row, [row], 1, d, [d], t, tg, name, tok).reshape(N_CHIPS, d // N_CHIPS, d)

    g_mix = [square_grad(a, dba, "grad_w_branch_a"), square_grad(rn, dbr, "grad_w_branch_b"),
             square_grad(mixb, dx2b, "grad_w_out")]
    dua, dva, d_ws, d_bs, d_sng, d_snb = sgu_bwd(da, proj, sgu_norm_g, sgu_norm_b, ws_b, bs_c, sent["ffn2"][3])
    dq, dkr, dv, dgr, dlg = ret_bwd(drn, r, proj, cols, mats, cdec, cos, sin)
    segs = [dua, dva, dq, dkr, dv, dgr, dga, dgb]
    dx1, d_bin, d_mixn = inproj_bwd_act(segs, win, x1, mix_norm, dx2)
    g_in = None
    for s in range(N_CHIPS):
        g_in = tn_matmul(hb2, [segs[2 * s], segs[2 * s + 1]], row, [row, row], 1, d, [d, d], t, tg, "grad_w_in_%d" % s,
                         tok, (g_in, s, N_CHIPS))
    groups["mix_in"] = groups["mix"] + groups["in"]
    sent["mix_in"] = exchange_start(g_mix + [g_in], "exchange_start_mix_in")
    grad_x, dg1, du1, act1, hb1, dyb1, d_ffn1n = ffn_bwd_act(dx1, xs, ffn1_norm, g1, u1, wg1, wu1, wd1, "ffn1_bwd_act",
                                                              sent["mix_in"][3])
    dlogit = dlg[:, 0:2, 0].T * jax.nn.sigmoid(-ret_decay_logit[0].astype(f32))
    small_g = dict(ffn1_norm=d_ffn1n, mix_norm=d_mixn, b_in=d_bin, sgu_norm_g=d_sng, sgu_norm_b=d_snb, sgu_w_s=d_ws,
                   sgu_b_s=d_bs, ret_decay_logit=dlogit, ffn2_norm=d_ffn2n, final_norm=d_final)
    shapes = [p[n].shape for n in SMALL] + [(1,)]
    small_sems, small_blk, small_land, small_tok = small_start(
        _pack([small_g[n] for n in SMALL] + [loss_blk[0, 0:1]], shapes))

    def send_one(which, grad):
        n = "ffn1_" + which
        groups[n] = (n,)
        sent[n] = exchange_start([grad], "exchange_start_" + n)
        return sent[n][3]

    ffn_weight_grads(hb1, dyb1, dg1, du1, act1, "ffn1_grad", small_tok, send_one)

    after = reduce_plane("ffn2", sent["ffn1_w_down"][3])
    after = reduce_plane("mix_in", after)
    after = update("ffn2", after)
    g8 = small_wait(small_blk, small_land, small_sems, after)
    no_state = [jnp.zeros((1,), f32)]
    sg, sd, sm, sv = adamw_small(g8, _pack([p[n] for n in SMALL] + no_state, shapes),
                                 _pack([mom[n] for n in SMALL] + no_state, shapes),
                                 _pack([var[n] for n in SMALL] + no_state, shapes))
    for res, blockv in ((out_g, sg), (out_d, sd), (out_m, sm), (out_v, sv)):
        for n, val in zip(SMALL, _unpack(blockv, shapes)):
            res[n] = val
    loss = _unpack(sg, shapes)[-1][0]
    after = update("mix_in", sg)
    after = reduce_plane("ffn1_w_gate", after)
    after = reduce_plane("ffn1_w_up", after)
    after = update("ffn1_w_gate", after)
    after = reduce_plane("ffn1_w_down", after)
    after = update("ffn1_w_up", after)
    update("ffn1_w_down", after)

    return (loss, grad_x[None], *[out_g[n] for n in WEIGHTS], *[out_d[n] for n in WEIGHTS],
            *[out_m[n] for n in WEIGHTS], *[out_v[n] for n in WEIGHTS])
```

```python
import jax
import jax.numpy as jnp
from jax import lax
from jax.experimental import pallas as pl
from jax.experimental.pallas import tpu as pltpu

f32 = jnp.float32
bf16 = jnp.bfloat16

SGU_CHUNK = 128
CHUNK = 256
RET_HEADS = 4
SGU_GROUPS = 4
ROPE_BASE = 10000.0
NORM_EPS = 1e-6
ADAM_LR = 0.001
ADAM_B1 = 0.9
ADAM_B2 = 0.999
ADAM_EPS = 1e-08
ADAM_WD = 0.01
ADAM_STEP = 10
N_CHIPS = 4
N_DEV = 8
MESH = pl.DeviceIdType.MESH
VMEM_LIMIT = 52 * 1024 * 1024
VMEM_LIMIT_WIDE = 62 * 1024 * 1024

_NT = (((1,), (1,)), ((), ()))
_TN = (((0,), (0,)), ((), ()))


def _cparams(limit=None):
    return pltpu.CompilerParams(vmem_limit_bytes=VMEM_LIMIT if limit is None else limit)


def _row_tile(t):
    return 512 if t >= 2048 else t // 2


def _dot(a, b):
    return jnp.dot(a, b, preferred_element_type=f32)


def _dot_nt(a, b):
    return lax.dot_general(a, b, _NT, preferred_element_type=f32)


def _dot_tn(a, b):
    return lax.dot_general(a, b, _TN, preferred_element_type=f32)


def _rms(x, g):
    r = lax.rsqrt(jnp.mean(x * x, axis=-1, keepdims=True) + NORM_EPS)
    xh = x * r
    return xh * g, xh, r


def _rms_bwd(dy, xh, r, g):
    dxh = dy * g
    return r * (dxh - xh * jnp.mean(dxh * xh, axis=-1, keepdims=True))


def _sigmoid(x):
    return jax.nn.sigmoid(x)


def _dsilu(g, sg):
    return sg * (1.0 + g * (1.0 - sg))


def _gelu(x):
    return 0.5 * x * (1.0 + lax.erf(x * 0.7071067811865476))


def _dgelu(x):
    return 0.5 * (1.0 + lax.erf(x * 0.7071067811865476)) + x * jnp.exp(-0.5 * x * x) * 0.3989422804014327


def _zero_at_first_step(*refs):
    @pl.when(pl.program_id(0) == 0)
    def _():
        for ref in refs:
            ref[...] = jnp.zeros_like(ref)


def _ffn_tile(t):
    return 256 if t >= 2048 else t // 2


def _ffn_fwd_rows(xx, ng_ref, wg_ref, wu_ref, wd_ref, g_ref, u_ref):
    y, _, _ = _rms(xx, ng_ref[...])
    h = y.astype(bf16)
    acc = None
    for s in range(wg_ref.shape[0]):
        g = _dot_nt(h, wg_ref[s])
        u = _dot_nt(h, wu_ref[s])
        g_ref[s] = g.astype(bf16)
        u_ref[s] = u.astype(bf16)
        part = _dot((g * _sigmoid(g) * u).astype(bf16), wd_ref[s])
        acc = part if acc is None else acc + part
    return xx + 0.5 * acc


def ffn_fwd(x, ng, wg, wu, wd, name):
    t, d = x.shape
    ns, fs, _ = wg.shape
    tm = _ffn_tile(t)

    def body(x_ref, ng_ref, wg_ref, wu_ref, wd_ref, xo_ref, g_ref, u_ref):
        xo_ref[...] = _ffn_fwd_rows(x_ref[...], ng_ref, wg_ref, wu_ref, wd_ref, g_ref, u_ref)

    row = pl.BlockSpec((tm, d), lambda i: (i, 0))
    shard = pl.BlockSpec((ns, tm, fs), lambda i: (0, i, 0))
    wspec = pl.BlockSpec((ns, fs, d), lambda i: (0, 0, 0), pipeline_mode=pl.Buffered(1))
    return pl.pallas_call(
        body, name=name, grid=(t // tm,),
        in_specs=[row, pl.BlockSpec((1, d), lambda i: (0, 0)), wspec, wspec, wspec],
        out_specs=[row, shard, shard],
        out_shape=[jax.ShapeDtypeStruct((t, d), f32), jax.ShapeDtypeStruct((ns, t, fs), bf16),
                   jax.ShapeDtypeStruct((ns, t, fs), bf16)],
        compiler_params=_cparams(),
    )(x, ng, wg, wu, wd)


def ffn_fwd_loss(x, ng, wg, wu, wd, fng, tgt, name):
    t, d = x.shape
    ns, fs, _ = wg.shape
    tm = _ffn_tile(t)

    def body(x_ref, ng_ref, wg_ref, wu_ref, wd_ref, fng_ref, t_ref, loss_ref, dx_ref, dfn_ref, g_ref, u_ref):
        _zero_at_first_step(loss_ref, dfn_ref)
        x3 = _ffn_fwd_rows(x_ref[...], ng_ref, wg_ref, wu_ref, wd_ref, g_ref, u_ref)
        y, xh, r = _rms(x3, fng_ref[...])
        diff = y - t_ref[...]
        part = 0.5 * jnp.sum(jnp.sum(diff * diff, axis=0, keepdims=True), axis=1, keepdims=True) / d
        loss_ref[...] += jnp.broadcast_to(part, (1, 128))
        dy = diff * (1.0 / d)
        dx_ref[...] = _rms_bwd(dy, xh, r, fng_ref[...])
        dfn_ref[...] += jnp.sum(dy * xh, axis=0, keepdims=True)

    row = pl.BlockSpec((tm, d), lambda i: (i, 0))
    vec = pl.BlockSpec((1, d), lambda i: (0, 0))
    shard = pl.BlockSpec((ns, tm, fs), lambda i: (0, i, 0))
    wspec = pl.BlockSpec((ns, fs, d), lambda i: (0, 0, 0), pipeline_mode=pl.Buffered(1))
    return pl.pallas_call(
        body, name=name, grid=(t // tm,),
        in_specs=[row, vec, wspec, wspec, wspec, vec, row],
        out_specs=[pl.BlockSpec((1, 128), lambda i: (0, 0)), row, vec, shard, shard],
        out_shape=[jax.ShapeDtypeStruct((1, 128), f32), jax.ShapeDtypeStruct((t, d), f32), jax.ShapeDtypeStruct((1, d), f32),
                   jax.ShapeDtypeStruct((ns, t, fs), bf16), jax.ShapeDtypeStruct((ns, t, fs), bf16)],
        compiler_params=_cparams(),
    )(x, ng, wg, wu, wd, fng, tgt)


def ffn_bwd_act(dxo, x, ng, g, u, wg, wu, wd, name, dep):
    t, d = x.shape
    ns, fs, _ = wg.shape
    tm = _ffn_tile(t)

    def body(dxo_ref, x_ref, ng_ref, g_ref, u_ref, wg_ref, wu_ref, wd_ref, dep_ref,
             dx_ref, dg_ref, du_ref, act_ref, hb_ref, dyb_ref, dng_ref):
        _zero_at_first_step(dng_ref)
        dxo = dxo_ref[...]
        dyb = (0.5 * dxo).astype(bf16)
        dyb_ref[...] = dyb
        dh = None
        for s in range(ns):
            dact = _dot_nt(dyb, wd_ref[s])
            gg = g_ref[s].astype(f32)
            uu = u_ref[s].astype(f32)
            sg = _sigmoid(gg)
            sil = gg * sg
            dgb = (dact * uu * _dsilu(gg, sg)).astype(bf16)
            dub = (dact * sil).astype(bf16)
            dg_ref[s] = dgb
            du_ref[s] = dub
            act_ref[s] = (sil * uu).astype(bf16)
            part = _dot(dgb, wg_ref[s]) + _dot(dub, wu_ref[s])
            dh = part if dh is None else dh + part
        y, xh, r = _rms(x_ref[...], ng_ref[...])
        hb_ref[...] = y.astype(bf16)
        dx_ref[...] = dxo + _rms_bwd(dh, xh, r, ng_ref[...])
        dng_ref[...] += jnp.sum(dh * xh, axis=0, keepdims=True)

    row = pl.BlockSpec((tm, d), lambda i: (i, 0))
    shard = pl.BlockSpec((ns, tm, fs), lambda i: (0, i, 0))
    wspec = pl.BlockSpec((ns, fs, d), lambda i: (0, 0, 0), pipeline_mode=pl.Buffered(1))
    vec = pl.BlockSpec((1, d), lambda i: (0, 0))
    return pl.pallas_call(
        body, name=name, grid=(t // tm,),
        in_specs=[row, row, vec, shard, shard, wspec, wspec, wspec, _ANY],
        out_specs=[row, shard, shard, shard, row, row, vec],
        out_shape=[jax.ShapeDtypeStruct((t, d), f32)] + [jax.ShapeDtypeStruct((ns, t, fs), bf16)] * 3
        + [jax.ShapeDtypeStruct((t, d), bf16)] * 2 + [jax.ShapeDtypeStruct((1, d), f32)],
        compiler_params=_cparams(VMEM_LIMIT_WIDE),
    )(dxo, x, ng, g, u, wg, wu, wd, dep)


def tn_matmul(xs, ys, x_spec, y_specs, n_shards, k1, k2s, t, tm, name, dep, into=None):
    k2 = sum(k2s)
    ny = len(ys)

    def body(*refs):
        x_ref = refs[0]
        y_refs = refs[1:1 + ny]
        steps = t // tm
        o_ref, acc = (refs[-1], None) if steps == 1 else (refs[-2], refs[-1])
        i = pl.program_id(1)
        xb = x_ref[0] if len(x_ref.shape) == 3 else x_ref[...]
        if steps > 1:
            @pl.when(i == 0)
            def _():
                acc[...] = jnp.zeros_like(acc)

        off = 0
        for y_ref, w in zip(y_refs, k2s):
            yb = y_ref[0] if len(y_ref.shape) == 3 else y_ref[...]
            part = _dot_tn(xb, yb)
            if steps == 1:
                o_ref[0, :, off:off + w] = part.astype(bf16)
            else:
                acc[:, off:off + w] += part
            off += w

        if steps > 1:
            @pl.when(i == steps - 1)
            def _():
                o_ref[0] = acc[...].astype(bf16)

    if into is None:
        slot0, total, extra, aliases = 0, n_shards, [], {}
    else:
        buf, slot0, total = into
        extra = [] if buf is None else [buf]
        aliases = {} if buf is None else {2 + ny: 0}
    return pl.pallas_call(
        body, name=name, grid=(n_shards, t // tm),
        in_specs=[x_spec] + list(y_specs) + [_ANY] * (1 + len(extra)),
        out_specs=pl.BlockSpec((1, k1, k2), lambda s, i: (slot0 + s, 0, 0)),
        out_shape=jax.ShapeDtypeStruct((total, k1, k2), bf16),
        scratch_shapes=[pltpu.VMEM((k1, k2), f32)] if t // tm > 1 else [],
        input_output_aliases=aliases,
        compiler_params=_cparams(),
    )(xs, *ys, dep, *extra)


def _pair_shards(w):
    s4, fs, d = w.shape
    return w.reshape(s4 // 2, 2 * fs, d)


def ffn_weight_grads(hb, dyb, dg, du, act, name, dep, each=None):
    t, d = hb.shape
    s2, _, fs2 = dg.shape
    tm = t
    row = pl.BlockSpec((tm, d), lambda s, i: (i, 0))
    shard = pl.BlockSpec((1, tm, fs2), lambda s, i: (s, i, 0))
    grads = []
    for xa, ya, which in ((dg, hb, "w_gate"), (du, hb, "w_up"), (act, dyb, "w_down")):
        g = tn_matmul(xa, [ya], shard, [row], s2, fs2, [d], t, tm, name + "_" + which, dep)
        g = g.reshape(2 * s2, fs2 // 2, d)
        if each is not None:
            dep = each(which, g)
        grads.append(g)
    return grads


def inproj_fwd(x1, ng, win, bin4, cos, sin):
    t, d = x1.shape
    s4, _, w2 = win.shape
    tm = _row_tile(t)
    dk = d // RET_HEADS
    scale = dk ** -0.5

    def body(x_ref, ng_ref, w_ref, b_ref, cos_ref, sin_ref, p_ref, hb_ref):
        y, _, _ = _rms(x_ref[...], ng_ref[...])
        h = y.astype(bf16)
        hb_ref[...] = h
        for s in range(s4):
            p = _dot(h, w_ref[s]) + b_ref[s]
            if s != 1:
                p_ref[s] = p.astype(bf16)
            else:
                cs, sn = cos_ref[...], sin_ref[...]
                for e in range(2 * RET_HEADS):
                    cols = slice(e * dk, (e + 1) * dk)
                    rot = _rot(p[:, cols], cs, sn)
                    p_ref[s, :, cols] = (rot if e < RET_HEADS else rot * scale).astype(bf16)

    tab = pl.BlockSpec((tm, dk // 2), lambda i: (i, 0))
    return pl.pallas_call(
        body, name="inproj_fwd", grid=(t // tm,),
        in_specs=[pl.BlockSpec((tm, d), lambda i: (i, 0)), pl.BlockSpec((1, d), lambda i: (0, 0)),
                  pl.BlockSpec((s4, d, w2), lambda i: (0, 0, 0), pipeline_mode=pl.Buffered(1)),
                  pl.BlockSpec((s4, 1, w2), lambda i: (0, 0, 0)), tab, tab],
        out_specs=[pl.BlockSpec((s4, tm, w2), lambda i: (0, i, 0)), pl.BlockSpec((tm, d), lambda i: (i, 0))],
        out_shape=[jax.ShapeDtypeStruct((s4, t, w2), bf16), jax.ShapeDtypeStruct((t, d), bf16)],
        compiler_params=_cparams(),
    )(x1, ng, win, bin4, cos, sin)


def _sgu_norm(va, ng, nb):
    gv = _gelu(va)
    mu = jnp.mean(gv, axis=-1, keepdims=True)
    xc = gv - mu
    rstd = lax.rsqrt(jnp.mean(xc * xc, axis=-1, keepdims=True) + NORM_EPS)
    xh = xc * rstd
    return xh, rstd, (xh * ng + nb).astype(bf16)


def sgu_fwd(proj, ng, nb, ws, bs, dep):
    _, t, w2 = proj.shape
    d = w2 // 2
    gd = d // SGU_GROUPS
    tm = _row_tile(t)

    def body(p_ref, ng_ref, nb_ref, ws_ref, bs_ref, dep_ref, a_ref):
        ua = p_ref[0, :, 0:d].astype(f32)
        va = p_ref[0, :, d:w2].astype(f32)
        gu = _gelu(ua)
        _, _, vn = _sgu_norm(va, ng_ref[...], nb_ref[...])
        for c in range(tm // SGU_CHUNK):
            rows = slice(c * SGU_CHUNK, (c + 1) * SGU_CHUNK)
            for g in range(SGU_GROUPS):
                cols = slice(g * gd, (g + 1) * gd)
                sg = _dot(ws_ref[g], vn[rows, cols]) + bs_ref[g]
                a_ref[rows, cols] = (gu[rows, cols] * sg).astype(bf16)

    return pl.pallas_call(
        body, name="sgu_fwd", grid=(t // tm,),
        in_specs=[pl.BlockSpec((1, tm, w2), lambda i: (0, i, 0)), pl.BlockSpec((1, d), lambda i: (0, 0)),
                  pl.BlockSpec((1, d), lambda i: (0, 0)), pl.BlockSpec((SGU_GROUPS, SGU_CHUNK, SGU_CHUNK), lambda i: (0, 0, 0)),
                  pl.BlockSpec((SGU_GROUPS, SGU_CHUNK, 1), lambda i: (0, 0, 0)), _ANY],
        out_specs=pl.BlockSpec((tm, d), lambda i: (i, 0)),
        out_shape=jax.ShapeDtypeStruct((t, d), bf16),
        compiler_params=_cparams(),
    )(proj, ng, nb, ws, bs, dep)


def sgu_bwd(da, proj, ng, nb, ws, bs, dep):
    _, t, w2 = proj.shape
    d = w2 // 2
    gd = d // SGU_GROUPS
    tm = _row_tile(t)

    def body(da_ref, p_ref, ng_ref, nb_ref, ws_ref, bs_ref, dep_ref,
             dua_ref, dva_ref, dws_ref, dbs_ref, dng_ref, dnb_ref, dvn_scr):
        _zero_at_first_step(dws_ref, dbs_ref, dng_ref, dnb_ref)
        ua = p_ref[0, :, 0:d].astype(f32)
        va = p_ref[0, :, d:w2].astype(f32)
        gu = _gelu(ua)
        xh, rstd, vn = _sgu_norm(va, ng_ref[...], nb_ref[...])
        dad = da_ref[...].astype(f32)
        dsb = (dad * gu).astype(bf16)
        for c in range(tm // SGU_CHUNK):
            rows = slice(c * SGU_CHUNK, (c + 1) * SGU_CHUNK)
            for g in range(SGU_GROUPS):
                cols = slice(g * gd, (g + 1) * gd)
                sg = _dot(ws_ref[g], vn[rows, cols]) + bs_ref[g]
                dua_ref[rows, cols] = (dad[rows, cols] * sg * _dgelu(ua[rows, cols])).astype(bf16)
                ds = dsb[rows, cols]
                dvn_scr[rows, cols] = _dot_tn(ws_ref[g], ds)
                dw = _dot_nt(ds, vn[rows, cols])
                db = jnp.sum(ds.astype(f32), axis=1, keepdims=True)
                dws_ref[g] += dw
                dbs_ref[g] += db
        dvn = dvn_scr[...]
        dng_ref[...] += jnp.sum(dvn * xh, axis=0, keepdims=True)
        dnb_ref[...] += jnp.sum(dvn, axis=0, keepdims=True)
        dxh = dvn * ng_ref[...]
        dgv = rstd * (dxh - jnp.mean(dxh, axis=-1, keepdims=True) - xh * jnp.mean(dxh * xh, axis=-1, keepdims=True))
        dva_ref[...] = (dgv * _dgelu(va)).astype(bf16)

    row = pl.BlockSpec((tm, d), lambda i: (i, 0))
    vec = pl.BlockSpec((1, d), lambda i: (0, 0))
    wsp = pl.BlockSpec((SGU_GROUPS, SGU_CHUNK, SGU_CHUNK), lambda i: (0, 0, 0))
    bsp = pl.BlockSpec((SGU_GROUPS, SGU_CHUNK, 1), lambda i: (0, 0, 0))
    return pl.pallas_call(
        body, name="sgu_bwd", grid=(t // tm,),
        in_specs=[row, pl.BlockSpec((1, tm, w2), lambda i: (0, i, 0)), vec, vec, wsp, bsp, _ANY],
        out_specs=[row, row, wsp, bsp, vec, vec],
        out_shape=[jax.ShapeDtypeStruct((t, d), bf16), jax.ShapeDtypeStruct((t, d), bf16),
                   jax.ShapeDtypeStruct((SGU_GROUPS, SGU_CHUNK, SGU_CHUNK), f32), jax.ShapeDtypeStruct((SGU_GROUPS, SGU_CHUNK, 1), f32),
                   jax.ShapeDtypeStruct((1, d), f32), jax.ShapeDtypeStruct((1, d), f32)],
        scratch_shapes=[pltpu.VMEM((tm, d), f32)],
        compiler_params=_cparams(),
    )(da, proj, ng, nb, ws, bs, dep)


def retention_constants(decay_logit, t, dk, zero):
    lg = jax.nn.log_sigmoid(decay_logit.astype(f32) + zero)
    lgf = lg[0][:, None]
    lgb = lg[1][:, None]
    idx = jnp.arange(CHUNK, dtype=f32)[None, :]
    af = jnp.exp((idx + 1.0) * lgf)
    ab = jnp.exp((CHUNK - idx) * lgb)
    kf = jnp.exp((CHUNK - 1.0 - idx) * lgf)
    kb = jnp.exp(idx * lgb)
    cols = jnp.stack([af, ab, kf, kb, af * (idx + 1.0), ab * (CHUNK - idx), kf * (CHUNK - 1.0 - idx), kb * idx], axis=1)
    cols = cols[..., None]
    diff = idx[0][:, None] - idx[0][None, :]
    dfm = jnp.where(diff >= 0, jnp.exp(jnp.maximum(diff, 0.0)[None] * lgf[:, :, None]), 0.0)
    dbm = jnp.where(diff < 0, jnp.exp(jnp.maximum(-diff, 0.0)[None] * lgb[:, :, None]), 0.0)
    mats = jnp.stack([dfm + dbm, dfm * diff[None], dbm * (-diff)[None]], axis=1)
    cdec = jnp.stack([jnp.broadcast_to(jnp.exp(CHUNK * lgf), (RET_HEADS, dk)),
                      jnp.broadcast_to(jnp.exp(CHUNK * lgb), (RET_HEADS, dk))], axis=1)
    theta = ROPE_BASE ** (-jnp.arange(0, dk, 2, dtype=f32) / dk)
    ang = (jnp.arange(t, dtype=f32) + zero)[:, None] * theta[None, :]
    return cols, mats, cdec, jnp.cos(ang), jnp.sin(ang)


def _rot(tr, cos, sin):
    half = tr.shape[-1] // 2
    t1 = tr[:, :half]
    t2 = tr[:, half:]
    return jnp.concatenate([t1 * cos - t2 * sin, t2 * cos + t1 * sin], axis=-1)


def _rot_inv(dt, cos, sin):
    half = dt.shape[-1] // 2
    d1 = dt[:, :half]
    d2 = dt[:, half:]
    return jnp.concatenate([d1 * cos + d2 * sin, d2 * cos - d1 * sin], axis=-1)


def _ret_tile(t):
    return 2048 if t >= 4096 else _row_tile(t)


def _ret_specs(t, d, dk, rt):
    nr = t // rt
    hq = d // dk

    def blk(p, n):
        return (1 - p) * (nr - 1 - n) + p * n

    q_spec = pl.BlockSpec((1, rt, dk), lambda h, p, n: (1, blk(p, n), h))
    k_spec = pl.BlockSpec((1, rt, dk), lambda h, p, n: (1, blk(p, n), hq + h))
    v_spec = pl.BlockSpec((1, rt, dk), lambda h, p, n: (2, blk(p, n), h))
    g_spec = pl.BlockSpec((1, rt, dk), lambda h, p, n: (2, blk(p, n), hq + h))
    tab_spec = pl.BlockSpec((rt, dk // 2), lambda h, p, n: (blk(p, n), 0))
    cols_spec = pl.BlockSpec((1, 8, CHUNK, 1), lambda h, p, n: (h, 0, 0, 0))
    mats_spec = pl.BlockSpec((1, 3, CHUNK, CHUNK), lambda h, p, n: (h, 0, 0, 0))
    cdec_spec = pl.BlockSpec((1, 2, dk), lambda h, p, n: (h, 0, 0))
    in_row = pl.BlockSpec((rt, dk), lambda h, p, n: (blk(p, n), h))
    out_row = pl.BlockSpec((rt, dk), lambda h, p, n: (p * n, h))
    return nr, blk, q_spec, k_spec, v_spec, g_spec, tab_spec, cols_spec, mats_spec, cdec_spec, in_row, out_row


def ret_fwd(proj, cols, mats, cdec, dep):
    _, t, w2 = proj.shape
    d = w2 // 2
    dk = d // RET_HEADS
    rt = _ret_tile(t)
    cpt = rt // CHUNK
    nr, blk, q_spec, k_spec, v_spec, g_spec, _, cols_spec, mats_spec, cdec_spec, _, out_row = _ret_specs(t, d, dk, rt)

    def body(q_ref, k_ref, v_ref, g_ref, cols_ref, mats_ref, cdec_ref, dep_ref, r_ref, rn_ref, sb_scr, st):
        p = pl.program_id(1)
        n = pl.program_id(2)
        af, ab, kf, kb = cols_ref[0, 0], cols_ref[0, 1], cols_ref[0, 2], cols_ref[0, 3]
        cf = cdec_ref[0, 0:1, :]
        cb = cdec_ref[0, 1:2, :]

        @pl.when(n == 0)
        def _():
            st[...] = jnp.zeros_like(st)

        @pl.when(p == 0)
        def _():
            for j in reversed(range(cpt)):
                rows = slice(j * CHUNK, (j + 1) * CHUNK)
                ch = blk(p, n) * cpt + j
                kk = k_ref[0, rows, :].astype(f32)
                sb_scr[ch] = st[...].astype(bf16)
                st[...] = st[...] * cb + _dot_tn((kk * kb).astype(bf16), v_ref[0, rows, :])

        @pl.when(p == 1)
        def _():
            for j in range(cpt):
                rows = slice(j * CHUNK, (j + 1) * CHUNK)
                ch = blk(p, n) * cpt + j
                qb = q_ref[0, rows, :]
                kkb = k_ref[0, rows, :]
                q = qb.astype(f32)
                kk = kkb.astype(f32)
                v = v_ref[0, rows, :]
                pm = (_dot_nt(qb, kkb) * mats_ref[0, 0]).astype(bf16)
                out = (_dot(pm, v) + _dot((q * af).astype(bf16), st[...].astype(bf16))
                       + _dot((q * ab).astype(bf16), sb_scr[ch]))
                st[...] = st[...] * cf + _dot_tn((kk * kf).astype(bf16), v)
                rhat = out * lax.rsqrt(jnp.mean(out * out, axis=-1, keepdims=True) + NORM_EPS)
                gg = g_ref[0, rows, :].astype(f32)
                r_ref[rows, :] = out.astype(bf16)
                rn_ref[rows, :] = (rhat * gg * _sigmoid(gg)).astype(bf16)

    return pl.pallas_call(
        body, name="ret_fwd", grid=(RET_HEADS, 2, nr),
        in_specs=[q_spec, k_spec, v_spec, g_spec, cols_spec, mats_spec, cdec_spec, _ANY],
        out_specs=[out_row, out_row],
        out_shape=[jax.ShapeDtypeStruct((t, d), bf16), jax.ShapeDtypeStruct((t, d), bf16)],
        scratch_shapes=[pltpu.VMEM((t // CHUNK, dk, dk), bf16), pltpu.VMEM((dk, dk), f32)],
        compiler_params=_cparams(),
    )(proj, proj, proj, proj, cols, mats, cdec, dep)


def ret_bwd(drn, r, proj, cols, mats, cdec, cos, sin):
    _, t, w2 = proj.shape
    d = w2 // 2
    dk = d // RET_HEADS
    rt = _ret_tile(t)
    cpt = rt // CHUNK
    nr, blk, q_spec, k_spec, v_spec, g_spec, tab_spec, cols_spec, mats_spec, cdec_spec, in_row, out_row = _ret_specs(t, d, dk, rt)
    scale = dk ** -0.5

    def body(drn_ref, r_ref, q_ref, k_ref, v_ref, g_ref, cos_ref, sin_ref, cols_ref, mats_ref, cdec_ref,
             dq_ref, dk_ref, dv_ref, dg_ref, dlg_ref,
             sb_scr, gf_scr, st_s, st_g, acc_af, acc_ab, acc_vf, acc_vb, acc_sf, acc_sb, dout_scr, dgr_scr):
        p = pl.program_id(1)
        n = pl.program_id(2)
        af, ab, kf, kb = cols_ref[0, 0], cols_ref[0, 1], cols_ref[0, 2], cols_ref[0, 3]
        af1, ab1, kf1, kb1 = cols_ref[0, 4], cols_ref[0, 5], cols_ref[0, 6], cols_ref[0, 7]
        cf = cdec_ref[0, 0:1, :]
        cb = cdec_ref[0, 1:2, :]

        @pl.when(n == 0)
        def _():
            st_s[...] = jnp.zeros_like(st_s)
            st_g[...] = jnp.zeros_like(st_g)

        @pl.when(jnp.logical_and(n == 0, p == 1))
        def _():
            for a in (acc_af, acc_ab, acc_vf, acc_vb, acc_sf, acc_sb):
                a[...] = jnp.zeros_like(a)

        def load(rows):
            cs, sn = cos_ref[rows, :], sin_ref[rows, :]
            q = q_ref[0, rows, :].astype(f32)
            kk = k_ref[0, rows, :].astype(f32)
            rr = r_ref[rows, :].astype(f32)
            rstd = lax.rsqrt(jnp.mean(rr * rr, axis=-1, keepdims=True) + NORM_EPS)
            rhat = rr * rstd
            gg = g_ref[0, rows, :].astype(f32)
            sg = _sigmoid(gg)
            dd = drn_ref[rows, :].astype(f32)
            drhat = dd * gg * sg
            dout = rstd * (drhat - rhat * jnp.mean(drhat * rhat, axis=-1, keepdims=True))
            dgr = dd * rhat * _dsilu(gg, sg)
            return q, kk, dout.astype(bf16), dgr, cs, sn

        @pl.when(p == 0)
        def _():
            for j in reversed(range(cpt)):
                rows = slice(j * CHUNK, (j + 1) * CHUNK)
                ch = blk(p, n) * cpt + j
                q, kk, doutb, dgr, _, _ = load(rows)
                kept = pl.ds(pl.multiple_of(ch * CHUNK, CHUNK), CHUNK)
                dout_scr[kept, :] = doutb
                dgr_scr[kept, :] = dgr.astype(bf16)
                sb_scr[ch] = st_s[...].astype(bf16)
                gf_scr[ch] = st_g[...].astype(bf16)
                st_s[...] = st_s[...] * cb + _dot_tn((kk * kb).astype(bf16), v_ref[0, rows, :])
                st_g[...] = st_g[...] * cf + _dot_tn((q * af).astype(bf16), doutb)

        @pl.when(p == 1)
        def _():
            for j in range(cpt):
                rows = slice(j * CHUNK, (j + 1) * CHUNK)
                ch = blk(p, n) * cpt + j
                kept = pl.ds(pl.multiple_of(ch * CHUNK, CHUNK), CHUNK)
                doutb = dout_scr[kept, :]
                cs, sn = cos_ref[rows, :], sin_ref[rows, :]
                v = v_ref[0, rows, :]
                qb = q_ref[0, rows, :]
                kkb = k_ref[0, rows, :]
                q = qb.astype(f32)
                kk = kkb.astype(f32)
                sf = st_s[...]
                gb = st_g[...]
                sfb = sf.astype(bf16)
                gbb = gb.astype(bf16)
                sbb = sb_scr[ch]
                gfb = gf_scr[ch]
                dmat = mats_ref[0, 0]
                scores = _dot_nt(qb, kkb)
                dpraw = _dot_nt(doutb, v)
                dpb = (dpraw * dmat).astype(bf16)
                pmb = (scores * dmat).astype(bf16)
                x1 = _dot_nt(doutb, sfb)
                x2 = _dot_nt(doutb, sbb)
                y1 = _dot_nt(v, gfb)
                y2 = _dot_nt(v, gbb)
                kdf = (kk * kf).astype(bf16)
                kdb = (kk * kb).astype(bf16)
                dq = _dot(dpb, kkb) + x1 * af + x2 * ab
                dkk = _dot_tn(dpb, qb) + y1 * kf + y2 * kb
                dv = _dot_tn(pmb, doutb) + _dot(kdf, gfb) + _dot(kdb, gbb)
                ps = dpraw * scores
                acc_af[...] += ps * mats_ref[0, 1]
                acc_ab[...] += ps * mats_ref[0, 2]
                acc_vf[...] += x1 * q * af1 + y1 * kk * kf1
                acc_vb[...] += x2 * q * ab1 + y2 * kk * kb1
                acc_sf[...] += gfb.astype(f32) * sf
                acc_sb[...] += gb * sbb.astype(f32)
                st_s[...] = sf * cf + _dot_tn(kdf, v)
                st_g[...] = gb * cb + _dot_tn((q * ab).astype(bf16), doutb)
                dq_ref[rows, :] = _rot_inv(dq, cs, sn).astype(bf16)
                dk_ref[rows, :] = (_rot_inv(dkk, cs, sn) * scale).astype(bf16)
                dv_ref[rows, :] = dv.astype(bf16)
                dg_ref[rows, :] = dgr_scr[kept, :]

        @pl.when(jnp.logical_and(p == 1, n == nr - 1))
        def _():
            tf = jnp.sum(acc_af[...]) + jnp.sum(acc_vf[...]) + CHUNK * jnp.sum(acc_sf[...] * cf)
            tb = jnp.sum(acc_ab[...]) + jnp.sum(acc_vb[...]) + CHUNK * jnp.sum(acc_sb[...] * cb)
            rid = lax.broadcasted_iota(jnp.int32, (8, 128), 0)
            dlg_ref[0] = jnp.where(rid == 0, tf, jnp.where(rid == 1, tb, 0.0))

    nch = t // CHUNK
    return pl.pallas_call(
        body, name="ret_bwd", grid=(RET_HEADS, 2, nr),
        in_specs=[in_row, in_row, q_spec, k_spec, v_spec, g_spec, tab_spec, tab_spec, cols_spec, mats_spec, cdec_spec],
        out_specs=[out_row, out_row, out_row, out_row, pl.BlockSpec((1, 8, 128), lambda h, p, n: (h, 0, 0))],
        out_shape=[jax.ShapeDtypeStruct((t, d), bf16)] * 4 + [jax.ShapeDtypeStruct((RET_HEADS, 8, 128), f32)],
        scratch_shapes=[pltpu.VMEM((nch, dk, dk), bf16), pltpu.VMEM((nch, dk, dk), bf16),
                        pltpu.VMEM((dk, dk), f32), pltpu.VMEM((dk, dk), f32),
                        pltpu.VMEM((CHUNK, CHUNK), f32), pltpu.VMEM((CHUNK, CHUNK), f32),
                        pltpu.VMEM((CHUNK, dk), f32), pltpu.VMEM((CHUNK, dk), f32),
                        pltpu.VMEM((dk, dk), f32), pltpu.VMEM((dk, dk), f32),
                        pltpu.VMEM((t, dk), bf16), pltpu.VMEM((t, dk), bf16)],
        compiler_params=_cparams(VMEM_LIMIT_WIDE),
    )(drn, r, proj, proj, proj, proj, cos, sin, cols, mats, cdec)


def mix_fwd(a, rn, proj, wa, wb, wo, x1):
    t, d = x1.shape
    tm = _row_tile(t)

    def body(a_ref, rn_ref, p_ref, wa_ref, wb_ref, wo_ref, x_ref, xo_ref, ba_ref, br_ref):
        ba = _dot(a_ref[...], wa_ref[...])
        br = _dot(rn_ref[...], wb_ref[...])
        sa = _sigmoid(p_ref[0, :, 0:d].astype(f32))
        sb = _sigmoid(p_ref[0, :, d:2 * d].astype(f32))
        mix = (sa * ba + sb * br).astype(bf16)
        xo_ref[...] = x_ref[...] + _dot(mix, wo_ref[...])
        ba_ref[...] = ba.astype(bf16)
        br_ref[...] = br.astype(bf16)

    row = pl.BlockSpec((tm, d), lambda i: (i, 0))
    wsp = pl.BlockSpec((d, d), lambda i: (0, 0))
    return pl.pallas_call(
        body, name="mix_fwd", grid=(t // tm,),
        in_specs=[row, row, pl.BlockSpec((1, tm, 2 * d), lambda i: (3, i, 0)), wsp, wsp, wsp, row],
        out_specs=[row, row, row],
        out_shape=[jax.ShapeDtypeStruct((t, d), f32), jax.ShapeDtypeStruct((t, d), bf16), jax.ShapeDtypeStruct((t, d), bf16)],
        compiler_params=_cparams(),
    )(a, rn, proj, wa, wb, wo, x1)


def mix_bwd_act(dx2, ba, br, proj, wa, wb, wo, dep):
    t, d = dx2.shape
    tm = _row_tile(t)

    def body(dx_ref, ba_ref, br_ref, p_ref, wa_ref, wb_ref, wo_ref, dep_ref,
             da_ref, drn_ref, dga_ref, dgb_ref, mix_ref, dba_ref, dbr_ref, dxb_ref):
        dxb = dx_ref[...].astype(bf16)
        dxb_ref[...] = dxb
        dmix = _dot_nt(dxb, wo_ref[...])
        ba = ba_ref[...].astype(f32)
        br = br_ref[...].astype(f32)
        sa = _sigmoid(p_ref[0, :, 0:d].astype(f32))
        sb = _sigmoid(p_ref[0, :, d:2 * d].astype(f32))
        mix_ref[...] = (sa * ba + sb * br).astype(bf16)
        dba = (dmix * sa).astype(bf16)
        dbr = (dmix * sb).astype(bf16)
        dba_ref[...] = dba
        dbr_ref[...] = dbr
        dga_ref[...] = (dmix * ba * sa * (1.0 - sa)).astype(bf16)
        dgb_ref[...] = (dmix * br * sb * (1.0 - sb)).astype(bf16)
        da_ref[...] = _dot_nt(dba, wa_ref[...]).astype(bf16)
        drn_ref[...] = _dot_nt(dbr, wb_ref[...]).astype(bf16)

    row = pl.BlockSpec((tm, d), lambda i: (i, 0))
    wsp = pl.BlockSpec((d, d), lambda i: (0, 0))
    return pl.pallas_call(
        body, name="mix_bwd_act", grid=(t // tm,),
        in_specs=[row, row, row, pl.BlockSpec((1, tm, 2 * d), lambda i: (3, i, 0)), wsp, wsp, wsp, _ANY],
        out_specs=[row] * 8,
        out_shape=[jax.ShapeDtypeStruct((t, d), bf16)] * 8,
        compiler_params=_cparams(),
    )(dx2, ba, br, proj, wa, wb, wo, dep)


def inproj_bwd_act(segs, win, x1, ng, dx2):
    t, d = x1.shape
    s4 = win.shape[0]
    tm = _row_tile(t)
    nseg = len(segs)

    def body(*refs):
        seg_refs = refs[:nseg]
        w_ref, x_ref, ng_ref, dx2_ref, dx1_ref, db_ref, dng_ref = refs[nseg:]
        _zero_at_first_step(db_ref, dng_ref)
        dh = None
        for e, sr in enumerate(seg_refs):
            sb = sr[...]
            part = _dot_nt(sb, w_ref[e // 2, :, (e % 2) * d:(e % 2 + 1) * d])
            dh = part if dh is None else dh + part
            db_ref[e] += jnp.sum(sb.astype(f32), axis=0, keepdims=True)
        _, xh, r = _rms(x_ref[...], ng_ref[...])
        dx1_ref[...] = dx2_ref[...] + _rms_bwd(dh, xh, r, ng_ref[...])
        dng_ref[...] += jnp.sum(dh * xh, axis=0, keepdims=True)

    row = pl.BlockSpec((tm, d), lambda i: (i, 0))
    vec = pl.BlockSpec((1, d), lambda i: (0, 0))
    return pl.pallas_call(
        body, name="inproj_bwd_act", grid=(t // tm,),
        in_specs=[row] * nseg + [pl.BlockSpec((s4, d, 2 * d), lambda i: (0, 0, 0), pipeline_mode=pl.Buffered(1)),
                                 row, vec, row],
        out_specs=[row, pl.BlockSpec((nseg, 1, d), lambda i: (0, 0, 0)), vec],
        out_shape=[jax.ShapeDtypeStruct((t, d), f32), jax.ShapeDtypeStruct((nseg, 1, d), f32),
                   jax.ShapeDtypeStruct((1, d), f32)],
        compiler_params=_cparams(VMEM_LIMIT_WIDE),
    )(*segs, win, x1, ng, dx2)


def _place():
    return lax.axis_index("x"), lax.axis_index("y"), lax.axis_index("c")


def _other_chips(x, y):
    return [(1 - x, y), (x, 1 - y), (1 - x, 1 - y)]


_ANY = pl.BlockSpec(memory_space=pl.ANY)


_HBM = pl.BlockSpec(memory_space=pltpu.HBM)
_SEM = pl.BlockSpec(memory_space=pltpu.SEMAPHORE)
_EFFECT = pltpu.SideEffectType.DATAFLOW_SIDE_EFFECTING


def _hbm(a):
    return pltpu.with_memory_space_constraint(a, pltpu.HBM)


def _half_rows(ref, c):
    half = ref.shape[1] // 2
    return pl.ds(pl.multiple_of(c * half, 16), half)


def _chip_copy(src, dst, send_sem, recv_sem, chip, c):
    return pltpu.make_async_remote_copy(src_ref=src, dst_ref=dst, send_sem=send_sem, recv_sem=recv_sem,
                                        device_id=(chip[0], chip[1], c), device_id_type=MESH)


def gather_start(bufs, groups, name):
    nb, ng = len(bufs), len(groups)

    def body(*refs):
        ins = refs[:nb]
        sems = refs[nb:nb + 2 * ng]
        token = refs[-1]
        x, y, c = _place()
        k = 2 * x + y
        for gi, grp in enumerate(groups):
            for wi, w in enumerate(grp):
                mine = ins[w].at[k, _half_rows(ins[w], c)]
                for j, chip in enumerate(_other_chips(x, y)):
                    _chip_copy(mine, mine, sems[2 * gi].at[3 * wi + j], sems[2 * gi + 1].at[3 * wi + j], chip, c).start()
        token[...] = jnp.zeros_like(token)

    sem_shapes = []
    for grp in groups:
        sem_shapes += [pltpu.SemaphoreType.DMA((3 * len(grp),)), pltpu.SemaphoreType.DMA((3 * len(grp),))]
    outs = pl.pallas_call(
        body, name=name,
        out_shape=sem_shapes + [pltpu.HBM(b.shape, b.dtype) for b in bufs] + [jax.ShapeDtypeStruct((8, 128), f32)],
        in_specs=[_HBM] * nb,
        out_specs=[_SEM] * (2 * ng) + [_HBM] * nb + [pl.BlockSpec(memory_space=pltpu.VMEM)],
        input_output_aliases={w: 2 * ng + w for w in range(nb)},
        compiler_params=pltpu.CompilerParams(has_side_effects=_EFFECT),
    )(*[_hbm(b) for b in bufs])
    sems = [(outs[2 * gi], outs[2 * gi + 1]) for gi in range(ng)]
    return sems, list(outs[2 * ng:2 * ng + nb]), outs[-1]


def gather_wait(bufs, sems, after, name):
    n = len(bufs)

    def body(*refs):
        ins = refs[:n]
        send_sems, recv_sems = refs[n], refs[n + 1]
        x, y, c = _place()
        k = 2 * x + y
        for wi in range(n):
            half = _half_rows(ins[wi], c)
            for j, chip in enumerate(_other_chips(x, y)):
                cp = _chip_copy(ins[wi].at[k, half], ins[wi].at[2 * chip[0] + chip[1], half], send_sems.at[3 * wi + j],
                                recv_sems.at[3 * wi + j], chip, c)
                cp.wait_send()
                cp.wait_recv()

    outs = pl.pallas_call(
        body, name=name,
        out_shape=[pltpu.HBM(b.shape, b.dtype) for b in bufs],
        in_specs=[_HBM] * n + [_SEM, _SEM, _ANY],
        out_specs=[_HBM] * n,
        input_output_aliases={i: i for i in range(n)},
        compiler_params=pltpu.CompilerParams(has_side_effects=_EFFECT),
    )(*bufs, sems[0], sems[1], after)
    return list(outs)


def gather_forward(bufs, name):
    n = len(bufs)

    def body(*refs):
        ins = refs[n:2 * n]
        send_sems, recv_sems = refs[2 * n], refs[2 * n + 1]
        x, y, c = _place()
        copies = []
        for wi in range(n):
            for j, chip in enumerate(_other_chips(x, y)):
                kp = 2 * chip[0] + chip[1]
                got = ins[wi].at[kp, _half_rows(ins[wi], c)]
                cp = pltpu.make_async_remote_copy(
                    src_ref=got, dst_ref=got, send_sem=send_sems.at[3 * wi + j], recv_sem=recv_sems.at[3 * wi + j],
                    device_id=(x, y, 1 - c), device_id_type=MESH)
                cp.start()
                copies.append((cp, wi, kp, j))
        for cp, wi, kp, j in copies:
            cp.wait_send()
            theirs = ins[wi].at[kp, _half_rows(ins[wi], 1 - c)]
            pltpu.make_async_remote_copy(
                src_ref=theirs, dst_ref=theirs, send_sem=send_sems.at[3 * wi + j], recv_sem=recv_sems.at[3 * wi + j],
                device_id=(x, y, 1 - c), device_id_type=MESH).wait_recv()

    outs = pl.pallas_call(
        body, name=name,
        out_shape=[jax.ShapeDtypeStruct(b.shape, b.dtype) for b in bufs],
        in_specs=[_ANY] * n, out_specs=[_ANY] * n,
        input_output_aliases={i: i for i in range(n)},
        scratch_shapes=[pltpu.SemaphoreType.DMA((3 * n,)), pltpu.SemaphoreType.DMA((3 * n,))],
    )(*bufs)
    return list(outs)


def forward_start(bufs, name):
    n = len(bufs)

    def body(*refs):
        x, y, c = _place()
        for wi in range(n):
            for j, chip in enumerate(_other_chips(x, y)):
                got = refs[wi].at[2 * chip[0] + chip[1], _half_rows(refs[wi], c)]
                _sibling_copy(got, got, refs[n].at[3 * wi + j], refs[n + 1].at[3 * wi + j]).start()
        refs[-1][...] = jnp.zeros_like(refs[-1])

    return _split_start(body, name, 3 * n, list(bufs))


def forward_wait(bufs, sems, after, name):
    n = len(bufs)

    def body(*refs):
        x, y, c = _place()
        for wi in range(n):
            for j, chip in enumerate(_other_chips(x, y)):
                kp = 2 * chip[0] + chip[1]
                got = refs[wi].at[kp, _half_rows(refs[wi], c)]
                theirs = refs[wi].at[kp, _half_rows(refs[wi], 1 - c)]
                _sibling_copy(got, got, refs[n].at[3 * wi + j], refs[n + 1].at[3 * wi + j]).wait_send()
                _sibling_copy(theirs, theirs, refs[n].at[3 * wi + j], refs[n + 1].at[3 * wi + j]).wait_recv()

    return _split_wait(body, name, list(bufs), sems, after)


def exchange_start(grads, name):
    n = len(grads)
    lands = [lax.empty((3,) + g.shape[1:], g.dtype) for g in grads]

    def body(*refs):
        ins = refs[:n]
        land = refs[n:2 * n]
        send_sems, recv_sems = refs[2 * n], refs[2 * n + 1]
        token = refs[-1]
        x, y, c = _place()
        for wi in range(n):
            for j, chip in enumerate(_other_chips(x, y)):
                _chip_copy(ins[wi].at[2 * chip[0] + chip[1]], land[wi].at[j], send_sems.at[3 * wi + j],
                           recv_sems.at[3 * wi + j], chip, c).start()
        token[...] = jnp.zeros_like(token)

    outs = pl.pallas_call(
        body, name=name,
        out_shape=[pltpu.SemaphoreType.DMA((3 * n,)), pltpu.SemaphoreType.DMA((3 * n,))]
        + [pltpu.HBM(g.shape, g.dtype) for g in grads] + [pltpu.HBM(l.shape, l.dtype) for l in lands]
        + [jax.ShapeDtypeStruct((8, 128), f32)],
        in_specs=[_HBM] * (2 * n),
        out_specs=[_SEM, _SEM] + [_HBM] * (2 * n) + [pl.BlockSpec(memory_space=pltpu.VMEM)],
        input_output_aliases={i: 2 + i for i in range(2 * n)},
        compiler_params=pltpu.CompilerParams(has_side_effects=_EFFECT),
    )(*[_hbm(g) for g in grads], *[_hbm(l) for l in lands])
    return (outs[0], outs[1]), list(outs[2:2 + n]), list(outs[2 + n:2 + 2 * n]), outs[-1]


def exchange_wait(grads, lands, sems, after, name):
    n = len(grads)

    def body(*refs):
        ins = refs[:n]
        land = refs[n:2 * n]
        send_sems, recv_sems = refs[2 * n], refs[2 * n + 1]
        x, y, c = _place()
        for wi in range(n):
            for j, chip in enumerate(_other_chips(x, y)):
                cp = _chip_copy(ins[wi].at[2 * chip[0] + chip[1]], land[wi].at[j], send_sems.at[3 * wi + j],
                                recv_sems.at[3 * wi + j], chip, c)
                cp.wait_send()
                cp.wait_recv()

    outs = pl.pallas_call(
        body, name=name,
        out_shape=[pltpu.HBM(g.shape, g.dtype) for g in grads] + [pltpu.HBM(l.shape, l.dtype) for l in lands],
        in_specs=[_HBM] * (2 * n) + [_SEM, _SEM, _ANY],
        out_specs=[_HBM] * (2 * n),
        input_output_aliases={i: i for i in range(2 * n)},
        compiler_params=pltpu.CompilerParams(has_side_effects=_EFFECT),
    )(*grads, *lands, sems[0], sems[1], after)
    return list(outs[:n]), list(outs[n:])


def _split_start(body, name, n_sems, operands):
    n = len(operands)
    outs = pl.pallas_call(
        body, name=name,
        out_shape=[pltpu.SemaphoreType.DMA((n_sems,)), pltpu.SemaphoreType.DMA((n_sems,))]
        + [pltpu.HBM(o.shape, o.dtype) for o in operands] + [jax.ShapeDtypeStruct((8, 128), f32)],
        in_specs=[_HBM] * n,
        out_specs=[_SEM, _SEM] + [_HBM] * n + [pl.BlockSpec(memory_space=pltpu.VMEM)],
        input_output_aliases={i: 2 + i for i in range(n)},
        compiler_params=pltpu.CompilerParams(has_side_effects=_EFFECT),
    )(*[_hbm(o) for o in operands])
    return (outs[0], outs[1]), list(outs[2:2 + n]), outs[-1]


def _split_wait(body, name, operands, sems, after):
    n = len(operands)
    outs = pl.pallas_call(
        body, name=name,
        out_shape=[pltpu.HBM(o.shape, o.dtype) for o in operands],
        in_specs=[_HBM] * n + [_SEM, _SEM, _ANY],
        out_specs=[_HBM] * n,
        input_output_aliases={i: i for i in range(n)},
        compiler_params=pltpu.CompilerParams(has_side_effects=_EFFECT),
    )(*operands, sems[0], sems[1], after)
    return list(outs)


def _sibling_copy(src, dst, send_sem, recv_sem):
    x, y, c = _place()
    return pltpu.make_async_remote_copy(src_ref=src, dst_ref=dst, send_sem=send_sem, recv_sem=recv_sem,
                                        device_id=(x, y, 1 - c), device_id_type=MESH)


def swap_start(parts, name):
    n = len(parts)

    def body(*refs):
        for w in range(n):
            _sibling_copy(refs[w], refs[n + w], refs[2 * n].at[w], refs[2 * n + 1].at[w]).start()
        refs[-1][...] = jnp.zeros_like(refs[-1])

    sems, ops, token = _split_start(body, name, n, list(parts) + [lax.empty(p.shape, p.dtype) for p in parts])
    return sems, ops[:n], ops[n:], token


def swap_wait(parts, lands, sems, after, name):
    n = len(parts)

    def body(*refs):
        for w in range(n):
            cp = _sibling_copy(refs[w], refs[n + w], refs[2 * n].at[w], refs[2 * n + 1].at[w])
            cp.wait_send()
            cp.wait_recv()

    outs = _split_wait(body, name, list(parts) + list(lands), sems, after)
    return outs[:n], outs[n:]


def _all_peers(x, y, c):
    return [(1 - x if m & 4 else x, 1 - y if m & 2 else y, 1 - c if m & 1 else c) for m in range(1, N_DEV)]


def small_start(block):
    land = jnp.broadcast_to(block[None], (N_DEV,) + block.shape)

    def body(b_ref, land_ref, send_sems, recv_sems, b_thru, land_thru, token):
        x, y, c = _place()
        me = 4 * x + 2 * y + c
        for m, peer in enumerate(_all_peers(x, y, c)):
            pltpu.make_async_remote_copy(src_ref=b_ref, dst_ref=land_ref.at[me], send_sem=send_sems.at[m],
                                         recv_sem=recv_sems.at[m], device_id=peer, device_id_type=MESH).start()
        token[...] = jnp.zeros_like(token)

    sems, ops, token = _split_start(body, "small_start", N_DEV - 1, [block, land])
    return sems, ops[0], ops[1], token


def small_wait(block, land, sems, after):
    def body(b_ref, land_ref, send_sems, recv_sems, after_ref, b_thru, land_thru):
        x, y, c = _place()
        for m, (px, py, pc) in enumerate(_all_peers(x, y, c)):
            cp = pltpu.make_async_remote_copy(src_ref=b_ref, dst_ref=land_ref.at[4 * px + 2 * py + pc],
                                              send_sem=send_sems.at[m], recv_sem=recv_sems.at[m],
                                              device_id=(px, py, pc), device_id_type=MESH)
            cp.wait_send()
            cp.wait_recv()

    return _split_wait(body, "small_wait", [block, land], sems, after)[1]


def _adamw(w, g, m, v):
    m = ADAM_B1 * m + (1.0 - ADAM_B1) * g
    v = ADAM_B2 * v + (1.0 - ADAM_B2) * (g * g)
    m_hat = m / (1.0 - ADAM_B1 ** ADAM_STEP)
    v_hat = v / (1.0 - ADAM_B2 ** ADAM_STEP)
    delta = -ADAM_LR * (m_hat / (jnp.sqrt(v_hat) + ADAM_EPS) + ADAM_WD * w)
    return delta, m, v


EW_BLOCK_BYTES = 2 * 1024 * 1024


def _ew_tile(rows, cols):
    for cand in (512, 352, 256, 176, 128, 64, 32, 16, 8):
        if rows % cand == 0 and cand * cols * 4 <= EW_BLOCK_BYTES:
            return cand
    return rows


def sum_partials(chip, own, land, name):
    _, r, c = own.shape
    tr = _ew_tile(r, c)

    def body(k_ref, own_ref, p_ref, o_ref):
        o_ref[...] = ((own_ref[0].astype(f32) + p_ref[0].astype(f32)) + p_ref[1].astype(f32)) + p_ref[2].astype(f32)

    return pl.pallas_call(
        body, name=name,
        grid_spec=pltpu.PrefetchScalarGridSpec(
            num_scalar_prefetch=1, grid=(r // tr,),
            in_specs=[pl.BlockSpec((1, tr, c), lambda i, k: (k[0], i, 0)), pl.BlockSpec((3, tr, c), lambda i, k: (0, i, 0))],
            out_specs=pl.BlockSpec((tr, c), lambda i, k: (i, 0))),
        out_shape=jax.ShapeDtypeStruct((r, c), f32),
        compiler_params=_cparams(),
    )(chip, own, land)


def adamw_shard(p_mine, p_sibling, w, m, v, name):
    r, c = w.shape
    tr = _ew_tile(r, c)

    def body(a_ref, b_ref, w_ref, m_ref, v_ref, g_ref, d_ref, mo_ref, vo_ref):
        g = a_ref[...] + b_ref[...]
        delta, mn, vn = _adamw(w_ref[...], g, m_ref[...], v_ref[...])
        g_ref[...] = g
        d_ref[...] = delta
        mo_ref[...] = mn
        vo_ref[...] = vn

    blk = pl.BlockSpec((tr, c), lambda i: (i, 0))
    return pl.pallas_call(
        body, name=name, grid=(r // tr,),
        in_specs=[blk] * 5, out_specs=[blk] * 4,
        out_shape=[jax.ShapeDtypeStruct((r, c), f32)] * 4,
        compiler_params=_cparams(),
    )(p_mine, p_sibling, w, m, v)


def adamw_small(g8, w, m, v):
    _, r, lanes = g8.shape

    def body(g_ref, w_ref, m_ref, v_ref, go_ref, d_ref, mo_ref, vo_ref):
        g = g_ref[0]
        for i in range(1, N_DEV):
            g = g + g_ref[i]
        delta, mn, vn = _adamw(w_ref[...], g, m_ref[...], v_ref[...])
        go_ref[...] = g
        d_ref[...] = delta
        mo_ref[...] = mn
        vo_ref[...] = vn

    return pl.pallas_call(
        body, name="adamw_small",
        out_shape=[jax.ShapeDtypeStruct((r, lanes), f32)] * 4,
        compiler_params=_cparams(),
    )(g8, w, m, v)


def _size(shape):
    n = 1
    for e in shape:
        n *= e
    return n


def _pack_rows(shapes):
    rows = [-(-_size(s) // 1024) * 8 for s in shapes]
    return rows, sum(rows)


def _pack(arrs, shapes):
    rows, _ = _pack_rows(shapes)
    parts = [jnp.pad(a.reshape(-1).astype(f32), (0, r * 128 - _size(s))).reshape(r, 128)
             for a, s, r in zip(arrs, shapes, rows)]
    return jnp.concatenate(parts, axis=0)


def _unpack(block, shapes):
    rows, _ = _pack_rows(shapes)
    out, off = [], 0
    for s, r in zip(shapes, rows):
        out.append(block[off:off + r].reshape(-1)[:_size(s)].reshape(s))
        off += r
    return out


TRANSPOSED = ("ffn1_w_gate", "ffn1_w_up", "ffn2_w_gate", "ffn2_w_up")


def _shard2d(a, n):
    return a[0].T if n in TRANSPOSED else a[0]


def _unshard(a, n):
    return (a.T if n in TRANSPOSED else a)[None]


BIG = ("ffn1_w_gate", "ffn1_w_up", "ffn1_w_down", "w_in", "w_branch_a", "w_branch_b", "w_out",
       "ffn2_w_gate", "ffn2_w_up", "ffn2_w_down")
SMALL = ("ffn1_norm", "mix_norm", "b_in", "sgu_norm_g", "sgu_norm_b", "sgu_w_s", "sgu_b_s", "ret_decay_logit",
         "ffn2_norm", "final_norm")
WEIGHTS = ("ffn1_norm", "ffn1_w_gate", "ffn1_w_up", "ffn1_w_down", "mix_norm", "w_in", "b_in", "sgu_norm_g",
           "sgu_norm_b", "sgu_w_s", "sgu_b_s", "ret_decay_logit", "w_branch_a", "w_branch_b", "w_out", "ffn2_norm",
           "ffn2_w_gate", "ffn2_w_up", "ffn2_w_down", "final_norm")


def kernel(x, ffn1_norm, ffn1_w_gate, ffn1_w_up, ffn1_w_down, mix_norm, w_in, b_in, sgu_norm_g, sgu_norm_b, sgu_w_s, sgu_b_s, ret_decay_logit, w_branch_a, w_branch_b, w_out, ffn2_norm, ffn2_w_gate, ffn2_w_up, ffn2_w_down, final_norm, loss_target, m_ffn1_norm, m_ffn1_w_gate, m_ffn1_w_up, m_ffn1_w_down, m_mix_norm, m_w_in, m_b_in, m_sgu_norm_g, m_sgu_norm_b, m_sgu_w_s, m_sgu_b_s, m_ret_decay_logit, m_w_branch_a, m_w_branch_b, m_w_out, m_ffn2_norm, m_ffn2_w_gate, m_ffn2_w_up, m_ffn2_w_down, m_final_norm, v_ffn1_norm, v_ffn1_w_gate, v_ffn1_w_up, v_ffn1_w_down, v_mix_norm, v_w_in, v_b_in, v_sgu_norm_g, v_sgu_norm_b, v_sgu_w_s, v_sgu_b_s, v_ret_decay_logit, v_w_branch_a, v_w_branch_b, v_w_out, v_ffn2_norm, v_ffn2_w_gate, v_ffn2_w_up, v_ffn2_w_down, v_final_norm):
    p = dict(ffn1_norm=ffn1_norm, ffn1_w_gate=ffn1_w_gate, ffn1_w_up=ffn1_w_up, ffn1_w_down=ffn1_w_down,
             mix_norm=mix_norm, w_in=w_in, b_in=b_in, sgu_norm_g=sgu_norm_g, sgu_norm_b=sgu_norm_b, sgu_w_s=sgu_w_s,
             sgu_b_s=sgu_b_s, ret_decay_logit=ret_decay_logit, w_branch_a=w_branch_a, w_branch_b=w_branch_b,
             w_out=w_out, ffn2_norm=ffn2_norm, ffn2_w_gate=ffn2_w_gate, ffn2_w_up=ffn2_w_up, ffn2_w_down=ffn2_w_down,
             final_norm=final_norm)
    mom = dict(ffn1_norm=m_ffn1_norm, ffn1_w_gate=m_ffn1_w_gate, ffn1_w_up=m_ffn1_w_up, ffn1_w_down=m_ffn1_w_down,
               mix_norm=m_mix_norm, w_in=m_w_in, b_in=m_b_in, sgu_norm_g=m_sgu_norm_g, sgu_norm_b=m_sgu_norm_b,
               sgu_w_s=m_sgu_w_s, sgu_b_s=m_sgu_b_s, ret_decay_logit=m_ret_decay_logit, w_branch_a=m_w_branch_a,
               w_branch_b=m_w_branch_b, w_out=m_w_out, ffn2_norm=m_ffn2_norm, ffn2_w_gate=m_ffn2_w_gate,
               ffn2_w_up=m_ffn2_w_up, ffn2_w_down=m_ffn2_w_down, final_norm=m_final_norm)
    var = dict(ffn1_norm=v_ffn1_norm, ffn1_w_gate=v_ffn1_w_gate, ffn1_w_up=v_ffn1_w_up, ffn1_w_down=v_ffn1_w_down,
               mix_norm=v_mix_norm, w_in=v_w_in, b_in=v_b_in, sgu_norm_g=v_sgu_norm_g, sgu_norm_b=v_sgu_norm_b,
               sgu_w_s=v_sgu_w_s, sgu_b_s=v_sgu_b_s, ret_decay_logit=v_ret_decay_logit, w_branch_a=v_w_branch_a,
               w_branch_b=v_w_branch_b, w_out=v_w_out, ffn2_norm=v_ffn2_norm, ffn2_w_gate=v_ffn2_w_gate,
               ffn2_w_up=v_ffn2_w_up, ffn2_w_down=v_ffn2_w_down, final_norm=v_final_norm)

    xs = x[0]
    tgt = loss_target[0]
    t, d = xs.shape
    dk = d // RET_HEADS

    shards2d = {n: _shard2d(p[n], n) for n in BIG}
    chip = (2 * lax.axis_index("x") + lax.axis_index("y")).astype(jnp.int32).reshape(1)
    groups = {"ffn1": ("ffn1_w_gate", "ffn1_w_up", "ffn1_w_down"), "in": ("w_in",),
              "mix": ("w_branch_a", "w_branch_b", "w_out"), "ffn2": ("ffn2_w_gate", "ffn2_w_up", "ffn2_w_down")}
    def own_slot(n, zero):
        sh = shards2d[n].astype(bf16) + zero
        return lax.dynamic_update_index_in_dim(lax.empty((N_CHIPS,) + sh.shape, bf16), sh, chip[0], 0)

    sems, bufs, tok = gather_start([own_slot(n, jnp.zeros((), bf16)) for n in groups["ffn1"]], [[0, 1, 2]],
                                   "gather_start_ffn1")
    gsem = {"ffn1": sems[0]}
    pending = dict(zip(groups["ffn1"], bufs))
    rest = [n for g in ("in", "mix", "ffn2") for n in groups[g]]
    sems, bufs, tok_rest = gather_start([own_slot(n, tok[0, 0].astype(bf16)) for n in rest],
                                 [[rest.index(n) for n in groups[g]] for g in ("in", "mix", "ffn2")], "gather_start_rest")
    gsem.update(zip(("in", "mix", "ffn2"), sems))
    pending.update(zip(rest, bufs))

    def arrive(gs, after):
        got = []
        for g in gs:
            got += gather_wait([pending[n] for n in groups[g]], gsem[g], after, "gather_wait_" + g)
        return gather_forward(got, "gather_forward_" + gs[0])

    bin4 = b_in.reshape(N_CHIPS, 1, 2 * d)
    ws_b = sgu_w_s[0].astype(bf16)
    bs_c = sgu_b_s[0][:, :, None]
    cols, mats, cdec, cos, sin = retention_constants(ret_decay_logit[0], t, dk, tok_rest[0, 0])

    wg1, wu1, wd1 = [_pair_shards(w) for w in arrive(["ffn1"], cos)]
    x1, g1, u1 = ffn_fwd(xs, ffn1_norm, wg1, wu1, wd1, "ffn1_fwd")
    win, = arrive(["in"], x1)
    proj, hb2 = inproj_fwd(x1, mix_norm, win, bin4, cos, sin)
    late = []
    for g in ("mix", "ffn2"):
        late += gather_wait([pending[n] for n in groups[g]], gsem[g], proj, "gather_wait_" + g)
    fsems, late, ftok = forward_start(late, "forward_start_mix")
    a = sgu_fwd(proj, sgu_norm_g, sgu_norm_b, ws_b, bs_c, ftok)
    r, rn = ret_fwd(proj, cols, mats, cdec, a)
    wa, wb, wo, wg2, wu2, wd2 = forward_wait(late, fsems, rn, "forward_wait_mix")
    wa, wb, wo = [w.reshape(d, d) for w in (wa, wb, wo)]
    wg2, wu2, wd2 = [_pair_shards(w) for w in (wg2, wu2, wd2)]
    x2, ba, br = mix_fwd(a, rn, proj, wa, wb, wo, x1)
    loss_blk, dx3, d_final, g2, u2 = ffn_fwd_loss(x2, ffn2_norm, wg2, wu2, wd2, final_norm.reshape(1, d), tgt, "ffn2_fwd")

    sent, swaps = {}, {}
    out_g, out_d, out_m, out_v = {}, {}, {}, {}

    def reduce_plane(g, after):
        gsems, own, lands, _ = sent[g]
        own, lands = exchange_wait(own, lands, gsems, after, "exchange_wait_" + g)
        plane = [sum_partials(chip, o, l, "sum_" + n) for n, o, l in zip(groups[g], own, lands)]
        swaps[g] = swap_start(plane, "swap_start_" + g)
        return swaps[g][3]

    def update(g, after):
        ssems, plane, lands, _ = swaps[g]
        plane, other = swap_wait(plane, lands, ssems, after, "swap_wait_" + g)
        for n, mine, sib in zip(groups[g], plane, other):
            res = adamw_shard(mine, sib, shards2d[n], _shard2d(mom[n], n), _shard2d(var[n], n), "adamw_" + n)
            out_g[n], out_d[n], out_m[n], out_v[n] = [_unshard(o, n) for o in res]
        return res[0]

    dx2, dg2, du2, act2, hb3, dyb2, d_ffn2n = ffn_bwd_act(dx3, x2, ffn2_norm, g2, u2, wg2, wu2, wd2, "ffn2_bwd_act", tok)
    sent["ffn2"] = exchange_start(ffn_weight_grads(hb3, dyb2, dg2, du2, act2, "ffn2_grad", tok), "exchange_start_ffn2")
    da, drn, dga, dgb, mixb, dba, dbr, dx2b = mix_bwd_act(dx2, ba, br, proj, wa, wb, wo, sent["ffn2"][3])
    tg = min(t, 2048)
    row = pl.BlockSpec((tg, d), lambda s, i: (i, 0))

    def square_grad(xa, ya, name):
        return tn_matmul(xa, [ya], row, [row], 1, d, [d], t, tg, name, tok).reshape(N_CHIPS, d // N_CHIPS, d)

    g_mix = [square_grad(a, dba, "grad_w_branch_a"), square_grad(rn, dbr, "grad_w_branch_b"),
             square_grad(mixb, dx2b, "grad_w_out")]
    dua, dva, d_ws, d_bs, d_sng, d_snb = sgu_bwd(da, proj, sgu_norm_g, sgu_norm_b, ws_b, bs_c, sent["ffn2"][3])
    dq, dkr, dv, dgr, dlg = ret_bwd(drn, r, proj, cols, mats, cdec, cos, sin)
    segs = [dua, dva, dq, dkr, dv, dgr, dga, dgb]
    dx1, d_bin, d_mixn = inproj_bwd_act(segs, win, x1, mix_norm, dx2)
    g_in = None
    for s in range(N_CHIPS):
        g_in = tn_matmul(hb2, [segs[2 * s], segs[2 * s + 1]], row, [row, row], 1, d, [d, d], t, tg, "grad_w_in_%d" % s,
                         tok, (g_in, s, N_CHIPS))
    groups["mix_in"] = groups["mix"] + groups["in"]
    sent["mix_in"] = exchange_start(g_mix + [g_in], "exchange_start_mix_in")
    grad_x, dg1, du1, act1, hb1, dyb1, d_ffn1n = ffn_bwd_act(dx1, xs, ffn1_norm, g1, u1, wg1, wu1, wd1, "ffn1_bwd_act",
                                                              sent["mix_in"][3])
    dlogit = dlg[:, 0:2, 0].T * jax.nn.sigmoid(-ret_decay_logit[0].astype(f32))
    small_g = dict(ffn1_norm=d_ffn1n, mix_norm=d_mixn, b_in=d_bin, sgu_norm_g=d_sng, sgu_norm_b=d_snb, sgu_w_s=d_ws,
                   sgu_b_s=d_bs, ret_decay_logit=dlogit, ffn2_norm=d_ffn2n, final_norm=d_final)
    shapes = [p[n].shape for n in SMALL] + [(1,)]
    small_sems, small_blk, small_land, small_tok = small_start(
        _pack([small_g[n] for n in SMALL] + [loss_blk[0, 0:1]], shapes))

    def send_one(which, grad):
        n = "ffn1_" + which
        groups[n] = (n,)
        sent[n] = exchange_start([grad], "exchange_start_" + n)
        return sent[n][3]

    ffn_weight_grads(hb1, dyb1, dg1, du1, act1, "ffn1_grad", small_tok, send_one)

    after = reduce_plane("ffn2", sent["ffn1_w_down"][3])
    after = reduce_plane("mix_in", after)
    after = update("ffn2", after)
    g8 = small_wait(small_blk, small_land, small_sems, after)
    no_state = [jnp.zeros((1,), f32)]
    sg, sd, sm, sv = adamw_small(g8, _pack([p[n] for n in SMALL] + no_state, shapes),
                                 _pack([mom[n] for n in SMALL] + no_state, shapes),
                                 _pack([var[n] for n in SMALL] + no_state, shapes))
    for res, blockv in ((out_g, sg), (out_d, sd), (out_m, sm), (out_v, sv)):
        for n, val in zip(SMALL, _unpack(blockv, shapes)):
            res[n] = val
    loss = _unpack(sg, shapes)[-1][0]
    after = update("mix_in", sg)
    after = reduce_plane("ffn1_w_gate", after)
    after = reduce_plane("ffn1_w_up", after)
    after = update("ffn1_w_gate", after)
    after = reduce_plane("ffn1_w_down", after)
    after = update("ffn1_w_up", after)
    update("ffn1_w_down", after)

    return (loss, grad_x[None], *[out_g[n] for n in WEIGHTS], *[out_d[n] for n in WEIGHTS],
            *[out_m[n] for n in WEIGHTS], *[out_v[n] for n in WEIGHTS])
```

```python
import jax
import jax.numpy as jnp
from jax import lax
from jax.experimental import pallas as pl
from jax.experimental.pallas import tpu as pltpu

f32 = jnp.float32
bf16 = jnp.bfloat16

SGU_CHUNK = 128
CHUNK = 256
RET_HEADS = 4
SGU_GROUPS = 4
ROPE_BASE = 10000.0
NORM_EPS = 1e-6
ADAM_LR = 0.001
ADAM_B1 = 0.9
ADAM_B2 = 0.999
ADAM_EPS = 1e-08
ADAM_WD = 0.01
ADAM_STEP = 10
N_CHIPS = 4
N_DEV = 8
MESH = pl.DeviceIdType.MESH
VMEM_LIMIT = 52 * 1024 * 1024
VMEM_LIMIT_WIDE = 62 * 1024 * 1024

_NT = (((1,), (1,)), ((), ()))
_TN = (((0,), (0,)), ((), ()))


def _cparams(limit=None):
    return pltpu.CompilerParams(vmem_limit_bytes=VMEM_LIMIT if limit is None else limit)


def _row_tile(t):
    return 512 if t >= 2048 else t // 2


def _dot(a, b):
    return jnp.dot(a, b, preferred_element_type=f32)


def _dot_nt(a, b):
    return lax.dot_general(a, b, _NT, preferred_element_type=f32)


def _dot_tn(a, b):
    return lax.dot_general(a, b, _TN, preferred_element_type=f32)


def _rms(x, g):
    r = lax.rsqrt(jnp.mean(x * x, axis=-1, keepdims=True) + NORM_EPS)
    xh = x * r
    return xh * g, xh, r


def _rms_bwd(dy, xh, r, g):
    dxh = dy * g
    return r * (dxh - xh * jnp.mean(dxh * xh, axis=-1, keepdims=True))


def _sigmoid(x):
    return jax.nn.sigmoid(x)


def _dsilu(g, sg):
    return sg * (1.0 + g * (1.0 - sg))


def _gelu(x):
    return 0.5 * x * (1.0 + lax.erf(x * 0.7071067811865476))


def _dgelu(x):
    return 0.5 * (1.0 + lax.erf(x * 0.7071067811865476)) + x * jnp.exp(-0.5 * x * x) * 0.3989422804014327


def _zero_at_first_step(*refs):
    @pl.when(pl.program_id(0) == 0)
    def _():
        for ref in refs:
            ref[...] = jnp.zeros_like(ref)


def _ffn_tile(t):
    return 256 if t >= 2048 else t // 2


def _ffn_fwd_rows(xx, ng_ref, wg_ref, wu_ref, wd_ref, g_ref, u_ref):
    y, _, _ = _rms(xx, ng_ref[...])
    h = y.astype(bf16)
    acc = None
    for s in range(wg_ref.shape[0]):
        g = _dot_nt(h, wg_ref[s])
        u = _dot_nt(h, wu_ref[s])
        g_ref[s] = g.astype(bf16)
        u_ref[s] = u.astype(bf16)
        part = _dot((g * _sigmoid(g) * u).astype(bf16), wd_ref[s])
        acc = part if acc is None else acc + part
    return xx + 0.5 * acc


def ffn_fwd(x, ng, wg, wu, wd, name):
    t, d = x.shape
    ns, fs, _ = wg.shape
    tm = _ffn_tile(t)

    def body(x_ref, ng_ref, wg_ref, wu_ref, wd_ref, xo_ref, g_ref, u_ref):
        xo_ref[...] = _ffn_fwd_rows(x_ref[...], ng_ref, wg_ref, wu_ref, wd_ref, g_ref, u_ref)

    row = pl.BlockSpec((tm, d), lambda i: (i, 0))
    shard = pl.BlockSpec((ns, tm, fs), lambda i: (0, i, 0))
    wspec = pl.BlockSpec((ns, fs, d), lambda i: (0, 0, 0), pipeline_mode=pl.Buffered(1))
    return pl.pallas_call(
        body, name=name, grid=(t // tm,),
        in_specs=[row, pl.BlockSpec((1, d), lambda i: (0, 0)), wspec, wspec, wspec],
        out_specs=[row, shard, shard],
        out_shape=[jax.ShapeDtypeStruct((t, d), f32), jax.ShapeDtypeStruct((ns, t, fs), bf16),
                   jax.ShapeDtypeStruct((ns, t, fs), bf16)],
        compiler_params=_cparams(),
    )(x, ng, wg, wu, wd)


def ffn_fwd_loss(x, ng, wg, wu, wd, fng, tgt, name):
    t, d = x.shape
    ns, fs, _ = wg.shape
    tm = _ffn_tile(t)

    def body(x_ref, ng_ref, wg_ref, wu_ref, wd_ref, fng_ref, t_ref, loss_ref, dx_ref, dfn_ref, g_ref, u_ref):
        _zero_at_first_step(loss_ref, dfn_ref)
        x3 = _ffn_fwd_rows(x_ref[...], ng_ref, wg_ref, wu_ref, wd_ref, g_ref, u_ref)
        y, xh, r = _rms(x3, fng_ref[...])
        diff = y - t_ref[...]
        part = 0.5 * jnp.sum(jnp.sum(diff * diff, axis=0, keepdims=True), axis=1, keepdims=True) / d
        loss_ref[...] += jnp.broadcast_to(part, (1, 128))
        dy = diff * (1.0 / d)
        dx_ref[...] = _rms_bwd(dy, xh, r, fng_ref[...])
        dfn_ref[...] += jnp.sum(dy * xh, axis=0, keepdims=True)

    row = pl.BlockSpec((tm, d), lambda i: (i, 0))
    vec = pl.BlockSpec((1, d), lambda i: (0, 0))
    shard = pl.BlockSpec((ns, tm, fs), lambda i: (0, i, 0))
    wspec = pl.BlockSpec((ns, fs, d), lambda i: (0, 0, 0), pipeline_mode=pl.Buffered(1))
    return pl.pallas_call(
        body, name=name, grid=(t // tm,),
        in_specs=[row, vec, wspec, wspec, wspec, vec, row],
        out_specs=[pl.BlockSpec((1, 128), lambda i: (0, 0)), row, vec, shard, shard],
        out_shape=[jax.ShapeDtypeStruct((1, 128), f32), jax.ShapeDtypeStruct((t, d), f32), jax.ShapeDtypeStruct((1, d), f32),
                   jax.ShapeDtypeStruct((ns, t, fs), bf16), jax.ShapeDtypeStruct((ns, t, fs), bf16)],
        compiler_params=_cparams(),
    )(x, ng, wg, wu, wd, fng, tgt)


def ffn_bwd_act(dxo, x, ng, g, u, wg, wu, wd, name, dep):
    t, d = x.shape
    ns, fs, _ = wg.shape
    tm = _ffn_tile(t)

    def body(dxo_ref, x_ref, ng_ref, g_ref, u_ref, wg_ref, wu_ref, wd_ref, dep_ref,
             dx_ref, dg_ref, du_ref, act_ref, hb_ref, dyb_ref, dng_ref):
        _zero_at_first_step(dng_ref)
        dxo = dxo_ref[...]
        dyb = (0.5 * dxo).astype(bf16)
        dyb_ref[...] = dyb
        dh = None
        for s in range(ns):
            dact = _dot_nt(dyb, wd_ref[s])
            gg = g_ref[s].astype(f32)
            uu = u_ref[s].astype(f32)
            sg = _sigmoid(gg)
            sil = gg * sg
            dgb = (dact * uu * _dsilu(gg, sg)).astype(bf16)
            dub = (dact * sil).astype(bf16)
            dg_ref[s] = dgb
            du_ref[s] = dub
            act_ref[s] = (sil * uu).astype(bf16)
            part = _dot(dgb, wg_ref[s]) + _dot(dub, wu_ref[s])
            dh = part if dh is None else dh + part
        y, xh, r = _rms(x_ref[...], ng_ref[...])
        hb_ref[...] = y.astype(bf16)
        dx_ref[...] = dxo + _rms_bwd(dh, xh, r, ng_ref[...])
        dng_ref[...] += jnp.sum(dh * xh, axis=0, keepdims=True)

    row = pl.BlockSpec((tm, d), lambda i: (i, 0))
    shard = pl.BlockSpec((ns, tm, fs), lambda i: (0, i, 0))
    wspec = pl.BlockSpec((ns, fs, d), lambda i: (0, 0, 0), pipeline_mode=pl.Buffered(1))
    vec = pl.BlockSpec((1, d), lambda i: (0, 0))
    return pl.pallas_call(
        body, name=name, grid=(t // tm,),
        in_specs=[row, row, vec, shard, shard, wspec, wspec, wspec, _ANY],
        out_specs=[row, shard, shard, shard, row, row, vec],
        out_shape=[jax.ShapeDtypeStruct((t, d), f32)] + [jax.ShapeDtypeStruct((ns, t, fs), bf16)] * 3
        + [jax.ShapeDtypeStruct((t, d), bf16)] * 2 + [jax.ShapeDtypeStruct((1, d), f32)],
        compiler_params=_cparams(VMEM_LIMIT_WIDE),
    )(dxo, x, ng, g, u, wg, wu, wd, dep)


def tn_matmul(xs, ys, x_spec, y_specs, n_shards, k1, k2s, t, tm, name, dep, into=None):
    k2 = sum(k2s)
    ny = len(ys)

    def body(*refs):
        x_ref = refs[0]
        y_refs = refs[1:1 + ny]
        steps = t // tm
        o_ref, acc = (refs[-1], None) if steps == 1 else (refs[-2], refs[-1])
        i = pl.program_id(1)
        xb = x_ref[0] if len(x_ref.shape) == 3 else x_ref[...]
        if steps > 1:
            @pl.when(i == 0)
            def _():
                acc[...] = jnp.zeros_like(acc)

        off = 0
        for y_ref, w in zip(y_refs, k2s):
            yb = y_ref[0] if len(y_ref.shape) == 3 else y_ref[...]
            part = _dot_tn(xb, yb)
            if steps == 1:
                o_ref[0, :, off:off + w] = part.astype(bf16)
            else:
                acc[:, off:off + w] += part
            off += w

        if steps > 1:
            @pl.when(i == steps - 1)
            def _():
                o_ref[0] = acc[...].astype(bf16)

    if into is None:
        slot0, total, extra, aliases = 0, n_shards, [], {}
    else:
        buf, slot0, total = into
        extra = [] if buf is None else [buf]
        aliases = {} if buf is None else {2 + ny: 0}
    return pl.pallas_call(
        body, name=name, grid=(n_shards, t // tm),
        in_specs=[x_spec] + list(y_specs) + [_ANY] * (1 + len(extra)),
        out_specs=pl.BlockSpec((1, k1, k2), lambda s, i: (slot0 + s, 0, 0)),
        out_shape=jax.ShapeDtypeStruct((total, k1, k2), bf16),
        scratch_shapes=[pltpu.VMEM((k1, k2), f32)] if t // tm > 1 else [],
        input_output_aliases=aliases,
        compiler_params=_cparams(),
    )(xs, *ys, dep, *extra)


def _pair_shards(w):
    s4, fs, d = w.shape
    return w.reshape(s4 // 2, 2 * fs, d)


def ffn_weight_grads(hb, dyb, dg, du, act, name, dep, each=None):
    t, d = hb.shape
    s2, _, fs2 = dg.shape
    tm = t
    row = pl.BlockSpec((tm, d), lambda s, i: (i, 0))
    shard = pl.BlockSpec((1, tm, fs2), lambda s, i: (s, i, 0))
    grads = []
    for xa, ya, which in ((dg, hb, "w_gate"), (du, hb, "w_up"), (act, dyb, "w_down")):
        g = tn_matmul(xa, [ya], shard, [row], s2, fs2, [d], t, tm, name + "_" + which, dep)
        g = g.reshape(2 * s2, fs2 // 2, d)
        if each is not None:
            dep = each(which, g)
        grads.append(g)
    return grads


def inproj_fwd(x1, ng, win, bin4, cos, sin, sng, snb, ws, bs):
    t, d = x1.shape
    s4, _, w2 = win.shape
    tm = _row_tile(t)
    dk = d // RET_HEADS
    scale = dk ** -0.5

    def body(x_ref, ng_ref, w_ref, b_ref, cos_ref, sin_ref, sng_ref, snb_ref, ws_ref, bs_ref, p_ref, hb_ref, a_ref):
        y, _, _ = _rms(x_ref[...], ng_ref[...])
        h = y.astype(bf16)
        hb_ref[...] = h
        uv = None
        for s in range(s4):
            p = _dot(h, w_ref[s]) + b_ref[s]
            if s == 0:
                uv = p.astype(bf16)
                p_ref[s] = uv
            elif s != 1:
                p_ref[s] = p.astype(bf16)
            else:
                cs, sn = cos_ref[...], sin_ref[...]
                for e in range(2 * RET_HEADS):
                    cols = slice(e * dk, (e + 1) * dk)
                    rot = _rot(p[:, cols], cs, sn)
                    p_ref[s, :, cols] = (rot if e < RET_HEADS else rot * scale).astype(bf16)
        _sgu_rows(uv[:, 0:d].astype(f32), uv[:, d:w2].astype(f32), sng_ref, snb_ref, ws_ref, bs_ref, a_ref)

    tab = pl.BlockSpec((tm, dk // 2), lambda i: (i, 0))
    row = pl.BlockSpec((tm, d), lambda i: (i, 0))
    vec = pl.BlockSpec((1, d), lambda i: (0, 0))
    return pl.pallas_call(
        body, name="inproj_fwd", grid=(t // tm,),
        in_specs=[row, vec, pl.BlockSpec((s4, d, w2), lambda i: (0, 0, 0), pipeline_mode=pl.Buffered(1)),
                  pl.BlockSpec((s4, 1, w2), lambda i: (0, 0, 0)), tab, tab, vec, vec,
                  pl.BlockSpec((SGU_GROUPS, SGU_CHUNK, SGU_CHUNK), lambda i: (0, 0, 0)),
                  pl.BlockSpec((SGU_GROUPS, SGU_CHUNK, 1), lambda i: (0, 0, 0))],
        out_specs=[pl.BlockSpec((s4, tm, w2), lambda i: (0, i, 0)), row, row],
        out_shape=[jax.ShapeDtypeStruct((s4, t, w2), bf16), jax.ShapeDtypeStruct((t, d), bf16),
                   jax.ShapeDtypeStruct((t, d), bf16)],
        compiler_params=_cparams(),
    )(x1, ng, win, bin4, cos, sin, sng, snb, ws, bs)


def _sgu_norm(va, ng, nb):
    gv = _gelu(va)
    mu = jnp.mean(gv, axis=-1, keepdims=True)
    xc = gv - mu
    rstd = lax.rsqrt(jnp.mean(xc * xc, axis=-1, keepdims=True) + NORM_EPS)
    xh = xc * rstd
    return xh, rstd, (xh * ng + nb).astype(bf16)


def _sgu_rows(ua, va, ng_ref, nb_ref, ws_ref, bs_ref, a_ref):
    tm, d = ua.shape
    gd = d // SGU_GROUPS
    gu = _gelu(ua)
    _, _, vn = _sgu_norm(va, ng_ref[...], nb_ref[...])
    for c in range(tm // SGU_CHUNK):
        rows = slice(c * SGU_CHUNK, (c + 1) * SGU_CHUNK)
        for g in range(SGU_GROUPS):
            cols = slice(g * gd, (g + 1) * gd)
            sg = _dot(ws_ref[g], vn[rows, cols]) + bs_ref[g]
            a_ref[rows, cols] = (gu[rows, cols] * sg).astype(bf16)


def sgu_bwd(da, proj, ng, nb, ws, bs, dep):
    _, t, w2 = proj.shape
    d = w2 // 2
    gd = d // SGU_GROUPS
    tm = _row_tile(t)

    def body(da_ref, p_ref, ng_ref, nb_ref, ws_ref, bs_ref, dep_ref,
             dua_ref, dva_ref, dws_ref, dbs_ref, dng_ref, dnb_ref, dvn_scr):
        _zero_at_first_step(dws_ref, dbs_ref, dng_ref, dnb_ref)
        ua = p_ref[0, :, 0:d].astype(f32)
        va = p_ref[0, :, d:w2].astype(f32)
        gu = _gelu(ua)
        xh, rstd, vn = _sgu_norm(va, ng_ref[...], nb_ref[...])
        dad = da_ref[...].astype(f32)
        dsb = (dad * gu).astype(bf16)
        for c in range(tm // SGU_CHUNK):
            rows = slice(c * SGU_CHUNK, (c + 1) * SGU_CHUNK)
            for g in range(SGU_GROUPS):
                cols = slice(g * gd, (g + 1) * gd)
                sg = _dot(ws_ref[g], vn[rows, cols]) + bs_ref[g]
                dua_ref[rows, cols] = (dad[rows, cols] * sg * _dgelu(ua[rows, cols])).astype(bf16)
                ds = dsb[rows, cols]
                dvn_scr[rows, cols] = _dot_tn(ws_ref[g], ds)
                dw = _dot_nt(ds, vn[rows, cols])
                db = jnp.sum(ds.astype(f32), axis=1, keepdims=True)
                dws_ref[g] += dw
                dbs_ref[g] += db
        dvn = dvn_scr[...]
        dng_ref[...] += jnp.sum(dvn * xh, axis=0, keepdims=True)
        dnb_ref[...] += jnp.sum(dvn, axis=0, keepdims=True)
        dxh = dvn * ng_ref[...]
        dgv = rstd * (dxh - jnp.mean(dxh, axis=-1, keepdims=True) - xh * jnp.mean(dxh * xh, axis=-1, keepdims=True))
        dva_ref[...] = (dgv * _dgelu(va)).astype(bf16)

    row = pl.BlockSpec((tm, d), lambda i: (i, 0))
    vec = pl.BlockSpec((1, d), lambda i: (0, 0))
    wsp = pl.BlockSpec((SGU_GROUPS, SGU_CHUNK, SGU_CHUNK), lambda i: (0, 0, 0))
    bsp = pl.BlockSpec((SGU_GROUPS, SGU_CHUNK, 1), lambda i: (0, 0, 0))
    return pl.pallas_call(
        body, name="sgu_bwd", grid=(t // tm,),
        in_specs=[row, pl.BlockSpec((1, tm, w2), lambda i: (0, i, 0)), vec, vec, wsp, bsp, _ANY],
        out_specs=[row, row, wsp, bsp, vec, vec],
        out_shape=[jax.ShapeDtypeStruct((t, d), bf16), jax.ShapeDtypeStruct((t, d), bf16),
                   jax.ShapeDtypeStruct((SGU_GROUPS, SGU_CHUNK, SGU_CHUNK), f32), jax.ShapeDtypeStruct((SGU_GROUPS, SGU_CHUNK, 1), f32),
                   jax.ShapeDtypeStruct((1, d), f32), jax.ShapeDtypeStruct((1, d), f32)],
        scratch_shapes=[pltpu.VMEM((tm, d), f32)],
        compiler_params=_cparams(),
    )(da, proj, ng, nb, ws, bs, dep)


def retention_constants(decay_logit, t, dk, zero):
    lg = jax.nn.log_sigmoid(decay_logit.astype(f32) + zero)
    lgf = lg[0][:, None]
    lgb = lg[1][:, None]
    idx = jnp.arange(CHUNK, dtype=f32)[None, :]
    af = jnp.exp((idx + 1.0) * lgf)
    ab = jnp.exp((CHUNK - idx) * lgb)
    kf = jnp.exp((CHUNK - 1.0 - idx) * lgf)
    kb = jnp.exp(idx * lgb)
    cols = jnp.stack([af, ab, kf, kb, af * (idx + 1.0), ab * (CHUNK - idx), kf * (CHUNK - 1.0 - idx), kb * idx], axis=1)
    cols = cols[..., None]
    diff = idx[0][:, None] - idx[0][None, :]
    dfm = jnp.where(diff >= 0, jnp.exp(jnp.maximum(diff, 0.0)[None] * lgf[:, :, None]), 0.0)
    dbm = jnp.where(diff < 0, jnp.exp(jnp.maximum(-diff, 0.0)[None] * lgb[:, :, None]), 0.0)
    mats = jnp.stack([dfm + dbm, dfm * diff[None], dbm * (-diff)[None]], axis=1)
    cdec = jnp.stack([jnp.broadcast_to(jnp.exp(CHUNK * lgf), (RET_HEADS, dk)),
                      jnp.broadcast_to(jnp.exp(CHUNK * lgb), (RET_HEADS, dk))], axis=1)
    theta = ROPE_BASE ** (-jnp.arange(0, dk, 2, dtype=f32) / dk)
    ang = (jnp.arange(t, dtype=f32) + zero)[:, None] * theta[None, :]
    return cols, mats, cdec, jnp.cos(ang), jnp.sin(ang)


def _rot(tr, cos, sin):
    half = tr.shape[-1] // 2
    t1 = tr[:, :half]
    t2 = tr[:, half:]
    return jnp.concatenate([t1 * cos - t2 * sin, t2 * cos + t1 * sin], axis=-1)


def _rot_inv(dt, cos, sin):
    half = dt.shape[-1] // 2
    d1 = dt[:, :half]
    d2 = dt[:, half:]
    return jnp.concatenate([d1 * cos + d2 * sin, d2 * cos - d1 * sin], axis=-1)


def _ret_tile(t):
    return 2048 if t >= 4096 else _row_tile(t)


def _ret_specs(t, d, dk, rt):
    nr = t // rt
    hq = d // dk

    def blk(p, n):
        return (1 - p) * (nr - 1 - n) + p * n

    q_spec = pl.BlockSpec((1, rt, dk), lambda h, p, n: (1, blk(p, n), h))
    k_spec = pl.BlockSpec((1, rt, dk), lambda h, p, n: (1, blk(p, n), hq + h))
    v_spec = pl.BlockSpec((1, rt, dk), lambda h, p, n: (2, blk(p, n), h))
    g_spec = pl.BlockSpec((1, rt, dk), lambda h, p, n: (2, blk(p, n), hq + h))
    tab_spec = pl.BlockSpec((rt, dk // 2), lambda h, p, n: (blk(p, n), 0))
    cols_spec = pl.BlockSpec((1, 8, CHUNK, 1), lambda h, p, n: (h, 0, 0, 0))
    mats_spec = pl.BlockSpec((1, 3, CHUNK, CHUNK), lambda h, p, n: (h, 0, 0, 0))
    cdec_spec = pl.BlockSpec((1, 2, dk), lambda h, p, n: (h, 0, 0))
    in_row = pl.BlockSpec((rt, dk), lambda h, p, n: (blk(p, n), h))
    out_row = pl.BlockSpec((rt, dk), lambda h, p, n: (p * n, h))
    return nr, blk, q_spec, k_spec, v_spec, g_spec, tab_spec, cols_spec, mats_spec, cdec_spec, in_row, out_row


def ret_fwd(proj, cols, mats, cdec, dep):
    _, t, w2 = proj.shape
    d = w2 // 2
    dk = d // RET_HEADS
    rt = _ret_tile(t)
    cpt = rt // CHUNK
    nr, blk, q_spec, k_spec, v_spec, g_spec, _, cols_spec, mats_spec, cdec_spec, _, out_row = _ret_specs(t, d, dk, rt)

    def body(q_ref, k_ref, v_ref, g_ref, cols_ref, mats_ref, cdec_ref, dep_ref, r_ref, rn_ref, sb_scr, st):
        p = pl.program_id(1)
        n = pl.program_id(2)
        af, ab, kf, kb = cols_ref[0, 0], cols_ref[0, 1], cols_ref[0, 2], cols_ref[0, 3]
        cf = cdec_ref[0, 0:1, :]
        cb = cdec_ref[0, 1:2, :]

        @pl.when(n == 0)
        def _():
            st[...] = jnp.zeros_like(st)

        @pl.when(p == 0)
        def _():
            for j in reversed(range(cpt)):
                rows = slice(j * CHUNK, (j + 1) * CHUNK)
                ch = blk(p, n) * cpt + j
                kk = k_ref[0, rows, :].astype(f32)
                sb_scr[ch] = st[...].astype(bf16)
                st[...] = st[...] * cb + _dot_tn((kk * kb).astype(bf16), v_ref[0, rows, :])

        @pl.when(p == 1)
        def _():
            for j in range(cpt):
                rows = slice(j * CHUNK, (j + 1) * CHUNK)
                ch = blk(p, n) * cpt + j
                qb = q_ref[0, rows, :]
                kkb = k_ref[0, rows, :]
                q = qb.astype(f32)
                kk = kkb.astype(f32)
                v = v_ref[0, rows, :]
                pm = (_dot_nt(qb, kkb) * mats_ref[0, 0]).astype(bf16)
                out = (_dot(pm, v) + _dot((q * af).astype(bf16), st[...].astype(bf16))
                       + _dot((q * ab).astype(bf16), sb_scr[ch]))
                st[...] = st[...] * cf + _dot_tn((kk * kf).astype(bf16), v)
                rhat = out * lax.rsqrt(jnp.mean(out * out, axis=-1, keepdims=True) + NORM_EPS)
                gg = g_ref[0, rows, :].astype(f32)
                r_ref[rows, :] = out.astype(bf16)
                rn_ref[rows, :] = (rhat * gg * _sigmoid(gg)).astype(bf16)

    return pl.pallas_call(
        body, name="ret_fwd", grid=(RET_HEADS, 2, nr),
        in_specs=[q_spec, k_spec, v_spec, g_spec, cols_spec, mats_spec, cdec_spec, _ANY],
        out_specs=[out_row, out_row],
        out_shape=[jax.ShapeDtypeStruct((t, d), bf16), jax.ShapeDtypeStruct((t, d), bf16)],
        scratch_shapes=[pltpu.VMEM((t // CHUNK, dk, dk), bf16), pltpu.VMEM((dk, dk), f32)],
        compiler_params=_cparams(),
    )(proj, proj, proj, proj, cols, mats, cdec, dep)


def ret_bwd(drn, r, proj, cols, mats, cdec, cos, sin):
    _, t, w2 = proj.shape
    d = w2 // 2
    dk = d // RET_HEADS
    rt = _ret_tile(t)
    cpt = rt // CHUNK
    nr, blk, q_spec, k_spec, v_spec, g_spec, tab_spec, cols_spec, mats_spec, cdec_spec, in_row, out_row = _ret_specs(t, d, dk, rt)
    scale = dk ** -0.5

    def body(drn_ref, r_ref, q_ref, k_ref, v_ref, g_ref, cos_ref, sin_ref, cols_ref, mats_ref, cdec_ref,
             dq_ref, dk_ref, dv_ref, dg_ref, dlg_ref,
             sb_scr, gf_scr, st_s, st_g, acc_af, acc_ab, acc_vf, acc_vb, acc_sf, acc_sb, dout_scr, dgr_scr):
        p = pl.program_id(1)
        n = pl.program_id(2)
        af, ab, kf, kb = cols_ref[0, 0], cols_ref[0, 1], cols_ref[0, 2], cols_ref[0, 3]
        af1, ab1, kf1, kb1 = cols_ref[0, 4], cols_ref[0, 5], cols_ref[0, 6], cols_ref[0, 7]
        cf = cdec_ref[0, 0:1, :]
        cb = cdec_ref[0, 1:2, :]

        @pl.when(n == 0)
        def _():
            st_s[...] = jnp.zeros_like(st_s)
            st_g[...] = jnp.zeros_like(st_g)

        @pl.when(jnp.logical_and(n == 0, p == 1))
        def _():
            for a in (acc_af, acc_ab, acc_vf, acc_vb, acc_sf, acc_sb):
                a[...] = jnp.zeros_like(a)

        def load(rows):
            cs, sn = cos_ref[rows, :], sin_ref[rows, :]
            q = q_ref[0, rows, :].astype(f32)
            kk = k_ref[0, rows, :].astype(f32)
            rr = r_ref[rows, :].astype(f32)
            rstd = lax.rsqrt(jnp.mean(rr * rr, axis=-1, keepdims=True) + NORM_EPS)
            rhat = rr * rstd
            gg = g_ref[0, rows, :].astype(f32)
            sg = _sigmoid(gg)
            dd = drn_ref[rows, :].astype(f32)
            drhat = dd * gg * sg
            dout = rstd * (drhat - rhat * jnp.mean(drhat * rhat, axis=-1, keepdims=True))
            dgr = dd * rhat * _dsilu(gg, sg)
            return q, kk, dout.astype(bf16), dgr, cs, sn

        @pl.when(p == 0)
        def _():
            for j in reversed(range(cpt)):
                rows = slice(j * CHUNK, (j + 1) * CHUNK)
                ch = blk(p, n) * cpt + j
                q, kk, doutb, dgr, _, _ = load(rows)
                kept = pl.ds(pl.multiple_of(ch * CHUNK, CHUNK), CHUNK)
                dout_scr[kept, :] = doutb
                dgr_scr[kept, :] = dgr.astype(bf16)
                sb_scr[ch] = st_s[...].astype(bf16)
                gf_scr[ch] = st_g[...].astype(bf16)
                st_s[...] = st_s[...] * cb + _dot_tn((kk * kb).astype(bf16), v_ref[0, rows, :])
                st_g[...] = st_g[...] * cf + _dot_tn((q * af).astype(bf16), doutb)

        @pl.when(p == 1)
        def _():
            for j in range(cpt):
                rows = slice(j * CHUNK, (j + 1) * CHUNK)
                ch = blk(p, n) * cpt + j
                kept = pl.ds(pl.multiple_of(ch * CHUNK, CHUNK), CHUNK)
                doutb = dout_scr[kept, :]
                cs, sn = cos_ref[rows, :], sin_ref[rows, :]
                v = v_ref[0, rows, :]
                qb = q_ref[0, rows, :]
                kkb = k_ref[0, rows, :]
                q = qb.astype(f32)
                kk = kkb.astype(f32)
                sf = st_s[...]
                gb = st_g[...]
                sfb = sf.astype(bf16)
                gbb = gb.astype(bf16)
                sbb = sb_scr[ch]
                gfb = gf_scr[ch]
                dmat = mats_ref[0, 0]
                scores = _dot_nt(qb, kkb)
                dpraw = _dot_nt(doutb, v)
                dpb = (dpraw * dmat).astype(bf16)
                pmb = (scores * dmat).astype(bf16)
                x1 = _dot_nt(doutb, sfb)
                x2 = _dot_nt(doutb, sbb)
                y1 = _dot_nt(v, gfb)
                y2 = _dot_nt(v, gbb)
                kdf = (kk * kf).astype(bf16)
                kdb = (kk * kb).astype(bf16)
                dq = _dot(dpb, kkb) + x1 * af + x2 * ab
                dkk = _dot_tn(dpb, qb) + y1 * kf + y2 * kb
                dv = _dot_tn(pmb, doutb) + _dot(kdf, gfb) + _dot(kdb, gbb)
                ps = dpraw * scores
                acc_af[...] += ps * mats_ref[0, 1]
                acc_ab[...] += ps * mats_ref[0, 2]
                acc_vf[...] += x1 * q * af1 + y1 * kk * kf1
                acc_vb[...] += x2 * q * ab1 + y2 * kk * kb1
                acc_sf[...] += gfb.astype(f32) * sf
                acc_sb[...] += gb * sbb.astype(f32)
                st_s[...] = sf * cf + _dot_tn(kdf, v)
                st_g[...] = gb * cb + _dot_tn((q * ab).astype(bf16), doutb)
                dq_ref[rows, :] = _rot_inv(dq, cs, sn).astype(bf16)
                dk_ref[rows, :] = (_rot_inv(dkk, cs, sn) * scale).astype(bf16)
                dv_ref[rows, :] = dv.astype(bf16)
                dg_ref[rows, :] = dgr_scr[kept, :]

        @pl.when(jnp.logical_and(p == 1, n == nr - 1))
        def _():
            tf = jnp.sum(acc_af[...]) + jnp.sum(acc_vf[...]) + CHUNK * jnp.sum(acc_sf[...] * cf)
            tb = jnp.sum(acc_ab[...]) + jnp.sum(acc_vb[...]) + CHUNK * jnp.sum(acc_sb[...] * cb)
            rid = lax.broadcasted_iota(jnp.int32, (8, 128), 0)
            dlg_ref[0] = jnp.where(rid == 0, tf, jnp.where(rid == 1, tb, 0.0))

    nch = t // CHUNK
    return pl.pallas_call(
        body, name="ret_bwd", grid=(RET_HEADS, 2, nr),
        in_specs=[in_row, in_row, q_spec, k_spec, v_spec, g_spec, tab_spec, tab_spec, cols_spec, mats_spec, cdec_spec],
        out_specs=[out_row, out_row, out_row, out_row, pl.BlockSpec((1, 8, 128), lambda h, p, n: (h, 0, 0))],
        out_shape=[jax.ShapeDtypeStruct((t, d), bf16)] * 4 + [jax.ShapeDtypeStruct((RET_HEADS, 8, 128), f32)],
        scratch_shapes=[pltpu.VMEM((nch, dk, dk), bf16), pltpu.VMEM((nch, dk, dk), bf16),
                        pltpu.VMEM((dk, dk), f32), pltpu.VMEM((dk, dk), f32),
                        pltpu.VMEM((CHUNK, CHUNK), f32), pltpu.VMEM((CHUNK, CHUNK), f32),
                        pltpu.VMEM((CHUNK, dk), f32), pltpu.VMEM((CHUNK, dk), f32),
                        pltpu.VMEM((dk, dk), f32), pltpu.VMEM((dk, dk), f32),
                        pltpu.VMEM((t, dk), bf16), pltpu.VMEM((t, dk), bf16)],
        compiler_params=_cparams(VMEM_LIMIT_WIDE),
    )(drn, r, proj, proj, proj, proj, cos, sin, cols, mats, cdec)


def mix_fwd(a, rn, proj, wa, wb, wo, x1):
    t, d = x1.shape
    tm = _row_tile(t)

    def body(a_ref, rn_ref, p_ref, wa_ref, wb_ref, wo_ref, x_ref, xo_ref, ba_ref, br_ref):
        ba = _dot(a_ref[...], wa_ref[...])
        br = _dot(rn_ref[...], wb_ref[...])
        sa = _sigmoid(p_ref[0, :, 0:d].astype(f32))
        sb = _sigmoid(p_ref[0, :, d:2 * d].astype(f32))
        mix = (sa * ba + sb * br).astype(bf16)
        xo_ref[...] = x_ref[...] + _dot(mix, wo_ref[...])
        ba_ref[...] = ba.astype(bf16)
        br_ref[...] = br.astype(bf16)

    row = pl.BlockSpec((tm, d), lambda i: (i, 0))
    wsp = pl.BlockSpec((d, d), lambda i: (0, 0))
    return pl.pallas_call(
        body, name="mix_fwd", grid=(t // tm,),
        in_specs=[row, row, pl.BlockSpec((1, tm, 2 * d), lambda i: (3, i, 0)), wsp, wsp, wsp, row],
        out_specs=[row, row, row],
        out_shape=[jax.ShapeDtypeStruct((t, d), f32), jax.ShapeDtypeStruct((t, d), bf16), jax.ShapeDtypeStruct((t, d), bf16)],
        compiler_params=_cparams(),
    )(a, rn, proj, wa, wb, wo, x1)


def mix_bwd_act(dx2, ba, br, proj, wa, wb, wo, dep):
    t, d = dx2.shape
    tm = _row_tile(t)

    def body(dx_ref, ba_ref, br_ref, p_ref, wa_ref, wb_ref, wo_ref, dep_ref,
             da_ref, drn_ref, dga_ref, dgb_ref, mix_ref, dba_ref, dbr_ref, dxb_ref):
        dxb = dx_ref[...].astype(bf16)
        dxb_ref[...] = dxb
        dmix = _dot_nt(dxb, wo_ref[...])
        ba = ba_ref[...].astype(f32)
        br = br_ref[...].astype(f32)
        sa = _sigmoid(p_ref[0, :, 0:d].astype(f32))
        sb = _sigmoid(p_ref[0, :, d:2 * d].astype(f32))
        mix_ref[...] = (sa * ba + sb * br).astype(bf16)
        dba = (dmix * sa).astype(bf16)
        dbr = (dmix * sb).astype(bf16)
        dba_ref[...] = dba
        dbr_ref[...] = dbr
        dga_ref[...] = (dmix * ba * sa * (1.0 - sa)).astype(bf16)
        dgb_ref[...] = (dmix * br * sb * (1.0 - sb)).astype(bf16)
        da_ref[...] = _dot_nt(dba, wa_ref[...]).astype(bf16)
        drn_ref[...] = _dot_nt(dbr, wb_ref[...]).astype(bf16)

    row = pl.BlockSpec((tm, d), lambda i: (i, 0))
    wsp = pl.BlockSpec((d, d), lambda i: (0, 0))
    return pl.pallas_call(
        body, name="mix_bwd_act", grid=(t // tm,),
        in_specs=[row, row, row, pl.BlockSpec((1, tm, 2 * d), lambda i: (3, i, 0)), wsp, wsp, wsp, _ANY],
        out_specs=[row] * 8,
        out_shape=[jax.ShapeDtypeStruct((t, d), bf16)] * 8,
        compiler_params=_cparams(),
    )(dx2, ba, br, proj, wa, wb, wo, dep)


def inproj_bwd_act(segs, win, x1, ng, dx2):
    t, d = x1.shape
    s4 = win.shape[0]
    tm = _row_tile(t)
    nseg = len(segs)

    def body(*refs):
        seg_refs = refs[:nseg]
        w_ref, x_ref, ng_ref, dx2_ref, dx1_ref, db_ref, dng_ref = refs[nseg:]
        _zero_at_first_step(db_ref, dng_ref)
        dh = None
        for e, sr in enumerate(seg_refs):
            sb = sr[...]
            part = _dot_nt(sb, w_ref[e // 2, :, (e % 2) * d:(e % 2 + 1) * d])
            dh = part if dh is None else dh + part
            db_ref[e] += jnp.sum(sb.astype(f32), axis=0, keepdims=True)
        _, xh, r = _rms(x_ref[...], ng_ref[...])
        dx1_ref[...] = dx2_ref[...] + _rms_bwd(dh, xh, r, ng_ref[...])
        dng_ref[...] += jnp.sum(dh * xh, axis=0, keepdims=True)

    row = pl.BlockSpec((tm, d), lambda i: (i, 0))
    vec = pl.BlockSpec((1, d), lambda i: (0, 0))
    return pl.pallas_call(
        body, name="inproj_bwd_act", grid=(t // tm,),
        in_specs=[row] * nseg + [pl.BlockSpec((s4, d, 2 * d), lambda i: (0, 0, 0), pipeline_mode=pl.Buffered(1)),
                                 row, vec, row],
        out_specs=[row, pl.BlockSpec((nseg, 1, d), lambda i: (0, 0, 0)), vec],
        out_shape=[jax.ShapeDtypeStruct((t, d), f32), jax.ShapeDtypeStruct((nseg, 1, d), f32),
                   jax.ShapeDtypeStruct((1, d), f32)],
        compiler_params=_cparams(VMEM_LIMIT_WIDE),
    )(*segs, win, x1, ng, dx2)


def _place():
    return lax.axis_index("x"), lax.axis_index("y"), lax.axis_index("c")


def _other_chips(x, y):
    return [(1 - x, y), (x, 1 - y), (1 - x, 1 - y)]


_ANY = pl.BlockSpec(memory_space=pl.ANY)


_HBM = pl.BlockSpec(memory_space=pltpu.HBM)
_SEM = pl.BlockSpec(memory_space=pltpu.SEMAPHORE)
_EFFECT = pltpu.SideEffectType.DATAFLOW_SIDE_EFFECTING


def _hbm(a):
    return pltpu.with_memory_space_constraint(a, pltpu.HBM)


def _half_rows(ref, c):
    half = ref.shape[1] // 2
    return pl.ds(pl.multiple_of(c * half, 16), half)


def _chip_copy(src, dst, send_sem, recv_sem, chip, c):
    return pltpu.make_async_remote_copy(src_ref=src, dst_ref=dst, send_sem=send_sem, recv_sem=recv_sem,
                                        device_id=(chip[0], chip[1], c), device_id_type=MESH)


def gather_start(bufs, groups, name):
    nb, ng = len(bufs), len(groups)

    def body(*refs):
        ins = refs[:nb]
        sems = refs[nb:nb + 2 * ng]
        token = refs[-1]
        x, y, c = _place()
        k = 2 * x + y
        for gi, grp in enumerate(groups):
            for wi, w in enumerate(grp):
                mine = ins[w].at[k, _half_rows(ins[w], c)]
                for j, chip in enumerate(_other_chips(x, y)):
                    _chip_copy(mine, mine, sems[2 * gi].at[3 * wi + j], sems[2 * gi + 1].at[3 * wi + j], chip, c).start()
        token[...] = jnp.zeros_like(token)

    sem_shapes = []
    for grp in groups:
        sem_shapes += [pltpu.SemaphoreType.DMA((3 * len(grp),)), pltpu.SemaphoreType.DMA((3 * len(grp),))]
    outs = pl.pallas_call(
        body, name=name,
        out_shape=sem_shapes + [pltpu.HBM(b.shape, b.dtype) for b in bufs] + [jax.ShapeDtypeStruct((8, 128), f32)],
        in_specs=[_HBM] * nb,
        out_specs=[_SEM] * (2 * ng) + [_HBM] * nb + [pl.BlockSpec(memory_space=pltpu.VMEM)],
        input_output_aliases={w: 2 * ng + w for w in range(nb)},
        compiler_params=pltpu.CompilerParams(has_side_effects=_EFFECT),
    )(*[_hbm(b) for b in bufs])
    sems = [(outs[2 * gi], outs[2 * gi + 1]) for gi in range(ng)]
    return sems, list(outs[2 * ng:2 * ng + nb]), outs[-1]


def gather_wait(bufs, sems, after, name):
    n = len(bufs)

    def body(*refs):
        ins = refs[:n]
        send_sems, recv_sems = refs[n], refs[n + 1]
        x, y, c = _place()
        k = 2 * x + y
        for wi in range(n):
            half = _half_rows(ins[wi], c)
            for j, chip in enumerate(_other_chips(x, y)):
                cp = _chip_copy(ins[wi].at[k, half], ins[wi].at[2 * chip[0] + chip[1], half], send_sems.at[3 * wi + j],
                                recv_sems.at[3 * wi + j], chip, c)
                cp.wait_send()
                cp.wait_recv()

    outs = pl.pallas_call(
        body, name=name,
        out_shape=[pltpu.HBM(b.shape, b.dtype) for b in bufs],
        in_specs=[_HBM] * n + [_SEM, _SEM, _ANY],
        out_specs=[_HBM] * n,
        input_output_aliases={i: i for i in range(n)},
        compiler_params=pltpu.CompilerParams(has_side_effects=_EFFECT),
    )(*bufs, sems[0], sems[1], after)
    return list(outs)


def gather_forward(bufs, name):
    n = len(bufs)

    def body(*refs):
        ins = refs[n:2 * n]
        send_sems, recv_sems = refs[2 * n], refs[2 * n + 1]
        x, y, c = _place()
        copies = []
        for wi in range(n):
            for j, chip in enumerate(_other_chips(x, y)):
                kp = 2 * chip[0] + chip[1]
                got = ins[wi].at[kp, _half_rows(ins[wi], c)]
                cp = pltpu.make_async_remote_copy(
                    src_ref=got, dst_ref=got, send_sem=send_sems.at[3 * wi + j], recv_sem=recv_sems.at[3 * wi + j],
                    device_id=(x, y, 1 - c), device_id_type=MESH)
                cp.start()
                copies.append((cp, wi, kp, j))
        for cp, wi, kp, j in copies:
            cp.wait_send()
            theirs = ins[wi].at[kp, _half_rows(ins[wi], 1 - c)]
            pltpu.make_async_remote_copy(
                src_ref=theirs, dst_ref=theirs, send_sem=send_sems.at[3 * wi + j], recv_sem=recv_sems.at[3 * wi + j],
                device_id=(x, y, 1 - c), device_id_type=MESH).wait_recv()

    outs = pl.pallas_call(
        body, name=name,
        out_shape=[jax.ShapeDtypeStruct(b.shape, b.dtype) for b in bufs],
        in_specs=[_ANY] * n, out_specs=[_ANY] * n,
        input_output_aliases={i: i for i in range(n)},
        scratch_shapes=[pltpu.SemaphoreType.DMA((3 * n,)), pltpu.SemaphoreType.DMA((3 * n,))],
    )(*bufs)
    return list(outs)


def forward_start(bufs, name):
    n = len(bufs)

    def body(*refs):
        x, y, c = _place()
        for wi in range(n):
            for j, chip in enumerate(_other_chips(x, y)):
                got = refs[wi].at[2 * chip[0] + chip[1], _half_rows(refs[wi], c)]
                _sibling_copy(got, got, refs[n].at[3 * wi + j], refs[n + 1].at[3 * wi + j]).start()
        refs[-1][...] = jnp.zeros_like(refs[-1])

    return _split_start(body, name, 3 * n, list(bufs))


def forward_wait(bufs, sems, after, name):
    n = len(bufs)

    def body(*refs):
        x, y, c = _place()
        for wi in range(n):
            for j, chip in enumerate(_other_chips(x, y)):
                kp = 2 * chip[0] + chip[1]
                got = refs[wi].at[kp, _half_rows(refs[wi], c)]
                theirs = refs[wi].at[kp, _half_rows(refs[wi], 1 - c)]
                _sibling_copy(got, got, refs[n].at[3 * wi + j], refs[n + 1].at[3 * wi + j]).wait_send()
                _sibling_copy(theirs, theirs, refs[n].at[3 * wi + j], refs[n + 1].at[3 * wi + j]).wait_recv()

    return _split_wait(body, name, list(bufs), sems, after)


def exchange_start(grads, name):
    n = len(grads)
    lands = [lax.empty((3,) + g.shape[1:], g.dtype) for g in grads]

    def body(*refs):
        ins = refs[:n]
        land = refs[n:2 * n]
        send_sems, recv_sems = refs[2 * n], refs[2 * n + 1]
        token = refs[-1]
        x, y, c = _place()
        for wi in range(n):
            for j, chip in enumerate(_other_chips(x, y)):
                _chip_copy(ins[wi].at[2 * chip[0] + chip[1]], land[wi].at[j], send_sems.at[3 * wi + j],
                           recv_sems.at[3 * wi + j], chip, c).start()
        token[...] = jnp.zeros_like(token)

    outs = pl.pallas_call(
        body, name=name,
        out_shape=[pltpu.SemaphoreType.DMA((3 * n,)), pltpu.SemaphoreType.DMA((3 * n,))]
        + [pltpu.HBM(g.shape, g.dtype) for g in grads] + [pltpu.HBM(l.shape, l.dtype) for l in lands]
        + [jax.ShapeDtypeStruct((8, 128), f32)],
        in_specs=[_HBM] * (2 * n),
        out_specs=[_SEM, _SEM] + [_HBM] * (2 * n) + [pl.BlockSpec(memory_space=pltpu.VMEM)],
        input_output_aliases={i: 2 + i for i in range(2 * n)},
        compiler_params=pltpu.CompilerParams(has_side_effects=_EFFECT),
    )(*[_hbm(g) for g in grads], *[_hbm(l) for l in lands])
    return (outs[0], outs[1]), list(outs[2:2 + n]), list(outs[2 + n:2 + 2 * n]), outs[-1]


def exchange_wait(grads, lands, sems, after, name):
    n = len(grads)

    def body(*refs):
        ins = refs[:n]
        land = refs[n:2 * n]
        send_sems, recv_sems = refs[2 * n], refs[2 * n + 1]
        x, y, c = _place()
        for wi in range(n):
            for j, chip in enumerate(_other_chips(x, y)):
                cp = _chip_copy(ins[wi].at[2 * chip[0] + chip[1]], land[wi].at[j], send_sems.at[3 * wi + j],
                                recv_sems.at[3 * wi + j], chip, c)
                cp.wait_send()
                cp.wait_recv()

    outs = pl.pallas_call(
        body, name=name,
        out_shape=[pltpu.HBM(g.shape, g.dtype) for g in grads] + [pltpu.HBM(l.shape, l.dtype) for l in lands],
        in_specs=[_HBM] * (2 * n) + [_SEM, _SEM, _ANY],
        out_specs=[_HBM] * (2 * n),
        input_output_aliases={i: i for i in range(2 * n)},
        compiler_params=pltpu.CompilerParams(has_side_effects=_EFFECT),
    )(*grads, *lands, sems[0], sems[1], after)
    return list(outs[:n]), list(outs[n:])


def _split_start(body, name, n_sems, operands):
    n = len(operands)
    outs = pl.pallas_call(
        body, name=name,
        out_shape=[pltpu.SemaphoreType.DMA((n_sems,)), pltpu.SemaphoreType.DMA((n_sems,))]
        + [pltpu.HBM(o.shape, o.dtype) for o in operands] + [jax.ShapeDtypeStruct((8, 128), f32)],
        in_specs=[_HBM] * n,
        out_specs=[_SEM, _SEM] + [_HBM] * n + [pl.BlockSpec(memory_space=pltpu.VMEM)],
        input_output_aliases={i: 2 + i for i in range(n)},
        compiler_params=pltpu.CompilerParams(has_side_effects=_EFFECT),
    )(*[_hbm(o) for o in operands])
    return (outs[0], outs[1]), list(outs[2:2 + n]), outs[-1]


def _split_wait(body, name, operands, sems, after):
    n = len(operands)
    outs = pl.pallas_call(
        body, name=name,
        out_shape=[pltpu.HBM(o.shape, o.dtype) for o in operands],
        in_specs=[_HBM] * n + [_SEM, _SEM, _ANY],
        out_specs=[_HBM] * n,
        input_output_aliases={i: i for i in range(n)},
        compiler_params=pltpu.CompilerParams(has_side_effects=_EFFECT),
    )(*operands, sems[0], sems[1], after)
    return list(outs)


def _sibling_copy(src, dst, send_sem, recv_sem):
    x, y, c = _place()
    return pltpu.make_async_remote_copy(src_ref=src, dst_ref=dst, send_sem=send_sem, recv_sem=recv_sem,
                                        device_id=(x, y, 1 - c), device_id_type=MESH)


def swap_start(parts, name):
    n = len(parts)

    def body(*refs):
        for w in range(n):
            _sibling_copy(refs[w], refs[n + w], refs[2 * n].at[w], refs[2 * n + 1].at[w]).start()
        refs[-1][...] = jnp.zeros_like(refs[-1])

    sems, ops, token = _split_start(body, name, n, list(parts) + [lax.empty(p.shape, p.dtype) for p in parts])
    return sems, ops[:n], ops[n:], token


def swap_wait(parts, lands, sems, after, name):
    n = len(parts)

    def body(*refs):
        for w in range(n):
            cp = _sibling_copy(refs[w], refs[n + w], refs[2 * n].at[w], refs[2 * n + 1].at[w])
            cp.wait_send()
            cp.wait_recv()

    outs = _split_wait(body, name, list(parts) + list(lands), sems, after)
    return outs[:n], outs[n:]


def _all_peers(x, y, c):
    return [(1 - x if m & 4 else x, 1 - y if m & 2 else y, 1 - c if m & 1 else c) for m in range(1, N_DEV)]


def small_start(block):
    land = jnp.broadcast_to(block[None], (N_DEV,) + block.shape)

    def body(b_ref, land_ref, send_sems, recv_sems, b_thru, land_thru, token):
        x, y, c = _place()
        me = 4 * x + 2 * y + c
        for m, peer in enumerate(_all_peers(x, y, c)):
            pltpu.make_async_remote_copy(src_ref=b_ref, dst_ref=land_ref.at[me], send_sem=send_sems.at[m],
                                         recv_sem=recv_sems.at[m], device_id=peer, device_id_type=MESH).start()
        token[...] = jnp.zeros_like(token)

    sems, ops, token = _split_start(body, "small_start", N_DEV - 1, [block, land])
    return sems, ops[0], ops[1], token


def small_wait(block, land, sems, after):
    def body(b_ref, land_ref, send_sems, recv_sems, after_ref, b_thru, land_thru):
        x, y, c = _place()
        for m, (px, py, pc) in enumerate(_all_peers(x, y, c)):
            cp = pltpu.make_async_remote_copy(src_ref=b_ref, dst_ref=land_ref.at[4 * px + 2 * py + pc],
                                              send_sem=send_sems.at[m], recv_sem=recv_sems.at[m],
                                              device_id=(px, py, pc), device_id_type=MESH)
            cp.wait_send()
            cp.wait_recv()

    return _split_wait(body, "small_wait", [block, land], sems, after)[1]


def _adamw(w, g, m, v):
    m = ADAM_B1 * m + (1.0 - ADAM_B1) * g
    v = ADAM_B2 * v + (1.0 - ADAM_B2) * (g * g)
    m_hat = m / (1.0 - ADAM_B1 ** ADAM_STEP)
    v_hat = v / (1.0 - ADAM_B2 ** ADAM_STEP)
    delta = -ADAM_LR * (m_hat / (jnp.sqrt(v_hat) + ADAM_EPS) + ADAM_WD * w)
    return delta, m, v


EW_BLOCK_BYTES = 2 * 1024 * 1024


def _ew_tile(rows, cols):
    for cand in (512, 352, 256, 176, 128, 64, 32, 16, 8):
        if rows % cand == 0 and cand * cols * 4 <= EW_BLOCK_BYTES:
            return cand
    return rows


def sum_partials(chip, own, land, name):
    _, r, c = own.shape
    tr = _ew_tile(r, c)

    def body(k_ref, own_ref, p_ref, o_ref):
        o_ref[...] = ((own_ref[0].astype(f32) + p_ref[0].astype(f32)) + p_ref[1].astype(f32)) + p_ref[2].astype(f32)

    return pl.pallas_call(
        body, name=name,
        grid_spec=pltpu.PrefetchScalarGridSpec(
            num_scalar_prefetch=1, grid=(r // tr,),
            in_specs=[pl.BlockSpec((1, tr, c), lambda i, k: (k[0], i, 0)), pl.BlockSpec((3, tr, c), lambda i, k: (0, i, 0))],
            out_specs=pl.BlockSpec((tr, c), lambda i, k: (i, 0))),
        out_shape=jax.ShapeDtypeStruct((r, c), f32),
        compiler_params=_cparams(),
    )(chip, own, land)


def adamw_shard(p_mine, p_sibling, w, m, v, name):
    r, c = w.shape
    tr = _ew_tile(r, c)

    def body(a_ref, b_ref, w_ref, m_ref, v_ref, g_ref, d_ref, mo_ref, vo_ref):
        g = a_ref[...] + b_ref[...]
        delta, mn, vn = _adamw(w_ref[...], g, m_ref[...], v_ref[...])
        g_ref[...] = g
        d_ref[...] = delta
        mo_ref[...] = mn
        vo_ref[...] = vn

    blk = pl.BlockSpec((tr, c), lambda i: (i, 0))
    return pl.pallas_call(
        body, name=name, grid=(r // tr,),
        in_specs=[blk] * 5, out_specs=[blk] * 4,
        out_shape=[jax.ShapeDtypeStruct((r, c), f32)] * 4,
        compiler_params=_cparams(),
    )(p_mine, p_sibling, w, m, v)


def adamw_small(g8, w, m, v):
    _, r, lanes = g8.shape

    def body(g_ref, w_ref, m_ref, v_ref, go_ref, d_ref, mo_ref, vo_ref):
        g = g_ref[0]
        for i in range(1, N_DEV):
            g = g + g_ref[i]
        delta, mn, vn = _adamw(w_ref[...], g, m_ref[...], v_ref[...])
        go_ref[...] = g
        d_ref[...] = delta
        mo_ref[...] = mn
        vo_ref[...] = vn

    return pl.pallas_call(
        body, name="adamw_small",
        out_shape=[jax.ShapeDtypeStruct((r, lanes), f32)] * 4,
        compiler_params=_cparams(),
    )(g8, w, m, v)


def _size(shape):
    n = 1
    for e in shape:
        n *= e
    return n


def _pack_rows(shapes):
    rows = [-(-_size(s) // 1024) * 8 for s in shapes]
    return rows, sum(rows)


def _pack(arrs, shapes):
    rows, _ = _pack_rows(shapes)
    parts = [jnp.pad(a.reshape(-1).astype(f32), (0, r * 128 - _size(s))).reshape(r, 128)
             for a, s, r in zip(arrs, shapes, rows)]
    return jnp.concatenate(parts, axis=0)


def _unpack(block, shapes):
    rows, _ = _pack_rows(shapes)
    out, off = [], 0
    for s, r in zip(shapes, rows):
        out.append(block[off:off + r].reshape(-1)[:_size(s)].reshape(s))
        off += r
    return out


TRANSPOSED = ("ffn1_w_gate", "ffn1_w_up", "ffn2_w_gate", "ffn2_w_up")


def _shard2d(a, n):
    return a[0].T if n in TRANSPOSED else a[0]


def _unshard(a, n):
    return (a.T if n in TRANSPOSED else a)[None]


BIG = ("ffn1_w_gate", "ffn1_w_up", "ffn1_w_down", "w_in", "w_branch_a", "w_branch_b", "w_out",
       "ffn2_w_gate", "ffn2_w_up", "ffn2_w_down")
SMALL = ("ffn1_norm", "mix_norm", "b_in", "sgu_norm_g", "sgu_norm_b", "sgu_w_s", "sgu_b_s", "ret_decay_logit",
         "ffn2_norm", "final_norm")
WEIGHTS = ("ffn1_norm", "ffn1_w_gate", "ffn1_w_up", "ffn1_w_down", "mix_norm", "w_in", "b_in", "sgu_norm_g",
           "sgu_norm_b", "sgu_w_s", "sgu_b_s", "ret_decay_logit", "w_branch_a", "w_branch_b", "w_out", "ffn2_norm",
           "ffn2_w_gate", "ffn2_w_up", "ffn2_w_down", "final_norm")


def kernel(x, ffn1_norm, ffn1_w_gate, ffn1_w_up, ffn1_w_down, mix_norm, w_in, b_in, sgu_norm_g, sgu_norm_b, sgu_w_s, sgu_b_s, ret_decay_logit, w_branch_a, w_branch_b, w_out, ffn2_norm, ffn2_w_gate, ffn2_w_up, ffn2_w_down, final_norm, loss_target, m_ffn1_norm, m_ffn1_w_gate, m_ffn1_w_up, m_ffn1_w_down, m_mix_norm, m_w_in, m_b_in, m_sgu_norm_g, m_sgu_norm_b, m_sgu_w_s, m_sgu_b_s, m_ret_decay_logit, m_w_branch_a, m_w_branch_b, m_w_out, m_ffn2_norm, m_ffn2_w_gate, m_ffn2_w_up, m_ffn2_w_down, m_final_norm, v_ffn1_norm, v_ffn1_w_gate, v_ffn1_w_up, v_ffn1_w_down, v_mix_norm, v_w_in, v_b_in, v_sgu_norm_g, v_sgu_norm_b, v_sgu_w_s, v_sgu_b_s, v_ret_decay_logit, v_w_branch_a, v_w_branch_b, v_w_out, v_ffn2_norm, v_ffn2_w_gate, v_ffn2_w_up, v_ffn2_w_down, v_final_norm):
    p = dict(ffn1_norm=ffn1_norm, ffn1_w_gate=ffn1_w_gate, ffn1_w_up=ffn1_w_up, ffn1_w_down=ffn1_w_down,
             mix_norm=mix_norm, w_in=w_in, b_in=b_in, sgu_norm_g=sgu_norm_g, sgu_norm_b=sgu_norm_b, sgu_w_s=sgu_w_s,
             sgu_b_s=sgu_b_s, ret_decay_logit=ret_decay_logit, w_branch_a=w_branch_a, w_branch_b=w_branch_b,
             w_out=w_out, ffn2_norm=ffn2_norm, ffn2_w_gate=ffn2_w_gate, ffn2_w_up=ffn2_w_up, ffn2_w_down=ffn2_w_down,
             final_norm=final_norm)
    mom = dict(ffn1_norm=m_ffn1_norm, ffn1_w_gate=m_ffn1_w_gate, ffn1_w_up=m_ffn1_w_up, ffn1_w_down=m_ffn1_w_down,
               mix_norm=m_mix_norm, w_in=m_w_in, b_in=m_b_in, sgu_norm_g=m_sgu_norm_g, sgu_norm_b=m_sgu_norm_b,
               sgu_w_s=m_sgu_w_s, sgu_b_s=m_sgu_b_s, ret_decay_logit=m_ret_decay_logit, w_branch_a=m_w_branch_a,
               w_branch_b=m_w_branch_b, w_out=m_w_out, ffn2_norm=m_ffn2_norm, ffn2_w_gate=m_ffn2_w_gate,
               ffn2_w_up=m_ffn2_w_up, ffn2_w_down=m_ffn2_w_down, final_norm=m_final_norm)
    var = dict(ffn1_norm=v_ffn1_norm, ffn1_w_gate=v_ffn1_w_gate, ffn1_w_up=v_ffn1_w_up, ffn1_w_down=v_ffn1_w_down,
               mix_norm=v_mix_norm, w_in=v_w_in, b_in=v_b_in, sgu_norm_g=v_sgu_norm_g, sgu_norm_b=v_sgu_norm_b,
               sgu_w_s=v_sgu_w_s, sgu_b_s=v_sgu_b_s, ret_decay_logit=v_ret_decay_logit, w_branch_a=v_w_branch_a,
               w_branch_b=v_w_branch_b, w_out=v_w_out, ffn2_norm=v_ffn2_norm, ffn2_w_gate=v_ffn2_w_gate,
               ffn2_w_up=v_ffn2_w_up, ffn2_w_down=v_ffn2_w_down, final_norm=v_final_norm)

    xs = x[0]
    tgt = loss_target[0]
    t, d = xs.shape
    dk = d // RET_HEADS

    shards2d = {n: _shard2d(p[n], n) for n in BIG}
    chip = (2 * lax.axis_index("x") + lax.axis_index("y")).astype(jnp.int32).reshape(1)
    groups = {"ffn1": ("ffn1_w_gate", "ffn1_w_up", "ffn1_w_down"), "in": ("w_in",),
              "mix": ("w_branch_a", "w_branch_b", "w_out"), "ffn2": ("ffn2_w_gate", "ffn2_w_up", "ffn2_w_down")}
    def own_slot(n, zero):
        sh = shards2d[n].astype(bf16) + zero
        return lax.dynamic_update_index_in_dim(lax.empty((N_CHIPS,) + sh.shape, bf16), sh, chip[0], 0)

    sems, bufs, tok = gather_start([own_slot(n, jnp.zeros((), bf16)) for n in groups["ffn1"]], [[0, 1, 2]],
                                   "gather_start_ffn1")
    gsem = {"ffn1": sems[0]}
    pending = dict(zip(groups["ffn1"], bufs))
    rest = [n for g in ("in", "mix", "ffn2") for n in groups[g]]
    sems, bufs, tok_rest = gather_start([own_slot(n, tok[0, 0].astype(bf16)) for n in rest],
                                 [[rest.index(n) for n in groups[g]] for g in ("in", "mix", "ffn2")], "gather_start_rest")
    gsem.update(zip(("in", "mix", "ffn2"), sems))
    pending.update(zip(rest, bufs))

    def arrive(gs, after):
        got = []
        for g in gs:
            got += gather_wait([pending[n] for n in groups[g]], gsem[g], after, "gather_wait_" + g)
        return gather_forward(got, "gather_forward_" + gs[0])

    bin4 = b_in.reshape(N_CHIPS, 1, 2 * d)
    ws_b = sgu_w_s[0].astype(bf16)
    bs_c = sgu_b_s[0][:, :, None]
    cols, mats, cdec, cos, sin = retention_constants(ret_decay_logit[0], t, dk, tok_rest[0, 0])

    wg1, wu1, wd1 = [_pair_shards(w) for w in arrive(["ffn1"], cos)]
    x1, g1, u1 = ffn_fwd(xs, ffn1_norm, wg1, wu1, wd1, "ffn1_fwd")
    win, = arrive(["in"], x1)
    proj, hb2, a = inproj_fwd(x1, mix_norm, win, bin4, cos, sin, sgu_norm_g, sgu_norm_b, ws_b, bs_c)
    late = []
    for g in ("mix", "ffn2"):
        late += gather_wait([pending[n] for n in groups[g]], gsem[g], proj, "gather_wait_" + g)
    fsems, late, ftok = forward_start(late, "forward_start_mix")
    r, rn = ret_fwd(proj, cols, mats, cdec, ftok)
    wa, wb, wo, wg2, wu2, wd2 = forward_wait(late, fsems, rn, "forward_wait_mix")
    wa, wb, wo = [w.reshape(d, d) for w in (wa, wb, wo)]
    wg2, wu2, wd2 = [_pair_shards(w) for w in (wg2, wu2, wd2)]
    x2, ba, br = mix_fwd(a, rn, proj, wa, wb, wo, x1)
    loss_blk, dx3, d_final, g2, u2 = ffn_fwd_loss(x2, ffn2_norm, wg2, wu2, wd2, final_norm.reshape(1, d), tgt, "ffn2_fwd")

    sent, swaps = {}, {}
    out_g, out_d, out_m, out_v = {}, {}, {}, {}

    def reduce_plane(g, after):
        gsems, own, lands, _ = sent[g]
        own, lands = exchange_wait(own, lands, gsems, after, "exchange_wait_" + g)
        plane = [sum_partials(chip, o, l, "sum_" + n) for n, o, l in zip(groups[g], own, lands)]
        swaps[g] = swap_start(plane, "swap_start_" + g)
        return swaps[g][3]

    def update(g, after):
        ssems, plane, lands, _ = swaps[g]
        plane, other = swap_wait(plane, lands, ssems, after, "swap_wait_" + g)
        for n, mine, sib in zip(groups[g], plane, other):
            res = adamw_shard(mine, sib, shards2d[n], _shard2d(mom[n], n), _shard2d(var[n], n), "adamw_" + n)
            out_g[n], out_d[n], out_m[n], out_v[n] = [_unshard(o, n) for o in res]
        return res[0]

    dx2, dg2, du2, act2, hb3, dyb2, d_ffn2n = ffn_bwd_act(dx3, x2, ffn2_norm, g2, u2, wg2, wu2, wd2, "ffn2_bwd_act", tok)
    sent["ffn2"] = exchange_start(ffn_weight_grads(hb3, dyb2, dg2, du2, act2, "ffn2_grad", tok), "exchange_start_ffn2")
    da, drn, dga, dgb, mixb, dba, dbr, dx2b = mix_bwd_act(dx2, ba, br, proj, wa, wb, wo, sent["ffn2"][3])
    tg = min(t, 2048)
    row = pl.BlockSpec((tg, d), lambda s, i: (i, 0))

    def square_grad(xa, ya, name):
        return tn_matmul(xa, [ya], row, [row], 1, d, [d], t, tg, name, tok).reshape(N_CHIPS, d // N_CHIPS, d)

    g_mix = [square_grad(a, dba, "grad_w_branch_a"), square_grad(rn, dbr, "grad_w_branch_b"),
             square_grad(mixb, dx2b, "grad_w_out")]
    dua, dva, d_ws, d_bs, d_sng, d_snb = sgu_bwd(da, proj, sgu_norm_g, sgu_norm_b, ws_b, bs_c, sent["ffn2"][3])
    dq, dkr, dv, dgr, dlg = ret_bwd(drn, r, proj, cols, mats, cdec, cos, sin)
    segs = [dua, dva, dq, dkr, dv, dgr, dga, dgb]
    dx1, d_bin, d_mixn = inproj_bwd_act(segs, win, x1, mix_norm, dx2)
    g_in = None
    for s in range(N_CHIPS):
        g_in = tn_matmul(hb2, [segs[2 * s], segs[2 * s + 1]], row, [row, row], 1, d, [d, d], t, tg, "grad_w_in_%d" % s,
                         tok, (g_in, s, N_CHIPS))
    groups["mix_in"] = groups["mix"] + groups["in"]
    sent["mix_in"] = exchange_start(g_mix + [g_in], "exchange_start_mix_in")
    grad_x, dg1, du1, act1, hb1, dyb1, d_ffn1n = ffn_bwd_act(dx1, xs, ffn1_norm, g1, u1, wg1, wu1, wd1, "ffn1_bwd_act",
                                                              sent["mix_in"][3])
    dlogit = dlg[:, 0:2, 0].T * jax.nn.sigmoid(-ret_decay_logit[0].astype(f32))
    small_g = dict(ffn1_norm=d_ffn1n, mix_norm=d_mixn, b_in=d_bin, sgu_norm_g=d_sng, sgu_norm_b=d_snb, sgu_w_s=d_ws,
                   sgu_b_s=d_bs, ret_decay_logit=dlogit, ffn2_norm=d_ffn2n, final_norm=d_final)
    shapes = [p[n].shape for n in SMALL] + [(1,)]
    small_sems, small_blk, small_land, small_tok = small_start(
        _pack([small_g[n] for n in SMALL] + [loss_blk[0, 0:1]], shapes))

    def send_one(which, grad):
        n = "ffn1_" + which
        groups[n] = (n,)
        sent[n] = exchange_start([grad], "exchange_start_" + n)
        return sent[n][3]

    ffn_weight_grads(hb1, dyb1, dg1, du1, act1, "ffn1_grad", small_tok, send_one)

    after = reduce_plane("ffn2", sent["ffn1_w_down"][3])
    after = reduce_plane("mix_in", after)
    after = update("ffn2", after)
    g8 = small_wait(small_blk, small_land, small_sems, after)
    no_state = [jnp.zeros((1,), f32)]
    sg, sd, sm, sv = adamw_small(g8, _pack([p[n] for n in SMALL] + no_state, shapes),
                                 _pack([mom[n] for n in SMALL] + no_state, shapes),
                                 _pack([var[n] for n in SMALL] + no_state, shapes))
    for res, blockv in ((out_g, sg), (out_d, sd), (out_m, sm), (out_v, sv)):
        for n, val in zip(SMALL, _unpack(blockv, shapes)):
            res[n] = val
    loss = _unpack(sg, shapes)[-1][0]
    after = update("mix_in", sg)
    after = reduce_plane("ffn1_w_gate", after)
    after = reduce_plane("ffn1_w_up", after)
    after = update("ffn1_w_gate", after)
    after = reduce_plane("ffn1_w_down", after)
    after = update("ffn1_w_up", after)
    update("ffn1_w_down", after)

    return (loss, grad_x[None], *[out_g[n] for n in WEIGHTS], *[out_d[n] for n in WEIGHTS],
            *[out_m[n] for n in WEIGHTS], *[out_v[n] for n in WEIGHTS])
```

```python
import jax
import jax.numpy as jnp
from jax import lax
from jax.experimental import pallas as pl
from jax.experimental.pallas import tpu as pltpu

f32 = jnp.float32
bf16 = jnp.bfloat16

SGU_CHUNK = 128
CHUNK = 256
RET_HEADS = 4
SGU_GROUPS = 4
ROPE_BASE = 10000.0
NORM_EPS = 1e-6
ADAM_LR = 0.001
ADAM_B1 = 0.9
ADAM_B2 = 0.999
ADAM_EPS = 1e-08
ADAM_WD = 0.01
ADAM_STEP = 10
N_CHIPS = 4
N_DEV = 8
MESH = pl.DeviceIdType.MESH
VMEM_LIMIT = 52 * 1024 * 1024
VMEM_LIMIT_WIDE = 62 * 1024 * 1024

_NT = (((1,), (1,)), ((), ()))
_TN = (((0,), (0,)), ((), ()))


def _cparams(limit=None):
    return pltpu.CompilerParams(vmem_limit_bytes=VMEM_LIMIT if limit is None else limit)


def _row_tile(t):
    return 512 if t >= 2048 else t // 2


def _dot(a, b):
    return jnp.dot(a, b, preferred_element_type=f32)


def _dot_nt(a, b):
    return lax.dot_general(a, b, _NT, preferred_element_type=f32)


def _dot_tn(a, b):
    return lax.dot_general(a, b, _TN, preferred_element_type=f32)


def _rms(x, g):
    r = lax.rsqrt(jnp.mean(x * x, axis=-1, keepdims=True) + NORM_EPS)
    xh = x * r
    return xh * g, xh, r


def _rms_bwd(dy, xh, r, g):
    dxh = dy * g
    return r * (dxh - xh * jnp.mean(dxh * xh, axis=-1, keepdims=True))


def _sigmoid(x):
    return jax.nn.sigmoid(x)


def _dsilu(g, sg):
    return sg * (1.0 + g * (1.0 - sg))


def _gelu(x):
    return 0.5 * x * (1.0 + lax.erf(x * 0.7071067811865476))


def _dgelu(x):
    return 0.5 * (1.0 + lax.erf(x * 0.7071067811865476)) + x * jnp.exp(-0.5 * x * x) * 0.3989422804014327


def _zero_at_first_step(*refs):
    @pl.when(pl.program_id(0) == 0)
    def _():
        for ref in refs:
            ref[...] = jnp.zeros_like(ref)


def _ffn_tile(t):
    return 256 if t >= 2048 else t // 2


def _ffn_fwd_rows(xx, ng_ref, wg_ref, wu_ref, wd_ref, g_ref, u_ref):
    y, _, _ = _rms(xx, ng_ref[...])
    h = y.astype(bf16)
    acc = None
    for s in range(wg_ref.shape[0]):
        g = _dot_nt(h, wg_ref[s])
        u = _dot_nt(h, wu_ref[s])
        g_ref[s] = g.astype(bf16)
        u_ref[s] = u.astype(bf16)
        part = _dot((g * _sigmoid(g) * u).astype(bf16), wd_ref[s])
        acc = part if acc is None else acc + part
    return xx + 0.5 * acc


def ffn_fwd(x, ng, wg, wu, wd, name):
    t, d = x.shape
    ns, fs, _ = wg.shape
    tm = _ffn_tile(t)

    def body(x_ref, ng_ref, wg_ref, wu_ref, wd_ref, xo_ref, g_ref, u_ref):
        xo_ref[...] = _ffn_fwd_rows(x_ref[...], ng_ref, wg_ref, wu_ref, wd_ref, g_ref, u_ref)

    row = pl.BlockSpec((tm, d), lambda i: (i, 0))
    shard = pl.BlockSpec((ns, tm, fs), lambda i: (0, i, 0))
    wspec = pl.BlockSpec((ns, fs, d), lambda i: (0, 0, 0), pipeline_mode=pl.Buffered(1))
    return pl.pallas_call(
        body, name=name, grid=(t // tm,),
        in_specs=[row, pl.BlockSpec((1, d), lambda i: (0, 0)), wspec, wspec, wspec],
        out_specs=[row, shard, shard],
        out_shape=[jax.ShapeDtypeStruct((t, d), f32), jax.ShapeDtypeStruct((ns, t, fs), bf16),
                   jax.ShapeDtypeStruct((ns, t, fs), bf16)],
        compiler_params=_cparams(),
    )(x, ng, wg, wu, wd)


def ffn_fwd_loss(x, ng, wg, wu, wd, fng, tgt, name):
    t, d = x.shape
    ns, fs, _ = wg.shape
    tm = _ffn_tile(t)

    def body(x_ref, ng_ref, wg_ref, wu_ref, wd_ref, fng_ref, t_ref, loss_ref, dx_ref, dfn_ref, g_ref, u_ref):
        _zero_at_first_step(loss_ref, dfn_ref)
        x3 = _ffn_fwd_rows(x_ref[...], ng_ref, wg_ref, wu_ref, wd_ref, g_ref, u_ref)
        y, xh, r = _rms(x3, fng_ref[...])
        diff = y - t_ref[...]
        part = 0.5 * jnp.sum(jnp.sum(diff * diff, axis=0, keepdims=True), axis=1, keepdims=True) / d
        loss_ref[...] += jnp.broadcast_to(part, (1, 128))
        dy = diff * (1.0 / d)
        dx_ref[...] = _rms_bwd(dy, xh, r, fng_ref[...])
        dfn_ref[...] += jnp.sum(dy * xh, axis=0, keepdims=True)

    row = pl.BlockSpec((tm, d), lambda i: (i, 0))
    vec = pl.BlockSpec((1, d), lambda i: (0, 0))
    shard = pl.BlockSpec((ns, tm, fs), lambda i: (0, i, 0))
    wspec = pl.BlockSpec((ns, fs, d), lambda i: (0, 0, 0), pipeline_mode=pl.Buffered(1))
    return pl.pallas_call(
        body, name=name, grid=(t // tm,),
        in_specs=[row, vec, wspec, wspec, wspec, vec, row],
        out_specs=[pl.BlockSpec((1, 128), lambda i: (0, 0)), row, vec, shard, shard],
        out_shape=[jax.ShapeDtypeStruct((1, 128), f32), jax.ShapeDtypeStruct((t, d), f32), jax.ShapeDtypeStruct((1, d), f32),
                   jax.ShapeDtypeStruct((ns, t, fs), bf16), jax.ShapeDtypeStruct((ns, t, fs), bf16)],
        compiler_params=_cparams(),
    )(x, ng, wg, wu, wd, fng, tgt)


def ffn_bwd_act(dxo, x, ng, g, u, wg, wu, wd, name, dep):
    t, d = x.shape
    ns, fs, _ = wg.shape
    tm = _ffn_tile(t)

    def body(dxo_ref, x_ref, ng_ref, g_ref, u_ref, wg_ref, wu_ref, wd_ref, dep_ref,
             dx_ref, dg_ref, du_ref, act_ref, hb_ref, dyb_ref, dng_ref):
        _zero_at_first_step(dng_ref)
        dxo = dxo_ref[...]
        dyb = (0.5 * dxo).astype(bf16)
        dyb_ref[...] = dyb
        dh = None
        for s in range(ns):
            dact = _dot_nt(dyb, wd_ref[s])
            gg = g_ref[s].astype(f32)
            uu = u_ref[s].astype(f32)
            sg = _sigmoid(gg)
            sil = gg * sg
            dgb = (dact * uu * _dsilu(gg, sg)).astype(bf16)
            dub = (dact * sil).astype(bf16)
            dg_ref[s] = dgb
            du_ref[s] = dub
            act_ref[s] = (sil * uu).astype(bf16)
            part = _dot(dgb, wg_ref[s]) + _dot(dub, wu_ref[s])
            dh = part if dh is None else dh + part
        y, xh, r = _rms(x_ref[...], ng_ref[...])
        hb_ref[...] = y.astype(bf16)
        dx_ref[...] = dxo + _rms_bwd(dh, xh, r, ng_ref[...])
        dng_ref[...] += jnp.sum(dh * xh, axis=0, keepdims=True)

    row = pl.BlockSpec((tm, d), lambda i: (i, 0))
    shard = pl.BlockSpec((ns, tm, fs), lambda i: (0, i, 0))
    wspec = pl.BlockSpec((ns, fs, d), lambda i: (0, 0, 0), pipeline_mode=pl.Buffered(1))
    vec = pl.BlockSpec((1, d), lambda i: (0, 0))
    return pl.pallas_call(
        body, name=name, grid=(t // tm,),
        in_specs=[row, row, vec, shard, shard, wspec, wspec, wspec, _ANY],
        out_specs=[row, shard, shard, shard, row, row, vec],
        out_shape=[jax.ShapeDtypeStruct((t, d), f32)] + [jax.ShapeDtypeStruct((ns, t, fs), bf16)] * 3
        + [jax.ShapeDtypeStruct((t, d), bf16)] * 2 + [jax.ShapeDtypeStruct((1, d), f32)],
        compiler_params=_cparams(VMEM_LIMIT_WIDE),
    )(dxo, x, ng, g, u, wg, wu, wd, dep)


def tn_matmul(xs, ys, x_spec, y_specs, n_shards, k1, k2s, t, tm, name, dep, into=None):
    k2 = sum(k2s)
    ny = len(ys)

    def body(*refs):
        x_ref = refs[0]
        y_refs = refs[1:1 + ny]
        steps = t // tm
        o_ref, acc = (refs[-1], None) if steps == 1 else (refs[-2], refs[-1])
        i = pl.program_id(1)
        xb = x_ref[0] if len(x_ref.shape) == 3 else x_ref[...]
        if steps > 1:
            @pl.when(i == 0)
            def _():
                acc[...] = jnp.zeros_like(acc)

        off = 0
        for y_ref, w in zip(y_refs, k2s):
            yb = y_ref[0] if len(y_ref.shape) == 3 else y_ref[...]
            part = _dot_tn(xb, yb)
            if steps == 1:
                o_ref[0, :, off:off + w] = part.astype(bf16)
            else:
                acc[:, off:off + w] += part
            off += w

        if steps > 1:
            @pl.when(i == steps - 1)
            def _():
                o_ref[0] = acc[...].astype(bf16)

    if into is None:
        slot0, total, extra, aliases = 0, n_shards, [], {}
    else:
        buf, slot0, total = into
        extra = [] if buf is None else [buf]
        aliases = {} if buf is None else {2 + ny: 0}
    return pl.pallas_call(
        body, name=name, grid=(n_shards, t // tm),
        in_specs=[x_spec] + list(y_specs) + [_ANY] * (1 + len(extra)),
        out_specs=pl.BlockSpec((1, k1, k2), lambda s, i: (slot0 + s, 0, 0)),
        out_shape=jax.ShapeDtypeStruct((total, k1, k2), bf16),
        scratch_shapes=[pltpu.VMEM((k1, k2), f32)] if t // tm > 1 else [],
        input_output_aliases=aliases,
        compiler_params=_cparams(),
    )(xs, *ys, dep, *extra)


def _pair_shards(w):
    s4, fs, d = w.shape
    return w.reshape(s4 // 2, 2 * fs, d)


def ffn_weight_grads(hb, dyb, dg, du, act, name, dep, each=None):
    t, d = hb.shape
    s2, _, fs2 = dg.shape
    tm = t
    row = pl.BlockSpec((tm, d), lambda s, i: (i, 0))
    shard = pl.BlockSpec((1, tm, fs2), lambda s, i: (s, i, 0))
    grads = []
    for xa, ya, which in ((dg, hb, "w_gate"), (du, hb, "w_up"), (act, dyb, "w_down")):
        g = tn_matmul(xa, [ya], shard, [row], s2, fs2, [d], t, tm, name + "_" + which, dep)
        g = g.reshape(2 * s2, fs2 // 2, d)
        if each is not None:
            dep = each(which, g)
        grads.append(g)
    return grads


def inproj_fwd(x1, ng, win, bin4, cos, sin, sng, snb, ws, bs):
    t, d = x1.shape
    s4, _, w2 = win.shape
    tm = _row_tile(t)
    dk = d // RET_HEADS
    scale = dk ** -0.5

    def body(x_ref, ng_ref, w_ref, b_ref, cos_ref, sin_ref, sng_ref, snb_ref, ws_ref, bs_ref, p_ref, hb_ref, a_ref):
        y, _, _ = _rms(x_ref[...], ng_ref[...])
        h = y.astype(bf16)
        hb_ref[...] = h
        uv = None
        for s in range(s4):
            p = _dot(h, w_ref[s]) + b_ref[s]
            if s == 0:
                uv = p.astype(bf16)
                p_ref[s] = uv
            elif s != 1:
                p_ref[s] = p.astype(bf16)
            else:
                cs, sn = cos_ref[...], sin_ref[...]
                for e in range(2 * RET_HEADS):
                    cols = slice(e * dk, (e + 1) * dk)
                    rot = _rot(p[:, cols], cs, sn)
                    p_ref[s, :, cols] = (rot if e < RET_HEADS else rot * scale).astype(bf16)
        _sgu_rows(uv[:, 0:d].astype(f32), uv[:, d:w2].astype(f32), sng_ref, snb_ref, ws_ref, bs_ref, a_ref)

    tab = pl.BlockSpec((tm, dk // 2), lambda i: (i, 0))
    row = pl.BlockSpec((tm, d), lambda i: (i, 0))
    vec = pl.BlockSpec((1, d), lambda i: (0, 0))
    return pl.pallas_call(
        body, name="inproj_fwd", grid=(t // tm,),
        in_specs=[row, vec, pl.BlockSpec((s4, d, w2), lambda i: (0, 0, 0), pipeline_mode=pl.Buffered(1)),
                  pl.BlockSpec((s4, 1, w2), lambda i: (0, 0, 0)), tab, tab, vec, vec,
                  pl.BlockSpec((SGU_GROUPS, SGU_CHUNK, SGU_CHUNK), lambda i: (0, 0, 0)),
                  pl.BlockSpec((SGU_GROUPS, SGU_CHUNK, 1), lambda i: (0, 0, 0))],
        out_specs=[pl.BlockSpec((s4, tm, w2), lambda i: (0, i, 0)), row, row],
        out_shape=[jax.ShapeDtypeStruct((s4, t, w2), bf16), jax.ShapeDtypeStruct((t, d), bf16),
                   jax.ShapeDtypeStruct((t, d), bf16)],
        compiler_params=_cparams(),
    )(x1, ng, win, bin4, cos, sin, sng, snb, ws, bs)


def _sgu_norm(va, ng, nb):
    gv = _gelu(va)
    mu = jnp.mean(gv, axis=-1, keepdims=True)
    xc = gv - mu
    rstd = lax.rsqrt(jnp.mean(xc * xc, axis=-1, keepdims=True) + NORM_EPS)
    xh = xc * rstd
    return xh, rstd, (xh * ng + nb).astype(bf16)


def _sgu_rows(ua, va, ng_ref, nb_ref, ws_ref, bs_ref, a_ref):
    tm, d = ua.shape
    gd = d // SGU_GROUPS
    gu = _gelu(ua)
    _, _, vn = _sgu_norm(va, ng_ref[...], nb_ref[...])
    for c in range(tm // SGU_CHUNK):
        rows = slice(c * SGU_CHUNK, (c + 1) * SGU_CHUNK)
        for g in range(SGU_GROUPS):
            cols = slice(g * gd, (g + 1) * gd)
            sg = _dot(ws_ref[g], vn[rows, cols]) + bs_ref[g]
            a_ref[rows, cols] = (gu[rows, cols] * sg).astype(bf16)


def _sgu_bwd_rows(dad, ua, va, ng_ref, nb_ref, ws_ref, bs_ref, dua_ref, dva_ref, dws_ref, dbs_ref, dng_ref, dnb_ref,
                  dvn_scr):
    tm, d = ua.shape
    gd = d // SGU_GROUPS
    gu = _gelu(ua)
    xh, rstd, vn = _sgu_norm(va, ng_ref[...], nb_ref[...])
    dsb = (dad * gu).astype(bf16)
    for c in range(tm // SGU_CHUNK):
        rows = slice(c * SGU_CHUNK, (c + 1) * SGU_CHUNK)
        for g in range(SGU_GROUPS):
            cols = slice(g * gd, (g + 1) * gd)
            sg = _dot(ws_ref[g], vn[rows, cols]) + bs_ref[g]
            dua_ref[rows, cols] = (dad[rows, cols] * sg * _dgelu(ua[rows, cols])).astype(bf16)
            ds = dsb[rows, cols]
            dvn_scr[rows, cols] = _dot_tn(ws_ref[g], ds)
            dws_ref[g] += _dot_nt(ds, vn[rows, cols])
            dbs_ref[g] += jnp.sum(ds.astype(f32), axis=1, keepdims=True)
    dvn = dvn_scr[...]
    dng_ref[...] += jnp.sum(dvn * xh, axis=0, keepdims=True)
    dnb_ref[...] += jnp.sum(dvn, axis=0, keepdims=True)
    dxh = dvn * ng_ref[...]
    dgv = rstd * (dxh - jnp.mean(dxh, axis=-1, keepdims=True) - xh * jnp.mean(dxh * xh, axis=-1, keepdims=True))
    dva_ref[...] = (dgv * _dgelu(va)).astype(bf16)


def retention_constants(decay_logit, t, dk, zero):
    lg = jax.nn.log_sigmoid(decay_logit.astype(f32) + zero)
    lgf = lg[0][:, None]
    lgb = lg[1][:, None]
    idx = jnp.arange(CHUNK, dtype=f32)[None, :]
    af = jnp.exp((idx + 1.0) * lgf)
    ab = jnp.exp((CHUNK - idx) * lgb)
    kf = jnp.exp((CHUNK - 1.0 - idx) * lgf)
    kb = jnp.exp(idx * lgb)
    cols = jnp.stack([af, ab, kf, kb, af * (idx + 1.0), ab * (CHUNK - idx), kf * (CHUNK - 1.0 - idx), kb * idx], axis=1)
    cols = cols[..., None]
    diff = idx[0][:, None] - idx[0][None, :]
    dfm = jnp.where(diff >= 0, jnp.exp(jnp.maximum(diff, 0.0)[None] * lgf[:, :, None]), 0.0)
    dbm = jnp.where(diff < 0, jnp.exp(jnp.maximum(-diff, 0.0)[None] * lgb[:, :, None]), 0.0)
    mats = jnp.stack([dfm + dbm, dfm * diff[None], dbm * (-diff)[None]], axis=1)
    cdec = jnp.stack([jnp.broadcast_to(jnp.exp(CHUNK * lgf), (RET_HEADS, dk)),
                      jnp.broadcast_to(jnp.exp(CHUNK * lgb), (RET_HEADS, dk))], axis=1)
    theta = ROPE_BASE ** (-jnp.arange(0, dk, 2, dtype=f32) / dk)
    ang = (jnp.arange(t, dtype=f32) + zero)[:, None] * theta[None, :]
    return cols, mats, cdec, jnp.cos(ang), jnp.sin(ang)


def _rot(tr, cos, sin):
    half = tr.shape[-1] // 2
    t1 = tr[:, :half]
    t2 = tr[:, half:]
    return jnp.concatenate([t1 * cos - t2 * sin, t2 * cos + t1 * sin], axis=-1)


def _rot_inv(dt, cos, sin):
    half = dt.shape[-1] // 2
    d1 = dt[:, :half]
    d2 = dt[:, half:]
    return jnp.concatenate([d1 * cos + d2 * sin, d2 * cos - d1 * sin], axis=-1)


def _ret_tile(t):
    return 2048 if t >= 4096 else _row_tile(t)


def _ret_specs(t, d, dk, rt):
    nr = t // rt
    hq = d // dk

    def blk(p, n):
        return (1 - p) * (nr - 1 - n) + p * n

    q_spec = pl.BlockSpec((1, rt, dk), lambda h, p, n: (1, blk(p, n), h))
    k_spec = pl.BlockSpec((1, rt, dk), lambda h, p, n: (1, blk(p, n), hq + h))
    v_spec = pl.BlockSpec((1, rt, dk), lambda h, p, n: (2, blk(p, n), h))
    g_spec = pl.BlockSpec((1, rt, dk), lambda h, p, n: (2, blk(p, n), hq + h))
    tab_spec = pl.BlockSpec((rt, dk // 2), lambda h, p, n: (blk(p, n), 0))
    cols_spec = pl.BlockSpec((1, 8, CHUNK, 1), lambda h, p, n: (h, 0, 0, 0))
    mats_spec = pl.BlockSpec((1, 3, CHUNK, CHUNK), lambda h, p, n: (h, 0, 0, 0))
    cdec_spec = pl.BlockSpec((1, 2, dk), lambda h, p, n: (h, 0, 0))
    in_row = pl.BlockSpec((rt, dk), lambda h, p, n: (blk(p, n), h))
    out_row = pl.BlockSpec((rt, dk), lambda h, p, n: (p * n, h))
    return nr, blk, q_spec, k_spec, v_spec, g_spec, tab_spec, cols_spec, mats_spec, cdec_spec, in_row, out_row


def ret_fwd(proj, cols, mats, cdec, dep):
    _, t, w2 = proj.shape
    d = w2 // 2
    dk = d // RET_HEADS
    rt = _ret_tile(t)
    cpt = rt // CHUNK
    nr, blk, q_spec, k_spec, v_spec, g_spec, _, cols_spec, mats_spec, cdec_spec, _, out_row = _ret_specs(t, d, dk, rt)

    def body(q_ref, k_ref, v_ref, g_ref, cols_ref, mats_ref, cdec_ref, dep_ref, r_ref, rn_ref, sb_scr, st):
        p = pl.program_id(1)
        n = pl.program_id(2)
        af, ab, kf, kb = cols_ref[0, 0], cols_ref[0, 1], cols_ref[0, 2], cols_ref[0, 3]
        cf = cdec_ref[0, 0:1, :]
        cb = cdec_ref[0, 1:2, :]

        @pl.when(n == 0)
        def _():
            st[...] = jnp.zeros_like(st)

        @pl.when(p == 0)
        def _():
            for j in reversed(range(cpt)):
                rows = slice(j * CHUNK, (j + 1) * CHUNK)
                ch = blk(p, n) * cpt + j
                kk = k_ref[0, rows, :].astype(f32)
                sb_scr[ch] = st[...].astype(bf16)
                st[...] = st[...] * cb + _dot_tn((kk * kb).astype(bf16), v_ref[0, rows, :])

        @pl.when(p == 1)
        def _():
            for j in range(cpt):
                rows = slice(j * CHUNK, (j + 1) * CHUNK)
                ch = blk(p, n) * cpt + j
                qb = q_ref[0, rows, :]
                kkb = k_ref[0, rows, :]
                q = qb.astype(f32)
                kk = kkb.astype(f32)
                v = v_ref[0, rows, :]
                pm = (_dot_nt(qb, kkb) * mats_ref[0, 0]).astype(bf16)
                out = (_dot(pm, v) + _dot((q * af).astype(bf16), st[...].astype(bf16))
                       + _dot((q * ab).astype(bf16), sb_scr[ch]))
                st[...] = st[...] * cf + _dot_tn((kk * kf).astype(bf16), v)
                rhat = out * lax.rsqrt(jnp.mean(out * out, axis=-1, keepdims=True) + NORM_EPS)
                gg = g_ref[0, rows, :].astype(f32)
                r_ref[rows, :] = out.astype(bf16)
                rn_ref[rows, :] = (rhat * gg * _sigmoid(gg)).astype(bf16)

    return pl.pallas_call(
        body, name="ret_fwd", grid=(RET_HEADS, 2, nr),
        in_specs=[q_spec, k_spec, v_spec, g_spec, cols_spec, mats_spec, cdec_spec, _ANY],
        out_specs=[out_row, out_row],
        out_shape=[jax.ShapeDtypeStruct((t, d), bf16), jax.ShapeDtypeStruct((t, d), bf16)],
        scratch_shapes=[pltpu.VMEM((t // CHUNK, dk, dk), bf16), pltpu.VMEM((dk, dk), f32)],
        compiler_params=_cparams(),
    )(proj, proj, proj, proj, cols, mats, cdec, dep)


def ret_bwd(drn, r, proj, cols, mats, cdec, cos, sin):
    _, t, w2 = proj.shape
    d = w2 // 2
    dk = d // RET_HEADS
    rt = _ret_tile(t)
    cpt = rt // CHUNK
    nr, blk, q_spec, k_spec, v_spec, g_spec, tab_spec, cols_spec, mats_spec, cdec_spec, in_row, out_row = _ret_specs(t, d, dk, rt)
    scale = dk ** -0.5

    def body(drn_ref, r_ref, q_ref, k_ref, v_ref, g_ref, cos_ref, sin_ref, cols_ref, mats_ref, cdec_ref,
             dq_ref, dk_ref, dv_ref, dg_ref, dlg_ref,
             sb_scr, gf_scr, st_s, st_g, acc_af, acc_ab, acc_vf, acc_vb, acc_sf, acc_sb, dout_scr, dgr_scr):
        p = pl.program_id(1)
        n = pl.program_id(2)
        af, ab, kf, kb = cols_ref[0, 0], cols_ref[0, 1], cols_ref[0, 2], cols_ref[0, 3]
        af1, ab1, kf1, kb1 = cols_ref[0, 4], cols_ref[0, 5], cols_ref[0, 6], cols_ref[0, 7]
        cf = cdec_ref[0, 0:1, :]
        cb = cdec_ref[0, 1:2, :]

        @pl.when(n == 0)
        def _():
            st_s[...] = jnp.zeros_like(st_s)
            st_g[...] = jnp.zeros_like(st_g)

        @pl.when(jnp.logical_and(n == 0, p == 1))
        def _():
            for a in (acc_af, acc_ab, acc_vf, acc_vb, acc_sf, acc_sb):
                a[...] = jnp.zeros_like(a)

        def load(rows):
            cs, sn = cos_ref[rows, :], sin_ref[rows, :]
            q = q_ref[0, rows, :].astype(f32)
            kk = k_ref[0, rows, :].astype(f32)
            rr = r_ref[rows, :].astype(f32)
            rstd = lax.rsqrt(jnp.mean(rr * rr, axis=-1, keepdims=True) + NORM_EPS)
            rhat = rr * rstd
            gg = g_ref[0, rows, :].astype(f32)
            sg = _sigmoid(gg)
            dd = drn_ref[rows, :].astype(f32)
            drhat = dd * gg * sg
            dout = rstd * (drhat - rhat * jnp.mean(drhat * rhat, axis=-1, keepdims=True))
            dgr = dd * rhat * _dsilu(gg, sg)
            return q, kk, dout.astype(bf16), dgr, cs, sn

        @pl.when(p == 0)
        def _():
            for j in reversed(range(cpt)):
                rows = slice(j * CHUNK, (j + 1) * CHUNK)
                ch = blk(p, n) * cpt + j
                q, kk, doutb, dgr, _, _ = load(rows)
                kept = pl.ds(pl.multiple_of(ch * CHUNK, CHUNK), CHUNK)
                dout_scr[kept, :] = doutb
                dgr_scr[kept, :] = dgr.astype(bf16)
                sb_scr[ch] = st_s[...].astype(bf16)
                gf_scr[ch] = st_g[...].astype(bf16)
                st_s[...] = st_s[...] * cb + _dot_tn((kk * kb).astype(bf16), v_ref[0, rows, :])
                st_g[...] = st_g[...] * cf + _dot_tn((q * af).astype(bf16), doutb)

        @pl.when(p == 1)
        def _():
            for j in range(cpt):
                rows = slice(j * CHUNK, (j + 1) * CHUNK)
                ch = blk(p, n) * cpt + j
                kept = pl.ds(pl.multiple_of(ch * CHUNK, CHUNK), CHUNK)
                doutb = dout_scr[kept, :]
                cs, sn = cos_ref[rows, :], sin_ref[rows, :]
                v = v_ref[0, rows, :]
                qb = q_ref[0, rows, :]
                kkb = k_ref[0, rows, :]
                q = qb.astype(f32)
                kk = kkb.astype(f32)
                sf = st_s[...]
                gb = st_g[...]
                sfb = sf.astype(bf16)
                gbb = gb.astype(bf16)
                sbb = sb_scr[ch]
                gfb = gf_scr[ch]
                dmat = mats_ref[0, 0]
                scores = _dot_nt(qb, kkb)
                dpraw = _dot_nt(doutb, v)
                dpb = (dpraw * dmat).astype(bf16)
                pmb = (scores * dmat).astype(bf16)
                x1 = _dot_nt(doutb, sfb)
                x2 = _dot_nt(doutb, sbb)
                y1 = _dot_nt(v, gfb)
                y2 = _dot_nt(v, gbb)
                kdf = (kk * kf).astype(bf16)
                kdb = (kk * kb).astype(bf16)
                dq = _dot(dpb, kkb) + x1 * af + x2 * ab
                dkk = _dot_tn(dpb, qb) + y1 * kf + y2 * kb
                dv = _dot_tn(pmb, doutb) + _dot(kdf, gfb) + _dot(kdb, gbb)
                ps = dpraw * scores
                acc_af[...] += ps * mats_ref[0, 1]
                acc_ab[...] += ps * mats_ref[0, 2]
                acc_vf[...] += x1 * q * af1 + y1 * kk * kf1
                acc_vb[...] += x2 * q * ab1 + y2 * kk * kb1
                acc_sf[...] += gfb.astype(f32) * sf
                acc_sb[...] += gb * sbb.astype(f32)
                st_s[...] = sf * cf + _dot_tn(kdf, v)
                st_g[...] = gb * cb + _dot_tn((q * ab).astype(bf16), doutb)
                dq_ref[rows, :] = _rot_inv(dq, cs, sn).astype(bf16)
                dk_ref[rows, :] = (_rot_inv(dkk, cs, sn) * scale).astype(bf16)
                dv_ref[rows, :] = dv.astype(bf16)
                dg_ref[rows, :] = dgr_scr[kept, :]

        @pl.when(jnp.logical_and(p == 1, n == nr - 1))
        def _():
            tf = jnp.sum(acc_af[...]) + jnp.sum(acc_vf[...]) + CHUNK * jnp.sum(acc_sf[...] * cf)
            tb = jnp.sum(acc_ab[...]) + jnp.sum(acc_vb[...]) + CHUNK * jnp.sum(acc_sb[...] * cb)
            rid = lax.broadcasted_iota(jnp.int32, (8, 128), 0)
            dlg_ref[0] = jnp.where(rid == 0, tf, jnp.where(rid == 1, tb, 0.0))

    nch = t // CHUNK
    return pl.pallas_call(
        body, name="ret_bwd", grid=(RET_HEADS, 2, nr),
        in_specs=[in_row, in_row, q_spec, k_spec, v_spec, g_spec, tab_spec, tab_spec, cols_spec, mats_spec, cdec_spec],
        out_specs=[out_row, out_row, out_row, out_row, pl.BlockSpec((1, 8, 128), lambda h, p, n: (h, 0, 0))],
        out_shape=[jax.ShapeDtypeStruct((t, d), bf16)] * 4 + [jax.ShapeDtypeStruct((RET_HEADS, 8, 128), f32)],
        scratch_shapes=[pltpu.VMEM((nch, dk, dk), bf16), pltpu.VMEM((nch, dk, dk), bf16),
                        pltpu.VMEM((dk, dk), f32), pltpu.VMEM((dk, dk), f32),
                        pltpu.VMEM((CHUNK, CHUNK), f32), pltpu.VMEM((CHUNK, CHUNK), f32),
                        pltpu.VMEM((CHUNK, dk), f32), pltpu.VMEM((CHUNK, dk), f32),
                        pltpu.VMEM((dk, dk), f32), pltpu.VMEM((dk, dk), f32),
                        pltpu.VMEM((t, dk), bf16), pltpu.VMEM((t, dk), bf16)],
        compiler_params=_cparams(VMEM_LIMIT_WIDE),
    )(drn, r, proj, proj, proj, proj, cos, sin, cols, mats, cdec)


def mix_fwd(a, rn, proj, wa, wb, wo, x1):
    t, d = x1.shape
    tm = _row_tile(t)

    def body(a_ref, rn_ref, p_ref, wa_ref, wb_ref, wo_ref, x_ref, xo_ref, ba_ref, br_ref):
        ba = _dot(a_ref[...], wa_ref[...])
        br = _dot(rn_ref[...], wb_ref[...])
        sa = _sigmoid(p_ref[0, :, 0:d].astype(f32))
        sb = _sigmoid(p_ref[0, :, d:2 * d].astype(f32))
        mix = (sa * ba + sb * br).astype(bf16)
        xo_ref[...] = x_ref[...] + _dot(mix, wo_ref[...])
        ba_ref[...] = ba.astype(bf16)
        br_ref[...] = br.astype(bf16)

    row = pl.BlockSpec((tm, d), lambda i: (i, 0))
    wsp = pl.BlockSpec((d, d), lambda i: (0, 0))
    return pl.pallas_call(
        body, name="mix_fwd", grid=(t // tm,),
        in_specs=[row, row, pl.BlockSpec((1, tm, 2 * d), lambda i: (3, i, 0)), wsp, wsp, wsp, row],
        out_specs=[row, row, row],
        out_shape=[jax.ShapeDtypeStruct((t, d), f32), jax.ShapeDtypeStruct((t, d), bf16), jax.ShapeDtypeStruct((t, d), bf16)],
        compiler_params=_cparams(),
    )(a, rn, proj, wa, wb, wo, x1)


def mix_bwd_act(dx2, ba, br, proj, wa, wb, wo, sng, snb, ws, bs, dep):
    t, d = dx2.shape
    tm = _row_tile(t)

    def body(dx_ref, ba_ref, br_ref, p_ref, uv_ref, wa_ref, wb_ref, wo_ref, sng_ref, snb_ref, ws_ref, bs_ref, dep_ref,
             drn_ref, dga_ref, dgb_ref, mix_ref, dba_ref, dbr_ref, dxb_ref,
             dua_ref, dva_ref, dws_ref, dbs_ref, dng_ref, dnb_ref, dvn_scr):
        _zero_at_first_step(dws_ref, dbs_ref, dng_ref, dnb_ref)
        dxb = dx_ref[...].astype(bf16)
        dxb_ref[...] = dxb
        dmix = _dot_nt(dxb, wo_ref[...])
        ba = ba_ref[...].astype(f32)
        br = br_ref[...].astype(f32)
        sa = _sigmoid(p_ref[0, :, 0:d].astype(f32))
        sb = _sigmoid(p_ref[0, :, d:2 * d].astype(f32))
        mix_ref[...] = (sa * ba + sb * br).astype(bf16)
        dba = (dmix * sa).astype(bf16)
        dbr = (dmix * sb).astype(bf16)
        dba_ref[...] = dba
        dbr_ref[...] = dbr
        dga_ref[...] = (dmix * ba * sa * (1.0 - sa)).astype(bf16)
        dgb_ref[...] = (dmix * br * sb * (1.0 - sb)).astype(bf16)
        drn_ref[...] = _dot_nt(dbr, wb_ref[...]).astype(bf16)
        da = _dot_nt(dba, wa_ref[...])
        _sgu_bwd_rows(da, uv_ref[0, :, 0:d].astype(f32), uv_ref[0, :, d:2 * d].astype(f32), sng_ref, snb_ref, ws_ref,
                      bs_ref, dua_ref, dva_ref, dws_ref, dbs_ref, dng_ref, dnb_ref, dvn_scr)

    row = pl.BlockSpec((tm, d), lambda i: (i, 0))
    vec = pl.BlockSpec((1, d), lambda i: (0, 0))
    wsp = pl.BlockSpec((d, d), lambda i: (0, 0))
    sws = pl.BlockSpec((SGU_GROUPS, SGU_CHUNK, SGU_CHUNK), lambda i: (0, 0, 0))
    sbs = pl.BlockSpec((SGU_GROUPS, SGU_CHUNK, 1), lambda i: (0, 0, 0))
    return pl.pallas_call(
        body, name="mix_bwd_act", grid=(t // tm,),
        in_specs=[row, row, row, pl.BlockSpec((1, tm, 2 * d), lambda i: (3, i, 0)),
                  pl.BlockSpec((1, tm, 2 * d), lambda i: (0, i, 0)), wsp, wsp, wsp, vec, vec, sws, sbs, _ANY],
        out_specs=[row] * 9 + [sws, sbs, vec, vec],
        out_shape=[jax.ShapeDtypeStruct((t, d), bf16)] * 9
        + [jax.ShapeDtypeStruct((SGU_GROUPS, SGU_CHUNK, SGU_CHUNK), f32), jax.ShapeDtypeStruct((SGU_GROUPS, SGU_CHUNK, 1), f32),
           jax.ShapeDtypeStruct((1, d), f32), jax.ShapeDtypeStruct((1, d), f32)],
        scratch_shapes=[pltpu.VMEM((tm, d), f32)],
        compiler_params=_cparams(VMEM_LIMIT_WIDE),
    )(dx2, ba, br, proj, proj, wa, wb, wo, sng, snb, ws, bs, dep)


def inproj_bwd_act(segs, win, x1, ng, dx2):
    t, d = x1.shape
    s4 = win.shape[0]
    tm = _row_tile(t)
    nseg = len(segs)

    def body(*refs):
        seg_refs = refs[:nseg]
        w_ref, x_ref, ng_ref, dx2_ref, dx1_ref, db_ref, dng_ref = refs[nseg:]
        _zero_at_first_step(db_ref, dng_ref)
        dh = None
        for e, sr in enumerate(seg_refs):
            sb = sr[...]
            part = _dot_nt(sb, w_ref[e // 2, :, (e % 2) * d:(e % 2 + 1) * d])
            dh = part if dh is None else dh + part
            db_ref[e] += jnp.sum(sb.astype(f32), axis=0, keepdims=True)
        _, xh, r = _rms(x_ref[...], ng_ref[...])
        dx1_ref[...] = dx2_ref[...] + _rms_bwd(dh, xh, r, ng_ref[...])
        dng_ref[...] += jnp.sum(dh * xh, axis=0, keepdims=True)

    row = pl.BlockSpec((tm, d), lambda i: (i, 0))
    vec = pl.BlockSpec((1, d), lambda i: (0, 0))
    return pl.pallas_call(
        body, name="inproj_bwd_act", grid=(t // tm,),
        in_specs=[row] * nseg + [pl.BlockSpec((s4, d, 2 * d), lambda i: (0, 0, 0), pipeline_mode=pl.Buffered(1)),
                                 row, vec, row],
        out_specs=[row, pl.BlockSpec((nseg, 1, d), lambda i: (0, 0, 0)), vec],
        out_shape=[jax.ShapeDtypeStruct((t, d), f32), jax.ShapeDtypeStruct((nseg, 1, d), f32),
                   jax.ShapeDtypeStruct((1, d), f32)],
        compiler_params=_cparams(VMEM_LIMIT_WIDE),
    )(*segs, win, x1, ng, dx2)


def _place():
    return lax.axis_index("x"), lax.axis_index("y"), lax.axis_index("c")


def _other_chips(x, y):
    return [(1 - x, y), (x, 1 - y), (1 - x, 1 - y)]


_ANY = pl.BlockSpec(memory_space=pl.ANY)


_HBM = pl.BlockSpec(memory_space=pltpu.HBM)
_SEM = pl.BlockSpec(memory_space=pltpu.SEMAPHORE)
_EFFECT = pltpu.SideEffectType.DATAFLOW_SIDE_EFFECTING


def _hbm(a):
    return pltpu.with_memory_space_constraint(a, pltpu.HBM)


def _half_rows(ref, c):
    half = ref.shape[1] // 2
    return pl.ds(pl.multiple_of(c * half, 16), half)


def _chip_copy(src, dst, send_sem, recv_sem, chip, c):
    return pltpu.make_async_remote_copy(src_ref=src, dst_ref=dst, send_sem=send_sem, recv_sem=recv_sem,
                                        device_id=(chip[0], chip[1], c), device_id_type=MESH)


def gather_start(bufs, groups, name):
    nb, ng = len(bufs), len(groups)

    def body(*refs):
        ins = refs[:nb]
        sems = refs[nb:nb + 2 * ng]
        token = refs[-1]
        x, y, c = _place()
        k = 2 * x + y
        for gi, grp in enumerate(groups):
            for wi, w in enumerate(grp):
                mine = ins[w].at[k, _half_rows(ins[w], c)]
                for j, chip in enumerate(_other_chips(x, y)):
                    _chip_copy(mine, mine, sems[2 * gi].at[3 * wi + j], sems[2 * gi + 1].at[3 * wi + j], chip, c).start()
        token[...] = jnp.zeros_like(token)

    sem_shapes = []
    for grp in groups:
        sem_shapes += [pltpu.SemaphoreType.DMA((3 * len(grp),)), pltpu.SemaphoreType.DMA((3 * len(grp),))]
    outs = pl.pallas_call(
        body, name=name,
        out_shape=sem_shapes + [pltpu.HBM(b.shape, b.dtype) for b in bufs] + [jax.ShapeDtypeStruct((8, 128), f32)],
        in_specs=[_HBM] * nb,
        out_specs=[_SEM] * (2 * ng) + [_HBM] * nb + [pl.BlockSpec(memory_space=pltpu.VMEM)],
        input_output_aliases={w: 2 * ng + w for w in range(nb)},
        compiler_params=pltpu.CompilerParams(has_side_effects=_EFFECT),
    )(*[_hbm(b) for b in bufs])
    sems = [(outs[2 * gi], outs[2 * gi + 1]) for gi in range(ng)]
    return sems, list(outs[2 * ng:2 * ng + nb]), outs[-1]


def gather_wait(bufs, sems, after, name):
    n = len(bufs)

    def body(*refs):
        ins = refs[:n]
        send_sems, recv_sems = refs[n], refs[n + 1]
        x, y, c = _place()
        k = 2 * x + y
        for wi in range(n):
            half = _half_rows(ins[wi], c)
            for j, chip in enumerate(_other_chips(x, y)):
                cp = _chip_copy(ins[wi].at[k, half], ins[wi].at[2 * chip[0] + chip[1], half], send_sems.at[3 * wi + j],
                                recv_sems.at[3 * wi + j], chip, c)
                cp.wait_send()
                cp.wait_recv()

    outs = pl.pallas_call(
        body, name=name,
        out_shape=[pltpu.HBM(b.shape, b.dtype) for b in bufs],
        in_specs=[_HBM] * n + [_SEM, _SEM, _ANY],
        out_specs=[_HBM] * n,
        input_output_aliases={i: i for i in range(n)},
        compiler_params=pltpu.CompilerParams(has_side_effects=_EFFECT),
    )(*bufs, sems[0], sems[1], after)
    return list(outs)


def gather_forward(bufs, name):
    n = len(bufs)

    def body(*refs):
        ins = refs[n:2 * n]
        send_sems, recv_sems = refs[2 * n], refs[2 * n + 1]
        x, y, c = _place()
        copies = []
        for wi in range(n):
            for j, chip in enumerate(_other_chips(x, y)):
                kp = 2 * chip[0] + chip[1]
                got = ins[wi].at[kp, _half_rows(ins[wi], c)]
                cp = pltpu.make_async_remote_copy(
                    src_ref=got, dst_ref=got, send_sem=send_sems.at[3 * wi + j], recv_sem=recv_sems.at[3 * wi + j],
                    device_id=(x, y, 1 - c), device_id_type=MESH)
                cp.start()
                copies.append((cp, wi, kp, j))
        for cp, wi, kp, j in copies:
            cp.wait_send()
            theirs = ins[wi].at[kp, _half_rows(ins[wi], 1 - c)]
            pltpu.make_async_remote_copy(
                src_ref=theirs, dst_ref=theirs, send_sem=send_sems.at[3 * wi + j], recv_sem=recv_sems.at[3 * wi + j],
                device_id=(x, y, 1 - c), device_id_type=MESH).wait_recv()

    outs = pl.pallas_call(
        body, name=name,
        out_shape=[jax.ShapeDtypeStruct(b.shape, b.dtype) for b in bufs],
        in_specs=[_ANY] * n, out_specs=[_ANY] * n,
        input_output_aliases={i: i for i in range(n)},
        scratch_shapes=[pltpu.SemaphoreType.DMA((3 * n,)), pltpu.SemaphoreType.DMA((3 * n,))],
    )(*bufs)
    return list(outs)


def forward_start(bufs, name):
    n = len(bufs)

    def body(*refs):
        x, y, c = _place()
        for wi in range(n):
            for j, chip in enumerate(_other_chips(x, y)):
                got = refs[wi].at[2 * chip[0] + chip[1], _half_rows(refs[wi], c)]
                _sibling_copy(got, got, refs[n].at[3 * wi + j], refs[n + 1].at[3 * wi + j]).start()
        refs[-1][...] = jnp.zeros_like(refs[-1])

    return _split_start(body, name, 3 * n, list(bufs))


def forward_wait(bufs, sems, after, name):
    n = len(bufs)

    def body(*refs):
        x, y, c = _place()
        for wi in range(n):
            for j, chip in enumerate(_other_chips(x, y)):
                kp = 2 * chip[0] + chip[1]
                got = refs[wi].at[kp, _half_rows(refs[wi], c)]
                theirs = refs[wi].at[kp, _half_rows(refs[wi], 1 - c)]
                _sibling_copy(got, got, refs[n].at[3 * wi + j], refs[n + 1].at[3 * wi + j]).wait_send()
                _sibling_copy(theirs, theirs, refs[n].at[3 * wi + j], refs[n + 1].at[3 * wi + j]).wait_recv()

    return _split_wait(body, name, list(bufs), sems, after)


def exchange_start(grads, name):
    n = len(grads)
    lands = [lax.empty((3,) + g.shape[1:], g.dtype) for g in grads]

    def body(*refs):
        ins = refs[:n]
        land = refs[n:2 * n]
        send_sems, recv_sems = refs[2 * n], refs[2 * n + 1]
        token = refs[-1]
        x, y, c = _place()
        for wi in range(n):
            for j, chip in enumerate(_other_chips(x, y)):
                _chip_copy(ins[wi].at[2 * chip[0] + chip[1]], land[wi].at[j], send_sems.at[3 * wi + j],
                           recv_sems.at[3 * wi + j], chip, c).start()
        token[...] = jnp.zeros_like(token)

    outs = pl.pallas_call(
        body, name=name,
        out_shape=[pltpu.SemaphoreType.DMA((3 * n,)), pltpu.SemaphoreType.DMA((3 * n,))]
        + [pltpu.HBM(g.shape, g.dtype) for g in grads] + [pltpu.HBM(l.shape, l.dtype) for l in lands]
        + [jax.ShapeDtypeStruct((8, 128), f32)],
        in_specs=[_HBM] * (2 * n),
        out_specs=[_SEM, _SEM] + [_HBM] * (2 * n) + [pl.BlockSpec(memory_space=pltpu.VMEM)],
        input_output_aliases={i: 2 + i for i in range(2 * n)},
        compiler_params=pltpu.CompilerParams(has_side_effects=_EFFECT),
    )(*[_hbm(g) for g in grads], *[_hbm(l) for l in lands])
    return (outs[0], outs[1]), list(outs[2:2 + n]), list(outs[2 + n:2 + 2 * n]), outs[-1]


def exchange_wait(grads, lands, sems, after, name):
    n = len(grads)

    def body(*refs):
        ins = refs[:n]
        land = refs[n:2 * n]
        send_sems, recv_sems = refs[2 * n], refs[2 * n + 1]
        x, y, c = _place()
        for wi in range(n):
            for j, chip in enumerate(_other_chips(x, y)):
                cp = _chip_copy(ins[wi].at[2 * chip[0] + chip[1]], land[wi].at[j], send_sems.at[3 * wi + j],
                                recv_sems.at[3 * wi + j], chip, c)
                cp.wait_send()
                cp.wait_recv()

    outs = pl.pallas_call(
        body, name=name,
        out_shape=[pltpu.HBM(g.shape, g.dtype) for g in grads] + [pltpu.HBM(l.shape, l.dtype) for l in lands],
        in_specs=[_HBM] * (2 * n) + [_SEM, _SEM, _ANY],
        out_specs=[_HBM] * (2 * n),
        input_output_aliases={i: i for i in range(2 * n)},
        compiler_params=pltpu.CompilerParams(has_side_effects=_EFFECT),
    )(*grads, *lands, sems[0], sems[1], after)
    return list(outs[:n]), list(outs[n:])


def _split_start(body, name, n_sems, operands):
    n = len(operands)
    outs = pl.pallas_call(
        body, name=name,
        out_shape=[pltpu.SemaphoreType.DMA((n_sems,)), pltpu.SemaphoreType.DMA((n_sems,))]
        + [pltpu.HBM(o.shape, o.dtype) for o in operands] + [jax.ShapeDtypeStruct((8, 128), f32)],
        in_specs=[_HBM] * n,
        out_specs=[_SEM, _SEM] + [_HBM] * n + [pl.BlockSpec(memory_space=pltpu.VMEM)],
        input_output_aliases={i: 2 + i for i in range(n)},
        compiler_params=pltpu.CompilerParams(has_side_effects=_EFFECT),
    )(*[_hbm(o) for o in operands])
    return (outs[0], outs[1]), list(outs[2:2 + n]), outs[-1]


def _split_wait(body, name, operands, sems, after):
    n = len(operands)
    outs = pl.pallas_call(
        body, name=name,
        out_shape=[pltpu.HBM(o.shape, o.dtype) for o in operands],
        in_specs=[_HBM] * n + [_SEM, _SEM, _ANY],
        out_specs=[_HBM] * n,
        input_output_aliases={i: i for i in range(n)},
        compiler_params=pltpu.CompilerParams(has_side_effects=_EFFECT),
    )(*operands, sems[0], sems[1], after)
    return list(outs)


def _sibling_copy(src, dst, send_sem, recv_sem):
    x, y, c = _place()
    return pltpu.make_async_remote_copy(src_ref=src, dst_ref=dst, send_sem=send_sem, recv_sem=recv_sem,
                                        device_id=(x, y, 1 - c), device_id_type=MESH)


def swap_start(parts, name):
    n = len(parts)

    def body(*refs):
        for w in range(n):
            _sibling_copy(refs[w], refs[n + w], refs[2 * n].at[w], refs[2 * n + 1].at[w]).start()
        refs[-1][...] = jnp.zeros_like(refs[-1])

    sems, ops, token = _split_start(body, name, n, list(parts) + [lax.empty(p.shape, p.dtype) for p in parts])
    return sems, ops[:n], ops[n:], token


def swap_wait(parts, lands, sems, after, name):
    n = len(parts)

    def body(*refs):
        for w in range(n):
            cp = _sibling_copy(refs[w], refs[n + w], refs[2 * n].at[w], refs[2 * n + 1].at[w])
            cp.wait_send()
            cp.wait_recv()

    outs = _split_wait(body, name, list(parts) + list(lands), sems, after)
    return outs[:n], outs[n:]


def _all_peers(x, y, c):
    return [(1 - x if m & 4 else x, 1 - y if m & 2 else y, 1 - c if m & 1 else c) for m in range(1, N_DEV)]


def small_start(block):
    land = jnp.broadcast_to(block[None], (N_DEV,) + block.shape)

    def body(b_ref, land_ref, send_sems, recv_sems, b_thru, land_thru, token):
        x, y, c = _place()
        me = 4 * x + 2 * y + c
        for m, peer in enumerate(_all_peers(x, y, c)):
            pltpu.make_async_remote_copy(src_ref=b_ref, dst_ref=land_ref.at[me], send_sem=send_sems.at[m],
                                         recv_sem=recv_sems.at[m], device_id=peer, device_id_type=MESH).start()
        token[...] = jnp.zeros_like(token)

    sems, ops, token = _split_start(body, "small_start", N_DEV - 1, [block, land])
    return sems, ops[0], ops[1], token


def small_wait(block, land, sems, after):
    def body(b_ref, land_ref, send_sems, recv_sems, after_ref, b_thru, land_thru):
        x, y, c = _place()
        for m, (px, py, pc) in enumerate(_all_peers(x, y, c)):
            cp = pltpu.make_async_remote_copy(src_ref=b_ref, dst_ref=land_ref.at[4 * px + 2 * py + pc],
                                              send_sem=send_sems.at[m], recv_sem=recv_sems.at[m],
                                              device_id=(px, py, pc), device_id_type=MESH)
            cp.wait_send()
            cp.wait_recv()

    return _split_wait(body, "small_wait", [block, land], sems, after)[1]


def _adamw(w, g, m, v):
    m = ADAM_B1 * m + (1.0 - ADAM_B1) * g
    v = ADAM_B2 * v + (1.0 - ADAM_B2) * (g * g)
    m_hat = m / (1.0 - ADAM_B1 ** ADAM_STEP)
    v_hat = v / (1.0 - ADAM_B2 ** ADAM_STEP)
    delta = -ADAM_LR * (m_hat / (jnp.sqrt(v_hat) + ADAM_EPS) + ADAM_WD * w)
    return delta, m, v


EW_BLOCK_BYTES = 2 * 1024 * 1024


def _ew_tile(rows, cols):
    for cand in (512, 352, 256, 176, 128, 64, 32, 16, 8):
        if rows % cand == 0 and cand * cols * 4 <= EW_BLOCK_BYTES:
            return cand
    return rows


def sum_partials(chip, own, land, name):
    _, r, c = own.shape
    tr = _ew_tile(r, c)

    def body(k_ref, own_ref, p_ref, o_ref):
        o_ref[...] = ((own_ref[0].astype(f32) + p_ref[0].astype(f32)) + p_ref[1].astype(f32)) + p_ref[2].astype(f32)

    return pl.pallas_call(
        body, name=name,
        grid_spec=pltpu.PrefetchScalarGridSpec(
            num_scalar_prefetch=1, grid=(r // tr,),
            in_specs=[pl.BlockSpec((1, tr, c), lambda i, k: (k[0], i, 0)), pl.BlockSpec((3, tr, c), lambda i, k: (0, i, 0))],
            out_specs=pl.BlockSpec((tr, c), lambda i, k: (i, 0))),
        out_shape=jax.ShapeDtypeStruct((r, c), f32),
        compiler_params=_cparams(),
    )(chip, own, land)


def adamw_shard(p_mine, p_sibling, w, m, v, name):
    r, c = w.shape
    tr = _ew_tile(r, c)

    def body(a_ref, b_ref, w_ref, m_ref, v_ref, g_ref, d_ref, mo_ref, vo_ref):
        g = a_ref[...] + b_ref[...]
        delta, mn, vn = _adamw(w_ref[...], g, m_ref[...], v_ref[...])
        g_ref[...] = g
        d_ref[...] = delta
        mo_ref[...] = mn
        vo_ref[...] = vn

    blk = pl.BlockSpec((tr, c), lambda i: (i, 0))
    return pl.pallas_call(
        body, name=name, grid=(r // tr,),
        in_specs=[blk] * 5, out_specs=[blk] * 4,
        out_shape=[jax.ShapeDtypeStruct((r, c), f32)] * 4,
        compiler_params=_cparams(),
    )(p_mine, p_sibling, w, m, v)


def adamw_small(g8, w, m, v):
    _, r, lanes = g8.shape

    def body(g_ref, w_ref, m_ref, v_ref, go_ref, d_ref, mo_ref, vo_ref):
        g = g_ref[0]
        for i in range(1, N_DEV):
            g = g + g_ref[i]
        delta, mn, vn = _adamw(w_ref[...], g, m_ref[...], v_ref[...])
        go_ref[...] = g
        d_ref[...] = delta
        mo_ref[...] = mn
        vo_ref[...] = vn

    return pl.pallas_call(
        body, name="adamw_small",
        out_shape=[jax.ShapeDtypeStruct((r, lanes), f32)] * 4,
        compiler_params=_cparams(),
    )(g8, w, m, v)


def _size(shape):
    n = 1
    for e in shape:
        n *= e
    return n


def _pack_rows(shapes):
    rows = [-(-_size(s) // 1024) * 8 for s in shapes]
    return rows, sum(rows)


def _pack(arrs, shapes):
    rows, _ = _pack_rows(shapes)
    parts = [jnp.pad(a.reshape(-1).astype(f32), (0, r * 128 - _size(s))).reshape(r, 128)
             for a, s, r in zip(arrs, shapes, rows)]
    return jnp.concatenate(parts, axis=0)


def _unpack(block, shapes):
    rows, _ = _pack_rows(shapes)
    out, off = [], 0
    for s, r in zip(shapes, rows):
        out.append(block[off:off + r].reshape(-1)[:_size(s)].reshape(s))
        off += r
    return out


TRANSPOSED = ("ffn1_w_gate", "ffn1_w_up", "ffn2_w_gate", "ffn2_w_up")


def _shard2d(a, n):
    return a[0].T if n in TRANSPOSED else a[0]


def _unshard(a, n):
    return (a.T if n in TRANSPOSED else a)[None]


BIG = ("ffn1_w_gate", "ffn1_w_up", "ffn1_w_down", "w_in", "w_branch_a", "w_branch_b", "w_out",
       "ffn2_w_gate", "ffn2_w_up", "ffn2_w_down")
SMALL = ("ffn1_norm", "mix_norm", "b_in", "sgu_norm_g", "sgu_norm_b", "sgu_w_s", "sgu_b_s", "ret_decay_logit",
         "ffn2_norm", "final_norm")
WEIGHTS = ("ffn1_norm", "ffn1_w_gate", "ffn1_w_up", "ffn1_w_down", "mix_norm", "w_in", "b_in", "sgu_norm_g",
           "sgu_norm_b", "sgu_w_s", "sgu_b_s", "ret_decay_logit", "w_branch_a", "w_branch_b", "w_out", "ffn2_norm",
           "ffn2_w_gate", "ffn2_w_up", "ffn2_w_down", "final_norm")


def kernel(x, ffn1_norm, ffn1_w_gate, ffn1_w_up, ffn1_w_down, mix_norm, w_in, b_in, sgu_norm_g, sgu_norm_b, sgu_w_s, sgu_b_s, ret_decay_logit, w_branch_a, w_branch_b, w_out, ffn2_norm, ffn2_w_gate, ffn2_w_up, ffn2_w_down, final_norm, loss_target, m_ffn1_norm, m_ffn1_w_gate, m_ffn1_w_up, m_ffn1_w_down, m_mix_norm, m_w_in, m_b_in, m_sgu_norm_g, m_sgu_norm_b, m_sgu_w_s, m_sgu_b_s, m_ret_decay_logit, m_w_branch_a, m_w_branch_b, m_w_out, m_ffn2_norm, m_ffn2_w_gate, m_ffn2_w_up, m_ffn2_w_down, m_final_norm, v_ffn1_norm, v_ffn1_w_gate, v_ffn1_w_up, v_ffn1_w_down, v_mix_norm, v_w_in, v_b_in, v_sgu_norm_g, v_sgu_norm_b, v_sgu_w_s, v_sgu_b_s, v_ret_decay_logit, v_w_branch_a, v_w_branch_b, v_w_out, v_ffn2_norm, v_ffn2_w_gate, v_ffn2_w_up, v_ffn2_w_down, v_final_norm):
    p = dict(ffn1_norm=ffn1_norm, ffn1_w_gate=ffn1_w_gate, ffn1_w_up=ffn1_w_up, ffn1_w_down=ffn1_w_down,
             mix_norm=mix_norm, w_in=w_in, b_in=b_in, sgu_norm_g=sgu_norm_g, sgu_norm_b=sgu_norm_b, sgu_w_s=sgu_w_s,
             sgu_b_s=sgu_b_s, ret_decay_logit=ret_decay_logit, w_branch_a=w_branch_a, w_branch_b=w_branch_b,
             w_out=w_out, ffn2_norm=ffn2_norm, ffn2_w_gate=ffn2_w_gate, ffn2_w_up=ffn2_w_up, ffn2_w_down=ffn2_w_down,
             final_norm=final_norm)
    mom = dict(ffn1_norm=m_ffn1_norm, ffn1_w_gate=m_ffn1_w_gate, ffn1_w_up=m_ffn1_w_up, ffn1_w_down=m_ffn1_w_down,
               mix_norm=m_mix_norm, w_in=m_w_in, b_in=m_b_in, sgu_norm_g=m_sgu_norm_g, sgu_norm_b=m_sgu_norm_b,
               sgu_w_s=m_sgu_w_s, sgu_b_s=m_sgu_b_s, ret_decay_logit=m_ret_decay_logit, w_branch_a=m_w_branch_a,
               w_branch_b=m_w_branch_b, w_out=m_w_out, ffn2_norm=m_ffn2_norm, ffn2_w_gate=m_ffn2_w_gate,
               ffn2_w_up=m_ffn2_w_up, ffn2_w_down=m_ffn2_w_down, final_norm=m_final_norm)
    var = dict(ffn1_norm=v_ffn1_norm, ffn1_w_gate=v_ffn1_w_gate, ffn1_w_up=v_ffn1_w_up, ffn1_w_down=v_ffn1_w_down,
               mix_norm=v_mix_norm, w_in=v_w_in, b_in=v_b_in, sgu_norm_g=v_sgu_norm_g, sgu_norm_b=v_sgu_norm_b,
               sgu_w_s=v_sgu_w_s, sgu_b_s=v_sgu_b_s, ret_decay_logit=v_ret_decay_logit, w_branch_a=v_w_branch_a,
               w_branch_b=v_w_branch_b, w_out=v_w_out, ffn2_norm=v_ffn2_norm, ffn2_w_gate=v_ffn2_w_gate,
               ffn2_w_up=v_ffn2_w_up, ffn2_w_down=v_ffn2_w_down, final_norm=v_final_norm)

    xs = x[0]
    tgt = loss_target[0]
    t, d = xs.shape
    dk = d // RET_HEADS

    shards2d = {n: _shard2d(p[n], n) for n in BIG}
    chip = (2 * lax.axis_index("x") + lax.axis_index("y")).astype(jnp.int32).reshape(1)
    groups = {"ffn1": ("ffn1_w_gate", "ffn1_w_up", "ffn1_w_down"), "in": ("w_in",),
              "mix": ("w_branch_a", "w_branch_b", "w_out"), "ffn2": ("ffn2_w_gate", "ffn2_w_up", "ffn2_w_down")}
    def own_slot(n, zero):
        sh = shards2d[n].astype(bf16) + zero
        return lax.dynamic_update_index_in_dim(lax.empty((N_CHIPS,) + sh.shape, bf16), sh, chip[0], 0)

    sems, bufs, tok = gather_start([own_slot(n, jnp.zeros((), bf16)) for n in groups["ffn1"]], [[0, 1, 2]],
                                   "gather_start_ffn1")
    gsem = {"ffn1": sems[0]}
    pending = dict(zip(groups["ffn1"], bufs))
    rest = [n for g in ("in", "mix", "ffn2") for n in groups[g]]
    sems, bufs, tok_rest = gather_start([own_slot(n, tok[0, 0].astype(bf16)) for n in rest],
                                 [[rest.index(n) for n in groups[g]] for g in ("in", "mix", "ffn2")], "gather_start_rest")
    gsem.update(zip(("in", "mix", "ffn2"), sems))
    pending.update(zip(rest, bufs))

    def arrive(gs, after):
        got = []
        for g in gs:
            got += gather_wait([pending[n] for n in groups[g]], gsem[g], after, "gather_wait_" + g)
        return gather_forward(got, "gather_forward_" + gs[0])

    bin4 = b_in.reshape(N_CHIPS, 1, 2 * d)
    ws_b = sgu_w_s[0].astype(bf16)
    bs_c = sgu_b_s[0][:, :, None]
    cols, mats, cdec, cos, sin = retention_constants(ret_decay_logit[0], t, dk, tok_rest[0, 0])

    wg1, wu1, wd1 = [_pair_shards(w) for w in arrive(["ffn1"], cos)]
    x1, g1, u1 = ffn_fwd(xs, ffn1_norm, wg1, wu1, wd1, "ffn1_fwd")
    win, = arrive(["in"], x1)
    proj, hb2, a = inproj_fwd(x1, mix_norm, win, bin4, cos, sin, sgu_norm_g, sgu_norm_b, ws_b, bs_c)
    late = []
    for g in ("mix", "ffn2"):
        late += gather_wait([pending[n] for n in groups[g]], gsem[g], proj, "gather_wait_" + g)
    fsems, late, ftok = forward_start(late, "forward_start_mix")
    r, rn = ret_fwd(proj, cols, mats, cdec, ftok)
    wa, wb, wo, wg2, wu2, wd2 = forward_wait(late, fsems, rn, "forward_wait_mix")
    wa, wb, wo = [w.reshape(d, d) for w in (wa, wb, wo)]
    wg2, wu2, wd2 = [_pair_shards(w) for w in (wg2, wu2, wd2)]
    x2, ba, br = mix_fwd(a, rn, proj, wa, wb, wo, x1)
    loss_blk, dx3, d_final, g2, u2 = ffn_fwd_loss(x2, ffn2_norm, wg2, wu2, wd2, final_norm.reshape(1, d), tgt, "ffn2_fwd")

    sent, swaps = {}, {}
    out_g, out_d, out_m, out_v = {}, {}, {}, {}

    def reduce_plane(g, after):
        gsems, own, lands, _ = sent[g]
        own, lands = exchange_wait(own, lands, gsems, after, "exchange_wait_" + g)
        plane = [sum_partials(chip, o, l, "sum_" + n) for n, o, l in zip(groups[g], own, lands)]
        swaps[g] = swap_start(plane, "swap_start_" + g)
        return swaps[g][3]

    def update(g, after):
        ssems, plane, lands, _ = swaps[g]
        plane, other = swap_wait(plane, lands, ssems, after, "swap_wait_" + g)
        for n, mine, sib in zip(groups[g], plane, other):
            res = adamw_shard(mine, sib, shards2d[n], _shard2d(mom[n], n), _shard2d(var[n], n), "adamw_" + n)
            out_g[n], out_d[n], out_m[n], out_v[n] = [_unshard(o, n) for o in res]
        return res[0]

    dx2, dg2, du2, act2, hb3, dyb2, d_ffn2n = ffn_bwd_act(dx3, x2, ffn2_norm, g2, u2, wg2, wu2, wd2, "ffn2_bwd_act", tok)
    sent["ffn2"] = exchange_start(ffn_weight_grads(hb3, dyb2, dg2, du2, act2, "ffn2_grad", tok), "exchange_start_ffn2")
    drn, dga, dgb, mixb, dba, dbr, dx2b, dua, dva, d_ws, d_bs, d_sng, d_snb = mix_bwd_act(
        dx2, ba, br, proj, wa, wb, wo, sgu_norm_g, sgu_norm_b, ws_b, bs_c, sent["ffn2"][3])
    tg = min(t, 2048)
    row = pl.BlockSpec((tg, d), lambda s, i: (i, 0))

    def square_grad(xa, ya, name):
        return tn_matmul(xa, [ya], row, [row], 1, d, [d], t, tg, name, tok).reshape(N_CHIPS, d // N_CHIPS, d)

    g_mix = [square_grad(a, dba, "grad_w_branch_a"), square_grad(rn, dbr, "grad_w_branch_b"),
             square_grad(mixb, dx2b, "grad_w_out")]
    dq, dkr, dv, dgr, dlg = ret_bwd(drn, r, proj, cols, mats, cdec, cos, sin)
    segs = [dua, dva, dq, dkr, dv, dgr, dga, dgb]
    dx1, d_bin, d_mixn = inproj_bwd_act(segs, win, x1, mix_norm, dx2)
    g_in = None
    for s in range(N_CHIPS):
        g_in = tn_matmul(hb2, [segs[2 * s], segs[2 * s + 1]], row, [row, row], 1, d, [d, d], t, tg, "grad_w_in_%d" % s,
                         tok, (g_in, s, N_CHIPS))
    groups["mix_in"] = groups["mix"] + groups["in"]
    sent["mix_in"] = exchange_start(g_mix + [g_in], "exchange_start_mix_in")
    grad_x, dg1, du1, act1, hb1, dyb1, d_ffn1n = ffn_bwd_act(dx1, xs, ffn1_norm, g1, u1, wg1, wu1, wd1, "ffn1_bwd_act",
                                                              sent["mix_in"][3])
    dlogit = dlg[:, 0:2, 0].T * jax.nn.sigmoid(-ret_decay_logit[0].astype(f32))
    small_g = dict(ffn1_norm=d_ffn1n, mix_norm=d_mixn, b_in=d_bin, sgu_norm_g=d_sng, sgu_norm_b=d_snb, sgu_w_s=d_ws,
                   sgu_b_s=d_bs, ret_decay_logit=dlogit, ffn2_norm=d_ffn2n, final_norm=d_final)
    shapes = [p[n].shape for n in SMALL] + [(1,)]
    small_sems, small_blk, small_land, small_tok = small_start(
        _pack([small_g[n] for n in SMALL] + [loss_blk[0, 0:1]], shapes))

    def send_one(which, grad):
        n = "ffn1_" + which
        groups[n] = (n,)
        sent[n] = exchange_start([grad], "exchange_start_" + n)
        return sent[n][3]

    ffn_weight_grads(hb1, dyb1, dg1, du1, act1, "ffn1_grad", small_tok, send_one)

    after = reduce_plane("ffn2", sent["ffn1_w_down"][3])
    after = reduce_plane("mix_in", after)
    after = update("ffn2", after)
    g8 = small_wait(small_blk, small_land, small_sems, after)
    no_state = [jnp.zeros((1,), f32)]
    sg, sd, sm, sv = adamw_small(g8, _pack([p[n] for n in SMALL] + no_state, shapes),
                                 _pack([mom[n] for n in SMALL] + no_state, shapes),
                                 _pack([var[n] for n in SMALL] + no_state, shapes))
    for res, blockv in ((out_g, sg), (out_d, sd), (out_m, sm), (out_v, sv)):
        for n, val in zip(SMALL, _unpack(blockv, shapes)):
            res[n] = val
    loss = _unpack(sg, shapes)[-1][0]
    after = update("mix_in", sg)
    after = reduce_plane("ffn1_w_gate", after)
    after = reduce_plane("ffn1_w_up", after)
    after = update("ffn1_w_gate", after)
    after = reduce_plane("ffn1_w_down", after)
    after = update("ffn1_w_up", after)
    update("ffn1_w_down", after)

    return (loss, grad_x[None], *[out_g[n] for n in WEIGHTS], *[out_d[n] for n in WEIGHTS],
            *[out_m[n] for n in WEIGHTS], *[out_v[n] for n in WEIGHTS])
```

```python
import jax
import jax.numpy as jnp
from jax import lax
from jax.experimental import pallas as pl
from jax.experimental.pallas import tpu as pltpu

f32 = jnp.float32
bf16 = jnp.bfloat16

SGU_CHUNK = 128
CHUNK = 256
RET_HEADS = 4
SGU_GROUPS = 4
ROPE_BASE = 10000.0
NORM_EPS = 1e-6
ADAM_LR = 0.001
ADAM_B1 = 0.9
ADAM_B2 = 0.999
ADAM_EPS = 1e-08
ADAM_WD = 0.01
ADAM_STEP = 10
N_CHIPS = 4
N_DEV = 8
MESH = pl.DeviceIdType.MESH
VMEM_LIMIT = 52 * 1024 * 1024
VMEM_LIMIT_WIDE = 62 * 1024 * 1024

_NT = (((1,), (1,)), ((), ()))
_TN = (((0,), (0,)), ((), ()))


def _cparams(limit=None):
    return pltpu.CompilerParams(vmem_limit_bytes=VMEM_LIMIT if limit is None else limit)


def _row_tile(t):
    return 512 if t >= 2048 else t // 2


def _dot(a, b):
    return jnp.dot(a, b, preferred_element_type=f32)


def _dot_nt(a, b):
    return lax.dot_general(a, b, _NT, preferred_element_type=f32)


def _dot_tn(a, b):
    return lax.dot_general(a, b, _TN, preferred_element_type=f32)


def _rms(x, g):
    r = lax.rsqrt(jnp.mean(x * x, axis=-1, keepdims=True) + NORM_EPS)
    xh = x * r
    return xh * g, xh, r


def _rms_bwd(dy, xh, r, g):
    dxh = dy * g
    return r * (dxh - xh * jnp.mean(dxh * xh, axis=-1, keepdims=True))


def _sigmoid(x):
    return jax.nn.sigmoid(x)


def _dsilu(g, sg):
    return sg * (1.0 + g * (1.0 - sg))


def _gelu(x):
    return 0.5 * x * (1.0 + lax.erf(x * 0.7071067811865476))


def _dgelu(x):
    return 0.5 * (1.0 + lax.erf(x * 0.7071067811865476)) + x * jnp.exp(-0.5 * x * x) * 0.3989422804014327


def _zero_at_first_step(*refs):
    @pl.when(pl.program_id(0) == 0)
    def _():
        for ref in refs:
            ref[...] = jnp.zeros_like(ref)


def _ffn_tile(t):
    return 256 if t >= 2048 else t // 2


def _ffn_fwd_rows(xx, ng_ref, wg_ref, wu_ref, wd_ref, g_ref, u_ref):
    y, _, _ = _rms(xx, ng_ref[...])
    h = y.astype(bf16)
    acc = None
    for s in range(wg_ref.shape[0]):
        g = _dot_nt(h, wg_ref[s])
        u = _dot_nt(h, wu_ref[s])
        g_ref[s] = g.astype(bf16)
        u_ref[s] = u.astype(bf16)
        part = _dot((g * _sigmoid(g) * u).astype(bf16), wd_ref[s])
        acc = part if acc is None else acc + part
    return xx + 0.5 * acc


def ffn_fwd(x, ng, wg, wu, wd, name):
    t, d = x.shape
    ns, fs, _ = wg.shape
    tm = _ffn_tile(t)

    def body(x_ref, ng_ref, wg_ref, wu_ref, wd_ref, xo_ref, g_ref, u_ref):
        xo_ref[...] = _ffn_fwd_rows(x_ref[...], ng_ref, wg_ref, wu_ref, wd_ref, g_ref, u_ref)

    row = pl.BlockSpec((tm, d), lambda i: (i, 0))
    shard = pl.BlockSpec((ns, tm, fs), lambda i: (0, i, 0))
    wspec = pl.BlockSpec((ns, fs, d), lambda i: (0, 0, 0), pipeline_mode=pl.Buffered(1))
    return pl.pallas_call(
        body, name=name, grid=(t // tm,),
        in_specs=[row, pl.BlockSpec((1, d), lambda i: (0, 0)), wspec, wspec, wspec],
        out_specs=[row, shard, shard],
        out_shape=[jax.ShapeDtypeStruct((t, d), f32), jax.ShapeDtypeStruct((ns, t, fs), bf16),
                   jax.ShapeDtypeStruct((ns, t, fs), bf16)],
        compiler_params=_cparams(),
    )(x, ng, wg, wu, wd)


def ffn_fwd_loss(x, ng, wg, wu, wd, fng, tgt, name):
    t, d = x.shape
    ns, fs, _ = wg.shape
    tm = _ffn_tile(t)

    def body(x_ref, ng_ref, wg_ref, wu_ref, wd_ref, fng_ref, t_ref, loss_ref, dx_ref, dfn_ref, g_ref, u_ref):
        _zero_at_first_step(loss_ref, dfn_ref)
        x3 = _ffn_fwd_rows(x_ref[...], ng_ref, wg_ref, wu_ref, wd_ref, g_ref, u_ref)
        y, xh, r = _rms(x3, fng_ref[...])
        diff = y - t_ref[...]
        part = 0.5 * jnp.sum(jnp.sum(diff * diff, axis=0, keepdims=True), axis=1, keepdims=True) / d
        loss_ref[...] += jnp.broadcast_to(part, (1, 128))
        dy = diff * (1.0 / d)
        dx_ref[...] = _rms_bwd(dy, xh, r, fng_ref[...])
        dfn_ref[...] += jnp.sum(dy * xh, axis=0, keepdims=True)

    row = pl.BlockSpec((tm, d), lambda i: (i, 0))
    vec = pl.BlockSpec((1, d), lambda i: (0, 0))
    shard = pl.BlockSpec((ns, tm, fs), lambda i: (0, i, 0))
    wspec = pl.BlockSpec((ns, fs, d), lambda i: (0, 0, 0), pipeline_mode=pl.Buffered(1))
    return pl.pallas_call(
        body, name=name, grid=(t // tm,),
        in_specs=[row, vec, wspec, wspec, wspec, vec, row],
        out_specs=[pl.BlockSpec((1, 128), lambda i: (0, 0)), row, vec, shard, shard],
        out_shape=[jax.ShapeDtypeStruct((1, 128), f32), jax.ShapeDtypeStruct((t, d), f32), jax.ShapeDtypeStruct((1, d), f32),
                   jax.ShapeDtypeStruct((ns, t, fs), bf16), jax.ShapeDtypeStruct((ns, t, fs), bf16)],
        compiler_params=_cparams(),
    )(x, ng, wg, wu, wd, fng, tgt)


def ffn_bwd_act(dxo, x, ng, g, u, wg, wu, wd, name, dep):
    t, d = x.shape
    ns, fs, _ = wg.shape
    tm = _ffn_tile(t)

    def body(dxo_ref, x_ref, ng_ref, g_ref, u_ref, wg_ref, wu_ref, wd_ref, dep_ref,
             dx_ref, dg_ref, du_ref, act_ref, hb_ref, dyb_ref, dng_ref):
        _zero_at_first_step(dng_ref)
        dxo = dxo_ref[...]
        dyb = (0.5 * dxo).astype(bf16)
        dyb_ref[...] = dyb
        dh = None
        for s in range(ns):
            dact = _dot_nt(dyb, wd_ref[s])
            gg = g_ref[s].astype(f32)
            uu = u_ref[s].astype(f32)
            sg = _sigmoid(gg)
            sil = gg * sg
            dgb = (dact * uu * _dsilu(gg, sg)).astype(bf16)
            dub = (dact * sil).astype(bf16)
            dg_ref[s] = dgb
            du_ref[s] = dub
            act_ref[s] = (sil * uu).astype(bf16)
            part = _dot(dgb, wg_ref[s]) + _dot(dub, wu_ref[s])
            dh = part if dh is None else dh + part
        y, xh, r = _rms(x_ref[...], ng_ref[...])
        hb_ref[...] = y.astype(bf16)
        dx_ref[...] = dxo + _rms_bwd(dh, xh, r, ng_ref[...])
        dng_ref[...] += jnp.sum(dh * xh, axis=0, keepdims=True)

    row = pl.BlockSpec((tm, d), lambda i: (i, 0))
    shard = pl.BlockSpec((ns, tm, fs), lambda i: (0, i, 0))
    wspec = pl.BlockSpec((ns, fs, d), lambda i: (0, 0, 0), pipeline_mode=pl.Buffered(1))
    vec = pl.BlockSpec((1, d), lambda i: (0, 0))
    return pl.pallas_call(
        body, name=name, grid=(t // tm,),
        in_specs=[row, row, vec, shard, shard, wspec, wspec, wspec, _ANY],
        out_specs=[row, shard, shard, shard, row, row, vec],
        out_shape=[jax.ShapeDtypeStruct((t, d), f32)] + [jax.ShapeDtypeStruct((ns, t, fs), bf16)] * 3
        + [jax.ShapeDtypeStruct((t, d), bf16)] * 2 + [jax.ShapeDtypeStruct((1, d), f32)],
        compiler_params=_cparams(VMEM_LIMIT_WIDE),
    )(dxo, x, ng, g, u, wg, wu, wd, dep)


def tn_matmul(xs, ys, x_spec, y_specs, n_shards, k1, k2s, t, tm, name, dep, into=None):
    k2 = sum(k2s)
    ny = len(ys)

    def body(*refs):
        x_ref = refs[0]
        y_refs = refs[1:1 + ny]
        steps = t // tm
        o_ref, acc = (refs[-1], None) if steps == 1 else (refs[-2], refs[-1])
        i = pl.program_id(1)
        xb = x_ref[0] if len(x_ref.shape) == 3 else x_ref[...]
        if steps > 1:
            @pl.when(i == 0)
            def _():
                acc[...] = jnp.zeros_like(acc)

        off = 0
        for y_ref, w in zip(y_refs, k2s):
            yb = y_ref[0] if len(y_ref.shape) == 3 else y_ref[...]
            part = _dot_tn(xb, yb)
            if steps == 1:
                o_ref[0, :, off:off + w] = part.astype(bf16)
            else:
                acc[:, off:off + w] += part
            off += w

        if steps > 1:
            @pl.when(i == steps - 1)
            def _():
                o_ref[0] = acc[...].astype(bf16)

    if into is None:
        slot0, total, extra, aliases = 0, n_shards, [], {}
    else:
        buf, slot0, total = into
        extra = [] if buf is None else [buf]
        aliases = {} if buf is None else {2 + ny: 0}
    return pl.pallas_call(
        body, name=name, grid=(n_shards, t // tm),
        in_specs=[x_spec] + list(y_specs) + [_ANY] * (1 + len(extra)),
        out_specs=pl.BlockSpec((1, k1, k2), lambda s, i: (slot0 + s, 0, 0)),
        out_shape=jax.ShapeDtypeStruct((total, k1, k2), bf16),
        scratch_shapes=[pltpu.VMEM((k1, k2), f32)] if t // tm > 1 else [],
        input_output_aliases=aliases,
        compiler_params=_cparams(),
    )(xs, *ys, dep, *extra)


def _pair_shards(w):
    s4, fs, d = w.shape
    return w.reshape(s4 // 2, 2 * fs, d)


def ffn_weight_grads(hb, dyb, dg, du, act, name, dep, each=None):
    t, d = hb.shape
    s2, _, fs2 = dg.shape
    tm = t
    row = pl.BlockSpec((tm, d), lambda s, i: (i, 0))
    shard = pl.BlockSpec((1, tm, fs2), lambda s, i: (s, i, 0))
    grads = []
    for xa, ya, which in ((dg, hb, "w_gate"), (du, hb, "w_up"), (act, dyb, "w_down")):
        g = tn_matmul(xa, [ya], shard, [row], s2, fs2, [d], t, tm, name + "_" + which, dep)
        g = g.reshape(2 * s2, fs2 // 2, d)
        if each is not None:
            dep = each(which, g)
        grads.append(g)
    return grads


def ffn_weight_grads_one_call(hb, dyb, dg, du, act, name, dep):
    t, d = hb.shape
    ns, _, fs = dg.shape
    steps = [(m, s) for m in range(3) for s in range(ns)]

    def body(hb_hbm, dyb_hbm, dg_hbm, du_hbm, act_hbm, dep_ref, gwg_hbm, gwu_hbm, gwd_hbm,
             y_buf, x_buf, o_buf, y_sems, x_sems, o_sems):
        srcs = (dg_hbm, du_hbm, act_hbm)
        dsts = (gwg_hbm, gwu_hbm, gwd_hbm)
        y_copies = [pltpu.make_async_copy(hb_hbm, y_buf.at[0], y_sems.at[0]),
                    pltpu.make_async_copy(dyb_hbm, y_buf.at[1], y_sems.at[1])]

        def x_copy(j):
            m, s = steps[j]
            return pltpu.make_async_copy(srcs[m].at[s], x_buf.at[j % 2], x_sems.at[j % 2])

        x_copy(0).start()
        y_copies[0].start()
        y_copies[1].start()
        out_copies = [None, None]
        for j, (m, s) in enumerate(steps):
            if j + 1 < len(steps):
                x_copy(j + 1).start()
            x_copy(j).wait()
            if j == 0:
                y_copies[0].wait()
            if (m, s) == (2, 0):
                y_copies[1].wait()
            if out_copies[j % 2] is not None:
                out_copies[j % 2].wait()
            o_buf[j % 2] = _dot_tn(x_buf[j % 2], y_buf[1 if m == 2 else 0]).astype(bf16)
            out_copies[j % 2] = pltpu.make_async_copy(o_buf.at[j % 2], dsts[m].at[s], o_sems.at[j % 2])
            out_copies[j % 2].start()
        for cp in out_copies:
            cp.wait()

    outs = pl.pallas_call(
        body, name=name,
        in_specs=[_ANY] * 6, out_specs=[_ANY] * 3,
        out_shape=[jax.ShapeDtypeStruct((ns, fs, d), bf16)] * 3,
        scratch_shapes=[pltpu.VMEM((2, t, d), bf16), pltpu.VMEM((2, t, fs), bf16), pltpu.VMEM((2, fs, d), bf16),
                        pltpu.SemaphoreType.DMA((2,)), pltpu.SemaphoreType.DMA((2,)), pltpu.SemaphoreType.DMA((2,))],
        compiler_params=_cparams(VMEM_LIMIT_WIDE),
    )(hb, dyb, dg, du, act, dep)
    return [g.reshape(2 * ns, fs // 2, d) for g in outs]


def inproj_fwd(x1, ng, win, bin4, cos, sin, sng, snb, ws, bs):
    t, d = x1.shape
    s4, _, w2 = win.shape
    tm = _row_tile(t)
    dk = d // RET_HEADS
    scale = dk ** -0.5

    def body(x_ref, ng_ref, w_ref, b_ref, cos_ref, sin_ref, sng_ref, snb_ref, ws_ref, bs_ref, p_ref, hb_ref, a_ref):
        y, _, _ = _rms(x_ref[...], ng_ref[...])
        h = y.astype(bf16)
        hb_ref[...] = h
        uv = None
        for s in range(s4):
            p = _dot(h, w_ref[s]) + b_ref[s]
            if s == 0:
                uv = p.astype(bf16)
                p_ref[s] = uv
            elif s != 1:
                p_ref[s] = p.astype(bf16)
            else:
                cs, sn = cos_ref[...], sin_ref[...]
                for e in range(2 * RET_HEADS):
                    cols = slice(e * dk, (e + 1) * dk)
                    rot = _rot(p[:, cols], cs, sn)
                    p_ref[s, :, cols] = (rot if e < RET_HEADS else rot * scale).astype(bf16)
        _sgu_rows(uv[:, 0:d].astype(f32), uv[:, d:w2].astype(f32), sng_ref, snb_ref, ws_ref, bs_ref, a_ref)

    tab = pl.BlockSpec((tm, dk // 2), lambda i: (i, 0))
    row = pl.BlockSpec((tm, d), lambda i: (i, 0))
    vec = pl.BlockSpec((1, d), lambda i: (0, 0))
    return pl.pallas_call(
        body, name="inproj_fwd", grid=(t // tm,),
        in_specs=[row, vec, pl.BlockSpec((s4, d, w2), lambda i: (0, 0, 0), pipeline_mode=pl.Buffered(1)),
                  pl.BlockSpec((s4, 1, w2), lambda i: (0, 0, 0)), tab, tab, vec, vec,
                  pl.BlockSpec((SGU_GROUPS, SGU_CHUNK, SGU_CHUNK), lambda i: (0, 0, 0)),
                  pl.BlockSpec((SGU_GROUPS, SGU_CHUNK, 1), lambda i: (0, 0, 0))],
        out_specs=[pl.BlockSpec((s4, tm, w2), lambda i: (0, i, 0)), row, row],
        out_shape=[jax.ShapeDtypeStruct((s4, t, w2), bf16), jax.ShapeDtypeStruct((t, d), bf16),
                   jax.ShapeDtypeStruct((t, d), bf16)],
        compiler_params=_cparams(),
    )(x1, ng, win, bin4, cos, sin, sng, snb, ws, bs)


def _sgu_norm(va, ng, nb):
    gv = _gelu(va)
    mu = jnp.mean(gv, axis=-1, keepdims=True)
    xc = gv - mu
    rstd = lax.rsqrt(jnp.mean(xc * xc, axis=-1, keepdims=True) + NORM_EPS)
    xh = xc * rstd
    return xh, rstd, (xh * ng + nb).astype(bf16)


def _sgu_rows(ua, va, ng_ref, nb_ref, ws_ref, bs_ref, a_ref):
    tm, d = ua.shape
    gd = d // SGU_GROUPS
    gu = _gelu(ua)
    _, _, vn = _sgu_norm(va, ng_ref[...], nb_ref[...])
    for c in range(tm // SGU_CHUNK):
        rows = slice(c * SGU_CHUNK, (c + 1) * SGU_CHUNK)
        for g in range(SGU_GROUPS):
            cols = slice(g * gd, (g + 1) * gd)
            sg = _dot(ws_ref[g], vn[rows, cols]) + bs_ref[g]
            a_ref[rows, cols] = (gu[rows, cols] * sg).astype(bf16)


def _sgu_bwd_rows(dad, ua, va, ng_ref, nb_ref, ws_ref, bs_ref, dua_ref, dva_ref, dws_ref, dbs_ref, dng_ref, dnb_ref,
                  dvn_scr):
    tm, d = ua.shape
    gd = d // SGU_GROUPS
    gu = _gelu(ua)
    xh, rstd, vn = _sgu_norm(va, ng_ref[...], nb_ref[...])
    dsb = (dad * gu).astype(bf16)
    for c in range(tm // SGU_CHUNK):
        rows = slice(c * SGU_CHUNK, (c + 1) * SGU_CHUNK)
        for g in range(SGU_GROUPS):
            cols = slice(g * gd, (g + 1) * gd)
            sg = _dot(ws_ref[g], vn[rows, cols]) + bs_ref[g]
            dua_ref[rows, cols] = (dad[rows, cols] * sg * _dgelu(ua[rows, cols])).astype(bf16)
            ds = dsb[rows, cols]
            dvn_scr[rows, cols] = _dot_tn(ws_ref[g], ds)
            dws_ref[g] += _dot_nt(ds, vn[rows, cols])
            dbs_ref[g] += jnp.sum(ds.astype(f32), axis=1, keepdims=True)
    dvn = dvn_scr[...]
    dng_ref[...] += jnp.sum(dvn * xh, axis=0, keepdims=True)
    dnb_ref[...] += jnp.sum(dvn, axis=0, keepdims=True)
    dxh = dvn * ng_ref[...]
    dgv = rstd * (dxh - jnp.mean(dxh, axis=-1, keepdims=True) - xh * jnp.mean(dxh * xh, axis=-1, keepdims=True))
    dva_ref[...] = (dgv * _dgelu(va)).astype(bf16)


def retention_constants(decay_logit, t, dk, zero):
    lg = jax.nn.log_sigmoid(decay_logit.astype(f32) + zero)
    lgf = lg[0][:, None]
    lgb = lg[1][:, None]
    idx = jnp.arange(CHUNK, dtype=f32)[None, :]
    af = jnp.exp((idx + 1.0) * lgf)
    ab = jnp.exp((CHUNK - idx) * lgb)
    kf = jnp.exp((CHUNK - 1.0 - idx) * lgf)
    kb = jnp.exp(idx * lgb)
    cols = jnp.stack([af, ab, kf, kb, af * (idx + 1.0), ab * (CHUNK - idx), kf * (CHUNK - 1.0 - idx), kb * idx], axis=1)
    cols = cols[..., None]
    diff = idx[0][:, None] - idx[0][None, :]
    dfm = jnp.where(diff >= 0, jnp.exp(jnp.maximum(diff, 0.0)[None] * lgf[:, :, None]), 0.0)
    dbm = jnp.where(diff < 0, jnp.exp(jnp.maximum(-diff, 0.0)[None] * lgb[:, :, None]), 0.0)
    mats = jnp.stack([dfm + dbm, dfm * diff[None], dbm * (-diff)[None]], axis=1)
    cdec = jnp.stack([jnp.broadcast_to(jnp.exp(CHUNK * lgf), (RET_HEADS, dk)),
                      jnp.broadcast_to(jnp.exp(CHUNK * lgb), (RET_HEADS, dk))], axis=1)
    theta = ROPE_BASE ** (-jnp.arange(0, dk, 2, dtype=f32) / dk)
    ang = (jnp.arange(t, dtype=f32) + zero)[:, None] * theta[None, :]
    return cols, mats, cdec, jnp.cos(ang), jnp.sin(ang)


def _rot(tr, cos, sin):
    half = tr.shape[-1] // 2
    t1 = tr[:, :half]
    t2 = tr[:, half:]
    return jnp.concatenate([t1 * cos - t2 * sin, t2 * cos + t1 * sin], axis=-1)


def _rot_inv(dt, cos, sin):
    half = dt.shape[-1] // 2
    d1 = dt[:, :half]
    d2 = dt[:, half:]
    return jnp.concatenate([d1 * cos + d2 * sin, d2 * cos - d1 * sin], axis=-1)


def _ret_tile(t):
    return 2048 if t >= 4096 else _row_tile(t)


def _ret_specs(t, d, dk, rt):
    nr = t // rt
    hq = d // dk

    def blk(p, n):
        return (1 - p) * (nr - 1 - n) + p * n

    q_spec = pl.BlockSpec((1, rt, dk), lambda h, p, n: (1, blk(p, n), h))
    k_spec = pl.BlockSpec((1, rt, dk), lambda h, p, n: (1, blk(p, n), hq + h))
    v_spec = pl.BlockSpec((1, rt, dk), lambda h, p, n: (2, blk(p, n), h))
    g_spec = pl.BlockSpec((1, rt, dk), lambda h, p, n: (2, blk(p, n), hq + h))
    tab_spec = pl.BlockSpec((rt, dk // 2), lambda h, p, n: (blk(p, n), 0))
    cols_spec = pl.BlockSpec((1, 8, CHUNK, 1), lambda h, p, n: (h, 0, 0, 0))
    mats_spec = pl.BlockSpec((1, 3, CHUNK, CHUNK), lambda h, p, n: (h, 0, 0, 0))
    cdec_spec = pl.BlockSpec((1, 2, dk), lambda h, p, n: (h, 0, 0))
    in_row = pl.BlockSpec((rt, dk), lambda h, p, n: (blk(p, n), h))
    out_row = pl.BlockSpec((rt, dk), lambda h, p, n: (p * n, h))
    return nr, blk, q_spec, k_spec, v_spec, g_spec, tab_spec, cols_spec, mats_spec, cdec_spec, in_row, out_row


def ret_fwd(proj, cols, mats, cdec, dep):
    _, t, w2 = proj.shape
    d = w2 // 2
    dk = d // RET_HEADS
    rt = _ret_tile(t)
    cpt = rt // CHUNK
    nr, blk, q_spec, k_spec, v_spec, g_spec, _, cols_spec, mats_spec, cdec_spec, _, out_row = _ret_specs(t, d, dk, rt)

    def body(q_ref, k_ref, v_ref, g_ref, cols_ref, mats_ref, cdec_ref, dep_ref, r_ref, rn_ref, sb_scr, st):
        p = pl.program_id(1)
        n = pl.program_id(2)
        af, ab, kf, kb = cols_ref[0, 0], cols_ref[0, 1], cols_ref[0, 2], cols_ref[0, 3]
        cf = cdec_ref[0, 0:1, :]
        cb = cdec_ref[0, 1:2, :]

        @pl.when(n == 0)
        def _():
            st[...] = jnp.zeros_like(st)

        @pl.when(p == 0)
        def _():
            for j in reversed(range(cpt)):
                rows = slice(j * CHUNK, (j + 1) * CHUNK)
                ch = blk(p, n) * cpt + j
                kk = k_ref[0, rows, :].astype(f32)
                sb_scr[ch] = st[...].astype(bf16)
                st[...] = st[...] * cb + _dot_tn((kk * kb).astype(bf16), v_ref[0, rows, :])

        @pl.when(p == 1)
        def _():
            for j in range(cpt):
                rows = slice(j * CHUNK, (j + 1) * CHUNK)
                ch = blk(p, n) * cpt + j
                qb = q_ref[0, rows, :]
                kkb = k_ref[0, rows, :]
                q = qb.astype(f32)
                kk = kkb.astype(f32)
                v = v_ref[0, rows, :]
                pm = (_dot_nt(qb, kkb) * mats_ref[0, 0]).astype(bf16)
                out = (_dot(pm, v) + _dot((q * af).astype(bf16), st[...].astype(bf16))
                       + _dot((q * ab).astype(bf16), sb_scr[ch]))
                st[...] = st[...] * cf + _dot_tn((kk * kf).astype(bf16), v)
                rhat = out * lax.rsqrt(jnp.mean(out * out, axis=-1, keepdims=True) + NORM_EPS)
                gg = g_ref[0, rows, :].astype(f32)
                r_ref[rows, :] = out.astype(bf16)
                rn_ref[rows, :] = (rhat * gg * _sigmoid(gg)).astype(bf16)

    return pl.pallas_call(
        body, name="ret_fwd", grid=(RET_HEADS, 2, nr),
        in_specs=[q_spec, k_spec, v_spec, g_spec, cols_spec, mats_spec, cdec_spec, _ANY],
        out_specs=[out_row, out_row],
        out_shape=[jax.ShapeDtypeStruct((t, d), bf16), jax.ShapeDtypeStruct((t, d), bf16)],
        scratch_shapes=[pltpu.VMEM((t // CHUNK, dk, dk), bf16), pltpu.VMEM((dk, dk), f32)],
        compiler_params=_cparams(),
    )(proj, proj, proj, proj, cols, mats, cdec, dep)


def ret_bwd(drn, r, proj, cols, mats, cdec, cos, sin):
    _, t, w2 = proj.shape
    d = w2 // 2
    dk = d // RET_HEADS
    rt = _ret_tile(t)
    cpt = rt // CHUNK
    nr, blk, q_spec, k_spec, v_spec, g_spec, tab_spec, cols_spec, mats_spec, cdec_spec, in_row, out_row = _ret_specs(t, d, dk, rt)
    scale = dk ** -0.5

    def body(drn_ref, r_ref, q_ref, k_ref, v_ref, g_ref, cos_ref, sin_ref, cols_ref, mats_ref, cdec_ref,
             dq_ref, dk_ref, dv_ref, dg_ref, dlg_ref,
             sb_scr, gf_scr, st_s, st_g, acc_af, acc_ab, acc_vf, acc_vb, acc_sf, acc_sb, dout_scr, dgr_scr):
        p = pl.program_id(1)
        n = pl.program_id(2)
        af, ab, kf, kb = cols_ref[0, 0], cols_ref[0, 1], cols_ref[0, 2], cols_ref[0, 3]
        af1, ab1, kf1, kb1 = cols_ref[0, 4], cols_ref[0, 5], cols_ref[0, 6], cols_ref[0, 7]
        cf = cdec_ref[0, 0:1, :]
        cb = cdec_ref[0, 1:2, :]

        @pl.when(n == 0)
        def _():
            st_s[...] = jnp.zeros_like(st_s)
            st_g[...] = jnp.zeros_like(st_g)

        @pl.when(jnp.logical_and(n == 0, p == 1))
        def _():
            for a in (acc_af, acc_ab, acc_vf, acc_vb, acc_sf, acc_sb):
                a[...] = jnp.zeros_like(a)

        def load(rows):
            cs, sn = cos_ref[rows, :], sin_ref[rows, :]
            q = q_ref[0, rows, :].astype(f32)
            kk = k_ref[0, rows, :].astype(f32)
            rr = r_ref[rows, :].astype(f32)
            rstd = lax.rsqrt(jnp.mean(rr * rr, axis=-1, keepdims=True) + NORM_EPS)
            rhat = rr * rstd
            gg = g_ref[0, rows, :].astype(f32)
            sg = _sigmoid(gg)
            dd = drn_ref[rows, :].astype(f32)
            drhat = dd * gg * sg
            dout = rstd * (drhat - rhat * jnp.mean(drhat * rhat, axis=-1, keepdims=True))
            dgr = dd * rhat * _dsilu(gg, sg)
            return q, kk, dout.astype(bf16), dgr, cs, sn

        @pl.when(p == 0)
        def _():
            for j in reversed(range(cpt)):
                rows = slice(j * CHUNK, (j + 1) * CHUNK)
                ch = blk(p, n) * cpt + j
                q, kk, doutb, dgr, _, _ = load(rows)
                kept = pl.ds(pl.multiple_of(ch * CHUNK, CHUNK), CHUNK)
                dout_scr[kept, :] = doutb
                dgr_scr[kept, :] = dgr.astype(bf16)
                sb_scr[ch] = st_s[...].astype(bf16)
                gf_scr[ch] = st_g[...].astype(bf16)
                st_s[...] = st_s[...] * cb + _dot_tn((kk * kb).astype(bf16), v_ref[0, rows, :])
                st_g[...] = st_g[...] * cf + _dot_tn((q * af).astype(bf16), doutb)

        @pl.when(p == 1)
        def _():
            for j in range(cpt):
                rows = slice(j * CHUNK, (j + 1) * CHUNK)
                ch = blk(p, n) * cpt + j
                kept = pl.ds(pl.multiple_of(ch * CHUNK, CHUNK), CHUNK)
                doutb = dout_scr[kept, :]
                cs, sn = cos_ref[rows, :], sin_ref[rows, :]
                v = v_ref[0, rows, :]
                qb = q_ref[0, rows, :]
                kkb = k_ref[0, rows, :]
                q = qb.astype(f32)
                kk = kkb.astype(f32)
                sf = st_s[...]
                gb = st_g[...]
                sfb = sf.astype(bf16)
                gbb = gb.astype(bf16)
                sbb = sb_scr[ch]
                gfb = gf_scr[ch]
                dmat = mats_ref[0, 0]
                scores = _dot_nt(qb, kkb)
                dpraw = _dot_nt(doutb, v)
                dpb = (dpraw * dmat).astype(bf16)
                pmb = (scores * dmat).astype(bf16)
                x1 = _dot_nt(doutb, sfb)
                x2 = _dot_nt(doutb, sbb)
                y1 = _dot_nt(v, gfb)
                y2 = _dot_nt(v, gbb)
                kdf = (kk * kf).astype(bf16)
                kdb = (kk * kb).astype(bf16)
                dq = _dot(dpb, kkb) + x1 * af + x2 * ab
                dkk = _dot_tn(dpb, qb) + y1 * kf + y2 * kb
                dv = _dot_tn(pmb, doutb) + _dot(kdf, gfb) + _dot(kdb, gbb)
                ps = dpraw * scores
                acc_af[...] += ps * mats_ref[0, 1]
                acc_ab[...] += ps * mats_ref[0, 2]
                acc_vf[...] += x1 * q * af1 + y1 * kk * kf1
                acc_vb[...] += x2 * q * ab1 + y2 * kk * kb1
                acc_sf[...] += gfb.astype(f32) * sf
                acc_sb[...] += gb * sbb.astype(f32)
                st_s[...] = sf * cf + _dot_tn(kdf, v)
                st_g[...] = gb * cb + _dot_tn((q * ab).astype(bf16), doutb)
                dq_ref[rows, :] = _rot_inv(dq, cs, sn).astype(bf16)
                dk_ref[rows, :] = (_rot_inv(dkk, cs, sn) * scale).astype(bf16)
                dv_ref[rows, :] = dv.astype(bf16)
                dg_ref[rows, :] = dgr_scr[kept, :]

        @pl.when(jnp.logical_and(p == 1, n == nr - 1))
        def _():
            tf = jnp.sum(acc_af[...]) + jnp.sum(acc_vf[...]) + CHUNK * jnp.sum(acc_sf[...] * cf)
            tb = jnp.sum(acc_ab[...]) + jnp.sum(acc_vb[...]) + CHUNK * jnp.sum(acc_sb[...] * cb)
            rid = lax.broadcasted_iota(jnp.int32, (8, 128), 0)
            dlg_ref[0] = jnp.where(rid == 0, tf, jnp.where(rid == 1, tb, 0.0))

    nch = t // CHUNK
    return pl.pallas_call(
        body, name="ret_bwd", grid=(RET_HEADS, 2, nr),
        in_specs=[in_row, in_row, q_spec, k_spec, v_spec, g_spec, tab_spec, tab_spec, cols_spec, mats_spec, cdec_spec],
        out_specs=[out_row, out_row, out_row, out_row, pl.BlockSpec((1, 8, 128), lambda h, p, n: (h, 0, 0))],
        out_shape=[jax.ShapeDtypeStruct((t, d), bf16)] * 4 + [jax.ShapeDtypeStruct((RET_HEADS, 8, 128), f32)],
        scratch_shapes=[pltpu.VMEM((nch, dk, dk), bf16), pltpu.VMEM((nch, dk, dk), bf16),
                        pltpu.VMEM((dk, dk), f32), pltpu.VMEM((dk, dk), f32),
                        pltpu.VMEM((CHUNK, CHUNK), f32), pltpu.VMEM((CHUNK, CHUNK), f32),
                        pltpu.VMEM((CHUNK, dk), f32), pltpu.VMEM((CHUNK, dk), f32),
                        pltpu.VMEM((dk, dk), f32), pltpu.VMEM((dk, dk), f32),
                        pltpu.VMEM((t, dk), bf16), pltpu.VMEM((t, dk), bf16)],
        compiler_params=_cparams(VMEM_LIMIT_WIDE),
    )(drn, r, proj, proj, proj, proj, cos, sin, cols, mats, cdec)


def mix_fwd(a, rn, proj, wa, wb, wo, x1):
    t, d = x1.shape
    tm = _row_tile(t)

    def body(a_ref, rn_ref, p_ref, wa_ref, wb_ref, wo_ref, x_ref, xo_ref, ba_ref, br_ref):
        ba = _dot(a_ref[...], wa_ref[...])
        br = _dot(rn_ref[...], wb_ref[...])
        sa = _sigmoid(p_ref[0, :, 0:d].astype(f32))
        sb = _sigmoid(p_ref[0, :, d:2 * d].astype(f32))
        mix = (sa * ba + sb * br).astype(bf16)
        xo_ref[...] = x_ref[...] + _dot(mix, wo_ref[...])
        ba_ref[...] = ba.astype(bf16)
        br_ref[...] = br.astype(bf16)

    row = pl.BlockSpec((tm, d), lambda i: (i, 0))
    wsp = pl.BlockSpec((d, d), lambda i: (0, 0))
    return pl.pallas_call(
        body, name="mix_fwd", grid=(t // tm,),
        in_specs=[row, row, pl.BlockSpec((1, tm, 2 * d), lambda i: (3, i, 0)), wsp, wsp, wsp, row],
        out_specs=[row, row, row],
        out_shape=[jax.ShapeDtypeStruct((t, d), f32), jax.ShapeDtypeStruct((t, d), bf16), jax.ShapeDtypeStruct((t, d), bf16)],
        compiler_params=_cparams(),
    )(a, rn, proj, wa, wb, wo, x1)


def mix_bwd_act(dx2, ba, br, proj, wa, wb, wo, sng, snb, ws, bs, dep):
    t, d = dx2.shape
    tm = _row_tile(t)

    def body(dx_ref, ba_ref, br_ref, p_ref, uv_ref, wa_ref, wb_ref, wo_ref, sng_ref, snb_ref, ws_ref, bs_ref, dep_ref,
             drn_ref, dga_ref, dgb_ref, mix_ref, dba_ref, dbr_ref, dxb_ref,
             dua_ref, dva_ref, dws_ref, dbs_ref, dng_ref, dnb_ref, dvn_scr):
        _zero_at_first_step(dws_ref, dbs_ref, dng_ref, dnb_ref)
        dxb = dx_ref[...].astype(bf16)
        dxb_ref[...] = dxb
        dmix = _dot_nt(dxb, wo_ref[...])
        ba = ba_ref[...].astype(f32)
        br = br_ref[...].astype(f32)
        sa = _sigmoid(p_ref[0, :, 0:d].astype(f32))
        sb = _sigmoid(p_ref[0, :, d:2 * d].astype(f32))
        mix_ref[...] = (sa * ba + sb * br).astype(bf16)
        dba = (dmix * sa).astype(bf16)
        dbr = (dmix * sb).astype(bf16)
        dba_ref[...] = dba
        dbr_ref[...] = dbr
        dga_ref[...] = (dmix * ba * sa * (1.0 - sa)).astype(bf16)
        dgb_ref[...] = (dmix * br * sb * (1.0 - sb)).astype(bf16)
        drn_ref[...] = _dot_nt(dbr, wb_ref[...]).astype(bf16)
        da = _dot_nt(dba, wa_ref[...])
        _sgu_bwd_rows(da, uv_ref[0, :, 0:d].astype(f32), uv_ref[0, :, d:2 * d].astype(f32), sng_ref, snb_ref, ws_ref,
                      bs_ref, dua_ref, dva_ref, dws_ref, dbs_ref, dng_ref, dnb_ref, dvn_scr)

    row = pl.BlockSpec((tm, d), lambda i: (i, 0))
    vec = pl.BlockSpec((1, d), lambda i: (0, 0))
    wsp = pl.BlockSpec((d, d), lambda i: (0, 0))
    sws = pl.BlockSpec((SGU_GROUPS, SGU_CHUNK, SGU_CHUNK), lambda i: (0, 0, 0))
    sbs = pl.BlockSpec((SGU_GROUPS, SGU_CHUNK, 1), lambda i: (0, 0, 0))
    return pl.pallas_call(
        body, name="mix_bwd_act", grid=(t // tm,),
        in_specs=[row, row, row, pl.BlockSpec((1, tm, 2 * d), lambda i: (3, i, 0)),
                  pl.BlockSpec((1, tm, 2 * d), lambda i: (0, i, 0)), wsp, wsp, wsp, vec, vec, sws, sbs, _ANY],
        out_specs=[row] * 9 + [sws, sbs, vec, vec],
        out_shape=[jax.ShapeDtypeStruct((t, d), bf16)] * 9
        + [jax.ShapeDtypeStruct((SGU_GROUPS, SGU_CHUNK, SGU_CHUNK), f32), jax.ShapeDtypeStruct((SGU_GROUPS, SGU_CHUNK, 1), f32),
           jax.ShapeDtypeStruct((1, d), f32), jax.ShapeDtypeStruct((1, d), f32)],
        scratch_shapes=[pltpu.VMEM((tm, d), f32)],
        compiler_params=_cparams(VMEM_LIMIT_WIDE),
    )(dx2, ba, br, proj, proj, wa, wb, wo, sng, snb, ws, bs, dep)


def inproj_bwd_act(segs, win, x1, ng, dx2):
    t, d = x1.shape
    s4 = win.shape[0]
    tm = _row_tile(t)
    nseg = len(segs)

    def body(*refs):
        seg_refs = refs[:nseg]
        w_ref, x_ref, ng_ref, dx2_ref, dx1_ref, db_ref, dng_ref = refs[nseg:]
        _zero_at_first_step(db_ref, dng_ref)
        dh = None
        for e, sr in enumerate(seg_refs):
            sb = sr[...]
            part = _dot_nt(sb, w_ref[e // 2, :, (e % 2) * d:(e % 2 + 1) * d])
            dh = part if dh is None else dh + part
            db_ref[e] += jnp.sum(sb.astype(f32), axis=0, keepdims=True)
        _, xh, r = _rms(x_ref[...], ng_ref[...])
        dx1_ref[...] = dx2_ref[...] + _rms_bwd(dh, xh, r, ng_ref[...])
        dng_ref[...] += jnp.sum(dh * xh, axis=0, keepdims=True)

    row = pl.BlockSpec((tm, d), lambda i: (i, 0))
    vec = pl.BlockSpec((1, d), lambda i: (0, 0))
    return pl.pallas_call(
        body, name="inproj_bwd_act", grid=(t // tm,),
        in_specs=[row] * nseg + [pl.BlockSpec((s4, d, 2 * d), lambda i: (0, 0, 0), pipeline_mode=pl.Buffered(1)),
                                 row, vec, row],
        out_specs=[row, pl.BlockSpec((nseg, 1, d), lambda i: (0, 0, 0)), vec],
        out_shape=[jax.ShapeDtypeStruct((t, d), f32), jax.ShapeDtypeStruct((nseg, 1, d), f32),
                   jax.ShapeDtypeStruct((1, d), f32)],
        compiler_params=_cparams(VMEM_LIMIT_WIDE),
    )(*segs, win, x1, ng, dx2)


def _place():
    return lax.axis_index("x"), lax.axis_index("y"), lax.axis_index("c")


def _other_chips(x, y):
    return [(1 - x, y), (x, 1 - y), (1 - x, 1 - y)]


_ANY = pl.BlockSpec(memory_space=pl.ANY)


_HBM = pl.BlockSpec(memory_space=pltpu.HBM)
_SEM = pl.BlockSpec(memory_space=pltpu.SEMAPHORE)
_EFFECT = pltpu.SideEffectType.DATAFLOW_SIDE_EFFECTING


def _hbm(a):
    return pltpu.with_memory_space_constraint(a, pltpu.HBM)


def _half_rows(ref, c):
    half = ref.shape[1] // 2
    return pl.ds(pl.multiple_of(c * half, 16), half)


def _chip_copy(src, dst, send_sem, recv_sem, chip, c):
    return pltpu.make_async_remote_copy(src_ref=src, dst_ref=dst, send_sem=send_sem, recv_sem=recv_sem,
                                        device_id=(chip[0], chip[1], c), device_id_type=MESH)


def gather_start(bufs, groups, name):
    nb, ng = len(bufs), len(groups)

    def body(*refs):
        ins = refs[:nb]
        sems = refs[nb:nb + 2 * ng]
        token = refs[-1]
        x, y, c = _place()
        k = 2 * x + y
        for gi, grp in enumerate(groups):
            for wi, w in enumerate(grp):
                mine = ins[w].at[k, _half_rows(ins[w], c)]
                for j, chip in enumerate(_other_chips(x, y)):
                    _chip_copy(mine, mine, sems[2 * gi].at[3 * wi + j], sems[2 * gi + 1].at[3 * wi + j], chip, c).start()
        token[...] = jnp.zeros_like(token)

    sem_shapes = []
    for grp in groups:
        sem_shapes += [pltpu.SemaphoreType.DMA((3 * len(grp),)), pltpu.SemaphoreType.DMA((3 * len(grp),))]
    outs = pl.pallas_call(
        body, name=name,
        out_shape=sem_shapes + [pltpu.HBM(b.shape, b.dtype) for b in bufs] + [jax.ShapeDtypeStruct((8, 128), f32)],
        in_specs=[_HBM] * nb,
        out_specs=[_SEM] * (2 * ng) + [_HBM] * nb + [pl.BlockSpec(memory_space=pltpu.VMEM)],
        input_output_aliases={w: 2 * ng + w for w in range(nb)},
        compiler_params=pltpu.CompilerParams(has_side_effects=_EFFECT),
    )(*[_hbm(b) for b in bufs])
    sems = [(outs[2 * gi], outs[2 * gi + 1]) for gi in range(ng)]
    return sems, list(outs[2 * ng:2 * ng + nb]), outs[-1]


def gather_wait(bufs, sems, after, name):
    n = len(bufs)

    def body(*refs):
        ins = refs[:n]
        send_sems, recv_sems = refs[n], refs[n + 1]
        x, y, c = _place()
        k = 2 * x + y
        for wi in range(n):
            half = _half_rows(ins[wi], c)
            for j, chip in enumerate(_other_chips(x, y)):
                cp = _chip_copy(ins[wi].at[k, half], ins[wi].at[2 * chip[0] + chip[1], half], send_sems.at[3 * wi + j],
                                recv_sems.at[3 * wi + j], chip, c)
                cp.wait_send()
                cp.wait_recv()

    outs = pl.pallas_call(
        body, name=name,
        out_shape=[pltpu.HBM(b.shape, b.dtype) for b in bufs],
        in_specs=[_HBM] * n + [_SEM, _SEM, _ANY],
        out_specs=[_HBM] * n,
        input_output_aliases={i: i for i in range(n)},
        compiler_params=pltpu.CompilerParams(has_side_effects=_EFFECT),
    )(*bufs, sems[0], sems[1], after)
    return list(outs)


def gather_forward(bufs, name):
    n = len(bufs)

    def body(*refs):
        ins = refs[n:2 * n]
        send_sems, recv_sems = refs[2 * n], refs[2 * n + 1]
        x, y, c = _place()
        copies = []
        for wi in range(n):
            for j, chip in enumerate(_other_chips(x, y)):
                kp = 2 * chip[0] + chip[1]
                got = ins[wi].at[kp, _half_rows(ins[wi], c)]
                cp = pltpu.make_async_remote_copy(
                    src_ref=got, dst_ref=got, send_sem=send_sems.at[3 * wi + j], recv_sem=recv_sems.at[3 * wi + j],
                    device_id=(x, y, 1 - c), device_id_type=MESH)
                cp.start()
                copies.append((cp, wi, kp, j))
        for cp, wi, kp, j in copies:
            cp.wait_send()
            theirs = ins[wi].at[kp, _half_rows(ins[wi], 1 - c)]
            pltpu.make_async_remote_copy(
                src_ref=theirs, dst_ref=theirs, send_sem=send_sems.at[3 * wi + j], recv_sem=recv_sems.at[3 * wi + j],
                device_id=(x, y, 1 - c), device_id_type=MESH).wait_recv()

    outs = pl.pallas_call(
        body, name=name,
        out_shape=[jax.ShapeDtypeStruct(b.shape, b.dtype) for b in bufs],
        in_specs=[_ANY] * n, out_specs=[_ANY] * n,
        input_output_aliases={i: i for i in range(n)},
        scratch_shapes=[pltpu.SemaphoreType.DMA((3 * n,)), pltpu.SemaphoreType.DMA((3 * n,))],
    )(*bufs)
    return list(outs)


def forward_start(bufs, name):
    n = len(bufs)

    def body(*refs):
        x, y, c = _place()
        for wi in range(n):
            for j, chip in enumerate(_other_chips(x, y)):
                got = refs[wi].at[2 * chip[0] + chip[1], _half_rows(refs[wi], c)]
                _sibling_copy(got, got, refs[n].at[3 * wi + j], refs[n + 1].at[3 * wi + j]).start()
        refs[-1][...] = jnp.zeros_like(refs[-1])

    return _split_start(body, name, 3 * n, list(bufs))


def forward_wait(bufs, sems, after, name):
    n = len(bufs)

    def body(*refs):
        x, y, c = _place()
        for wi in range(n):
            for j, chip in enumerate(_other_chips(x, y)):
                kp = 2 * chip[0] + chip[1]
                got = refs[wi].at[kp, _half_rows(refs[wi], c)]
                theirs = refs[wi].at[kp, _half_rows(refs[wi], 1 - c)]
                _sibling_copy(got, got, refs[n].at[3 * wi + j], refs[n + 1].at[3 * wi + j]).wait_send()
                _sibling_copy(theirs, theirs, refs[n].at[3 * wi + j], refs[n + 1].at[3 * wi + j]).wait_recv()

    return _split_wait(body, name, list(bufs), sems, after)


def exchange_start(grads, name):
    n = len(grads)
    lands = [lax.empty((3,) + g.shape[1:], g.dtype) for g in grads]

    def body(*refs):
        ins = refs[:n]
        land = refs[n:2 * n]
        send_sems, recv_sems = refs[2 * n], refs[2 * n + 1]
        token = refs[-1]
        x, y, c = _place()
        for wi in range(n):
            for j, chip in enumerate(_other_chips(x, y)):
                _chip_copy(ins[wi].at[2 * chip[0] + chip[1]], land[wi].at[j], send_sems.at[3 * wi + j],
                           recv_sems.at[3 * wi + j], chip, c).start()
        token[...] = jnp.zeros_like(token)

    outs = pl.pallas_call(
        body, name=name,
        out_shape=[pltpu.SemaphoreType.DMA((3 * n,)), pltpu.SemaphoreType.DMA((3 * n,))]
        + [pltpu.HBM(g.shape, g.dtype) for g in grads] + [pltpu.HBM(l.shape, l.dtype) for l in lands]
        + [jax.ShapeDtypeStruct((8, 128), f32)],
        in_specs=[_HBM] * (2 * n),
        out_specs=[_SEM, _SEM] + [_HBM] * (2 * n) + [pl.BlockSpec(memory_space=pltpu.VMEM)],
        input_output_aliases={i: 2 + i for i in range(2 * n)},
        compiler_params=pltpu.CompilerParams(has_side_effects=_EFFECT),
    )(*[_hbm(g) for g in grads], *[_hbm(l) for l in lands])
    return (outs[0], outs[1]), list(outs[2:2 + n]), list(outs[2 + n:2 + 2 * n]), outs[-1]


def exchange_wait(grads, lands, sems, after, name):
    n = len(grads)

    def body(*refs):
        ins = refs[:n]
        land = refs[n:2 * n]
        send_sems, recv_sems = refs[2 * n], refs[2 * n + 1]
        x, y, c = _place()
        for wi in range(n):
            for j, chip in enumerate(_other_chips(x, y)):
                cp = _chip_copy(ins[wi].at[2 * chip[0] + chip[1]], land[wi].at[j], send_sems.at[3 * wi + j],
                                recv_sems.at[3 * wi + j], chip, c)
                cp.wait_send()
                cp.wait_recv()

    outs = pl.pallas_call(
        body, name=name,
        out_shape=[pltpu.HBM(g.shape, g.dtype) for g in grads] + [pltpu.HBM(l.shape, l.dtype) for l in lands],
        in_specs=[_HBM] * (2 * n) + [_SEM, _SEM, _ANY],
        out_specs=[_HBM] * (2 * n),
        input_output_aliases={i: i for i in range(2 * n)},
        compiler_params=pltpu.CompilerParams(has_side_effects=_EFFECT),
    )(*grads, *lands, sems[0], sems[1], after)
    return list(outs[:n]), list(outs[n:])


def _split_start(body, name, n_sems, operands):
    n = len(operands)
    outs = pl.pallas_call(
        body, name=name,
        out_shape=[pltpu.SemaphoreType.DMA((n_sems,)), pltpu.SemaphoreType.DMA((n_sems,))]
        + [pltpu.HBM(o.shape, o.dtype) for o in operands] + [jax.ShapeDtypeStruct((8, 128), f32)],
        in_specs=[_HBM] * n,
        out_specs=[_SEM, _SEM] + [_HBM] * n + [pl.BlockSpec(memory_space=pltpu.VMEM)],
        input_output_aliases={i: 2 + i for i in range(n)},
        compiler_params=pltpu.CompilerParams(has_side_effects=_EFFECT),
    )(*[_hbm(o) for o in operands])
    return (outs[0], outs[1]), list(outs[2:2 + n]), outs[-1]


def _split_wait(body, name, operands, sems, after):
    n = len(operands)
    outs = pl.pallas_call(
        body, name=name,
        out_shape=[pltpu.HBM(o.shape, o.dtype) for o in operands],
        in_specs=[_HBM] * n + [_SEM, _SEM, _ANY],
        out_specs=[_HBM] * n,
        input_output_aliases={i: i for i in range(n)},
        compiler_params=pltpu.CompilerParams(has_side_effects=_EFFECT),
    )(*operands, sems[0], sems[1], after)
    return list(outs)


def _sibling_copy(src, dst, send_sem, recv_sem):
    x, y, c = _place()
    return pltpu.make_async_remote_copy(src_ref=src, dst_ref=dst, send_sem=send_sem, recv_sem=recv_sem,
                                        device_id=(x, y, 1 - c), device_id_type=MESH)


def swap_start(parts, name):
    n = len(parts)

    def body(*refs):
        for w in range(n):
            _sibling_copy(refs[w], refs[n + w], refs[2 * n].at[w], refs[2 * n + 1].at[w]).start()
        refs[-1][...] = jnp.zeros_like(refs[-1])

    sems, ops, token = _split_start(body, name, n, list(parts) + [lax.empty(p.shape, p.dtype) for p in parts])
    return sems, ops[:n], ops[n:], token


def swap_wait(parts, lands, sems, after, name):
    n = len(parts)

    def body(*refs):
        for w in range(n):
            cp = _sibling_copy(refs[w], refs[n + w], refs[2 * n].at[w], refs[2 * n + 1].at[w])
            cp.wait_send()
            cp.wait_recv()

    outs = _split_wait(body, name, list(parts) + list(lands), sems, after)
    return outs[:n], outs[n:]


def _all_peers(x, y, c):
    return [(1 - x if m & 4 else x, 1 - y if m & 2 else y, 1 - c if m & 1 else c) for m in range(1, N_DEV)]


def small_start(block):
    land = jnp.broadcast_to(block[None], (N_DEV,) + block.shape)

    def body(b_ref, land_ref, send_sems, recv_sems, b_thru, land_thru, token):
        x, y, c = _place()
        me = 4 * x + 2 * y + c
        for m, peer in enumerate(_all_peers(x, y, c)):
            pltpu.make_async_remote_copy(src_ref=b_ref, dst_ref=land_ref.at[me], send_sem=send_sems.at[m],
                                         recv_sem=recv_sems.at[m], device_id=peer, device_id_type=MESH).start()
        token[...] = jnp.zeros_like(token)

    sems, ops, token = _split_start(body, "small_start", N_DEV - 1, [block, land])
    return sems, ops[0], ops[1], token


def small_wait(block, land, sems, after):
    def body(b_ref, land_ref, send_sems, recv_sems, after_ref, b_thru, land_thru):
        x, y, c = _place()
        for m, (px, py, pc) in enumerate(_all_peers(x, y, c)):
            cp = pltpu.make_async_remote_copy(src_ref=b_ref, dst_ref=land_ref.at[4 * px + 2 * py + pc],
                                              send_sem=send_sems.at[m], recv_sem=recv_sems.at[m],
                                              device_id=(px, py, pc), device_id_type=MESH)
            cp.wait_send()
            cp.wait_recv()

    return _split_wait(body, "small_wait", [block, land], sems, after)[1]


def _adamw(w, g, m, v):
    m = ADAM_B1 * m + (1.0 - ADAM_B1) * g
    v = ADAM_B2 * v + (1.0 - ADAM_B2) * (g * g)
    m_hat = m / (1.0 - ADAM_B1 ** ADAM_STEP)
    v_hat = v / (1.0 - ADAM_B2 ** ADAM_STEP)
    delta = -ADAM_LR * (m_hat / (jnp.sqrt(v_hat) + ADAM_EPS) + ADAM_WD * w)
    return delta, m, v


EW_BLOCK_BYTES = 2 * 1024 * 1024


def _ew_tile(rows, cols):
    for cand in (512, 352, 256, 176, 128, 64, 32, 16, 8):
        if rows % cand == 0 and cand * cols * 4 <= EW_BLOCK_BYTES:
            return cand
    return rows


def sum_partials(chip, own, land, name):
    _, r, c = own.shape
    tr = _ew_tile(r, c)

    def body(k_ref, own_ref, p_ref, o_ref):
        o_ref[...] = ((own_ref[0].astype(f32) + p_ref[0].astype(f32)) + p_ref[1].astype(f32)) + p_ref[2].astype(f32)

    return pl.pallas_call(
        body, name=name,
        grid_spec=pltpu.PrefetchScalarGridSpec(
            num_scalar_prefetch=1, grid=(r // tr,),
            in_specs=[pl.BlockSpec((1, tr, c), lambda i, k: (k[0], i, 0)), pl.BlockSpec((3, tr, c), lambda i, k: (0, i, 0))],
            out_specs=pl.BlockSpec((tr, c), lambda i, k: (i, 0))),
        out_shape=jax.ShapeDtypeStruct((r, c), f32),
        compiler_params=_cparams(),
    )(chip, own, land)


def adamw_shard(p_mine, p_sibling, w, m, v, name):
    r, c = w.shape
    tr = _ew_tile(r, c)

    def body(a_ref, b_ref, w_ref, m_ref, v_ref, g_ref, d_ref, mo_ref, vo_ref):
        g = a_ref[...] + b_ref[...]
        delta, mn, vn = _adamw(w_ref[...], g, m_ref[...], v_ref[...])
        g_ref[...] = g
        d_ref[...] = delta
        mo_ref[...] = mn
        vo_ref[...] = vn

    blk = pl.BlockSpec((tr, c), lambda i: (i, 0))
    return pl.pallas_call(
        body, name=name, grid=(r // tr,),
        in_specs=[blk] * 5, out_specs=[blk] * 4,
        out_shape=[jax.ShapeDtypeStruct((r, c), f32)] * 4,
        compiler_params=_cparams(),
    )(p_mine, p_sibling, w, m, v)


def adamw_small(g8, w, m, v):
    _, r, lanes = g8.shape

    def body(g_ref, w_ref, m_ref, v_ref, go_ref, d_ref, mo_ref, vo_ref):
        g = g_ref[0]
        for i in range(1, N_DEV):
            g = g + g_ref[i]
        delta, mn, vn = _adamw(w_ref[...], g, m_ref[...], v_ref[...])
        go_ref[...] = g
        d_ref[...] = delta
        mo_ref[...] = mn
        vo_ref[...] = vn

    return pl.pallas_call(
        body, name="adamw_small",
        out_shape=[jax.ShapeDtypeStruct((r, lanes), f32)] * 4,
        compiler_params=_cparams(),
    )(g8, w, m, v)


def _size(shape):
    n = 1
    for e in shape:
        n *= e
    return n


def _pack_rows(shapes):
    rows = [-(-_size(s) // 1024) * 8 for s in shapes]
    return rows, sum(rows)


def _pack(arrs, shapes):
    rows, _ = _pack_rows(shapes)
    parts = [jnp.pad(a.reshape(-1).astype(f32), (0, r * 128 - _size(s))).reshape(r, 128)
             for a, s, r in zip(arrs, shapes, rows)]
    return jnp.concatenate(parts, axis=0)


def _unpack(block, shapes):
    rows, _ = _pack_rows(shapes)
    out, off = [], 0
    for s, r in zip(shapes, rows):
        out.append(block[off:off + r].reshape(-1)[:_size(s)].reshape(s))
        off += r
    return out


TRANSPOSED = ("ffn1_w_gate", "ffn1_w_up", "ffn2_w_gate", "ffn2_w_up")


def _shard2d(a, n):
    return a[0].T if n in TRANSPOSED else a[0]


def _unshard(a, n):
    return (a.T if n in TRANSPOSED else a)[None]


BIG = ("ffn1_w_gate", "ffn1_w_up", "ffn1_w_down", "w_in", "w_branch_a", "w_branch_b", "w_out",
       "ffn2_w_gate", "ffn2_w_up", "ffn2_w_down")
SMALL = ("ffn1_norm", "mix_norm", "b_in", "sgu_norm_g", "sgu_norm_b", "sgu_w_s", "sgu_b_s", "ret_decay_logit",
         "ffn2_norm", "final_norm")
WEIGHTS = ("ffn1_norm", "ffn1_w_gate", "ffn1_w_up", "ffn1_w_down", "mix_norm", "w_in", "b_in", "sgu_norm_g",
           "sgu_norm_b", "sgu_w_s", "sgu_b_s", "ret_decay_logit", "w_branch_a", "w_branch_b", "w_out", "ffn2_norm",
           "ffn2_w_gate", "ffn2_w_up", "ffn2_w_down", "final_norm")


def kernel(x, ffn1_norm, ffn1_w_gate, ffn1_w_up, ffn1_w_down, mix_norm, w_in, b_in, sgu_norm_g, sgu_norm_b, sgu_w_s, sgu_b_s, ret_decay_logit, w_branch_a, w_branch_b, w_out, ffn2_norm, ffn2_w_gate, ffn2_w_up, ffn2_w_down, final_norm, loss_target, m_ffn1_norm, m_ffn1_w_gate, m_ffn1_w_up, m_ffn1_w_down, m_mix_norm, m_w_in, m_b_in, m_sgu_norm_g, m_sgu_norm_b, m_sgu_w_s, m_sgu_b_s, m_ret_decay_logit, m_w_branch_a, m_w_branch_b, m_w_out, m_ffn2_norm, m_ffn2_w_gate, m_ffn2_w_up, m_ffn2_w_down, m_final_norm, v_ffn1_norm, v_ffn1_w_gate, v_ffn1_w_up, v_ffn1_w_down, v_mix_norm, v_w_in, v_b_in, v_sgu_norm_g, v_sgu_norm_b, v_sgu_w_s, v_sgu_b_s, v_ret_decay_logit, v_w_branch_a, v_w_branch_b, v_w_out, v_ffn2_norm, v_ffn2_w_gate, v_ffn2_w_up, v_ffn2_w_down, v_final_norm):
    p = dict(ffn1_norm=ffn1_norm, ffn1_w_gate=ffn1_w_gate, ffn1_w_up=ffn1_w_up, ffn1_w_down=ffn1_w_down,
             mix_norm=mix_norm, w_in=w_in, b_in=b_in, sgu_norm_g=sgu_norm_g, sgu_norm_b=sgu_norm_b, sgu_w_s=sgu_w_s,
             sgu_b_s=sgu_b_s, ret_decay_logit=ret_decay_logit, w_branch_a=w_branch_a, w_branch_b=w_branch_b,
             w_out=w_out, ffn2_norm=ffn2_norm, ffn2_w_gate=ffn2_w_gate, ffn2_w_up=ffn2_w_up, ffn2_w_down=ffn2_w_down,
             final_norm=final_norm)
    mom = dict(ffn1_norm=m_ffn1_norm, ffn1_w_gate=m_ffn1_w_gate, ffn1_w_up=m_ffn1_w_up, ffn1_w_down=m_ffn1_w_down,
               mix_norm=m_mix_norm, w_in=m_w_in, b_in=m_b_in, sgu_norm_g=m_sgu_norm_g, sgu_norm_b=m_sgu_norm_b,
               sgu_w_s=m_sgu_w_s, sgu_b_s=m_sgu_b_s, ret_decay_logit=m_ret_decay_logit, w_branch_a=m_w_branch_a,
               w_branch_b=m_w_branch_b, w_out=m_w_out, ffn2_norm=m_ffn2_norm, ffn2_w_gate=m_ffn2_w_gate,
               ffn2_w_up=m_ffn2_w_up, ffn2_w_down=m_ffn2_w_down, final_norm=m_final_norm)
    var = dict(ffn1_norm=v_ffn1_norm, ffn1_w_gate=v_ffn1_w_gate, ffn1_w_up=v_ffn1_w_up, ffn1_w_down=v_ffn1_w_down,
               mix_norm=v_mix_norm, w_in=v_w_in, b_in=v_b_in, sgu_norm_g=v_sgu_norm_g, sgu_norm_b=v_sgu_norm_b,
               sgu_w_s=v_sgu_w_s, sgu_b_s=v_sgu_b_s, ret_decay_logit=v_ret_decay_logit, w_branch_a=v_w_branch_a,
               w_branch_b=v_w_branch_b, w_out=v_w_out, ffn2_norm=v_ffn2_norm, ffn2_w_gate=v_ffn2_w_gate,
               ffn2_w_up=v_ffn2_w_up, ffn2_w_down=v_ffn2_w_down, final_norm=v_final_norm)

    xs = x[0]
    tgt = loss_target[0]
    t, d = xs.shape
    dk = d // RET_HEADS

    shards2d = {n: _shard2d(p[n], n) for n in BIG}
    chip = (2 * lax.axis_index("x") + lax.axis_index("y")).astype(jnp.int32).reshape(1)
    groups = {"ffn1": ("ffn1_w_gate", "ffn1_w_up", "ffn1_w_down"), "in": ("w_in",),
              "mix": ("w_branch_a", "w_branch_b", "w_out"), "ffn2": ("ffn2_w_gate", "ffn2_w_up", "ffn2_w_down")}
    def own_slot(n, zero):
        sh = shards2d[n].astype(bf16) + zero
        return lax.dynamic_update_index_in_dim(lax.empty((N_CHIPS,) + sh.shape, bf16), sh, chip[0], 0)

    sems, bufs, tok = gather_start([own_slot(n, jnp.zeros((), bf16)) for n in groups["ffn1"]], [[0, 1, 2]],
                                   "gather_start_ffn1")
    gsem = {"ffn1": sems[0]}
    pending = dict(zip(groups["ffn1"], bufs))
    rest = [n for g in ("in", "mix", "ffn2") for n in groups[g]]
    sems, bufs, tok_rest = gather_start([own_slot(n, tok[0, 0].astype(bf16)) for n in rest],
                                 [[rest.index(n) for n in groups[g]] for g in ("in", "mix", "ffn2")], "gather_start_rest")
    gsem.update(zip(("in", "mix", "ffn2"), sems))
    pending.update(zip(rest, bufs))

    def arrive(gs, after):
        got = []
        for g in gs:
            got += gather_wait([pending[n] for n in groups[g]], gsem[g], after, "gather_wait_" + g)
        return gather_forward(got, "gather_forward_" + gs[0])

    bin4 = b_in.reshape(N_CHIPS, 1, 2 * d)
    ws_b = sgu_w_s[0].astype(bf16)
    bs_c = sgu_b_s[0][:, :, None]
    cols, mats, cdec, cos, sin = retention_constants(ret_decay_logit[0], t, dk, tok_rest[0, 0])

    wg1, wu1, wd1 = [_pair_shards(w) for w in arrive(["ffn1"], cos)]
    x1, g1, u1 = ffn_fwd(xs, ffn1_norm, wg1, wu1, wd1, "ffn1_fwd")
    win, = arrive(["in"], x1)
    proj, hb2, a = inproj_fwd(x1, mix_norm, win, bin4, cos, sin, sgu_norm_g, sgu_norm_b, ws_b, bs_c)
    late = []
    for g in ("mix", "ffn2"):
        late += gather_wait([pending[n] for n in groups[g]], gsem[g], proj, "gather_wait_" + g)
    fsems, late, ftok = forward_start(late, "forward_start_mix")
    r, rn = ret_fwd(proj, cols, mats, cdec, ftok)
    wa, wb, wo, wg2, wu2, wd2 = forward_wait(late, fsems, rn, "forward_wait_mix")
    wa, wb, wo = [w.reshape(d, d) for w in (wa, wb, wo)]
    wg2, wu2, wd2 = [_pair_shards(w) for w in (wg2, wu2, wd2)]
    x2, ba, br = mix_fwd(a, rn, proj, wa, wb, wo, x1)
    loss_blk, dx3, d_final, g2, u2 = ffn_fwd_loss(x2, ffn2_norm, wg2, wu2, wd2, final_norm.reshape(1, d), tgt, "ffn2_fwd")

    sent, swaps = {}, {}
    out_g, out_d, out_m, out_v = {}, {}, {}, {}

    def reduce_plane(g, after):
        gsems, own, lands, _ = sent[g]
        own, lands = exchange_wait(own, lands, gsems, after, "exchange_wait_" + g)
        plane = [sum_partials(chip, o, l, "sum_" + n) for n, o, l in zip(groups[g], own, lands)]
        swaps[g] = swap_start(plane, "swap_start_" + g)
        return swaps[g][3]

    def update(g, after):
        ssems, plane, lands, _ = swaps[g]
        plane, other = swap_wait(plane, lands, ssems, after, "swap_wait_" + g)
        for n, mine, sib in zip(groups[g], plane, other):
            res = adamw_shard(mine, sib, shards2d[n], _shard2d(mom[n], n), _shard2d(var[n], n), "adamw_" + n)
            out_g[n], out_d[n], out_m[n], out_v[n] = [_unshard(o, n) for o in res]
        return res[0]

    dx2, dg2, du2, act2, hb3, dyb2, d_ffn2n = ffn_bwd_act(dx3, x2, ffn2_norm, g2, u2, wg2, wu2, wd2, "ffn2_bwd_act", tok)
    sent["ffn2"] = exchange_start(ffn_weight_grads_one_call(hb3, dyb2, dg2, du2, act2, "ffn2_grad", tok),
                                  "exchange_start_ffn2")
    drn, dga, dgb, mixb, dba, dbr, dx2b, dua, dva, d_ws, d_bs, d_sng, d_snb = mix_bwd_act(
        dx2, ba, br, proj, wa, wb, wo, sgu_norm_g, sgu_norm_b, ws_b, bs_c, sent["ffn2"][3])
    tg = min(t, 2048)
    row = pl.BlockSpec((tg, d), lambda s, i: (i, 0))

    def square_grad(xa, ya, name):
        return tn_matmul(xa, [ya], row, [row], 1, d, [d], t, tg, name, tok).reshape(N_CHIPS, d // N_CHIPS, d)

    g_mix = [square_grad(a, dba, "grad_w_branch_a"), square_grad(rn, dbr, "grad_w_branch_b"),
             square_grad(mixb, dx2b, "grad_w_out")]
    dq, dkr, dv, dgr, dlg = ret_bwd(drn, r, proj, cols, mats, cdec, cos, sin)
    segs = [dua, dva, dq, dkr, dv, dgr, dga, dgb]
    dx1, d_bin, d_mixn = inproj_bwd_act(segs, win, x1, mix_norm, dx2)
    g_in = None
    for s in range(N_CHIPS):
        g_in = tn_matmul(hb2, [segs[2 * s], segs[2 * s + 1]], row, [row, row], 1, d, [d, d], t, tg, "grad_w_in_%d" % s,
                         tok, (g_in, s, N_CHIPS))
    groups["mix_in"] = groups["mix"] + groups["in"]
    sent["mix_in"] = exchange_start(g_mix + [g_in], "exchange_start_mix_in")
    grad_x, dg1, du1, act1, hb1, dyb1, d_ffn1n = ffn_bwd_act(dx1, xs, ffn1_norm, g1, u1, wg1, wu1, wd1, "ffn1_bwd_act",
                                                              sent["mix_in"][3])
    dlogit = dlg[:, 0:2, 0].T * jax.nn.sigmoid(-ret_decay_logit[0].astype(f32))
    small_g = dict(ffn1_norm=d_ffn1n, mix_norm=d_mixn, b_in=d_bin, sgu_norm_g=d_sng, sgu_norm_b=d_snb, sgu_w_s=d_ws,
                   sgu_b_s=d_bs, ret_decay_logit=dlogit, ffn2_norm=d_ffn2n, final_norm=d_final)
    shapes = [p[n].shape for n in SMALL] + [(1,)]
    small_sems, small_blk, small_land, small_tok = small_start(
        _pack([small_g[n] for n in SMALL] + [loss_blk[0, 0:1]], shapes))

    def send_one(which, grad):
        n = "ffn1_" + which
        groups[n] = (n,)
        sent[n] = exchange_start([grad], "exchange_start_" + n)
        return sent[n][3]

    ffn_weight_grads(hb1, dyb1, dg1, du1, act1, "ffn1_grad", small_tok, send_one)

    after = reduce_plane("ffn2", sent["ffn1_w_down"][3])
    after = reduce_plane("mix_in", after)
    after = update("ffn2", after)
    g8 = small_wait(small_blk, small_land, small_sems, after)
    no_state = [jnp.zeros((1,), f32)]
    sg, sd, sm, sv = adamw_small(g8, _pack([p[n] for n in SMALL] + no_state, shapes),
                                 _pack([mom[n] for n in SMALL] + no_state, shapes),
                                 _pack([var[n] for n in SMALL] + no_state, shapes))
    for res, blockv in ((out_g, sg), (out_d, sd), (out_m, sm), (out_v, sv)):
        for n, val in zip(SMALL, _unpack(blockv, shapes)):
            res[n] = val
    loss = _unpack(sg, shapes)[-1][0]
    after = update("mix_in", sg)
    after = reduce_plane("ffn1_w_gate", after)
    after = reduce_plane("ffn1_w_up", after)
    after = update("ffn1_w_gate", after)
    after = reduce_plane("ffn1_w_down", after)
    after = update("ffn1_w_up", after)
    update("ffn1_w_down", after)

    return (loss, grad_x[None], *[out_g[n] for n in WEIGHTS], *[out_d[n] for n in WEIGHTS],
            *[out_m[n] for n in WEIGHTS], *[out_v[n] for n in WEIGHTS])
```

```python
import jax
import jax.numpy as jnp
from jax import lax
from jax.experimental import pallas as pl
from jax.experimental.pallas import tpu as pltpu

f32 = jnp.float32
bf16 = jnp.bfloat16

SGU_CHUNK = 128
CHUNK = 256
RET_HEADS = 4
SGU_GROUPS = 4
ROPE_BASE = 10000.0
NORM_EPS = 1e-6
ADAM_LR = 0.001
ADAM_B1 = 0.9
ADAM_B2 = 0.999
ADAM_EPS = 1e-08
ADAM_WD = 0.01
ADAM_STEP = 10
N_CHIPS = 4
N_DEV = 8
MESH = pl.DeviceIdType.MESH
VMEM_LIMIT = 52 * 1024 * 1024
VMEM_LIMIT_WIDE = 62 * 1024 * 1024

_NT = (((1,), (1,)), ((), ()))
_TN = (((0,), (0,)), ((), ()))


def _cparams(limit=None):
    return pltpu.CompilerParams(vmem_limit_bytes=VMEM_LIMIT if limit is None else limit)


def _row_tile(t):
    return 512 if t >= 2048 else t // 2


def _dot(a, b):
    return jnp.dot(a, b, preferred_element_type=f32)


def _dot_nt(a, b):
    return lax.dot_general(a, b, _NT, preferred_element_type=f32)


def _dot_tn(a, b):
    return lax.dot_general(a, b, _TN, preferred_element_type=f32)


def _rms(x, g):
    r = lax.rsqrt(jnp.mean(x * x, axis=-1, keepdims=True) + NORM_EPS)
    xh = x * r
    return xh * g, xh, r


def _rms_bwd(dy, xh, r, g):
    dxh = dy * g
    return r * (dxh - xh * jnp.mean(dxh * xh, axis=-1, keepdims=True))


def _sigmoid(x):
    return jax.nn.sigmoid(x)


def _dsilu(g, sg):
    return sg * (1.0 + g * (1.0 - sg))


def _gelu(x):
    return 0.5 * x * (1.0 + lax.erf(x * 0.7071067811865476))


def _dgelu(x):
    return 0.5 * (1.0 + lax.erf(x * 0.7071067811865476)) + x * jnp.exp(-0.5 * x * x) * 0.3989422804014327


def _zero_at_first_step(*refs):
    @pl.when(pl.program_id(0) == 0)
    def _():
        for ref in refs:
            ref[...] = jnp.zeros_like(ref)


def _ffn_tile(t):
    return 256 if t >= 2048 else t // 2


def _ffn_fwd_rows(xx, ng_ref, wg_ref, wu_ref, wd_ref, g_ref, u_ref):
    y, _, _ = _rms(xx, ng_ref[...])
    h = y.astype(bf16)
    acc = None
    for s in range(wg_ref.shape[0]):
        g = _dot_nt(h, wg_ref[s])
        u = _dot_nt(h, wu_ref[s])
        g_ref[s] = g.astype(bf16)
        u_ref[s] = u.astype(bf16)
        part = _dot((g * _sigmoid(g) * u).astype(bf16), wd_ref[s])
        acc = part if acc is None else acc + part
    return xx + 0.5 * acc


def ffn_fwd(x, ng, wg, wu, wd, name):
    t, d = x.shape
    ns, fs, _ = wg.shape
    tm = _ffn_tile(t)

    def body(x_ref, ng_ref, wg_ref, wu_ref, wd_ref, xo_ref, g_ref, u_ref):
        xo_ref[...] = _ffn_fwd_rows(x_ref[...], ng_ref, wg_ref, wu_ref, wd_ref, g_ref, u_ref)

    row = pl.BlockSpec((tm, d), lambda i: (i, 0))
    shard = pl.BlockSpec((ns, tm, fs), lambda i: (0, i, 0))
    wspec = pl.BlockSpec((ns, fs, d), lambda i: (0, 0, 0), pipeline_mode=pl.Buffered(1))
    return pl.pallas_call(
        body, name=name, grid=(t // tm,),
        in_specs=[row, pl.BlockSpec((1, d), lambda i: (0, 0)), wspec, wspec, wspec],
        out_specs=[row, shard, shard],
        out_shape=[jax.ShapeDtypeStruct((t, d), f32), jax.ShapeDtypeStruct((ns, t, fs), bf16),
                   jax.ShapeDtypeStruct((ns, t, fs), bf16)],
        compiler_params=_cparams(),
    )(x, ng, wg, wu, wd)


def ffn_fwd_loss(x, ng, wg, wu, wd, fng, tgt, name):
    t, d = x.shape
    ns, fs, _ = wg.shape
    tm = _ffn_tile(t)

    def body(x_ref, ng_ref, wg_ref, wu_ref, wd_ref, fng_ref, t_ref, loss_ref, dx_ref, dfn_ref, g_ref, u_ref):
        _zero_at_first_step(loss_ref, dfn_ref)
        x3 = _ffn_fwd_rows(x_ref[...], ng_ref, wg_ref, wu_ref, wd_ref, g_ref, u_ref)
        y, xh, r = _rms(x3, fng_ref[...])
        diff = y - t_ref[...]
        part = 0.5 * jnp.sum(jnp.sum(diff * diff, axis=0, keepdims=True), axis=1, keepdims=True) / d
        loss_ref[...] += jnp.broadcast_to(part, (1, 128))
        dy = diff * (1.0 / d)
        dx_ref[...] = _rms_bwd(dy, xh, r, fng_ref[...])
        dfn_ref[...] += jnp.sum(dy * xh, axis=0, keepdims=True)

    row = pl.BlockSpec((tm, d), lambda i: (i, 0))
    vec = pl.BlockSpec((1, d), lambda i: (0, 0))
    shard = pl.BlockSpec((ns, tm, fs), lambda i: (0, i, 0))
    wspec = pl.BlockSpec((ns, fs, d), lambda i: (0, 0, 0), pipeline_mode=pl.Buffered(1))
    return pl.pallas_call(
        body, name=name, grid=(t // tm,),
        in_specs=[row, vec, wspec, wspec, wspec, vec, row],
        out_specs=[pl.BlockSpec((1, 128), lambda i: (0, 0)), row, vec, shard, shard],
        out_shape=[jax.ShapeDtypeStruct((1, 128), f32), jax.ShapeDtypeStruct((t, d), f32), jax.ShapeDtypeStruct((1, d), f32),
                   jax.ShapeDtypeStruct((ns, t, fs), bf16), jax.ShapeDtypeStruct((ns, t, fs), bf16)],
        compiler_params=_cparams(),
    )(x, ng, wg, wu, wd, fng, tgt)


def ffn_bwd_act(dxo, x, ng, g, u, wg, wu, wd, name, dep):
    t, d = x.shape
    ns, fs, _ = wg.shape
    tm = _ffn_tile(t)

    def body(dxo_ref, x_ref, ng_ref, g_ref, u_ref, wg_ref, wu_ref, wd_ref, dep_ref,
             dx_ref, dg_ref, du_ref, act_ref, hb_ref, dyb_ref, dng_ref):
        _zero_at_first_step(dng_ref)
        dxo = dxo_ref[...]
        dyb = (0.5 * dxo).astype(bf16)
        dyb_ref[...] = dyb
        dh = None
        for s in range(ns):
            dact = _dot_nt(dyb, wd_ref[s])
            gg = g_ref[s].astype(f32)
            uu = u_ref[s].astype(f32)
            sg = _sigmoid(gg)
            sil = gg * sg
            dgb = (dact * uu * _dsilu(gg, sg)).astype(bf16)
            dub = (dact * sil).astype(bf16)
            dg_ref[s] = dgb
            du_ref[s] = dub
            act_ref[s] = (sil * uu).astype(bf16)
            part = _dot(dgb, wg_ref[s]) + _dot(dub, wu_ref[s])
            dh = part if dh is None else dh + part
        y, xh, r = _rms(x_ref[...], ng_ref[...])
        hb_ref[...] = y.astype(bf16)
        dx_ref[...] = dxo + _rms_bwd(dh, xh, r, ng_ref[...])
        dng_ref[...] += jnp.sum(dh * xh, axis=0, keepdims=True)

    row = pl.BlockSpec((tm, d), lambda i: (i, 0))
    shard = pl.BlockSpec((ns, tm, fs), lambda i: (0, i, 0))
    wspec = pl.BlockSpec((ns, fs, d), lambda i: (0, 0, 0), pipeline_mode=pl.Buffered(1))
    vec = pl.BlockSpec((1, d), lambda i: (0, 0))
    return pl.pallas_call(
        body, name=name, grid=(t // tm,),
        in_specs=[row, row, vec, shard, shard, wspec, wspec, wspec, _ANY],
        out_specs=[row, shard, shard, shard, row, row, vec],
        out_shape=[jax.ShapeDtypeStruct((t, d), f32)] + [jax.ShapeDtypeStruct((ns, t, fs), bf16)] * 3
        + [jax.ShapeDtypeStruct((t, d), bf16)] * 2 + [jax.ShapeDtypeStruct((1, d), f32)],
        compiler_params=_cparams(VMEM_LIMIT_WIDE),
    )(dxo, x, ng, g, u, wg, wu, wd, dep)


def tn_matmul(xs, ys, x_spec, y_specs, n_shards, k1, k2s, t, tm, name, dep, into=None):
    k2 = sum(k2s)
    ny = len(ys)

    def body(*refs):
        x_ref = refs[0]
        y_refs = refs[1:1 + ny]
        steps = t // tm
        o_ref, acc = (refs[-1], None) if steps == 1 else (refs[-2], refs[-1])
        i = pl.program_id(1)
        xb = x_ref[0] if len(x_ref.shape) == 3 else x_ref[...]
        if steps > 1:
            @pl.when(i == 0)
            def _():
                acc[...] = jnp.zeros_like(acc)

        off = 0
        for y_ref, w in zip(y_refs, k2s):
            yb = y_ref[0] if len(y_ref.shape) == 3 else y_ref[...]
            part = _dot_tn(xb, yb)
            if steps == 1:
                o_ref[0, :, off:off + w] = part.astype(bf16)
            else:
                acc[:, off:off + w] += part
            off += w

        if steps > 1:
            @pl.when(i == steps - 1)
            def _():
                o_ref[0] = acc[...].astype(bf16)

    if into is None:
        slot0, total, extra, aliases = 0, n_shards, [], {}
    else:
        buf, slot0, total = into
        extra = [] if buf is None else [buf]
        aliases = {} if buf is None else {2 + ny: 0}
    return pl.pallas_call(
        body, name=name, grid=(n_shards, t // tm),
        in_specs=[x_spec] + list(y_specs) + [_ANY] * (1 + len(extra)),
        out_specs=pl.BlockSpec((1, k1, k2), lambda s, i: (slot0 + s, 0, 0)),
        out_shape=jax.ShapeDtypeStruct((total, k1, k2), bf16),
        scratch_shapes=[pltpu.VMEM((k1, k2), f32)] if t // tm > 1 else [],
        input_output_aliases=aliases,
        compiler_params=_cparams(),
    )(xs, *ys, dep, *extra)


def _pair_shards(w):
    s4, fs, d = w.shape
    return w.reshape(s4 // 2, 2 * fs, d)


def ffn_weight_grads(hb, dyb, dg, du, act, name, dep, each=None):
    t, d = hb.shape
    s2, _, fs2 = dg.shape
    tm = t
    row = pl.BlockSpec((tm, d), lambda s, i: (i, 0))
    shard = pl.BlockSpec((1, tm, fs2), lambda s, i: (s, i, 0))
    grads = []
    for xa, ya, which in ((dg, hb, "w_gate"), (du, hb, "w_up"), (act, dyb, "w_down")):
        g = tn_matmul(xa, [ya], shard, [row], s2, fs2, [d], t, tm, name + "_" + which, dep)
        g = g.reshape(2 * s2, fs2 // 2, d)
        if each is not None:
            dep = each(which, g)
        grads.append(g)
    return grads


def ffn_weight_grads_one_call(hb, dyb, dg, du, act, name, dep):
    t, d = hb.shape
    ns, _, fs = dg.shape
    steps = [(m, s) for m in range(3) for s in range(ns)]

    def body(hb_hbm, dyb_hbm, dg_hbm, du_hbm, act_hbm, dep_ref, gwg_hbm, gwu_hbm, gwd_hbm,
             y_buf, x_buf, o_buf, y_sems, x_sems, o_sems):
        srcs = (dg_hbm, du_hbm, act_hbm)
        dsts = (gwg_hbm, gwu_hbm, gwd_hbm)
        y_copies = [pltpu.make_async_copy(hb_hbm, y_buf.at[0], y_sems.at[0]),
                    pltpu.make_async_copy(dyb_hbm, y_buf.at[1], y_sems.at[1])]

        def x_copy(j):
            m, s = steps[j]
            return pltpu.make_async_copy(srcs[m].at[s], x_buf.at[j % 2], x_sems.at[j % 2])

        x_copy(0).start()
        y_copies[0].start()
        y_copies[1].start()
        out_copies = [None, None]
        for j, (m, s) in enumerate(steps):
            if j + 1 < len(steps):
                x_copy(j + 1).start()
            x_copy(j).wait()
            if j == 0:
                y_copies[0].wait()
            if (m, s) == (2, 0):
                y_copies[1].wait()
            if out_copies[j % 2] is not None:
                out_copies[j % 2].wait()
            o_buf[j % 2] = _dot_tn(x_buf[j % 2], y_buf[1 if m == 2 else 0]).astype(bf16)
            out_copies[j % 2] = pltpu.make_async_copy(o_buf.at[j % 2], dsts[m].at[s], o_sems.at[j % 2])
            out_copies[j % 2].start()
        for cp in out_copies:
            cp.wait()

    outs = pl.pallas_call(
        body, name=name,
        in_specs=[_ANY] * 6, out_specs=[_ANY] * 3,
        out_shape=[jax.ShapeDtypeStruct((ns, fs, d), bf16)] * 3,
        scratch_shapes=[pltpu.VMEM((2, t, d), bf16), pltpu.VMEM((2, t, fs), bf16), pltpu.VMEM((2, fs, d), bf16),
                        pltpu.SemaphoreType.DMA((2,)), pltpu.SemaphoreType.DMA((2,)), pltpu.SemaphoreType.DMA((2,))],
        compiler_params=_cparams(VMEM_LIMIT_WIDE),
    )(hb, dyb, dg, du, act, dep)
    return [g.reshape(2 * ns, fs // 2, d) for g in outs]


def tn_matmuls_one_call(x_shared, xs, ys, out_shapes, place, name, dep):
    n = len(ys)
    t, kx = (x_shared if x_shared is not None else xs[0]).shape
    ky = ys[0].shape[1]
    nx = 0 if x_shared is not None else n
    nout = len(out_shapes)

    def body(*refs):
        if x_shared is not None:
            xsh_hbm, refs = refs[0], refs[1:]
        x_hbm, y_hbm = refs[:nx], refs[nx:nx + n]
        out_hbm = refs[nx + n + 1:nx + n + 1 + nout]
        x_buf, y_buf, o_buf, x_sems, y_sems, o_sems = refs[nx + n + 1 + nout:]

        def loads(j):
            cps = [pltpu.make_async_copy(y_hbm[j], y_buf.at[j % 2], y_sems.at[j % 2])]
            if x_shared is None:
                cps.append(pltpu.make_async_copy(x_hbm[j], x_buf.at[j % 2], x_sems.at[j % 2]))
            return cps

        if x_shared is not None:
            once = pltpu.make_async_copy(xsh_hbm, x_buf.at[0], x_sems.at[0])
            once.start()
        for cp in loads(0):
            cp.start()
        out_copies = [None, None]
        for j in range(n):
            if j + 1 < n:
                for cp in loads(j + 1):
                    cp.start()
            for cp in loads(j):
                cp.wait()
            if x_shared is not None and j == 0:
                once.wait()
            if out_copies[j % 2] is not None:
                out_copies[j % 2].wait()
            xb = x_buf[0] if x_shared is not None else x_buf[j % 2]
            o_buf[j % 2] = _dot_tn(xb, y_buf[j % 2]).astype(bf16)
            o, idx = place(j)
            out_copies[j % 2] = pltpu.make_async_copy(o_buf.at[j % 2], out_hbm[o].at[idx], o_sems.at[j % 2])
            out_copies[j % 2].start()
        for cp in out_copies:
            if cp is not None:
                cp.wait()

    operands = ([x_shared] if x_shared is not None else list(xs)) + list(ys) + [dep]
    outs = pl.pallas_call(
        body, name=name,
        in_specs=[_ANY] * len(operands), out_specs=[_ANY] * nout,
        out_shape=[jax.ShapeDtypeStruct(s, bf16) for s in out_shapes],
        scratch_shapes=[pltpu.VMEM((1 if x_shared is not None else 2, t, kx), bf16), pltpu.VMEM((2, t, ky), bf16),
                        pltpu.VMEM((2, kx, ky), bf16),
                        pltpu.SemaphoreType.DMA((2,)), pltpu.SemaphoreType.DMA((2,)), pltpu.SemaphoreType.DMA((2,))],
        compiler_params=_cparams(VMEM_LIMIT_WIDE),
    )(*operands)
    return list(outs)


def inproj_fwd(x1, ng, win, bin4, cos, sin, sng, snb, ws, bs):
    t, d = x1.shape
    s4, _, w2 = win.shape
    tm = _row_tile(t)
    dk = d // RET_HEADS
    scale = dk ** -0.5

    def body(x_ref, ng_ref, w_ref, b_ref, cos_ref, sin_ref, sng_ref, snb_ref, ws_ref, bs_ref, p_ref, hb_ref, a_ref):
        y, _, _ = _rms(x_ref[...], ng_ref[...])
        h = y.astype(bf16)
        hb_ref[...] = h
        uv = None
        for s in range(s4):
            p = _dot(h, w_ref[s]) + b_ref[s]
            if s == 0:
                uv = p.astype(bf16)
                p_ref[s] = uv
            elif s != 1:
                p_ref[s] = p.astype(bf16)
            else:
                cs, sn = cos_ref[...], sin_ref[...]
                for e in range(2 * RET_HEADS):
                    cols = slice(e * dk, (e + 1) * dk)
                    rot = _rot(p[:, cols], cs, sn)
                    p_ref[s, :, cols] = (rot if e < RET_HEADS else rot * scale).astype(bf16)
        _sgu_rows(uv[:, 0:d].astype(f32), uv[:, d:w2].astype(f32), sng_ref, snb_ref, ws_ref, bs_ref, a_ref)

    tab = pl.BlockSpec((tm, dk // 2), lambda i: (i, 0))
    row = pl.BlockSpec((tm, d), lambda i: (i, 0))
    vec = pl.BlockSpec((1, d), lambda i: (0, 0))
    return pl.pallas_call(
        body, name="inproj_fwd", grid=(t // tm,),
        in_specs=[row, vec, pl.BlockSpec((s4, d, w2), lambda i: (0, 0, 0), pipeline_mode=pl.Buffered(1)),
                  pl.BlockSpec((s4, 1, w2), lambda i: (0, 0, 0)), tab, tab, vec, vec,
                  pl.BlockSpec((SGU_GROUPS, SGU_CHUNK, SGU_CHUNK), lambda i: (0, 0, 0)),
                  pl.BlockSpec((SGU_GROUPS, SGU_CHUNK, 1), lambda i: (0, 0, 0))],
        out_specs=[pl.BlockSpec((s4, tm, w2), lambda i: (0, i, 0)), row, row],
        out_shape=[jax.ShapeDtypeStruct((s4, t, w2), bf16), jax.ShapeDtypeStruct((t, d), bf16),
                   jax.ShapeDtypeStruct((t, d), bf16)],
        compiler_params=_cparams(),
    )(x1, ng, win, bin4, cos, sin, sng, snb, ws, bs)


def _sgu_norm(va, ng, nb):
    gv = _gelu(va)
    mu = jnp.mean(gv, axis=-1, keepdims=True)
    xc = gv - mu
    rstd = lax.rsqrt(jnp.mean(xc * xc, axis=-1, keepdims=True) + NORM_EPS)
    xh = xc * rstd
    return xh, rstd, (xh * ng + nb).astype(bf16)


def _sgu_rows(ua, va, ng_ref, nb_ref, ws_ref, bs_ref, a_ref):
    tm, d = ua.shape
    gd = d // SGU_GROUPS
    gu = _gelu(ua)
    _, _, vn = _sgu_norm(va, ng_ref[...], nb_ref[...])
    for c in range(tm // SGU_CHUNK):
        rows = slice(c * SGU_CHUNK, (c + 1) * SGU_CHUNK)
        for g in range(SGU_GROUPS):
            cols = slice(g * gd, (g + 1) * gd)
            sg = _dot(ws_ref[g], vn[rows, cols]) + bs_ref[g]
            a_ref[rows, cols] = (gu[rows, cols] * sg).astype(bf16)


def _sgu_bwd_rows(dad, ua, va, ng_ref, nb_ref, ws_ref, bs_ref, dua_ref, dva_ref, dws_ref, dbs_ref, dng_ref, dnb_ref,
                  dvn_scr):
    tm, d = ua.shape
    gd = d // SGU_GROUPS
    gu = _gelu(ua)
    xh, rstd, vn = _sgu_norm(va, ng_ref[...], nb_ref[...])
    dsb = (dad * gu).astype(bf16)
    for c in range(tm // SGU_CHUNK):
        rows = slice(c * SGU_CHUNK, (c + 1) * SGU_CHUNK)
        for g in range(SGU_GROUPS):
            cols = slice(g * gd, (g + 1) * gd)
            sg = _dot(ws_ref[g], vn[rows, cols]) + bs_ref[g]
            dua_ref[rows, cols] = (dad[rows, cols] * sg * _dgelu(ua[rows, cols])).astype(bf16)
            ds = dsb[rows, cols]
            dvn_scr[rows, cols] = _dot_tn(ws_ref[g], ds)
            dws_ref[g] += _dot_nt(ds, vn[rows, cols])
            dbs_ref[g] += jnp.sum(ds.astype(f32), axis=1, keepdims=True)
    dvn = dvn_scr[...]
    dng_ref[...] += jnp.sum(dvn * xh, axis=0, keepdims=True)
    dnb_ref[...] += jnp.sum(dvn, axis=0, keepdims=True)
    dxh = dvn * ng_ref[...]
    dgv = rstd * (dxh - jnp.mean(dxh, axis=-1, keepdims=True) - xh * jnp.mean(dxh * xh, axis=-1, keepdims=True))
    dva_ref[...] = (dgv * _dgelu(va)).astype(bf16)


def retention_constants(decay_logit, t, dk, zero):
    lg = jax.nn.log_sigmoid(decay_logit.astype(f32) + zero)
    lgf = lg[0][:, None]
    lgb = lg[1][:, None]
    idx = jnp.arange(CHUNK, dtype=f32)[None, :]
    af = jnp.exp((idx + 1.0) * lgf)
    ab = jnp.exp((CHUNK - idx) * lgb)
    kf = jnp.exp((CHUNK - 1.0 - idx) * lgf)
    kb = jnp.exp(idx * lgb)
    cols = jnp.stack([af, ab, kf, kb, af * (idx + 1.0), ab * (CHUNK - idx), kf * (CHUNK - 1.0 - idx), kb * idx], axis=1)
    cols = cols[..., None]
    diff = idx[0][:, None] - idx[0][None, :]
    dfm = jnp.where(diff >= 0, jnp.exp(jnp.maximum(diff, 0.0)[None] * lgf[:, :, None]), 0.0)
    dbm = jnp.where(diff < 0, jnp.exp(jnp.maximum(-diff, 0.0)[None] * lgb[:, :, None]), 0.0)
    mats = jnp.stack([dfm + dbm, dfm * diff[None], dbm * (-diff)[None]], axis=1)
    cdec = jnp.stack([jnp.broadcast_to(jnp.exp(CHUNK * lgf), (RET_HEADS, dk)),
                      jnp.broadcast_to(jnp.exp(CHUNK * lgb), (RET_HEADS, dk))], axis=1)
    theta = ROPE_BASE ** (-jnp.arange(0, dk, 2, dtype=f32) / dk)
    ang = (jnp.arange(t, dtype=f32) + zero)[:, None] * theta[None, :]
    return cols, mats, cdec, jnp.cos(ang), jnp.sin(ang)


def _rot(tr, cos, sin):
    half = tr.shape[-1] // 2
    t1 = tr[:, :half]
    t2 = tr[:, half:]
    return jnp.concatenate([t1 * cos - t2 * sin, t2 * cos + t1 * sin], axis=-1)


def _rot_inv(dt, cos, sin):
    half = dt.shape[-1] // 2
    d1 = dt[:, :half]
    d2 = dt[:, half:]
    return jnp.concatenate([d1 * cos + d2 * sin, d2 * cos - d1 * sin], axis=-1)


def _ret_tile(t):
    return 2048 if t >= 4096 else _row_tile(t)


def _ret_specs(t, d, dk, rt):
    nr = t // rt
    hq = d // dk

    def blk(p, n):
        return (1 - p) * (nr - 1 - n) + p * n

    q_spec = pl.BlockSpec((1, rt, dk), lambda h, p, n: (1, blk(p, n), h))
    k_spec = pl.BlockSpec((1, rt, dk), lambda h, p, n: (1, blk(p, n), hq + h))
    v_spec = pl.BlockSpec((1, rt, dk), lambda h, p, n: (2, blk(p, n), h))
    g_spec = pl.BlockSpec((1, rt, dk), lambda h, p, n: (2, blk(p, n), hq + h))
    tab_spec = pl.BlockSpec((rt, dk // 2), lambda h, p, n: (blk(p, n), 0))
    cols_spec = pl.BlockSpec((1, 8, CHUNK, 1), lambda h, p, n: (h, 0, 0, 0))
    mats_spec = pl.BlockSpec((1, 3, CHUNK, CHUNK), lambda h, p, n: (h, 0, 0, 0))
    cdec_spec = pl.BlockSpec((1, 2, dk), lambda h, p, n: (h, 0, 0))
    in_row = pl.BlockSpec((rt, dk), lambda h, p, n: (blk(p, n), h))
    out_row = pl.BlockSpec((rt, dk), lambda h, p, n: (p * n, h))
    return nr, blk, q_spec, k_spec, v_spec, g_spec, tab_spec, cols_spec, mats_spec, cdec_spec, in_row, out_row


def ret_fwd(proj, cols, mats, cdec, dep):
    _, t, w2 = proj.shape
    d = w2 // 2
    dk = d // RET_HEADS
    rt = _ret_tile(t)
    cpt = rt // CHUNK
    nr, blk, q_spec, k_spec, v_spec, g_spec, _, cols_spec, mats_spec, cdec_spec, _, out_row = _ret_specs(t, d, dk, rt)

    def body(q_ref, k_ref, v_ref, g_ref, cols_ref, mats_ref, cdec_ref, dep_ref, r_ref, rn_ref, sb_scr, st):
        p = pl.program_id(1)
        n = pl.program_id(2)
        af, ab, kf, kb = cols_ref[0, 0], cols_ref[0, 1], cols_ref[0, 2], cols_ref[0, 3]
        cf = cdec_ref[0, 0:1, :]
        cb = cdec_ref[0, 1:2, :]

        @pl.when(n == 0)
        def _():
            st[...] = jnp.zeros_like(st)

        @pl.when(p == 0)
        def _():
            for j in reversed(range(cpt)):
                rows = slice(j * CHUNK, (j + 1) * CHUNK)
                ch = blk(p, n) * cpt + j
                kk = k_ref[0, rows, :].astype(f32)
                sb_scr[ch] = st[...].astype(bf16)
                st[...] = st[...] * cb + _dot_tn((kk * kb).astype(bf16), v_ref[0, rows, :])

        @pl.when(p == 1)
        def _():
            for j in range(cpt):
                rows = slice(j * CHUNK, (j + 1) * CHUNK)
                ch = blk(p, n) * cpt + j
                qb = q_ref[0, rows, :]
                kkb = k_ref[0, rows, :]
                q = qb.astype(f32)
                kk = kkb.astype(f32)
                v = v_ref[0, rows, :]
                pm = (_dot_nt(qb, kkb) * mats_ref[0, 0]).astype(bf16)
                out = (_dot(pm, v) + _dot((q * af).astype(bf16), st[...].astype(bf16))
                       + _dot((q * ab).astype(bf16), sb_scr[ch]))
                st[...] = st[...] * cf + _dot_tn((kk * kf).astype(bf16), v)
                rhat = out * lax.rsqrt(jnp.mean(out * out, axis=-1, keepdims=True) + NORM_EPS)
                gg = g_ref[0, rows, :].astype(f32)
                r_ref[rows, :] = out.astype(bf16)
                rn_ref[rows, :] = (rhat * gg * _sigmoid(gg)).astype(bf16)

    return pl.pallas_call(
        body, name="ret_fwd", grid=(RET_HEADS, 2, nr),
        in_specs=[q_spec, k_spec, v_spec, g_spec, cols_spec, mats_spec, cdec_spec, _ANY],
        out_specs=[out_row, out_row],
        out_shape=[jax.ShapeDtypeStruct((t, d), bf16), jax.ShapeDtypeStruct((t, d), bf16)],
        scratch_shapes=[pltpu.VMEM((t // CHUNK, dk, dk), bf16), pltpu.VMEM((dk, dk), f32)],
        compiler_params=_cparams(),
    )(proj, proj, proj, proj, cols, mats, cdec, dep)


def ret_bwd(drn, r, proj, cols, mats, cdec, cos, sin):
    _, t, w2 = proj.shape
    d = w2 // 2
    dk = d // RET_HEADS
    rt = _ret_tile(t)
    cpt = rt // CHUNK
    nr, blk, q_spec, k_spec, v_spec, g_spec, tab_spec, cols_spec, mats_spec, cdec_spec, in_row, out_row = _ret_specs(t, d, dk, rt)
    scale = dk ** -0.5

    def body(drn_ref, r_ref, q_ref, k_ref, v_ref, g_ref, cos_ref, sin_ref, cols_ref, mats_ref, cdec_ref,
             dq_ref, dk_ref, dv_ref, dg_ref, dlg_ref,
             sb_scr, gf_scr, st_s, st_g, acc_af, acc_ab, acc_vf, acc_vb, acc_sf, acc_sb, dout_scr, dgr_scr):
        p = pl.program_id(1)
        n = pl.program_id(2)
        af, ab, kf, kb = cols_ref[0, 0], cols_ref[0, 1], cols_ref[0, 2], cols_ref[0, 3]
        af1, ab1, kf1, kb1 = cols_ref[0, 4], cols_ref[0, 5], cols_ref[0, 6], cols_ref[0, 7]
        cf = cdec_ref[0, 0:1, :]
        cb = cdec_ref[0, 1:2, :]

        @pl.when(n == 0)
        def _():
            st_s[...] = jnp.zeros_like(st_s)
            st_g[...] = jnp.zeros_like(st_g)

        @pl.when(jnp.logical_and(n == 0, p == 1))
        def _():
            for a in (acc_af, acc_ab, acc_vf, acc_vb, acc_sf, acc_sb):
                a[...] = jnp.zeros_like(a)

        def load(rows):
            cs, sn = cos_ref[rows, :], sin_ref[rows, :]
            q = q_ref[0, rows, :].astype(f32)
            kk = k_ref[0, rows, :].astype(f32)
            rr = r_ref[rows, :].astype(f32)
            rstd = lax.rsqrt(jnp.mean(rr * rr, axis=-1, keepdims=True) + NORM_EPS)
            rhat = rr * rstd
            gg = g_ref[0, rows, :].astype(f32)
            sg = _sigmoid(gg)
            dd = drn_ref[rows, :].astype(f32)
            drhat = dd * gg * sg
            dout = rstd * (drhat - rhat * jnp.mean(drhat * rhat, axis=-1, keepdims=True))
            dgr = dd * rhat * _dsilu(gg, sg)
            return q, kk, dout.astype(bf16), dgr, cs, sn

        @pl.when(p == 0)
        def _():
            for j in reversed(range(cpt)):
                rows = slice(j * CHUNK, (j + 1) * CHUNK)
                ch = blk(p, n) * cpt + j
                q, kk, doutb, dgr, _, _ = load(rows)
                kept = pl.ds(pl.multiple_of(ch * CHUNK, CHUNK), CHUNK)
                dout_scr[kept, :] = doutb
                dgr_scr[kept, :] = dgr.astype(bf16)
                sb_scr[ch] = st_s[...].astype(bf16)
                gf_scr[ch] = st_g[...].astype(bf16)
                st_s[...] = st_s[...] * cb + _dot_tn((kk * kb).astype(bf16), v_ref[0, rows, :])
                st_g[...] = st_g[...] * cf + _dot_tn((q * af).astype(bf16), doutb)

        @pl.when(p == 1)
        def _():
            for j in range(cpt):
                rows = slice(j * CHUNK, (j + 1) * CHUNK)
                ch = blk(p, n) * cpt + j
                kept = pl.ds(pl.multiple_of(ch * CHUNK, CHUNK), CHUNK)
                doutb = dout_scr[kept, :]
                cs, sn = cos_ref[rows, :], sin_ref[rows, :]
                v = v_ref[0, rows, :]
                qb = q_ref[0, rows, :]
                kkb = k_ref[0, rows, :]
                q = qb.astype(f32)
                kk = kkb.astype(f32)
                sf = st_s[...]
                gb = st_g[...]
                sfb = sf.astype(bf16)
                gbb = gb.astype(bf16)
                sbb = sb_scr[ch]
                gfb = gf_scr[ch]
                dmat = mats_ref[0, 0]
                scores = _dot_nt(qb, kkb)
                dpraw = _dot_nt(doutb, v)
                dpb = (dpraw * dmat).astype(bf16)
                pmb = (scores * dmat).astype(bf16)
                x1 = _dot_nt(doutb, sfb)
                x2 = _dot_nt(doutb, sbb)
                y1 = _dot_nt(v, gfb)
                y2 = _dot_nt(v, gbb)
                kdf = (kk * kf).astype(bf16)
                kdb = (kk * kb).astype(bf16)
                dq = _dot(dpb, kkb) + x1 * af + x2 * ab
                dkk = _dot_tn(dpb, qb) + y1 * kf + y2 * kb
                dv = _dot_tn(pmb, doutb) + _dot(kdf, gfb) + _dot(kdb, gbb)
                ps = dpraw * scores
                acc_af[...] += ps * mats_ref[0, 1]
                acc_ab[...] += ps * mats_ref[0, 2]
                acc_vf[...] += x1 * q * af1 + y1 * kk * kf1
                acc_vb[...] += x2 * q * ab1 + y2 * kk * kb1
                acc_sf[...] += gfb.astype(f32) * sf
                acc_sb[...] += gb * sbb.astype(f32)
                st_s[...] = sf * cf + _dot_tn(kdf, v)
                st_g[...] = gb * cb + _dot_tn((q * ab).astype(bf16), doutb)
                dq_ref[rows, :] = _rot_inv(dq, cs, sn).astype(bf16)
                dk_ref[rows, :] = (_rot_inv(dkk, cs, sn) * scale).astype(bf16)
                dv_ref[rows, :] = dv.astype(bf16)
                dg_ref[rows, :] = dgr_scr[kept, :]

        @pl.when(jnp.logical_and(p == 1, n == nr - 1))
        def _():
            tf = jnp.sum(acc_af[...]) + jnp.sum(acc_vf[...]) + CHUNK * jnp.sum(acc_sf[...] * cf)
            tb = jnp.sum(acc_ab[...]) + jnp.sum(acc_vb[...]) + CHUNK * jnp.sum(acc_sb[...] * cb)
            rid = lax.broadcasted_iota(jnp.int32, (8, 128), 0)
            dlg_ref[0] = jnp.where(rid == 0, tf, jnp.where(rid == 1, tb, 0.0))

    nch = t // CHUNK
    return pl.pallas_call(
        body, name="ret_bwd", grid=(RET_HEADS, 2, nr),
        in_specs=[in_row, in_row, q_spec, k_spec, v_spec, g_spec, tab_spec, tab_spec, cols_spec, mats_spec, cdec_spec],
        out_specs=[out_row, out_row, out_row, out_row, pl.BlockSpec((1, 8, 128), lambda h, p, n: (h, 0, 0))],
        out_shape=[jax.ShapeDtypeStruct((t, d), bf16)] * 4 + [jax.ShapeDtypeStruct((RET_HEADS, 8, 128), f32)],
        scratch_shapes=[pltpu.VMEM((nch, dk, dk), bf16), pltpu.VMEM((nch, dk, dk), bf16),
                        pltpu.VMEM((dk, dk), f32), pltpu.VMEM((dk, dk), f32),
                        pltpu.VMEM((CHUNK, CHUNK), f32), pltpu.VMEM((CHUNK, CHUNK), f32),
                        pltpu.VMEM((CHUNK, dk), f32), pltpu.VMEM((CHUNK, dk), f32),
                        pltpu.VMEM((dk, dk), f32), pltpu.VMEM((dk, dk), f32),
                        pltpu.VMEM((t, dk), bf16), pltpu.VMEM((t, dk), bf16)],
        compiler_params=_cparams(VMEM_LIMIT_WIDE),
    )(drn, r, proj, proj, proj, proj, cos, sin, cols, mats, cdec)


def mix_fwd(a, rn, proj, wa, wb, wo, x1):
    t, d = x1.shape
    tm = _row_tile(t)

    def body(a_ref, rn_ref, p_ref, wa_ref, wb_ref, wo_ref, x_ref, xo_ref, ba_ref, br_ref):
        ba = _dot(a_ref[...], wa_ref[...])
        br = _dot(rn_ref[...], wb_ref[...])
        sa = _sigmoid(p_ref[0, :, 0:d].astype(f32))
        sb = _sigmoid(p_ref[0, :, d:2 * d].astype(f32))
        mix = (sa * ba + sb * br).astype(bf16)
        xo_ref[...] = x_ref[...] + _dot(mix, wo_ref[...])
        ba_ref[...] = ba.astype(bf16)
        br_ref[...] = br.astype(bf16)

    row = pl.BlockSpec((tm, d), lambda i: (i, 0))
    wsp = pl.BlockSpec((d, d), lambda i: (0, 0))
    return pl.pallas_call(
        body, name="mix_fwd", grid=(t // tm,),
        in_specs=[row, row, pl.BlockSpec((1, tm, 2 * d), lambda i: (3, i, 0)), wsp, wsp, wsp, row],
        out_specs=[row, row, row],
        out_shape=[jax.ShapeDtypeStruct((t, d), f32), jax.ShapeDtypeStruct((t, d), bf16), jax.ShapeDtypeStruct((t, d), bf16)],
        compiler_params=_cparams(),
    )(a, rn, proj, wa, wb, wo, x1)


def mix_bwd_act(dx2, ba, br, proj, wa, wb, wo, sng, snb, ws, bs, dep):
    t, d = dx2.shape
    tm = _row_tile(t)

    def body(dx_ref, ba_ref, br_ref, p_ref, uv_ref, wa_ref, wb_ref, wo_ref, sng_ref, snb_ref, ws_ref, bs_ref, dep_ref,
             drn_ref, dga_ref, dgb_ref, mix_ref, dba_ref, dbr_ref, dxb_ref,
             dua_ref, dva_ref, dws_ref, dbs_ref, dng_ref, dnb_ref, dvn_scr):
        _zero_at_first_step(dws_ref, dbs_ref, dng_ref, dnb_ref)
        dxb = dx_ref[...].astype(bf16)
        dxb_ref[...] = dxb
        dmix = _dot_nt(dxb, wo_ref[...])
        ba = ba_ref[...].astype(f32)
        br = br_ref[...].astype(f32)
        sa = _sigmoid(p_ref[0, :, 0:d].astype(f32))
        sb = _sigmoid(p_ref[0, :, d:2 * d].astype(f32))
        mix_ref[...] = (sa * ba + sb * br).astype(bf16)
        dba = (dmix * sa).astype(bf16)
        dbr = (dmix * sb).astype(bf16)
        dba_ref[...] = dba
        dbr_ref[...] = dbr
        dga_ref[...] = (dmix * ba * sa * (1.0 - sa)).astype(bf16)
        dgb_ref[...] = (dmix * br * sb * (1.0 - sb)).astype(bf16)
        drn_ref[...] = _dot_nt(dbr, wb_ref[...]).astype(bf16)
        da = _dot_nt(dba, wa_ref[...])
        _sgu_bwd_rows(da, uv_ref[0, :, 0:d].astype(f32), uv_ref[0, :, d:2 * d].astype(f32), sng_ref, snb_ref, ws_ref,
                      bs_ref, dua_ref, dva_ref, dws_ref, dbs_ref, dng_ref, dnb_ref, dvn_scr)

    row = pl.BlockSpec((tm, d), lambda i: (i, 0))
    vec = pl.BlockSpec((1, d), lambda i: (0, 0))
    wsp = pl.BlockSpec((d, d), lambda i: (0, 0))
    sws = pl.BlockSpec((SGU_GROUPS, SGU_CHUNK, SGU_CHUNK), lambda i: (0, 0, 0))
    sbs = pl.BlockSpec((SGU_GROUPS, SGU_CHUNK, 1), lambda i: (0, 0, 0))
    return pl.pallas_call(
        body, name="mix_bwd_act", grid=(t // tm,),
        in_specs=[row, row, row, pl.BlockSpec((1, tm, 2 * d), lambda i: (3, i, 0)),
                  pl.BlockSpec((1, tm, 2 * d), lambda i: (0, i, 0)), wsp, wsp, wsp, vec, vec, sws, sbs, _ANY],
        out_specs=[row] * 9 + [sws, sbs, vec, vec],
        out_shape=[jax.ShapeDtypeStruct((t, d), bf16)] * 9
        + [jax.ShapeDtypeStruct((SGU_GROUPS, SGU_CHUNK, SGU_CHUNK), f32), jax.ShapeDtypeStruct((SGU_GROUPS, SGU_CHUNK, 1), f32),
           jax.ShapeDtypeStruct((1, d), f32), jax.ShapeDtypeStruct((1, d), f32)],
        scratch_shapes=[pltpu.VMEM((tm, d), f32)],
        compiler_params=_cparams(VMEM_LIMIT_WIDE),
    )(dx2, ba, br, proj, proj, wa, wb, wo, sng, snb, ws, bs, dep)


def inproj_bwd_act(segs, win, x1, ng, dx2):
    t, d = x1.shape
    s4 = win.shape[0]
    tm = _row_tile(t)
    nseg = len(segs)

    def body(*refs):
        seg_refs = refs[:nseg]
        w_ref, x_ref, ng_ref, dx2_ref, dx1_ref, db_ref, dng_ref = refs[nseg:]
        _zero_at_first_step(db_ref, dng_ref)
        dh = None
        for e, sr in enumerate(seg_refs):
            sb = sr[...]
            part = _dot_nt(sb, w_ref[e // 2, :, (e % 2) * d:(e % 2 + 1) * d])
            dh = part if dh is None else dh + part
            db_ref[e] += jnp.sum(sb.astype(f32), axis=0, keepdims=True)
        _, xh, r = _rms(x_ref[...], ng_ref[...])
        dx1_ref[...] = dx2_ref[...] + _rms_bwd(dh, xh, r, ng_ref[...])
        dng_ref[...] += jnp.sum(dh * xh, axis=0, keepdims=True)

    row = pl.BlockSpec((tm, d), lambda i: (i, 0))
    vec = pl.BlockSpec((1, d), lambda i: (0, 0))
    return pl.pallas_call(
        body, name="inproj_bwd_act", grid=(t // tm,),
        in_specs=[row] * nseg + [pl.BlockSpec((s4, d, 2 * d), lambda i: (0, 0, 0), pipeline_mode=pl.Buffered(1)),
                                 row, vec, row],
        out_specs=[row, pl.BlockSpec((nseg, 1, d), lambda i: (0, 0, 0)), vec],
        out_shape=[jax.ShapeDtypeStruct((t, d), f32), jax.ShapeDtypeStruct((nseg, 1, d), f32),
                   jax.ShapeDtypeStruct((1, d), f32)],
        compiler_params=_cparams(VMEM_LIMIT_WIDE),
    )(*segs, win, x1, ng, dx2)


def _place():
    return lax.axis_index("x"), lax.axis_index("y"), lax.axis_index("c")


def _other_chips(x, y):
    return [(1 - x, y), (x, 1 - y), (1 - x, 1 - y)]


_ANY = pl.BlockSpec(memory_space=pl.ANY)


_HBM = pl.BlockSpec(memory_space=pltpu.HBM)
_SEM = pl.BlockSpec(memory_space=pltpu.SEMAPHORE)
_EFFECT = pltpu.SideEffectType.DATAFLOW_SIDE_EFFECTING


def _hbm(a):
    return pltpu.with_memory_space_constraint(a, pltpu.HBM)


def _half_rows(ref, c):
    half = ref.shape[1] // 2
    return pl.ds(pl.multiple_of(c * half, 16), half)


def _chip_copy(src, dst, send_sem, recv_sem, chip, c):
    return pltpu.make_async_remote_copy(src_ref=src, dst_ref=dst, send_sem=send_sem, recv_sem=recv_sem,
                                        device_id=(chip[0], chip[1], c), device_id_type=MESH)


def gather_start(bufs, groups, name):
    nb, ng = len(bufs), len(groups)

    def body(*refs):
        ins = refs[:nb]
        sems = refs[nb:nb + 2 * ng]
        token = refs[-1]
        x, y, c = _place()
        k = 2 * x + y
        for gi, grp in enumerate(groups):
            for wi, w in enumerate(grp):
                mine = ins[w].at[k, _half_rows(ins[w], c)]
                for j, chip in enumerate(_other_chips(x, y)):
                    _chip_copy(mine, mine, sems[2 * gi].at[3 * wi + j], sems[2 * gi + 1].at[3 * wi + j], chip, c).start()
        token[...] = jnp.zeros_like(token)

    sem_shapes = []
    for grp in groups:
        sem_shapes += [pltpu.SemaphoreType.DMA((3 * len(grp),)), pltpu.SemaphoreType.DMA((3 * len(grp),))]
    outs = pl.pallas_call(
        body, name=name,
        out_shape=sem_shapes + [pltpu.HBM(b.shape, b.dtype) for b in bufs] + [jax.ShapeDtypeStruct((8, 128), f32)],
        in_specs=[_HBM] * nb,
        out_specs=[_SEM] * (2 * ng) + [_HBM] * nb + [pl.BlockSpec(memory_space=pltpu.VMEM)],
        input_output_aliases={w: 2 * ng + w for w in range(nb)},
        compiler_params=pltpu.CompilerParams(has_side_effects=_EFFECT),
    )(*[_hbm(b) for b in bufs])
    sems = [(outs[2 * gi], outs[2 * gi + 1]) for gi in range(ng)]
    return sems, list(outs[2 * ng:2 * ng + nb]), outs[-1]


def gather_wait(bufs, sems, after, name):
    n = len(bufs)

    def body(*refs):
        ins = refs[:n]
        send_sems, recv_sems = refs[n], refs[n + 1]
        x, y, c = _place()
        k = 2 * x + y
        for wi in range(n):
            half = _half_rows(ins[wi], c)
            for j, chip in enumerate(_other_chips(x, y)):
                cp = _chip_copy(ins[wi].at[k, half], ins[wi].at[2 * chip[0] + chip[1], half], send_sems.at[3 * wi + j],
                                recv_sems.at[3 * wi + j], chip, c)
                cp.wait_send()
                cp.wait_recv()

    outs = pl.pallas_call(
        body, name=name,
        out_shape=[pltpu.HBM(b.shape, b.dtype) for b in bufs],
        in_specs=[_HBM] * n + [_SEM, _SEM, _ANY],
        out_specs=[_HBM] * n,
        input_output_aliases={i: i for i in range(n)},
        compiler_params=pltpu.CompilerParams(has_side_effects=_EFFECT),
    )(*bufs, sems[0], sems[1], after)
    return list(outs)


def gather_forward(bufs, name):
    n = len(bufs)

    def body(*refs):
        ins = refs[n:2 * n]
        send_sems, recv_sems = refs[2 * n], refs[2 * n + 1]
        x, y, c = _place()
        copies = []
        for wi in range(n):
            for j, chip in enumerate(_other_chips(x, y)):
                kp = 2 * chip[0] + chip[1]
                got = ins[wi].at[kp, _half_rows(ins[wi], c)]
                cp = pltpu.make_async_remote_copy(
                    src_ref=got, dst_ref=got, send_sem=send_sems.at[3 * wi + j], recv_sem=recv_sems.at[3 * wi + j],
                    device_id=(x, y, 1 - c), device_id_type=MESH)
                cp.start()
                copies.append((cp, wi, kp, j))
        for cp, wi, kp, j in copies:
            cp.wait_send()
            theirs = ins[wi].at[kp, _half_rows(ins[wi], 1 - c)]
            pltpu.make_async_remote_copy(
                src_ref=theirs, dst_ref=theirs, send_sem=send_sems.at[3 * wi + j], recv_sem=recv_sems.at[3 * wi + j],
                device_id=(x, y, 1 - c), device_id_type=MESH).wait_recv()

    outs = pl.pallas_call(
        body, name=name,
        out_shape=[jax.ShapeDtypeStruct(b.shape, b.dtype) for b in bufs],
        in_specs=[_ANY] * n, out_specs=[_ANY] * n,
        input_output_aliases={i: i for i in range(n)},
        scratch_shapes=[pltpu.SemaphoreType.DMA((3 * n,)), pltpu.SemaphoreType.DMA((3 * n,))],
    )(*bufs)
    return list(outs)


def forward_start(bufs, name):
    n = len(bufs)

    def body(*refs):
        x, y, c = _place()
        for wi in range(n):
            for j, chip in enumerate(_other_chips(x, y)):
                got = refs[wi].at[2 * chip[0] + chip[1], _half_rows(refs[wi], c)]
                _sibling_copy(got, got, refs[n].at[3 * wi + j], refs[n + 1].at[3 * wi + j]).start()
        refs[-1][...] = jnp.zeros_like(refs[-1])

    return _split_start(body, name, 3 * n, list(bufs))


def forward_wait(bufs, sems, after, name):
    n = len(bufs)

    def body(*refs):
        x, y, c = _place()
        for wi in range(n):
            for j, chip in enumerate(_other_chips(x, y)):
                kp = 2 * chip[0] + chip[1]
                got = refs[wi].at[kp, _half_rows(refs[wi], c)]
                theirs = refs[wi].at[kp, _half_rows(refs[wi], 1 - c)]
                _sibling_copy(got, got, refs[n].at[3 * wi + j], refs[n + 1].at[3 * wi + j]).wait_send()
                _sibling_copy(theirs, theirs, refs[n].at[3 * wi + j], refs[n + 1].at[3 * wi + j]).wait_recv()

    return _split_wait(body, name, list(bufs), sems, after)


def exchange_start(grads, name):
    n = len(grads)
    lands = [lax.empty((3,) + g.shape[1:], g.dtype) for g in grads]

    def body(*refs):
        ins = refs[:n]
        land = refs[n:2 * n]
        send_sems, recv_sems = refs[2 * n], refs[2 * n + 1]
        token = refs[-1]
        x, y, c = _place()
        for wi in range(n):
            for j, chip in enumerate(_other_chips(x, y)):
                _chip_copy(ins[wi].at[2 * chip[0] + chip[1]], land[wi].at[j], send_sems.at[3 * wi + j],
                           recv_sems.at[3 * wi + j], chip, c).start()
        token[...] = jnp.zeros_like(token)

    outs = pl.pallas_call(
        body, name=name,
        out_shape=[pltpu.SemaphoreType.DMA((3 * n,)), pltpu.SemaphoreType.DMA((3 * n,))]
        + [pltpu.HBM(g.shape, g.dtype) for g in grads] + [pltpu.HBM(l.shape, l.dtype) for l in lands]
        + [jax.ShapeDtypeStruct((8, 128), f32)],
        in_specs=[_HBM] * (2 * n),
        out_specs=[_SEM, _SEM] + [_HBM] * (2 * n) + [pl.BlockSpec(memory_space=pltpu.VMEM)],
        input_output_aliases={i: 2 + i for i in range(2 * n)},
        compiler_params=pltpu.CompilerParams(has_side_effects=_EFFECT),
    )(*[_hbm(g) for g in grads], *[_hbm(l) for l in lands])
    return (outs[0], outs[1]), list(outs[2:2 + n]), list(outs[2 + n:2 + 2 * n]), outs[-1]


def exchange_wait(grads, lands, sems, after, name):
    n = len(grads)

    def body(*refs):
        ins = refs[:n]
        land = refs[n:2 * n]
        send_sems, recv_sems = refs[2 * n], refs[2 * n + 1]
        x, y, c = _place()
        for wi in range(n):
            for j, chip in enumerate(_other_chips(x, y)):
                cp = _chip_copy(ins[wi].at[2 * chip[0] + chip[1]], land[wi].at[j], send_sems.at[3 * wi + j],
                                recv_sems.at[3 * wi + j], chip, c)
                cp.wait_send()
                cp.wait_recv()

    outs = pl.pallas_call(
        body, name=name,
        out_shape=[pltpu.HBM(g.shape, g.dtype) for g in grads] + [pltpu.HBM(l.shape, l.dtype) for l in lands],
        in_specs=[_HBM] * (2 * n) + [_SEM, _SEM, _ANY],
        out_specs=[_HBM] * (2 * n),
        input_output_aliases={i: i for i in range(2 * n)},
        compiler_params=pltpu.CompilerParams(has_side_effects=_EFFECT),
    )(*grads, *lands, sems[0], sems[1], after)
    return list(outs[:n]), list(outs[n:])


def _split_start(body, name, n_sems, operands):
    n = len(operands)
    outs = pl.pallas_call(
        body, name=name,
        out_shape=[pltpu.SemaphoreType.DMA((n_sems,)), pltpu.SemaphoreType.DMA((n_sems,))]
        + [pltpu.HBM(o.shape, o.dtype) for o in operands] + [jax.ShapeDtypeStruct((8, 128), f32)],
        in_specs=[_HBM] * n,
        out_specs=[_SEM, _SEM] + [_HBM] * n + [pl.BlockSpec(memory_space=pltpu.VMEM)],
        input_output_aliases={i: 2 + i for i in range(n)},
        compiler_params=pltpu.CompilerParams(has_side_effects=_EFFECT),
    )(*[_hbm(o) for o in operands])
    return (outs[0], outs[1]), list(outs[2:2 + n]), outs[-1]


def _split_wait(body, name, operands, sems, after):
    n = len(operands)
    outs = pl.pallas_call(
        body, name=name,
        out_shape=[pltpu.HBM(o.shape, o.dtype) for o in operands],
        in_specs=[_HBM] * n + [_SEM, _SEM, _ANY],
        out_specs=[_HBM] * n,
        input_output_aliases={i: i for i in range(n)},
        compiler_params=pltpu.CompilerParams(has_side_effects=_EFFECT),
    )(*operands, sems[0], sems[1], after)
    return list(outs)


def _sibling_copy(src, dst, send_sem, recv_sem):
    x, y, c = _place()
    return pltpu.make_async_remote_copy(src_ref=src, dst_ref=dst, send_sem=send_sem, recv_sem=recv_sem,
                                        device_id=(x, y, 1 - c), device_id_type=MESH)


def swap_start(parts, name):
    n = len(parts)

    def body(*refs):
        for w in range(n):
            _sibling_copy(refs[w], refs[n + w], refs[2 * n].at[w], refs[2 * n + 1].at[w]).start()
        refs[-1][...] = jnp.zeros_like(refs[-1])

    sems, ops, token = _split_start(body, name, n, list(parts) + [lax.empty(p.shape, p.dtype) for p in parts])
    return sems, ops[:n], ops[n:], token


def swap_wait(parts, lands, sems, after, name):
    n = len(parts)

    def body(*refs):
        for w in range(n):
            cp = _sibling_copy(refs[w], refs[n + w], refs[2 * n].at[w], refs[2 * n + 1].at[w])
            cp.wait_send()
            cp.wait_recv()

    outs = _split_wait(body, name, list(parts) + list(lands), sems, after)
    return outs[:n], outs[n:]


def _all_peers(x, y, c):
    return [(1 - x if m & 4 else x, 1 - y if m & 2 else y, 1 - c if m & 1 else c) for m in range(1, N_DEV)]


def small_start(block):
    land = jnp.broadcast_to(block[None], (N_DEV,) + block.shape)

    def body(b_ref, land_ref, send_sems, recv_sems, b_thru, land_thru, token):
        x, y, c = _place()
        me = 4 * x + 2 * y + c
        for m, peer in enumerate(_all_peers(x, y, c)):
            pltpu.make_async_remote_copy(src_ref=b_ref, dst_ref=land_ref.at[me], send_sem=send_sems.at[m],
                                         recv_sem=recv_sems.at[m], device_id=peer, device_id_type=MESH).start()
        token[...] = jnp.zeros_like(token)

    sems, ops, token = _split_start(body, "small_start", N_DEV - 1, [block, land])
    return sems, ops[0], ops[1], token


def small_wait(block, land, sems, after):
    def body(b_ref, land_ref, send_sems, recv_sems, after_ref, b_thru, land_thru):
        x, y, c = _place()
        for m, (px, py, pc) in enumerate(_all_peers(x, y, c)):
            cp = pltpu.make_async_remote_copy(src_ref=b_ref, dst_ref=land_ref.at[4 * px + 2 * py + pc],
                                              send_sem=send_sems.at[m], recv_sem=recv_sems.at[m],
                                              device_id=(px, py, pc), device_id_type=MESH)
            cp.wait_send()
            cp.wait_recv()

    return _split_wait(body, "small_wait", [block, land], sems, after)[1]


def _adamw(w, g, m, v):
    m = ADAM_B1 * m + (1.0 - ADAM_B1) * g
    v = ADAM_B2 * v + (1.0 - ADAM_B2) * (g * g)
    m_hat = m / (1.0 - ADAM_B1 ** ADAM_STEP)
    v_hat = v / (1.0 - ADAM_B2 ** ADAM_STEP)
    delta = -ADAM_LR * (m_hat / (jnp.sqrt(v_hat) + ADAM_EPS) + ADAM_WD * w)
    return delta, m, v


EW_BLOCK_BYTES = 2 * 1024 * 1024


def _ew_tile(rows, cols):
    for cand in (512, 352, 256, 176, 128, 64, 32, 16, 8):
        if rows % cand == 0 and cand * cols * 4 <= EW_BLOCK_BYTES:
            return cand
    return rows


def sum_partials(chip, own, land, name):
    _, r, c = own.shape
    tr = _ew_tile(r, c)

    def body(k_ref, own_ref, p_ref, o_ref):
        o_ref[...] = ((own_ref[0].astype(f32) + p_ref[0].astype(f32)) + p_ref[1].astype(f32)) + p_ref[2].astype(f32)

    return pl.pallas_call(
        body, name=name,
        grid_spec=pltpu.PrefetchScalarGridSpec(
            num_scalar_prefetch=1, grid=(r // tr,),
            in_specs=[pl.BlockSpec((1, tr, c), lambda i, k: (k[0], i, 0)), pl.BlockSpec((3, tr, c), lambda i, k: (0, i, 0))],
            out_specs=pl.BlockSpec((tr, c), lambda i, k: (i, 0))),
        out_shape=jax.ShapeDtypeStruct((r, c), f32),
        compiler_params=_cparams(),
    )(chip, own, land)


def adamw_shard(p_mine, p_sibling, w, m, v, name):
    r, c = w.shape
    tr = _ew_tile(r, c)

    def body(a_ref, b_ref, w_ref, m_ref, v_ref, g_ref, d_ref, mo_ref, vo_ref):
        g = a_ref[...] + b_ref[...]
        delta, mn, vn = _adamw(w_ref[...], g, m_ref[...], v_ref[...])
        g_ref[...] = g
        d_ref[...] = delta
        mo_ref[...] = mn
        vo_ref[...] = vn

    blk = pl.BlockSpec((tr, c), lambda i: (i, 0))
    return pl.pallas_call(
        body, name=name, grid=(r // tr,),
        in_specs=[blk] * 5, out_specs=[blk] * 4,
        out_shape=[jax.ShapeDtypeStruct((r, c), f32)] * 4,
        compiler_params=_cparams(),
    )(p_mine, p_sibling, w, m, v)


def adamw_small(g8, w, m, v):
    _, r, lanes = g8.shape

    def body(g_ref, w_ref, m_ref, v_ref, go_ref, d_ref, mo_ref, vo_ref):
        g = g_ref[0]
        for i in range(1, N_DEV):
            g = g + g_ref[i]
        delta, mn, vn = _adamw(w_ref[...], g, m_ref[...], v_ref[...])
        go_ref[...] = g
        d_ref[...] = delta
        mo_ref[...] = mn
        vo_ref[...] = vn

    return pl.pallas_call(
        body, name="adamw_small",
        out_shape=[jax.ShapeDtypeStruct((r, lanes), f32)] * 4,
        compiler_params=_cparams(),
    )(g8, w, m, v)


def _size(shape):
    n = 1
    for e in shape:
        n *= e
    return n


def _pack_rows(shapes):
    rows = [-(-_size(s) // 1024) * 8 for s in shapes]
    return rows, sum(rows)


def _pack(arrs, shapes):
    rows, _ = _pack_rows(shapes)
    parts = [jnp.pad(a.reshape(-1).astype(f32), (0, r * 128 - _size(s))).reshape(r, 128)
             for a, s, r in zip(arrs, shapes, rows)]
    return jnp.concatenate(parts, axis=0)


def _unpack(block, shapes):
    rows, _ = _pack_rows(shapes)
    out, off = [], 0
    for s, r in zip(shapes, rows):
        out.append(block[off:off + r].reshape(-1)[:_size(s)].reshape(s))
        off += r
    return out


TRANSPOSED = ("ffn1_w_gate", "ffn1_w_up", "ffn2_w_gate", "ffn2_w_up")


def _shard2d(a, n):
    return a[0].T if n in TRANSPOSED else a[0]


def _unshard(a, n):
    return (a.T if n in TRANSPOSED else a)[None]


BIG = ("ffn1_w_gate", "ffn1_w_up", "ffn1_w_down", "w_in", "w_branch_a", "w_branch_b", "w_out",
       "ffn2_w_gate", "ffn2_w_up", "ffn2_w_down")
SMALL = ("ffn1_norm", "mix_norm", "b_in", "sgu_norm_g", "sgu_norm_b", "sgu_w_s", "sgu_b_s", "ret_decay_logit",
         "ffn2_norm", "final_norm")
WEIGHTS = ("ffn1_norm", "ffn1_w_gate", "ffn1_w_up", "ffn1_w_down", "mix_norm", "w_in", "b_in", "sgu_norm_g",
           "sgu_norm_b", "sgu_w_s", "sgu_b_s", "ret_decay_logit", "w_branch_a", "w_branch_b", "w_out", "ffn2_norm",
           "ffn2_w_gate", "ffn2_w_up", "ffn2_w_down", "final_norm")


def kernel(x, ffn1_norm, ffn1_w_gate, ffn1_w_up, ffn1_w_down, mix_norm, w_in, b_in, sgu_norm_g, sgu_norm_b, sgu_w_s, sgu_b_s, ret_decay_logit, w_branch_a, w_branch_b, w_out, ffn2_norm, ffn2_w_gate, ffn2_w_up, ffn2_w_down, final_norm, loss_target, m_ffn1_norm, m_ffn1_w_gate, m_ffn1_w_up, m_ffn1_w_down, m_mix_norm, m_w_in, m_b_in, m_sgu_norm_g, m_sgu_norm_b, m_sgu_w_s, m_sgu_b_s, m_ret_decay_logit, m_w_branch_a, m_w_branch_b, m_w_out, m_ffn2_norm, m_ffn2_w_gate, m_ffn2_w_up, m_ffn2_w_down, m_final_norm, v_ffn1_norm, v_ffn1_w_gate, v_ffn1_w_up, v_ffn1_w_down, v_mix_norm, v_w_in, v_b_in, v_sgu_norm_g, v_sgu_norm_b, v_sgu_w_s, v_sgu_b_s, v_ret_decay_logit, v_w_branch_a, v_w_branch_b, v_w_out, v_ffn2_norm, v_ffn2_w_gate, v_ffn2_w_up, v_ffn2_w_down, v_final_norm):
    p = dict(ffn1_norm=ffn1_norm, ffn1_w_gate=ffn1_w_gate, ffn1_w_up=ffn1_w_up, ffn1_w_down=ffn1_w_down,
             mix_norm=mix_norm, w_in=w_in, b_in=b_in, sgu_norm_g=sgu_norm_g, sgu_norm_b=sgu_norm_b, sgu_w_s=sgu_w_s,
             sgu_b_s=sgu_b_s, ret_decay_logit=ret_decay_logit, w_branch_a=w_branch_a, w_branch_b=w_branch_b,
             w_out=w_out, ffn2_norm=ffn2_norm, ffn2_w_gate=ffn2_w_gate, ffn2_w_up=ffn2_w_up, ffn2_w_down=ffn2_w_down,
             final_norm=final_norm)
    mom = dict(ffn1_norm=m_ffn1_norm, ffn1_w_gate=m_ffn1_w_gate, ffn1_w_up=m_ffn1_w_up, ffn1_w_down=m_ffn1_w_down,
               mix_norm=m_mix_norm, w_in=m_w_in, b_in=m_b_in, sgu_norm_g=m_sgu_norm_g, sgu_norm_b=m_sgu_norm_b,
               sgu_w_s=m_sgu_w_s, sgu_b_s=m_sgu_b_s, ret_decay_logit=m_ret_decay_logit, w_branch_a=m_w_branch_a,
               w_branch_b=m_w_branch_b, w_out=m_w_out, ffn2_norm=m_ffn2_norm, ffn2_w_gate=m_ffn2_w_gate,
               ffn2_w_up=m_ffn2_w_up, ffn2_w_down=m_ffn2_w_down, final_norm=m_final_norm)
    var = dict(ffn1_norm=v_ffn1_norm, ffn1_w_gate=v_ffn1_w_gate, ffn1_w_up=v_ffn1_w_up, ffn1_w_down=v_ffn1_w_down,
               mix_norm=v_mix_norm, w_in=v_w_in, b_in=v_b_in, sgu_norm_g=v_sgu_norm_g, sgu_norm_b=v_sgu_norm_b,
               sgu_w_s=v_sgu_w_s, sgu_b_s=v_sgu_b_s, ret_decay_logit=v_ret_decay_logit, w_branch_a=v_w_branch_a,
               w_branch_b=v_w_branch_b, w_out=v_w_out, ffn2_norm=v_ffn2_norm, ffn2_w_gate=v_ffn2_w_gate,
               ffn2_w_up=v_ffn2_w_up, ffn2_w_down=v_ffn2_w_down, final_norm=v_final_norm)

    xs = x[0]
    tgt = loss_target[0]
    t, d = xs.shape
    dk = d // RET_HEADS

    shards2d = {n: _shard2d(p[n], n) for n in BIG}
    chip = (2 * lax.axis_index("x") + lax.axis_index("y")).astype(jnp.int32).reshape(1)
    groups = {"ffn1": ("ffn1_w_gate", "ffn1_w_up", "ffn1_w_down"), "in": ("w_in",),
              "mix": ("w_branch_a", "w_branch_b", "w_out"), "ffn2": ("ffn2_w_gate", "ffn2_w_up", "ffn2_w_down")}
    def own_slot(n, zero):
        sh = shards2d[n].astype(bf16) + zero
        return lax.dynamic_update_index_in_dim(lax.empty((N_CHIPS,) + sh.shape, bf16), sh, chip[0], 0)

    sems, bufs, tok = gather_start([own_slot(n, jnp.zeros((), bf16)) for n in groups["ffn1"]], [[0, 1, 2]],
                                   "gather_start_ffn1")
    gsem = {"ffn1": sems[0]}
    pending = dict(zip(groups["ffn1"], bufs))
    rest = [n for g in ("in", "mix", "ffn2") for n in groups[g]]
    sems, bufs, tok_rest = gather_start([own_slot(n, tok[0, 0].astype(bf16)) for n in rest],
                                 [[rest.index(n) for n in groups[g]] for g in ("in", "mix", "ffn2")], "gather_start_rest")
    gsem.update(zip(("in", "mix", "ffn2"), sems))
    pending.update(zip(rest, bufs))

    def arrive(gs, after):
        got = []
        for g in gs:
            got += gather_wait([pending[n] for n in groups[g]], gsem[g], after, "gather_wait_" + g)
        return gather_forward(got, "gather_forward_" + gs[0])

    bin4 = b_in.reshape(N_CHIPS, 1, 2 * d)
    ws_b = sgu_w_s[0].astype(bf16)
    bs_c = sgu_b_s[0][:, :, None]
    cols, mats, cdec, cos, sin = retention_constants(ret_decay_logit[0], t, dk, tok_rest[0, 0])

    wg1, wu1, wd1 = [_pair_shards(w) for w in arrive(["ffn1"], cos)]
    x1, g1, u1 = ffn_fwd(xs, ffn1_norm, wg1, wu1, wd1, "ffn1_fwd")
    win, = arrive(["in"], x1)
    proj, hb2, a = inproj_fwd(x1, mix_norm, win, bin4, cos, sin, sgu_norm_g, sgu_norm_b, ws_b, bs_c)
    late = []
    for g in ("mix", "ffn2"):
        late += gather_wait([pending[n] for n in groups[g]], gsem[g], proj, "gather_wait_" + g)
    fsems, late, ftok = forward_start(late, "forward_start_mix")
    r, rn = ret_fwd(proj, cols, mats, cdec, ftok)
    wa, wb, wo, wg2, wu2, wd2 = forward_wait(late, fsems, rn, "forward_wait_mix")
    wa, wb, wo = [w.reshape(d, d) for w in (wa, wb, wo)]
    wg2, wu2, wd2 = [_pair_shards(w) for w in (wg2, wu2, wd2)]
    x2, ba, br = mix_fwd(a, rn, proj, wa, wb, wo, x1)
    loss_blk, dx3, d_final, g2, u2 = ffn_fwd_loss(x2, ffn2_norm, wg2, wu2, wd2, final_norm.reshape(1, d), tgt, "ffn2_fwd")

    sent, swaps = {}, {}
    out_g, out_d, out_m, out_v = {}, {}, {}, {}

    def reduce_plane(g, after):
        gsems, own, lands, _ = sent[g]
        own, lands = exchange_wait(own, lands, gsems, after, "exchange_wait_" + g)
        plane = [sum_partials(chip, o, l, "sum_" + n) for n, o, l in zip(groups[g], own, lands)]
        swaps[g] = swap_start(plane, "swap_start_" + g)
        return swaps[g][3]

    def update(g, after):
        ssems, plane, lands, _ = swaps[g]
        plane, other = swap_wait(plane, lands, ssems, after, "swap_wait_" + g)
        for n, mine, sib in zip(groups[g], plane, other):
            res = adamw_shard(mine, sib, shards2d[n], _shard2d(mom[n], n), _shard2d(var[n], n), "adamw_" + n)
            out_g[n], out_d[n], out_m[n], out_v[n] = [_unshard(o, n) for o in res]
        return res[0]

    dx2, dg2, du2, act2, hb3, dyb2, d_ffn2n = ffn_bwd_act(dx3, x2, ffn2_norm, g2, u2, wg2, wu2, wd2, "ffn2_bwd_act", tok)
    sent["ffn2"] = exchange_start(ffn_weight_grads_one_call(hb3, dyb2, dg2, du2, act2, "ffn2_grad", tok),
                                  "exchange_start_ffn2")
    drn, dga, dgb, mixb, dba, dbr, dx2b, dua, dva, d_ws, d_bs, d_sng, d_snb = mix_bwd_act(
        dx2, ba, br, proj, wa, wb, wo, sgu_norm_g, sgu_norm_b, ws_b, bs_c, sent["ffn2"][3])
    g_mix = [g.reshape(N_CHIPS, d // N_CHIPS, d) for g in tn_matmuls_one_call(
        None, [a, rn, mixb], [dba, dbr, dx2b], [(d, d)] * 3, lambda j: (j, (slice(None), slice(None))), "grad_mix", tok)]
    dq, dkr, dv, dgr, dlg = ret_bwd(drn, r, proj, cols, mats, cdec, cos, sin)
    segs = [dua, dva, dq, dkr, dv, dgr, dga, dgb]
    dx1, d_bin, d_mixn = inproj_bwd_act(segs, win, x1, mix_norm, dx2)
    g_in, = tn_matmuls_one_call(hb2, None, segs, [(N_CHIPS, d, 2 * d)],
                                lambda j: (0, (j // 2, slice(None), pl.ds((j % 2) * d, d))), "grad_w_in", tok)
    groups["mix_in"] = groups["mix"] + groups["in"]
    sent["mix_in"] = exchange_start(g_mix + [g_in], "exchange_start_mix_in")
    grad_x, dg1, du1, act1, hb1, dyb1, d_ffn1n = ffn_bwd_act(dx1, xs, ffn1_norm, g1, u1, wg1, wu1, wd1, "ffn1_bwd_act",
                                                              sent["mix_in"][3])
    dlogit = dlg[:, 0:2, 0].T * jax.nn.sigmoid(-ret_decay_logit[0].astype(f32))
    small_g = dict(ffn1_norm=d_ffn1n, mix_norm=d_mixn, b_in=d_bin, sgu_norm_g=d_sng, sgu_norm_b=d_snb, sgu_w_s=d_ws,
                   sgu_b_s=d_bs, ret_decay_logit=dlogit, ffn2_norm=d_ffn2n, final_norm=d_final)
    shapes = [p[n].shape for n in SMALL] + [(1,)]
    small_sems, small_blk, small_land, small_tok = small_start(
        _pack([small_g[n] for n in SMALL] + [loss_blk[0, 0:1]], shapes))

    def send_one(which, grad):
        n = "ffn1_" + which
        groups[n] = (n,)
        sent[n] = exchange_start([grad], "exchange_start_" + n)
        return sent[n][3]

    ffn_weight_grads(hb1, dyb1, dg1, du1, act1, "ffn1_grad", small_tok, send_one)

    after = reduce_plane("ffn2", sent["ffn1_w_down"][3])
    after = reduce_plane("mix_in", after)
    after = update("ffn2", after)
    g8 = small_wait(small_blk, small_land, small_sems, after)
    no_state = [jnp.zeros((1,), f32)]
    sg, sd, sm, sv = adamw_small(g8, _pack([p[n] for n in SMALL] + no_state, shapes),
                                 _pack([mom[n] for n in SMALL] + no_state, shapes),
                                 _pack([var[n] for n in SMALL] + no_state, shapes))
    for res, blockv in ((out_g, sg), (out_d, sd), (out_m, sm), (out_v, sv)):
        for n, val in zip(SMALL, _unpack(blockv, shapes)):
            res[n] = val
    loss = _unpack(sg, shapes)[-1][0]
    after = update("mix_in", sg)
    after = reduce_plane("ffn1_w_gate", after)
    after = reduce_plane("ffn1_w_up", after)
    after = update("ffn1_w_gate", after)
    after = reduce_plane("ffn1_w_down", after)
    after = update("ffn1_w_up", after)
    update("ffn1_w_down", after)

    return (loss, grad_x[None], *[out_g[n] for n in WEIGHTS], *[out_d[n] for n in WEIGHTS],
            *[out_m[n] for n in WEIGHTS], *[out_v[n] for n in WEIGHTS])
```

```python
import jax
import jax.numpy as jnp
from jax import lax
from jax.experimental import pallas as pl
from jax.experimental.pallas import tpu as pltpu

f32 = jnp.float32
bf16 = jnp.bfloat16

SGU_CHUNK = 128
CHUNK = 256
RET_HEADS = 4
SGU_GROUPS = 4
ROPE_BASE = 10000.0
NORM_EPS = 1e-6
ADAM_LR = 0.001
ADAM_B1 = 0.9
ADAM_B2 = 0.999
ADAM_EPS = 1e-08
ADAM_WD = 0.01
ADAM_STEP = 10
N_CHIPS = 4
N_DEV = 8
MESH = pl.DeviceIdType.MESH
VMEM_LIMIT = 52 * 1024 * 1024
VMEM_LIMIT_WIDE = 62 * 1024 * 1024

_NT = (((1,), (1,)), ((), ()))
_TN = (((0,), (0,)), ((), ()))


def _cparams(limit=None):
    return pltpu.CompilerParams(vmem_limit_bytes=VMEM_LIMIT if limit is None else limit)


def _row_tile(t):
    return 512 if t >= 2048 else t // 2


def _dot(a, b):
    return jnp.dot(a, b, preferred_element_type=f32)


def _dot_nt(a, b):
    return lax.dot_general(a, b, _NT, preferred_element_type=f32)


def _dot_tn(a, b):
    return lax.dot_general(a, b, _TN, preferred_element_type=f32)


def _rms(x, g):
    r = lax.rsqrt(jnp.mean(x * x, axis=-1, keepdims=True) + NORM_EPS)
    xh = x * r
    return xh * g, xh, r


def _rms_bwd(dy, xh, r, g):
    dxh = dy * g
    return r * (dxh - xh * jnp.mean(dxh * xh, axis=-1, keepdims=True))


def _sigmoid(x):
    return jax.nn.sigmoid(x)


def _dsilu(g, sg):
    return sg * (1.0 + g * (1.0 - sg))


def _gelu(x):
    return 0.5 * x * (1.0 + lax.erf(x * 0.7071067811865476))


def _dgelu(x):
    return 0.5 * (1.0 + lax.erf(x * 0.7071067811865476)) + x * jnp.exp(-0.5 * x * x) * 0.3989422804014327


def _zero_at_first_step(*refs):
    @pl.when(pl.program_id(0) == 0)
    def _():
        for ref in refs:
            ref[...] = jnp.zeros_like(ref)


def _ffn_tile(t):
    return 256 if t >= 2048 else t // 2


def _ffn_fwd_rows(xx, ng_ref, wg_ref, wu_ref, wd_ref, g_ref, u_ref):
    y, _, _ = _rms(xx, ng_ref[...])
    h = y.astype(bf16)
    acc = None
    for s in range(wg_ref.shape[0]):
        g = _dot_nt(h, wg_ref[s])
        u = _dot_nt(h, wu_ref[s])
        g_ref[s] = g.astype(bf16)
        u_ref[s] = u.astype(bf16)
        part = _dot((g * _sigmoid(g) * u).astype(bf16), wd_ref[s])
        acc = part if acc is None else acc + part
    return xx + 0.5 * acc


def ffn_fwd(x, ng, wg, wu, wd, name):
    t, d = x.shape
    ns, fs, _ = wg.shape
    tm = _ffn_tile(t)

    def body(x_ref, ng_ref, wg_ref, wu_ref, wd_ref, xo_ref, g_ref, u_ref):
        xo_ref[...] = _ffn_fwd_rows(x_ref[...], ng_ref, wg_ref, wu_ref, wd_ref, g_ref, u_ref)

    row = pl.BlockSpec((tm, d), lambda i: (i, 0))
    shard = pl.BlockSpec((ns, tm, fs), lambda i: (0, i, 0))
    wspec = pl.BlockSpec((ns, fs, d), lambda i: (0, 0, 0), pipeline_mode=pl.Buffered(1))
    return pl.pallas_call(
        body, name=name, grid=(t // tm,),
        in_specs=[row, pl.BlockSpec((1, d), lambda i: (0, 0)), wspec, wspec, wspec],
        out_specs=[row, shard, shard],
        out_shape=[jax.ShapeDtypeStruct((t, d), f32), jax.ShapeDtypeStruct((ns, t, fs), bf16),
                   jax.ShapeDtypeStruct((ns, t, fs), bf16)],
        compiler_params=_cparams(),
    )(x, ng, wg, wu, wd)


def ffn_fwd_loss(x, ng, wg, wu, wd, fng, tgt, name):
    t, d = x.shape
    ns, fs, _ = wg.shape
    tm = _ffn_tile(t)

    def body(x_ref, ng_ref, wg_ref, wu_ref, wd_ref, fng_ref, t_ref, loss_ref, dx_ref, dfn_ref, g_ref, u_ref):
        _zero_at_first_step(loss_ref, dfn_ref)
        x3 = _ffn_fwd_rows(x_ref[...], ng_ref, wg_ref, wu_ref, wd_ref, g_ref, u_ref)
        y, xh, r = _rms(x3, fng_ref[...])
        diff = y - t_ref[...]
        part = 0.5 * jnp.sum(jnp.sum(diff * diff, axis=0, keepdims=True), axis=1, keepdims=True) / d
        loss_ref[...] += jnp.broadcast_to(part, (1, 128))
        dy = diff * (1.0 / d)
        dx_ref[...] = _rms_bwd(dy, xh, r, fng_ref[...])
        dfn_ref[...] += jnp.sum(dy * xh, axis=0, keepdims=True)

    row = pl.BlockSpec((tm, d), lambda i: (i, 0))
    vec = pl.BlockSpec((1, d), lambda i: (0, 0))
    shard = pl.BlockSpec((ns, tm, fs), lambda i: (0, i, 0))
    wspec = pl.BlockSpec((ns, fs, d), lambda i: (0, 0, 0), pipeline_mode=pl.Buffered(1))
    return pl.pallas_call(
        body, name=name, grid=(t // tm,),
        in_specs=[row, vec, wspec, wspec, wspec, vec, row],
        out_specs=[pl.BlockSpec((1, 128), lambda i: (0, 0)), row, vec, shard, shard],
        out_shape=[jax.ShapeDtypeStruct((1, 128), f32), jax.ShapeDtypeStruct((t, d), f32), jax.ShapeDtypeStruct((1, d), f32),
                   jax.ShapeDtypeStruct((ns, t, fs), bf16), jax.ShapeDtypeStruct((ns, t, fs), bf16)],
        compiler_params=_cparams(),
    )(x, ng, wg, wu, wd, fng, tgt)


def ffn_bwd_act(dxo, x, ng, g, u, wg, wu, wd, name, dep):
    t, d = x.shape
    ns, fs, _ = wg.shape
    tm = _ffn_tile(t)

    def body(dxo_ref, x_ref, ng_ref, g_ref, u_ref, wg_ref, wu_ref, wd_ref, dep_ref,
             dx_ref, dg_ref, du_ref, act_ref, hb_ref, dyb_ref, dng_ref):
        _zero_at_first_step(dng_ref)
        dxo = dxo_ref[...]
        dyb = (0.5 * dxo).astype(bf16)
        dyb_ref[...] = dyb
        dh = None
        for s in range(ns):
            dact = _dot_nt(dyb, wd_ref[s])
            gg = g_ref[s].astype(f32)
            uu = u_ref[s].astype(f32)
            sg = _sigmoid(gg)
            sil = gg * sg
            dgb = (dact * uu * _dsilu(gg, sg)).astype(bf16)
            dub = (dact * sil).astype(bf16)
            dg_ref[s] = dgb
            du_ref[s] = dub
            act_ref[s] = (sil * uu).astype(bf16)
            part = _dot(dgb, wg_ref[s]) + _dot(dub, wu_ref[s])
            dh = part if dh is None else dh + part
        y, xh, r = _rms(x_ref[...], ng_ref[...])
        hb_ref[...] = y.astype(bf16)
        dx_ref[...] = dxo + _rms_bwd(dh, xh, r, ng_ref[...])
        dng_ref[...] += jnp.sum(dh * xh, axis=0, keepdims=True)

    row = pl.BlockSpec((tm, d), lambda i: (i, 0))
    shard = pl.BlockSpec((ns, tm, fs), lambda i: (0, i, 0))
    wspec = pl.BlockSpec((ns, fs, d), lambda i: (0, 0, 0), pipeline_mode=pl.Buffered(1))
    vec = pl.BlockSpec((1, d), lambda i: (0, 0))
    return pl.pallas_call(
        body, name=name, grid=(t // tm,),
        in_specs=[row, row, vec, shard, shard, wspec, wspec, wspec, _ANY],
        out_specs=[row, shard, shard, shard, row, row, vec],
        out_shape=[jax.ShapeDtypeStruct((t, d), f32)] + [jax.ShapeDtypeStruct((ns, t, fs), bf16)] * 3
        + [jax.ShapeDtypeStruct((t, d), bf16)] * 2 + [jax.ShapeDtypeStruct((1, d), f32)],
        compiler_params=_cparams(VMEM_LIMIT_WIDE),
    )(dxo, x, ng, g, u, wg, wu, wd, dep)


def tn_matmul(xs, ys, x_spec, y_specs, n_shards, k1, k2s, t, tm, name, dep):
    k2 = sum(k2s)
    ny = len(ys)

    def body(*refs):
        x_ref = refs[0]
        y_refs = refs[1:1 + ny]
        steps = t // tm
        o_ref, acc = (refs[-1], None) if steps == 1 else (refs[-2], refs[-1])
        i = pl.program_id(1)
        xb = x_ref[0] if len(x_ref.shape) == 3 else x_ref[...]
        if steps > 1:
            @pl.when(i == 0)
            def _():
                acc[...] = jnp.zeros_like(acc)

        off = 0
        for y_ref, w in zip(y_refs, k2s):
            yb = y_ref[0] if len(y_ref.shape) == 3 else y_ref[...]
            part = _dot_tn(xb, yb)
            if steps == 1:
                o_ref[0, :, off:off + w] = part.astype(bf16)
            else:
                acc[:, off:off + w] += part
            off += w

        if steps > 1:
            @pl.when(i == steps - 1)
            def _():
                o_ref[0] = acc[...].astype(bf16)

    return pl.pallas_call(
        body, name=name, grid=(n_shards, t // tm),
        in_specs=[x_spec] + list(y_specs) + [_ANY],
        out_specs=pl.BlockSpec((1, k1, k2), lambda s, i: (s, 0, 0)),
        out_shape=jax.ShapeDtypeStruct((n_shards, k1, k2), bf16),
        scratch_shapes=[pltpu.VMEM((k1, k2), f32)] if t // tm > 1 else [],
        compiler_params=_cparams(),
    )(xs, *ys, dep)


def _pair_shards(w):
    s4, fs, d = w.shape
    return w.reshape(s4 // 2, 2 * fs, d)


def ffn_weight_grads(hb, dyb, dg, du, act, name, dep, each=None):
    t, d = hb.shape
    s2, _, fs2 = dg.shape
    tm = t
    row = pl.BlockSpec((tm, d), lambda s, i: (i, 0))
    shard = pl.BlockSpec((1, tm, fs2), lambda s, i: (s, i, 0))
    grads = []
    for xa, ya, which in ((dg, hb, "w_gate"), (du, hb, "w_up"), (act, dyb, "w_down")):
        g = tn_matmul(xa, [ya], shard, [row], s2, fs2, [d], t, tm, name + "_" + which, dep)
        g = g.reshape(2 * s2, fs2 // 2, d)
        if each is not None:
            dep = each(which, g)
        grads.append(g)
    return grads


def ffn_weight_grads_one_call(hb, dyb, dg, du, act, name, dep):
    t, d = hb.shape
    ns, _, fs = dg.shape
    steps = [(m, s) for m in range(3) for s in range(ns)]

    def body(hb_hbm, dyb_hbm, dg_hbm, du_hbm, act_hbm, dep_ref, gwg_hbm, gwu_hbm, gwd_hbm,
             y_buf, x_buf, o_buf, y_sems, x_sems, o_sems):
        srcs = (dg_hbm, du_hbm, act_hbm)
        dsts = (gwg_hbm, gwu_hbm, gwd_hbm)
        y_copies = [pltpu.make_async_copy(hb_hbm, y_buf.at[0], y_sems.at[0]),
                    pltpu.make_async_copy(dyb_hbm, y_buf.at[1], y_sems.at[1])]

        def x_copy(j):
            m, s = steps[j]
            return pltpu.make_async_copy(srcs[m].at[s], x_buf.at[j % 2], x_sems.at[j % 2])

        x_copy(0).start()
        y_copies[0].start()
        y_copies[1].start()
        out_copies = [None, None]
        for j, (m, s) in enumerate(steps):
            if j + 1 < len(steps):
                x_copy(j + 1).start()
            x_copy(j).wait()
            if j == 0:
                y_copies[0].wait()
            if (m, s) == (2, 0):
                y_copies[1].wait()
            if out_copies[j % 2] is not None:
                out_copies[j % 2].wait()
            o_buf[j % 2] = _dot_tn(x_buf[j % 2], y_buf[1 if m == 2 else 0]).astype(bf16)
            out_copies[j % 2] = pltpu.make_async_copy(o_buf.at[j % 2], dsts[m].at[s], o_sems.at[j % 2])
            out_copies[j % 2].start()
        for cp in out_copies:
            cp.wait()

    outs = pl.pallas_call(
        body, name=name,
        in_specs=[_ANY] * 6, out_specs=[_ANY] * 3,
        out_shape=[jax.ShapeDtypeStruct((ns, fs, d), bf16)] * 3,
        scratch_shapes=[pltpu.VMEM((2, t, d), bf16), pltpu.VMEM((2, t, fs), bf16), pltpu.VMEM((2, fs, d), bf16),
                        pltpu.SemaphoreType.DMA((2,)), pltpu.SemaphoreType.DMA((2,)), pltpu.SemaphoreType.DMA((2,))],
        compiler_params=_cparams(VMEM_LIMIT_WIDE),
    )(hb, dyb, dg, du, act, dep)
    return [g.reshape(2 * ns, fs // 2, d) for g in outs]


def tn_matmuls_one_call(x_shared, xs, ys, out_shapes, place, name, dep):
    n = len(ys)
    t, kx = (x_shared if x_shared is not None else xs[0]).shape
    ky = ys[0].shape[1]
    nx = 0 if x_shared is not None else n
    nout = len(out_shapes)

    def body(*refs):
        if x_shared is not None:
            xsh_hbm, refs = refs[0], refs[1:]
        x_hbm, y_hbm = refs[:nx], refs[nx:nx + n]
        out_hbm = refs[nx + n + 1:nx + n + 1 + nout]
        x_buf, y_buf, o_buf, x_sems, y_sems, o_sems, first_sems = refs[nx + n + 1 + nout:]

        def loads(j):
            cps = [pltpu.make_async_copy(y_hbm[j], y_buf.at[j % 2], y_sems.at[j % 2])]
            if x_shared is None:
                cps.append(pltpu.make_async_copy(x_hbm[j], x_buf.at[j % 2], x_sems.at[j % 2]))
            return cps

        half = t // 2
        first = []
        for r in range(2):
            rows = pl.ds(r * half, half)
            x_src = xsh_hbm if x_shared is not None else x_hbm[0]
            first.append((rows, [pltpu.make_async_copy(x_src.at[rows], x_buf.at[0, rows], first_sems.at[2 * r]),
                                 pltpu.make_async_copy(y_hbm[0].at[rows], y_buf.at[0, rows], first_sems.at[2 * r + 1])]))
        for _, cps in first:
            for cp in cps:
                cp.start()
        out_copies = [None, None]
        for j in range(n):
            if j + 1 < n:
                for cp in loads(j + 1):
                    cp.start()
            if j == 0:
                res = None
                for rows, cps in first:
                    for cp in cps:
                        cp.wait()
                    part = _dot_tn(x_buf[0, rows], y_buf[0, rows])
                    res = part if res is None else res + part
            else:
                for cp in loads(j):
                    cp.wait()
                res = _dot_tn(x_buf[0] if x_shared is not None else x_buf[j % 2], y_buf[j % 2])
            if out_copies[j % 2] is not None:
                out_copies[j % 2].wait()
            o_buf[j % 2] = res.astype(bf16)
            o, idx = place(j)
            out_copies[j % 2] = pltpu.make_async_copy(o_buf.at[j % 2], out_hbm[o].at[idx], o_sems.at[j % 2])
            out_copies[j % 2].start()
        for cp in out_copies:
            if cp is not None:
                cp.wait()

    operands = ([x_shared] if x_shared is not None else list(xs)) + list(ys) + [dep]
    outs = pl.pallas_call(
        body, name=name,
        in_specs=[_ANY] * len(operands), out_specs=[_ANY] * nout,
        out_shape=[jax.ShapeDtypeStruct(s, bf16) for s in out_shapes],
        scratch_shapes=[pltpu.VMEM((1 if x_shared is not None else 2, t, kx), bf16), pltpu.VMEM((2, t, ky), bf16),
                        pltpu.VMEM((2, kx, ky), bf16),
                        pltpu.SemaphoreType.DMA((2,)), pltpu.SemaphoreType.DMA((2,)), pltpu.SemaphoreType.DMA((2,)),
                        pltpu.SemaphoreType.DMA((4,))],
        compiler_params=_cparams(VMEM_LIMIT_WIDE),
    )(*operands)
    return list(outs)


def inproj_fwd(x1, ng, win, bin4, cos, sin, sng, snb, ws, bs):
    t, d = x1.shape
    s4, _, w2 = win.shape
    tm = _row_tile(t)
    dk = d // RET_HEADS
    scale = dk ** -0.5

    def body(x_ref, ng_ref, w_ref, b_ref, cos_ref, sin_ref, sng_ref, snb_ref, ws_ref, bs_ref, p_ref, hb_ref, a_ref):
        y, _, _ = _rms(x_ref[...], ng_ref[...])
        h = y.astype(bf16)
        hb_ref[...] = h
        uv = None
        for s in range(s4):
            p = _dot(h, w_ref[s]) + b_ref[s]
            if s == 0:
                uv = p.astype(bf16)
                p_ref[s] = uv
            elif s != 1:
                p_ref[s] = p.astype(bf16)
            else:
                cs, sn = cos_ref[...], sin_ref[...]
                for e in range(2 * RET_HEADS):
                    cols = slice(e * dk, (e + 1) * dk)
                    rot = _rot(p[:, cols], cs, sn)
                    p_ref[s, :, cols] = (rot if e < RET_HEADS else rot * scale).astype(bf16)
        _sgu_rows(uv[:, 0:d].astype(f32), uv[:, d:w2].astype(f32), sng_ref, snb_ref, ws_ref, bs_ref, a_ref)

    tab = pl.BlockSpec((tm, dk // 2), lambda i: (i, 0))
    row = pl.BlockSpec((tm, d), lambda i: (i, 0))
    vec = pl.BlockSpec((1, d), lambda i: (0, 0))
    return pl.pallas_call(
        body, name="inproj_fwd", grid=(t // tm,),
        in_specs=[row, vec, pl.BlockSpec((s4, d, w2), lambda i: (0, 0, 0), pipeline_mode=pl.Buffered(1)),
                  pl.BlockSpec((s4, 1, w2), lambda i: (0, 0, 0)), tab, tab, vec, vec,
                  pl.BlockSpec((SGU_GROUPS, SGU_CHUNK, SGU_CHUNK), lambda i: (0, 0, 0)),
                  pl.BlockSpec((SGU_GROUPS, SGU_CHUNK, 1), lambda i: (0, 0, 0))],
        out_specs=[pl.BlockSpec((s4, tm, w2), lambda i: (0, i, 0)), row, row],
        out_shape=[jax.ShapeDtypeStruct((s4, t, w2), bf16), jax.ShapeDtypeStruct((t, d), bf16),
                   jax.ShapeDtypeStruct((t, d), bf16)],
        compiler_params=_cparams(),
    )(x1, ng, win, bin4, cos, sin, sng, snb, ws, bs)


def _sgu_norm(va, ng, nb):
    gv = _gelu(va)
    mu = jnp.mean(gv, axis=-1, keepdims=True)
    xc = gv - mu
    rstd = lax.rsqrt(jnp.mean(xc * xc, axis=-1, keepdims=True) + NORM_EPS)
    xh = xc * rstd
    return xh, rstd, (xh * ng + nb).astype(bf16)


def _sgu_rows(ua, va, ng_ref, nb_ref, ws_ref, bs_ref, a_ref):
    tm, d = ua.shape
    gd = d // SGU_GROUPS
    gu = _gelu(ua)
    _, _, vn = _sgu_norm(va, ng_ref[...], nb_ref[...])
    for c in range(tm // SGU_CHUNK):
        rows = slice(c * SGU_CHUNK, (c + 1) * SGU_CHUNK)
        for g in range(SGU_GROUPS):
            cols = slice(g * gd, (g + 1) * gd)
            sg = _dot(ws_ref[g], vn[rows, cols]) + bs_ref[g]
            a_ref[rows, cols] = (gu[rows, cols] * sg).astype(bf16)


def _sgu_bwd_rows(dad, ua, va, ng_ref, nb_ref, ws_ref, bs_ref, dua_ref, dva_ref, dws_ref, dbs_ref, dng_ref, dnb_ref,
                  dvn_scr):
    tm, d = ua.shape
    gd = d // SGU_GROUPS
    gu = _gelu(ua)
    xh, rstd, vn = _sgu_norm(va, ng_ref[...], nb_ref[...])
    dsb = (dad * gu).astype(bf16)
    for c in range(tm // SGU_CHUNK):
        rows = slice(c * SGU_CHUNK, (c + 1) * SGU_CHUNK)
        for g in range(SGU_GROUPS):
            cols = slice(g * gd, (g + 1) * gd)
            sg = _dot(ws_ref[g], vn[rows, cols]) + bs_ref[g]
            dua_ref[rows, cols] = (dad[rows, cols] * sg * _dgelu(ua[rows, cols])).astype(bf16)
            ds = dsb[rows, cols]
            dvn_scr[rows, cols] = _dot_tn(ws_ref[g], ds)
            dws_ref[g] += _dot_nt(ds, vn[rows, cols])
            dbs_ref[g] += jnp.sum(ds.astype(f32), axis=1, keepdims=True)
    dvn = dvn_scr[...]
    dng_ref[...] += jnp.sum(dvn * xh, axis=0, keepdims=True)
    dnb_ref[...] += jnp.sum(dvn, axis=0, keepdims=True)
    dxh = dvn * ng_ref[...]
    dgv = rstd * (dxh - jnp.mean(dxh, axis=-1, keepdims=True) - xh * jnp.mean(dxh * xh, axis=-1, keepdims=True))
    dva_ref[...] = (dgv * _dgelu(va)).astype(bf16)


def retention_constants(decay_logit, t, dk, zero):
    lg = jax.nn.log_sigmoid(decay_logit.astype(f32) + zero)
    lgf = lg[0][:, None]
    lgb = lg[1][:, None]
    idx = jnp.arange(CHUNK, dtype=f32)[None, :]
    af = jnp.exp((idx + 1.0) * lgf)
    ab = jnp.exp((CHUNK - idx) * lgb)
    kf = jnp.exp((CHUNK - 1.0 - idx) * lgf)
    kb = jnp.exp(idx * lgb)
    cols = jnp.stack([af, ab, kf, kb, af * (idx + 1.0), ab * (CHUNK - idx), kf * (CHUNK - 1.0 - idx), kb * idx], axis=1)
    cols = cols[..., None]
    diff = idx[0][:, None] - idx[0][None, :]
    dfm = jnp.where(diff >= 0, jnp.exp(jnp.maximum(diff, 0.0)[None] * lgf[:, :, None]), 0.0)
    dbm = jnp.where(diff < 0, jnp.exp(jnp.maximum(-diff, 0.0)[None] * lgb[:, :, None]), 0.0)
    mats = jnp.stack([dfm + dbm, dfm * diff[None], dbm * (-diff)[None]], axis=1)
    cdec = jnp.stack([jnp.broadcast_to(jnp.exp(CHUNK * lgf), (RET_HEADS, dk)),
                      jnp.broadcast_to(jnp.exp(CHUNK * lgb), (RET_HEADS, dk))], axis=1)
    theta = ROPE_BASE ** (-jnp.arange(0, dk, 2, dtype=f32) / dk)
    ang = (jnp.arange(t, dtype=f32) + zero)[:, None] * theta[None, :]
    return cols, mats, cdec, jnp.cos(ang), jnp.sin(ang)


def _rot(tr, cos, sin):
    half = tr.shape[-1] // 2
    t1 = tr[:, :half]
    t2 = tr[:, half:]
    return jnp.concatenate([t1 * cos - t2 * sin, t2 * cos + t1 * sin], axis=-1)


def _rot_inv(dt, cos, sin):
    half = dt.shape[-1] // 2
    d1 = dt[:, :half]
    d2 = dt[:, half:]
    return jnp.concatenate([d1 * cos + d2 * sin, d2 * cos - d1 * sin], axis=-1)


def _ret_tile(t):
    return 2048 if t >= 4096 else _row_tile(t)


def _ret_specs(t, d, dk, rt):
    nr = t // rt
    hq = d // dk

    def blk(p, n):
        return (1 - p) * (nr - 1 - n) + p * n

    q_spec = pl.BlockSpec((1, rt, dk), lambda h, p, n: (1, blk(p, n), h))
    k_spec = pl.BlockSpec((1, rt, dk), lambda h, p, n: (1, blk(p, n), hq + h))
    v_spec = pl.BlockSpec((1, rt, dk), lambda h, p, n: (2, blk(p, n), h))
    g_spec = pl.BlockSpec((1, rt, dk), lambda h, p, n: (2, blk(p, n), hq + h))
    tab_spec = pl.BlockSpec((rt, dk // 2), lambda h, p, n: (blk(p, n), 0))
    cols_spec = pl.BlockSpec((1, 8, CHUNK, 1), lambda h, p, n: (h, 0, 0, 0))
    mats_spec = pl.BlockSpec((1, 3, CHUNK, CHUNK), lambda h, p, n: (h, 0, 0, 0))
    cdec_spec = pl.BlockSpec((1, 2, dk), lambda h, p, n: (h, 0, 0))
    in_row = pl.BlockSpec((rt, dk), lambda h, p, n: (blk(p, n), h))
    out_row = pl.BlockSpec((rt, dk), lambda h, p, n: (p * n, h))
    return nr, blk, q_spec, k_spec, v_spec, g_spec, tab_spec, cols_spec, mats_spec, cdec_spec, in_row, out_row


def ret_fwd(proj, cols, mats, cdec, dep):
    _, t, w2 = proj.shape
    d = w2 // 2
    dk = d // RET_HEADS
    rt = _ret_tile(t)
    cpt = rt // CHUNK
    nr, blk, q_spec, k_spec, v_spec, g_spec, _, cols_spec, mats_spec, cdec_spec, _, out_row = _ret_specs(t, d, dk, rt)

    def body(q_ref, k_ref, v_ref, g_ref, cols_ref, mats_ref, cdec_ref, dep_ref, r_ref, rn_ref, sb_scr, st):
        p = pl.program_id(1)
        n = pl.program_id(2)
        af, ab, kf, kb = cols_ref[0, 0], cols_ref[0, 1], cols_ref[0, 2], cols_ref[0, 3]
        cf = cdec_ref[0, 0:1, :]
        cb = cdec_ref[0, 1:2, :]

        @pl.when(n == 0)
        def _():
            st[...] = jnp.zeros_like(st)

        @pl.when(p == 0)
        def _():
            for j in reversed(range(cpt)):
                rows = slice(j * CHUNK, (j + 1) * CHUNK)
                ch = blk(p, n) * cpt + j
                kk = k_ref[0, rows, :].astype(f32)
                sb_scr[ch] = st[...].astype(bf16)
                st[...] = st[...] * cb + _dot_tn((kk * kb).astype(bf16), v_ref[0, rows, :])

        @pl.when(p == 1)
        def _():
            for j in range(cpt):
                rows = slice(j * CHUNK, (j + 1) * CHUNK)
                ch = blk(p, n) * cpt + j
                qb = q_ref[0, rows, :]
                kkb = k_ref[0, rows, :]
                q = qb.astype(f32)
                kk = kkb.astype(f32)
                v = v_ref[0, rows, :]
                pm = (_dot_nt(qb, kkb) * mats_ref[0, 0]).astype(bf16)
                out = (_dot(pm, v) + _dot((q * af).astype(bf16), st[...].astype(bf16))
                       + _dot((q * ab).astype(bf16), sb_scr[ch]))
                st[...] = st[...] * cf + _dot_tn((kk * kf).astype(bf16), v)
                rhat = out * lax.rsqrt(jnp.mean(out * out, axis=-1, keepdims=True) + NORM_EPS)
                gg = g_ref[0, rows, :].astype(f32)
                r_ref[rows, :] = out.astype(bf16)
                rn_ref[rows, :] = (rhat * gg * _sigmoid(gg)).astype(bf16)

    return pl.pallas_call(
        body, name="ret_fwd", grid=(RET_HEADS, 2, nr),
        in_specs=[q_spec, k_spec, v_spec, g_spec, cols_spec, mats_spec, cdec_spec, _ANY],
        out_specs=[out_row, out_row],
        out_shape=[jax.ShapeDtypeStruct((t, d), bf16), jax.ShapeDtypeStruct((t, d), bf16)],
        scratch_shapes=[pltpu.VMEM((t // CHUNK, dk, dk), bf16), pltpu.VMEM((dk, dk), f32)],
        compiler_params=_cparams(),
    )(proj, proj, proj, proj, cols, mats, cdec, dep)


def ret_bwd(drn, r, proj, cols, mats, cdec, cos, sin):
    _, t, w2 = proj.shape
    d = w2 // 2
    dk = d // RET_HEADS
    rt = _ret_tile(t)
    cpt = rt // CHUNK
    nr, blk, q_spec, k_spec, v_spec, g_spec, tab_spec, cols_spec, mats_spec, cdec_spec, in_row, out_row = _ret_specs(t, d, dk, rt)
    scale = dk ** -0.5

    def body(drn_ref, r_ref, q_ref, k_ref, v_ref, g_ref, cos_ref, sin_ref, cols_ref, mats_ref, cdec_ref,
             dq_ref, dk_ref, dv_ref, dg_ref, dlg_ref,
             sb_scr, gf_scr, st_s, st_g, acc_af, acc_ab, acc_vf, acc_vb, acc_sf, acc_sb, dout_scr, dgr_scr):
        p = pl.program_id(1)
        n = pl.program_id(2)
        af, ab, kf, kb = cols_ref[0, 0], cols_ref[0, 1], cols_ref[0, 2], cols_ref[0, 3]
        af1, ab1, kf1, kb1 = cols_ref[0, 4], cols_ref[0, 5], cols_ref[0, 6], cols_ref[0, 7]
        cf = cdec_ref[0, 0:1, :]
        cb = cdec_ref[0, 1:2, :]

        @pl.when(n == 0)
        def _():
            st_s[...] = jnp.zeros_like(st_s)
            st_g[...] = jnp.zeros_like(st_g)

        @pl.when(jnp.logical_and(n == 0, p == 1))
        def _():
            for a in (acc_af, acc_ab, acc_vf, acc_vb, acc_sf, acc_sb):
                a[...] = jnp.zeros_like(a)

        def load(rows):
            cs, sn = cos_ref[rows, :], sin_ref[rows, :]
            q = q_ref[0, rows, :].astype(f32)
            kk = k_ref[0, rows, :].astype(f32)
            rr = r_ref[rows, :].astype(f32)
            rstd = lax.rsqrt(jnp.mean(rr * rr, axis=-1, keepdims=True) + NORM_EPS)
            rhat = rr * rstd
            gg = g_ref[0, rows, :].astype(f32)
            sg = _sigmoid(gg)
            dd = drn_ref[rows, :].astype(f32)
            drhat = dd * gg * sg
            dout = rstd * (drhat - rhat * jnp.mean(drhat * rhat, axis=-1, keepdims=True))
            dgr = dd * rhat * _dsilu(gg, sg)
            return q, kk, dout.astype(bf16), dgr, cs, sn

        @pl.when(p == 0)
        def _():
            for j in reversed(range(cpt)):
                rows = slice(j * CHUNK, (j + 1) * CHUNK)
                ch = blk(p, n) * cpt + j
                q, kk, doutb, dgr, _, _ = load(rows)
                kept = pl.ds(pl.multiple_of(ch * CHUNK, CHUNK), CHUNK)
                dout_scr[kept, :] = doutb
                dgr_scr[kept, :] = dgr.astype(bf16)
                sb_scr[ch] = st_s[...].astype(bf16)
                gf_scr[ch] = st_g[...].astype(bf16)
                st_s[...] = st_s[...] * cb + _dot_tn((kk * kb).astype(bf16), v_ref[0, rows, :])
                st_g[...] = st_g[...] * cf + _dot_tn((q * af).astype(bf16), doutb)

        @pl.when(p == 1)
        def _():
            for j in range(cpt):
                rows = slice(j * CHUNK, (j + 1) * CHUNK)
                ch = blk(p, n) * cpt + j
                kept = pl.ds(pl.multiple_of(ch * CHUNK, CHUNK), CHUNK)
                doutb = dout_scr[kept, :]
                cs, sn = cos_ref[rows, :], sin_ref[rows, :]
                v = v_ref[0, rows, :]
                qb = q_ref[0, rows, :]
                kkb = k_ref[0, rows, :]
                q = qb.astype(f32)
                kk = kkb.astype(f32)
                sf = st_s[...]
                gb = st_g[...]
                sfb = sf.astype(bf16)
                gbb = gb.astype(bf16)
                sbb = sb_scr[ch]
                gfb = gf_scr[ch]
                dmat = mats_ref[0, 0]
                scores = _dot_nt(qb, kkb)
                dpraw = _dot_nt(doutb, v)
                dpb = (dpraw * dmat).astype(bf16)
                pmb = (scores * dmat).astype(bf16)
                x1 = _dot_nt(doutb, sfb)
                x2 = _dot_nt(doutb, sbb)
                y1 = _dot_nt(v, gfb)
                y2 = _dot_nt(v, gbb)
                kdf = (kk * kf).astype(bf16)
                kdb = (kk * kb).astype(bf16)
                dq = _dot(dpb, kkb) + x1 * af + x2 * ab
                dkk = _dot_tn(dpb, qb) + y1 * kf + y2 * kb
                dv = _dot_tn(pmb, doutb) + _dot(kdf, gfb) + _dot(kdb, gbb)
                ps = dpraw * scores
                acc_af[...] += ps * mats_ref[0, 1]
                acc_ab[...] += ps * mats_ref[0, 2]
                acc_vf[...] += x1 * q * af1 + y1 * kk * kf1
                acc_vb[...] += x2 * q * ab1 + y2 * kk * kb1
                acc_sf[...] += gfb.astype(f32) * sf
                acc_sb[...] += gb * sbb.astype(f32)
                st_s[...] = sf * cf + _dot_tn(kdf, v)
                st_g[...] = gb * cb + _dot_tn((q * ab).astype(bf16), doutb)
                dq_ref[rows, :] = _rot_inv(dq, cs, sn).astype(bf16)
                dk_ref[rows, :] = (_rot_inv(dkk, cs, sn) * scale).astype(bf16)
                dv_ref[rows, :] = dv.astype(bf16)
                dg_ref[rows, :] = dgr_scr[kept, :]

        @pl.when(jnp.logical_and(p == 1, n == nr - 1))
        def _():
            tf = jnp.sum(acc_af[...]) + jnp.sum(acc_vf[...]) + CHUNK * jnp.sum(acc_sf[...] * cf)
            tb = jnp.sum(acc_ab[...]) + jnp.sum(acc_vb[...]) + CHUNK * jnp.sum(acc_sb[...] * cb)
            rid = lax.broadcasted_iota(jnp.int32, (8, 128), 0)
            dlg_ref[0] = jnp.where(rid == 0, tf, jnp.where(rid == 1, tb, 0.0))

    nch = t // CHUNK
    return pl.pallas_call(
        body, name="ret_bwd", grid=(RET_HEADS, 2, nr),
        in_specs=[in_row, in_row, q_spec, k_spec, v_spec, g_spec, tab_spec, tab_spec, cols_spec, mats_spec, cdec_spec],
        out_specs=[out_row, out_row, out_row, out_row, pl.BlockSpec((1, 8, 128), lambda h, p, n: (h, 0, 0))],
        out_shape=[jax.ShapeDtypeStruct((t, d), bf16)] * 4 + [jax.ShapeDtypeStruct((RET_HEADS, 8, 128), f32)],
        scratch_shapes=[pltpu.VMEM((nch, dk, dk), bf16), pltpu.VMEM((nch, dk, dk), bf16),
                        pltpu.VMEM((dk, dk), f32), pltpu.VMEM((dk, dk), f32),
                        pltpu.VMEM((CHUNK, CHUNK), f32), pltpu.VMEM((CHUNK, CHUNK), f32),
                        pltpu.VMEM((CHUNK, dk), f32), pltpu.VMEM((CHUNK, dk), f32),
                        pltpu.VMEM((dk, dk), f32), pltpu.VMEM((dk, dk), f32),
                        pltpu.VMEM((t, dk), bf16), pltpu.VMEM((t, dk), bf16)],
        compiler_params=_cparams(VMEM_LIMIT_WIDE),
    )(drn, r, proj, proj, proj, proj, cos, sin, cols, mats, cdec)


def mix_fwd(a, rn, proj, wa, wb, wo, x1):
    t, d = x1.shape
    tm = _row_tile(t)

    def body(a_ref, rn_ref, p_ref, wa_ref, wb_ref, wo_ref, x_ref, xo_ref, ba_ref, br_ref):
        ba = _dot(a_ref[...], wa_ref[...])
        br = _dot(rn_ref[...], wb_ref[...])
        sa = _sigmoid(p_ref[0, :, 0:d].astype(f32))
        sb = _sigmoid(p_ref[0, :, d:2 * d].astype(f32))
        mix = (sa * ba + sb * br).astype(bf16)
        xo_ref[...] = x_ref[...] + _dot(mix, wo_ref[...])
        ba_ref[...] = ba.astype(bf16)
        br_ref[...] = br.astype(bf16)

    row = pl.BlockSpec((tm, d), lambda i: (i, 0))
    wsp = pl.BlockSpec((d, d), lambda i: (0, 0))
    return pl.pallas_call(
        body, name="mix_fwd", grid=(t // tm,),
        in_specs=[row, row, pl.BlockSpec((1, tm, 2 * d), lambda i: (3, i, 0)), wsp, wsp, wsp, row],
        out_specs=[row, row, row],
        out_shape=[jax.ShapeDtypeStruct((t, d), f32), jax.ShapeDtypeStruct((t, d), bf16), jax.ShapeDtypeStruct((t, d), bf16)],
        compiler_params=_cparams(),
    )(a, rn, proj, wa, wb, wo, x1)


def mix_bwd_act(dx2, ba, br, proj, wa, wb, wo, sng, snb, ws, bs, dep):
    t, d = dx2.shape
    tm = _row_tile(t)

    def body(dx_ref, ba_ref, br_ref, p_ref, uv_ref, wa_ref, wb_ref, wo_ref, sng_ref, snb_ref, ws_ref, bs_ref, dep_ref,
             drn_ref, dga_ref, dgb_ref, mix_ref, dba_ref, dbr_ref, dxb_ref,
             dua_ref, dva_ref, dws_ref, dbs_ref, dng_ref, dnb_ref, dvn_scr):
        _zero_at_first_step(dws_ref, dbs_ref, dng_ref, dnb_ref)
        dxb = dx_ref[...].astype(bf16)
        dxb_ref[...] = dxb
        dmix = _dot_nt(dxb, wo_ref[...])
        ba = ba_ref[...].astype(f32)
        br = br_ref[...].astype(f32)
        sa = _sigmoid(p_ref[0, :, 0:d].astype(f32))
        sb = _sigmoid(p_ref[0, :, d:2 * d].astype(f32))
        mix_ref[...] = (sa * ba + sb * br).astype(bf16)
        dba = (dmix * sa).astype(bf16)
        dbr = (dmix * sb).astype(bf16)
        dba_ref[...] = dba
        dbr_ref[...] = dbr
        dga_ref[...] = (dmix * ba * sa * (1.0 - sa)).astype(bf16)
        dgb_ref[...] = (dmix * br * sb * (1.0 - sb)).astype(bf16)
        drn_ref[...] = _dot_nt(dbr, wb_ref[...]).astype(bf16)
        da = _dot_nt(dba, wa_ref[...])
        _sgu_bwd_rows(da, uv_ref[0, :, 0:d].astype(f32), uv_ref[0, :, d:2 * d].astype(f32), sng_ref, snb_ref, ws_ref,
                      bs_ref, dua_ref, dva_ref, dws_ref, dbs_ref, dng_ref, dnb_ref, dvn_scr)

    row = pl.BlockSpec((tm, d), lambda i: (i, 0))
    vec = pl.BlockSpec((1, d), lambda i: (0, 0))
    wsp = pl.BlockSpec((d, d), lambda i: (0, 0))
    sws = pl.BlockSpec((SGU_GROUPS, SGU_CHUNK, SGU_CHUNK), lambda i: (0, 0, 0))
    sbs = pl.BlockSpec((SGU_GROUPS, SGU_CHUNK, 1), lambda i: (0, 0, 0))
    return pl.pallas_call(
        body, name="mix_bwd_act", grid=(t // tm,),
        in_specs=[row, row, row, pl.BlockSpec((1, tm, 2 * d), lambda i: (3, i, 0)),
                  pl.BlockSpec((1, tm, 2 * d), lambda i: (0, i, 0)), wsp, wsp, wsp, vec, vec, sws, sbs, _ANY],
        out_specs=[row] * 9 + [sws, sbs, vec, vec],
        out_shape=[jax.ShapeDtypeStruct((t, d), bf16)] * 9
        + [jax.ShapeDtypeStruct((SGU_GROUPS, SGU_CHUNK, SGU_CHUNK), f32), jax.ShapeDtypeStruct((SGU_GROUPS, SGU_CHUNK, 1), f32),
           jax.ShapeDtypeStruct((1, d), f32), jax.ShapeDtypeStruct((1, d), f32)],
        scratch_shapes=[pltpu.VMEM((tm, d), f32)],
        compiler_params=_cparams(VMEM_LIMIT_WIDE),
    )(dx2, ba, br, proj, proj, wa, wb, wo, sng, snb, ws, bs, dep)


def inproj_bwd_act(segs, win, x1, ng, dx2):
    t, d = x1.shape
    s4 = win.shape[0]
    tm = _row_tile(t)
    nseg = len(segs)

    def body(*refs):
        seg_refs = refs[:nseg]
        w_ref, x_ref, ng_ref, dx2_ref, dx1_ref, db_ref, dng_ref = refs[nseg:]
        _zero_at_first_step(db_ref, dng_ref)
        dh = None
        for e, sr in enumerate(seg_refs):
            sb = sr[...]
            part = _dot_nt(sb, w_ref[e // 2, :, (e % 2) * d:(e % 2 + 1) * d])
            dh = part if dh is None else dh + part
            db_ref[e] += jnp.sum(sb.astype(f32), axis=0, keepdims=True)
        _, xh, r = _rms(x_ref[...], ng_ref[...])
        dx1_ref[...] = dx2_ref[...] + _rms_bwd(dh, xh, r, ng_ref[...])
        dng_ref[...] += jnp.sum(dh * xh, axis=0, keepdims=True)

    row = pl.BlockSpec((tm, d), lambda i: (i, 0))
    vec = pl.BlockSpec((1, d), lambda i: (0, 0))
    return pl.pallas_call(
        body, name="inproj_bwd_act", grid=(t // tm,),
        in_specs=[row] * nseg + [pl.BlockSpec((s4, d, 2 * d), lambda i: (0, 0, 0), pipeline_mode=pl.Buffered(1)),
                                 row, vec, row],
        out_specs=[row, pl.BlockSpec((nseg, 1, d), lambda i: (0, 0, 0)), vec],
        out_shape=[jax.ShapeDtypeStruct((t, d), f32), jax.ShapeDtypeStruct((nseg, 1, d), f32),
                   jax.ShapeDtypeStruct((1, d), f32)],
        compiler_params=_cparams(VMEM_LIMIT_WIDE),
    )(*segs, win, x1, ng, dx2)


def _place():
    return lax.axis_index("x"), lax.axis_index("y"), lax.axis_index("c")


def _other_chips(x, y):
    return [(1 - x, y), (x, 1 - y), (1 - x, 1 - y)]


_ANY = pl.BlockSpec(memory_space=pl.ANY)


_HBM = pl.BlockSpec(memory_space=pltpu.HBM)
_SEM = pl.BlockSpec(memory_space=pltpu.SEMAPHORE)
_EFFECT = pltpu.SideEffectType.DATAFLOW_SIDE_EFFECTING


def _hbm(a):
    return pltpu.with_memory_space_constraint(a, pltpu.HBM)


def _half_rows(ref, c):
    half = ref.shape[1] // 2
    return pl.ds(pl.multiple_of(c * half, 16), half)


def _chip_copy(src, dst, send_sem, recv_sem, chip, c):
    return pltpu.make_async_remote_copy(src_ref=src, dst_ref=dst, send_sem=send_sem, recv_sem=recv_sem,
                                        device_id=(chip[0], chip[1], c), device_id_type=MESH)


def gather_start(bufs, groups, name):
    nb, ng = len(bufs), len(groups)

    def body(*refs):
        ins = refs[:nb]
        sems = refs[nb:nb + 2 * ng]
        token = refs[-1]
        x, y, c = _place()
        k = 2 * x + y
        for gi, grp in enumerate(groups):
            for wi, w in enumerate(grp):
                mine = ins[w].at[k, _half_rows(ins[w], c)]
                for j, chip in enumerate(_other_chips(x, y)):
                    _chip_copy(mine, mine, sems[2 * gi].at[3 * wi + j], sems[2 * gi + 1].at[3 * wi + j], chip, c).start()
        token[...] = jnp.zeros_like(token)

    sem_shapes = []
    for grp in groups:
        sem_shapes += [pltpu.SemaphoreType.DMA((3 * len(grp),)), pltpu.SemaphoreType.DMA((3 * len(grp),))]
    outs = pl.pallas_call(
        body, name=name,
        out_shape=sem_shapes + [pltpu.HBM(b.shape, b.dtype) for b in bufs] + [jax.ShapeDtypeStruct((8, 128), f32)],
        in_specs=[_HBM] * nb,
        out_specs=[_SEM] * (2 * ng) + [_HBM] * nb + [pl.BlockSpec(memory_space=pltpu.VMEM)],
        input_output_aliases={w: 2 * ng + w for w in range(nb)},
        compiler_params=pltpu.CompilerParams(has_side_effects=_EFFECT),
    )(*[_hbm(b) for b in bufs])
    sems = [(outs[2 * gi], outs[2 * gi + 1]) for gi in range(ng)]
    return sems, list(outs[2 * ng:2 * ng + nb]), outs[-1]


def gather_wait(bufs, sems, after, name):
    n = len(bufs)

    def body(*refs):
        ins = refs[:n]
        send_sems, recv_sems = refs[n], refs[n + 1]
        x, y, c = _place()
        k = 2 * x + y
        for wi in range(n):
            half = _half_rows(ins[wi], c)
            for j, chip in enumerate(_other_chips(x, y)):
                cp = _chip_copy(ins[wi].at[k, half], ins[wi].at[2 * chip[0] + chip[1], half], send_sems.at[3 * wi + j],
                                recv_sems.at[3 * wi + j], chip, c)
                cp.wait_send()
                cp.wait_recv()

    outs = pl.pallas_call(
        body, name=name,
        out_shape=[pltpu.HBM(b.shape, b.dtype) for b in bufs],
        in_specs=[_HBM] * n + [_SEM, _SEM, _ANY],
        out_specs=[_HBM] * n,
        input_output_aliases={i: i for i in range(n)},
        compiler_params=pltpu.CompilerParams(has_side_effects=_EFFECT),
    )(*bufs, sems[0], sems[1], after)
    return list(outs)


def gather_forward(bufs, name):
    n = len(bufs)

    def body(*refs):
        ins = refs[n:2 * n]
        send_sems, recv_sems = refs[2 * n], refs[2 * n + 1]
        x, y, c = _place()
        copies = []
        for wi in range(n):
            for j, chip in enumerate(_other_chips(x, y)):
                kp = 2 * chip[0] + chip[1]
                got = ins[wi].at[kp, _half_rows(ins[wi], c)]
                cp = pltpu.make_async_remote_copy(
                    src_ref=got, dst_ref=got, send_sem=send_sems.at[3 * wi + j], recv_sem=recv_sems.at[3 * wi + j],
                    device_id=(x, y, 1 - c), device_id_type=MESH)
                cp.start()
                copies.append((cp, wi, kp, j))
        for cp, wi, kp, j in copies:
            cp.wait_send()
            theirs = ins[wi].at[kp, _half_rows(ins[wi], 1 - c)]
            pltpu.make_async_remote_copy(
                src_ref=theirs, dst_ref=theirs, send_sem=send_sems.at[3 * wi + j], recv_sem=recv_sems.at[3 * wi + j],
                device_id=(x, y, 1 - c), device_id_type=MESH).wait_recv()

    outs = pl.pallas_call(
        body, name=name,
        out_shape=[jax.ShapeDtypeStruct(b.shape, b.dtype) for b in bufs],
        in_specs=[_ANY] * n, out_specs=[_ANY] * n,
        input_output_aliases={i: i for i in range(n)},
        scratch_shapes=[pltpu.SemaphoreType.DMA((3 * n,)), pltpu.SemaphoreType.DMA((3 * n,))],
    )(*bufs)
    return list(outs)


def forward_start(bufs, name):
    n = len(bufs)

    def body(*refs):
        x, y, c = _place()
        for wi in range(n):
            for j, chip in enumerate(_other_chips(x, y)):
                got = refs[wi].at[2 * chip[0] + chip[1], _half_rows(refs[wi], c)]
                _sibling_copy(got, got, refs[n].at[3 * wi + j], refs[n + 1].at[3 * wi + j]).start()
        refs[-1][...] = jnp.zeros_like(refs[-1])

    return _split_start(body, name, 3 * n, list(bufs))


def forward_wait(bufs, sems, after, name):
    n = len(bufs)

    def body(*refs):
        x, y, c = _place()
        for wi in range(n):
            for j, chip in enumerate(_other_chips(x, y)):
                kp = 2 * chip[0] + chip[1]
                got = refs[wi].at[kp, _half_rows(refs[wi], c)]
                theirs = refs[wi].at[kp, _half_rows(refs[wi], 1 - c)]
                _sibling_copy(got, got, refs[n].at[3 * wi + j], refs[n + 1].at[3 * wi + j]).wait_send()
                _sibling_copy(theirs, theirs, refs[n].at[3 * wi + j], refs[n + 1].at[3 * wi + j]).wait_recv()

    return _split_wait(body, name, list(bufs), sems, after)


def exchange_start(grads, name):
    n = len(grads)
    lands = [lax.empty((3,) + g.shape[1:], g.dtype) for g in grads]

    def body(*refs):
        ins = refs[:n]
        land = refs[n:2 * n]
        send_sems, recv_sems = refs[2 * n], refs[2 * n + 1]
        token = refs[-1]
        x, y, c = _place()
        for wi in range(n):
            for j, chip in enumerate(_other_chips(x, y)):
                _chip_copy(ins[wi].at[2 * chip[0] + chip[1]], land[wi].at[j], send_sems.at[3 * wi + j],
                           recv_sems.at[3 * wi + j], chip, c).start()
        token[...] = jnp.zeros_like(token)

    outs = pl.pallas_call(
        body, name=name,
        out_shape=[pltpu.SemaphoreType.DMA((3 * n,)), pltpu.SemaphoreType.DMA((3 * n,))]
        + [pltpu.HBM(g.shape, g.dtype) for g in grads] + [pltpu.HBM(l.shape, l.dtype) for l in lands]
        + [jax.ShapeDtypeStruct((8, 128), f32)],
        in_specs=[_HBM] * (2 * n),
        out_specs=[_SEM, _SEM] + [_HBM] * (2 * n) + [pl.BlockSpec(memory_space=pltpu.VMEM)],
        input_output_aliases={i: 2 + i for i in range(2 * n)},
        compiler_params=pltpu.CompilerParams(has_side_effects=_EFFECT),
    )(*[_hbm(g) for g in grads], *[_hbm(l) for l in lands])
    return (outs[0], outs[1]), list(outs[2:2 + n]), list(outs[2 + n:2 + 2 * n]), outs[-1]


def exchange_wait(grads, lands, sems, after, name):
    n = len(grads)

    def body(*refs):
        ins = refs[:n]
        land = refs[n:2 * n]
        send_sems, recv_sems = refs[2 * n], refs[2 * n + 1]
        x, y, c = _place()
        for wi in range(n):
            for j, chip in enumerate(_other_chips(x, y)):
                cp = _chip_copy(ins[wi].at[2 * chip[0] + chip[1]], land[wi].at[j], send_sems.at[3 * wi + j],
                                recv_sems.at[3 * wi + j], chip, c)
                cp.wait_send()
                cp.wait_recv()

    outs = pl.pallas_call(
        body, name=name,
        out_shape=[pltpu.HBM(g.shape, g.dtype) for g in grads] + [pltpu.HBM(l.shape, l.dtype) for l in lands],
        in_specs=[_HBM] * (2 * n) + [_SEM, _SEM, _ANY],
        out_specs=[_HBM] * (2 * n),
        input_output_aliases={i: i for i in range(2 * n)},
        compiler_params=pltpu.CompilerParams(has_side_effects=_EFFECT),
    )(*grads, *lands, sems[0], sems[1], after)
    return list(outs[:n]), list(outs[n:])


def _split_start(body, name, n_sems, operands):
    n = len(operands)
    outs = pl.pallas_call(
        body, name=name,
        out_shape=[pltpu.SemaphoreType.DMA((n_sems,)), pltpu.SemaphoreType.DMA((n_sems,))]
        + [pltpu.HBM(o.shape, o.dtype) for o in operands] + [jax.ShapeDtypeStruct((8, 128), f32)],
        in_specs=[_HBM] * n,
        out_specs=[_SEM, _SEM] + [_HBM] * n + [pl.BlockSpec(memory_space=pltpu.VMEM)],
        input_output_aliases={i: 2 + i for i in range(n)},
        compiler_params=pltpu.CompilerParams(has_side_effects=_EFFECT),
    )(*[_hbm(o) for o in operands])
    return (outs[0], outs[1]), list(outs[2:2 + n]), outs[-1]


def _split_wait(body, name, operands, sems, after):
    n = len(operands)
    outs = pl.pallas_call(
        body, name=name,
        out_shape=[pltpu.HBM(o.shape, o.dtype) for o in operands],
        in_specs=[_HBM] * n + [_SEM, _SEM, _ANY],
        out_specs=[_HBM] * n,
        input_output_aliases={i: i for i in range(n)},
        compiler_params=pltpu.CompilerParams(has_side_effects=_EFFECT),
    )(*operands, sems[0], sems[1], after)
    return list(outs)


def _sibling_copy(src, dst, send_sem, recv_sem):
    x, y, c = _place()
    return pltpu.make_async_remote_copy(src_ref=src, dst_ref=dst, send_sem=send_sem, recv_sem=recv_sem,
                                        device_id=(x, y, 1 - c), device_id_type=MESH)


def swap_start(parts, name):
    n = len(parts)

    def body(*refs):
        for w in range(n):
            _sibling_copy(refs[w], refs[n + w], refs[2 * n].at[w], refs[2 * n + 1].at[w]).start()
        refs[-1][...] = jnp.zeros_like(refs[-1])

    sems, ops, token = _split_start(body, name, n, list(parts) + [lax.empty(p.shape, p.dtype) for p in parts])
    return sems, ops[:n], ops[n:], token


def swap_wait(parts, lands, sems, after, name):
    n = len(parts)

    def body(*refs):
        for w in range(n):
            cp = _sibling_copy(refs[w], refs[n + w], refs[2 * n].at[w], refs[2 * n + 1].at[w])
            cp.wait_send()
            cp.wait_recv()

    outs = _split_wait(body, name, list(parts) + list(lands), sems, after)
    return outs[:n], outs[n:]


def _all_peers(x, y, c):
    return [(1 - x if m & 4 else x, 1 - y if m & 2 else y, 1 - c if m & 1 else c) for m in range(1, N_DEV)]


def small_start(block):
    land = jnp.broadcast_to(block[None], (N_DEV,) + block.shape)

    def body(b_ref, land_ref, send_sems, recv_sems, b_thru, land_thru, token):
        x, y, c = _place()
        me = 4 * x + 2 * y + c
        for m, peer in enumerate(_all_peers(x, y, c)):
            pltpu.make_async_remote_copy(src_ref=b_ref, dst_ref=land_ref.at[me], send_sem=send_sems.at[m],
                                         recv_sem=recv_sems.at[m], device_id=peer, device_id_type=MESH).start()
        token[...] = jnp.zeros_like(token)

    sems, ops, token = _split_start(body, "small_start", N_DEV - 1, [block, land])
    return sems, ops[0], ops[1], token


def small_wait(block, land, sems, after):
    def body(b_ref, land_ref, send_sems, recv_sems, after_ref, b_thru, land_thru):
        x, y, c = _place()
        for m, (px, py, pc) in enumerate(_all_peers(x, y, c)):
            cp = pltpu.make_async_remote_copy(src_ref=b_ref, dst_ref=land_ref.at[4 * px + 2 * py + pc],
                                              send_sem=send_sems.at[m], recv_sem=recv_sems.at[m],
                                              device_id=(px, py, pc), device_id_type=MESH)
            cp.wait_send()
            cp.wait_recv()

    return _split_wait(body, "small_wait", [block, land], sems, after)[1]


def _adamw(w, g, m, v):
    m = ADAM_B1 * m + (1.0 - ADAM_B1) * g
    v = ADAM_B2 * v + (1.0 - ADAM_B2) * (g * g)
    m_hat = m / (1.0 - ADAM_B1 ** ADAM_STEP)
    v_hat = v / (1.0 - ADAM_B2 ** ADAM_STEP)
    delta = -ADAM_LR * (m_hat / (jnp.sqrt(v_hat) + ADAM_EPS) + ADAM_WD * w)
    return delta, m, v


EW_BLOCK_BYTES = 2 * 1024 * 1024


def _ew_tile(rows, cols):
    for cand in (512, 352, 256, 176, 128, 64, 32, 16, 8):
        if rows % cand == 0 and cand * cols * 4 <= EW_BLOCK_BYTES:
            return cand
    return rows


def sum_partials(chip, own, land, name):
    _, r, c = own.shape
    tr = _ew_tile(r, c)

    def body(k_ref, own_ref, p_ref, o_ref):
        o_ref[...] = ((own_ref[0].astype(f32) + p_ref[0].astype(f32)) + p_ref[1].astype(f32)) + p_ref[2].astype(f32)

    return pl.pallas_call(
        body, name=name,
        grid_spec=pltpu.PrefetchScalarGridSpec(
            num_scalar_prefetch=1, grid=(r // tr,),
            in_specs=[pl.BlockSpec((1, tr, c), lambda i, k: (k[0], i, 0)), pl.BlockSpec((3, tr, c), lambda i, k: (0, i, 0))],
            out_specs=pl.BlockSpec((tr, c), lambda i, k: (i, 0))),
        out_shape=jax.ShapeDtypeStruct((r, c), f32),
        compiler_params=_cparams(),
    )(chip, own, land)


def adamw_shard(p_mine, p_sibling, w, m, v, name):
    r, c = w.shape
    tr = _ew_tile(r, c)

    def body(a_ref, b_ref, w_ref, m_ref, v_ref, g_ref, d_ref, mo_ref, vo_ref):
        g = a_ref[...] + b_ref[...]
        delta, mn, vn = _adamw(w_ref[...], g, m_ref[...], v_ref[...])
        g_ref[...] = g
        d_ref[...] = delta
        mo_ref[...] = mn
        vo_ref[...] = vn

    blk = pl.BlockSpec((tr, c), lambda i: (i, 0))
    return pl.pallas_call(
        body, name=name, grid=(r // tr,),
        in_specs=[blk] * 5, out_specs=[blk] * 4,
        out_shape=[jax.ShapeDtypeStruct((r, c), f32)] * 4,
        compiler_params=_cparams(),
    )(p_mine, p_sibling, w, m, v)


def adamw_small(g8, w, m, v):
    _, r, lanes = g8.shape

    def body(g_ref, w_ref, m_ref, v_ref, go_ref, d_ref, mo_ref, vo_ref):
        g = g_ref[0]
        for i in range(1, N_DEV):
            g = g + g_ref[i]
        delta, mn, vn = _adamw(w_ref[...], g, m_ref[...], v_ref[...])
        go_ref[...] = g
        d_ref[...] = delta
        mo_ref[...] = mn
        vo_ref[...] = vn

    return pl.pallas_call(
        body, name="adamw_small",
        out_shape=[jax.ShapeDtypeStruct((r, lanes), f32)] * 4,
        compiler_params=_cparams(),
    )(g8, w, m, v)


def _size(shape):
    n = 1
    for e in shape:
        n *= e
    return n


def _pack_rows(shapes):
    rows = [-(-_size(s) // 1024) * 8 for s in shapes]
    return rows, sum(rows)


def _pack(arrs, shapes):
    rows, _ = _pack_rows(shapes)
    parts = [jnp.pad(a.reshape(-1).astype(f32), (0, r * 128 - _size(s))).reshape(r, 128)
             for a, s, r in zip(arrs, shapes, rows)]
    return jnp.concatenate(parts, axis=0)


def _unpack(block, shapes):
    rows, _ = _pack_rows(shapes)
    out, off = [], 0
    for s, r in zip(shapes, rows):
        out.append(block[off:off + r].reshape(-1)[:_size(s)].reshape(s))
        off += r
    return out


TRANSPOSED = ("ffn1_w_gate", "ffn1_w_up", "ffn2_w_gate", "ffn2_w_up")


def _shard2d(a, n):
    return a[0].T if n in TRANSPOSED else a[0]


def _unshard(a, n):
    return (a.T if n in TRANSPOSED else a)[None]


BIG = ("ffn1_w_gate", "ffn1_w_up", "ffn1_w_down", "w_in", "w_branch_a", "w_branch_b", "w_out",
       "ffn2_w_gate", "ffn2_w_up", "ffn2_w_down")
SMALL = ("ffn1_norm", "mix_norm", "b_in", "sgu_norm_g", "sgu_norm_b", "sgu_w_s", "sgu_b_s", "ret_decay_logit",
         "ffn2_norm", "final_norm")
WEIGHTS = ("ffn1_norm", "ffn1_w_gate", "ffn1_w_up", "ffn1_w_down", "mix_norm", "w_in", "b_in", "sgu_norm_g",
           "sgu_norm_b", "sgu_w_s", "sgu_b_s", "ret_decay_logit", "w_branch_a", "w_branch_b", "w_out", "ffn2_norm",
           "ffn2_w_gate", "ffn2_w_up", "ffn2_w_down", "final_norm")


def kernel(x, ffn1_norm, ffn1_w_gate, ffn1_w_up, ffn1_w_down, mix_norm, w_in, b_in, sgu_norm_g, sgu_norm_b, sgu_w_s, sgu_b_s, ret_decay_logit, w_branch_a, w_branch_b, w_out, ffn2_norm, ffn2_w_gate, ffn2_w_up, ffn2_w_down, final_norm, loss_target, m_ffn1_norm, m_ffn1_w_gate, m_ffn1_w_up, m_ffn1_w_down, m_mix_norm, m_w_in, m_b_in, m_sgu_norm_g, m_sgu_norm_b, m_sgu_w_s, m_sgu_b_s, m_ret_decay_logit, m_w_branch_a, m_w_branch_b, m_w_out, m_ffn2_norm, m_ffn2_w_gate, m_ffn2_w_up, m_ffn2_w_down, m_final_norm, v_ffn1_norm, v_ffn1_w_gate, v_ffn1_w_up, v_ffn1_w_down, v_mix_norm, v_w_in, v_b_in, v_sgu_norm_g, v_sgu_norm_b, v_sgu_w_s, v_sgu_b_s, v_ret_decay_logit, v_w_branch_a, v_w_branch_b, v_w_out, v_ffn2_norm, v_ffn2_w_gate, v_ffn2_w_up, v_ffn2_w_down, v_final_norm):
    p = dict(ffn1_norm=ffn1_norm, ffn1_w_gate=ffn1_w_gate, ffn1_w_up=ffn1_w_up, ffn1_w_down=ffn1_w_down,
             mix_norm=mix_norm, w_in=w_in, b_in=b_in, sgu_norm_g=sgu_norm_g, sgu_norm_b=sgu_norm_b, sgu_w_s=sgu_w_s,
             sgu_b_s=sgu_b_s, ret_decay_logit=ret_decay_logit, w_branch_a=w_branch_a, w_branch_b=w_branch_b,
             w_out=w_out, ffn2_norm=ffn2_norm, ffn2_w_gate=ffn2_w_gate, ffn2_w_up=ffn2_w_up, ffn2_w_down=ffn2_w_down,
             final_norm=final_norm)
    mom = dict(ffn1_norm=m_ffn1_norm, ffn1_w_gate=m_ffn1_w_gate, ffn1_w_up=m_ffn1_w_up, ffn1_w_down=m_ffn1_w_down,
               mix_norm=m_mix_norm, w_in=m_w_in, b_in=m_b_in, sgu_norm_g=m_sgu_norm_g, sgu_norm_b=m_sgu_norm_b,
               sgu_w_s=m_sgu_w_s, sgu_b_s=m_sgu_b_s, ret_decay_logit=m_ret_decay_logit, w_branch_a=m_w_branch_a,
               w_branch_b=m_w_branch_b, w_out=m_w_out, ffn2_norm=m_ffn2_norm, ffn2_w_gate=m_ffn2_w_gate,
               ffn2_w_up=m_ffn2_w_up, ffn2_w_down=m_ffn2_w_down, final_norm=m_final_norm)
    var = dict(ffn1_norm=v_ffn1_norm, ffn1_w_gate=v_ffn1_w_gate, ffn1_w_up=v_ffn1_w_up, ffn1_w_down=v_ffn1_w_down,
               mix_norm=v_mix_norm, w_in=v_w_in, b_in=v_b_in, sgu_norm_g=v_sgu_norm_g, sgu_norm_b=v_sgu_norm_b,
               sgu_w_s=v_sgu_w_s, sgu_b_s=v_sgu_b_s, ret_decay_logit=v_ret_decay_logit, w_branch_a=v_w_branch_a,
               w_branch_b=v_w_branch_b, w_out=v_w_out, ffn2_norm=v_ffn2_norm, ffn2_w_gate=v_ffn2_w_gate,
               ffn2_w_up=v_ffn2_w_up, ffn2_w_down=v_ffn2_w_down, final_norm=v_final_norm)

    xs = x[0]
    tgt = loss_target[0]
    t, d = xs.shape
    dk = d // RET_HEADS

    shards2d = {n: _shard2d(p[n], n) for n in BIG}
    chip = (2 * lax.axis_index("x") + lax.axis_index("y")).astype(jnp.int32).reshape(1)
    groups = {"ffn1": ("ffn1_w_gate", "ffn1_w_up", "ffn1_w_down"), "in": ("w_in",),
              "mix": ("w_branch_a", "w_branch_b", "w_out"), "ffn2": ("ffn2_w_gate", "ffn2_w_up", "ffn2_w_down")}
    def own_slot(n, zero):
        sh = shards2d[n].astype(bf16) + zero
        return lax.dynamic_update_index_in_dim(lax.empty((N_CHIPS,) + sh.shape, bf16), sh, chip[0], 0)

    sems, bufs, tok = gather_start([own_slot(n, jnp.zeros((), bf16)) for n in groups["ffn1"]], [[0, 1, 2]],
                                   "gather_start_ffn1")
    gsem = {"ffn1": sems[0]}
    pending = dict(zip(groups["ffn1"], bufs))
    rest = [n for g in ("in", "mix", "ffn2") for n in groups[g]]
    sems, bufs, tok_rest = gather_start([own_slot(n, tok[0, 0].astype(bf16)) for n in rest],
                                 [[rest.index(n) for n in groups[g]] for g in ("in", "mix", "ffn2")], "gather_start_rest")
    gsem.update(zip(("in", "mix", "ffn2"), sems))
    pending.update(zip(rest, bufs))

    def arrive(gs, after):
        got = []
        for g in gs:
            got += gather_wait([pending[n] for n in groups[g]], gsem[g], after, "gather_wait_" + g)
        return gather_forward(got, "gather_forward_" + gs[0])

    bin4 = b_in.reshape(N_CHIPS, 1, 2 * d)
    ws_b = sgu_w_s[0].astype(bf16)
    bs_c = sgu_b_s[0][:, :, None]
    cols, mats, cdec, cos, sin = retention_constants(ret_decay_logit[0], t, dk, tok_rest[0, 0])

    wg1, wu1, wd1 = [_pair_shards(w) for w in arrive(["ffn1"], cos)]
    x1, g1, u1 = ffn_fwd(xs, ffn1_norm, wg1, wu1, wd1, "ffn1_fwd")
    win, = arrive(["in"], x1)
    proj, hb2, a = inproj_fwd(x1, mix_norm, win, bin4, cos, sin, sgu_norm_g, sgu_norm_b, ws_b, bs_c)
    late = []
    for g in ("mix", "ffn2"):
        late += gather_wait([pending[n] for n in groups[g]], gsem[g], proj, "gather_wait_" + g)
    fsems, late, ftok = forward_start(late, "forward_start_mix")
    r, rn = ret_fwd(proj, cols, mats, cdec, ftok)
    wa, wb, wo, wg2, wu2, wd2 = forward_wait(late, fsems, rn, "forward_wait_mix")
    wa, wb, wo = [w.reshape(d, d) for w in (wa, wb, wo)]
    wg2, wu2, wd2 = [_pair_shards(w) for w in (wg2, wu2, wd2)]
    x2, ba, br = mix_fwd(a, rn, proj, wa, wb, wo, x1)
    loss_blk, dx3, d_final, g2, u2 = ffn_fwd_loss(x2, ffn2_norm, wg2, wu2, wd2, final_norm.reshape(1, d), tgt, "ffn2_fwd")

    sent, swaps = {}, {}
    out_g, out_d, out_m, out_v = {}, {}, {}, {}

    def reduce_plane(g, after):
        gsems, own, lands, _ = sent[g]
        own, lands = exchange_wait(own, lands, gsems, after, "exchange_wait_" + g)
        plane = [sum_partials(chip, o, l, "sum_" + n) for n, o, l in zip(groups[g], own, lands)]
        swaps[g] = swap_start(plane, "swap_start_" + g)
        return swaps[g][3]

    def update(g, after):
        ssems, plane, lands, _ = swaps[g]
        plane, other = swap_wait(plane, lands, ssems, after, "swap_wait_" + g)
        for n, mine, sib in zip(groups[g], plane, other):
            res = adamw_shard(mine, sib, shards2d[n], _shard2d(mom[n], n), _shard2d(var[n], n), "adamw_" + n)
            out_g[n], out_d[n], out_m[n], out_v[n] = [_unshard(o, n) for o in res]
        return res[0]

    dx2, dg2, du2, act2, hb3, dyb2, d_ffn2n = ffn_bwd_act(dx3, x2, ffn2_norm, g2, u2, wg2, wu2, wd2, "ffn2_bwd_act", tok)
    sent["ffn2"] = exchange_start(ffn_weight_grads_one_call(hb3, dyb2, dg2, du2, act2, "ffn2_grad", tok),
                                  "exchange_start_ffn2")
    drn, dga, dgb, mixb, dba, dbr, dx2b, dua, dva, d_ws, d_bs, d_sng, d_snb = mix_bwd_act(
        dx2, ba, br, proj, wa, wb, wo, sgu_norm_g, sgu_norm_b, ws_b, bs_c, sent["ffn2"][3])
    g_mix = [g.reshape(N_CHIPS, d // N_CHIPS, d) for g in tn_matmuls_one_call(
        None, [a, rn, mixb], [dba, dbr, dx2b], [(d, d)] * 3, lambda j: (j, (slice(None), slice(None))), "grad_mix", tok)]
    dq, dkr, dv, dgr, dlg = ret_bwd(drn, r, proj, cols, mats, cdec, cos, sin)
    segs = [dua, dva, dq, dkr, dv, dgr, dga, dgb]
    dx1, d_bin, d_mixn = inproj_bwd_act(segs, win, x1, mix_norm, dx2)
    g_in, = tn_matmuls_one_call(hb2, None, segs, [(N_CHIPS, d, 2 * d)],
                                lambda j: (0, (j // 2, slice(None), pl.ds((j % 2) * d, d))), "grad_w_in", tok)
    groups["mix_in"] = groups["mix"] + groups["in"]
    sent["mix_in"] = exchange_start(g_mix + [g_in], "exchange_start_mix_in")
    grad_x, dg1, du1, act1, hb1, dyb1, d_ffn1n = ffn_bwd_act(dx1, xs, ffn1_norm, g1, u1, wg1, wu1, wd1, "ffn1_bwd_act",
                                                              sent["mix_in"][3])
    dlogit = dlg[:, 0:2, 0].T * jax.nn.sigmoid(-ret_decay_logit[0].astype(f32))
    small_g = dict(ffn1_norm=d_ffn1n, mix_norm=d_mixn, b_in=d_bin, sgu_norm_g=d_sng, sgu_norm_b=d_snb, sgu_w_s=d_ws,
                   sgu_b_s=d_bs, ret_decay_logit=dlogit, ffn2_norm=d_ffn2n, final_norm=d_final)
    shapes = [p[n].shape for n in SMALL] + [(1,)]
    small_sems, small_blk, small_land, small_tok = small_start(
        _pack([small_g[n] for n in SMALL] + [loss_blk[0, 0:1]], shapes))

    def send_one(which, grad):
        n = "ffn1_" + which
        groups[n] = (n,)
        sent[n] = exchange_start([grad], "exchange_start_" + n)
        return sent[n][3]

    ffn_weight_grads(hb1, dyb1, dg1, du1, act1, "ffn1_grad", small_tok, send_one)

    after = reduce_plane("ffn2", sent["ffn1_w_down"][3])
    after = reduce_plane("mix_in", after)
    after = update("ffn2", after)
    g8 = small_wait(small_blk, small_land, small_sems, after)
    no_state = [jnp.zeros((1,), f32)]
    sg, sd, sm, sv = adamw_small(g8, _pack([p[n] for n in SMALL] + no_state, shapes),
                                 _pack([mom[n] for n in SMALL] + no_state, shapes),
                                 _pack([var[n] for n in SMALL] + no_state, shapes))
    for res, blockv in ((out_g, sg), (out_d, sd), (out_m, sm), (out_v, sv)):
        for n, val in zip(SMALL, _unpack(blockv, shapes)):
            res[n] = val
    loss = _unpack(sg, shapes)[-1][0]
    after = update("mix_in", sg)
    after = reduce_plane("ffn1_w_gate", after)
    after = reduce_plane("ffn1_w_up", after)
    after = update("ffn1_w_gate", after)
    after = reduce_plane("ffn1_w_down", after)
    after = update("ffn1_w_up", after)
    update("ffn1_w_down", after)

    return (loss, grad_x[None], *[out_g[n] for n in WEIGHTS], *[out_d[n] for n in WEIGHTS],
            *[out_m[n] for n in WEIGHTS], *[out_v[n] for n in WEIGHTS])
```

```python
import jax
import jax.numpy as jnp
from jax import lax
from jax.experimental import pallas as pl
from jax.experimental.pallas import tpu as pltpu

f32 = jnp.float32
bf16 = jnp.bfloat16

SGU_CHUNK = 128
CHUNK = 256
RET_HEADS = 4
SGU_GROUPS = 4
ROPE_BASE = 10000.0
NORM_EPS = 1e-6
ADAM_LR = 0.001
ADAM_B1 = 0.9
ADAM_B2 = 0.999
ADAM_EPS = 1e-08
ADAM_WD = 0.01
ADAM_STEP = 10
N_CHIPS = 4
N_DEV = 8
MESH = pl.DeviceIdType.MESH
VMEM_LIMIT = 52 * 1024 * 1024
VMEM_LIMIT_WIDE = 62 * 1024 * 1024

_NT = (((1,), (1,)), ((), ()))
_TN = (((0,), (0,)), ((), ()))


def _cparams(limit=None):
    return pltpu.CompilerParams(vmem_limit_bytes=VMEM_LIMIT if limit is None else limit)


def _row_tile(t):
    return 512 if t >= 2048 else t // 2


def _dot(a, b):
    return jnp.dot(a, b, preferred_element_type=f32)


def _dot_nt(a, b):
    return lax.dot_general(a, b, _NT, preferred_element_type=f32)


def _dot_tn(a, b):
    return lax.dot_general(a, b, _TN, preferred_element_type=f32)


def _rms(x, g):
    r = lax.rsqrt(jnp.mean(x * x, axis=-1, keepdims=True) + NORM_EPS)
    xh = x * r
    return xh * g, xh, r


def _rms_bwd(dy, xh, r, g):
    dxh = dy * g
    return r * (dxh - xh * jnp.mean(dxh * xh, axis=-1, keepdims=True))


def _sigmoid(x):
    return jax.nn.sigmoid(x)


def _dsilu(g, sg):
    return sg * (1.0 + g * (1.0 - sg))


def _gelu(x):
    return 0.5 * x * (1.0 + lax.erf(x * 0.7071067811865476))


def _dgelu(x):
    return 0.5 * (1.0 + lax.erf(x * 0.7071067811865476)) + x * jnp.exp(-0.5 * x * x) * 0.3989422804014327


def _zero_at_first_step(*refs):
    @pl.when(pl.program_id(0) == 0)
    def _():
        for ref in refs:
            ref[...] = jnp.zeros_like(ref)


def _ffn_tile(t):
    return 256 if t >= 2048 else t // 2


def _ffn_fwd_rows(xx, ng_ref, wg_ref, wu_ref, wd_ref, g_ref, u_ref):
    y, _, _ = _rms(xx, ng_ref[...])
    h = y.astype(bf16)
    acc = None
    for s in range(wg_ref.shape[0]):
        g = _dot_nt(h, wg_ref[s])
        u = _dot_nt(h, wu_ref[s])
        g_ref[s] = g.astype(bf16)
        u_ref[s] = u.astype(bf16)
        part = _dot((g * _sigmoid(g) * u).astype(bf16), wd_ref[s])
        acc = part if acc is None else acc + part
    return xx + 0.5 * acc


def ffn_fwd(x, ng, wg, wu, wd, name):
    t, d = x.shape
    ns, fs, _ = wg.shape
    tm = _ffn_tile(t)

    def body(x_ref, ng_ref, wg_ref, wu_ref, wd_ref, xo_ref, g_ref, u_ref):
        xo_ref[...] = _ffn_fwd_rows(x_ref[...], ng_ref, wg_ref, wu_ref, wd_ref, g_ref, u_ref)

    row = pl.BlockSpec((tm, d), lambda i: (i, 0))
    shard = pl.BlockSpec((ns, tm, fs), lambda i: (0, i, 0))
    wspec = pl.BlockSpec((ns, fs, d), lambda i: (0, 0, 0), pipeline_mode=pl.Buffered(1))
    return pl.pallas_call(
        body, name=name, grid=(t // tm,),
        in_specs=[row, pl.BlockSpec((1, d), lambda i: (0, 0)), wspec, wspec, wspec],
        out_specs=[row, shard, shard],
        out_shape=[jax.ShapeDtypeStruct((t, d), f32), jax.ShapeDtypeStruct((ns, t, fs), bf16),
                   jax.ShapeDtypeStruct((ns, t, fs), bf16)],
        compiler_params=_cparams(),
    )(x, ng, wg, wu, wd)


def ffn_fwd_loss(x, ng, wg, wu, wd, fng, tgt, name):
    t, d = x.shape
    ns, fs, _ = wg.shape
    tm = _ffn_tile(t)

    def body(x_ref, ng_ref, wg_ref, wu_ref, wd_ref, fng_ref, t_ref, loss_ref, dx_ref, dfn_ref, g_ref, u_ref):
        _zero_at_first_step(loss_ref, dfn_ref)
        x3 = _ffn_fwd_rows(x_ref[...], ng_ref, wg_ref, wu_ref, wd_ref, g_ref, u_ref)
        y, xh, r = _rms(x3, fng_ref[...])
        diff = y - t_ref[...]
        part = 0.5 * jnp.sum(jnp.sum(diff * diff, axis=0, keepdims=True), axis=1, keepdims=True) / d
        loss_ref[...] += jnp.broadcast_to(part, (1, 128))
        dy = diff * (1.0 / d)
        dx_ref[...] = _rms_bwd(dy, xh, r, fng_ref[...])
        dfn_ref[...] += jnp.sum(dy * xh, axis=0, keepdims=True)

    row = pl.BlockSpec((tm, d), lambda i: (i, 0))
    vec = pl.BlockSpec((1, d), lambda i: (0, 0))
    shard = pl.BlockSpec((ns, tm, fs), lambda i: (0, i, 0))
    wspec = pl.BlockSpec((ns, fs, d), lambda i: (0, 0, 0), pipeline_mode=pl.Buffered(1))
    return pl.pallas_call(
        body, name=name, grid=(t // tm,),
        in_specs=[row, vec, wspec, wspec, wspec, vec, row],
        out_specs=[pl.BlockSpec((1, 128), lambda i: (0, 0)), row, vec, shard, shard],
        out_shape=[jax.ShapeDtypeStruct((1, 128), f32), jax.ShapeDtypeStruct((t, d), f32), jax.ShapeDtypeStruct((1, d), f32),
                   jax.ShapeDtypeStruct((ns, t, fs), bf16), jax.ShapeDtypeStruct((ns, t, fs), bf16)],
        compiler_params=_cparams(),
    )(x, ng, wg, wu, wd, fng, tgt)


def ffn_bwd_act(dxo, x, ng, g, u, wg, wu, wd, name, dep):
    t, d = x.shape
    ns, fs, _ = wg.shape
    tm = _ffn_tile(t)

    def body(dxo_ref, x_ref, ng_ref, g_ref, u_ref, wg_ref, wu_ref, wd_ref, dep_ref,
             dx_ref, dg_ref, du_ref, act_ref, hb_ref, dyb_ref, dng_ref):
        _zero_at_first_step(dng_ref)
        dxo = dxo_ref[...]
        dyb = (0.5 * dxo).astype(bf16)
        dyb_ref[...] = dyb
        dh = None
        for s in range(ns):
            dact = _dot_nt(dyb, wd_ref[s])
            gg = g_ref[s].astype(f32)
            uu = u_ref[s].astype(f32)
            sg = _sigmoid(gg)
            sil = gg * sg
            dgb = (dact * uu * _dsilu(gg, sg)).astype(bf16)
            dub = (dact * sil).astype(bf16)
            dg_ref[s] = dgb
            du_ref[s] = dub
            act_ref[s] = (sil * uu).astype(bf16)
            part = _dot(dgb, wg_ref[s]) + _dot(dub, wu_ref[s])
            dh = part if dh is None else dh + part
        y, xh, r = _rms(x_ref[...], ng_ref[...])
        hb_ref[...] = y.astype(bf16)
        dx_ref[...] = dxo + _rms_bwd(dh, xh, r, ng_ref[...])
        dng_ref[...] += jnp.sum(dh * xh, axis=0, keepdims=True)

    row = pl.BlockSpec((tm, d), lambda i: (i, 0))
    shard = pl.BlockSpec((ns, tm, fs), lambda i: (0, i, 0))
    wspec = pl.BlockSpec((ns, fs, d), lambda i: (0, 0, 0), pipeline_mode=pl.Buffered(1))
    vec = pl.BlockSpec((1, d), lambda i: (0, 0))
    return pl.pallas_call(
        body, name=name, grid=(t // tm,),
        in_specs=[row, row, vec, shard, shard, wspec, wspec, wspec, _ANY],
        out_specs=[row, shard, shard, shard, row, row, vec],
        out_shape=[jax.ShapeDtypeStruct((t, d), f32)] + [jax.ShapeDtypeStruct((ns, t, fs), bf16)] * 3
        + [jax.ShapeDtypeStruct((t, d), bf16)] * 2 + [jax.ShapeDtypeStruct((1, d), f32)],
        compiler_params=_cparams(VMEM_LIMIT_WIDE),
    )(dxo, x, ng, g, u, wg, wu, wd, dep)


def tn_matmul(xs, ys, x_spec, y_specs, n_shards, k1, k2s, t, tm, name, dep):
    k2 = sum(k2s)
    ny = len(ys)

    def body(*refs):
        x_ref = refs[0]
        y_refs = refs[1:1 + ny]
        steps = t // tm
        o_ref, acc = (refs[-1], None) if steps == 1 else (refs[-2], refs[-1])
        i = pl.program_id(1)
        xb = x_ref[0] if len(x_ref.shape) == 3 else x_ref[...]
        if steps > 1:
            @pl.when(i == 0)
            def _():
                acc[...] = jnp.zeros_like(acc)

        off = 0
        for y_ref, w in zip(y_refs, k2s):
            yb = y_ref[0] if len(y_ref.shape) == 3 else y_ref[...]
            part = _dot_tn(xb, yb)
            if steps == 1:
                o_ref[0, :, off:off + w] = part.astype(bf16)
            else:
                acc[:, off:off + w] += part
            off += w

        if steps > 1:
            @pl.when(i == steps - 1)
            def _():
                o_ref[0] = acc[...].astype(bf16)

    return pl.pallas_call(
        body, name=name, grid=(n_shards, t // tm),
        in_specs=[x_spec] + list(y_specs) + [_ANY],
        out_specs=pl.BlockSpec((1, k1, k2), lambda s, i: (s, 0, 0)),
        out_shape=jax.ShapeDtypeStruct((n_shards, k1, k2), bf16),
        scratch_shapes=[pltpu.VMEM((k1, k2), f32)] if t // tm > 1 else [],
        compiler_params=_cparams(),
    )(xs, *ys, dep)


def _pair_shards(w):
    s4, fs, d = w.shape
    return w.reshape(s4 // 2, 2 * fs, d)


def ffn_weight_grads(hb, dyb, dg, du, act, name, dep, each=None):
    t, d = hb.shape
    s2, _, fs2 = dg.shape
    tm = t
    row = pl.BlockSpec((tm, d), lambda s, i: (i, 0))
    shard = pl.BlockSpec((1, tm, fs2), lambda s, i: (s, i, 0))
    grads = []
    for xa, ya, which in ((dg, hb, "w_gate"), (du, hb, "w_up"), (act, dyb, "w_down")):
        g = tn_matmul(xa, [ya], shard, [row], s2, fs2, [d], t, tm, name + "_" + which, dep)
        g = g.reshape(2 * s2, fs2 // 2, d)
        if each is not None:
            dep = each(which, g)
        grads.append(g)
    return grads


def ffn_weight_grads_one_call(hb, dyb, dg, du, act, name, dep):
    t, d = hb.shape
    ns, _, fs = dg.shape
    steps = [(m, s) for m in range(3) for s in range(ns)]

    def body(hb_hbm, dyb_hbm, dg_hbm, du_hbm, act_hbm, dep_ref, gwg_hbm, gwu_hbm, gwd_hbm,
             y_buf, x_buf, o_buf, y_sems, x_sems, o_sems):
        srcs = (dg_hbm, du_hbm, act_hbm)
        dsts = (gwg_hbm, gwu_hbm, gwd_hbm)
        y_copies = [pltpu.make_async_copy(hb_hbm, y_buf.at[0], y_sems.at[0]),
                    pltpu.make_async_copy(dyb_hbm, y_buf.at[1], y_sems.at[1])]

        def x_copy(j):
            m, s = steps[j]
            return pltpu.make_async_copy(srcs[m].at[s], x_buf.at[j % 2], x_sems.at[j % 2])

        x_copy(0).start()
        y_copies[0].start()
        y_copies[1].start()
        out_copies = [None, None]
        for j, (m, s) in enumerate(steps):
            if j + 1 < len(steps):
                x_copy(j + 1).start()
            x_copy(j).wait()
            if j == 0:
                y_copies[0].wait()
            if (m, s) == (2, 0):
                y_copies[1].wait()
            if out_copies[j % 2] is not None:
                out_copies[j % 2].wait()
            o_buf[j % 2] = _dot_tn(x_buf[j % 2], y_buf[1 if m == 2 else 0]).astype(bf16)
            out_copies[j % 2] = pltpu.make_async_copy(o_buf.at[j % 2], dsts[m].at[s], o_sems.at[j % 2])
            out_copies[j % 2].start()
        for cp in out_copies:
            cp.wait()

    outs = pl.pallas_call(
        body, name=name,
        in_specs=[_ANY] * 6, out_specs=[_ANY] * 3,
        out_shape=[jax.ShapeDtypeStruct((ns, fs, d), bf16)] * 3,
        scratch_shapes=[pltpu.VMEM((2, t, d), bf16), pltpu.VMEM((2, t, fs), bf16), pltpu.VMEM((2, fs, d), bf16),
                        pltpu.SemaphoreType.DMA((2,)), pltpu.SemaphoreType.DMA((2,)), pltpu.SemaphoreType.DMA((2,))],
        compiler_params=_cparams(VMEM_LIMIT_WIDE),
    )(hb, dyb, dg, du, act, dep)
    return [g.reshape(2 * ns, fs // 2, d) for g in outs]


def tn_matmuls_one_call(x_shared, xs, ys, out_shapes, place, name, dep):
    n = len(ys)
    t, kx = (x_shared if x_shared is not None else xs[0]).shape
    ky = ys[0].shape[1]
    nx = 0 if x_shared is not None else n
    nout = len(out_shapes)

    def body(*refs):
        if x_shared is not None:
            xsh_hbm, refs = refs[0], refs[1:]
        x_hbm, y_hbm = refs[:nx], refs[nx:nx + n]
        out_hbm = refs[nx + n + 1:nx + n + 1 + nout]
        x_buf, y_buf, o_buf, x_sems, y_sems, o_sems, first_sems = refs[nx + n + 1 + nout:]

        def loads(j):
            cps = [pltpu.make_async_copy(y_hbm[j], y_buf.at[j % 2], y_sems.at[j % 2])]
            if x_shared is None:
                cps.append(pltpu.make_async_copy(x_hbm[j], x_buf.at[j % 2], x_sems.at[j % 2]))
            return cps

        half = t // 2
        first = []
        for r in range(2):
            rows = pl.ds(r * half, half)
            x_src = xsh_hbm if x_shared is not None else x_hbm[0]
            first.append((rows, [pltpu.make_async_copy(x_src.at[rows], x_buf.at[0, rows], first_sems.at[2 * r]),
                                 pltpu.make_async_copy(y_hbm[0].at[rows], y_buf.at[0, rows], first_sems.at[2 * r + 1])]))
        for _, cps in first:
            for cp in cps:
                cp.start()
        out_copies = [None, None]
        for j in range(n):
            if j + 1 < n:
                for cp in loads(j + 1):
                    cp.start()
            if j == 0:
                res = None
                for rows, cps in first:
                    for cp in cps:
                        cp.wait()
                    part = _dot_tn(x_buf[0, rows], y_buf[0, rows])
                    res = part if res is None else res + part
            else:
                for cp in loads(j):
                    cp.wait()
                res = _dot_tn(x_buf[0] if x_shared is not None else x_buf[j % 2], y_buf[j % 2])
            if out_copies[j % 2] is not None:
                out_copies[j % 2].wait()
            o_buf[j % 2] = res.astype(bf16)
            o, idx = place(j)
            out_copies[j % 2] = pltpu.make_async_copy(o_buf.at[j % 2], out_hbm[o].at[idx], o_sems.at[j % 2])
            out_copies[j % 2].start()
        for cp in out_copies:
            if cp is not None:
                cp.wait()

    operands = ([x_shared] if x_shared is not None else list(xs)) + list(ys) + [dep]
    outs = pl.pallas_call(
        body, name=name,
        in_specs=[_ANY] * len(operands), out_specs=[_ANY] * nout,
        out_shape=[jax.ShapeDtypeStruct(s, bf16) for s in out_shapes],
        scratch_shapes=[pltpu.VMEM((1 if x_shared is not None else 2, t, kx), bf16), pltpu.VMEM((2, t, ky), bf16),
                        pltpu.VMEM((2, kx, ky), bf16),
                        pltpu.SemaphoreType.DMA((2,)), pltpu.SemaphoreType.DMA((2,)), pltpu.SemaphoreType.DMA((2,)),
                        pltpu.SemaphoreType.DMA((4,))],
        compiler_params=_cparams(VMEM_LIMIT_WIDE),
    )(*operands)
    return list(outs)


def inproj_fwd(x1, ng, win, bin4, cos, sin, sng, snb, ws, bs):
    t, d = x1.shape
    s4, _, w2 = win.shape
    tm = _row_tile(t)
    dk = d // RET_HEADS
    scale = dk ** -0.5

    def body(x_ref, ng_ref, w_ref, b_ref, cos_ref, sin_ref, sng_ref, snb_ref, ws_ref, bs_ref, p_ref, hb_ref, a_ref):
        y, _, _ = _rms(x_ref[...], ng_ref[...])
        h = y.astype(bf16)
        hb_ref[...] = h
        uv = None
        for s in range(s4):
            p = _dot(h, w_ref[s]) + b_ref[s]
            if s == 0:
                uv = p.astype(bf16)
                p_ref[s] = uv
            elif s != 1:
                p_ref[s] = p.astype(bf16)
            else:
                cs, sn = cos_ref[...], sin_ref[...]
                for e in range(2 * RET_HEADS):
                    cols = slice(e * dk, (e + 1) * dk)
                    rot = _rot(p[:, cols], cs, sn)
                    p_ref[s, :, cols] = (rot if e < RET_HEADS else rot * scale).astype(bf16)
        _sgu_rows(uv[:, 0:d].astype(f32), uv[:, d:w2].astype(f32), sng_ref, snb_ref, ws_ref, bs_ref, a_ref)

    tab = pl.BlockSpec((tm, dk // 2), lambda i: (i, 0))
    row = pl.BlockSpec((tm, d), lambda i: (i, 0))
    vec = pl.BlockSpec((1, d), lambda i: (0, 0))
    return pl.pallas_call(
        body, name="inproj_fwd", grid=(t // tm,),
        in_specs=[row, vec, pl.BlockSpec((s4, d, w2), lambda i: (0, 0, 0), pipeline_mode=pl.Buffered(1)),
                  pl.BlockSpec((s4, 1, w2), lambda i: (0, 0, 0)), tab, tab, vec, vec,
                  pl.BlockSpec((SGU_GROUPS, SGU_CHUNK, SGU_CHUNK), lambda i: (0, 0, 0)),
                  pl.BlockSpec((SGU_GROUPS, SGU_CHUNK, 1), lambda i: (0, 0, 0))],
        out_specs=[pl.BlockSpec((s4, tm, w2), lambda i: (0, i, 0)), row, row],
        out_shape=[jax.ShapeDtypeStruct((s4, t, w2), bf16), jax.ShapeDtypeStruct((t, d), bf16),
                   jax.ShapeDtypeStruct((t, d), bf16)],
        compiler_params=_cparams(),
    )(x1, ng, win, bin4, cos, sin, sng, snb, ws, bs)


def _sgu_norm(va, ng, nb):
    gv = _gelu(va)
    mu = jnp.mean(gv, axis=-1, keepdims=True)
    xc = gv - mu
    rstd = lax.rsqrt(jnp.mean(xc * xc, axis=-1, keepdims=True) + NORM_EPS)
    xh = xc * rstd
    return xh, rstd, (xh * ng + nb).astype(bf16)


def _sgu_rows(ua, va, ng_ref, nb_ref, ws_ref, bs_ref, a_ref):
    tm, d = ua.shape
    gd = d // SGU_GROUPS
    gu = _gelu(ua)
    _, _, vn = _sgu_norm(va, ng_ref[...], nb_ref[...])
    for c in range(tm // SGU_CHUNK):
        rows = slice(c * SGU_CHUNK, (c + 1) * SGU_CHUNK)
        for g in range(SGU_GROUPS):
            cols = slice(g * gd, (g + 1) * gd)
            sg = _dot(ws_ref[g], vn[rows, cols]) + bs_ref[g]
            a_ref[rows, cols] = (gu[rows, cols] * sg).astype(bf16)


def _sgu_bwd_rows(dad, ua, va, ng_ref, nb_ref, ws_ref, bs_ref, dua_ref, dva_ref, dws_ref, dbs_ref, dng_ref, dnb_ref,
                  dvn_scr):
    tm, d = ua.shape
    gd = d // SGU_GROUPS
    gu = _gelu(ua)
    xh, rstd, vn = _sgu_norm(va, ng_ref[...], nb_ref[...])
    dsb = (dad * gu).astype(bf16)
    for c in range(tm // SGU_CHUNK):
        rows = slice(c * SGU_CHUNK, (c + 1) * SGU_CHUNK)
        for g in range(SGU_GROUPS):
            cols = slice(g * gd, (g + 1) * gd)
            sg = _dot(ws_ref[g], vn[rows, cols]) + bs_ref[g]
            dua_ref[rows, cols] = (dad[rows, cols] * sg * _dgelu(ua[rows, cols])).astype(bf16)
            ds = dsb[rows, cols]
            dvn_scr[rows, cols] = _dot_tn(ws_ref[g], ds)
            dws_ref[g] += _dot_nt(ds, vn[rows, cols])
            dbs_ref[g] += jnp.sum(ds.astype(f32), axis=1, keepdims=True)
    dvn = dvn_scr[...]
    dng_ref[...] += jnp.sum(dvn * xh, axis=0, keepdims=True)
    dnb_ref[...] += jnp.sum(dvn, axis=0, keepdims=True)
    dxh = dvn * ng_ref[...]
    dgv = rstd * (dxh - jnp.mean(dxh, axis=-1, keepdims=True) - xh * jnp.mean(dxh * xh, axis=-1, keepdims=True))
    dva_ref[...] = (dgv * _dgelu(va)).astype(bf16)


def retention_constants(decay_logit, t, dk, zero):
    lg = jax.nn.log_sigmoid(decay_logit.astype(f32) + zero)
    lgf = lg[0][:, None]
    lgb = lg[1][:, None]
    idx = jnp.arange(CHUNK, dtype=f32)[None, :]
    af = jnp.exp((idx + 1.0) * lgf)
    ab = jnp.exp((CHUNK - idx) * lgb)
    kf = jnp.exp((CHUNK - 1.0 - idx) * lgf)
    kb = jnp.exp(idx * lgb)
    cols = jnp.stack([af, ab, kf, kb, af * (idx + 1.0), ab * (CHUNK - idx), kf * (CHUNK - 1.0 - idx), kb * idx], axis=1)
    cols = cols[..., None]
    diff = idx[0][:, None] - idx[0][None, :]
    dfm = jnp.where(diff >= 0, jnp.exp(jnp.maximum(diff, 0.0)[None] * lgf[:, :, None]), 0.0)
    dbm = jnp.where(diff < 0, jnp.exp(jnp.maximum(-diff, 0.0)[None] * lgb[:, :, None]), 0.0)
    mats = jnp.stack([dfm + dbm, dfm * diff[None], dbm * (-diff)[None]], axis=1)
    cdec = jnp.stack([jnp.broadcast_to(jnp.exp(CHUNK * lgf), (RET_HEADS, dk)),
                      jnp.broadcast_to(jnp.exp(CHUNK * lgb), (RET_HEADS, dk))], axis=1)
    theta = ROPE_BASE ** (-jnp.arange(0, dk, 2, dtype=f32) / dk)
    ang = (jnp.arange(t, dtype=f32) + zero)[:, None] * theta[None, :]
    return cols, mats, cdec, jnp.cos(ang), jnp.sin(ang)


def _rot(tr, cos, sin):
    half = tr.shape[-1] // 2
    t1 = tr[:, :half]
    t2 = tr[:, half:]
    return jnp.concatenate([t1 * cos - t2 * sin, t2 * cos + t1 * sin], axis=-1)


def _rot_inv(dt, cos, sin):
    half = dt.shape[-1] // 2
    d1 = dt[:, :half]
    d2 = dt[:, half:]
    return jnp.concatenate([d1 * cos + d2 * sin, d2 * cos - d1 * sin], axis=-1)


def _ret_tile(t):
    return 2048 if t >= 4096 else _row_tile(t)


def _ret_specs(t, d, dk, rt):
    nr = t // rt
    hq = d // dk

    def blk(p, n):
        return (1 - p) * (nr - 1 - n) + p * n

    q_spec = pl.BlockSpec((1, rt, dk), lambda h, p, n: (1, blk(p, n), h))
    k_spec = pl.BlockSpec((1, rt, dk), lambda h, p, n: (1, blk(p, n), hq + h))
    v_spec = pl.BlockSpec((1, rt, dk), lambda h, p, n: (2, blk(p, n), h))
    g_spec = pl.BlockSpec((1, rt, dk), lambda h, p, n: (2, blk(p, n), hq + h))
    tab_spec = pl.BlockSpec((rt, dk // 2), lambda h, p, n: (blk(p, n), 0))
    cols_spec = pl.BlockSpec((1, 8, CHUNK, 1), lambda h, p, n: (h, 0, 0, 0))
    mats_spec = pl.BlockSpec((1, 3, CHUNK, CHUNK), lambda h, p, n: (h, 0, 0, 0))
    cdec_spec = pl.BlockSpec((1, 2, dk), lambda h, p, n: (h, 0, 0))
    in_row = pl.BlockSpec((rt, dk), lambda h, p, n: (blk(p, n), h))
    out_row = pl.BlockSpec((rt, dk), lambda h, p, n: (p * n, h))
    return nr, blk, q_spec, k_spec, v_spec, g_spec, tab_spec, cols_spec, mats_spec, cdec_spec, in_row, out_row


def ret_fwd(proj, cols, mats, cdec, dep):
    _, t, w2 = proj.shape
    d = w2 // 2
    dk = d // RET_HEADS
    rt = _ret_tile(t)
    cpt = rt // CHUNK
    nr, blk, q_spec, k_spec, v_spec, g_spec, _, cols_spec, mats_spec, cdec_spec, _, out_row = _ret_specs(t, d, dk, rt)

    def body(q_ref, k_ref, v_ref, g_ref, cols_ref, mats_ref, cdec_ref, dep_ref, r_ref, rn_ref, sb_scr, st):
        p = pl.program_id(1)
        n = pl.program_id(2)
        af, ab, kf, kb = cols_ref[0, 0], cols_ref[0, 1], cols_ref[0, 2], cols_ref[0, 3]
        cf = cdec_ref[0, 0:1, :]
        cb = cdec_ref[0, 1:2, :]

        @pl.when(n == 0)
        def _():
            st[...] = jnp.zeros_like(st)

        @pl.when(p == 0)
        def _():
            for j in reversed(range(cpt)):
                rows = slice(j * CHUNK, (j + 1) * CHUNK)
                ch = blk(p, n) * cpt + j
                kk = k_ref[0, rows, :].astype(f32)
                sb_scr[ch] = st[...].astype(bf16)
                st[...] = st[...] * cb + _dot_tn((kk * kb).astype(bf16), v_ref[0, rows, :])

        @pl.when(p == 1)
        def _():
            for j in range(cpt):
                rows = slice(j * CHUNK, (j + 1) * CHUNK)
                ch = blk(p, n) * cpt + j
                qb = q_ref[0, rows, :]
                kkb = k_ref[0, rows, :]
                q = qb.astype(f32)
                kk = kkb.astype(f32)
                v = v_ref[0, rows, :]
                pm = (_dot_nt(qb, kkb) * mats_ref[0, 0]).astype(bf16)
                out = (_dot(pm, v) + _dot((q * af).astype(bf16), st[...].astype(bf16))
                       + _dot((q * ab).astype(bf16), sb_scr[ch]))
                st[...] = st[...] * cf + _dot_tn((kk * kf).astype(bf16), v)
                rhat = out * lax.rsqrt(jnp.mean(out * out, axis=-1, keepdims=True) + NORM_EPS)
                gg = g_ref[0, rows, :].astype(f32)
                r_ref[rows, :] = out.astype(bf16)
                rn_ref[rows, :] = (rhat * gg * _sigmoid(gg)).astype(bf16)

    return pl.pallas_call(
        body, name="ret_fwd", grid=(RET_HEADS, 2, nr),
        in_specs=[q_spec, k_spec, v_spec, g_spec, cols_spec, mats_spec, cdec_spec, _ANY],
        out_specs=[out_row, out_row],
        out_shape=[jax.ShapeDtypeStruct((t, d), bf16), jax.ShapeDtypeStruct((t, d), bf16)],
        scratch_shapes=[pltpu.VMEM((t // CHUNK, dk, dk), bf16), pltpu.VMEM((dk, dk), f32)],
        compiler_params=_cparams(),
    )(proj, proj, proj, proj, cols, mats, cdec, dep)


def ret_bwd(drn, r, proj, cols, mats, cdec, cos, sin):
    _, t, w2 = proj.shape
    d = w2 // 2
    dk = d // RET_HEADS
    rt = _ret_tile(t)
    cpt = rt // CHUNK
    nr, blk, q_spec, k_spec, v_spec, g_spec, tab_spec, cols_spec, mats_spec, cdec_spec, in_row, out_row = _ret_specs(t, d, dk, rt)
    scale = dk ** -0.5

    def body(drn_ref, r_ref, q_ref, k_ref, v_ref, g_ref, cos_ref, sin_ref, cols_ref, mats_ref, cdec_ref,
             dq_ref, dk_ref, dv_ref, dg_ref, dlg_ref,
             sb_scr, gf_scr, st_s, st_g, acc_af, acc_ab, acc_vf, acc_vb, acc_sf, acc_sb, dout_scr, dgr_scr):
        p = pl.program_id(1)
        n = pl.program_id(2)
        af, ab, kf, kb = cols_ref[0, 0], cols_ref[0, 1], cols_ref[0, 2], cols_ref[0, 3]
        af1, ab1, kf1, kb1 = cols_ref[0, 4], cols_ref[0, 5], cols_ref[0, 6], cols_ref[0, 7]
        cf = cdec_ref[0, 0:1, :]
        cb = cdec_ref[0, 1:2, :]

        @pl.when(n == 0)
        def _():
            st_s[...] = jnp.zeros_like(st_s)
            st_g[...] = jnp.zeros_like(st_g)

        @pl.when(jnp.logical_and(n == 0, p == 1))
        def _():
            for a in (acc_af, acc_ab, acc_vf, acc_vb, acc_sf, acc_sb):
                a[...] = jnp.zeros_like(a)

        def load(rows):
            cs, sn = cos_ref[rows, :], sin_ref[rows, :]
            q = q_ref[0, rows, :].astype(f32)
            kk = k_ref[0, rows, :].astype(f32)
            rr = r_ref[rows, :].astype(f32)
            rstd = lax.rsqrt(jnp.mean(rr * rr, axis=-1, keepdims=True) + NORM_EPS)
            rhat = rr * rstd
            gg = g_ref[0, rows, :].astype(f32)
            sg = _sigmoid(gg)
            dd = drn_ref[rows, :].astype(f32)
            drhat = dd * gg * sg
            dout = rstd * (drhat - rhat * jnp.mean(drhat * rhat, axis=-1, keepdims=True))
            dgr = dd * rhat * _dsilu(gg, sg)
            return q, kk, dout.astype(bf16), dgr, cs, sn

        @pl.when(p == 0)
        def _():
            for j in reversed(range(cpt)):
                rows = slice(j * CHUNK, (j + 1) * CHUNK)
                ch = blk(p, n) * cpt + j
                q, kk, doutb, dgr, _, _ = load(rows)
                kept = pl.ds(pl.multiple_of(ch * CHUNK, CHUNK), CHUNK)
                dout_scr[kept, :] = doutb
                dgr_scr[kept, :] = dgr.astype(bf16)
                sb_scr[ch] = st_s[...].astype(bf16)
                gf_scr[ch] = st_g[...].astype(bf16)
                st_s[...] = st_s[...] * cb + _dot_tn((kk * kb).astype(bf16), v_ref[0, rows, :])
                st_g[...] = st_g[...] * cf + _dot_tn((q * af).astype(bf16), doutb)

        @pl.when(p == 1)
        def _():
            for j in range(cpt):
                rows = slice(j * CHUNK, (j + 1) * CHUNK)
                ch = blk(p, n) * cpt + j
                kept = pl.ds(pl.multiple_of(ch * CHUNK, CHUNK), CHUNK)
                doutb = dout_scr[kept, :]
                cs, sn = cos_ref[rows, :], sin_ref[rows, :]
                v = v_ref[0, rows, :]
                qb = q_ref[0, rows, :]
                kkb = k_ref[0, rows, :]
                q = qb.astype(f32)
                kk = kkb.astype(f32)
                sf = st_s[...]
                gb = st_g[...]
                sfb = sf.astype(bf16)
                gbb = gb.astype(bf16)
                sbb = sb_scr[ch]
                gfb = gf_scr[ch]
                dmat = mats_ref[0, 0]
                scores = _dot_nt(qb, kkb)
                dpraw = _dot_nt(doutb, v)
                dpb = (dpraw * dmat).astype(bf16)
                pmb = (scores * dmat).astype(bf16)
                x1 = _dot_nt(doutb, sfb)
                x2 = _dot_nt(doutb, sbb)
                y1 = _dot_nt(v, gfb)
                y2 = _dot_nt(v, gbb)
                kdf = (kk * kf).astype(bf16)
                kdb = (kk * kb).astype(bf16)
                dq = _dot(dpb, kkb) + x1 * af + x2 * ab
                dkk = _dot_tn(dpb, qb) + y1 * kf + y2 * kb
                dv = _dot_tn(pmb, doutb) + _dot(kdf, gfb) + _dot(kdb, gbb)
                ps = dpraw * scores
                acc_af[...] += ps * mats_ref[0, 1]
                acc_ab[...] += ps * mats_ref[0, 2]
                acc_vf[...] += x1 * q * af1 + y1 * kk * kf1
                acc_vb[...] += x2 * q * ab1 + y2 * kk * kb1
                acc_sf[...] += gfb.astype(f32) * sf
                acc_sb[...] += gb * sbb.astype(f32)
                st_s[...] = sf * cf + _dot_tn(kdf, v)
                st_g[...] = gb * cb + _dot_tn((q * ab).astype(bf16), doutb)
                dq_ref[rows, :] = _rot_inv(dq, cs, sn).astype(bf16)
                dk_ref[rows, :] = (_rot_inv(dkk, cs, sn) * scale).astype(bf16)
                dv_ref[rows, :] = dv.astype(bf16)
                dg_ref[rows, :] = dgr_scr[kept, :]

        @pl.when(jnp.logical_and(p == 1, n == nr - 1))
        def _():
            tf = jnp.sum(acc_af[...]) + jnp.sum(acc_vf[...]) + CHUNK * jnp.sum(acc_sf[...] * cf)
            tb = jnp.sum(acc_ab[...]) + jnp.sum(acc_vb[...]) + CHUNK * jnp.sum(acc_sb[...] * cb)
            rid = lax.broadcasted_iota(jnp.int32, (8, 128), 0)
            dlg_ref[0] = jnp.where(rid == 0, tf, jnp.where(rid == 1, tb, 0.0))

    nch = t // CHUNK
    return pl.pallas_call(
        body, name="ret_bwd", grid=(RET_HEADS, 2, nr),
        in_specs=[in_row, in_row, q_spec, k_spec, v_spec, g_spec, tab_spec, tab_spec, cols_spec, mats_spec, cdec_spec],
        out_specs=[out_row, out_row, out_row, out_row, pl.BlockSpec((1, 8, 128), lambda h, p, n: (h, 0, 0))],
        out_shape=[jax.ShapeDtypeStruct((t, d), bf16)] * 4 + [jax.ShapeDtypeStruct((RET_HEADS, 8, 128), f32)],
        scratch_shapes=[pltpu.VMEM((nch, dk, dk), bf16), pltpu.VMEM((nch, dk, dk), bf16),
                        pltpu.VMEM((dk, dk), f32), pltpu.VMEM((dk, dk), f32),
                        pltpu.VMEM((CHUNK, CHUNK), f32), pltpu.VMEM((CHUNK, CHUNK), f32),
                        pltpu.VMEM((CHUNK, dk), f32), pltpu.VMEM((CHUNK, dk), f32),
                        pltpu.VMEM((dk, dk), f32), pltpu.VMEM((dk, dk), f32),
                        pltpu.VMEM((t, dk), bf16), pltpu.VMEM((t, dk), bf16)],
        compiler_params=_cparams(VMEM_LIMIT_WIDE),
    )(drn, r, proj, proj, proj, proj, cos, sin, cols, mats, cdec)


def mix_fwd(a, rn, proj, wa, wb, wo, x1):
    t, d = x1.shape
    tm = _row_tile(t)

    def body(a_ref, rn_ref, p_ref, wa_ref, wb_ref, wo_ref, x_ref, xo_ref, ba_ref, br_ref):
        ba = _dot(a_ref[...], wa_ref[...])
        br = _dot(rn_ref[...], wb_ref[...])
        sa = _sigmoid(p_ref[0, :, 0:d].astype(f32))
        sb = _sigmoid(p_ref[0, :, d:2 * d].astype(f32))
        mix = (sa * ba + sb * br).astype(bf16)
        xo_ref[...] = x_ref[...] + _dot(mix, wo_ref[...])
        ba_ref[...] = ba.astype(bf16)
        br_ref[...] = br.astype(bf16)

    row = pl.BlockSpec((tm, d), lambda i: (i, 0))
    wsp = pl.BlockSpec((d, d), lambda i: (0, 0))
    return pl.pallas_call(
        body, name="mix_fwd", grid=(t // tm,),
        in_specs=[row, row, pl.BlockSpec((1, tm, 2 * d), lambda i: (3, i, 0)), wsp, wsp, wsp, row],
        out_specs=[row, row, row],
        out_shape=[jax.ShapeDtypeStruct((t, d), f32), jax.ShapeDtypeStruct((t, d), bf16), jax.ShapeDtypeStruct((t, d), bf16)],
        compiler_params=_cparams(),
    )(a, rn, proj, wa, wb, wo, x1)


def mix_bwd_act(dx2, ba, br, proj, wa, wb, wo, sng, snb, ws, bs, dep):
    t, d = dx2.shape
    tm = _row_tile(t)

    def body(dx_ref, ba_ref, br_ref, p_ref, uv_ref, wa_ref, wb_ref, wo_ref, sng_ref, snb_ref, ws_ref, bs_ref, dep_ref,
             drn_ref, dga_ref, dgb_ref, mix_ref, dba_ref, dbr_ref, dxb_ref,
             dua_ref, dva_ref, dws_ref, dbs_ref, dng_ref, dnb_ref, dvn_scr):
        _zero_at_first_step(dws_ref, dbs_ref, dng_ref, dnb_ref)
        dxb = dx_ref[...].astype(bf16)
        dxb_ref[...] = dxb
        dmix = _dot_nt(dxb, wo_ref[...])
        ba = ba_ref[...].astype(f32)
        br = br_ref[...].astype(f32)
        sa = _sigmoid(p_ref[0, :, 0:d].astype(f32))
        sb = _sigmoid(p_ref[0, :, d:2 * d].astype(f32))
        mix_ref[...] = (sa * ba + sb * br).astype(bf16)
        dba = (dmix * sa).astype(bf16)
        dbr = (dmix * sb).astype(bf16)
        dba_ref[...] = dba
        dbr_ref[...] = dbr
        dga_ref[...] = (dmix * ba * sa * (1.0 - sa)).astype(bf16)
        dgb_ref[...] = (dmix * br * sb * (1.0 - sb)).astype(bf16)
        drn_ref[...] = _dot_nt(dbr, wb_ref[...]).astype(bf16)
        da = _dot_nt(dba, wa_ref[...])
        _sgu_bwd_rows(da, uv_ref[0, :, 0:d].astype(f32), uv_ref[0, :, d:2 * d].astype(f32), sng_ref, snb_ref, ws_ref,
                      bs_ref, dua_ref, dva_ref, dws_ref, dbs_ref, dng_ref, dnb_ref, dvn_scr)

    row = pl.BlockSpec((tm, d), lambda i: (i, 0))
    vec = pl.BlockSpec((1, d), lambda i: (0, 0))
    wsp = pl.BlockSpec((d, d), lambda i: (0, 0))
    sws = pl.BlockSpec((SGU_GROUPS, SGU_CHUNK, SGU_CHUNK), lambda i: (0, 0, 0))
    sbs = pl.BlockSpec((SGU_GROUPS, SGU_CHUNK, 1), lambda i: (0, 0, 0))
    return pl.pallas_call(
        body, name="mix_bwd_act", grid=(t // tm,),
        in_specs=[row, row, row, pl.BlockSpec((1, tm, 2 * d), lambda i: (3, i, 0)),
                  pl.BlockSpec((1, tm, 2 * d), lambda i: (0, i, 0)), wsp, wsp, wsp, vec, vec, sws, sbs, _ANY],
        out_specs=[row] * 9 + [sws, sbs, vec, vec],
        out_shape=[jax.ShapeDtypeStruct((t, d), bf16)] * 9
        + [jax.ShapeDtypeStruct((SGU_GROUPS, SGU_CHUNK, SGU_CHUNK), f32), jax.ShapeDtypeStruct((SGU_GROUPS, SGU_CHUNK, 1), f32),
           jax.ShapeDtypeStruct((1, d), f32), jax.ShapeDtypeStruct((1, d), f32)],
        scratch_shapes=[pltpu.VMEM((tm, d), f32)],
        compiler_params=_cparams(VMEM_LIMIT_WIDE),
    )(dx2, ba, br, proj, proj, wa, wb, wo, sng, snb, ws, bs, dep)


def inproj_bwd_act(segs, win, x1, ng, dx2):
    t, d = x1.shape
    s4 = win.shape[0]
    tm = _row_tile(t)
    nseg = len(segs)

    def body(*refs):
        seg_refs = refs[:nseg]
        w_ref, x_ref, ng_ref, dx2_ref, dx1_ref, db_ref, dng_ref = refs[nseg:]
        _zero_at_first_step(db_ref, dng_ref)
        dh = None
        for e, sr in enumerate(seg_refs):
            sb = sr[...]
            part = _dot_nt(sb, w_ref[e // 2, :, (e % 2) * d:(e % 2 + 1) * d])
            dh = part if dh is None else dh + part
            db_ref[e] += jnp.sum(sb.astype(f32), axis=0, keepdims=True)
        _, xh, r = _rms(x_ref[...], ng_ref[...])
        dx1_ref[...] = dx2_ref[...] + _rms_bwd(dh, xh, r, ng_ref[...])
        dng_ref[...] += jnp.sum(dh * xh, axis=0, keepdims=True)

    row = pl.BlockSpec((tm, d), lambda i: (i, 0))
    vec = pl.BlockSpec((1, d), lambda i: (0, 0))
    return pl.pallas_call(
        body, name="inproj_bwd_act", grid=(t // tm,),
        in_specs=[row] * nseg + [pl.BlockSpec((s4, d, 2 * d), lambda i: (0, 0, 0), pipeline_mode=pl.Buffered(1)),
                                 row, vec, row],
        out_specs=[row, pl.BlockSpec((nseg, 1, d), lambda i: (0, 0, 0)), vec],
        out_shape=[jax.ShapeDtypeStruct((t, d), f32), jax.ShapeDtypeStruct((nseg, 1, d), f32),
                   jax.ShapeDtypeStruct((1, d), f32)],
        compiler_params=_cparams(VMEM_LIMIT_WIDE),
    )(*segs, win, x1, ng, dx2)


def _place():
    return lax.axis_index("x"), lax.axis_index("y"), lax.axis_index("c")


def _other_chips(x, y):
    return [(1 - x, y), (x, 1 - y), (1 - x, 1 - y)]


_ANY = pl.BlockSpec(memory_space=pl.ANY)


_HBM = pl.BlockSpec(memory_space=pltpu.HBM)
_SEM = pl.BlockSpec(memory_space=pltpu.SEMAPHORE)
_EFFECT = pltpu.SideEffectType.DATAFLOW_SIDE_EFFECTING


def _hbm(a):
    return pltpu.with_memory_space_constraint(a, pltpu.HBM)


def _half_rows(ref, c):
    half = ref.shape[1] // 2
    return pl.ds(pl.multiple_of(c * half, 16), half)


def _chip_copy(src, dst, send_sem, recv_sem, chip, c):
    return pltpu.make_async_remote_copy(src_ref=src, dst_ref=dst, send_sem=send_sem, recv_sem=recv_sem,
                                        device_id=(chip[0], chip[1], c), device_id_type=MESH)


def gather_start(bufs, groups, name):
    nb, ng = len(bufs), len(groups)

    def body(*refs):
        ins = refs[:nb]
        sems = refs[nb:nb + 2 * ng]
        token = refs[-1]
        x, y, c = _place()
        k = 2 * x + y
        for gi, grp in enumerate(groups):
            for wi, w in enumerate(grp):
                mine = ins[w].at[k, _half_rows(ins[w], c)]
                for j, chip in enumerate(_other_chips(x, y)):
                    _chip_copy(mine, mine, sems[2 * gi].at[3 * wi + j], sems[2 * gi + 1].at[3 * wi + j], chip, c).start()
        token[...] = jnp.zeros_like(token)

    sem_shapes = []
    for grp in groups:
        sem_shapes += [pltpu.SemaphoreType.DMA((3 * len(grp),)), pltpu.SemaphoreType.DMA((3 * len(grp),))]
    outs = pl.pallas_call(
        body, name=name,
        out_shape=sem_shapes + [pltpu.HBM(b.shape, b.dtype) for b in bufs] + [jax.ShapeDtypeStruct((8, 128), f32)],
        in_specs=[_HBM] * nb,
        out_specs=[_SEM] * (2 * ng) + [_HBM] * nb + [pl.BlockSpec(memory_space=pltpu.VMEM)],
        input_output_aliases={w: 2 * ng + w for w in range(nb)},
        compiler_params=pltpu.CompilerParams(has_side_effects=_EFFECT),
    )(*[_hbm(b) for b in bufs])
    sems = [(outs[2 * gi], outs[2 * gi + 1]) for gi in range(ng)]
    return sems, list(outs[2 * ng:2 * ng + nb]), outs[-1]


def gather_wait(bufs, sems, after, name):
    n = len(bufs)

    def body(*refs):
        ins = refs[:n]
        send_sems, recv_sems = refs[n], refs[n + 1]
        x, y, c = _place()
        k = 2 * x + y
        for wi in range(n):
            half = _half_rows(ins[wi], c)
            for j, chip in enumerate(_other_chips(x, y)):
                cp = _chip_copy(ins[wi].at[k, half], ins[wi].at[2 * chip[0] + chip[1], half], send_sems.at[3 * wi + j],
                                recv_sems.at[3 * wi + j], chip, c)
                cp.wait_send()
                cp.wait_recv()

    outs = pl.pallas_call(
        body, name=name,
        out_shape=[pltpu.HBM(b.shape, b.dtype) for b in bufs],
        in_specs=[_HBM] * n + [_SEM, _SEM, _ANY],
        out_specs=[_HBM] * n,
        input_output_aliases={i: i for i in range(n)},
        compiler_params=pltpu.CompilerParams(has_side_effects=_EFFECT),
    )(*bufs, sems[0], sems[1], after)
    return list(outs)


def gather_forward(bufs, name):
    n = len(bufs)

    def body(*refs):
        ins = refs[n:2 * n]
        send_sems, recv_sems = refs[2 * n], refs[2 * n + 1]
        x, y, c = _place()
        copies = []
        for wi in range(n):
            for j, chip in enumerate(_other_chips(x, y)):
                kp = 2 * chip[0] + chip[1]
                got = ins[wi].at[kp, _half_rows(ins[wi], c)]
                cp = pltpu.make_async_remote_copy(
                    src_ref=got, dst_ref=got, send_sem=send_sems.at[3 * wi + j], recv_sem=recv_sems.at[3 * wi + j],
                    device_id=(x, y, 1 - c), device_id_type=MESH)
                cp.start()
                copies.append((cp, wi, kp, j))
        for cp, wi, kp, j in copies:
            cp.wait_send()
            theirs = ins[wi].at[kp, _half_rows(ins[wi], 1 - c)]
            pltpu.make_async_remote_copy(
                src_ref=theirs, dst_ref=theirs, send_sem=send_sems.at[3 * wi + j], recv_sem=recv_sems.at[3 * wi + j],
                device_id=(x, y, 1 - c), device_id_type=MESH).wait_recv()

    outs = pl.pallas_call(
        body, name=name,
        out_shape=[jax.ShapeDtypeStruct(b.shape, b.dtype) for b in bufs],
        in_specs=[_ANY] * n, out_specs=[_ANY] * n,
        input_output_aliases={i: i for i in range(n)},
        scratch_shapes=[pltpu.SemaphoreType.DMA((3 * n,)), pltpu.SemaphoreType.DMA((3 * n,))],
    )(*bufs)
    return list(outs)


def forward_start(bufs, name):
    n = len(bufs)

    def body(*refs):
        x, y, c = _place()
        for wi in range(n):
            for j, chip in enumerate(_other_chips(x, y)):
                got = refs[wi].at[2 * chip[0] + chip[1], _half_rows(refs[wi], c)]
                _sibling_copy(got, got, refs[n].at[3 * wi + j], refs[n + 1].at[3 * wi + j]).start()
        refs[-1][...] = jnp.zeros_like(refs[-1])

    return _split_start(body, name, 3 * n, list(bufs))


def forward_wait(bufs, sems, after, name):
    n = len(bufs)

    def body(*refs):
        x, y, c = _place()
        for wi in range(n):
            for j, chip in enumerate(_other_chips(x, y)):
                kp = 2 * chip[0] + chip[1]
                got = refs[wi].at[kp, _half_rows(refs[wi], c)]
                theirs = refs[wi].at[kp, _half_rows(refs[wi], 1 - c)]
                _sibling_copy(got, got, refs[n].at[3 * wi + j], refs[n + 1].at[3 * wi + j]).wait_send()
                _sibling_copy(theirs, theirs, refs[n].at[3 * wi + j], refs[n + 1].at[3 * wi + j]).wait_recv()

    return _split_wait(body, name, list(bufs), sems, after)


def exchange_start(grads, name):
    n = len(grads)
    lands = [lax.empty((3,) + g.shape[1:], g.dtype) for g in grads]

    def body(*refs):
        ins = refs[:n]
        land = refs[n:2 * n]
        send_sems, recv_sems = refs[2 * n], refs[2 * n + 1]
        token = refs[-1]
        x, y, c = _place()
        for wi in range(n):
            for j, chip in enumerate(_other_chips(x, y)):
                _chip_copy(ins[wi].at[2 * chip[0] + chip[1]], land[wi].at[j], send_sems.at[3 * wi + j],
                           recv_sems.at[3 * wi + j], chip, c).start()
        token[...] = jnp.zeros_like(token)

    outs = pl.pallas_call(
        body, name=name,
        out_shape=[pltpu.SemaphoreType.DMA((3 * n,)), pltpu.SemaphoreType.DMA((3 * n,))]
        + [pltpu.HBM(g.shape, g.dtype) for g in grads] + [pltpu.HBM(l.shape, l.dtype) for l in lands]
        + [jax.ShapeDtypeStruct((8, 128), f32)],
        in_specs=[_HBM] * (2 * n),
        out_specs=[_SEM, _SEM] + [_HBM] * (2 * n) + [pl.BlockSpec(memory_space=pltpu.VMEM)],
        input_output_aliases={i: 2 + i for i in range(2 * n)},
        compiler_params=pltpu.CompilerParams(has_side_effects=_EFFECT),
    )(*[_hbm(g) for g in grads], *[_hbm(l) for l in lands])
    return (outs[0], outs[1]), list(outs[2:2 + n]), list(outs[2 + n:2 + 2 * n]), outs[-1]


def exchange_wait(grads, lands, sems, after, name):
    n = len(grads)

    def body(*refs):
        ins = refs[:n]
        land = refs[n:2 * n]
        send_sems, recv_sems = refs[2 * n], refs[2 * n + 1]
        x, y, c = _place()
        for wi in range(n):
            for j, chip in enumerate(_other_chips(x, y)):
                cp = _chip_copy(ins[wi].at[2 * chip[0] + chip[1]], land[wi].at[j], send_sems.at[3 * wi + j],
                                recv_sems.at[3 * wi + j], chip, c)
                cp.wait_send()
                cp.wait_recv()

    outs = pl.pallas_call(
        body, name=name,
        out_shape=[pltpu.HBM(g.shape, g.dtype) for g in grads] + [pltpu.HBM(l.shape, l.dtype) for l in lands],
        in_specs=[_HBM] * (2 * n) + [_SEM, _SEM, _ANY],
        out_specs=[_HBM] * (2 * n),
        input_output_aliases={i: i for i in range(2 * n)},
        compiler_params=pltpu.CompilerParams(has_side_effects=_EFFECT),
    )(*grads, *lands, sems[0], sems[1], after)
    return list(outs[:n]), list(outs[n:])


def _split_start(body, name, n_sems, operands):
    n = len(operands)
    outs = pl.pallas_call(
        body, name=name,
        out_shape=[pltpu.SemaphoreType.DMA((n_sems,)), pltpu.SemaphoreType.DMA((n_sems,))]
        + [pltpu.HBM(o.shape, o.dtype) for o in operands] + [jax.ShapeDtypeStruct((8, 128), f32)],
        in_specs=[_HBM] * n,
        out_specs=[_SEM, _SEM] + [_HBM] * n + [pl.BlockSpec(memory_space=pltpu.VMEM)],
        input_output_aliases={i: 2 + i for i in range(n)},
        compiler_params=pltpu.CompilerParams(has_side_effects=_EFFECT),
    )(*[_hbm(o) for o in operands])
    return (outs[0], outs[1]), list(outs[2:2 + n]), outs[-1]


def _split_wait(body, name, operands, sems, after):
    n = len(operands)
    outs = pl.pallas_call(
        body, name=name,
        out_shape=[pltpu.HBM(o.shape, o.dtype) for o in operands],
        in_specs=[_HBM] * n + [_SEM, _SEM, _ANY],
        out_specs=[_HBM] * n,
        input_output_aliases={i: i for i in range(n)},
        compiler_params=pltpu.CompilerParams(has_side_effects=_EFFECT),
    )(*operands, sems[0], sems[1], after)
    return list(outs)


def _sibling_copy(src, dst, send_sem, recv_sem):
    x, y, c = _place()
    return pltpu.make_async_remote_copy(src_ref=src, dst_ref=dst, send_sem=send_sem, recv_sem=recv_sem,
                                        device_id=(x, y, 1 - c), device_id_type=MESH)


def swap_start(parts, name):
    n = len(parts)

    def body(*refs):
        for w in range(n):
            _sibling_copy(refs[w], refs[n + w], refs[2 * n].at[w], refs[2 * n + 1].at[w]).start()
        refs[-1][...] = jnp.zeros_like(refs[-1])

    sems, ops, token = _split_start(body, name, n, list(parts) + [lax.empty(p.shape, p.dtype) for p in parts])
    return sems, ops[:n], ops[n:], token


def swap_wait(parts, lands, sems, after, name):
    n = len(parts)

    def body(*refs):
        for w in range(n):
            cp = _sibling_copy(refs[w], refs[n + w], refs[2 * n].at[w], refs[2 * n + 1].at[w])
            cp.wait_send()
            cp.wait_recv()

    outs = _split_wait(body, name, list(parts) + list(lands), sems, after)
    return outs[:n], outs[n:]


def _all_peers(x, y, c):
    return [(1 - x if m & 4 else x, 1 - y if m & 2 else y, 1 - c if m & 1 else c) for m in range(1, N_DEV)]


def small_start(block):
    land = jnp.broadcast_to(block[None], (N_DEV,) + block.shape)

    def body(b_ref, land_ref, send_sems, recv_sems, b_thru, land_thru, token):
        x, y, c = _place()
        me = 4 * x + 2 * y + c
        for m, peer in enumerate(_all_peers(x, y, c)):
            pltpu.make_async_remote_copy(src_ref=b_ref, dst_ref=land_ref.at[me], send_sem=send_sems.at[m],
                                         recv_sem=recv_sems.at[m], device_id=peer, device_id_type=MESH).start()
        token[...] = jnp.zeros_like(token)

    sems, ops, token = _split_start(body, "small_start", N_DEV - 1, [block, land])
    return sems, ops[0], ops[1], token


def small_wait(block, land, sems, after):
    def body(b_ref, land_ref, send_sems, recv_sems, after_ref, b_thru, land_thru):
        x, y, c = _place()
        for m, (px, py, pc) in enumerate(_all_peers(x, y, c)):
            cp = pltpu.make_async_remote_copy(src_ref=b_ref, dst_ref=land_ref.at[4 * px + 2 * py + pc],
                                              send_sem=send_sems.at[m], recv_sem=recv_sems.at[m],
                                              device_id=(px, py, pc), device_id_type=MESH)
            cp.wait_send()
            cp.wait_recv()

    return _split_wait(body, "small_wait", [block, land], sems, after)[1]


def _adamw(w, g, m, v):
    m = ADAM_B1 * m + (1.0 - ADAM_B1) * g
    v = ADAM_B2 * v + (1.0 - ADAM_B2) * (g * g)
    m_hat = m / (1.0 - ADAM_B1 ** ADAM_STEP)
    v_hat = v / (1.0 - ADAM_B2 ** ADAM_STEP)
    delta = -ADAM_LR * (m_hat / (jnp.sqrt(v_hat) + ADAM_EPS) + ADAM_WD * w)
    return delta, m, v


EW_BLOCK_BYTES = 2 * 1024 * 1024


def _ew_tile(rows, cols):
    for cand in (512, 352, 256, 176, 128, 64, 32, 16, 8):
        if rows % cand == 0 and cand * cols * 4 <= EW_BLOCK_BYTES:
            return cand
    return rows


def sum_partials(chip, own, land, name):
    _, r, c = own.shape
    tr = _ew_tile(r, c)

    def body(k_ref, own_ref, p_ref, o_ref):
        o_ref[...] = ((own_ref[0].astype(f32) + p_ref[0].astype(f32)) + p_ref[1].astype(f32)) + p_ref[2].astype(f32)

    return pl.pallas_call(
        body, name=name,
        grid_spec=pltpu.PrefetchScalarGridSpec(
            num_scalar_prefetch=1, grid=(r // tr,),
            in_specs=[pl.BlockSpec((1, tr, c), lambda i, k: (k[0], i, 0)), pl.BlockSpec((3, tr, c), lambda i, k: (0, i, 0))],
            out_specs=pl.BlockSpec((tr, c), lambda i, k: (i, 0))),
        out_shape=jax.ShapeDtypeStruct((r, c), f32),
        compiler_params=_cparams(),
    )(chip, own, land)


def adamw_shard(p_mine, p_sibling, w, m, v, name):
    r, c = w.shape
    tr = _ew_tile(r, c)

    def body(a_ref, b_ref, w_ref, m_ref, v_ref, g_ref, d_ref, mo_ref, vo_ref):
        g = a_ref[...] + b_ref[...]
        delta, mn, vn = _adamw(w_ref[...], g, m_ref[...], v_ref[...])
        g_ref[...] = g
        d_ref[...] = delta
        mo_ref[...] = mn
        vo_ref[...] = vn

    blk = pl.BlockSpec((tr, c), lambda i: (i, 0))
    return pl.pallas_call(
        body, name=name, grid=(r // tr,),
        in_specs=[blk] * 5, out_specs=[blk] * 4,
        out_shape=[jax.ShapeDtypeStruct((r, c), f32)] * 4,
        compiler_params=_cparams(),
    )(p_mine, p_sibling, w, m, v)


def adamw_small(g8, w, m, v):
    _, r, lanes = g8.shape

    def body(g_ref, w_ref, m_ref, v_ref, go_ref, d_ref, mo_ref, vo_ref):
        g = g_ref[0]
        for i in range(1, N_DEV):
            g = g + g_ref[i]
        delta, mn, vn = _adamw(w_ref[...], g, m_ref[...], v_ref[...])
        go_ref[...] = g
        d_ref[...] = delta
        mo_ref[...] = mn
        vo_ref[...] = vn

    return pl.pallas_call(
        body, name="adamw_small",
        out_shape=[jax.ShapeDtypeStruct((r, lanes), f32)] * 4,
        compiler_params=_cparams(),
    )(g8, w, m, v)


def _size(shape):
    n = 1
    for e in shape:
        n *= e
    return n


def _pack_rows(shapes):
    rows = [-(-_size(s) // 1024) * 8 for s in shapes]
    return rows, sum(rows)


def _pack(arrs, shapes):
    rows, _ = _pack_rows(shapes)
    parts = [jnp.pad(a.reshape(-1).astype(f32), (0, r * 128 - _size(s))).reshape(r, 128)
             for a, s, r in zip(arrs, shapes, rows)]
    return jnp.concatenate(parts, axis=0)


def _unpack(block, shapes):
    rows, _ = _pack_rows(shapes)
    out, off = [], 0
    for s, r in zip(shapes, rows):
        out.append(block[off:off + r].reshape(-1)[:_size(s)].reshape(s))
        off += r
    return out


TRANSPOSED = ("ffn1_w_gate", "ffn1_w_up", "ffn2_w_gate", "ffn2_w_up")


def _shard2d(a, n):
    return a[0].T if n in TRANSPOSED else a[0]


def _unshard(a, n):
    return (a.T if n in TRANSPOSED else a)[None]


BIG = ("ffn1_w_gate", "ffn1_w_up", "ffn1_w_down", "w_in", "w_branch_a", "w_branch_b", "w_out",
       "ffn2_w_gate", "ffn2_w_up", "ffn2_w_down")
SMALL = ("ffn1_norm", "mix_norm", "b_in", "sgu_norm_g", "sgu_norm_b", "sgu_w_s", "sgu_b_s", "ret_decay_logit",
         "ffn2_norm", "final_norm")
WEIGHTS = ("ffn1_norm", "ffn1_w_gate", "ffn1_w_up", "ffn1_w_down", "mix_norm", "w_in", "b_in", "sgu_norm_g",
           "sgu_norm_b", "sgu_w_s", "sgu_b_s", "ret_decay_logit", "w_branch_a", "w_branch_b", "w_out", "ffn2_norm",
           "ffn2_w_gate", "ffn2_w_up", "ffn2_w_down", "final_norm")


def kernel(x, ffn1_norm, ffn1_w_gate, ffn1_w_up, ffn1_w_down, mix_norm, w_in, b_in, sgu_norm_g, sgu_norm_b, sgu_w_s, sgu_b_s, ret_decay_logit, w_branch_a, w_branch_b, w_out, ffn2_norm, ffn2_w_gate, ffn2_w_up, ffn2_w_down, final_norm, loss_target, m_ffn1_norm, m_ffn1_w_gate, m_ffn1_w_up, m_ffn1_w_down, m_mix_norm, m_w_in, m_b_in, m_sgu_norm_g, m_sgu_norm_b, m_sgu_w_s, m_sgu_b_s, m_ret_decay_logit, m_w_branch_a, m_w_branch_b, m_w_out, m_ffn2_norm, m_ffn2_w_gate, m_ffn2_w_up, m_ffn2_w_down, m_final_norm, v_ffn1_norm, v_ffn1_w_gate, v_ffn1_w_up, v_ffn1_w_down, v_mix_norm, v_w_in, v_b_in, v_sgu_norm_g, v_sgu_norm_b, v_sgu_w_s, v_sgu_b_s, v_ret_decay_logit, v_w_branch_a, v_w_branch_b, v_w_out, v_ffn2_norm, v_ffn2_w_gate, v_ffn2_w_up, v_ffn2_w_down, v_final_norm):
    p = dict(ffn1_norm=ffn1_norm, ffn1_w_gate=ffn1_w_gate, ffn1_w_up=ffn1_w_up, ffn1_w_down=ffn1_w_down,
             mix_norm=mix_norm, w_in=w_in, b_in=b_in, sgu_norm_g=sgu_norm_g, sgu_norm_b=sgu_norm_b, sgu_w_s=sgu_w_s,
             sgu_b_s=sgu_b_s, ret_decay_logit=ret_decay_logit, w_branch_a=w_branch_a, w_branch_b=w_branch_b,
             w_out=w_out, ffn2_norm=ffn2_norm, ffn2_w_gate=ffn2_w_gate, ffn2_w_up=ffn2_w_up, ffn2_w_down=ffn2_w_down,
             final_norm=final_norm)
    mom = dict(ffn1_norm=m_ffn1_norm, ffn1_w_gate=m_ffn1_w_gate, ffn1_w_up=m_ffn1_w_up, ffn1_w_down=m_ffn1_w_down,
               mix_norm=m_mix_norm, w_in=m_w_in, b_in=m_b_in, sgu_norm_g=m_sgu_norm_g, sgu_norm_b=m_sgu_norm_b,
               sgu_w_s=m_sgu_w_s, sgu_b_s=m_sgu_b_s, ret_decay_logit=m_ret_decay_logit, w_branch_a=m_w_branch_a,
               w_branch_b=m_w_branch_b, w_out=m_w_out, ffn2_norm=m_ffn2_norm, ffn2_w_gate=m_ffn2_w_gate,
               ffn2_w_up=m_ffn2_w_up, ffn2_w_down=m_ffn2_w_down, final_norm=m_final_norm)
    var = dict(ffn1_norm=v_ffn1_norm, ffn1_w_gate=v_ffn1_w_gate, ffn1_w_up=v_ffn1_w_up, ffn1_w_down=v_ffn1_w_down,
               mix_norm=v_mix_norm, w_in=v_w_in, b_in=v_b_in, sgu_norm_g=v_sgu_norm_g, sgu_norm_b=v_sgu_norm_b,
               sgu_w_s=v_sgu_w_s, sgu_b_s=v_sgu_b_s, ret_decay_logit=v_ret_decay_logit, w_branch_a=v_w_branch_a,
               w_branch_b=v_w_branch_b, w_out=v_w_out, ffn2_norm=v_ffn2_norm, ffn2_w_gate=v_ffn2_w_gate,
               ffn2_w_up=v_ffn2_w_up, ffn2_w_down=v_ffn2_w_down, final_norm=v_final_norm)

    xs = x[0]
    tgt = loss_target[0]
    t, d = xs.shape
    dk = d // RET_HEADS

    shards2d = {n: _shard2d(p[n], n) for n in BIG}
    chip = (2 * lax.axis_index("x") + lax.axis_index("y")).astype(jnp.int32).reshape(1)
    groups = {"ffn1": ("ffn1_w_gate", "ffn1_w_up", "ffn1_w_down"), "in": ("w_in",),
              "mix": ("w_branch_a", "w_branch_b", "w_out"), "ffn2": ("ffn2_w_gate", "ffn2_w_up", "ffn2_w_down")}
    def own_slot(n, zero):
        sh = shards2d[n].astype(bf16) + zero
        return lax.dynamic_update_index_in_dim(lax.empty((N_CHIPS,) + sh.shape, bf16), sh, chip[0], 0)

    sems, bufs, tok = gather_start([own_slot(n, jnp.zeros((), bf16)) for n in groups["ffn1"]], [[0, 1, 2]],
                                   "gather_start_ffn1")
    gsem = {"ffn1": sems[0]}
    pending = dict(zip(groups["ffn1"], bufs))
    rest = [n for g in ("in", "mix", "ffn2") for n in groups[g]]
    sems, bufs, tok_rest = gather_start([own_slot(n, tok[0, 0].astype(bf16)) for n in rest],
                                 [[rest.index(n) for n in groups[g]] for g in ("in", "mix", "ffn2")], "gather_start_rest")
    gsem.update(zip(("in", "mix", "ffn2"), sems))
    pending.update(zip(rest, bufs))

    def arrive(gs, after):
        got = []
        for g in gs:
            got += gather_wait([pending[n] for n in groups[g]], gsem[g], after, "gather_wait_" + g)
        return gather_forward(got, "gather_forward_" + gs[0])

    bin4 = b_in.reshape(N_CHIPS, 1, 2 * d)
    ws_b = sgu_w_s[0].astype(bf16)
    bs_c = sgu_b_s[0][:, :, None]
    cols, mats, cdec, cos, sin = retention_constants(ret_decay_logit[0], t, dk, tok_rest[0, 0])

    wg1, wu1, wd1 = [_pair_shards(w) for w in arrive(["ffn1"], cos)]
    x1, g1, u1 = ffn_fwd(xs, ffn1_norm, wg1, wu1, wd1, "ffn1_fwd")
    win, = arrive(["in"], x1)
    proj, hb2, a = inproj_fwd(x1, mix_norm, win, bin4, cos, sin, sgu_norm_g, sgu_norm_b, ws_b, bs_c)
    late = []
    for g in ("mix", "ffn2"):
        late += gather_wait([pending[n] for n in groups[g]], gsem[g], proj, "gather_wait_" + g)
    fsems, late, ftok = forward_start(late, "forward_start_mix")
    r, rn = ret_fwd(proj, cols, mats, cdec, ftok)
    wa, wb, wo, wg2, wu2, wd2 = forward_wait(late, fsems, rn, "forward_wait_mix")
    wa, wb, wo = [w.reshape(d, d) for w in (wa, wb, wo)]
    wg2, wu2, wd2 = [_pair_shards(w) for w in (wg2, wu2, wd2)]
    x2, ba, br = mix_fwd(a, rn, proj, wa, wb, wo, x1)
    loss_blk, dx3, d_final, g2, u2 = ffn_fwd_loss(x2, ffn2_norm, wg2, wu2, wd2, final_norm.reshape(1, d), tgt, "ffn2_fwd")

    sent, swaps = {}, {}
    out_g, out_d, out_m, out_v = {}, {}, {}, {}

    def reduce_plane(g, after):
        gsems, own, lands, _ = sent[g]
        own, lands = exchange_wait(own, lands, gsems, after, "exchange_wait_" + g)
        plane = [sum_partials(chip, o, l, "sum_" + n) for n, o, l in zip(groups[g], own, lands)]
        swaps[g] = swap_start(plane, "swap_start_" + g)
        return swaps[g][3]

    def update(g, after):
        ssems, plane, lands, _ = swaps[g]
        plane, other = swap_wait(plane, lands, ssems, after, "swap_wait_" + g)
        for n, mine, sib in zip(groups[g], plane, other):
            res = adamw_shard(mine, sib, shards2d[n], _shard2d(mom[n], n), _shard2d(var[n], n), "adamw_" + n)
            out_g[n], out_d[n], out_m[n], out_v[n] = [_unshard(o, n) for o in res]
        return res[0]

    dx2, dg2, du2, act2, hb3, dyb2, d_ffn2n = ffn_bwd_act(dx3, x2, ffn2_norm, g2, u2, wg2, wu2, wd2, "ffn2_bwd_act", tok)
    sent["ffn2"] = exchange_start(ffn_weight_grads_one_call(hb3, dyb2, dg2, du2, act2, "ffn2_grad", tok),
                                  "exchange_start_ffn2")
    drn, dga, dgb, mixb, dba, dbr, dx2b, dua, dva, d_ws, d_bs, d_sng, d_snb = mix_bwd_act(
        dx2, ba, br, proj, wa, wb, wo, sgu_norm_g, sgu_norm_b, ws_b, bs_c, sent["ffn2"][3])
    g_mix = [g.reshape(N_CHIPS, d // N_CHIPS, d) for g in tn_matmuls_one_call(
        None, [a, rn, mixb], [dba, dbr, dx2b], [(d, d)] * 3, lambda j: (j, (slice(None), slice(None))), "grad_mix", tok)]
    dq, dkr, dv, dgr, dlg = ret_bwd(drn, r, proj, cols, mats, cdec, cos, sin)
    segs = [dua, dva, dq, dkr, dv, dgr, dga, dgb]
    dx1, d_bin, d_mixn = inproj_bwd_act(segs, win, x1, mix_norm, dx2)
    g_in, = tn_matmuls_one_call(hb2, None, segs, [(N_CHIPS, d, 2 * d)],
                                lambda j: (0, (j // 2, slice(None), pl.ds((j % 2) * d, d))), "grad_w_in", tok)
    groups["mix_in"] = groups["mix"] + groups["in"]
    sent["mix_in"] = exchange_start(g_mix + [g_in], "exchange_start_mix_in")
    grad_x, dg1, du1, act1, hb1, dyb1, d_ffn1n = ffn_bwd_act(dx1, xs, ffn1_norm, g1, u1, wg1, wu1, wd1, "ffn1_bwd_act",
                                                              sent["mix_in"][3])
    dlogit = dlg[:, 0:2, 0].T * jax.nn.sigmoid(-ret_decay_logit[0].astype(f32))
    small_g = dict(ffn1_norm=d_ffn1n, mix_norm=d_mixn, b_in=d_bin, sgu_norm_g=d_sng, sgu_norm_b=d_snb, sgu_w_s=d_ws,
                   sgu_b_s=d_bs, ret_decay_logit=dlogit, ffn2_norm=d_ffn2n, final_norm=d_final)
    shapes = [p[n].shape for n in SMALL] + [(1,)]
    small_sems, small_blk, small_land, small_tok = small_start(
        _pack([small_g[n] for n in SMALL] + [loss_blk[0, 0:1]], shapes))

    def send_one(which, grad):
        n = "ffn1_" + which
        groups[n] = (n,)
        sent[n] = exchange_start([grad], "exchange_start_" + n)
        return sent[n][3]

    for which, grad in zip(("w_gate", "w_up", "w_down"),
                           ffn_weight_grads_one_call(hb1, dyb1, dg1, du1, act1, "ffn1_grad", small_tok)):
        send_one(which, grad)

    after = reduce_plane("ffn2", sent["ffn1_w_down"][3])
    after = reduce_plane("mix_in", after)
    after = update("ffn2", after)
    g8 = small_wait(small_blk, small_land, small_sems, after)
    no_state = [jnp.zeros((1,), f32)]
    sg, sd, sm, sv = adamw_small(g8, _pack([p[n] for n in SMALL] + no_state, shapes),
                                 _pack([mom[n] for n in SMALL] + no_state, shapes),
                                 _pack([var[n] for n in SMALL] + no_state, shapes))
    for res, blockv in ((out_g, sg), (out_d, sd), (out_m, sm), (out_v, sv)):
        for n, val in zip(SMALL, _unpack(blockv, shapes)):
            res[n] = val
    loss = _unpack(sg, shapes)[-1][0]
    after = update("mix_in", sg)
    after = reduce_plane("ffn1_w_gate", after)
    after = reduce_plane("ffn1_w_up", after)
    after = update("ffn1_w_gate", after)
    after = reduce_plane("ffn1_w_down", after)
    after = update("ffn1_w_up", after)
    update("ffn1_w_down", after)

    return (loss, grad_x[None], *[out_g[n] for n in WEIGHTS], *[out_d[n] for n in WEIGHTS],
            *[out_m[n] for n in WEIGHTS], *[out_v[n] for n in WEIGHTS])
```
